```python
import jax, jax.numpy as jnp
from jax import lax
import numpy as np

D_MODEL = 1024
BATCH = 8
SEQ = 8192
DEPTH = 1

RET_HEADS = 8
RET_HEAD_DIM = 64
RET_WIDTH = RET_HEADS * RET_HEAD_DIM
RET_CHUNK = 128
MLA_HEADS = 8
MLA_NOPE_DIM = 64
MLA_ROPE_DIM = 32
MLA_V_DIM = 64
MLA_Q_RANK = 256
MLA_KV_RANK = 128
MLA_WIDTH = MLA_HEADS * MLA_V_DIM
MIX_WIDTH = RET_WIDTH + MLA_WIDTH
IN_WIDTH = 4 * RET_WIDTH + MLA_Q_RANK + MLA_KV_RANK + MLA_ROPE_DIM
D_FF = 2816
CONV_WIDTH = 3
Q_BLOCK = 128
ROPE_BASE = 10000.0
EPS = 1e-6

kernel_name = "hybrid_retention_mla_convffn"


def rms_norm(x, w):
    xf = x.astype(jnp.float32)
    y = xf * lax.rsqrt(jnp.mean(xf * xf, axis=-1, keepdims=True) + EPS)
    return (y * w.astype(jnp.float32)).astype(x.dtype)


def rope(x, positions):
    d = x.shape[-1]
    inv_freq = ROPE_BASE ** (-jnp.arange(0, d, 2, dtype=jnp.float32) / d)
    ang = positions.astype(jnp.float32)[..., None] * inv_freq
    if x.ndim == 4:
        ang = ang[:, :, None, :]
    cos, sin = jnp.cos(ang), jnp.sin(ang)
    xf = x.astype(jnp.float32)
    x1, x2 = xf[..., : d // 2], xf[..., d // 2:]
    return jnp.concatenate([x1 * cos - x2 * sin, x1 * sin + x2 * cos], axis=-1).astype(x.dtype)


def retention(q, k, v):
    B, S, H, dk = q.shape
    dv = v.shape[-1]
    C = RET_CHUNK
    N = S // C
    log_gamma = jnp.log1p(-jnp.power(2.0, -5.0 - jnp.arange(H, dtype=jnp.float32)))
    qc = q.astype(jnp.float32).reshape(B, N, C, H, dk)
    kc = k.astype(jnp.float32).reshape(B, N, C, H, dk)
    vc = v.astype(jnp.float32).reshape(B, N, C, H, dv)
    idx = jnp.arange(C, dtype=jnp.float32)
    diff = idx[:, None] - idx[None, :]
    decay_mask = jnp.where(diff >= 0, jnp.exp(log_gamma[:, None, None] * jnp.maximum(diff, 0.0)), 0.0)
    scores = jnp.einsum('bnihd,bnjhd->bnhij', qc, kc) * decay_mask
    o_inner = jnp.einsum('bnhij,bnjhe->bnihe', scores, vc)
    zeta = jnp.exp(log_gamma[:, None] * (C - 1.0 - idx))
    chunk_states = jnp.einsum('bnjhd,hj,bnjhe->nbhde', kc, zeta, vc)
    chunk_decay = jnp.exp(log_gamma * C)[None, :, None, None]

    def step(R, s_n):
        return chunk_decay * R + s_n, R

    _, r_prev = lax.scan(step, jnp.zeros((B, H, dk, dv), jnp.float32), chunk_states)
    r_prev = jnp.moveaxis(r_prev, 0, 1)
    xi = jnp.exp(log_gamma[:, None] * (idx + 1.0))
    o_cross = jnp.einsum('bnihd,bnhde,hi->bnihe', qc, r_prev, xi)
    return (o_inner + o_cross).reshape(B, S, H, dv)


def retention_group(q, k, v, g, positions, gn_w):
    B, S, _ = q.shape
    q = rope(q.reshape(B, S, RET_HEADS, RET_HEAD_DIM), positions)
    k = rope(k.reshape(B, S, RET_HEADS, RET_HEAD_DIM), positions) * (RET_HEAD_DIM ** -0.5)
    v = v.reshape(B, S, RET_HEADS, RET_HEAD_DIM)
    o = retention(q, k, v)
    mu = jnp.mean(o, axis=-1, keepdims=True)
    var = jnp.mean(jnp.square(o - mu), axis=-1, keepdims=True)
    o = ((o - mu) * lax.rsqrt(var + EPS)).reshape(B, S, RET_WIDTH) * gn_w.astype(jnp.float32)
    return (jax.nn.silu(g.astype(jnp.float32)) * o).astype(g.dtype)


def mla_group(c_q, c_kv, k_pe, positions, q_norm_w, w_uq, kv_norm_w, w_ukv):
    B, S, _ = c_q.shape
    H = MLA_HEADS
    q = jnp.einsum('bsr,rf->bsf', rms_norm(c_q, q_norm_w), w_uq).reshape(B, S, H, MLA_NOPE_DIM + MLA_ROPE_DIM)
    q_nope = q[..., :MLA_NOPE_DIM]
    q_pe = rope(q[..., MLA_NOPE_DIM:], positions)
    kv = jnp.einsum('bsr,rf->bsf', rms_norm(c_kv, kv_norm_w), w_ukv).reshape(B, S, H, MLA_NOPE_DIM + MLA_V_DIM)
    k_nope = kv[..., :MLA_NOPE_DIM]
    v = kv[..., MLA_NOPE_DIM:]
    k_pe = rope(k_pe, positions)
    scale = (MLA_NOPE_DIM + MLA_ROPE_DIM) ** -0.5
    N = S // Q_BLOCK
    qn_b = jnp.moveaxis(q_nope.reshape(B, N, Q_BLOCK, H, MLA_NOPE_DIM), 1, 0)
    qp_b = jnp.moveaxis(q_pe.reshape(B, N, Q_BLOCK, H, MLA_ROPE_DIM), 1, 0)
    key_pos = jnp.arange(S)
    neg = jnp.finfo(jnp.float32).min

    def block(args):
        qn, qp, blk = args
        s = (jnp.einsum('bqhd,bkhd->bhqk', qn, k_nope)
             + jnp.einsum('bqhr,bkr->bhqk', qp, k_pe)).astype(jnp.float32) * scale
        q_pos = blk * Q_BLOCK + jnp.arange(Q_BLOCK)
        s = jnp.where(key_pos[None, :] <= q_pos[:, None], s, neg)
        p = jax.nn.softmax(s, axis=-1).astype(v.dtype)
        return jnp.einsum('bhqk,bkhd->bqhd', p, v)

    o = lax.map(block, (qn_b, qp_b, jnp.arange(N)))
    return jnp.moveaxis(o, 0, 1).reshape(B, S, MLA_WIDTH)


def conv_ffn(h, w_up, conv_w, conv_b, w_down):
    S = h.shape[1]
    u = jnp.einsum('bsd,df->bsf', h, w_up)
    up = jnp.pad(u, ((0, 0), (CONV_WIDTH - 1, 0), (0, 0)))
    u = conv_b + sum(conv_w[j] * up[:, j:j + S] for j in range(CONV_WIDTH))
    gate, val = u[..., :D_FF], u[..., D_FF:]
    return jnp.einsum('bsf,fd->bsd', jax.nn.silu(gate) * val, w_down)


def _fwd_setup_inputs(seed: int = 0) -> dict:
    key = jax.random.key(seed)
    ks = jax.random.split(key, 20)
    f32 = jnp.float32

    def nrm(k, shape, fan_in):
        return jax.random.normal(k, shape, f32) * (fan_in ** -0.5)

    def gain(k, shape):
        return 1.0 + 0.02 * jax.random.normal(k, shape, f32)

    x = jax.random.normal(ks[0], (BATCH, SEQ, D_MODEL), f32)
    offset = jax.random.randint(ks[1], (BATCH, 1), 0, 4096, dtype=jnp.int32)
    positions = (offset + jnp.arange(SEQ, dtype=jnp.int32)[None, :]).astype(jnp.int32)
    return {
        "x": x,
        "positions": positions,
        "attn_norm_w": gain(ks[2], (DEPTH, D_MODEL)),
        "w_in": nrm(ks[3], (DEPTH, D_MODEL, IN_WIDTH), D_MODEL),
        "ret_gn_w": gain(ks[4], (DEPTH, RET_WIDTH)),
        "mla_q_norm_w": gain(ks[5], (DEPTH, MLA_Q_RANK)),
        "w_uq": nrm(ks[6], (DEPTH, MLA_Q_RANK, MLA_HEADS * (MLA_NOPE_DIM + MLA_ROPE_DIM)), MLA_Q_RANK),
        "mla_kv_norm_w": gain(ks[7], (DEPTH, MLA_KV_RANK)),
        "w_ukv": nrm(ks[8], (DEPTH, MLA_KV_RANK, MLA_HEADS * (MLA_NOPE_DIM + MLA_V_DIM)), MLA_KV_RANK),
        "w_out": nrm(ks[9], (DEPTH, MIX_WIDTH, D_MODEL), MIX_WIDTH),
        "ffn_norm_w": gain(ks[10], (DEPTH, D_MODEL)),
        "w_up": nrm(ks[11], (DEPTH, D_MODEL, 2 * D_FF), D_MODEL),
        "conv_w": nrm(ks[12], (DEPTH, CONV_WIDTH, 2 * D_FF), CONV_WIDTH),
        "conv_b": 0.01 * jax.random.normal(ks[13], (DEPTH, 2 * D_FF), f32),
        "w_down": nrm(ks[14], (DEPTH, D_FF, D_MODEL), D_FF),
        "final_norm_w": gain(ks[15], (D_MODEL,)),
    }


def _fwd_reference(x, positions, attn_norm_w, w_in, ret_gn_w, mla_q_norm_w, w_uq, mla_kv_norm_w, w_ukv,
              w_out, ffn_norm_w, w_up, conv_w, conv_b, w_down, final_norm_w):
    splits = np.cumsum([RET_WIDTH, RET_WIDTH, RET_WIDTH, RET_WIDTH, MLA_Q_RANK, MLA_KV_RANK]).tolist()
    for l in range(DEPTH):
        h = rms_norm(x, attn_norm_w[l])
        proj = jnp.einsum('bsd,df->bsf', h, w_in[l])
        r_q, r_k, r_v, r_g, c_q, c_kv, k_pe = jnp.split(proj, splits, axis=-1)
        y_ret = retention_group(r_q, r_k, r_v, r_g, positions, ret_gn_w[l])
        y_mla = mla_group(c_q, c_kv, k_pe, positions, mla_q_norm_w[l], w_uq[l],
                          mla_kv_norm_w[l], w_ukv[l])
        mixed = jnp.concatenate([y_ret, y_mla.astype(y_ret.dtype)], axis=-1)
        x = x + jnp.einsum('bsm,md->bsd', mixed, w_out[l])
        x = x + conv_ffn(rms_norm(x, ffn_norm_w[l]), w_up[l], conv_w[l], conv_b[l], w_down[l])
    return rms_norm(x, final_norm_w)


import jax as _jax
import jax.numpy as _jnp

TWIN_FORMAT = 'train_step'
FWD_PARAMS = ['x', 'positions', 'attn_norm_w', 'w_in', 'ret_gn_w', 'mla_q_norm_w', 'w_uq', 'mla_kv_norm_w', 'w_ukv', 'w_out', 'ffn_norm_w', 'w_up', 'conv_w', 'conv_b', 'w_down', 'final_norm_w']
TWIN_WEIGHTS = ['attn_norm_w', 'w_in', 'ret_gn_w', 'mla_q_norm_w', 'w_uq', 'mla_kv_norm_w', 'w_ukv', 'w_out', 'ffn_norm_w', 'w_up', 'conv_w', 'conv_b', 'w_down', 'final_norm_w']
TWIN_DIFF_INPUT = 'x'
TWIN_INPUTS = ['x', 'positions', 'attn_norm_w', 'w_in', 'ret_gn_w', 'mla_q_norm_w', 'w_uq', 'mla_kv_norm_w', 'w_ukv', 'w_out', 'ffn_norm_w', 'w_up', 'conv_w', 'conv_b', 'w_down', 'final_norm_w', 'loss_target', 'm_attn_norm_w', 'm_w_in', 'm_ret_gn_w', 'm_mla_q_norm_w', 'm_w_uq', 'm_mla_kv_norm_w', 'm_w_ukv', 'm_w_out', 'm_ffn_norm_w', 'm_w_up', 'm_conv_w', 'm_conv_b', 'm_w_down', 'm_final_norm_w', 'v_attn_norm_w', 'v_w_in', 'v_ret_gn_w', 'v_mla_q_norm_w', 'v_w_uq', 'v_mla_kv_norm_w', 'v_w_ukv', 'v_w_out', 'v_ffn_norm_w', 'v_w_up', 'v_conv_w', 'v_conv_b', 'v_w_down', 'v_final_norm_w']
TWIN_OUTPUTS = ['loss', 'grad_x', 'grad_attn_norm_w', 'grad_w_in', 'grad_ret_gn_w', 'grad_mla_q_norm_w', 'grad_w_uq', 'grad_mla_kv_norm_w', 'grad_w_ukv', 'grad_w_out', 'grad_ffn_norm_w', 'grad_w_up', 'grad_conv_w', 'grad_conv_b', 'grad_w_down', 'grad_final_norm_w', 'delta_attn_norm_w', 'delta_w_in', 'delta_ret_gn_w', 'delta_mla_q_norm_w', 'delta_w_uq', 'delta_mla_kv_norm_w', 'delta_w_ukv', 'delta_w_out', 'delta_ffn_norm_w', 'delta_w_up', 'delta_conv_w', 'delta_conv_b', 'delta_w_down', 'delta_final_norm_w', 'new_m_attn_norm_w', 'new_m_w_in', 'new_m_ret_gn_w', 'new_m_mla_q_norm_w', 'new_m_w_uq', 'new_m_mla_kv_norm_w', 'new_m_w_ukv', 'new_m_w_out', 'new_m_ffn_norm_w', 'new_m_w_up', 'new_m_conv_w', 'new_m_conv_b', 'new_m_w_down', 'new_m_final_norm_w', 'new_v_attn_norm_w', 'new_v_w_in', 'new_v_ret_gn_w', 'new_v_mla_q_norm_w', 'new_v_w_uq', 'new_v_mla_kv_norm_w', 'new_v_w_ukv', 'new_v_w_out', 'new_v_ffn_norm_w', 'new_v_w_up', 'new_v_conv_w', 'new_v_conv_b', 'new_v_w_down', 'new_v_final_norm_w']
TWIN_LEAF_KINDS = {'loss': 'loss', 'grad_x': 'grad_x', 'grad_attn_norm_w': 'grad_w', 'grad_w_in': 'grad_w', 'grad_ret_gn_w': 'grad_w', 'grad_mla_q_norm_w': 'grad_w', 'grad_w_uq': 'grad_w', 'grad_mla_kv_norm_w': 'grad_w', 'grad_w_ukv': 'grad_w', 'grad_w_out': 'grad_w', 'grad_ffn_norm_w': 'grad_w', 'grad_w_up': 'grad_w', 'grad_conv_w': 'grad_w', 'grad_conv_b': 'grad_w', 'grad_w_down': 'grad_w', 'grad_final_norm_w': 'grad_w', 'delta_attn_norm_w': 'delta_w', 'delta_w_in': 'delta_w', 'delta_ret_gn_w': 'delta_w', 'delta_mla_q_norm_w': 'delta_w', 'delta_w_uq': 'delta_w', 'delta_mla_kv_norm_w': 'delta_w', 'delta_w_ukv': 'delta_w', 'delta_w_out': 'delta_w', 'delta_ffn_norm_w': 'delta_w', 'delta_w_up': 'delta_w', 'delta_conv_w': 'delta_w', 'delta_conv_b': 'delta_w', 'delta_w_down': 'delta_w', 'delta_final_norm_w': 'delta_w', 'new_m_attn_norm_w': 'new_m', 'new_m_w_in': 'new_m', 'new_m_ret_gn_w': 'new_m', 'new_m_mla_q_norm_w': 'new_m', 'new_m_w_uq': 'new_m', 'new_m_mla_kv_norm_w': 'new_m', 'new_m_w_ukv': 'new_m', 'new_m_w_out': 'new_m', 'new_m_ffn_norm_w': 'new_m', 'new_m_w_up': 'new_m', 'new_m_conv_w': 'new_m', 'new_m_conv_b': 'new_m', 'new_m_w_down': 'new_m', 'new_m_final_norm_w': 'new_m', 'new_v_attn_norm_w': 'new_v', 'new_v_w_in': 'new_v', 'new_v_ret_gn_w': 'new_v', 'new_v_mla_q_norm_w': 'new_v', 'new_v_w_uq': 'new_v', 'new_v_mla_kv_norm_w': 'new_v', 'new_v_w_ukv': 'new_v', 'new_v_w_out': 'new_v', 'new_v_ffn_norm_w': 'new_v', 'new_v_w_up': 'new_v', 'new_v_conv_w': 'new_v', 'new_v_conv_b': 'new_v', 'new_v_w_down': 'new_v', 'new_v_final_norm_w': 'new_v'}


def _forward(args):
    return _fwd_reference(*[args[k] for k in FWD_PARAMS])


def _output_shape():
    def fwd():
        inp = _fwd_setup_inputs(0)
        return _fwd_reference(*[inp[k] for k in FWD_PARAMS])
    out = _jax.eval_shape(fwd)
    return out.shape, out.dtype

N_MICROBATCH = 1
ADAM_LR = 0.001
ADAM_B1 = 0.9
ADAM_B2 = 0.999
ADAM_EPS = 1e-08
ADAM_WD = 0.01
ADAM_STEP = 10
PER_EXAMPLE_BATCH_AXIS = {'x': 0, 'positions': 0, 'loss_target': 0}
SHARED_INPUTS = []
_WEIGHT_DTYPES = {'attn_norm_w': _jnp.float32, 'w_in': _jnp.float32, 'ret_gn_w': _jnp.float32, 'mla_q_norm_w': _jnp.float32, 'w_uq': _jnp.float32, 'mla_kv_norm_w': _jnp.float32, 'w_ukv': _jnp.float32, 'w_out': _jnp.float32, 'ffn_norm_w': _jnp.float32, 'w_up': _jnp.float32, 'conv_w': _jnp.float32, 'conv_b': _jnp.float32, 'w_down': _jnp.float32, 'final_norm_w': _jnp.float32}
MOMENT_SCALE = {'attn_norm_w': 2.372195e-01, 'w_in': 1.469731e-01, 'ret_gn_w': 1.766269e-01, 'mla_q_norm_w': 6.277567e-02, 'w_uq': 3.854965e-02, 'mla_kv_norm_w': 1.440009e-01, 'w_ukv': 4.796977e-02, 'w_out': 1.147093e-01, 'ffn_norm_w': 1.769606e-01, 'w_up': 7.556273e-02, 'conv_w': 7.752227e-02, 'conv_b': 7.726549e-02, 'w_down': 1.231143e-01, 'final_norm_w': 6.393243e+01}


def _to_microbatches(a, axis):
    t = _jnp.moveaxis(a, axis, 0)
    t = t.reshape((N_MICROBATCH, t.shape[0] // N_MICROBATCH) + t.shape[1:])
    return _jnp.moveaxis(t, 1, axis + 1)


def setup_inputs(seed: int = 0) -> dict:
    inp = _fwd_setup_inputs(seed)
    key = _jax.random.fold_in(_jax.random.key(seed), 7919)
    shape, _ = _output_shape()
    out = dict(inp)
    out["loss_target"] = _jax.random.normal(_jax.random.fold_in(key, 0), shape, _jnp.float32)
    for i, name in enumerate(TWIN_WEIGHTS):
        w = inp[name].astype(_jnp.float32)
        if MOMENT_SCALE is None:
            s = _jnp.sqrt(_jnp.mean(_jnp.square(w)) + 1e-30)
        else:
            s = MOMENT_SCALE[name]
        km, kv = _jax.random.split(_jax.random.fold_in(key, i + 1))
        out[name] = w
        out["m_" + name] = s * _jax.random.normal(km, w.shape, _jnp.float32)
        out["v_" + name] = (s * s) * _jax.random.uniform(kv, w.shape, _jnp.float32, 0.5, 1.5)
    if N_MICROBATCH > 1:
        for name, axis in PER_EXAMPLE_BATCH_AXIS.items():
            out[name] = _to_microbatches(out[name], axis)
    return {'x': out['x'], 'positions': out['positions'], 'attn_norm_w': out['attn_norm_w'], 'w_in': out['w_in'], 'ret_gn_w': out['ret_gn_w'], 'mla_q_norm_w': out['mla_q_norm_w'], 'w_uq': out['w_uq'], 'mla_kv_norm_w': out['mla_kv_norm_w'], 'w_ukv': out['w_ukv'], 'w_out': out['w_out'], 'ffn_norm_w': out['ffn_norm_w'], 'w_up': out['w_up'], 'conv_w': out['conv_w'], 'conv_b': out['conv_b'], 'w_down': out['w_down'], 'final_norm_w': out['final_norm_w'], 'loss_target': out['loss_target'], 'm_attn_norm_w': out['m_attn_norm_w'], 'm_w_in': out['m_w_in'], 'm_ret_gn_w': out['m_ret_gn_w'], 'm_mla_q_norm_w': out['m_mla_q_norm_w'], 'm_w_uq': out['m_w_uq'], 'm_mla_kv_norm_w': out['m_mla_kv_norm_w'], 'm_w_ukv': out['m_w_ukv'], 'm_w_out': out['m_w_out'], 'm_ffn_norm_w': out['m_ffn_norm_w'], 'm_w_up': out['m_w_up'], 'm_conv_w': out['m_conv_w'], 'm_conv_b': out['m_conv_b'], 'm_w_down': out['m_w_down'], 'm_final_norm_w': out['m_final_norm_w'], 'v_attn_norm_w': out['v_attn_norm_w'], 'v_w_in': out['v_w_in'], 'v_ret_gn_w': out['v_ret_gn_w'], 'v_mla_q_norm_w': out['v_mla_q_norm_w'], 'v_w_uq': out['v_w_uq'], 'v_mla_kv_norm_w': out['v_mla_kv_norm_w'], 'v_w_ukv': out['v_w_ukv'], 'v_w_out': out['v_w_out'], 'v_ffn_norm_w': out['v_ffn_norm_w'], 'v_w_up': out['v_w_up'], 'v_conv_w': out['v_conv_w'], 'v_conv_b': out['v_conv_b'], 'v_w_down': out['v_w_down'], 'v_final_norm_w': out['v_final_norm_w']}


def _loss(weights, diff, rest, loss_target):
    with _jax.named_scope("forward"):
        args = {**rest, TWIN_DIFF_INPUT: diff, **{k: w.astype(_WEIGHT_DTYPES[k]) for k, w in weights.items()}}
        y = _forward(args)
    with _jax.named_scope("loss_head"):
        err = _jnp.square(y.astype(_jnp.float32) - loss_target)
        return 0.5 * _jnp.sum(_jnp.mean(err, axis=-1)) if err.ndim else 0.5 * err


def _adamw(w, g, m, v):
    m = ADAM_B1 * m + (1.0 - ADAM_B1) * g
    v = ADAM_B2 * v + (1.0 - ADAM_B2) * _jnp.square(g)
    m_hat = m / (1.0 - ADAM_B1 ** ADAM_STEP)
    v_hat = v / (1.0 - ADAM_B2 ** ADAM_STEP)
    delta = -ADAM_LR * (m_hat / (_jnp.sqrt(v_hat) + ADAM_EPS) + ADAM_WD * w)
    return delta, m, v


def reference(x, positions, attn_norm_w, w_in, ret_gn_w, mla_q_norm_w, w_uq, mla_kv_norm_w, w_ukv, w_out, ffn_norm_w, w_up, conv_w, conv_b, w_down, final_norm_w, loss_target, m_attn_norm_w, m_w_in, m_ret_gn_w, m_mla_q_norm_w, m_w_uq, m_mla_kv_norm_w, m_w_ukv, m_w_out, m_ffn_norm_w, m_w_up, m_conv_w, m_conv_b, m_w_down, m_final_norm_w, v_attn_norm_w, v_w_in, v_ret_gn_w, v_mla_q_norm_w, v_w_uq, v_mla_kv_norm_w, v_w_ukv, v_w_out, v_ffn_norm_w, v_w_up, v_conv_w, v_conv_b, v_w_down, v_final_norm_w):
    given = dict(x=x, positions=positions, attn_norm_w=attn_norm_w, w_in=w_in, ret_gn_w=ret_gn_w, mla_q_norm_w=mla_q_norm_w, w_uq=w_uq, mla_kv_norm_w=mla_kv_norm_w, w_ukv=w_ukv, w_out=w_out, ffn_norm_w=ffn_norm_w, w_up=w_up, conv_w=conv_w, conv_b=conv_b, w_down=w_down, final_norm_w=final_norm_w, loss_target=loss_target, m_attn_norm_w=m_attn_norm_w, m_w_in=m_w_in, m_ret_gn_w=m_ret_gn_w, m_mla_q_norm_w=m_mla_q_norm_w, m_w_uq=m_w_uq, m_mla_kv_norm_w=m_mla_kv_norm_w, m_w_ukv=m_w_ukv, m_w_out=m_w_out, m_ffn_norm_w=m_ffn_norm_w, m_w_up=m_w_up, m_conv_w=m_conv_w, m_conv_b=m_conv_b, m_w_down=m_w_down, m_final_norm_w=m_final_norm_w, v_attn_norm_w=v_attn_norm_w, v_w_in=v_w_in, v_ret_gn_w=v_ret_gn_w, v_mla_q_norm_w=v_mla_q_norm_w, v_w_uq=v_w_uq, v_mla_kv_norm_w=v_mla_kv_norm_w, v_w_ukv=v_w_ukv, v_w_out=v_w_out, v_ffn_norm_w=v_ffn_norm_w, v_w_up=v_w_up, v_conv_w=v_conv_w, v_conv_b=v_conv_b, v_w_down=v_w_down, v_final_norm_w=v_final_norm_w)
    weights = {n: given[n] for n in TWIN_WEIGHTS}
    shared = {n: given[n] for n in SHARED_INPUTS}
    per_example = {n: given[n] for n in ['x', 'positions']}
    grad_fn = _jax.value_and_grad(_loss, argnums=(0, 1))

    def one_microbatch(ex, loss_target):
        ex = dict(ex)
        diff = ex.pop(TWIN_DIFF_INPUT)
        return grad_fn(weights, diff, {**shared, **ex}, loss_target)

    if N_MICROBATCH == 1:
        loss, (grad_w, grad_x) = one_microbatch(per_example, given["loss_target"])
    else:
        def body(carry, xs):
            loss_sum, grad_sum = carry
            l_k, (gw_k, gx_k) = one_microbatch(xs[0], xs[1])
            with _jax.named_scope("update"):
                return (loss_sum + l_k, _jax.tree.map(_jnp.add, grad_sum, gw_k)), gx_k

        init = (_jnp.zeros((), _jnp.float32), _jax.tree.map(_jnp.zeros_like, weights))
        (loss, grad_w), grad_x = _jax.lax.scan(body, init, (per_example, given["loss_target"]))
    with _jax.named_scope("update"):
        delta_w, new_m, new_v = {}, {}, {}
        for n in TWIN_WEIGHTS:
            delta_w[n], new_m[n], new_v[n] = _adamw(weights[n], grad_w[n], given["m_" + n], given["v_" + n])
    return (loss, grad_x, *[grad_w[n] for n in TWIN_WEIGHTS], *[delta_w[n] for n in TWIN_WEIGHTS],
            *[new_m[n] for n in TWIN_WEIGHTS], *[new_v[n] for n in TWIN_WEIGHTS])
```

```python
import functools

import numpy as np
import jax
import jax.numpy as jnp
from jax import lax
from jax.experimental import pallas as pl
from jax.experimental.pallas import tpu as pltpu

F32 = jnp.float32
BF16 = jnp.bfloat16
MESH = pl.DeviceIdType.MESH
ANY = pl.BlockSpec(memory_space=pl.ANY)

D_MODEL = 1024
RET_HEADS = 8
RET_HEAD_DIM = 64
RET_WIDTH = 512
RET_CHUNK = 128
MLA_HEADS = 8
MLA_NOPE = 64
MLA_ROPE = 32
MLA_V = 64
MLA_Q_RANK = 256
MLA_KV_RANK = 128
MLA_WIDTH = 512
IN_WIDTH = 2464
IN_PAD = 2560
D_FF = 2816
FF_HALF = 1408
ROPE_BASE = 10000.0
EPS = 1e-6
SCALE = float((MLA_NOPE + MLA_ROPE) ** -0.5)
K_SCALE = 0.125
N_DEV = 8

ADAM_LR = 0.001
ADAM_B1 = 0.9
ADAM_B2 = 0.999
ADAM_EPS = 1e-08
ADAM_WD = 0.01
ADAM_STEP = 10

VMEM_LIMIT = 56 * 1024 * 1024
MM_BUDGET = 40 * 1024 * 1024
NEG = -1e30

BIG = (("w_in", 1024, 308, True), ("w_uq", 256, 96, True), ("w_ukv", 128, 128, True),
       ("w_out", 128, 1024, False), ("w_up", 1024, 704, True), ("w_down", 352, 1024, False))
PACK_COLS = 1024
PACK_ROWS = 1536
SMALL = (("attn_norm_w", 1024), ("ret_gn_w", 512), ("mla_q_norm_w", 256), ("mla_kv_norm_w", 128),
         ("ffn_norm_w", 1024), ("conv_b", 5632), ("final_norm_w", 1024))
SMALL_N = 9600
SMALL_ROWS = 208


def _cp(sem=None, vmem=VMEM_LIMIT):
    return pltpu.CompilerParams(dimension_semantics=sem, vmem_limit_bytes=vmem)


def _dot(a, b):
    return jnp.dot(a, b, preferred_element_type=F32)


def _dot_nt(a, b):
    return lax.dot_general(a, b, (((1,), (1,)), ((), ())), preferred_element_type=F32)


def _dot_tn(a, b):
    return lax.dot_general(a, b, (((0,), (0,)), ((), ())), preferred_element_type=F32)


def _sigmoid(x):
    return 1.0 / (1.0 + jnp.exp(-x))


def _partner(x, half, period):
    n = x.shape[-1]
    lane = lax.broadcasted_iota(jnp.int32, x.shape, 1)
    return jnp.where((lane % period) < half, pltpu.roll(x, n - half, 1), pltpu.roll(x, half, 1))


def _rope(x, cos, ss, half, period):
    return x * cos + _partner(x, half, period) * ss


def _rope_t(dy, cos, ss, half, period):
    return dy * cos - _partner(dy, half, period) * ss


def _head_masks(shape):
    lane = lax.broadcasted_iota(jnp.int32, shape, 1)
    m0 = (lane < 64).astype(F32)
    return m0, 1.0 - m0


def _mm(a, b, *, name, add=None, out_dtype=F32):
    M, K = a.shape
    N = b.shape[1]
    osz = jnp.dtype(out_dtype).itemsize
    per_row = 2 * (K * a.dtype.itemsize + N * osz + (N * 4 if add is not None else 0))
    tm = 128
    for cand in (1024, 512, 256):
        if M % cand == 0 and cand * per_row + 4 * K * N <= MM_BUDGET:
            tm = cand
            break
    tm = min(tm, M)

    def body(*refs):
        if add is None:
            a_ref, b_ref, o_ref = refs
            acc = _dot(a_ref[...].astype(BF16), b_ref[...])
        else:
            a_ref, b_ref, r_ref, o_ref = refs
            acc = r_ref[...] + _dot(a_ref[...].astype(BF16), b_ref[...])
        o_ref[...] = acc.astype(out_dtype)

    in_specs = [pl.BlockSpec((tm, K), lambda i: (i, 0)), pl.BlockSpec((K, N), lambda i: (0, 0))]
    args = [a, b]
    if add is not None:
        in_specs.append(pl.BlockSpec((tm, N), lambda i: (i, 0)))
        args.append(add)
    return pl.pallas_call(
        body, name=name, grid=(M // tm,), in_specs=in_specs,
        out_specs=pl.BlockSpec((tm, N), lambda i: (i, 0)),
        out_shape=jax.ShapeDtypeStruct((M, N), out_dtype),
        compiler_params=_cp(("parallel",)))(*args)


def _mm_tn(a, b, *, name):
    T, M = a.shape
    N = b.shape[1]
    tk = min(T, 512)
    tm = M if M <= 1024 else 1408
    tn = N
    for cand in (1408, 1280, 1024):
        if N > 1408 and N % cand == 0:
            tn = cand
            break
    nk = T // tk

    def body(a_ref, b_ref, o_ref):
        @pl.when(pl.program_id(2) == 0)
        def _():
            o_ref[...] = jnp.zeros_like(o_ref)
        o_ref[...] += _dot_tn(a_ref[...].astype(BF16), b_ref[...].astype(BF16))

    return pl.pallas_call(
        body, name=name, grid=(M // tm, N // tn, nk),
        in_specs=[pl.BlockSpec((tk, tm), lambda i, j, k: (k, i)), pl.BlockSpec((tk, tn), lambda i, j, k: (k, j))],
        out_specs=pl.BlockSpec((tm, tn), lambda i, j, k: (i, j)),
        out_shape=jax.ShapeDtypeStruct((M, N), F32),
        compiler_params=_cp(("parallel", "parallel", "arbitrary")))(a, b)


def _rmsnorm_fwd(x, w, *, name):
    T, D = x.shape
    tm = min(T, 1024)

    def body(x_ref, w_ref, o_ref):
        xv = x_ref[...]
        r = lax.rsqrt(jnp.mean(xv * xv, axis=-1, keepdims=True) + EPS)
        o_ref[...] = (xv * r * w_ref[...]).astype(BF16)

    return pl.pallas_call(
        body, name=name, grid=(T // tm,),
        in_specs=[pl.BlockSpec((tm, D), lambda i: (i, 0)), pl.BlockSpec((1, D), lambda i: (0, 0))],
        out_specs=pl.BlockSpec((tm, D), lambda i: (i, 0)),
        out_shape=jax.ShapeDtypeStruct((T, D), BF16),
        compiler_params=_cp(("parallel",)))(x, w)


def _rmsnorm_bwd(x, w, dh, dres, *, name):
    T, D = x.shape
    tm = min(T, 512)

    def body(x_ref, w_ref, dh_ref, dr_ref, dx_ref, dw_ref):
        @pl.when(pl.program_id(0) == 0)
        def _():
            dw_ref[...] = jnp.zeros_like(dw_ref)
        xv = x_ref[...]
        r = lax.rsqrt(jnp.mean(xv * xv, axis=-1, keepdims=True) + EPS)
        xh = xv * r
        dh = dh_ref[...]
        g = dh * w_ref[...]
        dx_ref[...] = dr_ref[...] + r * (g - xh * jnp.mean(g * xh, axis=-1, keepdims=True))
        dw_ref[...] += jnp.sum(dh * xh, axis=0, keepdims=True)

    row = pl.BlockSpec((tm, D), lambda i: (i, 0))
    vec = pl.BlockSpec((1, D), lambda i: (0, 0))
    return pl.pallas_call(
        body, name=name, grid=(T // tm,), in_specs=[row, vec, row, row], out_specs=[row, vec],
        out_shape=[jax.ShapeDtypeStruct((T, D), F32), jax.ShapeDtypeStruct((1, D), F32)],
        compiler_params=_cp(("arbitrary",)))(x, w, dh, dres)


def _loss_head(x2, tgt, w, *, name):
    T, D = x2.shape
    tm = min(T, 512)

    def body(x_ref, t_ref, w_ref, loss_ref, dx_ref, dw_ref):
        @pl.when(pl.program_id(0) == 0)
        def _():
            dw_ref[...] = jnp.zeros_like(dw_ref)
            loss_ref[...] = jnp.zeros_like(loss_ref)
        xv = x_ref[...]
        wv = w_ref[...]
        r = lax.rsqrt(jnp.mean(xv * xv, axis=-1, keepdims=True) + EPS)
        xh = xv * r
        e = xh * wv - t_ref[...]
        part = 0.5 * jnp.sum(jnp.mean(e * e, axis=-1, keepdims=True), axis=0, keepdims=True)
        loss_ref[...] += jnp.broadcast_to(part, loss_ref.shape)
        dy = e * (1.0 / D)
        g = dy * wv
        dx_ref[...] = r * (g - xh * jnp.mean(g * xh, axis=-1, keepdims=True))
        dw_ref[...] += jnp.sum(dy * xh, axis=0, keepdims=True)

    row = pl.BlockSpec((tm, D), lambda i: (i, 0))
    vec = pl.BlockSpec((1, D), lambda i: (0, 0))
    return pl.pallas_call(
        body, name=name, grid=(T // tm,), in_specs=[row, row, vec],
        out_specs=[pl.BlockSpec((1, 128), lambda i: (0, 0)), row, vec],
        out_shape=[jax.ShapeDtypeStruct((1, 128), F32), jax.ShapeDtypeStruct((T, D), F32),
                   jax.ShapeDtypeStruct((1, D), F32)],
        compiler_params=_cp(("arbitrary",)))(x2, tgt, w)


def _ret_tables():
    C = RET_CHUNK
    h = jnp.arange(RET_HEADS, dtype=F32)
    log_gamma = jnp.log1p(-jnp.power(2.0, -5.0 - h))
    idx = jnp.arange(C, dtype=F32)
    diff = idx[:, None] - idx[None, :]
    dm = jnp.where(diff >= 0, jnp.exp(log_gamma[:, None, None] * jnp.maximum(diff, 0.0)), 0.0)
    dm = dm.reshape(4, 2 * C, C)
    lane_head = jnp.repeat(jnp.arange(RET_HEADS).reshape(4, 2), 64, axis=1)
    lg = log_gamma[lane_head]
    xi = jnp.exp(lg[:, None, :] * (idx[None, :, None] + 1.0))
    zeta = jnp.exp(lg[:, None, :] * (C - 1.0 - idx[None, :, None]))
    blk = (jnp.arange(128)[:, None] // 64) == (jnp.arange(128)[None, :] // 64)
    cd = jnp.where(blk[None], jnp.exp(lg * C)[:, :, None], 0.0)
    return dm.astype(F32), xi.astype(F32), zeta.astype(F32), cd.astype(F32)


def _ret_specs(tb, rev, nt):
    def tmap(t):
        return (nt - 1 - t) if rev else t
    qkv = [pl.BlockSpec((tb, 128), lambda p, t, o=o: (tmap(t), o + p)) for o in (0, 4, 8)]
    rope = [pl.BlockSpec((tb, 128), lambda p, t: (tmap(t), 0))] * 2
    tabs = [pl.BlockSpec((None, 256, 128), lambda p, t: (p, 0, 0))] + \
           [pl.BlockSpec((None, 128, 128), lambda p, t: (p, 0, 0))] * 3
    return qkv, rope, tabs


def _ret_fwd(proj, cos, ss, tabs, gnw, *, name):
    T = proj.shape[0]
    tb = min(T, 1024)
    nt = T // tb
    nchunk = tb // RET_CHUNK

    def body(q_ref, k_ref, v_ref, g_ref, cos_ref, ss_ref, dm_ref, xi_ref, zt_ref, cd_ref, gnw_ref,
             y_ref, o_ref, r_sc):
        @pl.when(pl.program_id(1) == 0)
        def _():
            r_sc[...] = jnp.zeros_like(r_sc)
        m0, m1 = _head_masks((128, 128))
        dm, xi, zt, cd = dm_ref[...], xi_ref[...], zt_ref[...], cd_ref[...]
        bm = (cd > 0).astype(F32)
        gnw = gnw_ref[...]
        for c in range(nchunk):
            rs = pl.ds(c * RET_CHUNK, RET_CHUNK)
            cs, sn = cos_ref[rs, :], ss_ref[rs, :]
            q = _rope(q_ref[rs, :], cs, sn, 32, 64)
            k = _rope(k_ref[rs, :], cs, sn, 32, 64) * K_SCALE
            v = v_ref[rs, :]
            kb, vb = k.astype(BF16), v.astype(BF16)
            qs = jnp.concatenate([q * m0, q * m1], axis=0).astype(BF16)
            s = (_dot_nt(qs, kb) * dm).astype(BF16)
            vs = jnp.concatenate([v * m0, v * m1], axis=0).astype(BF16)
            o = _dot(jnp.concatenate([s[:128], s[128:]], axis=1), vs)
            r = r_sc[...]
            o = o + _dot(q.astype(BF16), r.astype(BF16)) * xi
            r_sc[...] = cd * r + bm * _dot_tn((k * zt).astype(BF16), vb)
            mu = (jnp.sum(o * m0, axis=1, keepdims=True) * m0 + jnp.sum(o * m1, axis=1, keepdims=True) * m1) * (1.0 / 64)
            d = o - mu
            dd = d * d
            var = (jnp.sum(dd * m0, axis=1, keepdims=True) * m0 + jnp.sum(dd * m1, axis=1, keepdims=True) * m1) * (1.0 / 64)
            oh = d * lax.rsqrt(var + EPS)
            g = g_ref[rs, :]
            y_ref[rs, :] = (g * _sigmoid(g) * (oh * gnw)).astype(BF16)
            o_ref[rs, :] = o

    qkv, rope, tspec = _ret_specs(tb, False, nt)
    gspec = pl.BlockSpec((tb, 128), lambda p, t: (t, 12 + p))
    out = pl.BlockSpec((tb, 128), lambda p, t: (t, p))
    return pl.pallas_call(
        body, name=name, grid=(4, nt),
        in_specs=qkv + [gspec] + rope + tspec + [pl.BlockSpec((1, 128), lambda p, t: (0, p))],
        out_specs=[out, out],
        out_shape=[jax.ShapeDtypeStruct((T, RET_WIDTH), BF16), jax.ShapeDtypeStruct((T, RET_WIDTH), F32)],
        scratch_shapes=[pltpu.VMEM((128, 128), F32)],
        compiler_params=_cp(("parallel", "arbitrary")))(proj, proj, proj, proj, cos, ss, *tabs, gnw)


def _ret_bwd_dq(proj, do, cos, ss, tabs, *, name):
    T = proj.shape[0]
    tb = min(T, 1024)
    nt = T // tb
    nchunk = tb // RET_CHUNK

    def body(q_ref, k_ref, v_ref, do_ref, cos_ref, ss_ref, dm_ref, xi_ref, zt_ref, cd_ref, dq_ref, r_sc):
        del q_ref
        @pl.when(pl.program_id(1) == 0)
        def _():
            r_sc[...] = jnp.zeros_like(r_sc)
        m0, m1 = _head_masks((128, 128))
        dm, xi, zt, cd = dm_ref[...], xi_ref[...], zt_ref[...], cd_ref[...]
        bm = (cd > 0).astype(F32)
        for c in range(nchunk):
            rs = pl.ds(c * RET_CHUNK, RET_CHUNK)
            cs, sn = cos_ref[rs, :], ss_ref[rs, :]
            k = _rope(k_ref[rs, :], cs, sn, 32, 64) * K_SCALE
            vb = v_ref[rs, :].astype(BF16)
            dob = do_ref[rs, :]
            dof = dob.astype(F32)
            dos = jnp.concatenate([dof * m0, dof * m1], axis=0).astype(BF16)
            a = (_dot_nt(dos, vb) * dm).astype(BF16)
            ks = jnp.concatenate([k * m0, k * m1], axis=0).astype(BF16)
            r = r_sc[...]
            dq = _dot(jnp.concatenate([a[:128], a[128:]], axis=1), ks) + _dot_nt(dob, r.astype(BF16)) * xi
            r_sc[...] = cd * r + bm * _dot_tn((k * zt).astype(BF16), vb)
            dq_ref[rs, :] = _rope_t(dq, cs, sn, 32, 64).astype(BF16)

    qkv, rope, tspec = _ret_specs(tb, False, nt)
    blk = pl.BlockSpec((tb, 128), lambda p, t: (t, p))
    return pl.pallas_call(
        body, name=name, grid=(4, nt), in_specs=qkv + [blk] + rope + tspec, out_specs=blk,
        out_shape=jax.ShapeDtypeStruct((T, RET_WIDTH), BF16),
        scratch_shapes=[pltpu.VMEM((128, 128), F32)],
        compiler_params=_cp(("parallel", "arbitrary")))(proj, proj, proj, do, cos, ss, *tabs)


def _ret_bwd_dkv(proj, do, cos, ss, tabs, *, name):
    T = proj.shape[0]
    tb = min(T, 1024)
    nt = T // tb
    nchunk = tb // RET_CHUNK

    def body(q_ref, k_ref, v_ref, do_ref, cos_ref, ss_ref, dm_ref, xi_ref, zt_ref, cd_ref, dk_ref, dv_ref, u_sc):
        @pl.when(pl.program_id(1) == 0)
        def _():
            u_sc[...] = jnp.zeros_like(u_sc)
        m0, m1 = _head_masks((128, 128))
        dm, xi, zt, cd = dm_ref[...], xi_ref[...], zt_ref[...], cd_ref[...]
        bm = (cd > 0).astype(F32)
        for c in reversed(range(nchunk)):
            rs = pl.ds(c * RET_CHUNK, RET_CHUNK)
            cs, sn = cos_ref[rs, :], ss_ref[rs, :]
            q = _rope(q_ref[rs, :], cs, sn, 32, 64)
            k = _rope(k_ref[rs, :], cs, sn, 32, 64) * K_SCALE
            kb = k.astype(BF16)
            vb = v_ref[rs, :].astype(BF16)
            dob = do_ref[rs, :]
            dof = dob.astype(F32)
            qs = jnp.concatenate([q * m0, q * m1], axis=0).astype(BF16)
            dos = jnp.concatenate([dof * m0, dof * m1], axis=0).astype(BF16)
            s = (_dot_nt(qs, kb) * dm).astype(BF16)
            a = (_dot_nt(dos, vb) * dm).astype(BF16)
            ub = u_sc[...].astype(BF16)
            dk = _dot_tn(a, qs) + _dot_nt(vb, ub) * zt
            dv = _dot_tn(s, dos) + _dot(kb, ub) * zt
            u_sc[...] = cd * u_sc[...] + bm * _dot_tn((q * xi).astype(BF16), dob)
            dk_ref[rs, :] = (_rope_t(dk, cs, sn, 32, 64) * K_SCALE).astype(BF16)
            dv_ref[rs, :] = dv.astype(BF16)

    qkv, rope, tspec = _ret_specs(tb, True, nt)
    blk = pl.BlockSpec((tb, 128), lambda p, t: (nt - 1 - t, p))
    return pl.pallas_call(
        body, name=name, grid=(4, nt), in_specs=qkv + [blk] + rope + tspec, out_specs=[blk, blk],
        out_shape=[jax.ShapeDtypeStruct((T, RET_WIDTH), BF16)] * 2,
        scratch_shapes=[pltpu.VMEM((128, 128), F32)],
        compiler_params=_cp(("parallel", "arbitrary")))(proj, proj, proj, do, cos, ss, *tabs)


def _mix_bwd(dmixed, o_ret, proj, y_mla, gnw, *, name):
    T = dmixed.shape[0]
    tm = min(T, 512)

    def body(dm_ref, o_ref, g_ref, ym_ref, gnw_ref, do_ref, dg_ref, dom_ref, dl_ref, dw_ref):
        @pl.when(pl.program_id(0) == 0)
        def _():
            dw_ref[...] = jnp.zeros_like(dw_ref)
        m0, m1 = _head_masks((tm, 128))
        lane = lax.broadcasted_iota(jnp.int32, (tm, 128), 1)
        delta = jnp.zeros((tm, 128), F32)

        def gsum(z):
            return jnp.sum(z * m0, axis=1, keepdims=True) * m0 + jnp.sum(z * m1, axis=1, keepdims=True) * m1

        for p in range(4):
            cs = slice(128 * p, 128 * p + 128)
            dy = dm_ref[:, cs]
            o = o_ref[:, cs]
            g = g_ref[:, cs]
            w = gnw_ref[:, cs]
            d = o - gsum(o) * (1.0 / 64)
            rstd = lax.rsqrt(gsum(d * d) * (1.0 / 64) + EPS)
            oh = d * rstd
            sg = _sigmoid(g)
            dn = dy * (g * sg)
            dg_ref[:, cs] = (dy * (oh * w) * (sg * (1.0 + g * (1.0 - sg)))).astype(BF16)
            dw_ref[:, cs] += jnp.sum(dn * oh, axis=0, keepdims=True)
            doh = dn * w
            do = rstd * (doh - gsum(doh) * (1.0 / 64) - oh * (gsum(doh * oh) * (1.0 / 64)))
            do_ref[:, cs] = do.astype(BF16)
            dom = dm_ref[:, 512 + 128 * p:512 + 128 * p + 128]
            dom_ref[:, cs] = dom.astype(BF16)
            pr = dom * ym_ref[:, cs].astype(F32)
            delta = jnp.where(lane == 2 * p, jnp.sum(pr * m0, axis=1, keepdims=True), delta)
            delta = jnp.where(lane == 2 * p + 1, jnp.sum(pr * m1, axis=1, keepdims=True), delta)
        dl_ref[...] = delta

    half = pl.BlockSpec((tm, 512), lambda i: (i, 0))
    return pl.pallas_call(
        body, name=name, grid=(T // tm,),
        in_specs=[pl.BlockSpec((tm, 1024), lambda i: (i, 0)), half, pl.BlockSpec((tm, 512), lambda i: (i, 3)),
                  half, pl.BlockSpec((1, 512), lambda i: (0, 0))],
        out_specs=[half, half, half, pl.BlockSpec((tm, 128), lambda i: (i, 0)), pl.BlockSpec((1, 512), lambda i: (0, 0))],
        out_shape=[jax.ShapeDtypeStruct((T, 512), BF16)] * 3 + [jax.ShapeDtypeStruct((T, 128), F32),
                                                                jax.ShapeDtypeStruct((1, 512), F32)],
        compiler_params=_cp(("arbitrary",)))(dmixed, o_ret, proj, y_mla, gnw)


def _mla_prep_fwd(proj, qnw, kvnw, wuq, wk, wv, cos, ss, *, name):
    T = proj.shape[0]
    tm = min(T, 512)

    def body(lat_ref, qnw_ref, kvnw_ref, wuq_ref, wk_ref, wv_ref, cos_ref, ss_ref,
             q_ref, k_ref, v_ref, cqn_ref, ckvn_ref):
        cq = lat_ref[:, 0:256]
        ckv = lat_ref[:, 256:384]
        g3 = lat_ref[:, 384:512]
        cqn = (cq * lax.rsqrt(jnp.mean(cq * cq, axis=-1, keepdims=True) + EPS) * qnw_ref[...]).astype(BF16)
        ckvn = (ckv * lax.rsqrt(jnp.mean(ckv * ckv, axis=-1, keepdims=True) + EPS) * kvnw_ref[...]).astype(BF16)
        cqn_ref[...] = cqn
        ckvn_ref[...] = ckvn
        cs, sn = cos_ref[...], ss_ref[...]
        q = _dot(cqn, wuq_ref[...])
        k = _dot(ckvn, wk_ref[...])
        kpe = _rope(g3, cs, sn, 16, 32)
        for h in range(MLA_HEADS):
            hs = slice(128 * h, 128 * h + 128)
            q_ref[:, hs] = (_rope(q[:, hs], cs, sn, 16, 32) * SCALE).astype(BF16)
            k_ref[:, hs] = (k[:, hs] + kpe).astype(BF16)
        v_ref[...] = _dot(ckvn, wv_ref[...]).astype(BF16)

    def full(shape):
        return pl.BlockSpec(shape, lambda i: (0, 0))

    def row(w):
        return pl.BlockSpec((tm, w), lambda i: (i, 0))

    return pl.pallas_call(
        body, name=name, grid=(T // tm,),
        in_specs=[pl.BlockSpec((tm, 512), lambda i: (i, 4)), full((1, 256)), full((1, 128)), full((256, 1024)),
                  full((128, 1024)), full((128, 512)), row(128), row(128)],
        out_specs=[row(1024), row(1024), row(512), row(256), row(128)],
        out_shape=[jax.ShapeDtypeStruct((T, 1024), BF16), jax.ShapeDtypeStruct((T, 1024), BF16),
                   jax.ShapeDtypeStruct((T, 512), BF16), jax.ShapeDtypeStruct((T, 256), BF16),
                   jax.ShapeDtypeStruct((T, 128), BF16)],
        compiler_params=_cp(("parallel",)))(proj, qnw, kvnw, wuq, wk, wv, cos, ss)


def _mla_prep_bwd(dq, dk, dv, proj, qnw, kvnw, wuq_t, wk_t, wv_t, cos, ss, *, name):
    T = proj.shape[0]
    tm = min(T, 512)

    def body(dq_ref, dk_ref, dv_ref, lat_ref, qnw_ref, kvnw_ref, wuq_ref, wk_ref, wv_ref, cos_ref, ss_ref,
             dlat_ref, dqp_ref, dqnw_ref, dkvnw_ref):
        @pl.when(pl.program_id(0) == 0)
        def _():
            dqnw_ref[...] = jnp.zeros_like(dqnw_ref)
            dkvnw_ref[...] = jnp.zeros_like(dkvnw_ref)
        cs, sn = cos_ref[...], ss_ref[...]
        dkpe = jnp.zeros((tm, 128), F32)
        for h in range(MLA_HEADS):
            hs = slice(128 * h, 128 * h + 128)
            dqp_ref[:, hs] = _rope_t(dq_ref[:, hs] * SCALE, cs, sn, 16, 32).astype(BF16)
            dkpe = dkpe + dk_ref[:, hs]
        lane = lax.broadcasted_iota(jnp.int32, (tm, 128), 1)
        rope_lane = (lane >= MLA_NOPE) & (lane < MLA_NOPE + MLA_ROPE)
        dg3 = jnp.where(rope_lane, _rope_t(jnp.where(rope_lane, dkpe, 0.0), cs, sn, 16, 32), 0.0)

        def norm_bwd(x, w, dn):
            r = lax.rsqrt(jnp.mean(x * x, axis=-1, keepdims=True) + EPS)
            xh = x * r
            g = dn * w
            return r * (g - xh * jnp.mean(g * xh, axis=-1, keepdims=True)), jnp.sum(dn * xh, axis=0, keepdims=True)

        dcqn = _dot(dqp_ref[...], wuq_ref[...])
        dcq, dqnw = norm_bwd(lat_ref[:, 0:256], qnw_ref[...], dcqn)
        dckvn = _dot(dk_ref[...].astype(BF16), wk_ref[...]) + _dot(dv_ref[...], wv_ref[...])
        dckv, dkvnw = norm_bwd(lat_ref[:, 256:384], kvnw_ref[...], dckvn)
        dqnw_ref[...] += dqnw
        dkvnw_ref[...] += dkvnw
        dlat_ref[:, 0:256] = dcq.astype(BF16)
        dlat_ref[:, 256:384] = dckv.astype(BF16)
        dlat_ref[:, 384:512] = dg3.astype(BF16)

    def full(shape):
        return pl.BlockSpec(shape, lambda i: (0, 0))

    def row(w):
        return pl.BlockSpec((tm, w), lambda i: (i, 0))

    return pl.pallas_call(
        body, name=name, grid=(T // tm,),
        in_specs=[row(1024), row(1024), row(512), pl.BlockSpec((tm, 512), lambda i: (i, 4)), full((1, 256)),
                  full((1, 128)), full((1024, 256)), full((1024, 128)), full((512, 128)), row(128), row(128)],
        out_specs=[row(512), row(1024), full((1, 256)), full((1, 128))],
        out_shape=[jax.ShapeDtypeStruct((T, 512), BF16), jax.ShapeDtypeStruct((T, 1024), BF16),
                   jax.ShapeDtypeStruct((1, 256), F32), jax.ShapeDtypeStruct((1, 128), F32)],
        compiler_params=_cp(("arbitrary",)))(dq, dk, dv, proj, qnw, kvnw, wuq_t, wk_t, wv_t, cos, ss)


def _flash_fwd(q, k, v, *, name):
    T = q.shape[0]
    tq = min(T, 512)
    tk = tq
    nq = T // tq

    def body(q_ref, k_ref, v_ref, y_ref, lse_ref):
        qi = pl.program_id(1)
        m0, m1 = _head_masks((tq, 128))
        row = lax.broadcasted_iota(jnp.int32, (tq, tk), 0)
        col = lax.broadcasted_iota(jnp.int32, (tq, tk), 1)
        out = jnp.zeros((tq, 128), F32)
        for h in range(2):
            hs = slice(128 * h, 128 * h + 128)
            qh = q_ref[:, hs]

            def step(kb, carry, masked):
                m, l, acc = carry
                ks = pl.ds(pl.multiple_of(kb * tk, tk), tk)
                s = _dot_nt(qh, k_ref[ks, hs])
                if masked:
                    s = jnp.where(col <= row, s, NEG)
                mn = jnp.maximum(m, jnp.max(s, axis=1, keepdims=True))
                p = jnp.exp(s - mn)
                al = jnp.exp(m - mn)
                l = al * l + jnp.sum(p, axis=1, keepdims=True)
                acc = al * acc + _dot(p.astype(BF16), v_ref[ks, :])
                return mn, l, acc

            init = (jnp.full((tq, 1), NEG, F32), jnp.zeros((tq, 1), F32), jnp.zeros((tq, 128), F32))
            carry = lax.fori_loop(0, qi, lambda kb, c: step(kb, c, False), init)
            m, l, acc = step(qi, carry, True)
            out = out + (acc / l) * (m0 if h == 0 else m1)
            lse_ref[h] = m + jnp.log(l)
        y_ref[...] = out.astype(BF16)

    return pl.pallas_call(
        body, name=name, grid=(4, nq),
        in_specs=[pl.BlockSpec((tq, 256), lambda p, i: (i, p)), pl.BlockSpec((T, 256), lambda p, i: (0, p)),
                  pl.BlockSpec((T, 128), lambda p, i: (0, p))],
        out_specs=[pl.BlockSpec((tq, 128), lambda p, i: (i, p)), pl.BlockSpec((2, tq, 1), lambda p, i: (p, i, 0))],
        out_shape=[jax.ShapeDtypeStruct((T, MLA_WIDTH), BF16), jax.ShapeDtypeStruct((MLA_HEADS, T, 1), F32)],
        compiler_params=_cp(("parallel", "arbitrary")))(q, k, v)


def _flash_bwd(q, k, v, do, lse, delta, *, name):
    T = q.shape[0]
    tq = min(T, 512)
    tk = tq
    nq = T // tq

    def body(q_ref, k_ref, v_ref, do_ref, lse_ref, dl_ref, dq_ref, dk_ref, dv_ref):
        kb = pl.program_id(1)

        @pl.when(kb == 0)
        def _():
            dq_ref[...] = jnp.zeros_like(dq_ref)
        krow = lax.broadcasted_iota(jnp.int32, (tk, tq), 0)
        qcol = lax.broadcasted_iota(jnp.int32, (tk, tq), 1)
        masks = _head_masks((tk, 128))
        dv_acc = jnp.zeros((tk, 128), F32)
        for h in range(2):
            hs = slice(128 * h, 128 * h + 128)
            kh = k_ref[:, hs]
            vm = (v_ref[...].astype(F32) * masks[h]).astype(BF16)
            mh = masks[h]

            def step(qi, carry, masked):
                dk_acc, dv_in = carry
                qs = pl.ds(pl.multiple_of(qi * tq, tq), tq)
                qh = q_ref[qs, hs]
                dob = do_ref[qs, :]
                st = _dot_nt(kh, qh)
                pt = jnp.exp(st - lse_ref[h, qi])
                if masked:
                    pt = jnp.where(krow <= qcol, pt, 0.0)
                dom = (dob.astype(F32) * mh).astype(BF16)
                dv_out = dv_in + _dot(pt.astype(BF16), dom)
                dpt = _dot_nt(vm, dob)
                dst = (pt * (dpt - dl_ref[h, qi])).astype(BF16)
                dk_acc = dk_acc + _dot(dst, qh)
                dq_ref[qs, hs] += _dot_tn(dst, kh)
                return dk_acc, dv_out

            carry = step(kb, (jnp.zeros((tk, 128), F32), dv_acc), True)
            dk_acc, dv_acc = lax.fori_loop(kb + 1, nq, lambda qi, c: step(qi, c, False), carry)
            dk_ref[:, hs] = dk_acc
        dv_ref[...] = dv_acc.astype(BF16)

    stat = pl.BlockSpec((2, nq, 1, tq), lambda p, j: (p, 0, 0, 0))
    return pl.pallas_call(
        body, name=name, grid=(4, nq),
        in_specs=[pl.BlockSpec((T, 256), lambda p, j: (0, p)), pl.BlockSpec((tk, 256), lambda p, j: (j, p)),
                  pl.BlockSpec((tk, 128), lambda p, j: (j, p)), pl.BlockSpec((T, 128), lambda p, j: (0, p)), stat, stat],
        out_specs=[pl.BlockSpec((T, 256), lambda p, j: (0, p)), pl.BlockSpec((tk, 256), lambda p, j: (j, p)),
                   pl.BlockSpec((tk, 128), lambda p, j: (j, p))],
        out_shape=[jax.ShapeDtypeStruct((T, 1024), F32), jax.ShapeDtypeStruct((T, 1024), F32),
                   jax.ShapeDtypeStruct((T, MLA_WIDTH), BF16)],
        compiler_params=_cp(("parallel", "arbitrary")))(q, k, v, do, lse, delta)


def _shift_down(x, n, prev8):
    r = pltpu.roll(x, n, 0)
    row = lax.broadcasted_iota(jnp.int32, prev8.shape, 0)
    first = jnp.where(row < n, pltpu.roll(prev8, n, 0), r[:8])
    if x.shape[0] == 8:
        return first
    return jnp.concatenate([first, r[8:]], axis=0)


def _shift_up(x, n, next8):
    tm = x.shape[0]
    r = pltpu.roll(x, tm - n, 0)
    row = lax.broadcasted_iota(jnp.int32, next8.shape, 0)
    last = jnp.where(row >= 8 - n, pltpu.roll(next8, 8 - n, 0), r[tm - 8:])
    return jnp.concatenate([r[:tm - 8], last], axis=0)


def _conv_pre(u, prev8, cw_ref, cb_ref):
    p1 = _shift_down(u, 1, prev8)
    p2 = _shift_down(u, 2, prev8)
    up = cb_ref[...] + cw_ref[0:1, :] * p2 + cw_ref[1:2, :] * p1 + cw_ref[2:3, :] * u
    return up, p1, p2


def _conv_fwd(u, cw, cb, *, name):
    T = u.shape[0]
    tm = min(T, 512)
    W = 2 * FF_HALF

    def body(u_ref, prev_ref, cw_ref, cb_ref, a_ref):
        prev = jnp.where(pl.program_id(0) > 0, prev_ref[...], 0.0)
        up, _, _ = _conv_pre(u_ref[...], prev, cw_ref, cb_ref)
        gate = up[:, :FF_HALF]
        a_ref[...] = (gate * _sigmoid(gate) * up[:, FF_HALF:]).astype(BF16)

    return pl.pallas_call(
        body, name=name, grid=(T // tm, 2),
        in_specs=[pl.BlockSpec((tm, W), lambda i, j: (i, j)),
                  pl.BlockSpec((8, W), lambda i, j: (jnp.maximum(i * (tm // 8) - 1, 0), j)),
                  pl.BlockSpec((3, W), lambda i, j: (0, j)), pl.BlockSpec((1, W), lambda i, j: (0, j))],
        out_specs=pl.BlockSpec((tm, FF_HALF), lambda i, j: (i, j)),
        out_shape=jax.ShapeDtypeStruct((T, D_FF), BF16),
        compiler_params=_cp(("parallel", "parallel")))(u, u, cw, cb)


def _conv_bwd(u, da, cw, cb, *, name):
    T = u.shape[0]
    tm = min(T, 512)
    W = 2 * FF_HALF
    nt = T // tm

    def body(u_ref, prev_ref, next_ref, da_ref, dan_ref, cw_ref, cb_ref, du_ref, dw0_ref, dw1_ref, dw2_ref, db_ref):
        i = pl.program_id(1)

        @pl.when(i == 0)
        def _():
            for r in (dw0_ref, dw1_ref, dw2_ref, db_ref):
                r[...] = jnp.zeros_like(r)

        def dpre(u, prev8, da):
            up, p1, p2 = _conv_pre(u, prev8, cw_ref, cb_ref)
            gate, val = up[:, :FF_HALF], up[:, FF_HALF:]
            sg = _sigmoid(gate)
            dgate = da * val * (sg * (1.0 + gate * (1.0 - sg)))
            dval = da * (gate * sg)
            return jnp.concatenate([dgate, dval], axis=1), p1, p2

        u = u_ref[...]
        prev = jnp.where(i > 0, prev_ref[...], 0.0)
        dup, p1, p2 = dpre(u, prev, da_ref[...])
        dupn, _, _ = dpre(next_ref[...], u[tm - 8:], dan_ref[...])
        dupn = jnp.where(i < nt - 1, dupn, 0.0)
        du = cw_ref[2:3, :] * dup + cw_ref[1:2, :] * _shift_up(dup, 1, dupn) + cw_ref[0:1, :] * _shift_up(dup, 2, dupn)
        du_ref[...] = du.astype(BF16)
        dw0_ref[...] += jnp.sum(dup * p2, axis=0, keepdims=True)
        dw1_ref[...] += jnp.sum(dup * p1, axis=0, keepdims=True)
        dw2_ref[...] += jnp.sum(dup * u, axis=0, keepdims=True)
        db_ref[...] += jnp.sum(dup, axis=0, keepdims=True)

    nxt = lambda j, i: (jnp.minimum((i + 1) * (tm // 8), T // 8 - 1), j)
    vec = pl.BlockSpec((1, W), lambda j, i: (0, j))
    return pl.pallas_call(
        body, name=name, grid=(2, nt),
        in_specs=[pl.BlockSpec((tm, W), lambda j, i: (i, j)),
                  pl.BlockSpec((8, W), lambda j, i: (jnp.maximum(i * (tm // 8) - 1, 0), j)),
                  pl.BlockSpec((8, W), nxt),
                  pl.BlockSpec((tm, FF_HALF), lambda j, i: (i, j)), pl.BlockSpec((8, FF_HALF), nxt),
                  pl.BlockSpec((3, W), lambda j, i: (0, j)), vec],
        out_specs=[pl.BlockSpec((tm, W), lambda j, i: (i, j)), vec, vec, vec, vec],
        out_shape=[jax.ShapeDtypeStruct((T, 2 * D_FF), BF16)] + [jax.ShapeDtypeStruct((1, 2 * D_FF), F32)] * 4,
        compiler_params=_cp(("parallel", "arbitrary")))(u, u, u, da, da, cw, cb)


def _adamw(w, m, v, g_slots, *, name):
    R, C = w.shape
    ns = g_slots.shape[0]
    tr = 256 if R % 256 == 0 else R

    def body(w_ref, m_ref, v_ref, g_ref, go_ref, d_ref, mo_ref, vo_ref):
        g = g_ref[0]
        for s in range(1, ns):
            g = g + g_ref[s]
        mn = ADAM_B1 * m_ref[...] + (1.0 - ADAM_B1) * g
        vn = ADAM_B2 * v_ref[...] + (1.0 - ADAM_B2) * (g * g)
        m_hat = mn / (1.0 - ADAM_B1 ** ADAM_STEP)
        v_hat = vn / (1.0 - ADAM_B2 ** ADAM_STEP)
        go_ref[...] = g
        d_ref[...] = -ADAM_LR * (m_hat / (jnp.sqrt(v_hat) + ADAM_EPS) + ADAM_WD * w_ref[...])
        mo_ref[...] = mn
        vo_ref[...] = vn

    blk = pl.BlockSpec((tr, C), lambda i: (i, 0))
    return pl.pallas_call(
        body, name=name, grid=(R // tr,),
        in_specs=[blk, blk, blk, pl.BlockSpec((ns, tr, C), lambda i: (0, i, 0))],
        out_specs=[blk] * 4, out_shape=[jax.ShapeDtypeStruct((R, C), F32)] * 4,
        compiler_params=_cp(("parallel",)))(w, m, v, g_slots)


def _place():
    return lax.axis_index("x"), lax.axis_index("y"), lax.axis_index("c")


def _all_gather(x, *, name, in_vmem):
    def body(x_ref, out_ref, send_sems, recv_sems, local_sem):
        x_, y_, c_ = _place()
        me, sibling = (x_, y_, c_), (x_, y_, 1 - c_)
        chips = [(1 - x_, y_), (x_, 1 - y_), (1 - x_, 1 - y_)]

        def slot(px, py, pc):
            return out_ref.at[4 * px + 2 * py + pc]

        def copy(k, block, to, src=None):
            return pltpu.make_async_remote_copy(
                src_ref=slot(*block) if src is None else src, dst_ref=slot(*block),
                send_sem=send_sems.at[k], recv_sem=recv_sems.at[k], device_id=to, device_id_type=MESH)

        mine = pltpu.make_async_copy(x_ref, slot(*me), local_sem)
        mine.start()
        first = [copy(0, me, sibling, src=x_ref)]
        first += [copy(1 + j, me, (*chip, c_), src=x_ref) for j, chip in enumerate(chips)]
        for cp in first:
            cp.start()
        passed = [copy(4 + j, (*chip, c_), sibling) for j, chip in enumerate(chips)]
        for j, chip in enumerate(chips):
            copy(1 + j, (*chip, c_), me).wait_recv()
            passed[j].start()
        copy(0, sibling, me).wait_recv()
        for j, chip in enumerate(chips):
            copy(4 + j, (*chip, 1 - c_), me).wait_recv()
        for cp in first + passed:
            cp.wait_send()
        mine.wait()

    spec = pl.BlockSpec(memory_space=pltpu.VMEM) if in_vmem else ANY
    return pl.pallas_call(
        body, name=name, out_shape=jax.ShapeDtypeStruct((N_DEV,) + x.shape, x.dtype),
        in_specs=[spec], out_specs=spec,
        scratch_shapes=[pltpu.SemaphoreType.DMA((7,)), pltpu.SemaphoreType.DMA((7,)), pltpu.SemaphoreType.DMA],
        compiler_params=pltpu.CompilerParams(vmem_limit_bytes=VMEM_LIMIT))(x)


def _sum_slots(g, *, name):
    n = g.shape[0]

    def body(g_ref, o_ref):
        acc = g_ref[0]
        for s in range(1, n):
            acc = acc + g_ref[s]
        o_ref[...] = acc

    return pl.pallas_call(body, name=name, out_shape=jax.ShapeDtypeStruct(g.shape[1:], g.dtype))(g)


def _swap_sibling(x, *, name):
    def body(x_ref, out_ref, send_sem, recv_sem):
        x_, y_, c_ = _place()
        cp = pltpu.make_async_remote_copy(src_ref=x_ref, dst_ref=out_ref, send_sem=send_sem, recv_sem=recv_sem,
                                          device_id=(x_, y_, 1 - c_), device_id_type=MESH)
        cp.start()
        cp.wait()

    return pl.pallas_call(
        body, name=name, out_shape=jax.ShapeDtypeStruct(x.shape, x.dtype), in_specs=[ANY], out_specs=ANY,
        scratch_shapes=[pltpu.SemaphoreType.DMA, pltpu.SemaphoreType.DMA])(x)


def _exchange_chips(p, *, name):
    def body(p_ref, out_ref, send_sems, recv_sems, local_sem):
        x_, y_, c_ = _place()
        me_k = 2 * x_ + y_
        chips = [(1 - x_, y_), (x_, 1 - y_), (1 - x_, 1 - y_)]
        local = pltpu.make_async_copy(p_ref.at[me_k], out_ref.at[me_k], local_sem)
        local.start()

        def copy(j, src_k, dst_k, chip):
            return pltpu.make_async_remote_copy(
                src_ref=p_ref.at[src_k], dst_ref=out_ref.at[dst_k], send_sem=send_sems.at[j],
                recv_sem=recv_sems.at[j], device_id=(*chip, c_), device_id_type=MESH)

        sends = [copy(j, 2 * px + py, me_k, (px, py)) for j, (px, py) in enumerate(chips)]
        for cp in sends:
            cp.start()
        for j, (px, py) in enumerate(chips):
            copy(j, me_k, 2 * px + py, (px, py)).wait_recv()
        for cp in sends:
            cp.wait_send()
        local.wait()

    return pl.pallas_call(
        body, name=name, out_shape=jax.ShapeDtypeStruct(p.shape, p.dtype), in_specs=[ANY], out_specs=ANY,
        scratch_shapes=[pltpu.SemaphoreType.DMA((3,)), pltpu.SemaphoreType.DMA((3,)), pltpu.SemaphoreType.DMA])(p)


def _add2(a, b, *, name):
    n, R, C = a.shape
    tr = 256 if R % 256 == 0 else R

    def body(a_ref, b_ref, o_ref):
        o_ref[...] = a_ref[...] + b_ref[...]

    blk = pl.BlockSpec((1, tr, C), lambda s, i: (s, i, 0))
    return pl.pallas_call(body, name=name, grid=(n, R // tr), in_specs=[blk, blk], out_specs=blk,
                          out_shape=jax.ShapeDtypeStruct(a.shape, a.dtype),
                          compiler_params=_cp(("parallel", "parallel")))(a, b)


def _pack_local(parts):
    flat = jnp.concatenate([parts[n].reshape(-1) for n, _, _, _ in BIG])
    return jnp.pad(flat, (0, PACK_ROWS * PACK_COLS - flat.shape[0])).reshape(PACK_ROWS, PACK_COLS)


def _unpack_local(packed):
    flat = packed.reshape(-1)
    out, off = {}, 0
    for n, r, c, _ in BIG:
        out[n] = flat[off:off + r * c].reshape(1, r, c)
        off += r * c
    return out


def _unpack_gathered(g):
    flat = g.reshape(N_DEV, -1)
    out, off = {}, 0
    for n, r, c, cols in BIG:
        seg = flat[:, off:off + r * c].reshape(N_DEV, r, c)
        out[n] = seg.transpose(1, 0, 2).reshape(r, N_DEV * c) if cols else seg.reshape(N_DEV * r, c)
        off += r * c
    return out


def _pack_grads(grads):
    segs = []
    for n, r, c, cols in BIG:
        g = grads[n]
        segs.append(g.reshape(r, N_DEV, c).transpose(1, 0, 2).reshape(N_DEV, r * c) if cols
                    else g.reshape(N_DEV, r * c))
    flat = jnp.concatenate(segs, axis=1)
    flat = jnp.pad(flat, ((0, 0), (0, PACK_ROWS * PACK_COLS - flat.shape[1])))
    return flat.reshape(N_DEV, PACK_ROWS, PACK_COLS)


def _interleave_ff(w):
    g, v = w[..., :D_FF], w[..., D_FF:]
    return jnp.concatenate([g[..., :FF_HALF], v[..., :FF_HALF], g[..., FF_HALF:], v[..., FF_HALF:]], axis=-1)


def _deinterleave_ff(w):
    b = [w[..., i * FF_HALF:(i + 1) * FF_HALF] for i in range(4)]
    return jnp.concatenate([b[0], b[2], b[1], b[3]], axis=-1)


def _rope_tables(pos):
    p = pos.astype(F32)[:, None]
    inv_r = ROPE_BASE ** (-jnp.arange(0, RET_HEAD_DIM, 2, dtype=F32) / RET_HEAD_DIM)
    ang = p * inv_r
    c, s = jnp.cos(ang), jnp.sin(ang)
    cos_r = jnp.concatenate([c, c, c, c], axis=1)
    ss_r = jnp.concatenate([-s, s, -s, s], axis=1)
    inv_m = ROPE_BASE ** (-jnp.arange(0, MLA_ROPE, 2, dtype=F32) / MLA_ROPE)
    ang = p * inv_m
    c, s = jnp.cos(ang), jnp.sin(ang)
    T = pos.shape[0]
    cos_m = jnp.concatenate([jnp.ones((T, 64), F32), c, c, jnp.ones((T, 32), F32)], axis=1)
    ss_m = jnp.concatenate([jnp.zeros((T, 64), F32), -s, s, jnp.zeros((T, 32), F32)], axis=1)
    return cos_r, ss_r, cos_m, ss_m


def _prep_weights(full):
    w_in = full["w_in"]
    z = lambda n: jnp.zeros((D_MODEL, n), BF16)
    w_in_p = jnp.concatenate([w_in[:, :2432], z(64), w_in[:, 2432:2464], z(32)], axis=1)
    w_uq_p = jnp.pad(full["w_uq"].reshape(MLA_Q_RANK, MLA_HEADS, 96), ((0, 0), (0, 0), (0, 32))).reshape(MLA_Q_RANK, 1024)
    ukv = full["w_ukv"].reshape(MLA_KV_RANK, MLA_HEADS, 128)
    w_k = jnp.pad(ukv[:, :, :64], ((0, 0), (0, 0), (0, 64))).reshape(MLA_KV_RANK, 1024)
    w_v = ukv[:, :, 64:].reshape(MLA_KV_RANK, 512)
    w_up_p = _interleave_ff(full["w_up"])
    return dict(w_in=w_in_p, w_in_t=w_in_p.T, w_uq=w_uq_p, w_uq_t=w_uq_p.T, w_k=w_k, w_k_t=w_k.T, w_v=w_v,
                w_v_t=w_v.T, w_out=full["w_out"], w_out_t=full["w_out"].T, w_up=w_up_p, w_up_t=w_up_p.T,
                w_down=full["w_down"], w_down_t=full["w_down"].T)


def _local_step(x, pos, tgt, W, sm):
    cos_r, ss_r, cos_m, ss_m = _rope_tables(pos)
    tabs = _ret_tables()

    h = _rmsnorm_fwd(x, sm["attn_norm_w"], name="attn_norm")
    proj = _mm(h, W["w_in"], name="in_proj")
    y_ret, o_ret = _ret_fwd(proj, cos_r, ss_r, tabs, sm["ret_gn_w"], name="ret_fwd")
    q, k, v, cqn, ckvn = _mla_prep_fwd(proj, sm["mla_q_norm_w"], sm["mla_kv_norm_w"], W["w_uq"], W["w_k"], W["w_v"],
                                       cos_m, ss_m, name="mla_prep")
    y_mla, lse = _flash_fwd(q, k, v, name="mla_attn")
    mixed = jnp.concatenate([y_ret, y_mla], axis=1)
    x1 = _mm(mixed, W["w_out"], add=x, name="out_proj")
    h2 = _rmsnorm_fwd(x1, sm["ffn_norm_w"], name="ffn_norm")
    u = _mm(h2, W["w_up"], name="up_proj")
    a = _conv_fwd(u, sm["conv_w"], sm["conv_b"], name="conv_gate")
    x2 = _mm(a, W["w_down"], add=x1, name="down_proj")
    loss, dx2, d_final = _loss_head(x2, tgt, sm["final_norm_w"], name="loss_head")

    g = {}
    g["w_down"] = _mm_tn(a, dx2, name="dw_down")
    da = _mm(dx2, W["w_down_t"], name="d_act")
    du, dcw0, dcw1, dcw2, dcb = _conv_bwd(u, da, sm["conv_w"], sm["conv_b"], name="conv_bwd")
    g["w_up"] = _mm_tn(h2, du, name="dw_up")
    dh2 = _mm(du, W["w_up_t"], name="d_h2")
    dx1, d_ffn = _rmsnorm_bwd(x1, sm["ffn_norm_w"], dh2, dx2, name="ffn_norm_bwd")

    g["w_out"] = _mm_tn(mixed, dx1, name="dw_out")
    dmixed = _mm(dx1, W["w_out_t"], name="d_mixed")
    do_ret, dg, do_mla, delta, d_gn = _mix_bwd(dmixed, o_ret, proj, y_mla, sm["ret_gn_w"], name="mix_bwd")
    drq = _ret_bwd_dq(proj, do_ret, cos_r, ss_r, tabs, name="ret_bwd_dq")
    drk, drv = _ret_bwd_dkv(proj, do_ret, cos_r, ss_r, tabs, name="ret_bwd_dkv")
    T = x.shape[0]
    tq = min(T, 512)
    lse_r = lse.reshape(MLA_HEADS, T // tq, 1, tq)
    delta_r = delta[:, :MLA_HEADS].T.reshape(MLA_HEADS, T // tq, 1, tq)
    dq, dk, dv = _flash_bwd(q, k, v, do_mla, lse_r, delta_r, name="mla_attn_bwd")
    dlat, dqp, d_qn, d_kvn = _mla_prep_bwd(dq, dk, dv, proj, sm["mla_q_norm_w"], sm["mla_kv_norm_w"], W["w_uq_t"],
                                           W["w_k_t"], W["w_v_t"], cos_m, ss_m, name="mla_prep_bwd")
    g_uq = _mm_tn(cqn, dqp, name="dw_uq")
    g_k = _mm_tn(ckvn, dk, name="dw_ukv_k")
    g_v = _mm_tn(ckvn, dv, name="dw_ukv_v")
    g["w_uq"] = g_uq.reshape(MLA_Q_RANK, MLA_HEADS, 128)[:, :, :96].reshape(MLA_Q_RANK, 768)
    g["w_ukv"] = jnp.concatenate([g_k.reshape(MLA_KV_RANK, MLA_HEADS, 128)[:, :, :64],
                                  g_v.reshape(MLA_KV_RANK, MLA_HEADS, 64)], axis=2).reshape(MLA_KV_RANK, 1024)
    dproj = jnp.concatenate([drq, drk, drv, dg, dlat], axis=1)
    g_in = _mm_tn(h, dproj, name="dw_in")
    g["w_in"] = jnp.concatenate([g_in[:, :2432], g_in[:, 2496:2528]], axis=1)
    dh = _mm(dproj, W["w_in_t"], name="d_h")
    grad_x, d_attn = _rmsnorm_bwd(x, sm["attn_norm_w"], dh, dx1, name="attn_norm_bwd")
    g["w_up"] = _deinterleave_ff(g["w_up"])

    small = dict(attn_norm_w=d_attn, ret_gn_w=d_gn, mla_q_norm_w=d_qn, mla_kv_norm_w=d_kvn, ffn_norm_w=d_ffn,
                 conv_b=_deinterleave_ff(dcb), final_norm_w=d_final,
                 conv_w=_deinterleave_ff(jnp.concatenate([dcw0, dcw1, dcw2], axis=0)))
    return loss, grad_x, g, small


def kernel(x, positions, attn_norm_w, w_in, ret_gn_w, mla_q_norm_w, w_uq, mla_kv_norm_w, w_ukv, w_out, ffn_norm_w, w_up, conv_w, conv_b, w_down, final_norm_w, loss_target, m_attn_norm_w, m_w_in, m_ret_gn_w, m_mla_q_norm_w, m_w_uq, m_mla_kv_norm_w, m_w_ukv, m_w_out, m_ffn_norm_w, m_w_up, m_conv_w, m_conv_b, m_w_down, m_final_norm_w, v_attn_norm_w, v_w_in, v_ret_gn_w, v_mla_q_norm_w, v_w_uq, v_mla_kv_norm_w, v_w_ukv, v_w_out, v_ffn_norm_w, v_w_up, v_conv_w, v_conv_b, v_w_down, v_final_norm_w):
    a = dict(locals())
    x_, y_, c_ = _place()
    dev = 4 * x_ + 2 * y_ + c_

    shard = {n: a[n][0] for n, _, _, _ in BIG}
    gathered = _all_gather(_pack_local(shard).astype(BF16), name="gather_weights", in_vmem=False)
    W = _prep_weights(_unpack_gathered(gathered))
    cw_pad = jnp.pad(conv_w[0].reshape(-1), (0, 24 * 128 - 3 * 704)).reshape(24, 128)
    cw_all = _all_gather(cw_pad, name="gather_conv_w", in_vmem=True)
    conv_w_full = cw_all.reshape(N_DEV, -1)[:, :3 * 704].reshape(N_DEV, 3, 704).transpose(1, 0, 2).reshape(3, 2 * D_FF)
    sm = dict(attn_norm_w=attn_norm_w, ret_gn_w=ret_gn_w, mla_q_norm_w=mla_q_norm_w, mla_kv_norm_w=mla_kv_norm_w,
              ffn_norm_w=ffn_norm_w, final_norm_w=final_norm_w.reshape(1, D_MODEL),
              conv_w=_interleave_ff(conv_w_full), conv_b=_interleave_ff(conv_b))

    loss, grad_x, g, gs = _local_step(x[0], positions[0], loss_target[0], W, sm)

    gp = _pack_grads(g).reshape(4, 2, PACK_ROWS, PACK_COLS)
    mine = lax.dynamic_index_in_dim(gp, c_, axis=1, keepdims=False)
    theirs = lax.dynamic_index_in_dim(gp, 1 - c_, axis=1, keepdims=False)
    pair = _add2(mine, _swap_sibling(theirs, name="grad_swap_sibling"), name="grad_pair_sum")
    slots = _exchange_chips(pair, name="grad_exchange_chips")
    big = _adamw(_pack_local(shard), _pack_local({n: a["m_" + n][0] for n, _, _, _ in BIG}),
                 _pack_local({n: a["v_" + n][0] for n, _, _, _ in BIG}), slots, name="adamw_large")
    big = [_unpack_local(t) for t in big]

    vec = jnp.concatenate([gs[n].reshape(-1) for n, _ in SMALL] + [gs["conv_w"].reshape(-1), loss[0, :1]])
    vec = jnp.pad(vec, (0, SMALL_ROWS * 128 - vec.shape[0])).reshape(SMALL_ROWS, 128)
    tot = _sum_slots(_all_gather(vec, name="gather_small_grads", in_vmem=True), name="sum_small_grads").reshape(-1)
    loss_out = tot[SMALL_N + 3 * 2 * D_FF]
    g_cw = lax.dynamic_slice_in_dim(tot[SMALL_N:SMALL_N + 3 * 2 * D_FF].reshape(3, 2 * D_FF), dev * 704, 704, axis=1)

    def flat_small(prefix):
        return jnp.concatenate([a[prefix + n].reshape(-1) for n, _ in SMALL]).reshape(75, 128)

    sml = _adamw(flat_small(""), flat_small("m_"), flat_small("v_"), tot[:SMALL_N].reshape(1, 75, 128), name="adamw_small")
    cwo = _adamw(conv_w[0], m_conv_w[0], v_conv_w[0], g_cw[None], name="adamw_conv_w")

    def small_of(t, n):
        off = 0
        for nm, sz in SMALL:
            if nm == n:
                return t.reshape(-1)[off:off + sz].reshape(a[n].shape)
            off += sz

    names = ['attn_norm_w', 'w_in', 'ret_gn_w', 'mla_q_norm_w', 'w_uq', 'mla_kv_norm_w', 'w_ukv', 'w_out',
             'ffn_norm_w', 'w_up', 'conv_w', 'conv_b', 'w_down', 'final_norm_w']
    outs = [loss_out, grad_x[None]]
    for kind in range(4):
        for n in names:
            if n == "conv_w":
                outs.append(cwo[kind][None])
            elif n in big[kind]:
                outs.append(big[kind][n])
            else:
                outs.append(small_of(sml[kind], n))
    return tuple(outs)
```

```python
import functools

import numpy as np
import jax
import jax.numpy as jnp
from jax import lax
from jax.experimental import pallas as pl
from jax.experimental.pallas import tpu as pltpu

F32 = jnp.float32
BF16 = jnp.bfloat16
MESH = pl.DeviceIdType.MESH
ANY = pl.BlockSpec(memory_space=pl.ANY)

D_MODEL = 1024
RET_HEADS = 8
RET_HEAD_DIM = 64
RET_WIDTH = 512
RET_CHUNK = 128
MLA_HEADS = 8
MLA_NOPE = 64
MLA_ROPE = 32
MLA_V = 64
MLA_Q_RANK = 256
MLA_KV_RANK = 128
MLA_WIDTH = 512
IN_WIDTH = 2464
IN_PAD = 2560
D_FF = 2816
FF_HALF = 1408
ROPE_BASE = 10000.0
EPS = 1e-6
SCALE = float((MLA_NOPE + MLA_ROPE) ** -0.5)
K_SCALE = 0.125
N_DEV = 8

ADAM_LR = 0.001
ADAM_B1 = 0.9
ADAM_B2 = 0.999
ADAM_EPS = 1e-08
ADAM_WD = 0.01
ADAM_STEP = 10

VMEM_LIMIT = 56 * 1024 * 1024
MM_BUDGET = 40 * 1024 * 1024
NEG = -1e30

BIG = (("w_in", 308, 320, True), ("w_uq", 24, 32, True), ("w_ukv", 16, 16, True),
       ("w_out", 128, 128, False), ("w_up", 704, 704, True), ("w_down", 352, 352, False))
PACK_COLS = 1024
PACK_ROWS = 1600
FF_OWNER_ORDER = (0, 1, 4, 5, 2, 3, 6, 7)
SMALL = (("attn_norm_w", 1024), ("ret_gn_w", 512), ("mla_q_norm_w", 256), ("mla_kv_norm_w", 128),
         ("ffn_norm_w", 1024), ("conv_b", 5632), ("final_norm_w", 1024))
SMALL_N = 9600
SMALL_ROWS = 208


def _cp(sem=None, vmem=VMEM_LIMIT):
    return pltpu.CompilerParams(dimension_semantics=sem, vmem_limit_bytes=vmem)


def _dot(a, b):
    return jnp.dot(a, b, preferred_element_type=F32)


def _dot_nt(a, b):
    return lax.dot_general(a, b, (((1,), (1,)), ((), ())), preferred_element_type=F32)


def _dot_tn(a, b):
    return lax.dot_general(a, b, (((0,), (0,)), ((), ())), preferred_element_type=F32)


def _sigmoid(x):
    return 1.0 / (1.0 + jnp.exp(-x))


def _partner(x, half, period):
    n = x.shape[-1]
    lane = lax.broadcasted_iota(jnp.int32, x.shape, 1)
    return jnp.where((lane % period) < half, pltpu.roll(x, n - half, 1), pltpu.roll(x, half, 1))


def _rope(x, cos, ss, half, period):
    return x * cos + _partner(x, half, period) * ss


def _rope_t(dy, cos, ss, half, period):
    return dy * cos - _partner(dy, half, period) * ss


def _head_masks(shape):
    lane = lax.broadcasted_iota(jnp.int32, shape, 1)
    m0 = (lane < 64).astype(F32)
    return m0, 1.0 - m0


def _mm(a, b, *, name, add=None, out_dtype=F32, bt=False):
    M, K = a.shape
    N = b.shape[0] if bt else b.shape[1]
    osz = jnp.dtype(out_dtype).itemsize
    per_row = 2 * (K * a.dtype.itemsize + N * osz + (N * 4 if add is not None else 0))
    tm = 128
    for cand in (512, 256):
        if M % cand == 0 and cand * per_row + 4 * K * N <= MM_BUDGET:
            tm = cand
            break
    tm = min(tm, M)
    mul = _dot_nt if bt else _dot

    def body(*refs):
        if add is None:
            a_ref, b_ref, o_ref = refs
            acc = mul(a_ref[...].astype(BF16), b_ref[...])
        else:
            a_ref, b_ref, r_ref, o_ref = refs
            acc = r_ref[...] + mul(a_ref[...].astype(BF16), b_ref[...])
        o_ref[...] = acc.astype(out_dtype)

    in_specs = [pl.BlockSpec((tm, K), lambda i: (i, 0)), pl.BlockSpec(b.shape, lambda i: (0, 0))]
    args = [a, b]
    if add is not None:
        in_specs.append(pl.BlockSpec((tm, N), lambda i: (i, 0)))
        args.append(add)
    return pl.pallas_call(
        body, name=name, grid=(M // tm,), in_specs=in_specs,
        out_specs=pl.BlockSpec((tm, N), lambda i: (i, 0)),
        out_shape=jax.ShapeDtypeStruct((M, N), out_dtype),
        compiler_params=_cp(("parallel",)))(*args)


def _mm_tn(a, b, *, name):
    T, M = a.shape
    N = b.shape[1]
    tk = min(T, 512)

    def tile(n):
        for cand in (1408, 1280):
            if n > 1408 and n % cand == 0:
                return cand
        return n

    tm, tn = tile(M), tile(N)
    nk = T // tk

    def body(a_ref, b_ref, o_ref):
        @pl.when(pl.program_id(2) == 0)
        def _():
            o_ref[...] = jnp.zeros_like(o_ref)
        o_ref[...] += _dot_tn(a_ref[...].astype(BF16), b_ref[...].astype(BF16))

    return pl.pallas_call(
        body, name=name, grid=(M // tm, N // tn, nk),
        in_specs=[pl.BlockSpec((tk, tm), lambda i, j, k: (k, i)), pl.BlockSpec((tk, tn), lambda i, j, k: (k, j))],
        out_specs=pl.BlockSpec((tm, tn), lambda i, j, k: (i, j)),
        out_shape=jax.ShapeDtypeStruct((M, N), F32),
        compiler_params=_cp(("parallel", "parallel", "arbitrary")))(a, b)


def _rmsnorm_fwd(x, w, *, name):
    T, D = x.shape
    tm = min(T, 1024)

    def body(x_ref, w_ref, o_ref):
        xv = x_ref[...]
        r = lax.rsqrt(jnp.mean(xv * xv, axis=-1, keepdims=True) + EPS)
        o_ref[...] = (xv * r * w_ref[...]).astype(BF16)

    return pl.pallas_call(
        body, name=name, grid=(T // tm,),
        in_specs=[pl.BlockSpec((tm, D), lambda i: (i, 0)), pl.BlockSpec((1, D), lambda i: (0, 0))],
        out_specs=pl.BlockSpec((tm, D), lambda i: (i, 0)),
        out_shape=jax.ShapeDtypeStruct((T, D), BF16),
        compiler_params=_cp(("parallel",)))(x, w)


def _rmsnorm_bwd(x, w, dh, dres, *, name):
    T, D = x.shape
    tm = min(T, 512)

    def body(x_ref, w_ref, dh_ref, dr_ref, dx_ref, dw_ref):
        @pl.when(pl.program_id(0) == 0)
        def _():
            dw_ref[...] = jnp.zeros_like(dw_ref)
        xv = x_ref[...]
        r = lax.rsqrt(jnp.mean(xv * xv, axis=-1, keepdims=True) + EPS)
        xh = xv * r
        dh = dh_ref[...]
        g = dh * w_ref[...]
        dx_ref[...] = dr_ref[...] + r * (g - xh * jnp.mean(g * xh, axis=-1, keepdims=True))
        dw_ref[...] += jnp.sum(dh * xh, axis=0, keepdims=True)

    row = pl.BlockSpec((tm, D), lambda i: (i, 0))
    vec = pl.BlockSpec((1, D), lambda i: (0, 0))
    return pl.pallas_call(
        body, name=name, grid=(T // tm,), in_specs=[row, vec, row, row], out_specs=[row, vec],
        out_shape=[jax.ShapeDtypeStruct((T, D), F32), jax.ShapeDtypeStruct((1, D), F32)],
        compiler_params=_cp(("arbitrary",)))(x, w, dh, dres)


def _loss_head(x2, tgt, w, *, name):
    T, D = x2.shape
    tm = min(T, 512)

    def body(x_ref, t_ref, w_ref, loss_ref, dx_ref, dw_ref):
        @pl.when(pl.program_id(0) == 0)
        def _():
            dw_ref[...] = jnp.zeros_like(dw_ref)
            loss_ref[...] = jnp.zeros_like(loss_ref)
        xv = x_ref[...]
        wv = w_ref[...]
        r = lax.rsqrt(jnp.mean(xv * xv, axis=-1, keepdims=True) + EPS)
        xh = xv * r
        e = xh * wv - t_ref[...]
        part = 0.5 * jnp.sum(jnp.mean(e * e, axis=-1, keepdims=True), axis=0, keepdims=True)
        loss_ref[...] += jnp.broadcast_to(part, loss_ref.shape)
        dy = e * (1.0 / D)
        g = dy * wv
        dx_ref[...] = r * (g - xh * jnp.mean(g * xh, axis=-1, keepdims=True))
        dw_ref[...] += jnp.sum(dy * xh, axis=0, keepdims=True)

    row = pl.BlockSpec((tm, D), lambda i: (i, 0))
    vec = pl.BlockSpec((1, D), lambda i: (0, 0))
    return pl.pallas_call(
        body, name=name, grid=(T // tm,), in_specs=[row, row, vec],
        out_specs=[pl.BlockSpec((1, 128), lambda i: (0, 0)), row, vec],
        out_shape=[jax.ShapeDtypeStruct((1, 128), F32), jax.ShapeDtypeStruct((T, D), F32),
                   jax.ShapeDtypeStruct((1, D), F32)],
        compiler_params=_cp(("arbitrary",)))(x2, tgt, w)


def _ret_tables():
    C = RET_CHUNK
    h = jnp.arange(RET_HEADS, dtype=F32)
    log_gamma = jnp.log1p(-jnp.power(2.0, -5.0 - h))
    idx = jnp.arange(C, dtype=F32)
    diff = idx[:, None] - idx[None, :]
    dm = jnp.where(diff >= 0, jnp.exp(log_gamma[:, None, None] * jnp.maximum(diff, 0.0)), 0.0)
    dm = dm.reshape(4, 2 * C, C)
    lane_head = jnp.repeat(jnp.arange(RET_HEADS).reshape(4, 2), 64, axis=1)
    lg = log_gamma[lane_head]
    xi = jnp.exp(lg[:, None, :] * (idx[None, :, None] + 1.0))
    zeta = jnp.exp(lg[:, None, :] * (C - 1.0 - idx[None, :, None]))
    blk = (jnp.arange(128)[:, None] // 64) == (jnp.arange(128)[None, :] // 64)
    cd = jnp.where(blk[None], jnp.exp(lg * C)[:, :, None], 0.0)
    return dm.astype(F32), xi.astype(F32), zeta.astype(F32), cd.astype(F32)


def _ret_specs(tb, rev, nt):
    def tmap(t):
        return (nt - 1 - t) if rev else t
    qkv = [pl.BlockSpec((tb, 128), lambda p, t, o=o: (tmap(t), o + p)) for o in (0, 4, 8)]
    rope = [pl.BlockSpec((tb, 128), lambda p, t: (tmap(t), 0))] * 2
    tabs = [pl.BlockSpec((None, 256, 128), lambda p, t: (p, 0, 0))] + \
           [pl.BlockSpec((None, 128, 128), lambda p, t: (p, 0, 0))] * 3
    return qkv, rope, tabs


def _ret_fwd(proj, cos, ss, tabs, gnw, *, name):
    T = proj.shape[0]
    tb = min(T, 1024)
    nt = T // tb
    nchunk = tb // RET_CHUNK

    def body(q_ref, k_ref, v_ref, g_ref, cos_ref, ss_ref, dm_ref, xi_ref, zt_ref, cd_ref, gnw_ref,
             y_ref, o_ref, r_sc):
        @pl.when(pl.program_id(1) == 0)
        def _():
            r_sc[...] = jnp.zeros_like(r_sc)
        m0, m1 = _head_masks((128, 128))
        dm, xi, zt, cd = dm_ref[...], xi_ref[...], zt_ref[...], cd_ref[...]
        bm = (cd > 0).astype(F32)
        gnw = gnw_ref[...]
        for c in range(nchunk):
            rs = pl.ds(c * RET_CHUNK, RET_CHUNK)
            cs, sn = cos_ref[rs, :], ss_ref[rs, :]
            q = _rope(q_ref[rs, :], cs, sn, 32, 64)
            k = _rope(k_ref[rs, :], cs, sn, 32, 64) * K_SCALE
            v = v_ref[rs, :]
            kb, vb = k.astype(BF16), v.astype(BF16)
            qs = jnp.concatenate([q * m0, q * m1], axis=0).astype(BF16)
            s = (_dot_nt(qs, kb) * dm).astype(BF16)
            vs = jnp.concatenate([v * m0, v * m1], axis=0).astype(BF16)
            o = _dot(jnp.concatenate([s[:128], s[128:]], axis=1), vs)
            r = r_sc[...]
            o = o + _dot(q.astype(BF16), r.astype(BF16)) * xi
            r_sc[...] = cd * r + bm * _dot_tn((k * zt).astype(BF16), vb)
            mu = (jnp.sum(o * m0, axis=1, keepdims=True) * m0 + jnp.sum(o * m1, axis=1, keepdims=True) * m1) * (1.0 / 64)
            d = o - mu
            dd = d * d
            var = (jnp.sum(dd * m0, axis=1, keepdims=True) * m0 + jnp.sum(dd * m1, axis=1, keepdims=True) * m1) * (1.0 / 64)
            oh = d * lax.rsqrt(var + EPS)
            g = g_ref[rs, :]
            y_ref[rs, :] = (g * _sigmoid(g) * (oh * gnw)).astype(BF16)
            o_ref[rs, :] = o

    qkv, rope, tspec = _ret_specs(tb, False, nt)
    gspec = pl.BlockSpec((tb, 128), lambda p, t: (t, 12 + p))
    out = pl.BlockSpec((tb, 128), lambda p, t: (t, p))
    return pl.pallas_call(
        body, name=name, grid=(4, nt),
        in_specs=qkv + [gspec] + rope + tspec + [pl.BlockSpec((1, 128), lambda p, t: (0, p))],
        out_specs=[out, out],
        out_shape=[jax.ShapeDtypeStruct((T, RET_WIDTH), BF16), jax.ShapeDtypeStruct((T, RET_WIDTH), F32)],
        scratch_shapes=[pltpu.VMEM((128, 128), F32)],
        compiler_params=_cp(("parallel", "arbitrary")))(proj, proj, proj, proj, cos, ss, *tabs, gnw)


def _ret_bwd_dq(proj, do, cos, ss, tabs, *, name):
    T = proj.shape[0]
    tb = min(T, 1024)
    nt = T // tb
    nchunk = tb // RET_CHUNK

    def body(q_ref, k_ref, v_ref, do_ref, cos_ref, ss_ref, dm_ref, xi_ref, zt_ref, cd_ref, dq_ref, r_sc):
        del q_ref
        @pl.when(pl.program_id(1) == 0)
        def _():
            r_sc[...] = jnp.zeros_like(r_sc)
        m0, m1 = _head_masks((128, 128))
        dm, xi, zt, cd = dm_ref[...], xi_ref[...], zt_ref[...], cd_ref[...]
        bm = (cd > 0).astype(F32)
        for c in range(nchunk):
            rs = pl.ds(c * RET_CHUNK, RET_CHUNK)
            cs, sn = cos_ref[rs, :], ss_ref[rs, :]
            k = _rope(k_ref[rs, :], cs, sn, 32, 64) * K_SCALE
            vb = v_ref[rs, :].astype(BF16)
            dob = do_ref[rs, :]
            dof = dob.astype(F32)
            dos = jnp.concatenate([dof * m0, dof * m1], axis=0).astype(BF16)
            a = (_dot_nt(dos, vb) * dm).astype(BF16)
            ks = jnp.concatenate([k * m0, k * m1], axis=0).astype(BF16)
            r = r_sc[...]
            dq = _dot(jnp.concatenate([a[:128], a[128:]], axis=1), ks) + _dot_nt(dob, r.astype(BF16)) * xi
            r_sc[...] = cd * r + bm * _dot_tn((k * zt).astype(BF16), vb)
            dq_ref[rs, :] = _rope_t(dq, cs, sn, 32, 64).astype(BF16)

    qkv, rope, tspec = _ret_specs(tb, False, nt)
    blk = pl.BlockSpec((tb, 128), lambda p, t: (t, p))
    return pl.pallas_call(
        body, name=name, grid=(4, nt), in_specs=qkv + [blk] + rope + tspec, out_specs=blk,
        out_shape=jax.ShapeDtypeStruct((T, RET_WIDTH), BF16),
        scratch_shapes=[pltpu.VMEM((128, 128), F32)],
        compiler_params=_cp(("parallel", "arbitrary")))(proj, proj, proj, do, cos, ss, *tabs)


def _ret_bwd_dkv(proj, do, cos, ss, tabs, *, name):
    T = proj.shape[0]
    tb = min(T, 1024)
    nt = T // tb
    nchunk = tb // RET_CHUNK

    def body(q_ref, k_ref, v_ref, do_ref, cos_ref, ss_ref, dm_ref, xi_ref, zt_ref, cd_ref, dk_ref, dv_ref, u_sc):
        @pl.when(pl.program_id(1) == 0)
        def _():
            u_sc[...] = jnp.zeros_like(u_sc)
        m0, m1 = _head_masks((128, 128))
        dm, xi, zt, cd = dm_ref[...], xi_ref[...], zt_ref[...], cd_ref[...]
        bm = (cd > 0).astype(F32)
        for c in reversed(range(nchunk)):
            rs = pl.ds(c * RET_CHUNK, RET_CHUNK)
            cs, sn = cos_ref[rs, :], ss_ref[rs, :]
            q = _rope(q_ref[rs, :], cs, sn, 32, 64)
            k = _rope(k_ref[rs, :], cs, sn, 32, 64) * K_SCALE
            kb = k.astype(BF16)
            vb = v_ref[rs, :].astype(BF16)
            dob = do_ref[rs, :]
            dof = dob.astype(F32)
            qs = jnp.concatenate([q * m0, q * m1], axis=0).astype(BF16)
            dos = jnp.concatenate([dof * m0, dof * m1], axis=0).astype(BF16)
            s = (_dot_nt(qs, kb) * dm).astype(BF16)
            a = (_dot_nt(dos, vb) * dm).astype(BF16)
            ub = u_sc[...].astype(BF16)
            dk = _dot_tn(a, qs) + _dot_nt(vb, ub) * zt
            dv = _dot_tn(s, dos) + _dot(kb, ub) * zt
            u_sc[...] = cd * u_sc[...] + bm * _dot_tn((q * xi).astype(BF16), dob)
            dk_ref[rs, :] = (_rope_t(dk, cs, sn, 32, 64) * K_SCALE).astype(BF16)
            dv_ref[rs, :] = dv.astype(BF16)

    qkv, rope, tspec = _ret_specs(tb, True, nt)
    blk = pl.BlockSpec((tb, 128), lambda p, t: (nt - 1 - t, p))
    return pl.pallas_call(
        body, name=name, grid=(4, nt), in_specs=qkv + [blk] + rope + tspec, out_specs=[blk, blk],
        out_shape=[jax.ShapeDtypeStruct((T, RET_WIDTH), BF16)] * 2,
        scratch_shapes=[pltpu.VMEM((128, 128), F32)],
        compiler_params=_cp(("parallel", "arbitrary")))(proj, proj, proj, do, cos, ss, *tabs)


def _mix_bwd(dmixed, o_ret, proj, y_mla, gnw, *, name):
    T = dmixed.shape[0]
    tm = min(T, 512)

    def body(dm_ref, o_ref, g_ref, ym_ref, gnw_ref, do_ref, dg_ref, dom_ref, dl_ref, dw_ref):
        @pl.when(pl.program_id(0) == 0)
        def _():
            dw_ref[...] = jnp.zeros_like(dw_ref)
        m0, m1 = _head_masks((tm, 128))
        lane = lax.broadcasted_iota(jnp.int32, (tm, 128), 1)
        delta = jnp.zeros((tm, 128), F32)

        def gsum(z):
            return jnp.sum(z * m0, axis=1, keepdims=True) * m0 + jnp.sum(z * m1, axis=1, keepdims=True) * m1

        for p in range(4):
            cs = slice(128 * p, 128 * p + 128)
            dy = dm_ref[:, cs]
            o = o_ref[:, cs]
            g = g_ref[:, cs]
            w = gnw_ref[:, cs]
            d = o - gsum(o) * (1.0 / 64)
            rstd = lax.rsqrt(gsum(d * d) * (1.0 / 64) + EPS)
            oh = d * rstd
            sg = _sigmoid(g)
            dn = dy * (g * sg)
            dg_ref[:, cs] = (dy * (oh * w) * (sg * (1.0 + g * (1.0 - sg)))).astype(BF16)
            dw_ref[:, cs] += jnp.sum(dn * oh, axis=0, keepdims=True)
            doh = dn * w
            do = rstd * (doh - gsum(doh) * (1.0 / 64) - oh * (gsum(doh * oh) * (1.0 / 64)))
            do_ref[:, cs] = do.astype(BF16)
            dom = dm_ref[:, 512 + 128 * p:512 + 128 * p + 128]
            dom_ref[:, cs] = dom.astype(BF16)
            pr = dom * ym_ref[:, cs].astype(F32)
            delta = jnp.where(lane == 2 * p, jnp.sum(pr * m0, axis=1, keepdims=True), delta)
            delta = jnp.where(lane == 2 * p + 1, jnp.sum(pr * m1, axis=1, keepdims=True), delta)
        dl_ref[...] = delta

    half = pl.BlockSpec((tm, 512), lambda i: (i, 0))
    return pl.pallas_call(
        body, name=name, grid=(T // tm,),
        in_specs=[pl.BlockSpec((tm, 1024), lambda i: (i, 0)), half, pl.BlockSpec((tm, 512), lambda i: (i, 3)),
                  half, pl.BlockSpec((1, 512), lambda i: (0, 0))],
        out_specs=[half, half, half, pl.BlockSpec((tm, 128), lambda i: (i, 0)), pl.BlockSpec((1, 512), lambda i: (0, 0))],
        out_shape=[jax.ShapeDtypeStruct((T, 512), BF16)] * 3 + [jax.ShapeDtypeStruct((T, 128), F32),
                                                                jax.ShapeDtypeStruct((1, 512), F32)],
        compiler_params=_cp(("arbitrary",)))(dmixed, o_ret, proj, y_mla, gnw)


def _mla_prep_fwd(proj, qnw, kvnw, wuq, wk, wv, cos, ss, *, name):
    T = proj.shape[0]
    tm = min(T, 512)

    def body(lat_ref, qnw_ref, kvnw_ref, wuq_ref, wk_ref, wv_ref, cos_ref, ss_ref,
             q_ref, k_ref, v_ref, cqn_ref, ckvn_ref):
        cq = lat_ref[:, 0:256]
        ckv = lat_ref[:, 256:384]
        g3 = lat_ref[:, 384:512]
        cqn = (cq * lax.rsqrt(jnp.mean(cq * cq, axis=-1, keepdims=True) + EPS) * qnw_ref[...]).astype(BF16)
        ckvn = (ckv * lax.rsqrt(jnp.mean(ckv * ckv, axis=-1, keepdims=True) + EPS) * kvnw_ref[...]).astype(BF16)
        cqn_ref[...] = cqn
        ckvn_ref[...] = ckvn
        cs, sn = cos_ref[...], ss_ref[...]
        q = _dot_nt(cqn, wuq_ref[...])
        k = _dot_nt(ckvn, wk_ref[...])
        kpe = _rope(g3, cs, sn, 16, 32)
        for h in range(MLA_HEADS):
            hs = slice(128 * h, 128 * h + 128)
            q_ref[:, hs] = (_rope(q[:, hs], cs, sn, 16, 32) * SCALE).astype(BF16)
            k_ref[:, hs] = (k[:, hs] + kpe).astype(BF16)
        v_ref[...] = _dot_nt(ckvn, wv_ref[...]).astype(BF16)

    def full(shape):
        return pl.BlockSpec(shape, lambda i: (0, 0))

    def row(w):
        return pl.BlockSpec((tm, w), lambda i: (i, 0))

    return pl.pallas_call(
        body, name=name, grid=(T // tm,),
        in_specs=[pl.BlockSpec((tm, 512), lambda i: (i, 4)), full((1, 256)), full((1, 128)), full((1024, 256)),
                  full((1024, 128)), full((512, 128)), row(128), row(128)],
        out_specs=[row(1024), row(1024), row(512), row(256), row(128)],
        out_shape=[jax.ShapeDtypeStruct((T, 1024), BF16), jax.ShapeDtypeStruct((T, 1024), BF16),
                   jax.ShapeDtypeStruct((T, 512), BF16), jax.ShapeDtypeStruct((T, 256), BF16),
                   jax.ShapeDtypeStruct((T, 128), BF16)],
        compiler_params=_cp(("parallel",)))(proj, qnw, kvnw, wuq, wk, wv, cos, ss)


def _mla_prep_bwd(dq, dk, dv, proj, qnw, kvnw, wuq_t, wk_t, wv_t, cos, ss, *, name):
    T = proj.shape[0]
    tm = min(T, 512)

    def body(dq_ref, dk_ref, dv_ref, lat_ref, qnw_ref, kvnw_ref, wuq_ref, wk_ref, wv_ref, cos_ref, ss_ref,
             dlat_ref, dqp_ref, dqnw_ref, dkvnw_ref):
        @pl.when(pl.program_id(0) == 0)
        def _():
            dqnw_ref[...] = jnp.zeros_like(dqnw_ref)
            dkvnw_ref[...] = jnp.zeros_like(dkvnw_ref)
        cs, sn = cos_ref[...], ss_ref[...]
        dkpe = jnp.zeros((tm, 128), F32)
        for h in range(MLA_HEADS):
            hs = slice(128 * h, 128 * h + 128)
            dqp_ref[:, hs] = _rope_t(dq_ref[:, hs] * SCALE, cs, sn, 16, 32).astype(BF16)
            dkpe = dkpe + dk_ref[:, hs]
        lane = lax.broadcasted_iota(jnp.int32, (tm, 128), 1)
        rope_lane = (lane >= MLA_NOPE) & (lane < MLA_NOPE + MLA_ROPE)
        dg3 = jnp.where(rope_lane, _rope_t(jnp.where(rope_lane, dkpe, 0.0), cs, sn, 16, 32), 0.0)

        def norm_bwd(x, w, dn):
            r = lax.rsqrt(jnp.mean(x * x, axis=-1, keepdims=True) + EPS)
            xh = x * r
            g = dn * w
            return r * (g - xh * jnp.mean(g * xh, axis=-1, keepdims=True)), jnp.sum(dn * xh, axis=0, keepdims=True)

        dcqn = _dot(dqp_ref[...], wuq_ref[...])
        dcq, dqnw = norm_bwd(lat_ref[:, 0:256], qnw_ref[...], dcqn)
        dckvn = _dot(dk_ref[...].astype(BF16), wk_ref[...]) + _dot(dv_ref[...], wv_ref[...])
        dckv, dkvnw = norm_bwd(lat_ref[:, 256:384], kvnw_ref[...], dckvn)
        dqnw_ref[...] += dqnw
        dkvnw_ref[...] += dkvnw
        dlat_ref[:, 0:256] = dcq.astype(BF16)
        dlat_ref[:, 256:384] = dckv.astype(BF16)
        dlat_ref[:, 384:512] = dg3.astype(BF16)

    def full(shape):
        return pl.BlockSpec(shape, lambda i: (0, 0))

    def row(w):
        return pl.BlockSpec((tm, w), lambda i: (i, 0))

    return pl.pallas_call(
        body, name=name, grid=(T // tm,),
        in_specs=[row(1024), row(1024), row(512), pl.BlockSpec((tm, 512), lambda i: (i, 4)), full((1, 256)),
                  full((1, 128)), full((1024, 256)), full((1024, 128)), full((512, 128)), row(128), row(128)],
        out_specs=[row(512), row(1024), full((1, 256)), full((1, 128))],
        out_shape=[jax.ShapeDtypeStruct((T, 512), BF16), jax.ShapeDtypeStruct((T, 1024), BF16),
                   jax.ShapeDtypeStruct((1, 256), F32), jax.ShapeDtypeStruct((1, 128), F32)],
        compiler_params=_cp(("arbitrary",)))(dq, dk, dv, proj, qnw, kvnw, wuq_t, wk_t, wv_t, cos, ss)


def _flash_fwd(q, k, v, *, name):
    T = q.shape[0]
    tq = min(T, 512)
    tk = tq
    nq = T // tq

    def body(q_ref, k_ref, v_ref, y_ref, lse_ref):
        qi = pl.program_id(1)
        m0, m1 = _head_masks((tq, 128))
        row = lax.broadcasted_iota(jnp.int32, (tq, tk), 0)
        col = lax.broadcasted_iota(jnp.int32, (tq, tk), 1)
        def step(kb, carry, masked):
            ks = pl.ds(pl.multiple_of(kb * tk, tk), tk)
            vb = v_ref[ks, :]
            new = []
            for h in range(2):
                hs = slice(128 * h, 128 * h + 128)
                m, l, acc = carry[h]
                s = _dot_nt(q_ref[:, hs], k_ref[ks, hs])
                if masked:
                    s = jnp.where(col <= row, s, NEG)
                mn = jnp.maximum(m, jnp.max(s, axis=1, keepdims=True))
                p = jnp.exp(s - mn)
                al = jnp.exp(m - mn)
                l = al * l + jnp.sum(p, axis=1, keepdims=True)
                acc = al * acc + _dot(p.astype(BF16), vb)
                new.append((mn, l, acc))
            return tuple(new)

        init = (jnp.full((tq, 1), NEG, F32), jnp.zeros((tq, 1), F32), jnp.zeros((tq, 128), F32))
        carry = lax.fori_loop(0, qi, lambda kb, c: step(kb, c, False), (init, init))
        (ma, la, acca), (mb, lb, accb) = step(qi, carry, True)
        y_ref[...] = ((acca / la) * m0 + (accb / lb) * m1).astype(BF16)
        lse_ref[0] = ma + jnp.log(la)
        lse_ref[1] = mb + jnp.log(lb)

    return pl.pallas_call(
        body, name=name, grid=(4, nq),
        in_specs=[pl.BlockSpec((tq, 256), lambda p, i: (i, p)), pl.BlockSpec((T, 256), lambda p, i: (0, p)),
                  pl.BlockSpec((T, 128), lambda p, i: (0, p))],
        out_specs=[pl.BlockSpec((tq, 128), lambda p, i: (i, p)), pl.BlockSpec((2, tq, 1), lambda p, i: (p, i, 0))],
        out_shape=[jax.ShapeDtypeStruct((T, MLA_WIDTH), BF16), jax.ShapeDtypeStruct((MLA_HEADS, T, 1), F32)],
        compiler_params=_cp(("parallel", "arbitrary")))(q, k, v)


def _flash_bwd(q, k, v, do, lse, delta, *, name):
    T = q.shape[0]
    tq = min(T, 512)
    tk = tq
    nq = T // tq

    def body(q_ref, k_ref, v_ref, do_ref, lse_ref, dl_ref, dq_ref, dk_ref, dv_ref):
        kb = pl.program_id(1)

        @pl.when(kb == 0)
        def _():
            dq_ref[...] = jnp.zeros_like(dq_ref)
        krow = lax.broadcasted_iota(jnp.int32, (tk, tq), 0)
        qcol = lax.broadcasted_iota(jnp.int32, (tk, tq), 1)
        masks = _head_masks((tk, 128))
        vf = v_ref[...].astype(F32)
        vms = [(vf * masks[h]).astype(BF16) for h in range(2)]

        def step(qi, carry, masked):
            qs = pl.ds(pl.multiple_of(qi * tq, tq), tq)
            dob = do_ref[qs, :]
            dof = dob.astype(F32)
            dks, dv_acc = list(carry[:2]), carry[2]
            for h in range(2):
                hs = slice(128 * h, 128 * h + 128)
                kh = k_ref[:, hs]
                qh = q_ref[qs, hs]
                st = _dot_nt(kh, qh)
                pt = jnp.exp(st - lse_ref[h, qi])
                if masked:
                    pt = jnp.where(krow <= qcol, pt, 0.0)
                dv_acc = dv_acc + _dot(pt.astype(BF16), (dof * masks[h]).astype(BF16))
                dpt = _dot_nt(vms[h], dob)
                dst = (pt * (dpt - dl_ref[h, qi])).astype(BF16)
                dks[h] = dks[h] + _dot(dst, qh)
                dq_ref[qs, hs] += _dot_tn(dst, kh)
            return dks[0], dks[1], dv_acc

        zero = jnp.zeros((tk, 128), F32)
        carry = step(kb, (zero, zero, zero), True)
        dk0, dk1, dv_acc = lax.fori_loop(kb + 1, nq, lambda qi, c: step(qi, c, False), carry)
        dk_ref[:, 0:128] = dk0
        dk_ref[:, 128:256] = dk1
        dv_ref[...] = dv_acc.astype(BF16)

    stat = pl.BlockSpec((2, nq, 1, tq), lambda p, j: (p, 0, 0, 0))
    return pl.pallas_call(
        body, name=name, grid=(4, nq),
        in_specs=[pl.BlockSpec((T, 256), lambda p, j: (0, p)), pl.BlockSpec((tk, 256), lambda p, j: (j, p)),
                  pl.BlockSpec((tk, 128), lambda p, j: (j, p)), pl.BlockSpec((T, 128), lambda p, j: (0, p)), stat, stat],
        out_specs=[pl.BlockSpec((T, 256), lambda p, j: (0, p)), pl.BlockSpec((tk, 256), lambda p, j: (j, p)),
                   pl.BlockSpec((tk, 128), lambda p, j: (j, p))],
        out_shape=[jax.ShapeDtypeStruct((T, 1024), F32), jax.ShapeDtypeStruct((T, 1024), F32),
                   jax.ShapeDtypeStruct((T, MLA_WIDTH), BF16)],
        compiler_params=_cp(("parallel", "arbitrary")))(q, k, v, do, lse, delta)


def _shift_down(x, n, prev8):
    r = pltpu.roll(x, n, 0)
    row = lax.broadcasted_iota(jnp.int32, prev8.shape, 0)
    first = jnp.where(row < n, pltpu.roll(prev8, n, 0), r[:8])
    if x.shape[0] == 8:
        return first
    return jnp.concatenate([first, r[8:]], axis=0)


def _shift_up(x, n, next8):
    tm = x.shape[0]
    r = pltpu.roll(x, tm - n, 0)
    row = lax.broadcasted_iota(jnp.int32, next8.shape, 0)
    last = jnp.where(row >= 8 - n, pltpu.roll(next8, 8 - n, 0), r[tm - 8:])
    return jnp.concatenate([r[:tm - 8], last], axis=0)


def _conv_pre(u, prev8, cw_ref, cb_ref):
    p1 = _shift_down(u, 1, prev8)
    p2 = _shift_down(u, 2, prev8)
    up = cb_ref[...] + cw_ref[0:1, :] * p2 + cw_ref[1:2, :] * p1 + cw_ref[2:3, :] * u
    return up, p1, p2


def _conv_fwd(u, cw, cb, *, name):
    T = u.shape[0]
    tm = min(T, 512)
    W = 2 * FF_HALF

    def body(u_ref, prev_ref, cw_ref, cb_ref, a_ref):
        prev = jnp.where(pl.program_id(0) > 0, prev_ref[...], 0.0)
        up, _, _ = _conv_pre(u_ref[...], prev, cw_ref, cb_ref)
        gate = up[:, :FF_HALF]
        a_ref[...] = (gate * _sigmoid(gate) * up[:, FF_HALF:]).astype(BF16)

    return pl.pallas_call(
        body, name=name, grid=(T // tm, 2),
        in_specs=[pl.BlockSpec((tm, W), lambda i, j: (i, j)),
                  pl.BlockSpec((8, W), lambda i, j: (jnp.maximum(i * (tm // 8) - 1, 0), j)),
                  pl.BlockSpec((3, W), lambda i, j: (0, j)), pl.BlockSpec((1, W), lambda i, j: (0, j))],
        out_specs=pl.BlockSpec((tm, FF_HALF), lambda i, j: (i, j)),
        out_shape=jax.ShapeDtypeStruct((T, D_FF), BF16),
        compiler_params=_cp(("parallel", "parallel")))(u, u, cw, cb)


def _conv_bwd(u, da, cw, cb, *, name):
    T = u.shape[0]
    tm = min(T, 512)
    W = 2 * FF_HALF
    nt = T // tm

    def body(u_ref, prev_ref, next_ref, da_ref, dan_ref, cw_ref, cb_ref, du_ref, dw0_ref, dw1_ref, dw2_ref, db_ref):
        i = pl.program_id(1)

        @pl.when(i == 0)
        def _():
            for r in (dw0_ref, dw1_ref, dw2_ref, db_ref):
                r[...] = jnp.zeros_like(r)

        def dpre(u, prev8, da):
            up, p1, p2 = _conv_pre(u, prev8, cw_ref, cb_ref)
            gate, val = up[:, :FF_HALF], up[:, FF_HALF:]
            sg = _sigmoid(gate)
            dgate = da * val * (sg * (1.0 + gate * (1.0 - sg)))
            dval = da * (gate * sg)
            return jnp.concatenate([dgate, dval], axis=1), p1, p2

        u = u_ref[...]
        prev = jnp.where(i > 0, prev_ref[...], 0.0)
        dup, p1, p2 = dpre(u, prev, da_ref[...])
        dupn, _, _ = dpre(next_ref[...], u[tm - 8:], dan_ref[...])
        dupn = jnp.where(i < nt - 1, dupn, 0.0)
        du = cw_ref[2:3, :] * dup + cw_ref[1:2, :] * _shift_up(dup, 1, dupn) + cw_ref[0:1, :] * _shift_up(dup, 2, dupn)
        du_ref[...] = du.astype(BF16)
        dw0_ref[...] += jnp.sum(dup * p2, axis=0, keepdims=True)
        dw1_ref[...] += jnp.sum(dup * p1, axis=0, keepdims=True)
        dw2_ref[...] += jnp.sum(dup * u, axis=0, keepdims=True)
        db_ref[...] += jnp.sum(dup, axis=0, keepdims=True)

    nxt = lambda j, i: (jnp.minimum((i + 1) * (tm // 8), T // 8 - 1), j)
    vec = pl.BlockSpec((1, W), lambda j, i: (0, j))
    return pl.pallas_call(
        body, name=name, grid=(2, nt),
        in_specs=[pl.BlockSpec((tm, W), lambda j, i: (i, j)),
                  pl.BlockSpec((8, W), lambda j, i: (jnp.maximum(i * (tm // 8) - 1, 0), j)),
                  pl.BlockSpec((8, W), nxt),
                  pl.BlockSpec((tm, FF_HALF), lambda j, i: (i, j)), pl.BlockSpec((8, FF_HALF), nxt),
                  pl.BlockSpec((3, W), lambda j, i: (0, j)), vec],
        out_specs=[pl.BlockSpec((tm, W), lambda j, i: (i, j)), vec, vec, vec, vec],
        out_shape=[jax.ShapeDtypeStruct((T, 2 * D_FF), BF16)] + [jax.ShapeDtypeStruct((1, 2 * D_FF), F32)] * 4,
        compiler_params=_cp(("parallel", "arbitrary")))(u, u, u, da, da, cw, cb)


def _adamw(w, m, v, g_slots, *, name):
    R, C = w.shape
    ns = g_slots.shape[0]
    tr = _row_tile(R)

    def body(w_ref, m_ref, v_ref, g_ref, go_ref, d_ref, mo_ref, vo_ref):
        g = g_ref[0].astype(F32)
        for s in range(1, ns):
            g = g + g_ref[s].astype(F32)
        mn = ADAM_B1 * m_ref[...] + (1.0 - ADAM_B1) * g
        vn = ADAM_B2 * v_ref[...] + (1.0 - ADAM_B2) * (g * g)
        m_hat = mn / (1.0 - ADAM_B1 ** ADAM_STEP)
        v_hat = vn / (1.0 - ADAM_B2 ** ADAM_STEP)
        go_ref[...] = g
        d_ref[...] = -ADAM_LR * (m_hat / (jnp.sqrt(v_hat) + ADAM_EPS) + ADAM_WD * w_ref[...])
        mo_ref[...] = mn
        vo_ref[...] = vn

    blk = pl.BlockSpec((tr, C), lambda i: (i, 0))
    return pl.pallas_call(
        body, name=name, grid=(R // tr,),
        in_specs=[blk, blk, blk, pl.BlockSpec((ns, tr, C), lambda i: (0, i, 0))],
        out_specs=[blk] * 4, out_shape=[jax.ShapeDtypeStruct((R, C), F32)] * 4,
        compiler_params=_cp(("parallel",)))(w, m, v, g_slots)


def _place():
    return lax.axis_index("x"), lax.axis_index("y"), lax.axis_index("c")


def _all_gather(x, *, name, in_vmem):
    def body(x_ref, out_ref, send_sems, recv_sems, local_sem):
        x_, y_, c_ = _place()
        me, sibling = (x_, y_, c_), (x_, y_, 1 - c_)
        chips = [(1 - x_, y_), (x_, 1 - y_), (1 - x_, 1 - y_)]

        def slot(px, py, pc):
            return out_ref.at[4 * px + 2 * py + pc]

        def copy(k, block, to, src=None):
            return pltpu.make_async_remote_copy(
                src_ref=slot(*block) if src is None else src, dst_ref=slot(*block),
                send_sem=send_sems.at[k], recv_sem=recv_sems.at[k], device_id=to, device_id_type=MESH)

        mine = pltpu.make_async_copy(x_ref, slot(*me), local_sem)
        mine.start()
        first = [copy(0, me, sibling, src=x_ref)]
        first += [copy(1 + j, me, (*chip, c_), src=x_ref) for j, chip in enumerate(chips)]
        for cp in first:
            cp.start()
        passed = [copy(4 + j, (*chip, c_), sibling) for j, chip in enumerate(chips)]
        for j, chip in enumerate(chips):
            copy(1 + j, (*chip, c_), me).wait_recv()
            passed[j].start()
        copy(0, sibling, me).wait_recv()
        for j, chip in enumerate(chips):
            copy(4 + j, (*chip, 1 - c_), me).wait_recv()
        for cp in first + passed:
            cp.wait_send()
        mine.wait()

    spec = pl.BlockSpec(memory_space=pltpu.VMEM) if in_vmem else ANY
    return pl.pallas_call(
        body, name=name, out_shape=jax.ShapeDtypeStruct((N_DEV,) + x.shape, x.dtype),
        in_specs=[spec], out_specs=spec,
        scratch_shapes=[pltpu.SemaphoreType.DMA((7,)), pltpu.SemaphoreType.DMA((7,)), pltpu.SemaphoreType.DMA],
        compiler_params=pltpu.CompilerParams(vmem_limit_bytes=VMEM_LIMIT))(x)


def _sum_slots(g, *, name):
    n = g.shape[0]

    def body(g_ref, o_ref):
        acc = g_ref[0]
        for s in range(1, n):
            acc = acc + g_ref[s]
        o_ref[...] = acc

    return pl.pallas_call(body, name=name, out_shape=jax.ShapeDtypeStruct(g.shape[1:], g.dtype))(g)


def _swap_sibling(x, *, name):
    def body(x_ref, out_ref, send_sem, recv_sem):
        x_, y_, c_ = _place()
        cp = pltpu.make_async_remote_copy(src_ref=x_ref, dst_ref=out_ref, send_sem=send_sem, recv_sem=recv_sem,
                                          device_id=(x_, y_, 1 - c_), device_id_type=MESH)
        cp.start()
        cp.wait()

    return pl.pallas_call(
        body, name=name, out_shape=jax.ShapeDtypeStruct(x.shape, x.dtype), in_specs=[ANY], out_specs=ANY,
        scratch_shapes=[pltpu.SemaphoreType.DMA, pltpu.SemaphoreType.DMA])(x)


def _exchange_chips(p, *, name):
    def body(p_ref, out_ref, send_sems, recv_sems, local_sem):
        x_, y_, c_ = _place()
        me_k = 2 * x_ + y_
        chips = [(1 - x_, y_), (x_, 1 - y_), (1 - x_, 1 - y_)]
        local = pltpu.make_async_copy(p_ref.at[me_k], out_ref.at[me_k], local_sem)
        local.start()

        def copy(j, src_k, dst_k, chip):
            return pltpu.make_async_remote_copy(
                src_ref=p_ref.at[src_k], dst_ref=out_ref.at[dst_k], send_sem=send_sems.at[j],
                recv_sem=recv_sems.at[j], device_id=(*chip, c_), device_id_type=MESH)

        sends = [copy(j, 2 * px + py, me_k, (px, py)) for j, (px, py) in enumerate(chips)]
        for cp in sends:
            cp.start()
        for j, (px, py) in enumerate(chips):
            copy(j, me_k, 2 * px + py, (px, py)).wait_recv()
        for cp in sends:
            cp.wait_send()
        local.wait()

    return pl.pallas_call(
        body, name=name, out_shape=jax.ShapeDtypeStruct(p.shape, p.dtype), in_specs=[ANY], out_specs=ANY,
        scratch_shapes=[pltpu.SemaphoreType.DMA((3,)), pltpu.SemaphoreType.DMA((3,)), pltpu.SemaphoreType.DMA])(p)


def _row_tile(R):
    for cand in (256, 400, 200):
        if R % cand == 0:
            return cand
    return R


def _add2(a, b, *, name, out_dtype):
    n, R, C = a.shape
    tr = _row_tile(R)

    def body(a_ref, b_ref, o_ref):
        o_ref[...] = (a_ref[...] + b_ref[...]).astype(out_dtype)

    blk = pl.BlockSpec((1, tr, C), lambda s, i: (s, i, 0))
    return pl.pallas_call(body, name=name, grid=(n, R // tr), in_specs=[blk, blk], out_specs=blk,
                          out_shape=jax.ShapeDtypeStruct(a.shape, out_dtype),
                          compiler_params=_cp(("parallel", "parallel")))(a, b)


def _pack_local(parts):
    segs = []
    for n, r, rp, tr in BIG:
        w = parts[n].T if tr else parts[n]
        segs.append(jnp.pad(w.reshape(r, PACK_COLS), ((0, rp - r), (0, 0))))
    used = sum(rp for _, _, rp, _ in BIG)
    segs.append(jnp.zeros((PACK_ROWS - used, PACK_COLS), segs[0].dtype))
    return jnp.concatenate(segs, axis=0)


def _unpack_local(packed, like):
    out, off = {}, 0
    for n, r, rp, tr in BIG:
        rows, cols = like[n].shape
        seg = packed[off:off + r]
        out[n] = (seg.reshape(cols, rows).T if tr else seg)[None]
        off += rp
    return out


def _segments(g):
    out, off = {}, 0
    for n, r, rp, _ in BIG:
        out[n] = g[:, off:off + r]
        off += rp
    return out


def _pack_grads(g):
    g_in = jnp.concatenate([g["w_in_t"][:2432], g["w_in_t"][2496:2528]], axis=0).reshape(N_DEV, 308, PACK_COLS)
    g_uq = g["w_uq_t"].reshape(N_DEV, 128, MLA_Q_RANK)[:, :96].reshape(N_DEV, 24, PACK_COLS)
    g_ukv = jnp.concatenate([g["w_k_t"].reshape(N_DEV, 128, MLA_KV_RANK)[:, :64],
                             g["w_v_t"].reshape(N_DEV, 64, MLA_KV_RANK)], axis=1).reshape(N_DEV, 16, PACK_COLS)
    up = g["w_up_t"].reshape(N_DEV, 704, PACK_COLS)
    g_up = jnp.stack([up[FF_OWNER_ORDER.index(d)] for d in range(N_DEV)])
    parts = dict(w_in=g_in, w_uq=g_uq, w_ukv=g_ukv, w_out=g["w_out"].reshape(N_DEV, 128, PACK_COLS), w_up=g_up,
                 w_down=g["w_down"].reshape(N_DEV, 352, PACK_COLS))
    segs = [jnp.pad(parts[n], ((0, 0), (0, rp - r), (0, 0))) for n, r, rp, _ in BIG]
    used = sum(rp for _, _, rp, _ in BIG)
    segs.append(jnp.zeros((N_DEV, PACK_ROWS - used, PACK_COLS), F32))
    return jnp.concatenate(segs, axis=1)


def _interleave_ff(w):
    g, v = w[..., :D_FF], w[..., D_FF:]
    return jnp.concatenate([g[..., :FF_HALF], v[..., :FF_HALF], g[..., FF_HALF:], v[..., FF_HALF:]], axis=-1)


def _deinterleave_ff(w):
    b = [w[..., i * FF_HALF:(i + 1) * FF_HALF] for i in range(4)]
    return jnp.concatenate([b[0], b[2], b[1], b[3]], axis=-1)


def _rope_tables(pos):
    p = pos.astype(F32)[:, None]
    inv_r = ROPE_BASE ** (-jnp.arange(0, RET_HEAD_DIM, 2, dtype=F32) / RET_HEAD_DIM)
    ang = p * inv_r
    c, s = jnp.cos(ang), jnp.sin(ang)
    cos_r = jnp.concatenate([c, c, c, c], axis=1)
    ss_r = jnp.concatenate([-s, s, -s, s], axis=1)
    inv_m = ROPE_BASE ** (-jnp.arange(0, MLA_ROPE, 2, dtype=F32) / MLA_ROPE)
    ang = p * inv_m
    c, s = jnp.cos(ang), jnp.sin(ang)
    T = pos.shape[0]
    cos_m = jnp.concatenate([jnp.ones((T, 64), F32), c, c, jnp.ones((T, 32), F32)], axis=1)
    ss_m = jnp.concatenate([jnp.zeros((T, 64), F32), -s, s, jnp.zeros((T, 32), F32)], axis=1)
    return cos_r, ss_r, cos_m, ss_m


def _prep_weights(seg):
    w_in_t = seg["w_in"].reshape(IN_WIDTH, D_MODEL)
    z = lambda n: jnp.zeros((n, D_MODEL), BF16)
    w_in_t = jnp.concatenate([w_in_t[:2432], z(64), w_in_t[2432:2464], z(32)], axis=0)
    w_uq_t = jnp.pad(seg["w_uq"].reshape(MLA_HEADS, 96, MLA_Q_RANK), ((0, 0), (0, 32), (0, 0))).reshape(1024, MLA_Q_RANK)
    ukv = seg["w_ukv"].reshape(MLA_HEADS, 128, MLA_KV_RANK)
    w_k_t = jnp.pad(ukv[:, :64], ((0, 0), (0, 64), (0, 0))).reshape(1024, MLA_KV_RANK)
    w_v_t = ukv[:, 64:].reshape(512, MLA_KV_RANK)
    w_up_t = jnp.concatenate([seg["w_up"][d] for d in FF_OWNER_ORDER], axis=0)
    return dict(w_in_t=w_in_t, w_uq_t=w_uq_t, w_k_t=w_k_t, w_v_t=w_v_t, w_out=seg["w_out"].reshape(1024, D_MODEL),
                w_up_t=w_up_t, w_down=seg["w_down"].reshape(D_FF, D_MODEL))


def _local_step(x, pos, tgt, W, sm):
    cos_r, ss_r, cos_m, ss_m = _rope_tables(pos)
    tabs = _ret_tables()

    h = _rmsnorm_fwd(x, sm["attn_norm_w"], name="attn_norm")
    proj = _mm(h, W["w_in_t"], bt=True, name="in_proj")
    y_ret, o_ret = _ret_fwd(proj, cos_r, ss_r, tabs, sm["ret_gn_w"], name="ret_fwd")
    q, k, v, cqn, ckvn = _mla_prep_fwd(proj, sm["mla_q_norm_w"], sm["mla_kv_norm_w"], W["w_uq_t"], W["w_k_t"],
                                       W["w_v_t"], cos_m, ss_m, name="mla_prep")
    y_mla, lse = _flash_fwd(q, k, v, name="mla_attn")
    mixed = jnp.concatenate([y_ret, y_mla], axis=1)
    x1 = _mm(mixed, W["w_out"], add=x, name="out_proj")
    h2 = _rmsnorm_fwd(x1, sm["ffn_norm_w"], name="ffn_norm")
    u = _mm(h2, W["w_up_t"], bt=True, name="up_proj")
    a = _conv_fwd(u, sm["conv_w"], sm["conv_b"], name="conv_gate")
    x2 = _mm(a, W["w_down"], add=x1, name="down_proj")
    loss, dx2, d_final = _loss_head(x2, tgt, sm["final_norm_w"], name="loss_head")

    g = {}
    g["w_down"] = _mm_tn(a, dx2, name="dw_down")
    da = _mm(dx2, W["w_down"], bt=True, name="d_act")
    du, dcw0, dcw1, dcw2, dcb = _conv_bwd(u, da, sm["conv_w"], sm["conv_b"], name="conv_bwd")
    g["w_up_t"] = _mm_tn(du, h2, name="dw_up")
    dh2 = _mm(du, W["w_up_t"], name="d_h2")
    dx1, d_ffn = _rmsnorm_bwd(x1, sm["ffn_norm_w"], dh2, dx2, name="ffn_norm_bwd")

    g["w_out"] = _mm_tn(mixed, dx1, name="dw_out")
    dmixed = _mm(dx1, W["w_out"], bt=True, name="d_mixed")
    do_ret, dg, do_mla, delta, d_gn = _mix_bwd(dmixed, o_ret, proj, y_mla, sm["ret_gn_w"], name="mix_bwd")
    drq = _ret_bwd_dq(proj, do_ret, cos_r, ss_r, tabs, name="ret_bwd_dq")
    drk, drv = _ret_bwd_dkv(proj, do_ret, cos_r, ss_r, tabs, name="ret_bwd_dkv")
    T = x.shape[0]
    tq = min(T, 512)
    lse_r = lse.reshape(MLA_HEADS, T // tq, 1, tq)
    delta_r = delta[:, :MLA_HEADS].T.reshape(MLA_HEADS, T // tq, 1, tq)
    dq, dk, dv = _flash_bwd(q, k, v, do_mla, lse_r, delta_r, name="mla_attn_bwd")
    dlat, dqp, d_qn, d_kvn = _mla_prep_bwd(dq, dk, dv, proj, sm["mla_q_norm_w"], sm["mla_kv_norm_w"], W["w_uq_t"],
                                           W["w_k_t"], W["w_v_t"], cos_m, ss_m, name="mla_prep_bwd")
    g["w_uq_t"] = _mm_tn(dqp, cqn, name="dw_uq")
    g["w_k_t"] = _mm_tn(dk, ckvn, name="dw_ukv_k")
    g["w_v_t"] = _mm_tn(dv, ckvn, name="dw_ukv_v")
    dproj = jnp.concatenate([drq, drk, drv, dg, dlat], axis=1)
    g["w_in_t"] = _mm_tn(dproj, h, name="dw_in")
    dh = _mm(dproj, W["w_in_t"], name="d_h")
    grad_x, d_attn = _rmsnorm_bwd(x, sm["attn_norm_w"], dh, dx1, name="attn_norm_bwd")

    small = dict(attn_norm_w=d_attn, ret_gn_w=d_gn, mla_q_norm_w=d_qn, mla_kv_norm_w=d_kvn, ffn_norm_w=d_ffn,
                 conv_b=_deinterleave_ff(dcb), final_norm_w=d_final,
                 conv_w=_deinterleave_ff(jnp.concatenate([dcw0, dcw1, dcw2], axis=0)))
    return loss, grad_x, g, small


def kernel(x, positions, attn_norm_w, w_in, ret_gn_w, mla_q_norm_w, w_uq, mla_kv_norm_w, w_ukv, w_out, ffn_norm_w, w_up, conv_w, conv_b, w_down, final_norm_w, loss_target, m_attn_norm_w, m_w_in, m_ret_gn_w, m_mla_q_norm_w, m_w_uq, m_mla_kv_norm_w, m_w_ukv, m_w_out, m_ffn_norm_w, m_w_up, m_conv_w, m_conv_b, m_w_down, m_final_norm_w, v_attn_norm_w, v_w_in, v_ret_gn_w, v_mla_q_norm_w, v_w_uq, v_mla_kv_norm_w, v_w_ukv, v_w_out, v_ffn_norm_w, v_w_up, v_conv_w, v_conv_b, v_w_down, v_final_norm_w):
    a = dict(locals())
    x_, y_, c_ = _place()
    dev = 4 * x_ + 2 * y_ + c_

    shard = {n: a[n][0] for n, _, _, _ in BIG}
    gathered = _all_gather(_pack_local({n: w.astype(BF16) for n, w in shard.items()}), name="gather_weights", in_vmem=False)
    W = _prep_weights(_segments(gathered))
    cw_pad = jnp.pad(conv_w[0].reshape(-1), (0, 24 * 128 - 3 * 704)).reshape(24, 128)
    cw_all = _all_gather(cw_pad, name="gather_conv_w", in_vmem=True)
    conv_w_full = cw_all.reshape(N_DEV, -1)[:, :3 * 704].reshape(N_DEV, 3, 704).transpose(1, 0, 2).reshape(3, 2 * D_FF)
    sm = dict(attn_norm_w=attn_norm_w, ret_gn_w=ret_gn_w, mla_q_norm_w=mla_q_norm_w, mla_kv_norm_w=mla_kv_norm_w,
              ffn_norm_w=ffn_norm_w, final_norm_w=final_norm_w.reshape(1, D_MODEL),
              conv_w=_interleave_ff(conv_w_full), conv_b=_interleave_ff(conv_b))

    loss, grad_x, g, gs = _local_step(x[0], positions[0], loss_target[0], W, sm)

    gp = _pack_grads(g).reshape(4, 2, PACK_ROWS, PACK_COLS)
    mine = lax.dynamic_index_in_dim(gp, c_, axis=1, keepdims=False)
    theirs = lax.dynamic_index_in_dim(gp, 1 - c_, axis=1, keepdims=False)
    pair = _add2(mine, _swap_sibling(theirs, name="grad_swap_sibling"), out_dtype=BF16, name="grad_pair_sum")
    slots = _exchange_chips(pair, name="grad_exchange_chips")
    big = _adamw(_pack_local(shard), _pack_local({n: a["m_" + n][0] for n, _, _, _ in BIG}),
                 _pack_local({n: a["v_" + n][0] for n, _, _, _ in BIG}), slots, name="adamw_large")
    big = [_unpack_local(t, shard) for t in big]

    vec = jnp.concatenate([gs[n].reshape(-1) for n, _ in SMALL] + [gs["conv_w"].reshape(-1), loss[0, :1]])
    vec = jnp.pad(vec, (0, SMALL_ROWS * 128 - vec.shape[0])).reshape(SMALL_ROWS, 128)
    tot = _sum_slots(_all_gather(vec, name="gather_small_grads", in_vmem=True), name="sum_small_grads").reshape(-1)
    loss_out = tot[SMALL_N + 3 * 2 * D_FF]
    g_cw = lax.dynamic_slice_in_dim(tot[SMALL_N:SMALL_N + 3 * 2 * D_FF].reshape(3, 2 * D_FF), dev * 704, 704, axis=1)

    def flat_small(prefix):
        return jnp.concatenate([a[prefix + n].reshape(-1) for n, _ in SMALL]).reshape(75, 128)

    sml = _adamw(flat_small(""), flat_small("m_"), flat_small("v_"), tot[:SMALL_N].reshape(1, 75, 128), name="adamw_small")
    cwo = _adamw(conv_w[0], m_conv_w[0], v_conv_w[0], g_cw[None], name="adamw_conv_w")

    def small_of(t, n):
        off = 0
        for nm, sz in SMALL:
            if nm == n:
                return t.reshape(-1)[off:off + sz].reshape(a[n].shape)
            off += sz

    names = ['attn_norm_w', 'w_in', 'ret_gn_w', 'mla_q_norm_w', 'w_uq', 'mla_kv_norm_w', 'w_ukv', 'w_out',
             'ffn_norm_w', 'w_up', 'conv_w', 'conv_b', 'w_down', 'final_norm_w']
    outs = [loss_out, grad_x[None]]
    for kind in range(4):
        for n in names:
            if n == "conv_w":
                outs.append(cwo[kind][None])
            elif n in big[kind]:
                outs.append(big[kind][n])
            else:
                outs.append(small_of(sml[kind], n))
    return tuple(outs)
```

```python
import functools

import numpy as np
import jax
import jax.numpy as jnp
from jax import lax
from jax.experimental import pallas as pl
from jax.experimental.pallas import tpu as pltpu

F32 = jnp.float32
BF16 = jnp.bfloat16
MESH = pl.DeviceIdType.MESH
ANY = pl.BlockSpec(memory_space=pl.ANY)

D_MODEL = 1024
RET_HEADS = 8
RET_HEAD_DIM = 64
RET_WIDTH = 512
RET_CHUNK = 128
MLA_HEADS = 8
MLA_NOPE = 64
MLA_ROPE = 32
MLA_V = 64
MLA_Q_RANK = 256
MLA_KV_RANK = 128
MLA_WIDTH = 512
IN_WIDTH = 2464
IN_PAD = 2560
D_FF = 2816
FF_HALF = 1408
ROPE_BASE = 10000.0
EPS = 1e-6
SCALE = float((MLA_NOPE + MLA_ROPE) ** -0.5)
K_SCALE = 0.125
N_DEV = 8

ADAM_LR = 0.001
ADAM_B1 = 0.9
ADAM_B2 = 0.999
ADAM_EPS = 1e-08
ADAM_WD = 0.01
ADAM_STEP = 10

VMEM_LIMIT = 56 * 1024 * 1024
MM_BUDGET = 40 * 1024 * 1024
NEG = -1e30

BIG = (("w_in", 308, 320, True), ("w_uq", 24, 32, True), ("w_ukv", 16, 16, True),
       ("w_out", 128, 128, False), ("w_up", 704, 704, True), ("w_down", 352, 352, False))
PACK_COLS = 1024
PACK_ROWS = 1600
FF_OWNER_ORDER = (0, 1, 4, 5, 2, 3, 6, 7)
SMALL = (("attn_norm_w", 1024), ("ret_gn_w", 512), ("mla_q_norm_w", 256), ("mla_kv_norm_w", 128),
         ("ffn_norm_w", 1024), ("conv_b", 5632), ("final_norm_w", 1024))
SMALL_N = 9600
SMALL_ROWS = 208


def _cp(sem=None, vmem=VMEM_LIMIT):
    return pltpu.CompilerParams(dimension_semantics=sem, vmem_limit_bytes=vmem)


def _dot(a, b):
    return jnp.dot(a, b, preferred_element_type=F32)


def _dot_nt(a, b):
    return lax.dot_general(a, b, (((1,), (1,)), ((), ())), preferred_element_type=F32)


def _dot_tn(a, b):
    return lax.dot_general(a, b, (((0,), (0,)), ((), ())), preferred_element_type=F32)


def _sigmoid(x):
    return 0.5 * jnp.tanh(0.5 * x) + 0.5


def _partner(x, half, period):
    n = x.shape[-1]
    lane = lax.broadcasted_iota(jnp.int32, x.shape, 1)
    return jnp.where((lane % period) < half, pltpu.roll(x, n - half, 1), pltpu.roll(x, half, 1))


def _rope(x, cos, ss, half, period):
    return x * cos + _partner(x, half, period) * ss


def _rope_t(dy, cos, ss, half, period):
    return dy * cos - _partner(dy, half, period) * ss


def _head_masks(shape):
    lane = lax.broadcasted_iota(jnp.int32, shape, 1)
    m0 = (lane < 64).astype(F32)
    return m0, 1.0 - m0


def _mm(a, b, *, name, add=None, out_dtype=F32, bt=False):
    M, K = a.shape
    N = b.shape[0] if bt else b.shape[1]
    osz = jnp.dtype(out_dtype).itemsize
    per_row = 2 * (K * a.dtype.itemsize + N * osz + (N * 4 if add is not None else 0))
    tm = 128
    for cand in (512, 256):
        if M % cand == 0 and cand * per_row + 4 * K * N <= MM_BUDGET:
            tm = cand
            break
    tm = min(tm, M)
    mul = _dot_nt if bt else _dot

    def body(*refs):
        if add is None:
            a_ref, b_ref, o_ref = refs
            acc = mul(a_ref[...].astype(BF16), b_ref[...])
        else:
            a_ref, b_ref, r_ref, o_ref = refs
            acc = r_ref[...] + mul(a_ref[...].astype(BF16), b_ref[...])
        o_ref[...] = acc.astype(out_dtype)

    in_specs = [pl.BlockSpec((tm, K), lambda i: (i, 0)), pl.BlockSpec(b.shape, lambda i: (0, 0))]
    args = [a, b]
    if add is not None:
        in_specs.append(pl.BlockSpec((tm, N), lambda i: (i, 0)))
        args.append(add)
    return pl.pallas_call(
        body, name=name, grid=(M // tm,), in_specs=in_specs,
        out_specs=pl.BlockSpec((tm, N), lambda i: (i, 0)),
        out_shape=jax.ShapeDtypeStruct((M, N), out_dtype),
        compiler_params=_cp(("parallel",)))(*args)


def _mm_tn(a, b, *, name):
    T, M = a.shape
    N = b.shape[1]
    tk = min(T, 512)

    def tile(n):
        for cand in (1408, 1280):
            if n > 1408 and n % cand == 0:
                return cand
        return n

    tm, tn = tile(M), tile(N)
    nk = T // tk

    def body(a_ref, b_ref, o_ref):
        @pl.when(pl.program_id(2) == 0)
        def _():
            o_ref[...] = jnp.zeros_like(o_ref)
        o_ref[...] += _dot_tn(a_ref[...].astype(BF16), b_ref[...].astype(BF16))

    return pl.pallas_call(
        body, name=name, grid=(M // tm, N // tn, nk),
        in_specs=[pl.BlockSpec((tk, tm), lambda i, j, k: (k, i)), pl.BlockSpec((tk, tn), lambda i, j, k: (k, j))],
        out_specs=pl.BlockSpec((tm, tn), lambda i, j, k: (i, j)),
        out_shape=jax.ShapeDtypeStruct((M, N), F32),
        compiler_params=_cp(("parallel", "parallel", "arbitrary")))(a, b)


def _rmsnorm_fwd(x, w, *, name):
    T, D = x.shape
    tm = min(T, 1024)

    def body(x_ref, w_ref, o_ref):
        xv = x_ref[...]
        r = lax.rsqrt(jnp.mean(xv * xv, axis=-1, keepdims=True) + EPS)
        o_ref[...] = (xv * r * w_ref[...]).astype(BF16)

    return pl.pallas_call(
        body, name=name, grid=(T // tm,),
        in_specs=[pl.BlockSpec((tm, D), lambda i: (i, 0)), pl.BlockSpec((1, D), lambda i: (0, 0))],
        out_specs=pl.BlockSpec((tm, D), lambda i: (i, 0)),
        out_shape=jax.ShapeDtypeStruct((T, D), BF16),
        compiler_params=_cp(("parallel",)))(x, w)


def _rmsnorm_bwd(x, w, dh, dres, *, name):
    T, D = x.shape
    tm = min(T, 512)

    def body(x_ref, w_ref, dh_ref, dr_ref, dx_ref, dw_ref):
        @pl.when(pl.program_id(0) == 0)
        def _():
            dw_ref[...] = jnp.zeros_like(dw_ref)
        xv = x_ref[...]
        r = lax.rsqrt(jnp.mean(xv * xv, axis=-1, keepdims=True) + EPS)
        xh = xv * r
        dh = dh_ref[...]
        g = dh * w_ref[...]
        dx_ref[...] = dr_ref[...] + r * (g - xh * jnp.mean(g * xh, axis=-1, keepdims=True))
        dw_ref[...] += jnp.sum(dh * xh, axis=0, keepdims=True)

    row = pl.BlockSpec((tm, D), lambda i: (i, 0))
    vec = pl.BlockSpec((1, D), lambda i: (0, 0))
    return pl.pallas_call(
        body, name=name, grid=(T // tm,), in_specs=[row, vec, row, row], out_specs=[row, vec],
        out_shape=[jax.ShapeDtypeStruct((T, D), F32), jax.ShapeDtypeStruct((1, D), F32)],
        compiler_params=_cp(("arbitrary",)))(x, w, dh, dres)


def _loss_head(x2, tgt, w, *, name):
    T, D = x2.shape
    tm = min(T, 512)

    def body(x_ref, t_ref, w_ref, loss_ref, dx_ref, dw_ref):
        @pl.when(pl.program_id(0) == 0)
        def _():
            dw_ref[...] = jnp.zeros_like(dw_ref)
            loss_ref[...] = jnp.zeros_like(loss_ref)
        xv = x_ref[...]
        wv = w_ref[...]
        r = lax.rsqrt(jnp.mean(xv * xv, axis=-1, keepdims=True) + EPS)
        xh = xv * r
        e = xh * wv - t_ref[...]
        part = 0.5 * jnp.sum(jnp.mean(e * e, axis=-1, keepdims=True), axis=0, keepdims=True)
        loss_ref[...] += jnp.broadcast_to(part, loss_ref.shape)
        dy = e * (1.0 / D)
        g = dy * wv
        dx_ref[...] = r * (g - xh * jnp.mean(g * xh, axis=-1, keepdims=True))
        dw_ref[...] += jnp.sum(dy * xh, axis=0, keepdims=True)

    row = pl.BlockSpec((tm, D), lambda i: (i, 0))
    vec = pl.BlockSpec((1, D), lambda i: (0, 0))
    return pl.pallas_call(
        body, name=name, grid=(T // tm,), in_specs=[row, row, vec],
        out_specs=[pl.BlockSpec((1, 128), lambda i: (0, 0)), row, vec],
        out_shape=[jax.ShapeDtypeStruct((1, 128), F32), jax.ShapeDtypeStruct((T, D), F32),
                   jax.ShapeDtypeStruct((1, D), F32)],
        compiler_params=_cp(("arbitrary",)))(x2, tgt, w)


def _ret_tables():
    C = RET_CHUNK
    h = jnp.arange(RET_HEADS, dtype=F32)
    log_gamma = jnp.log1p(-jnp.power(2.0, -5.0 - h))
    idx = jnp.arange(C, dtype=F32)
    diff = idx[:, None] - idx[None, :]
    dm = jnp.where(diff >= 0, jnp.exp(log_gamma[:, None, None] * jnp.maximum(diff, 0.0)), 0.0)
    dm = dm.reshape(4, 2 * C, C)
    lane_head = jnp.repeat(jnp.arange(RET_HEADS).reshape(4, 2), 64, axis=1)
    lg = log_gamma[lane_head]
    xi = jnp.exp(lg[:, None, :] * (idx[None, :, None] + 1.0))
    zeta = jnp.exp(lg[:, None, :] * (C - 1.0 - idx[None, :, None]))
    blk = (jnp.arange(128)[:, None] // 64) == (jnp.arange(128)[None, :] // 64)
    cd = jnp.where(blk[None], jnp.exp(lg * C)[:, :, None], 0.0)
    return dm.astype(F32), xi.astype(F32), zeta.astype(F32), cd.astype(F32)


def _ret_specs(tb, rev, nt):
    def tmap(t):
        return (nt - 1 - t) if rev else t
    qkv = [pl.BlockSpec((tb, 128), lambda p, t, o=o: (tmap(t), o + p)) for o in (0, 4, 8)]
    rope = [pl.BlockSpec((tb, 128), lambda p, t: (tmap(t), 0))] * 2
    tabs = [pl.BlockSpec((None, 256, 128), lambda p, t: (p, 0, 0))] + \
           [pl.BlockSpec((None, 128, 128), lambda p, t: (p, 0, 0))] * 3
    return qkv, rope, tabs


def _ret_fwd(proj, cos, ss, tabs, gnw, *, name):
    T = proj.shape[0]
    tb = min(T, 1024)
    nt = T // tb
    nchunk = tb // RET_CHUNK

    def body(q_ref, k_ref, v_ref, g_ref, cos_ref, ss_ref, dm_ref, xi_ref, zt_ref, cd_ref, gnw_ref,
             y_ref, o_ref, r_sc):
        @pl.when(pl.program_id(1) == 0)
        def _():
            r_sc[...] = jnp.zeros_like(r_sc)
        m0, m1 = _head_masks((128, 128))
        dm, xi, zt, cd = dm_ref[...], xi_ref[...], zt_ref[...], cd_ref[...]
        bm = (cd > 0).astype(F32)
        gnw = gnw_ref[...]
        for c in range(nchunk):
            rs = pl.ds(c * RET_CHUNK, RET_CHUNK)
            cs, sn = cos_ref[rs, :], ss_ref[rs, :]
            q = _rope(q_ref[rs, :], cs, sn, 32, 64)
            k = _rope(k_ref[rs, :], cs, sn, 32, 64) * K_SCALE
            v = v_ref[rs, :]
            kb, vb = k.astype(BF16), v.astype(BF16)
            qs = jnp.concatenate([q * m0, q * m1], axis=0).astype(BF16)
            s = (_dot_nt(qs, kb) * dm).astype(BF16)
            vs = jnp.concatenate([v * m0, v * m1], axis=0).astype(BF16)
            o = _dot(jnp.concatenate([s[:128], s[128:]], axis=1), vs)
            r = r_sc[...]
            o = o + _dot(q.astype(BF16), r.astype(BF16)) * xi
            r_sc[...] = cd * r + bm * _dot_tn((k * zt).astype(BF16), vb)
            mu = (jnp.sum(o * m0, axis=1, keepdims=True) * m0 + jnp.sum(o * m1, axis=1, keepdims=True) * m1) * (1.0 / 64)
            d = o - mu
            dd = d * d
            var = (jnp.sum(dd * m0, axis=1, keepdims=True) * m0 + jnp.sum(dd * m1, axis=1, keepdims=True) * m1) * (1.0 / 64)
            oh = d * lax.rsqrt(var + EPS)
            g = g_ref[rs, :]
            y_ref[rs, :] = (g * _sigmoid(g) * (oh * gnw)).astype(BF16)
            o_ref[rs, :] = o

    qkv, rope, tspec = _ret_specs(tb, False, nt)
    gspec = pl.BlockSpec((tb, 128), lambda p, t: (t, 12 + p))
    out = pl.BlockSpec((tb, 128), lambda p, t: (t, p))
    return pl.pallas_call(
        body, name=name, grid=(4, nt),
        in_specs=qkv + [gspec] + rope + tspec + [pl.BlockSpec((1, 128), lambda p, t: (0, p))],
        out_specs=[out, out],
        out_shape=[jax.ShapeDtypeStruct((T, RET_WIDTH), BF16), jax.ShapeDtypeStruct((T, RET_WIDTH), F32)],
        scratch_shapes=[pltpu.VMEM((128, 128), F32)],
        compiler_params=_cp(("parallel", "arbitrary")))(proj, proj, proj, proj, cos, ss, *tabs, gnw)


def _ret_bwd_dq(proj, do, cos, ss, tabs, *, name):
    T = proj.shape[0]
    tb = min(T, 1024)
    nt = T // tb
    nchunk = tb // RET_CHUNK

    def body(q_ref, k_ref, v_ref, do_ref, cos_ref, ss_ref, dm_ref, xi_ref, zt_ref, cd_ref, dq_ref, r_sc):
        del q_ref
        @pl.when(pl.program_id(1) == 0)
        def _():
            r_sc[...] = jnp.zeros_like(r_sc)
        m0, m1 = _head_masks((128, 128))
        dm, xi, zt, cd = dm_ref[...], xi_ref[...], zt_ref[...], cd_ref[...]
        bm = (cd > 0).astype(F32)
        for c in range(nchunk):
            rs = pl.ds(c * RET_CHUNK, RET_CHUNK)
            cs, sn = cos_ref[rs, :], ss_ref[rs, :]
            k = _rope(k_ref[rs, :], cs, sn, 32, 64) * K_SCALE
            vb = v_ref[rs, :].astype(BF16)
            dob = do_ref[rs, :]
            dof = dob.astype(F32)
            dos = jnp.concatenate([dof * m0, dof * m1], axis=0).astype(BF16)
            a = (_dot_nt(dos, vb) * dm).astype(BF16)
            ks = jnp.concatenate([k * m0, k * m1], axis=0).astype(BF16)
            r = r_sc[...]
            dq = _dot(jnp.concatenate([a[:128], a[128:]], axis=1), ks) + _dot_nt(dob, r.astype(BF16)) * xi
            r_sc[...] = cd * r + bm * _dot_tn((k * zt).astype(BF16), vb)
            dq_ref[rs, :] = _rope_t(dq, cs, sn, 32, 64).astype(BF16)

    qkv, rope, tspec = _ret_specs(tb, False, nt)
    blk = pl.BlockSpec((tb, 128), lambda p, t: (t, p))
    return pl.pallas_call(
        body, name=name, grid=(4, nt), in_specs=qkv + [blk] + rope + tspec, out_specs=blk,
        out_shape=jax.ShapeDtypeStruct((T, RET_WIDTH), BF16),
        scratch_shapes=[pltpu.VMEM((128, 128), F32)],
        compiler_params=_cp(("parallel", "arbitrary")))(proj, proj, proj, do, cos, ss, *tabs)


def _ret_bwd_dkv(proj, do, cos, ss, tabs, *, name):
    T = proj.shape[0]
    tb = min(T, 1024)
    nt = T // tb
    nchunk = tb // RET_CHUNK

    def body(q_ref, k_ref, v_ref, do_ref, cos_ref, ss_ref, dm_ref, xi_ref, zt_ref, cd_ref, dk_ref, dv_ref, u_sc):
        @pl.when(pl.program_id(1) == 0)
        def _():
            u_sc[...] = jnp.zeros_like(u_sc)
        m0, m1 = _head_masks((128, 128))
        dm, xi, zt, cd = dm_ref[...], xi_ref[...], zt_ref[...], cd_ref[...]
        bm = (cd > 0).astype(F32)
        for c in reversed(range(nchunk)):
            rs = pl.ds(c * RET_CHUNK, RET_CHUNK)
            cs, sn = cos_ref[rs, :], ss_ref[rs, :]
            q = _rope(q_ref[rs, :], cs, sn, 32, 64)
            k = _rope(k_ref[rs, :], cs, sn, 32, 64) * K_SCALE
            kb = k.astype(BF16)
            vb = v_ref[rs, :].astype(BF16)
            dob = do_ref[rs, :]
            dof = dob.astype(F32)
            qs = jnp.concatenate([q * m0, q * m1], axis=0).astype(BF16)
            dos = jnp.concatenate([dof * m0, dof * m1], axis=0).astype(BF16)
            s = (_dot_nt(qs, kb) * dm).astype(BF16)
            a = (_dot_nt(dos, vb) * dm).astype(BF16)
            ub = u_sc[...].astype(BF16)
            dk = _dot_tn(a, qs) + _dot_nt(vb, ub) * zt
            dv = _dot_tn(s, dos) + _dot(kb, ub) * zt
            u_sc[...] = cd * u_sc[...] + bm * _dot_tn((q * xi).astype(BF16), dob)
            dk_ref[rs, :] = (_rope_t(dk, cs, sn, 32, 64) * K_SCALE).astype(BF16)
            dv_ref[rs, :] = dv.astype(BF16)

    qkv, rope, tspec = _ret_specs(tb, True, nt)
    blk = pl.BlockSpec((tb, 128), lambda p, t: (nt - 1 - t, p))
    return pl.pallas_call(
        body, name=name, grid=(4, nt), in_specs=qkv + [blk] + rope + tspec, out_specs=[blk, blk],
        out_shape=[jax.ShapeDtypeStruct((T, RET_WIDTH), BF16)] * 2,
        scratch_shapes=[pltpu.VMEM((128, 128), F32)],
        compiler_params=_cp(("parallel", "arbitrary")))(proj, proj, proj, do, cos, ss, *tabs)


def _mix_bwd(dmixed, o_ret, proj, y_mla, gnw, *, name):
    T = dmixed.shape[0]
    tm = min(T, 512)

    def body(dm_ref, o_ref, g_ref, ym_ref, gnw_ref, do_ref, dg_ref, dom_ref, dl_ref, dw_ref):
        @pl.when(pl.program_id(0) == 0)
        def _():
            dw_ref[...] = jnp.zeros_like(dw_ref)
        m0, m1 = _head_masks((tm, 128))
        lane = lax.broadcasted_iota(jnp.int32, (tm, 128), 1)
        delta = jnp.zeros((tm, 128), F32)

        def gsum(z):
            return jnp.sum(z * m0, axis=1, keepdims=True) * m0 + jnp.sum(z * m1, axis=1, keepdims=True) * m1

        for p in range(4):
            cs = slice(128 * p, 128 * p + 128)
            dy = dm_ref[:, cs]
            o = o_ref[:, cs]
            g = g_ref[:, cs]
            w = gnw_ref[:, cs]
            d = o - gsum(o) * (1.0 / 64)
            rstd = lax.rsqrt(gsum(d * d) * (1.0 / 64) + EPS)
            oh = d * rstd
            sg = _sigmoid(g)
            dn = dy * (g * sg)
            dg_ref[:, cs] = (dy * (oh * w) * (sg * (1.0 + g * (1.0 - sg)))).astype(BF16)
            dw_ref[:, cs] += jnp.sum(dn * oh, axis=0, keepdims=True)
            doh = dn * w
            do = rstd * (doh - gsum(doh) * (1.0 / 64) - oh * (gsum(doh * oh) * (1.0 / 64)))
            do_ref[:, cs] = do.astype(BF16)
            dom = dm_ref[:, 512 + 128 * p:512 + 128 * p + 128]
            dom_ref[:, cs] = dom.astype(BF16)
            pr = dom * ym_ref[:, cs].astype(F32)
            delta = jnp.where(lane == 2 * p, jnp.sum(pr * m0, axis=1, keepdims=True), delta)
            delta = jnp.where(lane == 2 * p + 1, jnp.sum(pr * m1, axis=1, keepdims=True), delta)
        dl_ref[...] = delta

    half = pl.BlockSpec((tm, 512), lambda i: (i, 0))
    return pl.pallas_call(
        body, name=name, grid=(T // tm,),
        in_specs=[pl.BlockSpec((tm, 1024), lambda i: (i, 0)), half, pl.BlockSpec((tm, 512), lambda i: (i, 3)),
                  half, pl.BlockSpec((1, 512), lambda i: (0, 0))],
        out_specs=[half, half, half, pl.BlockSpec((tm, 128), lambda i: (i, 0)), pl.BlockSpec((1, 512), lambda i: (0, 0))],
        out_shape=[jax.ShapeDtypeStruct((T, 512), BF16)] * 3 + [jax.ShapeDtypeStruct((T, 128), F32),
                                                                jax.ShapeDtypeStruct((1, 512), F32)],
        compiler_params=_cp(("arbitrary",)))(dmixed, o_ret, proj, y_mla, gnw)


def _mla_prep_fwd(proj, qnw, kvnw, wuq, wk, wv, cos, ss, *, name):
    T = proj.shape[0]
    tm = min(T, 512)

    def body(lat_ref, qnw_ref, kvnw_ref, wuq_ref, wk_ref, wv_ref, cos_ref, ss_ref,
             q_ref, k_ref, v_ref, cqn_ref, ckvn_ref):
        cq = lat_ref[:, 0:256]
        ckv = lat_ref[:, 256:384]
        g3 = lat_ref[:, 384:512]
        cqn = (cq * lax.rsqrt(jnp.mean(cq * cq, axis=-1, keepdims=True) + EPS) * qnw_ref[...]).astype(BF16)
        ckvn = (ckv * lax.rsqrt(jnp.mean(ckv * ckv, axis=-1, keepdims=True) + EPS) * kvnw_ref[...]).astype(BF16)
        cqn_ref[...] = cqn
        ckvn_ref[...] = ckvn
        cs, sn = cos_ref[...], ss_ref[...]
        q = _dot_nt(cqn, wuq_ref[...])
        k = _dot_nt(ckvn, wk_ref[...])
        kpe = _rope(g3, cs, sn, 16, 32)
        for h in range(MLA_HEADS):
            hs = slice(128 * h, 128 * h + 128)
            q_ref[:, hs] = (_rope(q[:, hs], cs, sn, 16, 32) * SCALE).astype(BF16)
            k_ref[:, hs] = (k[:, hs] + kpe).astype(BF16)
        v_ref[...] = _dot_nt(ckvn, wv_ref[...]).astype(BF16)

    def full(shape):
        return pl.BlockSpec(shape, lambda i: (0, 0))

    def row(w):
        return pl.BlockSpec((tm, w), lambda i: (i, 0))

    return pl.pallas_call(
        body, name=name, grid=(T // tm,),
        in_specs=[pl.BlockSpec((tm, 512), lambda i: (i, 4)), full((1, 256)), full((1, 128)), full((1024, 256)),
                  full((1024, 128)), full((512, 128)), row(128), row(128)],
        out_specs=[row(1024), row(1024), row(512), row(256), row(128)],
        out_shape=[jax.ShapeDtypeStruct((T, 1024), BF16), jax.ShapeDtypeStruct((T, 1024), BF16),
                   jax.ShapeDtypeStruct((T, 512), BF16), jax.ShapeDtypeStruct((T, 256), BF16),
                   jax.ShapeDtypeStruct((T, 128), BF16)],
        compiler_params=_cp(("parallel",)))(proj, qnw, kvnw, wuq, wk, wv, cos, ss)


def _mla_prep_bwd(dq, dk, dv, proj, qnw, kvnw, wuq_t, wk_t, wv_t, cos, ss, *, name):
    T = proj.shape[0]
    tm = min(T, 512)

    def body(dq_ref, dk_ref, dv_ref, lat_ref, qnw_ref, kvnw_ref, wuq_ref, wk_ref, wv_ref, cos_ref, ss_ref,
             dlat_ref, dqp_ref, dqnw_ref, dkvnw_ref):
        @pl.when(pl.program_id(0) == 0)
        def _():
            dqnw_ref[...] = jnp.zeros_like(dqnw_ref)
            dkvnw_ref[...] = jnp.zeros_like(dkvnw_ref)
        cs, sn = cos_ref[...], ss_ref[...]
        dkpe = jnp.zeros((tm, 128), F32)
        for h in range(MLA_HEADS):
            hs = slice(128 * h, 128 * h + 128)
            dqp_ref[:, hs] = _rope_t(dq_ref[:, hs] * SCALE, cs, sn, 16, 32).astype(BF16)
            dkpe = dkpe + dk_ref[:, hs]
        lane = lax.broadcasted_iota(jnp.int32, (tm, 128), 1)
        rope_lane = (lane >= MLA_NOPE) & (lane < MLA_NOPE + MLA_ROPE)
        dg3 = jnp.where(rope_lane, _rope_t(jnp.where(rope_lane, dkpe, 0.0), cs, sn, 16, 32), 0.0)

        def norm_bwd(x, w, dn):
            r = lax.rsqrt(jnp.mean(x * x, axis=-1, keepdims=True) + EPS)
            xh = x * r
            g = dn * w
            return r * (g - xh * jnp.mean(g * xh, axis=-1, keepdims=True)), jnp.sum(dn * xh, axis=0, keepdims=True)

        dcqn = _dot(dqp_ref[...], wuq_ref[...])
        dcq, dqnw = norm_bwd(lat_ref[:, 0:256], qnw_ref[...], dcqn)
        dckvn = _dot(dk_ref[...].astype(BF16), wk_ref[...]) + _dot(dv_ref[...], wv_ref[...])
        dckv, dkvnw = norm_bwd(lat_ref[:, 256:384], kvnw_ref[...], dckvn)
        dqnw_ref[...] += dqnw
        dkvnw_ref[...] += dkvnw
        dlat_ref[:, 0:256] = dcq.astype(BF16)
        dlat_ref[:, 256:384] = dckv.astype(BF16)
        dlat_ref[:, 384:512] = dg3.astype(BF16)

    def full(shape):
        return pl.BlockSpec(shape, lambda i: (0, 0))

    def row(w):
        return pl.BlockSpec((tm, w), lambda i: (i, 0))

    return pl.pallas_call(
        body, name=name, grid=(T // tm,),
        in_specs=[row(1024), row(1024), row(512), pl.BlockSpec((tm, 512), lambda i: (i, 4)), full((1, 256)),
                  full((1, 128)), full((1024, 256)), full((1024, 128)), full((512, 128)), row(128), row(128)],
        out_specs=[row(512), row(1024), full((1, 256)), full((1, 128))],
        out_shape=[jax.ShapeDtypeStruct((T, 512), BF16), jax.ShapeDtypeStruct((T, 1024), BF16),
                   jax.ShapeDtypeStruct((1, 256), F32), jax.ShapeDtypeStruct((1, 128), F32)],
        compiler_params=_cp(("arbitrary",)))(dq, dk, dv, proj, qnw, kvnw, wuq_t, wk_t, wv_t, cos, ss)


def _flash_fwd(q, k, vt, *, name):
    T = q.shape[0]
    tq = min(T, 512)
    tk = tq
    nq = T // tq

    def body(q_ref, k_ref, vt_ref, y_ref, lse_ref):
        qi = pl.program_id(1)
        krow = lax.broadcasted_iota(jnp.int32, (tk, tq), 0)
        qcol = lax.broadcasted_iota(jnp.int32, (tk, tq), 1)

        def step(kb, carry, masked):
            ks = pl.ds(pl.multiple_of(kb * tk, tk), tk)
            new = []
            for h in range(2):
                hs = slice(128 * h, 128 * h + 128)
                m, acc = carry[h]
                st = _dot_nt(k_ref[ks, hs], q_ref[:, hs])
                if masked:
                    st = jnp.where(krow <= qcol, st, NEG)
                mn = jnp.maximum(m, jnp.max(st, axis=0, keepdims=True))
                pt = jnp.exp((st - mn).astype(BF16))
                acc = jnp.exp(m - mn) * acc + _dot(vt_ref[kb, hs, :], pt)
                new.append((mn, acc))
            return tuple(new)

        init = (jnp.full((1, tq), NEG, F32), jnp.zeros((128, tq), F32))
        carry = lax.fori_loop(0, qi, lambda kb, c: step(kb, c, False), (init, init))
        (ma, acca), (mb, accb) = step(qi, carry, True)
        ot = jnp.concatenate([acca[:64] / acca[64:], accb[:64] / accb[64:]], axis=0)
        y_ref[...] = ot.T.astype(BF16)
        lse_ref[0, 0] = ma + jnp.log(acca[64:65])
        lse_ref[1, 0] = mb + jnp.log(accb[64:65])

    return pl.pallas_call(
        body, name=name, grid=(4, nq),
        in_specs=[pl.BlockSpec((tq, 256), lambda p, i: (i, p)), pl.BlockSpec((T, 256), lambda p, i: (0, p)),
                  pl.BlockSpec((nq, 256, tk), lambda p, i: (0, p, 0))],
        out_specs=[pl.BlockSpec((tq, 128), lambda p, i: (i, p)),
                   pl.BlockSpec((2, 1, 1, tq), lambda p, i: (p, i, 0, 0))],
        out_shape=[jax.ShapeDtypeStruct((T, MLA_WIDTH), BF16), jax.ShapeDtypeStruct((MLA_HEADS, nq, 1, tq), F32)],
        compiler_params=_cp(("parallel", "arbitrary")))(q, k, vt)


def _flash_bwd(q, k, v, do, lse, delta, *, name):
    T = q.shape[0]
    tq = min(T, 512)
    tk = tq
    nq = T // tq

    def body(q_ref, k_ref, v_ref, do_ref, lse_ref, dl_ref, dqt_ref, dk_ref, dv_ref):
        kb = pl.program_id(1)

        @pl.when(kb == 0)
        def _():
            dqt_ref[...] = jnp.zeros_like(dqt_ref)
        krow = lax.broadcasted_iota(jnp.int32, (tk, tq), 0)
        qcol = lax.broadcasted_iota(jnp.int32, (tk, tq), 1)
        masks = _head_masks((tk, 128))
        vf = v_ref[...].astype(F32)
        vms = [(vf * masks[h]).astype(BF16) for h in range(2)]

        def step(qi, carry, masked):
            qs = pl.ds(pl.multiple_of(qi * tq, tq), tq)
            dob = do_ref[qs, :]
            dof = dob.astype(F32)
            dks, dv_acc = list(carry[:2]), carry[2]
            for h in range(2):
                hs = slice(128 * h, 128 * h + 128)
                kh = k_ref[:, hs]
                qh = q_ref[qs, hs]
                st = _dot_nt(kh, qh)
                pt = jnp.exp((st - lse_ref[h, qi]).astype(BF16))
                if masked:
                    pt = jnp.where(krow <= qcol, pt, jnp.zeros_like(pt))
                dv_acc = dv_acc + _dot(pt, (dof * masks[h]).astype(BF16))
                dpt = _dot_nt(vms[h], dob)
                dst = pt * (dpt - dl_ref[h, qi]).astype(BF16)
                dks[h] = dks[h] + _dot(dst, qh)
                dqt_ref[qi, hs, :] += _dot_tn(kh, dst)
            return dks[0], dks[1], dv_acc

        zero = jnp.zeros((tk, 128), F32)
        carry = step(kb, (zero, zero, zero), True)
        dk0, dk1, dv_acc = lax.fori_loop(kb + 1, nq, lambda qi, c: step(qi, c, False), carry)
        dk_ref[:, 0:128] = dk0
        dk_ref[:, 128:256] = dk1
        dv_ref[...] = dv_acc.astype(BF16)

    stat = pl.BlockSpec((2, nq, 1, tq), lambda p, j: (p, 0, 0, 0))
    return pl.pallas_call(
        body, name=name, grid=(4, nq),
        in_specs=[pl.BlockSpec((T, 256), lambda p, j: (0, p)), pl.BlockSpec((tk, 256), lambda p, j: (j, p)),
                  pl.BlockSpec((tk, 128), lambda p, j: (j, p)), pl.BlockSpec((T, 128), lambda p, j: (0, p)), stat, stat],
        out_specs=[pl.BlockSpec((None, nq, 256, tq), lambda p, j: (p, 0, 0, 0)),
                   pl.BlockSpec((tk, 256), lambda p, j: (j, p)), pl.BlockSpec((tk, 128), lambda p, j: (j, p))],
        out_shape=[jax.ShapeDtypeStruct((4, nq, 256, tq), F32), jax.ShapeDtypeStruct((T, 1024), F32),
                   jax.ShapeDtypeStruct((T, MLA_WIDTH), BF16)],
        compiler_params=_cp(("parallel", "arbitrary")))(q, k, v, do, lse, delta)


def _shift_down(x, n, prev8):
    r = pltpu.roll(x, n, 0)
    row = lax.broadcasted_iota(jnp.int32, prev8.shape, 0)
    first = jnp.where(row < n, pltpu.roll(prev8, n, 0), r[:8])
    if x.shape[0] == 8:
        return first
    return jnp.concatenate([first, r[8:]], axis=0)


def _shift_up(x, n, next8):
    tm = x.shape[0]
    r = pltpu.roll(x, tm - n, 0)
    row = lax.broadcasted_iota(jnp.int32, next8.shape, 0)
    last = jnp.where(row >= 8 - n, pltpu.roll(next8, 8 - n, 0), r[tm - 8:])
    return jnp.concatenate([r[:tm - 8], last], axis=0)


def _conv_pre(u, prev8, cw_ref, cb_ref):
    p1 = _shift_down(u, 1, prev8)
    p2 = _shift_down(u, 2, prev8)
    up = cb_ref[...] + cw_ref[0:1, :] * p2 + cw_ref[1:2, :] * p1 + cw_ref[2:3, :] * u
    return up, p1, p2


def _conv_fwd(u, cw, cb, *, name):
    T = u.shape[0]
    tm = min(T, 512)
    W = 2 * FF_HALF

    def body(u_ref, prev_ref, cw_ref, cb_ref, a_ref):
        prev = jnp.where(pl.program_id(0) > 0, prev_ref[...], 0.0)
        up, _, _ = _conv_pre(u_ref[...], prev, cw_ref, cb_ref)
        gate = up[:, :FF_HALF]
        a_ref[...] = (gate * _sigmoid(gate) * up[:, FF_HALF:]).astype(BF16)

    return pl.pallas_call(
        body, name=name, grid=(T // tm, 2),
        in_specs=[pl.BlockSpec((tm, W), lambda i, j: (i, j)),
                  pl.BlockSpec((8, W), lambda i, j: (jnp.maximum(i * (tm // 8) - 1, 0), j)),
                  pl.BlockSpec((3, W), lambda i, j: (0, j)), pl.BlockSpec((1, W), lambda i, j: (0, j))],
        out_specs=pl.BlockSpec((tm, FF_HALF), lambda i, j: (i, j)),
        out_shape=jax.ShapeDtypeStruct((T, D_FF), BF16),
        compiler_params=_cp(("parallel", "parallel")))(u, u, cw, cb)


def _conv_bwd(u, da, cw, cb, *, name):
    T = u.shape[0]
    tm = min(T, 512)
    W = 2 * FF_HALF
    nt = T // tm

    def body(u_ref, prev_ref, next_ref, da_ref, dan_ref, cw_ref, cb_ref, du_ref, dw0_ref, dw1_ref, dw2_ref, db_ref):
        i = pl.program_id(1)

        @pl.when(i == 0)
        def _():
            for r in (dw0_ref, dw1_ref, dw2_ref, db_ref):
                r[...] = jnp.zeros_like(r)

        def dpre(u, prev8, da):
            up, p1, p2 = _conv_pre(u, prev8, cw_ref, cb_ref)
            gate, val = up[:, :FF_HALF], up[:, FF_HALF:]
            sg = _sigmoid(gate)
            dgate = da * val * (sg * (1.0 + gate * (1.0 - sg)))
            dval = da * (gate * sg)
            return jnp.concatenate([dgate, dval], axis=1), p1, p2

        u = u_ref[...]
        prev = jnp.where(i > 0, prev_ref[...], 0.0)
        dup, p1, p2 = dpre(u, prev, da_ref[...])
        dupn, _, _ = dpre(next_ref[...], u[tm - 8:], dan_ref[...])
        dupn = jnp.where(i < nt - 1, dupn, 0.0)
        du = cw_ref[2:3, :] * dup + cw_ref[1:2, :] * _shift_up(dup, 1, dupn) + cw_ref[0:1, :] * _shift_up(dup, 2, dupn)
        du_ref[...] = du.astype(BF16)
        dw0_ref[...] += jnp.sum(dup * p2, axis=0, keepdims=True)
        dw1_ref[...] += jnp.sum(dup * p1, axis=0, keepdims=True)
        dw2_ref[...] += jnp.sum(dup * u, axis=0, keepdims=True)
        db_ref[...] += jnp.sum(dup, axis=0, keepdims=True)

    nxt = lambda j, i: (jnp.minimum((i + 1) * (tm // 8), T // 8 - 1), j)
    vec = pl.BlockSpec((1, W), lambda j, i: (0, j))
    return pl.pallas_call(
        body, name=name, grid=(2, nt),
        in_specs=[pl.BlockSpec((tm, W), lambda j, i: (i, j)),
                  pl.BlockSpec((8, W), lambda j, i: (jnp.maximum(i * (tm // 8) - 1, 0), j)),
                  pl.BlockSpec((8, W), nxt),
                  pl.BlockSpec((tm, FF_HALF), lambda j, i: (i, j)), pl.BlockSpec((8, FF_HALF), nxt),
                  pl.BlockSpec((3, W), lambda j, i: (0, j)), vec],
        out_specs=[pl.BlockSpec((tm, W), lambda j, i: (i, j)), vec, vec, vec, vec],
        out_shape=[jax.ShapeDtypeStruct((T, 2 * D_FF), BF16)] + [jax.ShapeDtypeStruct((1, 2 * D_FF), F32)] * 4,
        compiler_params=_cp(("parallel", "arbitrary")))(u, u, u, da, da, cw, cb)


def _adamw(w, m, v, g_slots, *, name):
    R, C = w.shape
    ns = g_slots.shape[0]
    tr = _row_tile(R)

    def body(w_ref, m_ref, v_ref, g_ref, go_ref, d_ref, mo_ref, vo_ref):
        g = g_ref[0].astype(F32)
        for s in range(1, ns):
            g = g + g_ref[s].astype(F32)
        mn = ADAM_B1 * m_ref[...] + (1.0 - ADAM_B1) * g
        vn = ADAM_B2 * v_ref[...] + (1.0 - ADAM_B2) * (g * g)
        m_hat = mn / (1.0 - ADAM_B1 ** ADAM_STEP)
        v_hat = vn / (1.0 - ADAM_B2 ** ADAM_STEP)
        go_ref[...] = g
        d_ref[...] = -ADAM_LR * (m_hat / (jnp.sqrt(v_hat) + ADAM_EPS) + ADAM_WD * w_ref[...])
        mo_ref[...] = mn
        vo_ref[...] = vn

    blk = pl.BlockSpec((tr, C), lambda i: (i, 0))
    return pl.pallas_call(
        body, name=name, grid=(R // tr,),
        in_specs=[blk, blk, blk, pl.BlockSpec((ns, tr, C), lambda i: (0, i, 0))],
        out_specs=[blk] * 4, out_shape=[jax.ShapeDtypeStruct((R, C), F32)] * 4,
        compiler_params=_cp(("parallel",)))(w, m, v, g_slots)


def _place():
    return lax.axis_index("x"), lax.axis_index("y"), lax.axis_index("c")


def _all_gather(x, *, name, in_vmem):
    def body(x_ref, out_ref, send_sems, recv_sems, local_sem):
        x_, y_, c_ = _place()
        me, sibling = (x_, y_, c_), (x_, y_, 1 - c_)
        chips = [(1 - x_, y_), (x_, 1 - y_), (1 - x_, 1 - y_)]

        def slot(px, py, pc):
            return out_ref.at[4 * px + 2 * py + pc]

        def copy(k, block, to, src=None):
            return pltpu.make_async_remote_copy(
                src_ref=slot(*block) if src is None else src, dst_ref=slot(*block),
                send_sem=send_sems.at[k], recv_sem=recv_sems.at[k], device_id=to, device_id_type=MESH)

        mine = pltpu.make_async_copy(x_ref, slot(*me), local_sem)
        mine.start()
        first = [copy(0, me, sibling, src=x_ref)]
        first += [copy(1 + j, me, (*chip, c_), src=x_ref) for j, chip in enumerate(chips)]
        for cp in first:
            cp.start()
        passed = [copy(4 + j, (*chip, c_), sibling) for j, chip in enumerate(chips)]
        for j, chip in enumerate(chips):
            copy(1 + j, (*chip, c_), me).wait_recv()
            passed[j].start()
        copy(0, sibling, me).wait_recv()
        for j, chip in enumerate(chips):
            copy(4 + j, (*chip, 1 - c_), me).wait_recv()
        for cp in first + passed:
            cp.wait_send()
        mine.wait()

    spec = pl.BlockSpec(memory_space=pltpu.VMEM) if in_vmem else ANY
    return pl.pallas_call(
        body, name=name, out_shape=jax.ShapeDtypeStruct((N_DEV,) + x.shape, x.dtype),
        in_specs=[spec], out_specs=spec,
        scratch_shapes=[pltpu.SemaphoreType.DMA((7,)), pltpu.SemaphoreType.DMA((7,)), pltpu.SemaphoreType.DMA],
        compiler_params=pltpu.CompilerParams(vmem_limit_bytes=VMEM_LIMIT))(x)


def _sum_slots(g, *, name):
    n = g.shape[0]

    def body(g_ref, o_ref):
        acc = g_ref[0]
        for s in range(1, n):
            acc = acc + g_ref[s]
        o_ref[...] = acc

    return pl.pallas_call(body, name=name, out_shape=jax.ShapeDtypeStruct(g.shape[1:], g.dtype))(g)


def _swap_sibling(x, *, name):
    def body(x_ref, out_ref, send_sem, recv_sem):
        x_, y_, c_ = _place()
        cp = pltpu.make_async_remote_copy(src_ref=x_ref, dst_ref=out_ref, send_sem=send_sem, recv_sem=recv_sem,
                                          device_id=(x_, y_, 1 - c_), device_id_type=MESH)
        cp.start()
        cp.wait()

    return pl.pallas_call(
        body, name=name, out_shape=jax.ShapeDtypeStruct(x.shape, x.dtype), in_specs=[ANY], out_specs=ANY,
        scratch_shapes=[pltpu.SemaphoreType.DMA, pltpu.SemaphoreType.DMA])(x)


def _exchange_chips(p, *, name):
    def body(p_ref, out_ref, send_sems, recv_sems, local_sem):
        x_, y_, c_ = _place()
        me_k = 2 * x_ + y_
        chips = [(1 - x_, y_), (x_, 1 - y_), (1 - x_, 1 - y_)]
        local = pltpu.make_async_copy(p_ref.at[me_k], out_ref.at[me_k], local_sem)
        local.start()

        def copy(j, src_k, dst_k, chip):
            return pltpu.make_async_remote_copy(
                src_ref=p_ref.at[src_k], dst_ref=out_ref.at[dst_k], send_sem=send_sems.at[j],
                recv_sem=recv_sems.at[j], device_id=(*chip, c_), device_id_type=MESH)

        sends = [copy(j, 2 * px + py, me_k, (px, py)) for j, (px, py) in enumerate(chips)]
        for cp in sends:
            cp.start()
        for j, (px, py) in enumerate(chips):
            copy(j, me_k, 2 * px + py, (px, py)).wait_recv()
        for cp in sends:
            cp.wait_send()
        local.wait()

    return pl.pallas_call(
        body, name=name, out_shape=jax.ShapeDtypeStruct(p.shape, p.dtype), in_specs=[ANY], out_specs=ANY,
        scratch_shapes=[pltpu.SemaphoreType.DMA((3,)), pltpu.SemaphoreType.DMA((3,)), pltpu.SemaphoreType.DMA])(p)


def _row_tile(R):
    for cand in (256, 400, 200):
        if R % cand == 0:
            return cand
    return R


def _add2(a, b, *, name, out_dtype):
    n, R, C = a.shape
    tr = _row_tile(R)

    def body(a_ref, b_ref, o_ref):
        o_ref[...] = (a_ref[...] + b_ref[...]).astype(out_dtype)

    blk = pl.BlockSpec((1, tr, C), lambda s, i: (s, i, 0))
    return pl.pallas_call(body, name=name, grid=(n, R // tr), in_specs=[blk, blk], out_specs=blk,
                          out_shape=jax.ShapeDtypeStruct(a.shape, out_dtype),
                          compiler_params=_cp(("parallel", "parallel")))(a, b)


def _pack_local(parts):
    segs = []
    for n, r, rp, tr in BIG:
        w = parts[n].T if tr else parts[n]
        segs.append(jnp.pad(w.reshape(r, PACK_COLS), ((0, rp - r), (0, 0))))
    used = sum(rp for _, _, rp, _ in BIG)
    segs.append(jnp.zeros((PACK_ROWS - used, PACK_COLS), segs[0].dtype))
    return jnp.concatenate(segs, axis=0)


def _unpack_local(packed, like):
    out, off = {}, 0
    for n, r, rp, tr in BIG:
        rows, cols = like[n].shape
        seg = packed[off:off + r]
        out[n] = (seg.reshape(cols, rows).T if tr else seg)[None]
        off += rp
    return out


def _segments(g):
    out, off = {}, 0
    for n, r, rp, _ in BIG:
        out[n] = g[:, off:off + r]
        off += rp
    return out


def _pack_grads(g):
    g_in = jnp.concatenate([g["w_in_t"][:2432], g["w_in_t"][2496:2528]], axis=0).reshape(N_DEV, 308, PACK_COLS)
    g_uq = g["w_uq_t"].reshape(N_DEV, 128, MLA_Q_RANK)[:, :96].reshape(N_DEV, 24, PACK_COLS)
    g_ukv = jnp.concatenate([g["w_k_t"].reshape(N_DEV, 128, MLA_KV_RANK)[:, :64],
                             g["w_v_t"].reshape(N_DEV, 64, MLA_KV_RANK)], axis=1).reshape(N_DEV, 16, PACK_COLS)
    up = g["w_up_t"].reshape(N_DEV, 704, PACK_COLS)
    g_up = jnp.stack([up[FF_OWNER_ORDER.index(d)] for d in range(N_DEV)])
    parts = dict(w_in=g_in, w_uq=g_uq, w_ukv=g_ukv, w_out=g["w_out"].reshape(N_DEV, 128, PACK_COLS), w_up=g_up,
                 w_down=g["w_down"].reshape(N_DEV, 352, PACK_COLS))
    segs = [jnp.pad(parts[n], ((0, 0), (0, rp - r), (0, 0))) for n, r, rp, _ in BIG]
    used = sum(rp for _, _, rp, _ in BIG)
    segs.append(jnp.zeros((N_DEV, PACK_ROWS - used, PACK_COLS), F32))
    return jnp.concatenate(segs, axis=1)


def _interleave_ff(w):
    g, v = w[..., :D_FF], w[..., D_FF:]
    return jnp.concatenate([g[..., :FF_HALF], v[..., :FF_HALF], g[..., FF_HALF:], v[..., FF_HALF:]], axis=-1)


def _deinterleave_ff(w):
    b = [w[..., i * FF_HALF:(i + 1) * FF_HALF] for i in range(4)]
    return jnp.concatenate([b[0], b[2], b[1], b[3]], axis=-1)


def _rope_tables(pos):
    p = pos.astype(F32)[:, None]
    inv_r = ROPE_BASE ** (-jnp.arange(0, RET_HEAD_DIM, 2, dtype=F32) / RET_HEAD_DIM)
    ang = p * inv_r
    c, s = jnp.cos(ang), jnp.sin(ang)
    cos_r = jnp.concatenate([c, c, c, c], axis=1)
    ss_r = jnp.concatenate([-s, s, -s, s], axis=1)
    inv_m = ROPE_BASE ** (-jnp.arange(0, MLA_ROPE, 2, dtype=F32) / MLA_ROPE)
    ang = p * inv_m
    c, s = jnp.cos(ang), jnp.sin(ang)
    T = pos.shape[0]
    cos_m = jnp.concatenate([jnp.ones((T, 64), F32), c, c, jnp.ones((T, 32), F32)], axis=1)
    ss_m = jnp.concatenate([jnp.zeros((T, 64), F32), -s, s, jnp.zeros((T, 32), F32)], axis=1)
    return cos_r, ss_r, cos_m, ss_m


def _prep_weights(seg):
    w_in_t = seg["w_in"].reshape(IN_WIDTH, D_MODEL)
    z = lambda n: jnp.zeros((n, D_MODEL), BF16)
    w_in_t = jnp.concatenate([w_in_t[:2432], z(64), w_in_t[2432:2464], z(32)], axis=0)
    w_uq_t = jnp.pad(seg["w_uq"].reshape(MLA_HEADS, 96, MLA_Q_RANK), ((0, 0), (0, 32), (0, 0))).reshape(1024, MLA_Q_RANK)
    ukv = seg["w_ukv"].reshape(MLA_HEADS, 128, MLA_KV_RANK)
    w_k_t = jnp.pad(ukv[:, :64], ((0, 0), (0, 64), (0, 0))).reshape(1024, MLA_KV_RANK)
    w_v_t = ukv[:, 64:].reshape(512, MLA_KV_RANK)
    w_up_t = jnp.concatenate([seg["w_up"][d] for d in FF_OWNER_ORDER], axis=0)
    return dict(w_in_t=w_in_t, w_uq_t=w_uq_t, w_k_t=w_k_t, w_v_t=w_v_t, w_out=seg["w_out"].reshape(1024, D_MODEL),
                w_up_t=w_up_t, w_down=seg["w_down"].reshape(D_FF, D_MODEL))


def _local_step(x, pos, tgt, W, sm):
    cos_r, ss_r, cos_m, ss_m = _rope_tables(pos)
    tabs = _ret_tables()

    h = _rmsnorm_fwd(x, sm["attn_norm_w"], name="attn_norm")
    proj = _mm(h, W["w_in_t"], bt=True, name="in_proj")
    y_ret, o_ret = _ret_fwd(proj, cos_r, ss_r, tabs, sm["ret_gn_w"], name="ret_fwd")
    q, k, v, cqn, ckvn = _mla_prep_fwd(proj, sm["mla_q_norm_w"], sm["mla_kv_norm_w"], W["w_uq_t"], W["w_k_t"],
                                       W["w_v_t"], cos_m, ss_m, name="mla_prep")
    T = x.shape[0]
    tq = min(T, 512)
    vt = v.reshape(T // tq, tq, MLA_HEADS, MLA_V).transpose(0, 2, 3, 1)
    vt = jnp.concatenate([vt, jnp.ones_like(vt)], axis=2).reshape(T // tq, MLA_HEADS * 128, tq)
    y_mla, lse = _flash_fwd(q, k, vt, name="mla_attn")
    mixed = jnp.concatenate([y_ret, y_mla], axis=1)
    x1 = _mm(mixed, W["w_out"], add=x, name="out_proj")
    h2 = _rmsnorm_fwd(x1, sm["ffn_norm_w"], name="ffn_norm")
    u = _mm(h2, W["w_up_t"], bt=True, name="up_proj")
    a = _conv_fwd(u, sm["conv_w"], sm["conv_b"], name="conv_gate")
    x2 = _mm(a, W["w_down"], add=x1, name="down_proj")
    loss, dx2, d_final = _loss_head(x2, tgt, sm["final_norm_w"], name="loss_head")

    g = {}
    g["w_down"] = _mm_tn(a, dx2, name="dw_down")
    da = _mm(dx2, W["w_down"], bt=True, name="d_act")
    du, dcw0, dcw1, dcw2, dcb = _conv_bwd(u, da, sm["conv_w"], sm["conv_b"], name="conv_bwd")
    g["w_up_t"] = _mm_tn(du, h2, name="dw_up")
    dh2 = _mm(du, W["w_up_t"], name="d_h2")
    dx1, d_ffn = _rmsnorm_bwd(x1, sm["ffn_norm_w"], dh2, dx2, name="ffn_norm_bwd")

    g["w_out"] = _mm_tn(mixed, dx1, name="dw_out")
    dmixed = _mm(dx1, W["w_out"], bt=True, name="d_mixed")
    do_ret, dg, do_mla, delta, d_gn = _mix_bwd(dmixed, o_ret, proj, y_mla, sm["ret_gn_w"], name="mix_bwd")
    drq = _ret_bwd_dq(proj, do_ret, cos_r, ss_r, tabs, name="ret_bwd_dq")
    drk, drv = _ret_bwd_dkv(proj, do_ret, cos_r, ss_r, tabs, name="ret_bwd_dkv")
    delta_r = delta[:, :MLA_HEADS].T.reshape(MLA_HEADS, T // tq, 1, tq)
    dqt, dk, dv = _flash_bwd(q, k, v, do_mla, lse, delta_r, name="mla_attn_bwd")
    dq = dqt.transpose(1, 3, 0, 2).reshape(T, MLA_HEADS * 128)
    dlat, dqp, d_qn, d_kvn = _mla_prep_bwd(dq, dk, dv, proj, sm["mla_q_norm_w"], sm["mla_kv_norm_w"], W["w_uq_t"],
                                           W["w_k_t"], W["w_v_t"], cos_m, ss_m, name="mla_prep_bwd")
    g["w_uq_t"] = _mm_tn(dqp, cqn, name="dw_uq")
    g["w_k_t"] = _mm_tn(dk, ckvn, name="dw_ukv_k")
    g["w_v_t"] = _mm_tn(dv, ckvn, name="dw_ukv_v")
    dproj = jnp.concatenate([drq, drk, drv, dg, dlat], axis=1)
    g["w_in_t"] = _mm_tn(dproj, h, name="dw_in")
    dh = _mm(dproj, W["w_in_t"], name="d_h")
    grad_x, d_attn = _rmsnorm_bwd(x, sm["attn_norm_w"], dh, dx1, name="attn_norm_bwd")

    small = dict(attn_norm_w=d_attn, ret_gn_w=d_gn, mla_q_norm_w=d_qn, mla_kv_norm_w=d_kvn, ffn_norm_w=d_ffn,
                 conv_b=_deinterleave_ff(dcb), final_norm_w=d_final,
                 conv_w=_deinterleave_ff(jnp.concatenate([dcw0, dcw1, dcw2], axis=0)))
    return loss, grad_x, g, small


def kernel(x, positions, attn_norm_w, w_in, ret_gn_w, mla_q_norm_w, w_uq, mla_kv_norm_w, w_ukv, w_out, ffn_norm_w, w_up, conv_w, conv_b, w_down, final_norm_w, loss_target, m_attn_norm_w, m_w_in, m_ret_gn_w, m_mla_q_norm_w, m_w_uq, m_mla_kv_norm_w, m_w_ukv, m_w_out, m_ffn_norm_w, m_w_up, m_conv_w, m_conv_b, m_w_down, m_final_norm_w, v_attn_norm_w, v_w_in, v_ret_gn_w, v_mla_q_norm_w, v_w_uq, v_mla_kv_norm_w, v_w_ukv, v_w_out, v_ffn_norm_w, v_w_up, v_conv_w, v_conv_b, v_w_down, v_final_norm_w):
    a = dict(locals())
    x_, y_, c_ = _place()
    dev = 4 * x_ + 2 * y_ + c_

    shard = {n: a[n][0] for n, _, _, _ in BIG}
    gathered = _all_gather(_pack_local({n: w.astype(BF16) for n, w in shard.items()}), name="gather_weights", in_vmem=False)
    W = _prep_weights(_segments(gathered))
    cw_pad = jnp.pad(conv_w[0].reshape(-1), (0, 24 * 128 - 3 * 704)).reshape(24, 128)
    cw_all = _all_gather(cw_pad, name="gather_conv_w", in_vmem=True)
    conv_w_full = cw_all.reshape(N_DEV, -1)[:, :3 * 704].reshape(N_DEV, 3, 704).transpose(1, 0, 2).reshape(3, 2 * D_FF)
    sm = dict(attn_norm_w=attn_norm_w, ret_gn_w=ret_gn_w, mla_q_norm_w=mla_q_norm_w, mla_kv_norm_w=mla_kv_norm_w,
              ffn_norm_w=ffn_norm_w, final_norm_w=final_norm_w.reshape(1, D_MODEL),
              conv_w=_interleave_ff(conv_w_full), conv_b=_interleave_ff(conv_b))

    loss, grad_x, g, gs = _local_step(x[0], positions[0], loss_target[0], W, sm)

    gp = _pack_grads(g).reshape(4, 2, PACK_ROWS, PACK_COLS)
    mine = lax.dynamic_index_in_dim(gp, c_, axis=1, keepdims=False)
    theirs = lax.dynamic_index_in_dim(gp, 1 - c_, axis=1, keepdims=False)
    pair = _add2(mine, _swap_sibling(theirs, name="grad_swap_sibling"), out_dtype=BF16, name="grad_pair_sum")
    slots = _exchange_chips(pair, name="grad_exchange_chips")
    big = _adamw(_pack_local(shard), _pack_local({n: a["m_" + n][0] for n, _, _, _ in BIG}),
                 _pack_local({n: a["v_" + n][0] for n, _, _, _ in BIG}), slots, name="adamw_large")
    big = [_unpack_local(t, shard) for t in big]

    vec = jnp.concatenate([gs[n].reshape(-1) for n, _ in SMALL] + [gs["conv_w"].reshape(-1), loss[0, :1]])
    vec = jnp.pad(vec, (0, SMALL_ROWS * 128 - vec.shape[0])).reshape(SMALL_ROWS, 128)
    tot = _sum_slots(_all_gather(vec, name="gather_small_grads", in_vmem=True), name="sum_small_grads").reshape(-1)
    loss_out = tot[SMALL_N + 3 * 2 * D_FF]
    g_cw = lax.dynamic_slice_in_dim(tot[SMALL_N:SMALL_N + 3 * 2 * D_FF].reshape(3, 2 * D_FF), dev * 704, 704, axis=1)

    def flat_small(prefix):
        return jnp.concatenate([a[prefix + n].reshape(-1) for n, _ in SMALL]).reshape(75, 128)

    sml = _adamw(flat_small(""), flat_small("m_"), flat_small("v_"), tot[:SMALL_N].reshape(1, 75, 128), name="adamw_small")
    cwo = _adamw(conv_w[0], m_conv_w[0], v_conv_w[0], g_cw[None], name="adamw_conv_w")

    def small_of(t, n):
        off = 0
        for nm, sz in SMALL:
            if nm == n:
                return t.reshape(-1)[off:off + sz].reshape(a[n].shape)
            off += sz

    names = ['attn_norm_w', 'w_in', 'ret_gn_w', 'mla_q_norm_w', 'w_uq', 'mla_kv_norm_w', 'w_ukv', 'w_out',
             'ffn_norm_w', 'w_up', 'conv_w', 'conv_b', 'w_down', 'final_norm_w']
    outs = [loss_out, grad_x[None]]
    for kind in range(4):
        for n in names:
            if n == "conv_w":
                outs.append(cwo[kind][None])
            elif n in big[kind]:
                outs.append(big[kind][n])
            else:
                outs.append(small_of(sml[kind], n))
    return tuple(outs)
```

```python
import functools

import numpy as np
import jax
import jax.numpy as jnp
from jax import lax
from jax.experimental import pallas as pl
from jax.experimental.pallas import tpu as pltpu

F32 = jnp.float32
BF16 = jnp.bfloat16
MESH = pl.DeviceIdType.MESH
ANY = pl.BlockSpec(memory_space=pl.ANY)

D_MODEL = 1024
RET_HEADS = 8
RET_HEAD_DIM = 64
RET_WIDTH = 512
RET_CHUNK = 128
MLA_HEADS = 8
MLA_NOPE = 64
MLA_ROPE = 32
MLA_V = 64
MLA_Q_RANK = 256
MLA_KV_RANK = 128
MLA_WIDTH = 512
IN_WIDTH = 2464
IN_PAD = 2560
D_FF = 2816
FF_HALF = 1408
ROPE_BASE = 10000.0
EPS = 1e-6
SCALE = float((MLA_NOPE + MLA_ROPE) ** -0.5)
K_SCALE = 0.125
N_DEV = 8

ADAM_LR = 0.001
ADAM_B1 = 0.9
ADAM_B2 = 0.999
ADAM_EPS = 1e-08
ADAM_WD = 0.01
ADAM_STEP = 10

VMEM_LIMIT = 56 * 1024 * 1024
MM_BUDGET = 40 * 1024 * 1024
NEG = -1e30

BIG = (("w_in", 308, 320, True), ("w_uq", 24, 32, True), ("w_ukv", 16, 16, True),
       ("w_out", 128, 128, False), ("w_up", 704, 704, True), ("w_down", 352, 352, False))
PACK_COLS = 1024
PACK_ROWS = 1600
FF_OWNER_ORDER = (0, 1, 4, 5, 2, 3, 6, 7)
SMALL = (("attn_norm_w", 1024), ("ret_gn_w", 512), ("mla_q_norm_w", 256), ("mla_kv_norm_w", 128),
         ("ffn_norm_w", 1024), ("conv_b", 5632), ("final_norm_w", 1024))
SMALL_N = 9600
SMALL_ROWS = 208


def _cp(sem=None, vmem=VMEM_LIMIT):
    return pltpu.CompilerParams(dimension_semantics=sem, vmem_limit_bytes=vmem)


def _dot(a, b):
    return jnp.dot(a, b, preferred_element_type=F32)


def _dot_nt(a, b):
    return lax.dot_general(a, b, (((1,), (1,)), ((), ())), preferred_element_type=F32)


def _dot_tn(a, b):
    return lax.dot_general(a, b, (((0,), (0,)), ((), ())), preferred_element_type=F32)


def _sigmoid(x):
    return 0.5 * jnp.tanh(0.5 * x) + 0.5


def _partner(x, half, period):
    n = x.shape[-1]
    lane = lax.broadcasted_iota(jnp.int32, x.shape, 1)
    return jnp.where((lane % period) < half, pltpu.roll(x, n - half, 1), pltpu.roll(x, half, 1))


def _rope(x, cos, ss, half, period):
    return x * cos + _partner(x, half, period) * ss


def _rope_t(dy, cos, ss, half, period):
    return dy * cos - _partner(dy, half, period) * ss


def _head_masks(shape):
    lane = lax.broadcasted_iota(jnp.int32, shape, 1)
    m0 = (lane < 64).astype(F32)
    return m0, 1.0 - m0


def _mm(a, b, *, name, add=None, out_dtype=F32, bt=False):
    M, K = a.shape
    N = b.shape[0] if bt else b.shape[1]
    osz = jnp.dtype(out_dtype).itemsize
    per_row = 2 * (K * a.dtype.itemsize + N * osz + (N * 4 if add is not None else 0))
    tm = 128
    for cand in (512, 256):
        if M % cand == 0 and cand * per_row + 4 * K * N <= MM_BUDGET:
            tm = cand
            break
    tm = min(tm, M)
    mul = _dot_nt if bt else _dot

    def body(*refs):
        if add is None:
            a_ref, b_ref, o_ref = refs
            acc = mul(a_ref[...].astype(BF16), b_ref[...])
        else:
            a_ref, b_ref, r_ref, o_ref = refs
            acc = r_ref[...] + mul(a_ref[...].astype(BF16), b_ref[...])
        o_ref[...] = acc.astype(out_dtype)

    in_specs = [pl.BlockSpec((tm, K), lambda i: (i, 0)), pl.BlockSpec(b.shape, lambda i: (0, 0))]
    args = [a, b]
    if add is not None:
        in_specs.append(pl.BlockSpec((tm, N), lambda i: (i, 0)))
        args.append(add)
    return pl.pallas_call(
        body, name=name, grid=(M // tm,), in_specs=in_specs,
        out_specs=pl.BlockSpec((tm, N), lambda i: (i, 0)),
        out_shape=jax.ShapeDtypeStruct((M, N), out_dtype),
        compiler_params=_cp(("parallel",)))(*args)


def _mm_tn(a, b, *, name):
    T, M = a.shape
    N = b.shape[1]
    tk = min(T, 512)

    def tile(n):
        for cand in (1408, 1280):
            if n > 1408 and n % cand == 0:
                return cand
        return n

    tm, tn = tile(M), tile(N)
    nk = T // tk

    def body(a_ref, b_ref, o_ref):
        @pl.when(pl.program_id(2) == 0)
        def _():
            o_ref[...] = jnp.zeros_like(o_ref)
        o_ref[...] += _dot_tn(a_ref[...].astype(BF16), b_ref[...].astype(BF16))

    return pl.pallas_call(
        body, name=name, grid=(M // tm, N // tn, nk),
        in_specs=[pl.BlockSpec((tk, tm), lambda i, j, k: (k, i)), pl.BlockSpec((tk, tn), lambda i, j, k: (k, j))],
        out_specs=pl.BlockSpec((tm, tn), lambda i, j, k: (i, j)),
        out_shape=jax.ShapeDtypeStruct((M, N), F32),
        compiler_params=_cp(("parallel", "parallel", "arbitrary")))(a, b)


def _rmsnorm_fwd(x, w, *, name):
    T, D = x.shape
    tm = min(T, 1024)

    def body(x_ref, w_ref, o_ref):
        xv = x_ref[...]
        r = lax.rsqrt(jnp.mean(xv * xv, axis=-1, keepdims=True) + EPS)
        o_ref[...] = (xv * r * w_ref[...]).astype(BF16)

    return pl.pallas_call(
        body, name=name, grid=(T // tm,),
        in_specs=[pl.BlockSpec((tm, D), lambda i: (i, 0)), pl.BlockSpec((1, D), lambda i: (0, 0))],
        out_specs=pl.BlockSpec((tm, D), lambda i: (i, 0)),
        out_shape=jax.ShapeDtypeStruct((T, D), BF16),
        compiler_params=_cp(("parallel",)))(x, w)


def _rmsnorm_bwd(x, w, dh, dres, *, name):
    T, D = x.shape
    tm = min(T, 512)

    def body(x_ref, w_ref, dh_ref, dr_ref, dx_ref, dw_ref):
        @pl.when(pl.program_id(0) == 0)
        def _():
            dw_ref[...] = jnp.zeros_like(dw_ref)
        xv = x_ref[...]
        r = lax.rsqrt(jnp.mean(xv * xv, axis=-1, keepdims=True) + EPS)
        xh = xv * r
        dh = dh_ref[...]
        g = dh * w_ref[...]
        dx_ref[...] = dr_ref[...] + r * (g - xh * jnp.mean(g * xh, axis=-1, keepdims=True))
        dw_ref[...] += jnp.sum(dh * xh, axis=0, keepdims=True)

    row = pl.BlockSpec((tm, D), lambda i: (i, 0))
    vec = pl.BlockSpec((1, D), lambda i: (0, 0))
    return pl.pallas_call(
        body, name=name, grid=(T // tm,), in_specs=[row, vec, row, row], out_specs=[row, vec],
        out_shape=[jax.ShapeDtypeStruct((T, D), F32), jax.ShapeDtypeStruct((1, D), F32)],
        compiler_params=_cp(("arbitrary",)))(x, w, dh, dres)


def _loss_head(x2, tgt, w, *, name):
    T, D = x2.shape
    tm = min(T, 512)

    def body(x_ref, t_ref, w_ref, loss_ref, dx_ref, dw_ref):
        @pl.when(pl.program_id(0) == 0)
        def _():
            dw_ref[...] = jnp.zeros_like(dw_ref)
            loss_ref[...] = jnp.zeros_like(loss_ref)
        xv = x_ref[...]
        wv = w_ref[...]
        r = lax.rsqrt(jnp.mean(xv * xv, axis=-1, keepdims=True) + EPS)
        xh = xv * r
        e = xh * wv - t_ref[...]
        part = 0.5 * jnp.sum(jnp.mean(e * e, axis=-1, keepdims=True), axis=0, keepdims=True)
        loss_ref[...] += jnp.broadcast_to(part, loss_ref.shape)
        dy = e * (1.0 / D)
        g = dy * wv
        dx_ref[...] = r * (g - xh * jnp.mean(g * xh, axis=-1, keepdims=True))
        dw_ref[...] += jnp.sum(dy * xh, axis=0, keepdims=True)

    row = pl.BlockSpec((tm, D), lambda i: (i, 0))
    vec = pl.BlockSpec((1, D), lambda i: (0, 0))
    return pl.pallas_call(
        body, name=name, grid=(T // tm,), in_specs=[row, row, vec],
        out_specs=[pl.BlockSpec((1, 128), lambda i: (0, 0)), row, vec],
        out_shape=[jax.ShapeDtypeStruct((1, 128), F32), jax.ShapeDtypeStruct((T, D), F32),
                   jax.ShapeDtypeStruct((1, D), F32)],
        compiler_params=_cp(("arbitrary",)))(x2, tgt, w)


def _ret_tables():
    C = RET_CHUNK
    h = jnp.arange(RET_HEADS, dtype=F32)
    log_gamma = jnp.log1p(-jnp.power(2.0, -5.0 - h))
    idx = jnp.arange(C, dtype=F32)
    diff = idx[:, None] - idx[None, :]
    dm = jnp.where(diff >= 0, jnp.exp(log_gamma[:, None, None] * jnp.maximum(diff, 0.0)), 0.0)
    dm = dm.reshape(4, 2 * C, C)
    lane_head = jnp.repeat(jnp.arange(RET_HEADS).reshape(4, 2), 64, axis=1)
    lg = log_gamma[lane_head]
    xi = jnp.exp(lg[:, None, :] * (idx[None, :, None] + 1.0))
    zeta = jnp.exp(lg[:, None, :] * (C - 1.0 - idx[None, :, None]))
    blk = (jnp.arange(128)[:, None] // 64) == (jnp.arange(128)[None, :] // 64)
    cd = jnp.where(blk[None], jnp.exp(lg * C)[:, :, None], 0.0)
    return dm.astype(F32), xi.astype(F32), zeta.astype(F32), cd.astype(F32)


def _ret_specs(tb, rev, nt):
    def tmap(t):
        return (nt - 1 - t) if rev else t
    qkv = [pl.BlockSpec((tb, 128), lambda p, t, o=o: (tmap(t), o + p)) for o in (0, 4, 8)]
    rope = [pl.BlockSpec((tb, 128), lambda p, t: (tmap(t), 0))] * 2
    tabs = [pl.BlockSpec((None, 256, 128), lambda p, t: (p, 0, 0))] + \
           [pl.BlockSpec((None, 128, 128), lambda p, t: (p, 0, 0))] * 3
    return qkv, rope, tabs


def _ret_fwd(proj, cos, ss, tabs, gnw, *, name):
    T = proj.shape[0]
    tb = min(T, 1024)
    nt = T // tb
    nchunk = tb // RET_CHUNK

    def body(q_ref, k_ref, v_ref, g_ref, cos_ref, ss_ref, dm_ref, xi_ref, zt_ref, cd_ref, gnw_ref,
             y_ref, o_ref, r_sc):
        @pl.when(pl.program_id(1) == 0)
        def _():
            r_sc[...] = jnp.zeros_like(r_sc)
        m0, m1 = _head_masks((128, 128))
        dm, xi, zt, cd = dm_ref[...], xi_ref[...], zt_ref[...], cd_ref[...]
        bm = (cd > 0).astype(F32)
        gnw = gnw_ref[...]
        for c in range(nchunk):
            rs = pl.ds(c * RET_CHUNK, RET_CHUNK)
            cs, sn = cos_ref[rs, :], ss_ref[rs, :]
            q = _rope(q_ref[rs, :], cs, sn, 32, 64)
            k = _rope(k_ref[rs, :], cs, sn, 32, 64) * K_SCALE
            v = v_ref[rs, :]
            kb, vb = k.astype(BF16), v.astype(BF16)
            qs = jnp.concatenate([q * m0, q * m1], axis=0).astype(BF16)
            s = (_dot_nt(qs, kb) * dm).astype(BF16)
            vs = jnp.concatenate([v * m0, v * m1], axis=0).astype(BF16)
            o = _dot(jnp.concatenate([s[:128], s[128:]], axis=1), vs)
            r = r_sc[...]
            o = o + _dot(q.astype(BF16), r.astype(BF16)) * xi
            r_sc[...] = cd * r + bm * _dot_tn((k * zt).astype(BF16), vb)
            mu = (jnp.sum(o * m0, axis=1, keepdims=True) * m0 + jnp.sum(o * m1, axis=1, keepdims=True) * m1) * (1.0 / 64)
            d = o - mu
            dd = d * d
            var = (jnp.sum(dd * m0, axis=1, keepdims=True) * m0 + jnp.sum(dd * m1, axis=1, keepdims=True) * m1) * (1.0 / 64)
            oh = d * lax.rsqrt(var + EPS)
            g = g_ref[rs, :]
            y_ref[rs, :] = (g * _sigmoid(g) * (oh * gnw)).astype(BF16)
            o_ref[rs, :] = o

    qkv, rope, tspec = _ret_specs(tb, False, nt)
    gspec = pl.BlockSpec((tb, 128), lambda p, t: (t, 12 + p))
    out = pl.BlockSpec((tb, 128), lambda p, t: (t, p))
    return pl.pallas_call(
        body, name=name, grid=(4, nt),
        in_specs=qkv + [gspec] + rope + tspec + [pl.BlockSpec((1, 128), lambda p, t: (0, p))],
        out_specs=[out, out],
        out_shape=[jax.ShapeDtypeStruct((T, RET_WIDTH), BF16), jax.ShapeDtypeStruct((T, RET_WIDTH), F32)],
        scratch_shapes=[pltpu.VMEM((128, 128), F32)],
        compiler_params=_cp(("parallel", "arbitrary")))(proj, proj, proj, proj, cos, ss, *tabs, gnw)


def _ret_bwd_dq(proj, do, cos, ss, tabs, *, name):
    T = proj.shape[0]
    tb = min(T, 1024)
    nt = T // tb
    nchunk = tb // RET_CHUNK

    def body(q_ref, k_ref, v_ref, do_ref, cos_ref, ss_ref, dm_ref, xi_ref, zt_ref, cd_ref, dq_ref, r_sc):
        del q_ref
        @pl.when(pl.program_id(1) == 0)
        def _():
            r_sc[...] = jnp.zeros_like(r_sc)
        m0, m1 = _head_masks((128, 128))
        dm, xi, zt, cd = dm_ref[...], xi_ref[...], zt_ref[...], cd_ref[...]
        bm = (cd > 0).astype(F32)
        for c in range(nchunk):
            rs = pl.ds(c * RET_CHUNK, RET_CHUNK)
            cs, sn = cos_ref[rs, :], ss_ref[rs, :]
            k = _rope(k_ref[rs, :], cs, sn, 32, 64) * K_SCALE
            vb = v_ref[rs, :].astype(BF16)
            dob = do_ref[rs, :]
            dof = dob.astype(F32)
            dos = jnp.concatenate([dof * m0, dof * m1], axis=0).astype(BF16)
            a = (_dot_nt(dos, vb) * dm).astype(BF16)
            ks = jnp.concatenate([k * m0, k * m1], axis=0).astype(BF16)
            r = r_sc[...]
            dq = _dot(jnp.concatenate([a[:128], a[128:]], axis=1), ks) + _dot_nt(dob, r.astype(BF16)) * xi
            r_sc[...] = cd * r + bm * _dot_tn((k * zt).astype(BF16), vb)
            dq_ref[rs, :] = _rope_t(dq, cs, sn, 32, 64).astype(BF16)

    qkv, rope, tspec = _ret_specs(tb, False, nt)
    blk = pl.BlockSpec((tb, 128), lambda p, t: (t, p))
    return pl.pallas_call(
        body, name=name, grid=(4, nt), in_specs=qkv + [blk] + rope + tspec, out_specs=blk,
        out_shape=jax.ShapeDtypeStruct((T, RET_WIDTH), BF16),
        scratch_shapes=[pltpu.VMEM((128, 128), F32)],
        compiler_params=_cp(("parallel", "arbitrary")))(proj, proj, proj, do, cos, ss, *tabs)


def _ret_bwd_dkv(proj, do, cos, ss, tabs, *, name):
    T = proj.shape[0]
    tb = min(T, 1024)
    nt = T // tb
    nchunk = tb // RET_CHUNK

    def body(q_ref, k_ref, v_ref, do_ref, cos_ref, ss_ref, dm_ref, xi_ref, zt_ref, cd_ref, dk_ref, dv_ref, u_sc):
        @pl.when(pl.program_id(1) == 0)
        def _():
            u_sc[...] = jnp.zeros_like(u_sc)
        m0, m1 = _head_masks((128, 128))
        dm, xi, zt, cd = dm_ref[...], xi_ref[...], zt_ref[...], cd_ref[...]
        bm = (cd > 0).astype(F32)
        for c in reversed(range(nchunk)):
            rs = pl.ds(c * RET_CHUNK, RET_CHUNK)
            cs, sn = cos_ref[rs, :], ss_ref[rs, :]
            q = _rope(q_ref[rs, :], cs, sn, 32, 64)
            k = _rope(k_ref[rs, :], cs, sn, 32, 64) * K_SCALE
            kb = k.astype(BF16)
            vb = v_ref[rs, :].astype(BF16)
            dob = do_ref[rs, :]
            dof = dob.astype(F32)
            qs = jnp.concatenate([q * m0, q * m1], axis=0).astype(BF16)
            dos = jnp.concatenate([dof * m0, dof * m1], axis=0).astype(BF16)
            s = (_dot_nt(qs, kb) * dm).astype(BF16)
            a = (_dot_nt(dos, vb) * dm).astype(BF16)
            ub = u_sc[...].astype(BF16)
            dk = _dot_tn(a, qs) + _dot_nt(vb, ub) * zt
            dv = _dot_tn(s, dos) + _dot(kb, ub) * zt
            u_sc[...] = cd * u_sc[...] + bm * _dot_tn((q * xi).astype(BF16), dob)
            dk_ref[rs, :] = (_rope_t(dk, cs, sn, 32, 64) * K_SCALE).astype(BF16)
            dv_ref[rs, :] = dv.astype(BF16)

    qkv, rope, tspec = _ret_specs(tb, True, nt)
    blk = pl.BlockSpec((tb, 128), lambda p, t: (nt - 1 - t, p))
    return pl.pallas_call(
        body, name=name, grid=(4, nt), in_specs=qkv + [blk] + rope + tspec, out_specs=[blk, blk],
        out_shape=[jax.ShapeDtypeStruct((T, RET_WIDTH), BF16)] * 2,
        scratch_shapes=[pltpu.VMEM((128, 128), F32)],
        compiler_params=_cp(("parallel", "arbitrary")))(proj, proj, proj, do, cos, ss, *tabs)


def _mix_bwd(dmixed, o_ret, proj, y_mla, gnw, *, name):
    T = dmixed.shape[0]
    tm = min(T, 512)

    def body(dm_ref, o_ref, g_ref, ym_ref, gnw_ref, do_ref, dg_ref, dom_ref, dl_ref, dw_ref):
        @pl.when(pl.program_id(0) == 0)
        def _():
            dw_ref[...] = jnp.zeros_like(dw_ref)
        m0, m1 = _head_masks((tm, 128))
        lane = lax.broadcasted_iota(jnp.int32, (tm, 128), 1)
        delta = jnp.zeros((tm, 128), F32)

        def gsum(z):
            return jnp.sum(z * m0, axis=1, keepdims=True) * m0 + jnp.sum(z * m1, axis=1, keepdims=True) * m1

        for p in range(4):
            cs = slice(128 * p, 128 * p + 128)
            dy = dm_ref[:, cs]
            o = o_ref[:, cs]
            g = g_ref[:, cs]
            w = gnw_ref[:, cs]
            d = o - gsum(o) * (1.0 / 64)
            rstd = lax.rsqrt(gsum(d * d) * (1.0 / 64) + EPS)
            oh = d * rstd
            sg = _sigmoid(g)
            dn = dy * (g * sg)
            dg_ref[:, cs] = (dy * (oh * w) * (sg * (1.0 + g * (1.0 - sg)))).astype(BF16)
            dw_ref[:, cs] += jnp.sum(dn * oh, axis=0, keepdims=True)
            doh = dn * w
            do = rstd * (doh - gsum(doh) * (1.0 / 64) - oh * (gsum(doh * oh) * (1.0 / 64)))
            do_ref[:, cs] = do.astype(BF16)
            dom = dm_ref[:, 512 + 128 * p:512 + 128 * p + 128]
            dom_ref[:, cs] = dom.astype(BF16)
            pr = dom * ym_ref[:, cs].astype(F32)
            delta = jnp.where(lane == 2 * p, jnp.sum(pr * m0, axis=1, keepdims=True), delta)
            delta = jnp.where(lane == 2 * p + 1, jnp.sum(pr * m1, axis=1, keepdims=True), delta)
        dl_ref[...] = delta

    half = pl.BlockSpec((tm, 512), lambda i: (i, 0))
    return pl.pallas_call(
        body, name=name, grid=(T // tm,),
        in_specs=[pl.BlockSpec((tm, 1024), lambda i: (i, 0)), half, pl.BlockSpec((tm, 512), lambda i: (i, 3)),
                  half, pl.BlockSpec((1, 512), lambda i: (0, 0))],
        out_specs=[half, half, half, pl.BlockSpec((tm, 128), lambda i: (i, 0)), pl.BlockSpec((1, 512), lambda i: (0, 0))],
        out_shape=[jax.ShapeDtypeStruct((T, 512), BF16)] * 3 + [jax.ShapeDtypeStruct((T, 128), F32),
                                                                jax.ShapeDtypeStruct((1, 512), F32)],
        compiler_params=_cp(("arbitrary",)))(dmixed, o_ret, proj, y_mla, gnw)


def _mla_prep_fwd(proj, qnw, kvnw, wuq, wk, wv, cos, ss, *, name):
    T = proj.shape[0]
    tm = min(T, 512)

    def body(lat_ref, qnw_ref, kvnw_ref, wuq_ref, wk_ref, wv_ref, cos_ref, ss_ref,
             q_ref, k_ref, v_ref, cqn_ref, ckvn_ref):
        cq = lat_ref[:, 0:256]
        ckv = lat_ref[:, 256:384]
        g3 = lat_ref[:, 384:512]
        cqn = (cq * lax.rsqrt(jnp.mean(cq * cq, axis=-1, keepdims=True) + EPS) * qnw_ref[...]).astype(BF16)
        ckvn = (ckv * lax.rsqrt(jnp.mean(ckv * ckv, axis=-1, keepdims=True) + EPS) * kvnw_ref[...]).astype(BF16)
        cqn_ref[...] = cqn
        ckvn_ref[...] = ckvn
        cs, sn = cos_ref[...], ss_ref[...]
        q = _dot_nt(cqn, wuq_ref[...])
        k = _dot_nt(ckvn, wk_ref[...])
        kpe = _rope(g3, cs, sn, 16, 32)
        for h in range(MLA_HEADS):
            hs = slice(128 * h, 128 * h + 128)
            q_ref[:, hs] = (_rope(q[:, hs], cs, sn, 16, 32) * SCALE).astype(BF16)
            k_ref[:, hs] = (k[:, hs] + kpe).astype(BF16)
        v_ref[...] = _dot_nt(ckvn, wv_ref[...]).astype(BF16)

    def full(shape):
        return pl.BlockSpec(shape, lambda i: (0, 0))

    def row(w):
        return pl.BlockSpec((tm, w), lambda i: (i, 0))

    return pl.pallas_call(
        body, name=name, grid=(T // tm,),
        in_specs=[pl.BlockSpec((tm, 512), lambda i: (i, 4)), full((1, 256)), full((1, 128)), full((1024, 256)),
                  full((1024, 128)), full((512, 128)), row(128), row(128)],
        out_specs=[row(1024), row(1024), row(512), row(256), row(128)],
        out_shape=[jax.ShapeDtypeStruct((T, 1024), BF16), jax.ShapeDtypeStruct((T, 1024), BF16),
                   jax.ShapeDtypeStruct((T, 512), BF16), jax.ShapeDtypeStruct((T, 256), BF16),
                   jax.ShapeDtypeStruct((T, 128), BF16)],
        compiler_params=_cp(("parallel",)))(proj, qnw, kvnw, wuq, wk, wv, cos, ss)


def _mla_prep_bwd(dq, dk, dv, proj, qnw, kvnw, wuq_t, wk_t, wv_t, cos, ss, *, name):
    T = proj.shape[0]
    tm = min(T, 512)

    def body(dq_ref, dk_ref, dv_ref, lat_ref, qnw_ref, kvnw_ref, wuq_ref, wk_ref, wv_ref, cos_ref, ss_ref,
             dlat_ref, dqp_ref, dqnw_ref, dkvnw_ref):
        @pl.when(pl.program_id(0) == 0)
        def _():
            dqnw_ref[...] = jnp.zeros_like(dqnw_ref)
            dkvnw_ref[...] = jnp.zeros_like(dkvnw_ref)
        cs, sn = cos_ref[...], ss_ref[...]
        dkpe = jnp.zeros((tm, 128), F32)
        for h in range(MLA_HEADS):
            hs = slice(128 * h, 128 * h + 128)
            dqp_ref[:, hs] = _rope_t(dq_ref[:, hs] * SCALE, cs, sn, 16, 32).astype(BF16)
            dkpe = dkpe + dk_ref[:, hs]
        lane = lax.broadcasted_iota(jnp.int32, (tm, 128), 1)
        rope_lane = (lane >= MLA_NOPE) & (lane < MLA_NOPE + MLA_ROPE)
        dg3 = jnp.where(rope_lane, _rope_t(jnp.where(rope_lane, dkpe, 0.0), cs, sn, 16, 32), 0.0)

        def norm_bwd(x, w, dn):
            r = lax.rsqrt(jnp.mean(x * x, axis=-1, keepdims=True) + EPS)
            xh = x * r
            g = dn * w
            return r * (g - xh * jnp.mean(g * xh, axis=-1, keepdims=True)), jnp.sum(dn * xh, axis=0, keepdims=True)

        dcqn = _dot(dqp_ref[...], wuq_ref[...])
        dcq, dqnw = norm_bwd(lat_ref[:, 0:256], qnw_ref[...], dcqn)
        dckvn = _dot(dk_ref[...].astype(BF16), wk_ref[...]) + _dot(dv_ref[...], wv_ref[...])
        dckv, dkvnw = norm_bwd(lat_ref[:, 256:384], kvnw_ref[...], dckvn)
        dqnw_ref[...] += dqnw
        dkvnw_ref[...] += dkvnw
        dlat_ref[:, 0:256] = dcq.astype(BF16)
        dlat_ref[:, 256:384] = dckv.astype(BF16)
        dlat_ref[:, 384:512] = dg3.astype(BF16)

    def full(shape):
        return pl.BlockSpec(shape, lambda i: (0, 0))

    def row(w):
        return pl.BlockSpec((tm, w), lambda i: (i, 0))

    return pl.pallas_call(
        body, name=name, grid=(T // tm,),
        in_specs=[row(1024), row(1024), row(512), pl.BlockSpec((tm, 512), lambda i: (i, 4)), full((1, 256)),
                  full((1, 128)), full((1024, 256)), full((1024, 128)), full((512, 128)), row(128), row(128)],
        out_specs=[row(512), row(1024), full((1, 256)), full((1, 128))],
        out_shape=[jax.ShapeDtypeStruct((T, 512), BF16), jax.ShapeDtypeStruct((T, 1024), BF16),
                   jax.ShapeDtypeStruct((1, 256), F32), jax.ShapeDtypeStruct((1, 128), F32)],
        compiler_params=_cp(("arbitrary",)))(dq, dk, dv, proj, qnw, kvnw, wuq_t, wk_t, wv_t, cos, ss)


def _flash_fwd(q, k, v1, *, name):
    T = q.shape[0]
    tq = min(T, 512)
    tk = tq
    nq = T // tq

    def body(q_ref, k_ref, v_ref, y_ref, lse_ref):
        qi = pl.program_id(1)
        row = lax.broadcasted_iota(jnp.int32, (tq, tk), 0)
        col = lax.broadcasted_iota(jnp.int32, (tq, tk), 1)

        def step(kb, carry, masked):
            ks = pl.ds(pl.multiple_of(kb * tk, tk), tk)
            new = []
            for h in range(2):
                hs = slice(128 * h, 128 * h + 128)
                m, acc = carry[h]
                s = _dot_nt(q_ref[:, hs], k_ref[ks, hs])
                if masked:
                    s = jnp.where(col <= row, s, NEG)
                mn = jnp.maximum(m, jnp.max(s, axis=1, keepdims=True))
                p = jnp.exp((s - mn).astype(BF16))
                acc = jnp.exp(m - mn) * acc + _dot(p, v_ref[ks, hs])
                new.append((mn, acc))
            return tuple(new)

        init = (jnp.full((tq, 1), NEG, F32), jnp.zeros((tq, 128), F32))
        carry = lax.fori_loop(0, qi, lambda kb, c: step(kb, c, False), (init, init))
        (ma, acca), (mb, accb) = step(qi, carry, True)
        lane = lax.broadcasted_iota(jnp.int32, (tq, 128), 1)
        la, lb = pltpu.roll(acca, 64, 1), pltpu.roll(accb, 64, 1)
        y_ref[...] = jnp.where(lane < 64, acca / la, accb / lb).astype(BF16)
        lse_ref[0] = ma + jnp.log(acca[:, 64:65])
        lse_ref[1] = mb + jnp.log(accb[:, 0:1])

    return pl.pallas_call(
        body, name=name, grid=(4, nq),
        in_specs=[pl.BlockSpec((tq, 256), lambda p, i: (i, p)), pl.BlockSpec((T, 256), lambda p, i: (0, p)),
                  pl.BlockSpec((T, 256), lambda p, i: (0, p))],
        out_specs=[pl.BlockSpec((tq, 128), lambda p, i: (i, p)), pl.BlockSpec((2, tq, 1), lambda p, i: (p, i, 0))],
        out_shape=[jax.ShapeDtypeStruct((T, MLA_WIDTH), BF16), jax.ShapeDtypeStruct((MLA_HEADS, T, 1), F32)],
        compiler_params=_cp(("parallel", "arbitrary")))(q, k, v1)


def _flash_bwd(q, k, v, do, lse, delta, *, name):
    T = q.shape[0]
    tq = min(T, 512)
    tk = tq
    nq = T // tq

    def body(q_ref, k_ref, v_ref, do_ref, lse_ref, dl_ref, dqt_ref, dk_ref, dv_ref):
        kb = pl.program_id(1)

        @pl.when(kb == 0)
        def _():
            dqt_ref[...] = jnp.zeros_like(dqt_ref)
        krow = lax.broadcasted_iota(jnp.int32, (tk, tq), 0)
        qcol = lax.broadcasted_iota(jnp.int32, (tk, tq), 1)
        masks = _head_masks((tk, 128))
        vf = v_ref[...].astype(F32)
        vms = [(vf * masks[h]).astype(BF16) for h in range(2)]

        def step(qi, carry, masked):
            qs = pl.ds(pl.multiple_of(qi * tq, tq), tq)
            dob = do_ref[qs, :]
            dof = dob.astype(F32)
            dks, dv_acc = list(carry[:2]), carry[2]
            for h in range(2):
                hs = slice(128 * h, 128 * h + 128)
                kh = k_ref[:, hs]
                qh = q_ref[qs, hs]
                st = _dot_nt(kh, qh)
                pt = jnp.exp((st - lse_ref[h, qi]).astype(BF16))
                if masked:
                    pt = jnp.where(krow <= qcol, pt, jnp.zeros_like(pt))
                dv_acc = dv_acc + _dot(pt, (dof * masks[h]).astype(BF16))
                dpt = _dot_nt(vms[h], dob)
                dst = pt * (dpt - dl_ref[h, qi]).astype(BF16)
                dks[h] = dks[h] + _dot(dst, qh)
                dqt_ref[qi, hs, :] += _dot_tn(kh, dst)
            return dks[0], dks[1], dv_acc

        zero = jnp.zeros((tk, 128), F32)
        carry = step(kb, (zero, zero, zero), True)
        dk0, dk1, dv_acc = lax.fori_loop(kb + 1, nq, lambda qi, c: step(qi, c, False), carry)
        dk_ref[:, 0:128] = dk0
        dk_ref[:, 128:256] = dk1
        dv_ref[...] = dv_acc.astype(BF16)

    stat = pl.BlockSpec((2, nq, 1, tq), lambda p, j: (p, 0, 0, 0))
    return pl.pallas_call(
        body, name=name, grid=(4, nq),
        in_specs=[pl.BlockSpec((T, 256), lambda p, j: (0, p)), pl.BlockSpec((tk, 256), lambda p, j: (j, p)),
                  pl.BlockSpec((tk, 128), lambda p, j: (j, p)), pl.BlockSpec((T, 128), lambda p, j: (0, p)), stat, stat],
        out_specs=[pl.BlockSpec((None, nq, 256, tq), lambda p, j: (p, 0, 0, 0)),
                   pl.BlockSpec((tk, 256), lambda p, j: (j, p)), pl.BlockSpec((tk, 128), lambda p, j: (j, p))],
        out_shape=[jax.ShapeDtypeStruct((4, nq, 256, tq), F32), jax.ShapeDtypeStruct((T, 1024), F32),
                   jax.ShapeDtypeStruct((T, MLA_WIDTH), BF16)],
        compiler_params=_cp(("parallel", "arbitrary")))(q, k, v, do, lse, delta)


def _shift_down(x, n, prev8):
    r = pltpu.roll(x, n, 0)
    row = lax.broadcasted_iota(jnp.int32, prev8.shape, 0)
    first = jnp.where(row < n, pltpu.roll(prev8, n, 0), r[:8])
    if x.shape[0] == 8:
        return first
    return jnp.concatenate([first, r[8:]], axis=0)


def _shift_up(x, n, next8):
    tm = x.shape[0]
    r = pltpu.roll(x, tm - n, 0)
    row = lax.broadcasted_iota(jnp.int32, next8.shape, 0)
    last = jnp.where(row >= 8 - n, pltpu.roll(next8, 8 - n, 0), r[tm - 8:])
    return jnp.concatenate([r[:tm - 8], last], axis=0)


def _conv_pre(u, prev8, cw_ref, cb_ref):
    p1 = _shift_down(u, 1, prev8)
    p2 = _shift_down(u, 2, prev8)
    up = cb_ref[...] + cw_ref[0:1, :] * p2 + cw_ref[1:2, :] * p1 + cw_ref[2:3, :] * u
    return up, p1, p2


def _conv_fwd(u, cw, cb, *, name):
    T = u.shape[0]
    tm = min(T, 512)
    W = 2 * FF_HALF

    def body(u_ref, prev_ref, cw_ref, cb_ref, a_ref):
        prev = jnp.where(pl.program_id(0) > 0, prev_ref[...], 0.0)
        up, _, _ = _conv_pre(u_ref[...], prev, cw_ref, cb_ref)
        gate = up[:, :FF_HALF]
        a_ref[...] = (gate * _sigmoid(gate) * up[:, FF_HALF:]).astype(BF16)

    return pl.pallas_call(
        body, name=name, grid=(T // tm, 2),
        in_specs=[pl.BlockSpec((tm, W), lambda i, j: (i, j)),
                  pl.BlockSpec((8, W), lambda i, j: (jnp.maximum(i * (tm // 8) - 1, 0), j)),
                  pl.BlockSpec((3, W), lambda i, j: (0, j)), pl.BlockSpec((1, W), lambda i, j: (0, j))],
        out_specs=pl.BlockSpec((tm, FF_HALF), lambda i, j: (i, j)),
        out_shape=jax.ShapeDtypeStruct((T, D_FF), BF16),
        compiler_params=_cp(("parallel", "parallel")))(u, u, cw, cb)


def _conv_bwd(u, da, cw, cb, *, name):
    T = u.shape[0]
    tm = min(T, 512)
    W = 2 * FF_HALF
    nt = T // tm

    def body(u_ref, prev_ref, next_ref, da_ref, dan_ref, cw_ref, cb_ref, du_ref, dw0_ref, dw1_ref, dw2_ref, db_ref):
        i = pl.program_id(1)

        @pl.when(i == 0)
        def _():
            for r in (dw0_ref, dw1_ref, dw2_ref, db_ref):
                r[...] = jnp.zeros_like(r)

        def dpre(u, prev8, da):
            up, p1, p2 = _conv_pre(u, prev8, cw_ref, cb_ref)
            gate, val = up[:, :FF_HALF], up[:, FF_HALF:]
            sg = _sigmoid(gate)
            dgate = da * val * (sg * (1.0 + gate * (1.0 - sg)))
            dval = da * (gate * sg)
            return jnp.concatenate([dgate, dval], axis=1), p1, p2

        u = u_ref[...]
        prev = jnp.where(i > 0, prev_ref[...], 0.0)
        dup, p1, p2 = dpre(u, prev, da_ref[...])
        dupn, _, _ = dpre(next_ref[...], u[tm - 8:], dan_ref[...])
        dupn = jnp.where(i < nt - 1, dupn, 0.0)
        du = cw_ref[2:3, :] * dup + cw_ref[1:2, :] * _shift_up(dup, 1, dupn) + cw_ref[0:1, :] * _shift_up(dup, 2, dupn)
        du_ref[...] = du.astype(BF16)
        dw0_ref[...] += jnp.sum(dup * p2, axis=0, keepdims=True)
        dw1_ref[...] += jnp.sum(dup * p1, axis=0, keepdims=True)
        dw2_ref[...] += jnp.sum(dup * u, axis=0, keepdims=True)
        db_ref[...] += jnp.sum(dup, axis=0, keepdims=True)

    nxt = lambda j, i: (jnp.minimum((i + 1) * (tm // 8), T // 8 - 1), j)
    vec = pl.BlockSpec((1, W), lambda j, i: (0, j))
    return pl.pallas_call(
        body, name=name, grid=(2, nt),
        in_specs=[pl.BlockSpec((tm, W), lambda j, i: (i, j)),
                  pl.BlockSpec((8, W), lambda j, i: (jnp.maximum(i * (tm // 8) - 1, 0), j)),
                  pl.BlockSpec((8, W), nxt),
                  pl.BlockSpec((tm, FF_HALF), lambda j, i: (i, j)), pl.BlockSpec((8, FF_HALF), nxt),
                  pl.BlockSpec((3, W), lambda j, i: (0, j)), vec],
        out_specs=[pl.BlockSpec((tm, W), lambda j, i: (i, j)), vec, vec, vec, vec],
        out_shape=[jax.ShapeDtypeStruct((T, 2 * D_FF), BF16)] + [jax.ShapeDtypeStruct((1, 2 * D_FF), F32)] * 4,
        compiler_params=_cp(("parallel", "arbitrary")))(u, u, u, da, da, cw, cb)


def _adamw(w, m, v, g_slots, *, name):
    R, C = w.shape
    ns = g_slots.shape[0]
    tr = _row_tile(R)

    def body(w_ref, m_ref, v_ref, g_ref, go_ref, d_ref, mo_ref, vo_ref):
        g = g_ref[0].astype(F32)
        for s in range(1, ns):
            g = g + g_ref[s].astype(F32)
        mn = ADAM_B1 * m_ref[...] + (1.0 - ADAM_B1) * g
        vn = ADAM_B2 * v_ref[...] + (1.0 - ADAM_B2) * (g * g)
        m_hat = mn / (1.0 - ADAM_B1 ** ADAM_STEP)
        v_hat = vn / (1.0 - ADAM_B2 ** ADAM_STEP)
        go_ref[...] = g
        d_ref[...] = -ADAM_LR * (m_hat / (jnp.sqrt(v_hat) + ADAM_EPS) + ADAM_WD * w_ref[...])
        mo_ref[...] = mn
        vo_ref[...] = vn

    blk = pl.BlockSpec((tr, C), lambda i: (i, 0))
    return pl.pallas_call(
        body, name=name, grid=(R // tr,),
        in_specs=[blk, blk, blk, pl.BlockSpec((ns, tr, C), lambda i: (0, i, 0))],
        out_specs=[blk] * 4, out_shape=[jax.ShapeDtypeStruct((R, C), F32)] * 4,
        compiler_params=_cp(("parallel",)))(w, m, v, g_slots)


def _place():
    return lax.axis_index("x"), lax.axis_index("y"), lax.axis_index("c")


def _all_gather(x, *, name, in_vmem):
    def body(x_ref, out_ref, send_sems, recv_sems, local_sem):
        x_, y_, c_ = _place()
        me, sibling = (x_, y_, c_), (x_, y_, 1 - c_)
        chips = [(1 - x_, y_), (x_, 1 - y_), (1 - x_, 1 - y_)]

        def slot(px, py, pc):
            return out_ref.at[4 * px + 2 * py + pc]

        def copy(k, block, to, src=None):
            return pltpu.make_async_remote_copy(
                src_ref=slot(*block) if src is None else src, dst_ref=slot(*block),
                send_sem=send_sems.at[k], recv_sem=recv_sems.at[k], device_id=to, device_id_type=MESH)

        mine = pltpu.make_async_copy(x_ref, slot(*me), local_sem)
        mine.start()
        first = [copy(0, me, sibling, src=x_ref)]
        first += [copy(1 + j, me, (*chip, c_), src=x_ref) for j, chip in enumerate(chips)]
        for cp in first:
            cp.start()
        passed = [copy(4 + j, (*chip, c_), sibling) for j, chip in enumerate(chips)]
        for j, chip in enumerate(chips):
            copy(1 + j, (*chip, c_), me).wait_recv()
            passed[j].start()
        copy(0, sibling, me).wait_recv()
        for j, chip in enumerate(chips):
            copy(4 + j, (*chip, 1 - c_), me).wait_recv()
        for cp in first + passed:
            cp.wait_send()
        mine.wait()

    spec = pl.BlockSpec(memory_space=pltpu.VMEM) if in_vmem else ANY
    return pl.pallas_call(
        body, name=name, out_shape=jax.ShapeDtypeStruct((N_DEV,) + x.shape, x.dtype),
        in_specs=[spec], out_specs=spec,
        scratch_shapes=[pltpu.SemaphoreType.DMA((7,)), pltpu.SemaphoreType.DMA((7,)), pltpu.SemaphoreType.DMA],
        compiler_params=pltpu.CompilerParams(vmem_limit_bytes=VMEM_LIMIT))(x)


def _sum_slots(g, *, name):
    n = g.shape[0]

    def body(g_ref, o_ref):
        acc = g_ref[0]
        for s in range(1, n):
            acc = acc + g_ref[s]
        o_ref[...] = acc

    return pl.pallas_call(body, name=name, out_shape=jax.ShapeDtypeStruct(g.shape[1:], g.dtype))(g)


def _swap_sibling(x, *, name):
    def body(x_ref, out_ref, send_sem, recv_sem):
        x_, y_, c_ = _place()
        cp = pltpu.make_async_remote_copy(src_ref=x_ref, dst_ref=out_ref, send_sem=send_sem, recv_sem=recv_sem,
                                          device_id=(x_, y_, 1 - c_), device_id_type=MESH)
        cp.start()
        cp.wait()

    return pl.pallas_call(
        body, name=name, out_shape=jax.ShapeDtypeStruct(x.shape, x.dtype), in_specs=[ANY], out_specs=ANY,
        scratch_shapes=[pltpu.SemaphoreType.DMA, pltpu.SemaphoreType.DMA])(x)


def _exchange_chips(p, *, name):
    def body(p_ref, out_ref, send_sems, recv_sems, local_sem):
        x_, y_, c_ = _place()
        me_k = 2 * x_ + y_
        chips = [(1 - x_, y_), (x_, 1 - y_), (1 - x_, 1 - y_)]
        local = pltpu.make_async_copy(p_ref.at[me_k], out_ref.at[me_k], local_sem)
        local.start()

        def copy(j, src_k, dst_k, chip):
            return pltpu.make_async_remote_copy(
                src_ref=p_ref.at[src_k], dst_ref=out_ref.at[dst_k], send_sem=send_sems.at[j],
                recv_sem=recv_sems.at[j], device_id=(*chip, c_), device_id_type=MESH)

        sends = [copy(j, 2 * px + py, me_k, (px, py)) for j, (px, py) in enumerate(chips)]
        for cp in sends:
            cp.start()
        for j, (px, py) in enumerate(chips):
            copy(j, me_k, 2 * px + py, (px, py)).wait_recv()
        for cp in sends:
            cp.wait_send()
        local.wait()

    return pl.pallas_call(
        body, name=name, out_shape=jax.ShapeDtypeStruct(p.shape, p.dtype), in_specs=[ANY], out_specs=ANY,
        scratch_shapes=[pltpu.SemaphoreType.DMA((3,)), pltpu.SemaphoreType.DMA((3,)), pltpu.SemaphoreType.DMA])(p)


def _row_tile(R):
    for cand in (256, 400, 200):
        if R % cand == 0:
            return cand
    return R


def _add2(a, b, *, name, out_dtype):
    n, R, C = a.shape
    tr = _row_tile(R)

    def body(a_ref, b_ref, o_ref):
        o_ref[...] = (a_ref[...] + b_ref[...]).astype(out_dtype)

    blk = pl.BlockSpec((1, tr, C), lambda s, i: (s, i, 0))
    return pl.pallas_call(body, name=name, grid=(n, R // tr), in_specs=[blk, blk], out_specs=blk,
                          out_shape=jax.ShapeDtypeStruct(a.shape, out_dtype),
                          compiler_params=_cp(("parallel", "parallel")))(a, b)


def _pack_local(parts):
    segs = []
    for n, r, rp, tr in BIG:
        w = parts[n].T if tr else parts[n]
        segs.append(jnp.pad(w.reshape(r, PACK_COLS), ((0, rp - r), (0, 0))))
    used = sum(rp for _, _, rp, _ in BIG)
    segs.append(jnp.zeros((PACK_ROWS - used, PACK_COLS), segs[0].dtype))
    return jnp.concatenate(segs, axis=0)


def _unpack_local(packed, like):
    out, off = {}, 0
    for n, r, rp, tr in BIG:
        rows, cols = like[n].shape
        seg = packed[off:off + r]
        out[n] = (seg.reshape(cols, rows).T if tr else seg)[None]
        off += rp
    return out


def _segments(g):
    out, off = {}, 0
    for n, r, rp, _ in BIG:
        out[n] = g[:, off:off + r]
        off += rp
    return out


def _pack_grads(g):
    g_in = jnp.concatenate([g["w_in_t"][:2432], g["w_in_t"][2496:2528]], axis=0).reshape(N_DEV, 308, PACK_COLS)
    g_uq = g["w_uq_t"].reshape(N_DEV, 128, MLA_Q_RANK)[:, :96].reshape(N_DEV, 24, PACK_COLS)
    g_ukv = jnp.concatenate([g["w_k_t"].reshape(N_DEV, 128, MLA_KV_RANK)[:, :64],
                             g["w_v_t"].reshape(N_DEV, 64, MLA_KV_RANK)], axis=1).reshape(N_DEV, 16, PACK_COLS)
    up = g["w_up_t"].reshape(N_DEV, 704, PACK_COLS)
    g_up = jnp.stack([up[FF_OWNER_ORDER.index(d)] for d in range(N_DEV)])
    parts = dict(w_in=g_in, w_uq=g_uq, w_ukv=g_ukv, w_out=g["w_out"].reshape(N_DEV, 128, PACK_COLS), w_up=g_up,
                 w_down=g["w_down"].reshape(N_DEV, 352, PACK_COLS))
    segs = [jnp.pad(parts[n], ((0, 0), (0, rp - r), (0, 0))) for n, r, rp, _ in BIG]
    used = sum(rp for _, _, rp, _ in BIG)
    segs.append(jnp.zeros((N_DEV, PACK_ROWS - used, PACK_COLS), F32))
    return jnp.concatenate(segs, axis=1)


def _interleave_ff(w):
    g, v = w[..., :D_FF], w[..., D_FF:]
    return jnp.concatenate([g[..., :FF_HALF], v[..., :FF_HALF], g[..., FF_HALF:], v[..., FF_HALF:]], axis=-1)


def _deinterleave_ff(w):
    b = [w[..., i * FF_HALF:(i + 1) * FF_HALF] for i in range(4)]
    return jnp.concatenate([b[0], b[2], b[1], b[3]], axis=-1)


def _rope_tables(pos):
    p = pos.astype(F32)[:, None]
    inv_r = ROPE_BASE ** (-jnp.arange(0, RET_HEAD_DIM, 2, dtype=F32) / RET_HEAD_DIM)
    ang = p * inv_r
    c, s = jnp.cos(ang), jnp.sin(ang)
    cos_r = jnp.concatenate([c, c, c, c], axis=1)
    ss_r = jnp.concatenate([-s, s, -s, s], axis=1)
    inv_m = ROPE_BASE ** (-jnp.arange(0, MLA_ROPE, 2, dtype=F32) / MLA_ROPE)
    ang = p * inv_m
    c, s = jnp.cos(ang), jnp.sin(ang)
    T = pos.shape[0]
    cos_m = jnp.concatenate([jnp.ones((T, 64), F32), c, c, jnp.ones((T, 32), F32)], axis=1)
    ss_m = jnp.concatenate([jnp.zeros((T, 64), F32), -s, s, jnp.zeros((T, 32), F32)], axis=1)
    return cos_r, ss_r, cos_m, ss_m


def _prep_weights(seg):
    w_in_t = seg["w_in"].reshape(IN_WIDTH, D_MODEL)
    z = lambda n: jnp.zeros((n, D_MODEL), BF16)
    w_in_t = jnp.concatenate([w_in_t[:2432], z(64), w_in_t[2432:2464], z(32)], axis=0)
    w_uq_t = jnp.pad(seg["w_uq"].reshape(MLA_HEADS, 96, MLA_Q_RANK), ((0, 0), (0, 32), (0, 0))).reshape(1024, MLA_Q_RANK)
    ukv = seg["w_ukv"].reshape(MLA_HEADS, 128, MLA_KV_RANK)
    w_k_t = jnp.pad(ukv[:, :64], ((0, 0), (0, 64), (0, 0))).reshape(1024, MLA_KV_RANK)
    w_v_t = ukv[:, 64:].reshape(512, MLA_KV_RANK)
    w_up_t = jnp.concatenate([seg["w_up"][d] for d in FF_OWNER_ORDER], axis=0)
    return dict(w_in_t=w_in_t, w_uq_t=w_uq_t, w_k_t=w_k_t, w_v_t=w_v_t, w_out=seg["w_out"].reshape(1024, D_MODEL),
                w_up_t=w_up_t, w_down=seg["w_down"].reshape(D_FF, D_MODEL))


def _local_step(x, pos, tgt, W, sm):
    cos_r, ss_r, cos_m, ss_m = _rope_tables(pos)
    tabs = _ret_tables()

    h = _rmsnorm_fwd(x, sm["attn_norm_w"], name="attn_norm")
    proj = _mm(h, W["w_in_t"], bt=True, name="in_proj")
    y_ret, o_ret = _ret_fwd(proj, cos_r, ss_r, tabs, sm["ret_gn_w"], name="ret_fwd")
    q, k, v, cqn, ckvn = _mla_prep_fwd(proj, sm["mla_q_norm_w"], sm["mla_kv_norm_w"], W["w_uq_t"], W["w_k_t"],
                                       W["w_v_t"], cos_m, ss_m, name="mla_prep")
    T = x.shape[0]
    tq = min(T, 512)
    vp = v.reshape(T, 4, 2, MLA_V)
    one = jnp.ones((T, 4, MLA_V), BF16)
    v1 = jnp.stack([vp[:, :, 0], one, one, vp[:, :, 1]], axis=2).reshape(T, MLA_HEADS * 128)
    y_mla, lse = _flash_fwd(q, k, v1, name="mla_attn")
    lse = lse.reshape(MLA_HEADS, T // tq, 1, tq)
    mixed = jnp.concatenate([y_ret, y_mla], axis=1)
    x1 = _mm(mixed, W["w_out"], add=x, name="out_proj")
    h2 = _rmsnorm_fwd(x1, sm["ffn_norm_w"], name="ffn_norm")
    u = _mm(h2, W["w_up_t"], bt=True, name="up_proj")
    a = _conv_fwd(u, sm["conv_w"], sm["conv_b"], name="conv_gate")
    x2 = _mm(a, W["w_down"], add=x1, name="down_proj")
    loss, dx2, d_final = _loss_head(x2, tgt, sm["final_norm_w"], name="loss_head")

    g = {}
    g["w_down"] = _mm_tn(a, dx2, name="dw_down")
    da = _mm(dx2, W["w_down"], bt=True, name="d_act")
    du, dcw0, dcw1, dcw2, dcb = _conv_bwd(u, da, sm["conv_w"], sm["conv_b"], name="conv_bwd")
    g["w_up_t"] = _mm_tn(du, h2, name="dw_up")
    dh2 = _mm(du, W["w_up_t"], name="d_h2")
    dx1, d_ffn = _rmsnorm_bwd(x1, sm["ffn_norm_w"], dh2, dx2, name="ffn_norm_bwd")

    g["w_out"] = _mm_tn(mixed, dx1, name="dw_out")
    dmixed = _mm(dx1, W["w_out"], bt=True, name="d_mixed")
    do_ret, dg, do_mla, delta, d_gn = _mix_bwd(dmixed, o_ret, proj, y_mla, sm["ret_gn_w"], name="mix_bwd")
    drq = _ret_bwd_dq(proj, do_ret, cos_r, ss_r, tabs, name="ret_bwd_dq")
    drk, drv = _ret_bwd_dkv(proj, do_ret, cos_r, ss_r, tabs, name="ret_bwd_dkv")
    delta_r = delta[:, :MLA_HEADS].T.reshape(MLA_HEADS, T // tq, 1, tq)
    dqt, dk, dv = _flash_bwd(q, k, v, do_mla, lse, delta_r, name="mla_attn_bwd")
    dq = dqt.transpose(1, 3, 0, 2).reshape(T, MLA_HEADS * 128)
    dlat, dqp, d_qn, d_kvn = _mla_prep_bwd(dq, dk, dv, proj, sm["mla_q_norm_w"], sm["mla_kv_norm_w"], W["w_uq_t"],
                                           W["w_k_t"], W["w_v_t"], cos_m, ss_m, name="mla_prep_bwd")
    g["w_uq_t"] = _mm_tn(dqp, cqn, name="dw_uq")
    g["w_k_t"] = _mm_tn(dk, ckvn, name="dw_ukv_k")
    g["w_v_t"] = _mm_tn(dv, ckvn, name="dw_ukv_v")
    dproj = jnp.concatenate([drq, drk, drv, dg, dlat], axis=1)
    g["w_in_t"] = _mm_tn(dproj, h, name="dw_in")
    dh = _mm(dproj, W["w_in_t"], name="d_h")
    grad_x, d_attn = _rmsnorm_bwd(x, sm["attn_norm_w"], dh, dx1, name="attn_norm_bwd")

    small = dict(attn_norm_w=d_attn, ret_gn_w=d_gn, mla_q_norm_w=d_qn, mla_kv_norm_w=d_kvn, ffn_norm_w=d_ffn,
                 conv_b=_deinterleave_ff(dcb), final_norm_w=d_final,
                 conv_w=_deinterleave_ff(jnp.concatenate([dcw0, dcw1, dcw2], axis=0)))
    return loss, grad_x, g, small


def kernel(x, positions, attn_norm_w, w_in, ret_gn_w, mla_q_norm_w, w_uq, mla_kv_norm_w, w_ukv, w_out, ffn_norm_w, w_up, conv_w, conv_b, w_down, final_norm_w, loss_target, m_attn_norm_w, m_w_in, m_ret_gn_w, m_mla_q_norm_w, m_w_uq, m_mla_kv_norm_w, m_w_ukv, m_w_out, m_ffn_norm_w, m_w_up, m_conv_w, m_conv_b, m_w_down, m_final_norm_w, v_attn_norm_w, v_w_in, v_ret_gn_w, v_mla_q_norm_w, v_w_uq, v_mla_kv_norm_w, v_w_ukv, v_w_out, v_ffn_norm_w, v_w_up, v_conv_w, v_conv_b, v_w_down, v_final_norm_w):
    a = dict(locals())
    x_, y_, c_ = _place()
    dev = 4 * x_ + 2 * y_ + c_

    shard = {n: a[n][0] for n, _, _, _ in BIG}
    gathered = _all_gather(_pack_local({n: w.astype(BF16) for n, w in shard.items()}), name="gather_weights", in_vmem=False)
    W = _prep_weights(_segments(gathered))
    cw_pad = jnp.pad(conv_w[0].reshape(-1), (0, 24 * 128 - 3 * 704)).reshape(24, 128)
    cw_all = _all_gather(cw_pad, name="gather_conv_w", in_vmem=True)
    conv_w_full = cw_all.reshape(N_DEV, -1)[:, :3 * 704].reshape(N_DEV, 3, 704).transpose(1, 0, 2).reshape(3, 2 * D_FF)
    sm = dict(attn_norm_w=attn_norm_w, ret_gn_w=ret_gn_w, mla_q_norm_w=mla_q_norm_w, mla_kv_norm_w=mla_kv_norm_w,
              ffn_norm_w=ffn_norm_w, final_norm_w=final_norm_w.reshape(1, D_MODEL),
              conv_w=_interleave_ff(conv_w_full), conv_b=_interleave_ff(conv_b))

    loss, grad_x, g, gs = _local_step(x[0], positions[0], loss_target[0], W, sm)

    gp = _pack_grads(g).reshape(4, 2, PACK_ROWS, PACK_COLS)
    mine = lax.dynamic_index_in_dim(gp, c_, axis=1, keepdims=False)
    theirs = lax.dynamic_index_in_dim(gp, 1 - c_, axis=1, keepdims=False)
    pair = _add2(mine, _swap_sibling(theirs, name="grad_swap_sibling"), out_dtype=BF16, name="grad_pair_sum")
    slots = _exchange_chips(pair, name="grad_exchange_chips")
    big = _adamw(_pack_local(shard), _pack_local({n: a["m_" + n][0] for n, _, _, _ in BIG}),
                 _pack_local({n: a["v_" + n][0] for n, _, _, _ in BIG}), slots, name="adamw_large")
    big = [_unpack_local(t, shard) for t in big]

    vec = jnp.concatenate([gs[n].reshape(-1) for n, _ in SMALL] + [gs["conv_w"].reshape(-1), loss[0, :1]])
    vec = jnp.pad(vec, (0, SMALL_ROWS * 128 - vec.shape[0])).reshape(SMALL_ROWS, 128)
    tot = _sum_slots(_all_gather(vec, name="gather_small_grads", in_vmem=True), name="sum_small_grads").reshape(-1)
    loss_out = tot[SMALL_N + 3 * 2 * D_FF]
    g_cw = lax.dynamic_slice_in_dim(tot[SMALL_N:SMALL_N + 3 * 2 * D_FF].reshape(3, 2 * D_FF), dev * 704, 704, axis=1)

    def flat_small(prefix):
        return jnp.concatenate([a[prefix + n].reshape(-1) for n, _ in SMALL]).reshape(75, 128)

    sml = _adamw(flat_small(""), flat_small("m_"), flat_small("v_"), tot[:SMALL_N].reshape(1, 75, 128), name="adamw_small")
    cwo = _adamw(conv_w[0], m_conv_w[0], v_conv_w[0], g_cw[None], name="adamw_conv_w")

    def small_of(t, n):
        off = 0
        for nm, sz in SMALL:
            if nm == n:
                return t.reshape(-1)[off:off + sz].reshape(a[n].shape)
            off += sz

    names = ['attn_norm_w', 'w_in', 'ret_gn_w', 'mla_q_norm_w', 'w_uq', 'mla_kv_norm_w', 'w_ukv', 'w_out',
             'ffn_norm_w', 'w_up', 'conv_w', 'conv_b', 'w_down', 'final_norm_w']
    outs = [loss_out, grad_x[None]]
    for kind in range(4):
        for n in names:
            if n == "conv_w":
                outs.append(cwo[kind][None])
            elif n in big[kind]:
                outs.append(big[kind][n])
            else:
                outs.append(small_of(sml[kind], n))
    return tuple(outs)
```

```python
import functools

import numpy as np
import jax
import jax.numpy as jnp
from jax import lax
from jax.experimental import pallas as pl
from jax.experimental.pallas import tpu as pltpu

F32 = jnp.float32
BF16 = jnp.bfloat16
MESH = pl.DeviceIdType.MESH
ANY = pl.BlockSpec(memory_space=pl.ANY)

D_MODEL = 1024
RET_HEADS = 8
RET_HEAD_DIM = 64
RET_WIDTH = 512
RET_CHUNK = 128
MLA_HEADS = 8
MLA_NOPE = 64
MLA_ROPE = 32
MLA_V = 64
MLA_Q_RANK = 256
MLA_KV_RANK = 128
MLA_WIDTH = 512
IN_WIDTH = 2464
IN_PAD = 2560
D_FF = 2816
FF_HALF = 1408
ROPE_BASE = 10000.0
EPS = 1e-6
SCALE = float((MLA_NOPE + MLA_ROPE) ** -0.5)
K_SCALE = 0.125
N_DEV = 8

ADAM_LR = 0.001
ADAM_B1 = 0.9
ADAM_B2 = 0.999
ADAM_EPS = 1e-08
ADAM_WD = 0.01
ADAM_STEP = 10

VMEM_LIMIT = 56 * 1024 * 1024
MM_BUDGET = 40 * 1024 * 1024
NEG = -1e30

PACK_COLS = 1024
EARLY = ((("w_in", 308, 320, True), ("w_uq", 24, 32, True), ("w_ukv", 16, 16, True)), 384)
LATE = ((("w_out", 128, 128, False), ("w_up", 704, 704, True), ("w_down", 352, 352, False)), 1200)
BIG_NAMES = ("w_in", "w_uq", "w_ukv", "w_out", "w_up", "w_down")
FF_OWNER_ORDER = (0, 1, 4, 5, 2, 3, 6, 7)
SMALL = (("attn_norm_w", 1024), ("ret_gn_w", 512), ("mla_q_norm_w", 256), ("mla_kv_norm_w", 128),
         ("ffn_norm_w", 1024), ("conv_b", 5632), ("final_norm_w", 1024))
SMALL_N = 9600
SMALL_ROWS = 208


def _cp(sem=None, vmem=VMEM_LIMIT):
    return pltpu.CompilerParams(dimension_semantics=sem, vmem_limit_bytes=vmem)


def _dot(a, b):
    return jnp.dot(a, b, preferred_element_type=F32)


def _dot_nt(a, b):
    return lax.dot_general(a, b, (((1,), (1,)), ((), ())), preferred_element_type=F32)


def _dot_tn(a, b):
    return lax.dot_general(a, b, (((0,), (0,)), ((), ())), preferred_element_type=F32)


def _sigmoid(x):
    return 0.5 * jnp.tanh(0.5 * x) + 0.5


def _partner(x, half, period):
    n = x.shape[-1]
    lane = lax.broadcasted_iota(jnp.int32, x.shape, 1)
    return jnp.where((lane % period) < half, pltpu.roll(x, n - half, 1), pltpu.roll(x, half, 1))


def _rope(x, cos, ss, half, period):
    return x * cos + _partner(x, half, period) * ss


def _rope_t(dy, cos, ss, half, period):
    return dy * cos - _partner(dy, half, period) * ss


def _head_masks(shape):
    lane = lax.broadcasted_iota(jnp.int32, shape, 1)
    m0 = (lane < 64).astype(F32)
    return m0, 1.0 - m0


def _mm(a, b, *, name, add=None, out_dtype=F32, bt=False):
    M, K = a.shape
    N = b.shape[0] if bt else b.shape[1]
    osz = jnp.dtype(out_dtype).itemsize
    per_row = 2 * (K * a.dtype.itemsize + N * osz + (N * 4 if add is not None else 0))
    tm = 128
    for cand in (512, 256):
        if M % cand == 0 and cand * per_row + 4 * K * N <= MM_BUDGET:
            tm = cand
            break
    tm = min(tm, M)
    mul = _dot_nt if bt else _dot

    def body(*refs):
        if add is None:
            a_ref, b_ref, o_ref = refs
            acc = mul(a_ref[...].astype(BF16), b_ref[...])
        else:
            a_ref, b_ref, r_ref, o_ref = refs
            acc = r_ref[...] + mul(a_ref[...].astype(BF16), b_ref[...])
        o_ref[...] = acc.astype(out_dtype)

    in_specs = [pl.BlockSpec((tm, K), lambda i: (i, 0)), pl.BlockSpec(b.shape, lambda i: (0, 0))]
    args = [a, b]
    if add is not None:
        in_specs.append(pl.BlockSpec((tm, N), lambda i: (i, 0)))
        args.append(add)
    return pl.pallas_call(
        body, name=name, grid=(M // tm,), in_specs=in_specs,
        out_specs=pl.BlockSpec((tm, N), lambda i: (i, 0)),
        out_shape=jax.ShapeDtypeStruct((M, N), out_dtype),
        compiler_params=_cp(("parallel",)))(*args)


def _mm_tn(a, b, *, name):
    T, M = a.shape
    N = b.shape[1]
    tk = min(T, 512)

    def tile(n):
        for cand in (1408, 1280):
            if n > 1408 and n % cand == 0:
                return cand
        return n

    tm, tn = tile(M), tile(N)
    nk = T // tk

    def body(a_ref, b_ref, o_ref):
        @pl.when(pl.program_id(2) == 0)
        def _():
            o_ref[...] = jnp.zeros_like(o_ref)
        o_ref[...] += _dot_tn(a_ref[...].astype(BF16), b_ref[...].astype(BF16))

    return pl.pallas_call(
        body, name=name, grid=(M // tm, N // tn, nk),
        in_specs=[pl.BlockSpec((tk, tm), lambda i, j, k: (k, i)), pl.BlockSpec((tk, tn), lambda i, j, k: (k, j))],
        out_specs=pl.BlockSpec((tm, tn), lambda i, j, k: (i, j)),
        out_shape=jax.ShapeDtypeStruct((M, N), F32),
        compiler_params=_cp(("parallel", "parallel", "arbitrary")))(a, b)


def _rmsnorm_fwd(x, w, *, name):
    T, D = x.shape
    tm = min(T, 1024)

    def body(x_ref, w_ref, o_ref):
        xv = x_ref[...]
        r = lax.rsqrt(jnp.mean(xv * xv, axis=-1, keepdims=True) + EPS)
        o_ref[...] = (xv * r * w_ref[...]).astype(BF16)

    return pl.pallas_call(
        body, name=name, grid=(T // tm,),
        in_specs=[pl.BlockSpec((tm, D), lambda i: (i, 0)), pl.BlockSpec((1, D), lambda i: (0, 0))],
        out_specs=pl.BlockSpec((tm, D), lambda i: (i, 0)),
        out_shape=jax.ShapeDtypeStruct((T, D), BF16),
        compiler_params=_cp(("parallel",)))(x, w)


def _rmsnorm_bwd(x, w, dh, dres, *, name):
    T, D = x.shape
    tm = min(T, 512)

    def body(x_ref, w_ref, dh_ref, dr_ref, dx_ref, dw_ref):
        @pl.when(pl.program_id(0) == 0)
        def _():
            dw_ref[...] = jnp.zeros_like(dw_ref)
        xv = x_ref[...]
        r = lax.rsqrt(jnp.mean(xv * xv, axis=-1, keepdims=True) + EPS)
        xh = xv * r
        dh = dh_ref[...]
        g = dh * w_ref[...]
        dx_ref[...] = dr_ref[...] + r * (g - xh * jnp.mean(g * xh, axis=-1, keepdims=True))
        dw_ref[...] += jnp.sum(dh * xh, axis=0, keepdims=True)

    row = pl.BlockSpec((tm, D), lambda i: (i, 0))
    vec = pl.BlockSpec((1, D), lambda i: (0, 0))
    return pl.pallas_call(
        body, name=name, grid=(T // tm,), in_specs=[row, vec, row, row], out_specs=[row, vec],
        out_shape=[jax.ShapeDtypeStruct((T, D), F32), jax.ShapeDtypeStruct((1, D), F32)],
        compiler_params=_cp(("arbitrary",)))(x, w, dh, dres)


def _loss_head(x2, tgt, w, *, name):
    T, D = x2.shape
    tm = min(T, 512)

    def body(x_ref, t_ref, w_ref, loss_ref, dx_ref, dw_ref):
        @pl.when(pl.program_id(0) == 0)
        def _():
            dw_ref[...] = jnp.zeros_like(dw_ref)
            loss_ref[...] = jnp.zeros_like(loss_ref)
        xv = x_ref[...]
        wv = w_ref[...]
        r = lax.rsqrt(jnp.mean(xv * xv, axis=-1, keepdims=True) + EPS)
        xh = xv * r
        e = xh * wv - t_ref[...]
        part = 0.5 * jnp.sum(jnp.mean(e * e, axis=-1, keepdims=True), axis=0, keepdims=True)
        loss_ref[...] += jnp.broadcast_to(part, loss_ref.shape)
        dy = e * (1.0 / D)
        g = dy * wv
        dx_ref[...] = r * (g - xh * jnp.mean(g * xh, axis=-1, keepdims=True))
        dw_ref[...] += jnp.sum(dy * xh, axis=0, keepdims=True)

    row = pl.BlockSpec((tm, D), lambda i: (i, 0))
    vec = pl.BlockSpec((1, D), lambda i: (0, 0))
    return pl.pallas_call(
        body, name=name, grid=(T // tm,), in_specs=[row, row, vec],
        out_specs=[pl.BlockSpec((1, 128), lambda i: (0, 0)), row, vec],
        out_shape=[jax.ShapeDtypeStruct((1, 128), F32), jax.ShapeDtypeStruct((T, D), F32),
                   jax.ShapeDtypeStruct((1, D), F32)],
        compiler_params=_cp(("arbitrary",)))(x2, tgt, w)


def _ret_tables():
    C = RET_CHUNK
    h = jnp.arange(RET_HEADS, dtype=F32)
    log_gamma = jnp.log1p(-jnp.power(2.0, -5.0 - h))
    idx = jnp.arange(C, dtype=F32)
    diff = idx[:, None] - idx[None, :]
    dm = jnp.where(diff >= 0, jnp.exp(log_gamma[:, None, None] * jnp.maximum(diff, 0.0)), 0.0)
    dm = dm.reshape(4, 2 * C, C)
    lane_head = jnp.repeat(jnp.arange(RET_HEADS).reshape(4, 2), 64, axis=1)
    lg = log_gamma[lane_head]
    xi = jnp.exp(lg[:, None, :] * (idx[None, :, None] + 1.0))
    zeta = jnp.exp(lg[:, None, :] * (C - 1.0 - idx[None, :, None]))
    blk = (jnp.arange(128)[:, None] // 64) == (jnp.arange(128)[None, :] // 64)
    cd = jnp.where(blk[None], jnp.exp(lg * C)[:, :, None], 0.0)
    return dm.astype(F32), xi.astype(F32), zeta.astype(F32), cd.astype(F32)


def _ret_specs(tb, rev, nt):
    def tmap(t):
        return (nt - 1 - t) if rev else t
    qkv = [pl.BlockSpec((tb, 128), lambda p, t, o=o: (tmap(t), o + p)) for o in (0, 4, 8)]
    rope = [pl.BlockSpec((tb, 128), lambda p, t: (tmap(t), 0))] * 2
    tabs = [pl.BlockSpec((None, 256, 128), lambda p, t: (p, 0, 0))] + \
           [pl.BlockSpec((None, 128, 128), lambda p, t: (p, 0, 0))] * 3
    return qkv, rope, tabs


def _ret_fwd(proj, cos, ss, tabs, gnw, *, name):
    T = proj.shape[0]
    tb = min(T, 1024)
    nt = T // tb
    nchunk = tb // RET_CHUNK

    def body(q_ref, k_ref, v_ref, g_ref, cos_ref, ss_ref, dm_ref, xi_ref, zt_ref, cd_ref, gnw_ref,
             y_ref, o_ref, r_sc):
        @pl.when(pl.program_id(1) == 0)
        def _():
            r_sc[...] = jnp.zeros_like(r_sc)
        m0, m1 = _head_masks((128, 128))
        dm, xi, zt, cd = dm_ref[...], xi_ref[...], zt_ref[...], cd_ref[...]
        bm = (cd > 0).astype(F32)
        gnw = gnw_ref[...]
        for c in range(nchunk):
            rs = pl.ds(c * RET_CHUNK, RET_CHUNK)
            cs, sn = cos_ref[rs, :], ss_ref[rs, :]
            q = _rope(q_ref[rs, :], cs, sn, 32, 64)
            k = _rope(k_ref[rs, :], cs, sn, 32, 64) * K_SCALE
            v = v_ref[rs, :]
            kb, vb = k.astype(BF16), v.astype(BF16)
            qs = jnp.concatenate([q * m0, q * m1], axis=0).astype(BF16)
            s = (_dot_nt(qs, kb) * dm).astype(BF16)
            vs = jnp.concatenate([v * m0, v * m1], axis=0).astype(BF16)
            o = _dot(jnp.concatenate([s[:128], s[128:]], axis=1), vs)
            r = r_sc[...]
            o = o + _dot(q.astype(BF16), r.astype(BF16)) * xi
            r_sc[...] = cd * r + bm * _dot_tn((k * zt).astype(BF16), vb)
            mu = (jnp.sum(o * m0, axis=1, keepdims=True) * m0 + jnp.sum(o * m1, axis=1, keepdims=True) * m1) * (1.0 / 64)
            d = o - mu
            dd = d * d
            var = (jnp.sum(dd * m0, axis=1, keepdims=True) * m0 + jnp.sum(dd * m1, axis=1, keepdims=True) * m1) * (1.0 / 64)
            oh = d * lax.rsqrt(var + EPS)
            g = g_ref[rs, :]
            y_ref[rs, :] = (g * _sigmoid(g) * (oh * gnw)).astype(BF16)
            o_ref[rs, :] = o

    qkv, rope, tspec = _ret_specs(tb, False, nt)
    gspec = pl.BlockSpec((tb, 128), lambda p, t: (t, 12 + p))
    out = pl.BlockSpec((tb, 128), lambda p, t: (t, p))
    return pl.pallas_call(
        body, name=name, grid=(4, nt),
        in_specs=qkv + [gspec] + rope + tspec + [pl.BlockSpec((1, 128), lambda p, t: (0, p))],
        out_specs=[out, out],
        out_shape=[jax.ShapeDtypeStruct((T, RET_WIDTH), BF16), jax.ShapeDtypeStruct((T, RET_WIDTH), F32)],
        scratch_shapes=[pltpu.VMEM((128, 128), F32)],
        compiler_params=_cp(("parallel", "arbitrary")))(proj, proj, proj, proj, cos, ss, *tabs, gnw)


def _ret_bwd_dq(proj, do, cos, ss, tabs, *, name):
    T = proj.shape[0]
    tb = min(T, 1024)
    nt = T // tb
    nchunk = tb // RET_CHUNK

    def body(q_ref, k_ref, v_ref, do_ref, cos_ref, ss_ref, dm_ref, xi_ref, zt_ref, cd_ref, dq_ref, r_sc):
        del q_ref
        @pl.when(pl.program_id(1) == 0)
        def _():
            r_sc[...] = jnp.zeros_like(r_sc)
        m0, m1 = _head_masks((128, 128))
        dm, xi, zt, cd = dm_ref[...], xi_ref[...], zt_ref[...], cd_ref[...]
        bm = (cd > 0).astype(F32)
        for c in range(nchunk):
            rs = pl.ds(c * RET_CHUNK, RET_CHUNK)
            cs, sn = cos_ref[rs, :], ss_ref[rs, :]
            k = _rope(k_ref[rs, :], cs, sn, 32, 64) * K_SCALE
            vb = v_ref[rs, :].astype(BF16)
            dob = do_ref[rs, :]
            dof = dob.astype(F32)
            dos = jnp.concatenate([dof * m0, dof * m1], axis=0).astype(BF16)
            a = (_dot_nt(dos, vb) * dm).astype(BF16)
            ks = jnp.concatenate([k * m0, k * m1], axis=0).astype(BF16)
            r = r_sc[...]
            dq = _dot(jnp.concatenate([a[:128], a[128:]], axis=1), ks) + _dot_nt(dob, r.astype(BF16)) * xi
            r_sc[...] = cd * r + bm * _dot_tn((k * zt).astype(BF16), vb)
            dq_ref[rs, :] = _rope_t(dq, cs, sn, 32, 64).astype(BF16)

    qkv, rope, tspec = _ret_specs(tb, False, nt)
    blk = pl.BlockSpec((tb, 128), lambda p, t: (t, p))
    return pl.pallas_call(
        body, name=name, grid=(4, nt), in_specs=qkv + [blk] + rope + tspec, out_specs=blk,
        out_shape=jax.ShapeDtypeStruct((T, RET_WIDTH), BF16),
        scratch_shapes=[pltpu.VMEM((128, 128), F32)],
        compiler_params=_cp(("parallel", "arbitrary")))(proj, proj, proj, do, cos, ss, *tabs)


def _ret_bwd_dkv(proj, do, cos, ss, tabs, *, name):
    T = proj.shape[0]
    tb = min(T, 1024)
    nt = T // tb
    nchunk = tb // RET_CHUNK

    def body(q_ref, k_ref, v_ref, do_ref, cos_ref, ss_ref, dm_ref, xi_ref, zt_ref, cd_ref, dk_ref, dv_ref, u_sc):
        @pl.when(pl.program_id(1) == 0)
        def _():
            u_sc[...] = jnp.zeros_like(u_sc)
        m0, m1 = _head_masks((128, 128))
        dm, xi, zt, cd = dm_ref[...], xi_ref[...], zt_ref[...], cd_ref[...]
        bm = (cd > 0).astype(F32)
        for c in reversed(range(nchunk)):
            rs = pl.ds(c * RET_CHUNK, RET_CHUNK)
            cs, sn = cos_ref[rs, :], ss_ref[rs, :]
            q = _rope(q_ref[rs, :], cs, sn, 32, 64)
            k = _rope(k_ref[rs, :], cs, sn, 32, 64) * K_SCALE
            kb = k.astype(BF16)
            vb = v_ref[rs, :].astype(BF16)
            dob = do_ref[rs, :]
            dof = dob.astype(F32)
            qs = jnp.concatenate([q * m0, q * m1], axis=0).astype(BF16)
            dos = jnp.concatenate([dof * m0, dof * m1], axis=0).astype(BF16)
            s = (_dot_nt(qs, kb) * dm).astype(BF16)
            a = (_dot_nt(dos, vb) * dm).astype(BF16)
            ub = u_sc[...].astype(BF16)
            dk = _dot_tn(a, qs) + _dot_nt(vb, ub) * zt
            dv = _dot_tn(s, dos) + _dot(kb, ub) * zt
            u_sc[...] = cd * u_sc[...] + bm * _dot_tn((q * xi).astype(BF16), dob)
            dk_ref[rs, :] = (_rope_t(dk, cs, sn, 32, 64) * K_SCALE).astype(BF16)
            dv_ref[rs, :] = dv.astype(BF16)

    qkv, rope, tspec = _ret_specs(tb, True, nt)
    blk = pl.BlockSpec((tb, 128), lambda p, t: (nt - 1 - t, p))
    return pl.pallas_call(
        body, name=name, grid=(4, nt), in_specs=qkv + [blk] + rope + tspec, out_specs=[blk, blk],
        out_shape=[jax.ShapeDtypeStruct((T, RET_WIDTH), BF16)] * 2,
        scratch_shapes=[pltpu.VMEM((128, 128), F32)],
        compiler_params=_cp(("parallel", "arbitrary")))(proj, proj, proj, do, cos, ss, *tabs)


def _mix_bwd(dmixed, o_ret, proj, y_mla, gnw, *, name):
    T = dmixed.shape[0]
    tm = min(T, 512)

    def body(dm_ref, o_ref, g_ref, ym_ref, gnw_ref, do_ref, dg_ref, dom_ref, dl_ref, dw_ref):
        @pl.when(pl.program_id(0) == 0)
        def _():
            dw_ref[...] = jnp.zeros_like(dw_ref)
        m0, m1 = _head_masks((tm, 128))
        lane = lax.broadcasted_iota(jnp.int32, (tm, 128), 1)
        delta = jnp.zeros((tm, 128), F32)

        def gsum(z):
            return jnp.sum(z * m0, axis=1, keepdims=True) * m0 + jnp.sum(z * m1, axis=1, keepdims=True) * m1

        for p in range(4):
            cs = slice(128 * p, 128 * p + 128)
            dy = dm_ref[:, cs]
            o = o_ref[:, cs]
            g = g_ref[:, cs]
            w = gnw_ref[:, cs]
            d = o - gsum(o) * (1.0 / 64)
            rstd = lax.rsqrt(gsum(d * d) * (1.0 / 64) + EPS)
            oh = d * rstd
            sg = _sigmoid(g)
            dn = dy * (g * sg)
            dg_ref[:, cs] = (dy * (oh * w) * (sg * (1.0 + g * (1.0 - sg)))).astype(BF16)
            dw_ref[:, cs] += jnp.sum(dn * oh, axis=0, keepdims=True)
            doh = dn * w
            do = rstd * (doh - gsum(doh) * (1.0 / 64) - oh * (gsum(doh * oh) * (1.0 / 64)))
            do_ref[:, cs] = do.astype(BF16)
            dom = dm_ref[:, 512 + 128 * p:512 + 128 * p + 128]
            dom_ref[:, cs] = dom.astype(BF16)
            pr = dom * ym_ref[:, cs].astype(F32)
            delta = jnp.where(lane == 2 * p, jnp.sum(pr * m0, axis=1, keepdims=True), delta)
            delta = jnp.where(lane == 2 * p + 1, jnp.sum(pr * m1, axis=1, keepdims=True), delta)
        dl_ref[...] = delta

    half = pl.BlockSpec((tm, 512), lambda i: (i, 0))
    return pl.pallas_call(
        body, name=name, grid=(T // tm,),
        in_specs=[pl.BlockSpec((tm, 1024), lambda i: (i, 0)), half, pl.BlockSpec((tm, 512), lambda i: (i, 3)),
                  half, pl.BlockSpec((1, 512), lambda i: (0, 0))],
        out_specs=[half, half, half, pl.BlockSpec((tm, 128), lambda i: (i, 0)), pl.BlockSpec((1, 512), lambda i: (0, 0))],
        out_shape=[jax.ShapeDtypeStruct((T, 512), BF16)] * 3 + [jax.ShapeDtypeStruct((T, 128), F32),
                                                                jax.ShapeDtypeStruct((1, 512), F32)],
        compiler_params=_cp(("arbitrary",)))(dmixed, o_ret, proj, y_mla, gnw)


def _mla_prep_fwd(proj, qnw, kvnw, wuq, wk, wv, cos, ss, *, name):
    T = proj.shape[0]
    tm = min(T, 512)

    def body(lat_ref, qnw_ref, kvnw_ref, wuq_ref, wk_ref, wv_ref, cos_ref, ss_ref,
             q_ref, k_ref, v_ref, cqn_ref, ckvn_ref):
        cq = lat_ref[:, 0:256]
        ckv = lat_ref[:, 256:384]
        g3 = lat_ref[:, 384:512]
        cqn = (cq * lax.rsqrt(jnp.mean(cq * cq, axis=-1, keepdims=True) + EPS) * qnw_ref[...]).astype(BF16)
        ckvn = (ckv * lax.rsqrt(jnp.mean(ckv * ckv, axis=-1, keepdims=True) + EPS) * kvnw_ref[...]).astype(BF16)
        cqn_ref[...] = cqn
        ckvn_ref[...] = ckvn
        cs, sn = cos_ref[...], ss_ref[...]
        q = _dot_nt(cqn, wuq_ref[...])
        k = _dot_nt(ckvn, wk_ref[...])
        kpe = _rope(g3, cs, sn, 16, 32)
        for h in range(MLA_HEADS):
            hs = slice(128 * h, 128 * h + 128)
            q_ref[:, hs] = (_rope(q[:, hs], cs, sn, 16, 32) * SCALE).astype(BF16)
            k_ref[:, hs] = (k[:, hs] + kpe).astype(BF16)
        v_ref[...] = _dot_nt(ckvn, wv_ref[...]).astype(BF16)

    def full(shape):
        return pl.BlockSpec(shape, lambda i: (0, 0))

    def row(w):
        return pl.BlockSpec((tm, w), lambda i: (i, 0))

    return pl.pallas_call(
        body, name=name, grid=(T // tm,),
        in_specs=[pl.BlockSpec((tm, 512), lambda i: (i, 4)), full((1, 256)), full((1, 128)), full((1024, 256)),
                  full((1024, 128)), full((512, 128)), row(128), row(128)],
        out_specs=[row(1024), row(1024), row(512), row(256), row(128)],
        out_shape=[jax.ShapeDtypeStruct((T, 1024), BF16), jax.ShapeDtypeStruct((T, 1024), BF16),
                   jax.ShapeDtypeStruct((T, 512), BF16), jax.ShapeDtypeStruct((T, 256), BF16),
                   jax.ShapeDtypeStruct((T, 128), BF16)],
        compiler_params=_cp(("parallel",)))(proj, qnw, kvnw, wuq, wk, wv, cos, ss)


def _mla_prep_bwd(dq, dk, dv, proj, qnw, kvnw, wuq_t, wk_t, wv_t, cos, ss, *, name):
    T = proj.shape[0]
    tm = min(T, 512)

    def body(dq_ref, dk_ref, dv_ref, lat_ref, qnw_ref, kvnw_ref, wuq_ref, wk_ref, wv_ref, cos_ref, ss_ref,
             dlat_ref, dqp_ref, dqnw_ref, dkvnw_ref):
        @pl.when(pl.program_id(0) == 0)
        def _():
            dqnw_ref[...] = jnp.zeros_like(dqnw_ref)
            dkvnw_ref[...] = jnp.zeros_like(dkvnw_ref)
        cs, sn = cos_ref[...], ss_ref[...]
        dkpe = jnp.zeros((tm, 128), F32)
        for h in range(MLA_HEADS):
            hs = slice(128 * h, 128 * h + 128)
            dqp_ref[:, hs] = _rope_t(dq_ref[:, hs] * SCALE, cs, sn, 16, 32).astype(BF16)
            dkpe = dkpe + dk_ref[:, hs]
        lane = lax.broadcasted_iota(jnp.int32, (tm, 128), 1)
        rope_lane = (lane >= MLA_NOPE) & (lane < MLA_NOPE + MLA_ROPE)
        dg3 = jnp.where(rope_lane, _rope_t(jnp.where(rope_lane, dkpe, 0.0), cs, sn, 16, 32), 0.0)

        def norm_bwd(x, w, dn):
            r = lax.rsqrt(jnp.mean(x * x, axis=-1, keepdims=True) + EPS)
            xh = x * r
            g = dn * w
            return r * (g - xh * jnp.mean(g * xh, axis=-1, keepdims=True)), jnp.sum(dn * xh, axis=0, keepdims=True)

        dcqn = _dot(dqp_ref[...], wuq_ref[...])
        dcq, dqnw = norm_bwd(lat_ref[:, 0:256], qnw_ref[...], dcqn)
        dckvn = _dot(dk_ref[...].astype(BF16), wk_ref[...]) + _dot(dv_ref[...], wv_ref[...])
        dckv, dkvnw = norm_bwd(lat_ref[:, 256:384], kvnw_ref[...], dckvn)
        dqnw_ref[...] += dqnw
        dkvnw_ref[...] += dkvnw
        dlat_ref[:, 0:256] = dcq.astype(BF16)
        dlat_ref[:, 256:384] = dckv.astype(BF16)
        dlat_ref[:, 384:512] = dg3.astype(BF16)

    def full(shape):
        return pl.BlockSpec(shape, lambda i: (0, 0))

    def row(w):
        return pl.BlockSpec((tm, w), lambda i: (i, 0))

    return pl.pallas_call(
        body, name=name, grid=(T // tm,),
        in_specs=[row(1024), row(1024), row(512), pl.BlockSpec((tm, 512), lambda i: (i, 4)), full((1, 256)),
                  full((1, 128)), full((1024, 256)), full((1024, 128)), full((512, 128)), row(128), row(128)],
        out_specs=[row(512), row(1024), full((1, 256)), full((1, 128))],
        out_shape=[jax.ShapeDtypeStruct((T, 512), BF16), jax.ShapeDtypeStruct((T, 1024), BF16),
                   jax.ShapeDtypeStruct((1, 256), F32), jax.ShapeDtypeStruct((1, 128), F32)],
        compiler_params=_cp(("arbitrary",)))(dq, dk, dv, proj, qnw, kvnw, wuq_t, wk_t, wv_t, cos, ss)


def _flash_fwd(q, k, v1, *, name, gather=None):
    T = q.shape[0]
    tq = min(T, 512)
    tk = tq
    nq = T // tq

    def body(q_ref, k_ref, v_ref, *rest):
        if gather is None:
            y_ref, lse_ref = rest
        else:
            x_ref, y_ref, lse_ref, g_ref, *sems = rest
            start, forward, finish = _gather_phases(x_ref, g_ref, *sems)
            pl.when((pl.program_id(0) == 0) & (pl.program_id(1) == 0))(start)
            pl.when((pl.program_id(0) == 1) & (pl.program_id(1) == 0))(forward)
        attend(q_ref, k_ref, v_ref, y_ref, lse_ref)
        if gather is not None:
            pl.when((pl.program_id(0) == 3) & (pl.program_id(1) == nq - 1))(finish)

    def attend(q_ref, k_ref, v_ref, y_ref, lse_ref):
        qi = pl.program_id(1)
        row = lax.broadcasted_iota(jnp.int32, (tq, tk), 0)
        col = lax.broadcasted_iota(jnp.int32, (tq, tk), 1)

        def step(kb, carry, masked):
            ks = pl.ds(pl.multiple_of(kb * tk, tk), tk)
            new = []
            for h in range(2):
                hs = slice(128 * h, 128 * h + 128)
                m, acc = carry[h]
                s = _dot_nt(q_ref[:, hs], k_ref[ks, hs])
                if masked:
                    s = jnp.where(col <= row, s, NEG)
                mn = jnp.maximum(m, jnp.max(s, axis=1, keepdims=True))
                p = jnp.exp((s - mn).astype(BF16))
                acc = jnp.exp(m - mn) * acc + _dot(p, v_ref[ks, hs])
                new.append((mn, acc))
            return tuple(new)

        init = (jnp.full((tq, 1), NEG, F32), jnp.zeros((tq, 128), F32))
        carry = lax.fori_loop(0, qi, lambda kb, c: step(kb, c, False), (init, init))
        (ma, acca), (mb, accb) = step(qi, carry, True)
        lane = lax.broadcasted_iota(jnp.int32, (tq, 128), 1)
        la, lb = pltpu.roll(acca, 64, 1), pltpu.roll(accb, 64, 1)
        y_ref[...] = jnp.where(lane < 64, acca / la, accb / lb).astype(BF16)
        lse_ref[0] = ma + jnp.log(acca[:, 64:65])
        lse_ref[1] = mb + jnp.log(accb[:, 0:1])

    in_specs = [pl.BlockSpec((tq, 256), lambda p, i: (i, p)), pl.BlockSpec((T, 256), lambda p, i: (0, p)),
                pl.BlockSpec((T, 256), lambda p, i: (0, p))]
    out_specs = [pl.BlockSpec((tq, 128), lambda p, i: (i, p)), pl.BlockSpec((2, tq, 1), lambda p, i: (p, i, 0))]
    out_shape = [jax.ShapeDtypeStruct((T, MLA_WIDTH), BF16), jax.ShapeDtypeStruct((MLA_HEADS, T, 1), F32)]
    if gather is None:
        return pl.pallas_call(body, name=name, grid=(4, nq), in_specs=in_specs, out_specs=out_specs,
                              out_shape=out_shape, compiler_params=_cp(("parallel", "arbitrary")))(q, k, v1)
    return pl.pallas_call(
        body, name=name, grid=(4, nq), in_specs=in_specs + [ANY], out_specs=out_specs + [ANY],
        out_shape=out_shape + [jax.ShapeDtypeStruct((N_DEV,) + gather.shape, gather.dtype)],
        scratch_shapes=list(GATHER_SCRATCH),
        compiler_params=_cp(("arbitrary", "arbitrary")))(q, k, v1, gather)


def _flash_bwd(q, k, v, do, lse, delta, *, name, exchange=None):
    T = q.shape[0]
    tq = min(T, 512)
    tk = tq
    nq = T // tq

    def body(q_ref, k_ref, v_ref, do_ref, lse_ref, dl_ref, *rest):
        if exchange is None:
            backward(q_ref, k_ref, v_ref, do_ref, lse_ref, dl_ref, *rest)
        else:
            p_ref, dqt_ref, dk_ref, dv_ref, got_ref, *sems = rest
            start, finish = _exchange_phases(p_ref, got_ref, *sems)
            pl.when((pl.program_id(0) == 0) & (pl.program_id(1) == 0))(start)
            backward(q_ref, k_ref, v_ref, do_ref, lse_ref, dl_ref, dqt_ref, dk_ref, dv_ref)
            pl.when((pl.program_id(0) == 3) & (pl.program_id(1) == nq - 1))(finish)

    def backward(q_ref, k_ref, v_ref, do_ref, lse_ref, dl_ref, dqt_ref, dk_ref, dv_ref):
        kb = pl.program_id(1)

        @pl.when(kb == 0)
        def _():
            dqt_ref[...] = jnp.zeros_like(dqt_ref)
        krow = lax.broadcasted_iota(jnp.int32, (tk, tq), 0)
        qcol = lax.broadcasted_iota(jnp.int32, (tk, tq), 1)
        masks = _head_masks((tk, 128))
        vf = v_ref[...].astype(F32)
        vms = [(vf * masks[h]).astype(BF16) for h in range(2)]

        def step(qi, carry, masked):
            qs = pl.ds(pl.multiple_of(qi * tq, tq), tq)
            dob = do_ref[qs, :]
            dof = dob.astype(F32)
            dks, dv_acc = list(carry[:2]), carry[2]
            for h in range(2):
                hs = slice(128 * h, 128 * h + 128)
                kh = k_ref[:, hs]
                qh = q_ref[qs, hs]
                st = _dot_nt(kh, qh)
                pt = jnp.exp((st - lse_ref[h, qi]).astype(BF16))
                if masked:
                    pt = jnp.where(krow <= qcol, pt, jnp.zeros_like(pt))
                dv_acc = dv_acc + _dot(pt, (dof * masks[h]).astype(BF16))
                dpt = _dot_nt(vms[h], dob)
                dst = pt * (dpt - dl_ref[h, qi]).astype(BF16)
                dks[h] = dks[h] + _dot(dst, qh)
                dqt_ref[qi, hs, :] += _dot_tn(kh, dst)
            return dks[0], dks[1], dv_acc

        zero = jnp.zeros((tk, 128), F32)
        carry = step(kb, (zero, zero, zero), True)
        dk0, dk1, dv_acc = lax.fori_loop(kb + 1, nq, lambda qi, c: step(qi, c, False), carry)
        dk_ref[:, 0:128] = dk0
        dk_ref[:, 128:256] = dk1
        dv_ref[...] = dv_acc.astype(BF16)

    stat = pl.BlockSpec((2, nq, 1, tq), lambda p, j: (p, 0, 0, 0))
    in_specs = [pl.BlockSpec((T, 256), lambda p, j: (0, p)), pl.BlockSpec((tk, 256), lambda p, j: (j, p)),
                pl.BlockSpec((tk, 128), lambda p, j: (j, p)), pl.BlockSpec((T, 128), lambda p, j: (0, p)), stat, stat]
    out_specs = [pl.BlockSpec((None, nq, 256, tq), lambda p, j: (p, 0, 0, 0)),
                 pl.BlockSpec((tk, 256), lambda p, j: (j, p)), pl.BlockSpec((tk, 128), lambda p, j: (j, p))]
    out_shape = [jax.ShapeDtypeStruct((4, nq, 256, tq), F32), jax.ShapeDtypeStruct((T, 1024), F32),
                 jax.ShapeDtypeStruct((T, MLA_WIDTH), BF16)]
    if exchange is None:
        return pl.pallas_call(body, name=name, grid=(4, nq), in_specs=in_specs, out_specs=out_specs,
                              out_shape=out_shape,
                              compiler_params=_cp(("parallel", "arbitrary")))(q, k, v, do, lse, delta)
    return pl.pallas_call(
        body, name=name, grid=(4, nq), in_specs=in_specs + [ANY], out_specs=out_specs + [ANY],
        out_shape=out_shape + [jax.ShapeDtypeStruct(exchange.shape, exchange.dtype)],
        scratch_shapes=list(EXCHANGE_SCRATCH),
        compiler_params=_cp(("arbitrary", "arbitrary")))(q, k, v, do, lse, delta, exchange)


def _shift_down(x, n, prev8):
    r = pltpu.roll(x, n, 0)
    row = lax.broadcasted_iota(jnp.int32, prev8.shape, 0)
    first = jnp.where(row < n, pltpu.roll(prev8, n, 0), r[:8])
    if x.shape[0] == 8:
        return first
    return jnp.concatenate([first, r[8:]], axis=0)


def _shift_up(x, n, next8):
    tm = x.shape[0]
    r = pltpu.roll(x, tm - n, 0)
    row = lax.broadcasted_iota(jnp.int32, next8.shape, 0)
    last = jnp.where(row >= 8 - n, pltpu.roll(next8, 8 - n, 0), r[tm - 8:])
    return jnp.concatenate([r[:tm - 8], last], axis=0)


def _conv_pre(u, prev8, cw_ref, cb_ref):
    p1 = _shift_down(u, 1, prev8)
    p2 = _shift_down(u, 2, prev8)
    up = cb_ref[...] + cw_ref[0:1, :] * p2 + cw_ref[1:2, :] * p1 + cw_ref[2:3, :] * u
    return up, p1, p2


def _conv_fwd(u, cw, cb, *, name):
    T = u.shape[0]
    tm = min(T, 512)
    W = 2 * FF_HALF

    def body(u_ref, prev_ref, cw_ref, cb_ref, a_ref):
        prev = jnp.where(pl.program_id(0) > 0, prev_ref[...], 0.0)
        up, _, _ = _conv_pre(u_ref[...], prev, cw_ref, cb_ref)
        gate = up[:, :FF_HALF]
        a_ref[...] = (gate * _sigmoid(gate) * up[:, FF_HALF:]).astype(BF16)

    return pl.pallas_call(
        body, name=name, grid=(T // tm, 2),
        in_specs=[pl.BlockSpec((tm, W), lambda i, j: (i, j)),
                  pl.BlockSpec((8, W), lambda i, j: (jnp.maximum(i * (tm // 8) - 1, 0), j)),
                  pl.BlockSpec((3, W), lambda i, j: (0, j)), pl.BlockSpec((1, W), lambda i, j: (0, j))],
        out_specs=pl.BlockSpec((tm, FF_HALF), lambda i, j: (i, j)),
        out_shape=jax.ShapeDtypeStruct((T, D_FF), BF16),
        compiler_params=_cp(("parallel", "parallel")))(u, u, cw, cb)


def _conv_bwd(u, da, cw, cb, *, name):
    T = u.shape[0]
    tm = min(T, 512)
    W = 2 * FF_HALF
    nt = T // tm

    def body(u_ref, prev_ref, next_ref, da_ref, dan_ref, cw_ref, cb_ref, du_ref, dw0_ref, dw1_ref, dw2_ref, db_ref):
        i = pl.program_id(1)

        @pl.when(i == 0)
        def _():
            for r in (dw0_ref, dw1_ref, dw2_ref, db_ref):
                r[...] = jnp.zeros_like(r)

        def dpre(u, prev8, da):
            up, p1, p2 = _conv_pre(u, prev8, cw_ref, cb_ref)
            gate, val = up[:, :FF_HALF], up[:, FF_HALF:]
            sg = _sigmoid(gate)
            dgate = da * val * (sg * (1.0 + gate * (1.0 - sg)))
            dval = da * (gate * sg)
            return jnp.concatenate([dgate, dval], axis=1), p1, p2

        u = u_ref[...]
        prev = jnp.where(i > 0, prev_ref[...], 0.0)
        dup, p1, p2 = dpre(u, prev, da_ref[...])
        dupn, _, _ = dpre(next_ref[...], u[tm - 8:], dan_ref[...])
        dupn = jnp.where(i < nt - 1, dupn, 0.0)
        du = cw_ref[2:3, :] * dup + cw_ref[1:2, :] * _shift_up(dup, 1, dupn) + cw_ref[0:1, :] * _shift_up(dup, 2, dupn)
        du_ref[...] = du.astype(BF16)
        dw0_ref[...] += jnp.sum(dup * p2, axis=0, keepdims=True)
        dw1_ref[...] += jnp.sum(dup * p1, axis=0, keepdims=True)
        dw2_ref[...] += jnp.sum(dup * u, axis=0, keepdims=True)
        db_ref[...] += jnp.sum(dup, axis=0, keepdims=True)

    nxt = lambda j, i: (jnp.minimum((i + 1) * (tm // 8), T // 8 - 1), j)
    vec = pl.BlockSpec((1, W), lambda j, i: (0, j))
    return pl.pallas_call(
        body, name=name, grid=(2, nt),
        in_specs=[pl.BlockSpec((tm, W), lambda j, i: (i, j)),
                  pl.BlockSpec((8, W), lambda j, i: (jnp.maximum(i * (tm // 8) - 1, 0), j)),
                  pl.BlockSpec((8, W), nxt),
                  pl.BlockSpec((tm, FF_HALF), lambda j, i: (i, j)), pl.BlockSpec((8, FF_HALF), nxt),
                  pl.BlockSpec((3, W), lambda j, i: (0, j)), vec],
        out_specs=[pl.BlockSpec((tm, W), lambda j, i: (i, j)), vec, vec, vec, vec],
        out_shape=[jax.ShapeDtypeStruct((T, 2 * D_FF), BF16)] + [jax.ShapeDtypeStruct((1, 2 * D_FF), F32)] * 4,
        compiler_params=_cp(("parallel", "arbitrary")))(u, u, u, da, da, cw, cb)


def _adamw(w, m, v, g_slots, *, name):
    R, C = w.shape
    ns = g_slots.shape[0]
    tr = _row_tile(R)

    def body(w_ref, m_ref, v_ref, g_ref, go_ref, d_ref, mo_ref, vo_ref):
        g = g_ref[0].astype(F32)
        for s in range(1, ns):
            g = g + g_ref[s].astype(F32)
        mn = ADAM_B1 * m_ref[...] + (1.0 - ADAM_B1) * g
        vn = ADAM_B2 * v_ref[...] + (1.0 - ADAM_B2) * (g * g)
        m_hat = mn / (1.0 - ADAM_B1 ** ADAM_STEP)
        v_hat = vn / (1.0 - ADAM_B2 ** ADAM_STEP)
        go_ref[...] = g
        d_ref[...] = -ADAM_LR * (m_hat / (jnp.sqrt(v_hat) + ADAM_EPS) + ADAM_WD * w_ref[...])
        mo_ref[...] = mn
        vo_ref[...] = vn

    blk = pl.BlockSpec((tr, C), lambda i: (i, 0))
    return pl.pallas_call(
        body, name=name, grid=(R // tr,),
        in_specs=[blk, blk, blk, pl.BlockSpec((ns, tr, C), lambda i: (0, i, 0))],
        out_specs=[blk] * 4, out_shape=[jax.ShapeDtypeStruct((R, C), F32)] * 4,
        compiler_params=_cp(("parallel",)))(w, m, v, g_slots)


def _place():
    return lax.axis_index("x"), lax.axis_index("y"), lax.axis_index("c")


GATHER_SCRATCH = (pltpu.SemaphoreType.DMA((7,)), pltpu.SemaphoreType.DMA((7,)), pltpu.SemaphoreType.DMA)
EXCHANGE_SCRATCH = (pltpu.SemaphoreType.DMA((3,)), pltpu.SemaphoreType.DMA((3,)), pltpu.SemaphoreType.DMA)


def _gather_phases(x_ref, out_ref, send_sems, recv_sems, local_sem):
    x_, y_, c_ = _place()
    me, sibling = (x_, y_, c_), (x_, y_, 1 - c_)
    chips = [(1 - x_, y_), (x_, 1 - y_), (1 - x_, 1 - y_)]

    def slot(px, py, pc):
        return out_ref.at[4 * px + 2 * py + pc]

    def copy(k, block, to, src=None):
        return pltpu.make_async_remote_copy(
            src_ref=slot(*block) if src is None else src, dst_ref=slot(*block),
            send_sem=send_sems.at[k], recv_sem=recv_sems.at[k], device_id=to, device_id_type=MESH)

    def mine():
        return pltpu.make_async_copy(x_ref, slot(*me), local_sem)

    def first():
        return [copy(0, me, sibling, src=x_ref)] + [copy(1 + j, me, (*chip, c_), src=x_ref)
                                                     for j, chip in enumerate(chips)]

    def passed():
        return [copy(4 + j, (*chip, c_), sibling) for j, chip in enumerate(chips)]

    def start():
        mine().start()
        for cp in first():
            cp.start()

    def forward():
        fwd = passed()
        for j, chip in enumerate(chips):
            copy(1 + j, (*chip, c_), me).wait_recv()
            fwd[j].start()

    def finish():
        copy(0, sibling, me).wait_recv()
        for j, chip in enumerate(chips):
            copy(4 + j, (*chip, 1 - c_), me).wait_recv()
        for cp in first() + passed():
            cp.wait_send()
        mine().wait()

    return start, forward, finish


def _exchange_phases(p_ref, out_ref, send_sems, recv_sems, local_sem):
    x_, y_, c_ = _place()
    me_k = 2 * x_ + y_
    chips = [(1 - x_, y_), (x_, 1 - y_), (1 - x_, 1 - y_)]

    def local():
        return pltpu.make_async_copy(p_ref.at[me_k], out_ref.at[me_k], local_sem)

    def copy(j, src_k, dst_k, chip):
        return pltpu.make_async_remote_copy(
            src_ref=p_ref.at[src_k], dst_ref=out_ref.at[dst_k], send_sem=send_sems.at[j],
            recv_sem=recv_sems.at[j], device_id=(*chip, c_), device_id_type=MESH)

    def sends():
        return [copy(j, 2 * px + py, me_k, (px, py)) for j, (px, py) in enumerate(chips)]

    def start():
        local().start()
        for cp in sends():
            cp.start()

    def finish():
        for j, (px, py) in enumerate(chips):
            copy(j, me_k, 2 * px + py, (px, py)).wait_recv()
        for cp in sends():
            cp.wait_send()
        local().wait()

    return start, finish


def _all_gather(x, *, name, in_vmem):
    def body(x_ref, out_ref, send_sems, recv_sems, local_sem):
        for phase in _gather_phases(x_ref, out_ref, send_sems, recv_sems, local_sem):
            phase()

    spec = pl.BlockSpec(memory_space=pltpu.VMEM) if in_vmem else ANY
    return pl.pallas_call(
        body, name=name, out_shape=jax.ShapeDtypeStruct((N_DEV,) + x.shape, x.dtype),
        in_specs=[spec], out_specs=spec, scratch_shapes=list(GATHER_SCRATCH),
        compiler_params=pltpu.CompilerParams(vmem_limit_bytes=VMEM_LIMIT))(x)


def _sum_slots(g, *, name):
    n = g.shape[0]

    def body(g_ref, o_ref):
        acc = g_ref[0]
        for s in range(1, n):
            acc = acc + g_ref[s]
        o_ref[...] = acc

    return pl.pallas_call(body, name=name, out_shape=jax.ShapeDtypeStruct(g.shape[1:], g.dtype))(g)


def _swap_sibling(x, *, name):
    def body(x_ref, out_ref, send_sem, recv_sem):
        x_, y_, c_ = _place()
        cp = pltpu.make_async_remote_copy(src_ref=x_ref, dst_ref=out_ref, send_sem=send_sem, recv_sem=recv_sem,
                                          device_id=(x_, y_, 1 - c_), device_id_type=MESH)
        cp.start()
        cp.wait()

    return pl.pallas_call(
        body, name=name, out_shape=jax.ShapeDtypeStruct(x.shape, x.dtype), in_specs=[ANY], out_specs=ANY,
        scratch_shapes=[pltpu.SemaphoreType.DMA, pltpu.SemaphoreType.DMA])(x)


def _exchange_chips(p, *, name):
    def body(p_ref, out_ref, send_sems, recv_sems, local_sem):
        for phase in _exchange_phases(p_ref, out_ref, send_sems, recv_sems, local_sem):
            phase()

    return pl.pallas_call(
        body, name=name, out_shape=jax.ShapeDtypeStruct(p.shape, p.dtype), in_specs=[ANY], out_specs=ANY,
        scratch_shapes=list(EXCHANGE_SCRATCH))(p)


def _row_tile(R):
    for cand in (256, 400, 200):
        if R % cand == 0:
            return cand
    return R


def _add2(a, b, *, name, out_dtype):
    n, R, C = a.shape
    tr = _row_tile(R)

    def body(a_ref, b_ref, o_ref):
        o_ref[...] = (a_ref[...] + b_ref[...]).astype(out_dtype)

    blk = pl.BlockSpec((1, tr, C), lambda s, i: (s, i, 0))
    return pl.pallas_call(body, name=name, grid=(n, R // tr), in_specs=[blk, blk], out_specs=blk,
                          out_shape=jax.ShapeDtypeStruct(a.shape, out_dtype),
                          compiler_params=_cp(("parallel", "parallel")))(a, b)


def _pack_local(parts, group):
    table, rows = group
    segs = []
    for n, r, rp, tr in table:
        w = parts[n].T if tr else parts[n]
        segs.append(jnp.pad(w.reshape(r, PACK_COLS), ((0, rp - r), (0, 0))))
    segs.append(jnp.zeros((rows - sum(rp for _, _, rp, _ in table), PACK_COLS), segs[0].dtype))
    return jnp.concatenate(segs, axis=0)


def _unpack_local(packed, like, group):
    out, off = {}, 0
    for n, r, rp, tr in group[0]:
        rows, cols = like[n].shape
        seg = packed[off:off + r]
        out[n] = (seg.reshape(cols, rows).T if tr else seg)[None]
        off += rp
    return out


def _segments(g, group):
    out, off = {}, 0
    for n, r, rp, _ in group[0]:
        out[n] = g[:, off:off + r]
        off += rp
    return out


def _pack_grads(parts, group):
    table, rows = group
    segs = [jnp.pad(parts[n], ((0, 0), (0, rp - r), (0, 0))) for n, r, rp, _ in table]
    segs.append(jnp.zeros((N_DEV, rows - sum(rp for _, _, rp, _ in table), PACK_COLS), F32))
    return jnp.concatenate(segs, axis=1)


def _owner_rows_early(g):
    g_in = jnp.concatenate([g["w_in_t"][:2432], g["w_in_t"][2496:2528]], axis=0).reshape(N_DEV, 308, PACK_COLS)
    g_uq = g["w_uq_t"].reshape(N_DEV, 128, MLA_Q_RANK)[:, :96].reshape(N_DEV, 24, PACK_COLS)
    g_ukv = jnp.concatenate([g["w_k_t"].reshape(N_DEV, 128, MLA_KV_RANK)[:, :64],
                             g["w_v_t"].reshape(N_DEV, 64, MLA_KV_RANK)], axis=1).reshape(N_DEV, 16, PACK_COLS)
    return dict(w_in=g_in, w_uq=g_uq, w_ukv=g_ukv)


def _owner_rows_late(g):
    up = g["w_up_t"].reshape(N_DEV, 704, PACK_COLS)
    g_up = jnp.stack([up[FF_OWNER_ORDER.index(d)] for d in range(N_DEV)])
    return dict(w_out=g["w_out"].reshape(N_DEV, 128, PACK_COLS), w_up=g_up,
                w_down=g["w_down"].reshape(N_DEV, 352, PACK_COLS))


def _reduce_to_pairs(gp, *, name):
    c_ = lax.axis_index("c")
    gp = gp.reshape(4, 2, gp.shape[1], PACK_COLS)
    mine = lax.dynamic_index_in_dim(gp, c_, axis=1, keepdims=False)
    theirs = lax.dynamic_index_in_dim(gp, 1 - c_, axis=1, keepdims=False)
    return _add2(mine, _swap_sibling(theirs, name=name + "_swap"), out_dtype=BF16, name=name + "_sum")


def _interleave_ff(w):
    g, v = w[..., :D_FF], w[..., D_FF:]
    return jnp.concatenate([g[..., :FF_HALF], v[..., :FF_HALF], g[..., FF_HALF:], v[..., FF_HALF:]], axis=-1)


def _deinterleave_ff(w):
    b = [w[..., i * FF_HALF:(i + 1) * FF_HALF] for i in range(4)]
    return jnp.concatenate([b[0], b[2], b[1], b[3]], axis=-1)


def _rope_tables(pos):
    p = pos.astype(F32)[:, None]
    inv_r = ROPE_BASE ** (-jnp.arange(0, RET_HEAD_DIM, 2, dtype=F32) / RET_HEAD_DIM)
    ang = p * inv_r
    c, s = jnp.cos(ang), jnp.sin(ang)
    cos_r = jnp.concatenate([c, c, c, c], axis=1)
    ss_r = jnp.concatenate([-s, s, -s, s], axis=1)
    inv_m = ROPE_BASE ** (-jnp.arange(0, MLA_ROPE, 2, dtype=F32) / MLA_ROPE)
    ang = p * inv_m
    c, s = jnp.cos(ang), jnp.sin(ang)
    T = pos.shape[0]
    cos_m = jnp.concatenate([jnp.ones((T, 64), F32), c, c, jnp.ones((T, 32), F32)], axis=1)
    ss_m = jnp.concatenate([jnp.zeros((T, 64), F32), -s, s, jnp.zeros((T, 32), F32)], axis=1)
    return cos_r, ss_r, cos_m, ss_m


def _prep_early(gathered):
    seg = _segments(gathered, EARLY)
    w_in_t = seg["w_in"].reshape(IN_WIDTH, D_MODEL)
    z = lambda n: jnp.zeros((n, D_MODEL), BF16)
    w_in_t = jnp.concatenate([w_in_t[:2432], z(64), w_in_t[2432:2464], z(32)], axis=0)
    w_uq_t = jnp.pad(seg["w_uq"].reshape(MLA_HEADS, 96, MLA_Q_RANK), ((0, 0), (0, 32), (0, 0))).reshape(1024, MLA_Q_RANK)
    ukv = seg["w_ukv"].reshape(MLA_HEADS, 128, MLA_KV_RANK)
    w_k_t = jnp.pad(ukv[:, :64], ((0, 0), (0, 64), (0, 0))).reshape(1024, MLA_KV_RANK)
    w_v_t = ukv[:, 64:].reshape(512, MLA_KV_RANK)
    return dict(w_in_t=w_in_t, w_uq_t=w_uq_t, w_k_t=w_k_t, w_v_t=w_v_t)


def _prep_late(gathered):
    seg = _segments(gathered, LATE)
    w_up_t = jnp.concatenate([seg["w_up"][d] for d in FF_OWNER_ORDER], axis=0)
    return dict(w_out=seg["w_out"].reshape(1024, D_MODEL), w_up_t=w_up_t, w_down=seg["w_down"].reshape(D_FF, D_MODEL))


def _local_step(x, pos, tgt, W, sm, late):
    dist = not isinstance(late, dict)
    cos_r, ss_r, cos_m, ss_m = _rope_tables(pos)
    tabs = _ret_tables()

    h = _rmsnorm_fwd(x, sm["attn_norm_w"], name="attn_norm")
    proj = _mm(h, W["w_in_t"], bt=True, name="in_proj")
    y_ret, o_ret = _ret_fwd(proj, cos_r, ss_r, tabs, sm["ret_gn_w"], name="ret_fwd")
    q, k, v, cqn, ckvn = _mla_prep_fwd(proj, sm["mla_q_norm_w"], sm["mla_kv_norm_w"], W["w_uq_t"], W["w_k_t"],
                                       W["w_v_t"], cos_m, ss_m, name="mla_prep")
    T = x.shape[0]
    tq = min(T, 512)
    vp = v.reshape(T, 4, 2, MLA_V)
    one = jnp.ones((T, 4, MLA_V), BF16)
    v1 = jnp.stack([vp[:, :, 0], one, one, vp[:, :, 1]], axis=2).reshape(T, MLA_HEADS * 128)
    if dist:
        y_mla, lse, gathered = _flash_fwd(q, k, v1, gather=late, name="mla_attn")
        W = {**W, **_prep_late(gathered)}
    else:
        y_mla, lse = _flash_fwd(q, k, v1, name="mla_attn")
        W = {**W, **late}
    lse = lse.reshape(MLA_HEADS, T // tq, 1, tq)
    mixed = jnp.concatenate([y_ret, y_mla], axis=1)
    x1 = _mm(mixed, W["w_out"], add=x, name="out_proj")
    h2 = _rmsnorm_fwd(x1, sm["ffn_norm_w"], name="ffn_norm")
    u = _mm(h2, W["w_up_t"], bt=True, name="up_proj")
    a = _conv_fwd(u, sm["conv_w"], sm["conv_b"], name="conv_gate")
    x2 = _mm(a, W["w_down"], add=x1, name="down_proj")
    loss, dx2, d_final = _loss_head(x2, tgt, sm["final_norm_w"], name="loss_head")

    g = {}
    g["w_down"] = _mm_tn(a, dx2, name="dw_down")
    da = _mm(dx2, W["w_down"], bt=True, name="d_act")
    du, dcw0, dcw1, dcw2, dcb = _conv_bwd(u, da, sm["conv_w"], sm["conv_b"], name="conv_bwd")
    g["w_up_t"] = _mm_tn(du, h2, name="dw_up")
    dh2 = _mm(du, W["w_up_t"], name="d_h2")
    dx1, d_ffn = _rmsnorm_bwd(x1, sm["ffn_norm_w"], dh2, dx2, name="ffn_norm_bwd")

    g["w_out"] = _mm_tn(mixed, dx1, name="dw_out")
    dmixed = _mm(dx1, W["w_out"], bt=True, name="d_mixed")
    do_ret, dg, do_mla, delta, d_gn = _mix_bwd(dmixed, o_ret, proj, y_mla, sm["ret_gn_w"], name="mix_bwd")
    drq = _ret_bwd_dq(proj, do_ret, cos_r, ss_r, tabs, name="ret_bwd_dq")
    drk, drv = _ret_bwd_dkv(proj, do_ret, cos_r, ss_r, tabs, name="ret_bwd_dkv")
    delta_r = delta[:, :MLA_HEADS].T.reshape(MLA_HEADS, T // tq, 1, tq)
    if dist:
        pair = _reduce_to_pairs(_pack_grads(_owner_rows_late(g), LATE), name="grad_late")
        dqt, dk, dv, slots_late = _flash_bwd(q, k, v, do_mla, lse, delta_r, exchange=pair, name="mla_attn_bwd")
    else:
        dqt, dk, dv = _flash_bwd(q, k, v, do_mla, lse, delta_r, name="mla_attn_bwd")
        slots_late = None
    dq = dqt.transpose(1, 3, 0, 2).reshape(T, MLA_HEADS * 128)
    dlat, dqp, d_qn, d_kvn = _mla_prep_bwd(dq, dk, dv, proj, sm["mla_q_norm_w"], sm["mla_kv_norm_w"], W["w_uq_t"],
                                           W["w_k_t"], W["w_v_t"], cos_m, ss_m, name="mla_prep_bwd")
    g["w_uq_t"] = _mm_tn(dqp, cqn, name="dw_uq")
    g["w_k_t"] = _mm_tn(dk, ckvn, name="dw_ukv_k")
    g["w_v_t"] = _mm_tn(dv, ckvn, name="dw_ukv_v")
    dproj = jnp.concatenate([drq, drk, drv, dg, dlat], axis=1)
    g["w_in_t"] = _mm_tn(dproj, h, name="dw_in")
    dh = _mm(dproj, W["w_in_t"], name="d_h")
    grad_x, d_attn = _rmsnorm_bwd(x, sm["attn_norm_w"], dh, dx1, name="attn_norm_bwd")

    small = dict(attn_norm_w=d_attn, ret_gn_w=d_gn, mla_q_norm_w=d_qn, mla_kv_norm_w=d_kvn, ffn_norm_w=d_ffn,
                 conv_b=_deinterleave_ff(dcb), final_norm_w=d_final,
                 conv_w=_deinterleave_ff(jnp.concatenate([dcw0, dcw1, dcw2], axis=0)))
    return loss, grad_x, g, small, slots_late


def kernel(x, positions, attn_norm_w, w_in, ret_gn_w, mla_q_norm_w, w_uq, mla_kv_norm_w, w_ukv, w_out, ffn_norm_w, w_up, conv_w, conv_b, w_down, final_norm_w, loss_target, m_attn_norm_w, m_w_in, m_ret_gn_w, m_mla_q_norm_w, m_w_uq, m_mla_kv_norm_w, m_w_ukv, m_w_out, m_ffn_norm_w, m_w_up, m_conv_w, m_conv_b, m_w_down, m_final_norm_w, v_attn_norm_w, v_w_in, v_ret_gn_w, v_mla_q_norm_w, v_w_uq, v_mla_kv_norm_w, v_w_ukv, v_w_out, v_ffn_norm_w, v_w_up, v_conv_w, v_conv_b, v_w_down, v_final_norm_w):
    a = dict(locals())
    x_, y_, c_ = _place()
    dev = 4 * x_ + 2 * y_ + c_

    shard = {n: a[n][0] for n in BIG_NAMES}
    shard16 = {n: w.astype(BF16) for n, w in shard.items()}
    W = _prep_early(_all_gather(_pack_local(shard16, EARLY), name="gather_weights", in_vmem=False))
    cw_pad = jnp.pad(conv_w[0].reshape(-1), (0, 24 * 128 - 3 * 704)).reshape(24, 128)
    cw_all = _all_gather(cw_pad, name="gather_conv_w", in_vmem=True)
    conv_w_full = cw_all.reshape(N_DEV, -1)[:, :3 * 704].reshape(N_DEV, 3, 704).transpose(1, 0, 2).reshape(3, 2 * D_FF)
    sm = dict(attn_norm_w=attn_norm_w, ret_gn_w=ret_gn_w, mla_q_norm_w=mla_q_norm_w, mla_kv_norm_w=mla_kv_norm_w,
              ffn_norm_w=ffn_norm_w, final_norm_w=final_norm_w.reshape(1, D_MODEL),
              conv_w=_interleave_ff(conv_w_full), conv_b=_interleave_ff(conv_b))

    loss, grad_x, g, gs, slots_late = _local_step(x[0], positions[0], loss_target[0], W, sm, _pack_local(shard16, LATE))

    pair = _reduce_to_pairs(_pack_grads(_owner_rows_early(g), EARLY), name="grad_early")
    slots_early = _exchange_chips(pair, name="grad_early_exchange")
    big = [{}, {}, {}, {}]
    for group, slots, tag in ((EARLY, slots_early, "early"), (LATE, slots_late, "late")):
        names_g = [n for n, _, _, _ in group[0]]
        res = _adamw(_pack_local({n: shard[n] for n in names_g}, group),
                     _pack_local({n: a["m_" + n][0] for n in names_g}, group),
                     _pack_local({n: a["v_" + n][0] for n in names_g}, group), slots, name="adamw_" + tag)
        for kind in range(4):
            big[kind].update(_unpack_local(res[kind], shard, group))

    vec = jnp.concatenate([gs[n].reshape(-1) for n, _ in SMALL] + [gs["conv_w"].reshape(-1), loss[0, :1]])
    vec = jnp.pad(vec, (0, SMALL_ROWS * 128 - vec.shape[0])).reshape(SMALL_ROWS, 128)
    tot = _sum_slots(_all_gather(vec, name="gather_small_grads", in_vmem=True), name="sum_small_grads").reshape(-1)
    loss_out = tot[SMALL_N + 3 * 2 * D_FF]
    g_cw = lax.dynamic_slice_in_dim(tot[SMALL_N:SMALL_N + 3 * 2 * D_FF].reshape(3, 2 * D_FF), dev * 704, 704, axis=1)

    def flat_small(prefix):
        return jnp.concatenate([a[prefix + n].reshape(-1) for n, _ in SMALL]).reshape(75, 128)

    sml = _adamw(flat_small(""), flat_small("m_"), flat_small("v_"), tot[:SMALL_N].reshape(1, 75, 128), name="adamw_small")
    cwo = _adamw(conv_w[0], m_conv_w[0], v_conv_w[0], g_cw[None], name="adamw_conv_w")

    def small_of(t, n):
        off = 0
        for nm, sz in SMALL:
            if nm == n:
                return t.reshape(-1)[off:off + sz].reshape(a[n].shape)
            off += sz

    names = ['attn_norm_w', 'w_in', 'ret_gn_w', 'mla_q_norm_w', 'w_uq', 'mla_kv_norm_w', 'w_ukv', 'w_out',
             'ffn_norm_w', 'w_up', 'conv_w', 'conv_b', 'w_down', 'final_norm_w']
    outs = [loss_out, grad_x[None]]
    for kind in range(4):
        for n in names:
            if n == "conv_w":
                outs.append(cwo[kind][None])
            elif n in big[kind]:
                outs.append(big[kind][n])
            else:
                outs.append(small_of(sml[kind], n))
    return tuple(outs)
```

```python
import functools

import numpy as np
import jax
import jax.numpy as jnp
from jax import lax
from jax.experimental import pallas as pl
from jax.experimental.pallas import tpu as pltpu

F32 = jnp.float32
BF16 = jnp.bfloat16
MESH = pl.DeviceIdType.MESH
ANY = pl.BlockSpec(memory_space=pl.ANY)

D_MODEL = 1024
RET_HEADS = 8
RET_HEAD_DIM = 64
RET_WIDTH = 512
RET_CHUNK = 128
MLA_HEADS = 8
MLA_NOPE = 64
MLA_ROPE = 32
MLA_V = 64
MLA_Q_RANK = 256
MLA_KV_RANK = 128
MLA_WIDTH = 512
IN_WIDTH = 2464
IN_PAD = 2560
D_FF = 2816
FF_HALF = 1408
FF_OWNER_ORDER = (0, 1, 4, 5, 2, 3, 6, 7)
ROPE_BASE = 10000.0
EPS = 1e-6
SCALE = float((MLA_NOPE + MLA_ROPE) ** -0.5)
K_SCALE = 0.125
N_DEV = 8

ADAM_LR = 0.001
ADAM_B1 = 0.9
ADAM_B2 = 0.999
ADAM_EPS = 1e-08
ADAM_WD = 0.01
ADAM_STEP = 10

VMEM_LIMIT = 56 * 1024 * 1024
MM_BUDGET = 40 * 1024 * 1024
NEG = -1e30
FLASH_UNROLL = 4

PACK_COLS = 1024
EARLY = ((("w_in", 308, 320, True), ("w_uq", 24, 32, True), ("w_ukv", 16, 16, True)), 384)
LATE = ((("w_out", 128, 128, False), ("w_up", 704, 704, True), ("w_down", 352, 352, False)), 1200)
BIG_NAMES = ("w_in", "w_uq", "w_ukv", "w_out", "w_up", "w_down")
SMALL = (("attn_norm_w", 1024), ("ret_gn_w", 512), ("mla_q_norm_w", 256), ("mla_kv_norm_w", 128),
         ("ffn_norm_w", 1024), ("conv_b", 5632), ("final_norm_w", 1024))
SMALL_N = 9600
SMALL_ROWS = 208


def _cp(sem=None, vmem=VMEM_LIMIT):
    return pltpu.CompilerParams(dimension_semantics=sem, vmem_limit_bytes=vmem)


def _dot(a, b):
    return jnp.dot(a, b, preferred_element_type=F32)


def _dot_nt(a, b):
    return lax.dot_general(a, b, (((1,), (1,)), ((), ())), preferred_element_type=F32)


def _dot_tn(a, b):
    return lax.dot_general(a, b, (((0,), (0,)), ((), ())), preferred_element_type=F32)


def _sigmoid(x):
    return 0.5 * jnp.tanh(0.5 * x) + 0.5


def _partner(x, half, period):
    n = x.shape[-1]
    lane = lax.broadcasted_iota(jnp.int32, x.shape, 1)
    return jnp.where((lane % period) < half, pltpu.roll(x, n - half, 1), pltpu.roll(x, half, 1))


def _rope(x, cos, ss, half, period):
    return x * cos + _partner(x, half, period) * ss


def _rope_t(dy, cos, ss, half, period):
    return dy * cos - _partner(dy, half, period) * ss


def _head_masks(shape):
    lane = lax.broadcasted_iota(jnp.int32, shape, 1)
    m0 = (lane < 64).astype(F32)
    return m0, 1.0 - m0


def _mm(a, b, *, name, add=None, out_dtype=F32, bt=False):
    M, K = a.shape
    N = b.shape[0] if bt else b.shape[1]
    osz = jnp.dtype(out_dtype).itemsize
    per_row = 2 * (K * a.dtype.itemsize + N * osz + (N * 4 if add is not None else 0))
    tm = 128
    for cand in (512, 256):
        if M % cand == 0 and cand * per_row + 4 * K * N <= MM_BUDGET:
            tm = cand
            break
    tm = min(tm, M)
    mul = _dot_nt if bt else _dot

    def body(*refs):
        if add is None:
            a_ref, b_ref, o_ref = refs
            acc = mul(a_ref[...].astype(BF16), b_ref[...])
        else:
            a_ref, b_ref, r_ref, o_ref = refs
            acc = r_ref[...] + mul(a_ref[...].astype(BF16), b_ref[...])
        o_ref[...] = acc.astype(out_dtype)

    in_specs = [pl.BlockSpec((tm, K), lambda i: (i, 0)), pl.BlockSpec(b.shape, lambda i: (0, 0))]
    args = [a, b]
    if add is not None:
        in_specs.append(pl.BlockSpec((tm, N), lambda i: (i, 0)))
        args.append(add)
    return pl.pallas_call(
        body, name=name, grid=(M // tm,), in_specs=in_specs,
        out_specs=pl.BlockSpec((tm, N), lambda i: (i, 0)),
        out_shape=jax.ShapeDtypeStruct((M, N), out_dtype),
        compiler_params=_cp(("parallel",)))(*args)


def _mm_tn(a, b, *, name):
    T, M = a.shape
    N = b.shape[1]
    tk = min(T, 512)

    def tile(n):
        for cand in (1408, 1280):
            if n > 1408 and n % cand == 0:
                return cand
        return n

    tm, tn = tile(M), tile(N)
    nk = T // tk

    def body(a_ref, b_ref, o_ref):
        @pl.when(pl.program_id(2) == 0)
        def _():
            o_ref[...] = jnp.zeros_like(o_ref)
        o_ref[...] += _dot_tn(a_ref[...].astype(BF16), b_ref[...].astype(BF16))

    return pl.pallas_call(
        body, name=name, grid=(M // tm, N // tn, nk),
        in_specs=[pl.BlockSpec((tk, tm), lambda i, j, k: (k, i)), pl.BlockSpec((tk, tn), lambda i, j, k: (k, j))],
        out_specs=pl.BlockSpec((tm, tn), lambda i, j, k: (i, j)),
        out_shape=jax.ShapeDtypeStruct((M, N), F32),
        compiler_params=_cp(("parallel", "parallel", "arbitrary")))(a, b)


def _rmsnorm_fwd(x, w, *, name):
    T, D = x.shape
    tm = min(T, 1024)

    def body(x_ref, w_ref, o_ref):
        xv = x_ref[...]
        r = lax.rsqrt(jnp.mean(xv * xv, axis=-1, keepdims=True) + EPS)
        o_ref[...] = (xv * r * w_ref[...]).astype(BF16)

    return pl.pallas_call(
        body, name=name, grid=(T // tm,),
        in_specs=[pl.BlockSpec((tm, D), lambda i: (i, 0)), pl.BlockSpec((1, D), lambda i: (0, 0))],
        out_specs=pl.BlockSpec((tm, D), lambda i: (i, 0)),
        out_shape=jax.ShapeDtypeStruct((T, D), BF16),
        compiler_params=_cp(("parallel",)))(x, w)


def _rmsnorm_bwd(x, w, dh, dres, *, name):
    T, D = x.shape
    tm = min(T, 512)

    def body(x_ref, w_ref, dh_ref, dr_ref, dx_ref, dw_ref):
        @pl.when(pl.program_id(0) == 0)
        def _():
            dw_ref[...] = jnp.zeros_like(dw_ref)
        xv = x_ref[...]
        r = lax.rsqrt(jnp.mean(xv * xv, axis=-1, keepdims=True) + EPS)
        xh = xv * r
        dh = dh_ref[...]
        g = dh * w_ref[...]
        dx_ref[...] = dr_ref[...] + r * (g - xh * jnp.mean(g * xh, axis=-1, keepdims=True))
        dw_ref[...] += jnp.sum(dh * xh, axis=0, keepdims=True)

    row = pl.BlockSpec((tm, D), lambda i: (i, 0))
    vec = pl.BlockSpec((1, D), lambda i: (0, 0))
    return pl.pallas_call(
        body, name=name, grid=(T // tm,), in_specs=[row, vec, row, row], out_specs=[row, vec],
        out_shape=[jax.ShapeDtypeStruct((T, D), F32), jax.ShapeDtypeStruct((1, D), F32)],
        compiler_params=_cp(("arbitrary",)))(x, w, dh, dres)


def _loss_head(x2, tgt, w, *, name):
    T, D = x2.shape
    tm = min(T, 512)

    def body(x_ref, t_ref, w_ref, loss_ref, dx_ref, dxb_ref, dw_ref):
        @pl.when(pl.program_id(0) == 0)
        def _():
            dw_ref[...] = jnp.zeros_like(dw_ref)
            loss_ref[...] = jnp.zeros_like(loss_ref)
        xv = x_ref[...]
        wv = w_ref[...]
        r = lax.rsqrt(jnp.mean(xv * xv, axis=-1, keepdims=True) + EPS)
        xh = xv * r
        e = xh * wv - t_ref[...]
        part = 0.5 * jnp.sum(jnp.mean(e * e, axis=-1, keepdims=True), axis=0, keepdims=True)
        loss_ref[...] += jnp.broadcast_to(part, loss_ref.shape)
        dy = e * (1.0 / D)
        g = dy * wv
        dx = r * (g - xh * jnp.mean(g * xh, axis=-1, keepdims=True))
        dx_ref[...] = dx
        dxb_ref[...] = dx.astype(BF16)
        dw_ref[...] += jnp.sum(dy * xh, axis=0, keepdims=True)

    row = pl.BlockSpec((tm, D), lambda i: (i, 0))
    vec = pl.BlockSpec((1, D), lambda i: (0, 0))
    return pl.pallas_call(
        body, name=name, grid=(T // tm,), in_specs=[row, row, vec],
        out_specs=[pl.BlockSpec((1, 128), lambda i: (0, 0)), row, row, vec],
        out_shape=[jax.ShapeDtypeStruct((1, 128), F32), jax.ShapeDtypeStruct((T, D), F32),
                   jax.ShapeDtypeStruct((T, D), BF16), jax.ShapeDtypeStruct((1, D), F32)],
        compiler_params=_cp(("arbitrary",)))(x2, tgt, w)


def _ret_tables():
    C = RET_CHUNK
    h = jnp.arange(RET_HEADS, dtype=F32)
    log_gamma = jnp.log1p(-jnp.power(2.0, -5.0 - h))
    idx = jnp.arange(C, dtype=F32)
    diff = idx[:, None] - idx[None, :]
    dm = jnp.where(diff >= 0, jnp.exp(log_gamma[:, None, None] * jnp.maximum(diff, 0.0)), 0.0)
    dm = dm.reshape(4, 2 * C, C)
    lane_head = jnp.repeat(jnp.arange(RET_HEADS).reshape(4, 2), 64, axis=1)
    lg = log_gamma[lane_head]
    xi = jnp.exp(lg[:, None, :] * (idx[None, :, None] + 1.0))
    zeta = jnp.exp(lg[:, None, :] * (C - 1.0 - idx[None, :, None]))
    blk = (jnp.arange(128)[:, None] // 64) == (jnp.arange(128)[None, :] // 64)
    cd = jnp.where(blk[None], jnp.exp(lg * C)[:, :, None], 0.0)
    return dm.astype(F32), xi.astype(F32), zeta.astype(F32), cd.astype(F32)


def _ret_specs(tb, rev, nt):
    def tmap(t):
        return (nt - 1 - t) if rev else t
    qkv = [pl.BlockSpec((tb, 128), lambda p, t, o=o: (tmap(t), o + p)) for o in (0, 4, 8)]
    rope = [pl.BlockSpec((tb, 128), lambda p, t: (tmap(t), 0))] * 2
    tabs = [pl.BlockSpec((None, 256, 128), lambda p, t: (p, 0, 0))] + \
           [pl.BlockSpec((None, 128, 128), lambda p, t: (p, 0, 0))] * 3
    return qkv, rope, tabs


def _ret_fwd(proj, cos, ss, tabs, gnw, *, name):
    T = proj.shape[0]
    tb = min(T, 1024)
    nt = T // tb
    nchunk = tb // RET_CHUNK

    def body(q_ref, k_ref, v_ref, g_ref, cos_ref, ss_ref, dm_ref, xi_ref, zt_ref, cd_ref, gnw_ref,
             y_ref, o_ref, r_sc):
        @pl.when(pl.program_id(1) == 0)
        def _():
            r_sc[...] = jnp.zeros_like(r_sc)
        m0, m1 = _head_masks((128, 128))
        dm, xi, zt, cd = dm_ref[...], xi_ref[...], zt_ref[...], cd_ref[...]
        bm = (cd > 0).astype(F32)
        gnw = gnw_ref[...]
        for c in range(nchunk):
            rs = pl.ds(c * RET_CHUNK, RET_CHUNK)
            cs, sn = cos_ref[rs, :], ss_ref[rs, :]
            q = _rope(q_ref[rs, :], cs, sn, 32, 64)
            k = _rope(k_ref[rs, :], cs, sn, 32, 64) * K_SCALE
            v = v_ref[rs, :]
            kb, vb = k.astype(BF16), v.astype(BF16)
            qs = jnp.concatenate([q * m0, q * m1], axis=0).astype(BF16)
            s = (_dot_nt(qs, kb) * dm).astype(BF16)
            vs = jnp.concatenate([v * m0, v * m1], axis=0).astype(BF16)
            o = _dot(jnp.concatenate([s[:128], s[128:]], axis=1), vs)
            r = r_sc[...]
            o = o + _dot(q.astype(BF16), r.astype(BF16)) * xi
            r_sc[...] = cd * r + bm * _dot_tn((k * zt).astype(BF16), vb)
            mu = (jnp.sum(o * m0, axis=1, keepdims=True) * m0 + jnp.sum(o * m1, axis=1, keepdims=True) * m1) * (1.0 / 64)
            d = o - mu
            dd = d * d
            var = (jnp.sum(dd * m0, axis=1, keepdims=True) * m0 + jnp.sum(dd * m1, axis=1, keepdims=True) * m1) * (1.0 / 64)
            oh = d * lax.rsqrt(var + EPS)
            g = g_ref[rs, :]
            y_ref[rs, :] = (g * _sigmoid(g) * (oh * gnw)).astype(BF16)
            o_ref[rs, :] = o

    qkv, rope, tspec = _ret_specs(tb, False, nt)
    gspec = pl.BlockSpec((tb, 128), lambda p, t: (t, 12 + p))
    out = pl.BlockSpec((tb, 128), lambda p, t: (t, p))
    return pl.pallas_call(
        body, name=name, grid=(4, nt),
        in_specs=qkv + [gspec] + rope + tspec + [pl.BlockSpec((1, 128), lambda p, t: (0, p))],
        out_specs=[out, out],
        out_shape=[jax.ShapeDtypeStruct((T, RET_WIDTH), BF16), jax.ShapeDtypeStruct((T, RET_WIDTH), F32)],
        scratch_shapes=[pltpu.VMEM((128, 128), F32)],
        compiler_params=_cp(("parallel", "arbitrary")))(proj, proj, proj, proj, cos, ss, *tabs, gnw)


def _ret_bwd_dq(proj, do, cos, ss, tabs, *, name):
    T = proj.shape[0]
    tb = min(T, 1024)
    nt = T // tb
    nchunk = tb // RET_CHUNK

    def body(q_ref, k_ref, v_ref, do_ref, cos_ref, ss_ref, dm_ref, xi_ref, zt_ref, cd_ref, dq_ref, r_sc):
        del q_ref
        @pl.when(pl.program_id(1) == 0)
        def _():
            r_sc[...] = jnp.zeros_like(r_sc)
        m0, m1 = _head_masks((128, 128))
        dm, xi, zt, cd = dm_ref[...], xi_ref[...], zt_ref[...], cd_ref[...]
        bm = (cd > 0).astype(F32)
        for c in range(nchunk):
            rs = pl.ds(c * RET_CHUNK, RET_CHUNK)
            cs, sn = cos_ref[rs, :], ss_ref[rs, :]
            k = _rope(k_ref[rs, :], cs, sn, 32, 64) * K_SCALE
            vb = v_ref[rs, :].astype(BF16)
            dob = do_ref[rs, :]
            dof = dob.astype(F32)
            dos = jnp.concatenate([dof * m0, dof * m1], axis=0).astype(BF16)
            a = (_dot_nt(dos, vb) * dm).astype(BF16)
            ks = jnp.concatenate([k * m0, k * m1], axis=0).astype(BF16)
            r = r_sc[...]
            dq = _dot(jnp.concatenate([a[:128], a[128:]], axis=1), ks) + _dot_nt(dob, r.astype(BF16)) * xi
            r_sc[...] = cd * r + bm * _dot_tn((k * zt).astype(BF16), vb)
            dq_ref[rs, :] = _rope_t(dq, cs, sn, 32, 64).astype(BF16)

    qkv, rope, tspec = _ret_specs(tb, False, nt)
    blk = pl.BlockSpec((tb, 128), lambda p, t: (t, p))
    return pl.pallas_call(
        body, name=name, grid=(4, nt), in_specs=qkv + [blk] + rope + tspec, out_specs=blk,
        out_shape=jax.ShapeDtypeStruct((T, RET_WIDTH), BF16),
        scratch_shapes=[pltpu.VMEM((128, 128), F32)],
        compiler_params=_cp(("parallel", "arbitrary")))(proj, proj, proj, do, cos, ss, *tabs)


def _ret_bwd_dkv(proj, do, cos, ss, tabs, *, name):
    T = proj.shape[0]
    tb = min(T, 1024)
    nt = T // tb
    nchunk = tb // RET_CHUNK

    def body(q_ref, k_ref, v_ref, do_ref, cos_ref, ss_ref, dm_ref, xi_ref, zt_ref, cd_ref, dk_ref, dv_ref, u_sc):
        @pl.when(pl.program_id(1) == 0)
        def _():
            u_sc[...] = jnp.zeros_like(u_sc)
        m0, m1 = _head_masks((128, 128))
        dm, xi, zt, cd = dm_ref[...], xi_ref[...], zt_ref[...], cd_ref[...]
        bm = (cd > 0).astype(F32)
        for c in reversed(range(nchunk)):
            rs = pl.ds(c * RET_CHUNK, RET_CHUNK)
            cs, sn = cos_ref[rs, :], ss_ref[rs, :]
            q = _rope(q_ref[rs, :], cs, sn, 32, 64)
            k = _rope(k_ref[rs, :], cs, sn, 32, 64) * K_SCALE
            kb = k.astype(BF16)
            vb = v_ref[rs, :].astype(BF16)
            dob = do_ref[rs, :]
            dof = dob.astype(F32)
            qs = jnp.concatenate([q * m0, q * m1], axis=0).astype(BF16)
            dos = jnp.concatenate([dof * m0, dof * m1], axis=0).astype(BF16)
            s = (_dot_nt(qs, kb) * dm).astype(BF16)
            a = (_dot_nt(dos, vb) * dm).astype(BF16)
            ub = u_sc[...].astype(BF16)
            dk = _dot_tn(a, qs) + _dot_nt(vb, ub) * zt
            dv = _dot_tn(s, dos) + _dot(kb, ub) * zt
            u_sc[...] = cd * u_sc[...] + bm * _dot_tn((q * xi).astype(BF16), dob)
            dk_ref[rs, :] = (_rope_t(dk, cs, sn, 32, 64) * K_SCALE).astype(BF16)
            dv_ref[rs, :] = dv.astype(BF16)

    qkv, rope, tspec = _ret_specs(tb, True, nt)
    blk = pl.BlockSpec((tb, 128), lambda p, t: (nt - 1 - t, p))
    return pl.pallas_call(
        body, name=name, grid=(4, nt), in_specs=qkv + [blk] + rope + tspec, out_specs=[blk, blk],
        out_shape=[jax.ShapeDtypeStruct((T, RET_WIDTH), BF16)] * 2,
        scratch_shapes=[pltpu.VMEM((128, 128), F32)],
        compiler_params=_cp(("parallel", "arbitrary")))(proj, proj, proj, do, cos, ss, *tabs)


def _mix_bwd(dmixed, o_ret, proj, y_mla, gnw, *, name):
    T = dmixed.shape[0]
    tm = min(T, 512)

    def body(dm_ref, o_ref, g_ref, ym_ref, gnw_ref, do_ref, dg_ref, dom_ref, dl_ref, dw_ref):
        @pl.when(pl.program_id(0) == 0)
        def _():
            dw_ref[...] = jnp.zeros_like(dw_ref)
        m0, m1 = _head_masks((tm, 128))
        lane = lax.broadcasted_iota(jnp.int32, (tm, 128), 1)
        delta = jnp.zeros((tm, 128), F32)

        def gsum(z):
            return jnp.sum(z * m0, axis=1, keepdims=True) * m0 + jnp.sum(z * m1, axis=1, keepdims=True) * m1

        for p in range(4):
            cs = slice(128 * p, 128 * p + 128)
            dy = dm_ref[:, cs]
            o = o_ref[:, cs]
            g = g_ref[:, cs]
            w = gnw_ref[:, cs]
            d = o - gsum(o) * (1.0 / 64)
            rstd = lax.rsqrt(gsum(d * d) * (1.0 / 64) + EPS)
            oh = d * rstd
            sg = _sigmoid(g)
            dn = dy * (g * sg)
            dg_ref[:, cs] = (dy * (oh * w) * (sg * (1.0 + g * (1.0 - sg)))).astype(BF16)
            dw_ref[:, cs] += jnp.sum(dn * oh, axis=0, keepdims=True)
            doh = dn * w
            do = rstd * (doh - gsum(doh) * (1.0 / 64) - oh * (gsum(doh * oh) * (1.0 / 64)))
            do_ref[:, cs] = do.astype(BF16)
            dom = dm_ref[:, 512 + 128 * p:512 + 128 * p + 128]
            dom_ref[:, cs] = dom.astype(BF16)
            pr = dom * ym_ref[:, cs].astype(F32)
            delta = jnp.where(lane == 2 * p, jnp.sum(pr * m0, axis=1, keepdims=True), delta)
            delta = jnp.where(lane == 2 * p + 1, jnp.sum(pr * m1, axis=1, keepdims=True), delta)
        dl_ref[...] = delta

    half = pl.BlockSpec((tm, 512), lambda i: (i, 0))
    return pl.pallas_call(
        body, name=name, grid=(T // tm,),
        in_specs=[pl.BlockSpec((tm, 1024), lambda i: (i, 0)), half, pl.BlockSpec((tm, 512), lambda i: (i, 3)),
                  half, pl.BlockSpec((1, 512), lambda i: (0, 0))],
        out_specs=[half, half, half, pl.BlockSpec((tm, 128), lambda i: (i, 0)), pl.BlockSpec((1, 512), lambda i: (0, 0))],
        out_shape=[jax.ShapeDtypeStruct((T, 512), BF16)] * 3 + [jax.ShapeDtypeStruct((T, 128), F32),
                                                                jax.ShapeDtypeStruct((1, 512), F32)],
        compiler_params=_cp(("arbitrary",)))(dmixed, o_ret, proj, y_mla, gnw)


def _mla_prep_fwd(proj, qnw, kvnw, wuq, wk, wv, cos, ss, *, name):
    T = proj.shape[0]
    tm = min(T, 512)

    def body(lat_ref, qnw_ref, kvnw_ref, wuq_ref, wk_ref, wv_ref, cos_ref, ss_ref,
             q_ref, k_ref, v_ref, cqn_ref, ckvn_ref):
        cq = lat_ref[:, 0:256]
        ckv = lat_ref[:, 256:384]
        g3 = lat_ref[:, 384:512]
        cqn = (cq * lax.rsqrt(jnp.mean(cq * cq, axis=-1, keepdims=True) + EPS) * qnw_ref[...]).astype(BF16)
        ckvn = (ckv * lax.rsqrt(jnp.mean(ckv * ckv, axis=-1, keepdims=True) + EPS) * kvnw_ref[...]).astype(BF16)
        cqn_ref[...] = cqn
        ckvn_ref[...] = ckvn
        cs, sn = cos_ref[...], ss_ref[...]
        q = _dot_nt(cqn, wuq_ref[...])
        k = _dot_nt(ckvn, wk_ref[...])
        kpe = _rope(g3, cs, sn, 16, 32)
        for h in range(MLA_HEADS):
            hs = slice(128 * h, 128 * h + 128)
            q_ref[:, hs] = (_rope(q[:, hs], cs, sn, 16, 32) * SCALE).astype(BF16)
            k_ref[:, hs] = (k[:, hs] + kpe).astype(BF16)
        v_ref[...] = _dot_nt(ckvn, wv_ref[...]).astype(BF16)

    def full(shape):
        return pl.BlockSpec(shape, lambda i: (0, 0))

    def row(w):
        return pl.BlockSpec((tm, w), lambda i: (i, 0))

    return pl.pallas_call(
        body, name=name, grid=(T // tm,),
        in_specs=[pl.BlockSpec((tm, 512), lambda i: (i, 4)), full((1, 256)), full((1, 128)), full((1024, 256)),
                  full((1024, 128)), full((512, 128)), row(128), row(128)],
        out_specs=[row(1024), row(1024), row(512), row(256), row(128)],
        out_shape=[jax.ShapeDtypeStruct((T, 1024), BF16), jax.ShapeDtypeStruct((T, 1024), BF16),
                   jax.ShapeDtypeStruct((T, 512), BF16), jax.ShapeDtypeStruct((T, 256), BF16),
                   jax.ShapeDtypeStruct((T, 128), BF16)],
        compiler_params=_cp(("parallel",)))(proj, qnw, kvnw, wuq, wk, wv, cos, ss)


def _mla_prep_bwd(dq, dk, dv, proj, qnw, kvnw, wuq_t, wk_t, wv_t, cos, ss, *, name):
    T = proj.shape[0]
    tm = min(T, 512)

    def body(dq_ref, dk_ref, dv_ref, lat_ref, qnw_ref, kvnw_ref, wuq_ref, wk_ref, wv_ref, cos_ref, ss_ref,
             dlat_ref, dqp_ref, dqnw_ref, dkvnw_ref):
        @pl.when(pl.program_id(0) == 0)
        def _():
            dqnw_ref[...] = jnp.zeros_like(dqnw_ref)
            dkvnw_ref[...] = jnp.zeros_like(dkvnw_ref)
        cs, sn = cos_ref[...], ss_ref[...]
        dkpe = jnp.zeros((tm, 128), F32)
        for h in range(MLA_HEADS):
            hs = slice(128 * h, 128 * h + 128)
            dqp_ref[:, hs] = _rope_t(dq_ref[:, hs] * SCALE, cs, sn, 16, 32).astype(BF16)
            dkpe = dkpe + dk_ref[:, hs]
        lane = lax.broadcasted_iota(jnp.int32, (tm, 128), 1)
        rope_lane = (lane >= MLA_NOPE) & (lane < MLA_NOPE + MLA_ROPE)
        dg3 = jnp.where(rope_lane, _rope_t(jnp.where(rope_lane, dkpe, 0.0), cs, sn, 16, 32), 0.0)

        def norm_bwd(x, w, dn):
            r = lax.rsqrt(jnp.mean(x * x, axis=-1, keepdims=True) + EPS)
            xh = x * r
            g = dn * w
            return r * (g - xh * jnp.mean(g * xh, axis=-1, keepdims=True)), jnp.sum(dn * xh, axis=0, keepdims=True)

        dcqn = _dot(dqp_ref[...], wuq_ref[...])
        dcq, dqnw = norm_bwd(lat_ref[:, 0:256], qnw_ref[...], dcqn)
        dckvn = _dot(dk_ref[...].astype(BF16), wk_ref[...]) + _dot(dv_ref[...], wv_ref[...])
        dckv, dkvnw = norm_bwd(lat_ref[:, 256:384], kvnw_ref[...], dckvn)
        dqnw_ref[...] += dqnw
        dkvnw_ref[...] += dkvnw
        dlat_ref[:, 0:256] = dcq.astype(BF16)
        dlat_ref[:, 256:384] = dckv.astype(BF16)
        dlat_ref[:, 384:512] = dg3.astype(BF16)

    def full(shape):
        return pl.BlockSpec(shape, lambda i: (0, 0))

    def row(w):
        return pl.BlockSpec((tm, w), lambda i: (i, 0))

    return pl.pallas_call(
        body, name=name, grid=(T // tm,),
        in_specs=[row(1024), row(1024), row(512), pl.BlockSpec((tm, 512), lambda i: (i, 4)), full((1, 256)),
                  full((1, 128)), full((1024, 256)), full((1024, 128)), full((512, 128)), row(128), row(128)],
        out_specs=[row(512), row(1024), full((1, 256)), full((1, 128))],
        out_shape=[jax.ShapeDtypeStruct((T, 512), BF16), jax.ShapeDtypeStruct((T, 1024), BF16),
                   jax.ShapeDtypeStruct((1, 256), F32), jax.ShapeDtypeStruct((1, 128), F32)],
        compiler_params=_cp(("arbitrary",)))(dq, dk, dv, proj, qnw, kvnw, wuq_t, wk_t, wv_t, cos, ss)


def _flash_fwd(q, k, v1, *, name, gather=None):
    T = q.shape[0]
    tq = min(T, 512)
    tk = tq
    nq = T // tq

    def body(q_ref, k_ref, v_ref, *rest):
        if gather is None:
            y_ref, lse_ref = rest
        else:
            x_ref, y_ref, lse_ref, g_ref, *sems = rest
            start, forward, finish = _gather_phases(x_ref, g_ref, *sems)
            pl.when((pl.program_id(0) == 0) & (pl.program_id(1) == 0))(start)
            pl.when((pl.program_id(0) == 1) & (pl.program_id(1) == 0))(forward)
        attend(q_ref, k_ref, v_ref, y_ref, lse_ref)
        if gather is not None:
            pl.when((pl.program_id(0) == 3) & (pl.program_id(1) == nq - 1))(finish)

    def attend(q_ref, k_ref, v_ref, y_ref, lse_ref):
        qi = pl.program_id(1)
        row = lax.broadcasted_iota(jnp.int32, (tq, tk), 0)
        col = lax.broadcasted_iota(jnp.int32, (tq, tk), 1)

        def step(kb, carry, masked):
            ks = pl.ds(pl.multiple_of(kb * tk, tk), tk)
            new = []
            for h in range(2):
                hs = slice(128 * h, 128 * h + 128)
                m, acc = carry[h]
                s = _dot_nt(q_ref[:, hs], k_ref[ks, hs])
                if masked:
                    s = jnp.where(col <= row, s, NEG)
                mn = jnp.maximum(m, jnp.max(s, axis=1, keepdims=True))
                p = jnp.exp((s - mn).astype(BF16))
                acc = jnp.exp(m - mn) * acc + _dot(p, v_ref[ks, hs])
                new.append((mn, acc))
            return tuple(new)

        def unrolled(j, c):
            for u in range(FLASH_UNROLL):
                c = step(FLASH_UNROLL * j + u, c, False)
            return c

        init = (jnp.full((tq, 1), NEG, F32), jnp.zeros((tq, 128), F32))
        carry = lax.fori_loop(0, qi // FLASH_UNROLL, unrolled, (init, init))
        carry = lax.fori_loop(FLASH_UNROLL * (qi // FLASH_UNROLL), qi, lambda kb, c: step(kb, c, False), carry)
        (ma, acca), (mb, accb) = step(qi, carry, True)
        lane = lax.broadcasted_iota(jnp.int32, (tq, 128), 1)
        la, lb = pltpu.roll(acca, 64, 1), pltpu.roll(accb, 64, 1)
        y_ref[...] = jnp.where(lane < 64, acca / la, accb / lb).astype(BF16)
        lse_ref[0] = ma + jnp.log(acca[:, 64:65])
        lse_ref[1] = mb + jnp.log(accb[:, 0:1])

    in_specs = [pl.BlockSpec((tq, 256), lambda p, i: (i, p)), pl.BlockSpec((T, 256), lambda p, i: (0, p)),
                pl.BlockSpec((T, 256), lambda p, i: (0, p))]
    out_specs = [pl.BlockSpec((tq, 128), lambda p, i: (i, p)), pl.BlockSpec((2, tq, 1), lambda p, i: (p, i, 0))]
    out_shape = [jax.ShapeDtypeStruct((T, MLA_WIDTH), BF16), jax.ShapeDtypeStruct((MLA_HEADS, T, 1), F32)]
    if gather is None:
        return pl.pallas_call(body, name=name, grid=(4, nq), in_specs=in_specs, out_specs=out_specs,
                              out_shape=out_shape, compiler_params=_cp(("parallel", "arbitrary")))(q, k, v1)
    return pl.pallas_call(
        body, name=name, grid=(4, nq), in_specs=in_specs + [ANY], out_specs=out_specs + [ANY],
        out_shape=out_shape + [jax.ShapeDtypeStruct((N_DEV,) + gather.shape, gather.dtype)],
        scratch_shapes=list(GATHER_SCRATCH),
        compiler_params=_cp(("arbitrary", "arbitrary")))(q, k, v1, gather)


def _flash_bwd(q, k, v, do, lse, delta, *, name, exchange=None):
    T = q.shape[0]
    tq = min(T, 512)
    tk = tq
    nq = T // tq

    def body(q_ref, k_ref, v_ref, do_ref, lse_ref, dl_ref, *rest):
        if exchange is None:
            backward(q_ref, k_ref, v_ref, do_ref, lse_ref, dl_ref, *rest)
        else:
            p_ref, dqt_ref, dk_ref, dv_ref, got_ref, *sems = rest
            start, finish = _exchange_phases(p_ref, got_ref, *sems)
            pl.when((pl.program_id(0) == 0) & (pl.program_id(1) == 0))(start)
            backward(q_ref, k_ref, v_ref, do_ref, lse_ref, dl_ref, dqt_ref, dk_ref, dv_ref)
            pl.when((pl.program_id(0) == 3) & (pl.program_id(1) == nq - 1))(finish)

    def backward(q_ref, k_ref, v_ref, do_ref, lse_ref, dl_ref, dqt_ref, dk_ref, dv_ref):
        kb = pl.program_id(1)

        @pl.when(kb == 0)
        def _():
            dqt_ref[...] = jnp.zeros_like(dqt_ref)
        krow = lax.broadcasted_iota(jnp.int32, (tk, tq), 0)
        qcol = lax.broadcasted_iota(jnp.int32, (tk, tq), 1)
        masks = _head_masks((tk, 128))
        vf = v_ref[...].astype(F32)
        vms = [(vf * masks[h]).astype(BF16) for h in range(2)]

        def step(qi, carry, masked):
            qs = pl.ds(pl.multiple_of(qi * tq, tq), tq)
            dob = do_ref[qs, :]
            dof = dob.astype(F32)
            dks, dv_acc = list(carry[:2]), carry[2]
            for h in range(2):
                hs = slice(128 * h, 128 * h + 128)
                kh = k_ref[:, hs]
                qh = q_ref[qs, hs]
                st = _dot_nt(kh, qh)
                pt = jnp.exp((st - lse_ref[h, qi]).astype(BF16))
                if masked:
                    pt = jnp.where(krow <= qcol, pt, jnp.zeros_like(pt))
                dv_acc = dv_acc + _dot(pt, (dof * masks[h]).astype(BF16))
                dpt = _dot_nt(vms[h], dob)
                dst = pt * (dpt - dl_ref[h, qi]).astype(BF16)
                dks[h] = dks[h] + _dot(dst, qh)
                dqt_ref[qi, hs, :] += _dot_tn(kh, dst)
            return dks[0], dks[1], dv_acc

        zero = jnp.zeros((tk, 128), F32)
        carry = step(kb, (zero, zero, zero), True)

        def two_steps(j, c):
            qi = kb + 1 + 2 * j
            return step(qi + 1, step(qi, c, False), False)

        pairs = (nq - 1 - kb) // 2
        carry = lax.fori_loop(0, pairs, two_steps, carry)
        dk0, dk1, dv_acc = lax.fori_loop(kb + 1 + 2 * pairs, nq, lambda qi, c: step(qi, c, False), carry)
        dk_ref[:, 0:128] = dk0
        dk_ref[:, 128:256] = dk1
        dv_ref[...] = dv_acc.astype(BF16)

    stat = pl.BlockSpec((2, nq, 1, tq), lambda p, j: (p, 0, 0, 0))
    in_specs = [pl.BlockSpec((T, 256), lambda p, j: (0, p)), pl.BlockSpec((tk, 256), lambda p, j: (j, p)),
                pl.BlockSpec((tk, 128), lambda p, j: (j, p)), pl.BlockSpec((T, 128), lambda p, j: (0, p)), stat, stat]
    out_specs = [pl.BlockSpec((None, nq, 256, tq), lambda p, j: (p, 0, 0, 0)),
                 pl.BlockSpec((tk, 256), lambda p, j: (j, p)), pl.BlockSpec((tk, 128), lambda p, j: (j, p))]
    out_shape = [jax.ShapeDtypeStruct((4, nq, 256, tq), F32), jax.ShapeDtypeStruct((T, 1024), F32),
                 jax.ShapeDtypeStruct((T, MLA_WIDTH), BF16)]
    if exchange is None:
        return pl.pallas_call(body, name=name, grid=(4, nq), in_specs=in_specs, out_specs=out_specs,
                              out_shape=out_shape,
                              compiler_params=_cp(("parallel", "arbitrary")))(q, k, v, do, lse, delta)
    return pl.pallas_call(
        body, name=name, grid=(4, nq), in_specs=in_specs + [ANY], out_specs=out_specs + [ANY],
        out_shape=out_shape + [jax.ShapeDtypeStruct(exchange.shape, exchange.dtype)],
        scratch_shapes=list(EXCHANGE_SCRATCH),
        compiler_params=_cp(("arbitrary", "arbitrary")))(q, k, v, do, lse, delta, exchange)


def _shift_down(x, n, prev8):
    r = pltpu.roll(x, n, 0)
    row = lax.broadcasted_iota(jnp.int32, prev8.shape, 0)
    first = jnp.where(row < n, pltpu.roll(prev8, n, 0), r[:8])
    if x.shape[0] == 8:
        return first
    return jnp.concatenate([first, r[8:]], axis=0)


def _shift_up(x, n, next8):
    tm = x.shape[0]
    r = pltpu.roll(x, tm - n, 0)
    row = lax.broadcasted_iota(jnp.int32, next8.shape, 0)
    last = jnp.where(row >= 8 - n, pltpu.roll(next8, 8 - n, 0), r[tm - 8:])
    return jnp.concatenate([r[:tm - 8], last], axis=0)


def _conv_pre(u, prev8, cw_ref, cb_ref):
    p1 = _shift_down(u, 1, prev8)
    p2 = _shift_down(u, 2, prev8)
    up = cb_ref[...] + cw_ref[0:1, :] * p2 + cw_ref[1:2, :] * p1 + cw_ref[2:3, :] * u
    return up, p1, p2


def _conv_fwd(u, cw, cb, *, name):
    T = u.shape[0]
    tm = min(T, 512)
    W = 2 * FF_HALF

    def body(u_ref, prev_ref, cw_ref, cb_ref, a_ref):
        prev = jnp.where(pl.program_id(0) > 0, prev_ref[...], 0.0)
        up, _, _ = _conv_pre(u_ref[...], prev, cw_ref, cb_ref)
        gate = up[:, :FF_HALF]
        a_ref[...] = (gate * _sigmoid(gate) * up[:, FF_HALF:]).astype(BF16)

    return pl.pallas_call(
        body, name=name, grid=(T // tm, 2),
        in_specs=[pl.BlockSpec((tm, W), lambda i, j: (i, j)),
                  pl.BlockSpec((8, W), lambda i, j: (jnp.maximum(i * (tm // 8) - 1, 0), j)),
                  pl.BlockSpec((3, W), lambda i, j: (0, j)), pl.BlockSpec((1, W), lambda i, j: (0, j))],
        out_specs=pl.BlockSpec((tm, FF_HALF), lambda i, j: (i, j)),
        out_shape=jax.ShapeDtypeStruct((T, D_FF), BF16),
        compiler_params=_cp(("parallel", "parallel")))(u, u, cw, cb)


def _conv_bwd(u, da, cw, cb, *, name):
    T = u.shape[0]
    tm = min(T, 512)
    W = 2 * FF_HALF
    nt = T // tm

    def body(u_ref, prev_ref, next_ref, da_ref, dan_ref, cw_ref, cb_ref, du_ref, dw0_ref, dw1_ref, dw2_ref, db_ref):
        i = pl.program_id(1)

        @pl.when(i == 0)
        def _():
            for r in (dw0_ref, dw1_ref, dw2_ref, db_ref):
                r[...] = jnp.zeros_like(r)

        def dpre(u, prev8, da):
            up, p1, p2 = _conv_pre(u, prev8, cw_ref, cb_ref)
            gate, val = up[:, :FF_HALF], up[:, FF_HALF:]
            sg = _sigmoid(gate)
            dgate = da * val * (sg * (1.0 + gate * (1.0 - sg)))
            dval = da * (gate * sg)
            return jnp.concatenate([dgate, dval], axis=1), p1, p2

        u = u_ref[...]
        prev = jnp.where(i > 0, prev_ref[...], 0.0)
        dup, p1, p2 = dpre(u, prev, da_ref[...])
        dupn, _, _ = dpre(next_ref[...], u[tm - 8:], dan_ref[...])
        dupn = jnp.where(i < nt - 1, dupn, 0.0)
        du = cw_ref[2:3, :] * dup + cw_ref[1:2, :] * _shift_up(dup, 1, dupn) + cw_ref[0:1, :] * _shift_up(dup, 2, dupn)
        du_ref[...] = du.astype(BF16)
        dw0_ref[...] += jnp.sum(dup * p2, axis=0, keepdims=True)
        dw1_ref[...] += jnp.sum(dup * p1, axis=0, keepdims=True)
        dw2_ref[...] += jnp.sum(dup * u, axis=0, keepdims=True)
        db_ref[...] += jnp.sum(dup, axis=0, keepdims=True)

    nxt = lambda j, i: (jnp.minimum((i + 1) * (tm // 8), T // 8 - 1), j)
    vec = pl.BlockSpec((1, W), lambda j, i: (0, j))
    return pl.pallas_call(
        body, name=name, grid=(2, nt),
        in_specs=[pl.BlockSpec((tm, W), lambda j, i: (i, j)),
                  pl.BlockSpec((8, W), lambda j, i: (jnp.maximum(i * (tm // 8) - 1, 0), j)),
                  pl.BlockSpec((8, W), nxt),
                  pl.BlockSpec((tm, FF_HALF), lambda j, i: (i, j)), pl.BlockSpec((8, FF_HALF), nxt),
                  pl.BlockSpec((3, W), lambda j, i: (0, j)), vec],
        out_specs=[pl.BlockSpec((tm, W), lambda j, i: (i, j)), vec, vec, vec, vec],
        out_shape=[jax.ShapeDtypeStruct((T, 2 * D_FF), BF16)] + [jax.ShapeDtypeStruct((1, 2 * D_FF), F32)] * 4,
        compiler_params=_cp(("parallel", "arbitrary")))(u, u, u, da, da, cw, cb)


def _adamw(w, m, v, g_slots, *, name):
    R, C = w.shape
    ns = g_slots.shape[0]
    tr = _row_tile(R)

    def body(w_ref, m_ref, v_ref, g_ref, go_ref, d_ref, mo_ref, vo_ref):
        g = g_ref[0].astype(F32)
        for s in range(1, ns):
            g = g + g_ref[s].astype(F32)
        mn = ADAM_B1 * m_ref[...] + (1.0 - ADAM_B1) * g
        vn = ADAM_B2 * v_ref[...] + (1.0 - ADAM_B2) * (g * g)
        m_hat = mn / (1.0 - ADAM_B1 ** ADAM_STEP)
        v_hat = vn / (1.0 - ADAM_B2 ** ADAM_STEP)
        go_ref[...] = g
        d_ref[...] = -ADAM_LR * (m_hat / (jnp.sqrt(v_hat) + ADAM_EPS) + ADAM_WD * w_ref[...])
        mo_ref[...] = mn
        vo_ref[...] = vn

    blk = pl.BlockSpec((tr, C), lambda i: (i, 0))
    return pl.pallas_call(
        body, name=name, grid=(R // tr,),
        in_specs=[blk, blk, blk, pl.BlockSpec((ns, tr, C), lambda i: (0, i, 0))],
        out_specs=[blk] * 4, out_shape=[jax.ShapeDtypeStruct((R, C), F32)] * 4,
        compiler_params=_cp(("parallel",)))(w, m, v, g_slots)


def _place():
    return lax.axis_index("x"), lax.axis_index("y"), lax.axis_index("c")


GATHER_SCRATCH = (pltpu.SemaphoreType.DMA((7,)), pltpu.SemaphoreType.DMA((7,)), pltpu.SemaphoreType.DMA)
EXCHANGE_SCRATCH = (pltpu.SemaphoreType.DMA((3,)), pltpu.SemaphoreType.DMA((3,)), pltpu.SemaphoreType.DMA)


def _gather_phases(x_ref, out_ref, send_sems, recv_sems, local_sem):
    x_, y_, c_ = _place()
    me, sibling = (x_, y_, c_), (x_, y_, 1 - c_)
    chips = [(1 - x_, y_), (x_, 1 - y_), (1 - x_, 1 - y_)]

    def slot(px, py, pc):
        return out_ref.at[4 * px + 2 * py + pc]

    def copy(k, block, to, src=None):
        return pltpu.make_async_remote_copy(
            src_ref=slot(*block) if src is None else src, dst_ref=slot(*block),
            send_sem=send_sems.at[k], recv_sem=recv_sems.at[k], device_id=to, device_id_type=MESH)

    def mine():
        return pltpu.make_async_copy(x_ref, slot(*me), local_sem)

    def first():
        return [copy(0, me, sibling, src=x_ref)] + [copy(1 + j, me, (*chip, c_), src=x_ref)
                                                     for j, chip in enumerate(chips)]

    def passed():
        return [copy(4 + j, (*chip, c_), sibling) for j, chip in enumerate(chips)]

    def start():
        mine().start()
        for cp in first():
            cp.start()

    def forward():
        fwd = passed()
        for j, chip in enumerate(chips):
            copy(1 + j, (*chip, c_), me).wait_recv()
            fwd[j].start()

    def finish():
        copy(0, sibling, me).wait_recv()
        for j, chip in enumerate(chips):
            copy(4 + j, (*chip, 1 - c_), me).wait_recv()
        for cp in first() + passed():
            cp.wait_send()
        mine().wait()

    return start, forward, finish


def _exchange_phases(p_ref, out_ref, send_sems, recv_sems, local_sem):
    x_, y_, c_ = _place()
    me_k = 2 * x_ + y_
    chips = [(1 - x_, y_), (x_, 1 - y_), (1 - x_, 1 - y_)]

    def local():
        return pltpu.make_async_copy(p_ref.at[me_k], out_ref.at[me_k], local_sem)

    def copy(j, src_k, dst_k, chip):
        return pltpu.make_async_remote_copy(
            src_ref=p_ref.at[src_k], dst_ref=out_ref.at[dst_k], send_sem=send_sems.at[j],
            recv_sem=recv_sems.at[j], device_id=(*chip, c_), device_id_type=MESH)

    def sends():
        return [copy(j, 2 * px + py, me_k, (px, py)) for j, (px, py) in enumerate(chips)]

    def start():
        local().start()
        for cp in sends():
            cp.start()

    def finish():
        for j, (px, py) in enumerate(chips):
            copy(j, me_k, 2 * px + py, (px, py)).wait_recv()
        for cp in sends():
            cp.wait_send()
        local().wait()

    return start, finish


def _all_gather(x, *, name, in_vmem):
    def body(x_ref, out_ref, send_sems, recv_sems, local_sem):
        for phase in _gather_phases(x_ref, out_ref, send_sems, recv_sems, local_sem):
            phase()

    spec = pl.BlockSpec(memory_space=pltpu.VMEM) if in_vmem else ANY
    return pl.pallas_call(
        body, name=name, out_shape=jax.ShapeDtypeStruct((N_DEV,) + x.shape, x.dtype),
        in_specs=[spec], out_specs=spec, scratch_shapes=list(GATHER_SCRATCH),
        compiler_params=pltpu.CompilerParams(vmem_limit_bytes=VMEM_LIMIT))(x)


def _sum_slots(g, *, name):
    n = g.shape[0]

    def body(g_ref, o_ref):
        acc = g_ref[0]
        for s in range(1, n):
            acc = acc + g_ref[s]
        o_ref[...] = acc

    return pl.pallas_call(body, name=name, out_shape=jax.ShapeDtypeStruct(g.shape[1:], g.dtype))(g)


def _swap_sibling(g, *, name):
    def body(g_ref, out_ref, send_sems, recv_sems):
        x_, y_, c_ = _place()
        copies = [pltpu.make_async_remote_copy(src_ref=g_ref.at[k, 1 - c_], dst_ref=out_ref.at[k],
                                               send_sem=send_sems.at[k], recv_sem=recv_sems.at[k],
                                               device_id=(x_, y_, 1 - c_), device_id_type=MESH) for k in range(4)]
        for cp in copies:
            cp.start()
        for cp in copies:
            cp.wait()

    return pl.pallas_call(
        body, name=name, out_shape=jax.ShapeDtypeStruct((4,) + g.shape[2:], g.dtype), in_specs=[ANY], out_specs=ANY,
        scratch_shapes=[pltpu.SemaphoreType.DMA((4,)), pltpu.SemaphoreType.DMA((4,))])(g)


def _exchange_chips(p, *, name):
    def body(p_ref, out_ref, send_sems, recv_sems, local_sem):
        for phase in _exchange_phases(p_ref, out_ref, send_sems, recv_sems, local_sem):
            phase()

    return pl.pallas_call(
        body, name=name, out_shape=jax.ShapeDtypeStruct(p.shape, p.dtype), in_specs=[ANY], out_specs=ANY,
        scratch_shapes=list(EXCHANGE_SCRATCH))(p)


def _row_tile(R):
    for cand in (256, 400, 200):
        if R % cand == 0:
            return cand
    return R


def _add_own(g, b, *, name, out_dtype):
    n, _, R, C = g.shape
    tr = _row_tile(R)

    def body(c_ref, g_ref, b_ref, o_ref):
        del c_ref
        o_ref[...] = (g_ref[...] + b_ref[...]).astype(out_dtype)

    blk = pl.BlockSpec((None, tr, C), lambda s, i, c: (s, i, 0))
    grid_spec = pltpu.PrefetchScalarGridSpec(
        num_scalar_prefetch=1, grid=(n, R // tr),
        in_specs=[pl.BlockSpec((None, None, tr, C), lambda s, i, c: (s, c[0], i, 0)), blk], out_specs=blk)
    core = jnp.reshape(lax.axis_index("c"), (1,)).astype(jnp.int32)
    return pl.pallas_call(body, name=name, grid_spec=grid_spec, out_shape=jax.ShapeDtypeStruct(b.shape, out_dtype),
                          compiler_params=_cp(("parallel", "parallel")))(core, g, b)


def _pack_local(parts, group):
    table, rows = group
    segs = []
    for n, r, rp, tr in table:
        w = parts[n].T if tr else parts[n]
        segs.append(jnp.pad(w.reshape(r, PACK_COLS), ((0, rp - r), (0, 0))))
    segs.append(jnp.zeros((rows - sum(rp for _, _, rp, _ in table), PACK_COLS), segs[0].dtype))
    return jnp.concatenate(segs, axis=0)


def _unpack_local(packed, like, group):
    out, off = {}, 0
    for n, r, rp, tr in group[0]:
        rows, cols = like[n].shape
        seg = packed[off:off + r]
        out[n] = (seg.reshape(cols, rows).T if tr else seg)[None]
        off += rp
    return out


def _segments(g, group):
    out, off = {}, 0
    for n, r, rp, _ in group[0]:
        out[n] = g[:, off:off + r]
        off += rp
    return out


def _pack_grads(parts, group):
    table, rows = group
    segs = [jnp.pad(parts[n], ((0, 0), (0, rp - parts[n].shape[1]), (0, 0))) for n, _, rp, _ in table]
    segs.append(jnp.zeros((N_DEV, rows - sum(rp for _, _, rp, _ in table), PACK_COLS), F32))
    return jnp.concatenate(segs, axis=1)


def _owner_rows_early(g):
    g_in = jnp.concatenate([g["w_in_t"][:2432], g["w_in_t"][2496:2528]], axis=0).reshape(N_DEV, 308, PACK_COLS)
    g_uq = g["w_uq_t"].reshape(N_DEV, 128, MLA_Q_RANK)[:, :96].reshape(N_DEV, 24, PACK_COLS)
    g_ukv = jnp.concatenate([g["w_k_t"].reshape(N_DEV, 128, MLA_KV_RANK)[:, :64],
                             g["w_v_t"].reshape(N_DEV, 64, MLA_KV_RANK)], axis=1).reshape(N_DEV, 16, PACK_COLS)
    return dict(w_in=g_in, w_uq=g_uq, w_ukv=g_ukv)


def _owner_rows_late(g):
    up = g["w_up_t"].reshape(N_DEV, 704, PACK_COLS)
    g_up = jnp.stack([up[FF_OWNER_ORDER.index(d)] for d in range(N_DEV)])
    return dict(w_out=g["w_out"].reshape(N_DEV, 128, PACK_COLS), w_up=g_up,
                w_down=g["w_down"].reshape(N_DEV, 352, PACK_COLS))


def _reduce_to_pairs(gp, *, name):
    gp = gp.reshape(4, 2, gp.shape[1], PACK_COLS)
    return _add_own(gp, _swap_sibling(gp, name=name + "_swap"), out_dtype=BF16, name=name + "_sum")


def _interleave_ff(w):
    g, v = w[..., :D_FF], w[..., D_FF:]
    return jnp.concatenate([g[..., :FF_HALF], v[..., :FF_HALF], g[..., FF_HALF:], v[..., FF_HALF:]], axis=-1)


def _deinterleave_ff(w):
    b = [w[..., i * FF_HALF:(i + 1) * FF_HALF] for i in range(4)]
    return jnp.concatenate([b[0], b[2], b[1], b[3]], axis=-1)


def _rope_tables(pos):
    p = pos.astype(F32)[:, None]
    inv_r = ROPE_BASE ** (-jnp.arange(0, RET_HEAD_DIM, 2, dtype=F32) / RET_HEAD_DIM)
    ang = p * inv_r
    c, s = jnp.cos(ang), jnp.sin(ang)
    cos_r = jnp.concatenate([c, c, c, c], axis=1)
    ss_r = jnp.concatenate([-s, s, -s, s], axis=1)
    c, s = c[:, 0::2], s[:, 0::2]
    T = pos.shape[0]
    cos_m = jnp.concatenate([jnp.ones((T, 64), F32), c, c, jnp.ones((T, 32), F32)], axis=1)
    ss_m = jnp.concatenate([jnp.zeros((T, 64), F32), -s, s, jnp.zeros((T, 32), F32)], axis=1)
    return cos_r, ss_r, cos_m, ss_m


def _prep_early(gathered):
    seg = _segments(gathered, EARLY)
    w_in_t = seg["w_in"].reshape(IN_WIDTH, D_MODEL)
    z = lambda n: jnp.zeros((n, D_MODEL), BF16)
    w_in_t = jnp.concatenate([w_in_t[:2432], z(64), w_in_t[2432:2464], z(32)], axis=0)
    w_uq_t = jnp.pad(seg["w_uq"].reshape(MLA_HEADS, 96, MLA_Q_RANK), ((0, 0), (0, 32), (0, 0))).reshape(1024, MLA_Q_RANK)
    ukv = seg["w_ukv"].reshape(MLA_HEADS, 128, MLA_KV_RANK)
    w_k_t = jnp.pad(ukv[:, :64], ((0, 0), (0, 64), (0, 0))).reshape(1024, MLA_KV_RANK)
    w_v_t = ukv[:, 64:].reshape(512, MLA_KV_RANK)
    return dict(w_in_t=w_in_t, w_uq_t=w_uq_t, w_k_t=w_k_t, w_v_t=w_v_t)


def _prep_late(gathered):
    seg = _segments(gathered, LATE)
    w_up_t = jnp.concatenate([seg["w_up"][d] for d in FF_OWNER_ORDER], axis=0)
    return dict(w_out=seg["w_out"].reshape(1024, D_MODEL), w_up_t=w_up_t, w_down=seg["w_down"].reshape(D_FF, D_MODEL))


def _local_step(x, pos, tgt, W, sm, late):
    dist = not isinstance(late, dict)
    cos_r, ss_r, cos_m, ss_m = _rope_tables(pos)
    tabs = _ret_tables()

    h = _rmsnorm_fwd(x, sm["attn_norm_w"], name="attn_norm")
    proj = _mm(h, W["w_in_t"], bt=True, name="in_proj")
    y_ret, o_ret = _ret_fwd(proj, cos_r, ss_r, tabs, sm["ret_gn_w"], name="ret_fwd")
    q, k, v, cqn, ckvn = _mla_prep_fwd(proj, sm["mla_q_norm_w"], sm["mla_kv_norm_w"], W["w_uq_t"], W["w_k_t"],
                                       W["w_v_t"], cos_m, ss_m, name="mla_prep")
    T = x.shape[0]
    tq = min(T, 512)
    vp = v.reshape(T, 4, 2, MLA_V)
    one = jnp.ones((T, 4, MLA_V), BF16)
    v1 = jnp.stack([vp[:, :, 0], one, one, vp[:, :, 1]], axis=2).reshape(T, MLA_HEADS * 128)
    if dist:
        y_mla, lse, gathered = _flash_fwd(q, k, v1, gather=late, name="mla_attn")
        W = {**W, **_prep_late(gathered)}
    else:
        y_mla, lse = _flash_fwd(q, k, v1, name="mla_attn")
        W = {**W, **late}
    lse = lse.reshape(MLA_HEADS, T // tq, 1, tq)
    mixed = jnp.concatenate([y_ret, y_mla], axis=1)
    x1 = _mm(mixed, W["w_out"], add=x, name="out_proj")
    h2 = _rmsnorm_fwd(x1, sm["ffn_norm_w"], name="ffn_norm")
    u = _mm(h2, W["w_up_t"], bt=True, name="up_proj")
    a = _conv_fwd(u, sm["conv_w"], sm["conv_b"], name="conv_gate")
    x2 = _mm(a, W["w_down"], add=x1, name="down_proj")
    loss, dx2, dx2b, d_final = _loss_head(x2, tgt, sm["final_norm_w"], name="loss_head")

    g = {}
    g["w_down"] = _mm_tn(a, dx2b, name="dw_down")
    da = _mm(dx2b, W["w_down"], bt=True, name="d_act")
    du, dcw0, dcw1, dcw2, dcb = _conv_bwd(u, da, sm["conv_w"], sm["conv_b"], name="conv_bwd")
    g["w_up_t"] = _mm_tn(du, h2, name="dw_up")
    dh2 = _mm(du, W["w_up_t"], name="d_h2")
    dx1, d_ffn = _rmsnorm_bwd(x1, sm["ffn_norm_w"], dh2, dx2, name="ffn_norm_bwd")

    g["w_out"] = _mm_tn(mixed, dx1, name="dw_out")
    dmixed = _mm(dx1, W["w_out"], bt=True, name="d_mixed")
    do_ret, dg, do_mla, delta, d_gn = _mix_bwd(dmixed, o_ret, proj, y_mla, sm["ret_gn_w"], name="mix_bwd")
    drq = _ret_bwd_dq(proj, do_ret, cos_r, ss_r, tabs, name="ret_bwd_dq")
    drk, drv = _ret_bwd_dkv(proj, do_ret, cos_r, ss_r, tabs, name="ret_bwd_dkv")
    delta_r = delta[:, :MLA_HEADS].T.reshape(MLA_HEADS, T // tq, 1, tq)
    if dist:
        pair = _reduce_to_pairs(_pack_grads(_owner_rows_late(g), LATE), name="grad_late")
        dqt, dk, dv, slots_late = _flash_bwd(q, k, v, do_mla, lse, delta_r, exchange=pair, name="mla_attn_bwd")
    else:
        dqt, dk, dv = _flash_bwd(q, k, v, do_mla, lse, delta_r, name="mla_attn_bwd")
        slots_late = None
    dq = dqt.transpose(1, 3, 0, 2).reshape(T, MLA_HEADS * 128)
    dlat, dqp, d_qn, d_kvn = _mla_prep_bwd(dq, dk, dv, proj, sm["mla_q_norm_w"], sm["mla_kv_norm_w"], W["w_uq_t"],
                                           W["w_k_t"], W["w_v_t"], cos_m, ss_m, name="mla_prep_bwd")
    g["w_uq_t"] = _mm_tn(dqp, cqn, name="dw_uq")
    g["w_k_t"] = _mm_tn(dk, ckvn, name="dw_ukv_k")
    g["w_v_t"] = _mm_tn(dv, ckvn, name="dw_ukv_v")
    dproj = jnp.concatenate([drq, drk, drv, dg, dlat], axis=1)
    g["w_in_t"] = _mm_tn(dproj, h, name="dw_in")
    dh = _mm(dproj, W["w_in_t"], name="d_h")
    grad_x, d_attn = _rmsnorm_bwd(x, sm["attn_norm_w"], dh, dx1, name="attn_norm_bwd")

    small = dict(attn_norm_w=d_attn, ret_gn_w=d_gn, mla_q_norm_w=d_qn, mla_kv_norm_w=d_kvn, ffn_norm_w=d_ffn,
                 conv_b=_deinterleave_ff(dcb), final_norm_w=d_final,
                 conv_w=_deinterleave_ff(jnp.concatenate([dcw0, dcw1, dcw2], axis=0)))
    return loss, grad_x, g, small, slots_late


def kernel(x, positions, attn_norm_w, w_in, ret_gn_w, mla_q_norm_w, w_uq, mla_kv_norm_w, w_ukv, w_out, ffn_norm_w, w_up, conv_w, conv_b, w_down, final_norm_w, loss_target, m_attn_norm_w, m_w_in, m_ret_gn_w, m_mla_q_norm_w, m_w_uq, m_mla_kv_norm_w, m_w_ukv, m_w_out, m_ffn_norm_w, m_w_up, m_conv_w, m_conv_b, m_w_down, m_final_norm_w, v_attn_norm_w, v_w_in, v_ret_gn_w, v_mla_q_norm_w, v_w_uq, v_mla_kv_norm_w, v_w_ukv, v_w_out, v_ffn_norm_w, v_w_up, v_conv_w, v_conv_b, v_w_down, v_final_norm_w):
    a = dict(locals())
    x_, y_, c_ = _place()
    dev = 4 * x_ + 2 * y_ + c_

    shard = {n: a[n][0] for n in BIG_NAMES}
    shard16 = {n: w.astype(BF16) for n, w in shard.items()}
    W = _prep_early(_all_gather(_pack_local(shard16, EARLY), name="gather_weights", in_vmem=False))
    cw_pad = jnp.pad(conv_w[0].reshape(-1), (0, 24 * 128 - 3 * 704)).reshape(24, 128)
    cw_all = _all_gather(cw_pad, name="gather_conv_w", in_vmem=True)
    conv_w_full = cw_all.reshape(N_DEV, -1)[:, :3 * 704].reshape(N_DEV, 3, 704).transpose(1, 0, 2).reshape(3, 2 * D_FF)
    sm = dict(attn_norm_w=attn_norm_w, ret_gn_w=ret_gn_w, mla_q_norm_w=mla_q_norm_w, mla_kv_norm_w=mla_kv_norm_w,
              ffn_norm_w=ffn_norm_w, final_norm_w=final_norm_w.reshape(1, D_MODEL),
              conv_w=_interleave_ff(conv_w_full), conv_b=_interleave_ff(conv_b))

    loss, grad_x, g, gs, slots_late = _local_step(x[0], positions[0], loss_target[0], W, sm, _pack_local(shard16, LATE))

    pair = _reduce_to_pairs(_pack_grads(_owner_rows_early(g), EARLY), name="grad_early")
    slots_early = _exchange_chips(pair, name="grad_early_exchange")
    big = [{}, {}, {}, {}]
    for group, slots, tag in ((EARLY, slots_early, "early"), (LATE, slots_late, "late")):
        names_g = [n for n, _, _, _ in group[0]]
        res = _adamw(_pack_local({n: shard[n] for n in names_g}, group),
                     _pack_local({n: a["m_" + n][0] for n in names_g}, group),
                     _pack_local({n: a["v_" + n][0] for n in names_g}, group), slots, name="adamw_" + tag)
        for kind in range(4):
            big[kind].update(_unpack_local(res[kind], shard, group))

    vec = jnp.concatenate([gs[n].reshape(-1) for n, _ in SMALL] + [gs["conv_w"].reshape(-1), loss[0, :1]])
    vec = jnp.pad(vec, (0, SMALL_ROWS * 128 - vec.shape[0])).reshape(SMALL_ROWS, 128)
    tot = _sum_slots(_all_gather(vec, name="gather_small_grads", in_vmem=True), name="sum_small_grads").reshape(-1)
    loss_out = tot[SMALL_N + 3 * 2 * D_FF]
    g_cw = lax.dynamic_slice_in_dim(tot[SMALL_N:SMALL_N + 3 * 2 * D_FF].reshape(3, 2 * D_FF), dev * 704, 704, axis=1)

    def flat_small(prefix):
        return jnp.concatenate([a[prefix + n].reshape(-1) for n, _ in SMALL]).reshape(75, 128)

    sml = _adamw(flat_small(""), flat_small("m_"), flat_small("v_"), tot[:SMALL_N].reshape(1, 75, 128), name="adamw_small")
    cwo = _adamw(conv_w[0], m_conv_w[0], v_conv_w[0], g_cw[None], name="adamw_conv_w")

    def small_of(t, n):
        off = 0
        for nm, sz in SMALL:
            if nm == n:
                return t.reshape(-1)[off:off + sz].reshape(a[n].shape)
            off += sz

    names = ['attn_norm_w', 'w_in', 'ret_gn_w', 'mla_q_norm_w', 'w_uq', 'mla_kv_norm_w', 'w_ukv', 'w_out',
             'ffn_norm_w', 'w_up', 'conv_w', 'conv_b', 'w_down', 'final_norm_w']
    outs = [loss_out, grad_x[None]]
    for kind in range(4):
        for n in names:
            if n == "conv_w":
                outs.append(cwo[kind][None])
            elif n in big[kind]:
                outs.append(big[kind][n])
            else:
                outs.append(small_of(sml[kind], n))
    return tuple(outs)
```

```python
import functools

import numpy as np
import jax
import jax.numpy as jnp
from jax import lax
from jax.experimental import pallas as pl
from jax.experimental.pallas import tpu as pltpu

F32 = jnp.float32
BF16 = jnp.bfloat16
MESH = pl.DeviceIdType.MESH
ANY = pl.BlockSpec(memory_space=pl.ANY)

D_MODEL = 1024
RET_HEADS = 8
RET_HEAD_DIM = 64
RET_WIDTH = 512
RET_CHUNK = 128
MLA_HEADS = 8
MLA_NOPE = 64
MLA_ROPE = 32
MLA_V = 64
MLA_Q_RANK = 256
MLA_KV_RANK = 128
MLA_WIDTH = 512
IN_WIDTH = 2464
IN_PAD = 2560
D_FF = 2816
FF_HALF = 1408
FF_OWNER_ORDER = (0, 1, 4, 5, 2, 3, 6, 7)
ROPE_BASE = 10000.0
EPS = 1e-6
SCALE = float((MLA_NOPE + MLA_ROPE) ** -0.5)
K_SCALE = 0.125
N_DEV = 8

ADAM_LR = 0.001
ADAM_B1 = 0.9
ADAM_B2 = 0.999
ADAM_EPS = 1e-08
ADAM_WD = 0.01
ADAM_STEP = 10

VMEM_LIMIT = 56 * 1024 * 1024
MM_BUDGET = 40 * 1024 * 1024
NEG = -1e30
FLASH_UNROLL = 4

PACK_COLS = 1024
EARLY = ((("w_in", 308, 320, True), ("w_uq", 24, 32, True), ("w_ukv", 16, 16, True)), 384)
LATE = ((("w_out", 128, 128, False), ("w_up", 704, 704, True), ("w_down", 352, 352, False)), 1200)
BIG_NAMES = ("w_in", "w_uq", "w_ukv", "w_out", "w_up", "w_down")
SMALL = (("attn_norm_w", 1024), ("ret_gn_w", 512), ("mla_q_norm_w", 256), ("mla_kv_norm_w", 128),
         ("ffn_norm_w", 1024), ("conv_b", 5632), ("final_norm_w", 1024))
SMALL_N = 9600
SMALL_ROWS = 208


def _cp(sem=None, vmem=VMEM_LIMIT):
    return pltpu.CompilerParams(dimension_semantics=sem, vmem_limit_bytes=vmem)


def _dot(a, b):
    return jnp.dot(a, b, preferred_element_type=F32)


def _dot_nt(a, b):
    return lax.dot_general(a, b, (((1,), (1,)), ((), ())), preferred_element_type=F32)


def _dot_tn(a, b):
    return lax.dot_general(a, b, (((0,), (0,)), ((), ())), preferred_element_type=F32)


def _sigmoid(x):
    return 0.5 * jnp.tanh(0.5 * x) + 0.5


def _partner(x, half, period):
    n = x.shape[-1]
    lane = lax.broadcasted_iota(jnp.int32, x.shape, 1)
    return jnp.where((lane % period) < half, pltpu.roll(x, n - half, 1), pltpu.roll(x, half, 1))


def _rope(x, cos, ss, half, period):
    return x * cos + _partner(x, half, period) * ss


def _rope_t(dy, cos, ss, half, period):
    return dy * cos - _partner(dy, half, period) * ss


def _head_masks(shape):
    lane = lax.broadcasted_iota(jnp.int32, shape, 1)
    m0 = (lane < 64).astype(F32)
    return m0, 1.0 - m0


def _mm(a, b, *, name, add=None, out_dtype=F32, bt=False, exchange=None):
    M, K = a.shape
    N = b.shape[0] if bt else b.shape[1]
    osz = jnp.dtype(out_dtype).itemsize
    per_row = 2 * (K * a.dtype.itemsize + N * osz + (N * 4 if add is not None else 0))
    tm = 128
    for cand in (512, 256):
        if M % cand == 0 and cand * per_row + 4 * K * N <= MM_BUDGET:
            tm = cand
            break
    tm = min(tm, M)
    mul = _dot_nt if bt else _dot
    n_in = 2 if add is None else 3

    def body(*refs):
        a_ref, b_ref = refs[:2]
        acc = mul(a_ref[...].astype(BF16), b_ref[...])
        if add is not None:
            acc = refs[2][...] + acc
        if exchange is None:
            refs[n_in][...] = acc.astype(out_dtype)
        else:
            p_ref, o_ref, got_ref, *sems = refs[n_in:]
            start, finish = _exchange_phases(p_ref, got_ref, *sems)
            pl.when(pl.program_id(0) == 0)(start)
            o_ref[...] = acc.astype(out_dtype)
            pl.when(pl.program_id(0) == M // tm - 1)(finish)

    in_specs = [pl.BlockSpec((tm, K), lambda i: (i, 0)), pl.BlockSpec(b.shape, lambda i: (0, 0))]
    args = [a, b]
    if add is not None:
        in_specs.append(pl.BlockSpec((tm, N), lambda i: (i, 0)))
        args.append(add)
    out_spec = pl.BlockSpec((tm, N), lambda i: (i, 0))
    out_shape = jax.ShapeDtypeStruct((M, N), out_dtype)
    if exchange is None:
        return pl.pallas_call(body, name=name, grid=(M // tm,), in_specs=in_specs, out_specs=out_spec,
                              out_shape=out_shape, compiler_params=_cp(("parallel",)))(*args)
    return pl.pallas_call(
        body, name=name, grid=(M // tm,), in_specs=in_specs + [ANY], out_specs=[out_spec, ANY],
        out_shape=[out_shape, jax.ShapeDtypeStruct(exchange.shape, exchange.dtype)],
        scratch_shapes=list(EXCHANGE_SCRATCH), compiler_params=_cp(("arbitrary",)))(*args, exchange)


def _mm_tn(a, b, *, name):
    T, M = a.shape
    N = b.shape[1]
    tk = min(T, 512)

    def tile(n):
        for cand in (1408, 1280):
            if n > 1408 and n % cand == 0:
                return cand
        return n

    tm, tn = tile(M), tile(N)
    nk = T // tk

    def body(a_ref, b_ref, o_ref):
        @pl.when(pl.program_id(2) == 0)
        def _():
            o_ref[...] = jnp.zeros_like(o_ref)
        o_ref[...] += _dot_tn(a_ref[...].astype(BF16), b_ref[...].astype(BF16))

    return pl.pallas_call(
        body, name=name, grid=(M // tm, N // tn, nk),
        in_specs=[pl.BlockSpec((tk, tm), lambda i, j, k: (k, i)), pl.BlockSpec((tk, tn), lambda i, j, k: (k, j))],
        out_specs=pl.BlockSpec((tm, tn), lambda i, j, k: (i, j)),
        out_shape=jax.ShapeDtypeStruct((M, N), F32),
        compiler_params=_cp(("parallel", "parallel", "arbitrary")))(a, b)


def _rmsnorm_fwd(x, w, *, name, gather=None):
    T, D = x.shape
    tm = min(T, 1024)
    n = T // tm

    def body(x_ref, w_ref, *rest):
        if gather is not None:
            s_ref, o_ref, g_ref, *sems = rest
            start, forward, finish = _gather_phases(s_ref, g_ref, *sems)
            pl.when(pl.program_id(0) == 0)(start)
            pl.when(pl.program_id(0) == n // 2)(forward)
        else:
            o_ref, = rest
        xv = x_ref[...]
        r = lax.rsqrt(jnp.mean(xv * xv, axis=-1, keepdims=True) + EPS)
        o_ref[...] = (xv * r * w_ref[...]).astype(BF16)
        if gather is not None:
            pl.when(pl.program_id(0) == n - 1)(finish)

    in_specs = [pl.BlockSpec((tm, D), lambda i: (i, 0)), pl.BlockSpec((1, D), lambda i: (0, 0))]
    out_spec = pl.BlockSpec((tm, D), lambda i: (i, 0))
    out_shape = jax.ShapeDtypeStruct((T, D), BF16)
    if gather is None:
        return pl.pallas_call(body, name=name, grid=(n,), in_specs=in_specs, out_specs=out_spec, out_shape=out_shape,
                              compiler_params=_cp(("parallel",)))(x, w)
    return pl.pallas_call(
        body, name=name, grid=(n,), in_specs=in_specs + [ANY], out_specs=[out_spec, ANY],
        out_shape=[out_shape, jax.ShapeDtypeStruct((N_DEV,) + gather.shape, gather.dtype)],
        scratch_shapes=list(GATHER_SCRATCH), compiler_params=_cp(("arbitrary",)))(x, w, gather)


def _rmsnorm_bwd(x, w, dh, dres, *, name):
    T, D = x.shape
    tm = min(T, 512)

    def body(x_ref, w_ref, dh_ref, dr_ref, dx_ref, dw_ref):
        @pl.when(pl.program_id(0) == 0)
        def _():
            dw_ref[...] = jnp.zeros_like(dw_ref)
        xv = x_ref[...]
        r = lax.rsqrt(jnp.mean(xv * xv, axis=-1, keepdims=True) + EPS)
        xh = xv * r
        dh = dh_ref[...]
        g = dh * w_ref[...]
        dx_ref[...] = dr_ref[...] + r * (g - xh * jnp.mean(g * xh, axis=-1, keepdims=True))
        dw_ref[...] += jnp.sum(dh * xh, axis=0, keepdims=True)

    row = pl.BlockSpec((tm, D), lambda i: (i, 0))
    vec = pl.BlockSpec((1, D), lambda i: (0, 0))
    return pl.pallas_call(
        body, name=name, grid=(T // tm,), in_specs=[row, vec, row, row], out_specs=[row, vec],
        out_shape=[jax.ShapeDtypeStruct((T, D), F32), jax.ShapeDtypeStruct((1, D), F32)],
        compiler_params=_cp(("arbitrary",)))(x, w, dh, dres)


def _loss_head(x2, tgt, w, *, name):
    T, D = x2.shape
    tm = min(T, 512)

    def body(x_ref, t_ref, w_ref, loss_ref, dx_ref, dxb_ref, dw_ref):
        @pl.when(pl.program_id(0) == 0)
        def _():
            dw_ref[...] = jnp.zeros_like(dw_ref)
            loss_ref[...] = jnp.zeros_like(loss_ref)
        xv = x_ref[...]
        wv = w_ref[...]
        r = lax.rsqrt(jnp.mean(xv * xv, axis=-1, keepdims=True) + EPS)
        xh = xv * r
        e = xh * wv - t_ref[...]
        part = 0.5 * jnp.sum(jnp.mean(e * e, axis=-1, keepdims=True), axis=0, keepdims=True)
        loss_ref[...] += jnp.broadcast_to(part, loss_ref.shape)
        dy = e * (1.0 / D)
        g = dy * wv
        dx = r * (g - xh * jnp.mean(g * xh, axis=-1, keepdims=True))
        dx_ref[...] = dx
        dxb_ref[...] = dx.astype(BF16)
        dw_ref[...] += jnp.sum(dy * xh, axis=0, keepdims=True)

    row = pl.BlockSpec((tm, D), lambda i: (i, 0))
    vec = pl.BlockSpec((1, D), lambda i: (0, 0))
    return pl.pallas_call(
        body, name=name, grid=(T // tm,), in_specs=[row, row, vec],
        out_specs=[pl.BlockSpec((1, 128), lambda i: (0, 0)), row, row, vec],
        out_shape=[jax.ShapeDtypeStruct((1, 128), F32), jax.ShapeDtypeStruct((T, D), F32),
                   jax.ShapeDtypeStruct((T, D), BF16), jax.ShapeDtypeStruct((1, D), F32)],
        compiler_params=_cp(("arbitrary",)))(x2, tgt, w)


def _ret_tables():
    C = RET_CHUNK
    h = jnp.arange(RET_HEADS, dtype=F32)
    log_gamma = jnp.log1p(-jnp.power(2.0, -5.0 - h))
    idx = jnp.arange(C, dtype=F32)
    diff = idx[:, None] - idx[None, :]
    dm = jnp.where(diff >= 0, jnp.exp(log_gamma[:, None, None] * jnp.maximum(diff, 0.0)), 0.0)
    dm = dm.reshape(4, 2 * C, C)
    lane_head = jnp.repeat(jnp.arange(RET_HEADS).reshape(4, 2), 64, axis=1)
    lg = log_gamma[lane_head]
    xi = jnp.exp(lg[:, None, :] * (idx[None, :, None] + 1.0))
    zeta = jnp.exp(lg[:, None, :] * (C - 1.0 - idx[None, :, None]))
    blk = (jnp.arange(128)[:, None] // 64) == (jnp.arange(128)[None, :] // 64)
    cd = jnp.where(blk[None], jnp.exp(lg * C)[:, :, None], 0.0)
    return dm.astype(F32), xi.astype(F32), zeta.astype(F32), cd.astype(F32)


def _ret_specs(tb, rev, nt):
    def tmap(t):
        return (nt - 1 - t) if rev else t
    qkv = [pl.BlockSpec((tb, 128), lambda p, t, o=o: (tmap(t), o + p)) for o in (0, 4, 8)]
    rope = [pl.BlockSpec((tb, 128), lambda p, t: (tmap(t), 0))] * 2
    tabs = [pl.BlockSpec((None, 256, 128), lambda p, t: (p, 0, 0))] + \
           [pl.BlockSpec((None, 128, 128), lambda p, t: (p, 0, 0))] * 3
    return qkv, rope, tabs


def _ret_fwd(proj, cos, ss, tabs, gnw, *, name):
    T = proj.shape[0]
    tb = min(T, 1024)
    nt = T // tb
    nchunk = tb // RET_CHUNK

    def body(q_ref, k_ref, v_ref, g_ref, cos_ref, ss_ref, dm_ref, xi_ref, zt_ref, cd_ref, gnw_ref,
             y_ref, o_ref, r_sc):
        @pl.when(pl.program_id(1) == 0)
        def _():
            r_sc[...] = jnp.zeros_like(r_sc)
        m0, m1 = _head_masks((128, 128))
        dm, xi, zt, cd = dm_ref[...], xi_ref[...], zt_ref[...], cd_ref[...]
        bm = (cd > 0).astype(F32)
        gnw = gnw_ref[...]
        for c in range(nchunk):
            rs = pl.ds(c * RET_CHUNK, RET_CHUNK)
            cs, sn = cos_ref[rs, :], ss_ref[rs, :]
            q = _rope(q_ref[rs, :], cs, sn, 32, 64)
            k = _rope(k_ref[rs, :], cs, sn, 32, 64) * K_SCALE
            v = v_ref[rs, :]
            kb, vb = k.astype(BF16), v.astype(BF16)
            qs = jnp.concatenate([q * m0, q * m1], axis=0).astype(BF16)
            s = (_dot_nt(qs, kb) * dm).astype(BF16)
            vs = jnp.concatenate([v * m0, v * m1], axis=0).astype(BF16)
            o = _dot(jnp.concatenate([s[:128], s[128:]], axis=1), vs)
            r = r_sc[...]
            o = o + _dot(q.astype(BF16), r.astype(BF16)) * xi
            r_sc[...] = cd * r + bm * _dot_tn((k * zt).astype(BF16), vb)
            mu = (jnp.sum(o * m0, axis=1, keepdims=True) * m0 + jnp.sum(o * m1, axis=1, keepdims=True) * m1) * (1.0 / 64)
            d = o - mu
            dd = d * d
            var = (jnp.sum(dd * m0, axis=1, keepdims=True) * m0 + jnp.sum(dd * m1, axis=1, keepdims=True) * m1) * (1.0 / 64)
            oh = d * lax.rsqrt(var + EPS)
            g = g_ref[rs, :]
            y_ref[rs, :] = (g * _sigmoid(g) * (oh * gnw)).astype(BF16)
            o_ref[rs, :] = o

    qkv, rope, tspec = _ret_specs(tb, False, nt)
    gspec = pl.BlockSpec((tb, 128), lambda p, t: (t, 12 + p))
    out = pl.BlockSpec((tb, 128), lambda p, t: (t, p))
    return pl.pallas_call(
        body, name=name, grid=(4, nt),
        in_specs=qkv + [gspec] + rope + tspec + [pl.BlockSpec((1, 128), lambda p, t: (0, p))],
        out_specs=[out, out],
        out_shape=[jax.ShapeDtypeStruct((T, RET_WIDTH), BF16), jax.ShapeDtypeStruct((T, RET_WIDTH), F32)],
        scratch_shapes=[pltpu.VMEM((128, 128), F32)],
        compiler_params=_cp(("parallel", "arbitrary")))(proj, proj, proj, proj, cos, ss, *tabs, gnw)


def _ret_bwd_dq(proj, do, cos, ss, tabs, *, name):
    T = proj.shape[0]
    tb = min(T, 1024)
    nt = T // tb
    nchunk = tb // RET_CHUNK

    def body(q_ref, k_ref, v_ref, do_ref, cos_ref, ss_ref, dm_ref, xi_ref, zt_ref, cd_ref, dq_ref, r_sc):
        del q_ref
        @pl.when(pl.program_id(1) == 0)
        def _():
            r_sc[...] = jnp.zeros_like(r_sc)
        m0, m1 = _head_masks((128, 128))
        dm, xi, zt, cd = dm_ref[...], xi_ref[...], zt_ref[...], cd_ref[...]
        bm = (cd > 0).astype(F32)
        for c in range(nchunk):
            rs = pl.ds(c * RET_CHUNK, RET_CHUNK)
            cs, sn = cos_ref[rs, :], ss_ref[rs, :]
            k = _rope(k_ref[rs, :], cs, sn, 32, 64) * K_SCALE
            vb = v_ref[rs, :].astype(BF16)
            dob = do_ref[rs, :]
            dof = dob.astype(F32)
            dos = jnp.concatenate([dof * m0, dof * m1], axis=0).astype(BF16)
            a = (_dot_nt(dos, vb) * dm).astype(BF16)
            ks = jnp.concatenate([k * m0, k * m1], axis=0).astype(BF16)
            r = r_sc[...]
            dq = _dot(jnp.concatenate([a[:128], a[128:]], axis=1), ks) + _dot_nt(dob, r.astype(BF16)) * xi
            r_sc[...] = cd * r + bm * _dot_tn((k * zt).astype(BF16), vb)
            dq_ref[rs, :] = _rope_t(dq, cs, sn, 32, 64).astype(BF16)

    qkv, rope, tspec = _ret_specs(tb, False, nt)
    blk = pl.BlockSpec((tb, 128), lambda p, t: (t, p))
    return pl.pallas_call(
        body, name=name, grid=(4, nt), in_specs=qkv + [blk] + rope + tspec, out_specs=blk,
        out_shape=jax.ShapeDtypeStruct((T, RET_WIDTH), BF16),
        scratch_shapes=[pltpu.VMEM((128, 128), F32)],
        compiler_params=_cp(("parallel", "arbitrary")))(proj, proj, proj, do, cos, ss, *tabs)


def _ret_bwd_dkv(proj, do, cos, ss, tabs, *, name, swap=None):
    T = proj.shape[0]
    tb = min(T, 1024)
    nt = T // tb
    nchunk = tb // RET_CHUNK

    def body(q_ref, k_ref, v_ref, do_ref, cos_ref, ss_ref, dm_ref, xi_ref, zt_ref, cd_ref, *rest):
        if swap is None:
            backward(q_ref, k_ref, v_ref, do_ref, cos_ref, ss_ref, dm_ref, xi_ref, zt_ref, cd_ref, *rest)
        else:
            g_ref, dk_ref, dv_ref, got_ref, u_sc, *sems = rest
            start, finish = _swap_phases(g_ref, got_ref, *sems)
            pl.when((pl.program_id(0) == 0) & (pl.program_id(1) == 0))(start)
            backward(q_ref, k_ref, v_ref, do_ref, cos_ref, ss_ref, dm_ref, xi_ref, zt_ref, cd_ref, dk_ref, dv_ref, u_sc)
            pl.when((pl.program_id(0) == 3) & (pl.program_id(1) == nt - 1))(finish)

    def backward(q_ref, k_ref, v_ref, do_ref, cos_ref, ss_ref, dm_ref, xi_ref, zt_ref, cd_ref, dk_ref, dv_ref, u_sc):
        @pl.when(pl.program_id(1) == 0)
        def _():
            u_sc[...] = jnp.zeros_like(u_sc)
        m0, m1 = _head_masks((128, 128))
        dm, xi, zt, cd = dm_ref[...], xi_ref[...], zt_ref[...], cd_ref[...]
        bm = (cd > 0).astype(F32)
        for c in reversed(range(nchunk)):
            rs = pl.ds(c * RET_CHUNK, RET_CHUNK)
            cs, sn = cos_ref[rs, :], ss_ref[rs, :]
            q = _rope(q_ref[rs, :], cs, sn, 32, 64)
            k = _rope(k_ref[rs, :], cs, sn, 32, 64) * K_SCALE
            kb = k.astype(BF16)
            vb = v_ref[rs, :].astype(BF16)
            dob = do_ref[rs, :]
            dof = dob.astype(F32)
            qs = jnp.concatenate([q * m0, q * m1], axis=0).astype(BF16)
            dos = jnp.concatenate([dof * m0, dof * m1], axis=0).astype(BF16)
            s = (_dot_nt(qs, kb) * dm).astype(BF16)
            a = (_dot_nt(dos, vb) * dm).astype(BF16)
            ub = u_sc[...].astype(BF16)
            dk = _dot_tn(a, qs) + _dot_nt(vb, ub) * zt
            dv = _dot_tn(s, dos) + _dot(kb, ub) * zt
            u_sc[...] = cd * u_sc[...] + bm * _dot_tn((q * xi).astype(BF16), dob)
            dk_ref[rs, :] = (_rope_t(dk, cs, sn, 32, 64) * K_SCALE).astype(BF16)
            dv_ref[rs, :] = dv.astype(BF16)

    qkv, rope, tspec = _ret_specs(tb, True, nt)
    blk = pl.BlockSpec((tb, 128), lambda p, t: (nt - 1 - t, p))
    out_shape = [jax.ShapeDtypeStruct((T, RET_WIDTH), BF16)] * 2
    if swap is None:
        return pl.pallas_call(
            body, name=name, grid=(4, nt), in_specs=qkv + [blk] + rope + tspec, out_specs=[blk, blk],
            out_shape=out_shape, scratch_shapes=[pltpu.VMEM((128, 128), F32)],
            compiler_params=_cp(("parallel", "arbitrary")))(proj, proj, proj, do, cos, ss, *tabs)
    return pl.pallas_call(
        body, name=name, grid=(4, nt), in_specs=qkv + [blk] + rope + tspec + [ANY], out_specs=[blk, blk, ANY],
        out_shape=out_shape + [jax.ShapeDtypeStruct((4,) + swap.shape[2:], swap.dtype)],
        scratch_shapes=[pltpu.VMEM((128, 128), F32)] + list(SWAP_SCRATCH),
        compiler_params=_cp(("arbitrary", "arbitrary")))(proj, proj, proj, do, cos, ss, *tabs, swap)


def _mix_bwd(dmixed, o_ret, proj, y_mla, gnw, *, name):
    T = dmixed.shape[0]
    tm = min(T, 512)

    def body(dm_ref, o_ref, g_ref, ym_ref, gnw_ref, do_ref, dg_ref, dom_ref, dl_ref, dw_ref):
        @pl.when(pl.program_id(0) == 0)
        def _():
            dw_ref[...] = jnp.zeros_like(dw_ref)
        m0, m1 = _head_masks((tm, 128))
        lane = lax.broadcasted_iota(jnp.int32, (tm, 128), 1)
        delta = jnp.zeros((tm, 128), F32)

        def gsum(z):
            return jnp.sum(z * m0, axis=1, keepdims=True) * m0 + jnp.sum(z * m1, axis=1, keepdims=True) * m1

        for p in range(4):
            cs = slice(128 * p, 128 * p + 128)
            dy = dm_ref[:, cs]
            o = o_ref[:, cs]
            g = g_ref[:, cs]
            w = gnw_ref[:, cs]
            d = o - gsum(o) * (1.0 / 64)
            rstd = lax.rsqrt(gsum(d * d) * (1.0 / 64) + EPS)
            oh = d * rstd
            sg = _sigmoid(g)
            dn = dy * (g * sg)
            dg_ref[:, cs] = (dy * (oh * w) * (sg * (1.0 + g * (1.0 - sg)))).astype(BF16)
            dw_ref[:, cs] += jnp.sum(dn * oh, axis=0, keepdims=True)
            doh = dn * w
            do = rstd * (doh - gsum(doh) * (1.0 / 64) - oh * (gsum(doh * oh) * (1.0 / 64)))
            do_ref[:, cs] = do.astype(BF16)
            dom = dm_ref[:, 512 + 128 * p:512 + 128 * p + 128]
            dom_ref[:, cs] = dom.astype(BF16)
            pr = dom * ym_ref[:, cs].astype(F32)
            delta = jnp.where(lane == 2 * p, jnp.sum(pr * m0, axis=1, keepdims=True), delta)
            delta = jnp.where(lane == 2 * p + 1, jnp.sum(pr * m1, axis=1, keepdims=True), delta)
        dl_ref[...] = delta

    half = pl.BlockSpec((tm, 512), lambda i: (i, 0))
    return pl.pallas_call(
        body, name=name, grid=(T // tm,),
        in_specs=[pl.BlockSpec((tm, 1024), lambda i: (i, 0)), half, pl.BlockSpec((tm, 512), lambda i: (i, 3)),
                  half, pl.BlockSpec((1, 512), lambda i: (0, 0))],
        out_specs=[half, half, half, pl.BlockSpec((tm, 128), lambda i: (i, 0)), pl.BlockSpec((1, 512), lambda i: (0, 0))],
        out_shape=[jax.ShapeDtypeStruct((T, 512), BF16)] * 3 + [jax.ShapeDtypeStruct((T, 128), F32),
                                                                jax.ShapeDtypeStruct((1, 512), F32)],
        compiler_params=_cp(("arbitrary",)))(dmixed, o_ret, proj, y_mla, gnw)


def _mla_prep_fwd(proj, qnw, kvnw, wuq, wk, wv, cos, ss, *, name):
    T = proj.shape[0]
    tm = min(T, 512)

    def body(lat_ref, qnw_ref, kvnw_ref, wuq_ref, wk_ref, wv_ref, cos_ref, ss_ref,
             q_ref, k_ref, v_ref, cqn_ref, ckvn_ref):
        cq = lat_ref[:, 0:256]
        ckv = lat_ref[:, 256:384]
        g3 = lat_ref[:, 384:512]
        cqn = (cq * lax.rsqrt(jnp.mean(cq * cq, axis=-1, keepdims=True) + EPS) * qnw_ref[...]).astype(BF16)
        ckvn = (ckv * lax.rsqrt(jnp.mean(ckv * ckv, axis=-1, keepdims=True) + EPS) * kvnw_ref[...]).astype(BF16)
        cqn_ref[...] = cqn
        ckvn_ref[...] = ckvn
        cs, sn = cos_ref[...], ss_ref[...]
        q = _dot_nt(cqn, wuq_ref[...])
        k = _dot_nt(ckvn, wk_ref[...])
        kpe = _rope(g3, cs, sn, 16, 32)
        for h in range(MLA_HEADS):
            hs = slice(128 * h, 128 * h + 128)
            q_ref[:, hs] = (_rope(q[:, hs], cs, sn, 16, 32) * SCALE).astype(BF16)
            k_ref[:, hs] = (k[:, hs] + kpe).astype(BF16)
        v = _dot_nt(ckvn, wv_ref[...])
        lane = lax.broadcasted_iota(jnp.int32, (tm, 128), 1)
        for p in range(4):
            vp = v[:, 128 * p:128 * p + 128]
            v_ref[:, 256 * p:256 * p + 128] = jnp.where(lane < 64, vp, 1.0).astype(BF16)
            v_ref[:, 256 * p + 128:256 * p + 256] = jnp.where(lane < 64, 1.0, vp).astype(BF16)

    def full(shape):
        return pl.BlockSpec(shape, lambda i: (0, 0))

    def row(w):
        return pl.BlockSpec((tm, w), lambda i: (i, 0))

    return pl.pallas_call(
        body, name=name, grid=(T // tm,),
        in_specs=[pl.BlockSpec((tm, 512), lambda i: (i, 4)), full((1, 256)), full((1, 128)), full((1024, 256)),
                  full((1024, 128)), full((512, 128)), row(128), row(128)],
        out_specs=[row(1024), row(1024), row(1024), row(256), row(128)],
        out_shape=[jax.ShapeDtypeStruct((T, 1024), BF16), jax.ShapeDtypeStruct((T, 1024), BF16),
                   jax.ShapeDtypeStruct((T, 1024), BF16), jax.ShapeDtypeStruct((T, 256), BF16),
                   jax.ShapeDtypeStruct((T, 128), BF16)],
        compiler_params=_cp(("parallel",)))(proj, qnw, kvnw, wuq, wk, wv, cos, ss)


def _mla_prep_bwd(dq, dk, dv, proj, qnw, kvnw, wuq_t, wk_t, wv_t, cos, ss, *, name):
    T = proj.shape[0]
    tm = min(T, 512)

    def body(dq_ref, dk_ref, dv_ref, lat_ref, qnw_ref, kvnw_ref, wuq_ref, wk_ref, wv_ref, cos_ref, ss_ref,
             dlat_ref, dqp_ref, dqnw_ref, dkvnw_ref):
        @pl.when(pl.program_id(0) == 0)
        def _():
            dqnw_ref[...] = jnp.zeros_like(dqnw_ref)
            dkvnw_ref[...] = jnp.zeros_like(dkvnw_ref)
        cs, sn = cos_ref[...], ss_ref[...]
        dkpe = jnp.zeros((tm, 128), F32)
        for h in range(MLA_HEADS):
            hs = slice(128 * h, 128 * h + 128)
            dqp_ref[:, hs] = _rope_t(dq_ref[:, hs] * SCALE, cs, sn, 16, 32).astype(BF16)
            dkpe = dkpe + dk_ref[:, hs]
        lane = lax.broadcasted_iota(jnp.int32, (tm, 128), 1)
        rope_lane = (lane >= MLA_NOPE) & (lane < MLA_NOPE + MLA_ROPE)
        dg3 = jnp.where(rope_lane, _rope_t(jnp.where(rope_lane, dkpe, 0.0), cs, sn, 16, 32), 0.0)

        def norm_bwd(x, w, dn):
            r = lax.rsqrt(jnp.mean(x * x, axis=-1, keepdims=True) + EPS)
            xh = x * r
            g = dn * w
            return r * (g - xh * jnp.mean(g * xh, axis=-1, keepdims=True)), jnp.sum(dn * xh, axis=0, keepdims=True)

        dcqn = _dot(dqp_ref[...], wuq_ref[...])
        dcq, dqnw = norm_bwd(lat_ref[:, 0:256], qnw_ref[...], dcqn)
        dckvn = _dot(dk_ref[...].astype(BF16), wk_ref[...]) + _dot(dv_ref[...], wv_ref[...])
        dckv, dkvnw = norm_bwd(lat_ref[:, 256:384], kvnw_ref[...], dckvn)
        dqnw_ref[...] += dqnw
        dkvnw_ref[...] += dkvnw
        dlat_ref[:, 0:256] = dcq.astype(BF16)
        dlat_ref[:, 256:384] = dckv.astype(BF16)
        dlat_ref[:, 384:512] = dg3.astype(BF16)

    def full(shape):
        return pl.BlockSpec(shape, lambda i: (0, 0))

    def row(w):
        return pl.BlockSpec((tm, w), lambda i: (i, 0))

    return pl.pallas_call(
        body, name=name, grid=(T // tm,),
        in_specs=[row(1024), row(1024), row(512), pl.BlockSpec((tm, 512), lambda i: (i, 4)), full((1, 256)),
                  full((1, 128)), full((1024, 256)), full((1024, 128)), full((512, 128)), row(128), row(128)],
        out_specs=[row(512), row(1024), full((1, 256)), full((1, 128))],
        out_shape=[jax.ShapeDtypeStruct((T, 512), BF16), jax.ShapeDtypeStruct((T, 1024), BF16),
                   jax.ShapeDtypeStruct((1, 256), F32), jax.ShapeDtypeStruct((1, 128), F32)],
        compiler_params=_cp(("arbitrary",)))(dq, dk, dv, proj, qnw, kvnw, wuq_t, wk_t, wv_t, cos, ss)


def _flash_fwd(q, k, v1, *, name, gather=None):
    T = q.shape[0]
    tq = min(T, 512)
    tk = tq
    nq = T // tq

    def body(q_ref, k_ref, v_ref, *rest):
        if gather is None:
            y_ref, lse_ref = rest
        else:
            x_ref, y_ref, lse_ref, g_ref, *sems = rest
            start, forward, finish = _gather_phases(x_ref, g_ref, *sems)
            pl.when((pl.program_id(0) == 0) & (pl.program_id(1) == 0))(start)
            pl.when((pl.program_id(0) == 1) & (pl.program_id(1) == 0))(forward)
        attend(q_ref, k_ref, v_ref, y_ref, lse_ref)
        if gather is not None:
            pl.when((pl.program_id(0) == 3) & (pl.program_id(1) == nq - 1))(finish)

    def attend(q_ref, k_ref, v_ref, y_ref, lse_ref):
        qi = pl.program_id(1)
        row = lax.broadcasted_iota(jnp.int32, (tq, tk), 0)
        col = lax.broadcasted_iota(jnp.int32, (tq, tk), 1)

        def step(kb, carry, masked):
            ks = pl.ds(pl.multiple_of(kb * tk, tk), tk)
            new = []
            for h in range(2):
                hs = slice(128 * h, 128 * h + 128)
                m, acc = carry[h]
                s = _dot_nt(q_ref[:, hs], k_ref[ks, hs])
                if masked:
                    s = jnp.where(col <= row, s, NEG)
                mn = jnp.maximum(m, jnp.max(s, axis=1, keepdims=True))
                p = jnp.exp((s - mn).astype(BF16))
                acc = jnp.exp(m - mn) * acc + _dot(p, v_ref[ks, hs])
                new.append((mn, acc))
            return tuple(new)

        def unrolled(j, c):
            for u in range(FLASH_UNROLL):
                c = step(FLASH_UNROLL * j + u, c, False)
            return c

        init = (jnp.full((tq, 1), NEG, F32), jnp.zeros((tq, 128), F32))
        carry = lax.fori_loop(0, qi // FLASH_UNROLL, unrolled, (init, init))
        carry = lax.fori_loop(FLASH_UNROLL * (qi // FLASH_UNROLL), qi, lambda kb, c: step(kb, c, False), carry)
        (ma, acca), (mb, accb) = step(qi, carry, True)
        lane = lax.broadcasted_iota(jnp.int32, (tq, 128), 1)
        la, lb = pltpu.roll(acca, 64, 1), pltpu.roll(accb, 64, 1)
        y_ref[...] = jnp.where(lane < 64, acca / la, accb / lb).astype(BF16)
        lse_ref[0] = ma + jnp.log(acca[:, 64:65])
        lse_ref[1] = mb + jnp.log(accb[:, 0:1])

    in_specs = [pl.BlockSpec((tq, 256), lambda p, i: (i, p)), pl.BlockSpec((T, 256), lambda p, i: (0, p)),
                pl.BlockSpec((T, 256), lambda p, i: (0, p))]
    out_specs = [pl.BlockSpec((tq, 128), lambda p, i: (i, p)), pl.BlockSpec((2, tq, 1), lambda p, i: (p, i, 0))]
    out_shape = [jax.ShapeDtypeStruct((T, MLA_WIDTH), BF16), jax.ShapeDtypeStruct((MLA_HEADS, T, 1), F32)]
    if gather is None:
        return pl.pallas_call(body, name=name, grid=(4, nq), in_specs=in_specs, out_specs=out_specs,
                              out_shape=out_shape, compiler_params=_cp(("parallel", "arbitrary")))(q, k, v1)
    return pl.pallas_call(
        body, name=name, grid=(4, nq), in_specs=in_specs + [ANY], out_specs=out_specs + [ANY],
        out_shape=out_shape + [jax.ShapeDtypeStruct((N_DEV,) + gather.shape, gather.dtype)],
        scratch_shapes=list(GATHER_SCRATCH),
        compiler_params=_cp(("arbitrary", "arbitrary")))(q, k, v1, gather)


def _flash_bwd(q, k, v, do, lse, delta, *, name, exchange=None):
    T = q.shape[0]
    tq = min(T, 512)
    tk = tq
    nq = T // tq

    def body(q_ref, k_ref, v_ref, do_ref, lse_ref, dl_ref, *rest):
        if exchange is None:
            backward(q_ref, k_ref, v_ref, do_ref, lse_ref, dl_ref, *rest)
        else:
            p_ref, dqt_ref, dk_ref, dv_ref, got_ref, *sems = rest
            start, finish = _exchange_phases(p_ref, got_ref, *sems)
            pl.when((pl.program_id(0) == 0) & (pl.program_id(1) == 0))(start)
            backward(q_ref, k_ref, v_ref, do_ref, lse_ref, dl_ref, dqt_ref, dk_ref, dv_ref)
            pl.when((pl.program_id(0) == 3) & (pl.program_id(1) == nq - 1))(finish)

    def backward(q_ref, k_ref, v_ref, do_ref, lse_ref, dl_ref, dqt_ref, dk_ref, dv_ref):
        kb = pl.program_id(1)

        @pl.when(kb == 0)
        def _():
            dqt_ref[...] = jnp.zeros_like(dqt_ref)
        krow = lax.broadcasted_iota(jnp.int32, (tk, tq), 0)
        qcol = lax.broadcasted_iota(jnp.int32, (tk, tq), 1)
        masks = _head_masks((tk, 128))
        vms = [(v_ref[:, 128 * h:128 * h + 128].astype(F32) * masks[h]).astype(BF16) for h in range(2)]

        def step(qi, carry, masked):
            qs = pl.ds(pl.multiple_of(qi * tq, tq), tq)
            dob = do_ref[qs, :]
            dof = dob.astype(F32)
            dks, dv_acc = list(carry[:2]), carry[2]
            for h in range(2):
                hs = slice(128 * h, 128 * h + 128)
                kh = k_ref[:, hs]
                qh = q_ref[qs, hs]
                st = _dot_nt(kh, qh)
                pt = jnp.exp((st - lse_ref[h, qi]).astype(BF16))
                if masked:
                    pt = jnp.where(krow <= qcol, pt, jnp.zeros_like(pt))
                dv_acc = dv_acc + _dot(pt, (dof * masks[h]).astype(BF16))
                dpt = _dot_nt(vms[h], dob)
                dst = pt * (dpt - dl_ref[h, qi]).astype(BF16)
                dks[h] = dks[h] + _dot(dst, qh)
                dqt_ref[qi, hs, :] += _dot_tn(kh, dst)
            return dks[0], dks[1], dv_acc

        zero = jnp.zeros((tk, 128), F32)
        carry = step(kb, (zero, zero, zero), True)

        def two_steps(j, c):
            qi = kb + 1 + 2 * j
            return step(qi + 1, step(qi, c, False), False)

        pairs = (nq - 1 - kb) // 2
        carry = lax.fori_loop(0, pairs, two_steps, carry)
        dk0, dk1, dv_acc = lax.fori_loop(kb + 1 + 2 * pairs, nq, lambda qi, c: step(qi, c, False), carry)
        dk_ref[:, 0:128] = dk0
        dk_ref[:, 128:256] = dk1
        dv_ref[...] = dv_acc.astype(BF16)

    stat = pl.BlockSpec((2, nq, 1, tq), lambda p, j: (p, 0, 0, 0))
    in_specs = [pl.BlockSpec((T, 256), lambda p, j: (0, p)), pl.BlockSpec((tk, 256), lambda p, j: (j, p)),
                pl.BlockSpec((tk, 256), lambda p, j: (j, p)), pl.BlockSpec((T, 128), lambda p, j: (0, p)), stat, stat]
    out_specs = [pl.BlockSpec((None, nq, 256, tq), lambda p, j: (p, 0, 0, 0)),
                 pl.BlockSpec((tk, 256), lambda p, j: (j, p)), pl.BlockSpec((tk, 128), lambda p, j: (j, p))]
    out_shape = [jax.ShapeDtypeStruct((4, nq, 256, tq), F32), jax.ShapeDtypeStruct((T, 1024), F32),
                 jax.ShapeDtypeStruct((T, MLA_WIDTH), BF16)]
    if exchange is None:
        return pl.pallas_call(body, name=name, grid=(4, nq), in_specs=in_specs, out_specs=out_specs,
                              out_shape=out_shape,
                              compiler_params=_cp(("parallel", "arbitrary")))(q, k, v, do, lse, delta)
    return pl.pallas_call(
        body, name=name, grid=(4, nq), in_specs=in_specs + [ANY], out_specs=out_specs + [ANY],
        out_shape=out_shape + [jax.ShapeDtypeStruct(exchange.shape, exchange.dtype)],
        scratch_shapes=list(EXCHANGE_SCRATCH),
        compiler_params=_cp(("arbitrary", "arbitrary")))(q, k, v, do, lse, delta, exchange)


def _shift_down(x, n, prev8):
    r = pltpu.roll(x, n, 0)
    row = lax.broadcasted_iota(jnp.int32, prev8.shape, 0)
    first = jnp.where(row < n, pltpu.roll(prev8, n, 0), r[:8])
    if x.shape[0] == 8:
        return first
    return jnp.concatenate([first, r[8:]], axis=0)


def _shift_up(x, n, next8):
    tm = x.shape[0]
    r = pltpu.roll(x, tm - n, 0)
    row = lax.broadcasted_iota(jnp.int32, next8.shape, 0)
    last = jnp.where(row >= 8 - n, pltpu.roll(next8, 8 - n, 0), r[tm - 8:])
    return jnp.concatenate([r[:tm - 8], last], axis=0)


def _conv_pre(u, prev8, cw_ref, cb_ref):
    p1 = _shift_down(u, 1, prev8)
    p2 = _shift_down(u, 2, prev8)
    up = cb_ref[...] + cw_ref[0:1, :] * p2 + cw_ref[1:2, :] * p1 + cw_ref[2:3, :] * u
    return up, p1, p2


def _conv_fwd(u, cw, cb, *, name):
    T = u.shape[0]
    tm = min(T, 512)
    W = 2 * FF_HALF

    def body(u_ref, prev_ref, cw_ref, cb_ref, a_ref):
        prev = jnp.where(pl.program_id(0) > 0, prev_ref[...], 0.0)
        up, _, _ = _conv_pre(u_ref[...], prev, cw_ref, cb_ref)
        gate = up[:, :FF_HALF]
        a_ref[...] = (gate * _sigmoid(gate) * up[:, FF_HALF:]).astype(BF16)

    return pl.pallas_call(
        body, name=name, grid=(T // tm, 2),
        in_specs=[pl.BlockSpec((tm, W), lambda i, j: (i, j)),
                  pl.BlockSpec((8, W), lambda i, j: (jnp.maximum(i * (tm // 8) - 1, 0), j)),
                  pl.BlockSpec((3, W), lambda i, j: (0, j)), pl.BlockSpec((1, W), lambda i, j: (0, j))],
        out_specs=pl.BlockSpec((tm, FF_HALF), lambda i, j: (i, j)),
        out_shape=jax.ShapeDtypeStruct((T, D_FF), BF16),
        compiler_params=_cp(("parallel", "parallel")))(u, u, cw, cb)


def _conv_bwd(u, da, cw, cb, *, name):
    T = u.shape[0]
    tm = min(T, 512)
    W = 2 * FF_HALF
    nt = T // tm

    def body(u_ref, prev_ref, next_ref, da_ref, dan_ref, cw_ref, cb_ref, du_ref, dw0_ref, dw1_ref, dw2_ref, db_ref):
        i = pl.program_id(1)

        @pl.when(i == 0)
        def _():
            for r in (dw0_ref, dw1_ref, dw2_ref, db_ref):
                r[...] = jnp.zeros_like(r)

        def dpre(u, prev8, da):
            up, p1, p2 = _conv_pre(u, prev8, cw_ref, cb_ref)
            gate, val = up[:, :FF_HALF], up[:, FF_HALF:]
            sg = _sigmoid(gate)
            dgate = da * val * (sg * (1.0 + gate * (1.0 - sg)))
            dval = da * (gate * sg)
            return jnp.concatenate([dgate, dval], axis=1), p1, p2

        u = u_ref[...]
        prev = jnp.where(i > 0, prev_ref[...], 0.0)
        dup, p1, p2 = dpre(u, prev, da_ref[...])
        dupn, _, _ = dpre(next_ref[...], u[tm - 8:], dan_ref[...])
        dupn = jnp.where(i < nt - 1, dupn, 0.0)
        du = cw_ref[2:3, :] * dup + cw_ref[1:2, :] * _shift_up(dup, 1, dupn) + cw_ref[0:1, :] * _shift_up(dup, 2, dupn)
        du_ref[...] = du.astype(BF16)
        dw0_ref[...] += jnp.sum(dup * p2, axis=0, keepdims=True)
        dw1_ref[...] += jnp.sum(dup * p1, axis=0, keepdims=True)
        dw2_ref[...] += jnp.sum(dup * u, axis=0, keepdims=True)
        db_ref[...] += jnp.sum(dup, axis=0, keepdims=True)

    nxt = lambda j, i: (jnp.minimum((i + 1) * (tm // 8), T // 8 - 1), j)
    vec = pl.BlockSpec((1, W), lambda j, i: (0, j))
    return pl.pallas_call(
        body, name=name, grid=(2, nt),
        in_specs=[pl.BlockSpec((tm, W), lambda j, i: (i, j)),
                  pl.BlockSpec((8, W), lambda j, i: (jnp.maximum(i * (tm // 8) - 1, 0), j)),
                  pl.BlockSpec((8, W), nxt),
                  pl.BlockSpec((tm, FF_HALF), lambda j, i: (i, j)), pl.BlockSpec((8, FF_HALF), nxt),
                  pl.BlockSpec((3, W), lambda j, i: (0, j)), vec],
        out_specs=[pl.BlockSpec((tm, W), lambda j, i: (i, j)), vec, vec, vec, vec],
        out_shape=[jax.ShapeDtypeStruct((T, 2 * D_FF), BF16)] + [jax.ShapeDtypeStruct((1, 2 * D_FF), F32)] * 4,
        compiler_params=_cp(("parallel", "arbitrary")))(u, u, u, da, da, cw, cb)


def _adamw(w, m, v, g_slots, *, name):
    R, C = w.shape
    ns = g_slots.shape[0]
    tr = _row_tile(R)

    def body(w_ref, m_ref, v_ref, g_ref, go_ref, d_ref, mo_ref, vo_ref):
        g = g_ref[0].astype(F32)
        for s in range(1, ns):
            g = g + g_ref[s].astype(F32)
        mn = ADAM_B1 * m_ref[...] + (1.0 - ADAM_B1) * g
        vn = ADAM_B2 * v_ref[...] + (1.0 - ADAM_B2) * (g * g)
        m_hat = mn / (1.0 - ADAM_B1 ** ADAM_STEP)
        v_hat = vn / (1.0 - ADAM_B2 ** ADAM_STEP)
        go_ref[...] = g
        d_ref[...] = -ADAM_LR * (m_hat / (jnp.sqrt(v_hat) + ADAM_EPS) + ADAM_WD * w_ref[...])
        mo_ref[...] = mn
        vo_ref[...] = vn

    blk = pl.BlockSpec((tr, C), lambda i: (i, 0))
    return pl.pallas_call(
        body, name=name, grid=(R // tr,),
        in_specs=[blk, blk, blk, pl.BlockSpec((ns, tr, C), lambda i: (0, i, 0))],
        out_specs=[blk] * 4, out_shape=[jax.ShapeDtypeStruct((R, C), F32)] * 4,
        compiler_params=_cp(("parallel",)))(w, m, v, g_slots)


def _place():
    return lax.axis_index("x"), lax.axis_index("y"), lax.axis_index("c")


GATHER_SCRATCH = (pltpu.SemaphoreType.DMA((7,)), pltpu.SemaphoreType.DMA((7,)), pltpu.SemaphoreType.DMA)
EXCHANGE_SCRATCH = (pltpu.SemaphoreType.DMA((3,)), pltpu.SemaphoreType.DMA((3,)), pltpu.SemaphoreType.DMA)


def _gather_phases(x_ref, out_ref, send_sems, recv_sems, local_sem):
    x_, y_, c_ = _place()
    me, sibling = (x_, y_, c_), (x_, y_, 1 - c_)
    chips = [(1 - x_, y_), (x_, 1 - y_), (1 - x_, 1 - y_)]

    def slot(px, py, pc):
        return out_ref.at[4 * px + 2 * py + pc]

    def copy(k, block, to, src=None):
        return pltpu.make_async_remote_copy(
            src_ref=slot(*block) if src is None else src, dst_ref=slot(*block),
            send_sem=send_sems.at[k], recv_sem=recv_sems.at[k], device_id=to, device_id_type=MESH)

    def mine():
        return pltpu.make_async_copy(x_ref, slot(*me), local_sem)

    def first():
        return [copy(0, me, sibling, src=x_ref)] + [copy(1 + j, me, (*chip, c_), src=x_ref)
                                                     for j, chip in enumerate(chips)]

    def passed():
        return [copy(4 + j, (*chip, c_), sibling) for j, chip in enumerate(chips)]

    def start():
        mine().start()
        for cp in first():
            cp.start()

    def forward():
        fwd = passed()
        for j, chip in enumerate(chips):
            copy(1 + j, (*chip, c_), me).wait_recv()
            fwd[j].start()

    def finish():
        copy(0, sibling, me).wait_recv()
        for j, chip in enumerate(chips):
            copy(4 + j, (*chip, 1 - c_), me).wait_recv()
        for cp in first() + passed():
            cp.wait_send()
        mine().wait()

    return start, forward, finish


def _exchange_phases(p_ref, out_ref, send_sems, recv_sems, local_sem):
    x_, y_, c_ = _place()
    me_k = 2 * x_ + y_
    chips = [(1 - x_, y_), (x_, 1 - y_), (1 - x_, 1 - y_)]

    def local():
        return pltpu.make_async_copy(p_ref.at[me_k], out_ref.at[me_k], local_sem)

    def copy(j, src_k, dst_k, chip):
        return pltpu.make_async_remote_copy(
            src_ref=p_ref.at[src_k], dst_ref=out_ref.at[dst_k], send_sem=send_sems.at[j],
            recv_sem=recv_sems.at[j], device_id=(*chip, c_), device_id_type=MESH)

    def sends():
        return [copy(j, 2 * px + py, me_k, (px, py)) for j, (px, py) in enumerate(chips)]

    def start():
        local().start()
        for cp in sends():
            cp.start()

    def finish():
        for j, (px, py) in enumerate(chips):
            copy(j, me_k, 2 * px + py, (px, py)).wait_recv()
        for cp in sends():
            cp.wait_send()
        local().wait()

    return start, finish


def _all_gather(x, *, name, in_vmem):
    def body(x_ref, out_ref, send_sems, recv_sems, local_sem):
        for phase in _gather_phases(x_ref, out_ref, send_sems, recv_sems, local_sem):
            phase()

    spec = pl.BlockSpec(memory_space=pltpu.VMEM) if in_vmem else ANY
    return pl.pallas_call(
        body, name=name, out_shape=jax.ShapeDtypeStruct((N_DEV,) + x.shape, x.dtype),
        in_specs=[spec], out_specs=spec, scratch_shapes=list(GATHER_SCRATCH),
        compiler_params=pltpu.CompilerParams(vmem_limit_bytes=VMEM_LIMIT))(x)


def _sum_slots(g, *, name):
    n = g.shape[0]

    def body(g_ref, o_ref):
        acc = g_ref[0]
        for s in range(1, n):
            acc = acc + g_ref[s]
        o_ref[...] = acc

    return pl.pallas_call(body, name=name, out_shape=jax.ShapeDtypeStruct(g.shape[1:], g.dtype))(g)


SWAP_SCRATCH = (pltpu.SemaphoreType.DMA((4,)), pltpu.SemaphoreType.DMA((4,)))


def _swap_phases(g_ref, out_ref, send_sems, recv_sems):
    x_, y_, c_ = _place()

    def copies():
        return [pltpu.make_async_remote_copy(src_ref=g_ref.at[k, 1 - c_], dst_ref=out_ref.at[k],
                                             send_sem=send_sems.at[k], recv_sem=recv_sems.at[k],
                                             device_id=(x_, y_, 1 - c_), device_id_type=MESH) for k in range(4)]

    def start():
        for cp in copies():
            cp.start()

    def finish():
        for cp in copies():
            cp.wait()

    return start, finish


def _swap_sibling(g, *, name):
    def body(g_ref, out_ref, send_sems, recv_sems):
        for phase in _swap_phases(g_ref, out_ref, send_sems, recv_sems):
            phase()

    return pl.pallas_call(
        body, name=name, out_shape=jax.ShapeDtypeStruct((4,) + g.shape[2:], g.dtype), in_specs=[ANY], out_specs=ANY,
        scratch_shapes=list(SWAP_SCRATCH))(g)


def _row_tile(R):
    for cand in (256, 400, 200):
        if R % cand == 0:
            return cand
    return R


def _add_own(g, b, *, name, out_dtype):
    n, _, R, C = g.shape
    tr = _row_tile(R)

    def body(c_ref, g_ref, b_ref, o_ref):
        del c_ref
        o_ref[...] = (g_ref[...] + b_ref[...]).astype(out_dtype)

    blk = pl.BlockSpec((None, tr, C), lambda s, i, c: (s, i, 0))
    grid_spec = pltpu.PrefetchScalarGridSpec(
        num_scalar_prefetch=1, grid=(n, R // tr),
        in_specs=[pl.BlockSpec((None, None, tr, C), lambda s, i, c: (s, c[0], i, 0)), blk], out_specs=blk)
    core = jnp.reshape(lax.axis_index("c"), (1,)).astype(jnp.int32)
    return pl.pallas_call(body, name=name, grid_spec=grid_spec, out_shape=jax.ShapeDtypeStruct(b.shape, out_dtype),
                          compiler_params=_cp(("parallel", "parallel")))(core, g, b)


def _pack_local(parts, group):
    table, rows = group
    segs = []
    for n, r, rp, tr in table:
        w = parts[n].T if tr else parts[n]
        segs.append(jnp.pad(w.reshape(r, PACK_COLS), ((0, rp - r), (0, 0))))
    segs.append(jnp.zeros((rows - sum(rp for _, _, rp, _ in table), PACK_COLS), segs[0].dtype))
    return jnp.concatenate(segs, axis=0)


def _unpack_local(packed, like, group):
    out, off = {}, 0
    for n, r, rp, tr in group[0]:
        rows, cols = like[n].shape
        seg = packed[off:off + r]
        out[n] = (seg.reshape(cols, rows).T if tr else seg)[None]
        off += rp
    return out


def _segments(g, group):
    out, off = {}, 0
    for n, r, rp, _ in group[0]:
        out[n] = g[:, off:off + r]
        off += rp
    return out


def _pack_grads(parts, group):
    table, rows = group
    segs = [jnp.pad(parts[n], ((0, 0), (0, rp - parts[n].shape[1]), (0, 0))) for n, _, rp, _ in table]
    segs.append(jnp.zeros((N_DEV, rows - sum(rp for _, _, rp, _ in table), PACK_COLS), F32))
    return jnp.concatenate(segs, axis=1)


def _owner_rows_early(g):
    g_in = jnp.concatenate([g["w_in_t"][:2432], g["w_in_t"][2496:2528]], axis=0).reshape(N_DEV, 308, PACK_COLS)
    g_uq = g["w_uq_t"].reshape(N_DEV, 128, MLA_Q_RANK)[:, :96].reshape(N_DEV, 24, PACK_COLS)
    g_ukv = jnp.concatenate([g["w_k_t"].reshape(N_DEV, 128, MLA_KV_RANK)[:, :64],
                             g["w_v_t"].reshape(N_DEV, 64, MLA_KV_RANK)], axis=1).reshape(N_DEV, 16, PACK_COLS)
    return dict(w_in=g_in, w_uq=g_uq, w_ukv=g_ukv)


def _owner_rows_late(g):
    up = g["w_up_t"].reshape(N_DEV, 704, PACK_COLS)
    g_up = jnp.stack([up[FF_OWNER_ORDER.index(d)] for d in range(N_DEV)])
    return dict(w_out=g["w_out"].reshape(N_DEV, 128, PACK_COLS), w_up=g_up,
                w_down=g["w_down"].reshape(N_DEV, 352, PACK_COLS))


def _reduce_to_pairs(gp, *, name):
    gp = gp.reshape(4, 2, gp.shape[1], PACK_COLS)
    return _add_own(gp, _swap_sibling(gp, name=name + "_swap"), out_dtype=BF16, name=name + "_sum")


def _interleave_ff(w):
    g, v = w[..., :D_FF], w[..., D_FF:]
    return jnp.concatenate([g[..., :FF_HALF], v[..., :FF_HALF], g[..., FF_HALF:], v[..., FF_HALF:]], axis=-1)


def _deinterleave_ff(w):
    b = [w[..., i * FF_HALF:(i + 1) * FF_HALF] for i in range(4)]
    return jnp.concatenate([b[0], b[2], b[1], b[3]], axis=-1)


def _rope_tables(pos):
    p = pos.astype(F32)[:, None]
    inv_r = ROPE_BASE ** (-jnp.arange(0, RET_HEAD_DIM, 2, dtype=F32) / RET_HEAD_DIM)
    ang = p * jnp.tile(inv_r, 4)
    sign_r = jnp.tile(jnp.concatenate([-jnp.ones((32,), F32), jnp.ones((32,), F32)]), 2)
    cos_r, ss_r = jnp.cos(ang), jnp.sin(ang) * sign_r
    inv_m = ROPE_BASE ** (-jnp.arange(0, MLA_ROPE, 2, dtype=F32) / MLA_ROPE)
    ang = p * jnp.concatenate([jnp.zeros((64,), F32), inv_m, inv_m, jnp.zeros((32,), F32)])
    sign_m = jnp.concatenate([jnp.zeros((64,), F32), -jnp.ones((16,), F32), jnp.ones((16,), F32), jnp.zeros((32,), F32)])
    cos_m, ss_m = jnp.cos(ang), jnp.sin(ang) * sign_m
    return cos_r, ss_r, cos_m, ss_m


def _prep_early(gathered):
    seg = _segments(gathered, EARLY)
    w_in_t = seg["w_in"].reshape(IN_WIDTH, D_MODEL)
    z = lambda n: jnp.zeros((n, D_MODEL), BF16)
    w_in_t = jnp.concatenate([w_in_t[:2432], z(64), w_in_t[2432:2464], z(32)], axis=0)
    w_uq_t = jnp.pad(seg["w_uq"].reshape(MLA_HEADS, 96, MLA_Q_RANK), ((0, 0), (0, 32), (0, 0))).reshape(1024, MLA_Q_RANK)
    ukv = seg["w_ukv"].reshape(MLA_HEADS, 128, MLA_KV_RANK)
    w_k_t = jnp.pad(ukv[:, :64], ((0, 0), (0, 64), (0, 0))).reshape(1024, MLA_KV_RANK)
    w_v_t = ukv[:, 64:].reshape(512, MLA_KV_RANK)
    return dict(w_in_t=w_in_t, w_uq_t=w_uq_t, w_k_t=w_k_t, w_v_t=w_v_t)


def _prep_late(gathered):
    seg = _segments(gathered, LATE)
    w_up_t = jnp.concatenate([seg["w_up"][d] for d in FF_OWNER_ORDER], axis=0)
    return dict(w_out=seg["w_out"].reshape(1024, D_MODEL), w_up_t=w_up_t, w_down=seg["w_down"].reshape(D_FF, D_MODEL))


def _local_step(x, pos, tgt, early, sm, late):
    dist = not isinstance(late, dict)
    cos_r, ss_r, cos_m, ss_m = _rope_tables(pos)
    tabs = _ret_tables()

    if dist:
        h, gathered = _rmsnorm_fwd(x, sm["attn_norm_w"], gather=early, name="attn_norm")
        W = _prep_early(gathered)
    else:
        h = _rmsnorm_fwd(x, sm["attn_norm_w"], name="attn_norm")
        W = early
    proj = _mm(h, W["w_in_t"], bt=True, name="in_proj")
    y_ret, o_ret = _ret_fwd(proj, cos_r, ss_r, tabs, sm["ret_gn_w"], name="ret_fwd")
    q, k, v1, cqn, ckvn = _mla_prep_fwd(proj, sm["mla_q_norm_w"], sm["mla_kv_norm_w"], W["w_uq_t"], W["w_k_t"],
                                       W["w_v_t"], cos_m, ss_m, name="mla_prep")
    T = x.shape[0]
    tq = min(T, 512)
    if dist:
        y_mla, lse, gathered = _flash_fwd(q, k, v1, gather=late, name="mla_attn")
        W = {**W, **_prep_late(gathered)}
    else:
        y_mla, lse = _flash_fwd(q, k, v1, name="mla_attn")
        W = {**W, **late}
    lse = lse.reshape(MLA_HEADS, T // tq, 1, tq)
    mixed = jnp.concatenate([y_ret, y_mla], axis=1)
    x1 = _mm(mixed, W["w_out"], add=x, name="out_proj")
    h2 = _rmsnorm_fwd(x1, sm["ffn_norm_w"], name="ffn_norm")
    u = _mm(h2, W["w_up_t"], bt=True, name="up_proj")
    a = _conv_fwd(u, sm["conv_w"], sm["conv_b"], name="conv_gate")
    x2 = _mm(a, W["w_down"], add=x1, name="down_proj")
    loss, dx2, dx2b, d_final = _loss_head(x2, tgt, sm["final_norm_w"], name="loss_head")

    g = {}
    g["w_down"] = _mm_tn(a, dx2b, name="dw_down")
    da = _mm(dx2b, W["w_down"], bt=True, name="d_act")
    du, dcw0, dcw1, dcw2, dcb = _conv_bwd(u, da, sm["conv_w"], sm["conv_b"], name="conv_bwd")
    g["w_up_t"] = _mm_tn(du, h2, name="dw_up")
    dh2 = _mm(du, W["w_up_t"], name="d_h2")
    dx1, d_ffn = _rmsnorm_bwd(x1, sm["ffn_norm_w"], dh2, dx2, name="ffn_norm_bwd")

    g["w_out"] = _mm_tn(mixed, dx1, name="dw_out")
    dmixed = _mm(dx1, W["w_out"], bt=True, name="d_mixed")
    do_ret, dg, do_mla, delta, d_gn = _mix_bwd(dmixed, o_ret, proj, y_mla, sm["ret_gn_w"], name="mix_bwd")
    drq = _ret_bwd_dq(proj, do_ret, cos_r, ss_r, tabs, name="ret_bwd_dq")
    delta_r = delta[:, :MLA_HEADS].T.reshape(MLA_HEADS, T // tq, 1, tq)
    if dist:
        gl = _pack_grads(_owner_rows_late(g), LATE).reshape(4, 2, LATE[1], PACK_COLS)
        drk, drv, theirs = _ret_bwd_dkv(proj, do_ret, cos_r, ss_r, tabs, swap=gl, name="ret_bwd_dkv")
        pair = _add_own(gl, theirs, out_dtype=BF16, name="grad_late_sum")
        dqt, dk, dv, slots_late = _flash_bwd(q, k, v1, do_mla, lse, delta_r, exchange=pair, name="mla_attn_bwd")
    else:
        drk, drv = _ret_bwd_dkv(proj, do_ret, cos_r, ss_r, tabs, name="ret_bwd_dkv")
        dqt, dk, dv = _flash_bwd(q, k, v1, do_mla, lse, delta_r, name="mla_attn_bwd")
        slots_late = None
    dq = dqt.transpose(1, 3, 0, 2).reshape(T, MLA_HEADS * 128)
    dlat, dqp, d_qn, d_kvn = _mla_prep_bwd(dq, dk, dv, proj, sm["mla_q_norm_w"], sm["mla_kv_norm_w"], W["w_uq_t"],
                                           W["w_k_t"], W["w_v_t"], cos_m, ss_m, name="mla_prep_bwd")
    g["w_uq_t"] = _mm_tn(dqp, cqn, name="dw_uq")
    g["w_k_t"] = _mm_tn(dk, ckvn, name="dw_ukv_k")
    g["w_v_t"] = _mm_tn(dv, ckvn, name="dw_ukv_v")
    dproj = jnp.concatenate([drq, drk, drv, dg, dlat], axis=1)
    g["w_in_t"] = _mm_tn(dproj, h, name="dw_in")
    if dist:
        pair = _reduce_to_pairs(_pack_grads(_owner_rows_early(g), EARLY), name="grad_early")
        dh, slots_early = _mm(dproj, W["w_in_t"], exchange=pair, name="d_h")
    else:
        dh = _mm(dproj, W["w_in_t"], name="d_h")
        slots_early = None
    grad_x, d_attn = _rmsnorm_bwd(x, sm["attn_norm_w"], dh, dx1, name="attn_norm_bwd")

    small = dict(attn_norm_w=d_attn, ret_gn_w=d_gn, mla_q_norm_w=d_qn, mla_kv_norm_w=d_kvn, ffn_norm_w=d_ffn,
                 conv_b=_deinterleave_ff(dcb), final_norm_w=d_final,
                 conv_w=_deinterleave_ff(jnp.concatenate([dcw0, dcw1, dcw2], axis=0)))
    return loss, grad_x, g, small, slots_early, slots_late


def kernel(x, positions, attn_norm_w, w_in, ret_gn_w, mla_q_norm_w, w_uq, mla_kv_norm_w, w_ukv, w_out, ffn_norm_w, w_up, conv_w, conv_b, w_down, final_norm_w, loss_target, m_attn_norm_w, m_w_in, m_ret_gn_w, m_mla_q_norm_w, m_w_uq, m_mla_kv_norm_w, m_w_ukv, m_w_out, m_ffn_norm_w, m_w_up, m_conv_w, m_conv_b, m_w_down, m_final_norm_w, v_attn_norm_w, v_w_in, v_ret_gn_w, v_mla_q_norm_w, v_w_uq, v_mla_kv_norm_w, v_w_ukv, v_w_out, v_ffn_norm_w, v_w_up, v_conv_w, v_conv_b, v_w_down, v_final_norm_w):
    a = dict(locals())
    x_, y_, c_ = _place()
    dev = 4 * x_ + 2 * y_ + c_

    shard = {n: a[n][0] for n in BIG_NAMES}
    shard16 = {n: w.astype(BF16) for n, w in shard.items()}
    cw_pad = jnp.pad(conv_w[0].reshape(-1), (0, 24 * 128 - 3 * 704)).reshape(24, 128)
    cw_all = _all_gather(cw_pad, name="gather_conv_w", in_vmem=True)
    conv_w_full = cw_all.reshape(N_DEV, -1)[:, :3 * 704].reshape(N_DEV, 3, 704).transpose(1, 0, 2).reshape(3, 2 * D_FF)
    sm = dict(attn_norm_w=attn_norm_w, ret_gn_w=ret_gn_w, mla_q_norm_w=mla_q_norm_w, mla_kv_norm_w=mla_kv_norm_w,
              ffn_norm_w=ffn_norm_w, final_norm_w=final_norm_w.reshape(1, D_MODEL),
              conv_w=_interleave_ff(conv_w_full), conv_b=_interleave_ff(conv_b))

    loss, grad_x, _, gs, slots_early, slots_late = _local_step(
        x[0], positions[0], loss_target[0], _pack_local(shard16, EARLY), sm, _pack_local(shard16, LATE))

    big = [{}, {}, {}, {}]
    for group, slots, tag in ((EARLY, slots_early, "early"), (LATE, slots_late, "late")):
        names_g = [n for n, _, _, _ in group[0]]
        res = _adamw(_pack_local({n: shard[n] for n in names_g}, group),
                     _pack_local({n: a["m_" + n][0] for n in names_g}, group),
                     _pack_local({n: a["v_" + n][0] for n in names_g}, group), slots, name="adamw_" + tag)
        for kind in range(4):
            big[kind].update(_unpack_local(res[kind], shard, group))

    vec = jnp.concatenate([gs[n].reshape(-1) for n, _ in SMALL] + [gs["conv_w"].reshape(-1), loss[0, :1]])
    vec = jnp.pad(vec, (0, SMALL_ROWS * 128 - vec.shape[0])).reshape(SMALL_ROWS, 128)
    tot = _sum_slots(_all_gather(vec, name="gather_small_grads", in_vmem=True), name="sum_small_grads").reshape(-1)
    loss_out = tot[SMALL_N + 3 * 2 * D_FF]
    g_cw = lax.dynamic_slice_in_dim(tot[SMALL_N:SMALL_N + 3 * 2 * D_FF].reshape(3, 2 * D_FF), dev * 704, 704, axis=1)

    def flat_small(prefix):
        return jnp.concatenate([a[prefix + n].reshape(-1) for n, _ in SMALL]).reshape(75, 128)

    sml = _adamw(flat_small(""), flat_small("m_"), flat_small("v_"), tot[:SMALL_N].reshape(1, 75, 128), name="adamw_small")
    cwo = _adamw(conv_w[0], m_conv_w[0], v_conv_w[0], g_cw[None], name="adamw_conv_w")

    def small_of(t, n):
        off = 0
        for nm, sz in SMALL:
            if nm == n:
                return t.reshape(-1)[off:off + sz].reshape(a[n].shape)
            off += sz

    names = ['attn_norm_w', 'w_in', 'ret_gn_w', 'mla_q_norm_w', 'w_uq', 'mla_kv_norm_w', 'w_ukv', 'w_out',
             'ffn_norm_w', 'w_up', 'conv_w', 'conv_b', 'w_down', 'final_norm_w']
    outs = [loss_out, grad_x[None]]
    for kind in range(4):
        for n in names:
            if n == "conv_w":
                outs.append(cwo[kind][None])
            elif n in big[kind]:
                outs.append(big[kind][n])
            else:
                outs.append(small_of(sml[kind], n))
    return tuple(outs)
```

```python
import functools

import numpy as np
import jax
import jax.numpy as jnp
from jax import lax
from jax.experimental import pallas as pl
from jax.experimental.pallas import tpu as pltpu

F32 = jnp.float32
BF16 = jnp.bfloat16
MESH = pl.DeviceIdType.MESH
ANY = pl.BlockSpec(memory_space=pl.ANY)

D_MODEL = 1024
RET_HEADS = 8
RET_HEAD_DIM = 64
RET_WIDTH = 512
RET_CHUNK = 128
MLA_HEADS = 8
MLA_NOPE = 64
MLA_ROPE = 32
MLA_V = 64
MLA_Q_RANK = 256
MLA_KV_RANK = 128
MLA_WIDTH = 512
IN_WIDTH = 2464
IN_PAD = 2560
D_FF = 2816
FF_HALF = 1408
FF_OWNER_ORDER = (0, 1, 4, 5, 2, 3, 6, 7)
ROPE_BASE = 10000.0
EPS = 1e-6
SCALE = float((MLA_NOPE + MLA_ROPE) ** -0.5)
K_SCALE = 0.125
N_DEV = 8

ADAM_LR = 0.001
ADAM_B1 = 0.9
ADAM_B2 = 0.999
ADAM_EPS = 1e-08
ADAM_WD = 0.01
ADAM_STEP = 10

VMEM_LIMIT = 56 * 1024 * 1024
MM_BUDGET = 40 * 1024 * 1024
NEG = -1e30
FLASH_UNROLL = 4

PACK_COLS = 1024
EARLY = ((("w_in", 308, 320, True), ("w_uq", 24, 32, True), ("w_ukv", 16, 16, True)), 384)
LATE = ((("w_out", 128, 128, False), ("w_up", 704, 704, True), ("w_down", 352, 352, False)), 1200)
BIG_NAMES = ("w_in", "w_uq", "w_ukv", "w_out", "w_up", "w_down")
SMALL = (("attn_norm_w", 1024), ("ret_gn_w", 512), ("mla_q_norm_w", 256), ("mla_kv_norm_w", 128),
         ("ffn_norm_w", 1024), ("conv_b", 5632), ("final_norm_w", 1024))
SMALL_VECTORS = SMALL + (("conv_w0", 5632), ("conv_w1", 5632), ("conv_w2", 5632), ("loss", 128))
SMALL_ROWS = 32


def _cp(sem=None, vmem=VMEM_LIMIT):
    return pltpu.CompilerParams(dimension_semantics=sem, vmem_limit_bytes=vmem)


def _dot(a, b):
    return jnp.dot(a, b, preferred_element_type=F32)


def _dot_nt(a, b):
    return lax.dot_general(a, b, (((1,), (1,)), ((), ())), preferred_element_type=F32)


def _dot_tn(a, b):
    return lax.dot_general(a, b, (((0,), (0,)), ((), ())), preferred_element_type=F32)


def _sigmoid(x):
    return 0.5 * jnp.tanh(0.5 * x) + 0.5


def _partner(x, half, period):
    n = x.shape[-1]
    lane = lax.broadcasted_iota(jnp.int32, x.shape, 1)
    return jnp.where((lane % period) < half, pltpu.roll(x, n - half, 1), pltpu.roll(x, half, 1))


def _rope(x, cos, ss, half, period):
    return x * cos + _partner(x, half, period) * ss


def _rope_t(dy, cos, ss, half, period):
    return dy * cos - _partner(dy, half, period) * ss


def _head_masks(shape):
    lane = lax.broadcasted_iota(jnp.int32, shape, 1)
    m0 = (lane < 64).astype(F32)
    return m0, 1.0 - m0


def _mm(a, b, *, name, add=None, out_dtype=F32, bt=False, exchange=None):
    M, K = a.shape
    N = b.shape[0] if bt else b.shape[1]
    osz = jnp.dtype(out_dtype).itemsize
    per_row = 2 * (K * a.dtype.itemsize + N * osz + (N * 4 if add is not None else 0))
    tm = 128
    for cand in (512, 256):
        if M % cand == 0 and cand * per_row + 4 * K * N <= MM_BUDGET:
            tm = cand
            break
    tm = min(tm, M)
    mul = _dot_nt if bt else _dot
    n_in = 2 if add is None else 3

    def body(*refs):
        a_ref, b_ref = refs[:2]
        acc = mul(a_ref[...].astype(BF16), b_ref[...])
        if add is not None:
            acc = refs[2][...] + acc
        if exchange is None:
            refs[n_in][...] = acc.astype(out_dtype)
        else:
            p_ref, o_ref, got_ref, *sems = refs[n_in:]
            start, finish = _exchange_phases(p_ref, got_ref, *sems)
            pl.when(pl.program_id(0) == 0)(start)
            o_ref[...] = acc.astype(out_dtype)
            pl.when(pl.program_id(0) == M // tm - 1)(finish)

    in_specs = [pl.BlockSpec((tm, K), lambda i: (i, 0)), pl.BlockSpec(b.shape, lambda i: (0, 0))]
    args = [a, b]
    if add is not None:
        in_specs.append(pl.BlockSpec((tm, N), lambda i: (i, 0)))
        args.append(add)
    out_spec = pl.BlockSpec((tm, N), lambda i: (i, 0))
    out_shape = jax.ShapeDtypeStruct((M, N), out_dtype)
    if exchange is None:
        return pl.pallas_call(body, name=name, grid=(M // tm,), in_specs=in_specs, out_specs=out_spec,
                              out_shape=out_shape, compiler_params=_cp(("parallel",)))(*args)
    return pl.pallas_call(
        body, name=name, grid=(M // tm,), in_specs=in_specs + [ANY], out_specs=[out_spec, ANY],
        out_shape=[out_shape, jax.ShapeDtypeStruct(exchange.shape, exchange.dtype)],
        scratch_shapes=list(EXCHANGE_SCRATCH), compiler_params=_cp(("arbitrary",)))(*args, exchange)


def _mm_tn(a, b, *, name):
    T, M = a.shape
    N = b.shape[1]
    tk = min(T, 512)

    def tile(n):
        for cand in (1408, 1280):
            if n > 1408 and n % cand == 0:
                return cand
        return n

    tm, tn = tile(M), tile(N)
    nk = T // tk

    def body(a_ref, b_ref, o_ref):
        @pl.when(pl.program_id(2) == 0)
        def _():
            o_ref[...] = jnp.zeros_like(o_ref)
        o_ref[...] += _dot_tn(a_ref[...].astype(BF16), b_ref[...].astype(BF16))

    return pl.pallas_call(
        body, name=name, grid=(M // tm, N // tn, nk),
        in_specs=[pl.BlockSpec((tk, tm), lambda i, j, k: (k, i)), pl.BlockSpec((tk, tn), lambda i, j, k: (k, j))],
        out_specs=pl.BlockSpec((tm, tn), lambda i, j, k: (i, j)),
        out_shape=jax.ShapeDtypeStruct((M, N), F32),
        compiler_params=_cp(("parallel", "parallel", "arbitrary")))(a, b)


def _rmsnorm_fwd(x, w, *, name, gather=None):
    T, D = x.shape
    tm = min(T, 1024)
    n = T // tm

    def body(x_ref, w_ref, *rest):
        if gather is not None:
            s_ref, o_ref, g_ref, *sems = rest
            start, forward, finish = _gather_phases(s_ref, g_ref, *sems)
            pl.when(pl.program_id(0) == 0)(start)
            pl.when(pl.program_id(0) == n // 2)(forward)
        else:
            o_ref, = rest
        xv = x_ref[...]
        r = lax.rsqrt(jnp.mean(xv * xv, axis=-1, keepdims=True) + EPS)
        o_ref[...] = (xv * r * w_ref[...]).astype(BF16)
        if gather is not None:
            pl.when(pl.program_id(0) == n - 1)(finish)

    in_specs = [pl.BlockSpec((tm, D), lambda i: (i, 0)), pl.BlockSpec((1, D), lambda i: (0, 0))]
    out_spec = pl.BlockSpec((tm, D), lambda i: (i, 0))
    out_shape = jax.ShapeDtypeStruct((T, D), BF16)
    if gather is None:
        return pl.pallas_call(body, name=name, grid=(n,), in_specs=in_specs, out_specs=out_spec, out_shape=out_shape,
                              compiler_params=_cp(("parallel",)))(x, w)
    return pl.pallas_call(
        body, name=name, grid=(n,), in_specs=in_specs + [ANY], out_specs=[out_spec, ANY],
        out_shape=[out_shape, jax.ShapeDtypeStruct((N_DEV,) + gather.shape, gather.dtype)],
        scratch_shapes=list(GATHER_SCRATCH), compiler_params=_cp(("arbitrary",)))(x, w, gather)


def _rmsnorm_bwd(x, w, dh, dres, *, name):
    T, D = x.shape
    tm = min(T, 512)

    def body(x_ref, w_ref, dh_ref, dr_ref, dx_ref, dw_ref):
        @pl.when(pl.program_id(0) == 0)
        def _():
            dw_ref[...] = jnp.zeros_like(dw_ref)
        xv = x_ref[...]
        r = lax.rsqrt(jnp.mean(xv * xv, axis=-1, keepdims=True) + EPS)
        xh = xv * r
        dh = dh_ref[...]
        g = dh * w_ref[...]
        dx_ref[...] = dr_ref[...] + r * (g - xh * jnp.mean(g * xh, axis=-1, keepdims=True))
        dw_ref[...] += jnp.sum(dh * xh, axis=0, keepdims=True)

    row = pl.BlockSpec((tm, D), lambda i: (i, 0))
    vec = pl.BlockSpec((1, D), lambda i: (0, 0))
    return pl.pallas_call(
        body, name=name, grid=(T // tm,), in_specs=[row, vec, row, row], out_specs=[row, vec],
        out_shape=[jax.ShapeDtypeStruct((T, D), F32), jax.ShapeDtypeStruct((1, D), F32)],
        compiler_params=_cp(("arbitrary",)))(x, w, dh, dres)


def _loss_head(x2, tgt, w, *, name):
    T, D = x2.shape
    tm = min(T, 512)

    def body(x_ref, t_ref, w_ref, loss_ref, dx_ref, dxb_ref, dw_ref):
        @pl.when(pl.program_id(0) == 0)
        def _():
            dw_ref[...] = jnp.zeros_like(dw_ref)
            loss_ref[...] = jnp.zeros_like(loss_ref)
        xv = x_ref[...]
        wv = w_ref[...]
        r = lax.rsqrt(jnp.mean(xv * xv, axis=-1, keepdims=True) + EPS)
        xh = xv * r
        e = xh * wv - t_ref[...]
        part = 0.5 * jnp.sum(jnp.mean(e * e, axis=-1, keepdims=True), axis=0, keepdims=True)
        loss_ref[...] += jnp.broadcast_to(part, loss_ref.shape)
        dy = e * (1.0 / D)
        g = dy * wv
        dx = r * (g - xh * jnp.mean(g * xh, axis=-1, keepdims=True))
        dx_ref[...] = dx
        dxb_ref[...] = dx.astype(BF16)
        dw_ref[...] += jnp.sum(dy * xh, axis=0, keepdims=True)

    row = pl.BlockSpec((tm, D), lambda i: (i, 0))
    vec = pl.BlockSpec((1, D), lambda i: (0, 0))
    return pl.pallas_call(
        body, name=name, grid=(T // tm,), in_specs=[row, row, vec],
        out_specs=[pl.BlockSpec((1, 128), lambda i: (0, 0)), row, row, vec],
        out_shape=[jax.ShapeDtypeStruct((1, 128), F32), jax.ShapeDtypeStruct((T, D), F32),
                   jax.ShapeDtypeStruct((T, D), BF16), jax.ShapeDtypeStruct((1, D), F32)],
        compiler_params=_cp(("arbitrary",)))(x2, tgt, w)


def _ret_tables():
    C = RET_CHUNK
    h = jnp.arange(RET_HEADS, dtype=F32)
    log_gamma = jnp.log1p(-jnp.power(2.0, -5.0 - h))
    idx = jnp.arange(C, dtype=F32)
    diff = idx[:, None] - idx[None, :]
    dm = jnp.where(diff >= 0, jnp.exp(log_gamma[:, None, None] * jnp.maximum(diff, 0.0)), 0.0)
    dm = dm.reshape(4, 2 * C, C)
    lane_head = jnp.repeat(jnp.arange(RET_HEADS).reshape(4, 2), 64, axis=1)
    lg = log_gamma[lane_head]
    xi = jnp.exp(lg[:, None, :] * (idx[None, :, None] + 1.0))
    zeta = jnp.exp(lg[:, None, :] * (C - 1.0 - idx[None, :, None]))
    blk = (jnp.arange(128)[:, None] // 64) == (jnp.arange(128)[None, :] // 64)
    cd = jnp.where(blk[None], jnp.exp(lg * C)[:, :, None], 0.0)
    return dm.astype(F32), xi.astype(F32), zeta.astype(F32), cd.astype(F32)


def _ret_specs(tb, rev, nt):
    def tmap(t):
        return (nt - 1 - t) if rev else t
    qkv = [pl.BlockSpec((tb, 128), lambda p, t, o=o: (tmap(t), o + p)) for o in (0, 4, 8)]
    rope = [pl.BlockSpec((tb, 128), lambda p, t: (tmap(t), 0))] * 2
    tabs = [pl.BlockSpec((None, 256, 128), lambda p, t: (p, 0, 0))] + \
           [pl.BlockSpec((None, 128, 128), lambda p, t: (p, 0, 0))] * 3
    return qkv, rope, tabs


def _ret_fwd(proj, cos, ss, tabs, gnw, *, name):
    T = proj.shape[0]
    tb = min(T, 1024)
    nt = T // tb
    nchunk = tb // RET_CHUNK

    def body(q_ref, k_ref, v_ref, g_ref, cos_ref, ss_ref, dm_ref, xi_ref, zt_ref, cd_ref, gnw_ref,
             y_ref, o_ref, r_sc):
        @pl.when(pl.program_id(1) == 0)
        def _():
            r_sc[...] = jnp.zeros_like(r_sc)
        m0, m1 = _head_masks((128, 128))
        dm, xi, zt, cd = dm_ref[...], xi_ref[...], zt_ref[...], cd_ref[...]
        bm = (cd > 0).astype(F32)
        gnw = gnw_ref[...]
        for c in range(nchunk):
            rs = pl.ds(c * RET_CHUNK, RET_CHUNK)
            cs, sn = cos_ref[rs, :], ss_ref[rs, :]
            q = _rope(q_ref[rs, :], cs, sn, 32, 64)
            k = _rope(k_ref[rs, :], cs, sn, 32, 64) * K_SCALE
            v = v_ref[rs, :]
            kb, vb = k.astype(BF16), v.astype(BF16)
            qs = jnp.concatenate([q * m0, q * m1], axis=0).astype(BF16)
            s = (_dot_nt(qs, kb) * dm).astype(BF16)
            vs = jnp.concatenate([v * m0, v * m1], axis=0).astype(BF16)
            o = _dot(jnp.concatenate([s[:128], s[128:]], axis=1), vs)
            r = r_sc[...]
            o = o + _dot(q.astype(BF16), r.astype(BF16)) * xi
            r_sc[...] = cd * r + bm * _dot_tn((k * zt).astype(BF16), vb)
            mu = (jnp.sum(o * m0, axis=1, keepdims=True) * m0 + jnp.sum(o * m1, axis=1, keepdims=True) * m1) * (1.0 / 64)
            d = o - mu
            dd = d * d
            var = (jnp.sum(dd * m0, axis=1, keepdims=True) * m0 + jnp.sum(dd * m1, axis=1, keepdims=True) * m1) * (1.0 / 64)
            oh = d * lax.rsqrt(var + EPS)
            g = g_ref[rs, :]
            y_ref[rs, :] = (g * _sigmoid(g) * (oh * gnw)).astype(BF16)
            o_ref[rs, :] = o

    qkv, rope, tspec = _ret_specs(tb, False, nt)
    gspec = pl.BlockSpec((tb, 128), lambda p, t: (t, 12 + p))
    out = pl.BlockSpec((tb, 128), lambda p, t: (t, p))
    return pl.pallas_call(
        body, name=name, grid=(4, nt),
        in_specs=qkv + [gspec] + rope + tspec + [pl.BlockSpec((1, 128), lambda p, t: (0, p))],
        out_specs=[out, out],
        out_shape=[jax.ShapeDtypeStruct((T, RET_WIDTH), BF16), jax.ShapeDtypeStruct((T, RET_WIDTH), F32)],
        scratch_shapes=[pltpu.VMEM((128, 128), F32)],
        compiler_params=_cp(("parallel", "arbitrary")))(proj, proj, proj, proj, cos, ss, *tabs, gnw)


def _ret_bwd_dq(proj, do, cos, ss, tabs, *, name):
    T = proj.shape[0]
    tb = min(T, 1024)
    nt = T // tb
    nchunk = tb // RET_CHUNK

    def body(q_ref, k_ref, v_ref, do_ref, cos_ref, ss_ref, dm_ref, xi_ref, zt_ref, cd_ref, dq_ref, r_sc):
        del q_ref
        @pl.when(pl.program_id(1) == 0)
        def _():
            r_sc[...] = jnp.zeros_like(r_sc)
        m0, m1 = _head_masks((128, 128))
        dm, xi, zt, cd = dm_ref[...], xi_ref[...], zt_ref[...], cd_ref[...]
        bm = (cd > 0).astype(F32)
        for c in range(nchunk):
            rs = pl.ds(c * RET_CHUNK, RET_CHUNK)
            cs, sn = cos_ref[rs, :], ss_ref[rs, :]
            k = _rope(k_ref[rs, :], cs, sn, 32, 64) * K_SCALE
            vb = v_ref[rs, :].astype(BF16)
            dob = do_ref[rs, :]
            dof = dob.astype(F32)
            dos = jnp.concatenate([dof * m0, dof * m1], axis=0).astype(BF16)
            a = (_dot_nt(dos, vb) * dm).astype(BF16)
            ks = jnp.concatenate([k * m0, k * m1], axis=0).astype(BF16)
            r = r_sc[...]
            dq = _dot(jnp.concatenate([a[:128], a[128:]], axis=1), ks) + _dot_nt(dob, r.astype(BF16)) * xi
            r_sc[...] = cd * r + bm * _dot_tn((k * zt).astype(BF16), vb)
            dq_ref[rs, :] = _rope_t(dq, cs, sn, 32, 64).astype(BF16)

    qkv, rope, tspec = _ret_specs(tb, False, nt)
    blk = pl.BlockSpec((tb, 128), lambda p, t: (t, p))
    return pl.pallas_call(
        body, name=name, grid=(4, nt), in_specs=qkv + [blk] + rope + tspec, out_specs=blk,
        out_shape=jax.ShapeDtypeStruct((T, RET_WIDTH), BF16),
        scratch_shapes=[pltpu.VMEM((128, 128), F32)],
        compiler_params=_cp(("parallel", "arbitrary")))(proj, proj, proj, do, cos, ss, *tabs)


def _ret_bwd_dkv(proj, do, cos, ss, tabs, *, name, swap=None):
    T = proj.shape[0]
    tb = min(T, 1024)
    nt = T // tb
    nchunk = tb // RET_CHUNK

    def body(q_ref, k_ref, v_ref, do_ref, cos_ref, ss_ref, dm_ref, xi_ref, zt_ref, cd_ref, *rest):
        if swap is None:
            backward(q_ref, k_ref, v_ref, do_ref, cos_ref, ss_ref, dm_ref, xi_ref, zt_ref, cd_ref, *rest)
        else:
            g_ref, dk_ref, dv_ref, got_ref, u_sc, *sems = rest
            start, finish = _swap_phases(g_ref, got_ref, *sems)
            pl.when((pl.program_id(0) == 0) & (pl.program_id(1) == 0))(start)
            backward(q_ref, k_ref, v_ref, do_ref, cos_ref, ss_ref, dm_ref, xi_ref, zt_ref, cd_ref, dk_ref, dv_ref, u_sc)
            pl.when((pl.program_id(0) == 3) & (pl.program_id(1) == nt - 1))(finish)

    def backward(q_ref, k_ref, v_ref, do_ref, cos_ref, ss_ref, dm_ref, xi_ref, zt_ref, cd_ref, dk_ref, dv_ref, u_sc):
        @pl.when(pl.program_id(1) == 0)
        def _():
            u_sc[...] = jnp.zeros_like(u_sc)
        m0, m1 = _head_masks((128, 128))
        dm, xi, zt, cd = dm_ref[...], xi_ref[...], zt_ref[...], cd_ref[...]
        bm = (cd > 0).astype(F32)
        for c in reversed(range(nchunk)):
            rs = pl.ds(c * RET_CHUNK, RET_CHUNK)
            cs, sn = cos_ref[rs, :], ss_ref[rs, :]
            q = _rope(q_ref[rs, :], cs, sn, 32, 64)
            k = _rope(k_ref[rs, :], cs, sn, 32, 64) * K_SCALE
            kb = k.astype(BF16)
            vb = v_ref[rs, :].astype(BF16)
            dob = do_ref[rs, :]
            dof = dob.astype(F32)
            qs = jnp.concatenate([q * m0, q * m1], axis=0).astype(BF16)
            dos = jnp.concatenate([dof * m0, dof * m1], axis=0).astype(BF16)
            s = (_dot_nt(qs, kb) * dm).astype(BF16)
            a = (_dot_nt(dos, vb) * dm).astype(BF16)
            ub = u_sc[...].astype(BF16)
            dk = _dot_tn(a, qs) + _dot_nt(vb, ub) * zt
            dv = _dot_tn(s, dos) + _dot(kb, ub) * zt
            u_sc[...] = cd * u_sc[...] + bm * _dot_tn((q * xi).astype(BF16), dob)
            dk_ref[rs, :] = (_rope_t(dk, cs, sn, 32, 64) * K_SCALE).astype(BF16)
            dv_ref[rs, :] = dv.astype(BF16)

    qkv, rope, tspec = _ret_specs(tb, True, nt)
    blk = pl.BlockSpec((tb, 128), lambda p, t: (nt - 1 - t, p))
    out_shape = [jax.ShapeDtypeStruct((T, RET_WIDTH), BF16)] * 2
    if swap is None:
        return pl.pallas_call(
            body, name=name, grid=(4, nt), in_specs=qkv + [blk] + rope + tspec, out_specs=[blk, blk],
            out_shape=out_shape, scratch_shapes=[pltpu.VMEM((128, 128), F32)],
            compiler_params=_cp(("parallel", "arbitrary")))(proj, proj, proj, do, cos, ss, *tabs)
    return pl.pallas_call(
        body, name=name, grid=(4, nt), in_specs=qkv + [blk] + rope + tspec + [ANY], out_specs=[blk, blk, ANY],
        out_shape=out_shape + [jax.ShapeDtypeStruct((4,) + swap.shape[2:], swap.dtype)],
        scratch_shapes=[pltpu.VMEM((128, 128), F32)] + list(SWAP_SCRATCH),
        compiler_params=_cp(("arbitrary", "arbitrary")))(proj, proj, proj, do, cos, ss, *tabs, swap)


def _mix_bwd(dmixed, o_ret, proj, y_mla, gnw, *, name):
    T = dmixed.shape[0]
    tm = min(T, 512)

    def body(dm_ref, o_ref, g_ref, ym_ref, gnw_ref, do_ref, dg_ref, dom_ref, dl_ref, dw_ref):
        @pl.when(pl.program_id(0) == 0)
        def _():
            dw_ref[...] = jnp.zeros_like(dw_ref)
        m0, m1 = _head_masks((tm, 128))
        lane = lax.broadcasted_iota(jnp.int32, (tm, 128), 1)
        delta = jnp.zeros((tm, 128), F32)

        def gsum(z):
            return jnp.sum(z * m0, axis=1, keepdims=True) * m0 + jnp.sum(z * m1, axis=1, keepdims=True) * m1

        for p in range(4):
            cs = slice(128 * p, 128 * p + 128)
            dy = dm_ref[:, cs]
            o = o_ref[:, cs]
            g = g_ref[:, cs]
            w = gnw_ref[:, cs]
            d = o - gsum(o) * (1.0 / 64)
            rstd = lax.rsqrt(gsum(d * d) * (1.0 / 64) + EPS)
            oh = d * rstd
            sg = _sigmoid(g)
            dn = dy * (g * sg)
            dg_ref[:, cs] = (dy * (oh * w) * (sg * (1.0 + g * (1.0 - sg)))).astype(BF16)
            dw_ref[:, cs] += jnp.sum(dn * oh, axis=0, keepdims=True)
            doh = dn * w
            do = rstd * (doh - gsum(doh) * (1.0 / 64) - oh * (gsum(doh * oh) * (1.0 / 64)))
            do_ref[:, cs] = do.astype(BF16)
            dom = dm_ref[:, 512 + 128 * p:512 + 128 * p + 128]
            dom_ref[:, cs] = dom.astype(BF16)
            pr = dom * ym_ref[:, cs].astype(F32)
            delta = jnp.where(lane == 2 * p, jnp.sum(pr * m0, axis=1, keepdims=True), delta)
            delta = jnp.where(lane == 2 * p + 1, jnp.sum(pr * m1, axis=1, keepdims=True), delta)
        dl_ref[...] = delta.T[0:MLA_HEADS]

    half = pl.BlockSpec((tm, 512), lambda i: (i, 0))
    return pl.pallas_call(
        body, name=name, grid=(T // tm,),
        in_specs=[pl.BlockSpec((tm, 1024), lambda i: (i, 0)), half, pl.BlockSpec((tm, 512), lambda i: (i, 3)),
                  half, pl.BlockSpec((1, 512), lambda i: (0, 0))],
        out_specs=[half, half, half, pl.BlockSpec((MLA_HEADS, tm), lambda i: (0, i)),
                   pl.BlockSpec((1, 512), lambda i: (0, 0))],
        out_shape=[jax.ShapeDtypeStruct((T, 512), BF16)] * 3 + [jax.ShapeDtypeStruct((MLA_HEADS, T), F32),
                                                                jax.ShapeDtypeStruct((1, 512), F32)],
        compiler_params=_cp(("arbitrary",)))(dmixed, o_ret, proj, y_mla, gnw)


def _mla_prep_fwd(proj, qnw, kvnw, wuq, wk, wv, cos, ss, *, name):
    T = proj.shape[0]
    tm = min(T, 512)

    def body(lat_ref, qnw_ref, kvnw_ref, wuq_ref, wk_ref, wv_ref, cos_ref, ss_ref,
             q_ref, k_ref, v_ref, cqn_ref, ckvn_ref):
        cq = lat_ref[:, 0:256]
        ckv = lat_ref[:, 256:384]
        g3 = lat_ref[:, 384:512]
        cqn = (cq * lax.rsqrt(jnp.mean(cq * cq, axis=-1, keepdims=True) + EPS) * qnw_ref[...]).astype(BF16)
        ckvn = (ckv * lax.rsqrt(jnp.mean(ckv * ckv, axis=-1, keepdims=True) + EPS) * kvnw_ref[...]).astype(BF16)
        cqn_ref[...] = cqn
        ckvn_ref[...] = ckvn
        cs, sn = cos_ref[...], ss_ref[...]
        q = _dot_nt(cqn, wuq_ref[...])
        k = _dot_nt(ckvn, wk_ref[...])
        kpe = _rope(g3, cs, sn, 16, 32)
        for h in range(MLA_HEADS):
            hs = slice(128 * h, 128 * h + 128)
            q_ref[:, hs] = (_rope(q[:, hs], cs, sn, 16, 32) * SCALE).astype(BF16)
            k_ref[:, hs] = (k[:, hs] + kpe).astype(BF16)
        v = _dot_nt(ckvn, wv_ref[...])
        lane = lax.broadcasted_iota(jnp.int32, (tm, 128), 1)
        for p in range(4):
            vp = v[:, 128 * p:128 * p + 128]
            v_ref[:, 256 * p:256 * p + 128] = jnp.where(lane < 64, vp, 1.0).astype(BF16)
            v_ref[:, 256 * p + 128:256 * p + 256] = jnp.where(lane < 64, 1.0, vp).astype(BF16)

    def full(shape):
        return pl.BlockSpec(shape, lambda i: (0, 0))

    def row(w):
        return pl.BlockSpec((tm, w), lambda i: (i, 0))

    return pl.pallas_call(
        body, name=name, grid=(T // tm,),
        in_specs=[pl.BlockSpec((tm, 512), lambda i: (i, 4)), full((1, 256)), full((1, 128)), full((1024, 256)),
                  full((1024, 128)), full((512, 128)), row(128), row(128)],
        out_specs=[row(1024), row(1024), row(1024), row(256), row(128)],
        out_shape=[jax.ShapeDtypeStruct((T, 1024), BF16), jax.ShapeDtypeStruct((T, 1024), BF16),
                   jax.ShapeDtypeStruct((T, 1024), BF16), jax.ShapeDtypeStruct((T, 256), BF16),
                   jax.ShapeDtypeStruct((T, 128), BF16)],
        compiler_params=_cp(("parallel",)))(proj, qnw, kvnw, wuq, wk, wv, cos, ss)


def _mla_prep_bwd(dq, dk, dv, proj, qnw, kvnw, wuq_t, wk_t, wv_t, cos, ss, *, name):
    T = proj.shape[0]
    tm = min(T, 512)

    def body(dq_ref, dk_ref, dv_ref, lat_ref, qnw_ref, kvnw_ref, wuq_ref, wk_ref, wv_ref, cos_ref, ss_ref,
             dlat_ref, dqp_ref, dqnw_ref, dkvnw_ref):
        @pl.when(pl.program_id(0) == 0)
        def _():
            dqnw_ref[...] = jnp.zeros_like(dqnw_ref)
            dkvnw_ref[...] = jnp.zeros_like(dkvnw_ref)
        cs, sn = cos_ref[...], ss_ref[...]
        dkpe = jnp.zeros((tm, 128), F32)
        for h in range(MLA_HEADS):
            hs = slice(128 * h, 128 * h + 128)
            dqp_ref[:, hs] = _rope_t(dq_ref[:, hs] * SCALE, cs, sn, 16, 32).astype(BF16)
            dkpe = dkpe + dk_ref[:, hs]
        lane = lax.broadcasted_iota(jnp.int32, (tm, 128), 1)
        rope_lane = (lane >= MLA_NOPE) & (lane < MLA_NOPE + MLA_ROPE)
        dg3 = jnp.where(rope_lane, _rope_t(jnp.where(rope_lane, dkpe, 0.0), cs, sn, 16, 32), 0.0)

        def norm_bwd(x, w, dn):
            r = lax.rsqrt(jnp.mean(x * x, axis=-1, keepdims=True) + EPS)
            xh = x * r
            g = dn * w
            return r * (g - xh * jnp.mean(g * xh, axis=-1, keepdims=True)), jnp.sum(dn * xh, axis=0, keepdims=True)

        dcqn = _dot(dqp_ref[...], wuq_ref[...])
        dcq, dqnw = norm_bwd(lat_ref[:, 0:256], qnw_ref[...], dcqn)
        dckvn = _dot(dk_ref[...].astype(BF16), wk_ref[...]) + _dot(dv_ref[...], wv_ref[...])
        dckv, dkvnw = norm_bwd(lat_ref[:, 256:384], kvnw_ref[...], dckvn)
        dqnw_ref[...] += dqnw
        dkvnw_ref[...] += dkvnw
        dlat_ref[:, 0:256] = dcq.astype(BF16)
        dlat_ref[:, 256:384] = dckv.astype(BF16)
        dlat_ref[:, 384:512] = dg3.astype(BF16)

    def full(shape):
        return pl.BlockSpec(shape, lambda i: (0, 0))

    def row(w):
        return pl.BlockSpec((tm, w), lambda i: (i, 0))

    return pl.pallas_call(
        body, name=name, grid=(T // tm,),
        in_specs=[row(1024), row(1024), row(512), pl.BlockSpec((tm, 512), lambda i: (i, 4)), full((1, 256)),
                  full((1, 128)), full((1024, 256)), full((1024, 128)), full((512, 128)), row(128), row(128)],
        out_specs=[row(512), row(1024), full((1, 256)), full((1, 128))],
        out_shape=[jax.ShapeDtypeStruct((T, 512), BF16), jax.ShapeDtypeStruct((T, 1024), BF16),
                   jax.ShapeDtypeStruct((1, 256), F32), jax.ShapeDtypeStruct((1, 128), F32)],
        compiler_params=_cp(("arbitrary",)))(dq, dk, dv, proj, qnw, kvnw, wuq_t, wk_t, wv_t, cos, ss)


def _flash_fwd(q, k, v1, *, name, gather=None):
    T = q.shape[0]
    tq = min(T, 512)
    tk = tq
    nq = T // tq

    def body(q_ref, k_ref, v_ref, *rest):
        if gather is None:
            y_ref, lse_ref = rest
        else:
            x_ref, y_ref, lse_ref, g_ref, *sems = rest
            start, forward, finish = _gather_phases(x_ref, g_ref, *sems)
            pl.when((pl.program_id(0) == 0) & (pl.program_id(1) == 0))(start)
            pl.when((pl.program_id(0) == 1) & (pl.program_id(1) == 0))(forward)
        attend(q_ref, k_ref, v_ref, y_ref, lse_ref)
        if gather is not None:
            pl.when((pl.program_id(0) == 3) & (pl.program_id(1) == nq - 1))(finish)

    def attend(q_ref, k_ref, v_ref, y_ref, lse_ref):
        qi = pl.program_id(1)
        row = lax.broadcasted_iota(jnp.int32, (tq, tk), 0)
        col = lax.broadcasted_iota(jnp.int32, (tq, tk), 1)

        def step(kb, carry, masked):
            ks = pl.ds(pl.multiple_of(kb * tk, tk), tk)
            new = []
            for h in range(2):
                hs = slice(128 * h, 128 * h + 128)
                m, acc = carry[h]
                s = _dot_nt(q_ref[:, hs], k_ref[ks, hs])
                if masked:
                    s = jnp.where(col <= row, s, NEG)
                mn = jnp.maximum(m, jnp.max(s, axis=1, keepdims=True))
                p = jnp.exp((s - mn).astype(BF16))
                acc = jnp.exp(m - mn) * acc + _dot(p, v_ref[ks, hs])
                new.append((mn, acc))
            return tuple(new)

        def unrolled(j, c):
            for u in range(FLASH_UNROLL):
                c = step(FLASH_UNROLL * j + u, c, False)
            return c

        init = (jnp.full((tq, 1), NEG, F32), jnp.zeros((tq, 128), F32))
        carry = lax.fori_loop(0, qi // FLASH_UNROLL, unrolled, (init, init))
        carry = lax.fori_loop(FLASH_UNROLL * (qi // FLASH_UNROLL), qi, lambda kb, c: step(kb, c, False), carry)
        (ma, acca), (mb, accb) = step(qi, carry, True)
        lane = lax.broadcasted_iota(jnp.int32, (tq, 128), 1)
        la, lb = pltpu.roll(acca, 64, 1), pltpu.roll(accb, 64, 1)
        y_ref[...] = jnp.where(lane < 64, acca / la, accb / lb).astype(BF16)
        lse_ref[0, 0] = jnp.broadcast_to(ma + jnp.log(acca[:, 64:65]), (tq, 128)).T[0:1]
        lse_ref[1, 0] = jnp.broadcast_to(mb + jnp.log(accb[:, 0:1]), (tq, 128)).T[0:1]

    in_specs = [pl.BlockSpec((tq, 256), lambda p, i: (i, p)), pl.BlockSpec((T, 256), lambda p, i: (0, p)),
                pl.BlockSpec((T, 256), lambda p, i: (0, p))]
    out_specs = [pl.BlockSpec((tq, 128), lambda p, i: (i, p)), pl.BlockSpec((2, 1, 1, tq), lambda p, i: (p, i, 0, 0))]
    out_shape = [jax.ShapeDtypeStruct((T, MLA_WIDTH), BF16), jax.ShapeDtypeStruct((MLA_HEADS, nq, 1, tq), F32)]
    if gather is None:
        return pl.pallas_call(body, name=name, grid=(4, nq), in_specs=in_specs, out_specs=out_specs,
                              out_shape=out_shape, compiler_params=_cp(("parallel", "arbitrary")))(q, k, v1)
    return pl.pallas_call(
        body, name=name, grid=(4, nq), in_specs=in_specs + [ANY], out_specs=out_specs + [ANY],
        out_shape=out_shape + [jax.ShapeDtypeStruct((N_DEV,) + gather.shape, gather.dtype)],
        scratch_shapes=list(GATHER_SCRATCH),
        compiler_params=_cp(("arbitrary", "arbitrary")))(q, k, v1, gather)


def _flash_bwd(q, k, v, do, lse, delta, *, name, exchange=None):
    T = q.shape[0]
    tq = min(T, 512)
    tk = tq
    nq = T // tq

    def body(q_ref, k_ref, v_ref, do_ref, lse_ref, dl_ref, *rest):
        if exchange is None:
            backward(q_ref, k_ref, v_ref, do_ref, lse_ref, dl_ref, *rest)
        else:
            p_ref, dqt_ref, dk_ref, dv_ref, got_ref, *sems = rest
            start, finish = _exchange_phases(p_ref, got_ref, *sems)
            pl.when((pl.program_id(0) == 0) & (pl.program_id(1) == 0))(start)
            backward(q_ref, k_ref, v_ref, do_ref, lse_ref, dl_ref, dqt_ref, dk_ref, dv_ref)
            pl.when((pl.program_id(0) == 3) & (pl.program_id(1) == nq - 1))(finish)

    def backward(q_ref, k_ref, v_ref, do_ref, lse_ref, dl_ref, dqt_ref, dk_ref, dv_ref):
        kb = pl.program_id(1)

        @pl.when(kb == 0)
        def _():
            dqt_ref[...] = jnp.zeros_like(dqt_ref)
        krow = lax.broadcasted_iota(jnp.int32, (tk, tq), 0)
        qcol = lax.broadcasted_iota(jnp.int32, (tk, tq), 1)
        masks = _head_masks((tk, 128))
        vms = [(v_ref[:, 128 * h:128 * h + 128].astype(F32) * masks[h]).astype(BF16) for h in range(2)]

        def step(qi, carry, masked):
            qs = pl.ds(pl.multiple_of(qi * tq, tq), tq)
            dob = do_ref[qs, :]
            dof = dob.astype(F32)
            dks, dv_acc = list(carry[:2]), carry[2]
            for h in range(2):
                hs = slice(128 * h, 128 * h + 128)
                kh = k_ref[:, hs]
                qh = q_ref[qs, hs]
                st = _dot_nt(kh, qh)
                pt = jnp.exp((st - lse_ref[h, qi]).astype(BF16))
                if masked:
                    pt = jnp.where(krow <= qcol, pt, jnp.zeros_like(pt))
                dv_acc = dv_acc + _dot(pt, (dof * masks[h]).astype(BF16))
                dpt = _dot_nt(vms[h], dob)
                dst = pt * (dpt - dl_ref[h, qi]).astype(BF16)
                dks[h] = dks[h] + _dot(dst, qh)
                dqt_ref[qi, hs, :] += _dot_tn(kh, dst)
            return dks[0], dks[1], dv_acc

        zero = jnp.zeros((tk, 128), F32)
        carry = step(kb, (zero, zero, zero), True)

        def two_steps(j, c):
            qi = kb + 1 + 2 * j
            return step(qi + 1, step(qi, c, False), False)

        pairs = (nq - 1 - kb) // 2
        carry = lax.fori_loop(0, pairs, two_steps, carry)
        dk0, dk1, dv_acc = lax.fori_loop(kb + 1 + 2 * pairs, nq, lambda qi, c: step(qi, c, False), carry)
        dk_ref[:, 0:128] = dk0
        dk_ref[:, 128:256] = dk1
        dv_ref[...] = dv_acc.astype(BF16)

    stat = pl.BlockSpec((2, nq, 1, tq), lambda p, j: (p, 0, 0, 0))
    in_specs = [pl.BlockSpec((T, 256), lambda p, j: (0, p)), pl.BlockSpec((tk, 256), lambda p, j: (j, p)),
                pl.BlockSpec((tk, 256), lambda p, j: (j, p)), pl.BlockSpec((T, 128), lambda p, j: (0, p)), stat, stat]
    out_specs = [pl.BlockSpec((None, nq, 256, tq), lambda p, j: (p, 0, 0, 0)),
                 pl.BlockSpec((tk, 256), lambda p, j: (j, p)), pl.BlockSpec((tk, 128), lambda p, j: (j, p))]
    out_shape = [jax.ShapeDtypeStruct((4, nq, 256, tq), F32), jax.ShapeDtypeStruct((T, 1024), F32),
                 jax.ShapeDtypeStruct((T, MLA_WIDTH), BF16)]
    if exchange is None:
        return pl.pallas_call(body, name=name, grid=(4, nq), in_specs=in_specs, out_specs=out_specs,
                              out_shape=out_shape,
                              compiler_params=_cp(("parallel", "arbitrary")))(q, k, v, do, lse, delta)
    return pl.pallas_call(
        body, name=name, grid=(4, nq), in_specs=in_specs + [ANY], out_specs=out_specs + [ANY],
        out_shape=out_shape + [jax.ShapeDtypeStruct(exchange.shape, exchange.dtype)],
        scratch_shapes=list(EXCHANGE_SCRATCH),
        compiler_params=_cp(("arbitrary", "arbitrary")))(q, k, v, do, lse, delta, exchange)


def _shift_down(x, n, prev8):
    r = pltpu.roll(x, n, 0)
    row = lax.broadcasted_iota(jnp.int32, prev8.shape, 0)
    first = jnp.where(row < n, pltpu.roll(prev8, n, 0), r[:8])
    if x.shape[0] == 8:
        return first
    return jnp.concatenate([first, r[8:]], axis=0)


def _shift_up(x, n, next8):
    tm = x.shape[0]
    r = pltpu.roll(x, tm - n, 0)
    row = lax.broadcasted_iota(jnp.int32, next8.shape, 0)
    last = jnp.where(row >= 8 - n, pltpu.roll(next8, 8 - n, 0), r[tm - 8:])
    return jnp.concatenate([r[:tm - 8], last], axis=0)


def _conv_pre(u, prev8, cw_ref, cb_ref):
    p1 = _shift_down(u, 1, prev8)
    p2 = _shift_down(u, 2, prev8)
    up = cb_ref[...] + cw_ref[0:1, :] * p2 + cw_ref[1:2, :] * p1 + cw_ref[2:3, :] * u
    return up, p1, p2


def _conv_fwd(u, cw, cb, *, name):
    T = u.shape[0]
    tm = min(T, 512)
    W = 2 * FF_HALF

    def body(u_ref, prev_ref, cw_ref, cb_ref, a_ref):
        prev = jnp.where(pl.program_id(0) > 0, prev_ref[...], 0.0)
        up, _, _ = _conv_pre(u_ref[...], prev, cw_ref, cb_ref)
        gate = up[:, :FF_HALF]
        a_ref[...] = (gate * _sigmoid(gate) * up[:, FF_HALF:]).astype(BF16)

    return pl.pallas_call(
        body, name=name, grid=(T // tm, 2),
        in_specs=[pl.BlockSpec((tm, W), lambda i, j: (i, j)),
                  pl.BlockSpec((8, W), lambda i, j: (jnp.maximum(i * (tm // 8) - 1, 0), j)),
                  pl.BlockSpec((3, W), lambda i, j: (0, j)), pl.BlockSpec((1, W), lambda i, j: (0, j))],
        out_specs=pl.BlockSpec((tm, FF_HALF), lambda i, j: (i, j)),
        out_shape=jax.ShapeDtypeStruct((T, D_FF), BF16),
        compiler_params=_cp(("parallel", "parallel")))(u, u, cw, cb)


def _conv_bwd(u, da, cw, cb, *, name):
    T = u.shape[0]
    tm = min(T, 512)
    W = 2 * FF_HALF
    nt = T // tm

    def body(u_ref, prev_ref, next_ref, da_ref, dan_ref, cw_ref, cb_ref, du_ref, dw0_ref, dw1_ref, dw2_ref, db_ref):
        i = pl.program_id(1)

        @pl.when(i == 0)
        def _():
            for r in (dw0_ref, dw1_ref, dw2_ref, db_ref):
                r[...] = jnp.zeros_like(r)

        def dpre(u, prev8, da):
            up, p1, p2 = _conv_pre(u, prev8, cw_ref, cb_ref)
            gate, val = up[:, :FF_HALF], up[:, FF_HALF:]
            sg = _sigmoid(gate)
            dgate = da * val * (sg * (1.0 + gate * (1.0 - sg)))
            dval = da * (gate * sg)
            return jnp.concatenate([dgate, dval], axis=1), p1, p2

        u = u_ref[...]
        prev = jnp.where(i > 0, prev_ref[...], 0.0)
        dup, p1, p2 = dpre(u, prev, da_ref[...])
        dupn, _, _ = dpre(next_ref[...], u[tm - 8:], dan_ref[...])
        dupn = jnp.where(i < nt - 1, dupn, 0.0)
        du = cw_ref[2:3, :] * dup + cw_ref[1:2, :] * _shift_up(dup, 1, dupn) + cw_ref[0:1, :] * _shift_up(dup, 2, dupn)
        du_ref[...] = du.astype(BF16)
        dw0_ref[...] += jnp.sum(dup * p2, axis=0, keepdims=True)
        dw1_ref[...] += jnp.sum(dup * p1, axis=0, keepdims=True)
        dw2_ref[...] += jnp.sum(dup * u, axis=0, keepdims=True)
        db_ref[...] += jnp.sum(dup, axis=0, keepdims=True)

    nxt = lambda j, i: (jnp.minimum((i + 1) * (tm // 8), T // 8 - 1), j)
    vec = pl.BlockSpec((1, W), lambda j, i: (0, j))
    return pl.pallas_call(
        body, name=name, grid=(2, nt),
        in_specs=[pl.BlockSpec((tm, W), lambda j, i: (i, j)),
                  pl.BlockSpec((8, W), lambda j, i: (jnp.maximum(i * (tm // 8) - 1, 0), j)),
                  pl.BlockSpec((8, W), nxt),
                  pl.BlockSpec((tm, FF_HALF), lambda j, i: (i, j)), pl.BlockSpec((8, FF_HALF), nxt),
                  pl.BlockSpec((3, W), lambda j, i: (0, j)), vec],
        out_specs=[pl.BlockSpec((tm, W), lambda j, i: (i, j)), vec, vec, vec, vec],
        out_shape=[jax.ShapeDtypeStruct((T, 2 * D_FF), BF16)] + [jax.ShapeDtypeStruct((1, 2 * D_FF), F32)] * 4,
        compiler_params=_cp(("parallel", "arbitrary")))(u, u, u, da, da, cw, cb)


def _adamw(w, m, v, g_slots, *, name):
    R, C = w.shape
    ns = g_slots.shape[0]
    tr = _row_tile(R)

    def body(w_ref, m_ref, v_ref, g_ref, go_ref, d_ref, mo_ref, vo_ref):
        g = g_ref[0].astype(F32)
        for s in range(1, ns):
            g = g + g_ref[s].astype(F32)
        mn = ADAM_B1 * m_ref[...] + (1.0 - ADAM_B1) * g
        vn = ADAM_B2 * v_ref[...] + (1.0 - ADAM_B2) * (g * g)
        m_hat = mn / (1.0 - ADAM_B1 ** ADAM_STEP)
        v_hat = vn / (1.0 - ADAM_B2 ** ADAM_STEP)
        go_ref[...] = g
        d_ref[...] = -ADAM_LR * (m_hat / (jnp.sqrt(v_hat) + ADAM_EPS) + ADAM_WD * w_ref[...])
        mo_ref[...] = mn
        vo_ref[...] = vn

    blk = pl.BlockSpec((tr, C), lambda i: (i, 0))
    return pl.pallas_call(
        body, name=name, grid=(R // tr,),
        in_specs=[blk, blk, blk, pl.BlockSpec((ns, tr, C), lambda i: (0, i, 0))],
        out_specs=[blk] * 4, out_shape=[jax.ShapeDtypeStruct((R, C), F32)] * 4,
        compiler_params=_cp(("parallel",)))(w, m, v, g_slots)


def _place():
    return lax.axis_index("x"), lax.axis_index("y"), lax.axis_index("c")


GATHER_SCRATCH = (pltpu.SemaphoreType.DMA((7,)), pltpu.SemaphoreType.DMA((7,)), pltpu.SemaphoreType.DMA)
EXCHANGE_SCRATCH = (pltpu.SemaphoreType.DMA((3,)), pltpu.SemaphoreType.DMA((3,)), pltpu.SemaphoreType.DMA)


def _gather_phases(x_ref, out_ref, send_sems, recv_sems, local_sem):
    x_, y_, c_ = _place()
    me, sibling = (x_, y_, c_), (x_, y_, 1 - c_)
    chips = [(1 - x_, y_), (x_, 1 - y_), (1 - x_, 1 - y_)]

    def slot(px, py, pc):
        return out_ref.at[4 * px + 2 * py + pc]

    def copy(k, block, to, src=None):
        return pltpu.make_async_remote_copy(
            src_ref=slot(*block) if src is None else src, dst_ref=slot(*block),
            send_sem=send_sems.at[k], recv_sem=recv_sems.at[k], device_id=to, device_id_type=MESH)

    def mine():
        return pltpu.make_async_copy(x_ref, slot(*me), local_sem)

    def first():
        return [copy(0, me, sibling, src=x_ref)] + [copy(1 + j, me, (*chip, c_), src=x_ref)
                                                     for j, chip in enumerate(chips)]

    def passed():
        return [copy(4 + j, (*chip, c_), sibling) for j, chip in enumerate(chips)]

    def start():
        mine().start()
        for cp in first():
            cp.start()

    def forward():
        fwd = passed()
        for j, chip in enumerate(chips):
            copy(1 + j, (*chip, c_), me).wait_recv()
            fwd[j].start()

    def finish():
        copy(0, sibling, me).wait_recv()
        for j, chip in enumerate(chips):
            copy(4 + j, (*chip, 1 - c_), me).wait_recv()
        for cp in first() + passed():
            cp.wait_send()
        mine().wait()

    return start, forward, finish


def _exchange_phases(p_ref, out_ref, send_sems, recv_sems, local_sem):
    x_, y_, c_ = _place()
    me_k = 2 * x_ + y_
    chips = [(1 - x_, y_), (x_, 1 - y_), (1 - x_, 1 - y_)]

    def local():
        return pltpu.make_async_copy(p_ref.at[me_k], out_ref.at[me_k], local_sem)

    def copy(j, src_k, dst_k, chip):
        return pltpu.make_async_remote_copy(
            src_ref=p_ref.at[src_k], dst_ref=out_ref.at[dst_k], send_sem=send_sems.at[j],
            recv_sem=recv_sems.at[j], device_id=(*chip, c_), device_id_type=MESH)

    def sends():
        return [copy(j, 2 * px + py, me_k, (px, py)) for j, (px, py) in enumerate(chips)]

    def start():
        local().start()
        for cp in sends():
            cp.start()

    def finish():
        for j, (px, py) in enumerate(chips):
            copy(j, me_k, 2 * px + py, (px, py)).wait_recv()
        for cp in sends():
            cp.wait_send()
        local().wait()

    return start, finish


def _all_gather(x, *, name, in_vmem):
    def body(x_ref, out_ref, send_sems, recv_sems, local_sem):
        for phase in _gather_phases(x_ref, out_ref, send_sems, recv_sems, local_sem):
            phase()

    spec = pl.BlockSpec(memory_space=pltpu.VMEM) if in_vmem else ANY
    return pl.pallas_call(
        body, name=name, out_shape=jax.ShapeDtypeStruct((N_DEV,) + x.shape, x.dtype),
        in_specs=[spec], out_specs=spec, scratch_shapes=list(GATHER_SCRATCH),
        compiler_params=pltpu.CompilerParams(vmem_limit_bytes=VMEM_LIMIT))(x)


def _small_rows():
    table, row = [], 0
    for n, size in SMALL_VECTORS:
        table.append((n, size, row))
        row += -(-size // PACK_COLS)
    return table


def _ff_chunk_source(c):
    block, off = divmod(c * 128, FF_HALF)
    return (0, 2, 1, 3)[block] * FF_HALF + off


def _pack_small(parts, *, name):
    table = _small_rows()

    def body(*refs):
        out = refs[-1]
        out[...] = jnp.zeros_like(out)
        for ref, (n, size, row) in zip(refs, table):
            if size != 2 * D_FF:
                out[row:row + 1, 0:size] = ref[...]
                continue
            for c in range(size // 128):
                src = _ff_chunk_source(c)
                r, lane = divmod(c * 128, PACK_COLS)
                out[row + r:row + r + 1, lane:lane + 128] = ref[:, src:src + 128]

    return pl.pallas_call(body, name=name, out_shape=jax.ShapeDtypeStruct((SMALL_ROWS, PACK_COLS), F32))(
        *[parts[n] for n, _, _ in table])


def _sum_small(g, *, name):
    table = _small_rows()
    shapes = [(n, size) for n, size, _ in table if not n.startswith("conv_w")]
    shapes.insert(7, ("conv_w", 2 * D_FF))

    def body(g_ref, *outs):
        def total(row, width):
            acc = g_ref[0, row:row + 1, 0:width]
            for d in range(1, N_DEV):
                acc = acc + g_ref[d, row:row + 1, 0:width]
            return acc

        out_of = {n: o for (n, _), o in zip(shapes, outs)}
        for n, size, row in table:
            o, j = (out_of["conv_w"], int(n[-1])) if n.startswith("conv_w") else (out_of[n], 0)
            for i in range(-(-size // PACK_COLS)):
                width = min(PACK_COLS, size - PACK_COLS * i)
                o[j:j + 1, PACK_COLS * i:PACK_COLS * i + width] = total(row + i, width)

    out_shape = [jax.ShapeDtypeStruct((3 if n == "conv_w" else 1, size), F32) for n, size in shapes]
    res = pl.pallas_call(body, name=name, out_shape=out_shape)(g)
    return {n: r for (n, _), r in zip(shapes, res)}


def _adamw_small(ws, ms, vs, gs, *, name):
    k = len(ws)

    def body(*refs):
        w_refs, m_refs, v_refs, g_refs = (refs[i * k:(i + 1) * k] for i in range(4))
        outs = refs[4 * k:]
        for i in range(k):
            g = g_refs[i][...]
            mn = ADAM_B1 * m_refs[i][...] + (1.0 - ADAM_B1) * g
            vn = ADAM_B2 * v_refs[i][...] + (1.0 - ADAM_B2) * (g * g)
            m_hat = mn / (1.0 - ADAM_B1 ** ADAM_STEP)
            v_hat = vn / (1.0 - ADAM_B2 ** ADAM_STEP)
            outs[i][...] = g
            outs[k + i][...] = -ADAM_LR * (m_hat / (jnp.sqrt(v_hat) + ADAM_EPS) + ADAM_WD * w_refs[i][...])
            outs[2 * k + i][...] = mn
            outs[3 * k + i][...] = vn

    out_shape = [jax.ShapeDtypeStruct(w.shape, F32) for _ in range(4) for w in ws]
    res = pl.pallas_call(body, name=name, out_shape=out_shape)(*ws, *ms, *vs, *gs)
    return [res[i * k:(i + 1) * k] for i in range(4)]


SWAP_SCRATCH = (pltpu.SemaphoreType.DMA((4,)), pltpu.SemaphoreType.DMA((4,)))


def _swap_phases(g_ref, out_ref, send_sems, recv_sems):
    x_, y_, c_ = _place()

    def copies():
        return [pltpu.make_async_remote_copy(src_ref=g_ref.at[k, 1 - c_], dst_ref=out_ref.at[k],
                                             send_sem=send_sems.at[k], recv_sem=recv_sems.at[k],
                                             device_id=(x_, y_, 1 - c_), device_id_type=MESH) for k in range(4)]

    def start():
        for cp in copies():
            cp.start()

    def finish():
        for cp in copies():
            cp.wait()

    return start, finish


def _swap_sibling(g, *, name):
    def body(g_ref, out_ref, send_sems, recv_sems):
        for phase in _swap_phases(g_ref, out_ref, send_sems, recv_sems):
            phase()

    return pl.pallas_call(
        body, name=name, out_shape=jax.ShapeDtypeStruct((4,) + g.shape[2:], g.dtype), in_specs=[ANY], out_specs=ANY,
        scratch_shapes=list(SWAP_SCRATCH))(g)


def _row_tile(R):
    for cand in (256, 400, 200):
        if R % cand == 0:
            return cand
    return R


def _add_own(g, b, *, name, out_dtype):
    n, _, R, C = g.shape
    tr = _row_tile(R)

    def body(c_ref, g_ref, b_ref, o_ref):
        del c_ref
        o_ref[...] = (g_ref[...] + b_ref[...]).astype(out_dtype)

    blk = pl.BlockSpec((None, tr, C), lambda s, i, c: (s, i, 0))
    grid_spec = pltpu.PrefetchScalarGridSpec(
        num_scalar_prefetch=1, grid=(n, R // tr),
        in_specs=[pl.BlockSpec((None, None, tr, C), lambda s, i, c: (s, c[0], i, 0)), blk], out_specs=blk)
    core = jnp.reshape(lax.axis_index("c"), (1,)).astype(jnp.int32)
    return pl.pallas_call(body, name=name, grid_spec=grid_spec, out_shape=jax.ShapeDtypeStruct(b.shape, out_dtype),
                          compiler_params=_cp(("parallel", "parallel")))(core, g, b)


def _pack_local(parts, group):
    table, rows = group
    segs = []
    for n, r, rp, tr in table:
        w = parts[n].T if tr else parts[n]
        segs.append(jnp.pad(w.reshape(r, PACK_COLS), ((0, rp - r), (0, 0))))
    segs.append(jnp.zeros((rows - sum(rp for _, _, rp, _ in table), PACK_COLS), segs[0].dtype))
    return jnp.concatenate(segs, axis=0)


def _unpack_local(packed, like, group):
    out, off = {}, 0
    for n, r, rp, tr in group[0]:
        rows, cols = like[n].shape
        seg = packed[off:off + r]
        out[n] = (seg.reshape(cols, rows).T if tr else seg)[None]
        off += rp
    return out


def _segments(g, group):
    out, off = {}, 0
    for n, r, rp, _ in group[0]:
        out[n] = g[:, off:off + r]
        off += rp
    return out


def _pack_grads(parts, group):
    table, rows = group
    segs = [jnp.pad(parts[n], ((0, 0), (0, rp - parts[n].shape[1]), (0, 0))) for n, _, rp, _ in table]
    segs.append(jnp.zeros((N_DEV, rows - sum(rp for _, _, rp, _ in table), PACK_COLS), F32))
    return jnp.concatenate(segs, axis=1)


def _owner_rows_early(g):
    g_in = jnp.concatenate([g["w_in_t"][:2432], g["w_in_t"][2496:2528]], axis=0).reshape(N_DEV, 308, PACK_COLS)
    g_uq = g["w_uq_t"].reshape(N_DEV, 128, MLA_Q_RANK)[:, :96].reshape(N_DEV, 24, PACK_COLS)
    g_ukv = jnp.concatenate([g["w_k_t"].reshape(N_DEV, 128, MLA_KV_RANK)[:, :64],
                             g["w_v_t"].reshape(N_DEV, 64, MLA_KV_RANK)], axis=1).reshape(N_DEV, 16, PACK_COLS)
    return dict(w_in=g_in, w_uq=g_uq, w_ukv=g_ukv)


def _owner_rows_late(g):
    up = g["w_up_t"].reshape(N_DEV, 704, PACK_COLS)
    g_up = jnp.stack([up[FF_OWNER_ORDER.index(d)] for d in range(N_DEV)])
    return dict(w_out=g["w_out"].reshape(N_DEV, 128, PACK_COLS), w_up=g_up,
                w_down=g["w_down"].reshape(N_DEV, 352, PACK_COLS))


def _reduce_to_pairs(gp, *, name):
    gp = gp.reshape(4, 2, gp.shape[1], PACK_COLS)
    return _add_own(gp, _swap_sibling(gp, name=name + "_swap"), out_dtype=BF16, name=name + "_sum")


def _interleave_ff(w):
    g, v = w[..., :D_FF], w[..., D_FF:]
    return jnp.concatenate([g[..., :FF_HALF], v[..., :FF_HALF], g[..., FF_HALF:], v[..., FF_HALF:]], axis=-1)


def _rope_tables(pos):
    p = pos.astype(F32)[:, None]
    inv_r = ROPE_BASE ** (-jnp.arange(0, RET_HEAD_DIM, 2, dtype=F32) / RET_HEAD_DIM)
    ang = p * jnp.tile(inv_r, 4)
    sign_r = jnp.tile(jnp.concatenate([-jnp.ones((32,), F32), jnp.ones((32,), F32)]), 2)
    cos_r, ss_r = jnp.cos(ang), jnp.sin(ang) * sign_r
    inv_m = ROPE_BASE ** (-jnp.arange(0, MLA_ROPE, 2, dtype=F32) / MLA_ROPE)
    ang = p * jnp.concatenate([jnp.zeros((64,), F32), inv_m, inv_m, jnp.zeros((32,), F32)])
    sign_m = jnp.concatenate([jnp.zeros((64,), F32), -jnp.ones((16,), F32), jnp.ones((16,), F32), jnp.zeros((32,), F32)])
    cos_m, ss_m = jnp.cos(ang), jnp.sin(ang) * sign_m
    return cos_r, ss_r, cos_m, ss_m


def _prep_early(gathered):
    seg = _segments(gathered, EARLY)
    w_in_t = seg["w_in"].reshape(IN_WIDTH, D_MODEL)
    z = lambda n: jnp.zeros((n, D_MODEL), BF16)
    w_in_t = jnp.concatenate([w_in_t[:2432], z(64), w_in_t[2432:2464], z(32)], axis=0)
    w_uq_t = jnp.pad(seg["w_uq"].reshape(MLA_HEADS, 96, MLA_Q_RANK), ((0, 0), (0, 32), (0, 0))).reshape(1024, MLA_Q_RANK)
    ukv = seg["w_ukv"].reshape(MLA_HEADS, 128, MLA_KV_RANK)
    w_k_t = jnp.pad(ukv[:, :64], ((0, 0), (0, 64), (0, 0))).reshape(1024, MLA_KV_RANK)
    w_v_t = ukv[:, 64:].reshape(512, MLA_KV_RANK)
    return dict(w_in_t=w_in_t, w_uq_t=w_uq_t, w_k_t=w_k_t, w_v_t=w_v_t)


def _prep_late(gathered):
    seg = _segments(gathered, LATE)
    w_up_t = jnp.concatenate([seg["w_up"][d] for d in FF_OWNER_ORDER], axis=0)
    return dict(w_out=seg["w_out"].reshape(1024, D_MODEL), w_up_t=w_up_t, w_down=seg["w_down"].reshape(D_FF, D_MODEL))


def _local_step(x, pos, tgt, early, sm, late):
    dist = not isinstance(late, dict)
    cos_r, ss_r, cos_m, ss_m = _rope_tables(pos)
    tabs = _ret_tables()

    if dist:
        h, gathered = _rmsnorm_fwd(x, sm["attn_norm_w"], gather=early, name="attn_norm")
        W = _prep_early(gathered)
    else:
        h = _rmsnorm_fwd(x, sm["attn_norm_w"], name="attn_norm")
        W = early
    proj = _mm(h, W["w_in_t"], bt=True, name="in_proj")
    y_ret, o_ret = _ret_fwd(proj, cos_r, ss_r, tabs, sm["ret_gn_w"], name="ret_fwd")
    q, k, v1, cqn, ckvn = _mla_prep_fwd(proj, sm["mla_q_norm_w"], sm["mla_kv_norm_w"], W["w_uq_t"], W["w_k_t"],
                                       W["w_v_t"], cos_m, ss_m, name="mla_prep")
    T = x.shape[0]
    tq = min(T, 512)
    if dist:
        y_mla, lse, gathered = _flash_fwd(q, k, v1, gather=late, name="mla_attn")
        W = {**W, **_prep_late(gathered)}
    else:
        y_mla, lse = _flash_fwd(q, k, v1, name="mla_attn")
        W = {**W, **late}
    mixed = jnp.concatenate([y_ret, y_mla], axis=1)
    x1 = _mm(mixed, W["w_out"], add=x, name="out_proj")
    h2 = _rmsnorm_fwd(x1, sm["ffn_norm_w"], name="ffn_norm")
    u = _mm(h2, W["w_up_t"], bt=True, name="up_proj")
    a = _conv_fwd(u, sm["conv_w"], sm["conv_b"], name="conv_gate")
    x2 = _mm(a, W["w_down"], add=x1, name="down_proj")
    loss, dx2, dx2b, d_final = _loss_head(x2, tgt, sm["final_norm_w"], name="loss_head")

    g = {}
    g["w_down"] = _mm_tn(a, dx2b, name="dw_down")
    da = _mm(dx2b, W["w_down"], bt=True, name="d_act")
    du, dcw0, dcw1, dcw2, dcb = _conv_bwd(u, da, sm["conv_w"], sm["conv_b"], name="conv_bwd")
    g["w_up_t"] = _mm_tn(du, h2, name="dw_up")
    dh2 = _mm(du, W["w_up_t"], name="d_h2")
    dx1, d_ffn = _rmsnorm_bwd(x1, sm["ffn_norm_w"], dh2, dx2, name="ffn_norm_bwd")

    g["w_out"] = _mm_tn(mixed, dx1, name="dw_out")
    dmixed = _mm(dx1, W["w_out"], bt=True, name="d_mixed")
    do_ret, dg, do_mla, delta, d_gn = _mix_bwd(dmixed, o_ret, proj, y_mla, sm["ret_gn_w"], name="mix_bwd")
    drq = _ret_bwd_dq(proj, do_ret, cos_r, ss_r, tabs, name="ret_bwd_dq")
    delta_r = delta.reshape(MLA_HEADS, T // tq, 1, tq)
    if dist:
        gl = _pack_grads(_owner_rows_late(g), LATE).reshape(4, 2, LATE[1], PACK_COLS)
        drk, drv, theirs = _ret_bwd_dkv(proj, do_ret, cos_r, ss_r, tabs, swap=gl, name="ret_bwd_dkv")
        pair = _add_own(gl, theirs, out_dtype=BF16, name="grad_late_sum")
        dqt, dk, dv, slots_late = _flash_bwd(q, k, v1, do_mla, lse, delta_r, exchange=pair, name="mla_attn_bwd")
    else:
        drk, drv = _ret_bwd_dkv(proj, do_ret, cos_r, ss_r, tabs, name="ret_bwd_dkv")
        dqt, dk, dv = _flash_bwd(q, k, v1, do_mla, lse, delta_r, name="mla_attn_bwd")
        slots_late = None
    dq = dqt.transpose(1, 3, 0, 2).reshape(T, MLA_HEADS * 128)
    dlat, dqp, d_qn, d_kvn = _mla_prep_bwd(dq, dk, dv, proj, sm["mla_q_norm_w"], sm["mla_kv_norm_w"], W["w_uq_t"],
                                           W["w_k_t"], W["w_v_t"], cos_m, ss_m, name="mla_prep_bwd")
    g["w_uq_t"] = _mm_tn(dqp, cqn, name="dw_uq")
    g["w_k_t"] = _mm_tn(dk, ckvn, name="dw_ukv_k")
    g["w_v_t"] = _mm_tn(dv, ckvn, name="dw_ukv_v")
    dproj = jnp.concatenate([drq, drk, drv, dg, dlat], axis=1)
    g["w_in_t"] = _mm_tn(dproj, h, name="dw_in")
    if dist:
        pair = _reduce_to_pairs(_pack_grads(_owner_rows_early(g), EARLY), name="grad_early")
        dh, slots_early = _mm(dproj, W["w_in_t"], exchange=pair, name="d_h")
    else:
        dh = _mm(dproj, W["w_in_t"], name="d_h")
        slots_early = None
    grad_x, d_attn = _rmsnorm_bwd(x, sm["attn_norm_w"], dh, dx1, name="attn_norm_bwd")

    small = dict(attn_norm_w=d_attn, ret_gn_w=d_gn, mla_q_norm_w=d_qn, mla_kv_norm_w=d_kvn, ffn_norm_w=d_ffn,
                 conv_b=dcb, final_norm_w=d_final, conv_w0=dcw0, conv_w1=dcw1, conv_w2=dcw2, loss=loss)
    return loss, grad_x, g, small, slots_early, slots_late


def kernel(x, positions, attn_norm_w, w_in, ret_gn_w, mla_q_norm_w, w_uq, mla_kv_norm_w, w_ukv, w_out, ffn_norm_w, w_up, conv_w, conv_b, w_down, final_norm_w, loss_target, m_attn_norm_w, m_w_in, m_ret_gn_w, m_mla_q_norm_w, m_w_uq, m_mla_kv_norm_w, m_w_ukv, m_w_out, m_ffn_norm_w, m_w_up, m_conv_w, m_conv_b, m_w_down, m_final_norm_w, v_attn_norm_w, v_w_in, v_ret_gn_w, v_mla_q_norm_w, v_w_uq, v_mla_kv_norm_w, v_w_ukv, v_w_out, v_ffn_norm_w, v_w_up, v_conv_w, v_conv_b, v_w_down, v_final_norm_w):
    a = dict(locals())
    x_, y_, c_ = _place()
    dev = 4 * x_ + 2 * y_ + c_

    shard = {n: a[n][0] for n in BIG_NAMES}
    shard16 = {n: w.astype(BF16) for n, w in shard.items()}
    cw_pad = jnp.pad(conv_w[0].reshape(-1), (0, 24 * 128 - 3 * 704)).reshape(24, 128)
    cw_all = _all_gather(cw_pad, name="gather_conv_w", in_vmem=True)
    conv_w_full = cw_all.reshape(N_DEV, -1)[:, :3 * 704].reshape(N_DEV, 3, 704).transpose(1, 0, 2).reshape(3, 2 * D_FF)
    sm = dict(attn_norm_w=attn_norm_w, ret_gn_w=ret_gn_w, mla_q_norm_w=mla_q_norm_w, mla_kv_norm_w=mla_kv_norm_w,
              ffn_norm_w=ffn_norm_w, final_norm_w=final_norm_w.reshape(1, D_MODEL),
              conv_w=_interleave_ff(conv_w_full), conv_b=_interleave_ff(conv_b))

    loss, grad_x, _, gs, slots_early, slots_late = _local_step(
        x[0], positions[0], loss_target[0], _pack_local(shard16, EARLY), sm, _pack_local(shard16, LATE))

    big = [{}, {}, {}, {}]
    for group, slots, tag in ((EARLY, slots_early, "early"), (LATE, slots_late, "late")):
        names_g = [n for n, _, _, _ in group[0]]
        res = _adamw(_pack_local({n: shard[n] for n in names_g}, group),
                     _pack_local({n: a["m_" + n][0] for n in names_g}, group),
                     _pack_local({n: a["v_" + n][0] for n in names_g}, group), slots, name="adamw_" + tag)
        for kind in range(4):
            big[kind].update(_unpack_local(res[kind], shard, group))

    packed = _pack_small(gs, name="pack_small_grads")
    tot = _sum_small(_all_gather(packed, name="gather_small_grads", in_vmem=True), name="sum_small_grads")
    loss_out = tot["loss"][0, 0]
    g_cw = lax.dynamic_slice_in_dim(tot["conv_w"], dev * 704, 704, axis=1)

    def rows_of(prefix):
        return [a[prefix + n].reshape(1, size) for n, size in SMALL]

    sml = _adamw_small(rows_of(""), rows_of("m_"), rows_of("v_"), [tot[n] for n, _ in SMALL], name="adamw_small")
    cwo = _adamw(conv_w[0], m_conv_w[0], v_conv_w[0], g_cw[None], name="adamw_conv_w")

    def small_of(kind, n):
        return sml[kind][[nm for nm, _ in SMALL].index(n)].reshape(a[n].shape)

    names = ['attn_norm_w', 'w_in', 'ret_gn_w', 'mla_q_norm_w', 'w_uq', 'mla_kv_norm_w', 'w_ukv', 'w_out',
             'ffn_norm_w', 'w_up', 'conv_w', 'conv_b', 'w_down', 'final_norm_w']
    outs = [loss_out, grad_x[None]]
    for kind in range(4):
        for n in names:
            if n == "conv_w":
                outs.append(cwo[kind][None])
            elif n in big[kind]:
                outs.append(big[kind][n])
            else:
                outs.append(small_of(kind, n))
    return tuple(outs)
```

```python
import functools

import numpy as np
import jax
import jax.numpy as jnp
from jax import lax
from jax.experimental import pallas as pl
from jax.experimental.pallas import tpu as pltpu

F32 = jnp.float32
BF16 = jnp.bfloat16
MESH = pl.DeviceIdType.MESH
ANY = pl.BlockSpec(memory_space=pl.ANY)

D_MODEL = 1024
RET_HEADS = 8
RET_HEAD_DIM = 64
RET_WIDTH = 512
RET_CHUNK = 128
MLA_HEADS = 8
MLA_NOPE = 64
MLA_ROPE = 32
MLA_V = 64
MLA_Q_RANK = 256
MLA_KV_RANK = 128
MLA_WIDTH = 512
IN_WIDTH = 2464
IN_PAD = 2560
D_FF = 2816
FF_HALF = 1408
FF_OWNER_ORDER = (0, 1, 4, 5, 2, 3, 6, 7)
ROPE_BASE = 10000.0
EPS = 1e-6
SCALE = float((MLA_NOPE + MLA_ROPE) ** -0.5)
K_SCALE = 0.125
N_DEV = 8

ADAM_LR = 0.001
ADAM_B1 = 0.9
ADAM_B2 = 0.999
ADAM_EPS = 1e-08
ADAM_WD = 0.01
ADAM_STEP = 10

VMEM_LIMIT = 56 * 1024 * 1024
MM_BUDGET = 40 * 1024 * 1024
NEG = -1e30
FLASH_UNROLL = 4

PACK_COLS = 1024
EARLY = ((("w_in", 308, 320, True), ("w_uq", 24, 32, True), ("w_ukv", 16, 16, True)), 384)
LATE = ((("w_out", 128, 128, False), ("w_up", 704, 704, True), ("w_down", 352, 352, False)), 1200)
BIG_NAMES = ("w_in", "w_uq", "w_ukv", "w_out", "w_up", "w_down")
SMALL = (("attn_norm_w", 1024), ("ret_gn_w", 512), ("mla_q_norm_w", 256), ("mla_kv_norm_w", 128),
         ("ffn_norm_w", 1024), ("conv_b", 5632), ("final_norm_w", 1024))
SMALL_VECTORS = SMALL + (("conv_w0", 5632), ("conv_w1", 5632), ("conv_w2", 5632), ("loss", 128))
SMALL_ROWS = 32


def _cp(sem=None, vmem=VMEM_LIMIT):
    return pltpu.CompilerParams(dimension_semantics=sem, vmem_limit_bytes=vmem)


def _dot(a, b):
    return jnp.dot(a, b, preferred_element_type=F32)


def _dot_nt(a, b):
    return lax.dot_general(a, b, (((1,), (1,)), ((), ())), preferred_element_type=F32)


def _dot_tn(a, b):
    return lax.dot_general(a, b, (((0,), (0,)), ((), ())), preferred_element_type=F32)


def _sigmoid(x):
    return 0.5 * jnp.tanh(0.5 * x) + 0.5


def _partner(x, half, period):
    n = x.shape[-1]
    lane = lax.broadcasted_iota(jnp.int32, x.shape, 1)
    return jnp.where((lane % period) < half, pltpu.roll(x, n - half, 1), pltpu.roll(x, half, 1))


def _rope(x, cos, ss, half, period):
    return x * cos + _partner(x, half, period) * ss


def _rope_t(dy, cos, ss, half, period):
    return dy * cos - _partner(dy, half, period) * ss


def _head_masks(shape):
    lane = lax.broadcasted_iota(jnp.int32, shape, 1)
    m0 = (lane < 64).astype(F32)
    return m0, 1.0 - m0


def _mm(a, b, *, name, add=None, out_dtype=F32, bt=False, exchange=None):
    M, K = a.shape
    N = b.shape[0] if bt else b.shape[1]
    osz = jnp.dtype(out_dtype).itemsize
    per_row = 2 * (K * a.dtype.itemsize + N * osz + (N * 4 if add is not None else 0))
    tm = 128
    for cand in (512, 256):
        if M % cand == 0 and cand * per_row + 4 * K * N <= MM_BUDGET:
            tm = cand
            break
    tm = min(tm, M)
    mul = _dot_nt if bt else _dot
    n_in = 2 if add is None else 3

    def body(*refs):
        a_ref, b_ref = refs[:2]
        acc = mul(a_ref[...].astype(BF16), b_ref[...])
        if add is not None:
            acc = refs[2][...] + acc
        if exchange is None:
            refs[n_in][...] = acc.astype(out_dtype)
        else:
            p_ref, o_ref, got_ref, *sems = refs[n_in:]
            start, finish = _exchange_phases(p_ref, got_ref, *sems)
            pl.when(pl.program_id(0) == 0)(start)
            o_ref[...] = acc.astype(out_dtype)
            pl.when(pl.program_id(0) == M // tm - 1)(finish)

    in_specs = [pl.BlockSpec((tm, K), lambda i: (i, 0)), pl.BlockSpec(b.shape, lambda i: (0, 0))]
    args = [a, b]
    if add is not None:
        in_specs.append(pl.BlockSpec((tm, N), lambda i: (i, 0)))
        args.append(add)
    out_spec = pl.BlockSpec((tm, N), lambda i: (i, 0))
    out_shape = jax.ShapeDtypeStruct((M, N), out_dtype)
    if exchange is None:
        return pl.pallas_call(body, name=name, grid=(M // tm,), in_specs=in_specs, out_specs=out_spec,
                              out_shape=out_shape, compiler_params=_cp(("parallel",)))(*args)
    return pl.pallas_call(
        body, name=name, grid=(M // tm,), in_specs=in_specs + [ANY], out_specs=[out_spec, ANY],
        out_shape=[out_shape, jax.ShapeDtypeStruct(exchange.shape, exchange.dtype)],
        scratch_shapes=list(EXCHANGE_SCRATCH), compiler_params=_cp(("arbitrary",)))(*args, exchange)


def _mm_tn(a, b, *, name):
    T, M = a.shape
    N = b.shape[1]
    tk = min(T, 512)

    def tile(n):
        for cand in (1408, 1280):
            if n > 1408 and n % cand == 0:
                return cand
        return n

    tm, tn = tile(M), tile(N)
    nk = T // tk

    def body(a_ref, b_ref, o_ref):
        @pl.when(pl.program_id(2) == 0)
        def _():
            o_ref[...] = jnp.zeros_like(o_ref)
        o_ref[...] += _dot_tn(a_ref[...].astype(BF16), b_ref[...].astype(BF16))

    return pl.pallas_call(
        body, name=name, grid=(M // tm, N // tn, nk),
        in_specs=[pl.BlockSpec((tk, tm), lambda i, j, k: (k, i)), pl.BlockSpec((tk, tn), lambda i, j, k: (k, j))],
        out_specs=pl.BlockSpec((tm, tn), lambda i, j, k: (i, j)),
        out_shape=jax.ShapeDtypeStruct((M, N), F32),
        compiler_params=_cp(("parallel", "parallel", "arbitrary")))(a, b)


def _rmsnorm_fwd(x, w, *, name, gather=None):
    T, D = x.shape
    tm = min(T, 1024)
    n = T // tm

    def body(x_ref, w_ref, *rest):
        if gather is not None:
            s_ref, o_ref, g_ref, *sems = rest
            start, forward, finish = _gather_phases(s_ref, g_ref, *sems)
            pl.when(pl.program_id(0) == 0)(start)
            pl.when(pl.program_id(0) == n // 2)(forward)
        else:
            o_ref, = rest
        xv = x_ref[...]
        r = lax.rsqrt(jnp.mean(xv * xv, axis=-1, keepdims=True) + EPS)
        o_ref[...] = (xv * r * w_ref[...]).astype(BF16)
        if gather is not None:
            pl.when(pl.program_id(0) == n - 1)(finish)

    in_specs = [pl.BlockSpec((tm, D), lambda i: (i, 0)), pl.BlockSpec((1, D), lambda i: (0, 0))]
    out_spec = pl.BlockSpec((tm, D), lambda i: (i, 0))
    out_shape = jax.ShapeDtypeStruct((T, D), BF16)
    if gather is None:
        return pl.pallas_call(body, name=name, grid=(n,), in_specs=in_specs, out_specs=out_spec, out_shape=out_shape,
                              compiler_params=_cp(("parallel",)))(x, w)
    return pl.pallas_call(
        body, name=name, grid=(n,), in_specs=in_specs + [ANY], out_specs=[out_spec, ANY],
        out_shape=[out_shape, jax.ShapeDtypeStruct((N_DEV,) + gather.shape, gather.dtype)],
        scratch_shapes=list(GATHER_SCRATCH), compiler_params=_cp(("arbitrary",)))(x, w, gather)


def _rmsnorm_bwd(x, w, dh, dres, *, name):
    T, D = x.shape
    tm = min(T, 512)

    def body(x_ref, w_ref, dh_ref, dr_ref, dx_ref, dw_ref):
        @pl.when(pl.program_id(0) == 0)
        def _():
            dw_ref[...] = jnp.zeros_like(dw_ref)
        xv = x_ref[...]
        r = lax.rsqrt(jnp.mean(xv * xv, axis=-1, keepdims=True) + EPS)
        xh = xv * r
        dh = dh_ref[...]
        g = dh * w_ref[...]
        dx_ref[...] = dr_ref[...] + r * (g - xh * jnp.mean(g * xh, axis=-1, keepdims=True))
        dw_ref[...] += jnp.sum(dh * xh, axis=0, keepdims=True)

    row = pl.BlockSpec((tm, D), lambda i: (i, 0))
    vec = pl.BlockSpec((1, D), lambda i: (0, 0))
    return pl.pallas_call(
        body, name=name, grid=(T // tm,), in_specs=[row, vec, row, row], out_specs=[row, vec],
        out_shape=[jax.ShapeDtypeStruct((T, D), F32), jax.ShapeDtypeStruct((1, D), F32)],
        compiler_params=_cp(("arbitrary",)))(x, w, dh, dres)


def _loss_head(x2, tgt, w, *, name):
    T, D = x2.shape
    tm = min(T, 512)

    def body(x_ref, t_ref, w_ref, loss_ref, dx_ref, dxb_ref, dw_ref):
        @pl.when(pl.program_id(0) == 0)
        def _():
            dw_ref[...] = jnp.zeros_like(dw_ref)
            loss_ref[...] = jnp.zeros_like(loss_ref)
        xv = x_ref[...]
        wv = w_ref[...]
        r = lax.rsqrt(jnp.mean(xv * xv, axis=-1, keepdims=True) + EPS)
        xh = xv * r
        e = xh * wv - t_ref[...]
        part = 0.5 * jnp.sum(jnp.mean(e * e, axis=-1, keepdims=True), axis=0, keepdims=True)
        loss_ref[...] += jnp.broadcast_to(part, loss_ref.shape)
        dy = e * (1.0 / D)
        g = dy * wv
        dx = r * (g - xh * jnp.mean(g * xh, axis=-1, keepdims=True))
        dx_ref[...] = dx
        dxb_ref[...] = dx.astype(BF16)
        dw_ref[...] += jnp.sum(dy * xh, axis=0, keepdims=True)

    row = pl.BlockSpec((tm, D), lambda i: (i, 0))
    vec = pl.BlockSpec((1, D), lambda i: (0, 0))
    return pl.pallas_call(
        body, name=name, grid=(T // tm,), in_specs=[row, row, vec],
        out_specs=[pl.BlockSpec((1, 128), lambda i: (0, 0)), row, row, vec],
        out_shape=[jax.ShapeDtypeStruct((1, 128), F32), jax.ShapeDtypeStruct((T, D), F32),
                   jax.ShapeDtypeStruct((T, D), BF16), jax.ShapeDtypeStruct((1, D), F32)],
        compiler_params=_cp(("arbitrary",)))(x2, tgt, w)


def _ret_tables():
    C = RET_CHUNK
    h = jnp.arange(RET_HEADS, dtype=F32)
    log_gamma = jnp.log1p(-jnp.power(2.0, -5.0 - h))
    idx = jnp.arange(C, dtype=F32)
    diff = idx[:, None] - idx[None, :]
    dm = jnp.where(diff >= 0, jnp.exp(log_gamma[:, None, None] * jnp.maximum(diff, 0.0)), 0.0)
    dm = dm.reshape(4, 2 * C, C)
    lane_head = jnp.repeat(jnp.arange(RET_HEADS).reshape(4, 2), 64, axis=1)
    lg = log_gamma[lane_head]
    xi = jnp.exp(lg[:, None, :] * (idx[None, :, None] + 1.0))
    zeta = jnp.exp(lg[:, None, :] * (C - 1.0 - idx[None, :, None]))
    blk = (jnp.arange(128)[:, None] // 64) == (jnp.arange(128)[None, :] // 64)
    cd = jnp.where(blk[None], jnp.exp(lg * C)[:, :, None], 0.0)
    return dm.astype(F32), xi.astype(F32), zeta.astype(F32), cd.astype(F32)


def _ret_specs(tb, rev, nt):
    def tmap(t):
        return (nt - 1 - t) if rev else t
    qkv = [pl.BlockSpec((tb, 128), lambda p, t, o=o: (tmap(t), o + p)) for o in (0, 4, 8)]
    rope = [pl.BlockSpec((tb, 128), lambda p, t: (tmap(t), 0))] * 2
    tabs = [pl.BlockSpec((None, 256, 128), lambda p, t: (p, 0, 0))] + \
           [pl.BlockSpec((None, 128, 128), lambda p, t: (p, 0, 0))] * 3
    return qkv, rope, tabs


def _ret_fwd(proj, cos, ss, tabs, gnw, *, name):
    T = proj.shape[0]
    tb = min(T, 1024)
    nt = T // tb
    nchunk = tb // RET_CHUNK

    def body(q_ref, k_ref, v_ref, g_ref, cos_ref, ss_ref, dm_ref, xi_ref, zt_ref, cd_ref, gnw_ref,
             y_ref, o_ref, r_sc):
        @pl.when(pl.program_id(1) == 0)
        def _():
            r_sc[...] = jnp.zeros_like(r_sc)
        m0, m1 = _head_masks((128, 128))
        dm, xi, zt, cd = dm_ref[...], xi_ref[...], zt_ref[...], cd_ref[...]
        bm = (cd > 0).astype(F32)
        gnw = gnw_ref[...]
        for c in range(nchunk):
            rs = pl.ds(c * RET_CHUNK, RET_CHUNK)
            cs, sn = cos_ref[rs, :], ss_ref[rs, :]
            q = _rope(q_ref[rs, :], cs, sn, 32, 64)
            k = _rope(k_ref[rs, :], cs, sn, 32, 64) * K_SCALE
            v = v_ref[rs, :]
            kb, vb = k.astype(BF16), v.astype(BF16)
            qs = jnp.concatenate([q * m0, q * m1], axis=0).astype(BF16)
            s = (_dot_nt(qs, kb) * dm).astype(BF16)
            vs = jnp.concatenate([v * m0, v * m1], axis=0).astype(BF16)
            o = _dot(jnp.concatenate([s[:128], s[128:]], axis=1), vs)
            r = r_sc[...]
            o = o + _dot(q.astype(BF16), r.astype(BF16)) * xi
            r_sc[...] = cd * r + bm * _dot_tn((k * zt).astype(BF16), vb)
            mu = (jnp.sum(o * m0, axis=1, keepdims=True) * m0 + jnp.sum(o * m1, axis=1, keepdims=True) * m1) * (1.0 / 64)
            d = o - mu
            dd = d * d
            var = (jnp.sum(dd * m0, axis=1, keepdims=True) * m0 + jnp.sum(dd * m1, axis=1, keepdims=True) * m1) * (1.0 / 64)
            oh = d * lax.rsqrt(var + EPS)
            g = g_ref[rs, :]
            y_ref[rs, :] = (g * _sigmoid(g) * (oh * gnw)).astype(BF16)
            o_ref[rs, :] = o

    qkv, rope, tspec = _ret_specs(tb, False, nt)
    gspec = pl.BlockSpec((tb, 128), lambda p, t: (t, 12 + p))
    out = pl.BlockSpec((tb, 128), lambda p, t: (t, p))
    return pl.pallas_call(
        body, name=name, grid=(4, nt),
        in_specs=qkv + [gspec] + rope + tspec + [pl.BlockSpec((1, 128), lambda p, t: (0, p))],
        out_specs=[out, out],
        out_shape=[jax.ShapeDtypeStruct((T, RET_WIDTH), BF16), jax.ShapeDtypeStruct((T, RET_WIDTH), F32)],
        scratch_shapes=[pltpu.VMEM((128, 128), F32)],
        compiler_params=_cp(("parallel", "arbitrary")))(proj, proj, proj, proj, cos, ss, *tabs, gnw)


def _ret_bwd_dq(proj, do, cos, ss, tabs, *, name):
    T = proj.shape[0]
    tb = min(T, 1024)
    nt = T // tb
    nchunk = tb // RET_CHUNK

    def body(q_ref, k_ref, v_ref, do_ref, cos_ref, ss_ref, dm_ref, xi_ref, zt_ref, cd_ref, dq_ref, r_sc):
        del q_ref
        @pl.when(pl.program_id(1) == 0)
        def _():
            r_sc[...] = jnp.zeros_like(r_sc)
        m0, m1 = _head_masks((128, 128))
        dm, xi, zt, cd = dm_ref[...], xi_ref[...], zt_ref[...], cd_ref[...]
        bm = (cd > 0).astype(F32)
        for c in range(nchunk):
            rs = pl.ds(c * RET_CHUNK, RET_CHUNK)
            cs, sn = cos_ref[rs, :], ss_ref[rs, :]
            k = _rope(k_ref[rs, :], cs, sn, 32, 64) * K_SCALE
            vb = v_ref[rs, :].astype(BF16)
            dob = do_ref[rs, :]
            dof = dob.astype(F32)
            dos = jnp.concatenate([dof * m0, dof * m1], axis=0).astype(BF16)
            a = (_dot_nt(dos, vb) * dm).astype(BF16)
            ks = jnp.concatenate([k * m0, k * m1], axis=0).astype(BF16)
            r = r_sc[...]
            dq = _dot(jnp.concatenate([a[:128], a[128:]], axis=1), ks) + _dot_nt(dob, r.astype(BF16)) * xi
            r_sc[...] = cd * r + bm * _dot_tn((k * zt).astype(BF16), vb)
            dq_ref[rs, :] = _rope_t(dq, cs, sn, 32, 64).astype(BF16)

    qkv, rope, tspec = _ret_specs(tb, False, nt)
    blk = pl.BlockSpec((tb, 128), lambda p, t: (t, p))
    return pl.pallas_call(
        body, name=name, grid=(4, nt), in_specs=qkv + [blk] + rope + tspec, out_specs=blk,
        out_shape=jax.ShapeDtypeStruct((T, RET_WIDTH), BF16),
        scratch_shapes=[pltpu.VMEM((128, 128), F32)],
        compiler_params=_cp(("parallel", "arbitrary")))(proj, proj, proj, do, cos, ss, *tabs)


def _ret_bwd_dkv(proj, do, cos, ss, tabs, *, name, swap=None):
    T = proj.shape[0]
    tb = min(T, 1024)
    nt = T // tb
    nchunk = tb // RET_CHUNK

    def body(q_ref, k_ref, v_ref, do_ref, cos_ref, ss_ref, dm_ref, xi_ref, zt_ref, cd_ref, *rest):
        if swap is None:
            backward(q_ref, k_ref, v_ref, do_ref, cos_ref, ss_ref, dm_ref, xi_ref, zt_ref, cd_ref, *rest)
        else:
            g_ref, dk_ref, dv_ref, got_ref, u_sc, *sems = rest
            start, finish = _swap_phases(g_ref, got_ref, *sems)
            pl.when((pl.program_id(0) == 0) & (pl.program_id(1) == 0))(start)
            backward(q_ref, k_ref, v_ref, do_ref, cos_ref, ss_ref, dm_ref, xi_ref, zt_ref, cd_ref, dk_ref, dv_ref, u_sc)
            pl.when((pl.program_id(0) == 3) & (pl.program_id(1) == nt - 1))(finish)

    def backward(q_ref, k_ref, v_ref, do_ref, cos_ref, ss_ref, dm_ref, xi_ref, zt_ref, cd_ref, dk_ref, dv_ref, u_sc):
        @pl.when(pl.program_id(1) == 0)
        def _():
            u_sc[...] = jnp.zeros_like(u_sc)
        m0, m1 = _head_masks((128, 128))
        dm, xi, zt, cd = dm_ref[...], xi_ref[...], zt_ref[...], cd_ref[...]
        bm = (cd > 0).astype(F32)
        for c in reversed(range(nchunk)):
            rs = pl.ds(c * RET_CHUNK, RET_CHUNK)
            cs, sn = cos_ref[rs, :], ss_ref[rs, :]
            q = _rope(q_ref[rs, :], cs, sn, 32, 64)
            k = _rope(k_ref[rs, :], cs, sn, 32, 64) * K_SCALE
            kb = k.astype(BF16)
            vb = v_ref[rs, :].astype(BF16)
            dob = do_ref[rs, :]
            dof = dob.astype(F32)
            qs = jnp.concatenate([q * m0, q * m1], axis=0).astype(BF16)
            dos = jnp.concatenate([dof * m0, dof * m1], axis=0).astype(BF16)
            s = (_dot_nt(qs, kb) * dm).astype(BF16)
            a = (_dot_nt(dos, vb) * dm).astype(BF16)
            ub = u_sc[...].astype(BF16)
            dk = _dot_tn(a, qs) + _dot_nt(vb, ub) * zt
            dv = _dot_tn(s, dos) + _dot(kb, ub) * zt
            u_sc[...] = cd * u_sc[...] + bm * _dot_tn((q * xi).astype(BF16), dob)
            dk_ref[rs, :] = (_rope_t(dk, cs, sn, 32, 64) * K_SCALE).astype(BF16)
            dv_ref[rs, :] = dv.astype(BF16)

    qkv, rope, tspec = _ret_specs(tb, True, nt)
    blk = pl.BlockSpec((tb, 128), lambda p, t: (nt - 1 - t, p))
    out_shape = [jax.ShapeDtypeStruct((T, RET_WIDTH), BF16)] * 2
    if swap is None:
        return pl.pallas_call(
            body, name=name, grid=(4, nt), in_specs=qkv + [blk] + rope + tspec, out_specs=[blk, blk],
            out_shape=out_shape, scratch_shapes=[pltpu.VMEM((128, 128), F32)],
            compiler_params=_cp(("parallel", "arbitrary")))(proj, proj, proj, do, cos, ss, *tabs)
    return pl.pallas_call(
        body, name=name, grid=(4, nt), in_specs=qkv + [blk] + rope + tspec + [ANY], out_specs=[blk, blk, ANY],
        out_shape=out_shape + [jax.ShapeDtypeStruct((4,) + swap.shape[2:], swap.dtype)],
        scratch_shapes=[pltpu.VMEM((128, 128), F32)] + list(SWAP_SCRATCH),
        compiler_params=_cp(("arbitrary", "arbitrary")))(proj, proj, proj, do, cos, ss, *tabs, swap)


def _mix_bwd(dmixed, o_ret, proj, y_mla, gnw, *, name):
    T = dmixed.shape[0]
    tm = min(T, 512)

    def body(dm_ref, o_ref, g_ref, ym_ref, gnw_ref, do_ref, dg_ref, dom_ref, dl_ref, dw_ref):
        @pl.when(pl.program_id(0) == 0)
        def _():
            dw_ref[...] = jnp.zeros_like(dw_ref)
        m0, m1 = _head_masks((tm, 128))
        lane = lax.broadcasted_iota(jnp.int32, (tm, 128), 1)
        delta = jnp.zeros((tm, 128), F32)

        def gsum(z):
            return jnp.sum(z * m0, axis=1, keepdims=True) * m0 + jnp.sum(z * m1, axis=1, keepdims=True) * m1

        for p in range(4):
            cs = slice(128 * p, 128 * p + 128)
            dy = dm_ref[:, cs]
            o = o_ref[:, cs]
            g = g_ref[:, cs]
            w = gnw_ref[:, cs]
            d = o - gsum(o) * (1.0 / 64)
            rstd = lax.rsqrt(gsum(d * d) * (1.0 / 64) + EPS)
            oh = d * rstd
            sg = _sigmoid(g)
            dn = dy * (g * sg)
            dg_ref[:, cs] = (dy * (oh * w) * (sg * (1.0 + g * (1.0 - sg)))).astype(BF16)
            dw_ref[:, cs] += jnp.sum(dn * oh, axis=0, keepdims=True)
            doh = dn * w
            do = rstd * (doh - gsum(doh) * (1.0 / 64) - oh * (gsum(doh * oh) * (1.0 / 64)))
            do_ref[:, cs] = do.astype(BF16)
            dom = dm_ref[:, 512 + 128 * p:512 + 128 * p + 128]
            dom_ref[:, cs] = dom.astype(BF16)
            pr = dom * ym_ref[:, cs].astype(F32)
            delta = jnp.where(lane == 2 * p, jnp.sum(pr * m0, axis=1, keepdims=True), delta)
            delta = jnp.where(lane == 2 * p + 1, jnp.sum(pr * m1, axis=1, keepdims=True), delta)
        dl_ref[...] = delta.T[0:MLA_HEADS]

    half = pl.BlockSpec((tm, 512), lambda i: (i, 0))
    return pl.pallas_call(
        body, name=name, grid=(T // tm,),
        in_specs=[pl.BlockSpec((tm, 1024), lambda i: (i, 0)), half, pl.BlockSpec((tm, 512), lambda i: (i, 3)),
                  half, pl.BlockSpec((1, 512), lambda i: (0, 0))],
        out_specs=[half, half, half, pl.BlockSpec((MLA_HEADS, tm), lambda i: (0, i)),
                   pl.BlockSpec((1, 512), lambda i: (0, 0))],
        out_shape=[jax.ShapeDtypeStruct((T, 512), BF16)] * 3 + [jax.ShapeDtypeStruct((MLA_HEADS, T), F32),
                                                                jax.ShapeDtypeStruct((1, 512), F32)],
        compiler_params=_cp(("arbitrary",)))(dmixed, o_ret, proj, y_mla, gnw)


def _mla_prep_fwd(proj, qnw, kvnw, wuq, wk, wv, cos, ss, *, name):
    T = proj.shape[0]
    tm = min(T, 512)

    def body(lat_ref, qnw_ref, kvnw_ref, wuq_ref, wk_ref, wv_ref, cos_ref, ss_ref,
             q_ref, k_ref, v_ref, cqn_ref, ckvn_ref):
        cq = lat_ref[:, 0:256]
        ckv = lat_ref[:, 256:384]
        g3 = lat_ref[:, 384:512]
        cqn = (cq * lax.rsqrt(jnp.mean(cq * cq, axis=-1, keepdims=True) + EPS) * qnw_ref[...]).astype(BF16)
        ckvn = (ckv * lax.rsqrt(jnp.mean(ckv * ckv, axis=-1, keepdims=True) + EPS) * kvnw_ref[...]).astype(BF16)
        cqn_ref[...] = cqn
        ckvn_ref[...] = ckvn
        cs, sn = cos_ref[...], ss_ref[...]
        q = _dot_nt(cqn, wuq_ref[...])
        k = _dot_nt(ckvn, wk_ref[...])
        kpe = _rope(g3, cs, sn, 16, 32)
        for h in range(MLA_HEADS):
            hs = slice(128 * h, 128 * h + 128)
            q_ref[:, hs] = (_rope(q[:, hs], cs, sn, 16, 32) * SCALE).astype(BF16)
            k_ref[:, hs] = (k[:, hs] + kpe).astype(BF16)
        v = _dot_nt(ckvn, wv_ref[...])
        lane = lax.broadcasted_iota(jnp.int32, (tm, 128), 1)
        for p in range(4):
            vp = v[:, 128 * p:128 * p + 128]
            v_ref[:, 256 * p:256 * p + 128] = jnp.where(lane < 64, vp, 1.0).astype(BF16)
            v_ref[:, 256 * p + 128:256 * p + 256] = jnp.where(lane < 64, 1.0, vp).astype(BF16)

    def full(shape):
        return pl.BlockSpec(shape, lambda i: (0, 0))

    def row(w):
        return pl.BlockSpec((tm, w), lambda i: (i, 0))

    return pl.pallas_call(
        body, name=name, grid=(T // tm,),
        in_specs=[pl.BlockSpec((tm, 512), lambda i: (i, 4)), full((1, 256)), full((1, 128)), full((1024, 256)),
                  full((1024, 128)), full((512, 128)), row(128), row(128)],
        out_specs=[row(1024), row(1024), row(1024), row(256), row(128)],
        out_shape=[jax.ShapeDtypeStruct((T, 1024), BF16), jax.ShapeDtypeStruct((T, 1024), BF16),
                   jax.ShapeDtypeStruct((T, 1024), BF16), jax.ShapeDtypeStruct((T, 256), BF16),
                   jax.ShapeDtypeStruct((T, 128), BF16)],
        compiler_params=_cp(("parallel",)))(proj, qnw, kvnw, wuq, wk, wv, cos, ss)


def _mla_prep_bwd(dq, dk, dv, proj, qnw, kvnw, wuq_t, wk_t, wv_t, cos, ss, *, name):
    T = proj.shape[0]
    tm = min(T, 512)

    def body(dq_ref, dk_ref, dv_ref, lat_ref, qnw_ref, kvnw_ref, wuq_ref, wk_ref, wv_ref, cos_ref, ss_ref,
             dlat_ref, dqp_ref, dqnw_ref, dkvnw_ref):
        @pl.when(pl.program_id(0) == 0)
        def _():
            dqnw_ref[...] = jnp.zeros_like(dqnw_ref)
            dkvnw_ref[...] = jnp.zeros_like(dkvnw_ref)
        cs, sn = cos_ref[...], ss_ref[...]
        dkpe = jnp.zeros((tm, 128), F32)
        for h in range(MLA_HEADS):
            hs = slice(128 * h, 128 * h + 128)
            dqp_ref[:, hs] = _rope_t(dq_ref[:, hs] * SCALE, cs, sn, 16, 32).astype(BF16)
            dkpe = dkpe + dk_ref[:, hs]
        lane = lax.broadcasted_iota(jnp.int32, (tm, 128), 1)
        rope_lane = (lane >= MLA_NOPE) & (lane < MLA_NOPE + MLA_ROPE)
        dg3 = jnp.where(rope_lane, _rope_t(jnp.where(rope_lane, dkpe, 0.0), cs, sn, 16, 32), 0.0)

        def norm_bwd(x, w, dn):
            r = lax.rsqrt(jnp.mean(x * x, axis=-1, keepdims=True) + EPS)
            xh = x * r
            g = dn * w
            return r * (g - xh * jnp.mean(g * xh, axis=-1, keepdims=True)), jnp.sum(dn * xh, axis=0, keepdims=True)

        dcqn = _dot(dqp_ref[...], wuq_ref[...])
        dcq, dqnw = norm_bwd(lat_ref[:, 0:256], qnw_ref[...], dcqn)
        dckvn = _dot(dk_ref[...].astype(BF16), wk_ref[...]) + _dot(dv_ref[...], wv_ref[...])
        dckv, dkvnw = norm_bwd(lat_ref[:, 256:384], kvnw_ref[...], dckvn)
        dqnw_ref[...] += dqnw
        dkvnw_ref[...] += dkvnw
        dlat_ref[:, 0:256] = dcq.astype(BF16)
        dlat_ref[:, 256:384] = dckv.astype(BF16)
        dlat_ref[:, 384:512] = dg3.astype(BF16)

    def full(shape):
        return pl.BlockSpec(shape, lambda i: (0, 0))

    def row(w):
        return pl.BlockSpec((tm, w), lambda i: (i, 0))

    return pl.pallas_call(
        body, name=name, grid=(T // tm,),
        in_specs=[row(1024), row(1024), row(512), pl.BlockSpec((tm, 512), lambda i: (i, 4)), full((1, 256)),
                  full((1, 128)), full((1024, 256)), full((1024, 128)), full((512, 128)), row(128), row(128)],
        out_specs=[row(512), row(1024), full((1, 256)), full((1, 128))],
        out_shape=[jax.ShapeDtypeStruct((T, 512), BF16), jax.ShapeDtypeStruct((T, 1024), BF16),
                   jax.ShapeDtypeStruct((1, 256), F32), jax.ShapeDtypeStruct((1, 128), F32)],
        compiler_params=_cp(("arbitrary",)))(dq, dk, dv, proj, qnw, kvnw, wuq_t, wk_t, wv_t, cos, ss)


def _flash_fwd(q, k, v1, *, name, gather=None):
    T = q.shape[0]
    tq = min(T, 512)
    tk = tq
    nq = T // tq

    def body(q_ref, k_ref, v_ref, *rest):
        if gather is None:
            y_ref, lse_ref = rest
        else:
            x_ref, y_ref, lse_ref, g_ref, *sems = rest
            start, forward, finish = _gather_phases(x_ref, g_ref, *sems)
            pl.when((pl.program_id(0) == 0) & (pl.program_id(1) == 0))(start)
            pl.when((pl.program_id(0) == 1) & (pl.program_id(1) == 0))(forward)
        attend(q_ref, k_ref, v_ref, y_ref, lse_ref)
        if gather is not None:
            pl.when((pl.program_id(0) == 3) & (pl.program_id(1) == nq - 1))(finish)

    def attend(q_ref, k_ref, v_ref, y_ref, lse_ref):
        qi = pl.program_id(1)
        row = lax.broadcasted_iota(jnp.int32, (tq, tk), 0)
        col = lax.broadcasted_iota(jnp.int32, (tq, tk), 1)

        def step(kb, carry, masked):
            ks = pl.ds(pl.multiple_of(kb * tk, tk), tk)
            new = []
            for h in range(2):
                hs = slice(128 * h, 128 * h + 128)
                m, acc = carry[h]
                s = _dot_nt(q_ref[:, hs], k_ref[ks, hs])
                if masked:
                    s = jnp.where(col <= row, s, NEG)
                mn = jnp.maximum(m, jnp.max(s, axis=1, keepdims=True))
                p = jnp.exp((s - mn).astype(BF16))
                acc = jnp.exp(m - mn) * acc + _dot(p, v_ref[ks, hs])
                new.append((mn, acc))
            return tuple(new)

        def unrolled(j, c):
            for u in range(FLASH_UNROLL):
                c = step(FLASH_UNROLL * j + u, c, False)
            return c

        init = (jnp.full((tq, 1), NEG, F32), jnp.zeros((tq, 128), F32))
        carry = lax.fori_loop(0, qi // FLASH_UNROLL, unrolled, (init, init))
        carry = lax.fori_loop(FLASH_UNROLL * (qi // FLASH_UNROLL), qi, lambda kb, c: step(kb, c, False), carry)
        (ma, acca), (mb, accb) = step(qi, carry, True)
        lane = lax.broadcasted_iota(jnp.int32, (tq, 128), 1)
        la, lb = pltpu.roll(acca, 64, 1), pltpu.roll(accb, 64, 1)
        y_ref[...] = jnp.where(lane < 64, acca / la, accb / lb).astype(BF16)
        lse_ref[0, 0] = jnp.broadcast_to(ma + jnp.log(acca[:, 64:65]), (tq, 128)).T[0:1]
        lse_ref[1, 0] = jnp.broadcast_to(mb + jnp.log(accb[:, 0:1]), (tq, 128)).T[0:1]

    in_specs = [pl.BlockSpec((tq, 256), lambda p, i: (i, p)), pl.BlockSpec((T, 256), lambda p, i: (0, p)),
                pl.BlockSpec((T, 256), lambda p, i: (0, p))]
    out_specs = [pl.BlockSpec((tq, 128), lambda p, i: (i, p)), pl.BlockSpec((2, 1, 1, tq), lambda p, i: (p, i, 0, 0))]
    out_shape = [jax.ShapeDtypeStruct((T, MLA_WIDTH), BF16), jax.ShapeDtypeStruct((MLA_HEADS, nq, 1, tq), F32)]
    if gather is None:
        return pl.pallas_call(body, name=name, grid=(4, nq), in_specs=in_specs, out_specs=out_specs,
                              out_shape=out_shape, compiler_params=_cp(("parallel", "arbitrary")))(q, k, v1)
    return pl.pallas_call(
        body, name=name, grid=(4, nq), in_specs=in_specs + [ANY], out_specs=out_specs + [ANY],
        out_shape=out_shape + [jax.ShapeDtypeStruct((N_DEV,) + gather.shape, gather.dtype)],
        scratch_shapes=list(GATHER_SCRATCH),
        compiler_params=_cp(("arbitrary", "arbitrary")))(q, k, v1, gather)


def _flash_bwd(q, k, v, do, lse, delta, *, name, exchange=None):
    T = q.shape[0]
    tq = min(T, 512)
    tk = tq
    nq = T // tq

    def body(q_ref, k_ref, v_ref, do_ref, lse_ref, dl_ref, *rest):
        if exchange is None:
            backward(q_ref, k_ref, v_ref, do_ref, lse_ref, dl_ref, *rest)
        else:
            p_ref, dqt_ref, dk_ref, dv_ref, got_ref, *sems = rest
            start, finish = _exchange_phases(p_ref, got_ref, *sems)
            pl.when((pl.program_id(0) == 0) & (pl.program_id(1) == 0))(start)
            backward(q_ref, k_ref, v_ref, do_ref, lse_ref, dl_ref, dqt_ref, dk_ref, dv_ref)
            pl.when((pl.program_id(0) == 3) & (pl.program_id(1) == nq - 1))(finish)

    def backward(q_ref, k_ref, v_ref, do_ref, lse_ref, dl_ref, dqt_ref, dk_ref, dv_ref):
        kb = pl.program_id(1)

        @pl.when(kb == 0)
        def _():
            dqt_ref[...] = jnp.zeros_like(dqt_ref)
        krow = lax.broadcasted_iota(jnp.int32, (tk, tq), 0)
        qcol = lax.broadcasted_iota(jnp.int32, (tk, tq), 1)
        masks = _head_masks((tk, 128))
        vms = [(v_ref[:, 128 * h:128 * h + 128].astype(F32) * masks[h]).astype(BF16) for h in range(2)]

        def step(qi, carry, masked):
            qs = pl.ds(pl.multiple_of(qi * tq, tq), tq)
            dob = do_ref[qs, :]
            dof = dob.astype(F32)
            dks, dv_acc = list(carry[:2]), carry[2]
            for h in range(2):
                hs = slice(128 * h, 128 * h + 128)
                kh = k_ref[:, hs]
                qh = q_ref[qs, hs]
                st = _dot_nt(kh, qh)
                pt = jnp.exp((st - lse_ref[h, qi]).astype(BF16))
                if masked:
                    pt = jnp.where(krow <= qcol, pt, jnp.zeros_like(pt))
                dv_acc = dv_acc + _dot(pt, (dof * masks[h]).astype(BF16))
                dpt = _dot_nt(vms[h], dob)
                dst = pt * (dpt - dl_ref[h, qi]).astype(BF16)
                dks[h] = dks[h] + _dot(dst, qh)
                dqt_ref[qi, hs, :] += _dot_tn(kh, dst)
            return dks[0], dks[1], dv_acc

        zero = jnp.zeros((tk, 128), F32)
        carry = step(kb, (zero, zero, zero), True)

        def two_steps(j, c):
            qi = kb + 1 + 2 * j
            return step(qi + 1, step(qi, c, False), False)

        pairs = (nq - 1 - kb) // 2
        carry = lax.fori_loop(0, pairs, two_steps, carry)
        dk0, dk1, dv_acc = lax.fori_loop(kb + 1 + 2 * pairs, nq, lambda qi, c: step(qi, c, False), carry)
        dk_ref[:, 0:128] = dk0
        dk_ref[:, 128:256] = dk1
        dv_ref[...] = dv_acc.astype(BF16)

    stat = pl.BlockSpec((2, nq, 1, tq), lambda p, j: (p, 0, 0, 0))
    in_specs = [pl.BlockSpec((T, 256), lambda p, j: (0, p)), pl.BlockSpec((tk, 256), lambda p, j: (j, p)),
                pl.BlockSpec((tk, 256), lambda p, j: (j, p)), pl.BlockSpec((T, 128), lambda p, j: (0, p)), stat, stat]
    out_specs = [pl.BlockSpec((None, nq, 256, tq), lambda p, j: (p, 0, 0, 0)),
                 pl.BlockSpec((tk, 256), lambda p, j: (j, p)), pl.BlockSpec((tk, 128), lambda p, j: (j, p))]
    out_shape = [jax.ShapeDtypeStruct((4, nq, 256, tq), F32), jax.ShapeDtypeStruct((T, 1024), F32),
                 jax.ShapeDtypeStruct((T, MLA_WIDTH), BF16)]
    if exchange is None:
        return pl.pallas_call(body, name=name, grid=(4, nq), in_specs=in_specs, out_specs=out_specs,
                              out_shape=out_shape,
                              compiler_params=_cp(("parallel", "arbitrary")))(q, k, v, do, lse, delta)
    return pl.pallas_call(
        body, name=name, grid=(4, nq), in_specs=in_specs + [ANY], out_specs=out_specs + [ANY],
        out_shape=out_shape + [jax.ShapeDtypeStruct(exchange.shape, exchange.dtype)],
        scratch_shapes=list(EXCHANGE_SCRATCH),
        compiler_params=_cp(("arbitrary", "arbitrary")))(q, k, v, do, lse, delta, exchange)


def _shift_down(x, n, prev8):
    r = pltpu.roll(x, n, 0)
    row = lax.broadcasted_iota(jnp.int32, prev8.shape, 0)
    first = jnp.where(row < n, pltpu.roll(prev8, n, 0), r[:8])
    if x.shape[0] == 8:
        return first
    return jnp.concatenate([first, r[8:]], axis=0)


def _shift_up(x, n, next8):
    tm = x.shape[0]
    r = pltpu.roll(x, tm - n, 0)
    row = lax.broadcasted_iota(jnp.int32, next8.shape, 0)
    last = jnp.where(row >= 8 - n, pltpu.roll(next8, 8 - n, 0), r[tm - 8:])
    return jnp.concatenate([r[:tm - 8], last], axis=0)


def _conv_pre(u, prev8, cw_ref, cb_ref):
    p1 = _shift_down(u, 1, prev8)
    p2 = _shift_down(u, 2, prev8)
    up = cb_ref[...] + cw_ref[0:1, :] * p2 + cw_ref[1:2, :] * p1 + cw_ref[2:3, :] * u
    return up, p1, p2


def _up_proj_conv(h2, w_up_t, cw, cb, *, name):
    T, K = h2.shape
    tm = min(T, 256)

    def body(h_ref, w_ref, cw_ref, cb_ref, u_ref, a_ref, carry_sc):
        @pl.when(pl.program_id(0) == 0)
        def _():
            carry_sc[...] = jnp.zeros_like(carry_sc)
        h = h_ref[...]
        for blk in range(2):
            ups = []
            for half in range(2):
                cs = slice((2 * blk + half) * FF_HALF, (2 * blk + half + 1) * FF_HALF)
                u = _dot_nt(h, w_ref[cs, :])
                u_ref[:, cs] = u
                prev = carry_sc[:, cs]
                ups.append(cb_ref[:, cs] + cw_ref[0:1, cs] * _shift_down(u, 2, prev)
                           + cw_ref[1:2, cs] * _shift_down(u, 1, prev) + cw_ref[2:3, cs] * u)
                carry_sc[:, cs] = u[tm - 8:]
            gate, val = ups
            a_ref[:, blk * FF_HALF:(blk + 1) * FF_HALF] = (gate * _sigmoid(gate) * val).astype(BF16)

    def full(shape):
        return pl.BlockSpec(shape, lambda i: (0, 0))

    return pl.pallas_call(
        body, name=name, grid=(T // tm,),
        in_specs=[pl.BlockSpec((tm, K), lambda i: (i, 0)), full(w_up_t.shape), full(cw.shape), full(cb.shape)],
        out_specs=[pl.BlockSpec((tm, 2 * D_FF), lambda i: (i, 0)), pl.BlockSpec((tm, D_FF), lambda i: (i, 0))],
        out_shape=[jax.ShapeDtypeStruct((T, 2 * D_FF), F32), jax.ShapeDtypeStruct((T, D_FF), BF16)],
        scratch_shapes=[pltpu.VMEM((8, 2 * D_FF), F32)],
        compiler_params=_cp(("arbitrary",)))(h2, w_up_t, cw, cb)


def _conv_bwd(u, da, cw, cb, *, name):
    T = u.shape[0]
    tm = min(T, 512)
    W = 2 * FF_HALF
    nt = T // tm

    def body(u_ref, prev_ref, next_ref, da_ref, dan_ref, cw_ref, cb_ref, du_ref, dw0_ref, dw1_ref, dw2_ref, db_ref):
        i = pl.program_id(1)

        @pl.when(i == 0)
        def _():
            for r in (dw0_ref, dw1_ref, dw2_ref, db_ref):
                r[...] = jnp.zeros_like(r)

        def dpre(u, prev8, da):
            up, p1, p2 = _conv_pre(u, prev8, cw_ref, cb_ref)
            gate, val = up[:, :FF_HALF], up[:, FF_HALF:]
            sg = _sigmoid(gate)
            dgate = da * val * (sg * (1.0 + gate * (1.0 - sg)))
            dval = da * (gate * sg)
            return jnp.concatenate([dgate, dval], axis=1), p1, p2

        u = u_ref[...]
        prev = jnp.where(i > 0, prev_ref[...], 0.0)
        dup, p1, p2 = dpre(u, prev, da_ref[...])
        dupn, _, _ = dpre(next_ref[...], u[tm - 8:], dan_ref[...])
        dupn = jnp.where(i < nt - 1, dupn, 0.0)
        du = cw_ref[2:3, :] * dup + cw_ref[1:2, :] * _shift_up(dup, 1, dupn) + cw_ref[0:1, :] * _shift_up(dup, 2, dupn)
        du_ref[...] = du.astype(BF16)
        dw0_ref[...] += jnp.sum(dup * p2, axis=0, keepdims=True)
        dw1_ref[...] += jnp.sum(dup * p1, axis=0, keepdims=True)
        dw2_ref[...] += jnp.sum(dup * u, axis=0, keepdims=True)
        db_ref[...] += jnp.sum(dup, axis=0, keepdims=True)

    nxt = lambda j, i: (jnp.minimum((i + 1) * (tm // 8), T // 8 - 1), j)
    vec = pl.BlockSpec((1, W), lambda j, i: (0, j))
    return pl.pallas_call(
        body, name=name, grid=(2, nt),
        in_specs=[pl.BlockSpec((tm, W), lambda j, i: (i, j)),
                  pl.BlockSpec((8, W), lambda j, i: (jnp.maximum(i * (tm // 8) - 1, 0), j)),
                  pl.BlockSpec((8, W), nxt),
                  pl.BlockSpec((tm, FF_HALF), lambda j, i: (i, j)), pl.BlockSpec((8, FF_HALF), nxt),
                  pl.BlockSpec((3, W), lambda j, i: (0, j)), vec],
        out_specs=[pl.BlockSpec((tm, W), lambda j, i: (i, j)), vec, vec, vec, vec],
        out_shape=[jax.ShapeDtypeStruct((T, 2 * D_FF), BF16)] + [jax.ShapeDtypeStruct((1, 2 * D_FF), F32)] * 4,
        compiler_params=_cp(("parallel", "arbitrary")))(u, u, u, da, da, cw, cb)


def _adamw(w, m, v, g_slots, *, name):
    R, C = w.shape
    ns = g_slots.shape[0]
    tr = _row_tile(R)

    def body(w_ref, m_ref, v_ref, g_ref, go_ref, d_ref, mo_ref, vo_ref):
        g = g_ref[0].astype(F32)
        for s in range(1, ns):
            g = g + g_ref[s].astype(F32)
        mn = ADAM_B1 * m_ref[...] + (1.0 - ADAM_B1) * g
        vn = ADAM_B2 * v_ref[...] + (1.0 - ADAM_B2) * (g * g)
        m_hat = mn / (1.0 - ADAM_B1 ** ADAM_STEP)
        v_hat = vn / (1.0 - ADAM_B2 ** ADAM_STEP)
        go_ref[...] = g
        d_ref[...] = -ADAM_LR * (m_hat / (jnp.sqrt(v_hat) + ADAM_EPS) + ADAM_WD * w_ref[...])
        mo_ref[...] = mn
        vo_ref[...] = vn

    blk = pl.BlockSpec((tr, C), lambda i: (i, 0))
    return pl.pallas_call(
        body, name=name, grid=(R // tr,),
        in_specs=[blk, blk, blk, pl.BlockSpec((ns, tr, C), lambda i: (0, i, 0))],
        out_specs=[blk] * 4, out_shape=[jax.ShapeDtypeStruct((R, C), F32)] * 4,
        compiler_params=_cp(("parallel",)))(w, m, v, g_slots)


def _place():
    return lax.axis_index("x"), lax.axis_index("y"), lax.axis_index("c")


GATHER_SCRATCH = (pltpu.SemaphoreType.DMA((7,)), pltpu.SemaphoreType.DMA((7,)), pltpu.SemaphoreType.DMA)
EXCHANGE_SCRATCH = (pltpu.SemaphoreType.DMA((3,)), pltpu.SemaphoreType.DMA((3,)), pltpu.SemaphoreType.DMA)


def _gather_phases(x_ref, out_ref, send_sems, recv_sems, local_sem):
    x_, y_, c_ = _place()
    me, sibling = (x_, y_, c_), (x_, y_, 1 - c_)
    chips = [(1 - x_, y_), (x_, 1 - y_), (1 - x_, 1 - y_)]

    def slot(px, py, pc):
        return out_ref.at[4 * px + 2 * py + pc]

    def copy(k, block, to, src=None):
        return pltpu.make_async_remote_copy(
            src_ref=slot(*block) if src is None else src, dst_ref=slot(*block),
            send_sem=send_sems.at[k], recv_sem=recv_sems.at[k], device_id=to, device_id_type=MESH)

    def mine():
        return pltpu.make_async_copy(x_ref, slot(*me), local_sem)

    def first():
        return [copy(0, me, sibling, src=x_ref)] + [copy(1 + j, me, (*chip, c_), src=x_ref)
                                                     for j, chip in enumerate(chips)]

    def passed():
        return [copy(4 + j, (*chip, c_), sibling) for j, chip in enumerate(chips)]

    def start():
        mine().start()
        for cp in first():
            cp.start()

    def forward():
        fwd = passed()
        for j, chip in enumerate(chips):
            copy(1 + j, (*chip, c_), me).wait_recv()
            fwd[j].start()

    def finish():
        copy(0, sibling, me).wait_recv()
        for j, chip in enumerate(chips):
            copy(4 + j, (*chip, 1 - c_), me).wait_recv()
        for cp in first() + passed():
            cp.wait_send()
        mine().wait()

    return start, forward, finish


def _exchange_phases(p_ref, out_ref, send_sems, recv_sems, local_sem):
    x_, y_, c_ = _place()
    me_k = 2 * x_ + y_
    chips = [(1 - x_, y_), (x_, 1 - y_), (1 - x_, 1 - y_)]

    def local():
        return pltpu.make_async_copy(p_ref.at[me_k], out_ref.at[me_k], local_sem)

    def copy(j, src_k, dst_k, chip):
        return pltpu.make_async_remote_copy(
            src_ref=p_ref.at[src_k], dst_ref=out_ref.at[dst_k], send_sem=send_sems.at[j],
            recv_sem=recv_sems.at[j], device_id=(*chip, c_), device_id_type=MESH)

    def sends():
        return [copy(j, 2 * px + py, me_k, (px, py)) for j, (px, py) in enumerate(chips)]

    def start():
        local().start()
        for cp in sends():
            cp.start()

    def finish():
        for j, (px, py) in enumerate(chips):
            copy(j, me_k, 2 * px + py, (px, py)).wait_recv()
        for cp in sends():
            cp.wait_send()
        local().wait()

    return start, finish


def _all_gather(x, *, name, in_vmem):
    def body(x_ref, out_ref, send_sems, recv_sems, local_sem):
        for phase in _gather_phases(x_ref, out_ref, send_sems, recv_sems, local_sem):
            phase()

    spec = pl.BlockSpec(memory_space=pltpu.VMEM) if in_vmem else ANY
    return pl.pallas_call(
        body, name=name, out_shape=jax.ShapeDtypeStruct((N_DEV,) + x.shape, x.dtype),
        in_specs=[spec], out_specs=spec, scratch_shapes=list(GATHER_SCRATCH),
        compiler_params=pltpu.CompilerParams(vmem_limit_bytes=VMEM_LIMIT))(x)


def _small_rows():
    table, row = [], 0
    for n, size in SMALL_VECTORS:
        table.append((n, size, row))
        row += -(-size // PACK_COLS)
    return table


def _ff_chunk_source(c):
    block, off = divmod(c * 128, FF_HALF)
    return (0, 2, 1, 3)[block] * FF_HALF + off


def _pack_small(parts, *, name):
    table = _small_rows()

    def body(*refs):
        out = refs[-1]
        out[...] = jnp.zeros_like(out)
        for ref, (n, size, row) in zip(refs, table):
            if size != 2 * D_FF:
                out[row:row + 1, 0:size] = ref[...]
                continue
            for c in range(size // 128):
                src = _ff_chunk_source(c)
                r, lane = divmod(c * 128, PACK_COLS)
                out[row + r:row + r + 1, lane:lane + 128] = ref[:, src:src + 128]

    return pl.pallas_call(body, name=name, out_shape=jax.ShapeDtypeStruct((SMALL_ROWS, PACK_COLS), F32))(
        *[parts[n] for n, _, _ in table])


def _sum_small(g, *, name):
    table = _small_rows()
    shapes = [(n, size) for n, size, _ in table if not n.startswith("conv_w")]
    shapes.insert(7, ("conv_w", 2 * D_FF))

    def body(g_ref, *outs):
        def total(row, width):
            acc = g_ref[0, row:row + 1, 0:width]
            for d in range(1, N_DEV):
                acc = acc + g_ref[d, row:row + 1, 0:width]
            return acc

        out_of = {n: o for (n, _), o in zip(shapes, outs)}
        for n, size, row in table:
            o, j = (out_of["conv_w"], int(n[-1])) if n.startswith("conv_w") else (out_of[n], 0)
            for i in range(-(-size // PACK_COLS)):
                width = min(PACK_COLS, size - PACK_COLS * i)
                o[j:j + 1, PACK_COLS * i:PACK_COLS * i + width] = total(row + i, width)

    out_shape = [jax.ShapeDtypeStruct((3 if n == "conv_w" else 1, size), F32) for n, size in shapes]
    res = pl.pallas_call(body, name=name, out_shape=out_shape)(g)
    return {n: r for (n, _), r in zip(shapes, res)}


def _adamw_small(ws, ms, vs, gs, *, name):
    k = len(ws)

    def body(*refs):
        w_refs, m_refs, v_refs, g_refs = (refs[i * k:(i + 1) * k] for i in range(4))
        outs = refs[4 * k:]
        for i in range(k):
            g = g_refs[i][...]
            mn = ADAM_B1 * m_refs[i][...] + (1.0 - ADAM_B1) * g
            vn = ADAM_B2 * v_refs[i][...] + (1.0 - ADAM_B2) * (g * g)
            m_hat = mn / (1.0 - ADAM_B1 ** ADAM_STEP)
            v_hat = vn / (1.0 - ADAM_B2 ** ADAM_STEP)
            outs[i][...] = g
            outs[k + i][...] = -ADAM_LR * (m_hat / (jnp.sqrt(v_hat) + ADAM_EPS) + ADAM_WD * w_refs[i][...])
            outs[2 * k + i][...] = mn
            outs[3 * k + i][...] = vn

    out_shape = [jax.ShapeDtypeStruct(w.shape, F32) for _ in range(4) for w in ws]
    res = pl.pallas_call(body, name=name, out_shape=out_shape)(*ws, *ms, *vs, *gs)
    return [res[i * k:(i + 1) * k] for i in range(4)]


SWAP_SCRATCH = (pltpu.SemaphoreType.DMA((4,)), pltpu.SemaphoreType.DMA((4,)))


def _swap_phases(g_ref, out_ref, send_sems, recv_sems):
    x_, y_, c_ = _place()

    def copies():
        return [pltpu.make_async_remote_copy(src_ref=g_ref.at[k, 1 - c_], dst_ref=out_ref.at[k],
                                             send_sem=send_sems.at[k], recv_sem=recv_sems.at[k],
                                             device_id=(x_, y_, 1 - c_), device_id_type=MESH) for k in range(4)]

    def start():
        for cp in copies():
            cp.start()

    def finish():
        for cp in copies():
            cp.wait()

    return start, finish


def _swap_sibling(g, *, name):
    def body(g_ref, out_ref, send_sems, recv_sems):
        for phase in _swap_phases(g_ref, out_ref, send_sems, recv_sems):
            phase()

    return pl.pallas_call(
        body, name=name, out_shape=jax.ShapeDtypeStruct((4,) + g.shape[2:], g.dtype), in_specs=[ANY], out_specs=ANY,
        scratch_shapes=list(SWAP_SCRATCH))(g)


def _row_tile(R):
    for cand in (256, 400, 200):
        if R % cand == 0:
            return cand
    return R


def _add_own(g, b, *, name, out_dtype):
    n, _, R, C = g.shape
    tr = _row_tile(R)

    def body(c_ref, g_ref, b_ref, o_ref):
        del c_ref
        o_ref[...] = (g_ref[...] + b_ref[...]).astype(out_dtype)

    blk = pl.BlockSpec((None, tr, C), lambda s, i, c: (s, i, 0))
    grid_spec = pltpu.PrefetchScalarGridSpec(
        num_scalar_prefetch=1, grid=(n, R // tr),
        in_specs=[pl.BlockSpec((None, None, tr, C), lambda s, i, c: (s, c[0], i, 0)), blk], out_specs=blk)
    core = jnp.reshape(lax.axis_index("c"), (1,)).astype(jnp.int32)
    return pl.pallas_call(body, name=name, grid_spec=grid_spec, out_shape=jax.ShapeDtypeStruct(b.shape, out_dtype),
                          compiler_params=_cp(("parallel", "parallel")))(core, g, b)


def _pack_local(parts, group):
    table, rows = group
    segs = []
    for n, r, rp, tr in table:
        w = parts[n].T if tr else parts[n]
        segs.append(jnp.pad(w.reshape(r, PACK_COLS), ((0, rp - r), (0, 0))))
    segs.append(jnp.zeros((rows - sum(rp for _, _, rp, _ in table), PACK_COLS), segs[0].dtype))
    return jnp.concatenate(segs, axis=0)


def _unpack_local(packed, like, group):
    out, off = {}, 0
    for n, r, rp, tr in group[0]:
        rows, cols = like[n].shape
        seg = packed[off:off + r]
        out[n] = (seg.reshape(cols, rows).T if tr else seg)[None]
        off += rp
    return out


def _segments(g, group):
    out, off = {}, 0
    for n, r, rp, _ in group[0]:
        out[n] = g[:, off:off + r]
        off += rp
    return out


def _pack_grads(parts, group):
    table, rows = group
    segs = [jnp.pad(parts[n], ((0, 0), (0, rp - parts[n].shape[1]), (0, 0))) for n, _, rp, _ in table]
    segs.append(jnp.zeros((N_DEV, rows - sum(rp for _, _, rp, _ in table), PACK_COLS), F32))
    return jnp.concatenate(segs, axis=1)


def _owner_rows_early(g):
    g_in = jnp.concatenate([g["w_in_t"][:2432], g["w_in_t"][2496:2528]], axis=0).reshape(N_DEV, 308, PACK_COLS)
    g_uq = g["w_uq_t"].reshape(N_DEV, 128, MLA_Q_RANK)[:, :96].reshape(N_DEV, 24, PACK_COLS)
    g_ukv = jnp.concatenate([g["w_k_t"].reshape(N_DEV, 128, MLA_KV_RANK)[:, :64],
                             g["w_v_t"].reshape(N_DEV, 64, MLA_KV_RANK)], axis=1).reshape(N_DEV, 16, PACK_COLS)
    return dict(w_in=g_in, w_uq=g_uq, w_ukv=g_ukv)


def _owner_rows_late(g):
    g_up = g["w_up_t"].reshape(2, 2, 2, 704, PACK_COLS).swapaxes(0, 1).reshape(N_DEV, 704, PACK_COLS)
    return dict(w_out=g["w_out"].reshape(N_DEV, 128, PACK_COLS), w_up=g_up,
                w_down=g["w_down"].reshape(N_DEV, 352, PACK_COLS))


def _reduce_to_pairs(gp, *, name):
    gp = gp.reshape(4, 2, gp.shape[1], PACK_COLS)
    return _add_own(gp, _swap_sibling(gp, name=name + "_swap"), out_dtype=BF16, name=name + "_sum")


def _interleave_ff(w):
    g, v = w[..., :D_FF], w[..., D_FF:]
    return jnp.concatenate([g[..., :FF_HALF], v[..., :FF_HALF], g[..., FF_HALF:], v[..., FF_HALF:]], axis=-1)


def _rope_tables(pos):
    p = pos.astype(F32)[:, None]
    inv_r = ROPE_BASE ** (-jnp.arange(0, RET_HEAD_DIM, 2, dtype=F32) / RET_HEAD_DIM)
    ang = p * jnp.tile(inv_r, 4)
    sign_r = jnp.tile(jnp.concatenate([-jnp.ones((32,), F32), jnp.ones((32,), F32)]), 2)
    cos_r, ss_r = jnp.cos(ang), jnp.sin(ang) * sign_r
    inv_m = ROPE_BASE ** (-jnp.arange(0, MLA_ROPE, 2, dtype=F32) / MLA_ROPE)
    ang = p * jnp.concatenate([jnp.zeros((64,), F32), inv_m, inv_m, jnp.zeros((32,), F32)])
    sign_m = jnp.concatenate([jnp.zeros((64,), F32), -jnp.ones((16,), F32), jnp.ones((16,), F32), jnp.zeros((32,), F32)])
    cos_m, ss_m = jnp.cos(ang), jnp.sin(ang) * sign_m
    return cos_r, ss_r, cos_m, ss_m


def _prep_early(gathered):
    seg = _segments(gathered, EARLY)
    w_in_t = seg["w_in"].reshape(IN_WIDTH, D_MODEL)
    z = lambda n: jnp.zeros((n, D_MODEL), BF16)
    w_in_t = jnp.concatenate([w_in_t[:2432], z(64), w_in_t[2432:2464], z(32)], axis=0)
    w_uq_t = jnp.pad(seg["w_uq"].reshape(MLA_HEADS, 96, MLA_Q_RANK), ((0, 0), (0, 32), (0, 0))).reshape(1024, MLA_Q_RANK)
    ukv = seg["w_ukv"].reshape(MLA_HEADS, 128, MLA_KV_RANK)
    w_k_t = jnp.pad(ukv[:, :64], ((0, 0), (0, 64), (0, 0))).reshape(1024, MLA_KV_RANK)
    w_v_t = ukv[:, 64:].reshape(512, MLA_KV_RANK)
    return dict(w_in_t=w_in_t, w_uq_t=w_uq_t, w_k_t=w_k_t, w_v_t=w_v_t)


def _prep_late(gathered):
    seg = _segments(gathered, LATE)
    w_up_t = seg["w_up"].reshape(2, 2, 2, 704, D_MODEL).swapaxes(0, 1).reshape(2 * D_FF, D_MODEL)
    return dict(w_out=seg["w_out"].reshape(1024, D_MODEL), w_up_t=w_up_t, w_down=seg["w_down"].reshape(D_FF, D_MODEL))


def _local_step(x, pos, tgt, early, sm, late):
    dist = not isinstance(late, dict)
    cos_r, ss_r, cos_m, ss_m = _rope_tables(pos)
    tabs = _ret_tables()

    if dist:
        h, gathered = _rmsnorm_fwd(x, sm["attn_norm_w"], gather=early, name="attn_norm")
        W = _prep_early(gathered)
    else:
        h = _rmsnorm_fwd(x, sm["attn_norm_w"], name="attn_norm")
        W = early
    proj = _mm(h, W["w_in_t"], bt=True, name="in_proj")
    y_ret, o_ret = _ret_fwd(proj, cos_r, ss_r, tabs, sm["ret_gn_w"], name="ret_fwd")
    q, k, v1, cqn, ckvn = _mla_prep_fwd(proj, sm["mla_q_norm_w"], sm["mla_kv_norm_w"], W["w_uq_t"], W["w_k_t"],
                                       W["w_v_t"], cos_m, ss_m, name="mla_prep")
    T = x.shape[0]
    tq = min(T, 512)
    if dist:
        y_mla, lse, gathered = _flash_fwd(q, k, v1, gather=late, name="mla_attn")
        W = {**W, **_prep_late(gathered)}
    else:
        y_mla, lse = _flash_fwd(q, k, v1, name="mla_attn")
        W = {**W, **late}
    mixed = jnp.concatenate([y_ret, y_mla], axis=1)
    x1 = _mm(mixed, W["w_out"], add=x, name="out_proj")
    h2 = _rmsnorm_fwd(x1, sm["ffn_norm_w"], name="ffn_norm")
    u, a = _up_proj_conv(h2, W["w_up_t"], sm["conv_w"], sm["conv_b"], name="up_proj_conv")
    x2 = _mm(a, W["w_down"], add=x1, name="down_proj")
    loss, dx2, dx2b, d_final = _loss_head(x2, tgt, sm["final_norm_w"], name="loss_head")

    g = {}
    g["w_down"] = _mm_tn(a, dx2b, name="dw_down")
    da = _mm(dx2b, W["w_down"], bt=True, name="d_act")
    du, dcw0, dcw1, dcw2, dcb = _conv_bwd(u, da, sm["conv_w"], sm["conv_b"], name="conv_bwd")
    g["w_up_t"] = _mm_tn(du, h2, name="dw_up")
    dh2 = _mm(du, W["w_up_t"], name="d_h2")
    dx1, d_ffn = _rmsnorm_bwd(x1, sm["ffn_norm_w"], dh2, dx2, name="ffn_norm_bwd")

    g["w_out"] = _mm_tn(mixed, dx1, name="dw_out")
    dmixed = _mm(dx1, W["w_out"], bt=True, name="d_mixed")
    do_ret, dg, do_mla, delta, d_gn = _mix_bwd(dmixed, o_ret, proj, y_mla, sm["ret_gn_w"], name="mix_bwd")
    drq = _ret_bwd_dq(proj, do_ret, cos_r, ss_r, tabs, name="ret_bwd_dq")
    delta_r = delta.reshape(MLA_HEADS, T // tq, 1, tq)
    if dist:
        gl = _pack_grads(_owner_rows_late(g), LATE).reshape(4, 2, LATE[1], PACK_COLS)
        drk, drv, theirs = _ret_bwd_dkv(proj, do_ret, cos_r, ss_r, tabs, swap=gl, name="ret_bwd_dkv")
        pair = _add_own(gl, theirs, out_dtype=BF16, name="grad_late_sum")
        dqt, dk, dv, slots_late = _flash_bwd(q, k, v1, do_mla, lse, delta_r, exchange=pair, name="mla_attn_bwd")
    else:
        drk, drv = _ret_bwd_dkv(proj, do_ret, cos_r, ss_r, tabs, name="ret_bwd_dkv")
        dqt, dk, dv = _flash_bwd(q, k, v1, do_mla, lse, delta_r, name="mla_attn_bwd")
        slots_late = None
    dq = dqt.transpose(1, 3, 0, 2).reshape(T, MLA_HEADS * 128)
    dlat, dqp, d_qn, d_kvn = _mla_prep_bwd(dq, dk, dv, proj, sm["mla_q_norm_w"], sm["mla_kv_norm_w"], W["w_uq_t"],
                                           W["w_k_t"], W["w_v_t"], cos_m, ss_m, name="mla_prep_bwd")
    g["w_uq_t"] = _mm_tn(dqp, cqn, name="dw_uq")
    g["w_k_t"] = _mm_tn(dk, ckvn, name="dw_ukv_k")
    g["w_v_t"] = _mm_tn(dv, ckvn, name="dw_ukv_v")
    dproj = jnp.concatenate([drq, drk, drv, dg, dlat], axis=1)
    g["w_in_t"] = _mm_tn(dproj, h, name="dw_in")
    if dist:
        pair = _reduce_to_pairs(_pack_grads(_owner_rows_early(g), EARLY), name="grad_early")
        dh, slots_early = _mm(dproj, W["w_in_t"], exchange=pair, name="d_h")
    else:
        dh = _mm(dproj, W["w_in_t"], name="d_h")
        slots_early = None
    grad_x, d_attn = _rmsnorm_bwd(x, sm["attn_norm_w"], dh, dx1, name="attn_norm_bwd")

    small = dict(attn_norm_w=d_attn, ret_gn_w=d_gn, mla_q_norm_w=d_qn, mla_kv_norm_w=d_kvn, ffn_norm_w=d_ffn,
                 conv_b=dcb, final_norm_w=d_final, conv_w0=dcw0, conv_w1=dcw1, conv_w2=dcw2, loss=loss)
    return loss, grad_x, g, small, slots_early, slots_late


def kernel(x, positions, attn_norm_w, w_in, ret_gn_w, mla_q_norm_w, w_uq, mla_kv_norm_w, w_ukv, w_out, ffn_norm_w, w_up, conv_w, conv_b, w_down, final_norm_w, loss_target, m_attn_norm_w, m_w_in, m_ret_gn_w, m_mla_q_norm_w, m_w_uq, m_mla_kv_norm_w, m_w_ukv, m_w_out, m_ffn_norm_w, m_w_up, m_conv_w, m_conv_b, m_w_down, m_final_norm_w, v_attn_norm_w, v_w_in, v_ret_gn_w, v_mla_q_norm_w, v_w_uq, v_mla_kv_norm_w, v_w_ukv, v_w_out, v_ffn_norm_w, v_w_up, v_conv_w, v_conv_b, v_w_down, v_final_norm_w):
    a = dict(locals())
    x_, y_, c_ = _place()
    dev = 4 * x_ + 2 * y_ + c_

    shard = {n: a[n][0] for n in BIG_NAMES}
    shard16 = {n: w.astype(BF16) for n, w in shard.items()}
    cw_pad = jnp.pad(conv_w[0].reshape(-1), (0, 24 * 128 - 3 * 704)).reshape(24, 128)
    cw_all = _all_gather(cw_pad, name="gather_conv_w", in_vmem=True)
    conv_w_full = cw_all.reshape(N_DEV, -1)[:, :3 * 704].reshape(N_DEV, 3, 704).transpose(1, 0, 2).reshape(3, 2 * D_FF)
    sm = dict(attn_norm_w=attn_norm_w, ret_gn_w=ret_gn_w, mla_q_norm_w=mla_q_norm_w, mla_kv_norm_w=mla_kv_norm_w,
              ffn_norm_w=ffn_norm_w, final_norm_w=final_norm_w.reshape(1, D_MODEL),
              conv_w=_interleave_ff(conv_w_full), conv_b=_interleave_ff(conv_b))

    loss, grad_x, _, gs, slots_early, slots_late = _local_step(
        x[0], positions[0], loss_target[0], _pack_local(shard16, EARLY), sm, _pack_local(shard16, LATE))

    big = [{}, {}, {}, {}]
    for group, slots, tag in ((EARLY, slots_early, "early"), (LATE, slots_late, "late")):
        names_g = [n for n, _, _, _ in group[0]]
        res = _adamw(_pack_local({n: shard[n] for n in names_g}, group),
                     _pack_local({n: a["m_" + n][0] for n in names_g}, group),
                     _pack_local({n: a["v_" + n][0] for n in names_g}, group), slots, name="adamw_" + tag)
        for kind in range(4):
            big[kind].update(_unpack_local(res[kind], shard, group))

    packed = _pack_small(gs, name="pack_small_grads")
    tot = _sum_small(_all_gather(packed, name="gather_small_grads", in_vmem=True), name="sum_small_grads")
    loss_out = tot["loss"][0, 0]
    g_cw = lax.dynamic_slice_in_dim(tot["conv_w"], dev * 704, 704, axis=1)

    def rows_of(prefix):
        return [a[prefix + n].reshape(1, size) for n, size in SMALL]

    sml = _adamw_small(rows_of(""), rows_of("m_"), rows_of("v_"), [tot[n] for n, _ in SMALL], name="adamw_small")
    cwo = _adamw(conv_w[0], m_conv_w[0], v_conv_w[0], g_cw[None], name="adamw_conv_w")

    def small_of(kind, n):
        return sml[kind][[nm for nm, _ in SMALL].index(n)].reshape(a[n].shape)

    names = ['attn_norm_w', 'w_in', 'ret_gn_w', 'mla_q_norm_w', 'w_uq', 'mla_kv_norm_w', 'w_ukv', 'w_out',
             'ffn_norm_w', 'w_up', 'conv_w', 'conv_b', 'w_down', 'final_norm_w']
    outs = [loss_out, grad_x[None]]
    for kind in range(4):
        for n in names:
            if n == "conv_w":
                outs.append(cwo[kind][None])
            elif n in big[kind]:
                outs.append(big[kind][n])
            else:
                outs.append(small_of(kind, n))
    return tuple(outs)
```

```python
import functools

import numpy as np
import jax
import jax.numpy as jnp
from jax import lax
from jax.experimental import pallas as pl
from jax.experimental.pallas import tpu as pltpu

F32 = jnp.float32
BF16 = jnp.bfloat16
MESH = pl.DeviceIdType.MESH
ANY = pl.BlockSpec(memory_space=pl.ANY)

D_MODEL = 1024
RET_HEADS = 8
RET_HEAD_DIM = 64
RET_WIDTH = 512
RET_CHUNK = 128
MLA_HEADS = 8
MLA_NOPE = 64
MLA_ROPE = 32
MLA_V = 64
MLA_Q_RANK = 256
MLA_KV_RANK = 128
MLA_WIDTH = 512
IN_WIDTH = 2464
IN_PAD = 2560
D_FF = 2816
FF_HALF = 1408
ROPE_BASE = 10000.0
EPS = 1e-6
SCALE = float((MLA_NOPE + MLA_ROPE) ** -0.5)
K_SCALE = 0.125
N_DEV = 8

ADAM_LR = 0.001
ADAM_B1 = 0.9
ADAM_B2 = 0.999
ADAM_EPS = 1e-08
ADAM_WD = 0.01
ADAM_STEP = 10

VMEM_LIMIT = 56 * 1024 * 1024
MM_BUDGET = 40 * 1024 * 1024
NEG = -1e30
FLASH_UNROLL = 4

PACK_COLS = 1024
EARLY = ((("w_in", 308, 320, True), ("w_uq", 24, 32, True), ("w_ukv", 16, 16, True)), 384)
LATE = ((("w_out", 128, 128, False), ("w_up", 704, 704, True), ("w_down", 352, 352, False)), 1200)
BIG_NAMES = ("w_in", "w_uq", "w_ukv", "w_out", "w_up", "w_down")
SMALL = (("attn_norm_w", 1024), ("ret_gn_w", 512), ("mla_q_norm_w", 256), ("mla_kv_norm_w", 128),
         ("ffn_norm_w", 1024), ("conv_b", 5632), ("final_norm_w", 1024))
SMALL_VECTORS = SMALL + (("conv_w0", 5632), ("conv_w1", 5632), ("conv_w2", 5632), ("loss", 128))
SMALL_ROWS = 32


def _cp(sem=None, vmem=VMEM_LIMIT):
    return pltpu.CompilerParams(dimension_semantics=sem, vmem_limit_bytes=vmem)


def _dot(a, b):
    return jnp.dot(a, b, preferred_element_type=F32)


def _dot_nt(a, b):
    return lax.dot_general(a, b, (((1,), (1,)), ((), ())), preferred_element_type=F32)


def _dot_tn(a, b):
    return lax.dot_general(a, b, (((0,), (0,)), ((), ())), preferred_element_type=F32)


def _sigmoid(x):
    return 0.5 * jnp.tanh(0.5 * x) + 0.5


def _partner(x, half, period):
    n = x.shape[-1]
    lane = lax.broadcasted_iota(jnp.int32, x.shape, 1)
    return jnp.where((lane % period) < half, pltpu.roll(x, n - half, 1), pltpu.roll(x, half, 1))


def _rope(x, cos, ss, half, period):
    return x * cos + _partner(x, half, period) * ss


def _rope_t(dy, cos, ss, half, period):
    return dy * cos - _partner(dy, half, period) * ss


def _head_masks(shape):
    lane = lax.broadcasted_iota(jnp.int32, shape, 1)
    m0 = (lane < 64).astype(F32)
    return m0, 1.0 - m0


def _mm(a, b, *, name, add=None, out_dtype=F32, bt=False):
    M, K = a.shape
    N = b.shape[0] if bt else b.shape[1]
    osz = jnp.dtype(out_dtype).itemsize
    per_row = 2 * (K * a.dtype.itemsize + N * osz + (N * 4 if add is not None else 0))
    tm = 128
    for cand in (512, 256):
        if M % cand == 0 and cand * per_row + 4 * K * N <= MM_BUDGET:
            tm = cand
            break
    tm = min(tm, M)
    mul = _dot_nt if bt else _dot
    n_in = 2 if add is None else 3

    def body(*refs):
        a_ref, b_ref = refs[:2]
        acc = mul(a_ref[...].astype(BF16), b_ref[...])
        if add is not None:
            acc = refs[2][...] + acc
        refs[n_in][...] = acc.astype(out_dtype)

    in_specs = [pl.BlockSpec((tm, K), lambda i: (i, 0)), pl.BlockSpec(b.shape, lambda i: (0, 0))]
    args = [a, b]
    if add is not None:
        in_specs.append(pl.BlockSpec((tm, N), lambda i: (i, 0)))
        args.append(add)
    return pl.pallas_call(
        body, name=name, grid=(M // tm,), in_specs=in_specs, out_specs=pl.BlockSpec((tm, N), lambda i: (i, 0)),
        out_shape=jax.ShapeDtypeStruct((M, N), out_dtype), compiler_params=_cp(("parallel",)))(*args)


def _mm_tn(a, b, *, name):
    T, M = a.shape
    N = b.shape[1]
    tk = min(T, 512)

    def tile(n):
        for cand in (1408, 1280):
            if n > 1408 and n % cand == 0:
                return cand
        return n

    tm, tn = tile(M), tile(N)
    nk = T // tk

    def body(a_ref, b_ref, o_ref):
        @pl.when(pl.program_id(2) == 0)
        def _():
            o_ref[...] = jnp.zeros_like(o_ref)
        o_ref[...] += _dot_tn(a_ref[...].astype(BF16), b_ref[...].astype(BF16))

    return pl.pallas_call(
        body, name=name, grid=(M // tm, N // tn, nk),
        in_specs=[pl.BlockSpec((tk, tm), lambda i, j, k: (k, i)), pl.BlockSpec((tk, tn), lambda i, j, k: (k, j))],
        out_specs=pl.BlockSpec((tm, tn), lambda i, j, k: (i, j)),
        out_shape=jax.ShapeDtypeStruct((M, N), F32),
        compiler_params=_cp(("parallel", "parallel", "arbitrary")))(a, b)


def _rmsnorm_fwd(x, w, *, name, gather=None):
    T, D = x.shape
    tm = min(T, 1024)
    n = T // tm

    def body(x_ref, w_ref, *rest):
        if gather is not None:
            s_ref, o_ref, g_ref, *sems = rest
            start, forward, finish = _gather_phases(s_ref, g_ref, *sems)
            pl.when(pl.program_id(0) == 0)(start)
            pl.when(pl.program_id(0) == n // 2)(forward)
        else:
            o_ref, = rest
        xv = x_ref[...]
        r = lax.rsqrt(jnp.mean(xv * xv, axis=-1, keepdims=True) + EPS)
        o_ref[...] = (xv * r * w_ref[...]).astype(BF16)
        if gather is not None:
            pl.when(pl.program_id(0) == n - 1)(finish)

    in_specs = [pl.BlockSpec((tm, D), lambda i: (i, 0)), pl.BlockSpec((1, D), lambda i: (0, 0))]
    out_spec = pl.BlockSpec((tm, D), lambda i: (i, 0))
    out_shape = jax.ShapeDtypeStruct((T, D), BF16)
    if gather is None:
        return pl.pallas_call(body, name=name, grid=(n,), in_specs=in_specs, out_specs=out_spec, out_shape=out_shape,
                              compiler_params=_cp(("parallel",)))(x, w)
    return pl.pallas_call(
        body, name=name, grid=(n,), in_specs=in_specs + [ANY], out_specs=[out_spec, ANY],
        out_shape=[out_shape, jax.ShapeDtypeStruct((N_DEV,) + gather.shape, gather.dtype)],
        scratch_shapes=list(GATHER_SCRATCH), compiler_params=_cp(("arbitrary",)))(x, w, gather)


def _mm_norm_bwd(a, b, x, w, dres, *, name, exchange=None):
    T, K = a.shape
    D = b.shape[1]
    tm = min(T, 256 if K > 4096 else 512)
    n = T // tm

    def body(a_ref, b_ref, x_ref, w_ref, dr_ref, *rest):
        if exchange is None:
            dx_ref, dw_ref = rest
        else:
            p_ref, dx_ref, dw_ref, got_ref, *sems = rest
            start, finish = _exchange_phases(p_ref, got_ref, *sems)
            pl.when(pl.program_id(0) == 0)(start)

        @pl.when(pl.program_id(0) == 0)
        def _():
            dw_ref[...] = jnp.zeros_like(dw_ref)
        dh = _dot(a_ref[...], b_ref[...])
        xv = x_ref[...]
        r = lax.rsqrt(jnp.mean(xv * xv, axis=-1, keepdims=True) + EPS)
        xh = xv * r
        g = dh * w_ref[...]
        dx_ref[...] = dr_ref[...] + r * (g - xh * jnp.mean(g * xh, axis=-1, keepdims=True))
        dw_ref[...] += jnp.sum(dh * xh, axis=0, keepdims=True)
        if exchange is not None:
            pl.when(pl.program_id(0) == n - 1)(finish)

    row = pl.BlockSpec((tm, D), lambda i: (i, 0))
    vec = pl.BlockSpec((1, D), lambda i: (0, 0))
    in_specs = [pl.BlockSpec((tm, K), lambda i: (i, 0)), pl.BlockSpec((K, D), lambda i: (0, 0)), row, vec, row]
    out_shape = [jax.ShapeDtypeStruct((T, D), F32), jax.ShapeDtypeStruct((1, D), F32)]
    if exchange is None:
        return pl.pallas_call(body, name=name, grid=(n,), in_specs=in_specs, out_specs=[row, vec], out_shape=out_shape,
                              compiler_params=_cp(("arbitrary",)))(a, b, x, w, dres)
    return pl.pallas_call(
        body, name=name, grid=(n,), in_specs=in_specs + [ANY], out_specs=[row, vec, ANY],
        out_shape=out_shape + [jax.ShapeDtypeStruct(exchange.shape, exchange.dtype)],
        scratch_shapes=list(EXCHANGE_SCRATCH), compiler_params=_cp(("arbitrary",)))(a, b, x, w, dres, exchange)


def _down_proj_loss(a, w_down, x1, tgt, w, *, name):
    T, D = x1.shape
    K = a.shape[1]
    tm = min(T, 512)

    def body(a_ref, b_ref, x_ref, t_ref, w_ref, loss_ref, dx_ref, dxb_ref, dw_ref):
        @pl.when(pl.program_id(0) == 0)
        def _():
            dw_ref[...] = jnp.zeros_like(dw_ref)
            loss_ref[...] = jnp.zeros_like(loss_ref)
        xv = x_ref[...] + _dot(a_ref[...], b_ref[...])
        wv = w_ref[...]
        r = lax.rsqrt(jnp.mean(xv * xv, axis=-1, keepdims=True) + EPS)
        xh = xv * r
        e = xh * wv - t_ref[...]
        part = 0.5 * jnp.sum(jnp.mean(e * e, axis=-1, keepdims=True), axis=0, keepdims=True)
        loss_ref[...] += jnp.broadcast_to(part, loss_ref.shape)
        dy = e * (1.0 / D)
        g = dy * wv
        dx = r * (g - xh * jnp.mean(g * xh, axis=-1, keepdims=True))
        dx_ref[...] = dx
        dxb_ref[...] = dx.astype(BF16)
        dw_ref[...] += jnp.sum(dy * xh, axis=0, keepdims=True)

    row = pl.BlockSpec((tm, D), lambda i: (i, 0))
    vec = pl.BlockSpec((1, D), lambda i: (0, 0))
    return pl.pallas_call(
        body, name=name, grid=(T // tm,),
        in_specs=[pl.BlockSpec((tm, K), lambda i: (i, 0)), pl.BlockSpec((K, D), lambda i: (0, 0)), row, row, vec],
        out_specs=[pl.BlockSpec((1, 128), lambda i: (0, 0)), row, row, vec],
        out_shape=[jax.ShapeDtypeStruct((1, 128), F32), jax.ShapeDtypeStruct((T, D), F32),
                   jax.ShapeDtypeStruct((T, D), BF16), jax.ShapeDtypeStruct((1, D), F32)],
        compiler_params=_cp(("arbitrary",)))(a, w_down, x1, tgt, w)


def _ret_tables():
    C = RET_CHUNK
    h = jnp.arange(RET_HEADS, dtype=F32)
    log_gamma = jnp.log1p(-jnp.power(2.0, -5.0 - h))
    idx = jnp.arange(C, dtype=F32)
    diff = idx[:, None] - idx[None, :]
    dm = jnp.where(diff >= 0, jnp.exp(log_gamma[:, None, None] * jnp.maximum(diff, 0.0)), 0.0)
    dm = dm.reshape(4, 2 * C, C)
    lane_head = jnp.repeat(jnp.arange(RET_HEADS).reshape(4, 2), 64, axis=1)
    lg = log_gamma[lane_head]
    xi = jnp.exp(lg[:, None, :] * (idx[None, :, None] + 1.0))
    zeta = jnp.exp(lg[:, None, :] * (C - 1.0 - idx[None, :, None]))
    blk = (jnp.arange(128)[:, None] // 64) == (jnp.arange(128)[None, :] // 64)
    cd = jnp.where(blk[None], jnp.exp(lg * C)[:, :, None], 0.0)
    return dm.astype(F32), xi.astype(F32), zeta.astype(F32), cd.astype(F32)


def _ret_specs(tb, rev, nt):
    def tmap(t):
        return (nt - 1 - t) if rev else t
    qkv = [pl.BlockSpec((tb, 128), lambda p, t, o=o: (tmap(t), o + p)) for o in (0, 4, 8)]
    rope = [pl.BlockSpec((tb, 128), lambda p, t: (tmap(t), 0))] * 2
    tabs = [pl.BlockSpec((None, 256, 128), lambda p, t: (p, 0, 0))] + \
           [pl.BlockSpec((None, 128, 128), lambda p, t: (p, 0, 0))] * 3
    return qkv, rope, tabs


def _ret_fwd(proj, cos, ss, tabs, gnw, *, name):
    T = proj.shape[0]
    tb = min(T, 1024)
    nt = T // tb
    nchunk = tb // RET_CHUNK

    def body(q_ref, k_ref, v_ref, g_ref, cos_ref, ss_ref, dm_ref, xi_ref, zt_ref, cd_ref, gnw_ref,
             y_ref, o_ref, r_sc):
        @pl.when(pl.program_id(1) == 0)
        def _():
            r_sc[...] = jnp.zeros_like(r_sc)
        m0, m1 = _head_masks((128, 128))
        dm, xi, zt, cd = dm_ref[...], xi_ref[...], zt_ref[...], cd_ref[...]
        bm = (cd > 0).astype(F32)
        gnw = gnw_ref[...]
        for c in range(nchunk):
            rs = pl.ds(c * RET_CHUNK, RET_CHUNK)
            cs, sn = cos_ref[rs, :], ss_ref[rs, :]
            q = _rope(q_ref[rs, :], cs, sn, 32, 64)
            k = _rope(k_ref[rs, :], cs, sn, 32, 64) * K_SCALE
            v = v_ref[rs, :]
            kb, vb = k.astype(BF16), v.astype(BF16)
            qs = jnp.concatenate([q * m0, q * m1], axis=0).astype(BF16)
            s = (_dot_nt(qs, kb) * dm).astype(BF16)
            vs = jnp.concatenate([v * m0, v * m1], axis=0).astype(BF16)
            o = _dot(jnp.concatenate([s[:128], s[128:]], axis=1), vs)
            r = r_sc[...]
            o = o + _dot(q.astype(BF16), r.astype(BF16)) * xi
            r_sc[...] = cd * r + bm * _dot_tn((k * zt).astype(BF16), vb)
            mu = (jnp.sum(o * m0, axis=1, keepdims=True) * m0 + jnp.sum(o * m1, axis=1, keepdims=True) * m1) * (1.0 / 64)
            d = o - mu
            dd = d * d
            var = (jnp.sum(dd * m0, axis=1, keepdims=True) * m0 + jnp.sum(dd * m1, axis=1, keepdims=True) * m1) * (1.0 / 64)
            oh = d * lax.rsqrt(var + EPS)
            g = g_ref[rs, :]
            y_ref[rs, :] = (g * _sigmoid(g) * (oh * gnw)).astype(BF16)
            o_ref[rs, :] = o

    qkv, rope, tspec = _ret_specs(tb, False, nt)
    gspec = pl.BlockSpec((tb, 128), lambda p, t: (t, 12 + p))
    out = pl.BlockSpec((tb, 128), lambda p, t: (t, p))
    return pl.pallas_call(
        body, name=name, grid=(4, nt),
        in_specs=qkv + [gspec] + rope + tspec + [pl.BlockSpec((1, 128), lambda p, t: (0, p))],
        out_specs=[out, out],
        out_shape=[jax.ShapeDtypeStruct((T, RET_WIDTH), BF16), jax.ShapeDtypeStruct((T, RET_WIDTH), F32)],
        scratch_shapes=[pltpu.VMEM((128, 128), F32)],
        compiler_params=_cp(("parallel", "arbitrary")))(proj, proj, proj, proj, cos, ss, *tabs, gnw)


def _ret_bwd_dq(proj, do, cos, ss, tabs, *, name):
    T = proj.shape[0]
    tb = min(T, 1024)
    nt = T // tb
    nchunk = tb // RET_CHUNK

    def body(q_ref, k_ref, v_ref, do_ref, cos_ref, ss_ref, dm_ref, xi_ref, zt_ref, cd_ref, dq_ref, r_sc):
        del q_ref
        @pl.when(pl.program_id(1) == 0)
        def _():
            r_sc[...] = jnp.zeros_like(r_sc)
        m0, m1 = _head_masks((128, 128))
        dm, xi, zt, cd = dm_ref[...], xi_ref[...], zt_ref[...], cd_ref[...]
        bm = (cd > 0).astype(F32)
        for c in range(nchunk):
            rs = pl.ds(c * RET_CHUNK, RET_CHUNK)
            cs, sn = cos_ref[rs, :], ss_ref[rs, :]
            k = _rope(k_ref[rs, :], cs, sn, 32, 64) * K_SCALE
            vb = v_ref[rs, :].astype(BF16)
            dob = do_ref[rs, :]
            dof = dob.astype(F32)
            dos = jnp.concatenate([dof * m0, dof * m1], axis=0).astype(BF16)
            a = (_dot_nt(dos, vb) * dm).astype(BF16)
            ks = jnp.concatenate([k * m0, k * m1], axis=0).astype(BF16)
            r = r_sc[...]
            dq = _dot(jnp.concatenate([a[:128], a[128:]], axis=1), ks) + _dot_nt(dob, r.astype(BF16)) * xi
            r_sc[...] = cd * r + bm * _dot_tn((k * zt).astype(BF16), vb)
            dq_ref[rs, :] = _rope_t(dq, cs, sn, 32, 64).astype(BF16)

    qkv, rope, tspec = _ret_specs(tb, False, nt)
    blk = pl.BlockSpec((tb, 128), lambda p, t: (t, p))
    return pl.pallas_call(
        body, name=name, grid=(4, nt), in_specs=qkv + [blk] + rope + tspec, out_specs=blk,
        out_shape=jax.ShapeDtypeStruct((T, RET_WIDTH), BF16),
        scratch_shapes=[pltpu.VMEM((128, 128), F32)],
        compiler_params=_cp(("parallel", "arbitrary")))(proj, proj, proj, do, cos, ss, *tabs)


def _ret_bwd_dkv(proj, do, cos, ss, tabs, *, name, swap=None):
    T = proj.shape[0]
    tb = min(T, 1024)
    nt = T // tb
    nchunk = tb // RET_CHUNK

    def body(q_ref, k_ref, v_ref, do_ref, cos_ref, ss_ref, dm_ref, xi_ref, zt_ref, cd_ref, *rest):
        if swap is None:
            backward(q_ref, k_ref, v_ref, do_ref, cos_ref, ss_ref, dm_ref, xi_ref, zt_ref, cd_ref, *rest)
        else:
            g_ref, dk_ref, dv_ref, got_ref, u_sc, *sems = rest
            start, finish = _swap_phases(g_ref, got_ref, *sems)
            pl.when((pl.program_id(0) == 0) & (pl.program_id(1) == 0))(start)
            backward(q_ref, k_ref, v_ref, do_ref, cos_ref, ss_ref, dm_ref, xi_ref, zt_ref, cd_ref, dk_ref, dv_ref, u_sc)
            pl.when((pl.program_id(0) == 3) & (pl.program_id(1) == nt - 1))(finish)

    def backward(q_ref, k_ref, v_ref, do_ref, cos_ref, ss_ref, dm_ref, xi_ref, zt_ref, cd_ref, dk_ref, dv_ref, u_sc):
        @pl.when(pl.program_id(1) == 0)
        def _():
            u_sc[...] = jnp.zeros_like(u_sc)
        m0, m1 = _head_masks((128, 128))
        dm, xi, zt, cd = dm_ref[...], xi_ref[...], zt_ref[...], cd_ref[...]
        bm = (cd > 0).astype(F32)
        for c in reversed(range(nchunk)):
            rs = pl.ds(c * RET_CHUNK, RET_CHUNK)
            cs, sn = cos_ref[rs, :], ss_ref[rs, :]
            q = _rope(q_ref[rs, :], cs, sn, 32, 64)
            k = _rope(k_ref[rs, :], cs, sn, 32, 64) * K_SCALE
            kb = k.astype(BF16)
            vb = v_ref[rs, :].astype(BF16)
            dob = do_ref[rs, :]
            dof = dob.astype(F32)
            qs = jnp.concatenate([q * m0, q * m1], axis=0).astype(BF16)
            dos = jnp.concatenate([dof * m0, dof * m1], axis=0).astype(BF16)
            s = (_dot_nt(qs, kb) * dm).astype(BF16)
            a = (_dot_nt(dos, vb) * dm).astype(BF16)
            ub = u_sc[...].astype(BF16)
            dk = _dot_tn(a, qs) + _dot_nt(vb, ub) * zt
            dv = _dot_tn(s, dos) + _dot(kb, ub) * zt
            u_sc[...] = cd * u_sc[...] + bm * _dot_tn((q * xi).astype(BF16), dob)
            dk_ref[rs, :] = (_rope_t(dk, cs, sn, 32, 64) * K_SCALE).astype(BF16)
            dv_ref[rs, :] = dv.astype(BF16)

    qkv, rope, tspec = _ret_specs(tb, True, nt)
    blk = pl.BlockSpec((tb, 128), lambda p, t: (nt - 1 - t, p))
    out_shape = [jax.ShapeDtypeStruct((T, RET_WIDTH), BF16)] * 2
    if swap is None:
        return pl.pallas_call(
            body, name=name, grid=(4, nt), in_specs=qkv + [blk] + rope + tspec, out_specs=[blk, blk],
            out_shape=out_shape, scratch_shapes=[pltpu.VMEM((128, 128), F32)],
            compiler_params=_cp(("parallel", "arbitrary")))(proj, proj, proj, do, cos, ss, *tabs)
    return pl.pallas_call(
        body, name=name, grid=(4, nt), in_specs=qkv + [blk] + rope + tspec + [ANY], out_specs=[blk, blk, ANY],
        out_shape=out_shape + [jax.ShapeDtypeStruct((4,) + swap.shape[2:], swap.dtype)],
        scratch_shapes=[pltpu.VMEM((128, 128), F32)] + list(SWAP_SCRATCH),
        compiler_params=_cp(("arbitrary", "arbitrary")))(proj, proj, proj, do, cos, ss, *tabs, swap)


def _mix_bwd(dmixed, o_ret, proj, y_mla, gnw, *, name):
    T = dmixed.shape[0]
    tm = min(T, 512)

    def body(dm_ref, o_ref, g_ref, ym_ref, gnw_ref, do_ref, dg_ref, dom_ref, dl_ref, dw_ref):
        @pl.when(pl.program_id(0) == 0)
        def _():
            dw_ref[...] = jnp.zeros_like(dw_ref)
        m0, m1 = _head_masks((tm, 128))
        lane = lax.broadcasted_iota(jnp.int32, (tm, 128), 1)
        delta = jnp.zeros((tm, 128), F32)

        def gsum(z):
            return jnp.sum(z * m0, axis=1, keepdims=True) * m0 + jnp.sum(z * m1, axis=1, keepdims=True) * m1

        for p in range(4):
            cs = slice(128 * p, 128 * p + 128)
            dy = dm_ref[:, cs]
            o = o_ref[:, cs]
            g = g_ref[:, cs]
            w = gnw_ref[:, cs]
            d = o - gsum(o) * (1.0 / 64)
            rstd = lax.rsqrt(gsum(d * d) * (1.0 / 64) + EPS)
            oh = d * rstd
            sg = _sigmoid(g)
            dn = dy * (g * sg)
            dg_ref[:, cs] = (dy * (oh * w) * (sg * (1.0 + g * (1.0 - sg)))).astype(BF16)
            dw_ref[:, cs] += jnp.sum(dn * oh, axis=0, keepdims=True)
            doh = dn * w
            do = rstd * (doh - gsum(doh) * (1.0 / 64) - oh * (gsum(doh * oh) * (1.0 / 64)))
            do_ref[:, cs] = do.astype(BF16)
            dom = dm_ref[:, 512 + 128 * p:512 + 128 * p + 128]
            dom_ref[:, cs] = dom.astype(BF16)
            pr = dom * ym_ref[:, cs].astype(F32)
            delta = jnp.where(lane == 2 * p, jnp.sum(pr * m0, axis=1, keepdims=True), delta)
            delta = jnp.where(lane == 2 * p + 1, jnp.sum(pr * m1, axis=1, keepdims=True), delta)
        dl_ref[...] = delta.T[0:MLA_HEADS]

    half = pl.BlockSpec((tm, 512), lambda i: (i, 0))
    return pl.pallas_call(
        body, name=name, grid=(T // tm,),
        in_specs=[pl.BlockSpec((tm, 1024), lambda i: (i, 0)), half, pl.BlockSpec((tm, 512), lambda i: (i, 3)),
                  half, pl.BlockSpec((1, 512), lambda i: (0, 0))],
        out_specs=[half, half, half, pl.BlockSpec((MLA_HEADS, tm), lambda i: (0, i)),
                   pl.BlockSpec((1, 512), lambda i: (0, 0))],
        out_shape=[jax.ShapeDtypeStruct((T, 512), BF16)] * 3 + [jax.ShapeDtypeStruct((MLA_HEADS, T), F32),
                                                                jax.ShapeDtypeStruct((1, 512), F32)],
        compiler_params=_cp(("arbitrary",)))(dmixed, o_ret, proj, y_mla, gnw)


def _mla_prep_fwd(proj, qnw, kvnw, wuq, wk, wv, cos, ss, *, name):
    T = proj.shape[0]
    tm = min(T, 512)

    def body(lat_ref, qnw_ref, kvnw_ref, wuq_ref, wk_ref, wv_ref, cos_ref, ss_ref,
             q_ref, k_ref, v_ref, cqn_ref, ckvn_ref):
        cq = lat_ref[:, 0:256]
        ckv = lat_ref[:, 256:384]
        g3 = lat_ref[:, 384:512]
        cqn = (cq * lax.rsqrt(jnp.mean(cq * cq, axis=-1, keepdims=True) + EPS) * qnw_ref[...]).astype(BF16)
        ckvn = (ckv * lax.rsqrt(jnp.mean(ckv * ckv, axis=-1, keepdims=True) + EPS) * kvnw_ref[...]).astype(BF16)
        cqn_ref[...] = cqn
        ckvn_ref[...] = ckvn
        cs, sn = cos_ref[...], ss_ref[...]
        q = _dot_nt(cqn, wuq_ref[...])
        k = _dot_nt(ckvn, wk_ref[...])
        kpe = _rope(g3, cs, sn, 16, 32)
        for h in range(MLA_HEADS):
            hs = slice(128 * h, 128 * h + 128)
            q_ref[:, hs] = (_rope(q[:, hs], cs, sn, 16, 32) * SCALE).astype(BF16)
            k_ref[:, hs] = (k[:, hs] + kpe).astype(BF16)
        v = _dot_nt(ckvn, wv_ref[...])
        lane = lax.broadcasted_iota(jnp.int32, (tm, 128), 1)
        for p in range(4):
            vp = v[:, 128 * p:128 * p + 128]
            v_ref[:, 256 * p:256 * p + 128] = jnp.where(lane < 64, vp, 1.0).astype(BF16)
            v_ref[:, 256 * p + 128:256 * p + 256] = jnp.where(lane < 64, 1.0, vp).astype(BF16)

    def full(shape):
        return pl.BlockSpec(shape, lambda i: (0, 0))

    def row(w):
        return pl.BlockSpec((tm, w), lambda i: (i, 0))

    return pl.pallas_call(
        body, name=name, grid=(T // tm,),
        in_specs=[pl.BlockSpec((tm, 512), lambda i: (i, 4)), full((1, 256)), full((1, 128)), full((1024, 256)),
                  full((1024, 128)), full((512, 128)), row(128), row(128)],
        out_specs=[row(1024), row(1024), row(1024), row(256), row(128)],
        out_shape=[jax.ShapeDtypeStruct((T, 1024), BF16), jax.ShapeDtypeStruct((T, 1024), BF16),
                   jax.ShapeDtypeStruct((T, 1024), BF16), jax.ShapeDtypeStruct((T, 256), BF16),
                   jax.ShapeDtypeStruct((T, 128), BF16)],
        compiler_params=_cp(("parallel",)))(proj, qnw, kvnw, wuq, wk, wv, cos, ss)


def _mla_prep_bwd(dq, dk, dv, proj, qnw, kvnw, wuq_t, wk_t, wv_t, cos, ss, ret_grads, *, name):
    T = proj.shape[0]
    tm = min(T, 512)

    def body(dq_ref, dk_ref, dv_ref, lat_ref, qnw_ref, kvnw_ref, wuq_ref, wk_ref, wv_ref, cos_ref, ss_ref,
             rq_ref, rk_ref, rv_ref, rg_ref, dproj_ref, dqp_ref, dqnw_ref, dkvnw_ref):
        for j, r in enumerate((rq_ref, rk_ref, rv_ref, rg_ref)):
            dproj_ref[:, 512 * j:512 * j + 512] = r[...]
        dlat_ref = dproj_ref.at[:, 2048:2560]

        @pl.when(pl.program_id(0) == 0)
        def _():
            dqnw_ref[...] = jnp.zeros_like(dqnw_ref)
            dkvnw_ref[...] = jnp.zeros_like(dkvnw_ref)
        cs, sn = cos_ref[...], ss_ref[...]
        dkpe = jnp.zeros((tm, 128), F32)
        for h in range(MLA_HEADS):
            hs = slice(128 * h, 128 * h + 128)
            dqp_ref[:, hs] = _rope_t(dq_ref[:, hs] * SCALE, cs, sn, 16, 32).astype(BF16)
            dkpe = dkpe + dk_ref[:, hs]
        lane = lax.broadcasted_iota(jnp.int32, (tm, 128), 1)
        rope_lane = (lane >= MLA_NOPE) & (lane < MLA_NOPE + MLA_ROPE)
        dg3 = jnp.where(rope_lane, _rope_t(jnp.where(rope_lane, dkpe, 0.0), cs, sn, 16, 32), 0.0)

        def norm_bwd(x, w, dn):
            r = lax.rsqrt(jnp.mean(x * x, axis=-1, keepdims=True) + EPS)
            xh = x * r
            g = dn * w
            return r * (g - xh * jnp.mean(g * xh, axis=-1, keepdims=True)), jnp.sum(dn * xh, axis=0, keepdims=True)

        dcqn = _dot(dqp_ref[...], wuq_ref[...])
        dcq, dqnw = norm_bwd(lat_ref[:, 0:256], qnw_ref[...], dcqn)
        dckvn = _dot(dk_ref[...].astype(BF16), wk_ref[...]) + _dot(dv_ref[...], wv_ref[...])
        dckv, dkvnw = norm_bwd(lat_ref[:, 256:384], kvnw_ref[...], dckvn)
        dqnw_ref[...] += dqnw
        dkvnw_ref[...] += dkvnw
        dlat_ref[:, 0:256] = dcq.astype(BF16)
        dlat_ref[:, 256:384] = dckv.astype(BF16)
        dlat_ref[:, 384:512] = dg3.astype(BF16)

    def full(shape):
        return pl.BlockSpec(shape, lambda i: (0, 0))

    def row(w):
        return pl.BlockSpec((tm, w), lambda i: (i, 0))

    return pl.pallas_call(
        body, name=name, grid=(T // tm,),
        in_specs=[row(1024), row(1024), row(512), pl.BlockSpec((tm, 512), lambda i: (i, 4)), full((1, 256)),
                  full((1, 128)), full((1024, 256)), full((1024, 128)), full((512, 128)), row(128), row(128)]
                 + [row(512)] * 4,
        out_specs=[row(IN_PAD), row(1024), full((1, 256)), full((1, 128))],
        out_shape=[jax.ShapeDtypeStruct((T, IN_PAD), BF16), jax.ShapeDtypeStruct((T, 1024), BF16),
                   jax.ShapeDtypeStruct((1, 256), F32), jax.ShapeDtypeStruct((1, 128), F32)],
        compiler_params=_cp(("arbitrary",)))(dq, dk, dv, proj, qnw, kvnw, wuq_t, wk_t, wv_t, cos, ss, *ret_grads)


def _flash_fwd(q, k, v1, *, name, gather=None):
    T = q.shape[0]
    tq = min(T, 512)
    tk = tq
    nq = T // tq

    def body(q_ref, k_ref, v_ref, *rest):
        if gather is None:
            y_ref, lse_ref = rest
        else:
            x_ref, y_ref, lse_ref, g_ref, *sems = rest
            start, forward, finish = _gather_phases(x_ref, g_ref, *sems)
            pl.when((pl.program_id(0) == 0) & (pl.program_id(1) == 0))(start)
            pl.when((pl.program_id(0) == 1) & (pl.program_id(1) == 0))(forward)
        attend(q_ref, k_ref, v_ref, y_ref, lse_ref)
        if gather is not None:
            pl.when((pl.program_id(0) == 3) & (pl.program_id(1) == nq - 1))(finish)

    def attend(q_ref, k_ref, v_ref, y_ref, lse_ref):
        qi = pl.program_id(1)
        row = lax.broadcasted_iota(jnp.int32, (tq, tk), 0)
        col = lax.broadcasted_iota(jnp.int32, (tq, tk), 1)

        def step(kb, carry, masked):
            ks = pl.ds(pl.multiple_of(kb * tk, tk), tk)
            new = []
            for h in range(2):
                hs = slice(128 * h, 128 * h + 128)
                m, acc = carry[h]
                s = _dot_nt(q_ref[:, hs], k_ref[ks, hs])
                if masked:
                    s = jnp.where(col <= row, s, NEG)
                mn = jnp.maximum(m, jnp.max(s, axis=1, keepdims=True))
                p = jnp.exp((s - mn).astype(BF16))
                acc = jnp.exp(m - mn) * acc + _dot(p, v_ref[ks, hs])
                new.append((mn, acc))
            return tuple(new)

        def unrolled(j, c):
            for u in range(FLASH_UNROLL):
                c = step(FLASH_UNROLL * j + u, c, False)
            return c

        init = (jnp.full((tq, 1), NEG, F32), jnp.zeros((tq, 128), F32))
        carry = lax.fori_loop(0, qi // FLASH_UNROLL, unrolled, (init, init))
        carry = lax.fori_loop(FLASH_UNROLL * (qi // FLASH_UNROLL), qi, lambda kb, c: step(kb, c, False), carry)
        (ma, acca), (mb, accb) = step(qi, carry, True)
        lane = lax.broadcasted_iota(jnp.int32, (tq, 128), 1)
        la, lb = pltpu.roll(acca, 64, 1), pltpu.roll(accb, 64, 1)
        y_ref[...] = jnp.where(lane < 64, acca / la, accb / lb).astype(BF16)
        lse_ref[0, 0] = jnp.broadcast_to(ma + jnp.log(acca[:, 64:65]), (tq, 128)).T[0:1]
        lse_ref[1, 0] = jnp.broadcast_to(mb + jnp.log(accb[:, 0:1]), (tq, 128)).T[0:1]

    in_specs = [pl.BlockSpec((tq, 256), lambda p, i: (i, p)), pl.BlockSpec((T, 256), lambda p, i: (0, p)),
                pl.BlockSpec((T, 256), lambda p, i: (0, p))]
    out_specs = [pl.BlockSpec((tq, 128), lambda p, i: (i, p)), pl.BlockSpec((2, 1, 1, tq), lambda p, i: (p, i, 0, 0))]
    out_shape = [jax.ShapeDtypeStruct((T, MLA_WIDTH), BF16), jax.ShapeDtypeStruct((MLA_HEADS, nq, 1, tq), F32)]
    if gather is None:
        return pl.pallas_call(body, name=name, grid=(4, nq), in_specs=in_specs, out_specs=out_specs,
                              out_shape=out_shape, compiler_params=_cp(("parallel", "arbitrary")))(q, k, v1)
    return pl.pallas_call(
        body, name=name, grid=(4, nq), in_specs=in_specs + [ANY], out_specs=out_specs + [ANY],
        out_shape=out_shape + [jax.ShapeDtypeStruct((N_DEV,) + gather.shape, gather.dtype)],
        scratch_shapes=list(GATHER_SCRATCH),
        compiler_params=_cp(("arbitrary", "arbitrary")))(q, k, v1, gather)


def _flash_bwd(q, k, v, do, lse, delta, *, name, exchange=None):
    T = q.shape[0]
    tq = min(T, 512)
    tk = tq
    nq = T // tq

    def body(q_ref, k_ref, v_ref, do_ref, lse_ref, dl_ref, *rest):
        if exchange is None:
            backward(q_ref, k_ref, v_ref, do_ref, lse_ref, dl_ref, *rest)
        else:
            p_ref, dqt_ref, dk_ref, dv_ref, got_ref, *sems = rest
            start, finish = _exchange_phases(p_ref, got_ref, *sems)
            pl.when((pl.program_id(0) == 0) & (pl.program_id(1) == 0))(start)
            backward(q_ref, k_ref, v_ref, do_ref, lse_ref, dl_ref, dqt_ref, dk_ref, dv_ref)
            pl.when((pl.program_id(0) == 3) & (pl.program_id(1) == nq - 1))(finish)

    def backward(q_ref, k_ref, v_ref, do_ref, lse_ref, dl_ref, dqt_ref, dk_ref, dv_ref):
        kb = pl.program_id(1)

        @pl.when(kb == 0)
        def _():
            dqt_ref[...] = jnp.zeros_like(dqt_ref)
        krow = lax.broadcasted_iota(jnp.int32, (tk, tq), 0)
        qcol = lax.broadcasted_iota(jnp.int32, (tk, tq), 1)
        masks = _head_masks((tk, 128))
        vms = [(v_ref[:, 128 * h:128 * h + 128].astype(F32) * masks[h]).astype(BF16) for h in range(2)]

        def step(qi, carry, masked):
            qs = pl.ds(pl.multiple_of(qi * tq, tq), tq)
            dob = do_ref[qs, :]
            dof = dob.astype(F32)
            dks, dv_acc = list(carry[:2]), carry[2]
            for h in range(2):
                hs = slice(128 * h, 128 * h + 128)
                kh = k_ref[:, hs]
                qh = q_ref[qs, hs]
                st = _dot_nt(kh, qh)
                pt = jnp.exp((st - lse_ref[h, qi]).astype(BF16))
                if masked:
                    pt = jnp.where(krow <= qcol, pt, jnp.zeros_like(pt))
                dv_acc = dv_acc + _dot(pt, (dof * masks[h]).astype(BF16))
                dpt = _dot_nt(vms[h], dob)
                dst = pt * (dpt - dl_ref[h, qi]).astype(BF16)
                dks[h] = dks[h] + _dot(dst, qh)
                dqt_ref[qi, hs, :] += _dot_tn(kh, dst)
            return dks[0], dks[1], dv_acc

        zero = jnp.zeros((tk, 128), F32)
        carry = step(kb, (zero, zero, zero), True)

        def two_steps(j, c):
            qi = kb + 1 + 2 * j
            return step(qi + 1, step(qi, c, False), False)

        pairs = (nq - 1 - kb) // 2
        carry = lax.fori_loop(0, pairs, two_steps, carry)
        dk0, dk1, dv_acc = lax.fori_loop(kb + 1 + 2 * pairs, nq, lambda qi, c: step(qi, c, False), carry)
        dk_ref[:, 0:128] = dk0
        dk_ref[:, 128:256] = dk1
        dv_ref[...] = dv_acc.astype(BF16)

    stat = pl.BlockSpec((2, nq, 1, tq), lambda p, j: (p, 0, 0, 0))
    in_specs = [pl.BlockSpec((T, 256), lambda p, j: (0, p)), pl.BlockSpec((tk, 256), lambda p, j: (j, p)),
                pl.BlockSpec((tk, 256), lambda p, j: (j, p)), pl.BlockSpec((T, 128), lambda p, j: (0, p)), stat, stat]
    out_specs = [pl.BlockSpec((None, nq, 256, tq), lambda p, j: (p, 0, 0, 0)),
                 pl.BlockSpec((tk, 256), lambda p, j: (j, p)), pl.BlockSpec((tk, 128), lambda p, j: (j, p))]
    out_shape = [jax.ShapeDtypeStruct((4, nq, 256, tq), F32), jax.ShapeDtypeStruct((T, 1024), F32),
                 jax.ShapeDtypeStruct((T, MLA_WIDTH), BF16)]
    if exchange is None:
        return pl.pallas_call(body, name=name, grid=(4, nq), in_specs=in_specs, out_specs=out_specs,
                              out_shape=out_shape,
                              compiler_params=_cp(("parallel", "arbitrary")))(q, k, v, do, lse, delta)
    return pl.pallas_call(
        body, name=name, grid=(4, nq), in_specs=in_specs + [ANY], out_specs=out_specs + [ANY],
        out_shape=out_shape + [jax.ShapeDtypeStruct(exchange.shape, exchange.dtype)],
        scratch_shapes=list(EXCHANGE_SCRATCH),
        compiler_params=_cp(("arbitrary", "arbitrary")))(q, k, v, do, lse, delta, exchange)


def _shift_down(x, n, prev8):
    r = pltpu.roll(x, n, 0)
    row = lax.broadcasted_iota(jnp.int32, prev8.shape, 0)
    first = jnp.where(row < n, pltpu.roll(prev8, n, 0), r[:8])
    if x.shape[0] == 8:
        return first
    return jnp.concatenate([first, r[8:]], axis=0)


def _shift_up(x, n, next8):
    tm = x.shape[0]
    r = pltpu.roll(x, tm - n, 0)
    row = lax.broadcasted_iota(jnp.int32, next8.shape, 0)
    last = jnp.where(row >= 8 - n, pltpu.roll(next8, 8 - n, 0), r[tm - 8:])
    return jnp.concatenate([r[:tm - 8], last], axis=0)


def _conv_pre(u, prev8, cw_ref, cb_ref):
    p1 = _shift_down(u, 1, prev8)
    p2 = _shift_down(u, 2, prev8)
    up = cb_ref[...] + cw_ref[0:1, :] * p2 + cw_ref[1:2, :] * p1 + cw_ref[2:3, :] * u
    return up, p1, p2


def _up_proj_conv(h2, w_up_t, cw, cb, *, name):
    T, K = h2.shape
    tm = min(T, 256)

    def body(h_ref, w_ref, cw_ref, cb_ref, u_ref, a_ref, carry_sc):
        @pl.when(pl.program_id(0) == 0)
        def _():
            carry_sc[...] = jnp.zeros_like(carry_sc)
        h = h_ref[...]
        for blk in range(2):
            ups = []
            for half in range(2):
                cs = slice((2 * blk + half) * FF_HALF, (2 * blk + half + 1) * FF_HALF)
                u = _dot_nt(h, w_ref[cs, :])
                u_ref[:, cs] = u
                prev = carry_sc[:, cs]
                ups.append(cb_ref[:, cs] + cw_ref[0:1, cs] * _shift_down(u, 2, prev)
                           + cw_ref[1:2, cs] * _shift_down(u, 1, prev) + cw_ref[2:3, cs] * u)
                carry_sc[:, cs] = u[tm - 8:]
            gate, val = ups
            a_ref[:, blk * FF_HALF:(blk + 1) * FF_HALF] = (gate * _sigmoid(gate) * val).astype(BF16)

    def full(shape):
        return pl.BlockSpec(shape, lambda i: (0, 0))

    return pl.pallas_call(
        body, name=name, grid=(T // tm,),
        in_specs=[pl.BlockSpec((tm, K), lambda i: (i, 0)), full(w_up_t.shape), full(cw.shape), full(cb.shape)],
        out_specs=[pl.BlockSpec((tm, 2 * D_FF), lambda i: (i, 0)), pl.BlockSpec((tm, D_FF), lambda i: (i, 0))],
        out_shape=[jax.ShapeDtypeStruct((T, 2 * D_FF), F32), jax.ShapeDtypeStruct((T, D_FF), BF16)],
        scratch_shapes=[pltpu.VMEM((8, 2 * D_FF), F32)],
        compiler_params=_cp(("arbitrary",)))(h2, w_up_t, cw, cb)


def _conv_bwd(u, da, cw, cb, *, name):
    T = u.shape[0]
    tm = min(T, 512)
    W = 2 * FF_HALF
    nt = T // tm

    def body(u_ref, prev_ref, next_ref, da_ref, dan_ref, cw_ref, cb_ref, du_ref, dw0_ref, dw1_ref, dw2_ref, db_ref):
        i = pl.program_id(1)

        @pl.when(i == 0)
        def _():
            for r in (dw0_ref, dw1_ref, dw2_ref, db_ref):
                r[...] = jnp.zeros_like(r)

        def dpre(u, prev8, da):
            up, p1, p2 = _conv_pre(u, prev8, cw_ref, cb_ref)
            gate, val = up[:, :FF_HALF], up[:, FF_HALF:]
            sg = _sigmoid(gate)
            dgate = da * val * (sg * (1.0 + gate * (1.0 - sg)))
            dval = da * (gate * sg)
            return jnp.concatenate([dgate, dval], axis=1), p1, p2

        u = u_ref[...]
        prev = jnp.where(i > 0, prev_ref[...], 0.0)
        dup, p1, p2 = dpre(u, prev, da_ref[...])
        dupn, _, _ = dpre(next_ref[...], u[tm - 8:], dan_ref[...])
        dupn = jnp.where(i < nt - 1, dupn, 0.0)
        du = cw_ref[2:3, :] * dup + cw_ref[1:2, :] * _shift_up(dup, 1, dupn) + cw_ref[0:1, :] * _shift_up(dup, 2, dupn)
        du_ref[...] = du.astype(BF16)
        dw0_ref[...] += jnp.sum(dup * p2, axis=0, keepdims=True)
        dw1_ref[...] += jnp.sum(dup * p1, axis=0, keepdims=True)
        dw2_ref[...] += jnp.sum(dup * u, axis=0, keepdims=True)
        db_ref[...] += jnp.sum(dup, axis=0, keepdims=True)

    nxt = lambda j, i: (jnp.minimum((i + 1) * (tm // 8), T // 8 - 1), j)
    vec = pl.BlockSpec((1, W), lambda j, i: (0, j))
    return pl.pallas_call(
        body, name=name, grid=(2, nt),
        in_specs=[pl.BlockSpec((tm, W), lambda j, i: (i, j)),
                  pl.BlockSpec((8, W), lambda j, i: (jnp.maximum(i * (tm // 8) - 1, 0), j)),
                  pl.BlockSpec((8, W), nxt),
                  pl.BlockSpec((tm, FF_HALF), lambda j, i: (i, j)), pl.BlockSpec((8, FF_HALF), nxt),
                  pl.BlockSpec((3, W), lambda j, i: (0, j)), vec],
        out_specs=[pl.BlockSpec((tm, W), lambda j, i: (i, j)), vec, vec, vec, vec],
        out_shape=[jax.ShapeDtypeStruct((T, 2 * D_FF), BF16)] + [jax.ShapeDtypeStruct((1, 2 * D_FF), F32)] * 4,
        compiler_params=_cp(("parallel", "arbitrary")))(u, u, u, da, da, cw, cb)


def _adamw(w, m, v, g_slots, *, name):
    R, C = w.shape
    ns = g_slots.shape[0]
    tr = _row_tile(R)

    def body(w_ref, m_ref, v_ref, g_ref, go_ref, d_ref, mo_ref, vo_ref):
        g = g_ref[0].astype(F32)
        for s in range(1, ns):
            g = g + g_ref[s].astype(F32)
        mn = ADAM_B1 * m_ref[...] + (1.0 - ADAM_B1) * g
        vn = ADAM_B2 * v_ref[...] + (1.0 - ADAM_B2) * (g * g)
        m_hat = mn / (1.0 - ADAM_B1 ** ADAM_STEP)
        v_hat = vn / (1.0 - ADAM_B2 ** ADAM_STEP)
        go_ref[...] = g
        d_ref[...] = -ADAM_LR * (m_hat / (jnp.sqrt(v_hat) + ADAM_EPS) + ADAM_WD * w_ref[...])
        mo_ref[...] = mn
        vo_ref[...] = vn

    blk = pl.BlockSpec((tr, C), lambda i: (i, 0))
    return pl.pallas_call(
        body, name=name, grid=(R // tr,),
        in_specs=[blk, blk, blk, pl.BlockSpec((ns, tr, C), lambda i: (0, i, 0))],
        out_specs=[blk] * 4, out_shape=[jax.ShapeDtypeStruct((R, C), F32)] * 4,
        compiler_params=_cp(("parallel",)))(w, m, v, g_slots)


def _place():
    return lax.axis_index("x"), lax.axis_index("y"), lax.axis_index("c")


GATHER_SCRATCH = (pltpu.SemaphoreType.DMA((7,)), pltpu.SemaphoreType.DMA((7,)), pltpu.SemaphoreType.DMA)
EXCHANGE_SCRATCH = (pltpu.SemaphoreType.DMA((3,)), pltpu.SemaphoreType.DMA((3,)), pltpu.SemaphoreType.DMA)


def _gather_phases(x_ref, out_ref, send_sems, recv_sems, local_sem):
    x_, y_, c_ = _place()
    me, sibling = (x_, y_, c_), (x_, y_, 1 - c_)
    chips = [(1 - x_, y_), (x_, 1 - y_), (1 - x_, 1 - y_)]

    def slot(px, py, pc):
        return out_ref.at[4 * px + 2 * py + pc]

    def copy(k, block, to, src=None):
        return pltpu.make_async_remote_copy(
            src_ref=slot(*block) if src is None else src, dst_ref=slot(*block),
            send_sem=send_sems.at[k], recv_sem=recv_sems.at[k], device_id=to, device_id_type=MESH)

    def mine():
        return pltpu.make_async_copy(x_ref, slot(*me), local_sem)

    def first():
        return [copy(0, me, sibling, src=x_ref)] + [copy(1 + j, me, (*chip, c_), src=x_ref)
                                                     for j, chip in enumerate(chips)]

    def passed():
        return [copy(4 + j, (*chip, c_), sibling) for j, chip in enumerate(chips)]

    def start():
        mine().start()
        for cp in first():
            cp.start()

    def forward():
        fwd = passed()
        for j, chip in enumerate(chips):
            copy(1 + j, (*chip, c_), me).wait_recv()
            fwd[j].start()

    def finish():
        copy(0, sibling, me).wait_recv()
        for j, chip in enumerate(chips):
            copy(4 + j, (*chip, 1 - c_), me).wait_recv()
        for cp in first() + passed():
            cp.wait_send()
        mine().wait()

    return start, forward, finish


def _exchange_phases(p_ref, out_ref, send_sems, recv_sems, local_sem):
    x_, y_, c_ = _place()
    me_k = 2 * x_ + y_
    chips = [(1 - x_, y_), (x_, 1 - y_), (1 - x_, 1 - y_)]

    def local():
        return pltpu.make_async_copy(p_ref.at[me_k], out_ref.at[me_k], local_sem)

    def copy(j, src_k, dst_k, chip):
        return pltpu.make_async_remote_copy(
            src_ref=p_ref.at[src_k], dst_ref=out_ref.at[dst_k], send_sem=send_sems.at[j],
            recv_sem=recv_sems.at[j], device_id=(*chip, c_), device_id_type=MESH)

    def sends():
        return [copy(j, 2 * px + py, me_k, (px, py)) for j, (px, py) in enumerate(chips)]

    def start():
        local().start()
        for cp in sends():
            cp.start()

    def finish():
        for j, (px, py) in enumerate(chips):
            copy(j, me_k, 2 * px + py, (px, py)).wait_recv()
        for cp in sends():
            cp.wait_send()
        local().wait()

    return start, finish


def _all_gather(x, *, name, in_vmem):
    def body(x_ref, out_ref, send_sems, recv_sems, local_sem):
        for phase in _gather_phases(x_ref, out_ref, send_sems, recv_sems, local_sem):
            phase()

    spec = pl.BlockSpec(memory_space=pltpu.VMEM) if in_vmem else ANY
    return pl.pallas_call(
        body, name=name, out_shape=jax.ShapeDtypeStruct((N_DEV,) + x.shape, x.dtype),
        in_specs=[spec], out_specs=spec, scratch_shapes=list(GATHER_SCRATCH),
        compiler_params=pltpu.CompilerParams(vmem_limit_bytes=VMEM_LIMIT))(x)


def _small_rows():
    table, row = [], 0
    for n, size in SMALL_VECTORS:
        table.append((n, size, row))
        row += -(-size // PACK_COLS)
    return table


def _ff_chunk_source(c):
    block, off = divmod(c * 128, FF_HALF)
    return (0, 2, 1, 3)[block] * FF_HALF + off


def _pack_small(parts, *, name):
    table = _small_rows()

    def body(*refs):
        out = refs[-1]
        out[...] = jnp.zeros_like(out)
        for ref, (n, size, row) in zip(refs, table):
            if size != 2 * D_FF:
                out[row:row + 1, 0:size] = ref[...]
                continue
            for c in range(size // 128):
                src = _ff_chunk_source(c)
                r, lane = divmod(c * 128, PACK_COLS)
                out[row + r:row + r + 1, lane:lane + 128] = ref[:, src:src + 128]

    return pl.pallas_call(body, name=name, out_shape=jax.ShapeDtypeStruct((SMALL_ROWS, PACK_COLS), F32))(
        *[parts[n] for n, _, _ in table])


def _sum_small(g, *, name):
    table = _small_rows()
    shapes = [(n, size) for n, size, _ in table if not n.startswith("conv_w")]
    shapes.insert(7, ("conv_w", 2 * D_FF))

    def body(g_ref, *outs):
        def total(row, width):
            acc = g_ref[0, row:row + 1, 0:width]
            for d in range(1, N_DEV):
                acc = acc + g_ref[d, row:row + 1, 0:width]
            return acc

        out_of = {n: o for (n, _), o in zip(shapes, outs)}
        for n, size, row in table:
            o, j = (out_of["conv_w"], int(n[-1])) if n.startswith("conv_w") else (out_of[n], 0)
            for i in range(-(-size // PACK_COLS)):
                width = min(PACK_COLS, size - PACK_COLS * i)
                o[j:j + 1, PACK_COLS * i:PACK_COLS * i + width] = total(row + i, width)

    out_shape = [jax.ShapeDtypeStruct((3 if n == "conv_w" else 1, size), F32) for n, size in shapes]
    res = pl.pallas_call(body, name=name, out_shape=out_shape)(g)
    return {n: r for (n, _), r in zip(shapes, res)}


def _adamw_small(ws, ms, vs, gs, *, name):
    k = len(ws)

    def body(*refs):
        w_refs, m_refs, v_refs, g_refs = (refs[i * k:(i + 1) * k] for i in range(4))
        outs = refs[4 * k:]
        for i in range(k):
            g = g_refs[i][...]
            mn = ADAM_B1 * m_refs[i][...] + (1.0 - ADAM_B1) * g
            vn = ADAM_B2 * v_refs[i][...] + (1.0 - ADAM_B2) * (g * g)
            m_hat = mn / (1.0 - ADAM_B1 ** ADAM_STEP)
            v_hat = vn / (1.0 - ADAM_B2 ** ADAM_STEP)
            outs[i][...] = g
            outs[k + i][...] = -ADAM_LR * (m_hat / (jnp.sqrt(v_hat) + ADAM_EPS) + ADAM_WD * w_refs[i][...])
            outs[2 * k + i][...] = mn
            outs[3 * k + i][...] = vn

    out_shape = [jax.ShapeDtypeStruct(w.shape, F32) for _ in range(4) for w in ws]
    res = pl.pallas_call(body, name=name, out_shape=out_shape)(*ws, *ms, *vs, *gs)
    return [res[i * k:(i + 1) * k] for i in range(4)]


SWAP_SCRATCH = (pltpu.SemaphoreType.DMA((4,)), pltpu.SemaphoreType.DMA((4,)))


def _swap_phases(g_ref, out_ref, send_sems, recv_sems):
    x_, y_, c_ = _place()

    def copies():
        return [pltpu.make_async_remote_copy(src_ref=g_ref.at[k, 1 - c_], dst_ref=out_ref.at[k],
                                             send_sem=send_sems.at[k], recv_sem=recv_sems.at[k],
                                             device_id=(x_, y_, 1 - c_), device_id_type=MESH) for k in range(4)]

    def start():
        for cp in copies():
            cp.start()

    def finish():
        for cp in copies():
            cp.wait()

    return start, finish


def _swap_sibling(g, *, name):
    def body(g_ref, out_ref, send_sems, recv_sems):
        for phase in _swap_phases(g_ref, out_ref, send_sems, recv_sems):
            phase()

    return pl.pallas_call(
        body, name=name, out_shape=jax.ShapeDtypeStruct((4,) + g.shape[2:], g.dtype), in_specs=[ANY], out_specs=ANY,
        scratch_shapes=list(SWAP_SCRATCH))(g)


def _row_tile(R):
    for cand in (256, 400, 200):
        if R % cand == 0:
            return cand
    return R


def _add_own(g, b, *, name, out_dtype):
    n, _, R, C = g.shape
    tr = _row_tile(R)

    def body(c_ref, g_ref, b_ref, o_ref):
        del c_ref
        o_ref[...] = (g_ref[...] + b_ref[...]).astype(out_dtype)

    blk = pl.BlockSpec((None, tr, C), lambda s, i, c: (s, i, 0))
    grid_spec = pltpu.PrefetchScalarGridSpec(
        num_scalar_prefetch=1, grid=(n, R // tr),
        in_specs=[pl.BlockSpec((None, None, tr, C), lambda s, i, c: (s, c[0], i, 0)), blk], out_specs=blk)
    core = jnp.reshape(lax.axis_index("c"), (1,)).astype(jnp.int32)
    return pl.pallas_call(body, name=name, grid_spec=grid_spec, out_shape=jax.ShapeDtypeStruct(b.shape, out_dtype),
                          compiler_params=_cp(("parallel", "parallel")))(core, g, b)


def _pack_local(parts, group):
    table, rows = group
    segs = []
    for n, r, rp, tr in table:
        w = parts[n].T if tr else parts[n]
        segs.append(jnp.pad(w.reshape(r, PACK_COLS), ((0, rp - r), (0, 0))))
    segs.append(jnp.zeros((rows - sum(rp for _, _, rp, _ in table), PACK_COLS), segs[0].dtype))
    return jnp.concatenate(segs, axis=0)


def _unpack_local(packed, like, group):
    out, off = {}, 0
    for n, r, rp, tr in group[0]:
        rows, cols = like[n].shape
        seg = packed[off:off + r]
        out[n] = (seg.reshape(cols, rows).T if tr else seg)[None]
        off += rp
    return out


def _segments(g, group):
    out, off = {}, 0
    for n, r, rp, _ in group[0]:
        out[n] = g[:, off:off + r]
        off += rp
    return out


def _pack_grads(parts, group):
    table, rows = group
    segs = [jnp.pad(parts[n], ((0, 0), (0, rp - parts[n].shape[1]), (0, 0))) for n, _, rp, _ in table]
    segs.append(jnp.zeros((N_DEV, rows - sum(rp for _, _, rp, _ in table), PACK_COLS), F32))
    return jnp.concatenate(segs, axis=1)


def _owner_rows_early(g):
    g_in = jnp.concatenate([g["w_in_t"][:2432], g["w_in_t"][2496:2528]], axis=0).reshape(N_DEV, 308, PACK_COLS)
    g_uq = g["w_uq_t"].reshape(N_DEV, 128, MLA_Q_RANK)[:, :96].reshape(N_DEV, 24, PACK_COLS)
    g_ukv = jnp.concatenate([g["w_k_t"].reshape(N_DEV, 128, MLA_KV_RANK)[:, :64],
                             g["w_v_t"].reshape(N_DEV, 64, MLA_KV_RANK)], axis=1).reshape(N_DEV, 16, PACK_COLS)
    return dict(w_in=g_in, w_uq=g_uq, w_ukv=g_ukv)


def _owner_rows_late(g):
    g_up = g["w_up_t"].reshape(2, 2, 2, 704, PACK_COLS).swapaxes(0, 1).reshape(N_DEV, 704, PACK_COLS)
    return dict(w_out=g["w_out"].reshape(N_DEV, 128, PACK_COLS), w_up=g_up,
                w_down=g["w_down"].reshape(N_DEV, 352, PACK_COLS))


def _reduce_to_pairs(gp, *, name):
    gp = gp.reshape(4, 2, gp.shape[1], PACK_COLS)
    return _add_own(gp, _swap_sibling(gp, name=name + "_swap"), out_dtype=BF16, name=name + "_sum")


def _interleave_ff(w):
    g, v = w[..., :D_FF], w[..., D_FF:]
    return jnp.concatenate([g[..., :FF_HALF], v[..., :FF_HALF], g[..., FF_HALF:], v[..., FF_HALF:]], axis=-1)


def _rope_tables(pos):
    p = pos.astype(F32)[:, None]
    inv_r = ROPE_BASE ** (-jnp.arange(0, RET_HEAD_DIM, 2, dtype=F32) / RET_HEAD_DIM)
    ang = p * jnp.tile(inv_r, 4)
    sign_r = jnp.tile(jnp.concatenate([-jnp.ones((32,), F32), jnp.ones((32,), F32)]), 2)
    cos_r, ss_r = jnp.cos(ang), jnp.sin(ang) * sign_r
    inv_m = ROPE_BASE ** (-jnp.arange(0, MLA_ROPE, 2, dtype=F32) / MLA_ROPE)
    ang = p * jnp.concatenate([jnp.zeros((64,), F32), inv_m, inv_m, jnp.zeros((32,), F32)])
    sign_m = jnp.concatenate([jnp.zeros((64,), F32), -jnp.ones((16,), F32), jnp.ones((16,), F32), jnp.zeros((32,), F32)])
    cos_m, ss_m = jnp.cos(ang), jnp.sin(ang) * sign_m
    return cos_r, ss_r, cos_m, ss_m


def _prep_early(gathered):
    seg = _segments(gathered, EARLY)
    w_in_t = seg["w_in"].reshape(IN_WIDTH, D_MODEL)
    z = lambda n: jnp.zeros((n, D_MODEL), BF16)
    w_in_t = jnp.concatenate([w_in_t[:2432], z(64), w_in_t[2432:2464], z(32)], axis=0)
    w_uq_t = jnp.pad(seg["w_uq"].reshape(MLA_HEADS, 96, MLA_Q_RANK), ((0, 0), (0, 32), (0, 0))).reshape(1024, MLA_Q_RANK)
    ukv = seg["w_ukv"].reshape(MLA_HEADS, 128, MLA_KV_RANK)
    w_k_t = jnp.pad(ukv[:, :64], ((0, 0), (0, 64), (0, 0))).reshape(1024, MLA_KV_RANK)
    w_v_t = ukv[:, 64:].reshape(512, MLA_KV_RANK)
    return dict(w_in_t=w_in_t, w_uq_t=w_uq_t, w_k_t=w_k_t, w_v_t=w_v_t)


def _prep_late(gathered):
    seg = _segments(gathered, LATE)
    w_up_t = seg["w_up"].reshape(2, 2, 2, 704, D_MODEL).swapaxes(0, 1).reshape(2 * D_FF, D_MODEL)
    return dict(w_out=seg["w_out"].reshape(1024, D_MODEL), w_up_t=w_up_t, w_down=seg["w_down"].reshape(D_FF, D_MODEL))


def _local_step(x, pos, tgt, early, sm, late):
    dist = not isinstance(late, dict)
    cos_r, ss_r, cos_m, ss_m = _rope_tables(pos)
    tabs = _ret_tables()

    if dist:
        h, gathered = _rmsnorm_fwd(x, sm["attn_norm_w"], gather=early, name="attn_norm")
        W = _prep_early(gathered)
    else:
        h = _rmsnorm_fwd(x, sm["attn_norm_w"], name="attn_norm")
        W = early
    proj = _mm(h, W["w_in_t"], bt=True, name="in_proj")
    y_ret, o_ret = _ret_fwd(proj, cos_r, ss_r, tabs, sm["ret_gn_w"], name="ret_fwd")
    q, k, v1, cqn, ckvn = _mla_prep_fwd(proj, sm["mla_q_norm_w"], sm["mla_kv_norm_w"], W["w_uq_t"], W["w_k_t"],
                                       W["w_v_t"], cos_m, ss_m, name="mla_prep")
    T = x.shape[0]
    tq = min(T, 512)
    if dist:
        y_mla, lse, gathered = _flash_fwd(q, k, v1, gather=late, name="mla_attn")
        W = {**W, **_prep_late(gathered)}
    else:
        y_mla, lse = _flash_fwd(q, k, v1, name="mla_attn")
        W = {**W, **late}
    mixed = jnp.concatenate([y_ret, y_mla], axis=1)
    x1 = _mm(mixed, W["w_out"], add=x, name="out_proj")
    h2 = _rmsnorm_fwd(x1, sm["ffn_norm_w"], name="ffn_norm")
    u, a = _up_proj_conv(h2, W["w_up_t"], sm["conv_w"], sm["conv_b"], name="up_proj_conv")
    loss, dx2, dx2b, d_final = _down_proj_loss(a, W["w_down"], x1, tgt, sm["final_norm_w"], name="down_proj_loss")

    g = {}
    g["w_down"] = _mm_tn(a, dx2b, name="dw_down")
    da = _mm(dx2b, W["w_down"], bt=True, name="d_act")
    du, dcw0, dcw1, dcw2, dcb = _conv_bwd(u, da, sm["conv_w"], sm["conv_b"], name="conv_bwd")
    g["w_up_t"] = _mm_tn(du, h2, name="dw_up")
    dx1, d_ffn = _mm_norm_bwd(du, W["w_up_t"], x1, sm["ffn_norm_w"], dx2, name="d_h2_ffn_norm_bwd")

    g["w_out"] = _mm_tn(mixed, dx1, name="dw_out")
    dmixed = _mm(dx1, W["w_out"], bt=True, name="d_mixed")
    do_ret, dg, do_mla, delta, d_gn = _mix_bwd(dmixed, o_ret, proj, y_mla, sm["ret_gn_w"], name="mix_bwd")
    drq = _ret_bwd_dq(proj, do_ret, cos_r, ss_r, tabs, name="ret_bwd_dq")
    delta_r = delta.reshape(MLA_HEADS, T // tq, 1, tq)
    if dist:
        gl = _pack_grads(_owner_rows_late(g), LATE).reshape(4, 2, LATE[1], PACK_COLS)
        drk, drv, theirs = _ret_bwd_dkv(proj, do_ret, cos_r, ss_r, tabs, swap=gl, name="ret_bwd_dkv")
        pair = _add_own(gl, theirs, out_dtype=BF16, name="grad_late_sum")
        dqt, dk, dv, slots_late = _flash_bwd(q, k, v1, do_mla, lse, delta_r, exchange=pair, name="mla_attn_bwd")
    else:
        drk, drv = _ret_bwd_dkv(proj, do_ret, cos_r, ss_r, tabs, name="ret_bwd_dkv")
        dqt, dk, dv = _flash_bwd(q, k, v1, do_mla, lse, delta_r, name="mla_attn_bwd")
        slots_late = None
    dq = dqt.transpose(1, 3, 0, 2).reshape(T, MLA_HEADS * 128)
    dproj, dqp, d_qn, d_kvn = _mla_prep_bwd(dq, dk, dv, proj, sm["mla_q_norm_w"], sm["mla_kv_norm_w"], W["w_uq_t"],
                                            W["w_k_t"], W["w_v_t"], cos_m, ss_m, (drq, drk, drv, dg),
                                            name="mla_prep_bwd")
    g["w_uq_t"] = _mm_tn(dqp, cqn, name="dw_uq")
    g["w_k_t"] = _mm_tn(dk, ckvn, name="dw_ukv_k")
    g["w_v_t"] = _mm_tn(dv, ckvn, name="dw_ukv_v")
    g["w_in_t"] = _mm_tn(dproj, h, name="dw_in")
    if dist:
        pair = _reduce_to_pairs(_pack_grads(_owner_rows_early(g), EARLY), name="grad_early")
        grad_x, d_attn, slots_early = _mm_norm_bwd(dproj, W["w_in_t"], x, sm["attn_norm_w"], dx1, exchange=pair,
                                                   name="d_h_attn_norm_bwd")
    else:
        grad_x, d_attn = _mm_norm_bwd(dproj, W["w_in_t"], x, sm["attn_norm_w"], dx1, name="d_h_attn_norm_bwd")
        slots_early = None

    small = dict(attn_norm_w=d_attn, ret_gn_w=d_gn, mla_q_norm_w=d_qn, mla_kv_norm_w=d_kvn, ffn_norm_w=d_ffn,
                 conv_b=dcb, final_norm_w=d_final, conv_w0=dcw0, conv_w1=dcw1, conv_w2=dcw2, loss=loss)
    return loss, grad_x, g, small, slots_early, slots_late


def kernel(x, positions, attn_norm_w, w_in, ret_gn_w, mla_q_norm_w, w_uq, mla_kv_norm_w, w_ukv, w_out, ffn_norm_w, w_up, conv_w, conv_b, w_down, final_norm_w, loss_target, m_attn_norm_w, m_w_in, m_ret_gn_w, m_mla_q_norm_w, m_w_uq, m_mla_kv_norm_w, m_w_ukv, m_w_out, m_ffn_norm_w, m_w_up, m_conv_w, m_conv_b, m_w_down, m_final_norm_w, v_attn_norm_w, v_w_in, v_ret_gn_w, v_mla_q_norm_w, v_w_uq, v_mla_kv_norm_w, v_w_ukv, v_w_out, v_ffn_norm_w, v_w_up, v_conv_w, v_conv_b, v_w_down, v_final_norm_w):
    a = dict(locals())
    x_, y_, c_ = _place()
    dev = 4 * x_ + 2 * y_ + c_

    shard = {n: a[n][0] for n in BIG_NAMES}
    shard16 = {n: w.astype(BF16) for n, w in shard.items()}
    cw_pad = jnp.pad(conv_w[0].reshape(-1), (0, 24 * 128 - 3 * 704)).reshape(24, 128)
    cw_all = _all_gather(cw_pad, name="gather_conv_w", in_vmem=True)
    conv_w_full = cw_all.reshape(N_DEV, -1)[:, :3 * 704].reshape(N_DEV, 3, 704).transpose(1, 0, 2).reshape(3, 2 * D_FF)
    sm = dict(attn_norm_w=attn_norm_w, ret_gn_w=ret_gn_w, mla_q_norm_w=mla_q_norm_w, mla_kv_norm_w=mla_kv_norm_w,
              ffn_norm_w=ffn_norm_w, final_norm_w=final_norm_w.reshape(1, D_MODEL),
              conv_w=_interleave_ff(conv_w_full), conv_b=_interleave_ff(conv_b))

    loss, grad_x, _, gs, slots_early, slots_late = _local_step(
        x[0], positions[0], loss_target[0], _pack_local(shard16, EARLY), sm, _pack_local(shard16, LATE))

    big = [{}, {}, {}, {}]
    for group, slots, tag in ((EARLY, slots_early, "early"), (LATE, slots_late, "late")):
        names_g = [n for n, _, _, _ in group[0]]
        res = _adamw(_pack_local({n: shard[n] for n in names_g}, group),
                     _pack_local({n: a["m_" + n][0] for n in names_g}, group),
                     _pack_local({n: a["v_" + n][0] for n in names_g}, group), slots, name="adamw_" + tag)
        for kind in range(4):
            big[kind].update(_unpack_local(res[kind], shard, group))

    packed = _pack_small(gs, name="pack_small_grads")
    tot = _sum_small(_all_gather(packed, name="gather_small_grads", in_vmem=True), name="sum_small_grads")
    loss_out = tot["loss"][0, 0]
    g_cw = lax.dynamic_slice_in_dim(tot["conv_w"], dev * 704, 704, axis=1)

    def rows_of(prefix):
        return [a[prefix + n].reshape(1, size) for n, size in SMALL]

    sml = _adamw_small(rows_of(""), rows_of("m_"), rows_of("v_"), [tot[n] for n, _ in SMALL], name="adamw_small")
    cwo = _adamw(conv_w[0], m_conv_w[0], v_conv_w[0], g_cw[None], name="adamw_conv_w")

    def small_of(kind, n):
        return sml[kind][[nm for nm, _ in SMALL].index(n)].reshape(a[n].shape)

    names = ['attn_norm_w', 'w_in', 'ret_gn_w', 'mla_q_norm_w', 'w_uq', 'mla_kv_norm_w', 'w_ukv', 'w_out',
             'ffn_norm_w', 'w_up', 'conv_w', 'conv_b', 'w_down', 'final_norm_w']
    outs = [loss_out, grad_x[None]]
    for kind in range(4):
        for n in names:
            if n == "conv_w":
                outs.append(cwo[kind][None])
            elif n in big[kind]:
                outs.append(big[kind][n])
            else:
                outs.append(small_of(kind, n))
    return tuple(outs)
```

```python
import functools

import numpy as np
import jax
import jax.numpy as jnp
from jax import lax
from jax.experimental import pallas as pl
from jax.experimental.pallas import tpu as pltpu

F32 = jnp.float32
BF16 = jnp.bfloat16
MESH = pl.DeviceIdType.MESH
ANY = pl.BlockSpec(memory_space=pl.ANY)

D_MODEL = 1024
RET_HEADS = 8
RET_HEAD_DIM = 64
RET_WIDTH = 512
RET_CHUNK = 128
MLA_HEADS = 8
MLA_NOPE = 64
MLA_ROPE = 32
MLA_V = 64
MLA_Q_RANK = 256
MLA_KV_RANK = 128
MLA_WIDTH = 512
IN_WIDTH = 2464
IN_PAD = 2560
D_FF = 2816
FF_HALF = 1408
ROPE_BASE = 10000.0
EPS = 1e-6
SCALE = float((MLA_NOPE + MLA_ROPE) ** -0.5)
K_SCALE = 0.125
N_DEV = 8

ADAM_LR = 0.001
ADAM_B1 = 0.9
ADAM_B2 = 0.999
ADAM_EPS = 1e-08
ADAM_WD = 0.01
ADAM_STEP = 10

VMEM_LIMIT = 56 * 1024 * 1024
MM_BUDGET = 40 * 1024 * 1024
NEG = -1e30
FLASH_UNROLL = 4

PACK_COLS = 1024
EARLY = ((("w_in", 308, 320, True), ("w_uq", 24, 32, True), ("w_ukv", 16, 16, True)), 384)
LATE = ((("w_out", 128, 128, False), ("w_up", 704, 704, True), ("w_down", 352, 352, False)), 1200)
BIG_NAMES = ("w_in", "w_uq", "w_ukv", "w_out", "w_up", "w_down")
SMALL = (("attn_norm_w", 1024), ("ret_gn_w", 512), ("mla_q_norm_w", 256), ("mla_kv_norm_w", 128),
         ("ffn_norm_w", 1024), ("conv_b", 5632), ("final_norm_w", 1024))
SMALL_VECTORS = SMALL + (("conv_w0", 5632), ("conv_w1", 5632), ("conv_w2", 5632), ("loss", 128))
SMALL_ROWS = 32


def _cp(sem=None, vmem=VMEM_LIMIT):
    return pltpu.CompilerParams(dimension_semantics=sem, vmem_limit_bytes=vmem)


def _dot(a, b):
    return jnp.dot(a, b, preferred_element_type=F32)


def _dot_nt(a, b):
    return lax.dot_general(a, b, (((1,), (1,)), ((), ())), preferred_element_type=F32)


def _dot_tn(a, b):
    return lax.dot_general(a, b, (((0,), (0,)), ((), ())), preferred_element_type=F32)


def _sigmoid(x):
    return 0.5 * jnp.tanh(0.5 * x) + 0.5


def _partner(x, half, period):
    n = x.shape[-1]
    lane = lax.broadcasted_iota(jnp.int32, x.shape, 1)
    return jnp.where((lane % period) < half, pltpu.roll(x, n - half, 1), pltpu.roll(x, half, 1))


def _rope(x, cos, ss, half, period):
    return x * cos + _partner(x, half, period) * ss


def _rope_t(dy, cos, ss, half, period):
    return dy * cos - _partner(dy, half, period) * ss


def _head_masks(shape):
    lane = lax.broadcasted_iota(jnp.int32, shape, 1)
    m0 = (lane < 64).astype(F32)
    return m0, 1.0 - m0


def _mm(a, b, *, name, add=None, out_dtype=F32, bt=False):
    parts = a if isinstance(a, tuple) else (a,)
    M = parts[0].shape[0]
    K = sum(p.shape[1] for p in parts)
    N = b.shape[0] if bt else b.shape[1]
    osz = jnp.dtype(out_dtype).itemsize
    per_row = 2 * (K * parts[0].dtype.itemsize + N * osz + (N * 4 if add is not None else 0))
    tm = 128
    for cand in (512, 256):
        if M % cand == 0 and cand * per_row + 4 * K * N <= MM_BUDGET:
            tm = cand
            break
    tm = min(tm, M)
    mul = _dot_nt if bt else _dot
    n_a = len(parts)
    n_in = n_a + (1 if add is None else 2)

    def body(*refs):
        av = refs[0][...] if n_a == 1 else jnp.concatenate([r[...] for r in refs[:n_a]], axis=1)
        acc = mul(av.astype(BF16), refs[n_a][...])
        if add is not None:
            acc = refs[n_a + 1][...] + acc
        refs[n_in][...] = acc.astype(out_dtype)

    in_specs = [pl.BlockSpec((tm, p.shape[1]), lambda i: (i, 0)) for p in parts]
    in_specs.append(pl.BlockSpec(b.shape, lambda i: (0, 0)))
    args = [*parts, b]
    if add is not None:
        in_specs.append(pl.BlockSpec((tm, N), lambda i: (i, 0)))
        args.append(add)
    return pl.pallas_call(
        body, name=name, grid=(M // tm,), in_specs=in_specs, out_specs=pl.BlockSpec((tm, N), lambda i: (i, 0)),
        out_shape=jax.ShapeDtypeStruct((M, N), out_dtype), compiler_params=_cp(("parallel",)))(*args)


def _mm_tn(a, b, *, name):
    parts = a if isinstance(a, tuple) else (a,)
    T = parts[0].shape[0]
    M = sum(p.shape[1] for p in parts)
    N = b.shape[1]
    tk = min(T, 512)

    def tile(n):
        for cand in (1408, 1280):
            if n > 1408 and n % cand == 0:
                return cand
        return n

    tm, tn = tile(M), tile(N)
    nk = T // tk
    n_a = len(parts)
    assert n_a == 1 or tm == M

    def body(*refs):
        o_ref = refs[n_a + 1]

        @pl.when(pl.program_id(2) == 0)
        def _():
            o_ref[...] = jnp.zeros_like(o_ref)
        av = refs[0][...] if n_a == 1 else jnp.concatenate([r[...] for r in refs[:n_a]], axis=1)
        o_ref[...] += _dot_tn(av.astype(BF16), refs[n_a][...].astype(BF16))

    if n_a == 1:
        a_specs = [pl.BlockSpec((tk, tm), lambda i, j, k: (k, i))]
    else:
        a_specs = [pl.BlockSpec((tk, p.shape[1]), lambda i, j, k: (k, 0)) for p in parts]
    return pl.pallas_call(
        body, name=name, grid=(M // tm, N // tn, nk),
        in_specs=a_specs + [pl.BlockSpec((tk, tn), lambda i, j, k: (k, j))],
        out_specs=pl.BlockSpec((tm, tn), lambda i, j, k: (i, j)),
        out_shape=jax.ShapeDtypeStruct((M, N), F32),
        compiler_params=_cp(("parallel", "parallel", "arbitrary")))(*parts, b)


def _rmsnorm_fwd(x, w, *, name, gather=None):
    T, D = x.shape
    tm = min(T, 1024)
    n = T // tm

    def body(x_ref, w_ref, *rest):
        if gather is not None:
            s_ref, o_ref, g_ref, *sems = rest
            start, forward, finish = _gather_phases(s_ref, g_ref, *sems)
            pl.when(pl.program_id(0) == 0)(start)
            pl.when(pl.program_id(0) == n // 2)(forward)
        else:
            o_ref, = rest
        xv = x_ref[...]
        r = lax.rsqrt(jnp.mean(xv * xv, axis=-1, keepdims=True) + EPS)
        o_ref[...] = (xv * r * w_ref[...]).astype(BF16)
        if gather is not None:
            pl.when(pl.program_id(0) == n - 1)(finish)

    in_specs = [pl.BlockSpec((tm, D), lambda i: (i, 0)), pl.BlockSpec((1, D), lambda i: (0, 0))]
    out_spec = pl.BlockSpec((tm, D), lambda i: (i, 0))
    out_shape = jax.ShapeDtypeStruct((T, D), BF16)
    if gather is None:
        return pl.pallas_call(body, name=name, grid=(n,), in_specs=in_specs, out_specs=out_spec, out_shape=out_shape,
                              compiler_params=_cp(("parallel",)))(x, w)
    return pl.pallas_call(
        body, name=name, grid=(n,), in_specs=in_specs + [ANY], out_specs=[out_spec, ANY],
        out_shape=[out_shape, jax.ShapeDtypeStruct((N_DEV,) + gather.shape, gather.dtype)],
        scratch_shapes=list(GATHER_SCRATCH), compiler_params=_cp(("arbitrary",)))(x, w, gather)


def _mm_norm_bwd(a, b, x, w, dres, *, name, exchange=None):
    T, K = a.shape
    D = b.shape[1]
    tm = min(T, 256 if K > 4096 else 512)
    n = T // tm

    def body(a_ref, b_ref, x_ref, w_ref, dr_ref, *rest):
        if exchange is None:
            dx_ref, dw_ref = rest
        else:
            p_ref, dx_ref, dw_ref, got_ref, *sems = rest
            start, finish = _exchange_phases(p_ref, got_ref, *sems)
            pl.when(pl.program_id(0) == 0)(start)

        @pl.when(pl.program_id(0) == 0)
        def _():
            dw_ref[...] = jnp.zeros_like(dw_ref)
        dh = _dot(a_ref[...], b_ref[...])
        xv = x_ref[...]
        r = lax.rsqrt(jnp.mean(xv * xv, axis=-1, keepdims=True) + EPS)
        xh = xv * r
        g = dh * w_ref[...]
        dx_ref[...] = dr_ref[...] + r * (g - xh * jnp.mean(g * xh, axis=-1, keepdims=True))
        dw_ref[...] += jnp.sum(dh * xh, axis=0, keepdims=True)
        if exchange is not None:
            pl.when(pl.program_id(0) == n - 1)(finish)

    row = pl.BlockSpec((tm, D), lambda i: (i, 0))
    vec = pl.BlockSpec((1, D), lambda i: (0, 0))
    in_specs = [pl.BlockSpec((tm, K), lambda i: (i, 0)), pl.BlockSpec((K, D), lambda i: (0, 0)), row, vec, row]
    out_shape = [jax.ShapeDtypeStruct((T, D), F32), jax.ShapeDtypeStruct((1, D), F32)]
    if exchange is None:
        return pl.pallas_call(body, name=name, grid=(n,), in_specs=in_specs, out_specs=[row, vec], out_shape=out_shape,
                              compiler_params=_cp(("arbitrary",)))(a, b, x, w, dres)
    return pl.pallas_call(
        body, name=name, grid=(n,), in_specs=in_specs + [ANY], out_specs=[row, vec, ANY],
        out_shape=out_shape + [jax.ShapeDtypeStruct(exchange.shape, exchange.dtype)],
        scratch_shapes=list(EXCHANGE_SCRATCH), compiler_params=_cp(("arbitrary",)))(a, b, x, w, dres, exchange)


def _down_proj_loss(a, w_down, x1, tgt, w, *, name):
    T, D = x1.shape
    K = a.shape[1]
    tm = min(T, 512)

    def body(a_ref, b_ref, x_ref, t_ref, w_ref, loss_ref, dx_ref, dxb_ref, dw_ref):
        @pl.when(pl.program_id(0) == 0)
        def _():
            dw_ref[...] = jnp.zeros_like(dw_ref)
            loss_ref[...] = jnp.zeros_like(loss_ref)
        xv = x_ref[...] + _dot(a_ref[...], b_ref[...])
        wv = w_ref[...]
        r = lax.rsqrt(jnp.mean(xv * xv, axis=-1, keepdims=True) + EPS)
        xh = xv * r
        e = xh * wv - t_ref[...]
        part = 0.5 * jnp.sum(jnp.mean(e * e, axis=-1, keepdims=True), axis=0, keepdims=True)
        loss_ref[...] += jnp.broadcast_to(part, loss_ref.shape)
        dy = e * (1.0 / D)
        g = dy * wv
        dx = r * (g - xh * jnp.mean(g * xh, axis=-1, keepdims=True))
        dx_ref[...] = dx
        dxb_ref[...] = dx.astype(BF16)
        dw_ref[...] += jnp.sum(dy * xh, axis=0, keepdims=True)

    row = pl.BlockSpec((tm, D), lambda i: (i, 0))
    vec = pl.BlockSpec((1, D), lambda i: (0, 0))
    return pl.pallas_call(
        body, name=name, grid=(T // tm,),
        in_specs=[pl.BlockSpec((tm, K), lambda i: (i, 0)), pl.BlockSpec((K, D), lambda i: (0, 0)), row, row, vec],
        out_specs=[pl.BlockSpec((1, 128), lambda i: (0, 0)), row, row, vec],
        out_shape=[jax.ShapeDtypeStruct((1, 128), F32), jax.ShapeDtypeStruct((T, D), F32),
                   jax.ShapeDtypeStruct((T, D), BF16), jax.ShapeDtypeStruct((1, D), F32)],
        compiler_params=_cp(("arbitrary",)))(a, w_down, x1, tgt, w)


def _ret_tables():
    C = RET_CHUNK
    h = jnp.arange(RET_HEADS, dtype=F32)
    log_gamma = jnp.log1p(-jnp.power(2.0, -5.0 - h))
    idx = jnp.arange(C, dtype=F32)
    diff = idx[:, None] - idx[None, :]
    dm = jnp.where(diff >= 0, jnp.exp(log_gamma[:, None, None] * jnp.maximum(diff, 0.0)), 0.0)
    dm = dm.reshape(4, 2 * C, C)
    lane_head = jnp.repeat(jnp.arange(RET_HEADS).reshape(4, 2), 64, axis=1)
    lg = log_gamma[lane_head]
    xi = jnp.exp(lg[:, None, :] * (idx[None, :, None] + 1.0))
    zeta = jnp.exp(lg[:, None, :] * (C - 1.0 - idx[None, :, None]))
    blk = (jnp.arange(128)[:, None] // 64) == (jnp.arange(128)[None, :] // 64)
    cd = jnp.where(blk[None], jnp.exp(lg * C)[:, :, None], 0.0)
    return dm.astype(F32), xi.astype(F32), zeta.astype(F32), cd.astype(F32)


def _ret_specs(tb, rev, nt):
    def tmap(t):
        return (nt - 1 - t) if rev else t
    qkv = [pl.BlockSpec((tb, 128), lambda p, t, o=o: (tmap(t), o + p)) for o in (0, 4, 8)]
    rope = [pl.BlockSpec((tb, 128), lambda p, t: (tmap(t), 0))] * 2
    tabs = [pl.BlockSpec((None, 256, 128), lambda p, t: (p, 0, 0))] + \
           [pl.BlockSpec((None, 128, 128), lambda p, t: (p, 0, 0))] * 3
    return qkv, rope, tabs


def _ret_fwd(proj, cos, ss, tabs, gnw, *, name):
    T = proj.shape[0]
    tb = min(T, 1024)
    nt = T // tb
    nchunk = tb // RET_CHUNK

    def body(q_ref, k_ref, v_ref, g_ref, cos_ref, ss_ref, dm_ref, xi_ref, zt_ref, cd_ref, gnw_ref,
             y_ref, o_ref, r_sc):
        @pl.when(pl.program_id(1) == 0)
        def _():
            r_sc[...] = jnp.zeros_like(r_sc)
        m0, m1 = _head_masks((128, 128))
        dm, xi, zt, cd = dm_ref[...], xi_ref[...], zt_ref[...], cd_ref[...]
        bm = (cd > 0).astype(F32)
        gnw = gnw_ref[...]
        for c in range(nchunk):
            rs = pl.ds(c * RET_CHUNK, RET_CHUNK)
            cs, sn = cos_ref[rs, :], ss_ref[rs, :]
            q = _rope(q_ref[rs, :], cs, sn, 32, 64)
            k = _rope(k_ref[rs, :], cs, sn, 32, 64) * K_SCALE
            v = v_ref[rs, :]
            kb, vb = k.astype(BF16), v.astype(BF16)
            qs = jnp.concatenate([q * m0, q * m1], axis=0).astype(BF16)
            s = (_dot_nt(qs, kb) * dm).astype(BF16)
            vs = jnp.concatenate([v * m0, v * m1], axis=0).astype(BF16)
            o = _dot(jnp.concatenate([s[:128], s[128:]], axis=1), vs)
            r = r_sc[...]
            o = o + _dot(q.astype(BF16), r.astype(BF16)) * xi
            r_sc[...] = cd * r + bm * _dot_tn((k * zt).astype(BF16), vb)
            mu = (jnp.sum(o * m0, axis=1, keepdims=True) * m0 + jnp.sum(o * m1, axis=1, keepdims=True) * m1) * (1.0 / 64)
            d = o - mu
            dd = d * d
            var = (jnp.sum(dd * m0, axis=1, keepdims=True) * m0 + jnp.sum(dd * m1, axis=1, keepdims=True) * m1) * (1.0 / 64)
            oh = d * lax.rsqrt(var + EPS)
            g = g_ref[rs, :]
            y_ref[rs, :] = (g * _sigmoid(g) * (oh * gnw)).astype(BF16)
            o_ref[rs, :] = o

    qkv, rope, tspec = _ret_specs(tb, False, nt)
    gspec = pl.BlockSpec((tb, 128), lambda p, t: (t, 12 + p))
    out = pl.BlockSpec((tb, 128), lambda p, t: (t, p))
    return pl.pallas_call(
        body, name=name, grid=(4, nt),
        in_specs=qkv + [gspec] + rope + tspec + [pl.BlockSpec((1, 128), lambda p, t: (0, p))],
        out_specs=[out, out],
        out_shape=[jax.ShapeDtypeStruct((T, RET_WIDTH), BF16), jax.ShapeDtypeStruct((T, RET_WIDTH), F32)],
        scratch_shapes=[pltpu.VMEM((128, 128), F32)],
        compiler_params=_cp(("parallel", "arbitrary")))(proj, proj, proj, proj, cos, ss, *tabs, gnw)


def _ret_bwd_dq(proj, do, cos, ss, tabs, *, name):
    T = proj.shape[0]
    tb = min(T, 1024)
    nt = T // tb
    nchunk = tb // RET_CHUNK

    def body(q_ref, k_ref, v_ref, do_ref, cos_ref, ss_ref, dm_ref, xi_ref, zt_ref, cd_ref, dq_ref, r_sc):
        del q_ref
        @pl.when(pl.program_id(1) == 0)
        def _():
            r_sc[...] = jnp.zeros_like(r_sc)
        m0, m1 = _head_masks((128, 128))
        dm, xi, zt, cd = dm_ref[...], xi_ref[...], zt_ref[...], cd_ref[...]
        bm = (cd > 0).astype(F32)
        for c in range(nchunk):
            rs = pl.ds(c * RET_CHUNK, RET_CHUNK)
            cs, sn = cos_ref[rs, :], ss_ref[rs, :]
            k = _rope(k_ref[rs, :], cs, sn, 32, 64) * K_SCALE
            vb = v_ref[rs, :].astype(BF16)
            dob = do_ref[rs, :]
            dof = dob.astype(F32)
            dos = jnp.concatenate([dof * m0, dof * m1], axis=0).astype(BF16)
            a = (_dot_nt(dos, vb) * dm).astype(BF16)
            ks = jnp.concatenate([k * m0, k * m1], axis=0).astype(BF16)
            r = r_sc[...]
            dq = _dot(jnp.concatenate([a[:128], a[128:]], axis=1), ks) + _dot_nt(dob, r.astype(BF16)) * xi
            r_sc[...] = cd * r + bm * _dot_tn((k * zt).astype(BF16), vb)
            dq_ref[rs, :] = _rope_t(dq, cs, sn, 32, 64).astype(BF16)

    qkv, rope, tspec = _ret_specs(tb, False, nt)
    blk = pl.BlockSpec((tb, 128), lambda p, t: (t, p))
    return pl.pallas_call(
        body, name=name, grid=(4, nt), in_specs=qkv + [blk] + rope + tspec, out_specs=blk,
        out_shape=jax.ShapeDtypeStruct((T, RET_WIDTH), BF16),
        scratch_shapes=[pltpu.VMEM((128, 128), F32)],
        compiler_params=_cp(("parallel", "arbitrary")))(proj, proj, proj, do, cos, ss, *tabs)


def _ret_bwd_dkv(proj, do, cos, ss, tabs, *, name, swap=None):
    T = proj.shape[0]
    tb = min(T, 1024)
    nt = T // tb
    nchunk = tb // RET_CHUNK

    def body(q_ref, k_ref, v_ref, do_ref, cos_ref, ss_ref, dm_ref, xi_ref, zt_ref, cd_ref, *rest):
        if swap is None:
            backward(q_ref, k_ref, v_ref, do_ref, cos_ref, ss_ref, dm_ref, xi_ref, zt_ref, cd_ref, *rest)
        else:
            g_ref, dk_ref, dv_ref, got_ref, u_sc, *sems = rest
            start, finish = _swap_phases(g_ref, got_ref, *sems)
            pl.when((pl.program_id(0) == 0) & (pl.program_id(1) == 0))(start)
            backward(q_ref, k_ref, v_ref, do_ref, cos_ref, ss_ref, dm_ref, xi_ref, zt_ref, cd_ref, dk_ref, dv_ref, u_sc)
            pl.when((pl.program_id(0) == 3) & (pl.program_id(1) == nt - 1))(finish)

    def backward(q_ref, k_ref, v_ref, do_ref, cos_ref, ss_ref, dm_ref, xi_ref, zt_ref, cd_ref, dk_ref, dv_ref, u_sc):
        @pl.when(pl.program_id(1) == 0)
        def _():
            u_sc[...] = jnp.zeros_like(u_sc)
        m0, m1 = _head_masks((128, 128))
        dm, xi, zt, cd = dm_ref[...], xi_ref[...], zt_ref[...], cd_ref[...]
        bm = (cd > 0).astype(F32)
        for c in reversed(range(nchunk)):
            rs = pl.ds(c * RET_CHUNK, RET_CHUNK)
            cs, sn = cos_ref[rs, :], ss_ref[rs, :]
            q = _rope(q_ref[rs, :], cs, sn, 32, 64)
            k = _rope(k_ref[rs, :], cs, sn, 32, 64) * K_SCALE
            kb = k.astype(BF16)
            vb = v_ref[rs, :].astype(BF16)
            dob = do_ref[rs, :]
            dof = dob.astype(F32)
            qs = jnp.concatenate([q * m0, q * m1], axis=0).astype(BF16)
            dos = jnp.concatenate([dof * m0, dof * m1], axis=0).astype(BF16)
            s = (_dot_nt(qs, kb) * dm).astype(BF16)
            a = (_dot_nt(dos, vb) * dm).astype(BF16)
            ub = u_sc[...].astype(BF16)
            dk = _dot_tn(a, qs) + _dot_nt(vb, ub) * zt
            dv = _dot_tn(s, dos) + _dot(kb, ub) * zt
            u_sc[...] = cd * u_sc[...] + bm * _dot_tn((q * xi).astype(BF16), dob)
            dk_ref[rs, :] = (_rope_t(dk, cs, sn, 32, 64) * K_SCALE).astype(BF16)
            dv_ref[rs, :] = dv.astype(BF16)

    qkv, rope, tspec = _ret_specs(tb, True, nt)
    blk = pl.BlockSpec((tb, 128), lambda p, t: (nt - 1 - t, p))
    out_shape = [jax.ShapeDtypeStruct((T, RET_WIDTH), BF16)] * 2
    if swap is None:
        return pl.pallas_call(
            body, name=name, grid=(4, nt), in_specs=qkv + [blk] + rope + tspec, out_specs=[blk, blk],
            out_shape=out_shape, scratch_shapes=[pltpu.VMEM((128, 128), F32)],
            compiler_params=_cp(("parallel", "arbitrary")))(proj, proj, proj, do, cos, ss, *tabs)
    return pl.pallas_call(
        body, name=name, grid=(4, nt), in_specs=qkv + [blk] + rope + tspec + [ANY], out_specs=[blk, blk, ANY],
        out_shape=out_shape + [jax.ShapeDtypeStruct((4,) + swap.shape[2:], swap.dtype)],
        scratch_shapes=[pltpu.VMEM((128, 128), F32)] + list(SWAP_SCRATCH),
        compiler_params=_cp(("arbitrary", "arbitrary")))(proj, proj, proj, do, cos, ss, *tabs, swap)


def _mix_bwd(dmixed, o_ret, proj, y_mla, gnw, *, name):
    T = dmixed.shape[0]
    tm = min(T, 512)

    def body(dm_ref, o_ref, g_ref, ym_ref, gnw_ref, do_ref, dg_ref, dom_ref, dl_ref, dw_ref):
        @pl.when(pl.program_id(0) == 0)
        def _():
            dw_ref[...] = jnp.zeros_like(dw_ref)
        m0, m1 = _head_masks((tm, 128))
        lane = lax.broadcasted_iota(jnp.int32, (tm, 128), 1)
        delta = jnp.zeros((tm, 128), F32)

        def gsum(z):
            return jnp.sum(z * m0, axis=1, keepdims=True) * m0 + jnp.sum(z * m1, axis=1, keepdims=True) * m1

        for p in range(4):
            cs = slice(128 * p, 128 * p + 128)
            dy = dm_ref[:, cs]
            o = o_ref[:, cs]
            g = g_ref[:, cs]
            w = gnw_ref[:, cs]
            d = o - gsum(o) * (1.0 / 64)
            rstd = lax.rsqrt(gsum(d * d) * (1.0 / 64) + EPS)
            oh = d * rstd
            sg = _sigmoid(g)
            dn = dy * (g * sg)
            dg_ref[:, cs] = (dy * (oh * w) * (sg * (1.0 + g * (1.0 - sg)))).astype(BF16)
            dw_ref[:, cs] += jnp.sum(dn * oh, axis=0, keepdims=True)
            doh = dn * w
            do = rstd * (doh - gsum(doh) * (1.0 / 64) - oh * (gsum(doh * oh) * (1.0 / 64)))
            do_ref[:, cs] = do.astype(BF16)
            dom = dm_ref[:, 512 + 128 * p:512 + 128 * p + 128]
            dom_ref[:, cs] = dom.astype(BF16)
            pr = dom * ym_ref[:, cs].astype(F32)
            delta = jnp.where(lane == 2 * p, jnp.sum(pr * m0, axis=1, keepdims=True), delta)
            delta = jnp.where(lane == 2 * p + 1, jnp.sum(pr * m1, axis=1, keepdims=True), delta)
        dl_ref[...] = delta.T[0:MLA_HEADS]

    half = pl.BlockSpec((tm, 512), lambda i: (i, 0))
    return pl.pallas_call(
        body, name=name, grid=(T // tm,),
        in_specs=[pl.BlockSpec((tm, 1024), lambda i: (i, 0)), half, pl.BlockSpec((tm, 512), lambda i: (i, 3)),
                  half, pl.BlockSpec((1, 512), lambda i: (0, 0))],
        out_specs=[half, half, half, pl.BlockSpec((MLA_HEADS, tm), lambda i: (0, i)),
                   pl.BlockSpec((1, 512), lambda i: (0, 0))],
        out_shape=[jax.ShapeDtypeStruct((T, 512), BF16)] * 3 + [jax.ShapeDtypeStruct((MLA_HEADS, T), F32),
                                                                jax.ShapeDtypeStruct((1, 512), F32)],
        compiler_params=_cp(("arbitrary",)))(dmixed, o_ret, proj, y_mla, gnw)


def _mla_prep_fwd(proj, qnw, kvnw, wuq, wk, wv, cos, ss, *, name):
    T = proj.shape[0]
    tm = min(T, 512)

    def body(lat_ref, qnw_ref, kvnw_ref, wuq_ref, wk_ref, wv_ref, cos_ref, ss_ref,
             q_ref, k_ref, v_ref, cqn_ref, ckvn_ref):
        cq = lat_ref[:, 0:256]
        ckv = lat_ref[:, 256:384]
        g3 = lat_ref[:, 384:512]
        cqn = (cq * lax.rsqrt(jnp.mean(cq * cq, axis=-1, keepdims=True) + EPS) * qnw_ref[...]).astype(BF16)
        ckvn = (ckv * lax.rsqrt(jnp.mean(ckv * ckv, axis=-1, keepdims=True) + EPS) * kvnw_ref[...]).astype(BF16)
        cqn_ref[...] = cqn
        ckvn_ref[...] = ckvn
        cs, sn = cos_ref[...], ss_ref[...]
        q = _dot_nt(cqn, wuq_ref[...])
        k = _dot_nt(ckvn, wk_ref[...])
        kpe = _rope(g3, cs, sn, 16, 32)
        for h in range(MLA_HEADS):
            hs = slice(128 * h, 128 * h + 128)
            q_ref[:, hs] = (_rope(q[:, hs], cs, sn, 16, 32) * SCALE).astype(BF16)
            k_ref[:, hs] = (k[:, hs] + kpe).astype(BF16)
        v = _dot_nt(ckvn, wv_ref[...])
        lane = lax.broadcasted_iota(jnp.int32, (tm, 128), 1)
        for p in range(4):
            vp = v[:, 128 * p:128 * p + 128]
            v_ref[:, 256 * p:256 * p + 128] = jnp.where(lane < 64, vp, 1.0).astype(BF16)
            v_ref[:, 256 * p + 128:256 * p + 256] = jnp.where(lane < 64, 1.0, vp).astype(BF16)

    def full(shape):
        return pl.BlockSpec(shape, lambda i: (0, 0))

    def row(w):
        return pl.BlockSpec((tm, w), lambda i: (i, 0))

    return pl.pallas_call(
        body, name=name, grid=(T // tm,),
        in_specs=[pl.BlockSpec((tm, 512), lambda i: (i, 4)), full((1, 256)), full((1, 128)), full((1024, 256)),
                  full((1024, 128)), full((512, 128)), row(128), row(128)],
        out_specs=[row(1024), row(1024), row(1024), row(256), row(128)],
        out_shape=[jax.ShapeDtypeStruct((T, 1024), BF16), jax.ShapeDtypeStruct((T, 1024), BF16),
                   jax.ShapeDtypeStruct((T, 1024), BF16), jax.ShapeDtypeStruct((T, 256), BF16),
                   jax.ShapeDtypeStruct((T, 128), BF16)],
        compiler_params=_cp(("parallel",)))(proj, qnw, kvnw, wuq, wk, wv, cos, ss)


def _mla_prep_bwd(dq, dk, dv, proj, qnw, kvnw, wuq_t, wk_t, wv_t, cos, ss, ret_grads, *, name):
    T = proj.shape[0]
    tm = min(T, 512)

    def body(dq_ref, dk_ref, dv_ref, lat_ref, qnw_ref, kvnw_ref, wuq_ref, wk_ref, wv_ref, cos_ref, ss_ref,
             rq_ref, rk_ref, rv_ref, rg_ref, dproj_ref, dqp_ref, dqnw_ref, dkvnw_ref):
        for j, r in enumerate((rq_ref, rk_ref, rv_ref, rg_ref)):
            dproj_ref[:, 512 * j:512 * j + 512] = r[...]
        dlat_ref = dproj_ref.at[:, 2048:2560]

        @pl.when(pl.program_id(0) == 0)
        def _():
            dqnw_ref[...] = jnp.zeros_like(dqnw_ref)
            dkvnw_ref[...] = jnp.zeros_like(dkvnw_ref)
        cs, sn = cos_ref[...], ss_ref[...]
        dkpe = jnp.zeros((tm, 128), F32)
        for h in range(MLA_HEADS):
            hs = slice(128 * h, 128 * h + 128)
            dqp_ref[:, hs] = _rope_t(dq_ref[:, hs] * SCALE, cs, sn, 16, 32).astype(BF16)
            dkpe = dkpe + dk_ref[:, hs]
        lane = lax.broadcasted_iota(jnp.int32, (tm, 128), 1)
        rope_lane = (lane >= MLA_NOPE) & (lane < MLA_NOPE + MLA_ROPE)
        dg3 = jnp.where(rope_lane, _rope_t(jnp.where(rope_lane, dkpe, 0.0), cs, sn, 16, 32), 0.0)

        def norm_bwd(x, w, dn):
            r = lax.rsqrt(jnp.mean(x * x, axis=-1, keepdims=True) + EPS)
            xh = x * r
            g = dn * w
            return r * (g - xh * jnp.mean(g * xh, axis=-1, keepdims=True)), jnp.sum(dn * xh, axis=0, keepdims=True)

        dcqn = _dot(dqp_ref[...], wuq_ref[...])
        dcq, dqnw = norm_bwd(lat_ref[:, 0:256], qnw_ref[...], dcqn)
        dckvn = _dot(dk_ref[...].astype(BF16), wk_ref[...]) + _dot(dv_ref[...], wv_ref[...])
        dckv, dkvnw = norm_bwd(lat_ref[:, 256:384], kvnw_ref[...], dckvn)
        dqnw_ref[...] += dqnw
        dkvnw_ref[...] += dkvnw
        dlat_ref[:, 0:256] = dcq.astype(BF16)
        dlat_ref[:, 256:384] = dckv.astype(BF16)
        dlat_ref[:, 384:512] = dg3.astype(BF16)

    def full(shape):
        return pl.BlockSpec(shape, lambda i: (0, 0))

    def row(w):
        return pl.BlockSpec((tm, w), lambda i: (i, 0))

    return pl.pallas_call(
        body, name=name, grid=(T // tm,),
        in_specs=[row(1024), row(1024), row(512), pl.BlockSpec((tm, 512), lambda i: (i, 4)), full((1, 256)),
                  full((1, 128)), full((1024, 256)), full((1024, 128)), full((512, 128)), row(128), row(128)]
                 + [row(512)] * 4,
        out_specs=[row(IN_PAD), row(1024), full((1, 256)), full((1, 128))],
        out_shape=[jax.ShapeDtypeStruct((T, IN_PAD), BF16), jax.ShapeDtypeStruct((T, 1024), BF16),
                   jax.ShapeDtypeStruct((1, 256), F32), jax.ShapeDtypeStruct((1, 128), F32)],
        compiler_params=_cp(("arbitrary",)))(dq, dk, dv, proj, qnw, kvnw, wuq_t, wk_t, wv_t, cos, ss, *ret_grads)


def _flash_fwd(q, k, v1, *, name, gather=None):
    T = q.shape[0]
    tq = min(T, 512)
    tk = tq
    nq = T // tq

    def body(q_ref, k_ref, v_ref, *rest):
        if gather is None:
            y_ref, lse_ref = rest
        else:
            x_ref, y_ref, lse_ref, g_ref, *sems = rest
            start, forward, finish = _gather_phases(x_ref, g_ref, *sems)
            pl.when((pl.program_id(0) == 0) & (pl.program_id(1) == 0))(start)
            pl.when((pl.program_id(0) == 1) & (pl.program_id(1) == 0))(forward)
        attend(q_ref, k_ref, v_ref, y_ref, lse_ref)
        if gather is not None:
            pl.when((pl.program_id(0) == 3) & (pl.program_id(1) == nq - 1))(finish)

    def attend(q_ref, k_ref, v_ref, y_ref, lse_ref):
        qi = pl.program_id(1)
        row = lax.broadcasted_iota(jnp.int32, (tq, tk), 0)
        col = lax.broadcasted_iota(jnp.int32, (tq, tk), 1)

        def step(kb, carry, masked):
            ks = pl.ds(pl.multiple_of(kb * tk, tk), tk)
            new = []
            for h in range(2):
                hs = slice(128 * h, 128 * h + 128)
                m, acc = carry[h]
                s = _dot_nt(q_ref[:, hs], k_ref[ks, hs])
                if masked:
                    s = jnp.where(col <= row, s, NEG)
                mn = jnp.maximum(m, jnp.max(s, axis=1, keepdims=True))
                p = jnp.exp((s - mn).astype(BF16))
                acc = jnp.exp(m - mn) * acc + _dot(p, v_ref[ks, hs])
                new.append((mn, acc))
            return tuple(new)

        def unrolled(j, c):
            for u in range(FLASH_UNROLL):
                c = step(FLASH_UNROLL * j + u, c, False)
            return c

        init = (jnp.full((tq, 1), NEG, F32), jnp.zeros((tq, 128), F32))
        carry = lax.fori_loop(0, qi // FLASH_UNROLL, unrolled, (init, init))
        carry = lax.fori_loop(FLASH_UNROLL * (qi // FLASH_UNROLL), qi, lambda kb, c: step(kb, c, False), carry)
        (ma, acca), (mb, accb) = step(qi, carry, True)
        lane = lax.broadcasted_iota(jnp.int32, (tq, 128), 1)
        la, lb = pltpu.roll(acca, 64, 1), pltpu.roll(accb, 64, 1)
        y_ref[...] = jnp.where(lane < 64, acca / la, accb / lb).astype(BF16)
        lse_ref[0, 0] = jnp.broadcast_to(ma + jnp.log(acca[:, 64:65]), (tq, 128)).T[0:1]
        lse_ref[1, 0] = jnp.broadcast_to(mb + jnp.log(accb[:, 0:1]), (tq, 128)).T[0:1]

    in_specs = [pl.BlockSpec((tq, 256), lambda p, i: (i, p)), pl.BlockSpec((T, 256), lambda p, i: (0, p)),
                pl.BlockSpec((T, 256), lambda p, i: (0, p))]
    out_specs = [pl.BlockSpec((tq, 128), lambda p, i: (i, p)), pl.BlockSpec((2, 1, 1, tq), lambda p, i: (p, i, 0, 0))]
    out_shape = [jax.ShapeDtypeStruct((T, MLA_WIDTH), BF16), jax.ShapeDtypeStruct((MLA_HEADS, nq, 1, tq), F32)]
    if gather is None:
        return pl.pallas_call(body, name=name, grid=(4, nq), in_specs=in_specs, out_specs=out_specs,
                              out_shape=out_shape, compiler_params=_cp(("parallel", "arbitrary")))(q, k, v1)
    return pl.pallas_call(
        body, name=name, grid=(4, nq), in_specs=in_specs + [ANY], out_specs=out_specs + [ANY],
        out_shape=out_shape + [jax.ShapeDtypeStruct((N_DEV,) + gather.shape, gather.dtype)],
        scratch_shapes=list(GATHER_SCRATCH),
        compiler_params=_cp(("arbitrary", "arbitrary")))(q, k, v1, gather)


def _flash_bwd(q, k, v, do, lse, delta, *, name, exchange=None):
    T = q.shape[0]
    tq = min(T, 512)
    tk = tq
    nq = T // tq

    def body(q_ref, k_ref, v_ref, do_ref, lse_ref, dl_ref, *rest):
        if exchange is None:
            backward(q_ref, k_ref, v_ref, do_ref, lse_ref, dl_ref, *rest)
        else:
            p_ref, dqt_ref, dk_ref, dv_ref, got_ref, *sems = rest
            start, finish = _exchange_phases(p_ref, got_ref, *sems)
            pl.when((pl.program_id(0) == 0) & (pl.program_id(1) == 0))(start)
            backward(q_ref, k_ref, v_ref, do_ref, lse_ref, dl_ref, dqt_ref, dk_ref, dv_ref)
            pl.when((pl.program_id(0) == 3) & (pl.program_id(1) == nq - 1))(finish)

    def backward(q_ref, k_ref, v_ref, do_ref, lse_ref, dl_ref, dqt_ref, dk_ref, dv_ref):
        kb = pl.program_id(1)

        @pl.when(kb == 0)
        def _():
            dqt_ref[...] = jnp.zeros_like(dqt_ref)
        krow = lax.broadcasted_iota(jnp.int32, (tk, tq), 0)
        qcol = lax.broadcasted_iota(jnp.int32, (tk, tq), 1)
        masks = _head_masks((tk, 128))
        vms = [(v_ref[:, 128 * h:128 * h + 128].astype(F32) * masks[h]).astype(BF16) for h in range(2)]

        def step(qi, carry, masked):
            qs = pl.ds(pl.multiple_of(qi * tq, tq), tq)
            dob = do_ref[qs, :]
            dof = dob.astype(F32)
            dks, dv_acc = list(carry[:2]), carry[2]
            for h in range(2):
                hs = slice(128 * h, 128 * h + 128)
                kh = k_ref[:, hs]
                qh = q_ref[qs, hs]
                st = _dot_nt(kh, qh)
                pt = jnp.exp((st - lse_ref[h, qi]).astype(BF16))
                if masked:
                    pt = jnp.where(krow <= qcol, pt, jnp.zeros_like(pt))
                dv_acc = dv_acc + _dot(pt, (dof * masks[h]).astype(BF16))
                dpt = _dot_nt(vms[h], dob)
                dst = pt * (dpt - dl_ref[h, qi]).astype(BF16)
                dks[h] = dks[h] + _dot(dst, qh)
                dqt_ref[qi, hs, :] += _dot_tn(kh, dst)
            return dks[0], dks[1], dv_acc

        zero = jnp.zeros((tk, 128), F32)
        carry = step(kb, (zero, zero, zero), True)

        def two_steps(j, c):
            qi = kb + 1 + 2 * j
            return step(qi + 1, step(qi, c, False), False)

        pairs = (nq - 1 - kb) // 2
        carry = lax.fori_loop(0, pairs, two_steps, carry)
        dk0, dk1, dv_acc = lax.fori_loop(kb + 1 + 2 * pairs, nq, lambda qi, c: step(qi, c, False), carry)
        dk_ref[:, 0:128] = dk0
        dk_ref[:, 128:256] = dk1
        dv_ref[...] = dv_acc.astype(BF16)

    stat = pl.BlockSpec((2, nq, 1, tq), lambda p, j: (p, 0, 0, 0))
    in_specs = [pl.BlockSpec((T, 256), lambda p, j: (0, p)), pl.BlockSpec((tk, 256), lambda p, j: (j, p)),
                pl.BlockSpec((tk, 256), lambda p, j: (j, p)), pl.BlockSpec((T, 128), lambda p, j: (0, p)), stat, stat]
    out_specs = [pl.BlockSpec((None, nq, 256, tq), lambda p, j: (p, 0, 0, 0)),
                 pl.BlockSpec((tk, 256), lambda p, j: (j, p)), pl.BlockSpec((tk, 128), lambda p, j: (j, p))]
    out_shape = [jax.ShapeDtypeStruct((4, nq, 256, tq), F32), jax.ShapeDtypeStruct((T, 1024), F32),
                 jax.ShapeDtypeStruct((T, MLA_WIDTH), BF16)]
    if exchange is None:
        return pl.pallas_call(body, name=name, grid=(4, nq), in_specs=in_specs, out_specs=out_specs,
                              out_shape=out_shape,
                              compiler_params=_cp(("parallel", "arbitrary")))(q, k, v, do, lse, delta)
    return pl.pallas_call(
        body, name=name, grid=(4, nq), in_specs=in_specs + [ANY], out_specs=out_specs + [ANY],
        out_shape=out_shape + [jax.ShapeDtypeStruct(exchange.shape, exchange.dtype)],
        scratch_shapes=list(EXCHANGE_SCRATCH),
        compiler_params=_cp(("arbitrary", "arbitrary")))(q, k, v, do, lse, delta, exchange)


def _shift_down(x, n, prev8):
    r = pltpu.roll(x, n, 0)
    row = lax.broadcasted_iota(jnp.int32, prev8.shape, 0)
    first = jnp.where(row < n, pltpu.roll(prev8, n, 0), r[:8])
    if x.shape[0] == 8:
        return first
    return jnp.concatenate([first, r[8:]], axis=0)


def _shift_up(x, n, next8):
    tm = x.shape[0]
    r = pltpu.roll(x, tm - n, 0)
    row = lax.broadcasted_iota(jnp.int32, next8.shape, 0)
    last = jnp.where(row >= 8 - n, pltpu.roll(next8, 8 - n, 0), r[tm - 8:])
    return jnp.concatenate([r[:tm - 8], last], axis=0)


def _conv_pre(u, prev8, cw_ref, cb_ref):
    p1 = _shift_down(u, 1, prev8)
    p2 = _shift_down(u, 2, prev8)
    up = cb_ref[...] + cw_ref[0:1, :] * p2 + cw_ref[1:2, :] * p1 + cw_ref[2:3, :] * u
    return up, p1, p2


def _up_proj_conv(x1, nw, w_up_t, cw, cb, *, name):
    T, K = x1.shape
    tm = min(T, 256)

    def body(x_ref, nw_ref, w_ref, cw_ref, cb_ref, h_ref, u_ref, a_ref, carry_sc):
        @pl.when(pl.program_id(0) == 0)
        def _():
            carry_sc[...] = jnp.zeros_like(carry_sc)
        xv = x_ref[...]
        h = (xv * lax.rsqrt(jnp.mean(xv * xv, axis=-1, keepdims=True) + EPS) * nw_ref[...]).astype(BF16)
        h_ref[...] = h
        for blk in range(2):
            ups = []
            for half in range(2):
                cs = slice((2 * blk + half) * FF_HALF, (2 * blk + half + 1) * FF_HALF)
                u = _dot_nt(h, w_ref[cs, :])
                u_ref[:, cs] = u
                prev = carry_sc[:, cs]
                ups.append(cb_ref[:, cs] + cw_ref[0:1, cs] * _shift_down(u, 2, prev)
                           + cw_ref[1:2, cs] * _shift_down(u, 1, prev) + cw_ref[2:3, cs] * u)
                carry_sc[:, cs] = u[tm - 8:]
            gate, val = ups
            a_ref[:, blk * FF_HALF:(blk + 1) * FF_HALF] = (gate * _sigmoid(gate) * val).astype(BF16)

    def full(shape):
        return pl.BlockSpec(shape, lambda i: (0, 0))

    return pl.pallas_call(
        body, name=name, grid=(T // tm,),
        in_specs=[pl.BlockSpec((tm, K), lambda i: (i, 0)), full(nw.shape), full(w_up_t.shape), full(cw.shape),
                  full(cb.shape)],
        out_specs=[pl.BlockSpec((tm, K), lambda i: (i, 0)), pl.BlockSpec((tm, 2 * D_FF), lambda i: (i, 0)),
                   pl.BlockSpec((tm, D_FF), lambda i: (i, 0))],
        out_shape=[jax.ShapeDtypeStruct((T, K), BF16), jax.ShapeDtypeStruct((T, 2 * D_FF), F32),
                   jax.ShapeDtypeStruct((T, D_FF), BF16)],
        scratch_shapes=[pltpu.VMEM((8, 2 * D_FF), F32)],
        compiler_params=_cp(("arbitrary",)))(x1, nw, w_up_t, cw, cb)


def _conv_bwd(u, da, cw, cb, *, name):
    T = u.shape[0]
    tm = min(T, 512)
    W = 2 * FF_HALF
    nt = T // tm

    def body(u_ref, prev_ref, next_ref, da_ref, dan_ref, cw_ref, cb_ref, du_ref, dw0_ref, dw1_ref, dw2_ref, db_ref):
        i = pl.program_id(1)

        @pl.when(i == 0)
        def _():
            for r in (dw0_ref, dw1_ref, dw2_ref, db_ref):
                r[...] = jnp.zeros_like(r)

        def dpre(u, prev8, da):
            up, p1, p2 = _conv_pre(u, prev8, cw_ref, cb_ref)
            gate, val = up[:, :FF_HALF], up[:, FF_HALF:]
            sg = _sigmoid(gate)
            dgate = da * val * (sg * (1.0 + gate * (1.0 - sg)))
            dval = da * (gate * sg)
            return jnp.concatenate([dgate, dval], axis=1), p1, p2

        u = u_ref[...]
        prev = jnp.where(i > 0, prev_ref[...], 0.0)
        dup, p1, p2 = dpre(u, prev, da_ref[...])
        dupn, _, _ = dpre(next_ref[...], u[tm - 8:], dan_ref[...])
        dupn = jnp.where(i < nt - 1, dupn, 0.0)
        du = cw_ref[2:3, :] * dup + cw_ref[1:2, :] * _shift_up(dup, 1, dupn) + cw_ref[0:1, :] * _shift_up(dup, 2, dupn)
        du_ref[...] = du.astype(BF16)
        dw0_ref[...] += jnp.sum(dup * p2, axis=0, keepdims=True)
        dw1_ref[...] += jnp.sum(dup * p1, axis=0, keepdims=True)
        dw2_ref[...] += jnp.sum(dup * u, axis=0, keepdims=True)
        db_ref[...] += jnp.sum(dup, axis=0, keepdims=True)

    nxt = lambda j, i: (jnp.minimum((i + 1) * (tm // 8), T // 8 - 1), j)
    vec = pl.BlockSpec((1, W), lambda j, i: (0, j))
    return pl.pallas_call(
        body, name=name, grid=(2, nt),
        in_specs=[pl.BlockSpec((tm, W), lambda j, i: (i, j)),
                  pl.BlockSpec((8, W), lambda j, i: (jnp.maximum(i * (tm // 8) - 1, 0), j)),
                  pl.BlockSpec((8, W), nxt),
                  pl.BlockSpec((tm, FF_HALF), lambda j, i: (i, j)), pl.BlockSpec((8, FF_HALF), nxt),
                  pl.BlockSpec((3, W), lambda j, i: (0, j)), vec],
        out_specs=[pl.BlockSpec((tm, W), lambda j, i: (i, j)), vec, vec, vec, vec],
        out_shape=[jax.ShapeDtypeStruct((T, 2 * D_FF), BF16)] + [jax.ShapeDtypeStruct((1, 2 * D_FF), F32)] * 4,
        compiler_params=_cp(("parallel", "arbitrary")))(u, u, u, da, da, cw, cb)


def _adamw(w, m, v, g_slots, *, name):
    R, C = w.shape
    ns = g_slots.shape[0]
    tr = _row_tile(R)

    def body(w_ref, m_ref, v_ref, g_ref, go_ref, d_ref, mo_ref, vo_ref):
        g = g_ref[0].astype(F32)
        for s in range(1, ns):
            g = g + g_ref[s].astype(F32)
        mn = ADAM_B1 * m_ref[...] + (1.0 - ADAM_B1) * g
        vn = ADAM_B2 * v_ref[...] + (1.0 - ADAM_B2) * (g * g)
        m_hat = mn / (1.0 - ADAM_B1 ** ADAM_STEP)
        v_hat = vn / (1.0 - ADAM_B2 ** ADAM_STEP)
        go_ref[...] = g
        d_ref[...] = -ADAM_LR * (m_hat / (jnp.sqrt(v_hat) + ADAM_EPS) + ADAM_WD * w_ref[...])
        mo_ref[...] = mn
        vo_ref[...] = vn

    blk = pl.BlockSpec((tr, C), lambda i: (i, 0))
    return pl.pallas_call(
        body, name=name, grid=(R // tr,),
        in_specs=[blk, blk, blk, pl.BlockSpec((ns, tr, C), lambda i: (0, i, 0))],
        out_specs=[blk] * 4, out_shape=[jax.ShapeDtypeStruct((R, C), F32)] * 4,
        compiler_params=_cp(("parallel",)))(w, m, v, g_slots)


def _place():
    return lax.axis_index("x"), lax.axis_index("y"), lax.axis_index("c")


GATHER_SCRATCH = (pltpu.SemaphoreType.DMA((7,)), pltpu.SemaphoreType.DMA((7,)), pltpu.SemaphoreType.DMA)
EXCHANGE_SCRATCH = (pltpu.SemaphoreType.DMA((3,)), pltpu.SemaphoreType.DMA((3,)), pltpu.SemaphoreType.DMA)


def _gather_phases(x_ref, out_ref, send_sems, recv_sems, local_sem):
    x_, y_, c_ = _place()
    me, sibling = (x_, y_, c_), (x_, y_, 1 - c_)
    chips = [(1 - x_, y_), (x_, 1 - y_), (1 - x_, 1 - y_)]

    def slot(px, py, pc):
        return out_ref.at[4 * px + 2 * py + pc]

    def copy(k, block, to, src=None):
        return pltpu.make_async_remote_copy(
            src_ref=slot(*block) if src is None else src, dst_ref=slot(*block),
            send_sem=send_sems.at[k], recv_sem=recv_sems.at[k], device_id=to, device_id_type=MESH)

    def mine():
        return pltpu.make_async_copy(x_ref, slot(*me), local_sem)

    def first():
        return [copy(0, me, sibling, src=x_ref)] + [copy(1 + j, me, (*chip, c_), src=x_ref)
                                                     for j, chip in enumerate(chips)]

    def passed():
        return [copy(4 + j, (*chip, c_), sibling) for j, chip in enumerate(chips)]

    def start():
        mine().start()
        for cp in first():
            cp.start()

    def forward():
        fwd = passed()
        for j, chip in enumerate(chips):
            copy(1 + j, (*chip, c_), me).wait_recv()
            fwd[j].start()

    def finish():
        copy(0, sibling, me).wait_recv()
        for j, chip in enumerate(chips):
            copy(4 + j, (*chip, 1 - c_), me).wait_recv()
        for cp in first() + passed():
            cp.wait_send()
        mine().wait()

    return start, forward, finish


def _exchange_phases(p_ref, out_ref, send_sems, recv_sems, local_sem):
    x_, y_, c_ = _place()
    me_k = 2 * x_ + y_
    chips = [(1 - x_, y_), (x_, 1 - y_), (1 - x_, 1 - y_)]

    def local():
        return pltpu.make_async_copy(p_ref.at[me_k], out_ref.at[me_k], local_sem)

    def copy(j, src_k, dst_k, chip):
        return pltpu.make_async_remote_copy(
            src_ref=p_ref.at[src_k], dst_ref=out_ref.at[dst_k], send_sem=send_sems.at[j],
            recv_sem=recv_sems.at[j], device_id=(*chip, c_), device_id_type=MESH)

    def sends():
        return [copy(j, 2 * px + py, me_k, (px, py)) for j, (px, py) in enumerate(chips)]

    def start():
        local().start()
        for cp in sends():
            cp.start()

    def finish():
        for j, (px, py) in enumerate(chips):
            copy(j, me_k, 2 * px + py, (px, py)).wait_recv()
        for cp in sends():
            cp.wait_send()
        local().wait()

    return start, finish


def _all_gather(x, *, name, in_vmem):
    def body(x_ref, out_ref, send_sems, recv_sems, local_sem):
        for phase in _gather_phases(x_ref, out_ref, send_sems, recv_sems, local_sem):
            phase()

    spec = pl.BlockSpec(memory_space=pltpu.VMEM) if in_vmem else ANY
    return pl.pallas_call(
        body, name=name, out_shape=jax.ShapeDtypeStruct((N_DEV,) + x.shape, x.dtype),
        in_specs=[spec], out_specs=spec, scratch_shapes=list(GATHER_SCRATCH),
        compiler_params=pltpu.CompilerParams(vmem_limit_bytes=VMEM_LIMIT))(x)


def _small_rows():
    table, row = [], 0
    for n, size in SMALL_VECTORS:
        table.append((n, size, row))
        row += -(-size // PACK_COLS)
    return table


def _ff_chunk_source(c):
    block, off = divmod(c * 128, FF_HALF)
    return (0, 2, 1, 3)[block] * FF_HALF + off


def _pack_small(parts, *, name):
    table = _small_rows()

    def body(*refs):
        out = refs[-1]
        out[...] = jnp.zeros_like(out)
        for ref, (n, size, row) in zip(refs, table):
            if size != 2 * D_FF:
                out[row:row + 1, 0:size] = ref[...]
                continue
            for c in range(size // 128):
                src = _ff_chunk_source(c)
                r, lane = divmod(c * 128, PACK_COLS)
                out[row + r:row + r + 1, lane:lane + 128] = ref[:, src:src + 128]

    return pl.pallas_call(body, name=name, out_shape=jax.ShapeDtypeStruct((SMALL_ROWS, PACK_COLS), F32))(
        *[parts[n] for n, _, _ in table])


def _sum_small(g, *, name):
    table = _small_rows()
    shapes = [(n, size) for n, size, _ in table if not n.startswith("conv_w")]
    shapes.insert(7, ("conv_w", 2 * D_FF))

    def body(g_ref, *outs):
        def total(row, width):
            acc = g_ref[0, row:row + 1, 0:width]
            for d in range(1, N_DEV):
                acc = acc + g_ref[d, row:row + 1, 0:width]
            return acc

        out_of = {n: o for (n, _), o in zip(shapes, outs)}
        for n, size, row in table:
            o, j = (out_of["conv_w"], int(n[-1])) if n.startswith("conv_w") else (out_of[n], 0)
            for i in range(-(-size // PACK_COLS)):
                width = min(PACK_COLS, size - PACK_COLS * i)
                o[j:j + 1, PACK_COLS * i:PACK_COLS * i + width] = total(row + i, width)

    out_shape = [jax.ShapeDtypeStruct((3 if n == "conv_w" else 1, size), F32) for n, size in shapes]
    res = pl.pallas_call(body, name=name, out_shape=out_shape)(g)
    return {n: r for (n, _), r in zip(shapes, res)}


def _adamw_small(ws, ms, vs, gs, *, name):
    k = len(ws)

    def body(*refs):
        w_refs, m_refs, v_refs, g_refs = (refs[i * k:(i + 1) * k] for i in range(4))
        outs = refs[4 * k:]
        for i in range(k):
            g = g_refs[i][...]
            mn = ADAM_B1 * m_refs[i][...] + (1.0 - ADAM_B1) * g
            vn = ADAM_B2 * v_refs[i][...] + (1.0 - ADAM_B2) * (g * g)
            m_hat = mn / (1.0 - ADAM_B1 ** ADAM_STEP)
            v_hat = vn / (1.0 - ADAM_B2 ** ADAM_STEP)
            outs[i][...] = g
            outs[k + i][...] = -ADAM_LR * (m_hat / (jnp.sqrt(v_hat) + ADAM_EPS) + ADAM_WD * w_refs[i][...])
            outs[2 * k + i][...] = mn
            outs[3 * k + i][...] = vn

    out_shape = [jax.ShapeDtypeStruct(w.shape, F32) for _ in range(4) for w in ws]
    res = pl.pallas_call(body, name=name, out_shape=out_shape)(*ws, *ms, *vs, *gs)
    return [res[i * k:(i + 1) * k] for i in range(4)]


SWAP_SCRATCH = (pltpu.SemaphoreType.DMA((4,)), pltpu.SemaphoreType.DMA((4,)))


def _swap_phases(g_ref, out_ref, send_sems, recv_sems):
    x_, y_, c_ = _place()

    def copies():
        return [pltpu.make_async_remote_copy(src_ref=g_ref.at[k, 1 - c_], dst_ref=out_ref.at[k],
                                             send_sem=send_sems.at[k], recv_sem=recv_sems.at[k],
                                             device_id=(x_, y_, 1 - c_), device_id_type=MESH) for k in range(4)]

    def start():
        for cp in copies():
            cp.start()

    def finish():
        for cp in copies():
            cp.wait()

    return start, finish


def _swap_sibling(g, *, name):
    def body(g_ref, out_ref, send_sems, recv_sems):
        for phase in _swap_phases(g_ref, out_ref, send_sems, recv_sems):
            phase()

    return pl.pallas_call(
        body, name=name, out_shape=jax.ShapeDtypeStruct((4,) + g.shape[2:], g.dtype), in_specs=[ANY], out_specs=ANY,
        scratch_shapes=list(SWAP_SCRATCH))(g)


def _row_tile(R):
    for cand in (256, 400, 200):
        if R % cand == 0:
            return cand
    return R


def _add_own(g, b, *, name, out_dtype):
    n, _, R, C = g.shape
    tr = _row_tile(R)

    def body(c_ref, g_ref, b_ref, o_ref):
        del c_ref
        o_ref[...] = (g_ref[...] + b_ref[...]).astype(out_dtype)

    blk = pl.BlockSpec((None, tr, C), lambda s, i, c: (s, i, 0))
    grid_spec = pltpu.PrefetchScalarGridSpec(
        num_scalar_prefetch=1, grid=(n, R // tr),
        in_specs=[pl.BlockSpec((None, None, tr, C), lambda s, i, c: (s, c[0], i, 0)), blk], out_specs=blk)
    core = jnp.reshape(lax.axis_index("c"), (1,)).astype(jnp.int32)
    return pl.pallas_call(body, name=name, grid_spec=grid_spec, out_shape=jax.ShapeDtypeStruct(b.shape, out_dtype),
                          compiler_params=_cp(("parallel", "parallel")))(core, g, b)


def _pack_local(parts, group):
    table, rows = group
    segs = []
    for n, r, rp, tr in table:
        w = parts[n].T if tr else parts[n]
        segs.append(jnp.pad(w.reshape(r, PACK_COLS), ((0, rp - r), (0, 0))))
    segs.append(jnp.zeros((rows - sum(rp for _, _, rp, _ in table), PACK_COLS), segs[0].dtype))
    return jnp.concatenate(segs, axis=0)


def _unpack_local(packed, like, group):
    out, off = {}, 0
    for n, r, rp, tr in group[0]:
        rows, cols = like[n].shape
        seg = packed[off:off + r]
        out[n] = (seg.reshape(cols, rows).T if tr else seg)[None]
        off += rp
    return out


def _segments(g, group):
    out, off = {}, 0
    for n, r, rp, _ in group[0]:
        out[n] = g[:, off:off + r]
        off += rp
    return out


def _pack_grads(parts, group):
    table, rows = group
    segs = [jnp.pad(parts[n], ((0, 0), (0, rp - parts[n].shape[1]), (0, 0))) for n, _, rp, _ in table]
    segs.append(jnp.zeros((N_DEV, rows - sum(rp for _, _, rp, _ in table), PACK_COLS), F32))
    return jnp.concatenate(segs, axis=1)


def _owner_rows_early(g):
    g_in = jnp.concatenate([g["w_in_t"][:2432], g["w_in_t"][2496:2528]], axis=0).reshape(N_DEV, 308, PACK_COLS)
    g_uq = g["w_uq_t"].reshape(N_DEV, 128, MLA_Q_RANK)[:, :96].reshape(N_DEV, 24, PACK_COLS)
    g_ukv = jnp.concatenate([g["w_k_t"].reshape(N_DEV, 128, MLA_KV_RANK)[:, :64],
                             g["w_v_t"].reshape(N_DEV, 64, MLA_KV_RANK)], axis=1).reshape(N_DEV, 16, PACK_COLS)
    return dict(w_in=g_in, w_uq=g_uq, w_ukv=g_ukv)


def _owner_rows_late(g):
    g_up = g["w_up_t"].reshape(2, 2, 2, 704, PACK_COLS).swapaxes(0, 1).reshape(N_DEV, 704, PACK_COLS)
    return dict(w_out=g["w_out"].reshape(N_DEV, 128, PACK_COLS), w_up=g_up,
                w_down=g["w_down"].reshape(N_DEV, 352, PACK_COLS))


def _reduce_to_pairs(gp, *, name):
    gp = gp.reshape(4, 2, gp.shape[1], PACK_COLS)
    return _add_own(gp, _swap_sibling(gp, name=name + "_swap"), out_dtype=BF16, name=name + "_sum")


def _interleave_ff(w):
    g, v = w[..., :D_FF], w[..., D_FF:]
    return jnp.concatenate([g[..., :FF_HALF], v[..., :FF_HALF], g[..., FF_HALF:], v[..., FF_HALF:]], axis=-1)


def _rope_tables(pos):
    p = pos.astype(F32)[:, None]
    inv_r = ROPE_BASE ** (-jnp.arange(0, RET_HEAD_DIM, 2, dtype=F32) / RET_HEAD_DIM)
    ang = p * jnp.tile(inv_r, 4)
    sign_r = jnp.tile(jnp.concatenate([-jnp.ones((32,), F32), jnp.ones((32,), F32)]), 2)
    cos_r, ss_r = jnp.cos(ang), jnp.sin(ang) * sign_r
    inv_m = ROPE_BASE ** (-jnp.arange(0, MLA_ROPE, 2, dtype=F32) / MLA_ROPE)
    ang = p * jnp.concatenate([jnp.zeros((64,), F32), inv_m, inv_m, jnp.zeros((32,), F32)])
    sign_m = jnp.concatenate([jnp.zeros((64,), F32), -jnp.ones((16,), F32), jnp.ones((16,), F32), jnp.zeros((32,), F32)])
    cos_m, ss_m = jnp.cos(ang), jnp.sin(ang) * sign_m
    return cos_r, ss_r, cos_m, ss_m


def _prep_early(gathered):
    seg = _segments(gathered, EARLY)
    w_in_t = seg["w_in"].reshape(IN_WIDTH, D_MODEL)
    z = lambda n: jnp.zeros((n, D_MODEL), BF16)
    w_in_t = jnp.concatenate([w_in_t[:2432], z(64), w_in_t[2432:2464], z(32)], axis=0)
    w_uq_t = jnp.pad(seg["w_uq"].reshape(MLA_HEADS, 96, MLA_Q_RANK), ((0, 0), (0, 32), (0, 0))).reshape(1024, MLA_Q_RANK)
    ukv = seg["w_ukv"].reshape(MLA_HEADS, 128, MLA_KV_RANK)
    w_k_t = jnp.pad(ukv[:, :64], ((0, 0), (0, 64), (0, 0))).reshape(1024, MLA_KV_RANK)
    w_v_t = ukv[:, 64:].reshape(512, MLA_KV_RANK)
    return dict(w_in_t=w_in_t, w_uq_t=w_uq_t, w_k_t=w_k_t, w_v_t=w_v_t)


def _prep_late(gathered):
    seg = _segments(gathered, LATE)
    w_up_t = seg["w_up"].reshape(2, 2, 2, 704, D_MODEL).swapaxes(0, 1).reshape(2 * D_FF, D_MODEL)
    return dict(w_out=seg["w_out"].reshape(1024, D_MODEL), w_up_t=w_up_t, w_down=seg["w_down"].reshape(D_FF, D_MODEL))


def _local_step(x, pos, tgt, early, sm, late):
    dist = not isinstance(late, dict)
    cos_r, ss_r, cos_m, ss_m = _rope_tables(pos)
    tabs = _ret_tables()

    if dist:
        h, gathered = _rmsnorm_fwd(x, sm["attn_norm_w"], gather=early, name="attn_norm")
        W = _prep_early(gathered)
    else:
        h = _rmsnorm_fwd(x, sm["attn_norm_w"], name="attn_norm")
        W = early
    proj = _mm(h, W["w_in_t"], bt=True, name="in_proj")
    y_ret, o_ret = _ret_fwd(proj, cos_r, ss_r, tabs, sm["ret_gn_w"], name="ret_fwd")
    q, k, v1, cqn, ckvn = _mla_prep_fwd(proj, sm["mla_q_norm_w"], sm["mla_kv_norm_w"], W["w_uq_t"], W["w_k_t"],
                                       W["w_v_t"], cos_m, ss_m, name="mla_prep")
    T = x.shape[0]
    tq = min(T, 512)
    if dist:
        y_mla, lse, gathered = _flash_fwd(q, k, v1, gather=late, name="mla_attn")
        W = {**W, **_prep_late(gathered)}
    else:
        y_mla, lse = _flash_fwd(q, k, v1, name="mla_attn")
        W = {**W, **late}
    mixed = (y_ret, y_mla)
    x1 = _mm(mixed, W["w_out"], add=x, name="out_proj")
    h2, u, a = _up_proj_conv(x1, sm["ffn_norm_w"], W["w_up_t"], sm["conv_w"], sm["conv_b"], name="ffn_norm_up_conv")
    loss, dx2, dx2b, d_final = _down_proj_loss(a, W["w_down"], x1, tgt, sm["final_norm_w"], name="down_proj_loss")

    g = {}
    g["w_down"] = _mm_tn(a, dx2b, name="dw_down")
    da = _mm(dx2b, W["w_down"], bt=True, name="d_act")
    du, dcw0, dcw1, dcw2, dcb = _conv_bwd(u, da, sm["conv_w"], sm["conv_b"], name="conv_bwd")
    g["w_up_t"] = _mm_tn(du, h2, name="dw_up")
    dx1, d_ffn = _mm_norm_bwd(du, W["w_up_t"], x1, sm["ffn_norm_w"], dx2, name="d_h2_ffn_norm_bwd")

    g["w_out"] = _mm_tn(mixed, dx1, name="dw_out")
    dmixed = _mm(dx1, W["w_out"], bt=True, name="d_mixed")
    do_ret, dg, do_mla, delta, d_gn = _mix_bwd(dmixed, o_ret, proj, y_mla, sm["ret_gn_w"], name="mix_bwd")
    drq = _ret_bwd_dq(proj, do_ret, cos_r, ss_r, tabs, name="ret_bwd_dq")
    delta_r = delta.reshape(MLA_HEADS, T // tq, 1, tq)
    if dist:
        gl = _pack_grads(_owner_rows_late(g), LATE).reshape(4, 2, LATE[1], PACK_COLS)
        drk, drv, theirs = _ret_bwd_dkv(proj, do_ret, cos_r, ss_r, tabs, swap=gl, name="ret_bwd_dkv")
        pair = _add_own(gl, theirs, out_dtype=BF16, name="grad_late_sum")
        dqt, dk, dv, slots_late = _flash_bwd(q, k, v1, do_mla, lse, delta_r, exchange=pair, name="mla_attn_bwd")
    else:
        drk, drv = _ret_bwd_dkv(proj, do_ret, cos_r, ss_r, tabs, name="ret_bwd_dkv")
        dqt, dk, dv = _flash_bwd(q, k, v1, do_mla, lse, delta_r, name="mla_attn_bwd")
        slots_late = None
    dq = dqt.transpose(1, 3, 0, 2).reshape(T, MLA_HEADS * 128)
    dproj, dqp, d_qn, d_kvn = _mla_prep_bwd(dq, dk, dv, proj, sm["mla_q_norm_w"], sm["mla_kv_norm_w"], W["w_uq_t"],
                                            W["w_k_t"], W["w_v_t"], cos_m, ss_m, (drq, drk, drv, dg),
                                            name="mla_prep_bwd")
    g["w_uq_t"] = _mm_tn(dqp, cqn, name="dw_uq")
    g["w_k_t"] = _mm_tn(dk, ckvn, name="dw_ukv_k")
    g["w_v_t"] = _mm_tn(dv, ckvn, name="dw_ukv_v")
    g["w_in_t"] = _mm_tn(dproj, h, name="dw_in")
    if dist:
        pair = _reduce_to_pairs(_pack_grads(_owner_rows_early(g), EARLY), name="grad_early")
        grad_x, d_attn, slots_early = _mm_norm_bwd(dproj, W["w_in_t"], x, sm["attn_norm_w"], dx1, exchange=pair,
                                                   name="d_h_attn_norm_bwd")
    else:
        grad_x, d_attn = _mm_norm_bwd(dproj, W["w_in_t"], x, sm["attn_norm_w"], dx1, name="d_h_attn_norm_bwd")
        slots_early = None

    small = dict(attn_norm_w=d_attn, ret_gn_w=d_gn, mla_q_norm_w=d_qn, mla_kv_norm_w=d_kvn, ffn_norm_w=d_ffn,
                 conv_b=dcb, final_norm_w=d_final, conv_w0=dcw0, conv_w1=dcw1, conv_w2=dcw2, loss=loss)
    return loss, grad_x, g, small, slots_early, slots_late


def kernel(x, positions, attn_norm_w, w_in, ret_gn_w, mla_q_norm_w, w_uq, mla_kv_norm_w, w_ukv, w_out, ffn_norm_w, w_up, conv_w, conv_b, w_down, final_norm_w, loss_target, m_attn_norm_w, m_w_in, m_ret_gn_w, m_mla_q_norm_w, m_w_uq, m_mla_kv_norm_w, m_w_ukv, m_w_out, m_ffn_norm_w, m_w_up, m_conv_w, m_conv_b, m_w_down, m_final_norm_w, v_attn_norm_w, v_w_in, v_ret_gn_w, v_mla_q_norm_w, v_w_uq, v_mla_kv_norm_w, v_w_ukv, v_w_out, v_ffn_norm_w, v_w_up, v_conv_w, v_conv_b, v_w_down, v_final_norm_w):
    a = dict(locals())
    x_, y_, c_ = _place()
    dev = 4 * x_ + 2 * y_ + c_

    shard = {n: a[n][0] for n in BIG_NAMES}
    shard16 = {n: w.astype(BF16) for n, w in shard.items()}
    cw_pad = jnp.pad(conv_w[0].reshape(-1), (0, 24 * 128 - 3 * 704)).reshape(24, 128)
    cw_all = _all_gather(cw_pad, name="gather_conv_w", in_vmem=True)
    conv_w_full = cw_all.reshape(N_DEV, -1)[:, :3 * 704].reshape(N_DEV, 3, 704).transpose(1, 0, 2).reshape(3, 2 * D_FF)
    sm = dict(attn_norm_w=attn_norm_w, ret_gn_w=ret_gn_w, mla_q_norm_w=mla_q_norm_w, mla_kv_norm_w=mla_kv_norm_w,
              ffn_norm_w=ffn_norm_w, final_norm_w=final_norm_w.reshape(1, D_MODEL),
              conv_w=_interleave_ff(conv_w_full), conv_b=_interleave_ff(conv_b))

    loss, grad_x, _, gs, slots_early, slots_late = _local_step(
        x[0], positions[0], loss_target[0], _pack_local(shard16, EARLY), sm, _pack_local(shard16, LATE))

    big = [{}, {}, {}, {}]
    for group, slots, tag in ((EARLY, slots_early, "early"), (LATE, slots_late, "late")):
        names_g = [n for n, _, _, _ in group[0]]
        res = _adamw(_pack_local({n: shard[n] for n in names_g}, group),
                     _pack_local({n: a["m_" + n][0] for n in names_g}, group),
                     _pack_local({n: a["v_" + n][0] for n in names_g}, group), slots, name="adamw_" + tag)
        for kind in range(4):
            big[kind].update(_unpack_local(res[kind], shard, group))

    packed = _pack_small(gs, name="pack_small_grads")
    tot = _sum_small(_all_gather(packed, name="gather_small_grads", in_vmem=True), name="sum_small_grads")
    loss_out = tot["loss"][0, 0]
    g_cw = lax.dynamic_slice_in_dim(tot["conv_w"], dev * 704, 704, axis=1)

    def rows_of(prefix):
        return [a[prefix + n].reshape(1, size) for n, size in SMALL]

    sml = _adamw_small(rows_of(""), rows_of("m_"), rows_of("v_"), [tot[n] for n, _ in SMALL], name="adamw_small")
    cwo = _adamw(conv_w[0], m_conv_w[0], v_conv_w[0], g_cw[None], name="adamw_conv_w")

    def small_of(kind, n):
        return sml[kind][[nm for nm, _ in SMALL].index(n)].reshape(a[n].shape)

    names = ['attn_norm_w', 'w_in', 'ret_gn_w', 'mla_q_norm_w', 'w_uq', 'mla_kv_norm_w', 'w_ukv', 'w_out',
             'ffn_norm_w', 'w_up', 'conv_w', 'conv_b', 'w_down', 'final_norm_w']
    outs = [loss_out, grad_x[None]]
    for kind in range(4):
        for n in names:
            if n == "conv_w":
                outs.append(cwo[kind][None])
            elif n in big[kind]:
                outs.append(big[kind][n])
            else:
                outs.append(small_of(kind, n))
    return tuple(outs)
```

```python
import functools

import numpy as np
import jax
import jax.numpy as jnp
from jax import lax
from jax.experimental import pallas as pl
from jax.experimental.pallas import tpu as pltpu

F32 = jnp.float32
BF16 = jnp.bfloat16
MESH = pl.DeviceIdType.MESH
ANY = pl.BlockSpec(memory_space=pl.ANY)

D_MODEL = 1024
RET_HEADS = 8
RET_HEAD_DIM = 64
RET_WIDTH = 512
RET_CHUNK = 128
MLA_HEADS = 8
MLA_NOPE = 64
MLA_ROPE = 32
MLA_V = 64
MLA_Q_RANK = 256
MLA_KV_RANK = 128
MLA_WIDTH = 512
IN_WIDTH = 2464
IN_PAD = 2560
D_FF = 2816
FF_HALF = 1408
ROPE_BASE = 10000.0
EPS = 1e-6
SCALE = float((MLA_NOPE + MLA_ROPE) ** -0.5)
K_SCALE = 0.125
N_DEV = 8

ADAM_LR = 0.001
ADAM_B1 = 0.9
ADAM_B2 = 0.999
ADAM_EPS = 1e-08
ADAM_WD = 0.01
ADAM_STEP = 10

VMEM_LIMIT = 56 * 1024 * 1024
MM_BUDGET = 40 * 1024 * 1024
NEG = -1e30
FLASH_UNROLL = 4

PACK_COLS = 1024
EARLY = ((("w_in", 308, 320, True), ("w_uq", 24, 32, True), ("w_ukv", 16, 16, True)), 384)
LATE = ((("w_out", 128, 128, False), ("w_up", 704, 704, True), ("w_down", 352, 352, False)), 1200)
BIG_NAMES = ("w_in", "w_uq", "w_ukv", "w_out", "w_up", "w_down")
SMALL = (("attn_norm_w", 1024), ("ret_gn_w", 512), ("mla_q_norm_w", 256), ("mla_kv_norm_w", 128),
         ("ffn_norm_w", 1024), ("conv_b", 5632), ("final_norm_w", 1024))
SMALL_VECTORS = SMALL + (("conv_w0", 5632), ("conv_w1", 5632), ("conv_w2", 5632), ("loss", 128))
SMALL_ROWS = 32


def _cp(sem=None, vmem=VMEM_LIMIT):
    return pltpu.CompilerParams(dimension_semantics=sem, vmem_limit_bytes=vmem)


def _dot(a, b):
    return jnp.dot(a, b, preferred_element_type=F32)


def _dot_nt(a, b):
    return lax.dot_general(a, b, (((1,), (1,)), ((), ())), preferred_element_type=F32)


def _dot_tn(a, b):
    return lax.dot_general(a, b, (((0,), (0,)), ((), ())), preferred_element_type=F32)


def _sigmoid(x):
    return 0.5 * jnp.tanh(0.5 * x) + 0.5


def _partner(x, half, period):
    n = x.shape[-1]
    lane = lax.broadcasted_iota(jnp.int32, x.shape, 1)
    return jnp.where((lane % period) < half, pltpu.roll(x, n - half, 1), pltpu.roll(x, half, 1))


def _rope(x, cos, ss, half, period):
    return x * cos + _partner(x, half, period) * ss


def _rope_t(dy, cos, ss, half, period):
    return dy * cos - _partner(dy, half, period) * ss


def _head_masks(shape):
    lane = lax.broadcasted_iota(jnp.int32, shape, 1)
    m0 = (lane < 64).astype(F32)
    return m0, 1.0 - m0


def _mm(a, b, *, name, add=None, out_dtype=F32, bt=False):
    parts = a if isinstance(a, tuple) else (a,)
    M = parts[0].shape[0]
    K = sum(p.shape[1] for p in parts)
    N = b.shape[0] if bt else b.shape[1]
    osz = jnp.dtype(out_dtype).itemsize
    per_row = 2 * (K * parts[0].dtype.itemsize + N * osz + (N * 4 if add is not None else 0))
    tm = 128
    for cand in (512, 256):
        if M % cand == 0 and cand * per_row + 4 * K * N <= MM_BUDGET:
            tm = cand
            break
    tm = min(tm, M)
    mul = _dot_nt if bt else _dot
    n_a = len(parts)
    n_in = n_a + (1 if add is None else 2)

    def body(*refs):
        av = refs[0][...] if n_a == 1 else jnp.concatenate([r[...] for r in refs[:n_a]], axis=1)
        acc = mul(av.astype(BF16), refs[n_a][...])
        if add is not None:
            acc = refs[n_a + 1][...] + acc
        refs[n_in][...] = acc.astype(out_dtype)

    in_specs = [pl.BlockSpec((tm, p.shape[1]), lambda i: (i, 0)) for p in parts]
    in_specs.append(pl.BlockSpec(b.shape, lambda i: (0, 0)))
    args = [*parts, b]
    if add is not None:
        in_specs.append(pl.BlockSpec((tm, N), lambda i: (i, 0)))
        args.append(add)
    return pl.pallas_call(
        body, name=name, grid=(M // tm,), in_specs=in_specs, out_specs=pl.BlockSpec((tm, N), lambda i: (i, 0)),
        out_shape=jax.ShapeDtypeStruct((M, N), out_dtype), compiler_params=_cp(("parallel",)))(*args)


def _mm_tn(a, b, *, name):
    parts = a if isinstance(a, tuple) else (a,)
    T = parts[0].shape[0]
    M = sum(p.shape[1] for p in parts)
    N = b.shape[1]
    tk = min(T, 512)

    def tile(n):
        for cand in (1408, 1280):
            if n > 1408 and n % cand == 0:
                return cand
        return n

    tm, tn = tile(M), tile(N)
    nk = T // tk
    n_a = len(parts)
    assert n_a == 1 or tm == M

    def body(*refs):
        o_ref = refs[n_a + 1]

        @pl.when(pl.program_id(2) == 0)
        def _():
            o_ref[...] = jnp.zeros_like(o_ref)
        av = refs[0][...] if n_a == 1 else jnp.concatenate([r[...] for r in refs[:n_a]], axis=1)
        o_ref[...] += _dot_tn(av.astype(BF16), refs[n_a][...].astype(BF16))

    if n_a == 1:
        a_specs = [pl.BlockSpec((tk, tm), lambda i, j, k: (k, i))]
    else:
        a_specs = [pl.BlockSpec((tk, p.shape[1]), lambda i, j, k: (k, 0)) for p in parts]
    return pl.pallas_call(
        body, name=name, grid=(M // tm, N // tn, nk),
        in_specs=a_specs + [pl.BlockSpec((tk, tn), lambda i, j, k: (k, j))],
        out_specs=pl.BlockSpec((tm, tn), lambda i, j, k: (i, j)),
        out_shape=jax.ShapeDtypeStruct((M, N), F32),
        compiler_params=_cp(("parallel", "parallel", "arbitrary")))(*parts, b)


def _rmsnorm_fwd(x, w, *, name, gather=None):
    T, D = x.shape
    tm = min(T, 1024)
    n = T // tm

    def body(x_ref, w_ref, *rest):
        if gather is not None:
            s_ref, o_ref, g_ref, *sems = rest
            start, forward, finish = _gather_phases(s_ref, g_ref, *sems)
            pl.when(pl.program_id(0) == 0)(start)
            pl.when(pl.program_id(0) == n // 2)(forward)
        else:
            o_ref, = rest
        xv = x_ref[...]
        r = lax.rsqrt(jnp.mean(xv * xv, axis=-1, keepdims=True) + EPS)
        o_ref[...] = (xv * r * w_ref[...]).astype(BF16)
        if gather is not None:
            pl.when(pl.program_id(0) == n - 1)(finish)

    in_specs = [pl.BlockSpec((tm, D), lambda i: (i, 0)), pl.BlockSpec((1, D), lambda i: (0, 0))]
    out_spec = pl.BlockSpec((tm, D), lambda i: (i, 0))
    out_shape = jax.ShapeDtypeStruct((T, D), BF16)
    if gather is None:
        return pl.pallas_call(body, name=name, grid=(n,), in_specs=in_specs, out_specs=out_spec, out_shape=out_shape,
                              compiler_params=_cp(("parallel",)))(x, w)
    return pl.pallas_call(
        body, name=name, grid=(n,), in_specs=in_specs + [ANY], out_specs=[out_spec, ANY],
        out_shape=[out_shape, jax.ShapeDtypeStruct((N_DEV,) + gather.shape, gather.dtype)],
        scratch_shapes=list(GATHER_SCRATCH), compiler_params=_cp(("arbitrary",)))(x, w, gather)


def _mm_norm_bwd(a, b, x, w, dres, *, name, exchange=None):
    T, K = a.shape
    D = b.shape[1]
    tm = min(T, 256 if K > 4096 else 512)
    n = T // tm

    def body(a_ref, b_ref, x_ref, w_ref, dr_ref, *rest):
        if exchange is None:
            dx_ref, dw_ref = rest
        else:
            p_ref, dx_ref, dw_ref, got_ref, *sems = rest
            start, finish = _exchange_phases(p_ref, got_ref, *sems)
            pl.when(pl.program_id(0) == 0)(start)

        @pl.when(pl.program_id(0) == 0)
        def _():
            dw_ref[...] = jnp.zeros_like(dw_ref)
        dh = _dot(a_ref[...], b_ref[...])
        xv = x_ref[...]
        r = lax.rsqrt(jnp.mean(xv * xv, axis=-1, keepdims=True) + EPS)
        xh = xv * r
        g = dh * w_ref[...]
        dx_ref[...] = dr_ref[...] + r * (g - xh * jnp.mean(g * xh, axis=-1, keepdims=True))
        dw_ref[...] += jnp.sum(dh * xh, axis=0, keepdims=True)
        if exchange is not None:
            pl.when(pl.program_id(0) == n - 1)(finish)

    row = pl.BlockSpec((tm, D), lambda i: (i, 0))
    vec = pl.BlockSpec((1, D), lambda i: (0, 0))
    in_specs = [pl.BlockSpec((tm, K), lambda i: (i, 0)), pl.BlockSpec((K, D), lambda i: (0, 0)), row, vec, row]
    out_shape = [jax.ShapeDtypeStruct((T, D), F32), jax.ShapeDtypeStruct((1, D), F32)]
    if exchange is None:
        return pl.pallas_call(body, name=name, grid=(n,), in_specs=in_specs, out_specs=[row, vec], out_shape=out_shape,
                              compiler_params=_cp(("arbitrary",)))(a, b, x, w, dres)
    return pl.pallas_call(
        body, name=name, grid=(n,), in_specs=in_specs + [ANY], out_specs=[row, vec, ANY],
        out_shape=out_shape + [jax.ShapeDtypeStruct(exchange.shape, exchange.dtype)],
        scratch_shapes=list(EXCHANGE_SCRATCH), compiler_params=_cp(("arbitrary",)))(a, b, x, w, dres, exchange)


def _down_proj_loss(a, w_down, x1, tgt, w, *, name):
    T, D = x1.shape
    K = a.shape[1]
    tm = min(T, 512)

    def body(a_ref, b_ref, x_ref, t_ref, w_ref, loss_ref, dx_ref, dxb_ref, dw_ref):
        @pl.when(pl.program_id(0) == 0)
        def _():
            dw_ref[...] = jnp.zeros_like(dw_ref)
            loss_ref[...] = jnp.zeros_like(loss_ref)
        xv = x_ref[...] + _dot(a_ref[...], b_ref[...])
        wv = w_ref[...]
        r = lax.rsqrt(jnp.mean(xv * xv, axis=-1, keepdims=True) + EPS)
        xh = xv * r
        e = xh * wv - t_ref[...]
        part = 0.5 * jnp.sum(jnp.mean(e * e, axis=-1, keepdims=True), axis=0, keepdims=True)
        loss_ref[...] += jnp.broadcast_to(part, loss_ref.shape)
        dy = e * (1.0 / D)
        g = dy * wv
        dx = r * (g - xh * jnp.mean(g * xh, axis=-1, keepdims=True))
        dx_ref[...] = dx
        dxb_ref[...] = dx.astype(BF16)
        dw_ref[...] += jnp.sum(dy * xh, axis=0, keepdims=True)

    row = pl.BlockSpec((tm, D), lambda i: (i, 0))
    vec = pl.BlockSpec((1, D), lambda i: (0, 0))
    return pl.pallas_call(
        body, name=name, grid=(T // tm,),
        in_specs=[pl.BlockSpec((tm, K), lambda i: (i, 0)), pl.BlockSpec((K, D), lambda i: (0, 0)), row, row, vec],
        out_specs=[pl.BlockSpec((1, 128), lambda i: (0, 0)), row, row, vec],
        out_shape=[jax.ShapeDtypeStruct((1, 128), F32), jax.ShapeDtypeStruct((T, D), F32),
                   jax.ShapeDtypeStruct((T, D), BF16), jax.ShapeDtypeStruct((1, D), F32)],
        compiler_params=_cp(("arbitrary",)))(a, w_down, x1, tgt, w)


def _ret_tables():
    C = RET_CHUNK
    h = jnp.arange(RET_HEADS, dtype=F32)
    log_gamma = jnp.log1p(-jnp.power(2.0, -5.0 - h))
    idx = jnp.arange(C, dtype=F32)
    diff = idx[:, None] - idx[None, :]
    dm = jnp.where(diff >= 0, jnp.exp(log_gamma[:, None, None] * jnp.maximum(diff, 0.0)), 0.0)
    dm = dm.reshape(4, 2 * C, C)
    lane_head = jnp.repeat(jnp.arange(RET_HEADS).reshape(4, 2), 64, axis=1)
    lg = log_gamma[lane_head]
    xi = jnp.exp(lg[:, None, :] * (idx[None, :, None] + 1.0))
    zeta = jnp.exp(lg[:, None, :] * (C - 1.0 - idx[None, :, None]))
    blk = (jnp.arange(128)[:, None] // 64) == (jnp.arange(128)[None, :] // 64)
    cd = jnp.where(blk[None], jnp.exp(lg * C)[:, :, None], 0.0)
    return dm.astype(F32), xi.astype(F32), zeta.astype(F32), cd.astype(F32)


def _ret_specs(tb, rev, nt):
    def tmap(t):
        return (nt - 1 - t) if rev else t
    qkv = [pl.BlockSpec((tb, 128), lambda p, t, o=o: (tmap(t), o + p)) for o in (0, 4, 8)]
    rope = [pl.BlockSpec((tb, 128), lambda p, t: (tmap(t), 0))] * 2
    tabs = [pl.BlockSpec((None, 256, 128), lambda p, t: (p, 0, 0))] + \
           [pl.BlockSpec((None, 128, 128), lambda p, t: (p, 0, 0))] * 3
    return qkv, rope, tabs


def _ret_fwd(proj, cos, ss, tabs, gnw, *, name):
    T = proj.shape[0]
    tb = min(T, 1024)
    nt = T // tb
    nchunk = tb // RET_CHUNK

    def body(q_ref, k_ref, v_ref, g_ref, cos_ref, ss_ref, dm_ref, xi_ref, zt_ref, cd_ref, gnw_ref,
             y_ref, o_ref, r_sc):
        @pl.when(pl.program_id(1) == 0)
        def _():
            r_sc[...] = jnp.zeros_like(r_sc)
        m0, m1 = _head_masks((128, 128))
        dm, xi, zt, cd = dm_ref[...], xi_ref[...], zt_ref[...], cd_ref[...]
        bm = (cd > 0).astype(F32)
        gnw = gnw_ref[...]
        for c in range(nchunk):
            rs = pl.ds(c * RET_CHUNK, RET_CHUNK)
            cs, sn = cos_ref[rs, :], ss_ref[rs, :]
            q = _rope(q_ref[rs, :], cs, sn, 32, 64)
            k = _rope(k_ref[rs, :], cs, sn, 32, 64) * K_SCALE
            v = v_ref[rs, :]
            kb, vb = k.astype(BF16), v.astype(BF16)
            qs = jnp.concatenate([q * m0, q * m1], axis=0).astype(BF16)
            s = (_dot_nt(qs, kb) * dm).astype(BF16)
            vs = jnp.concatenate([v * m0, v * m1], axis=0).astype(BF16)
            o = _dot(jnp.concatenate([s[:128], s[128:]], axis=1), vs)
            r = r_sc[...]
            o = o + _dot(q.astype(BF16), r.astype(BF16)) * xi
            r_sc[...] = cd * r + bm * _dot_tn((k * zt).astype(BF16), vb)
            mu = (jnp.sum(o * m0, axis=1, keepdims=True) * m0 + jnp.sum(o * m1, axis=1, keepdims=True) * m1) * (1.0 / 64)
            d = o - mu
            dd = d * d
            var = (jnp.sum(dd * m0, axis=1, keepdims=True) * m0 + jnp.sum(dd * m1, axis=1, keepdims=True) * m1) * (1.0 / 64)
            oh = d * lax.rsqrt(var + EPS)
            g = g_ref[rs, :]
            y_ref[rs, :] = (g * _sigmoid(g) * (oh * gnw)).astype(BF16)
            o_ref[rs, :] = o

    qkv, rope, tspec = _ret_specs(tb, False, nt)
    gspec = pl.BlockSpec((tb, 128), lambda p, t: (t, 12 + p))
    out = pl.BlockSpec((tb, 128), lambda p, t: (t, p))
    return pl.pallas_call(
        body, name=name, grid=(4, nt),
        in_specs=qkv + [gspec] + rope + tspec + [pl.BlockSpec((1, 128), lambda p, t: (0, p))],
        out_specs=[out, out],
        out_shape=[jax.ShapeDtypeStruct((T, RET_WIDTH), BF16), jax.ShapeDtypeStruct((T, RET_WIDTH), F32)],
        scratch_shapes=[pltpu.VMEM((128, 128), F32)],
        compiler_params=_cp(("parallel", "arbitrary")))(proj, proj, proj, proj, cos, ss, *tabs, gnw)


def _ret_bwd_dq(proj, do, cos, ss, tabs, *, name):
    T = proj.shape[0]
    tb = min(T, 1024)
    nt = T // tb
    nchunk = tb // RET_CHUNK

    def body(q_ref, k_ref, v_ref, do_ref, cos_ref, ss_ref, dm_ref, xi_ref, zt_ref, cd_ref, dq_ref, r_sc):
        del q_ref
        @pl.when(pl.program_id(1) == 0)
        def _():
            r_sc[...] = jnp.zeros_like(r_sc)
        m0, m1 = _head_masks((128, 128))
        dm, xi, zt, cd = dm_ref[...], xi_ref[...], zt_ref[...], cd_ref[...]
        bm = (cd > 0).astype(F32)
        for c in range(nchunk):
            rs = pl.ds(c * RET_CHUNK, RET_CHUNK)
            cs, sn = cos_ref[rs, :], ss_ref[rs, :]
            k = _rope(k_ref[rs, :], cs, sn, 32, 64) * K_SCALE
            vb = v_ref[rs, :].astype(BF16)
            dob = do_ref[rs, :]
            dof = dob.astype(F32)
            dos = jnp.concatenate([dof * m0, dof * m1], axis=0).astype(BF16)
            a = (_dot_nt(dos, vb) * dm).astype(BF16)
            ks = jnp.concatenate([k * m0, k * m1], axis=0).astype(BF16)
            r = r_sc[...]
            dq = _dot(jnp.concatenate([a[:128], a[128:]], axis=1), ks) + _dot_nt(dob, r.astype(BF16)) * xi
            r_sc[...] = cd * r + bm * _dot_tn((k * zt).astype(BF16), vb)
            dq_ref[rs, :] = _rope_t(dq, cs, sn, 32, 64).astype(BF16)

    qkv, rope, tspec = _ret_specs(tb, False, nt)
    blk = pl.BlockSpec((tb, 128), lambda p, t: (t, p))
    return pl.pallas_call(
        body, name=name, grid=(4, nt), in_specs=qkv + [blk] + rope + tspec, out_specs=blk,
        out_shape=jax.ShapeDtypeStruct((T, RET_WIDTH), BF16),
        scratch_shapes=[pltpu.VMEM((128, 128), F32)],
        compiler_params=_cp(("parallel", "arbitrary")))(proj, proj, proj, do, cos, ss, *tabs)


def _ret_bwd_dkv(proj, do, cos, ss, tabs, *, name, swap=None):
    T = proj.shape[0]
    tb = min(T, 1024)
    nt = T // tb
    nchunk = tb // RET_CHUNK

    def body(q_ref, k_ref, v_ref, do_ref, cos_ref, ss_ref, dm_ref, xi_ref, zt_ref, cd_ref, *rest):
        if swap is None:
            backward(q_ref, k_ref, v_ref, do_ref, cos_ref, ss_ref, dm_ref, xi_ref, zt_ref, cd_ref, *rest)
        else:
            g_ref, dk_ref, dv_ref, got_ref, u_sc, *sems = rest
            start, finish = _swap_phases(g_ref, got_ref, *sems)
            pl.when((pl.program_id(0) == 0) & (pl.program_id(1) == 0))(start)
            backward(q_ref, k_ref, v_ref, do_ref, cos_ref, ss_ref, dm_ref, xi_ref, zt_ref, cd_ref, dk_ref, dv_ref, u_sc)
            pl.when((pl.program_id(0) == 3) & (pl.program_id(1) == nt - 1))(finish)

    def backward(q_ref, k_ref, v_ref, do_ref, cos_ref, ss_ref, dm_ref, xi_ref, zt_ref, cd_ref, dk_ref, dv_ref, u_sc):
        @pl.when(pl.program_id(1) == 0)
        def _():
            u_sc[...] = jnp.zeros_like(u_sc)
        m0, m1 = _head_masks((128, 128))
        dm, xi, zt, cd = dm_ref[...], xi_ref[...], zt_ref[...], cd_ref[...]
        bm = (cd > 0).astype(F32)
        for c in reversed(range(nchunk)):
            rs = pl.ds(c * RET_CHUNK, RET_CHUNK)
            cs, sn = cos_ref[rs, :], ss_ref[rs, :]
            q = _rope(q_ref[rs, :], cs, sn, 32, 64)
            k = _rope(k_ref[rs, :], cs, sn, 32, 64) * K_SCALE
            kb = k.astype(BF16)
            vb = v_ref[rs, :].astype(BF16)
            dob = do_ref[rs, :]
            dof = dob.astype(F32)
            qs = jnp.concatenate([q * m0, q * m1], axis=0).astype(BF16)
            dos = jnp.concatenate([dof * m0, dof * m1], axis=0).astype(BF16)
            s = (_dot_nt(qs, kb) * dm).astype(BF16)
            a = (_dot_nt(dos, vb) * dm).astype(BF16)
            ub = u_sc[...].astype(BF16)
            dk = _dot_tn(a, qs) + _dot_nt(vb, ub) * zt
            dv = _dot_tn(s, dos) + _dot(kb, ub) * zt
            u_sc[...] = cd * u_sc[...] + bm * _dot_tn((q * xi).astype(BF16), dob)
            dk_ref[rs, :] = (_rope_t(dk, cs, sn, 32, 64) * K_SCALE).astype(BF16)
            dv_ref[rs, :] = dv.astype(BF16)

    qkv, rope, tspec = _ret_specs(tb, True, nt)
    blk = pl.BlockSpec((tb, 128), lambda p, t: (nt - 1 - t, p))
    out_shape = [jax.ShapeDtypeStruct((T, RET_WIDTH), BF16)] * 2
    if swap is None:
        return pl.pallas_call(
            body, name=name, grid=(4, nt), in_specs=qkv + [blk] + rope + tspec, out_specs=[blk, blk],
            out_shape=out_shape, scratch_shapes=[pltpu.VMEM((128, 128), F32)],
            compiler_params=_cp(("parallel", "arbitrary")))(proj, proj, proj, do, cos, ss, *tabs)
    return pl.pallas_call(
        body, name=name, grid=(4, nt), in_specs=qkv + [blk] + rope + tspec + [ANY], out_specs=[blk, blk, ANY],
        out_shape=out_shape + [jax.ShapeDtypeStruct((4,) + swap.shape[2:], swap.dtype)],
        scratch_shapes=[pltpu.VMEM((128, 128), F32)] + list(SWAP_SCRATCH),
        compiler_params=_cp(("arbitrary", "arbitrary")))(proj, proj, proj, do, cos, ss, *tabs, swap)


def _mix_bwd(dmixed, o_ret, proj, y_mla, gnw, *, name):
    T = dmixed.shape[0]
    tm = min(T, 512)

    def body(dm_ref, o_ref, g_ref, ym_ref, gnw_ref, do_ref, dg_ref, dom_ref, dl_ref, dw_ref):
        @pl.when(pl.program_id(0) == 0)
        def _():
            dw_ref[...] = jnp.zeros_like(dw_ref)
        m0, m1 = _head_masks((tm, 128))
        lane = lax.broadcasted_iota(jnp.int32, (tm, 128), 1)
        delta = jnp.zeros((tm, 128), F32)

        def gsum(z):
            return jnp.sum(z * m0, axis=1, keepdims=True) * m0 + jnp.sum(z * m1, axis=1, keepdims=True) * m1

        for p in range(4):
            cs = slice(128 * p, 128 * p + 128)
            dy = dm_ref[:, cs]
            o = o_ref[:, cs]
            g = g_ref[:, cs]
            w = gnw_ref[:, cs]
            d = o - gsum(o) * (1.0 / 64)
            rstd = lax.rsqrt(gsum(d * d) * (1.0 / 64) + EPS)
            oh = d * rstd
            sg = _sigmoid(g)
            dn = dy * (g * sg)
            dg_ref[:, cs] = (dy * (oh * w) * (sg * (1.0 + g * (1.0 - sg)))).astype(BF16)
            dw_ref[:, cs] += jnp.sum(dn * oh, axis=0, keepdims=True)
            doh = dn * w
            do = rstd * (doh - gsum(doh) * (1.0 / 64) - oh * (gsum(doh * oh) * (1.0 / 64)))
            do_ref[:, cs] = do.astype(BF16)
            dom = dm_ref[:, 512 + 128 * p:512 + 128 * p + 128]
            dom_ref[:, cs] = dom.astype(BF16)
            pr = dom * ym_ref[:, cs].astype(F32)
            delta = jnp.where(lane == 2 * p, jnp.sum(pr * m0, axis=1, keepdims=True), delta)
            delta = jnp.where(lane == 2 * p + 1, jnp.sum(pr * m1, axis=1, keepdims=True), delta)
        dl_ref[...] = delta.T[0:MLA_HEADS]

    half = pl.BlockSpec((tm, 512), lambda i: (i, 0))
    return pl.pallas_call(
        body, name=name, grid=(T // tm,),
        in_specs=[pl.BlockSpec((tm, 1024), lambda i: (i, 0)), half, pl.BlockSpec((tm, 512), lambda i: (i, 3)),
                  half, pl.BlockSpec((1, 512), lambda i: (0, 0))],
        out_specs=[half, half, half, pl.BlockSpec((MLA_HEADS, tm), lambda i: (0, i)),
                   pl.BlockSpec((1, 512), lambda i: (0, 0))],
        out_shape=[jax.ShapeDtypeStruct((T, 512), BF16)] * 3 + [jax.ShapeDtypeStruct((MLA_HEADS, T), F32),
                                                                jax.ShapeDtypeStruct((1, 512), F32)],
        compiler_params=_cp(("arbitrary",)))(dmixed, o_ret, proj, y_mla, gnw)


def _mla_prep_fwd(proj, qnw, kvnw, wuq, wk, wv, cos, ss, *, name):
    T = proj.shape[0]
    tm = min(T, 512)

    def body(lat_ref, qnw_ref, kvnw_ref, wuq_ref, wk_ref, wv_ref, cos_ref, ss_ref,
             q_ref, k_ref, v_ref, cqn_ref, ckvn_ref):
        cq = lat_ref[:, 0:256]
        ckv = lat_ref[:, 256:384]
        g3 = lat_ref[:, 384:512]
        cqn = (cq * lax.rsqrt(jnp.mean(cq * cq, axis=-1, keepdims=True) + EPS) * qnw_ref[...]).astype(BF16)
        ckvn = (ckv * lax.rsqrt(jnp.mean(ckv * ckv, axis=-1, keepdims=True) + EPS) * kvnw_ref[...]).astype(BF16)
        cqn_ref[...] = cqn
        ckvn_ref[...] = ckvn
        cs, sn = cos_ref[...], ss_ref[...]
        q = _dot_nt(cqn, wuq_ref[...])
        k = _dot_nt(ckvn, wk_ref[...])
        kpe = _rope(g3, cs, sn, 16, 32)
        for h in range(MLA_HEADS):
            hs = slice(128 * h, 128 * h + 128)
            q_ref[:, hs] = (_rope(q[:, hs], cs, sn, 16, 32) * SCALE).astype(BF16)
            k_ref[:, hs] = (k[:, hs] + kpe).astype(BF16)
        v = _dot_nt(ckvn, wv_ref[...])
        lane = lax.broadcasted_iota(jnp.int32, (tm, 128), 1)
        for p in range(4):
            vp = v[:, 128 * p:128 * p + 128]
            v_ref[:, 256 * p:256 * p + 128] = jnp.where(lane < 64, vp, 1.0).astype(BF16)
            v_ref[:, 256 * p + 128:256 * p + 256] = jnp.where(lane < 64, 1.0, vp).astype(BF16)

    def full(shape):
        return pl.BlockSpec(shape, lambda i: (0, 0))

    def row(w):
        return pl.BlockSpec((tm, w), lambda i: (i, 0))

    return pl.pallas_call(
        body, name=name, grid=(T // tm,),
        in_specs=[pl.BlockSpec((tm, 512), lambda i: (i, 4)), full((1, 256)), full((1, 128)), full((1024, 256)),
                  full((1024, 128)), full((512, 128)), row(128), row(128)],
        out_specs=[row(1024), row(1024), row(1024), row(256), row(128)],
        out_shape=[jax.ShapeDtypeStruct((T, 1024), BF16), jax.ShapeDtypeStruct((T, 1024), BF16),
                   jax.ShapeDtypeStruct((T, 1024), BF16), jax.ShapeDtypeStruct((T, 256), BF16),
                   jax.ShapeDtypeStruct((T, 128), BF16)],
        compiler_params=_cp(("parallel",)))(proj, qnw, kvnw, wuq, wk, wv, cos, ss)


def _mla_prep_bwd(dq, dk, dv, proj, qnw, kvnw, wuq_t, wk_t, wv_t, cos, ss, ret_grads, *, name):
    T = proj.shape[0]
    tm = min(T, 512)

    def body(dq_ref, dk_ref, dv_ref, lat_ref, qnw_ref, kvnw_ref, wuq_ref, wk_ref, wv_ref, cos_ref, ss_ref,
             rq_ref, rk_ref, rv_ref, rg_ref, dproj_ref, dqp_ref, dqnw_ref, dkvnw_ref):
        for j, r in enumerate((rq_ref, rk_ref, rv_ref, rg_ref)):
            dproj_ref[:, 512 * j:512 * j + 512] = r[...]
        dlat_ref = dproj_ref.at[:, 2048:2560]

        @pl.when(pl.program_id(0) == 0)
        def _():
            dqnw_ref[...] = jnp.zeros_like(dqnw_ref)
            dkvnw_ref[...] = jnp.zeros_like(dkvnw_ref)
        cs, sn = cos_ref[...], ss_ref[...]
        dkpe = jnp.zeros((tm, 128), F32)
        for h in range(MLA_HEADS):
            hs = slice(128 * h, 128 * h + 128)
            dqp_ref[:, hs] = _rope_t(dq_ref[:, hs] * SCALE, cs, sn, 16, 32).astype(BF16)
            dkpe = dkpe + dk_ref[:, hs]
        lane = lax.broadcasted_iota(jnp.int32, (tm, 128), 1)
        rope_lane = (lane >= MLA_NOPE) & (lane < MLA_NOPE + MLA_ROPE)
        dg3 = jnp.where(rope_lane, _rope_t(jnp.where(rope_lane, dkpe, 0.0), cs, sn, 16, 32), 0.0)

        def norm_bwd(x, w, dn):
            r = lax.rsqrt(jnp.mean(x * x, axis=-1, keepdims=True) + EPS)
            xh = x * r
            g = dn * w
            return r * (g - xh * jnp.mean(g * xh, axis=-1, keepdims=True)), jnp.sum(dn * xh, axis=0, keepdims=True)

        dcqn = _dot(dqp_ref[...], wuq_ref[...])
        dcq, dqnw = norm_bwd(lat_ref[:, 0:256], qnw_ref[...], dcqn)
        dckvn = _dot(dk_ref[...].astype(BF16), wk_ref[...]) + _dot(dv_ref[...], wv_ref[...])
        dckv, dkvnw = norm_bwd(lat_ref[:, 256:384], kvnw_ref[...], dckvn)
        dqnw_ref[...] += dqnw
        dkvnw_ref[...] += dkvnw
        dlat_ref[:, 0:256] = dcq.astype(BF16)
        dlat_ref[:, 256:384] = dckv.astype(BF16)
        dlat_ref[:, 384:512] = dg3.astype(BF16)

    def full(shape):
        return pl.BlockSpec(shape, lambda i: (0, 0))

    def row(w):
        return pl.BlockSpec((tm, w), lambda i: (i, 0))

    return pl.pallas_call(
        body, name=name, grid=(T // tm,),
        in_specs=[row(1024), row(1024), row(512), pl.BlockSpec((tm, 512), lambda i: (i, 4)), full((1, 256)),
                  full((1, 128)), full((1024, 256)), full((1024, 128)), full((512, 128)), row(128), row(128)]
                 + [row(512)] * 4,
        out_specs=[row(IN_PAD), row(1024), full((1, 256)), full((1, 128))],
        out_shape=[jax.ShapeDtypeStruct((T, IN_PAD), BF16), jax.ShapeDtypeStruct((T, 1024), BF16),
                   jax.ShapeDtypeStruct((1, 256), F32), jax.ShapeDtypeStruct((1, 128), F32)],
        compiler_params=_cp(("arbitrary",)))(dq, dk, dv, proj, qnw, kvnw, wuq_t, wk_t, wv_t, cos, ss, *ret_grads)


def _flash_fwd(q, k, v1, *, name, gather=None):
    T = q.shape[0]
    tq = min(T, 512)
    tk = tq
    nq = T // tq

    def body(q_ref, k_ref, v_ref, *rest):
        if gather is None:
            y_ref, lse_ref = rest
        else:
            x_ref, y_ref, lse_ref, g_ref, *sems = rest
            start, forward, finish = _gather_phases(x_ref, g_ref, *sems)
            pl.when((pl.program_id(0) == 0) & (pl.program_id(1) == 0))(start)
            pl.when((pl.program_id(0) == 1) & (pl.program_id(1) == 0))(forward)
        attend(q_ref, k_ref, v_ref, y_ref, lse_ref)
        if gather is not None:
            pl.when((pl.program_id(0) == 3) & (pl.program_id(1) == nq - 1))(finish)

    def attend(q_ref, k_ref, v_ref, y_ref, lse_ref):
        qi = pl.program_id(1)
        row = lax.broadcasted_iota(jnp.int32, (tq, tk), 0)
        col = lax.broadcasted_iota(jnp.int32, (tq, tk), 1)

        def step(kb, carry, masked):
            ks = pl.ds(pl.multiple_of(kb * tk, tk), tk)
            new = []
            for h in range(2):
                hs = slice(128 * h, 128 * h + 128)
                m, acc = carry[h]
                s = _dot_nt(q_ref[:, hs], k_ref[ks, hs])
                if masked:
                    s = jnp.where(col <= row, s, NEG)
                mn = jnp.maximum(m, jnp.max(s, axis=1, keepdims=True))
                p = jnp.exp((s - mn).astype(BF16))
                acc = jnp.exp(m - mn) * acc + _dot(p, v_ref[ks, hs])
                new.append((mn, acc))
            return tuple(new)

        def unrolled(j, c):
            for u in range(FLASH_UNROLL):
                c = step(FLASH_UNROLL * j + u, c, False)
            return c

        init = (jnp.full((tq, 1), NEG, F32), jnp.zeros((tq, 128), F32))
        carry = lax.fori_loop(0, qi // FLASH_UNROLL, unrolled, (init, init))
        carry = lax.fori_loop(FLASH_UNROLL * (qi // FLASH_UNROLL), qi, lambda kb, c: step(kb, c, False), carry)
        (ma, acca), (mb, accb) = step(qi, carry, True)
        lane = lax.broadcasted_iota(jnp.int32, (tq, 128), 1)
        la, lb = pltpu.roll(acca, 64, 1), pltpu.roll(accb, 64, 1)
        y_ref[...] = jnp.where(lane < 64, acca / la, accb / lb).astype(BF16)
        lse_ref[0, 0] = jnp.broadcast_to(ma + jnp.log(acca[:, 64:65]), (tq, 128)).T[0:1]
        lse_ref[1, 0] = jnp.broadcast_to(mb + jnp.log(accb[:, 0:1]), (tq, 128)).T[0:1]

    in_specs = [pl.BlockSpec((tq, 256), lambda p, i: (i, p)), pl.BlockSpec((T, 256), lambda p, i: (0, p)),
                pl.BlockSpec((T, 256), lambda p, i: (0, p))]
    out_specs = [pl.BlockSpec((tq, 128), lambda p, i: (i, p)), pl.BlockSpec((2, 1, 1, tq), lambda p, i: (p, i, 0, 0))]
    out_shape = [jax.ShapeDtypeStruct((T, MLA_WIDTH), BF16), jax.ShapeDtypeStruct((MLA_HEADS, nq, 1, tq), F32)]
    if gather is None:
        return pl.pallas_call(body, name=name, grid=(4, nq), in_specs=in_specs, out_specs=out_specs,
                              out_shape=out_shape, compiler_params=_cp(("parallel", "arbitrary")))(q, k, v1)
    return pl.pallas_call(
        body, name=name, grid=(4, nq), in_specs=in_specs + [ANY], out_specs=out_specs + [ANY],
        out_shape=out_shape + [jax.ShapeDtypeStruct((N_DEV,) + gather.shape, gather.dtype)],
        scratch_shapes=list(GATHER_SCRATCH),
        compiler_params=_cp(("arbitrary", "arbitrary")))(q, k, v1, gather)


def _flash_bwd(q, k, v, do, lse, delta, *, name, exchange=None):
    T = q.shape[0]
    tq = min(T, 512)
    tk = tq
    nq = T // tq

    def body(q_ref, k_ref, v_ref, do_ref, lse_ref, dl_ref, *rest):
        if exchange is None:
            backward(q_ref, k_ref, v_ref, do_ref, lse_ref, dl_ref, *rest)
        else:
            p_ref, dqt_ref, dk_ref, dv_ref, got_ref, *sems = rest
            start, finish = _exchange_phases(p_ref, got_ref, *sems)
            pl.when((pl.program_id(0) == 0) & (pl.program_id(1) == 0))(start)
            backward(q_ref, k_ref, v_ref, do_ref, lse_ref, dl_ref, dqt_ref, dk_ref, dv_ref)
            pl.when((pl.program_id(0) == 3) & (pl.program_id(1) == nq - 1))(finish)

    def backward(q_ref, k_ref, v_ref, do_ref, lse_ref, dl_ref, dqt_ref, dk_ref, dv_ref):
        kb = pl.program_id(1)

        @pl.when(kb == 0)
        def _():
            dqt_ref[...] = jnp.zeros_like(dqt_ref)
        krow = lax.broadcasted_iota(jnp.int32, (tk, tq), 0)
        qcol = lax.broadcasted_iota(jnp.int32, (tk, tq), 1)
        masks = _head_masks((tk, 128))
        vms = [(v_ref[:, 128 * h:128 * h + 128].astype(F32) * masks[h]).astype(BF16) for h in range(2)]

        def step(qi, carry, masked):
            qs = pl.ds(pl.multiple_of(qi * tq, tq), tq)
            dob = do_ref[qs, :]
            dof = dob.astype(F32)
            dks, dv_acc = list(carry[:2]), carry[2]
            for h in range(2):
                hs = slice(128 * h, 128 * h + 128)
                kh = k_ref[:, hs]
                qh = q_ref[qs, hs]
                st = _dot_nt(kh, qh)
                pt = jnp.exp((st - lse_ref[h, qi]).astype(BF16))
                if masked:
                    pt = jnp.where(krow <= qcol, pt, jnp.zeros_like(pt))
                dv_acc = dv_acc + _dot(pt, (dof * masks[h]).astype(BF16))
                dpt = _dot_nt(vms[h], dob)
                dst = pt * (dpt - dl_ref[h, qi]).astype(BF16)
                dks[h] = dks[h] + _dot(dst, qh)
                dqt_ref[qi, hs, :] += _dot_tn(kh, dst)
            return dks[0], dks[1], dv_acc

        zero = jnp.zeros((tk, 128), F32)
        carry = step(kb, (zero, zero, zero), True)

        def two_steps(j, c):
            qi = kb + 1 + 2 * j
            return step(qi + 1, step(qi, c, False), False)

        pairs = (nq - 1 - kb) // 2
        carry = lax.fori_loop(0, pairs, two_steps, carry)
        dk0, dk1, dv_acc = lax.fori_loop(kb + 1 + 2 * pairs, nq, lambda qi, c: step(qi, c, False), carry)
        dk_ref[:, 0:128] = dk0
        dk_ref[:, 128:256] = dk1
        dv_ref[...] = dv_acc.astype(BF16)

    stat = pl.BlockSpec((2, nq, 1, tq), lambda p, j: (p, 0, 0, 0))
    in_specs = [pl.BlockSpec((T, 256), lambda p, j: (0, p)), pl.BlockSpec((tk, 256), lambda p, j: (j, p)),
                pl.BlockSpec((tk, 256), lambda p, j: (j, p)), pl.BlockSpec((T, 128), lambda p, j: (0, p)), stat, stat]
    out_specs = [pl.BlockSpec((None, nq, 256, tq), lambda p, j: (p, 0, 0, 0)),
                 pl.BlockSpec((tk, 256), lambda p, j: (j, p)), pl.BlockSpec((tk, 128), lambda p, j: (j, p))]
    out_shape = [jax.ShapeDtypeStruct((4, nq, 256, tq), F32), jax.ShapeDtypeStruct((T, 1024), F32),
                 jax.ShapeDtypeStruct((T, MLA_WIDTH), BF16)]
    if exchange is None:
        return pl.pallas_call(body, name=name, grid=(4, nq), in_specs=in_specs, out_specs=out_specs,
                              out_shape=out_shape,
                              compiler_params=_cp(("parallel", "arbitrary")))(q, k, v, do, lse, delta)
    return pl.pallas_call(
        body, name=name, grid=(4, nq), in_specs=in_specs + [ANY], out_specs=out_specs + [ANY],
        out_shape=out_shape + [jax.ShapeDtypeStruct(exchange.shape, exchange.dtype)],
        scratch_shapes=list(EXCHANGE_SCRATCH),
        compiler_params=_cp(("arbitrary", "arbitrary")))(q, k, v, do, lse, delta, exchange)


def _shift_down(x, n, prev8):
    r = pltpu.roll(x, n, 0)
    row = lax.broadcasted_iota(jnp.int32, prev8.shape, 0)
    first = jnp.where(row < n, pltpu.roll(prev8, n, 0), r[:8])
    if x.shape[0] == 8:
        return first
    return jnp.concatenate([first, r[8:]], axis=0)


def _shift_up(x, n, next8):
    tm = x.shape[0]
    r = pltpu.roll(x, tm - n, 0)
    row = lax.broadcasted_iota(jnp.int32, next8.shape, 0)
    last = jnp.where(row >= 8 - n, pltpu.roll(next8, 8 - n, 0), r[tm - 8:])
    return jnp.concatenate([r[:tm - 8], last], axis=0)


def _conv_pre(u, prev8, cw_ref, cb_ref):
    p1 = _shift_down(u, 1, prev8)
    p2 = _shift_down(u, 2, prev8)
    up = cb_ref[...] + cw_ref[0:1, :] * p2 + cw_ref[1:2, :] * p1 + cw_ref[2:3, :] * u
    return up, p1, p2


def _up_proj_conv(x1, nw, w_up_t, cw, cb, *, name):
    T, K = x1.shape
    tm = min(T, 256)

    def body(x_ref, nw_ref, w_ref, cw_ref, cb_ref, h_ref, u_ref, a_ref, carry_sc):
        @pl.when(pl.program_id(0) == 0)
        def _():
            carry_sc[...] = jnp.zeros_like(carry_sc)
        xv = x_ref[...]
        h = (xv * lax.rsqrt(jnp.mean(xv * xv, axis=-1, keepdims=True) + EPS) * nw_ref[...]).astype(BF16)
        h_ref[...] = h
        for blk in range(2):
            ups = []
            for half in range(2):
                cs = slice((2 * blk + half) * FF_HALF, (2 * blk + half + 1) * FF_HALF)
                u = _dot_nt(h, w_ref[cs, :])
                u_ref[:, cs] = u
                prev = carry_sc[:, cs]
                ups.append(cb_ref[:, cs] + cw_ref[0:1, cs] * _shift_down(u, 2, prev)
                           + cw_ref[1:2, cs] * _shift_down(u, 1, prev) + cw_ref[2:3, cs] * u)
                carry_sc[:, cs] = u[tm - 8:]
            gate, val = ups
            a_ref[:, blk * FF_HALF:(blk + 1) * FF_HALF] = (gate * _sigmoid(gate) * val).astype(BF16)

    def full(shape):
        return pl.BlockSpec(shape, lambda i: (0, 0))

    return pl.pallas_call(
        body, name=name, grid=(T // tm,),
        in_specs=[pl.BlockSpec((tm, K), lambda i: (i, 0)), full(nw.shape), full(w_up_t.shape), full(cw.shape),
                  full(cb.shape)],
        out_specs=[pl.BlockSpec((tm, K), lambda i: (i, 0)), pl.BlockSpec((tm, 2 * D_FF), lambda i: (i, 0)),
                   pl.BlockSpec((tm, D_FF), lambda i: (i, 0))],
        out_shape=[jax.ShapeDtypeStruct((T, K), BF16), jax.ShapeDtypeStruct((T, 2 * D_FF), F32),
                   jax.ShapeDtypeStruct((T, D_FF), BF16)],
        scratch_shapes=[pltpu.VMEM((8, 2 * D_FF), F32)],
        compiler_params=_cp(("arbitrary",)))(x1, nw, w_up_t, cw, cb)


def _conv_bwd(u, da, cw, cb, *, name):
    T = u.shape[0]
    tm = min(T, 512)
    W = 2 * FF_HALF
    nt = T // tm

    def body(u_ref, prev_ref, next_ref, da_ref, dan_ref, cw_ref, cb_ref, du_ref, dw0_ref, dw1_ref, dw2_ref, db_ref):
        i = pl.program_id(1)

        @pl.when(i == 0)
        def _():
            for r in (dw0_ref, dw1_ref, dw2_ref, db_ref):
                r[...] = jnp.zeros_like(r)

        def dpre(u, prev8, da):
            up, p1, p2 = _conv_pre(u, prev8, cw_ref, cb_ref)
            gate, val = up[:, :FF_HALF], up[:, FF_HALF:]
            sg = _sigmoid(gate)
            dgate = da * val * (sg * (1.0 + gate * (1.0 - sg)))
            dval = da * (gate * sg)
            return jnp.concatenate([dgate, dval], axis=1), p1, p2

        u = u_ref[...]
        prev = jnp.where(i > 0, prev_ref[...], 0.0)
        dup, p1, p2 = dpre(u, prev, da_ref[...])
        dupn, _, _ = dpre(next_ref[...], u[tm - 8:], dan_ref[...])
        dupn = jnp.where(i < nt - 1, dupn, 0.0)
        du = cw_ref[2:3, :] * dup + cw_ref[1:2, :] * _shift_up(dup, 1, dupn) + cw_ref[0:1, :] * _shift_up(dup, 2, dupn)
        du_ref[...] = du.astype(BF16)
        dw0_ref[...] += jnp.sum(dup * p2, axis=0, keepdims=True)
        dw1_ref[...] += jnp.sum(dup * p1, axis=0, keepdims=True)
        dw2_ref[...] += jnp.sum(dup * u, axis=0, keepdims=True)
        db_ref[...] += jnp.sum(dup, axis=0, keepdims=True)

    nxt = lambda j, i: (jnp.minimum((i + 1) * (tm // 8), T // 8 - 1), j)
    vec = pl.BlockSpec((1, W), lambda j, i: (0, j))
    return pl.pallas_call(
        body, name=name, grid=(2, nt),
        in_specs=[pl.BlockSpec((tm, W), lambda j, i: (i, j)),
                  pl.BlockSpec((8, W), lambda j, i: (jnp.maximum(i * (tm // 8) - 1, 0), j)),
                  pl.BlockSpec((8, W), nxt),
                  pl.BlockSpec((tm, FF_HALF), lambda j, i: (i, j)), pl.BlockSpec((8, FF_HALF), nxt),
                  pl.BlockSpec((3, W), lambda j, i: (0, j)), vec],
        out_specs=[pl.BlockSpec((tm, W), lambda j, i: (i, j)), vec, vec, vec, vec],
        out_shape=[jax.ShapeDtypeStruct((T, 2 * D_FF), BF16)] + [jax.ShapeDtypeStruct((1, 2 * D_FF), F32)] * 4,
        compiler_params=_cp(("parallel", "arbitrary")))(u, u, u, da, da, cw, cb)


def _sum_chips(slots, *, name):
    ns, R, C = slots.shape
    tr = _row_tile(R)

    def body(g_ref, o_ref):
        g = g_ref[0].astype(F32)
        for s in range(1, ns):
            g = g + g_ref[s].astype(F32)
        o_ref[...] = g

    return pl.pallas_call(
        body, name=name, grid=(R // tr,), in_specs=[pl.BlockSpec((ns, tr, C), lambda i: (0, i, 0))],
        out_specs=pl.BlockSpec((tr, C), lambda i: (i, 0)), out_shape=jax.ShapeDtypeStruct((R, C), F32),
        compiler_params=_cp(("parallel",)))(slots)


def _place():
    return lax.axis_index("x"), lax.axis_index("y"), lax.axis_index("c")


GATHER_SCRATCH = (pltpu.SemaphoreType.DMA((7,)), pltpu.SemaphoreType.DMA((7,)), pltpu.SemaphoreType.DMA)
EXCHANGE_SCRATCH = (pltpu.SemaphoreType.DMA((3,)), pltpu.SemaphoreType.DMA((3,)), pltpu.SemaphoreType.DMA)


def _gather_phases(x_ref, out_ref, send_sems, recv_sems, local_sem):
    x_, y_, c_ = _place()
    me, sibling = (x_, y_, c_), (x_, y_, 1 - c_)
    chips = [(1 - x_, y_), (x_, 1 - y_), (1 - x_, 1 - y_)]

    def slot(px, py, pc):
        return out_ref.at[4 * px + 2 * py + pc]

    def copy(k, block, to, src=None):
        return pltpu.make_async_remote_copy(
            src_ref=slot(*block) if src is None else src, dst_ref=slot(*block),
            send_sem=send_sems.at[k], recv_sem=recv_sems.at[k], device_id=to, device_id_type=MESH)

    def mine():
        return pltpu.make_async_copy(x_ref, slot(*me), local_sem)

    def first():
        return [copy(0, me, sibling, src=x_ref)] + [copy(1 + j, me, (*chip, c_), src=x_ref)
                                                     for j, chip in enumerate(chips)]

    def passed():
        return [copy(4 + j, (*chip, c_), sibling) for j, chip in enumerate(chips)]

    def start():
        mine().start()
        for cp in first():
            cp.start()

    def forward():
        fwd = passed()
        for j, chip in enumerate(chips):
            copy(1 + j, (*chip, c_), me).wait_recv()
            fwd[j].start()

    def finish():
        copy(0, sibling, me).wait_recv()
        for j, chip in enumerate(chips):
            copy(4 + j, (*chip, 1 - c_), me).wait_recv()
        for cp in first() + passed():
            cp.wait_send()
        mine().wait()

    return start, forward, finish


def _exchange_phases(p_ref, out_ref, send_sems, recv_sems, local_sem):
    x_, y_, c_ = _place()
    me_k = 2 * x_ + y_
    chips = [(1 - x_, y_), (x_, 1 - y_), (1 - x_, 1 - y_)]

    def local():
        return pltpu.make_async_copy(p_ref.at[me_k], out_ref.at[me_k], local_sem)

    def copy(j, src_k, dst_k, chip):
        return pltpu.make_async_remote_copy(
            src_ref=p_ref.at[src_k], dst_ref=out_ref.at[dst_k], send_sem=send_sems.at[j],
            recv_sem=recv_sems.at[j], device_id=(*chip, c_), device_id_type=MESH)

    def sends():
        return [copy(j, 2 * px + py, me_k, (px, py)) for j, (px, py) in enumerate(chips)]

    def start():
        local().start()
        for cp in sends():
            cp.start()

    def finish():
        for j, (px, py) in enumerate(chips):
            copy(j, me_k, 2 * px + py, (px, py)).wait_recv()
        for cp in sends():
            cp.wait_send()
        local().wait()

    return start, finish


def _all_gather(x, *, name, in_vmem):
    def body(x_ref, out_ref, send_sems, recv_sems, local_sem):
        for phase in _gather_phases(x_ref, out_ref, send_sems, recv_sems, local_sem):
            phase()

    spec = pl.BlockSpec(memory_space=pltpu.VMEM) if in_vmem else ANY
    return pl.pallas_call(
        body, name=name, out_shape=jax.ShapeDtypeStruct((N_DEV,) + x.shape, x.dtype),
        in_specs=[spec], out_specs=spec, scratch_shapes=list(GATHER_SCRATCH),
        compiler_params=pltpu.CompilerParams(vmem_limit_bytes=VMEM_LIMIT))(x)


def _small_rows():
    table, row = [], 0
    for n, size in SMALL_VECTORS:
        table.append((n, size, row))
        row += -(-size // PACK_COLS)
    return table


def _ff_chunk_source(c):
    block, off = divmod(c * 128, FF_HALF)
    return (0, 2, 1, 3)[block] * FF_HALF + off


def _pack_small(parts, *, name):
    table = _small_rows()

    def body(*refs):
        out = refs[-1]
        out[...] = jnp.zeros_like(out)
        for ref, (n, size, row) in zip(refs, table):
            if size != 2 * D_FF:
                out[row:row + 1, 0:size] = ref[...]
                continue
            for c in range(size // 128):
                src = _ff_chunk_source(c)
                r, lane = divmod(c * 128, PACK_COLS)
                out[row + r:row + r + 1, lane:lane + 128] = ref[:, src:src + 128]

    return pl.pallas_call(body, name=name, out_shape=jax.ShapeDtypeStruct((SMALL_ROWS, PACK_COLS), F32))(
        *[parts[n] for n, _, _ in table])


def _sum_small(g, *, name):
    table = _small_rows()
    shapes = [(n, size) for n, size, _ in table if not n.startswith("conv_w")]
    shapes.insert(7, ("conv_w", 2 * D_FF))

    def body(g_ref, *outs):
        def total(row, width):
            acc = g_ref[0, row:row + 1, 0:width]
            for d in range(1, N_DEV):
                acc = acc + g_ref[d, row:row + 1, 0:width]
            return acc

        out_of = {n: o for (n, _), o in zip(shapes, outs)}
        for n, size, row in table:
            o, j = (out_of["conv_w"], int(n[-1])) if n.startswith("conv_w") else (out_of[n], 0)
            for i in range(-(-size // PACK_COLS)):
                width = min(PACK_COLS, size - PACK_COLS * i)
                o[j:j + 1, PACK_COLS * i:PACK_COLS * i + width] = total(row + i, width)

    out_shape = [jax.ShapeDtypeStruct((3 if n == "conv_w" else 1, size), F32) for n, size in shapes]
    res = pl.pallas_call(body, name=name, out_shape=out_shape)(g)
    return {n: r for (n, _), r in zip(shapes, res)}


def _adamw_multi(ws, ms, vs, gs, *, name):
    k = len(ws)

    def body(*refs):
        w_refs, m_refs, v_refs, g_refs = (refs[i * k:(i + 1) * k] for i in range(4))
        outs = refs[4 * k:]
        for i in range(k):
            g = g_refs[i][...]
            mn = ADAM_B1 * m_refs[i][...] + (1.0 - ADAM_B1) * g
            vn = ADAM_B2 * v_refs[i][...] + (1.0 - ADAM_B2) * (g * g)
            m_hat = mn / (1.0 - ADAM_B1 ** ADAM_STEP)
            v_hat = vn / (1.0 - ADAM_B2 ** ADAM_STEP)
            outs[i][...] = g
            outs[k + i][...] = -ADAM_LR * (m_hat / (jnp.sqrt(v_hat) + ADAM_EPS) + ADAM_WD * w_refs[i][...])
            outs[2 * k + i][...] = mn
            outs[3 * k + i][...] = vn

    out_shape = [jax.ShapeDtypeStruct(w.shape, F32) for _ in range(4) for w in ws]
    res = pl.pallas_call(body, name=name, out_shape=out_shape, compiler_params=_cp())(*ws, *ms, *vs, *gs)
    return [res[i * k:(i + 1) * k] for i in range(4)]


SWAP_SCRATCH = (pltpu.SemaphoreType.DMA((4,)), pltpu.SemaphoreType.DMA((4,)))


def _swap_phases(g_ref, out_ref, send_sems, recv_sems):
    x_, y_, c_ = _place()

    def copies():
        return [pltpu.make_async_remote_copy(src_ref=g_ref.at[k, 1 - c_], dst_ref=out_ref.at[k],
                                             send_sem=send_sems.at[k], recv_sem=recv_sems.at[k],
                                             device_id=(x_, y_, 1 - c_), device_id_type=MESH) for k in range(4)]

    def start():
        for cp in copies():
            cp.start()

    def finish():
        for cp in copies():
            cp.wait()

    return start, finish


def _swap_sibling(g, *, name):
    def body(g_ref, out_ref, send_sems, recv_sems):
        for phase in _swap_phases(g_ref, out_ref, send_sems, recv_sems):
            phase()

    return pl.pallas_call(
        body, name=name, out_shape=jax.ShapeDtypeStruct((4,) + g.shape[2:], g.dtype), in_specs=[ANY], out_specs=ANY,
        scratch_shapes=list(SWAP_SCRATCH))(g)


def _row_tile(R):
    for cand in (256, 400, 200):
        if R % cand == 0:
            return cand
    return R


def _add_own(g, b, *, name, out_dtype):
    n, _, R, C = g.shape
    tr = _row_tile(R)

    def body(c_ref, g_ref, b_ref, o_ref):
        del c_ref
        o_ref[...] = (g_ref[...] + b_ref[...]).astype(out_dtype)

    blk = pl.BlockSpec((None, tr, C), lambda s, i, c: (s, i, 0))
    grid_spec = pltpu.PrefetchScalarGridSpec(
        num_scalar_prefetch=1, grid=(n, R // tr),
        in_specs=[pl.BlockSpec((None, None, tr, C), lambda s, i, c: (s, c[0], i, 0)), blk], out_specs=blk)
    core = jnp.reshape(lax.axis_index("c"), (1,)).astype(jnp.int32)
    return pl.pallas_call(body, name=name, grid_spec=grid_spec, out_shape=jax.ShapeDtypeStruct(b.shape, out_dtype),
                          compiler_params=_cp(("parallel", "parallel")))(core, g, b)


def _pack_local(parts, group):
    table, rows = group
    segs = []
    for n, r, rp, tr in table:
        w = parts[n].T if tr else parts[n]
        segs.append(jnp.pad(w.reshape(r, PACK_COLS), ((0, rp - r), (0, 0))))
    segs.append(jnp.zeros((rows - sum(rp for _, _, rp, _ in table), PACK_COLS), segs[0].dtype))
    return jnp.concatenate(segs, axis=0)


def _unpack_local(packed, like, group):
    out, off = {}, 0
    for n, r, rp, tr in group[0]:
        rows, cols = like[n].shape
        seg = packed[off:off + r]
        out[n] = (seg.reshape(cols, rows).T if tr else seg)[None]
        off += rp
    return out


def _segments(g, group):
    out, off = {}, 0
    for n, r, rp, _ in group[0]:
        out[n] = g[:, off:off + r]
        off += rp
    return out


def _pack_grads(parts, group):
    table, rows = group
    segs = [jnp.pad(parts[n], ((0, 0), (0, rp - parts[n].shape[1]), (0, 0))) for n, _, rp, _ in table]
    segs.append(jnp.zeros((N_DEV, rows - sum(rp for _, _, rp, _ in table), PACK_COLS), F32))
    return jnp.concatenate(segs, axis=1)


def _owner_rows_early(g):
    g_in = jnp.concatenate([g["w_in_t"][:2432], g["w_in_t"][2496:2528]], axis=0).reshape(N_DEV, 308, PACK_COLS)
    g_uq = g["w_uq_t"].reshape(N_DEV, 128, MLA_Q_RANK)[:, :96].reshape(N_DEV, 24, PACK_COLS)
    g_ukv = jnp.concatenate([g["w_k_t"].reshape(N_DEV, 128, MLA_KV_RANK)[:, :64],
                             g["w_v_t"].reshape(N_DEV, 64, MLA_KV_RANK)], axis=1).reshape(N_DEV, 16, PACK_COLS)
    return dict(w_in=g_in, w_uq=g_uq, w_ukv=g_ukv)


def _owner_rows_late(g):
    g_up = g["w_up_t"].reshape(2, 2, 2, 704, PACK_COLS).swapaxes(0, 1).reshape(N_DEV, 704, PACK_COLS)
    return dict(w_out=g["w_out"].reshape(N_DEV, 128, PACK_COLS), w_up=g_up,
                w_down=g["w_down"].reshape(N_DEV, 352, PACK_COLS))


def _reduce_to_pairs(gp, *, name):
    gp = gp.reshape(4, 2, gp.shape[1], PACK_COLS)
    return _add_own(gp, _swap_sibling(gp, name=name + "_swap"), out_dtype=BF16, name=name + "_sum")


def _interleave_ff(w):
    g, v = w[..., :D_FF], w[..., D_FF:]
    return jnp.concatenate([g[..., :FF_HALF], v[..., :FF_HALF], g[..., FF_HALF:], v[..., FF_HALF:]], axis=-1)


def _rope_tables(pos):
    p = pos.astype(F32)[:, None]
    inv_r = ROPE_BASE ** (-jnp.arange(0, RET_HEAD_DIM, 2, dtype=F32) / RET_HEAD_DIM)
    ang = p * jnp.tile(inv_r, 4)
    sign_r = jnp.tile(jnp.concatenate([-jnp.ones((32,), F32), jnp.ones((32,), F32)]), 2)
    cos_r, ss_r = jnp.cos(ang), jnp.sin(ang) * sign_r
    inv_m = ROPE_BASE ** (-jnp.arange(0, MLA_ROPE, 2, dtype=F32) / MLA_ROPE)
    ang = p * jnp.concatenate([jnp.zeros((64,), F32), inv_m, inv_m, jnp.zeros((32,), F32)])
    sign_m = jnp.concatenate([jnp.zeros((64,), F32), -jnp.ones((16,), F32), jnp.ones((16,), F32), jnp.zeros((32,), F32)])
    cos_m, ss_m = jnp.cos(ang), jnp.sin(ang) * sign_m
    return cos_r, ss_r, cos_m, ss_m


def _prep_early(gathered):
    seg = _segments(gathered, EARLY)
    w_in_t = seg["w_in"].reshape(IN_WIDTH, D_MODEL)
    z = lambda n: jnp.zeros((n, D_MODEL), BF16)
    w_in_t = jnp.concatenate([w_in_t[:2432], z(64), w_in_t[2432:2464], z(32)], axis=0)
    w_uq_t = jnp.pad(seg["w_uq"].reshape(MLA_HEADS, 96, MLA_Q_RANK), ((0, 0), (0, 32), (0, 0))).reshape(1024, MLA_Q_RANK)
    ukv = seg["w_ukv"].reshape(MLA_HEADS, 128, MLA_KV_RANK)
    w_k_t = jnp.pad(ukv[:, :64], ((0, 0), (0, 64), (0, 0))).reshape(1024, MLA_KV_RANK)
    w_v_t = ukv[:, 64:].reshape(512, MLA_KV_RANK)
    return dict(w_in_t=w_in_t, w_uq_t=w_uq_t, w_k_t=w_k_t, w_v_t=w_v_t)


def _prep_late(gathered):
    seg = _segments(gathered, LATE)
    w_up_t = seg["w_up"].reshape(2, 2, 2, 704, D_MODEL).swapaxes(0, 1).reshape(2 * D_FF, D_MODEL)
    return dict(w_out=seg["w_out"].reshape(1024, D_MODEL), w_up_t=w_up_t, w_down=seg["w_down"].reshape(D_FF, D_MODEL))


def _local_step(x, pos, tgt, early, sm, late):
    dist = not isinstance(late, dict)
    cos_r, ss_r, cos_m, ss_m = _rope_tables(pos)
    tabs = _ret_tables()

    if dist:
        h, gathered = _rmsnorm_fwd(x, sm["attn_norm_w"], gather=early, name="attn_norm")
        W = _prep_early(gathered)
    else:
        h = _rmsnorm_fwd(x, sm["attn_norm_w"], name="attn_norm")
        W = early
    proj = _mm(h, W["w_in_t"], bt=True, name="in_proj")
    y_ret, o_ret = _ret_fwd(proj, cos_r, ss_r, tabs, sm["ret_gn_w"], name="ret_fwd")
    q, k, v1, cqn, ckvn = _mla_prep_fwd(proj, sm["mla_q_norm_w"], sm["mla_kv_norm_w"], W["w_uq_t"], W["w_k_t"],
                                       W["w_v_t"], cos_m, ss_m, name="mla_prep")
    T = x.shape[0]
    tq = min(T, 512)
    if dist:
        y_mla, lse, gathered = _flash_fwd(q, k, v1, gather=late, name="mla_attn")
        W = {**W, **_prep_late(gathered)}
    else:
        y_mla, lse = _flash_fwd(q, k, v1, name="mla_attn")
        W = {**W, **late}
    mixed = (y_ret, y_mla)
    x1 = _mm(mixed, W["w_out"], add=x, name="out_proj")
    h2, u, a = _up_proj_conv(x1, sm["ffn_norm_w"], W["w_up_t"], sm["conv_w"], sm["conv_b"], name="ffn_norm_up_conv")
    loss, dx2, dx2b, d_final = _down_proj_loss(a, W["w_down"], x1, tgt, sm["final_norm_w"], name="down_proj_loss")

    g = {}
    g["w_down"] = _mm_tn(a, dx2b, name="dw_down")
    da = _mm(dx2b, W["w_down"], bt=True, name="d_act")
    du, dcw0, dcw1, dcw2, dcb = _conv_bwd(u, da, sm["conv_w"], sm["conv_b"], name="conv_bwd")
    g["w_up_t"] = _mm_tn(du, h2, name="dw_up")
    dx1, d_ffn = _mm_norm_bwd(du, W["w_up_t"], x1, sm["ffn_norm_w"], dx2, name="d_h2_ffn_norm_bwd")

    g["w_out"] = _mm_tn(mixed, dx1, name="dw_out")
    dmixed = _mm(dx1, W["w_out"], bt=True, name="d_mixed")
    do_ret, dg, do_mla, delta, d_gn = _mix_bwd(dmixed, o_ret, proj, y_mla, sm["ret_gn_w"], name="mix_bwd")
    drq = _ret_bwd_dq(proj, do_ret, cos_r, ss_r, tabs, name="ret_bwd_dq")
    delta_r = delta.reshape(MLA_HEADS, T // tq, 1, tq)
    if dist:
        gl = _pack_grads(_owner_rows_late(g), LATE).reshape(4, 2, LATE[1], PACK_COLS)
        drk, drv, theirs = _ret_bwd_dkv(proj, do_ret, cos_r, ss_r, tabs, swap=gl, name="ret_bwd_dkv")
        pair = _add_own(gl, theirs, out_dtype=BF16, name="grad_late_sum")
        dqt, dk, dv, slots_late = _flash_bwd(q, k, v1, do_mla, lse, delta_r, exchange=pair, name="mla_attn_bwd")
    else:
        drk, drv = _ret_bwd_dkv(proj, do_ret, cos_r, ss_r, tabs, name="ret_bwd_dkv")
        dqt, dk, dv = _flash_bwd(q, k, v1, do_mla, lse, delta_r, name="mla_attn_bwd")
        slots_late = None
    dq = dqt.transpose(1, 3, 0, 2).reshape(T, MLA_HEADS * 128)
    dproj, dqp, d_qn, d_kvn = _mla_prep_bwd(dq, dk, dv, proj, sm["mla_q_norm_w"], sm["mla_kv_norm_w"], W["w_uq_t"],
                                            W["w_k_t"], W["w_v_t"], cos_m, ss_m, (drq, drk, drv, dg),
                                            name="mla_prep_bwd")
    g["w_uq_t"] = _mm_tn(dqp, cqn, name="dw_uq")
    g["w_k_t"] = _mm_tn(dk, ckvn, name="dw_ukv_k")
    g["w_v_t"] = _mm_tn(dv, ckvn, name="dw_ukv_v")
    g["w_in_t"] = _mm_tn(dproj, h, name="dw_in")
    if dist:
        pair = _reduce_to_pairs(_pack_grads(_owner_rows_early(g), EARLY), name="grad_early")
        grad_x, d_attn, slots_early = _mm_norm_bwd(dproj, W["w_in_t"], x, sm["attn_norm_w"], dx1, exchange=pair,
                                                   name="d_h_attn_norm_bwd")
    else:
        grad_x, d_attn = _mm_norm_bwd(dproj, W["w_in_t"], x, sm["attn_norm_w"], dx1, name="d_h_attn_norm_bwd")
        slots_early = None

    small = dict(attn_norm_w=d_attn, ret_gn_w=d_gn, mla_q_norm_w=d_qn, mla_kv_norm_w=d_kvn, ffn_norm_w=d_ffn,
                 conv_b=dcb, final_norm_w=d_final, conv_w0=dcw0, conv_w1=dcw1, conv_w2=dcw2, loss=loss)
    return loss, grad_x, g, small, slots_early, slots_late


def kernel(x, positions, attn_norm_w, w_in, ret_gn_w, mla_q_norm_w, w_uq, mla_kv_norm_w, w_ukv, w_out, ffn_norm_w, w_up, conv_w, conv_b, w_down, final_norm_w, loss_target, m_attn_norm_w, m_w_in, m_ret_gn_w, m_mla_q_norm_w, m_w_uq, m_mla_kv_norm_w, m_w_ukv, m_w_out, m_ffn_norm_w, m_w_up, m_conv_w, m_conv_b, m_w_down, m_final_norm_w, v_attn_norm_w, v_w_in, v_ret_gn_w, v_mla_q_norm_w, v_w_uq, v_mla_kv_norm_w, v_w_ukv, v_w_out, v_ffn_norm_w, v_w_up, v_conv_w, v_conv_b, v_w_down, v_final_norm_w):
    a = dict(locals())
    x_, y_, c_ = _place()
    dev = 4 * x_ + 2 * y_ + c_

    shard = {n: a[n][0] for n in BIG_NAMES}
    shard16 = {n: w.astype(BF16) for n, w in shard.items()}
    cw_pad = jnp.pad(conv_w[0].reshape(-1), (0, 24 * 128 - 3 * 704)).reshape(24, 128)
    cw_all = _all_gather(cw_pad, name="gather_conv_w", in_vmem=True)
    conv_w_full = cw_all.reshape(N_DEV, -1)[:, :3 * 704].reshape(N_DEV, 3, 704).transpose(1, 0, 2).reshape(3, 2 * D_FF)
    sm = dict(attn_norm_w=attn_norm_w, ret_gn_w=ret_gn_w, mla_q_norm_w=mla_q_norm_w, mla_kv_norm_w=mla_kv_norm_w,
              ffn_norm_w=ffn_norm_w, final_norm_w=final_norm_w.reshape(1, D_MODEL),
              conv_w=_interleave_ff(conv_w_full), conv_b=_interleave_ff(conv_b))

    loss, grad_x, _, gs, slots_early, slots_late = _local_step(
        x[0], positions[0], loss_target[0], _pack_local(shard16, EARLY), sm, _pack_local(shard16, LATE))

    big = [{}, {}, {}, {}]
    for group, slots, tag, calls in ((EARLY, slots_early, "early", (("w_in", "w_uq", "w_ukv"),)),
                                     (LATE, slots_late, "late", (("w_out", "w_down"), ("w_up",)))):
        grads = _unpack_local(_sum_chips(slots, name="grad_sum_" + tag), shard, group)
        for names_c in calls:
            res = _adamw_multi([shard[n] for n in names_c], [a["m_" + n][0] for n in names_c],
                               [a["v_" + n][0] for n in names_c], [grads[n][0] for n in names_c],
                               name="adamw_" + "_".join(names_c))
            for kind in range(4):
                for n, r in zip(names_c, res[kind]):
                    big[kind][n] = r[None]

    packed = _pack_small(gs, name="pack_small_grads")
    tot = _sum_small(_all_gather(packed, name="gather_small_grads", in_vmem=True), name="sum_small_grads")
    loss_out = tot["loss"][0, 0]
    g_cw = lax.dynamic_slice_in_dim(tot["conv_w"], dev * 704, 704, axis=1)

    def rows_of(prefix):
        return [a[prefix + n].reshape(1, size) for n, size in SMALL]

    sml = _adamw_multi(rows_of("") + [conv_w[0]], rows_of("m_") + [m_conv_w[0]], rows_of("v_") + [v_conv_w[0]],
                       [tot[n] for n, _ in SMALL] + [g_cw], name="adamw_small")
    cwo = [kind[-1] for kind in sml]

    def small_of(kind, n):
        return sml[kind][[nm for nm, _ in SMALL].index(n)].reshape(a[n].shape)

    names = ['attn_norm_w', 'w_in', 'ret_gn_w', 'mla_q_norm_w', 'w_uq', 'mla_kv_norm_w', 'w_ukv', 'w_out',
             'ffn_norm_w', 'w_up', 'conv_w', 'conv_b', 'w_down', 'final_norm_w']
    outs = [loss_out, grad_x[None]]
    for kind in range(4):
        for n in names:
            if n == "conv_w":
                outs.append(cwo[kind][None])
            elif n in big[kind]:
                outs.append(big[kind][n])
            else:
                outs.append(small_of(kind, n))
    return tuple(outs)
```

```python
import functools

import numpy as np
import jax
import jax.numpy as jnp
from jax import lax
from jax.experimental import pallas as pl
from jax.experimental.pallas import tpu as pltpu

F32 = jnp.float32
BF16 = jnp.bfloat16
MESH = pl.DeviceIdType.MESH
ANY = pl.BlockSpec(memory_space=pl.ANY)

D_MODEL = 1024
RET_HEADS = 8
RET_HEAD_DIM = 64
RET_WIDTH = 512
RET_CHUNK = 128
MLA_HEADS = 8
MLA_NOPE = 64
MLA_ROPE = 32
MLA_V = 64
MLA_Q_RANK = 256
MLA_KV_RANK = 128
MLA_WIDTH = 512
IN_WIDTH = 2464
IN_PAD = 2560
D_FF = 2816
FF_HALF = 1408
ROPE_BASE = 10000.0
EPS = 1e-6
SCALE = float((MLA_NOPE + MLA_ROPE) ** -0.5)
K_SCALE = 0.125
N_DEV = 8

ADAM_LR = 0.001
ADAM_B1 = 0.9
ADAM_B2 = 0.999
ADAM_EPS = 1e-08
ADAM_WD = 0.01
ADAM_STEP = 10

VMEM_LIMIT = 56 * 1024 * 1024
MM_BUDGET = 40 * 1024 * 1024
NEG = -1e30
FLASH_UNROLL = 4
FLASH_BWD_UNROLL = 3

PACK_COLS = 1024
EARLY = ((("w_in", 308, 320, True), ("w_uq", 24, 32, True), ("w_ukv", 16, 16, True)), 384)
LATE = ((("w_out", 128, 128, False), ("w_up", 704, 704, True), ("w_down", 352, 352, False)), 1200)
BIG_NAMES = ("w_in", "w_uq", "w_ukv", "w_out", "w_up", "w_down")
SMALL = (("attn_norm_w", 1024), ("ret_gn_w", 512), ("mla_q_norm_w", 256), ("mla_kv_norm_w", 128),
         ("ffn_norm_w", 1024), ("conv_b", 5632), ("final_norm_w", 1024))
SMALL_VECTORS = SMALL + (("conv_w0", 5632), ("conv_w1", 5632), ("conv_w2", 5632), ("loss", 128))
SMALL_ROWS = 32


def _cp(sem=None, vmem=VMEM_LIMIT):
    return pltpu.CompilerParams(dimension_semantics=sem, vmem_limit_bytes=vmem)


def _dot(a, b):
    return jnp.dot(a, b, preferred_element_type=F32)


def _dot_nt(a, b):
    return lax.dot_general(a, b, (((1,), (1,)), ((), ())), preferred_element_type=F32)


def _dot_tn(a, b):
    return lax.dot_general(a, b, (((0,), (0,)), ((), ())), preferred_element_type=F32)


def _sigmoid(x):
    return 0.5 * jnp.tanh(0.5 * x) + 0.5


def _partner(x, half, period):
    n = x.shape[-1]
    lane = lax.broadcasted_iota(jnp.int32, x.shape, 1)
    return jnp.where((lane % period) < half, pltpu.roll(x, n - half, 1), pltpu.roll(x, half, 1))


def _rope(x, cos, ss, half, period):
    return x * cos + _partner(x, half, period) * ss


def _rope_t(dy, cos, ss, half, period):
    return dy * cos - _partner(dy, half, period) * ss


def _head_masks(shape):
    lane = lax.broadcasted_iota(jnp.int32, shape, 1)
    m0 = (lane < 64).astype(F32)
    return m0, 1.0 - m0


def _mm(a, b, *, name, add=None, out_dtype=F32, bt=False):
    parts = a if isinstance(a, tuple) else (a,)
    M = parts[0].shape[0]
    K = sum(p.shape[1] for p in parts)
    N = b.shape[0] if bt else b.shape[1]
    osz = jnp.dtype(out_dtype).itemsize
    per_row = 2 * (K * parts[0].dtype.itemsize + N * osz + (N * 4 if add is not None else 0))
    tm = 128
    for cand in (512, 256):
        if M % cand == 0 and cand * per_row + 4 * K * N <= MM_BUDGET:
            tm = cand
            break
    tm = min(tm, M)
    mul = _dot_nt if bt else _dot
    n_a = len(parts)
    n_in = n_a + (1 if add is None else 2)

    def body(*refs):
        av = refs[0][...] if n_a == 1 else jnp.concatenate([r[...] for r in refs[:n_a]], axis=1)
        acc = mul(av.astype(BF16), refs[n_a][...])
        if add is not None:
            acc = refs[n_a + 1][...] + acc
        refs[n_in][...] = acc.astype(out_dtype)

    in_specs = [pl.BlockSpec((tm, p.shape[1]), lambda i: (i, 0)) for p in parts]
    in_specs.append(pl.BlockSpec(b.shape, lambda i: (0, 0)))
    args = [*parts, b]
    if add is not None:
        in_specs.append(pl.BlockSpec((tm, N), lambda i: (i, 0)))
        args.append(add)
    return pl.pallas_call(
        body, name=name, grid=(M // tm,), in_specs=in_specs, out_specs=pl.BlockSpec((tm, N), lambda i: (i, 0)),
        out_shape=jax.ShapeDtypeStruct((M, N), out_dtype), compiler_params=_cp(("parallel",)))(*args)


def _mm_tn(a, b, *, name):
    parts = a if isinstance(a, tuple) else (a,)
    T = parts[0].shape[0]
    M = sum(p.shape[1] for p in parts)
    N = b.shape[1]
    tk = min(T, 512)

    def tile(n):
        for cand in (1408, 1280):
            if n > 1408 and n % cand == 0:
                return cand
        return n

    tm, tn = tile(M), tile(N)
    nk = T // tk
    n_a = len(parts)
    assert n_a == 1 or tm == M

    def body(*refs):
        o_ref = refs[n_a + 1]

        @pl.when(pl.program_id(2) == 0)
        def _():
            o_ref[...] = jnp.zeros_like(o_ref)
        av = refs[0][...] if n_a == 1 else jnp.concatenate([r[...] for r in refs[:n_a]], axis=1)
        o_ref[...] += _dot_tn(av.astype(BF16), refs[n_a][...].astype(BF16))

    if n_a == 1:
        a_specs = [pl.BlockSpec((tk, tm), lambda i, j, k: (k, i))]
    else:
        a_specs = [pl.BlockSpec((tk, p.shape[1]), lambda i, j, k: (k, 0)) for p in parts]
    return pl.pallas_call(
        body, name=name, grid=(M // tm, N // tn, nk),
        in_specs=a_specs + [pl.BlockSpec((tk, tn), lambda i, j, k: (k, j))],
        out_specs=pl.BlockSpec((tm, tn), lambda i, j, k: (i, j)),
        out_shape=jax.ShapeDtypeStruct((M, N), F32),
        compiler_params=_cp(("parallel", "parallel", "arbitrary")))(*parts, b)


def _rmsnorm_fwd(x, w, *, name, gather=None):
    T, D = x.shape
    tm = min(T, 1024)
    n = T // tm

    def body(x_ref, w_ref, *rest):
        if gather is not None:
            s_ref, o_ref, g_ref, *sems = rest
            start, forward, finish = _gather_phases(s_ref, g_ref, *sems)
            pl.when(pl.program_id(0) == 0)(start)
            pl.when(pl.program_id(0) == n // 2)(forward)
        else:
            o_ref, = rest
        xv = x_ref[...]
        r = lax.rsqrt(jnp.mean(xv * xv, axis=-1, keepdims=True) + EPS)
        o_ref[...] = (xv * r * w_ref[...]).astype(BF16)
        if gather is not None:
            pl.when(pl.program_id(0) == n - 1)(finish)

    in_specs = [pl.BlockSpec((tm, D), lambda i: (i, 0)), pl.BlockSpec((1, D), lambda i: (0, 0))]
    out_spec = pl.BlockSpec((tm, D), lambda i: (i, 0))
    out_shape = jax.ShapeDtypeStruct((T, D), BF16)
    if gather is None:
        return pl.pallas_call(body, name=name, grid=(n,), in_specs=in_specs, out_specs=out_spec, out_shape=out_shape,
                              compiler_params=_cp(("parallel",)))(x, w)
    return pl.pallas_call(
        body, name=name, grid=(n,), in_specs=in_specs + [ANY], out_specs=[out_spec, ANY],
        out_shape=[out_shape, jax.ShapeDtypeStruct((N_DEV,) + gather.shape, gather.dtype)],
        scratch_shapes=list(GATHER_SCRATCH), compiler_params=_cp(("arbitrary",)))(x, w, gather)


def _mm_norm_bwd(a, b, x, w, dres, *, name, exchange=None):
    T, K = a.shape
    D = b.shape[1]
    tm = min(T, 256 if K > 4096 else 512)
    n = T // tm

    def body(a_ref, b_ref, x_ref, w_ref, dr_ref, *rest):
        if exchange is None:
            dx_ref, dw_ref = rest
        else:
            p_ref, dx_ref, dw_ref, got_ref, *sems = rest
            start, finish = _exchange_phases(p_ref, got_ref, *sems)
            pl.when(pl.program_id(0) == 0)(start)

        @pl.when(pl.program_id(0) == 0)
        def _():
            dw_ref[...] = jnp.zeros_like(dw_ref)
        dh = _dot(a_ref[...], b_ref[...])
        xv = x_ref[...]
        r = lax.rsqrt(jnp.mean(xv * xv, axis=-1, keepdims=True) + EPS)
        xh = xv * r
        g = dh * w_ref[...]
        dx_ref[...] = dr_ref[...] + r * (g - xh * jnp.mean(g * xh, axis=-1, keepdims=True))
        dw_ref[...] += jnp.sum(dh * xh, axis=0, keepdims=True)
        if exchange is not None:
            pl.when(pl.program_id(0) == n - 1)(finish)

    row = pl.BlockSpec((tm, D), lambda i: (i, 0))
    vec = pl.BlockSpec((1, D), lambda i: (0, 0))
    in_specs = [pl.BlockSpec((tm, K), lambda i: (i, 0)), pl.BlockSpec((K, D), lambda i: (0, 0)), row, vec, row]
    out_shape = [jax.ShapeDtypeStruct((T, D), F32), jax.ShapeDtypeStruct((1, D), F32)]
    if exchange is None:
        return pl.pallas_call(body, name=name, grid=(n,), in_specs=in_specs, out_specs=[row, vec], out_shape=out_shape,
                              compiler_params=_cp(("arbitrary",)))(a, b, x, w, dres)
    return pl.pallas_call(
        body, name=name, grid=(n,), in_specs=in_specs + [ANY], out_specs=[row, vec, ANY],
        out_shape=out_shape + [jax.ShapeDtypeStruct(exchange.shape, exchange.dtype)],
        scratch_shapes=list(EXCHANGE_SCRATCH), compiler_params=_cp(("arbitrary",)))(a, b, x, w, dres, exchange)


def _down_proj_loss(a, w_down, x1, tgt, w, *, name):
    T, D = x1.shape
    K = a.shape[1]
    tm = min(T, 512)

    def body(a_ref, b_ref, x_ref, t_ref, w_ref, loss_ref, dx_ref, dxb_ref, dw_ref):
        @pl.when(pl.program_id(0) == 0)
        def _():
            dw_ref[...] = jnp.zeros_like(dw_ref)
            loss_ref[...] = jnp.zeros_like(loss_ref)
        xv = x_ref[...] + _dot(a_ref[...], b_ref[...])
        wv = w_ref[...]
        r = lax.rsqrt(jnp.mean(xv * xv, axis=-1, keepdims=True) + EPS)
        xh = xv * r
        e = xh * wv - t_ref[...]
        part = 0.5 * jnp.sum(jnp.mean(e * e, axis=-1, keepdims=True), axis=0, keepdims=True)
        loss_ref[...] += jnp.broadcast_to(part, loss_ref.shape)
        dy = e * (1.0 / D)
        g = dy * wv
        dx = r * (g - xh * jnp.mean(g * xh, axis=-1, keepdims=True))
        dx_ref[...] = dx
        dxb_ref[...] = dx.astype(BF16)
        dw_ref[...] += jnp.sum(dy * xh, axis=0, keepdims=True)

    row = pl.BlockSpec((tm, D), lambda i: (i, 0))
    vec = pl.BlockSpec((1, D), lambda i: (0, 0))
    return pl.pallas_call(
        body, name=name, grid=(T // tm,),
        in_specs=[pl.BlockSpec((tm, K), lambda i: (i, 0)), pl.BlockSpec((K, D), lambda i: (0, 0)), row, row, vec],
        out_specs=[pl.BlockSpec((1, 128), lambda i: (0, 0)), row, row, vec],
        out_shape=[jax.ShapeDtypeStruct((1, 128), F32), jax.ShapeDtypeStruct((T, D), F32),
                   jax.ShapeDtypeStruct((T, D), BF16), jax.ShapeDtypeStruct((1, D), F32)],
        compiler_params=_cp(("arbitrary",)))(a, w_down, x1, tgt, w)


def _ret_tables():
    C = RET_CHUNK
    h = jnp.arange(RET_HEADS, dtype=F32)
    log_gamma = jnp.log1p(-jnp.power(2.0, -5.0 - h))
    idx = jnp.arange(C, dtype=F32)
    diff = idx[:, None] - idx[None, :]
    dm = jnp.where(diff >= 0, jnp.exp(log_gamma[:, None, None] * jnp.maximum(diff, 0.0)), 0.0)
    dm = dm.reshape(4, 2 * C, C)
    lane_head = jnp.repeat(jnp.arange(RET_HEADS).reshape(4, 2), 64, axis=1)
    lg = log_gamma[lane_head]
    xi = jnp.exp(lg[:, None, :] * (idx[None, :, None] + 1.0))
    zeta = jnp.exp(lg[:, None, :] * (C - 1.0 - idx[None, :, None]))
    blk = (jnp.arange(128)[:, None] // 64) == (jnp.arange(128)[None, :] // 64)
    cd = jnp.where(blk[None], jnp.exp(lg * C)[:, :, None], 0.0)
    return dm.astype(F32), xi.astype(F32), zeta.astype(F32), cd.astype(F32)


def _ret_specs(tb, rev, nt):
    def tmap(t):
        return (nt - 1 - t) if rev else t
    qkv = [pl.BlockSpec((tb, 128), lambda p, t, o=o: (tmap(t), o + p)) for o in (0, 4, 8)]
    rope = [pl.BlockSpec((tb, 128), lambda p, t: (tmap(t), 0))] * 2
    tabs = [pl.BlockSpec((None, 256, 128), lambda p, t: (p, 0, 0))] + \
           [pl.BlockSpec((None, 128, 128), lambda p, t: (p, 0, 0))] * 3
    return qkv, rope, tabs


def _ret_fwd(proj, cos, ss, tabs, gnw, *, name):
    T = proj.shape[0]
    tb = min(T, 1024)
    nt = T // tb
    nchunk = tb // RET_CHUNK

    def body(q_ref, k_ref, v_ref, g_ref, cos_ref, ss_ref, dm_ref, xi_ref, zt_ref, cd_ref, gnw_ref,
             y_ref, o_ref, r_sc):
        @pl.when(pl.program_id(1) == 0)
        def _():
            r_sc[...] = jnp.zeros_like(r_sc)
        m0, m1 = _head_masks((128, 128))
        dm, xi, zt, cd = dm_ref[...], xi_ref[...], zt_ref[...], cd_ref[...]
        bm = (cd > 0).astype(F32)
        gnw = gnw_ref[...]
        for c in range(nchunk):
            rs = pl.ds(c * RET_CHUNK, RET_CHUNK)
            cs, sn = cos_ref[rs, :], ss_ref[rs, :]
            q = _rope(q_ref[rs, :], cs, sn, 32, 64)
            k = _rope(k_ref[rs, :], cs, sn, 32, 64) * K_SCALE
            v = v_ref[rs, :]
            kb, vb = k.astype(BF16), v.astype(BF16)
            qs = jnp.concatenate([q * m0, q * m1], axis=0).astype(BF16)
            s = (_dot_nt(qs, kb) * dm).astype(BF16)
            vs = jnp.concatenate([v * m0, v * m1], axis=0).astype(BF16)
            o = _dot(jnp.concatenate([s[:128], s[128:]], axis=1), vs)
            r = r_sc[...]
            o = o + _dot(q.astype(BF16), r.astype(BF16)) * xi
            r_sc[...] = cd * r + bm * _dot_tn((k * zt).astype(BF16), vb)
            mu = (jnp.sum(o * m0, axis=1, keepdims=True) * m0 + jnp.sum(o * m1, axis=1, keepdims=True) * m1) * (1.0 / 64)
            d = o - mu
            dd = d * d
            var = (jnp.sum(dd * m0, axis=1, keepdims=True) * m0 + jnp.sum(dd * m1, axis=1, keepdims=True) * m1) * (1.0 / 64)
            oh = d * lax.rsqrt(var + EPS)
            g = g_ref[rs, :]
            y_ref[rs, :] = (g * _sigmoid(g) * (oh * gnw)).astype(BF16)
            o_ref[rs, :] = o

    qkv, rope, tspec = _ret_specs(tb, False, nt)
    gspec = pl.BlockSpec((tb, 128), lambda p, t: (t, 12 + p))
    out = pl.BlockSpec((tb, 128), lambda p, t: (t, p))
    return pl.pallas_call(
        body, name=name, grid=(4, nt),
        in_specs=qkv + [gspec] + rope + tspec + [pl.BlockSpec((1, 128), lambda p, t: (0, p))],
        out_specs=[out, out],
        out_shape=[jax.ShapeDtypeStruct((T, RET_WIDTH), BF16), jax.ShapeDtypeStruct((T, RET_WIDTH), F32)],
        scratch_shapes=[pltpu.VMEM((128, 128), F32)],
        compiler_params=_cp(("parallel", "arbitrary")))(proj, proj, proj, proj, cos, ss, *tabs, gnw)


def _ret_bwd_dq(proj, do, cos, ss, tabs, *, name):
    T = proj.shape[0]
    tb = min(T, 1024)
    nt = T // tb
    nchunk = tb // RET_CHUNK

    def body(q_ref, k_ref, v_ref, do_ref, cos_ref, ss_ref, dm_ref, xi_ref, zt_ref, cd_ref, dq_ref, r_sc):
        del q_ref
        @pl.when(pl.program_id(1) == 0)
        def _():
            r_sc[...] = jnp.zeros_like(r_sc)
        m0, m1 = _head_masks((128, 128))
        dm, xi, zt, cd = dm_ref[...], xi_ref[...], zt_ref[...], cd_ref[...]
        bm = (cd > 0).astype(F32)
        for c in range(nchunk):
            rs = pl.ds(c * RET_CHUNK, RET_CHUNK)
            cs, sn = cos_ref[rs, :], ss_ref[rs, :]
            k = _rope(k_ref[rs, :], cs, sn, 32, 64) * K_SCALE
            vb = v_ref[rs, :].astype(BF16)
            dob = do_ref[rs, :]
            dof = dob.astype(F32)
            dos = jnp.concatenate([dof * m0, dof * m1], axis=0).astype(BF16)
            a = (_dot_nt(dos, vb) * dm).astype(BF16)
            ks = jnp.concatenate([k * m0, k * m1], axis=0).astype(BF16)
            r = r_sc[...]
            dq = _dot(jnp.concatenate([a[:128], a[128:]], axis=1), ks) + _dot_nt(dob, r.astype(BF16)) * xi
            r_sc[...] = cd * r + bm * _dot_tn((k * zt).astype(BF16), vb)
            dq_ref[rs, :] = _rope_t(dq, cs, sn, 32, 64).astype(BF16)

    qkv, rope, tspec = _ret_specs(tb, False, nt)
    blk = pl.BlockSpec((tb, 128), lambda p, t: (t, p))
    return pl.pallas_call(
        body, name=name, grid=(4, nt), in_specs=qkv + [blk] + rope + tspec, out_specs=blk,
        out_shape=jax.ShapeDtypeStruct((T, RET_WIDTH), BF16),
        scratch_shapes=[pltpu.VMEM((128, 128), F32)],
        compiler_params=_cp(("parallel", "arbitrary")))(proj, proj, proj, do, cos, ss, *tabs)


def _ret_bwd_dkv(proj, do, cos, ss, tabs, *, name, swap=None):
    T = proj.shape[0]
    tb = min(T, 1024)
    nt = T // tb
    nchunk = tb // RET_CHUNK

    def body(q_ref, k_ref, v_ref, do_ref, cos_ref, ss_ref, dm_ref, xi_ref, zt_ref, cd_ref, *rest):
        if swap is None:
            backward(q_ref, k_ref, v_ref, do_ref, cos_ref, ss_ref, dm_ref, xi_ref, zt_ref, cd_ref, *rest)
        else:
            g_ref, dk_ref, dv_ref, got_ref, u_sc, *sems = rest
            start, finish = _swap_phases(g_ref, got_ref, *sems)
            pl.when((pl.program_id(0) == 0) & (pl.program_id(1) == 0))(start)
            backward(q_ref, k_ref, v_ref, do_ref, cos_ref, ss_ref, dm_ref, xi_ref, zt_ref, cd_ref, dk_ref, dv_ref, u_sc)
            pl.when((pl.program_id(0) == 3) & (pl.program_id(1) == nt - 1))(finish)

    def backward(q_ref, k_ref, v_ref, do_ref, cos_ref, ss_ref, dm_ref, xi_ref, zt_ref, cd_ref, dk_ref, dv_ref, u_sc):
        @pl.when(pl.program_id(1) == 0)
        def _():
            u_sc[...] = jnp.zeros_like(u_sc)
        m0, m1 = _head_masks((128, 128))
        dm, xi, zt, cd = dm_ref[...], xi_ref[...], zt_ref[...], cd_ref[...]
        bm = (cd > 0).astype(F32)
        for c in reversed(range(nchunk)):
            rs = pl.ds(c * RET_CHUNK, RET_CHUNK)
            cs, sn = cos_ref[rs, :], ss_ref[rs, :]
            q = _rope(q_ref[rs, :], cs, sn, 32, 64)
            k = _rope(k_ref[rs, :], cs, sn, 32, 64) * K_SCALE
            kb = k.astype(BF16)
            vb = v_ref[rs, :].astype(BF16)
            dob = do_ref[rs, :]
            dof = dob.astype(F32)
            qs = jnp.concatenate([q * m0, q * m1], axis=0).astype(BF16)
            dos = jnp.concatenate([dof * m0, dof * m1], axis=0).astype(BF16)
            s = (_dot_nt(qs, kb) * dm).astype(BF16)
            a = (_dot_nt(dos, vb) * dm).astype(BF16)
            ub = u_sc[...].astype(BF16)
            dk = _dot_tn(a, qs) + _dot_nt(vb, ub) * zt
            dv = _dot_tn(s, dos) + _dot(kb, ub) * zt
            u_sc[...] = cd * u_sc[...] + bm * _dot_tn((q * xi).astype(BF16), dob)
            dk_ref[rs, :] = (_rope_t(dk, cs, sn, 32, 64) * K_SCALE).astype(BF16)
            dv_ref[rs, :] = dv.astype(BF16)

    qkv, rope, tspec = _ret_specs(tb, True, nt)
    blk = pl.BlockSpec((tb, 128), lambda p, t: (nt - 1 - t, p))
    out_shape = [jax.ShapeDtypeStruct((T, RET_WIDTH), BF16)] * 2
    if swap is None:
        return pl.pallas_call(
            body, name=name, grid=(4, nt), in_specs=qkv + [blk] + rope + tspec, out_specs=[blk, blk],
            out_shape=out_shape, scratch_shapes=[pltpu.VMEM((128, 128), F32)],
            compiler_params=_cp(("parallel", "arbitrary")))(proj, proj, proj, do, cos, ss, *tabs)
    return pl.pallas_call(
        body, name=name, grid=(4, nt), in_specs=qkv + [blk] + rope + tspec + [ANY], out_specs=[blk, blk, ANY],
        out_shape=out_shape + [jax.ShapeDtypeStruct((4,) + swap.shape[2:], swap.dtype)],
        scratch_shapes=[pltpu.VMEM((128, 128), F32)] + list(SWAP_SCRATCH),
        compiler_params=_cp(("arbitrary", "arbitrary")))(proj, proj, proj, do, cos, ss, *tabs, swap)


def _mix_bwd(dmixed, o_ret, proj, y_mla, gnw, *, name):
    T = dmixed.shape[0]
    tm = min(T, 512)

    def body(dm_ref, o_ref, g_ref, ym_ref, gnw_ref, do_ref, dg_ref, dom_ref, dl_ref, dw_ref):
        @pl.when(pl.program_id(0) == 0)
        def _():
            dw_ref[...] = jnp.zeros_like(dw_ref)
        m0, m1 = _head_masks((tm, 128))
        lane = lax.broadcasted_iota(jnp.int32, (tm, 128), 1)
        delta = jnp.zeros((tm, 128), F32)

        def gsum(z):
            return jnp.sum(z * m0, axis=1, keepdims=True) * m0 + jnp.sum(z * m1, axis=1, keepdims=True) * m1

        for p in range(4):
            cs = slice(128 * p, 128 * p + 128)
            dy = dm_ref[:, cs]
            o = o_ref[:, cs]
            g = g_ref[:, cs]
            w = gnw_ref[:, cs]
            d = o - gsum(o) * (1.0 / 64)
            rstd = lax.rsqrt(gsum(d * d) * (1.0 / 64) + EPS)
            oh = d * rstd
            sg = _sigmoid(g)
            dn = dy * (g * sg)
            dg_ref[:, cs] = (dy * (oh * w) * (sg * (1.0 + g * (1.0 - sg)))).astype(BF16)
            dw_ref[:, cs] += jnp.sum(dn * oh, axis=0, keepdims=True)
            doh = dn * w
            do = rstd * (doh - gsum(doh) * (1.0 / 64) - oh * (gsum(doh * oh) * (1.0 / 64)))
            do_ref[:, cs] = do.astype(BF16)
            dom = dm_ref[:, 512 + 128 * p:512 + 128 * p + 128]
            dom_ref[:, cs] = dom.astype(BF16)
            pr = dom * ym_ref[:, cs].astype(F32)
            delta = jnp.where(lane == 2 * p, jnp.sum(pr * m0, axis=1, keepdims=True), delta)
            delta = jnp.where(lane == 2 * p + 1, jnp.sum(pr * m1, axis=1, keepdims=True), delta)
        dl_ref[...] = delta.T[0:MLA_HEADS]

    half = pl.BlockSpec((tm, 512), lambda i: (i, 0))
    return pl.pallas_call(
        body, name=name, grid=(T // tm,),
        in_specs=[pl.BlockSpec((tm, 1024), lambda i: (i, 0)), half, pl.BlockSpec((tm, 512), lambda i: (i, 3)),
                  half, pl.BlockSpec((1, 512), lambda i: (0, 0))],
        out_specs=[half, half, half, pl.BlockSpec((MLA_HEADS, tm), lambda i: (0, i)),
                   pl.BlockSpec((1, 512), lambda i: (0, 0))],
        out_shape=[jax.ShapeDtypeStruct((T, 512), BF16)] * 3 + [jax.ShapeDtypeStruct((MLA_HEADS, T), F32),
                                                                jax.ShapeDtypeStruct((1, 512), F32)],
        compiler_params=_cp(("arbitrary",)))(dmixed, o_ret, proj, y_mla, gnw)


def _mla_prep_fwd(proj, qnw, kvnw, wuq, wk, wv, cos, ss, *, name):
    T = proj.shape[0]
    tm = min(T, 512)

    def body(lat_ref, qnw_ref, kvnw_ref, wuq_ref, wk_ref, wv_ref, cos_ref, ss_ref,
             q_ref, k_ref, v_ref, cqn_ref, ckvn_ref):
        cq = lat_ref[:, 0:256]
        ckv = lat_ref[:, 256:384]
        g3 = lat_ref[:, 384:512]
        cqn = (cq * lax.rsqrt(jnp.mean(cq * cq, axis=-1, keepdims=True) + EPS) * qnw_ref[...]).astype(BF16)
        ckvn = (ckv * lax.rsqrt(jnp.mean(ckv * ckv, axis=-1, keepdims=True) + EPS) * kvnw_ref[...]).astype(BF16)
        cqn_ref[...] = cqn
        ckvn_ref[...] = ckvn
        cs, sn = cos_ref[...], ss_ref[...]
        q = _dot_nt(cqn, wuq_ref[...])
        k = _dot_nt(ckvn, wk_ref[...])
        kpe = _rope(g3, cs, sn, 16, 32)
        for h in range(MLA_HEADS):
            hs = slice(128 * h, 128 * h + 128)
            q_ref[:, hs] = (_rope(q[:, hs], cs, sn, 16, 32) * SCALE).astype(BF16)
            k_ref[:, hs] = (k[:, hs] + kpe).astype(BF16)
        v = _dot_nt(ckvn, wv_ref[...])
        lane = lax.broadcasted_iota(jnp.int32, (tm, 128), 1)
        for p in range(4):
            vp = v[:, 128 * p:128 * p + 128]
            v_ref[:, 256 * p:256 * p + 128] = jnp.where(lane < 64, vp, 1.0).astype(BF16)
            v_ref[:, 256 * p + 128:256 * p + 256] = jnp.where(lane < 64, 1.0, vp).astype(BF16)

    def full(shape):
        return pl.BlockSpec(shape, lambda i: (0, 0))

    def row(w):
        return pl.BlockSpec((tm, w), lambda i: (i, 0))

    return pl.pallas_call(
        body, name=name, grid=(T // tm,),
        in_specs=[pl.BlockSpec((tm, 512), lambda i: (i, 4)), full((1, 256)), full((1, 128)), full((1024, 256)),
                  full((1024, 128)), full((512, 128)), row(128), row(128)],
        out_specs=[row(1024), row(1024), row(1024), row(256), row(128)],
        out_shape=[jax.ShapeDtypeStruct((T, 1024), BF16), jax.ShapeDtypeStruct((T, 1024), BF16),
                   jax.ShapeDtypeStruct((T, 1024), BF16), jax.ShapeDtypeStruct((T, 256), BF16),
                   jax.ShapeDtypeStruct((T, 128), BF16)],
        compiler_params=_cp(("parallel",)))(proj, qnw, kvnw, wuq, wk, wv, cos, ss)


def _mla_prep_bwd(dq, dk, dv, proj, qnw, kvnw, wuq_t, wk_t, wv_t, cos, ss, ret_grads, *, name):
    T = proj.shape[0]
    tm = min(T, 512)

    def body(dq_ref, dk_ref, dv_ref, lat_ref, qnw_ref, kvnw_ref, wuq_ref, wk_ref, wv_ref, cos_ref, ss_ref,
             rq_ref, rk_ref, rv_ref, rg_ref, dproj_ref, dqp_ref, dqnw_ref, dkvnw_ref):
        for j, r in enumerate((rq_ref, rk_ref, rv_ref, rg_ref)):
            dproj_ref[:, 512 * j:512 * j + 512] = r[...]
        dlat_ref = dproj_ref.at[:, 2048:2560]

        @pl.when(pl.program_id(0) == 0)
        def _():
            dqnw_ref[...] = jnp.zeros_like(dqnw_ref)
            dkvnw_ref[...] = jnp.zeros_like(dkvnw_ref)
        cs, sn = cos_ref[...], ss_ref[...]
        dkpe = jnp.zeros((tm, 128), F32)
        for h in range(MLA_HEADS):
            hs = slice(128 * h, 128 * h + 128)
            dqp_ref[:, hs] = _rope_t(dq_ref[:, hs] * SCALE, cs, sn, 16, 32).astype(BF16)
            dkpe = dkpe + dk_ref[:, hs]
        lane = lax.broadcasted_iota(jnp.int32, (tm, 128), 1)
        rope_lane = (lane >= MLA_NOPE) & (lane < MLA_NOPE + MLA_ROPE)
        dg3 = jnp.where(rope_lane, _rope_t(jnp.where(rope_lane, dkpe, 0.0), cs, sn, 16, 32), 0.0)

        def norm_bwd(x, w, dn):
            r = lax.rsqrt(jnp.mean(x * x, axis=-1, keepdims=True) + EPS)
            xh = x * r
            g = dn * w
            return r * (g - xh * jnp.mean(g * xh, axis=-1, keepdims=True)), jnp.sum(dn * xh, axis=0, keepdims=True)

        dcqn = _dot(dqp_ref[...], wuq_ref[...])
        dcq, dqnw = norm_bwd(lat_ref[:, 0:256], qnw_ref[...], dcqn)
        dckvn = _dot(dk_ref[...].astype(BF16), wk_ref[...]) + _dot(dv_ref[...], wv_ref[...])
        dckv, dkvnw = norm_bwd(lat_ref[:, 256:384], kvnw_ref[...], dckvn)
        dqnw_ref[...] += dqnw
        dkvnw_ref[...] += dkvnw
        dlat_ref[:, 0:256] = dcq.astype(BF16)
        dlat_ref[:, 256:384] = dckv.astype(BF16)
        dlat_ref[:, 384:512] = dg3.astype(BF16)

    def full(shape):
        return pl.BlockSpec(shape, lambda i: (0, 0))

    def row(w):
        return pl.BlockSpec((tm, w), lambda i: (i, 0))

    return pl.pallas_call(
        body, name=name, grid=(T // tm,),
        in_specs=[row(1024), row(1024), row(512), pl.BlockSpec((tm, 512), lambda i: (i, 4)), full((1, 256)),
                  full((1, 128)), full((1024, 256)), full((1024, 128)), full((512, 128)), row(128), row(128)]
                 + [row(512)] * 4,
        out_specs=[row(IN_PAD), row(1024), full((1, 256)), full((1, 128))],
        out_shape=[jax.ShapeDtypeStruct((T, IN_PAD), BF16), jax.ShapeDtypeStruct((T, 1024), BF16),
                   jax.ShapeDtypeStruct((1, 256), F32), jax.ShapeDtypeStruct((1, 128), F32)],
        compiler_params=_cp(("arbitrary",)))(dq, dk, dv, proj, qnw, kvnw, wuq_t, wk_t, wv_t, cos, ss, *ret_grads)


def _flash_fwd(q, k, v1, *, name, gather=None):
    T = q.shape[0]
    tq = min(T, 512)
    tk = tq
    nq = T // tq

    def body(q_ref, k_ref, v_ref, *rest):
        if gather is None:
            y_ref, lse_ref = rest
        else:
            x_ref, y_ref, lse_ref, g_ref, *sems = rest
            start, forward, finish = _gather_phases(x_ref, g_ref, *sems)
            pl.when((pl.program_id(0) == 0) & (pl.program_id(1) == 0))(start)
            pl.when((pl.program_id(0) == 1) & (pl.program_id(1) == 0))(forward)
        attend(q_ref, k_ref, v_ref, y_ref, lse_ref)
        if gather is not None:
            pl.when((pl.program_id(0) == 3) & (pl.program_id(1) == nq - 1))(finish)

    def attend(q_ref, k_ref, v_ref, y_ref, lse_ref):
        qi = pl.program_id(1)
        row = lax.broadcasted_iota(jnp.int32, (tq, tk), 0)
        col = lax.broadcasted_iota(jnp.int32, (tq, tk), 1)

        def step(kb, carry, masked):
            ks = pl.ds(pl.multiple_of(kb * tk, tk), tk)
            new = []
            for h in range(2):
                hs = slice(128 * h, 128 * h + 128)
                m, acc = carry[h]
                s = _dot_nt(q_ref[:, hs], k_ref[ks, hs])
                if masked:
                    s = jnp.where(col <= row, s, NEG)
                mn = jnp.maximum(m, jnp.max(s, axis=1, keepdims=True))
                p = jnp.exp((s - mn).astype(BF16))
                acc = jnp.exp(m - mn) * acc + _dot(p, v_ref[ks, hs])
                new.append((mn, acc))
            return tuple(new)

        def unrolled(j, c):
            for u in range(FLASH_UNROLL):
                c = step(FLASH_UNROLL * j + u, c, False)
            return c

        init = (jnp.full((tq, 1), NEG, F32), jnp.zeros((tq, 128), F32))
        carry = lax.fori_loop(0, qi // FLASH_UNROLL, unrolled, (init, init))
        carry = lax.fori_loop(FLASH_UNROLL * (qi // FLASH_UNROLL), qi, lambda kb, c: step(kb, c, False), carry)
        (ma, acca), (mb, accb) = step(qi, carry, True)
        lane = lax.broadcasted_iota(jnp.int32, (tq, 128), 1)
        la, lb = pltpu.roll(acca, 64, 1), pltpu.roll(accb, 64, 1)
        y_ref[...] = jnp.where(lane < 64, acca / la, accb / lb).astype(BF16)
        lse_ref[0, 0] = jnp.broadcast_to(ma + jnp.log(acca[:, 64:65]), (tq, 128)).T[0:1]
        lse_ref[1, 0] = jnp.broadcast_to(mb + jnp.log(accb[:, 0:1]), (tq, 128)).T[0:1]

    in_specs = [pl.BlockSpec((tq, 256), lambda p, i: (i, p)), pl.BlockSpec((T, 256), lambda p, i: (0, p)),
                pl.BlockSpec((T, 256), lambda p, i: (0, p))]
    out_specs = [pl.BlockSpec((tq, 128), lambda p, i: (i, p)), pl.BlockSpec((2, 1, 1, tq), lambda p, i: (p, i, 0, 0))]
    out_shape = [jax.ShapeDtypeStruct((T, MLA_WIDTH), BF16), jax.ShapeDtypeStruct((MLA_HEADS, nq, 1, tq), F32)]
    if gather is None:
        return pl.pallas_call(body, name=name, grid=(4, nq), in_specs=in_specs, out_specs=out_specs,
                              out_shape=out_shape, compiler_params=_cp(("parallel", "arbitrary")))(q, k, v1)
    return pl.pallas_call(
        body, name=name, grid=(4, nq), in_specs=in_specs + [ANY], out_specs=out_specs + [ANY],
        out_shape=out_shape + [jax.ShapeDtypeStruct((N_DEV,) + gather.shape, gather.dtype)],
        scratch_shapes=list(GATHER_SCRATCH),
        compiler_params=_cp(("arbitrary", "arbitrary")))(q, k, v1, gather)


def _flash_bwd(q, k, v, do, lse, delta, *, name, exchange=None):
    T = q.shape[0]
    tq = min(T, 512)
    tk = tq
    nq = T // tq

    def body(q_ref, k_ref, v_ref, do_ref, lse_ref, dl_ref, *rest):
        if exchange is None:
            backward(q_ref, k_ref, v_ref, do_ref, lse_ref, dl_ref, *rest)
        else:
            p_ref, dqt_ref, dk_ref, dv_ref, got_ref, *sems = rest
            start, finish = _exchange_phases(p_ref, got_ref, *sems)
            pl.when((pl.program_id(0) == 0) & (pl.program_id(1) == 0))(start)
            backward(q_ref, k_ref, v_ref, do_ref, lse_ref, dl_ref, dqt_ref, dk_ref, dv_ref)
            pl.when((pl.program_id(0) == 3) & (pl.program_id(1) == nq - 1))(finish)

    def backward(q_ref, k_ref, v_ref, do_ref, lse_ref, dl_ref, dqt_ref, dk_ref, dv_ref):
        kb = pl.program_id(1)

        @pl.when(kb == 0)
        def _():
            dqt_ref[...] = jnp.zeros_like(dqt_ref)
        krow = lax.broadcasted_iota(jnp.int32, (tk, tq), 0)
        qcol = lax.broadcasted_iota(jnp.int32, (tk, tq), 1)
        masks = _head_masks((tk, 128))
        vms = [(v_ref[:, 128 * h:128 * h + 128].astype(F32) * masks[h]).astype(BF16) for h in range(2)]

        def step(qi, carry, masked):
            qs = pl.ds(pl.multiple_of(qi * tq, tq), tq)
            dob = do_ref[qs, :]
            dof = dob.astype(F32)
            dks, dv_acc = list(carry[:2]), carry[2]
            for h in range(2):
                hs = slice(128 * h, 128 * h + 128)
                kh = k_ref[:, hs]
                qh = q_ref[qs, hs]
                st = _dot_nt(kh, qh)
                pt = jnp.exp((st - lse_ref[h, qi]).astype(BF16))
                if masked:
                    pt = jnp.where(krow <= qcol, pt, jnp.zeros_like(pt))
                dv_acc = dv_acc + _dot(pt, (dof * masks[h]).astype(BF16))
                dpt = _dot_nt(vms[h], dob)
                dst = pt * (dpt - dl_ref[h, qi]).astype(BF16)
                dks[h] = dks[h] + _dot(dst, qh)
                dqt_ref[qi, hs, :] += _dot_tn(kh, dst)
            return dks[0], dks[1], dv_acc

        zero = jnp.zeros((tk, 128), F32)
        carry = step(kb, (zero, zero, zero), True)

        def unrolled(j, c):
            for u in range(FLASH_BWD_UNROLL):
                c = step(kb + 1 + FLASH_BWD_UNROLL * j + u, c, False)
            return c

        trips = (nq - 1 - kb) // FLASH_BWD_UNROLL
        carry = lax.fori_loop(0, trips, unrolled, carry)
        dk0, dk1, dv_acc = lax.fori_loop(kb + 1 + FLASH_BWD_UNROLL * trips, nq, lambda qi, c: step(qi, c, False), carry)
        dk_ref[:, 0:128] = dk0
        dk_ref[:, 128:256] = dk1
        dv_ref[...] = dv_acc.astype(BF16)

    stat = pl.BlockSpec((2, nq, 1, tq), lambda p, j: (p, 0, 0, 0))
    in_specs = [pl.BlockSpec((T, 256), lambda p, j: (0, p)), pl.BlockSpec((tk, 256), lambda p, j: (j, p)),
                pl.BlockSpec((tk, 256), lambda p, j: (j, p)), pl.BlockSpec((T, 128), lambda p, j: (0, p)), stat, stat]
    out_specs = [pl.BlockSpec((None, nq, 256, tq), lambda p, j: (p, 0, 0, 0)),
                 pl.BlockSpec((tk, 256), lambda p, j: (j, p)), pl.BlockSpec((tk, 128), lambda p, j: (j, p))]
    out_shape = [jax.ShapeDtypeStruct((4, nq, 256, tq), F32), jax.ShapeDtypeStruct((T, 1024), F32),
                 jax.ShapeDtypeStruct((T, MLA_WIDTH), BF16)]
    if exchange is None:
        return pl.pallas_call(body, name=name, grid=(4, nq), in_specs=in_specs, out_specs=out_specs,
                              out_shape=out_shape,
                              compiler_params=_cp(("parallel", "arbitrary")))(q, k, v, do, lse, delta)
    return pl.pallas_call(
        body, name=name, grid=(4, nq), in_specs=in_specs + [ANY], out_specs=out_specs + [ANY],
        out_shape=out_shape + [jax.ShapeDtypeStruct(exchange.shape, exchange.dtype)],
        scratch_shapes=list(EXCHANGE_SCRATCH),
        compiler_params=_cp(("arbitrary", "arbitrary")))(q, k, v, do, lse, delta, exchange)


def _shift_down(x, n, prev8):
    r = pltpu.roll(x, n, 0)
    row = lax.broadcasted_iota(jnp.int32, prev8.shape, 0)
    first = jnp.where(row < n, pltpu.roll(prev8, n, 0), r[:8])
    if x.shape[0] == 8:
        return first
    return jnp.concatenate([first, r[8:]], axis=0)


def _shift_up(x, n, next8):
    tm = x.shape[0]
    r = pltpu.roll(x, tm - n, 0)
    row = lax.broadcasted_iota(jnp.int32, next8.shape, 0)
    last = jnp.where(row >= 8 - n, pltpu.roll(next8, 8 - n, 0), r[tm - 8:])
    return jnp.concatenate([r[:tm - 8], last], axis=0)


def _conv_pre(u, prev8, cw_ref, cb_ref):
    p1 = _shift_down(u, 1, prev8)
    p2 = _shift_down(u, 2, prev8)
    up = cb_ref[...] + cw_ref[0:1, :] * p2 + cw_ref[1:2, :] * p1 + cw_ref[2:3, :] * u
    return up, p1, p2


def _up_proj_conv(x1, nw, w_up_t, cw, cb, *, name):
    T, K = x1.shape
    tm = min(T, 256)

    def body(x_ref, nw_ref, w_ref, cw_ref, cb_ref, h_ref, u_ref, a_ref, carry_sc):
        @pl.when(pl.program_id(0) == 0)
        def _():
            carry_sc[...] = jnp.zeros_like(carry_sc)
        xv = x_ref[...]
        h = (xv * lax.rsqrt(jnp.mean(xv * xv, axis=-1, keepdims=True) + EPS) * nw_ref[...]).astype(BF16)
        h_ref[...] = h
        for blk in range(2):
            ups = []
            for half in range(2):
                cs = slice((2 * blk + half) * FF_HALF, (2 * blk + half + 1) * FF_HALF)
                u = _dot_nt(h, w_ref[cs, :])
                u_ref[:, cs] = u
                prev = carry_sc[:, cs]
                ups.append(cb_ref[:, cs] + cw_ref[0:1, cs] * _shift_down(u, 2, prev)
                           + cw_ref[1:2, cs] * _shift_down(u, 1, prev) + cw_ref[2:3, cs] * u)
                carry_sc[:, cs] = u[tm - 8:]
            gate, val = ups
            a_ref[:, blk * FF_HALF:(blk + 1) * FF_HALF] = (gate * _sigmoid(gate) * val).astype(BF16)

    def full(shape):
        return pl.BlockSpec(shape, lambda i: (0, 0))

    return pl.pallas_call(
        body, name=name, grid=(T // tm,),
        in_specs=[pl.BlockSpec((tm, K), lambda i: (i, 0)), full(nw.shape), full(w_up_t.shape), full(cw.shape),
                  full(cb.shape)],
        out_specs=[pl.BlockSpec((tm, K), lambda i: (i, 0)), pl.BlockSpec((tm, 2 * D_FF), lambda i: (i, 0)),
                   pl.BlockSpec((tm, D_FF), lambda i: (i, 0))],
        out_shape=[jax.ShapeDtypeStruct((T, K), BF16), jax.ShapeDtypeStruct((T, 2 * D_FF), F32),
                   jax.ShapeDtypeStruct((T, D_FF), BF16)],
        scratch_shapes=[pltpu.VMEM((8, 2 * D_FF), F32)],
        compiler_params=_cp(("arbitrary",)))(x1, nw, w_up_t, cw, cb)


def _conv_bwd(u, da, cw, cb, *, name):
    T = u.shape[0]
    tm = min(T, 512)
    W = 2 * FF_HALF
    nt = T // tm

    def body(u_ref, prev_ref, next_ref, da_ref, dan_ref, cw_ref, cb_ref, du_ref, dw0_ref, dw1_ref, dw2_ref, db_ref):
        i = pl.program_id(1)

        @pl.when(i == 0)
        def _():
            for r in (dw0_ref, dw1_ref, dw2_ref, db_ref):
                r[...] = jnp.zeros_like(r)

        def dpre(u, prev8, da):
            up, p1, p2 = _conv_pre(u, prev8, cw_ref, cb_ref)
            gate, val = up[:, :FF_HALF], up[:, FF_HALF:]
            sg = _sigmoid(gate)
            dgate = da * val * (sg * (1.0 + gate * (1.0 - sg)))
            dval = da * (gate * sg)
            return jnp.concatenate([dgate, dval], axis=1), p1, p2

        u = u_ref[...]
        prev = jnp.where(i > 0, prev_ref[...], 0.0)
        dup, p1, p2 = dpre(u, prev, da_ref[...])
        dupn, _, _ = dpre(next_ref[...], u[tm - 8:], dan_ref[...])
        dupn = jnp.where(i < nt - 1, dupn, 0.0)
        du = cw_ref[2:3, :] * dup + cw_ref[1:2, :] * _shift_up(dup, 1, dupn) + cw_ref[0:1, :] * _shift_up(dup, 2, dupn)
        du_ref[...] = du.astype(BF16)
        dw0_ref[...] += jnp.sum(dup * p2, axis=0, keepdims=True)
        dw1_ref[...] += jnp.sum(dup * p1, axis=0, keepdims=True)
        dw2_ref[...] += jnp.sum(dup * u, axis=0, keepdims=True)
        db_ref[...] += jnp.sum(dup, axis=0, keepdims=True)

    nxt = lambda j, i: (jnp.minimum((i + 1) * (tm // 8), T // 8 - 1), j)
    vec = pl.BlockSpec((1, W), lambda j, i: (0, j))
    return pl.pallas_call(
        body, name=name, grid=(2, nt),
        in_specs=[pl.BlockSpec((tm, W), lambda j, i: (i, j)),
                  pl.BlockSpec((8, W), lambda j, i: (jnp.maximum(i * (tm // 8) - 1, 0), j)),
                  pl.BlockSpec((8, W), nxt),
                  pl.BlockSpec((tm, FF_HALF), lambda j, i: (i, j)), pl.BlockSpec((8, FF_HALF), nxt),
                  pl.BlockSpec((3, W), lambda j, i: (0, j)), vec],
        out_specs=[pl.BlockSpec((tm, W), lambda j, i: (i, j)), vec, vec, vec, vec],
        out_shape=[jax.ShapeDtypeStruct((T, 2 * D_FF), BF16)] + [jax.ShapeDtypeStruct((1, 2 * D_FF), F32)] * 4,
        compiler_params=_cp(("parallel", "arbitrary")))(u, u, u, da, da, cw, cb)


def _sum_chips(slots, *, name):
    ns, R, C = slots.shape
    tr = _row_tile(R)

    def body(g_ref, o_ref):
        g = g_ref[0].astype(F32)
        for s in range(1, ns):
            g = g + g_ref[s].astype(F32)
        o_ref[...] = g

    return pl.pallas_call(
        body, name=name, grid=(R // tr,), in_specs=[pl.BlockSpec((ns, tr, C), lambda i: (0, i, 0))],
        out_specs=pl.BlockSpec((tr, C), lambda i: (i, 0)), out_shape=jax.ShapeDtypeStruct((R, C), F32),
        compiler_params=_cp(("parallel",)))(slots)


def _place():
    return lax.axis_index("x"), lax.axis_index("y"), lax.axis_index("c")


GATHER_SCRATCH = (pltpu.SemaphoreType.DMA((7,)), pltpu.SemaphoreType.DMA((7,)), pltpu.SemaphoreType.DMA)
EXCHANGE_SCRATCH = (pltpu.SemaphoreType.DMA((3,)), pltpu.SemaphoreType.DMA((3,)), pltpu.SemaphoreType.DMA)


def _gather_phases(x_ref, out_ref, send_sems, recv_sems, local_sem):
    x_, y_, c_ = _place()
    me, sibling = (x_, y_, c_), (x_, y_, 1 - c_)
    chips = [(1 - x_, y_), (x_, 1 - y_), (1 - x_, 1 - y_)]

    def slot(px, py, pc):
        return out_ref.at[4 * px + 2 * py + pc]

    def copy(k, block, to, src=None):
        return pltpu.make_async_remote_copy(
            src_ref=slot(*block) if src is None else src, dst_ref=slot(*block),
            send_sem=send_sems.at[k], recv_sem=recv_sems.at[k], device_id=to, device_id_type=MESH)

    def mine():
        return pltpu.make_async_copy(x_ref, slot(*me), local_sem)

    def first():
        return [copy(0, me, sibling, src=x_ref)] + [copy(1 + j, me, (*chip, c_), src=x_ref)
                                                     for j, chip in enumerate(chips)]

    def passed():
        return [copy(4 + j, (*chip, c_), sibling) for j, chip in enumerate(chips)]

    def start():
        mine().start()
        for cp in first():
            cp.start()

    def forward():
        fwd = passed()
        for j, chip in enumerate(chips):
            copy(1 + j, (*chip, c_), me).wait_recv()
            fwd[j].start()

    def finish():
        copy(0, sibling, me).wait_recv()
        for j, chip in enumerate(chips):
            copy(4 + j, (*chip, 1 - c_), me).wait_recv()
        for cp in first() + passed():
            cp.wait_send()
        mine().wait()

    return start, forward, finish


def _exchange_phases(p_ref, out_ref, send_sems, recv_sems, local_sem):
    x_, y_, c_ = _place()
    me_k = 2 * x_ + y_
    chips = [(1 - x_, y_), (x_, 1 - y_), (1 - x_, 1 - y_)]

    def local():
        return pltpu.make_async_copy(p_ref.at[me_k], out_ref.at[me_k], local_sem)

    def copy(j, src_k, dst_k, chip):
        return pltpu.make_async_remote_copy(
            src_ref=p_ref.at[src_k], dst_ref=out_ref.at[dst_k], send_sem=send_sems.at[j],
            recv_sem=recv_sems.at[j], device_id=(*chip, c_), device_id_type=MESH)

    def sends():
        return [copy(j, 2 * px + py, me_k, (px, py)) for j, (px, py) in enumerate(chips)]

    def start():
        local().start()
        for cp in sends():
            cp.start()

    def finish():
        for j, (px, py) in enumerate(chips):
            copy(j, me_k, 2 * px + py, (px, py)).wait_recv()
        for cp in sends():
            cp.wait_send()
        local().wait()

    return start, finish


def _all_gather(x, *, name, in_vmem):
    def body(x_ref, out_ref, send_sems, recv_sems, local_sem):
        for phase in _gather_phases(x_ref, out_ref, send_sems, recv_sems, local_sem):
            phase()

    spec = pl.BlockSpec(memory_space=pltpu.VMEM) if in_vmem else ANY
    return pl.pallas_call(
        body, name=name, out_shape=jax.ShapeDtypeStruct((N_DEV,) + x.shape, x.dtype),
        in_specs=[spec], out_specs=spec, scratch_shapes=list(GATHER_SCRATCH),
        compiler_params=pltpu.CompilerParams(vmem_limit_bytes=VMEM_LIMIT))(x)


def _small_rows():
    table, row = [], 0
    for n, size in SMALL_VECTORS:
        table.append((n, size, row))
        row += -(-size // PACK_COLS)
    return table


def _ff_chunk_source(c):
    block, off = divmod(c * 128, FF_HALF)
    return (0, 2, 1, 3)[block] * FF_HALF + off


def _pack_small(parts, *, name):
    table = _small_rows()

    def body(*refs):
        out = refs[-1]
        out[...] = jnp.zeros_like(out)
        for ref, (n, size, row) in zip(refs, table):
            if size != 2 * D_FF:
                out[row:row + 1, 0:size] = ref[...]
                continue
            for c in range(size // 128):
                src = _ff_chunk_source(c)
                r, lane = divmod(c * 128, PACK_COLS)
                out[row + r:row + r + 1, lane:lane + 128] = ref[:, src:src + 128]

    return pl.pallas_call(body, name=name, out_shape=jax.ShapeDtypeStruct((SMALL_ROWS, PACK_COLS), F32))(
        *[parts[n] for n, _, _ in table])


def _sum_small(g, *, name):
    table = _small_rows()
    shapes = [(n, size) for n, size, _ in table if not n.startswith("conv_w")]
    shapes.insert(7, ("conv_w", 2 * D_FF))

    def body(g_ref, *outs):
        def total(row, width):
            acc = g_ref[0, row:row + 1, 0:width]
            for d in range(1, N_DEV):
                acc = acc + g_ref[d, row:row + 1, 0:width]
            return acc

        out_of = {n: o for (n, _), o in zip(shapes, outs)}
        for n, size, row in table:
            o, j = (out_of["conv_w"], int(n[-1])) if n.startswith("conv_w") else (out_of[n], 0)
            for i in range(-(-size // PACK_COLS)):
                width = min(PACK_COLS, size - PACK_COLS * i)
                o[j:j + 1, PACK_COLS * i:PACK_COLS * i + width] = total(row + i, width)

    out_shape = [jax.ShapeDtypeStruct((3 if n == "conv_w" else 1, size), F32) for n, size in shapes]
    res = pl.pallas_call(body, name=name, out_shape=out_shape)(g)
    return {n: r for (n, _), r in zip(shapes, res)}


def _adamw_multi(ws, ms, vs, gs, *, name):
    k = len(ws)

    def body(*refs):
        w_refs, m_refs, v_refs, g_refs = (refs[i * k:(i + 1) * k] for i in range(4))
        outs = refs[4 * k:]
        for i in range(k):
            g = g_refs[i][...]
            mn = ADAM_B1 * m_refs[i][...] + (1.0 - ADAM_B1) * g
            vn = ADAM_B2 * v_refs[i][...] + (1.0 - ADAM_B2) * (g * g)
            m_hat = mn / (1.0 - ADAM_B1 ** ADAM_STEP)
            v_hat = vn / (1.0 - ADAM_B2 ** ADAM_STEP)
            outs[i][...] = g
            outs[k + i][...] = -ADAM_LR * (m_hat / (jnp.sqrt(v_hat) + ADAM_EPS) + ADAM_WD * w_refs[i][...])
            outs[2 * k + i][...] = mn
            outs[3 * k + i][...] = vn

    out_shape = [jax.ShapeDtypeStruct(w.shape, F32) for _ in range(4) for w in ws]
    res = pl.pallas_call(body, name=name, out_shape=out_shape, compiler_params=_cp())(*ws, *ms, *vs, *gs)
    return [res[i * k:(i + 1) * k] for i in range(4)]


SWAP_SCRATCH = (pltpu.SemaphoreType.DMA((4,)), pltpu.SemaphoreType.DMA((4,)))


def _swap_phases(g_ref, out_ref, send_sems, recv_sems):
    x_, y_, c_ = _place()

    def copies():
        return [pltpu.make_async_remote_copy(src_ref=g_ref.at[k, 1 - c_], dst_ref=out_ref.at[k],
                                             send_sem=send_sems.at[k], recv_sem=recv_sems.at[k],
                                             device_id=(x_, y_, 1 - c_), device_id_type=MESH) for k in range(4)]

    def start():
        for cp in copies():
            cp.start()

    def finish():
        for cp in copies():
            cp.wait()

    return start, finish


def _swap_sibling(g, *, name):
    def body(g_ref, out_ref, send_sems, recv_sems):
        for phase in _swap_phases(g_ref, out_ref, send_sems, recv_sems):
            phase()

    return pl.pallas_call(
        body, name=name, out_shape=jax.ShapeDtypeStruct((4,) + g.shape[2:], g.dtype), in_specs=[ANY], out_specs=ANY,
        scratch_shapes=list(SWAP_SCRATCH))(g)


def _row_tile(R):
    for cand in (256, 400, 200):
        if R % cand == 0:
            return cand
    return R


def _add_own(g, b, *, name, out_dtype):
    n, _, R, C = g.shape
    tr = _row_tile(R)

    def body(c_ref, g_ref, b_ref, o_ref):
        del c_ref
        o_ref[...] = (g_ref[...] + b_ref[...]).astype(out_dtype)

    blk = pl.BlockSpec((None, tr, C), lambda s, i, c: (s, i, 0))
    grid_spec = pltpu.PrefetchScalarGridSpec(
        num_scalar_prefetch=1, grid=(n, R // tr),
        in_specs=[pl.BlockSpec((None, None, tr, C), lambda s, i, c: (s, c[0], i, 0)), blk], out_specs=blk)
    core = jnp.reshape(lax.axis_index("c"), (1,)).astype(jnp.int32)
    return pl.pallas_call(body, name=name, grid_spec=grid_spec, out_shape=jax.ShapeDtypeStruct(b.shape, out_dtype),
                          compiler_params=_cp(("parallel", "parallel")))(core, g, b)


def _pack_local(parts, group):
    table, rows = group
    segs = []
    for n, r, rp, tr in table:
        w = parts[n].T if tr else parts[n]
        segs.append(jnp.pad(w.reshape(r, PACK_COLS), ((0, rp - r), (0, 0))))
    segs.append(jnp.zeros((rows - sum(rp for _, _, rp, _ in table), PACK_COLS), segs[0].dtype))
    return jnp.concatenate(segs, axis=0)


def _unpack_local(packed, like, group):
    out, off = {}, 0
    for n, r, rp, tr in group[0]:
        rows, cols = like[n].shape
        seg = packed[off:off + r]
        out[n] = (seg.reshape(cols, rows).T if tr else seg)[None]
        off += rp
    return out


def _segments(g, group):
    out, off = {}, 0
    for n, r, rp, _ in group[0]:
        out[n] = g[:, off:off + r]
        off += rp
    return out


def _pack_grads(parts, group):
    table, rows = group
    segs = [jnp.pad(parts[n], ((0, 0), (0, rp - parts[n].shape[1]), (0, 0))) for n, _, rp, _ in table]
    segs.append(jnp.zeros((N_DEV, rows - sum(rp for _, _, rp, _ in table), PACK_COLS), F32))
    return jnp.concatenate(segs, axis=1)


def _owner_rows_early(g):
    g_in = jnp.concatenate([g["w_in_t"][:2432], g["w_in_t"][2496:2528]], axis=0).reshape(N_DEV, 308, PACK_COLS)
    g_uq = g["w_uq_t"].reshape(N_DEV, 128, MLA_Q_RANK)[:, :96].reshape(N_DEV, 24, PACK_COLS)
    g_ukv = jnp.concatenate([g["w_k_t"].reshape(N_DEV, 128, MLA_KV_RANK)[:, :64],
                             g["w_v_t"].reshape(N_DEV, 64, MLA_KV_RANK)], axis=1).reshape(N_DEV, 16, PACK_COLS)
    return dict(w_in=g_in, w_uq=g_uq, w_ukv=g_ukv)


def _owner_rows_late(g):
    g_up = g["w_up_t"].reshape(2, 2, 2, 704, PACK_COLS).swapaxes(0, 1).reshape(N_DEV, 704, PACK_COLS)
    return dict(w_out=g["w_out"].reshape(N_DEV, 128, PACK_COLS), w_up=g_up,
                w_down=g["w_down"].reshape(N_DEV, 352, PACK_COLS))


def _reduce_to_pairs(gp, *, name):
    gp = gp.reshape(4, 2, gp.shape[1], PACK_COLS)
    return _add_own(gp, _swap_sibling(gp, name=name + "_swap"), out_dtype=BF16, name=name + "_sum")


def _interleave_ff(w):
    g, v = w[..., :D_FF], w[..., D_FF:]
    return jnp.concatenate([g[..., :FF_HALF], v[..., :FF_HALF], g[..., FF_HALF:], v[..., FF_HALF:]], axis=-1)


def _rope_tables(pos):
    p = pos.astype(F32)[:, None]
    inv_r = ROPE_BASE ** (-jnp.arange(0, RET_HEAD_DIM, 2, dtype=F32) / RET_HEAD_DIM)
    ang = p * jnp.tile(inv_r, 4)
    sign_r = jnp.tile(jnp.concatenate([-jnp.ones((32,), F32), jnp.ones((32,), F32)]), 2)
    cos_r, ss_r = jnp.cos(ang), jnp.sin(ang) * sign_r
    inv_m = ROPE_BASE ** (-jnp.arange(0, MLA_ROPE, 2, dtype=F32) / MLA_ROPE)
    ang = p * jnp.concatenate([jnp.zeros((64,), F32), inv_m, inv_m, jnp.zeros((32,), F32)])
    sign_m = jnp.concatenate([jnp.zeros((64,), F32), -jnp.ones((16,), F32), jnp.ones((16,), F32), jnp.zeros((32,), F32)])
    cos_m, ss_m = jnp.cos(ang), jnp.sin(ang) * sign_m
    return cos_r, ss_r, cos_m, ss_m


def _prep_early(gathered):
    seg = _segments(gathered, EARLY)
    w_in_t = seg["w_in"].reshape(IN_WIDTH, D_MODEL)
    z = lambda n: jnp.zeros((n, D_MODEL), BF16)
    w_in_t = jnp.concatenate([w_in_t[:2432], z(64), w_in_t[2432:2464], z(32)], axis=0)
    w_uq_t = jnp.pad(seg["w_uq"].reshape(MLA_HEADS, 96, MLA_Q_RANK), ((0, 0), (0, 32), (0, 0))).reshape(1024, MLA_Q_RANK)
    ukv = seg["w_ukv"].reshape(MLA_HEADS, 128, MLA_KV_RANK)
    w_k_t = jnp.pad(ukv[:, :64], ((0, 0), (0, 64), (0, 0))).reshape(1024, MLA_KV_RANK)
    w_v_t = ukv[:, 64:].reshape(512, MLA_KV_RANK)
    return dict(w_in_t=w_in_t, w_uq_t=w_uq_t, w_k_t=w_k_t, w_v_t=w_v_t)


def _prep_late(gathered):
    seg = _segments(gathered, LATE)
    w_up_t = seg["w_up"].reshape(2, 2, 2, 704, D_MODEL).swapaxes(0, 1).reshape(2 * D_FF, D_MODEL)
    return dict(w_out=seg["w_out"].reshape(1024, D_MODEL), w_up_t=w_up_t, w_down=seg["w_down"].reshape(D_FF, D_MODEL))


def _local_step(x, pos, tgt, early, sm, late):
    dist = not isinstance(late, dict)
    cos_r, ss_r, cos_m, ss_m = _rope_tables(pos)
    tabs = _ret_tables()

    if dist:
        h, gathered = _rmsnorm_fwd(x, sm["attn_norm_w"], gather=early, name="attn_norm")
        W = _prep_early(gathered)
    else:
        h = _rmsnorm_fwd(x, sm["attn_norm_w"], name="attn_norm")
        W = early
    proj = _mm(h, W["w_in_t"], bt=True, name="in_proj")
    y_ret, o_ret = _ret_fwd(proj, cos_r, ss_r, tabs, sm["ret_gn_w"], name="ret_fwd")
    q, k, v1, cqn, ckvn = _mla_prep_fwd(proj, sm["mla_q_norm_w"], sm["mla_kv_norm_w"], W["w_uq_t"], W["w_k_t"],
                                       W["w_v_t"], cos_m, ss_m, name="mla_prep")
    T = x.shape[0]
    tq = min(T, 512)
    if dist:
        y_mla, lse, gathered = _flash_fwd(q, k, v1, gather=late, name="mla_attn")
        W = {**W, **_prep_late(gathered)}
    else:
        y_mla, lse = _flash_fwd(q, k, v1, name="mla_attn")
        W = {**W, **late}
    mixed = (y_ret, y_mla)
    x1 = _mm(mixed, W["w_out"], add=x, name="out_proj")
    h2, u, a = _up_proj_conv(x1, sm["ffn_norm_w"], W["w_up_t"], sm["conv_w"], sm["conv_b"], name="ffn_norm_up_conv")
    loss, dx2, dx2b, d_final = _down_proj_loss(a, W["w_down"], x1, tgt, sm["final_norm_w"], name="down_proj_loss")

    g = {}
    g["w_down"] = _mm_tn(a, dx2b, name="dw_down")
    da = _mm(dx2b, W["w_down"], bt=True, name="d_act")
    du, dcw0, dcw1, dcw2, dcb = _conv_bwd(u, da, sm["conv_w"], sm["conv_b"], name="conv_bwd")
    g["w_up_t"] = _mm_tn(du, h2, name="dw_up")
    dx1, d_ffn = _mm_norm_bwd(du, W["w_up_t"], x1, sm["ffn_norm_w"], dx2, name="d_h2_ffn_norm_bwd")

    g["w_out"] = _mm_tn(mixed, dx1, name="dw_out")
    dmixed = _mm(dx1, W["w_out"], bt=True, name="d_mixed")
    do_ret, dg, do_mla, delta, d_gn = _mix_bwd(dmixed, o_ret, proj, y_mla, sm["ret_gn_w"], name="mix_bwd")
    drq = _ret_bwd_dq(proj, do_ret, cos_r, ss_r, tabs, name="ret_bwd_dq")
    delta_r = delta.reshape(MLA_HEADS, T // tq, 1, tq)
    if dist:
        gl = _pack_grads(_owner_rows_late(g), LATE).reshape(4, 2, LATE[1], PACK_COLS)
        drk, drv, theirs = _ret_bwd_dkv(proj, do_ret, cos_r, ss_r, tabs, swap=gl, name="ret_bwd_dkv")
        pair = _add_own(gl, theirs, out_dtype=BF16, name="grad_late_sum")
        dqt, dk, dv, slots_late = _flash_bwd(q, k, v1, do_mla, lse, delta_r, exchange=pair, name="mla_attn_bwd")
    else:
        drk, drv = _ret_bwd_dkv(proj, do_ret, cos_r, ss_r, tabs, name="ret_bwd_dkv")
        dqt, dk, dv = _flash_bwd(q, k, v1, do_mla, lse, delta_r, name="mla_attn_bwd")
        slots_late = None
    dq = dqt.transpose(1, 3, 0, 2).reshape(T, MLA_HEADS * 128)
    dproj, dqp, d_qn, d_kvn = _mla_prep_bwd(dq, dk, dv, proj, sm["mla_q_norm_w"], sm["mla_kv_norm_w"], W["w_uq_t"],
                                            W["w_k_t"], W["w_v_t"], cos_m, ss_m, (drq, drk, drv, dg),
                                            name="mla_prep_bwd")
    g["w_uq_t"] = _mm_tn(dqp, cqn, name="dw_uq")
    g["w_k_t"] = _mm_tn(dk, ckvn, name="dw_ukv_k")
    g["w_v_t"] = _mm_tn(dv, ckvn, name="dw_ukv_v")
    g["w_in_t"] = _mm_tn(dproj, h, name="dw_in")
    if dist:
        pair = _reduce_to_pairs(_pack_grads(_owner_rows_early(g), EARLY), name="grad_early")
        grad_x, d_attn, slots_early = _mm_norm_bwd(dproj, W["w_in_t"], x, sm["attn_norm_w"], dx1, exchange=pair,
                                                   name="d_h_attn_norm_bwd")
    else:
        grad_x, d_attn = _mm_norm_bwd(dproj, W["w_in_t"], x, sm["attn_norm_w"], dx1, name="d_h_attn_norm_bwd")
        slots_early = None

    small = dict(attn_norm_w=d_attn, ret_gn_w=d_gn, mla_q_norm_w=d_qn, mla_kv_norm_w=d_kvn, ffn_norm_w=d_ffn,
                 conv_b=dcb, final_norm_w=d_final, conv_w0=dcw0, conv_w1=dcw1, conv_w2=dcw2, loss=loss)
    return loss, grad_x, g, small, slots_early, slots_late


def kernel(x, positions, attn_norm_w, w_in, ret_gn_w, mla_q_norm_w, w_uq, mla_kv_norm_w, w_ukv, w_out, ffn_norm_w, w_up, conv_w, conv_b, w_down, final_norm_w, loss_target, m_attn_norm_w, m_w_in, m_ret_gn_w, m_mla_q_norm_w, m_w_uq, m_mla_kv_norm_w, m_w_ukv, m_w_out, m_ffn_norm_w, m_w_up, m_conv_w, m_conv_b, m_w_down, m_final_norm_w, v_attn_norm_w, v_w_in, v_ret_gn_w, v_mla_q_norm_w, v_w_uq, v_mla_kv_norm_w, v_w_ukv, v_w_out, v_ffn_norm_w, v_w_up, v_conv_w, v_conv_b, v_w_down, v_final_norm_w):
    a = dict(locals())
    x_, y_, c_ = _place()
    dev = 4 * x_ + 2 * y_ + c_

    shard = {n: a[n][0] for n in BIG_NAMES}
    shard16 = {n: w.astype(BF16) for n, w in shard.items()}
    cw_pad = jnp.pad(conv_w[0].reshape(-1), (0, 24 * 128 - 3 * 704)).reshape(24, 128)
    cw_all = _all_gather(cw_pad, name="gather_conv_w", in_vmem=True)
    conv_w_full = cw_all.reshape(N_DEV, -1)[:, :3 * 704].reshape(N_DEV, 3, 704).transpose(1, 0, 2).reshape(3, 2 * D_FF)
    sm = dict(attn_norm_w=attn_norm_w, ret_gn_w=ret_gn_w, mla_q_norm_w=mla_q_norm_w, mla_kv_norm_w=mla_kv_norm_w,
              ffn_norm_w=ffn_norm_w, final_norm_w=final_norm_w.reshape(1, D_MODEL),
              conv_w=_interleave_ff(conv_w_full), conv_b=_interleave_ff(conv_b))

    loss, grad_x, _, gs, slots_early, slots_late = _local_step(
        x[0], positions[0], loss_target[0], _pack_local(shard16, EARLY), sm, _pack_local(shard16, LATE))

    big = [{}, {}, {}, {}]
    for group, slots, tag, calls in ((EARLY, slots_early, "early", (("w_in", "w_uq", "w_ukv"),)),
                                     (LATE, slots_late, "late", (("w_out", "w_down"), ("w_up",)))):
        grads = _unpack_local(_sum_chips(slots, name="grad_sum_" + tag), shard, group)
        for names_c in calls:
            res = _adamw_multi([shard[n] for n in names_c], [a["m_" + n][0] for n in names_c],
                               [a["v_" + n][0] for n in names_c], [grads[n][0] for n in names_c],
                               name="adamw_" + "_".join(names_c))
            for kind in range(4):
                for n, r in zip(names_c, res[kind]):
                    big[kind][n] = r[None]

    packed = _pack_small(gs, name="pack_small_grads")
    tot = _sum_small(_all_gather(packed, name="gather_small_grads", in_vmem=True), name="sum_small_grads")
    loss_out = tot["loss"][0, 0]
    g_cw = lax.dynamic_slice_in_dim(tot["conv_w"], dev * 704, 704, axis=1)

    def rows_of(prefix):
        return [a[prefix + n].reshape(1, size) for n, size in SMALL]

    sml = _adamw_multi(rows_of("") + [conv_w[0]], rows_of("m_") + [m_conv_w[0]], rows_of("v_") + [v_conv_w[0]],
                       [tot[n] for n, _ in SMALL] + [g_cw], name="adamw_small")
    cwo = [kind[-1] for kind in sml]

    def small_of(kind, n):
        return sml[kind][[nm for nm, _ in SMALL].index(n)].reshape(a[n].shape)

    names = ['attn_norm_w', 'w_in', 'ret_gn_w', 'mla_q_norm_w', 'w_uq', 'mla_kv_norm_w', 'w_ukv', 'w_out',
             'ffn_norm_w', 'w_up', 'conv_w', 'conv_b', 'w_down', 'final_norm_w']
    outs = [loss_out, grad_x[None]]
    for kind in range(4):
        for n in names:
            if n == "conv_w":
                outs.append(cwo[kind][None])
            elif n in big[kind]:
                outs.append(big[kind][n])
            else:
                outs.append(small_of(kind, n))
    return tuple(outs)
```

```python
import functools

import numpy as np
import jax
import jax.numpy as jnp
from jax import lax
from jax.experimental import pallas as pl
from jax.experimental.pallas import tpu as pltpu

F32 = jnp.float32
BF16 = jnp.bfloat16
MESH = pl.DeviceIdType.MESH
ANY = pl.BlockSpec(memory_space=pl.ANY)

D_MODEL = 1024
RET_HEADS = 8
RET_HEAD_DIM = 64
RET_WIDTH = 512
RET_CHUNK = 128
MLA_HEADS = 8
MLA_NOPE = 64
MLA_ROPE = 32
MLA_V = 64
MLA_Q_RANK = 256
MLA_KV_RANK = 128
MLA_WIDTH = 512
IN_WIDTH = 2464
IN_PAD = 2560
D_FF = 2816
FF_HALF = 1408
ROPE_BASE = 10000.0
EPS = 1e-6
SCALE = float((MLA_NOPE + MLA_ROPE) ** -0.5)
K_SCALE = 0.125
N_DEV = 8

ADAM_LR = 0.001
ADAM_B1 = 0.9
ADAM_B2 = 0.999
ADAM_EPS = 1e-08
ADAM_WD = 0.01
ADAM_STEP = 10

VMEM_LIMIT = 56 * 1024 * 1024
MM_BUDGET = 40 * 1024 * 1024
NEG = -1e30
FLASH_UNROLL = 4
FLASH_BWD_UNROLL = 3

PACK_COLS = 1024
EARLY = ((("w_in", 308, 320, True), ("w_uq", 24, 32, True), ("w_ukv", 16, 16, True)), 384)
LATE = ((("w_out", 128, 128, False), ("w_up", 704, 704, True), ("w_down", 352, 352, False)), 1200)
BIG_NAMES = ("w_in", "w_uq", "w_ukv", "w_out", "w_up", "w_down")
SMALL = (("attn_norm_w", 1024), ("ret_gn_w", 512), ("mla_q_norm_w", 256), ("mla_kv_norm_w", 128),
         ("ffn_norm_w", 1024), ("conv_b", 5632), ("final_norm_w", 1024))
SMALL_VECTORS = SMALL + (("conv_w0", 5632), ("conv_w1", 5632), ("conv_w2", 5632), ("loss", 128))
SMALL_ROWS = 32


def _cp(sem=None, vmem=VMEM_LIMIT):
    return pltpu.CompilerParams(dimension_semantics=sem, vmem_limit_bytes=vmem)


def _dot(a, b):
    return jnp.dot(a, b, preferred_element_type=F32)


def _dot_nt(a, b):
    return lax.dot_general(a, b, (((1,), (1,)), ((), ())), preferred_element_type=F32)


def _dot_tn(a, b):
    return lax.dot_general(a, b, (((0,), (0,)), ((), ())), preferred_element_type=F32)


def _sigmoid(x):
    return 0.5 * jnp.tanh(0.5 * x) + 0.5


def _partner(x, half, period):
    n = x.shape[-1]
    lane = lax.broadcasted_iota(jnp.int32, x.shape, 1)
    return jnp.where((lane % period) < half, pltpu.roll(x, n - half, 1), pltpu.roll(x, half, 1))


def _rope(x, cos, ss, half, period):
    return x * cos + _partner(x, half, period) * ss


def _rope_t(dy, cos, ss, half, period):
    return dy * cos - _partner(dy, half, period) * ss


def _head_masks(shape):
    lane = lax.broadcasted_iota(jnp.int32, shape, 1)
    m0 = (lane < 64).astype(F32)
    return m0, 1.0 - m0


def _mm(a, b, *, name, add=None, out_dtype=F32, bt=False):
    parts = a if isinstance(a, tuple) else (a,)
    M = parts[0].shape[0]
    K = sum(p.shape[1] for p in parts)
    N = b.shape[0] if bt else b.shape[1]
    osz = jnp.dtype(out_dtype).itemsize
    per_row = 2 * (K * parts[0].dtype.itemsize + N * osz + (N * 4 if add is not None else 0))
    tm = 128
    for cand in (512, 256):
        if M % cand == 0 and cand * per_row + 4 * K * N <= MM_BUDGET:
            tm = cand
            break
    tm = min(tm, M)
    mul = _dot_nt if bt else _dot
    n_a = len(parts)
    n_in = n_a + (1 if add is None else 2)

    def body(*refs):
        av = refs[0][...] if n_a == 1 else jnp.concatenate([r[...] for r in refs[:n_a]], axis=1)
        acc = mul(av.astype(BF16), refs[n_a][...])
        if add is not None:
            acc = refs[n_a + 1][...] + acc
        refs[n_in][...] = acc.astype(out_dtype)

    in_specs = [pl.BlockSpec((tm, p.shape[1]), lambda i: (i, 0)) for p in parts]
    in_specs.append(pl.BlockSpec(b.shape, lambda i: (0, 0)))
    args = [*parts, b]
    if add is not None:
        in_specs.append(pl.BlockSpec((tm, N), lambda i: (i, 0)))
        args.append(add)
    return pl.pallas_call(
        body, name=name, grid=(M // tm,), in_specs=in_specs, out_specs=pl.BlockSpec((tm, N), lambda i: (i, 0)),
        out_shape=jax.ShapeDtypeStruct((M, N), out_dtype), compiler_params=_cp(("parallel",)))(*args)


def _mm_tn(a, b, *, name):
    parts = a if isinstance(a, tuple) else (a,)
    T = parts[0].shape[0]
    M = sum(p.shape[1] for p in parts)
    N = b.shape[1]
    tk = min(T, 512)

    def tile(n):
        for cand in (1408, 1280):
            if n > 1408 and n % cand == 0:
                return cand
        return n

    tm, tn = tile(M), tile(N)
    nk = T // tk
    n_a = len(parts)
    assert n_a == 1 or tm == M

    def body(*refs):
        o_ref = refs[n_a + 1]

        @pl.when(pl.program_id(2) == 0)
        def _():
            o_ref[...] = jnp.zeros_like(o_ref)
        av = refs[0][...] if n_a == 1 else jnp.concatenate([r[...] for r in refs[:n_a]], axis=1)
        o_ref[...] += _dot_tn(av.astype(BF16), refs[n_a][...].astype(BF16))

    if n_a == 1:
        a_specs = [pl.BlockSpec((tk, tm), lambda i, j, k: (k, i))]
    else:
        a_specs = [pl.BlockSpec((tk, p.shape[1]), lambda i, j, k: (k, 0)) for p in parts]
    return pl.pallas_call(
        body, name=name, grid=(M // tm, N // tn, nk),
        in_specs=a_specs + [pl.BlockSpec((tk, tn), lambda i, j, k: (k, j))],
        out_specs=pl.BlockSpec((tm, tn), lambda i, j, k: (i, j)),
        out_shape=jax.ShapeDtypeStruct((M, N), F32),
        compiler_params=_cp(("parallel", "parallel", "arbitrary")))(*parts, b)


def _rmsnorm_fwd(x, w, *, name, gather=None):
    T, D = x.shape
    tm = min(T, 1024)
    n = T // tm

    def body(x_ref, w_ref, *rest):
        if gather is not None:
            s_ref, o_ref, g_ref, *sems = rest
            start, forward, finish = _gather_phases(s_ref, g_ref, *sems)
            pl.when(pl.program_id(0) == 0)(start)
            pl.when(pl.program_id(0) == n // 2)(forward)
        else:
            o_ref, = rest
        xv = x_ref[...]
        r = lax.rsqrt(jnp.mean(xv * xv, axis=-1, keepdims=True) + EPS)
        o_ref[...] = (xv * r * w_ref[...]).astype(BF16)
        if gather is not None:
            pl.when(pl.program_id(0) == n - 1)(finish)

    in_specs = [pl.BlockSpec((tm, D), lambda i: (i, 0)), pl.BlockSpec((1, D), lambda i: (0, 0))]
    out_spec = pl.BlockSpec((tm, D), lambda i: (i, 0))
    out_shape = jax.ShapeDtypeStruct((T, D), BF16)
    if gather is None:
        return pl.pallas_call(body, name=name, grid=(n,), in_specs=in_specs, out_specs=out_spec, out_shape=out_shape,
                              compiler_params=_cp(("parallel",)))(x, w)
    return pl.pallas_call(
        body, name=name, grid=(n,), in_specs=in_specs + [ANY], out_specs=[out_spec, ANY],
        out_shape=[out_shape, jax.ShapeDtypeStruct((N_DEV,) + gather.shape, gather.dtype)],
        scratch_shapes=list(GATHER_SCRATCH), compiler_params=_cp(("arbitrary",)))(x, w, gather)


def _mm_norm_bwd(a, b, x, w, dres, *, name, exchange=None):
    T, K = a.shape
    D = b.shape[1]
    tm = min(T, 256 if K > 4096 else 512)
    n = T // tm

    def body(a_ref, b_ref, x_ref, w_ref, dr_ref, *rest):
        if exchange is None:
            dx_ref, dw_ref = rest
        else:
            p_ref, dx_ref, dw_ref, got_ref, *sems = rest
            start, finish = _exchange_phases(p_ref, got_ref, *sems)
            pl.when(pl.program_id(0) == 0)(start)

        @pl.when(pl.program_id(0) == 0)
        def _():
            dw_ref[...] = jnp.zeros_like(dw_ref)
        dh = _dot(a_ref[...], b_ref[...])
        xv = x_ref[...]
        r = lax.rsqrt(jnp.mean(xv * xv, axis=-1, keepdims=True) + EPS)
        xh = xv * r
        g = dh * w_ref[...]
        dx_ref[...] = dr_ref[...] + r * (g - xh * jnp.mean(g * xh, axis=-1, keepdims=True))
        dw_ref[...] += jnp.sum(dh * xh, axis=0, keepdims=True)
        if exchange is not None:
            pl.when(pl.program_id(0) == n - 1)(finish)

    row = pl.BlockSpec((tm, D), lambda i: (i, 0))
    vec = pl.BlockSpec((1, D), lambda i: (0, 0))
    in_specs = [pl.BlockSpec((tm, K), lambda i: (i, 0)), pl.BlockSpec((K, D), lambda i: (0, 0)), row, vec, row]
    out_shape = [jax.ShapeDtypeStruct((T, D), F32), jax.ShapeDtypeStruct((1, D), F32)]
    if exchange is None:
        return pl.pallas_call(body, name=name, grid=(n,), in_specs=in_specs, out_specs=[row, vec], out_shape=out_shape,
                              compiler_params=_cp(("arbitrary",)))(a, b, x, w, dres)
    return pl.pallas_call(
        body, name=name, grid=(n,), in_specs=in_specs + [ANY], out_specs=[row, vec, ANY],
        out_shape=out_shape + [jax.ShapeDtypeStruct(exchange.shape, exchange.dtype)],
        scratch_shapes=list(EXCHANGE_SCRATCH), compiler_params=_cp(("arbitrary",)))(a, b, x, w, dres, exchange)


def _down_proj_loss(a, w_down, x1, tgt, w, *, name):
    T, D = x1.shape
    K = a.shape[1]
    tm = min(T, 512)

    def body(a_ref, b_ref, x_ref, t_ref, w_ref, loss_ref, dx_ref, dxb_ref, dw_ref):
        @pl.when(pl.program_id(0) == 0)
        def _():
            dw_ref[...] = jnp.zeros_like(dw_ref)
            loss_ref[...] = jnp.zeros_like(loss_ref)
        xv = x_ref[...] + _dot(a_ref[...], b_ref[...])
        wv = w_ref[...]
        r = lax.rsqrt(jnp.mean(xv * xv, axis=-1, keepdims=True) + EPS)
        xh = xv * r
        e = xh * wv - t_ref[...]
        part = 0.5 * jnp.sum(jnp.mean(e * e, axis=-1, keepdims=True), axis=0, keepdims=True)
        loss_ref[...] += jnp.broadcast_to(part, loss_ref.shape)
        dy = e * (1.0 / D)
        g = dy * wv
        dx = r * (g - xh * jnp.mean(g * xh, axis=-1, keepdims=True))
        dx_ref[...] = dx
        dxb_ref[...] = dx.astype(BF16)
        dw_ref[...] += jnp.sum(dy * xh, axis=0, keepdims=True)

    row = pl.BlockSpec((tm, D), lambda i: (i, 0))
    vec = pl.BlockSpec((1, D), lambda i: (0, 0))
    return pl.pallas_call(
        body, name=name, grid=(T // tm,),
        in_specs=[pl.BlockSpec((tm, K), lambda i: (i, 0)), pl.BlockSpec((K, D), lambda i: (0, 0)), row, row, vec],
        out_specs=[pl.BlockSpec((1, 128), lambda i: (0, 0)), row, row, vec],
        out_shape=[jax.ShapeDtypeStruct((1, 128), F32), jax.ShapeDtypeStruct((T, D), F32),
                   jax.ShapeDtypeStruct((T, D), BF16), jax.ShapeDtypeStruct((1, D), F32)],
        compiler_params=_cp(("arbitrary",)))(a, w_down, x1, tgt, w)


def _ret_tables():
    C = RET_CHUNK
    h = jnp.arange(RET_HEADS, dtype=F32)
    log_gamma = jnp.log1p(-jnp.power(2.0, -5.0 - h))
    idx = jnp.arange(C, dtype=F32)
    diff = idx[:, None] - idx[None, :]
    dm = jnp.where(diff >= 0, jnp.exp(log_gamma[:, None, None] * jnp.maximum(diff, 0.0)), 0.0)
    dm = dm.reshape(4, 2 * C, C)
    lane_head = jnp.repeat(jnp.arange(RET_HEADS).reshape(4, 2), 64, axis=1)
    lg = log_gamma[lane_head]
    xi = jnp.exp(lg[:, None, :] * (idx[None, :, None] + 1.0))
    zeta = jnp.exp(lg[:, None, :] * (C - 1.0 - idx[None, :, None]))
    blk = (jnp.arange(128)[:, None] // 64) == (jnp.arange(128)[None, :] // 64)
    cd = jnp.where(blk[None], jnp.exp(lg * C)[:, :, None], 0.0)
    return dm.astype(F32), xi.astype(F32), zeta.astype(F32), cd.astype(F32)


def _ret_specs(tb, rev, nt):
    def tmap(t):
        return (nt - 1 - t) if rev else t
    qkv = [pl.BlockSpec((tb, 128), lambda p, t, o=o: (tmap(t), o + p)) for o in (0, 4, 8)]
    rope = [pl.BlockSpec((tb, 128), lambda p, t: (tmap(t), 0))] * 2
    tabs = [pl.BlockSpec((None, 256, 128), lambda p, t: (p, 0, 0))] + \
           [pl.BlockSpec((None, 128, 128), lambda p, t: (p, 0, 0))] * 3
    return qkv, rope, tabs


def _ret_fwd(proj, cos, ss, tabs, gnw, *, name):
    T = proj.shape[0]
    tb = min(T, 1024)
    nt = T // tb
    nchunk = tb // RET_CHUNK

    def body(q_ref, k_ref, v_ref, g_ref, cos_ref, ss_ref, dm_ref, xi_ref, zt_ref, cd_ref, gnw_ref,
             y_ref, o_ref, r_sc):
        @pl.when(pl.program_id(1) == 0)
        def _():
            r_sc[...] = jnp.zeros_like(r_sc)
        m0, m1 = _head_masks((128, 128))
        dm, xi, zt, cd = dm_ref[...], xi_ref[...], zt_ref[...], cd_ref[...]
        bm = (cd > 0).astype(F32)
        gnw = gnw_ref[...]
        for c in range(nchunk):
            rs = pl.ds(c * RET_CHUNK, RET_CHUNK)
            cs, sn = cos_ref[rs, :], ss_ref[rs, :]
            q = _rope(q_ref[rs, :], cs, sn, 32, 64)
            k = _rope(k_ref[rs, :], cs, sn, 32, 64) * K_SCALE
            v = v_ref[rs, :]
            kb, vb = k.astype(BF16), v.astype(BF16)
            qs = jnp.concatenate([q * m0, q * m1], axis=0).astype(BF16)
            s = (_dot_nt(qs, kb) * dm).astype(BF16)
            vs = jnp.concatenate([v * m0, v * m1], axis=0).astype(BF16)
            o = _dot(jnp.concatenate([s[:128], s[128:]], axis=1), vs)
            r = r_sc[...]
            o = o + _dot(q.astype(BF16), r.astype(BF16)) * xi
            r_sc[...] = cd * r + bm * _dot_tn((k * zt).astype(BF16), vb)
            mu = (jnp.sum(o * m0, axis=1, keepdims=True) * m0 + jnp.sum(o * m1, axis=1, keepdims=True) * m1) * (1.0 / 64)
            d = o - mu
            dd = d * d
            var = (jnp.sum(dd * m0, axis=1, keepdims=True) * m0 + jnp.sum(dd * m1, axis=1, keepdims=True) * m1) * (1.0 / 64)
            oh = d * lax.rsqrt(var + EPS)
            g = g_ref[rs, :]
            y_ref[rs, :] = (g * _sigmoid(g) * (oh * gnw)).astype(BF16)
            o_ref[rs, :] = o

    qkv, rope, tspec = _ret_specs(tb, False, nt)
    gspec = pl.BlockSpec((tb, 128), lambda p, t: (t, 12 + p))
    out = pl.BlockSpec((tb, 128), lambda p, t: (t, p))
    return pl.pallas_call(
        body, name=name, grid=(4, nt),
        in_specs=qkv + [gspec] + rope + tspec + [pl.BlockSpec((1, 128), lambda p, t: (0, p))],
        out_specs=[out, out],
        out_shape=[jax.ShapeDtypeStruct((T, RET_WIDTH), BF16), jax.ShapeDtypeStruct((T, RET_WIDTH), F32)],
        scratch_shapes=[pltpu.VMEM((128, 128), F32)],
        compiler_params=_cp(("parallel", "arbitrary")))(proj, proj, proj, proj, cos, ss, *tabs, gnw)


def _ret_bwd_dq(proj, do, cos, ss, tabs, *, name):
    T = proj.shape[0]
    tb = min(T, 1024)
    nt = T // tb
    nchunk = tb // RET_CHUNK

    def body(q_ref, k_ref, v_ref, do_ref, cos_ref, ss_ref, dm_ref, xi_ref, zt_ref, cd_ref, dq_ref, r_sc):
        del q_ref
        @pl.when(pl.program_id(1) == 0)
        def _():
            r_sc[...] = jnp.zeros_like(r_sc)
        m0, m1 = _head_masks((128, 128))
        dm, xi, zt, cd = dm_ref[...], xi_ref[...], zt_ref[...], cd_ref[...]
        bm = (cd > 0).astype(F32)
        for c in range(nchunk):
            rs = pl.ds(c * RET_CHUNK, RET_CHUNK)
            cs, sn = cos_ref[rs, :], ss_ref[rs, :]
            k = _rope(k_ref[rs, :], cs, sn, 32, 64) * K_SCALE
            vb = v_ref[rs, :].astype(BF16)
            dob = do_ref[rs, :]
            dof = dob.astype(F32)
            dos = jnp.concatenate([dof * m0, dof * m1], axis=0).astype(BF16)
            a = (_dot_nt(dos, vb) * dm).astype(BF16)
            ks = jnp.concatenate([k * m0, k * m1], axis=0).astype(BF16)
            r = r_sc[...]
            dq = _dot(jnp.concatenate([a[:128], a[128:]], axis=1), ks) + _dot_nt(dob, r.astype(BF16)) * xi
            r_sc[...] = cd * r + bm * _dot_tn((k * zt).astype(BF16), vb)
            dq_ref[rs, :] = _rope_t(dq, cs, sn, 32, 64).astype(BF16)

    qkv, rope, tspec = _ret_specs(tb, False, nt)
    blk = pl.BlockSpec((tb, 128), lambda p, t: (t, p))
    return pl.pallas_call(
        body, name=name, grid=(4, nt), in_specs=qkv + [blk] + rope + tspec, out_specs=blk,
        out_shape=jax.ShapeDtypeStruct((T, RET_WIDTH), BF16),
        scratch_shapes=[pltpu.VMEM((128, 128), F32)],
        compiler_params=_cp(("parallel", "arbitrary")))(proj, proj, proj, do, cos, ss, *tabs)


def _ret_bwd_dkv(proj, do, cos, ss, tabs, *, name, swap=None):
    T = proj.shape[0]
    tb = min(T, 1024)
    nt = T // tb
    nchunk = tb // RET_CHUNK

    def body(q_ref, k_ref, v_ref, do_ref, cos_ref, ss_ref, dm_ref, xi_ref, zt_ref, cd_ref, *rest):
        if swap is None:
            backward(q_ref, k_ref, v_ref, do_ref, cos_ref, ss_ref, dm_ref, xi_ref, zt_ref, cd_ref, *rest)
        else:
            g_ref, dk_ref, dv_ref, got_ref, u_sc, *sems = rest
            start, finish = _swap_phases(g_ref, got_ref, *sems)
            pl.when((pl.program_id(0) == 0) & (pl.program_id(1) == 0))(start)
            backward(q_ref, k_ref, v_ref, do_ref, cos_ref, ss_ref, dm_ref, xi_ref, zt_ref, cd_ref, dk_ref, dv_ref, u_sc)
            pl.when((pl.program_id(0) == 3) & (pl.program_id(1) == nt - 1))(finish)

    def backward(q_ref, k_ref, v_ref, do_ref, cos_ref, ss_ref, dm_ref, xi_ref, zt_ref, cd_ref, dk_ref, dv_ref, u_sc):
        @pl.when(pl.program_id(1) == 0)
        def _():
            u_sc[...] = jnp.zeros_like(u_sc)
        m0, m1 = _head_masks((128, 128))
        dm, xi, zt, cd = dm_ref[...], xi_ref[...], zt_ref[...], cd_ref[...]
        bm = (cd > 0).astype(F32)
        for c in reversed(range(nchunk)):
            rs = pl.ds(c * RET_CHUNK, RET_CHUNK)
            cs, sn = cos_ref[rs, :], ss_ref[rs, :]
            q = _rope(q_ref[rs, :], cs, sn, 32, 64)
            k = _rope(k_ref[rs, :], cs, sn, 32, 64) * K_SCALE
            kb = k.astype(BF16)
            vb = v_ref[rs, :].astype(BF16)
            dob = do_ref[rs, :]
            dof = dob.astype(F32)
            qs = jnp.concatenate([q * m0, q * m1], axis=0).astype(BF16)
            dos = jnp.concatenate([dof * m0, dof * m1], axis=0).astype(BF16)
            s = (_dot_nt(qs, kb) * dm).astype(BF16)
            a = (_dot_nt(dos, vb) * dm).astype(BF16)
            ub = u_sc[...].astype(BF16)
            dk = _dot_tn(a, qs) + _dot_nt(vb, ub) * zt
            dv = _dot_tn(s, dos) + _dot(kb, ub) * zt
            u_sc[...] = cd * u_sc[...] + bm * _dot_tn((q * xi).astype(BF16), dob)
            dk_ref[rs, :] = (_rope_t(dk, cs, sn, 32, 64) * K_SCALE).astype(BF16)
            dv_ref[rs, :] = dv.astype(BF16)

    qkv, rope, tspec = _ret_specs(tb, True, nt)
    blk = pl.BlockSpec((tb, 128), lambda p, t: (nt - 1 - t, p))
    out_shape = [jax.ShapeDtypeStruct((T, RET_WIDTH), BF16)] * 2
    if swap is None:
        return pl.pallas_call(
            body, name=name, grid=(4, nt), in_specs=qkv + [blk] + rope + tspec, out_specs=[blk, blk],
            out_shape=out_shape, scratch_shapes=[pltpu.VMEM((128, 128), F32)],
            compiler_params=_cp(("parallel", "arbitrary")))(proj, proj, proj, do, cos, ss, *tabs)
    return pl.pallas_call(
        body, name=name, grid=(4, nt), in_specs=qkv + [blk] + rope + tspec + [ANY], out_specs=[blk, blk, ANY],
        out_shape=out_shape + [jax.ShapeDtypeStruct((4,) + swap.shape[2:], swap.dtype)],
        scratch_shapes=[pltpu.VMEM((128, 128), F32)] + list(SWAP_SCRATCH),
        compiler_params=_cp(("arbitrary", "arbitrary")))(proj, proj, proj, do, cos, ss, *tabs, swap)


def _mix_bwd(dmixed, o_ret, proj, y_mla, gnw, *, name):
    T = dmixed.shape[0]
    tm = min(T, 512)

    def body(dm_ref, o_ref, g_ref, ym_ref, gnw_ref, do_ref, dg_ref, dom_ref, dl_ref, dw_ref):
        @pl.when(pl.program_id(0) == 0)
        def _():
            dw_ref[...] = jnp.zeros_like(dw_ref)
        m0, m1 = _head_masks((tm, 128))
        lane = lax.broadcasted_iota(jnp.int32, (tm, 128), 1)
        delta = jnp.zeros((tm, 128), F32)

        def gsum(z):
            return jnp.sum(z * m0, axis=1, keepdims=True) * m0 + jnp.sum(z * m1, axis=1, keepdims=True) * m1

        for p in range(4):
            cs = slice(128 * p, 128 * p + 128)
            dy = dm_ref[:, cs]
            o = o_ref[:, cs]
            g = g_ref[:, cs]
            w = gnw_ref[:, cs]
            d = o - gsum(o) * (1.0 / 64)
            rstd = lax.rsqrt(gsum(d * d) * (1.0 / 64) + EPS)
            oh = d * rstd
            sg = _sigmoid(g)
            dn = dy * (g * sg)
            dg_ref[:, cs] = (dy * (oh * w) * (sg * (1.0 + g * (1.0 - sg)))).astype(BF16)
            dw_ref[:, cs] += jnp.sum(dn * oh, axis=0, keepdims=True)
            doh = dn * w
            do = rstd * (doh - gsum(doh) * (1.0 / 64) - oh * (gsum(doh * oh) * (1.0 / 64)))
            do_ref[:, cs] = do.astype(BF16)
            dom = dm_ref[:, 512 + 128 * p:512 + 128 * p + 128]
            dom_ref[:, cs] = dom.astype(BF16)
            pr = dom * ym_ref[:, cs].astype(F32)
            delta = jnp.where(lane == 2 * p, jnp.sum(pr * m0, axis=1, keepdims=True), delta)
            delta = jnp.where(lane == 2 * p + 1, jnp.sum(pr * m1, axis=1, keepdims=True), delta)
        dl_ref[...] = delta.T[0:MLA_HEADS]

    half = pl.BlockSpec((tm, 512), lambda i: (i, 0))
    return pl.pallas_call(
        body, name=name, grid=(T // tm,),
        in_specs=[pl.BlockSpec((tm, 1024), lambda i: (i, 0)), half, pl.BlockSpec((tm, 512), lambda i: (i, 3)),
                  half, pl.BlockSpec((1, 512), lambda i: (0, 0))],
        out_specs=[half, half, half, pl.BlockSpec((MLA_HEADS, tm), lambda i: (0, i)),
                   pl.BlockSpec((1, 512), lambda i: (0, 0))],
        out_shape=[jax.ShapeDtypeStruct((T, 512), BF16)] * 3 + [jax.ShapeDtypeStruct((MLA_HEADS, T), F32),
                                                                jax.ShapeDtypeStruct((1, 512), F32)],
        compiler_params=_cp(("arbitrary",)))(dmixed, o_ret, proj, y_mla, gnw)


def _mla_prep_fwd(proj, qnw, kvnw, wuq, wk, wv, cos, ss, *, name):
    T = proj.shape[0]
    tm = min(T, 512)

    def body(lat_ref, qnw_ref, kvnw_ref, wuq_ref, wk_ref, wv_ref, cos_ref, ss_ref,
             q_ref, k_ref, v_ref, cqn_ref, ckvn_ref):
        cq = lat_ref[:, 0:256]
        ckv = lat_ref[:, 256:384]
        g3 = lat_ref[:, 384:512]
        cqn = (cq * lax.rsqrt(jnp.mean(cq * cq, axis=-1, keepdims=True) + EPS) * qnw_ref[...]).astype(BF16)
        ckvn = (ckv * lax.rsqrt(jnp.mean(ckv * ckv, axis=-1, keepdims=True) + EPS) * kvnw_ref[...]).astype(BF16)
        cqn_ref[...] = cqn
        ckvn_ref[...] = ckvn
        cs, sn = cos_ref[...], ss_ref[...]
        q = _dot_nt(cqn, wuq_ref[...])
        k = _dot_nt(ckvn, wk_ref[...])
        kpe = _rope(g3, cs, sn, 16, 32)
        for h in range(MLA_HEADS):
            hs = slice(128 * h, 128 * h + 128)
            q_ref[:, hs] = (_rope(q[:, hs], cs, sn, 16, 32) * SCALE).astype(BF16)
            k_ref[:, hs] = (k[:, hs] + kpe).astype(BF16)
        v = _dot_nt(ckvn, wv_ref[...])
        lane = lax.broadcasted_iota(jnp.int32, (tm, 128), 1)
        for p in range(4):
            vp = v[:, 128 * p:128 * p + 128]
            v_ref[:, 256 * p:256 * p + 128] = jnp.where(lane < 64, vp, 1.0).astype(BF16)
            v_ref[:, 256 * p + 128:256 * p + 256] = jnp.where(lane < 64, 1.0, vp).astype(BF16)

    def full(shape):
        return pl.BlockSpec(shape, lambda i: (0, 0))

    def row(w):
        return pl.BlockSpec((tm, w), lambda i: (i, 0))

    return pl.pallas_call(
        body, name=name, grid=(T // tm,),
        in_specs=[pl.BlockSpec((tm, 512), lambda i: (i, 4)), full((1, 256)), full((1, 128)), full((1024, 256)),
                  full((1024, 128)), full((512, 128)), row(128), row(128)],
        out_specs=[row(1024), row(1024), row(1024), row(256), row(128)],
        out_shape=[jax.ShapeDtypeStruct((T, 1024), BF16), jax.ShapeDtypeStruct((T, 1024), BF16),
                   jax.ShapeDtypeStruct((T, 1024), BF16), jax.ShapeDtypeStruct((T, 256), BF16),
                   jax.ShapeDtypeStruct((T, 128), BF16)],
        compiler_params=_cp(("parallel",)))(proj, qnw, kvnw, wuq, wk, wv, cos, ss)


def _mla_prep_bwd(dq, dk, dv, proj, qnw, kvnw, wuq_t, wk_t, wv_t, cos, ss, ret_grads, cqn, ckvn, *, name):
    T = proj.shape[0]
    tm = min(T, 512)

    def body(dq_ref, dk_ref, dv_ref, lat_ref, qnw_ref, kvnw_ref, wuq_ref, wk_ref, wv_ref, cos_ref, ss_ref,
             rq_ref, rk_ref, rv_ref, rg_ref, cqn_ref, ckvn_ref,
             dproj_ref, gwuq_ref, gwk_ref, gwv_ref, dqnw_ref, dkvnw_ref, dqp_ref):
        for j, r in enumerate((rq_ref, rk_ref, rv_ref, rg_ref)):
            dproj_ref[:, 512 * j:512 * j + 512] = r[...]
        dlat_ref = dproj_ref.at[:, 2048:2560]

        @pl.when(pl.program_id(0) == 0)
        def _():
            for r in (gwuq_ref, gwk_ref, gwv_ref, dqnw_ref, dkvnw_ref):
                r[...] = jnp.zeros_like(r)
        cs, sn = cos_ref[...], ss_ref[...]
        dkpe = jnp.zeros((tm, 128), F32)
        for h in range(MLA_HEADS):
            hs = slice(128 * h, 128 * h + 128)
            dqp_ref[:, hs] = _rope_t(dq_ref[:, hs] * SCALE, cs, sn, 16, 32).astype(BF16)
            dkpe = dkpe + dk_ref[:, hs]
        lane = lax.broadcasted_iota(jnp.int32, (tm, 128), 1)
        rope_lane = (lane >= MLA_NOPE) & (lane < MLA_NOPE + MLA_ROPE)
        dg3 = jnp.where(rope_lane, _rope_t(jnp.where(rope_lane, dkpe, 0.0), cs, sn, 16, 32), 0.0)

        def norm_bwd(x, w, dn):
            r = lax.rsqrt(jnp.mean(x * x, axis=-1, keepdims=True) + EPS)
            xh = x * r
            g = dn * w
            return r * (g - xh * jnp.mean(g * xh, axis=-1, keepdims=True)), jnp.sum(dn * xh, axis=0, keepdims=True)

        dqp = dqp_ref[...]
        dkb = dk_ref[...].astype(BF16)
        dvb = dv_ref[...]
        dcqn = _dot(dqp, wuq_ref[...])
        dcq, dqnw = norm_bwd(lat_ref[:, 0:256], qnw_ref[...], dcqn)
        dckvn = _dot(dkb, wk_ref[...]) + _dot(dvb, wv_ref[...])
        dckv, dkvnw = norm_bwd(lat_ref[:, 256:384], kvnw_ref[...], dckvn)
        gwuq_ref[...] += _dot_tn(dqp, cqn_ref[...])
        gwk_ref[...] += _dot_tn(dkb, ckvn_ref[...])
        gwv_ref[...] += _dot_tn(dvb, ckvn_ref[...])
        dqnw_ref[...] += dqnw
        dkvnw_ref[...] += dkvnw
        dlat_ref[:, 0:256] = dcq.astype(BF16)
        dlat_ref[:, 256:384] = dckv.astype(BF16)
        dlat_ref[:, 384:512] = dg3.astype(BF16)

    def full(shape):
        return pl.BlockSpec(shape, lambda i: (0, 0))

    def row(w):
        return pl.BlockSpec((tm, w), lambda i: (i, 0))

    return pl.pallas_call(
        body, name=name, grid=(T // tm,),
        in_specs=[row(1024), row(1024), row(512), pl.BlockSpec((tm, 512), lambda i: (i, 4)), full((1, 256)),
                  full((1, 128)), full((1024, 256)), full((1024, 128)), full((512, 128)), row(128), row(128)]
                 + [row(512)] * 4 + [row(256), row(128)],
        out_specs=[row(IN_PAD), full((1024, 256)), full((1024, 128)), full((512, 128)), full((1, 256)),
                   full((1, 128))],
        out_shape=[jax.ShapeDtypeStruct((T, IN_PAD), BF16), jax.ShapeDtypeStruct((1024, 256), F32),
                   jax.ShapeDtypeStruct((1024, 128), F32), jax.ShapeDtypeStruct((512, 128), F32),
                   jax.ShapeDtypeStruct((1, 256), F32), jax.ShapeDtypeStruct((1, 128), F32)],
        scratch_shapes=[pltpu.VMEM((tm, 1024), BF16)],
        compiler_params=_cp(("arbitrary",)))(dq, dk, dv, proj, qnw, kvnw, wuq_t, wk_t, wv_t, cos, ss, *ret_grads,
                                             cqn, ckvn)


def _flash_fwd(q, k, v1, *, name, gather=None):
    T = q.shape[0]
    tq = min(T, 512)
    tk = tq
    nq = T // tq

    def body(q_ref, k_ref, v_ref, *rest):
        if gather is None:
            y_ref, lse_ref = rest
        else:
            x_ref, y_ref, lse_ref, g_ref, *sems = rest
            start, forward, finish = _gather_phases(x_ref, g_ref, *sems)
            pl.when((pl.program_id(0) == 0) & (pl.program_id(1) == 0))(start)
            pl.when((pl.program_id(0) == 1) & (pl.program_id(1) == 0))(forward)
        attend(q_ref, k_ref, v_ref, y_ref, lse_ref)
        if gather is not None:
            pl.when((pl.program_id(0) == 3) & (pl.program_id(1) == nq - 1))(finish)

    def attend(q_ref, k_ref, v_ref, y_ref, lse_ref):
        qi = pl.program_id(1)
        row = lax.broadcasted_iota(jnp.int32, (tq, tk), 0)
        col = lax.broadcasted_iota(jnp.int32, (tq, tk), 1)

        def step(kb, carry, masked):
            ks = pl.ds(pl.multiple_of(kb * tk, tk), tk)
            new = []
            for h in range(2):
                hs = slice(128 * h, 128 * h + 128)
                m, acc = carry[h]
                s = _dot_nt(q_ref[:, hs], k_ref[ks, hs])
                if masked:
                    s = jnp.where(col <= row, s, NEG)
                mn = jnp.maximum(m, jnp.max(s, axis=1, keepdims=True))
                p = jnp.exp((s - mn).astype(BF16))
                acc = jnp.exp(m - mn) * acc + _dot(p, v_ref[ks, hs])
                new.append((mn, acc))
            return tuple(new)

        def unrolled(j, c):
            for u in range(FLASH_UNROLL):
                c = step(FLASH_UNROLL * j + u, c, False)
            return c

        init = (jnp.full((tq, 1), NEG, F32), jnp.zeros((tq, 128), F32))
        carry = lax.fori_loop(0, qi // FLASH_UNROLL, unrolled, (init, init))
        carry = lax.fori_loop(FLASH_UNROLL * (qi // FLASH_UNROLL), qi, lambda kb, c: step(kb, c, False), carry)
        (ma, acca), (mb, accb) = step(qi, carry, True)
        lane = lax.broadcasted_iota(jnp.int32, (tq, 128), 1)
        la, lb = pltpu.roll(acca, 64, 1), pltpu.roll(accb, 64, 1)
        y_ref[...] = jnp.where(lane < 64, acca / la, accb / lb).astype(BF16)
        lse_ref[0, 0] = jnp.broadcast_to(ma + jnp.log(acca[:, 64:65]), (tq, 128)).T[0:1]
        lse_ref[1, 0] = jnp.broadcast_to(mb + jnp.log(accb[:, 0:1]), (tq, 128)).T[0:1]

    in_specs = [pl.BlockSpec((tq, 256), lambda p, i: (i, p)), pl.BlockSpec((T, 256), lambda p, i: (0, p)),
                pl.BlockSpec((T, 256), lambda p, i: (0, p))]
    out_specs = [pl.BlockSpec((tq, 128), lambda p, i: (i, p)), pl.BlockSpec((2, 1, 1, tq), lambda p, i: (p, i, 0, 0))]
    out_shape = [jax.ShapeDtypeStruct((T, MLA_WIDTH), BF16), jax.ShapeDtypeStruct((MLA_HEADS, nq, 1, tq), F32)]
    if gather is None:
        return pl.pallas_call(body, name=name, grid=(4, nq), in_specs=in_specs, out_specs=out_specs,
                              out_shape=out_shape, compiler_params=_cp(("parallel", "arbitrary")))(q, k, v1)
    return pl.pallas_call(
        body, name=name, grid=(4, nq), in_specs=in_specs + [ANY], out_specs=out_specs + [ANY],
        out_shape=out_shape + [jax.ShapeDtypeStruct((N_DEV,) + gather.shape, gather.dtype)],
        scratch_shapes=list(GATHER_SCRATCH),
        compiler_params=_cp(("arbitrary", "arbitrary")))(q, k, v1, gather)


def _flash_bwd(q, k, v, do, lse, delta, *, name, exchange=None):
    T = q.shape[0]
    tq = min(T, 512)
    tk = tq
    nq = T // tq

    def body(q_ref, k_ref, v_ref, do_ref, lse_ref, dl_ref, *rest):
        if exchange is None:
            backward(q_ref, k_ref, v_ref, do_ref, lse_ref, dl_ref, *rest)
        else:
            p_ref, dqt_ref, dk_ref, dv_ref, got_ref, *sems = rest
            start, finish = _exchange_phases(p_ref, got_ref, *sems)
            pl.when((pl.program_id(0) == 0) & (pl.program_id(1) == 0))(start)
            backward(q_ref, k_ref, v_ref, do_ref, lse_ref, dl_ref, dqt_ref, dk_ref, dv_ref)
            pl.when((pl.program_id(0) == 3) & (pl.program_id(1) == nq - 1))(finish)

    def backward(q_ref, k_ref, v_ref, do_ref, lse_ref, dl_ref, dqt_ref, dk_ref, dv_ref):
        kb = pl.program_id(1)

        @pl.when(kb == 0)
        def _():
            dqt_ref[...] = jnp.zeros_like(dqt_ref)
        krow = lax.broadcasted_iota(jnp.int32, (tk, tq), 0)
        qcol = lax.broadcasted_iota(jnp.int32, (tk, tq), 1)
        masks = _head_masks((tk, 128))
        vms = [(v_ref[:, 128 * h:128 * h + 128].astype(F32) * masks[h]).astype(BF16) for h in range(2)]

        def step(qi, carry, masked):
            qs = pl.ds(pl.multiple_of(qi * tq, tq), tq)
            dob = do_ref[qs, :]
            dof = dob.astype(F32)
            dks, dv_acc = list(carry[:2]), carry[2]
            for h in range(2):
                hs = slice(128 * h, 128 * h + 128)
                kh = k_ref[:, hs]
                qh = q_ref[qs, hs]
                st = _dot_nt(kh, qh)
                pt = jnp.exp((st - lse_ref[h, qi]).astype(BF16))
                if masked:
                    pt = jnp.where(krow <= qcol, pt, jnp.zeros_like(pt))
                dv_acc = dv_acc + _dot(pt, (dof * masks[h]).astype(BF16))
                dpt = _dot_nt(vms[h], dob)
                dst = pt * (dpt - dl_ref[h, qi]).astype(BF16)
                dks[h] = dks[h] + _dot(dst, qh)
                dqt_ref[qi, hs, :] += _dot_tn(kh, dst)
            return dks[0], dks[1], dv_acc

        zero = jnp.zeros((tk, 128), F32)
        carry = step(kb, (zero, zero, zero), True)

        def unrolled(j, c):
            for u in range(FLASH_BWD_UNROLL):
                c = step(kb + 1 + FLASH_BWD_UNROLL * j + u, c, False)
            return c

        trips = (nq - 1 - kb) // FLASH_BWD_UNROLL
        carry = lax.fori_loop(0, trips, unrolled, carry)
        dk0, dk1, dv_acc = lax.fori_loop(kb + 1 + FLASH_BWD_UNROLL * trips, nq, lambda qi, c: step(qi, c, False), carry)
        dk_ref[:, 0:128] = dk0
        dk_ref[:, 128:256] = dk1
        dv_ref[...] = dv_acc.astype(BF16)

    stat = pl.BlockSpec((2, nq, 1, tq), lambda p, j: (p, 0, 0, 0))
    in_specs = [pl.BlockSpec((T, 256), lambda p, j: (0, p)), pl.BlockSpec((tk, 256), lambda p, j: (j, p)),
                pl.BlockSpec((tk, 256), lambda p, j: (j, p)), pl.BlockSpec((T, 128), lambda p, j: (0, p)), stat, stat]
    out_specs = [pl.BlockSpec((None, nq, 256, tq), lambda p, j: (p, 0, 0, 0)),
                 pl.BlockSpec((tk, 256), lambda p, j: (j, p)), pl.BlockSpec((tk, 128), lambda p, j: (j, p))]
    out_shape = [jax.ShapeDtypeStruct((4, nq, 256, tq), F32), jax.ShapeDtypeStruct((T, 1024), F32),
                 jax.ShapeDtypeStruct((T, MLA_WIDTH), BF16)]
    if exchange is None:
        return pl.pallas_call(body, name=name, grid=(4, nq), in_specs=in_specs, out_specs=out_specs,
                              out_shape=out_shape,
                              compiler_params=_cp(("parallel", "arbitrary")))(q, k, v, do, lse, delta)
    return pl.pallas_call(
        body, name=name, grid=(4, nq), in_specs=in_specs + [ANY], out_specs=out_specs + [ANY],
        out_shape=out_shape + [jax.ShapeDtypeStruct(exchange.shape, exchange.dtype)],
        scratch_shapes=list(EXCHANGE_SCRATCH),
        compiler_params=_cp(("arbitrary", "arbitrary")))(q, k, v, do, lse, delta, exchange)


def _shift_down(x, n, prev8):
    r = pltpu.roll(x, n, 0)
    row = lax.broadcasted_iota(jnp.int32, prev8.shape, 0)
    first = jnp.where(row < n, pltpu.roll(prev8, n, 0), r[:8])
    if x.shape[0] == 8:
        return first
    return jnp.concatenate([first, r[8:]], axis=0)


def _shift_up(x, n, next8):
    tm = x.shape[0]
    r = pltpu.roll(x, tm - n, 0)
    row = lax.broadcasted_iota(jnp.int32, next8.shape, 0)
    last = jnp.where(row >= 8 - n, pltpu.roll(next8, 8 - n, 0), r[tm - 8:])
    return jnp.concatenate([r[:tm - 8], last], axis=0)


def _conv_pre(u, prev8, cw_ref, cb_ref):
    p1 = _shift_down(u, 1, prev8)
    p2 = _shift_down(u, 2, prev8)
    up = cb_ref[...] + cw_ref[0:1, :] * p2 + cw_ref[1:2, :] * p1 + cw_ref[2:3, :] * u
    return up, p1, p2


def _up_proj_conv(x1, nw, w_up_t, cw, cb, *, name):
    T, K = x1.shape
    tm = min(T, 256)

    def body(x_ref, nw_ref, w_ref, cw_ref, cb_ref, h_ref, u_ref, a_ref, carry_sc):
        @pl.when(pl.program_id(0) == 0)
        def _():
            carry_sc[...] = jnp.zeros_like(carry_sc)
        xv = x_ref[...]
        h = (xv * lax.rsqrt(jnp.mean(xv * xv, axis=-1, keepdims=True) + EPS) * nw_ref[...]).astype(BF16)
        h_ref[...] = h
        for blk in range(2):
            ups = []
            for half in range(2):
                cs = slice((2 * blk + half) * FF_HALF, (2 * blk + half + 1) * FF_HALF)
                u = _dot_nt(h, w_ref[cs, :])
                u_ref[:, cs] = u
                prev = carry_sc[:, cs]
                ups.append(cb_ref[:, cs] + cw_ref[0:1, cs] * _shift_down(u, 2, prev)
                           + cw_ref[1:2, cs] * _shift_down(u, 1, prev) + cw_ref[2:3, cs] * u)
                carry_sc[:, cs] = u[tm - 8:]
            gate, val = ups
            a_ref[:, blk * FF_HALF:(blk + 1) * FF_HALF] = (gate * _sigmoid(gate) * val).astype(BF16)

    def full(shape):
        return pl.BlockSpec(shape, lambda i: (0, 0))

    return pl.pallas_call(
        body, name=name, grid=(T // tm,),
        in_specs=[pl.BlockSpec((tm, K), lambda i: (i, 0)), full(nw.shape), full(w_up_t.shape), full(cw.shape),
                  full(cb.shape)],
        out_specs=[pl.BlockSpec((tm, K), lambda i: (i, 0)), pl.BlockSpec((tm, 2 * D_FF), lambda i: (i, 0)),
                   pl.BlockSpec((tm, D_FF), lambda i: (i, 0))],
        out_shape=[jax.ShapeDtypeStruct((T, K), BF16), jax.ShapeDtypeStruct((T, 2 * D_FF), F32),
                   jax.ShapeDtypeStruct((T, D_FF), BF16)],
        scratch_shapes=[pltpu.VMEM((8, 2 * D_FF), F32)],
        compiler_params=_cp(("arbitrary",)))(x1, nw, w_up_t, cw, cb)


def _conv_bwd(u, da, cw, cb, *, name):
    T = u.shape[0]
    tm = min(T, 512)
    W = 2 * FF_HALF
    nt = T // tm

    def body(u_ref, prev_ref, next_ref, da_ref, dan_ref, cw_ref, cb_ref, du_ref, dw0_ref, dw1_ref, dw2_ref, db_ref):
        i = pl.program_id(1)

        @pl.when(i == 0)
        def _():
            for r in (dw0_ref, dw1_ref, dw2_ref, db_ref):
                r[...] = jnp.zeros_like(r)

        def dpre(u, prev8, da):
            up, p1, p2 = _conv_pre(u, prev8, cw_ref, cb_ref)
            gate, val = up[:, :FF_HALF], up[:, FF_HALF:]
            sg = _sigmoid(gate)
            dgate = da * val * (sg * (1.0 + gate * (1.0 - sg)))
            dval = da * (gate * sg)
            return jnp.concatenate([dgate, dval], axis=1), p1, p2

        u = u_ref[...]
        prev = jnp.where(i > 0, prev_ref[...], 0.0)
        dup, p1, p2 = dpre(u, prev, da_ref[...])
        dupn, _, _ = dpre(next_ref[...], u[tm - 8:], dan_ref[...])
        dupn = jnp.where(i < nt - 1, dupn, 0.0)
        du = cw_ref[2:3, :] * dup + cw_ref[1:2, :] * _shift_up(dup, 1, dupn) + cw_ref[0:1, :] * _shift_up(dup, 2, dupn)
        du_ref[...] = du.astype(BF16)
        dw0_ref[...] += jnp.sum(dup * p2, axis=0, keepdims=True)
        dw1_ref[...] += jnp.sum(dup * p1, axis=0, keepdims=True)
        dw2_ref[...] += jnp.sum(dup * u, axis=0, keepdims=True)
        db_ref[...] += jnp.sum(dup, axis=0, keepdims=True)

    nxt = lambda j, i: (jnp.minimum((i + 1) * (tm // 8), T // 8 - 1), j)
    vec = pl.BlockSpec((1, W), lambda j, i: (0, j))
    return pl.pallas_call(
        body, name=name, grid=(2, nt),
        in_specs=[pl.BlockSpec((tm, W), lambda j, i: (i, j)),
                  pl.BlockSpec((8, W), lambda j, i: (jnp.maximum(i * (tm // 8) - 1, 0), j)),
                  pl.BlockSpec((8, W), nxt),
                  pl.BlockSpec((tm, FF_HALF), lambda j, i: (i, j)), pl.BlockSpec((8, FF_HALF), nxt),
                  pl.BlockSpec((3, W), lambda j, i: (0, j)), vec],
        out_specs=[pl.BlockSpec((tm, W), lambda j, i: (i, j)), vec, vec, vec, vec],
        out_shape=[jax.ShapeDtypeStruct((T, 2 * D_FF), BF16)] + [jax.ShapeDtypeStruct((1, 2 * D_FF), F32)] * 4,
        compiler_params=_cp(("parallel", "arbitrary")))(u, u, u, da, da, cw, cb)


def _sum_chips(slots, *, name):
    ns, R, C = slots.shape
    tr = _row_tile(R)

    def body(g_ref, o_ref):
        g = g_ref[0].astype(F32)
        for s in range(1, ns):
            g = g + g_ref[s].astype(F32)
        o_ref[...] = g

    return pl.pallas_call(
        body, name=name, grid=(R // tr,), in_specs=[pl.BlockSpec((ns, tr, C), lambda i: (0, i, 0))],
        out_specs=pl.BlockSpec((tr, C), lambda i: (i, 0)), out_shape=jax.ShapeDtypeStruct((R, C), F32),
        compiler_params=_cp(("parallel",)))(slots)


def _place():
    return lax.axis_index("x"), lax.axis_index("y"), lax.axis_index("c")


GATHER_SCRATCH = (pltpu.SemaphoreType.DMA((7,)), pltpu.SemaphoreType.DMA((7,)), pltpu.SemaphoreType.DMA)
EXCHANGE_SCRATCH = (pltpu.SemaphoreType.DMA((3,)), pltpu.SemaphoreType.DMA((3,)), pltpu.SemaphoreType.DMA)


def _gather_phases(x_ref, out_ref, send_sems, recv_sems, local_sem):
    x_, y_, c_ = _place()
    me, sibling = (x_, y_, c_), (x_, y_, 1 - c_)
    chips = [(1 - x_, y_), (x_, 1 - y_), (1 - x_, 1 - y_)]

    def slot(px, py, pc):
        return out_ref.at[4 * px + 2 * py + pc]

    def copy(k, block, to, src=None):
        return pltpu.make_async_remote_copy(
            src_ref=slot(*block) if src is None else src, dst_ref=slot(*block),
            send_sem=send_sems.at[k], recv_sem=recv_sems.at[k], device_id=to, device_id_type=MESH)

    def mine():
        return pltpu.make_async_copy(x_ref, slot(*me), local_sem)

    def first():
        return [copy(0, me, sibling, src=x_ref)] + [copy(1 + j, me, (*chip, c_), src=x_ref)
                                                     for j, chip in enumerate(chips)]

    def passed():
        return [copy(4 + j, (*chip, c_), sibling) for j, chip in enumerate(chips)]

    def start():
        mine().start()
        for cp in first():
            cp.start()

    def forward():
        fwd = passed()
        for j, chip in enumerate(chips):
            copy(1 + j, (*chip, c_), me).wait_recv()
            fwd[j].start()

    def finish():
        copy(0, sibling, me).wait_recv()
        for j, chip in enumerate(chips):
            copy(4 + j, (*chip, 1 - c_), me).wait_recv()
        for cp in first() + passed():
            cp.wait_send()
        mine().wait()

    return start, forward, finish


def _exchange_phases(p_ref, out_ref, send_sems, recv_sems, local_sem):
    x_, y_, c_ = _place()
    me_k = 2 * x_ + y_
    chips = [(1 - x_, y_), (x_, 1 - y_), (1 - x_, 1 - y_)]

    def local():
        return pltpu.make_async_copy(p_ref.at[me_k], out_ref.at[me_k], local_sem)

    def copy(j, src_k, dst_k, chip):
        return pltpu.make_async_remote_copy(
            src_ref=p_ref.at[src_k], dst_ref=out_ref.at[dst_k], send_sem=send_sems.at[j],
            recv_sem=recv_sems.at[j], device_id=(*chip, c_), device_id_type=MESH)

    def sends():
        return [copy(j, 2 * px + py, me_k, (px, py)) for j, (px, py) in enumerate(chips)]

    def start():
        local().start()
        for cp in sends():
            cp.start()

    def finish():
        for j, (px, py) in enumerate(chips):
            copy(j, me_k, 2 * px + py, (px, py)).wait_recv()
        for cp in sends():
            cp.wait_send()
        local().wait()

    return start, finish


def _all_gather(x, *, name, in_vmem):
    def body(x_ref, out_ref, send_sems, recv_sems, local_sem):
        for phase in _gather_phases(x_ref, out_ref, send_sems, recv_sems, local_sem):
            phase()

    spec = pl.BlockSpec(memory_space=pltpu.VMEM) if in_vmem else ANY
    return pl.pallas_call(
        body, name=name, out_shape=jax.ShapeDtypeStruct((N_DEV,) + x.shape, x.dtype),
        in_specs=[spec], out_specs=spec, scratch_shapes=list(GATHER_SCRATCH),
        compiler_params=pltpu.CompilerParams(vmem_limit_bytes=VMEM_LIMIT))(x)


def _small_rows():
    table, row = [], 0
    for n, size in SMALL_VECTORS:
        table.append((n, size, row))
        row += -(-size // PACK_COLS)
    return table


def _ff_chunk_source(c):
    block, off = divmod(c * 128, FF_HALF)
    return (0, 2, 1, 3)[block] * FF_HALF + off


def _pack_small(parts, *, name):
    table = _small_rows()

    def body(*refs):
        out = refs[-1]
        out[...] = jnp.zeros_like(out)
        for ref, (n, size, row) in zip(refs, table):
            if size != 2 * D_FF:
                out[row:row + 1, 0:size] = ref[...]
                continue
            for c in range(size // 128):
                src = _ff_chunk_source(c)
                r, lane = divmod(c * 128, PACK_COLS)
                out[row + r:row + r + 1, lane:lane + 128] = ref[:, src:src + 128]

    return pl.pallas_call(body, name=name, out_shape=jax.ShapeDtypeStruct((SMALL_ROWS, PACK_COLS), F32))(
        *[parts[n] for n, _, _ in table])


def _sum_small(g, *, name):
    table = _small_rows()
    shapes = [(n, size) for n, size, _ in table if not n.startswith("conv_w")]
    shapes.insert(7, ("conv_w", 2 * D_FF))

    def body(g_ref, *outs):
        def total(row, width):
            acc = g_ref[0, row:row + 1, 0:width]
            for d in range(1, N_DEV):
                acc = acc + g_ref[d, row:row + 1, 0:width]
            return acc

        out_of = {n: o for (n, _), o in zip(shapes, outs)}
        for n, size, row in table:
            o, j = (out_of["conv_w"], int(n[-1])) if n.startswith("conv_w") else (out_of[n], 0)
            for i in range(-(-size // PACK_COLS)):
                width = min(PACK_COLS, size - PACK_COLS * i)
                o[j:j + 1, PACK_COLS * i:PACK_COLS * i + width] = total(row + i, width)

    out_shape = [jax.ShapeDtypeStruct((3 if n == "conv_w" else 1, size), F32) for n, size in shapes]
    res = pl.pallas_call(body, name=name, out_shape=out_shape)(g)
    return {n: r for (n, _), r in zip(shapes, res)}


def _adamw_multi(ws, ms, vs, gs, *, name):
    k = len(ws)

    def body(*refs):
        w_refs, m_refs, v_refs, g_refs = (refs[i * k:(i + 1) * k] for i in range(4))
        outs = refs[4 * k:]
        for i in range(k):
            g = g_refs[i][...]
            mn = ADAM_B1 * m_refs[i][...] + (1.0 - ADAM_B1) * g
            vn = ADAM_B2 * v_refs[i][...] + (1.0 - ADAM_B2) * (g * g)
            m_hat = mn / (1.0 - ADAM_B1 ** ADAM_STEP)
            v_hat = vn / (1.0 - ADAM_B2 ** ADAM_STEP)
            outs[i][...] = g
            outs[k + i][...] = -ADAM_LR * (m_hat / (jnp.sqrt(v_hat) + ADAM_EPS) + ADAM_WD * w_refs[i][...])
            outs[2 * k + i][...] = mn
            outs[3 * k + i][...] = vn

    out_shape = [jax.ShapeDtypeStruct(w.shape, F32) for _ in range(4) for w in ws]
    res = pl.pallas_call(body, name=name, out_shape=out_shape, compiler_params=_cp())(*ws, *ms, *vs, *gs)
    return [res[i * k:(i + 1) * k] for i in range(4)]


SWAP_SCRATCH = (pltpu.SemaphoreType.DMA((4,)), pltpu.SemaphoreType.DMA((4,)))


def _swap_phases(g_ref, out_ref, send_sems, recv_sems):
    x_, y_, c_ = _place()

    def copies():
        return [pltpu.make_async_remote_copy(src_ref=g_ref.at[k, 1 - c_], dst_ref=out_ref.at[k],
                                             send_sem=send_sems.at[k], recv_sem=recv_sems.at[k],
                                             device_id=(x_, y_, 1 - c_), device_id_type=MESH) for k in range(4)]

    def start():
        for cp in copies():
            cp.start()

    def finish():
        for cp in copies():
            cp.wait()

    return start, finish


def _swap_sibling(g, *, name):
    def body(g_ref, out_ref, send_sems, recv_sems):
        for phase in _swap_phases(g_ref, out_ref, send_sems, recv_sems):
            phase()

    return pl.pallas_call(
        body, name=name, out_shape=jax.ShapeDtypeStruct((4,) + g.shape[2:], g.dtype), in_specs=[ANY], out_specs=ANY,
        scratch_shapes=list(SWAP_SCRATCH))(g)


def _row_tile(R):
    for cand in (256, 400, 200):
        if R % cand == 0:
            return cand
    return R


def _add_own(g, b, *, name, out_dtype):
    n, _, R, C = g.shape
    tr = _row_tile(R)

    def body(c_ref, g_ref, b_ref, o_ref):
        del c_ref
        o_ref[...] = (g_ref[...] + b_ref[...]).astype(out_dtype)

    blk = pl.BlockSpec((None, tr, C), lambda s, i, c: (s, i, 0))
    grid_spec = pltpu.PrefetchScalarGridSpec(
        num_scalar_prefetch=1, grid=(n, R // tr),
        in_specs=[pl.BlockSpec((None, None, tr, C), lambda s, i, c: (s, c[0], i, 0)), blk], out_specs=blk)
    core = jnp.reshape(lax.axis_index("c"), (1,)).astype(jnp.int32)
    return pl.pallas_call(body, name=name, grid_spec=grid_spec, out_shape=jax.ShapeDtypeStruct(b.shape, out_dtype),
                          compiler_params=_cp(("parallel", "parallel")))(core, g, b)


def _pack_local(parts, group):
    table, rows = group
    segs = []
    for n, r, rp, tr in table:
        w = parts[n].T if tr else parts[n]
        segs.append(jnp.pad(w.reshape(r, PACK_COLS), ((0, rp - r), (0, 0))))
    segs.append(jnp.zeros((rows - sum(rp for _, _, rp, _ in table), PACK_COLS), segs[0].dtype))
    return jnp.concatenate(segs, axis=0)


def _unpack_local(packed, like, group):
    out, off = {}, 0
    for n, r, rp, tr in group[0]:
        rows, cols = like[n].shape
        seg = packed[off:off + r]
        out[n] = (seg.reshape(cols, rows).T if tr else seg)[None]
        off += rp
    return out


def _segments(g, group):
    out, off = {}, 0
    for n, r, rp, _ in group[0]:
        out[n] = g[:, off:off + r]
        off += rp
    return out


def _pack_grads(parts, group):
    table, rows = group
    segs = [jnp.pad(parts[n], ((0, 0), (0, rp - parts[n].shape[1]), (0, 0))) for n, _, rp, _ in table]
    segs.append(jnp.zeros((N_DEV, rows - sum(rp for _, _, rp, _ in table), PACK_COLS), F32))
    return jnp.concatenate(segs, axis=1)


def _owner_rows_early(g):
    g_in = jnp.concatenate([g["w_in_t"][:2432], g["w_in_t"][2496:2528]], axis=0).reshape(N_DEV, 308, PACK_COLS)
    g_uq = g["w_uq_t"].reshape(N_DEV, 128, MLA_Q_RANK)[:, :96].reshape(N_DEV, 24, PACK_COLS)
    g_ukv = jnp.concatenate([g["w_k_t"].reshape(N_DEV, 128, MLA_KV_RANK)[:, :64],
                             g["w_v_t"].reshape(N_DEV, 64, MLA_KV_RANK)], axis=1).reshape(N_DEV, 16, PACK_COLS)
    return dict(w_in=g_in, w_uq=g_uq, w_ukv=g_ukv)


def _owner_rows_late(g):
    g_up = g["w_up_t"].reshape(2, 2, 2, 704, PACK_COLS).swapaxes(0, 1).reshape(N_DEV, 704, PACK_COLS)
    return dict(w_out=g["w_out"].reshape(N_DEV, 128, PACK_COLS), w_up=g_up,
                w_down=g["w_down"].reshape(N_DEV, 352, PACK_COLS))


def _reduce_to_pairs(gp, *, name):
    gp = gp.reshape(4, 2, gp.shape[1], PACK_COLS)
    return _add_own(gp, _swap_sibling(gp, name=name + "_swap"), out_dtype=BF16, name=name + "_sum")


def _interleave_ff(w):
    g, v = w[..., :D_FF], w[..., D_FF:]
    return jnp.concatenate([g[..., :FF_HALF], v[..., :FF_HALF], g[..., FF_HALF:], v[..., FF_HALF:]], axis=-1)


def _rope_tables(pos):
    p = pos.astype(F32)[:, None]
    inv_r = ROPE_BASE ** (-jnp.arange(0, RET_HEAD_DIM, 2, dtype=F32) / RET_HEAD_DIM)
    ang = p * jnp.tile(inv_r, 4)
    sign_r = jnp.tile(jnp.concatenate([-jnp.ones((32,), F32), jnp.ones((32,), F32)]), 2)
    cos_r, ss_r = jnp.cos(ang), jnp.sin(ang) * sign_r
    inv_m = ROPE_BASE ** (-jnp.arange(0, MLA_ROPE, 2, dtype=F32) / MLA_ROPE)
    ang = p * jnp.concatenate([jnp.zeros((64,), F32), inv_m, inv_m, jnp.zeros((32,), F32)])
    sign_m = jnp.concatenate([jnp.zeros((64,), F32), -jnp.ones((16,), F32), jnp.ones((16,), F32), jnp.zeros((32,), F32)])
    cos_m, ss_m = jnp.cos(ang), jnp.sin(ang) * sign_m
    return cos_r, ss_r, cos_m, ss_m


def _prep_early(gathered):
    seg = _segments(gathered, EARLY)
    w_in_t = seg["w_in"].reshape(IN_WIDTH, D_MODEL)
    z = lambda n: jnp.zeros((n, D_MODEL), BF16)
    w_in_t = jnp.concatenate([w_in_t[:2432], z(64), w_in_t[2432:2464], z(32)], axis=0)
    w_uq_t = jnp.pad(seg["w_uq"].reshape(MLA_HEADS, 96, MLA_Q_RANK), ((0, 0), (0, 32), (0, 0))).reshape(1024, MLA_Q_RANK)
    ukv = seg["w_ukv"].reshape(MLA_HEADS, 128, MLA_KV_RANK)
    w_k_t = jnp.pad(ukv[:, :64], ((0, 0), (0, 64), (0, 0))).reshape(1024, MLA_KV_RANK)
    w_v_t = ukv[:, 64:].reshape(512, MLA_KV_RANK)
    return dict(w_in_t=w_in_t, w_uq_t=w_uq_t, w_k_t=w_k_t, w_v_t=w_v_t)


def _prep_late(gathered):
    seg = _segments(gathered, LATE)
    w_up_t = seg["w_up"].reshape(2, 2, 2, 704, D_MODEL).swapaxes(0, 1).reshape(2 * D_FF, D_MODEL)
    return dict(w_out=seg["w_out"].reshape(1024, D_MODEL), w_up_t=w_up_t, w_down=seg["w_down"].reshape(D_FF, D_MODEL))


def _local_step(x, pos, tgt, early, sm, late):
    dist = not isinstance(late, dict)
    cos_r, ss_r, cos_m, ss_m = _rope_tables(pos)
    tabs = _ret_tables()

    if dist:
        h, gathered = _rmsnorm_fwd(x, sm["attn_norm_w"], gather=early, name="attn_norm")
        W = _prep_early(gathered)
    else:
        h = _rmsnorm_fwd(x, sm["attn_norm_w"], name="attn_norm")
        W = early
    proj = _mm(h, W["w_in_t"], bt=True, name="in_proj")
    y_ret, o_ret = _ret_fwd(proj, cos_r, ss_r, tabs, sm["ret_gn_w"], name="ret_fwd")
    q, k, v1, cqn, ckvn = _mla_prep_fwd(proj, sm["mla_q_norm_w"], sm["mla_kv_norm_w"], W["w_uq_t"], W["w_k_t"],
                                       W["w_v_t"], cos_m, ss_m, name="mla_prep")
    T = x.shape[0]
    tq = min(T, 512)
    if dist:
        y_mla, lse, gathered = _flash_fwd(q, k, v1, gather=late, name="mla_attn")
        W = {**W, **_prep_late(gathered)}
    else:
        y_mla, lse = _flash_fwd(q, k, v1, name="mla_attn")
        W = {**W, **late}
    mixed = (y_ret, y_mla)
    x1 = _mm(mixed, W["w_out"], add=x, name="out_proj")
    h2, u, a = _up_proj_conv(x1, sm["ffn_norm_w"], W["w_up_t"], sm["conv_w"], sm["conv_b"], name="ffn_norm_up_conv")
    loss, dx2, dx2b, d_final = _down_proj_loss(a, W["w_down"], x1, tgt, sm["final_norm_w"], name="down_proj_loss")

    g = {}
    g["w_down"] = _mm_tn(a, dx2b, name="dw_down")
    da = _mm(dx2b, W["w_down"], bt=True, name="d_act")
    du, dcw0, dcw1, dcw2, dcb = _conv_bwd(u, da, sm["conv_w"], sm["conv_b"], name="conv_bwd")
    g["w_up_t"] = _mm_tn(du, h2, name="dw_up")
    dx1, d_ffn = _mm_norm_bwd(du, W["w_up_t"], x1, sm["ffn_norm_w"], dx2, name="d_h2_ffn_norm_bwd")

    g["w_out"] = _mm_tn(mixed, dx1, name="dw_out")
    dmixed = _mm(dx1, W["w_out"], bt=True, name="d_mixed")
    do_ret, dg, do_mla, delta, d_gn = _mix_bwd(dmixed, o_ret, proj, y_mla, sm["ret_gn_w"], name="mix_bwd")
    drq = _ret_bwd_dq(proj, do_ret, cos_r, ss_r, tabs, name="ret_bwd_dq")
    delta_r = delta.reshape(MLA_HEADS, T // tq, 1, tq)
    if dist:
        gl = _pack_grads(_owner_rows_late(g), LATE).reshape(4, 2, LATE[1], PACK_COLS)
        drk, drv, theirs = _ret_bwd_dkv(proj, do_ret, cos_r, ss_r, tabs, swap=gl, name="ret_bwd_dkv")
        pair = _add_own(gl, theirs, out_dtype=BF16, name="grad_late_sum")
        dqt, dk, dv, slots_late = _flash_bwd(q, k, v1, do_mla, lse, delta_r, exchange=pair, name="mla_attn_bwd")
    else:
        drk, drv = _ret_bwd_dkv(proj, do_ret, cos_r, ss_r, tabs, name="ret_bwd_dkv")
        dqt, dk, dv = _flash_bwd(q, k, v1, do_mla, lse, delta_r, name="mla_attn_bwd")
        slots_late = None
    dq = dqt.transpose(1, 3, 0, 2).reshape(T, MLA_HEADS * 128)
    dproj, g["w_uq_t"], g["w_k_t"], g["w_v_t"], d_qn, d_kvn = _mla_prep_bwd(
        dq, dk, dv, proj, sm["mla_q_norm_w"], sm["mla_kv_norm_w"], W["w_uq_t"], W["w_k_t"], W["w_v_t"], cos_m, ss_m,
        (drq, drk, drv, dg), cqn, ckvn, name="mla_prep_bwd")
    g["w_in_t"] = _mm_tn(dproj, h, name="dw_in")
    if dist:
        pair = _reduce_to_pairs(_pack_grads(_owner_rows_early(g), EARLY), name="grad_early")
        grad_x, d_attn, slots_early = _mm_norm_bwd(dproj, W["w_in_t"], x, sm["attn_norm_w"], dx1, exchange=pair,
                                                   name="d_h_attn_norm_bwd")
    else:
        grad_x, d_attn = _mm_norm_bwd(dproj, W["w_in_t"], x, sm["attn_norm_w"], dx1, name="d_h_attn_norm_bwd")
        slots_early = None

    small = dict(attn_norm_w=d_attn, ret_gn_w=d_gn, mla_q_norm_w=d_qn, mla_kv_norm_w=d_kvn, ffn_norm_w=d_ffn,
                 conv_b=dcb, final_norm_w=d_final, conv_w0=dcw0, conv_w1=dcw1, conv_w2=dcw2, loss=loss)
    return loss, grad_x, g, small, slots_early, slots_late


def kernel(x, positions, attn_norm_w, w_in, ret_gn_w, mla_q_norm_w, w_uq, mla_kv_norm_w, w_ukv, w_out, ffn_norm_w, w_up, conv_w, conv_b, w_down, final_norm_w, loss_target, m_attn_norm_w, m_w_in, m_ret_gn_w, m_mla_q_norm_w, m_w_uq, m_mla_kv_norm_w, m_w_ukv, m_w_out, m_ffn_norm_w, m_w_up, m_conv_w, m_conv_b, m_w_down, m_final_norm_w, v_attn_norm_w, v_w_in, v_ret_gn_w, v_mla_q_norm_w, v_w_uq, v_mla_kv_norm_w, v_w_ukv, v_w_out, v_ffn_norm_w, v_w_up, v_conv_w, v_conv_b, v_w_down, v_final_norm_w):
    a = dict(locals())
    x_, y_, c_ = _place()
    dev = 4 * x_ + 2 * y_ + c_

    shard = {n: a[n][0] for n in BIG_NAMES}
    shard16 = {n: w.astype(BF16) for n, w in shard.items()}
    cw_pad = jnp.pad(conv_w[0].reshape(-1), (0, 24 * 128 - 3 * 704)).reshape(24, 128)
    cw_all = _all_gather(cw_pad, name="gather_conv_w", in_vmem=True)
    conv_w_full = cw_all.reshape(N_DEV, -1)[:, :3 * 704].reshape(N_DEV, 3, 704).transpose(1, 0, 2).reshape(3, 2 * D_FF)
    sm = dict(attn_norm_w=attn_norm_w, ret_gn_w=ret_gn_w, mla_q_norm_w=mla_q_norm_w, mla_kv_norm_w=mla_kv_norm_w,
              ffn_norm_w=ffn_norm_w, final_norm_w=final_norm_w.reshape(1, D_MODEL),
              conv_w=_interleave_ff(conv_w_full), conv_b=_interleave_ff(conv_b))

    loss, grad_x, _, gs, slots_early, slots_late = _local_step(
        x[0], positions[0], loss_target[0], _pack_local(shard16, EARLY), sm, _pack_local(shard16, LATE))

    big = [{}, {}, {}, {}]
    for group, slots, tag, calls in ((EARLY, slots_early, "early", (("w_in", "w_uq", "w_ukv"),)),
                                     (LATE, slots_late, "late", (("w_out", "w_down"), ("w_up",)))):
        grads = _unpack_local(_sum_chips(slots, name="grad_sum_" + tag), shard, group)
        for names_c in calls:
            res = _adamw_multi([shard[n] for n in names_c], [a["m_" + n][0] for n in names_c],
                               [a["v_" + n][0] for n in names_c], [grads[n][0] for n in names_c],
                               name="adamw_" + "_".join(names_c))
            for kind in range(4):
                for n, r in zip(names_c, res[kind]):
                    big[kind][n] = r[None]

    packed = _pack_small(gs, name="pack_small_grads")
    tot = _sum_small(_all_gather(packed, name="gather_small_grads", in_vmem=True), name="sum_small_grads")
    loss_out = tot["loss"][0, 0]
    g_cw = lax.dynamic_slice_in_dim(tot["conv_w"], dev * 704, 704, axis=1)

    def rows_of(prefix):
        return [a[prefix + n].reshape(1, size) for n, size in SMALL]

    sml = _adamw_multi(rows_of("") + [conv_w[0]], rows_of("m_") + [m_conv_w[0]], rows_of("v_") + [v_conv_w[0]],
                       [tot[n] for n, _ in SMALL] + [g_cw], name="adamw_small")
    cwo = [kind[-1] for kind in sml]

    def small_of(kind, n):
        return sml[kind][[nm for nm, _ in SMALL].index(n)].reshape(a[n].shape)

    names = ['attn_norm_w', 'w_in', 'ret_gn_w', 'mla_q_norm_w', 'w_uq', 'mla_kv_norm_w', 'w_ukv', 'w_out',
             'ffn_norm_w', 'w_up', 'conv_w', 'conv_b', 'w_down', 'final_norm_w']
    outs = [loss_out, grad_x[None]]
    for kind in range(4):
        for n in names:
            if n == "conv_w":
                outs.append(cwo[kind][None])
            elif n in big[kind]:
                outs.append(big[kind][n])
            else:
                outs.append(small_of(kind, n))
    return tuple(outs)
```

```python
import functools

import numpy as np
import jax
import jax.numpy as jnp
from jax import lax
from jax.experimental import pallas as pl
from jax.experimental.pallas import tpu as pltpu

F32 = jnp.float32
BF16 = jnp.bfloat16
MESH = pl.DeviceIdType.MESH
ANY = pl.BlockSpec(memory_space=pl.ANY)

D_MODEL = 1024
RET_HEADS = 8
RET_HEAD_DIM = 64
RET_WIDTH = 512
RET_CHUNK = 128
MLA_HEADS = 8
MLA_NOPE = 64
MLA_ROPE = 32
MLA_V = 64
MLA_Q_RANK = 256
MLA_KV_RANK = 128
MLA_WIDTH = 512
IN_WIDTH = 2464
IN_PAD = 2560
D_FF = 2816
FF_HALF = 1408
ROPE_BASE = 10000.0
EPS = 1e-6
SCALE = float((MLA_NOPE + MLA_ROPE) ** -0.5)
K_SCALE = 0.125
N_DEV = 8

ADAM_LR = 0.001
ADAM_B1 = 0.9
ADAM_B2 = 0.999
ADAM_EPS = 1e-08
ADAM_WD = 0.01
ADAM_STEP = 10

VMEM_LIMIT = 56 * 1024 * 1024
MM_BUDGET = 40 * 1024 * 1024
NEG = -1e30
FLASH_UNROLL = 4
FLASH_BWD_UNROLL = 3

PACK_COLS = 1024
EARLY = ((("w_in", 308, 320, True), ("w_uq", 24, 32, True), ("w_ukv", 16, 16, True)), 384)
LATE = ((("w_out", 128, 128, False), ("w_up", 704, 704, True), ("w_down", 352, 352, False)), 1200)
BIG_NAMES = ("w_in", "w_uq", "w_ukv", "w_out", "w_up", "w_down")
SMALL = (("attn_norm_w", 1024), ("ret_gn_w", 512), ("mla_q_norm_w", 256), ("mla_kv_norm_w", 128),
         ("ffn_norm_w", 1024), ("conv_b", 5632), ("final_norm_w", 1024))
SMALL_VECTORS = SMALL + (("conv_w0", 5632), ("conv_w1", 5632), ("conv_w2", 5632), ("loss", 128))
SMALL_ROWS = 32


def _cp(sem=None, vmem=VMEM_LIMIT):
    return pltpu.CompilerParams(dimension_semantics=sem, vmem_limit_bytes=vmem)


def _dot(a, b):
    return jnp.dot(a, b, preferred_element_type=F32)


def _dot_nt(a, b):
    return lax.dot_general(a, b, (((1,), (1,)), ((), ())), preferred_element_type=F32)


def _dot_tn(a, b):
    return lax.dot_general(a, b, (((0,), (0,)), ((), ())), preferred_element_type=F32)


def _sigmoid(x):
    return 0.5 * jnp.tanh(0.5 * x) + 0.5


def _partner(x, half, period):
    n = x.shape[-1]
    lane = lax.broadcasted_iota(jnp.int32, x.shape, 1)
    return jnp.where((lane % period) < half, pltpu.roll(x, n - half, 1), pltpu.roll(x, half, 1))


def _rope(x, cos, ss, half, period):
    return x * cos + _partner(x, half, period) * ss


def _rope_t(dy, cos, ss, half, period):
    return dy * cos - _partner(dy, half, period) * ss


def _head_masks(shape):
    lane = lax.broadcasted_iota(jnp.int32, shape, 1)
    m0 = (lane < 64).astype(F32)
    return m0, 1.0 - m0


def _mm(a, b, *, name, add=None, out_dtype=F32, bt=False):
    parts = a if isinstance(a, tuple) else (a,)
    M = parts[0].shape[0]
    K = sum(p.shape[1] for p in parts)
    N = b.shape[0] if bt else b.shape[1]
    osz = jnp.dtype(out_dtype).itemsize
    per_row = 2 * (K * parts[0].dtype.itemsize + N * osz + (N * 4 if add is not None else 0))
    tm = 128
    for cand in (512, 256):
        if M % cand == 0 and cand * per_row + 4 * K * N <= MM_BUDGET:
            tm = cand
            break
    tm = min(tm, M)
    mul = _dot_nt if bt else _dot
    n_a = len(parts)
    n_in = n_a + (1 if add is None else 2)

    def body(*refs):
        av = refs[0][...] if n_a == 1 else jnp.concatenate([r[...] for r in refs[:n_a]], axis=1)
        acc = mul(av.astype(BF16), refs[n_a][...])
        if add is not None:
            acc = refs[n_a + 1][...] + acc
        refs[n_in][...] = acc.astype(out_dtype)

    in_specs = [pl.BlockSpec((tm, p.shape[1]), lambda i: (i, 0)) for p in parts]
    in_specs.append(pl.BlockSpec(b.shape, lambda i: (0, 0)))
    args = [*parts, b]
    if add is not None:
        in_specs.append(pl.BlockSpec((tm, N), lambda i: (i, 0)))
        args.append(add)
    return pl.pallas_call(
        body, name=name, grid=(M // tm,), in_specs=in_specs, out_specs=pl.BlockSpec((tm, N), lambda i: (i, 0)),
        out_shape=jax.ShapeDtypeStruct((M, N), out_dtype), compiler_params=_cp(("parallel",)))(*args)


def _mm_tn(a, b, *, name):
    parts = a if isinstance(a, tuple) else (a,)
    T = parts[0].shape[0]
    M = sum(p.shape[1] for p in parts)
    N = b.shape[1]
    tk = min(T, 512)

    def tile(n):
        for cand in (1408, 1280):
            if n > 1408 and n % cand == 0:
                return cand
        return n

    tm, tn = tile(M), tile(N)
    nk = T // tk
    n_a = len(parts)
    assert n_a == 1 or tm == M

    def body(*refs):
        o_ref = refs[n_a + 1]

        @pl.when(pl.program_id(2) == 0)
        def _():
            o_ref[...] = jnp.zeros_like(o_ref)
        av = refs[0][...] if n_a == 1 else jnp.concatenate([r[...] for r in refs[:n_a]], axis=1)
        o_ref[...] += _dot_tn(av.astype(BF16), refs[n_a][...].astype(BF16))

    if n_a == 1:
        a_specs = [pl.BlockSpec((tk, tm), lambda i, j, k: (k, i))]
    else:
        a_specs = [pl.BlockSpec((tk, p.shape[1]), lambda i, j, k: (k, 0)) for p in parts]
    return pl.pallas_call(
        body, name=name, grid=(M // tm, N // tn, nk),
        in_specs=a_specs + [pl.BlockSpec((tk, tn), lambda i, j, k: (k, j))],
        out_specs=pl.BlockSpec((tm, tn), lambda i, j, k: (i, j)),
        out_shape=jax.ShapeDtypeStruct((M, N), F32),
        compiler_params=_cp(("parallel", "parallel", "arbitrary")))(*parts, b)


def _rmsnorm_fwd(x, w, *, name, gather=None):
    T, D = x.shape
    tm = min(T, 1024)
    n = T // tm

    def body(x_ref, w_ref, *rest):
        if gather is not None:
            s_ref, o_ref, g_ref, *sems = rest
            start, forward, finish = _gather_phases(s_ref, g_ref, *sems)
            pl.when(pl.program_id(0) == 0)(start)
            pl.when(pl.program_id(0) == n // 2)(forward)
        else:
            o_ref, = rest
        xv = x_ref[...]
        r = lax.rsqrt(jnp.mean(xv * xv, axis=-1, keepdims=True) + EPS)
        o_ref[...] = (xv * r * w_ref[...]).astype(BF16)
        if gather is not None:
            pl.when(pl.program_id(0) == n - 1)(finish)

    in_specs = [pl.BlockSpec((tm, D), lambda i: (i, 0)), pl.BlockSpec((1, D), lambda i: (0, 0))]
    out_spec = pl.BlockSpec((tm, D), lambda i: (i, 0))
    out_shape = jax.ShapeDtypeStruct((T, D), BF16)
    if gather is None:
        return pl.pallas_call(body, name=name, grid=(n,), in_specs=in_specs, out_specs=out_spec, out_shape=out_shape,
                              compiler_params=_cp(("parallel",)))(x, w)
    return pl.pallas_call(
        body, name=name, grid=(n,), in_specs=in_specs + [ANY], out_specs=[out_spec, ANY],
        out_shape=[out_shape, jax.ShapeDtypeStruct((N_DEV,) + gather.shape, gather.dtype)],
        scratch_shapes=list(GATHER_SCRATCH), compiler_params=_cp(("arbitrary",)))(x, w, gather)


def _mm_norm_bwd(a, b, x, w, dres, *, name, exchange=None):
    T, K = a.shape
    D = b.shape[1]
    tm = min(T, 256 if K > 4096 else 512)
    n = T // tm

    def body(a_ref, b_ref, x_ref, w_ref, dr_ref, *rest):
        if exchange is None:
            dx_ref, dw_ref = rest
        else:
            p_ref, dx_ref, dw_ref, got_ref, *sems = rest
            start, finish = _exchange_phases(p_ref, got_ref, *sems)
            pl.when(pl.program_id(0) == 0)(start)

        @pl.when(pl.program_id(0) == 0)
        def _():
            dw_ref[...] = jnp.zeros_like(dw_ref)
        dh = _dot(a_ref[...], b_ref[...])
        xv = x_ref[...]
        r = lax.rsqrt(jnp.mean(xv * xv, axis=-1, keepdims=True) + EPS)
        xh = xv * r
        g = dh * w_ref[...]
        dx_ref[...] = dr_ref[...] + r * (g - xh * jnp.mean(g * xh, axis=-1, keepdims=True))
        dw_ref[...] += jnp.sum(dh * xh, axis=0, keepdims=True)
        if exchange is not None:
            pl.when(pl.program_id(0) == n - 1)(finish)

    row = pl.BlockSpec((tm, D), lambda i: (i, 0))
    vec = pl.BlockSpec((1, D), lambda i: (0, 0))
    in_specs = [pl.BlockSpec((tm, K), lambda i: (i, 0)), pl.BlockSpec((K, D), lambda i: (0, 0)), row, vec, row]
    out_shape = [jax.ShapeDtypeStruct((T, D), F32), jax.ShapeDtypeStruct((1, D), F32)]
    if exchange is None:
        return pl.pallas_call(body, name=name, grid=(n,), in_specs=in_specs, out_specs=[row, vec], out_shape=out_shape,
                              compiler_params=_cp(("arbitrary",)))(a, b, x, w, dres)
    return pl.pallas_call(
        body, name=name, grid=(n,), in_specs=in_specs + [ANY], out_specs=[row, vec, ANY],
        out_shape=out_shape + [jax.ShapeDtypeStruct(exchange.shape, exchange.dtype)],
        scratch_shapes=list(EXCHANGE_SCRATCH), compiler_params=_cp(("arbitrary",)))(a, b, x, w, dres, exchange)


def _down_proj_loss(a, w_down, x1, tgt, w, *, name):
    T, D = x1.shape
    K = a.shape[1]
    tm = min(T, 512)

    def body(a_ref, b_ref, x_ref, t_ref, w_ref, loss_ref, dx_ref, dxb_ref, dw_ref):
        @pl.when(pl.program_id(0) == 0)
        def _():
            dw_ref[...] = jnp.zeros_like(dw_ref)
            loss_ref[...] = jnp.zeros_like(loss_ref)
        xv = x_ref[...] + _dot(a_ref[...], b_ref[...])
        wv = w_ref[...]
        r = lax.rsqrt(jnp.mean(xv * xv, axis=-1, keepdims=True) + EPS)
        xh = xv * r
        e = xh * wv - t_ref[...]
        part = 0.5 * jnp.sum(jnp.mean(e * e, axis=-1, keepdims=True), axis=0, keepdims=True)
        loss_ref[...] += jnp.broadcast_to(part, loss_ref.shape)
        dy = e * (1.0 / D)
        g = dy * wv
        dx = r * (g - xh * jnp.mean(g * xh, axis=-1, keepdims=True))
        dx_ref[...] = dx
        dxb_ref[...] = dx.astype(BF16)
        dw_ref[...] += jnp.sum(dy * xh, axis=0, keepdims=True)

    row = pl.BlockSpec((tm, D), lambda i: (i, 0))
    vec = pl.BlockSpec((1, D), lambda i: (0, 0))
    return pl.pallas_call(
        body, name=name, grid=(T // tm,),
        in_specs=[pl.BlockSpec((tm, K), lambda i: (i, 0)), pl.BlockSpec((K, D), lambda i: (0, 0)), row, row, vec],
        out_specs=[pl.BlockSpec((1, 128), lambda i: (0, 0)), row, row, vec],
        out_shape=[jax.ShapeDtypeStruct((1, 128), F32), jax.ShapeDtypeStruct((T, D), F32),
                   jax.ShapeDtypeStruct((T, D), BF16), jax.ShapeDtypeStruct((1, D), F32)],
        compiler_params=_cp(("arbitrary",)))(a, w_down, x1, tgt, w)


def _ret_tables():
    C = RET_CHUNK
    h = jnp.arange(RET_HEADS, dtype=F32)
    log_gamma = jnp.log1p(-jnp.power(2.0, -5.0 - h))
    idx = jnp.arange(C, dtype=F32)
    diff = idx[:, None] - idx[None, :]
    dm = jnp.where(diff >= 0, jnp.exp(log_gamma[:, None, None] * jnp.maximum(diff, 0.0)), 0.0)
    dm = dm.reshape(4, 2 * C, C)
    lane_head = jnp.repeat(jnp.arange(RET_HEADS).reshape(4, 2), 64, axis=1)
    lg = log_gamma[lane_head]
    xi = jnp.exp(lg[:, None, :] * (idx[None, :, None] + 1.0))
    zeta = jnp.exp(lg[:, None, :] * (C - 1.0 - idx[None, :, None]))
    blk = (jnp.arange(128)[:, None] // 64) == (jnp.arange(128)[None, :] // 64)
    cd = jnp.where(blk[None], jnp.exp(lg * C)[:, :, None], 0.0)
    return dm.astype(F32), xi.astype(F32), zeta.astype(F32), cd.astype(F32)


def _ret_specs(tb, rev, nt):
    def tmap(t):
        return (nt - 1 - t) if rev else t
    qkv = [pl.BlockSpec((tb, 128), lambda p, t, o=o: (tmap(t), o + p)) for o in (0, 4, 8)]
    rope = [pl.BlockSpec((tb, 128), lambda p, t: (tmap(t), 0))] * 2
    tabs = [pl.BlockSpec((None, 256, 128), lambda p, t: (p, 0, 0))] + \
           [pl.BlockSpec((None, 128, 128), lambda p, t: (p, 0, 0))] * 3
    return qkv, rope, tabs


def _ret_fwd(proj, cos, ss, tabs, gnw, *, name):
    T = proj.shape[0]
    tb = min(T, 1024)
    nt = T // tb
    nchunk = tb // RET_CHUNK

    def body(q_ref, k_ref, v_ref, g_ref, cos_ref, ss_ref, dm_ref, xi_ref, zt_ref, cd_ref, gnw_ref,
             y_ref, o_ref, r_sc):
        @pl.when(pl.program_id(1) == 0)
        def _():
            r_sc[...] = jnp.zeros_like(r_sc)
        m0, m1 = _head_masks((128, 128))
        dm, xi, zt, cd = dm_ref[...], xi_ref[...], zt_ref[...], cd_ref[...]
        bm = (cd > 0).astype(F32)
        gnw = gnw_ref[...]
        for c in range(nchunk):
            rs = pl.ds(c * RET_CHUNK, RET_CHUNK)
            cs, sn = cos_ref[rs, :], ss_ref[rs, :]
            q = _rope(q_ref[rs, :], cs, sn, 32, 64)
            k = _rope(k_ref[rs, :], cs, sn, 32, 64) * K_SCALE
            v = v_ref[rs, :]
            kb, vb = k.astype(BF16), v.astype(BF16)
            qs = jnp.concatenate([q * m0, q * m1], axis=0).astype(BF16)
            s = (_dot_nt(qs, kb) * dm).astype(BF16)
            vs = jnp.concatenate([v * m0, v * m1], axis=0).astype(BF16)
            o = _dot(jnp.concatenate([s[:128], s[128:]], axis=1), vs)
            r = r_sc[...]
            o = o + _dot(q.astype(BF16), r.astype(BF16)) * xi
            r_sc[...] = cd * r + bm * _dot_tn((k * zt).astype(BF16), vb)
            mu = (jnp.sum(o * m0, axis=1, keepdims=True) * m0 + jnp.sum(o * m1, axis=1, keepdims=True) * m1) * (1.0 / 64)
            d = o - mu
            dd = d * d
            var = (jnp.sum(dd * m0, axis=1, keepdims=True) * m0 + jnp.sum(dd * m1, axis=1, keepdims=True) * m1) * (1.0 / 64)
            oh = d * lax.rsqrt(var + EPS)
            g = g_ref[rs, :]
            y_ref[rs, :] = (g * _sigmoid(g) * (oh * gnw)).astype(BF16)
            o_ref[rs, :] = o

    qkv, rope, tspec = _ret_specs(tb, False, nt)
    gspec = pl.BlockSpec((tb, 128), lambda p, t: (t, 12 + p))
    out = pl.BlockSpec((tb, 128), lambda p, t: (t, p))
    return pl.pallas_call(
        body, name=name, grid=(4, nt),
        in_specs=qkv + [gspec] + rope + tspec + [pl.BlockSpec((1, 128), lambda p, t: (0, p))],
        out_specs=[out, out],
        out_shape=[jax.ShapeDtypeStruct((T, RET_WIDTH), BF16), jax.ShapeDtypeStruct((T, RET_WIDTH), F32)],
        scratch_shapes=[pltpu.VMEM((128, 128), F32)],
        compiler_params=_cp(("parallel", "arbitrary")))(proj, proj, proj, proj, cos, ss, *tabs, gnw)


def _ret_bwd_dq(proj, do, cos, ss, tabs, *, name):
    T = proj.shape[0]
    tb = min(T, 1024)
    nt = T // tb
    nchunk = tb // RET_CHUNK

    def body(q_ref, k_ref, v_ref, do_ref, cos_ref, ss_ref, dm_ref, xi_ref, zt_ref, cd_ref, dq_ref, r_sc):
        del q_ref
        @pl.when(pl.program_id(1) == 0)
        def _():
            r_sc[...] = jnp.zeros_like(r_sc)
        m0, m1 = _head_masks((128, 128))
        dm, xi, zt, cd = dm_ref[...], xi_ref[...], zt_ref[...], cd_ref[...]
        bm = (cd > 0).astype(F32)
        for c in range(nchunk):
            rs = pl.ds(c * RET_CHUNK, RET_CHUNK)
            cs, sn = cos_ref[rs, :], ss_ref[rs, :]
            k = _rope(k_ref[rs, :], cs, sn, 32, 64) * K_SCALE
            vb = v_ref[rs, :].astype(BF16)
            dob = do_ref[rs, :]
            dof = dob.astype(F32)
            dos = jnp.concatenate([dof * m0, dof * m1], axis=0).astype(BF16)
            a = (_dot_nt(dos, vb) * dm).astype(BF16)
            ks = jnp.concatenate([k * m0, k * m1], axis=0).astype(BF16)
            r = r_sc[...]
            dq = _dot(jnp.concatenate([a[:128], a[128:]], axis=1), ks) + _dot_nt(dob, r.astype(BF16)) * xi
            r_sc[...] = cd * r + bm * _dot_tn((k * zt).astype(BF16), vb)
            dq_ref[rs, :] = _rope_t(dq, cs, sn, 32, 64).astype(BF16)

    qkv, rope, tspec = _ret_specs(tb, False, nt)
    blk = pl.BlockSpec((tb, 128), lambda p, t: (t, p))
    return pl.pallas_call(
        body, name=name, grid=(4, nt), in_specs=qkv + [blk] + rope + tspec, out_specs=blk,
        out_shape=jax.ShapeDtypeStruct((T, RET_WIDTH), BF16),
        scratch_shapes=[pltpu.VMEM((128, 128), F32)],
        compiler_params=_cp(("parallel", "arbitrary")))(proj, proj, proj, do, cos, ss, *tabs)


def _ret_bwd_dkv(proj, do, cos, ss, tabs, *, name, swap=None):
    T = proj.shape[0]
    tb = min(T, 1024)
    nt = T // tb
    nchunk = tb // RET_CHUNK

    def body(q_ref, k_ref, v_ref, do_ref, cos_ref, ss_ref, dm_ref, xi_ref, zt_ref, cd_ref, *rest):
        if swap is None:
            backward(q_ref, k_ref, v_ref, do_ref, cos_ref, ss_ref, dm_ref, xi_ref, zt_ref, cd_ref, *rest)
        else:
            g_ref, dk_ref, dv_ref, got_ref, u_sc, *sems = rest
            start, finish = _swap_phases(g_ref, got_ref, *sems)
            pl.when((pl.program_id(0) == 0) & (pl.program_id(1) == 0))(start)
            backward(q_ref, k_ref, v_ref, do_ref, cos_ref, ss_ref, dm_ref, xi_ref, zt_ref, cd_ref, dk_ref, dv_ref, u_sc)
            pl.when((pl.program_id(0) == 3) & (pl.program_id(1) == nt - 1))(finish)

    def backward(q_ref, k_ref, v_ref, do_ref, cos_ref, ss_ref, dm_ref, xi_ref, zt_ref, cd_ref, dk_ref, dv_ref, u_sc):
        @pl.when(pl.program_id(1) == 0)
        def _():
            u_sc[...] = jnp.zeros_like(u_sc)
        m0, m1 = _head_masks((128, 128))
        dm, xi, zt, cd = dm_ref[...], xi_ref[...], zt_ref[...], cd_ref[...]
        bm = (cd > 0).astype(F32)
        for c in reversed(range(nchunk)):
            rs = pl.ds(c * RET_CHUNK, RET_CHUNK)
            cs, sn = cos_ref[rs, :], ss_ref[rs, :]
            q = _rope(q_ref[rs, :], cs, sn, 32, 64)
            k = _rope(k_ref[rs, :], cs, sn, 32, 64) * K_SCALE
            kb = k.astype(BF16)
            vb = v_ref[rs, :].astype(BF16)
            dob = do_ref[rs, :]
            dof = dob.astype(F32)
            qs = jnp.concatenate([q * m0, q * m1], axis=0).astype(BF16)
            dos = jnp.concatenate([dof * m0, dof * m1], axis=0).astype(BF16)
            s = (_dot_nt(qs, kb) * dm).astype(BF16)
            a = (_dot_nt(dos, vb) * dm).astype(BF16)
            ub = u_sc[...].astype(BF16)
            dk = _dot_tn(a, qs) + _dot_nt(vb, ub) * zt
            dv = _dot_tn(s, dos) + _dot(kb, ub) * zt
            u_sc[...] = cd * u_sc[...] + bm * _dot_tn((q * xi).astype(BF16), dob)
            dk_ref[rs, :] = (_rope_t(dk, cs, sn, 32, 64) * K_SCALE).astype(BF16)
            dv_ref[rs, :] = dv.astype(BF16)

    qkv, rope, tspec = _ret_specs(tb, True, nt)
    blk = pl.BlockSpec((tb, 128), lambda p, t: (nt - 1 - t, p))
    out_shape = [jax.ShapeDtypeStruct((T, RET_WIDTH), BF16)] * 2
    if swap is None:
        return pl.pallas_call(
            body, name=name, grid=(4, nt), in_specs=qkv + [blk] + rope + tspec, out_specs=[blk, blk],
            out_shape=out_shape, scratch_shapes=[pltpu.VMEM((128, 128), F32)],
            compiler_params=_cp(("parallel", "arbitrary")))(proj, proj, proj, do, cos, ss, *tabs)
    return pl.pallas_call(
        body, name=name, grid=(4, nt), in_specs=qkv + [blk] + rope + tspec + [ANY], out_specs=[blk, blk, ANY],
        out_shape=out_shape + [jax.ShapeDtypeStruct((4,) + swap.shape[2:], swap.dtype)],
        scratch_shapes=[pltpu.VMEM((128, 128), F32)] + list(SWAP_SCRATCH),
        compiler_params=_cp(("arbitrary", "arbitrary")))(proj, proj, proj, do, cos, ss, *tabs, swap)


def _mix_bwd(dmixed, o_ret, proj, y_mla, gnw, *, name):
    T = dmixed.shape[0]
    tm = min(T, 512)

    def body(dm_ref, o_ref, g_ref, ym_ref, gnw_ref, do_ref, dg_ref, dom_ref, dl_ref, dw_ref):
        @pl.when(pl.program_id(0) == 0)
        def _():
            dw_ref[...] = jnp.zeros_like(dw_ref)
        m0, m1 = _head_masks((tm, 128))
        lane = lax.broadcasted_iota(jnp.int32, (tm, 128), 1)
        delta = jnp.zeros((tm, 128), F32)

        def gsum(z):
            return jnp.sum(z * m0, axis=1, keepdims=True) * m0 + jnp.sum(z * m1, axis=1, keepdims=True) * m1

        for p in range(4):
            cs = slice(128 * p, 128 * p + 128)
            dy = dm_ref[:, cs]
            o = o_ref[:, cs]
            g = g_ref[:, cs]
            w = gnw_ref[:, cs]
            d = o - gsum(o) * (1.0 / 64)
            rstd = lax.rsqrt(gsum(d * d) * (1.0 / 64) + EPS)
            oh = d * rstd
            sg = _sigmoid(g)
            dn = dy * (g * sg)
            dg_ref[:, cs] = (dy * (oh * w) * (sg * (1.0 + g * (1.0 - sg)))).astype(BF16)
            dw_ref[:, cs] += jnp.sum(dn * oh, axis=0, keepdims=True)
            doh = dn * w
            do = rstd * (doh - gsum(doh) * (1.0 / 64) - oh * (gsum(doh * oh) * (1.0 / 64)))
            do_ref[:, cs] = do.astype(BF16)
            dom = dm_ref[:, 512 + 128 * p:512 + 128 * p + 128]
            dom_ref[:, cs] = dom.astype(BF16)
            pr = dom * ym_ref[:, cs].astype(F32)
            delta = jnp.where(lane == 2 * p, jnp.sum(pr * m0, axis=1, keepdims=True), delta)
            delta = jnp.where(lane == 2 * p + 1, jnp.sum(pr * m1, axis=1, keepdims=True), delta)
        dl_ref[...] = delta.T[0:MLA_HEADS]

    half = pl.BlockSpec((tm, 512), lambda i: (i, 0))
    return pl.pallas_call(
        body, name=name, grid=(T // tm,),
        in_specs=[pl.BlockSpec((tm, 1024), lambda i: (i, 0)), half, pl.BlockSpec((tm, 512), lambda i: (i, 3)),
                  half, pl.BlockSpec((1, 512), lambda i: (0, 0))],
        out_specs=[half, half, half, pl.BlockSpec((MLA_HEADS, tm), lambda i: (0, i)),
                   pl.BlockSpec((1, 512), lambda i: (0, 0))],
        out_shape=[jax.ShapeDtypeStruct((T, 512), BF16)] * 3 + [jax.ShapeDtypeStruct((MLA_HEADS, T), F32),
                                                                jax.ShapeDtypeStruct((1, 512), F32)],
        compiler_params=_cp(("arbitrary",)))(dmixed, o_ret, proj, y_mla, gnw)


def _mla_prep_fwd(proj, qnw, kvnw, wuq, wk, wv, cos, ss, *, name):
    T = proj.shape[0]
    tm = min(T, 512)

    def body(lat_ref, qnw_ref, kvnw_ref, wuq_ref, wk_ref, wv_ref, cos_ref, ss_ref,
             q_ref, k_ref, v_ref, cqn_ref, ckvn_ref):
        cq = lat_ref[:, 0:256]
        ckv = lat_ref[:, 256:384]
        g3 = lat_ref[:, 384:512]
        cqn = (cq * lax.rsqrt(jnp.mean(cq * cq, axis=-1, keepdims=True) + EPS) * qnw_ref[...]).astype(BF16)
        ckvn = (ckv * lax.rsqrt(jnp.mean(ckv * ckv, axis=-1, keepdims=True) + EPS) * kvnw_ref[...]).astype(BF16)
        cqn_ref[...] = cqn
        ckvn_ref[...] = ckvn
        cs, sn = cos_ref[...], ss_ref[...]
        q = _dot_nt(cqn, wuq_ref[...])
        k = _dot_nt(ckvn, wk_ref[...])
        kpe = _rope(g3, cs, sn, 16, 32)
        for h in range(MLA_HEADS):
            hs = slice(128 * h, 128 * h + 128)
            q_ref[:, hs] = (_rope(q[:, hs], cs, sn, 16, 32) * SCALE).astype(BF16)
            k_ref[:, hs] = (k[:, hs] + kpe).astype(BF16)
        v = _dot_nt(ckvn, wv_ref[...])
        lane = lax.broadcasted_iota(jnp.int32, (tm, 128), 1)
        for p in range(4):
            vp = v[:, 128 * p:128 * p + 128]
            v_ref[:, 256 * p:256 * p + 128] = jnp.where(lane < 64, vp, 1.0).astype(BF16)
            v_ref[:, 256 * p + 128:256 * p + 256] = jnp.where(lane < 64, 1.0, vp).astype(BF16)

    def full(shape):
        return pl.BlockSpec(shape, lambda i: (0, 0))

    def row(w):
        return pl.BlockSpec((tm, w), lambda i: (i, 0))

    return pl.pallas_call(
        body, name=name, grid=(T // tm,),
        in_specs=[pl.BlockSpec((tm, 512), lambda i: (i, 4)), full((1, 256)), full((1, 128)), full((1024, 256)),
                  full((1024, 128)), full((512, 128)), row(128), row(128)],
        out_specs=[row(1024), row(1024), row(1024), row(256), row(128)],
        out_shape=[jax.ShapeDtypeStruct((T, 1024), BF16), jax.ShapeDtypeStruct((T, 1024), BF16),
                   jax.ShapeDtypeStruct((T, 1024), BF16), jax.ShapeDtypeStruct((T, 256), BF16),
                   jax.ShapeDtypeStruct((T, 128), BF16)],
        compiler_params=_cp(("parallel",)))(proj, qnw, kvnw, wuq, wk, wv, cos, ss)


def _mla_prep_bwd(dq, dk, dv, proj, qnw, kvnw, wuq_t, wk_t, wv_t, cos, ss, ret_grads, cqn, ckvn, h, *, name):
    T = proj.shape[0]
    tm = min(T, 256)

    def body(dq_ref, dk_ref, dv_ref, lat_ref, qnw_ref, kvnw_ref, wuq_ref, wk_ref, wv_ref, cos_ref, ss_ref,
             rq_ref, rk_ref, rv_ref, rg_ref, cqn_ref, ckvn_ref, h_ref,
             dproj_ref, gwin_ref, gwuq_ref, gwk_ref, gwv_ref, dqnw_ref, dkvnw_ref, dqp_ref):
        for j, r in enumerate((rq_ref, rk_ref, rv_ref, rg_ref)):
            dproj_ref[:, 512 * j:512 * j + 512] = r[...]
        dlat_ref = dproj_ref.at[:, 2048:2560]

        @pl.when(pl.program_id(0) == 0)
        def _():
            for r in (gwin_ref, gwuq_ref, gwk_ref, gwv_ref, dqnw_ref, dkvnw_ref):
                r[...] = jnp.zeros_like(r)
        cs, sn = cos_ref[...], ss_ref[...]
        dkpe = jnp.zeros((tm, 128), F32)
        for h in range(MLA_HEADS):
            hs = slice(128 * h, 128 * h + 128)
            dqp_ref[:, hs] = _rope_t(dq_ref[:, hs] * SCALE, cs, sn, 16, 32).astype(BF16)
            dkpe = dkpe + dk_ref[:, hs]
        lane = lax.broadcasted_iota(jnp.int32, (tm, 128), 1)
        rope_lane = (lane >= MLA_NOPE) & (lane < MLA_NOPE + MLA_ROPE)
        dg3 = jnp.where(rope_lane, _rope_t(jnp.where(rope_lane, dkpe, 0.0), cs, sn, 16, 32), 0.0)

        def norm_bwd(x, w, dn):
            r = lax.rsqrt(jnp.mean(x * x, axis=-1, keepdims=True) + EPS)
            xh = x * r
            g = dn * w
            return r * (g - xh * jnp.mean(g * xh, axis=-1, keepdims=True)), jnp.sum(dn * xh, axis=0, keepdims=True)

        dqp = dqp_ref[...]
        dkb = dk_ref[...].astype(BF16)
        dvb = dv_ref[...]
        dcqn = _dot(dqp, wuq_ref[...])
        dcq, dqnw = norm_bwd(lat_ref[:, 0:256], qnw_ref[...], dcqn)
        dckvn = _dot(dkb, wk_ref[...]) + _dot(dvb, wv_ref[...])
        dckv, dkvnw = norm_bwd(lat_ref[:, 256:384], kvnw_ref[...], dckvn)
        gwuq_ref[...] += _dot_tn(dqp, cqn_ref[...])
        gwk_ref[...] += _dot_tn(dkb, ckvn_ref[...])
        gwv_ref[...] += _dot_tn(dvb, ckvn_ref[...])
        dqnw_ref[...] += dqnw
        dkvnw_ref[...] += dkvnw
        dlat_ref[:, 0:256] = dcq.astype(BF16)
        dlat_ref[:, 256:384] = dckv.astype(BF16)
        dlat_ref[:, 384:512] = dg3.astype(BF16)
        gwin_ref[...] += _dot_tn(dproj_ref[...], h_ref[...])

    def full(shape):
        return pl.BlockSpec(shape, lambda i: (0, 0))

    def row(w):
        return pl.BlockSpec((tm, w), lambda i: (i, 0))

    return pl.pallas_call(
        body, name=name, grid=(T // tm,),
        in_specs=[row(1024), row(1024), row(512), pl.BlockSpec((tm, 512), lambda i: (i, 4)), full((1, 256)),
                  full((1, 128)), full((1024, 256)), full((1024, 128)), full((512, 128)), row(128), row(128)]
                 + [row(512)] * 4 + [row(256), row(128), row(D_MODEL)],
        out_specs=[row(IN_PAD), full((IN_PAD, D_MODEL)), full((1024, 256)), full((1024, 128)), full((512, 128)),
                   full((1, 256)), full((1, 128))],
        out_shape=[jax.ShapeDtypeStruct((T, IN_PAD), BF16), jax.ShapeDtypeStruct((IN_PAD, D_MODEL), F32),
                   jax.ShapeDtypeStruct((1024, 256), F32), jax.ShapeDtypeStruct((1024, 128), F32),
                   jax.ShapeDtypeStruct((512, 128), F32), jax.ShapeDtypeStruct((1, 256), F32),
                   jax.ShapeDtypeStruct((1, 128), F32)],
        scratch_shapes=[pltpu.VMEM((tm, 1024), BF16)],
        compiler_params=_cp(("arbitrary",)))(dq, dk, dv, proj, qnw, kvnw, wuq_t, wk_t, wv_t, cos, ss, *ret_grads,
                                             cqn, ckvn, h)


def _flash_fwd(q, k, v1, *, name, gather=None):
    T = q.shape[0]
    tq = min(T, 512)
    tk = tq
    nq = T // tq

    def body(q_ref, k_ref, v_ref, *rest):
        if gather is None:
            y_ref, lse_ref = rest
        else:
            x_ref, y_ref, lse_ref, g_ref, *sems = rest
            start, forward, finish = _gather_phases(x_ref, g_ref, *sems)
            pl.when((pl.program_id(0) == 0) & (pl.program_id(1) == 0))(start)
            pl.when((pl.program_id(0) == 1) & (pl.program_id(1) == 0))(forward)
        attend(q_ref, k_ref, v_ref, y_ref, lse_ref)
        if gather is not None:
            pl.when((pl.program_id(0) == 3) & (pl.program_id(1) == nq - 1))(finish)

    def attend(q_ref, k_ref, v_ref, y_ref, lse_ref):
        qi = pl.program_id(1)
        row = lax.broadcasted_iota(jnp.int32, (tq, tk), 0)
        col = lax.broadcasted_iota(jnp.int32, (tq, tk), 1)

        def step(kb, carry, masked):
            ks = pl.ds(pl.multiple_of(kb * tk, tk), tk)
            new = []
            for h in range(2):
                hs = slice(128 * h, 128 * h + 128)
                m, acc = carry[h]
                s = _dot_nt(q_ref[:, hs], k_ref[ks, hs])
                if masked:
                    s = jnp.where(col <= row, s, NEG)
                mn = jnp.maximum(m, jnp.max(s, axis=1, keepdims=True))
                p = jnp.exp((s - mn).astype(BF16))
                acc = jnp.exp(m - mn) * acc + _dot(p, v_ref[ks, hs])
                new.append((mn, acc))
            return tuple(new)

        def unrolled(j, c):
            for u in range(FLASH_UNROLL):
                c = step(FLASH_UNROLL * j + u, c, False)
            return c

        init = (jnp.full((tq, 1), NEG, F32), jnp.zeros((tq, 128), F32))
        carry = lax.fori_loop(0, qi // FLASH_UNROLL, unrolled, (init, init))
        carry = lax.fori_loop(FLASH_UNROLL * (qi // FLASH_UNROLL), qi, lambda kb, c: step(kb, c, False), carry)
        (ma, acca), (mb, accb) = step(qi, carry, True)
        lane = lax.broadcasted_iota(jnp.int32, (tq, 128), 1)
        la, lb = pltpu.roll(acca, 64, 1), pltpu.roll(accb, 64, 1)
        y_ref[...] = jnp.where(lane < 64, acca / la, accb / lb).astype(BF16)
        lse_ref[0, 0] = jnp.broadcast_to(ma + jnp.log(acca[:, 64:65]), (tq, 128)).T[0:1]
        lse_ref[1, 0] = jnp.broadcast_to(mb + jnp.log(accb[:, 0:1]), (tq, 128)).T[0:1]

    in_specs = [pl.BlockSpec((tq, 256), lambda p, i: (i, p)), pl.BlockSpec((T, 256), lambda p, i: (0, p)),
                pl.BlockSpec((T, 256), lambda p, i: (0, p))]
    out_specs = [pl.BlockSpec((tq, 128), lambda p, i: (i, p)), pl.BlockSpec((2, 1, 1, tq), lambda p, i: (p, i, 0, 0))]
    out_shape = [jax.ShapeDtypeStruct((T, MLA_WIDTH), BF16), jax.ShapeDtypeStruct((MLA_HEADS, nq, 1, tq), F32)]
    if gather is None:
        return pl.pallas_call(body, name=name, grid=(4, nq), in_specs=in_specs, out_specs=out_specs,
                              out_shape=out_shape, compiler_params=_cp(("parallel", "arbitrary")))(q, k, v1)
    return pl.pallas_call(
        body, name=name, grid=(4, nq), in_specs=in_specs + [ANY], out_specs=out_specs + [ANY],
        out_shape=out_shape + [jax.ShapeDtypeStruct((N_DEV,) + gather.shape, gather.dtype)],
        scratch_shapes=list(GATHER_SCRATCH),
        compiler_params=_cp(("arbitrary", "arbitrary")))(q, k, v1, gather)


def _flash_bwd(q, k, v, do, lse, delta, *, name, exchange=None):
    T = q.shape[0]
    tq = min(T, 512)
    tk = tq
    nq = T // tq

    def body(q_ref, k_ref, v_ref, do_ref, lse_ref, dl_ref, *rest):
        if exchange is None:
            backward(q_ref, k_ref, v_ref, do_ref, lse_ref, dl_ref, *rest)
        else:
            p_ref, dqt_ref, dk_ref, dv_ref, got_ref, *sems = rest
            start, finish = _exchange_phases(p_ref, got_ref, *sems)
            pl.when((pl.program_id(0) == 0) & (pl.program_id(1) == 0))(start)
            backward(q_ref, k_ref, v_ref, do_ref, lse_ref, dl_ref, dqt_ref, dk_ref, dv_ref)
            pl.when((pl.program_id(0) == 3) & (pl.program_id(1) == nq - 1))(finish)

    def backward(q_ref, k_ref, v_ref, do_ref, lse_ref, dl_ref, dqt_ref, dk_ref, dv_ref):
        kb = pl.program_id(1)

        @pl.when(kb == 0)
        def _():
            dqt_ref[...] = jnp.zeros_like(dqt_ref)
        krow = lax.broadcasted_iota(jnp.int32, (tk, tq), 0)
        qcol = lax.broadcasted_iota(jnp.int32, (tk, tq), 1)
        masks = _head_masks((tk, 128))
        vms = [(v_ref[:, 128 * h:128 * h + 128].astype(F32) * masks[h]).astype(BF16) for h in range(2)]

        def step(qi, carry, masked):
            qs = pl.ds(pl.multiple_of(qi * tq, tq), tq)
            dob = do_ref[qs, :]
            dof = dob.astype(F32)
            dks, dv_acc = list(carry[:2]), carry[2]
            for h in range(2):
                hs = slice(128 * h, 128 * h + 128)
                kh = k_ref[:, hs]
                qh = q_ref[qs, hs]
                st = _dot_nt(kh, qh)
                pt = jnp.exp((st - lse_ref[h, qi]).astype(BF16))
                if masked:
                    pt = jnp.where(krow <= qcol, pt, jnp.zeros_like(pt))
                dv_acc = dv_acc + _dot(pt, (dof * masks[h]).astype(BF16))
                dpt = _dot_nt(vms[h], dob)
                dst = pt * (dpt - dl_ref[h, qi]).astype(BF16)
                dks[h] = dks[h] + _dot(dst, qh)
                dqt_ref[qi, hs, :] += _dot_tn(kh, dst)
            return dks[0], dks[1], dv_acc

        zero = jnp.zeros((tk, 128), F32)
        carry = step(kb, (zero, zero, zero), True)

        def unrolled(j, c):
            for u in range(FLASH_BWD_UNROLL):
                c = step(kb + 1 + FLASH_BWD_UNROLL * j + u, c, False)
            return c

        trips = (nq - 1 - kb) // FLASH_BWD_UNROLL
        carry = lax.fori_loop(0, trips, unrolled, carry)
        dk0, dk1, dv_acc = lax.fori_loop(kb + 1 + FLASH_BWD_UNROLL * trips, nq, lambda qi, c: step(qi, c, False), carry)
        dk_ref[:, 0:128] = dk0
        dk_ref[:, 128:256] = dk1
        dv_ref[...] = dv_acc.astype(BF16)

    stat = pl.BlockSpec((2, nq, 1, tq), lambda p, j: (p, 0, 0, 0))
    in_specs = [pl.BlockSpec((T, 256), lambda p, j: (0, p)), pl.BlockSpec((tk, 256), lambda p, j: (j, p)),
                pl.BlockSpec((tk, 256), lambda p, j: (j, p)), pl.BlockSpec((T, 128), lambda p, j: (0, p)), stat, stat]
    out_specs = [pl.BlockSpec((None, nq, 256, tq), lambda p, j: (p, 0, 0, 0)),
                 pl.BlockSpec((tk, 256), lambda p, j: (j, p)), pl.BlockSpec((tk, 128), lambda p, j: (j, p))]
    out_shape = [jax.ShapeDtypeStruct((4, nq, 256, tq), F32), jax.ShapeDtypeStruct((T, 1024), F32),
                 jax.ShapeDtypeStruct((T, MLA_WIDTH), BF16)]
    if exchange is None:
        return pl.pallas_call(body, name=name, grid=(4, nq), in_specs=in_specs, out_specs=out_specs,
                              out_shape=out_shape,
                              compiler_params=_cp(("parallel", "arbitrary")))(q, k, v, do, lse, delta)
    return pl.pallas_call(
        body, name=name, grid=(4, nq), in_specs=in_specs + [ANY], out_specs=out_specs + [ANY],
        out_shape=out_shape + [jax.ShapeDtypeStruct(exchange.shape, exchange.dtype)],
        scratch_shapes=list(EXCHANGE_SCRATCH),
        compiler_params=_cp(("arbitrary", "arbitrary")))(q, k, v, do, lse, delta, exchange)


def _shift_down(x, n, prev8):
    r = pltpu.roll(x, n, 0)
    row = lax.broadcasted_iota(jnp.int32, prev8.shape, 0)
    first = jnp.where(row < n, pltpu.roll(prev8, n, 0), r[:8])
    if x.shape[0] == 8:
        return first
    return jnp.concatenate([first, r[8:]], axis=0)


def _shift_up(x, n, next8):
    tm = x.shape[0]
    r = pltpu.roll(x, tm - n, 0)
    row = lax.broadcasted_iota(jnp.int32, next8.shape, 0)
    last = jnp.where(row >= 8 - n, pltpu.roll(next8, 8 - n, 0), r[tm - 8:])
    return jnp.concatenate([r[:tm - 8], last], axis=0)


def _conv_pre(u, prev8, cw_ref, cb_ref):
    p1 = _shift_down(u, 1, prev8)
    p2 = _shift_down(u, 2, prev8)
    up = cb_ref[...] + cw_ref[0:1, :] * p2 + cw_ref[1:2, :] * p1 + cw_ref[2:3, :] * u
    return up, p1, p2


def _up_proj_conv(x1, nw, w_up_t, cw, cb, *, name):
    T, K = x1.shape
    tm = min(T, 256)

    def body(x_ref, nw_ref, w_ref, cw_ref, cb_ref, h_ref, u_ref, a_ref, carry_sc):
        @pl.when(pl.program_id(0) == 0)
        def _():
            carry_sc[...] = jnp.zeros_like(carry_sc)
        xv = x_ref[...]
        h = (xv * lax.rsqrt(jnp.mean(xv * xv, axis=-1, keepdims=True) + EPS) * nw_ref[...]).astype(BF16)
        h_ref[...] = h
        for blk in range(2):
            ups = []
            for half in range(2):
                cs = slice((2 * blk + half) * FF_HALF, (2 * blk + half + 1) * FF_HALF)
                u = _dot_nt(h, w_ref[cs, :])
                u_ref[:, cs] = u
                prev = carry_sc[:, cs]
                ups.append(cb_ref[:, cs] + cw_ref[0:1, cs] * _shift_down(u, 2, prev)
                           + cw_ref[1:2, cs] * _shift_down(u, 1, prev) + cw_ref[2:3, cs] * u)
                carry_sc[:, cs] = u[tm - 8:]
            gate, val = ups
            a_ref[:, blk * FF_HALF:(blk + 1) * FF_HALF] = (gate * _sigmoid(gate) * val).astype(BF16)

    def full(shape):
        return pl.BlockSpec(shape, lambda i: (0, 0))

    return pl.pallas_call(
        body, name=name, grid=(T // tm,),
        in_specs=[pl.BlockSpec((tm, K), lambda i: (i, 0)), full(nw.shape), full(w_up_t.shape), full(cw.shape),
                  full(cb.shape)],
        out_specs=[pl.BlockSpec((tm, K), lambda i: (i, 0)), pl.BlockSpec((tm, 2 * D_FF), lambda i: (i, 0)),
                   pl.BlockSpec((tm, D_FF), lambda i: (i, 0))],
        out_shape=[jax.ShapeDtypeStruct((T, K), BF16), jax.ShapeDtypeStruct((T, 2 * D_FF), F32),
                   jax.ShapeDtypeStruct((T, D_FF), BF16)],
        scratch_shapes=[pltpu.VMEM((8, 2 * D_FF), F32)],
        compiler_params=_cp(("arbitrary",)))(x1, nw, w_up_t, cw, cb)


def _conv_bwd(u, da, cw, cb, *, name):
    T = u.shape[0]
    tm = min(T, 512)
    W = 2 * FF_HALF
    nt = T // tm

    def body(u_ref, prev_ref, next_ref, da_ref, dan_ref, cw_ref, cb_ref, du_ref, dw0_ref, dw1_ref, dw2_ref, db_ref):
        i = pl.program_id(1)

        @pl.when(i == 0)
        def _():
            for r in (dw0_ref, dw1_ref, dw2_ref, db_ref):
                r[...] = jnp.zeros_like(r)

        def dpre(u, prev8, da):
            up, p1, p2 = _conv_pre(u, prev8, cw_ref, cb_ref)
            gate, val = up[:, :FF_HALF], up[:, FF_HALF:]
            sg = _sigmoid(gate)
            dgate = da * val * (sg * (1.0 + gate * (1.0 - sg)))
            dval = da * (gate * sg)
            return jnp.concatenate([dgate, dval], axis=1), p1, p2

        u = u_ref[...]
        prev = jnp.where(i > 0, prev_ref[...], 0.0)
        dup, p1, p2 = dpre(u, prev, da_ref[...])
        dupn, _, _ = dpre(next_ref[...], u[tm - 8:], dan_ref[...])
        dupn = jnp.where(i < nt - 1, dupn, 0.0)
        du = cw_ref[2:3, :] * dup + cw_ref[1:2, :] * _shift_up(dup, 1, dupn) + cw_ref[0:1, :] * _shift_up(dup, 2, dupn)
        du_ref[...] = du.astype(BF16)
        dw0_ref[...] += jnp.sum(dup * p2, axis=0, keepdims=True)
        dw1_ref[...] += jnp.sum(dup * p1, axis=0, keepdims=True)
        dw2_ref[...] += jnp.sum(dup * u, axis=0, keepdims=True)
        db_ref[...] += jnp.sum(dup, axis=0, keepdims=True)

    nxt = lambda j, i: (jnp.minimum((i + 1) * (tm // 8), T // 8 - 1), j)
    vec = pl.BlockSpec((1, W), lambda j, i: (0, j))
    return pl.pallas_call(
        body, name=name, grid=(2, nt),
        in_specs=[pl.BlockSpec((tm, W), lambda j, i: (i, j)),
                  pl.BlockSpec((8, W), lambda j, i: (jnp.maximum(i * (tm // 8) - 1, 0), j)),
                  pl.BlockSpec((8, W), nxt),
                  pl.BlockSpec((tm, FF_HALF), lambda j, i: (i, j)), pl.BlockSpec((8, FF_HALF), nxt),
                  pl.BlockSpec((3, W), lambda j, i: (0, j)), vec],
        out_specs=[pl.BlockSpec((tm, W), lambda j, i: (i, j)), vec, vec, vec, vec],
        out_shape=[jax.ShapeDtypeStruct((T, 2 * D_FF), BF16)] + [jax.ShapeDtypeStruct((1, 2 * D_FF), F32)] * 4,
        compiler_params=_cp(("parallel", "arbitrary")))(u, u, u, da, da, cw, cb)


def _sum_chips(slots, *, name):
    ns, R, C = slots.shape
    tr = _row_tile(R)

    def body(g_ref, o_ref):
        g = g_ref[0].astype(F32)
        for s in range(1, ns):
            g = g + g_ref[s].astype(F32)
        o_ref[...] = g

    return pl.pallas_call(
        body, name=name, grid=(R // tr,), in_specs=[pl.BlockSpec((ns, tr, C), lambda i: (0, i, 0))],
        out_specs=pl.BlockSpec((tr, C), lambda i: (i, 0)), out_shape=jax.ShapeDtypeStruct((R, C), F32),
        compiler_params=_cp(("parallel",)))(slots)


def _place():
    return lax.axis_index("x"), lax.axis_index("y"), lax.axis_index("c")


GATHER_SCRATCH = (pltpu.SemaphoreType.DMA((7,)), pltpu.SemaphoreType.DMA((7,)), pltpu.SemaphoreType.DMA)
EXCHANGE_SCRATCH = (pltpu.SemaphoreType.DMA((3,)), pltpu.SemaphoreType.DMA((3,)), pltpu.SemaphoreType.DMA)


def _gather_phases(x_ref, out_ref, send_sems, recv_sems, local_sem):
    x_, y_, c_ = _place()
    me, sibling = (x_, y_, c_), (x_, y_, 1 - c_)
    chips = [(1 - x_, y_), (x_, 1 - y_), (1 - x_, 1 - y_)]

    def slot(px, py, pc):
        return out_ref.at[4 * px + 2 * py + pc]

    def copy(k, block, to, src=None):
        return pltpu.make_async_remote_copy(
            src_ref=slot(*block) if src is None else src, dst_ref=slot(*block),
            send_sem=send_sems.at[k], recv_sem=recv_sems.at[k], device_id=to, device_id_type=MESH)

    def mine():
        return pltpu.make_async_copy(x_ref, slot(*me), local_sem)

    def first():
        return [copy(0, me, sibling, src=x_ref)] + [copy(1 + j, me, (*chip, c_), src=x_ref)
                                                     for j, chip in enumerate(chips)]

    def passed():
        return [copy(4 + j, (*chip, c_), sibling) for j, chip in enumerate(chips)]

    def start():
        mine().start()
        for cp in first():
            cp.start()

    def forward():
        fwd = passed()
        for j, chip in enumerate(chips):
            copy(1 + j, (*chip, c_), me).wait_recv()
            fwd[j].start()

    def finish():
        copy(0, sibling, me).wait_recv()
        for j, chip in enumerate(chips):
            copy(4 + j, (*chip, 1 - c_), me).wait_recv()
        for cp in first() + passed():
            cp.wait_send()
        mine().wait()

    return start, forward, finish


def _exchange_phases(p_ref, out_ref, send_sems, recv_sems, local_sem):
    x_, y_, c_ = _place()
    me_k = 2 * x_ + y_
    chips = [(1 - x_, y_), (x_, 1 - y_), (1 - x_, 1 - y_)]

    def local():
        return pltpu.make_async_copy(p_ref.at[me_k], out_ref.at[me_k], local_sem)

    def copy(j, src_k, dst_k, chip):
        return pltpu.make_async_remote_copy(
            src_ref=p_ref.at[src_k], dst_ref=out_ref.at[dst_k], send_sem=send_sems.at[j],
            recv_sem=recv_sems.at[j], device_id=(*chip, c_), device_id_type=MESH)

    def sends():
        return [copy(j, 2 * px + py, me_k, (px, py)) for j, (px, py) in enumerate(chips)]

    def start():
        local().start()
        for cp in sends():
            cp.start()

    def finish():
        for j, (px, py) in enumerate(chips):
            copy(j, me_k, 2 * px + py, (px, py)).wait_recv()
        for cp in sends():
            cp.wait_send()
        local().wait()

    return start, finish


def _all_gather(x, *, name, in_vmem):
    def body(x_ref, out_ref, send_sems, recv_sems, local_sem):
        for phase in _gather_phases(x_ref, out_ref, send_sems, recv_sems, local_sem):
            phase()

    spec = pl.BlockSpec(memory_space=pltpu.VMEM) if in_vmem else ANY
    return pl.pallas_call(
        body, name=name, out_shape=jax.ShapeDtypeStruct((N_DEV,) + x.shape, x.dtype),
        in_specs=[spec], out_specs=spec, scratch_shapes=list(GATHER_SCRATCH),
        compiler_params=pltpu.CompilerParams(vmem_limit_bytes=VMEM_LIMIT))(x)


def _small_rows():
    table, row = [], 0
    for n, size in SMALL_VECTORS:
        table.append((n, size, row))
        row += -(-size // PACK_COLS)
    return table


def _ff_chunk_source(c):
    block, off = divmod(c * 128, FF_HALF)
    return (0, 2, 1, 3)[block] * FF_HALF + off


def _pack_small(parts, *, name):
    table = _small_rows()

    def body(*refs):
        out = refs[-1]
        out[...] = jnp.zeros_like(out)
        for ref, (n, size, row) in zip(refs, table):
            if size != 2 * D_FF:
                out[row:row + 1, 0:size] = ref[...]
                continue
            for c in range(size // 128):
                src = _ff_chunk_source(c)
                r, lane = divmod(c * 128, PACK_COLS)
                out[row + r:row + r + 1, lane:lane + 128] = ref[:, src:src + 128]

    return pl.pallas_call(body, name=name, out_shape=jax.ShapeDtypeStruct((SMALL_ROWS, PACK_COLS), F32))(
        *[parts[n] for n, _, _ in table])


def _sum_small(g, *, name):
    table = _small_rows()
    shapes = [(n, size) for n, size, _ in table if not n.startswith("conv_w")]
    shapes.insert(7, ("conv_w", 2 * D_FF))

    def body(g_ref, *outs):
        def total(row, width):
            acc = g_ref[0, row:row + 1, 0:width]
            for d in range(1, N_DEV):
                acc = acc + g_ref[d, row:row + 1, 0:width]
            return acc

        out_of = {n: o for (n, _), o in zip(shapes, outs)}
        for n, size, row in table:
            o, j = (out_of["conv_w"], int(n[-1])) if n.startswith("conv_w") else (out_of[n], 0)
            for i in range(-(-size // PACK_COLS)):
                width = min(PACK_COLS, size - PACK_COLS * i)
                o[j:j + 1, PACK_COLS * i:PACK_COLS * i + width] = total(row + i, width)

    out_shape = [jax.ShapeDtypeStruct((3 if n == "conv_w" else 1, size), F32) for n, size in shapes]
    res = pl.pallas_call(body, name=name, out_shape=out_shape)(g)
    return {n: r for (n, _), r in zip(shapes, res)}


def _adamw_multi(ws, ms, vs, gs, *, name):
    k = len(ws)

    def body(*refs):
        w_refs, m_refs, v_refs, g_refs = (refs[i * k:(i + 1) * k] for i in range(4))
        outs = refs[4 * k:]
        for i in range(k):
            g = g_refs[i][...]
            mn = ADAM_B1 * m_refs[i][...] + (1.0 - ADAM_B1) * g
            vn = ADAM_B2 * v_refs[i][...] + (1.0 - ADAM_B2) * (g * g)
            m_hat = mn / (1.0 - ADAM_B1 ** ADAM_STEP)
            v_hat = vn / (1.0 - ADAM_B2 ** ADAM_STEP)
            outs[i][...] = g
            outs[k + i][...] = -ADAM_LR * (m_hat / (jnp.sqrt(v_hat) + ADAM_EPS) + ADAM_WD * w_refs[i][...])
            outs[2 * k + i][...] = mn
            outs[3 * k + i][...] = vn

    out_shape = [jax.ShapeDtypeStruct(w.shape, F32) for _ in range(4) for w in ws]
    res = pl.pallas_call(body, name=name, out_shape=out_shape, compiler_params=_cp())(*ws, *ms, *vs, *gs)
    return [res[i * k:(i + 1) * k] for i in range(4)]


SWAP_SCRATCH = (pltpu.SemaphoreType.DMA((4,)), pltpu.SemaphoreType.DMA((4,)))


def _swap_phases(g_ref, out_ref, send_sems, recv_sems):
    x_, y_, c_ = _place()

    def copies():
        return [pltpu.make_async_remote_copy(src_ref=g_ref.at[k, 1 - c_], dst_ref=out_ref.at[k],
                                             send_sem=send_sems.at[k], recv_sem=recv_sems.at[k],
                                             device_id=(x_, y_, 1 - c_), device_id_type=MESH) for k in range(4)]

    def start():
        for cp in copies():
            cp.start()

    def finish():
        for cp in copies():
            cp.wait()

    return start, finish


def _swap_sibling(g, *, name):
    def body(g_ref, out_ref, send_sems, recv_sems):
        for phase in _swap_phases(g_ref, out_ref, send_sems, recv_sems):
            phase()

    return pl.pallas_call(
        body, name=name, out_shape=jax.ShapeDtypeStruct((4,) + g.shape[2:], g.dtype), in_specs=[ANY], out_specs=ANY,
        scratch_shapes=list(SWAP_SCRATCH))(g)


def _row_tile(R):
    for cand in (256, 400, 200):
        if R % cand == 0:
            return cand
    return R


def _add_own(g, b, *, name, out_dtype):
    n, _, R, C = g.shape
    tr = _row_tile(R)

    def body(c_ref, g_ref, b_ref, o_ref):
        del c_ref
        o_ref[...] = (g_ref[...] + b_ref[...]).astype(out_dtype)

    blk = pl.BlockSpec((None, tr, C), lambda s, i, c: (s, i, 0))
    grid_spec = pltpu.PrefetchScalarGridSpec(
        num_scalar_prefetch=1, grid=(n, R // tr),
        in_specs=[pl.BlockSpec((None, None, tr, C), lambda s, i, c: (s, c[0], i, 0)), blk], out_specs=blk)
    core = jnp.reshape(lax.axis_index("c"), (1,)).astype(jnp.int32)
    return pl.pallas_call(body, name=name, grid_spec=grid_spec, out_shape=jax.ShapeDtypeStruct(b.shape, out_dtype),
                          compiler_params=_cp(("parallel", "parallel")))(core, g, b)


def _pack_local(parts, group):
    table, rows = group
    segs = []
    for n, r, rp, tr in table:
        w = parts[n].T if tr else parts[n]
        segs.append(jnp.pad(w.reshape(r, PACK_COLS), ((0, rp - r), (0, 0))))
    segs.append(jnp.zeros((rows - sum(rp for _, _, rp, _ in table), PACK_COLS), segs[0].dtype))
    return jnp.concatenate(segs, axis=0)


def _unpack_local(packed, like, group):
    out, off = {}, 0
    for n, r, rp, tr in group[0]:
        rows, cols = like[n].shape
        seg = packed[off:off + r]
        out[n] = (seg.reshape(cols, rows).T if tr else seg)[None]
        off += rp
    return out


def _segments(g, group):
    out, off = {}, 0
    for n, r, rp, _ in group[0]:
        out[n] = g[:, off:off + r]
        off += rp
    return out


def _pack_grads(parts, group):
    table, rows = group
    segs = [jnp.pad(parts[n], ((0, 0), (0, rp - parts[n].shape[1]), (0, 0))) for n, _, rp, _ in table]
    segs.append(jnp.zeros((N_DEV, rows - sum(rp for _, _, rp, _ in table), PACK_COLS), F32))
    return jnp.concatenate(segs, axis=1)


def _owner_rows_early(g):
    g_in = jnp.concatenate([g["w_in_t"][:2432], g["w_in_t"][2496:2528]], axis=0).reshape(N_DEV, 308, PACK_COLS)
    g_uq = g["w_uq_t"].reshape(N_DEV, 128, MLA_Q_RANK)[:, :96].reshape(N_DEV, 24, PACK_COLS)
    g_ukv = jnp.concatenate([g["w_k_t"].reshape(N_DEV, 128, MLA_KV_RANK)[:, :64],
                             g["w_v_t"].reshape(N_DEV, 64, MLA_KV_RANK)], axis=1).reshape(N_DEV, 16, PACK_COLS)
    return dict(w_in=g_in, w_uq=g_uq, w_ukv=g_ukv)


def _owner_rows_late(g):
    g_up = g["w_up_t"].reshape(2, 2, 2, 704, PACK_COLS).swapaxes(0, 1).reshape(N_DEV, 704, PACK_COLS)
    return dict(w_out=g["w_out"].reshape(N_DEV, 128, PACK_COLS), w_up=g_up,
                w_down=g["w_down"].reshape(N_DEV, 352, PACK_COLS))


def _reduce_to_pairs(gp, *, name):
    gp = gp.reshape(4, 2, gp.shape[1], PACK_COLS)
    return _add_own(gp, _swap_sibling(gp, name=name + "_swap"), out_dtype=BF16, name=name + "_sum")


def _interleave_ff(w):
    g, v = w[..., :D_FF], w[..., D_FF:]
    return jnp.concatenate([g[..., :FF_HALF], v[..., :FF_HALF], g[..., FF_HALF:], v[..., FF_HALF:]], axis=-1)


def _rope_tables(pos):
    p = pos.astype(F32)[:, None]
    inv_r = ROPE_BASE ** (-jnp.arange(0, RET_HEAD_DIM, 2, dtype=F32) / RET_HEAD_DIM)
    ang = p * jnp.tile(inv_r, 4)
    sign_r = jnp.tile(jnp.concatenate([-jnp.ones((32,), F32), jnp.ones((32,), F32)]), 2)
    cos_r, ss_r = jnp.cos(ang), jnp.sin(ang) * sign_r
    inv_m = ROPE_BASE ** (-jnp.arange(0, MLA_ROPE, 2, dtype=F32) / MLA_ROPE)
    ang = p * jnp.concatenate([jnp.zeros((64,), F32), inv_m, inv_m, jnp.zeros((32,), F32)])
    sign_m = jnp.concatenate([jnp.zeros((64,), F32), -jnp.ones((16,), F32), jnp.ones((16,), F32), jnp.zeros((32,), F32)])
    cos_m, ss_m = jnp.cos(ang), jnp.sin(ang) * sign_m
    return cos_r, ss_r, cos_m, ss_m


def _prep_early(gathered):
    seg = _segments(gathered, EARLY)
    w_in_t = seg["w_in"].reshape(IN_WIDTH, D_MODEL)
    z = lambda n: jnp.zeros((n, D_MODEL), BF16)
    w_in_t = jnp.concatenate([w_in_t[:2432], z(64), w_in_t[2432:2464], z(32)], axis=0)
    w_uq_t = jnp.pad(seg["w_uq"].reshape(MLA_HEADS, 96, MLA_Q_RANK), ((0, 0), (0, 32), (0, 0))).reshape(1024, MLA_Q_RANK)
    ukv = seg["w_ukv"].reshape(MLA_HEADS, 128, MLA_KV_RANK)
    w_k_t = jnp.pad(ukv[:, :64], ((0, 0), (0, 64), (0, 0))).reshape(1024, MLA_KV_RANK)
    w_v_t = ukv[:, 64:].reshape(512, MLA_KV_RANK)
    return dict(w_in_t=w_in_t, w_uq_t=w_uq_t, w_k_t=w_k_t, w_v_t=w_v_t)


def _prep_late(gathered):
    seg = _segments(gathered, LATE)
    w_up_t = seg["w_up"].reshape(2, 2, 2, 704, D_MODEL).swapaxes(0, 1).reshape(2 * D_FF, D_MODEL)
    return dict(w_out=seg["w_out"].reshape(1024, D_MODEL), w_up_t=w_up_t, w_down=seg["w_down"].reshape(D_FF, D_MODEL))


def _local_step(x, pos, tgt, early, sm, late):
    dist = not isinstance(late, dict)
    cos_r, ss_r, cos_m, ss_m = _rope_tables(pos)
    tabs = _ret_tables()

    if dist:
        h, gathered = _rmsnorm_fwd(x, sm["attn_norm_w"], gather=early, name="attn_norm")
        W = _prep_early(gathered)
    else:
        h = _rmsnorm_fwd(x, sm["attn_norm_w"], name="attn_norm")
        W = early
    proj = _mm(h, W["w_in_t"], bt=True, name="in_proj")
    y_ret, o_ret = _ret_fwd(proj, cos_r, ss_r, tabs, sm["ret_gn_w"], name="ret_fwd")
    q, k, v1, cqn, ckvn = _mla_prep_fwd(proj, sm["mla_q_norm_w"], sm["mla_kv_norm_w"], W["w_uq_t"], W["w_k_t"],
                                       W["w_v_t"], cos_m, ss_m, name="mla_prep")
    T = x.shape[0]
    tq = min(T, 512)
    if dist:
        y_mla, lse, gathered = _flash_fwd(q, k, v1, gather=late, name="mla_attn")
        W = {**W, **_prep_late(gathered)}
    else:
        y_mla, lse = _flash_fwd(q, k, v1, name="mla_attn")
        W = {**W, **late}
    mixed = (y_ret, y_mla)
    x1 = _mm(mixed, W["w_out"], add=x, name="out_proj")
    h2, u, a = _up_proj_conv(x1, sm["ffn_norm_w"], W["w_up_t"], sm["conv_w"], sm["conv_b"], name="ffn_norm_up_conv")
    loss, dx2, dx2b, d_final = _down_proj_loss(a, W["w_down"], x1, tgt, sm["final_norm_w"], name="down_proj_loss")

    g = {}
    g["w_down"] = _mm_tn(a, dx2b, name="dw_down")
    da = _mm(dx2b, W["w_down"], bt=True, name="d_act")
    du, dcw0, dcw1, dcw2, dcb = _conv_bwd(u, da, sm["conv_w"], sm["conv_b"], name="conv_bwd")
    g["w_up_t"] = _mm_tn(du, h2, name="dw_up")
    dx1, d_ffn = _mm_norm_bwd(du, W["w_up_t"], x1, sm["ffn_norm_w"], dx2, name="d_h2_ffn_norm_bwd")

    g["w_out"] = _mm_tn(mixed, dx1, name="dw_out")
    dmixed = _mm(dx1, W["w_out"], bt=True, name="d_mixed")
    do_ret, dg, do_mla, delta, d_gn = _mix_bwd(dmixed, o_ret, proj, y_mla, sm["ret_gn_w"], name="mix_bwd")
    drq = _ret_bwd_dq(proj, do_ret, cos_r, ss_r, tabs, name="ret_bwd_dq")
    delta_r = delta.reshape(MLA_HEADS, T // tq, 1, tq)
    if dist:
        gl = _pack_grads(_owner_rows_late(g), LATE).reshape(4, 2, LATE[1], PACK_COLS)
        drk, drv, theirs = _ret_bwd_dkv(proj, do_ret, cos_r, ss_r, tabs, swap=gl, name="ret_bwd_dkv")
        pair = _add_own(gl, theirs, out_dtype=BF16, name="grad_late_sum")
        dqt, dk, dv, slots_late = _flash_bwd(q, k, v1, do_mla, lse, delta_r, exchange=pair, name="mla_attn_bwd")
    else:
        drk, drv = _ret_bwd_dkv(proj, do_ret, cos_r, ss_r, tabs, name="ret_bwd_dkv")
        dqt, dk, dv = _flash_bwd(q, k, v1, do_mla, lse, delta_r, name="mla_attn_bwd")
        slots_late = None
    dq = dqt.transpose(1, 3, 0, 2).reshape(T, MLA_HEADS * 128)
    dproj, g["w_in_t"], g["w_uq_t"], g["w_k_t"], g["w_v_t"], d_qn, d_kvn = _mla_prep_bwd(
        dq, dk, dv, proj, sm["mla_q_norm_w"], sm["mla_kv_norm_w"], W["w_uq_t"], W["w_k_t"], W["w_v_t"], cos_m, ss_m,
        (drq, drk, drv, dg), cqn, ckvn, h, name="mla_prep_bwd")
    if dist:
        pair = _reduce_to_pairs(_pack_grads(_owner_rows_early(g), EARLY), name="grad_early")
        grad_x, d_attn, slots_early = _mm_norm_bwd(dproj, W["w_in_t"], x, sm["attn_norm_w"], dx1, exchange=pair,
                                                   name="d_h_attn_norm_bwd")
    else:
        grad_x, d_attn = _mm_norm_bwd(dproj, W["w_in_t"], x, sm["attn_norm_w"], dx1, name="d_h_attn_norm_bwd")
        slots_early = None

    small = dict(attn_norm_w=d_attn, ret_gn_w=d_gn, mla_q_norm_w=d_qn, mla_kv_norm_w=d_kvn, ffn_norm_w=d_ffn,
                 conv_b=dcb, final_norm_w=d_final, conv_w0=dcw0, conv_w1=dcw1, conv_w2=dcw2, loss=loss)
    return loss, grad_x, g, small, slots_early, slots_late


def kernel(x, positions, attn_norm_w, w_in, ret_gn_w, mla_q_norm_w, w_uq, mla_kv_norm_w, w_ukv, w_out, ffn_norm_w, w_up, conv_w, conv_b, w_down, final_norm_w, loss_target, m_attn_norm_w, m_w_in, m_ret_gn_w, m_mla_q_norm_w, m_w_uq, m_mla_kv_norm_w, m_w_ukv, m_w_out, m_ffn_norm_w, m_w_up, m_conv_w, m_conv_b, m_w_down, m_final_norm_w, v_attn_norm_w, v_w_in, v_ret_gn_w, v_mla_q_norm_w, v_w_uq, v_mla_kv_norm_w, v_w_ukv, v_w_out, v_ffn_norm_w, v_w_up, v_conv_w, v_conv_b, v_w_down, v_final_norm_w):
    a = dict(locals())
    x_, y_, c_ = _place()
    dev = 4 * x_ + 2 * y_ + c_

    shard = {n: a[n][0] for n in BIG_NAMES}
    shard16 = {n: w.astype(BF16) for n, w in shard.items()}
    cw_pad = jnp.pad(conv_w[0].reshape(-1), (0, 24 * 128 - 3 * 704)).reshape(24, 128)
    cw_all = _all_gather(cw_pad, name="gather_conv_w", in_vmem=True)
    conv_w_full = cw_all.reshape(N_DEV, -1)[:, :3 * 704].reshape(N_DEV, 3, 704).transpose(1, 0, 2).reshape(3, 2 * D_FF)
    sm = dict(attn_norm_w=attn_norm_w, ret_gn_w=ret_gn_w, mla_q_norm_w=mla_q_norm_w, mla_kv_norm_w=mla_kv_norm_w,
              ffn_norm_w=ffn_norm_w, final_norm_w=final_norm_w.reshape(1, D_MODEL),
              conv_w=_interleave_ff(conv_w_full), conv_b=_interleave_ff(conv_b))

    loss, grad_x, _, gs, slots_early, slots_late = _local_step(
        x[0], positions[0], loss_target[0], _pack_local(shard16, EARLY), sm, _pack_local(shard16, LATE))

    big = [{}, {}, {}, {}]
    for group, slots, tag, calls in ((EARLY, slots_early, "early", (("w_in", "w_uq", "w_ukv"),)),
                                     (LATE, slots_late, "late", (("w_out", "w_down"), ("w_up",)))):
        grads = _unpack_local(_sum_chips(slots, name="grad_sum_" + tag), shard, group)
        for names_c in calls:
            res = _adamw_multi([shard[n] for n in names_c], [a["m_" + n][0] for n in names_c],
                               [a["v_" + n][0] for n in names_c], [grads[n][0] for n in names_c],
                               name="adamw_" + "_".join(names_c))
            for kind in range(4):
                for n, r in zip(names_c, res[kind]):
                    big[kind][n] = r[None]

    packed = _pack_small(gs, name="pack_small_grads")
    tot = _sum_small(_all_gather(packed, name="gather_small_grads", in_vmem=True), name="sum_small_grads")
    loss_out = tot["loss"][0, 0]
    g_cw = lax.dynamic_slice_in_dim(tot["conv_w"], dev * 704, 704, axis=1)

    def rows_of(prefix):
        return [a[prefix + n].reshape(1, size) for n, size in SMALL]

    sml = _adamw_multi(rows_of("") + [conv_w[0]], rows_of("m_") + [m_conv_w[0]], rows_of("v_") + [v_conv_w[0]],
                       [tot[n] for n, _ in SMALL] + [g_cw], name="adamw_small")
    cwo = [kind[-1] for kind in sml]

    def small_of(kind, n):
        return sml[kind][[nm for nm, _ in SMALL].index(n)].reshape(a[n].shape)

    names = ['attn_norm_w', 'w_in', 'ret_gn_w', 'mla_q_norm_w', 'w_uq', 'mla_kv_norm_w', 'w_ukv', 'w_out',
             'ffn_norm_w', 'w_up', 'conv_w', 'conv_b', 'w_down', 'final_norm_w']
    outs = [loss_out, grad_x[None]]
    for kind in range(4):
        for n in names:
            if n == "conv_w":
                outs.append(cwo[kind][None])
            elif n in big[kind]:
                outs.append(big[kind][n])
            else:
                outs.append(small_of(kind, n))
    return tuple(outs)
```

```python
import functools

import numpy as np
import jax
import jax.numpy as jnp
from jax import lax
from jax.experimental import pallas as pl
from jax.experimental.pallas import tpu as pltpu

F32 = jnp.float32
BF16 = jnp.bfloat16
MESH = pl.DeviceIdType.MESH
ANY = pl.BlockSpec(memory_space=pl.ANY)

D_MODEL = 1024
RET_HEADS = 8
RET_HEAD_DIM = 64
RET_WIDTH = 512
RET_CHUNK = 128
MLA_HEADS = 8
MLA_NOPE = 64
MLA_ROPE = 32
MLA_V = 64
MLA_Q_RANK = 256
MLA_KV_RANK = 128
MLA_WIDTH = 512
IN_WIDTH = 2464
IN_PAD = 2560
D_FF = 2816
FF_HALF = 1408
ROPE_BASE = 10000.0
EPS = 1e-6
SCALE = float((MLA_NOPE + MLA_ROPE) ** -0.5)
K_SCALE = 0.125
N_DEV = 8

ADAM_LR = 0.001
ADAM_B1 = 0.9
ADAM_B2 = 0.999
ADAM_EPS = 1e-08
ADAM_WD = 0.01
ADAM_STEP = 10

VMEM_LIMIT = 56 * 1024 * 1024
MM_BUDGET = 40 * 1024 * 1024
NEG = -1e30
FLASH_UNROLL = 4
FLASH_BWD_UNROLL = 3

PACK_COLS = 1024
EARLY = ((("w_in", 308, 320, True), ("w_uq", 24, 32, True), ("w_ukv", 16, 16, True)), 384)
LATE = ((("w_out", 128, 128, False), ("w_up", 704, 704, True), ("w_down", 352, 352, False)), 1200)
BIG_NAMES = ("w_in", "w_uq", "w_ukv", "w_out", "w_up", "w_down")
SMALL = (("attn_norm_w", 1024), ("ret_gn_w", 512), ("mla_q_norm_w", 256), ("mla_kv_norm_w", 128),
         ("ffn_norm_w", 1024), ("conv_b", 5632), ("final_norm_w", 1024))
SMALL_VECTORS = SMALL + (("conv_w0", 5632), ("conv_w1", 5632), ("conv_w2", 5632), ("loss", 128))
SMALL_ROWS = 32


def _cp(sem=None, vmem=VMEM_LIMIT):
    return pltpu.CompilerParams(dimension_semantics=sem, vmem_limit_bytes=vmem)


def _dot(a, b):
    return jnp.dot(a, b, preferred_element_type=F32)


def _dot_nt(a, b):
    return lax.dot_general(a, b, (((1,), (1,)), ((), ())), preferred_element_type=F32)


def _dot_tn(a, b):
    return lax.dot_general(a, b, (((0,), (0,)), ((), ())), preferred_element_type=F32)


def _sigmoid(x):
    return 0.5 * jnp.tanh(0.5 * x) + 0.5


def _partner(x, half, period):
    n = x.shape[-1]
    lane = lax.broadcasted_iota(jnp.int32, x.shape, 1)
    return jnp.where((lane % period) < half, pltpu.roll(x, n - half, 1), pltpu.roll(x, half, 1))


def _rope(x, cos, ss, half, period):
    return x * cos + _partner(x, half, period) * ss


def _rope_t(dy, cos, ss, half, period):
    return dy * cos - _partner(dy, half, period) * ss


def _head_masks(shape):
    lane = lax.broadcasted_iota(jnp.int32, shape, 1)
    m0 = (lane < 64).astype(F32)
    return m0, 1.0 - m0


def _mm(a, b, *, name, add=None, out_dtype=F32, bt=False):
    parts = a if isinstance(a, tuple) else (a,)
    M = parts[0].shape[0]
    K = sum(p.shape[1] for p in parts)
    N = b.shape[0] if bt else b.shape[1]
    osz = jnp.dtype(out_dtype).itemsize
    per_row = 2 * (K * parts[0].dtype.itemsize + N * osz + (N * 4 if add is not None else 0))
    tm = 128
    for cand in (512, 256):
        if M % cand == 0 and cand * per_row + 4 * K * N <= MM_BUDGET:
            tm = cand
            break
    tm = min(tm, M)
    mul = _dot_nt if bt else _dot
    n_a = len(parts)
    n_in = n_a + (1 if add is None else 2)

    def body(*refs):
        av = refs[0][...] if n_a == 1 else jnp.concatenate([r[...] for r in refs[:n_a]], axis=1)
        acc = mul(av.astype(BF16), refs[n_a][...])
        if add is not None:
            acc = refs[n_a + 1][...] + acc
        refs[n_in][...] = acc.astype(out_dtype)

    in_specs = [pl.BlockSpec((tm, p.shape[1]), lambda i: (i, 0)) for p in parts]
    in_specs.append(pl.BlockSpec(b.shape, lambda i: (0, 0)))
    args = [*parts, b]
    if add is not None:
        in_specs.append(pl.BlockSpec((tm, N), lambda i: (i, 0)))
        args.append(add)
    return pl.pallas_call(
        body, name=name, grid=(M // tm,), in_specs=in_specs, out_specs=pl.BlockSpec((tm, N), lambda i: (i, 0)),
        out_shape=jax.ShapeDtypeStruct((M, N), out_dtype), compiler_params=_cp(("parallel",)))(*args)


def _mm_tn(a, b, *, name):
    parts = a if isinstance(a, tuple) else (a,)
    T = parts[0].shape[0]
    M = sum(p.shape[1] for p in parts)
    N = b.shape[1]
    tk = min(T, 512)

    def tile(n):
        for cand in (1408, 1280):
            if n > 1408 and n % cand == 0:
                return cand
        return n

    tm, tn = tile(M), tile(N)
    nk = T // tk
    n_a = len(parts)
    assert n_a == 1 or tm == M

    def body(*refs):
        o_ref = refs[n_a + 1]

        @pl.when(pl.program_id(2) == 0)
        def _():
            o_ref[...] = jnp.zeros_like(o_ref)
        av = refs[0][...] if n_a == 1 else jnp.concatenate([r[...] for r in refs[:n_a]], axis=1)
        o_ref[...] += _dot_tn(av.astype(BF16), refs[n_a][...].astype(BF16))

    if n_a == 1:
        a_specs = [pl.BlockSpec((tk, tm), lambda i, j, k: (k, i))]
    else:
        a_specs = [pl.BlockSpec((tk, p.shape[1]), lambda i, j, k: (k, 0)) for p in parts]
    return pl.pallas_call(
        body, name=name, grid=(M // tm, N // tn, nk),
        in_specs=a_specs + [pl.BlockSpec((tk, tn), lambda i, j, k: (k, j))],
        out_specs=pl.BlockSpec((tm, tn), lambda i, j, k: (i, j)),
        out_shape=jax.ShapeDtypeStruct((M, N), F32),
        compiler_params=_cp(("parallel", "parallel", "arbitrary")))(*parts, b)


def _rmsnorm_fwd(x, w, *, name, gather=None):
    T, D = x.shape
    tm = min(T, 1024)
    n = T // tm

    def body(x_ref, w_ref, *rest):
        if gather is not None:
            s_ref, o_ref, g_ref, *sems = rest
            start, forward, finish = _gather_phases(s_ref, g_ref, *sems)
            pl.when(pl.program_id(0) == 0)(start)
            pl.when(pl.program_id(0) == n // 2)(forward)
        else:
            o_ref, = rest
        xv = x_ref[...]
        r = lax.rsqrt(jnp.mean(xv * xv, axis=-1, keepdims=True) + EPS)
        o_ref[...] = (xv * r * w_ref[...]).astype(BF16)
        if gather is not None:
            pl.when(pl.program_id(0) == n - 1)(finish)

    in_specs = [pl.BlockSpec((tm, D), lambda i: (i, 0)), pl.BlockSpec((1, D), lambda i: (0, 0))]
    out_spec = pl.BlockSpec((tm, D), lambda i: (i, 0))
    out_shape = jax.ShapeDtypeStruct((T, D), BF16)
    if gather is None:
        return pl.pallas_call(body, name=name, grid=(n,), in_specs=in_specs, out_specs=out_spec, out_shape=out_shape,
                              compiler_params=_cp(("parallel",)))(x, w)
    return pl.pallas_call(
        body, name=name, grid=(n,), in_specs=in_specs + [ANY], out_specs=[out_spec, ANY],
        out_shape=[out_shape, jax.ShapeDtypeStruct((N_DEV,) + gather.shape, gather.dtype)],
        scratch_shapes=list(GATHER_SCRATCH), compiler_params=_cp(("arbitrary",)))(x, w, gather)


def _mm_norm_bwd(a, b, x, w, dres, *, name, exchange=None):
    T, K = a.shape
    D = b.shape[1]
    tm = min(T, 256 if K > 4096 else 512)
    n = T // tm

    def body(a_ref, b_ref, x_ref, w_ref, dr_ref, *rest):
        if exchange is None:
            dx_ref, dw_ref = rest
        else:
            p_ref, dx_ref, dw_ref, got_ref, *sems = rest
            start, finish = _exchange_phases(p_ref, got_ref, *sems)
            pl.when(pl.program_id(0) == 0)(start)

        @pl.when(pl.program_id(0) == 0)
        def _():
            dw_ref[...] = jnp.zeros_like(dw_ref)
        dh = _dot(a_ref[...], b_ref[...])
        xv = x_ref[...]
        r = lax.rsqrt(jnp.mean(xv * xv, axis=-1, keepdims=True) + EPS)
        xh = xv * r
        g = dh * w_ref[...]
        dx_ref[...] = dr_ref[...] + r * (g - xh * jnp.mean(g * xh, axis=-1, keepdims=True))
        dw_ref[...] += jnp.sum(dh * xh, axis=0, keepdims=True)
        if exchange is not None:
            pl.when(pl.program_id(0) == n - 1)(finish)

    row = pl.BlockSpec((tm, D), lambda i: (i, 0))
    vec = pl.BlockSpec((1, D), lambda i: (0, 0))
    in_specs = [pl.BlockSpec((tm, K), lambda i: (i, 0)), pl.BlockSpec((K, D), lambda i: (0, 0)), row, vec, row]
    out_shape = [jax.ShapeDtypeStruct((T, D), F32), jax.ShapeDtypeStruct((1, D), F32)]
    if exchange is None:
        return pl.pallas_call(body, name=name, grid=(n,), in_specs=in_specs, out_specs=[row, vec], out_shape=out_shape,
                              compiler_params=_cp(("arbitrary",)))(a, b, x, w, dres)
    return pl.pallas_call(
        body, name=name, grid=(n,), in_specs=in_specs + [ANY], out_specs=[row, vec, ANY],
        out_shape=out_shape + [jax.ShapeDtypeStruct(exchange.shape, exchange.dtype)],
        scratch_shapes=list(EXCHANGE_SCRATCH), compiler_params=_cp(("arbitrary",)))(a, b, x, w, dres, exchange)


def _down_proj_loss(a, w_down, x1, tgt, w, *, name):
    T, D = x1.shape
    K = a.shape[1]
    tm = min(T, 512)

    def body(a_ref, b_ref, x_ref, t_ref, w_ref, loss_ref, dx_ref, dxb_ref, dw_ref):
        @pl.when(pl.program_id(0) == 0)
        def _():
            dw_ref[...] = jnp.zeros_like(dw_ref)
            loss_ref[...] = jnp.zeros_like(loss_ref)
        xv = x_ref[...] + _dot(a_ref[...], b_ref[...])
        wv = w_ref[...]
        r = lax.rsqrt(jnp.mean(xv * xv, axis=-1, keepdims=True) + EPS)
        xh = xv * r
        e = xh * wv - t_ref[...]
        part = 0.5 * jnp.sum(jnp.mean(e * e, axis=-1, keepdims=True), axis=0, keepdims=True)
        loss_ref[...] += jnp.broadcast_to(part, loss_ref.shape)
        dy = e * (1.0 / D)
        g = dy * wv
        dx = r * (g - xh * jnp.mean(g * xh, axis=-1, keepdims=True))
        dx_ref[...] = dx
        dxb_ref[...] = dx.astype(BF16)
        dw_ref[...] += jnp.sum(dy * xh, axis=0, keepdims=True)

    row = pl.BlockSpec((tm, D), lambda i: (i, 0))
    vec = pl.BlockSpec((1, D), lambda i: (0, 0))
    return pl.pallas_call(
        body, name=name, grid=(T // tm,),
        in_specs=[pl.BlockSpec((tm, K), lambda i: (i, 0)), pl.BlockSpec((K, D), lambda i: (0, 0)), row, row, vec],
        out_specs=[pl.BlockSpec((1, 128), lambda i: (0, 0)), row, row, vec],
        out_shape=[jax.ShapeDtypeStruct((1, 128), F32), jax.ShapeDtypeStruct((T, D), F32),
                   jax.ShapeDtypeStruct((T, D), BF16), jax.ShapeDtypeStruct((1, D), F32)],
        compiler_params=_cp(("arbitrary",)))(a, w_down, x1, tgt, w)


def _ret_tables():
    C = RET_CHUNK
    h = jnp.arange(RET_HEADS, dtype=F32)
    log_gamma = jnp.log1p(-jnp.power(2.0, -5.0 - h))
    idx = jnp.arange(C, dtype=F32)
    diff = idx[:, None] - idx[None, :]
    dm = jnp.where(diff >= 0, jnp.exp(log_gamma[:, None, None] * jnp.maximum(diff, 0.0)), 0.0)
    dm = dm.reshape(4, 2 * C, C)
    lane_head = jnp.repeat(jnp.arange(RET_HEADS).reshape(4, 2), 64, axis=1)
    lg = log_gamma[lane_head]
    xi = jnp.exp(lg[:, None, :] * (idx[None, :, None] + 1.0))
    zeta = jnp.exp(lg[:, None, :] * (C - 1.0 - idx[None, :, None]))
    blk = (jnp.arange(128)[:, None] // 64) == (jnp.arange(128)[None, :] // 64)
    cd = jnp.where(blk[None], jnp.exp(lg * C)[:, :, None], 0.0)
    return dm.astype(F32), xi.astype(F32), zeta.astype(F32), cd.astype(F32)


def _ret_specs(tb, rev, nt):
    def tmap(t):
        return (nt - 1 - t) if rev else t
    qkv = [pl.BlockSpec((tb, 128), lambda p, t, o=o: (tmap(t), o + p)) for o in (0, 4, 8)]
    rope = [pl.BlockSpec((tb, 128), lambda p, t: (tmap(t), 0))] * 2
    tabs = [pl.BlockSpec((None, 256, 128), lambda p, t: (p, 0, 0))] + \
           [pl.BlockSpec((None, 128, 128), lambda p, t: (p, 0, 0))] * 3
    return qkv, rope, tabs


def _ret_fwd(proj, cos, ss, tabs, gnw, *, name):
    T = proj.shape[0]
    tb = min(T, 1024)
    nt = T // tb
    nchunk = tb // RET_CHUNK

    def body(q_ref, k_ref, v_ref, g_ref, cos_ref, ss_ref, dm_ref, xi_ref, zt_ref, cd_ref, gnw_ref,
             y_ref, o_ref, r_sc):
        @pl.when(pl.program_id(1) == 0)
        def _():
            r_sc[...] = jnp.zeros_like(r_sc)
        m0, m1 = _head_masks((128, 128))
        dm, xi, zt, cd = dm_ref[...], xi_ref[...], zt_ref[...], cd_ref[...]
        bm = (cd > 0).astype(F32)
        gnw = gnw_ref[...]
        for c in range(nchunk):
            rs = pl.ds(c * RET_CHUNK, RET_CHUNK)
            cs, sn = cos_ref[rs, :], ss_ref[rs, :]
            q = _rope(q_ref[rs, :], cs, sn, 32, 64)
            k = _rope(k_ref[rs, :], cs, sn, 32, 64) * K_SCALE
            v = v_ref[rs, :]
            kb, vb = k.astype(BF16), v.astype(BF16)
            qs = jnp.concatenate([q * m0, q * m1], axis=0).astype(BF16)
            s = (_dot_nt(qs, kb) * dm).astype(BF16)
            vs = jnp.concatenate([v * m0, v * m1], axis=0).astype(BF16)
            o = _dot(jnp.concatenate([s[:128], s[128:]], axis=1), vs)
            r = r_sc[...]
            o = o + _dot(q.astype(BF16), r.astype(BF16)) * xi
            r_sc[...] = cd * r + bm * _dot_tn((k * zt).astype(BF16), vb)
            mu = (jnp.sum(o * m0, axis=1, keepdims=True) * m0 + jnp.sum(o * m1, axis=1, keepdims=True) * m1) * (1.0 / 64)
            d = o - mu
            dd = d * d
            var = (jnp.sum(dd * m0, axis=1, keepdims=True) * m0 + jnp.sum(dd * m1, axis=1, keepdims=True) * m1) * (1.0 / 64)
            oh = d * lax.rsqrt(var + EPS)
            g = g_ref[rs, :]
            y_ref[rs, :] = (g * _sigmoid(g) * (oh * gnw)).astype(BF16)
            o_ref[rs, :] = o

    qkv, rope, tspec = _ret_specs(tb, False, nt)
    gspec = pl.BlockSpec((tb, 128), lambda p, t: (t, 12 + p))
    out = pl.BlockSpec((tb, 128), lambda p, t: (t, p))
    return pl.pallas_call(
        body, name=name, grid=(4, nt),
        in_specs=qkv + [gspec] + rope + tspec + [pl.BlockSpec((1, 128), lambda p, t: (0, p))],
        out_specs=[out, out],
        out_shape=[jax.ShapeDtypeStruct((T, RET_WIDTH), BF16), jax.ShapeDtypeStruct((T, RET_WIDTH), F32)],
        scratch_shapes=[pltpu.VMEM((128, 128), F32)],
        compiler_params=_cp(("parallel", "arbitrary")))(proj, proj, proj, proj, cos, ss, *tabs, gnw)


def _ret_bwd_dq(proj, do, cos, ss, tabs, *, name):
    T = proj.shape[0]
    tb = min(T, 1024)
    nt = T // tb
    nchunk = tb // RET_CHUNK

    def body(q_ref, k_ref, v_ref, do_ref, cos_ref, ss_ref, dm_ref, xi_ref, zt_ref, cd_ref, dq_ref, r_sc):
        del q_ref
        @pl.when(pl.program_id(1) == 0)
        def _():
            r_sc[...] = jnp.zeros_like(r_sc)
        m0, m1 = _head_masks((128, 128))
        dm, xi, zt, cd = dm_ref[...], xi_ref[...], zt_ref[...], cd_ref[...]
        bm = (cd > 0).astype(F32)
        for c in range(nchunk):
            rs = pl.ds(c * RET_CHUNK, RET_CHUNK)
            cs, sn = cos_ref[rs, :], ss_ref[rs, :]
            k = _rope(k_ref[rs, :], cs, sn, 32, 64) * K_SCALE
            vb = v_ref[rs, :].astype(BF16)
            dob = do_ref[rs, :]
            dof = dob.astype(F32)
            dos = jnp.concatenate([dof * m0, dof * m1], axis=0).astype(BF16)
            a = (_dot_nt(dos, vb) * dm).astype(BF16)
            ks = jnp.concatenate([k * m0, k * m1], axis=0).astype(BF16)
            r = r_sc[...]
            dq = _dot(jnp.concatenate([a[:128], a[128:]], axis=1), ks) + _dot_nt(dob, r.astype(BF16)) * xi
            r_sc[...] = cd * r + bm * _dot_tn((k * zt).astype(BF16), vb)
            dq_ref[rs, :] = _rope_t(dq, cs, sn, 32, 64).astype(BF16)

    qkv, rope, tspec = _ret_specs(tb, False, nt)
    blk = pl.BlockSpec((tb, 128), lambda p, t: (t, p))
    return pl.pallas_call(
        body, name=name, grid=(4, nt), in_specs=qkv + [blk] + rope + tspec, out_specs=blk,
        out_shape=jax.ShapeDtypeStruct((T, RET_WIDTH), BF16),
        scratch_shapes=[pltpu.VMEM((128, 128), F32)],
        compiler_params=_cp(("parallel", "arbitrary")))(proj, proj, proj, do, cos, ss, *tabs)


def _ret_bwd_dkv(proj, do, cos, ss, tabs, *, name, swap=None):
    T = proj.shape[0]
    tb = min(T, 1024)
    nt = T // tb
    nchunk = tb // RET_CHUNK

    def body(q_ref, k_ref, v_ref, do_ref, cos_ref, ss_ref, dm_ref, xi_ref, zt_ref, cd_ref, *rest):
        if swap is None:
            backward(q_ref, k_ref, v_ref, do_ref, cos_ref, ss_ref, dm_ref, xi_ref, zt_ref, cd_ref, *rest)
        else:
            g_ref, dk_ref, dv_ref, got_ref, u_sc, *sems = rest
            start, finish = _swap_phases(g_ref, got_ref, *sems)
            pl.when((pl.program_id(0) == 0) & (pl.program_id(1) == 0))(start)
            backward(q_ref, k_ref, v_ref, do_ref, cos_ref, ss_ref, dm_ref, xi_ref, zt_ref, cd_ref, dk_ref, dv_ref, u_sc)
            pl.when((pl.program_id(0) == 3) & (pl.program_id(1) == nt - 1))(finish)

    def backward(q_ref, k_ref, v_ref, do_ref, cos_ref, ss_ref, dm_ref, xi_ref, zt_ref, cd_ref, dk_ref, dv_ref, u_sc):
        @pl.when(pl.program_id(1) == 0)
        def _():
            u_sc[...] = jnp.zeros_like(u_sc)
        m0, m1 = _head_masks((128, 128))
        dm, xi, zt, cd = dm_ref[...], xi_ref[...], zt_ref[...], cd_ref[...]
        bm = (cd > 0).astype(F32)
        for c in reversed(range(nchunk)):
            rs = pl.ds(c * RET_CHUNK, RET_CHUNK)
            cs, sn = cos_ref[rs, :], ss_ref[rs, :]
            q = _rope(q_ref[rs, :], cs, sn, 32, 64)
            k = _rope(k_ref[rs, :], cs, sn, 32, 64) * K_SCALE
            kb = k.astype(BF16)
            vb = v_ref[rs, :].astype(BF16)
            dob = do_ref[rs, :]
            dof = dob.astype(F32)
            qs = jnp.concatenate([q * m0, q * m1], axis=0).astype(BF16)
            dos = jnp.concatenate([dof * m0, dof * m1], axis=0).astype(BF16)
            s = (_dot_nt(qs, kb) * dm).astype(BF16)
            a = (_dot_nt(dos, vb) * dm).astype(BF16)
            ub = u_sc[...].astype(BF16)
            dk = _dot_tn(a, qs) + _dot_nt(vb, ub) * zt
            dv = _dot_tn(s, dos) + _dot(kb, ub) * zt
            u_sc[...] = cd * u_sc[...] + bm * _dot_tn((q * xi).astype(BF16), dob)
            dk_ref[rs, :] = (_rope_t(dk, cs, sn, 32, 64) * K_SCALE).astype(BF16)
            dv_ref[rs, :] = dv.astype(BF16)

    qkv, rope, tspec = _ret_specs(tb, True, nt)
    blk = pl.BlockSpec((tb, 128), lambda p, t: (nt - 1 - t, p))
    out_shape = [jax.ShapeDtypeStruct((T, RET_WIDTH), BF16)] * 2
    if swap is None:
        return pl.pallas_call(
            body, name=name, grid=(4, nt), in_specs=qkv + [blk] + rope + tspec, out_specs=[blk, blk],
            out_shape=out_shape, scratch_shapes=[pltpu.VMEM((128, 128), F32)],
            compiler_params=_cp(("parallel", "arbitrary")))(proj, proj, proj, do, cos, ss, *tabs)
    return pl.pallas_call(
        body, name=name, grid=(4, nt), in_specs=qkv + [blk] + rope + tspec + [ANY], out_specs=[blk, blk, ANY],
        out_shape=out_shape + [jax.ShapeDtypeStruct((4,) + swap.shape[2:], swap.dtype)],
        scratch_shapes=[pltpu.VMEM((128, 128), F32)] + list(SWAP_SCRATCH),
        compiler_params=_cp(("arbitrary", "arbitrary")))(proj, proj, proj, do, cos, ss, *tabs, swap)


def _mix_bwd(dmixed, o_ret, proj, y_mla, gnw, *, name):
    T = dmixed.shape[0]
    tm = min(T, 512)

    def body(dm_ref, o_ref, g_ref, ym_ref, gnw_ref, do_ref, dg_ref, dom_ref, dl_ref, dw_ref):
        @pl.when(pl.program_id(0) == 0)
        def _():
            dw_ref[...] = jnp.zeros_like(dw_ref)
        m0, m1 = _head_masks((tm, 128))
        lane = lax.broadcasted_iota(jnp.int32, (tm, 128), 1)
        delta = jnp.zeros((tm, 128), F32)

        def gsum(z):
            return jnp.sum(z * m0, axis=1, keepdims=True) * m0 + jnp.sum(z * m1, axis=1, keepdims=True) * m1

        for p in range(4):
            cs = slice(128 * p, 128 * p + 128)
            dy = dm_ref[:, cs]
            o = o_ref[:, cs]
            g = g_ref[:, cs]
            w = gnw_ref[:, cs]
            d = o - gsum(o) * (1.0 / 64)
            rstd = lax.rsqrt(gsum(d * d) * (1.0 / 64) + EPS)
            oh = d * rstd
            sg = _sigmoid(g)
            dn = dy * (g * sg)
            dg_ref[:, cs] = (dy * (oh * w) * (sg * (1.0 + g * (1.0 - sg)))).astype(BF16)
            dw_ref[:, cs] += jnp.sum(dn * oh, axis=0, keepdims=True)
            doh = dn * w
            do = rstd * (doh - gsum(doh) * (1.0 / 64) - oh * (gsum(doh * oh) * (1.0 / 64)))
            do_ref[:, cs] = do.astype(BF16)
            dom = dm_ref[:, 512 + 128 * p:512 + 128 * p + 128]
            dom_ref[:, cs] = dom.astype(BF16)
            pr = dom * ym_ref[:, cs].astype(F32)
            delta = jnp.where(lane == 2 * p, jnp.sum(pr * m0, axis=1, keepdims=True), delta)
            delta = jnp.where(lane == 2 * p + 1, jnp.sum(pr * m1, axis=1, keepdims=True), delta)
        dl_ref[...] = delta.T[0:MLA_HEADS]

    half = pl.BlockSpec((tm, 512), lambda i: (i, 0))
    return pl.pallas_call(
        body, name=name, grid=(T // tm,),
        in_specs=[pl.BlockSpec((tm, 1024), lambda i: (i, 0)), half, pl.BlockSpec((tm, 512), lambda i: (i, 3)),
                  half, pl.BlockSpec((1, 512), lambda i: (0, 0))],
        out_specs=[half, half, half, pl.BlockSpec((MLA_HEADS, tm), lambda i: (0, i)),
                   pl.BlockSpec((1, 512), lambda i: (0, 0))],
        out_shape=[jax.ShapeDtypeStruct((T, 512), BF16)] * 3 + [jax.ShapeDtypeStruct((MLA_HEADS, T), F32),
                                                                jax.ShapeDtypeStruct((1, 512), F32)],
        compiler_params=_cp(("arbitrary",)))(dmixed, o_ret, proj, y_mla, gnw)


def _mla_prep_fwd(proj, qnw, kvnw, wuq, wk, wv, cos, ss, *, name):
    T = proj.shape[0]
    tm = min(T, 512)

    def body(lat_ref, qnw_ref, kvnw_ref, wuq_ref, wk_ref, wv_ref, cos_ref, ss_ref,
             q_ref, k_ref, v_ref, cqn_ref, ckvn_ref):
        cq = lat_ref[:, 0:256]
        ckv = lat_ref[:, 256:384]
        g3 = lat_ref[:, 384:512]
        cqn = (cq * lax.rsqrt(jnp.mean(cq * cq, axis=-1, keepdims=True) + EPS) * qnw_ref[...]).astype(BF16)
        ckvn = (ckv * lax.rsqrt(jnp.mean(ckv * ckv, axis=-1, keepdims=True) + EPS) * kvnw_ref[...]).astype(BF16)
        cqn_ref[...] = cqn
        ckvn_ref[...] = ckvn
        cs, sn = cos_ref[...], ss_ref[...]
        q = _dot_nt(cqn, wuq_ref[...])
        k = _dot_nt(ckvn, wk_ref[...])
        kpe = _rope(g3, cs, sn, 16, 32)
        for h in range(MLA_HEADS):
            hs = slice(128 * h, 128 * h + 128)
            q_ref[:, hs] = (_rope(q[:, hs], cs, sn, 16, 32) * SCALE).astype(BF16)
            k_ref[:, hs] = (k[:, hs] + kpe).astype(BF16)
        v = _dot_nt(ckvn, wv_ref[...])
        lane = lax.broadcasted_iota(jnp.int32, (tm, 128), 1)
        for p in range(4):
            vp = v[:, 128 * p:128 * p + 128]
            v_ref[:, 256 * p:256 * p + 128] = jnp.where(lane < 64, vp, 1.0).astype(BF16)
            v_ref[:, 256 * p + 128:256 * p + 256] = jnp.where(lane < 64, 1.0, vp).astype(BF16)

    def full(shape):
        return pl.BlockSpec(shape, lambda i: (0, 0))

    def row(w):
        return pl.BlockSpec((tm, w), lambda i: (i, 0))

    return pl.pallas_call(
        body, name=name, grid=(T // tm,),
        in_specs=[pl.BlockSpec((tm, 512), lambda i: (i, 4)), full((1, 256)), full((1, 128)), full((1024, 256)),
                  full((1024, 128)), full((512, 128)), row(128), row(128)],
        out_specs=[row(1024), row(1024), row(1024), row(256), row(128)],
        out_shape=[jax.ShapeDtypeStruct((T, 1024), BF16), jax.ShapeDtypeStruct((T, 1024), BF16),
                   jax.ShapeDtypeStruct((T, 1024), BF16), jax.ShapeDtypeStruct((T, 256), BF16),
                   jax.ShapeDtypeStruct((T, 128), BF16)],
        compiler_params=_cp(("parallel",)))(proj, qnw, kvnw, wuq, wk, wv, cos, ss)


def _mla_prep_bwd(dq, dk, dv, proj, qnw, kvnw, wuq_t, wk_t, wv_t, cos, ss, ret_grads, cqn, ckvn, h, *, name):
    T = proj.shape[0]
    tm = min(T, 256)

    def body(dq_ref, dk_ref, dv_ref, lat_ref, qnw_ref, kvnw_ref, wuq_ref, wk_ref, wv_ref, cos_ref, ss_ref,
             rq_ref, rk_ref, rv_ref, rg_ref, cqn_ref, ckvn_ref, h_ref,
             dproj_ref, gwin_ref, gwuq_ref, gwk_ref, gwv_ref, dqnw_ref, dkvnw_ref, dqp_ref):
        for j, r in enumerate((rq_ref, rk_ref, rv_ref, rg_ref)):
            dproj_ref[:, 512 * j:512 * j + 512] = r[...]
        dlat_ref = dproj_ref.at[:, 2048:2560]

        @pl.when(pl.program_id(0) == 0)
        def _():
            for r in (gwin_ref, gwuq_ref, gwk_ref, gwv_ref, dqnw_ref, dkvnw_ref):
                r[...] = jnp.zeros_like(r)
        cs, sn = cos_ref[...], ss_ref[...]
        dkpe = jnp.zeros((tm, 128), F32)
        for h in range(MLA_HEADS):
            hs = slice(128 * h, 128 * h + 128)
            dqp_ref[:, hs] = _rope_t(dq_ref[:, hs] * SCALE, cs, sn, 16, 32).astype(BF16)
            dkpe = dkpe + dk_ref[:, hs]
        lane = lax.broadcasted_iota(jnp.int32, (tm, 128), 1)
        rope_lane = (lane >= MLA_NOPE) & (lane < MLA_NOPE + MLA_ROPE)
        dg3 = jnp.where(rope_lane, _rope_t(jnp.where(rope_lane, dkpe, 0.0), cs, sn, 16, 32), 0.0)

        def norm_bwd(x, w, dn):
            r = lax.rsqrt(jnp.mean(x * x, axis=-1, keepdims=True) + EPS)
            xh = x * r
            g = dn * w
            return r * (g - xh * jnp.mean(g * xh, axis=-1, keepdims=True)), jnp.sum(dn * xh, axis=0, keepdims=True)

        dqp = dqp_ref[...]
        dkb = dk_ref[...].astype(BF16)
        dvb = dv_ref[...]
        dcqn = _dot(dqp, wuq_ref[...])
        dcq, dqnw = norm_bwd(lat_ref[:, 0:256], qnw_ref[...], dcqn)
        dckvn = _dot(dkb, wk_ref[...]) + _dot(dvb, wv_ref[...])
        dckv, dkvnw = norm_bwd(lat_ref[:, 256:384], kvnw_ref[...], dckvn)
        gwuq_ref[...] += _dot_tn(dqp, cqn_ref[...])
        gwk_ref[...] += _dot_tn(dkb, ckvn_ref[...])
        gwv_ref[...] += _dot_tn(dvb, ckvn_ref[...])
        dqnw_ref[...] += dqnw
        dkvnw_ref[...] += dkvnw
        dlat_ref[:, 0:256] = dcq.astype(BF16)
        dlat_ref[:, 256:384] = dckv.astype(BF16)
        dlat_ref[:, 384:512] = dg3.astype(BF16)
        gwin_ref[...] += _dot_tn(dproj_ref[...], h_ref[...])

    def full(shape):
        return pl.BlockSpec(shape, lambda i: (0, 0))

    def row(w):
        return pl.BlockSpec((tm, w), lambda i: (i, 0))

    return pl.pallas_call(
        body, name=name, grid=(T // tm,),
        in_specs=[row(1024), row(1024), row(512), pl.BlockSpec((tm, 512), lambda i: (i, 4)), full((1, 256)),
                  full((1, 128)), full((1024, 256)), full((1024, 128)), full((512, 128)), row(128), row(128)]
                 + [row(512)] * 4 + [row(256), row(128), row(D_MODEL)],
        out_specs=[row(IN_PAD), full((IN_PAD, D_MODEL)), full((1024, 256)), full((1024, 128)), full((512, 128)),
                   full((1, 256)), full((1, 128))],
        out_shape=[jax.ShapeDtypeStruct((T, IN_PAD), BF16), jax.ShapeDtypeStruct((IN_PAD, D_MODEL), F32),
                   jax.ShapeDtypeStruct((1024, 256), F32), jax.ShapeDtypeStruct((1024, 128), F32),
                   jax.ShapeDtypeStruct((512, 128), F32), jax.ShapeDtypeStruct((1, 256), F32),
                   jax.ShapeDtypeStruct((1, 128), F32)],
        scratch_shapes=[pltpu.VMEM((tm, 1024), BF16)],
        compiler_params=_cp(("arbitrary",)))(dq, dk, dv, proj, qnw, kvnw, wuq_t, wk_t, wv_t, cos, ss, *ret_grads,
                                             cqn, ckvn, h)


def _flash_fwd(q, k, v1, *, name, gather=None):
    T = q.shape[0]
    tq = min(T, 512)
    tk = tq
    nq = T // tq

    def body(q_ref, k_ref, v_ref, *rest):
        if gather is None:
            y_ref, lse_ref = rest
        else:
            x_ref, y_ref, lse_ref, g_ref, *sems = rest
            start, forward, finish = _gather_phases(x_ref, g_ref, *sems)
            pl.when((pl.program_id(0) == 0) & (pl.program_id(1) == 0))(start)
            pl.when((pl.program_id(0) == 1) & (pl.program_id(1) == 0))(forward)
        attend(q_ref, k_ref, v_ref, y_ref, lse_ref)
        if gather is not None:
            pl.when((pl.program_id(0) == 3) & (pl.program_id(1) == nq - 1))(finish)

    def attend(q_ref, k_ref, v_ref, y_ref, lse_ref):
        qi = pl.program_id(1)
        row = lax.broadcasted_iota(jnp.int32, (tq, tk), 0)
        col = lax.broadcasted_iota(jnp.int32, (tq, tk), 1)

        def step(kb, carry, masked):
            ks = pl.ds(pl.multiple_of(kb * tk, tk), tk)
            new = []
            for h in range(2):
                hs = slice(128 * h, 128 * h + 128)
                m, acc = carry[h]
                s = _dot_nt(q_ref[:, hs], k_ref[ks, hs])
                if masked:
                    s = jnp.where(col <= row, s, NEG)
                mn = jnp.maximum(m, jnp.max(s, axis=1, keepdims=True))
                p = jnp.exp((s - mn).astype(BF16))
                acc = jnp.exp(m - mn) * acc + _dot(p, v_ref[ks, hs])
                new.append((mn, acc))
            return tuple(new)

        def unrolled(j, c):
            for u in range(FLASH_UNROLL):
                c = step(FLASH_UNROLL * j + u, c, False)
            return c

        init = (jnp.full((tq, 1), NEG, F32), jnp.zeros((tq, 128), F32))
        carry = lax.fori_loop(0, qi // FLASH_UNROLL, unrolled, (init, init))
        carry = lax.fori_loop(FLASH_UNROLL * (qi // FLASH_UNROLL), qi, lambda kb, c: step(kb, c, False), carry)
        (ma, acca), (mb, accb) = step(qi, carry, True)
        lane = lax.broadcasted_iota(jnp.int32, (tq, 128), 1)
        la, lb = pltpu.roll(acca, 64, 1), pltpu.roll(accb, 64, 1)
        y_ref[...] = jnp.where(lane < 64, acca / la, accb / lb).astype(BF16)
        lse_ref[0, 0] = jnp.broadcast_to(ma + jnp.log(acca[:, 64:65]), (tq, 128)).T[0:1]
        lse_ref[1, 0] = jnp.broadcast_to(mb + jnp.log(accb[:, 0:1]), (tq, 128)).T[0:1]

    in_specs = [pl.BlockSpec((tq, 256), lambda p, i: (i, p)), pl.BlockSpec((T, 256), lambda p, i: (0, p)),
                pl.BlockSpec((T, 256), lambda p, i: (0, p))]
    out_specs = [pl.BlockSpec((tq, 128), lambda p, i: (i, p)), pl.BlockSpec((2, 1, 1, tq), lambda p, i: (p, i, 0, 0))]
    out_shape = [jax.ShapeDtypeStruct((T, MLA_WIDTH), BF16), jax.ShapeDtypeStruct((MLA_HEADS, nq, 1, tq), F32)]
    if gather is None:
        return pl.pallas_call(body, name=name, grid=(4, nq), in_specs=in_specs, out_specs=out_specs,
                              out_shape=out_shape, compiler_params=_cp(("parallel", "arbitrary")))(q, k, v1)
    return pl.pallas_call(
        body, name=name, grid=(4, nq), in_specs=in_specs + [ANY], out_specs=out_specs + [ANY],
        out_shape=out_shape + [jax.ShapeDtypeStruct((N_DEV,) + gather.shape, gather.dtype)],
        scratch_shapes=list(GATHER_SCRATCH),
        compiler_params=_cp(("arbitrary", "arbitrary")))(q, k, v1, gather)


def _flash_bwd(q, k, v, do, lse, delta, *, name, exchange=None):
    T = q.shape[0]
    tq = min(T, 512)
    tk = tq
    nq = T // tq

    def body(q_ref, k_ref, v_ref, do_ref, lse_ref, dl_ref, *rest):
        if exchange is None:
            backward(q_ref, k_ref, v_ref, do_ref, lse_ref, dl_ref, *rest)
        else:
            p_ref, dqt_ref, dk_ref, dv_ref, got_ref, *sems = rest
            start, finish = _exchange_phases(p_ref, got_ref, *sems)
            pl.when((pl.program_id(0) == 0) & (pl.program_id(1) == 0))(start)
            backward(q_ref, k_ref, v_ref, do_ref, lse_ref, dl_ref, dqt_ref, dk_ref, dv_ref)
            pl.when((pl.program_id(0) == 3) & (pl.program_id(1) == nq - 1))(finish)

    def backward(q_ref, k_ref, v_ref, do_ref, lse_ref, dl_ref, dqt_ref, dk_ref, dv_ref):
        kb = pl.program_id(1)

        @pl.when(kb == 0)
        def _():
            dqt_ref[...] = jnp.zeros_like(dqt_ref)
        krow = lax.broadcasted_iota(jnp.int32, (tk, tq), 0)
        qcol = lax.broadcasted_iota(jnp.int32, (tk, tq), 1)
        masks = _head_masks((tk, 128))
        vms = [(v_ref[:, 128 * h:128 * h + 128].astype(F32) * masks[h]).astype(BF16) for h in range(2)]

        def step(qi, carry, masked):
            qs = pl.ds(pl.multiple_of(qi * tq, tq), tq)
            dob = do_ref[qs, :]
            dof = dob.astype(F32)
            dks, dv_acc = list(carry[:2]), carry[2]
            for h in range(2):
                hs = slice(128 * h, 128 * h + 128)
                kh = k_ref[:, hs]
                qh = q_ref[qs, hs]
                st = _dot_nt(kh, qh)
                pt = jnp.exp((st - lse_ref[h, qi]).astype(BF16))
                if masked:
                    pt = jnp.where(krow <= qcol, pt, jnp.zeros_like(pt))
                dv_acc = dv_acc + _dot(pt, (dof * masks[h]).astype(BF16))
                dpt = _dot_nt(vms[h], dob)
                dst = pt * (dpt - dl_ref[h, qi]).astype(BF16)
                dks[h] = dks[h] + _dot(dst, qh)
                dqt_ref[qi, hs, :] += _dot_tn(kh, dst)
            return dks[0], dks[1], dv_acc

        zero = jnp.zeros((tk, 128), F32)
        carry = step(kb, (zero, zero, zero), True)

        def unrolled(j, c):
            for u in range(FLASH_BWD_UNROLL):
                c = step(kb + 1 + FLASH_BWD_UNROLL * j + u, c, False)
            return c

        trips = (nq - 1 - kb) // FLASH_BWD_UNROLL
        carry = lax.fori_loop(0, trips, unrolled, carry)
        dk0, dk1, dv_acc = lax.fori_loop(kb + 1 + FLASH_BWD_UNROLL * trips, nq, lambda qi, c: step(qi, c, False), carry)
        dk_ref[:, 0:128] = dk0
        dk_ref[:, 128:256] = dk1
        dv_ref[...] = dv_acc.astype(BF16)

    stat = pl.BlockSpec((2, nq, 1, tq), lambda p, j: (p, 0, 0, 0))
    in_specs = [pl.BlockSpec((T, 256), lambda p, j: (0, p)), pl.BlockSpec((tk, 256), lambda p, j: (j, p)),
                pl.BlockSpec((tk, 256), lambda p, j: (j, p)), pl.BlockSpec((T, 128), lambda p, j: (0, p)), stat, stat]
    out_specs = [pl.BlockSpec((None, nq, 256, tq), lambda p, j: (p, 0, 0, 0)),
                 pl.BlockSpec((tk, 256), lambda p, j: (j, p)), pl.BlockSpec((tk, 128), lambda p, j: (j, p))]
    out_shape = [jax.ShapeDtypeStruct((4, nq, 256, tq), F32), jax.ShapeDtypeStruct((T, 1024), F32),
                 jax.ShapeDtypeStruct((T, MLA_WIDTH), BF16)]
    if exchange is None:
        return pl.pallas_call(body, name=name, grid=(4, nq), in_specs=in_specs, out_specs=out_specs,
                              out_shape=out_shape,
                              compiler_params=_cp(("parallel", "arbitrary")))(q, k, v, do, lse, delta)
    return pl.pallas_call(
        body, name=name, grid=(4, nq), in_specs=in_specs + [ANY], out_specs=out_specs + [ANY],
        out_shape=out_shape + [jax.ShapeDtypeStruct(exchange.shape, exchange.dtype)],
        scratch_shapes=list(EXCHANGE_SCRATCH),
        compiler_params=_cp(("arbitrary", "arbitrary")))(q, k, v, do, lse, delta, exchange)


def _shift_down(x, n, prev8):
    r = pltpu.roll(x, n, 0)
    row = lax.broadcasted_iota(jnp.int32, prev8.shape, 0)
    first = jnp.where(row < n, pltpu.roll(prev8, n, 0), r[:8])
    if x.shape[0] == 8:
        return first
    return jnp.concatenate([first, r[8:]], axis=0)


def _shift_up(x, n, next8):
    tm = x.shape[0]
    r = pltpu.roll(x, tm - n, 0)
    row = lax.broadcasted_iota(jnp.int32, next8.shape, 0)
    last = jnp.where(row >= 8 - n, pltpu.roll(next8, 8 - n, 0), r[tm - 8:])
    return jnp.concatenate([r[:tm - 8], last], axis=0)


def _conv_pre(u, prev8, cw_ref, cb_ref):
    p1 = _shift_down(u, 1, prev8)
    p2 = _shift_down(u, 2, prev8)
    up = cb_ref[...] + cw_ref[0:1, :] * p2 + cw_ref[1:2, :] * p1 + cw_ref[2:3, :] * u
    return up, p1, p2


def _up_proj_conv(x1, nw, w_up_t, cw, cb, *, name):
    T, K = x1.shape
    tm = min(T, 256)

    def body(x_ref, nw_ref, w_ref, cw_ref, cb_ref, h_ref, u_ref, a_ref, carry_sc):
        @pl.when(pl.program_id(0) == 0)
        def _():
            carry_sc[...] = jnp.zeros_like(carry_sc)
        xv = x_ref[...]
        h = (xv * lax.rsqrt(jnp.mean(xv * xv, axis=-1, keepdims=True) + EPS) * nw_ref[...]).astype(BF16)
        h_ref[...] = h
        for blk in range(2):
            ups = []
            for half in range(2):
                cs = slice((2 * blk + half) * FF_HALF, (2 * blk + half + 1) * FF_HALF)
                u = _dot_nt(h, w_ref[cs, :])
                u_ref[:, cs] = u
                prev = carry_sc[:, cs]
                ups.append(cb_ref[:, cs] + cw_ref[0:1, cs] * _shift_down(u, 2, prev)
                           + cw_ref[1:2, cs] * _shift_down(u, 1, prev) + cw_ref[2:3, cs] * u)
                carry_sc[:, cs] = u[tm - 8:]
            gate, val = ups
            a_ref[:, blk * FF_HALF:(blk + 1) * FF_HALF] = (gate * _sigmoid(gate) * val).astype(BF16)

    def full(shape):
        return pl.BlockSpec(shape, lambda i: (0, 0))

    return pl.pallas_call(
        body, name=name, grid=(T // tm,),
        in_specs=[pl.BlockSpec((tm, K), lambda i: (i, 0)), full(nw.shape), full(w_up_t.shape), full(cw.shape),
                  full(cb.shape)],
        out_specs=[pl.BlockSpec((tm, K), lambda i: (i, 0)), pl.BlockSpec((tm, 2 * D_FF), lambda i: (i, 0)),
                   pl.BlockSpec((tm, D_FF), lambda i: (i, 0))],
        out_shape=[jax.ShapeDtypeStruct((T, K), BF16), jax.ShapeDtypeStruct((T, 2 * D_FF), F32),
                   jax.ShapeDtypeStruct((T, D_FF), BF16)],
        scratch_shapes=[pltpu.VMEM((8, 2 * D_FF), F32)],
        compiler_params=_cp(("arbitrary",)))(x1, nw, w_up_t, cw, cb)


def _conv_bwd(u, da, cw, cb, *, name):
    T = u.shape[0]
    tm = min(T, 512)
    W = 2 * FF_HALF
    nt = T // tm

    def body(u_ref, prev_ref, next_ref, da_ref, dan_ref, cw_ref, cb_ref, du_ref, dw0_ref, dw1_ref, dw2_ref, db_ref):
        i = pl.program_id(1)

        @pl.when(i == 0)
        def _():
            for r in (dw0_ref, dw1_ref, dw2_ref, db_ref):
                r[...] = jnp.zeros_like(r)

        def dpre(u, prev8, da):
            up, p1, p2 = _conv_pre(u, prev8, cw_ref, cb_ref)
            gate, val = up[:, :FF_HALF], up[:, FF_HALF:]
            sg = _sigmoid(gate)
            dgate = da * val * (sg * (1.0 + gate * (1.0 - sg)))
            dval = da * (gate * sg)
            return jnp.concatenate([dgate, dval], axis=1), p1, p2

        u = u_ref[...]
        prev = jnp.where(i > 0, prev_ref[...], 0.0)
        dup, p1, p2 = dpre(u, prev, da_ref[...])
        dupn, _, _ = dpre(next_ref[...], u[tm - 8:], dan_ref[...])
        dupn = jnp.where(i < nt - 1, dupn, 0.0)
        du = cw_ref[2:3, :] * dup + cw_ref[1:2, :] * _shift_up(dup, 1, dupn) + cw_ref[0:1, :] * _shift_up(dup, 2, dupn)
        du_ref[...] = du.astype(BF16)
        dw0_ref[...] += jnp.sum(dup * p2, axis=0, keepdims=True)
        dw1_ref[...] += jnp.sum(dup * p1, axis=0, keepdims=True)
        dw2_ref[...] += jnp.sum(dup * u, axis=0, keepdims=True)
        db_ref[...] += jnp.sum(dup, axis=0, keepdims=True)

    nxt = lambda j, i: (jnp.minimum((i + 1) * (tm // 8), T // 8 - 1), j)
    vec = pl.BlockSpec((1, W), lambda j, i: (0, j))
    return pl.pallas_call(
        body, name=name, grid=(2, nt),
        in_specs=[pl.BlockSpec((tm, W), lambda j, i: (i, j)),
                  pl.BlockSpec((8, W), lambda j, i: (jnp.maximum(i * (tm // 8) - 1, 0), j)),
                  pl.BlockSpec((8, W), nxt),
                  pl.BlockSpec((tm, FF_HALF), lambda j, i: (i, j)), pl.BlockSpec((8, FF_HALF), nxt),
                  pl.BlockSpec((3, W), lambda j, i: (0, j)), vec],
        out_specs=[pl.BlockSpec((tm, W), lambda j, i: (i, j)), vec, vec, vec, vec],
        out_shape=[jax.ShapeDtypeStruct((T, 2 * D_FF), BF16)] + [jax.ShapeDtypeStruct((1, 2 * D_FF), F32)] * 4,
        compiler_params=_cp(("parallel", "arbitrary")))(u, u, u, da, da, cw, cb)


def _sum_chips(slots, *, name):
    ns, R, C = slots.shape
    tr = _row_tile(R)

    def body(g_ref, o_ref):
        g = g_ref[0].astype(F32)
        for s in range(1, ns):
            g = g + g_ref[s].astype(F32)
        o_ref[...] = g

    return pl.pallas_call(
        body, name=name, grid=(R // tr,), in_specs=[pl.BlockSpec((ns, tr, C), lambda i: (0, i, 0))],
        out_specs=pl.BlockSpec((tr, C), lambda i: (i, 0)), out_shape=jax.ShapeDtypeStruct((R, C), F32),
        compiler_params=_cp(("parallel",)))(slots)


def _place():
    return lax.axis_index("x"), lax.axis_index("y"), lax.axis_index("c")


GATHER_SCRATCH = (pltpu.SemaphoreType.DMA((7,)), pltpu.SemaphoreType.DMA((7,)), pltpu.SemaphoreType.DMA)
EXCHANGE_SCRATCH = (pltpu.SemaphoreType.DMA((3,)), pltpu.SemaphoreType.DMA((3,)), pltpu.SemaphoreType.DMA)


def _gather_phases(x_ref, out_ref, send_sems, recv_sems, local_sem):
    x_, y_, c_ = _place()
    me, sibling = (x_, y_, c_), (x_, y_, 1 - c_)
    chips = [(1 - x_, y_), (x_, 1 - y_), (1 - x_, 1 - y_)]

    def slot(px, py, pc):
        return out_ref.at[4 * px + 2 * py + pc]

    def copy(k, block, to, src=None):
        return pltpu.make_async_remote_copy(
            src_ref=slot(*block) if src is None else src, dst_ref=slot(*block),
            send_sem=send_sems.at[k], recv_sem=recv_sems.at[k], device_id=to, device_id_type=MESH)

    def mine():
        return pltpu.make_async_copy(x_ref, slot(*me), local_sem)

    def first():
        return [copy(0, me, sibling, src=x_ref)] + [copy(1 + j, me, (*chip, c_), src=x_ref)
                                                     for j, chip in enumerate(chips)]

    def passed():
        return [copy(4 + j, (*chip, c_), sibling) for j, chip in enumerate(chips)]

    def start():
        mine().start()
        for cp in first():
            cp.start()

    def forward():
        fwd = passed()
        for j, chip in enumerate(chips):
            copy(1 + j, (*chip, c_), me).wait_recv()
            fwd[j].start()

    def finish():
        copy(0, sibling, me).wait_recv()
        for j, chip in enumerate(chips):
            copy(4 + j, (*chip, 1 - c_), me).wait_recv()
        for cp in first() + passed():
            cp.wait_send()
        mine().wait()

    return start, forward, finish


def _exchange_phases(p_ref, out_ref, send_sems, recv_sems, local_sem):
    x_, y_, c_ = _place()
    me_k = 2 * x_ + y_
    chips = [(1 - x_, y_), (x_, 1 - y_), (1 - x_, 1 - y_)]

    def local():
        return pltpu.make_async_copy(p_ref.at[me_k], out_ref.at[me_k], local_sem)

    def copy(j, src_k, dst_k, chip):
        return pltpu.make_async_remote_copy(
            src_ref=p_ref.at[src_k], dst_ref=out_ref.at[dst_k], send_sem=send_sems.at[j],
            recv_sem=recv_sems.at[j], device_id=(*chip, c_), device_id_type=MESH)

    def sends():
        return [copy(j, 2 * px + py, me_k, (px, py)) for j, (px, py) in enumerate(chips)]

    def start():
        local().start()
        for cp in sends():
            cp.start()

    def finish():
        for j, (px, py) in enumerate(chips):
            copy(j, me_k, 2 * px + py, (px, py)).wait_recv()
        for cp in sends():
            cp.wait_send()
        local().wait()

    return start, finish


def _all_gather(x, *, name, in_vmem):
    def body(x_ref, out_ref, send_sems, recv_sems, local_sem):
        for phase in _gather_phases(x_ref, out_ref, send_sems, recv_sems, local_sem):
            phase()

    spec = pl.BlockSpec(memory_space=pltpu.VMEM) if in_vmem else ANY
    return pl.pallas_call(
        body, name=name, out_shape=jax.ShapeDtypeStruct((N_DEV,) + x.shape, x.dtype),
        in_specs=[spec], out_specs=spec, scratch_shapes=list(GATHER_SCRATCH),
        compiler_params=pltpu.CompilerParams(vmem_limit_bytes=VMEM_LIMIT))(x)


def _small_rows():
    table, row = [], 0
    for n, size in SMALL_VECTORS:
        table.append((n, size, row))
        row += -(-size // PACK_COLS)
    return table


def _ff_chunk_source(c):
    block, off = divmod(c * 128, FF_HALF)
    return (0, 2, 1, 3)[block] * FF_HALF + off


def _pack_small(parts, *, name):
    table = _small_rows()

    def body(*refs):
        out = refs[-1]
        out[...] = jnp.zeros_like(out)
        for ref, (n, size, row) in zip(refs, table):
            if size != 2 * D_FF:
                out[row:row + 1, 0:size] = ref[...]
                continue
            for c in range(size // 128):
                src = _ff_chunk_source(c)
                r, lane = divmod(c * 128, PACK_COLS)
                out[row + r:row + r + 1, lane:lane + 128] = ref[:, src:src + 128]

    return pl.pallas_call(body, name=name, out_shape=jax.ShapeDtypeStruct((SMALL_ROWS, PACK_COLS), F32))(
        *[parts[n] for n, _, _ in table])


def _sum_small(g, *, name):
    table = _small_rows()
    shapes = [(n, size) for n, size, _ in table if not n.startswith("conv_w")]
    shapes.insert(7, ("conv_w", 2 * D_FF))

    def body(g_ref, *outs):
        def total(row, width):
            acc = g_ref[0, row:row + 1, 0:width]
            for d in range(1, N_DEV):
                acc = acc + g_ref[d, row:row + 1, 0:width]
            return acc

        out_of = {n: o for (n, _), o in zip(shapes, outs)}
        for n, size, row in table:
            o, j = (out_of["conv_w"], int(n[-1])) if n.startswith("conv_w") else (out_of[n], 0)
            for i in range(-(-size // PACK_COLS)):
                width = min(PACK_COLS, size - PACK_COLS * i)
                o[j:j + 1, PACK_COLS * i:PACK_COLS * i + width] = total(row + i, width)

    out_shape = [jax.ShapeDtypeStruct((3 if n == "conv_w" else 1, size), F32) for n, size in shapes]
    res = pl.pallas_call(body, name=name, out_shape=out_shape)(g)
    return {n: r for (n, _), r in zip(shapes, res)}


def _adamw_multi(ws, ms, vs, gs, *, name):
    k = len(ws)

    def body(*refs):
        w_refs, m_refs, v_refs, g_refs = (refs[i * k:(i + 1) * k] for i in range(4))
        outs = refs[4 * k:]
        for i in range(k):
            g = g_refs[i][...]
            mn = ADAM_B1 * m_refs[i][...] + (1.0 - ADAM_B1) * g
            vn = ADAM_B2 * v_refs[i][...] + (1.0 - ADAM_B2) * (g * g)
            m_hat = mn / (1.0 - ADAM_B1 ** ADAM_STEP)
            v_hat = vn / (1.0 - ADAM_B2 ** ADAM_STEP)
            outs[i][...] = g
            outs[k + i][...] = -ADAM_LR * (m_hat / (jnp.sqrt(v_hat) + ADAM_EPS) + ADAM_WD * w_refs[i][...])
            outs[2 * k + i][...] = mn
            outs[3 * k + i][...] = vn

    out_shape = [jax.ShapeDtypeStruct(w.shape, F32) for _ in range(4) for w in ws]
    res = pl.pallas_call(body, name=name, out_shape=out_shape, compiler_params=_cp())(*ws, *ms, *vs, *gs)
    return [res[i * k:(i + 1) * k] for i in range(4)]


SWAP_SCRATCH = (pltpu.SemaphoreType.DMA((4,)), pltpu.SemaphoreType.DMA((4,)))


def _swap_phases(g_ref, out_ref, send_sems, recv_sems):
    x_, y_, c_ = _place()

    def copies():
        return [pltpu.make_async_remote_copy(src_ref=g_ref.at[k, 1 - c_], dst_ref=out_ref.at[k],
                                             send_sem=send_sems.at[k], recv_sem=recv_sems.at[k],
                                             device_id=(x_, y_, 1 - c_), device_id_type=MESH) for k in range(4)]

    def start():
        for cp in copies():
            cp.start()

    def finish():
        for cp in copies():
            cp.wait()

    return start, finish


def _swap_sibling(g, *, name):
    def body(g_ref, out_ref, send_sems, recv_sems):
        for phase in _swap_phases(g_ref, out_ref, send_sems, recv_sems):
            phase()

    return pl.pallas_call(
        body, name=name, out_shape=jax.ShapeDtypeStruct((4,) + g.shape[2:], g.dtype), in_specs=[ANY], out_specs=ANY,
        scratch_shapes=list(SWAP_SCRATCH))(g)


def _row_tile(R):
    for cand in (256, 400, 200):
        if R % cand == 0:
            return cand
    return R


def _add_own(g, b, *, name, out_dtype):
    n, _, R, C = g.shape
    tr = _row_tile(R)

    def body(c_ref, g_ref, b_ref, o_ref):
        del c_ref
        o_ref[...] = (g_ref[...] + b_ref[...]).astype(out_dtype)

    blk = pl.BlockSpec((None, tr, C), lambda s, i, c: (s, i, 0))
    grid_spec = pltpu.PrefetchScalarGridSpec(
        num_scalar_prefetch=1, grid=(n, R // tr),
        in_specs=[pl.BlockSpec((None, None, tr, C), lambda s, i, c: (s, c[0], i, 0)), blk], out_specs=blk)
    core = jnp.reshape(lax.axis_index("c"), (1,)).astype(jnp.int32)
    return pl.pallas_call(body, name=name, grid_spec=grid_spec, out_shape=jax.ShapeDtypeStruct(b.shape, out_dtype),
                          compiler_params=_cp(("parallel", "parallel")))(core, g, b)


def _pack_local(parts, group, tail=None):
    table, rows = group
    segs = []
    for n, r, rp, tr in table:
        w = parts[n].T if tr else parts[n]
        segs.append(jnp.pad(w.reshape(r, PACK_COLS), ((0, rp - r), (0, 0))))
    spare = rows - sum(rp for _, _, rp, _ in table)
    segs.append(jnp.zeros((spare, PACK_COLS), segs[0].dtype) if tail is None else tail)
    return jnp.concatenate(segs, axis=0)


CONV_W_BITS = 2 * 3 * 704
SPARE_EARLY = 16


def _conv_w_as_rows(conv_w_shard):
    bits = lax.bitcast_convert_type(conv_w_shard.reshape(-1), BF16).reshape(-1)
    return jnp.pad(bits, (0, SPARE_EARLY * PACK_COLS - CONV_W_BITS)).reshape(SPARE_EARLY, PACK_COLS)


def _conv_w_from_rows(gathered):
    bits = gathered[:, EARLY[1] - SPARE_EARLY:].reshape(N_DEV, -1)[:, :CONV_W_BITS].reshape(N_DEV, 3 * 704, 2)
    w = lax.bitcast_convert_type(bits, F32).reshape(N_DEV, 3, 704)
    return w.transpose(1, 0, 2).reshape(3, 2 * D_FF)


def _unpack_local(packed, like, group):
    out, off = {}, 0
    for n, r, rp, tr in group[0]:
        rows, cols = like[n].shape
        seg = packed[off:off + r]
        out[n] = (seg.reshape(cols, rows).T if tr else seg)[None]
        off += rp
    return out


def _segments(g, group):
    out, off = {}, 0
    for n, r, rp, _ in group[0]:
        out[n] = g[:, off:off + r]
        off += rp
    return out


def _pack_grads(parts, group):
    table, rows = group
    segs = [jnp.pad(parts[n], ((0, 0), (0, rp - parts[n].shape[1]), (0, 0))) for n, _, rp, _ in table]
    segs.append(jnp.zeros((N_DEV, rows - sum(rp for _, _, rp, _ in table), PACK_COLS), F32))
    return jnp.concatenate(segs, axis=1)


def _owner_rows_early(g):
    g_in = jnp.concatenate([g["w_in_t"][:2432], g["w_in_t"][2496:2528]], axis=0).reshape(N_DEV, 308, PACK_COLS)
    g_uq = g["w_uq_t"].reshape(N_DEV, 128, MLA_Q_RANK)[:, :96].reshape(N_DEV, 24, PACK_COLS)
    g_ukv = jnp.concatenate([g["w_k_t"].reshape(N_DEV, 128, MLA_KV_RANK)[:, :64],
                             g["w_v_t"].reshape(N_DEV, 64, MLA_KV_RANK)], axis=1).reshape(N_DEV, 16, PACK_COLS)
    return dict(w_in=g_in, w_uq=g_uq, w_ukv=g_ukv)


def _owner_rows_late(g):
    g_up = g["w_up_t"].reshape(2, 2, 2, 704, PACK_COLS).swapaxes(0, 1).reshape(N_DEV, 704, PACK_COLS)
    return dict(w_out=g["w_out"].reshape(N_DEV, 128, PACK_COLS), w_up=g_up,
                w_down=g["w_down"].reshape(N_DEV, 352, PACK_COLS))


def _reduce_to_pairs(gp, *, name):
    gp = gp.reshape(4, 2, gp.shape[1], PACK_COLS)
    return _add_own(gp, _swap_sibling(gp, name=name + "_swap"), out_dtype=BF16, name=name + "_sum")


def _interleave_ff(w):
    g, v = w[..., :D_FF], w[..., D_FF:]
    return jnp.concatenate([g[..., :FF_HALF], v[..., :FF_HALF], g[..., FF_HALF:], v[..., FF_HALF:]], axis=-1)


def _rope_tables(pos):
    p = pos.astype(F32)[:, None]
    inv_r = ROPE_BASE ** (-jnp.arange(0, RET_HEAD_DIM, 2, dtype=F32) / RET_HEAD_DIM)
    ang = p * jnp.tile(inv_r, 4)
    sign_r = jnp.tile(jnp.concatenate([-jnp.ones((32,), F32), jnp.ones((32,), F32)]), 2)
    cos_r, ss_r = jnp.cos(ang), jnp.sin(ang) * sign_r
    inv_m = ROPE_BASE ** (-jnp.arange(0, MLA_ROPE, 2, dtype=F32) / MLA_ROPE)
    ang = p * jnp.concatenate([jnp.zeros((64,), F32), inv_m, inv_m, jnp.zeros((32,), F32)])
    sign_m = jnp.concatenate([jnp.zeros((64,), F32), -jnp.ones((16,), F32), jnp.ones((16,), F32), jnp.zeros((32,), F32)])
    cos_m, ss_m = jnp.cos(ang), jnp.sin(ang) * sign_m
    return cos_r, ss_r, cos_m, ss_m


def _prep_early(gathered):
    seg = _segments(gathered, EARLY)
    w_in_t = seg["w_in"].reshape(IN_WIDTH, D_MODEL)
    z = lambda n: jnp.zeros((n, D_MODEL), BF16)
    w_in_t = jnp.concatenate([w_in_t[:2432], z(64), w_in_t[2432:2464], z(32)], axis=0)
    w_uq_t = jnp.pad(seg["w_uq"].reshape(MLA_HEADS, 96, MLA_Q_RANK), ((0, 0), (0, 32), (0, 0))).reshape(1024, MLA_Q_RANK)
    ukv = seg["w_ukv"].reshape(MLA_HEADS, 128, MLA_KV_RANK)
    w_k_t = jnp.pad(ukv[:, :64], ((0, 0), (0, 64), (0, 0))).reshape(1024, MLA_KV_RANK)
    w_v_t = ukv[:, 64:].reshape(512, MLA_KV_RANK)
    return dict(w_in_t=w_in_t, w_uq_t=w_uq_t, w_k_t=w_k_t, w_v_t=w_v_t)


def _prep_late(gathered):
    seg = _segments(gathered, LATE)
    w_up_t = seg["w_up"].reshape(2, 2, 2, 704, D_MODEL).swapaxes(0, 1).reshape(2 * D_FF, D_MODEL)
    return dict(w_out=seg["w_out"].reshape(1024, D_MODEL), w_up_t=w_up_t, w_down=seg["w_down"].reshape(D_FF, D_MODEL))


def _local_step(x, pos, tgt, early, sm, late):
    dist = not isinstance(late, dict)
    cos_r, ss_r, cos_m, ss_m = _rope_tables(pos)
    tabs = _ret_tables()

    if dist:
        h, gathered = _rmsnorm_fwd(x, sm["attn_norm_w"], gather=early, name="attn_norm")
        W = _prep_early(gathered)
        sm = {**sm, "conv_w": _interleave_ff(_conv_w_from_rows(gathered))}
    else:
        h = _rmsnorm_fwd(x, sm["attn_norm_w"], name="attn_norm")
        W = early
    proj = _mm(h, W["w_in_t"], bt=True, name="in_proj")
    y_ret, o_ret = _ret_fwd(proj, cos_r, ss_r, tabs, sm["ret_gn_w"], name="ret_fwd")
    q, k, v1, cqn, ckvn = _mla_prep_fwd(proj, sm["mla_q_norm_w"], sm["mla_kv_norm_w"], W["w_uq_t"], W["w_k_t"],
                                       W["w_v_t"], cos_m, ss_m, name="mla_prep")
    T = x.shape[0]
    tq = min(T, 512)
    if dist:
        y_mla, lse, gathered = _flash_fwd(q, k, v1, gather=late, name="mla_attn")
        W = {**W, **_prep_late(gathered)}
    else:
        y_mla, lse = _flash_fwd(q, k, v1, name="mla_attn")
        W = {**W, **late}
    mixed = (y_ret, y_mla)
    x1 = _mm(mixed, W["w_out"], add=x, name="out_proj")
    h2, u, a = _up_proj_conv(x1, sm["ffn_norm_w"], W["w_up_t"], sm["conv_w"], sm["conv_b"], name="ffn_norm_up_conv")
    loss, dx2, dx2b, d_final = _down_proj_loss(a, W["w_down"], x1, tgt, sm["final_norm_w"], name="down_proj_loss")

    g = {}
    g["w_down"] = _mm_tn(a, dx2b, name="dw_down")
    da = _mm(dx2b, W["w_down"], bt=True, name="d_act")
    du, dcw0, dcw1, dcw2, dcb = _conv_bwd(u, da, sm["conv_w"], sm["conv_b"], name="conv_bwd")
    g["w_up_t"] = _mm_tn(du, h2, name="dw_up")
    dx1, d_ffn = _mm_norm_bwd(du, W["w_up_t"], x1, sm["ffn_norm_w"], dx2, name="d_h2_ffn_norm_bwd")

    g["w_out"] = _mm_tn(mixed, dx1, name="dw_out")
    dmixed = _mm(dx1, W["w_out"], bt=True, name="d_mixed")
    do_ret, dg, do_mla, delta, d_gn = _mix_bwd(dmixed, o_ret, proj, y_mla, sm["ret_gn_w"], name="mix_bwd")
    drq = _ret_bwd_dq(proj, do_ret, cos_r, ss_r, tabs, name="ret_bwd_dq")
    delta_r = delta.reshape(MLA_HEADS, T // tq, 1, tq)
    if dist:
        gl = _pack_grads(_owner_rows_late(g), LATE).reshape(4, 2, LATE[1], PACK_COLS)
        drk, drv, theirs = _ret_bwd_dkv(proj, do_ret, cos_r, ss_r, tabs, swap=gl, name="ret_bwd_dkv")
        pair = _add_own(gl, theirs, out_dtype=BF16, name="grad_late_sum")
        dqt, dk, dv, slots_late = _flash_bwd(q, k, v1, do_mla, lse, delta_r, exchange=pair, name="mla_attn_bwd")
    else:
        drk, drv = _ret_bwd_dkv(proj, do_ret, cos_r, ss_r, tabs, name="ret_bwd_dkv")
        dqt, dk, dv = _flash_bwd(q, k, v1, do_mla, lse, delta_r, name="mla_attn_bwd")
        slots_late = None
    dq = dqt.transpose(1, 3, 0, 2).reshape(T, MLA_HEADS * 128)
    dproj, g["w_in_t"], g["w_uq_t"], g["w_k_t"], g["w_v_t"], d_qn, d_kvn = _mla_prep_bwd(
        dq, dk, dv, proj, sm["mla_q_norm_w"], sm["mla_kv_norm_w"], W["w_uq_t"], W["w_k_t"], W["w_v_t"], cos_m, ss_m,
        (drq, drk, drv, dg), cqn, ckvn, h, name="mla_prep_bwd")
    if dist:
        pair = _reduce_to_pairs(_pack_grads(_owner_rows_early(g), EARLY), name="grad_early")
        grad_x, d_attn, slots_early = _mm_norm_bwd(dproj, W["w_in_t"], x, sm["attn_norm_w"], dx1, exchange=pair,
                                                   name="d_h_attn_norm_bwd")
    else:
        grad_x, d_attn = _mm_norm_bwd(dproj, W["w_in_t"], x, sm["attn_norm_w"], dx1, name="d_h_attn_norm_bwd")
        slots_early = None

    small = dict(attn_norm_w=d_attn, ret_gn_w=d_gn, mla_q_norm_w=d_qn, mla_kv_norm_w=d_kvn, ffn_norm_w=d_ffn,
                 conv_b=dcb, final_norm_w=d_final, conv_w0=dcw0, conv_w1=dcw1, conv_w2=dcw2, loss=loss)
    return loss, grad_x, g, small, slots_early, slots_late


def kernel(x, positions, attn_norm_w, w_in, ret_gn_w, mla_q_norm_w, w_uq, mla_kv_norm_w, w_ukv, w_out, ffn_norm_w, w_up, conv_w, conv_b, w_down, final_norm_w, loss_target, m_attn_norm_w, m_w_in, m_ret_gn_w, m_mla_q_norm_w, m_w_uq, m_mla_kv_norm_w, m_w_ukv, m_w_out, m_ffn_norm_w, m_w_up, m_conv_w, m_conv_b, m_w_down, m_final_norm_w, v_attn_norm_w, v_w_in, v_ret_gn_w, v_mla_q_norm_w, v_w_uq, v_mla_kv_norm_w, v_w_ukv, v_w_out, v_ffn_norm_w, v_w_up, v_conv_w, v_conv_b, v_w_down, v_final_norm_w):
    a = dict(locals())
    x_, y_, c_ = _place()
    dev = 4 * x_ + 2 * y_ + c_

    shard = {n: a[n][0] for n in BIG_NAMES}
    shard16 = {n: w.astype(BF16) for n, w in shard.items()}
    sm = dict(attn_norm_w=attn_norm_w, ret_gn_w=ret_gn_w, mla_q_norm_w=mla_q_norm_w, mla_kv_norm_w=mla_kv_norm_w,
              ffn_norm_w=ffn_norm_w, final_norm_w=final_norm_w.reshape(1, D_MODEL), conv_b=_interleave_ff(conv_b))

    loss, grad_x, _, gs, slots_early, slots_late = _local_step(
        x[0], positions[0], loss_target[0], _pack_local(shard16, EARLY, tail=_conv_w_as_rows(conv_w[0])), sm,
        _pack_local(shard16, LATE))

    big = [{}, {}, {}, {}]
    for group, slots, tag, calls in ((EARLY, slots_early, "early", (("w_in", "w_uq", "w_ukv"),)),
                                     (LATE, slots_late, "late", (("w_out", "w_down"), ("w_up",)))):
        grads = _unpack_local(_sum_chips(slots, name="grad_sum_" + tag), shard, group)
        for names_c in calls:
            res = _adamw_multi([shard[n] for n in names_c], [a["m_" + n][0] for n in names_c],
                               [a["v_" + n][0] for n in names_c], [grads[n][0] for n in names_c],
                               name="adamw_" + "_".join(names_c))
            for kind in range(4):
                for n, r in zip(names_c, res[kind]):
                    big[kind][n] = r[None]

    packed = _pack_small(gs, name="pack_small_grads")
    tot = _sum_small(_all_gather(packed, name="gather_small_grads", in_vmem=True), name="sum_small_grads")
    loss_out = tot["loss"][0, 0]
    g_cw = lax.dynamic_slice_in_dim(tot["conv_w"], dev * 704, 704, axis=1)

    def rows_of(prefix):
        return [a[prefix + n].reshape(1, size) for n, size in SMALL]

    sml = _adamw_multi(rows_of("") + [conv_w[0]], rows_of("m_") + [m_conv_w[0]], rows_of("v_") + [v_conv_w[0]],
                       [tot[n] for n, _ in SMALL] + [g_cw], name="adamw_small")
    cwo = [kind[-1] for kind in sml]

    def small_of(kind, n):
        return sml[kind][[nm for nm, _ in SMALL].index(n)].reshape(a[n].shape)

    names = ['attn_norm_w', 'w_in', 'ret_gn_w', 'mla_q_norm_w', 'w_uq', 'mla_kv_norm_w', 'w_ukv', 'w_out',
             'ffn_norm_w', 'w_up', 'conv_w', 'conv_b', 'w_down', 'final_norm_w']
    outs = [loss_out, grad_x[None]]
    for kind in range(4):
        for n in names:
            if n == "conv_w":
                outs.append(cwo[kind][None])
            elif n in big[kind]:
                outs.append(big[kind][n])
            else:
                outs.append(small_of(kind, n))
    return tuple(outs)
```

```python
import functools

import numpy as np
import jax
import jax.numpy as jnp
from jax import lax
from jax.experimental import pallas as pl
from jax.experimental.pallas import tpu as pltpu

F32 = jnp.float32
BF16 = jnp.bfloat16
MESH = pl.DeviceIdType.MESH
ANY = pl.BlockSpec(memory_space=pl.ANY)

D_MODEL = 1024
RET_HEADS = 8
RET_HEAD_DIM = 64
RET_WIDTH = 512
RET_CHUNK = 128
MLA_HEADS = 8
MLA_NOPE = 64
MLA_ROPE = 32
MLA_V = 64
MLA_Q_RANK = 256
MLA_KV_RANK = 128
MLA_WIDTH = 512
IN_WIDTH = 2464
IN_PAD = 2560
D_FF = 2816
FF_HALF = 1408
ROPE_BASE = 10000.0
EPS = 1e-6
SCALE = float((MLA_NOPE + MLA_ROPE) ** -0.5)
K_SCALE = 0.125
N_DEV = 8

ADAM_LR = 0.001
ADAM_B1 = 0.9
ADAM_B2 = 0.999
ADAM_EPS = 1e-08
ADAM_WD = 0.01
ADAM_STEP = 10

VMEM_LIMIT = 56 * 1024 * 1024
MM_BUDGET = 40 * 1024 * 1024
NEG = -1e30
FLASH_UNROLL = 4
FLASH_BWD_UNROLL = 3

PACK_COLS = 1024
EARLY = ((("w_in", 308, 320, True), ("w_uq", 24, 32, True), ("w_ukv", 16, 16, True)), 384)
LATE = ((("w_out", 128, 128, False), ("w_up", 704, 704, True), ("w_down", 352, 352, False)), 1200)
BIG_NAMES = ("w_in", "w_uq", "w_ukv", "w_out", "w_up", "w_down")
SMALL = (("attn_norm_w", 1024), ("ret_gn_w", 512), ("mla_q_norm_w", 256), ("mla_kv_norm_w", 128),
         ("ffn_norm_w", 1024), ("conv_b", 5632), ("final_norm_w", 1024))
SMALL_VECTORS = SMALL + (("conv_w0", 5632), ("conv_w1", 5632), ("conv_w2", 5632), ("loss", 128))
SMALL_ROWS = 32


def _cp(sem=None, vmem=VMEM_LIMIT):
    return pltpu.CompilerParams(dimension_semantics=sem, vmem_limit_bytes=vmem)


def _dot(a, b):
    return jnp.dot(a, b, preferred_element_type=F32)


def _dot_nt(a, b):
    return lax.dot_general(a, b, (((1,), (1,)), ((), ())), preferred_element_type=F32)


def _dot_tn(a, b):
    return lax.dot_general(a, b, (((0,), (0,)), ((), ())), preferred_element_type=F32)


def _sigmoid(x):
    return 0.5 * jnp.tanh(0.5 * x) + 0.5


def _partner(x, half, period):
    n = x.shape[-1]
    lane = lax.broadcasted_iota(jnp.int32, x.shape, 1)
    return jnp.where((lane % period) < half, pltpu.roll(x, n - half, 1), pltpu.roll(x, half, 1))


def _rope(x, cos, ss, half, period):
    return x * cos + _partner(x, half, period) * ss


def _rope_t(dy, cos, ss, half, period):
    return dy * cos - _partner(dy, half, period) * ss


def _head_masks(shape):
    lane = lax.broadcasted_iota(jnp.int32, shape, 1)
    m0 = (lane < 64).astype(F32)
    return m0, 1.0 - m0


def _mm(a, b, *, name, add=None, out_dtype=F32, bt=False):
    parts = a if isinstance(a, tuple) else (a,)
    M = parts[0].shape[0]
    K = sum(p.shape[1] for p in parts)
    N = b.shape[0] if bt else b.shape[1]
    osz = jnp.dtype(out_dtype).itemsize
    per_row = 2 * (K * parts[0].dtype.itemsize + N * osz + (N * 4 if add is not None else 0))
    tm = 128
    for cand in (512, 256):
        if M % cand == 0 and cand * per_row + 4 * K * N <= MM_BUDGET:
            tm = cand
            break
    tm = min(tm, M)
    mul = _dot_nt if bt else _dot
    n_a = len(parts)
    n_in = n_a + (1 if add is None else 2)

    def body(*refs):
        av = refs[0][...] if n_a == 1 else jnp.concatenate([r[...] for r in refs[:n_a]], axis=1)
        acc = mul(av.astype(BF16), refs[n_a][...])
        if add is not None:
            acc = refs[n_a + 1][...] + acc
        refs[n_in][...] = acc.astype(out_dtype)

    in_specs = [pl.BlockSpec((tm, p.shape[1]), lambda i: (i, 0)) for p in parts]
    in_specs.append(pl.BlockSpec(b.shape, lambda i: (0, 0)))
    args = [*parts, b]
    if add is not None:
        in_specs.append(pl.BlockSpec((tm, N), lambda i: (i, 0)))
        args.append(add)
    return pl.pallas_call(
        body, name=name, grid=(M // tm,), in_specs=in_specs, out_specs=pl.BlockSpec((tm, N), lambda i: (i, 0)),
        out_shape=jax.ShapeDtypeStruct((M, N), out_dtype), compiler_params=_cp(("parallel",)))(*args)


def _mm_tn(a, b, *, name):
    parts = a if isinstance(a, tuple) else (a,)
    T = parts[0].shape[0]
    M = sum(p.shape[1] for p in parts)
    N = b.shape[1]
    tk = min(T, 512)

    def tile(n):
        for cand in (1408, 1280):
            if n > 1408 and n % cand == 0:
                return cand
        return n

    tm, tn = tile(M), tile(N)
    nk = T // tk
    n_a = len(parts)
    assert n_a == 1 or tm == M

    def body(*refs):
        o_ref = refs[n_a + 1]

        @pl.when(pl.program_id(2) == 0)
        def _():
            o_ref[...] = jnp.zeros_like(o_ref)
        av = refs[0][...] if n_a == 1 else jnp.concatenate([r[...] for r in refs[:n_a]], axis=1)
        o_ref[...] += _dot_tn(av.astype(BF16), refs[n_a][...].astype(BF16))

    if n_a == 1:
        a_specs = [pl.BlockSpec((tk, tm), lambda i, j, k: (k, i))]
    else:
        a_specs = [pl.BlockSpec((tk, p.shape[1]), lambda i, j, k: (k, 0)) for p in parts]
    return pl.pallas_call(
        body, name=name, grid=(M // tm, N // tn, nk),
        in_specs=a_specs + [pl.BlockSpec((tk, tn), lambda i, j, k: (k, j))],
        out_specs=pl.BlockSpec((tm, tn), lambda i, j, k: (i, j)),
        out_shape=jax.ShapeDtypeStruct((M, N), F32),
        compiler_params=_cp(("parallel", "parallel", "arbitrary")))(*parts, b)


def _rmsnorm_fwd(x, w, *, name, gather=None):
    T, D = x.shape
    tm = min(T, 1024)
    n = T // tm

    def body(x_ref, w_ref, *rest):
        if gather is not None:
            s_ref, o_ref, g_ref, *sems = rest
            start, forward, finish = _gather_phases(s_ref, g_ref, *sems)
            pl.when(pl.program_id(0) == 0)(start)
            pl.when(pl.program_id(0) == n // 2)(forward)
        else:
            o_ref, = rest
        xv = x_ref[...]
        r = lax.rsqrt(jnp.mean(xv * xv, axis=-1, keepdims=True) + EPS)
        o_ref[...] = (xv * r * w_ref[...]).astype(BF16)
        if gather is not None:
            pl.when(pl.program_id(0) == n - 1)(finish)

    in_specs = [pl.BlockSpec((tm, D), lambda i: (i, 0)), pl.BlockSpec((1, D), lambda i: (0, 0))]
    out_spec = pl.BlockSpec((tm, D), lambda i: (i, 0))
    out_shape = jax.ShapeDtypeStruct((T, D), BF16)
    if gather is None:
        return pl.pallas_call(body, name=name, grid=(n,), in_specs=in_specs, out_specs=out_spec, out_shape=out_shape,
                              compiler_params=_cp(("parallel",)))(x, w)
    return pl.pallas_call(
        body, name=name, grid=(n,), in_specs=in_specs + [ANY], out_specs=[out_spec, ANY],
        out_shape=[out_shape, jax.ShapeDtypeStruct((N_DEV,) + gather.shape, gather.dtype)],
        scratch_shapes=list(GATHER_SCRATCH), compiler_params=_cp(("arbitrary",)))(x, w, gather)


def _mm_norm_bwd(a, b, x, w, dres, *, name, exchange=None):
    T, K = a.shape
    D = b.shape[1]
    tm = min(T, 256 if K > 4096 else 512)
    n = T // tm

    def body(a_ref, b_ref, x_ref, w_ref, dr_ref, *rest):
        if exchange is None:
            dx_ref, dw_ref = rest
        else:
            p_ref, dx_ref, dw_ref, got_ref, *sems = rest
            start, finish = _exchange_phases(p_ref, got_ref, *sems)
            pl.when(pl.program_id(0) == 0)(start)

        @pl.when(pl.program_id(0) == 0)
        def _():
            dw_ref[...] = jnp.zeros_like(dw_ref)
        dh = _dot(a_ref[...], b_ref[...])
        xv = x_ref[...]
        r = lax.rsqrt(jnp.mean(xv * xv, axis=-1, keepdims=True) + EPS)
        xh = xv * r
        g = dh * w_ref[...]
        dx_ref[...] = dr_ref[...] + r * (g - xh * jnp.mean(g * xh, axis=-1, keepdims=True))
        dw_ref[...] += jnp.sum(dh * xh, axis=0, keepdims=True)
        if exchange is not None:
            pl.when(pl.program_id(0) == n - 1)(finish)

    row = pl.BlockSpec((tm, D), lambda i: (i, 0))
    vec = pl.BlockSpec((1, D), lambda i: (0, 0))
    in_specs = [pl.BlockSpec((tm, K), lambda i: (i, 0)), pl.BlockSpec((K, D), lambda i: (0, 0)), row, vec, row]
    out_shape = [jax.ShapeDtypeStruct((T, D), F32), jax.ShapeDtypeStruct((1, D), F32)]
    if exchange is None:
        return pl.pallas_call(body, name=name, grid=(n,), in_specs=in_specs, out_specs=[row, vec], out_shape=out_shape,
                              compiler_params=_cp(("arbitrary",)))(a, b, x, w, dres)
    return pl.pallas_call(
        body, name=name, grid=(n,), in_specs=in_specs + [ANY], out_specs=[row, vec, ANY],
        out_shape=out_shape + [jax.ShapeDtypeStruct(exchange.shape, exchange.dtype)],
        scratch_shapes=list(EXCHANGE_SCRATCH), compiler_params=_cp(("arbitrary",)))(a, b, x, w, dres, exchange)


def _down_proj_loss(a, w_down, x1, tgt, w, *, name):
    T, D = x1.shape
    K = a.shape[1]
    tm = min(T, 512)

    def body(a_ref, b_ref, x_ref, t_ref, w_ref, loss_ref, dx_ref, dxb_ref, dw_ref):
        @pl.when(pl.program_id(0) == 0)
        def _():
            dw_ref[...] = jnp.zeros_like(dw_ref)
            loss_ref[...] = jnp.zeros_like(loss_ref)
        xv = x_ref[...] + _dot(a_ref[...], b_ref[...])
        wv = w_ref[...]
        r = lax.rsqrt(jnp.mean(xv * xv, axis=-1, keepdims=True) + EPS)
        xh = xv * r
        e = xh * wv - t_ref[...]
        part = 0.5 * jnp.sum(jnp.mean(e * e, axis=-1, keepdims=True), axis=0, keepdims=True)
        loss_ref[...] += jnp.broadcast_to(part, loss_ref.shape)
        dy = e * (1.0 / D)
        g = dy * wv
        dx = r * (g - xh * jnp.mean(g * xh, axis=-1, keepdims=True))
        dx_ref[...] = dx
        dxb_ref[...] = dx.astype(BF16)
        dw_ref[...] += jnp.sum(dy * xh, axis=0, keepdims=True)

    row = pl.BlockSpec((tm, D), lambda i: (i, 0))
    vec = pl.BlockSpec((1, D), lambda i: (0, 0))
    return pl.pallas_call(
        body, name=name, grid=(T // tm,),
        in_specs=[pl.BlockSpec((tm, K), lambda i: (i, 0)), pl.BlockSpec((K, D), lambda i: (0, 0)), row, row, vec],
        out_specs=[pl.BlockSpec((1, 128), lambda i: (0, 0)), row, row, vec],
        out_shape=[jax.ShapeDtypeStruct((1, 128), F32), jax.ShapeDtypeStruct((T, D), F32),
                   jax.ShapeDtypeStruct((T, D), BF16), jax.ShapeDtypeStruct((1, D), F32)],
        compiler_params=_cp(("arbitrary",)))(a, w_down, x1, tgt, w)


def _ret_tables():
    C = RET_CHUNK
    h = jnp.arange(RET_HEADS, dtype=F32)
    log_gamma = jnp.log1p(-jnp.power(2.0, -5.0 - h))
    idx = jnp.arange(C, dtype=F32)
    diff = idx[:, None] - idx[None, :]
    dm = jnp.where(diff >= 0, jnp.exp(log_gamma[:, None, None] * jnp.maximum(diff, 0.0)), 0.0)
    dm = dm.reshape(4, 2 * C, C)
    lane_head = jnp.repeat(jnp.arange(RET_HEADS).reshape(4, 2), 64, axis=1)
    lg = log_gamma[lane_head]
    xi = jnp.exp(lg[:, None, :] * (idx[None, :, None] + 1.0))
    zeta = jnp.exp(lg[:, None, :] * (C - 1.0 - idx[None, :, None]))
    blk = (jnp.arange(128)[:, None] // 64) == (jnp.arange(128)[None, :] // 64)
    cd = jnp.where(blk[None], jnp.exp(lg * C)[:, :, None], 0.0)
    return dm.astype(F32), xi.astype(F32), zeta.astype(F32), cd.astype(F32)


def _ret_specs(tb, rev, nt, roped=False):
    def tmap(t):
        return (nt - 1 - t) if rev else t
    offsets = (0, 0, 8) if roped else (0, 4, 8)
    qkv = [pl.BlockSpec((tb, 128), lambda p, t, o=o: (tmap(t), o + p)) for o in offsets]
    rope = [pl.BlockSpec((tb, 128), lambda p, t: (tmap(t), 0))] * 2
    tabs = [pl.BlockSpec((None, 256, 128), lambda p, t: (p, 0, 0))] + \
           [pl.BlockSpec((None, 128, 128), lambda p, t: (p, 0, 0))] * 3
    return qkv, rope, tabs


def _ret_fwd(proj, cos, ss, tabs, gnw, *, name):
    T = proj.shape[0]
    tb = min(T, 1024)
    nt = T // tb
    nchunk = tb // RET_CHUNK

    def body(q_ref, k_ref, v_ref, g_ref, cos_ref, ss_ref, dm_ref, xi_ref, zt_ref, cd_ref, gnw_ref,
             y_ref, o_ref, qr_ref, kr_ref, r_sc):
        @pl.when(pl.program_id(1) == 0)
        def _():
            r_sc[...] = jnp.zeros_like(r_sc)
        m0, m1 = _head_masks((128, 128))
        dm, xi, zt, cd = dm_ref[...], xi_ref[...], zt_ref[...], cd_ref[...]
        bm = (cd > 0).astype(F32)
        gnw = gnw_ref[...]
        for c in range(nchunk):
            rs = pl.ds(c * RET_CHUNK, RET_CHUNK)
            cs, sn = cos_ref[rs, :], ss_ref[rs, :]
            q = _rope(q_ref[rs, :], cs, sn, 32, 64)
            k = _rope(k_ref[rs, :], cs, sn, 32, 64) * K_SCALE
            v = v_ref[rs, :]
            kb, vb = k.astype(BF16), v.astype(BF16)
            qr_ref[rs, :] = q.astype(BF16)
            kr_ref[rs, :] = kb
            qs = jnp.concatenate([q * m0, q * m1], axis=0).astype(BF16)
            s = (_dot_nt(qs, kb) * dm).astype(BF16)
            vs = jnp.concatenate([v * m0, v * m1], axis=0).astype(BF16)
            o = _dot(jnp.concatenate([s[:128], s[128:]], axis=1), vs)
            r = r_sc[...]
            o = o + _dot(q.astype(BF16), r.astype(BF16)) * xi
            r_sc[...] = cd * r + bm * _dot_tn((k * zt).astype(BF16), vb)
            mu = (jnp.sum(o * m0, axis=1, keepdims=True) * m0 + jnp.sum(o * m1, axis=1, keepdims=True) * m1) * (1.0 / 64)
            d = o - mu
            dd = d * d
            var = (jnp.sum(dd * m0, axis=1, keepdims=True) * m0 + jnp.sum(dd * m1, axis=1, keepdims=True) * m1) * (1.0 / 64)
            oh = d * lax.rsqrt(var + EPS)
            g = g_ref[rs, :]
            y_ref[rs, :] = (g * _sigmoid(g) * (oh * gnw)).astype(BF16)
            o_ref[rs, :] = o

    qkv, rope, tspec = _ret_specs(tb, False, nt)
    gspec = pl.BlockSpec((tb, 128), lambda p, t: (t, 12 + p))
    out = pl.BlockSpec((tb, 128), lambda p, t: (t, p))
    return pl.pallas_call(
        body, name=name, grid=(4, nt),
        in_specs=qkv + [gspec] + rope + tspec + [pl.BlockSpec((1, 128), lambda p, t: (0, p))],
        out_specs=[out, out, out, out],
        out_shape=[jax.ShapeDtypeStruct((T, RET_WIDTH), BF16), jax.ShapeDtypeStruct((T, RET_WIDTH), F32),
                   jax.ShapeDtypeStruct((T, RET_WIDTH), BF16), jax.ShapeDtypeStruct((T, RET_WIDTH), BF16)],
        scratch_shapes=[pltpu.VMEM((128, 128), F32)],
        compiler_params=_cp(("parallel", "arbitrary")))(proj, proj, proj, proj, cos, ss, *tabs, gnw)


def _ret_bwd_dq(qr, kr, proj, do, cos, ss, tabs, *, name):
    T = proj.shape[0]
    tb = min(T, 1024)
    nt = T // tb
    nchunk = tb // RET_CHUNK

    def body(q_ref, k_ref, v_ref, do_ref, cos_ref, ss_ref, dm_ref, xi_ref, zt_ref, cd_ref, dq_ref, r_sc):
        del q_ref
        @pl.when(pl.program_id(1) == 0)
        def _():
            r_sc[...] = jnp.zeros_like(r_sc)
        m0, m1 = _head_masks((128, 128))
        dm, xi, zt, cd = dm_ref[...], xi_ref[...], zt_ref[...], cd_ref[...]
        bm = (cd > 0).astype(F32)
        for c in range(nchunk):
            rs = pl.ds(c * RET_CHUNK, RET_CHUNK)
            cs, sn = cos_ref[rs, :], ss_ref[rs, :]
            k = k_ref[rs, :].astype(F32)
            vb = v_ref[rs, :].astype(BF16)
            dob = do_ref[rs, :]
            dof = dob.astype(F32)
            dos = jnp.concatenate([dof * m0, dof * m1], axis=0).astype(BF16)
            a = (_dot_nt(dos, vb) * dm).astype(BF16)
            ks = jnp.concatenate([k * m0, k * m1], axis=0).astype(BF16)
            r = r_sc[...]
            dq = _dot(jnp.concatenate([a[:128], a[128:]], axis=1), ks) + _dot_nt(dob, r.astype(BF16)) * xi
            r_sc[...] = cd * r + bm * _dot_tn((k * zt).astype(BF16), vb)
            dq_ref[rs, :] = _rope_t(dq, cs, sn, 32, 64).astype(BF16)

    qkv, rope, tspec = _ret_specs(tb, False, nt, roped=True)
    blk = pl.BlockSpec((tb, 128), lambda p, t: (t, p))
    return pl.pallas_call(
        body, name=name, grid=(4, nt), in_specs=qkv + [blk] + rope + tspec, out_specs=blk,
        out_shape=jax.ShapeDtypeStruct((T, RET_WIDTH), BF16),
        scratch_shapes=[pltpu.VMEM((128, 128), F32)],
        compiler_params=_cp(("parallel", "arbitrary")))(qr, kr, proj, do, cos, ss, *tabs)


def _ret_bwd_dkv(qr, kr, proj, do, cos, ss, tabs, *, name, swap=None):
    T = proj.shape[0]
    tb = min(T, 1024)
    nt = T // tb
    nchunk = tb // RET_CHUNK

    def body(q_ref, k_ref, v_ref, do_ref, cos_ref, ss_ref, dm_ref, xi_ref, zt_ref, cd_ref, *rest):
        if swap is None:
            backward(q_ref, k_ref, v_ref, do_ref, cos_ref, ss_ref, dm_ref, xi_ref, zt_ref, cd_ref, *rest)
        else:
            g_ref, dk_ref, dv_ref, got_ref, u_sc, *sems = rest
            start, finish = _swap_phases(g_ref, got_ref, *sems)
            pl.when((pl.program_id(0) == 0) & (pl.program_id(1) == 0))(start)
            backward(q_ref, k_ref, v_ref, do_ref, cos_ref, ss_ref, dm_ref, xi_ref, zt_ref, cd_ref, dk_ref, dv_ref, u_sc)
            pl.when((pl.program_id(0) == 3) & (pl.program_id(1) == nt - 1))(finish)

    def backward(q_ref, k_ref, v_ref, do_ref, cos_ref, ss_ref, dm_ref, xi_ref, zt_ref, cd_ref, dk_ref, dv_ref, u_sc):
        @pl.when(pl.program_id(1) == 0)
        def _():
            u_sc[...] = jnp.zeros_like(u_sc)
        m0, m1 = _head_masks((128, 128))
        dm, xi, zt, cd = dm_ref[...], xi_ref[...], zt_ref[...], cd_ref[...]
        bm = (cd > 0).astype(F32)
        for c in reversed(range(nchunk)):
            rs = pl.ds(c * RET_CHUNK, RET_CHUNK)
            cs, sn = cos_ref[rs, :], ss_ref[rs, :]
            kb = k_ref[rs, :]
            q = q_ref[rs, :].astype(F32)
            vb = v_ref[rs, :].astype(BF16)
            dob = do_ref[rs, :]
            dof = dob.astype(F32)
            qs = jnp.concatenate([q * m0, q * m1], axis=0).astype(BF16)
            dos = jnp.concatenate([dof * m0, dof * m1], axis=0).astype(BF16)
            s = (_dot_nt(qs, kb) * dm).astype(BF16)
            a = (_dot_nt(dos, vb) * dm).astype(BF16)
            ub = u_sc[...].astype(BF16)
            dk = _dot_tn(a, qs) + _dot_nt(vb, ub) * zt
            dv = _dot_tn(s, dos) + _dot(kb, ub) * zt
            u_sc[...] = cd * u_sc[...] + bm * _dot_tn((q * xi).astype(BF16), dob)
            dk_ref[rs, :] = (_rope_t(dk, cs, sn, 32, 64) * K_SCALE).astype(BF16)
            dv_ref[rs, :] = dv.astype(BF16)

    qkv, rope, tspec = _ret_specs(tb, True, nt, roped=True)
    blk = pl.BlockSpec((tb, 128), lambda p, t: (nt - 1 - t, p))
    out_shape = [jax.ShapeDtypeStruct((T, RET_WIDTH), BF16)] * 2
    if swap is None:
        return pl.pallas_call(
            body, name=name, grid=(4, nt), in_specs=qkv + [blk] + rope + tspec, out_specs=[blk, blk],
            out_shape=out_shape, scratch_shapes=[pltpu.VMEM((128, 128), F32)],
            compiler_params=_cp(("parallel", "arbitrary")))(qr, kr, proj, do, cos, ss, *tabs)
    return pl.pallas_call(
        body, name=name, grid=(4, nt), in_specs=qkv + [blk] + rope + tspec + [ANY], out_specs=[blk, blk, ANY],
        out_shape=out_shape + [jax.ShapeDtypeStruct((4,) + swap.shape[2:], swap.dtype)],
        scratch_shapes=[pltpu.VMEM((128, 128), F32)] + list(SWAP_SCRATCH),
        compiler_params=_cp(("arbitrary", "arbitrary")))(qr, kr, proj, do, cos, ss, *tabs, swap)


def _mix_bwd(dx1, w_out, o_ret, proj, y_mla, gnw, *, name):
    T = dx1.shape[0]
    tm = min(T, 512)

    def body(dx_ref, wo_ref, o_ref, g_ref, ym_ref, gnw_ref, do_ref, dg_ref, dom_ref, dl_ref, dw_ref, dm_ref):
        @pl.when(pl.program_id(0) == 0)
        def _():
            dw_ref[...] = jnp.zeros_like(dw_ref)
        dm_ref[...] = _dot_nt(dx_ref[...].astype(BF16), wo_ref[...])
        m0, m1 = _head_masks((tm, 128))
        lane = lax.broadcasted_iota(jnp.int32, (tm, 128), 1)
        delta = jnp.zeros((tm, 128), F32)

        def gsum(z):
            return jnp.sum(z * m0, axis=1, keepdims=True) * m0 + jnp.sum(z * m1, axis=1, keepdims=True) * m1

        for p in range(4):
            cs = slice(128 * p, 128 * p + 128)
            dy = dm_ref[:, cs]
            o = o_ref[:, cs]
            g = g_ref[:, cs]
            w = gnw_ref[:, cs]
            d = o - gsum(o) * (1.0 / 64)
            rstd = lax.rsqrt(gsum(d * d) * (1.0 / 64) + EPS)
            oh = d * rstd
            sg = _sigmoid(g)
            dn = dy * (g * sg)
            dg_ref[:, cs] = (dy * (oh * w) * (sg * (1.0 + g * (1.0 - sg)))).astype(BF16)
            dw_ref[:, cs] += jnp.sum(dn * oh, axis=0, keepdims=True)
            doh = dn * w
            do = rstd * (doh - gsum(doh) * (1.0 / 64) - oh * (gsum(doh * oh) * (1.0 / 64)))
            do_ref[:, cs] = do.astype(BF16)
            dom = dm_ref[:, 512 + 128 * p:512 + 128 * p + 128]
            dom_ref[:, cs] = dom.astype(BF16)
            pr = dom * ym_ref[:, cs].astype(F32)
            delta = jnp.where(lane == 2 * p, jnp.sum(pr * m0, axis=1, keepdims=True), delta)
            delta = jnp.where(lane == 2 * p + 1, jnp.sum(pr * m1, axis=1, keepdims=True), delta)
        dl_ref[...] = delta.T[0:MLA_HEADS]

    half = pl.BlockSpec((tm, 512), lambda i: (i, 0))
    return pl.pallas_call(
        body, name=name, grid=(T // tm,),
        in_specs=[pl.BlockSpec((tm, D_MODEL), lambda i: (i, 0)), pl.BlockSpec(w_out.shape, lambda i: (0, 0)), half,
                  pl.BlockSpec((tm, 512), lambda i: (i, 3)), half, pl.BlockSpec((1, 512), lambda i: (0, 0))],
        out_specs=[half, half, half, pl.BlockSpec((MLA_HEADS, tm), lambda i: (0, i)),
                   pl.BlockSpec((1, 512), lambda i: (0, 0))],
        out_shape=[jax.ShapeDtypeStruct((T, 512), BF16)] * 3 + [jax.ShapeDtypeStruct((MLA_HEADS, T), F32),
                                                                jax.ShapeDtypeStruct((1, 512), F32)],
        scratch_shapes=[pltpu.VMEM((tm, 1024), F32)],
        compiler_params=_cp(("arbitrary",)))(dx1, w_out, o_ret, proj, y_mla, gnw)


def _mla_prep_fwd(proj, qnw, kvnw, wuq, wk, wv, cos, ss, *, name):
    T = proj.shape[0]
    tm = min(T, 512)

    def body(lat_ref, qnw_ref, kvnw_ref, wuq_ref, wk_ref, wv_ref, cos_ref, ss_ref,
             q_ref, k_ref, v_ref, cqn_ref, ckvn_ref):
        cq = lat_ref[:, 0:256]
        ckv = lat_ref[:, 256:384]
        g3 = lat_ref[:, 384:512]
        cqn = (cq * lax.rsqrt(jnp.mean(cq * cq, axis=-1, keepdims=True) + EPS) * qnw_ref[...]).astype(BF16)
        ckvn = (ckv * lax.rsqrt(jnp.mean(ckv * ckv, axis=-1, keepdims=True) + EPS) * kvnw_ref[...]).astype(BF16)
        cqn_ref[...] = cqn
        ckvn_ref[...] = ckvn
        cs, sn = cos_ref[...], ss_ref[...]
        q = _dot_nt(cqn, wuq_ref[...])
        k = _dot_nt(ckvn, wk_ref[...])
        kpe = _rope(g3, cs, sn, 16, 32)
        for h in range(MLA_HEADS):
            hs = slice(128 * h, 128 * h + 128)
            q_ref[:, hs] = (_rope(q[:, hs], cs, sn, 16, 32) * SCALE).astype(BF16)
            k_ref[:, hs] = (k[:, hs] + kpe).astype(BF16)
        v = _dot_nt(ckvn, wv_ref[...])
        lane = lax.broadcasted_iota(jnp.int32, (tm, 128), 1)
        for p in range(4):
            vp = v[:, 128 * p:128 * p + 128]
            v_ref[:, 256 * p:256 * p + 128] = jnp.where(lane < 64, vp, 1.0).astype(BF16)
            v_ref[:, 256 * p + 128:256 * p + 256] = jnp.where(lane < 64, 1.0, vp).astype(BF16)

    def full(shape):
        return pl.BlockSpec(shape, lambda i: (0, 0))

    def row(w):
        return pl.BlockSpec((tm, w), lambda i: (i, 0))

    return pl.pallas_call(
        body, name=name, grid=(T // tm,),
        in_specs=[pl.BlockSpec((tm, 512), lambda i: (i, 4)), full((1, 256)), full((1, 128)), full((1024, 256)),
                  full((1024, 128)), full((512, 128)), row(128), row(128)],
        out_specs=[row(1024), row(1024), row(1024), row(256), row(128)],
        out_shape=[jax.ShapeDtypeStruct((T, 1024), BF16), jax.ShapeDtypeStruct((T, 1024), BF16),
                   jax.ShapeDtypeStruct((T, 1024), BF16), jax.ShapeDtypeStruct((T, 256), BF16),
                   jax.ShapeDtypeStruct((T, 128), BF16)],
        compiler_params=_cp(("parallel",)))(proj, qnw, kvnw, wuq, wk, wv, cos, ss)


def _mla_prep_bwd(dq, dk, dv, proj, qnw, kvnw, wuq_t, wk_t, wv_t, cos, ss, ret_grads, cqn, ckvn, h, *, name):
    T = proj.shape[0]
    tm = min(T, 256)

    def body(dq_ref, dk_ref, dv_ref, lat_ref, qnw_ref, kvnw_ref, wuq_ref, wk_ref, wv_ref, cos_ref, ss_ref,
             rq_ref, rk_ref, rv_ref, rg_ref, cqn_ref, ckvn_ref, h_ref,
             dproj_ref, gwin_ref, gwuq_ref, gwk_ref, gwv_ref, dqnw_ref, dkvnw_ref, dqp_ref):
        for j, r in enumerate((rq_ref, rk_ref, rv_ref, rg_ref)):
            dproj_ref[:, 512 * j:512 * j + 512] = r[...]
        dlat_ref = dproj_ref.at[:, 2048:2560]

        @pl.when(pl.program_id(0) == 0)
        def _():
            for r in (gwin_ref, gwuq_ref, gwk_ref, gwv_ref, dqnw_ref, dkvnw_ref):
                r[...] = jnp.zeros_like(r)
        cs, sn = cos_ref[...], ss_ref[...]
        dkpe = jnp.zeros((tm, 128), F32)
        for h in range(MLA_HEADS):
            hs = slice(128 * h, 128 * h + 128)
            dqp_ref[:, hs] = _rope_t(dq_ref[:, hs] * SCALE, cs, sn, 16, 32).astype(BF16)
            dkpe = dkpe + dk_ref[:, hs]
        lane = lax.broadcasted_iota(jnp.int32, (tm, 128), 1)
        rope_lane = (lane >= MLA_NOPE) & (lane < MLA_NOPE + MLA_ROPE)
        dg3 = jnp.where(rope_lane, _rope_t(jnp.where(rope_lane, dkpe, 0.0), cs, sn, 16, 32), 0.0)

        def norm_bwd(x, w, dn):
            r = lax.rsqrt(jnp.mean(x * x, axis=-1, keepdims=True) + EPS)
            xh = x * r
            g = dn * w
            return r * (g - xh * jnp.mean(g * xh, axis=-1, keepdims=True)), jnp.sum(dn * xh, axis=0, keepdims=True)

        dqp = dqp_ref[...]
        dkb = dk_ref[...].astype(BF16)
        dvb = dv_ref[...]
        dcqn = _dot(dqp, wuq_ref[...])
        dcq, dqnw = norm_bwd(lat_ref[:, 0:256], qnw_ref[...], dcqn)
        dckvn = _dot(dkb, wk_ref[...]) + _dot(dvb, wv_ref[...])
        dckv, dkvnw = norm_bwd(lat_ref[:, 256:384], kvnw_ref[...], dckvn)
        gwuq_ref[...] += _dot_tn(dqp, cqn_ref[...])
        gwk_ref[...] += _dot_tn(dkb, ckvn_ref[...])
        gwv_ref[...] += _dot_tn(dvb, ckvn_ref[...])
        dqnw_ref[...] += dqnw
        dkvnw_ref[...] += dkvnw
        dlat_ref[:, 0:256] = dcq.astype(BF16)
        dlat_ref[:, 256:384] = dckv.astype(BF16)
        dlat_ref[:, 384:512] = dg3.astype(BF16)
        gwin_ref[...] += _dot_tn(dproj_ref[...], h_ref[...])

    def full(shape):
        return pl.BlockSpec(shape, lambda i: (0, 0))

    def row(w):
        return pl.BlockSpec((tm, w), lambda i: (i, 0))

    return pl.pallas_call(
        body, name=name, grid=(T // tm,),
        in_specs=[row(1024), row(1024), row(512), pl.BlockSpec((tm, 512), lambda i: (i, 4)), full((1, 256)),
                  full((1, 128)), full((1024, 256)), full((1024, 128)), full((512, 128)), row(128), row(128)]
                 + [row(512)] * 4 + [row(256), row(128), row(D_MODEL)],
        out_specs=[row(IN_PAD), full((IN_PAD, D_MODEL)), full((1024, 256)), full((1024, 128)), full((512, 128)),
                   full((1, 256)), full((1, 128))],
        out_shape=[jax.ShapeDtypeStruct((T, IN_PAD), BF16), jax.ShapeDtypeStruct((IN_PAD, D_MODEL), F32),
                   jax.ShapeDtypeStruct((1024, 256), F32), jax.ShapeDtypeStruct((1024, 128), F32),
                   jax.ShapeDtypeStruct((512, 128), F32), jax.ShapeDtypeStruct((1, 256), F32),
                   jax.ShapeDtypeStruct((1, 128), F32)],
        scratch_shapes=[pltpu.VMEM((tm, 1024), BF16)],
        compiler_params=_cp(("arbitrary",)))(dq, dk, dv, proj, qnw, kvnw, wuq_t, wk_t, wv_t, cos, ss, *ret_grads,
                                             cqn, ckvn, h)


def _flash_fwd(q, k, v1, *, name, gather=None):
    T = q.shape[0]
    tq = min(T, 512)
    tk = tq
    nq = T // tq

    def body(q_ref, k_ref, v_ref, *rest):
        if gather is None:
            y_ref, lse_ref = rest
        else:
            x_ref, y_ref, lse_ref, g_ref, *sems = rest
            start, forward, finish = _gather_phases(x_ref, g_ref, *sems)
            pl.when((pl.program_id(0) == 0) & (pl.program_id(1) == 0))(start)
            pl.when((pl.program_id(0) == 1) & (pl.program_id(1) == 0))(forward)
        attend(q_ref, k_ref, v_ref, y_ref, lse_ref)
        if gather is not None:
            pl.when((pl.program_id(0) == 3) & (pl.program_id(1) == nq - 1))(finish)

    def attend(q_ref, k_ref, v_ref, y_ref, lse_ref):
        qi = pl.program_id(1)
        row = lax.broadcasted_iota(jnp.int32, (tq, tk), 0)
        col = lax.broadcasted_iota(jnp.int32, (tq, tk), 1)

        def step(kb, carry, masked):
            ks = pl.ds(pl.multiple_of(kb * tk, tk), tk)
            new = []
            for h in range(2):
                hs = slice(128 * h, 128 * h + 128)
                m, acc = carry[h]
                s = _dot_nt(q_ref[:, hs], k_ref[ks, hs])
                if masked:
                    s = jnp.where(col <= row, s, NEG)
                mn = jnp.maximum(m, jnp.max(s, axis=1, keepdims=True))
                p = jnp.exp((s - mn).astype(BF16))
                acc = jnp.exp(m - mn) * acc + _dot(p, v_ref[ks, hs])
                new.append((mn, acc))
            return tuple(new)

        def unrolled(j, c):
            for u in range(FLASH_UNROLL):
                c = step(FLASH_UNROLL * j + u, c, False)
            return c

        init = (jnp.full((tq, 1), NEG, F32), jnp.zeros((tq, 128), F32))
        carry = lax.fori_loop(0, qi // FLASH_UNROLL, unrolled, (init, init))
        carry = lax.fori_loop(FLASH_UNROLL * (qi // FLASH_UNROLL), qi, lambda kb, c: step(kb, c, False), carry)
        (ma, acca), (mb, accb) = step(qi, carry, True)
        lane = lax.broadcasted_iota(jnp.int32, (tq, 128), 1)
        la, lb = pltpu.roll(acca, 64, 1), pltpu.roll(accb, 64, 1)
        y_ref[...] = jnp.where(lane < 64, acca / la, accb / lb).astype(BF16)
        lse_ref[0, 0] = jnp.broadcast_to(ma + jnp.log(acca[:, 64:65]), (tq, 128)).T[0:1]
        lse_ref[1, 0] = jnp.broadcast_to(mb + jnp.log(accb[:, 0:1]), (tq, 128)).T[0:1]

    in_specs = [pl.BlockSpec((tq, 256), lambda p, i: (i, p)), pl.BlockSpec((T, 256), lambda p, i: (0, p)),
                pl.BlockSpec((T, 256), lambda p, i: (0, p))]
    out_specs = [pl.BlockSpec((tq, 128), lambda p, i: (i, p)), pl.BlockSpec((2, 1, 1, tq), lambda p, i: (p, i, 0, 0))]
    out_shape = [jax.ShapeDtypeStruct((T, MLA_WIDTH), BF16), jax.ShapeDtypeStruct((MLA_HEADS, nq, 1, tq), F32)]
    if gather is None:
        return pl.pallas_call(body, name=name, grid=(4, nq), in_specs=in_specs, out_specs=out_specs,
                              out_shape=out_shape, compiler_params=_cp(("parallel", "arbitrary")))(q, k, v1)
    return pl.pallas_call(
        body, name=name, grid=(4, nq), in_specs=in_specs + [ANY], out_specs=out_specs + [ANY],
        out_shape=out_shape + [jax.ShapeDtypeStruct((N_DEV,) + gather.shape, gather.dtype)],
        scratch_shapes=list(GATHER_SCRATCH),
        compiler_params=_cp(("arbitrary", "arbitrary")))(q, k, v1, gather)


def _flash_bwd(q, k, v, do, lse, delta, *, name, exchange=None):
    T = q.shape[0]
    tq = min(T, 512)
    tk = tq
    nq = T // tq

    def body(q_ref, k_ref, v_ref, do_ref, lse_ref, dl_ref, *rest):
        if exchange is None:
            backward(q_ref, k_ref, v_ref, do_ref, lse_ref, dl_ref, *rest)
        else:
            p_ref, dqt_ref, dk_ref, dv_ref, got_ref, *sems = rest
            start, finish = _exchange_phases(p_ref, got_ref, *sems)
            pl.when((pl.program_id(0) == 0) & (pl.program_id(1) == 0))(start)
            backward(q_ref, k_ref, v_ref, do_ref, lse_ref, dl_ref, dqt_ref, dk_ref, dv_ref)
            pl.when((pl.program_id(0) == 3) & (pl.program_id(1) == nq - 1))(finish)

    def backward(q_ref, k_ref, v_ref, do_ref, lse_ref, dl_ref, dqt_ref, dk_ref, dv_ref):
        kb = pl.program_id(1)

        @pl.when(kb == 0)
        def _():
            dqt_ref[...] = jnp.zeros_like(dqt_ref)
        krow = lax.broadcasted_iota(jnp.int32, (tk, tq), 0)
        qcol = lax.broadcasted_iota(jnp.int32, (tk, tq), 1)
        masks = _head_masks((tk, 128))
        vms = [(v_ref[:, 128 * h:128 * h + 128].astype(F32) * masks[h]).astype(BF16) for h in range(2)]

        def step(qi, carry, masked):
            qs = pl.ds(pl.multiple_of(qi * tq, tq), tq)
            dob = do_ref[qs, :]
            dof = dob.astype(F32)
            dks, dv_acc = list(carry[:2]), carry[2]
            for h in range(2):
                hs = slice(128 * h, 128 * h + 128)
                kh = k_ref[:, hs]
                qh = q_ref[qs, hs]
                st = _dot_nt(kh, qh)
                pt = jnp.exp((st - lse_ref[h, qi]).astype(BF16))
                if masked:
                    pt = jnp.where(krow <= qcol, pt, jnp.zeros_like(pt))
                dv_acc = dv_acc + _dot(pt, (dof * masks[h]).astype(BF16))
                dpt = _dot_nt(vms[h], dob)
                dst = pt * (dpt - dl_ref[h, qi]).astype(BF16)
                dks[h] = dks[h] + _dot(dst, qh)
                dqt_ref[qi, hs, :] += _dot_tn(kh, dst)
            return dks[0], dks[1], dv_acc

        zero = jnp.zeros((tk, 128), F32)
        carry = step(kb, (zero, zero, zero), True)

        def unrolled(j, c):
            for u in range(FLASH_BWD_UNROLL):
                c = step(kb + 1 + FLASH_BWD_UNROLL * j + u, c, False)
            return c

        trips = (nq - 1 - kb) // FLASH_BWD_UNROLL
        carry = lax.fori_loop(0, trips, unrolled, carry)
        dk0, dk1, dv_acc = lax.fori_loop(kb + 1 + FLASH_BWD_UNROLL * trips, nq, lambda qi, c: step(qi, c, False), carry)
        dk_ref[:, 0:128] = dk0
        dk_ref[:, 128:256] = dk1
        dv_ref[...] = dv_acc.astype(BF16)

    stat = pl.BlockSpec((2, nq, 1, tq), lambda p, j: (p, 0, 0, 0))
    in_specs = [pl.BlockSpec((T, 256), lambda p, j: (0, p)), pl.BlockSpec((tk, 256), lambda p, j: (j, p)),
                pl.BlockSpec((tk, 256), lambda p, j: (j, p)), pl.BlockSpec((T, 128), lambda p, j: (0, p)), stat, stat]
    out_specs = [pl.BlockSpec((None, nq, 256, tq), lambda p, j: (p, 0, 0, 0)),
                 pl.BlockSpec((tk, 256), lambda p, j: (j, p)), pl.BlockSpec((tk, 128), lambda p, j: (j, p))]
    out_shape = [jax.ShapeDtypeStruct((4, nq, 256, tq), F32), jax.ShapeDtypeStruct((T, 1024), F32),
                 jax.ShapeDtypeStruct((T, MLA_WIDTH), BF16)]
    if exchange is None:
        return pl.pallas_call(body, name=name, grid=(4, nq), in_specs=in_specs, out_specs=out_specs,
                              out_shape=out_shape,
                              compiler_params=_cp(("parallel", "arbitrary")))(q, k, v, do, lse, delta)
    return pl.pallas_call(
        body, name=name, grid=(4, nq), in_specs=in_specs + [ANY], out_specs=out_specs + [ANY],
        out_shape=out_shape + [jax.ShapeDtypeStruct(exchange.shape, exchange.dtype)],
        scratch_shapes=list(EXCHANGE_SCRATCH),
        compiler_params=_cp(("arbitrary", "arbitrary")))(q, k, v, do, lse, delta, exchange)


def _shift_down(x, n, prev8):
    r = pltpu.roll(x, n, 0)
    row = lax.broadcasted_iota(jnp.int32, prev8.shape, 0)
    first = jnp.where(row < n, pltpu.roll(prev8, n, 0), r[:8])
    if x.shape[0] == 8:
        return first
    return jnp.concatenate([first, r[8:]], axis=0)


def _shift_up(x, n, next8):
    tm = x.shape[0]
    r = pltpu.roll(x, tm - n, 0)
    row = lax.broadcasted_iota(jnp.int32, next8.shape, 0)
    last = jnp.where(row >= 8 - n, pltpu.roll(next8, 8 - n, 0), r[tm - 8:])
    return jnp.concatenate([r[:tm - 8], last], axis=0)


def _conv_pre(u, prev8, cw_ref, cb_ref):
    p1 = _shift_down(u, 1, prev8)
    p2 = _shift_down(u, 2, prev8)
    up = cb_ref[...] + cw_ref[0:1, :] * p2 + cw_ref[1:2, :] * p1 + cw_ref[2:3, :] * u
    return up, p1, p2


def _up_proj_conv(x1, nw, w_up_t, cw, cb, *, name):
    T, K = x1.shape
    tm = min(T, 256)

    def body(x_ref, nw_ref, w_ref, cw_ref, cb_ref, h_ref, u_ref, a_ref, carry_sc):
        @pl.when(pl.program_id(0) == 0)
        def _():
            carry_sc[...] = jnp.zeros_like(carry_sc)
        xv = x_ref[...]
        h = (xv * lax.rsqrt(jnp.mean(xv * xv, axis=-1, keepdims=True) + EPS) * nw_ref[...]).astype(BF16)
        h_ref[...] = h
        for blk in range(2):
            ups = []
            for half in range(2):
                cs = slice((2 * blk + half) * FF_HALF, (2 * blk + half + 1) * FF_HALF)
                u = _dot_nt(h, w_ref[cs, :])
                u_ref[:, cs] = u
                prev = carry_sc[:, cs]
                ups.append(cb_ref[:, cs] + cw_ref[0:1, cs] * _shift_down(u, 2, prev)
                           + cw_ref[1:2, cs] * _shift_down(u, 1, prev) + cw_ref[2:3, cs] * u)
                carry_sc[:, cs] = u[tm - 8:]
            gate, val = ups
            a_ref[:, blk * FF_HALF:(blk + 1) * FF_HALF] = (gate * _sigmoid(gate) * val).astype(BF16)

    def full(shape):
        return pl.BlockSpec(shape, lambda i: (0, 0))

    return pl.pallas_call(
        body, name=name, grid=(T // tm,),
        in_specs=[pl.BlockSpec((tm, K), lambda i: (i, 0)), full(nw.shape), full(w_up_t.shape), full(cw.shape),
                  full(cb.shape)],
        out_specs=[pl.BlockSpec((tm, K), lambda i: (i, 0)), pl.BlockSpec((tm, 2 * D_FF), lambda i: (i, 0)),
                   pl.BlockSpec((tm, D_FF), lambda i: (i, 0))],
        out_shape=[jax.ShapeDtypeStruct((T, K), BF16), jax.ShapeDtypeStruct((T, 2 * D_FF), F32),
                   jax.ShapeDtypeStruct((T, D_FF), BF16)],
        scratch_shapes=[pltpu.VMEM((8, 2 * D_FF), F32)],
        compiler_params=_cp(("arbitrary",)))(x1, nw, w_up_t, cw, cb)


def _conv_bwd(u, da, cw, cb, *, name):
    T = u.shape[0]
    tm = min(T, 512)
    W = 2 * FF_HALF
    nt = T // tm

    def body(u_ref, prev_ref, next_ref, da_ref, dan_ref, cw_ref, cb_ref, du_ref, dw0_ref, dw1_ref, dw2_ref, db_ref):
        i = pl.program_id(1)

        @pl.when(i == 0)
        def _():
            for r in (dw0_ref, dw1_ref, dw2_ref, db_ref):
                r[...] = jnp.zeros_like(r)

        def dpre(u, prev8, da):
            up, p1, p2 = _conv_pre(u, prev8, cw_ref, cb_ref)
            gate, val = up[:, :FF_HALF], up[:, FF_HALF:]
            sg = _sigmoid(gate)
            dgate = da * val * (sg * (1.0 + gate * (1.0 - sg)))
            dval = da * (gate * sg)
            return jnp.concatenate([dgate, dval], axis=1), p1, p2

        u = u_ref[...]
        prev = jnp.where(i > 0, prev_ref[...], 0.0)
        dup, p1, p2 = dpre(u, prev, da_ref[...])
        dupn, _, _ = dpre(next_ref[...], u[tm - 8:], dan_ref[...])
        dupn = jnp.where(i < nt - 1, dupn, 0.0)
        du = cw_ref[2:3, :] * dup + cw_ref[1:2, :] * _shift_up(dup, 1, dupn) + cw_ref[0:1, :] * _shift_up(dup, 2, dupn)
        du_ref[...] = du.astype(BF16)
        dw0_ref[...] += jnp.sum(dup * p2, axis=0, keepdims=True)
        dw1_ref[...] += jnp.sum(dup * p1, axis=0, keepdims=True)
        dw2_ref[...] += jnp.sum(dup * u, axis=0, keepdims=True)
        db_ref[...] += jnp.sum(dup, axis=0, keepdims=True)

    nxt = lambda j, i: (jnp.minimum((i + 1) * (tm // 8), T // 8 - 1), j)
    vec = pl.BlockSpec((1, W), lambda j, i: (0, j))
    return pl.pallas_call(
        body, name=name, grid=(2, nt),
        in_specs=[pl.BlockSpec((tm, W), lambda j, i: (i, j)),
                  pl.BlockSpec((8, W), lambda j, i: (jnp.maximum(i * (tm // 8) - 1, 0), j)),
                  pl.BlockSpec((8, W), nxt),
                  pl.BlockSpec((tm, FF_HALF), lambda j, i: (i, j)), pl.BlockSpec((8, FF_HALF), nxt),
                  pl.BlockSpec((3, W), lambda j, i: (0, j)), vec],
        out_specs=[pl.BlockSpec((tm, W), lambda j, i: (i, j)), vec, vec, vec, vec],
        out_shape=[jax.ShapeDtypeStruct((T, 2 * D_FF), BF16)] + [jax.ShapeDtypeStruct((1, 2 * D_FF), F32)] * 4,
        compiler_params=_cp(("parallel", "arbitrary")))(u, u, u, da, da, cw, cb)


def _sum_chips(slots, *, name):
    ns, R, C = slots.shape
    tr = _row_tile(R)

    def body(g_ref, o_ref):
        g = g_ref[0].astype(F32)
        for s in range(1, ns):
            g = g + g_ref[s].astype(F32)
        o_ref[...] = g

    return pl.pallas_call(
        body, name=name, grid=(R // tr,), in_specs=[pl.BlockSpec((ns, tr, C), lambda i: (0, i, 0))],
        out_specs=pl.BlockSpec((tr, C), lambda i: (i, 0)), out_shape=jax.ShapeDtypeStruct((R, C), F32),
        compiler_params=_cp(("parallel",)))(slots)


def _place():
    return lax.axis_index("x"), lax.axis_index("y"), lax.axis_index("c")


GATHER_SCRATCH = (pltpu.SemaphoreType.DMA((7,)), pltpu.SemaphoreType.DMA((7,)), pltpu.SemaphoreType.DMA)
EXCHANGE_SCRATCH = (pltpu.SemaphoreType.DMA((3,)), pltpu.SemaphoreType.DMA((3,)), pltpu.SemaphoreType.DMA)


def _gather_phases(x_ref, out_ref, send_sems, recv_sems, local_sem):
    x_, y_, c_ = _place()
    me, sibling = (x_, y_, c_), (x_, y_, 1 - c_)
    chips = [(1 - x_, y_), (x_, 1 - y_), (1 - x_, 1 - y_)]

    def slot(px, py, pc):
        return out_ref.at[4 * px + 2 * py + pc]

    def copy(k, block, to, src=None):
        return pltpu.make_async_remote_copy(
            src_ref=slot(*block) if src is None else src, dst_ref=slot(*block),
            send_sem=send_sems.at[k], recv_sem=recv_sems.at[k], device_id=to, device_id_type=MESH)

    def mine():
        return pltpu.make_async_copy(x_ref, slot(*me), local_sem)

    def first():
        return [copy(0, me, sibling, src=x_ref)] + [copy(1 + j, me, (*chip, c_), src=x_ref)
                                                     for j, chip in enumerate(chips)]

    def passed():
        return [copy(4 + j, (*chip, c_), sibling) for j, chip in enumerate(chips)]

    def start():
        mine().start()
        for cp in first():
            cp.start()

    def forward():
        fwd = passed()
        for j, chip in enumerate(chips):
            copy(1 + j, (*chip, c_), me).wait_recv()
            fwd[j].start()

    def finish():
        copy(0, sibling, me).wait_recv()
        for j, chip in enumerate(chips):
            copy(4 + j, (*chip, 1 - c_), me).wait_recv()
        for cp in first() + passed():
            cp.wait_send()
        mine().wait()

    return start, forward, finish


def _exchange_phases(p_ref, out_ref, send_sems, recv_sems, local_sem):
    x_, y_, c_ = _place()
    me_k = 2 * x_ + y_
    chips = [(1 - x_, y_), (x_, 1 - y_), (1 - x_, 1 - y_)]

    def local():
        return pltpu.make_async_copy(p_ref.at[me_k], out_ref.at[me_k], local_sem)

    def copy(j, src_k, dst_k, chip):
        return pltpu.make_async_remote_copy(
            src_ref=p_ref.at[src_k], dst_ref=out_ref.at[dst_k], send_sem=send_sems.at[j],
            recv_sem=recv_sems.at[j], device_id=(*chip, c_), device_id_type=MESH)

    def sends():
        return [copy(j, 2 * px + py, me_k, (px, py)) for j, (px, py) in enumerate(chips)]

    def start():
        local().start()
        for cp in sends():
            cp.start()

    def finish():
        for j, (px, py) in enumerate(chips):
            copy(j, me_k, 2 * px + py, (px, py)).wait_recv()
        for cp in sends():
            cp.wait_send()
        local().wait()

    return start, finish


def _all_gather(x, *, name, in_vmem):
    def body(x_ref, out_ref, send_sems, recv_sems, local_sem):
        for phase in _gather_phases(x_ref, out_ref, send_sems, recv_sems, local_sem):
            phase()

    spec = pl.BlockSpec(memory_space=pltpu.VMEM) if in_vmem else ANY
    return pl.pallas_call(
        body, name=name, out_shape=jax.ShapeDtypeStruct((N_DEV,) + x.shape, x.dtype),
        in_specs=[spec], out_specs=spec, scratch_shapes=list(GATHER_SCRATCH),
        compiler_params=pltpu.CompilerParams(vmem_limit_bytes=VMEM_LIMIT))(x)


def _small_rows():
    table, row = [], 0
    for n, size in SMALL_VECTORS:
        table.append((n, size, row))
        row += -(-size // PACK_COLS)
    return table


def _ff_chunk_source(c):
    block, off = divmod(c * 128, FF_HALF)
    return (0, 2, 1, 3)[block] * FF_HALF + off


def _pack_small(parts, *, name):
    table = _small_rows()

    def body(*refs):
        out = refs[-1]
        out[...] = jnp.zeros_like(out)
        for ref, (n, size, row) in zip(refs, table):
            if size != 2 * D_FF:
                out[row:row + 1, 0:size] = ref[...]
                continue
            for c in range(size // 128):
                src = _ff_chunk_source(c)
                r, lane = divmod(c * 128, PACK_COLS)
                out[row + r:row + r + 1, lane:lane + 128] = ref[:, src:src + 128]

    return pl.pallas_call(body, name=name, out_shape=jax.ShapeDtypeStruct((SMALL_ROWS, PACK_COLS), F32))(
        *[parts[n] for n, _, _ in table])


def _sum_small(g, *, name):
    table = _small_rows()
    shapes = [(n, size) for n, size, _ in table if not n.startswith("conv_w")]
    shapes.insert(7, ("conv_w", 2 * D_FF))

    def body(g_ref, *outs):
        def total(row, width):
            acc = g_ref[0, row:row + 1, 0:width]
            for d in range(1, N_DEV):
                acc = acc + g_ref[d, row:row + 1, 0:width]
            return acc

        out_of = {n: o for (n, _), o in zip(shapes, outs)}
        for n, size, row in table:
            o, j = (out_of["conv_w"], int(n[-1])) if n.startswith("conv_w") else (out_of[n], 0)
            for i in range(-(-size // PACK_COLS)):
                width = min(PACK_COLS, size - PACK_COLS * i)
                o[j:j + 1, PACK_COLS * i:PACK_COLS * i + width] = total(row + i, width)

    out_shape = [jax.ShapeDtypeStruct((3 if n == "conv_w" else 1, size), F32) for n, size in shapes]
    res = pl.pallas_call(body, name=name, out_shape=out_shape)(g)
    return {n: r for (n, _), r in zip(shapes, res)}


def _adamw_multi(ws, ms, vs, gs, *, name):
    k = len(ws)

    def body(*refs):
        w_refs, m_refs, v_refs, g_refs = (refs[i * k:(i + 1) * k] for i in range(4))
        outs = refs[4 * k:]
        for i in range(k):
            g = g_refs[i][...]
            mn = ADAM_B1 * m_refs[i][...] + (1.0 - ADAM_B1) * g
            vn = ADAM_B2 * v_refs[i][...] + (1.0 - ADAM_B2) * (g * g)
            m_hat = mn / (1.0 - ADAM_B1 ** ADAM_STEP)
            v_hat = vn / (1.0 - ADAM_B2 ** ADAM_STEP)
            outs[i][...] = g
            outs[k + i][...] = -ADAM_LR * (m_hat / (jnp.sqrt(v_hat) + ADAM_EPS) + ADAM_WD * w_refs[i][...])
            outs[2 * k + i][...] = mn
            outs[3 * k + i][...] = vn

    out_shape = [jax.ShapeDtypeStruct(w.shape, F32) for _ in range(4) for w in ws]
    res = pl.pallas_call(body, name=name, out_shape=out_shape, compiler_params=_cp())(*ws, *ms, *vs, *gs)
    return [res[i * k:(i + 1) * k] for i in range(4)]


SWAP_SCRATCH = (pltpu.SemaphoreType.DMA((4,)), pltpu.SemaphoreType.DMA((4,)))


def _swap_phases(g_ref, out_ref, send_sems, recv_sems):
    x_, y_, c_ = _place()

    def copies():
        return [pltpu.make_async_remote_copy(src_ref=g_ref.at[k, 1 - c_], dst_ref=out_ref.at[k],
                                             send_sem=send_sems.at[k], recv_sem=recv_sems.at[k],
                                             device_id=(x_, y_, 1 - c_), device_id_type=MESH) for k in range(4)]

    def start():
        for cp in copies():
            cp.start()

    def finish():
        for cp in copies():
            cp.wait()

    return start, finish


def _swap_sibling(g, *, name):
    def body(g_ref, out_ref, send_sems, recv_sems):
        for phase in _swap_phases(g_ref, out_ref, send_sems, recv_sems):
            phase()

    return pl.pallas_call(
        body, name=name, out_shape=jax.ShapeDtypeStruct((4,) + g.shape[2:], g.dtype), in_specs=[ANY], out_specs=ANY,
        scratch_shapes=list(SWAP_SCRATCH))(g)


def _row_tile(R):
    for cand in (256, 400, 200):
        if R % cand == 0:
            return cand
    return R


def _add_own(g, b, *, name, out_dtype):
    n, _, R, C = g.shape
    tr = _row_tile(R)

    def body(c_ref, g_ref, b_ref, o_ref):
        del c_ref
        o_ref[...] = (g_ref[...] + b_ref[...]).astype(out_dtype)

    blk = pl.BlockSpec((None, tr, C), lambda s, i, c: (s, i, 0))
    grid_spec = pltpu.PrefetchScalarGridSpec(
        num_scalar_prefetch=1, grid=(n, R // tr),
        in_specs=[pl.BlockSpec((None, None, tr, C), lambda s, i, c: (s, c[0], i, 0)), blk], out_specs=blk)
    core = jnp.reshape(lax.axis_index("c"), (1,)).astype(jnp.int32)
    return pl.pallas_call(body, name=name, grid_spec=grid_spec, out_shape=jax.ShapeDtypeStruct(b.shape, out_dtype),
                          compiler_params=_cp(("parallel", "parallel")))(core, g, b)


def _pack_local(parts, group, tail=None):
    table, rows = group
    segs = []
    for n, r, rp, tr in table:
        w = parts[n].T if tr else parts[n]
        segs.append(jnp.pad(w.reshape(r, PACK_COLS), ((0, rp - r), (0, 0))))
    spare = rows - sum(rp for _, _, rp, _ in table)
    segs.append(jnp.zeros((spare, PACK_COLS), segs[0].dtype) if tail is None else tail)
    return jnp.concatenate(segs, axis=0)


CONV_W_BITS = 2 * 3 * 704
SPARE_EARLY = 16


def _conv_w_as_rows(conv_w_shard):
    bits = lax.bitcast_convert_type(conv_w_shard.reshape(-1), BF16).reshape(-1)
    return jnp.pad(bits, (0, SPARE_EARLY * PACK_COLS - CONV_W_BITS)).reshape(SPARE_EARLY, PACK_COLS)


def _conv_w_from_rows(gathered):
    bits = gathered[:, EARLY[1] - SPARE_EARLY:].reshape(N_DEV, -1)[:, :CONV_W_BITS].reshape(N_DEV, 3 * 704, 2)
    w = lax.bitcast_convert_type(bits, F32).reshape(N_DEV, 3, 704)
    return w.transpose(1, 0, 2).reshape(3, 2 * D_FF)


def _unpack_local(packed, like, group):
    out, off = {}, 0
    for n, r, rp, tr in group[0]:
        rows, cols = like[n].shape
        seg = packed[off:off + r]
        out[n] = (seg.reshape(cols, rows).T if tr else seg)[None]
        off += rp
    return out


def _segments(g, group):
    out, off = {}, 0
    for n, r, rp, _ in group[0]:
        out[n] = g[:, off:off + r]
        off += rp
    return out


def _pack_grads(parts, group):
    table, rows = group
    segs = [jnp.pad(parts[n], ((0, 0), (0, rp - parts[n].shape[1]), (0, 0))) for n, _, rp, _ in table]
    segs.append(jnp.zeros((N_DEV, rows - sum(rp for _, _, rp, _ in table), PACK_COLS), F32))
    return jnp.concatenate(segs, axis=1)


def _owner_rows_early(g):
    g_in = jnp.concatenate([g["w_in_t"][:2432], g["w_in_t"][2496:2528]], axis=0).reshape(N_DEV, 308, PACK_COLS)
    g_uq = g["w_uq_t"].reshape(N_DEV, 128, MLA_Q_RANK)[:, :96].reshape(N_DEV, 24, PACK_COLS)
    g_ukv = jnp.concatenate([g["w_k_t"].reshape(N_DEV, 128, MLA_KV_RANK)[:, :64],
                             g["w_v_t"].reshape(N_DEV, 64, MLA_KV_RANK)], axis=1).reshape(N_DEV, 16, PACK_COLS)
    return dict(w_in=g_in, w_uq=g_uq, w_ukv=g_ukv)


def _owner_rows_late(g):
    g_up = g["w_up_t"].reshape(2, 2, 2, 704, PACK_COLS).swapaxes(0, 1).reshape(N_DEV, 704, PACK_COLS)
    return dict(w_out=g["w_out"].reshape(N_DEV, 128, PACK_COLS), w_up=g_up,
                w_down=g["w_down"].reshape(N_DEV, 352, PACK_COLS))


def _reduce_to_pairs(gp, *, name):
    gp = gp.reshape(4, 2, gp.shape[1], PACK_COLS)
    return _add_own(gp, _swap_sibling(gp, name=name + "_swap"), out_dtype=BF16, name=name + "_sum")


def _interleave_ff(w):
    g, v = w[..., :D_FF], w[..., D_FF:]
    return jnp.concatenate([g[..., :FF_HALF], v[..., :FF_HALF], g[..., FF_HALF:], v[..., FF_HALF:]], axis=-1)


def _rope_tables(pos):
    p = pos.astype(F32)[:, None]
    inv_r = ROPE_BASE ** (-jnp.arange(0, RET_HEAD_DIM, 2, dtype=F32) / RET_HEAD_DIM)
    ang = p * jnp.tile(inv_r, 4)
    sign_r = jnp.tile(jnp.concatenate([-jnp.ones((32,), F32), jnp.ones((32,), F32)]), 2)
    cos_r, ss_r = jnp.cos(ang), jnp.sin(ang) * sign_r
    inv_m = ROPE_BASE ** (-jnp.arange(0, MLA_ROPE, 2, dtype=F32) / MLA_ROPE)
    ang = p * jnp.concatenate([jnp.zeros((64,), F32), inv_m, inv_m, jnp.zeros((32,), F32)])
    sign_m = jnp.concatenate([jnp.zeros((64,), F32), -jnp.ones((16,), F32), jnp.ones((16,), F32), jnp.zeros((32,), F32)])
    cos_m, ss_m = jnp.cos(ang), jnp.sin(ang) * sign_m
    return cos_r, ss_r, cos_m, ss_m


def _prep_early(gathered):
    seg = _segments(gathered, EARLY)
    w_in_t = seg["w_in"].reshape(IN_WIDTH, D_MODEL)
    z = lambda n: jnp.zeros((n, D_MODEL), BF16)
    w_in_t = jnp.concatenate([w_in_t[:2432], z(64), w_in_t[2432:2464], z(32)], axis=0)
    w_uq_t = jnp.pad(seg["w_uq"].reshape(MLA_HEADS, 96, MLA_Q_RANK), ((0, 0), (0, 32), (0, 0))).reshape(1024, MLA_Q_RANK)
    ukv = seg["w_ukv"].reshape(MLA_HEADS, 128, MLA_KV_RANK)
    w_k_t = jnp.pad(ukv[:, :64], ((0, 0), (0, 64), (0, 0))).reshape(1024, MLA_KV_RANK)
    w_v_t = ukv[:, 64:].reshape(512, MLA_KV_RANK)
    return dict(w_in_t=w_in_t, w_uq_t=w_uq_t, w_k_t=w_k_t, w_v_t=w_v_t)


def _prep_late(gathered):
    seg = _segments(gathered, LATE)
    w_up_t = seg["w_up"].reshape(2, 2, 2, 704, D_MODEL).swapaxes(0, 1).reshape(2 * D_FF, D_MODEL)
    return dict(w_out=seg["w_out"].reshape(1024, D_MODEL), w_up_t=w_up_t, w_down=seg["w_down"].reshape(D_FF, D_MODEL))


def _local_step(x, pos, tgt, early, sm, late):
    dist = not isinstance(late, dict)
    cos_r, ss_r, cos_m, ss_m = _rope_tables(pos)
    tabs = _ret_tables()

    if dist:
        h, gathered = _rmsnorm_fwd(x, sm["attn_norm_w"], gather=early, name="attn_norm")
        W = _prep_early(gathered)
        sm = {**sm, "conv_w": _interleave_ff(_conv_w_from_rows(gathered))}
    else:
        h = _rmsnorm_fwd(x, sm["attn_norm_w"], name="attn_norm")
        W = early
    proj = _mm(h, W["w_in_t"], bt=True, name="in_proj")
    y_ret, o_ret, qr, kr = _ret_fwd(proj, cos_r, ss_r, tabs, sm["ret_gn_w"], name="ret_fwd")
    q, k, v1, cqn, ckvn = _mla_prep_fwd(proj, sm["mla_q_norm_w"], sm["mla_kv_norm_w"], W["w_uq_t"], W["w_k_t"],
                                       W["w_v_t"], cos_m, ss_m, name="mla_prep")
    T = x.shape[0]
    tq = min(T, 512)
    if dist:
        y_mla, lse, gathered = _flash_fwd(q, k, v1, gather=late, name="mla_attn")
        W = {**W, **_prep_late(gathered)}
    else:
        y_mla, lse = _flash_fwd(q, k, v1, name="mla_attn")
        W = {**W, **late}
    mixed = (y_ret, y_mla)
    x1 = _mm(mixed, W["w_out"], add=x, name="out_proj")
    h2, u, a = _up_proj_conv(x1, sm["ffn_norm_w"], W["w_up_t"], sm["conv_w"], sm["conv_b"], name="ffn_norm_up_conv")
    loss, dx2, dx2b, d_final = _down_proj_loss(a, W["w_down"], x1, tgt, sm["final_norm_w"], name="down_proj_loss")

    g = {}
    g["w_down"] = _mm_tn(a, dx2b, name="dw_down")
    da = _mm(dx2b, W["w_down"], bt=True, name="d_act")
    du, dcw0, dcw1, dcw2, dcb = _conv_bwd(u, da, sm["conv_w"], sm["conv_b"], name="conv_bwd")
    g["w_up_t"] = _mm_tn(du, h2, name="dw_up")
    dx1, d_ffn = _mm_norm_bwd(du, W["w_up_t"], x1, sm["ffn_norm_w"], dx2, name="d_h2_ffn_norm_bwd")

    g["w_out"] = _mm_tn(mixed, dx1, name="dw_out")
    do_ret, dg, do_mla, delta, d_gn = _mix_bwd(dx1, W["w_out"], o_ret, proj, y_mla, sm["ret_gn_w"], name="d_mixed_mix_bwd")
    drq = _ret_bwd_dq(qr, kr, proj, do_ret, cos_r, ss_r, tabs, name="ret_bwd_dq")
    delta_r = delta.reshape(MLA_HEADS, T // tq, 1, tq)
    if dist:
        gl = _pack_grads(_owner_rows_late(g), LATE).reshape(4, 2, LATE[1], PACK_COLS)
        drk, drv, theirs = _ret_bwd_dkv(qr, kr, proj, do_ret, cos_r, ss_r, tabs, swap=gl, name="ret_bwd_dkv")
        pair = _add_own(gl, theirs, out_dtype=BF16, name="grad_late_sum")
        dqt, dk, dv, slots_late = _flash_bwd(q, k, v1, do_mla, lse, delta_r, exchange=pair, name="mla_attn_bwd")
    else:
        drk, drv = _ret_bwd_dkv(qr, kr, proj, do_ret, cos_r, ss_r, tabs, name="ret_bwd_dkv")
        dqt, dk, dv = _flash_bwd(q, k, v1, do_mla, lse, delta_r, name="mla_attn_bwd")
        slots_late = None
    dq = dqt.transpose(1, 3, 0, 2).reshape(T, MLA_HEADS * 128)
    dproj, g["w_in_t"], g["w_uq_t"], g["w_k_t"], g["w_v_t"], d_qn, d_kvn = _mla_prep_bwd(
        dq, dk, dv, proj, sm["mla_q_norm_w"], sm["mla_kv_norm_w"], W["w_uq_t"], W["w_k_t"], W["w_v_t"], cos_m, ss_m,
        (drq, drk, drv, dg), cqn, ckvn, h, name="mla_prep_bwd")
    if dist:
        pair = _reduce_to_pairs(_pack_grads(_owner_rows_early(g), EARLY), name="grad_early")
        grad_x, d_attn, slots_early = _mm_norm_bwd(dproj, W["w_in_t"], x, sm["attn_norm_w"], dx1, exchange=pair,
                                                   name="d_h_attn_norm_bwd")
    else:
        grad_x, d_attn = _mm_norm_bwd(dproj, W["w_in_t"], x, sm["attn_norm_w"], dx1, name="d_h_attn_norm_bwd")
        slots_early = None

    small = dict(attn_norm_w=d_attn, ret_gn_w=d_gn, mla_q_norm_w=d_qn, mla_kv_norm_w=d_kvn, ffn_norm_w=d_ffn,
                 conv_b=dcb, final_norm_w=d_final, conv_w0=dcw0, conv_w1=dcw1, conv_w2=dcw2, loss=loss)
    return loss, grad_x, g, small, slots_early, slots_late


def kernel(x, positions, attn_norm_w, w_in, ret_gn_w, mla_q_norm_w, w_uq, mla_kv_norm_w, w_ukv, w_out, ffn_norm_w, w_up, conv_w, conv_b, w_down, final_norm_w, loss_target, m_attn_norm_w, m_w_in, m_ret_gn_w, m_mla_q_norm_w, m_w_uq, m_mla_kv_norm_w, m_w_ukv, m_w_out, m_ffn_norm_w, m_w_up, m_conv_w, m_conv_b, m_w_down, m_final_norm_w, v_attn_norm_w, v_w_in, v_ret_gn_w, v_mla_q_norm_w, v_w_uq, v_mla_kv_norm_w, v_w_ukv, v_w_out, v_ffn_norm_w, v_w_up, v_conv_w, v_conv_b, v_w_down, v_final_norm_w):
    a = dict(locals())
    x_, y_, c_ = _place()
    dev = 4 * x_ + 2 * y_ + c_

    shard = {n: a[n][0] for n in BIG_NAMES}
    shard16 = {n: w.astype(BF16) for n, w in shard.items()}
    sm = dict(attn_norm_w=attn_norm_w, ret_gn_w=ret_gn_w, mla_q_norm_w=mla_q_norm_w, mla_kv_norm_w=mla_kv_norm_w,
              ffn_norm_w=ffn_norm_w, final_norm_w=final_norm_w.reshape(1, D_MODEL), conv_b=_interleave_ff(conv_b))

    loss, grad_x, _, gs, slots_early, slots_late = _local_step(
        x[0], positions[0], loss_target[0], _pack_local(shard16, EARLY, tail=_conv_w_as_rows(conv_w[0])), sm,
        _pack_local(shard16, LATE))

    big = [{}, {}, {}, {}]
    for group, slots, tag, calls in ((EARLY, slots_early, "early", (("w_in", "w_uq", "w_ukv"),)),
                                     (LATE, slots_late, "late", (("w_out", "w_down"), ("w_up",)))):
        grads = _unpack_local(_sum_chips(slots, name="grad_sum_" + tag), shard, group)
        for names_c in calls:
            res = _adamw_multi([shard[n] for n in names_c], [a["m_" + n][0] for n in names_c],
                               [a["v_" + n][0] for n in names_c], [grads[n][0] for n in names_c],
                               name="adamw_" + "_".join(names_c))
            for kind in range(4):
                for n, r in zip(names_c, res[kind]):
                    big[kind][n] = r[None]

    packed = _pack_small(gs, name="pack_small_grads")
    tot = _sum_small(_all_gather(packed, name="gather_small_grads", in_vmem=True), name="sum_small_grads")
    loss_out = tot["loss"][0, 0]
    g_cw = lax.dynamic_slice_in_dim(tot["conv_w"], dev * 704, 704, axis=1)

    def rows_of(prefix):
        return [a[prefix + n].reshape(1, size) for n, size in SMALL]

    sml = _adamw_multi(rows_of("") + [conv_w[0]], rows_of("m_") + [m_conv_w[0]], rows_of("v_") + [v_conv_w[0]],
                       [tot[n] for n, _ in SMALL] + [g_cw], name="adamw_small")
    cwo = [kind[-1] for kind in sml]

    def small_of(kind, n):
        return sml[kind][[nm for nm, _ in SMALL].index(n)].reshape(a[n].shape)

    names = ['attn_norm_w', 'w_in', 'ret_gn_w', 'mla_q_norm_w', 'w_uq', 'mla_kv_norm_w', 'w_ukv', 'w_out',
             'ffn_norm_w', 'w_up', 'conv_w', 'conv_b', 'w_down', 'final_norm_w']
    outs = [loss_out, grad_x[None]]
    for kind in range(4):
        for n in names:
            if n == "conv_w":
                outs.append(cwo[kind][None])
            elif n in big[kind]:
                outs.append(big[kind][n])
            else:
                outs.append(small_of(kind, n))
    return tuple(outs)
```

```python
import functools

import numpy as np
import jax
import jax.numpy as jnp
from jax import lax
from jax.experimental import pallas as pl
from jax.experimental.pallas import tpu as pltpu

F32 = jnp.float32
BF16 = jnp.bfloat16
MESH = pl.DeviceIdType.MESH
ANY = pl.BlockSpec(memory_space=pl.ANY)

D_MODEL = 1024
RET_HEADS = 8
RET_HEAD_DIM = 64
RET_WIDTH = 512
RET_CHUNK = 128
MLA_HEADS = 8
MLA_NOPE = 64
MLA_ROPE = 32
MLA_V = 64
MLA_Q_RANK = 256
MLA_KV_RANK = 128
MLA_WIDTH = 512
IN_WIDTH = 2464
IN_PAD = 2560
D_FF = 2816
FF_HALF = 1408
ROPE_BASE = 10000.0
EPS = 1e-6
SCALE = float((MLA_NOPE + MLA_ROPE) ** -0.5)
K_SCALE = 0.125
N_DEV = 8

ADAM_LR = 0.001
ADAM_B1 = 0.9
ADAM_B2 = 0.999
ADAM_EPS = 1e-08
ADAM_WD = 0.01
ADAM_STEP = 10

VMEM_LIMIT = 56 * 1024 * 1024
MM_BUDGET = 40 * 1024 * 1024
NEG = -1e30
FLASH_UNROLL = 4
FLASH_BWD_UNROLL = 3

PACK_COLS = 1024
EARLY = ((("w_in", 308, 320, True), ("w_uq", 24, 32, True), ("w_ukv", 16, 16, True)), 384)
LATE = ((("w_out", 128, 128, False), ("w_up", 704, 704, True), ("w_down", 352, 352, False)), 1200)
BIG_NAMES = ("w_in", "w_uq", "w_ukv", "w_out", "w_up", "w_down")
SMALL = (("attn_norm_w", 1024), ("ret_gn_w", 512), ("mla_q_norm_w", 256), ("mla_kv_norm_w", 128),
         ("ffn_norm_w", 1024), ("conv_b", 5632), ("final_norm_w", 1024))
SMALL_VECTORS = SMALL + (("conv_w0", 5632), ("conv_w1", 5632), ("conv_w2", 5632), ("loss", 128))
SMALL_ROWS = 32


def _cp(sem=None, vmem=VMEM_LIMIT):
    return pltpu.CompilerParams(dimension_semantics=sem, vmem_limit_bytes=vmem)


def _dot(a, b):
    return jnp.dot(a, b, preferred_element_type=F32)


def _dot_nt(a, b):
    return lax.dot_general(a, b, (((1,), (1,)), ((), ())), preferred_element_type=F32)


def _dot_tn(a, b):
    return lax.dot_general(a, b, (((0,), (0,)), ((), ())), preferred_element_type=F32)


def _sigmoid(x):
    return 0.5 * jnp.tanh(0.5 * x) + 0.5


def _partner(x, half, period):
    n = x.shape[-1]
    lane = lax.broadcasted_iota(jnp.int32, x.shape, 1)
    return jnp.where((lane % period) < half, pltpu.roll(x, n - half, 1), pltpu.roll(x, half, 1))


def _rope(x, cos, ss, half, period):
    return x * cos + _partner(x, half, period) * ss


def _rope_t(dy, cos, ss, half, period):
    return dy * cos - _partner(dy, half, period) * ss


def _head_masks(shape):
    lane = lax.broadcasted_iota(jnp.int32, shape, 1)
    m0 = (lane < 64).astype(F32)
    return m0, 1.0 - m0


def _mm(a, b, *, name, add=None, out_dtype=F32, bt=False):
    parts = a if isinstance(a, tuple) else (a,)
    M = parts[0].shape[0]
    K = sum(p.shape[1] for p in parts)
    N = b.shape[0] if bt else b.shape[1]
    osz = jnp.dtype(out_dtype).itemsize
    per_row = 2 * (K * parts[0].dtype.itemsize + N * osz + (N * 4 if add is not None else 0))
    tm = 128
    for cand in (512, 256):
        if M % cand == 0 and cand * per_row + 4 * K * N <= MM_BUDGET:
            tm = cand
            break
    tm = min(tm, M)
    mul = _dot_nt if bt else _dot
    n_a = len(parts)
    n_in = n_a + (1 if add is None else 2)

    def body(*refs):
        av = refs[0][...] if n_a == 1 else jnp.concatenate([r[...] for r in refs[:n_a]], axis=1)
        acc = mul(av.astype(BF16), refs[n_a][...])
        if add is not None:
            acc = refs[n_a + 1][...] + acc
        refs[n_in][...] = acc.astype(out_dtype)

    in_specs = [pl.BlockSpec((tm, p.shape[1]), lambda i: (i, 0)) for p in parts]
    in_specs.append(pl.BlockSpec(b.shape, lambda i: (0, 0)))
    args = [*parts, b]
    if add is not None:
        in_specs.append(pl.BlockSpec((tm, N), lambda i: (i, 0)))
        args.append(add)
    return pl.pallas_call(
        body, name=name, grid=(M // tm,), in_specs=in_specs, out_specs=pl.BlockSpec((tm, N), lambda i: (i, 0)),
        out_shape=jax.ShapeDtypeStruct((M, N), out_dtype), compiler_params=_cp(("parallel",)))(*args)


def _mm_tn(a, b, *, name):
    parts = a if isinstance(a, tuple) else (a,)
    T = parts[0].shape[0]
    M = sum(p.shape[1] for p in parts)
    N = b.shape[1]
    tk = min(T, 512)

    def tile(n):
        for cand in (1408, 1280):
            if n > 1408 and n % cand == 0:
                return cand
        return n

    tm, tn = tile(M), tile(N)
    nk = T // tk
    n_a = len(parts)
    assert n_a == 1 or tm == M

    def body(*refs):
        o_ref = refs[n_a + 1]

        @pl.when(pl.program_id(2) == 0)
        def _():
            o_ref[...] = jnp.zeros_like(o_ref)
        av = refs[0][...] if n_a == 1 else jnp.concatenate([r[...] for r in refs[:n_a]], axis=1)
        o_ref[...] += _dot_tn(av.astype(BF16), refs[n_a][...].astype(BF16))

    if n_a == 1:
        a_specs = [pl.BlockSpec((tk, tm), lambda i, j, k: (k, i))]
    else:
        a_specs = [pl.BlockSpec((tk, p.shape[1]), lambda i, j, k: (k, 0)) for p in parts]
    return pl.pallas_call(
        body, name=name, grid=(M // tm, N // tn, nk),
        in_specs=a_specs + [pl.BlockSpec((tk, tn), lambda i, j, k: (k, j))],
        out_specs=pl.BlockSpec((tm, tn), lambda i, j, k: (i, j)),
        out_shape=jax.ShapeDtypeStruct((M, N), F32),
        compiler_params=_cp(("parallel", "parallel", "arbitrary")))(*parts, b)


def _rmsnorm_fwd(x, w, *, name, gather=None):
    T, D = x.shape
    tm = min(T, 1024)
    n = T // tm

    def body(x_ref, w_ref, *rest):
        if gather is not None:
            s_ref, o_ref, g_ref, *sems = rest
            start, forward, finish = _gather_phases(s_ref, g_ref, *sems)
            pl.when(pl.program_id(0) == 0)(start)
            pl.when(pl.program_id(0) == n // 2)(forward)
        else:
            o_ref, = rest
        xv = x_ref[...]
        r = lax.rsqrt(jnp.mean(xv * xv, axis=-1, keepdims=True) + EPS)
        o_ref[...] = (xv * r * w_ref[...]).astype(BF16)
        if gather is not None:
            pl.when(pl.program_id(0) == n - 1)(finish)

    in_specs = [pl.BlockSpec((tm, D), lambda i: (i, 0)), pl.BlockSpec((1, D), lambda i: (0, 0))]
    out_spec = pl.BlockSpec((tm, D), lambda i: (i, 0))
    out_shape = jax.ShapeDtypeStruct((T, D), BF16)
    if gather is None:
        return pl.pallas_call(body, name=name, grid=(n,), in_specs=in_specs, out_specs=out_spec, out_shape=out_shape,
                              compiler_params=_cp(("parallel",)))(x, w)
    return pl.pallas_call(
        body, name=name, grid=(n,), in_specs=in_specs + [ANY], out_specs=[out_spec, ANY],
        out_shape=[out_shape, jax.ShapeDtypeStruct((N_DEV,) + gather.shape, gather.dtype)],
        scratch_shapes=list(GATHER_SCRATCH), compiler_params=_cp(("arbitrary",)))(x, w, gather)


def _mm_norm_bwd(a, b, x, w, dres, *, name, exchange=None):
    T, K = a.shape
    D = b.shape[1]
    tm = min(T, 256 if K > 4096 else 512)
    n = T // tm

    def body(a_ref, b_ref, x_ref, w_ref, dr_ref, *rest):
        if exchange is None:
            dx_ref, dw_ref = rest
        else:
            p_ref, dx_ref, dw_ref, got_ref, *sems = rest
            start, finish = _exchange_phases(p_ref, got_ref, *sems)
            pl.when(pl.program_id(0) == 0)(start)

        @pl.when(pl.program_id(0) == 0)
        def _():
            dw_ref[...] = jnp.zeros_like(dw_ref)
        dh = _dot(a_ref[...], b_ref[...])
        xv = x_ref[...]
        r = lax.rsqrt(jnp.mean(xv * xv, axis=-1, keepdims=True) + EPS)
        xh = xv * r
        g = dh * w_ref[...]
        dx_ref[...] = dr_ref[...] + r * (g - xh * jnp.mean(g * xh, axis=-1, keepdims=True))
        dw_ref[...] += jnp.sum(dh * xh, axis=0, keepdims=True)
        if exchange is not None:
            pl.when(pl.program_id(0) == n - 1)(finish)

    row = pl.BlockSpec((tm, D), lambda i: (i, 0))
    vec = pl.BlockSpec((1, D), lambda i: (0, 0))
    in_specs = [pl.BlockSpec((tm, K), lambda i: (i, 0)), pl.BlockSpec((K, D), lambda i: (0, 0)), row, vec, row]
    out_shape = [jax.ShapeDtypeStruct((T, D), F32), jax.ShapeDtypeStruct((1, D), F32)]
    if exchange is None:
        return pl.pallas_call(body, name=name, grid=(n,), in_specs=in_specs, out_specs=[row, vec], out_shape=out_shape,
                              compiler_params=_cp(("arbitrary",)))(a, b, x, w, dres)
    return pl.pallas_call(
        body, name=name, grid=(n,), in_specs=in_specs + [ANY], out_specs=[row, vec, ANY],
        out_shape=out_shape + [jax.ShapeDtypeStruct(exchange.shape, exchange.dtype)],
        scratch_shapes=list(EXCHANGE_SCRATCH), compiler_params=_cp(("arbitrary",)))(a, b, x, w, dres, exchange)


def _down_proj_loss(a, w_down, x1, tgt, w, *, name):
    T, D = x1.shape
    K = a.shape[1]
    tm = min(T, 512)

    def body(a_ref, b_ref, x_ref, t_ref, w_ref, loss_ref, dx_ref, dxb_ref, dw_ref):
        @pl.when(pl.program_id(0) == 0)
        def _():
            dw_ref[...] = jnp.zeros_like(dw_ref)
            loss_ref[...] = jnp.zeros_like(loss_ref)
        xv = x_ref[...] + _dot(a_ref[...], b_ref[...])
        wv = w_ref[...]
        r = lax.rsqrt(jnp.mean(xv * xv, axis=-1, keepdims=True) + EPS)
        xh = xv * r
        e = xh * wv - t_ref[...]
        part = 0.5 * jnp.sum(jnp.mean(e * e, axis=-1, keepdims=True), axis=0, keepdims=True)
        loss_ref[...] += jnp.broadcast_to(part, loss_ref.shape)
        dy = e * (1.0 / D)
        g = dy * wv
        dx = r * (g - xh * jnp.mean(g * xh, axis=-1, keepdims=True))
        dx_ref[...] = dx
        dxb_ref[...] = dx.astype(BF16)
        dw_ref[...] += jnp.sum(dy * xh, axis=0, keepdims=True)

    row = pl.BlockSpec((tm, D), lambda i: (i, 0))
    vec = pl.BlockSpec((1, D), lambda i: (0, 0))
    return pl.pallas_call(
        body, name=name, grid=(T // tm,),
        in_specs=[pl.BlockSpec((tm, K), lambda i: (i, 0)), pl.BlockSpec((K, D), lambda i: (0, 0)), row, row, vec],
        out_specs=[pl.BlockSpec((1, 128), lambda i: (0, 0)), row, row, vec],
        out_shape=[jax.ShapeDtypeStruct((1, 128), F32), jax.ShapeDtypeStruct((T, D), F32),
                   jax.ShapeDtypeStruct((T, D), BF16), jax.ShapeDtypeStruct((1, D), F32)],
        compiler_params=_cp(("arbitrary",)))(a, w_down, x1, tgt, w)


def _ret_tables():
    C = RET_CHUNK
    h = jnp.arange(RET_HEADS, dtype=F32)
    log_gamma = jnp.log1p(-jnp.power(2.0, -5.0 - h))
    idx = jnp.arange(C, dtype=F32)
    diff = idx[:, None] - idx[None, :]
    dm = jnp.where(diff >= 0, jnp.exp(log_gamma[:, None, None] * jnp.maximum(diff, 0.0)), 0.0)
    dm = dm.reshape(4, 2 * C, C)
    lane_head = jnp.repeat(jnp.arange(RET_HEADS).reshape(4, 2), 64, axis=1)
    lg = log_gamma[lane_head]
    xi = jnp.exp(lg[:, None, :] * (idx[None, :, None] + 1.0))
    zeta = jnp.exp(lg[:, None, :] * (C - 1.0 - idx[None, :, None]))
    blk = (jnp.arange(128)[:, None] // 64) == (jnp.arange(128)[None, :] // 64)
    cd = jnp.where(blk[None], jnp.exp(lg * C)[:, :, None], 0.0)
    return dm.astype(F32), xi.astype(F32), zeta.astype(F32), cd.astype(F32)


def _ret_specs(tb, rev, nt, roped=False):
    def tmap(t):
        return (nt - 1 - t) if rev else t
    offsets = (0, 0, 8) if roped else (0, 4, 8)
    qkv = [pl.BlockSpec((tb, 128), lambda p, t, o=o: (tmap(t), o + p)) for o in offsets]
    rope = [pl.BlockSpec((tb, 128), lambda p, t: (tmap(t), 0))] * 2
    tabs = [pl.BlockSpec((None, 256, 128), lambda p, t: (p, 0, 0))] + \
           [pl.BlockSpec((None, 128, 128), lambda p, t: (p, 0, 0))] * 3
    return qkv, rope, tabs


def _ret_fwd(proj, cos, ss, tabs, gnw, *, name):
    T = proj.shape[0]
    tb = min(T, 1024)
    nt = T // tb
    nchunk = tb // RET_CHUNK

    def body(q_ref, k_ref, v_ref, g_ref, cos_ref, ss_ref, dm_ref, xi_ref, zt_ref, cd_ref, gnw_ref,
             y_ref, o_ref, qr_ref, kr_ref, r_sc):
        @pl.when(pl.program_id(1) == 0)
        def _():
            r_sc[...] = jnp.zeros_like(r_sc)
        m0, m1 = _head_masks((128, 128))
        dm, xi, zt, cd = dm_ref[...], xi_ref[...], zt_ref[...], cd_ref[...]
        bm = (cd > 0).astype(F32)
        gnw = gnw_ref[...]
        for c in range(nchunk):
            rs = pl.ds(c * RET_CHUNK, RET_CHUNK)
            cs, sn = cos_ref[rs, :], ss_ref[rs, :]
            q = _rope(q_ref[rs, :], cs, sn, 32, 64)
            k = _rope(k_ref[rs, :], cs, sn, 32, 64) * K_SCALE
            v = v_ref[rs, :]
            kb, vb = k.astype(BF16), v.astype(BF16)
            qr_ref[rs, :] = q.astype(BF16)
            kr_ref[rs, :] = kb
            qs = jnp.concatenate([q * m0, q * m1], axis=0).astype(BF16)
            s = (_dot_nt(qs, kb) * dm).astype(BF16)
            vs = jnp.concatenate([v * m0, v * m1], axis=0).astype(BF16)
            o = _dot(jnp.concatenate([s[:128], s[128:]], axis=1), vs)
            r = r_sc[...]
            o = o + _dot(q.astype(BF16), r.astype(BF16)) * xi
            r_sc[...] = cd * r + bm * _dot_tn((k * zt).astype(BF16), vb)
            mu = (jnp.sum(o * m0, axis=1, keepdims=True) * m0 + jnp.sum(o * m1, axis=1, keepdims=True) * m1) * (1.0 / 64)
            d = o - mu
            dd = d * d
            var = (jnp.sum(dd * m0, axis=1, keepdims=True) * m0 + jnp.sum(dd * m1, axis=1, keepdims=True) * m1) * (1.0 / 64)
            oh = d * lax.rsqrt(var + EPS)
            g = g_ref[rs, :]
            y_ref[rs, :] = (g * _sigmoid(g) * (oh * gnw)).astype(BF16)
            o_ref[rs, :] = o

    qkv, rope, tspec = _ret_specs(tb, False, nt)
    gspec = pl.BlockSpec((tb, 128), lambda p, t: (t, 12 + p))
    out = pl.BlockSpec((tb, 128), lambda p, t: (t, p))
    return pl.pallas_call(
        body, name=name, grid=(4, nt),
        in_specs=qkv + [gspec] + rope + tspec + [pl.BlockSpec((1, 128), lambda p, t: (0, p))],
        out_specs=[out, out, out, out],
        out_shape=[jax.ShapeDtypeStruct((T, RET_WIDTH), BF16), jax.ShapeDtypeStruct((T, RET_WIDTH), F32),
                   jax.ShapeDtypeStruct((T, RET_WIDTH), BF16), jax.ShapeDtypeStruct((T, RET_WIDTH), BF16)],
        scratch_shapes=[pltpu.VMEM((128, 128), F32)],
        compiler_params=_cp(("parallel", "arbitrary")))(proj, proj, proj, proj, cos, ss, *tabs, gnw)


def _ret_bwd_dq(qr, kr, proj, do, cos, ss, tabs, *, name):
    T = proj.shape[0]
    tb = min(T, 1024)
    nt = T // tb
    nchunk = tb // RET_CHUNK

    def body(q_ref, k_ref, v_ref, do_ref, cos_ref, ss_ref, dm_ref, xi_ref, zt_ref, cd_ref, dq_ref, r_sc):
        del q_ref
        @pl.when(pl.program_id(1) == 0)
        def _():
            r_sc[...] = jnp.zeros_like(r_sc)
        m0, m1 = _head_masks((128, 128))
        dm, xi, zt, cd = dm_ref[...], xi_ref[...], zt_ref[...], cd_ref[...]
        bm = (cd > 0).astype(F32)
        for c in range(nchunk):
            rs = pl.ds(c * RET_CHUNK, RET_CHUNK)
            cs, sn = cos_ref[rs, :], ss_ref[rs, :]
            k = k_ref[rs, :].astype(F32)
            vb = v_ref[rs, :].astype(BF16)
            dob = do_ref[rs, :]
            dof = dob.astype(F32)
            dos = jnp.concatenate([dof * m0, dof * m1], axis=0).astype(BF16)
            a = (_dot_nt(dos, vb) * dm).astype(BF16)
            ks = jnp.concatenate([k * m0, k * m1], axis=0).astype(BF16)
            r = r_sc[...]
            dq = _dot(jnp.concatenate([a[:128], a[128:]], axis=1), ks) + _dot_nt(dob, r.astype(BF16)) * xi
            r_sc[...] = cd * r + bm * _dot_tn((k * zt).astype(BF16), vb)
            dq_ref[rs, :] = _rope_t(dq, cs, sn, 32, 64).astype(BF16)

    qkv, rope, tspec = _ret_specs(tb, False, nt, roped=True)
    blk = pl.BlockSpec((tb, 128), lambda p, t: (t, p))
    return pl.pallas_call(
        body, name=name, grid=(4, nt), in_specs=qkv + [blk] + rope + tspec, out_specs=blk,
        out_shape=jax.ShapeDtypeStruct((T, RET_WIDTH), BF16),
        scratch_shapes=[pltpu.VMEM((128, 128), F32)],
        compiler_params=_cp(("parallel", "arbitrary")))(qr, kr, proj, do, cos, ss, *tabs)


def _ret_bwd_dkv(qr, kr, proj, do, cos, ss, tabs, *, name, swap=None):
    T = proj.shape[0]
    tb = min(T, 1024)
    nt = T // tb
    nchunk = tb // RET_CHUNK

    def body(q_ref, k_ref, v_ref, do_ref, cos_ref, ss_ref, dm_ref, xi_ref, zt_ref, cd_ref, *rest):
        if swap is None:
            backward(q_ref, k_ref, v_ref, do_ref, cos_ref, ss_ref, dm_ref, xi_ref, zt_ref, cd_ref, *rest)
        else:
            g_ref, dk_ref, dv_ref, got_ref, u_sc, *sems = rest
            start, finish = _swap_phases(g_ref, got_ref, *sems)
            pl.when((pl.program_id(0) == 0) & (pl.program_id(1) == 0))(start)
            backward(q_ref, k_ref, v_ref, do_ref, cos_ref, ss_ref, dm_ref, xi_ref, zt_ref, cd_ref, dk_ref, dv_ref, u_sc)
            pl.when((pl.program_id(0) == 3) & (pl.program_id(1) == nt - 1))(finish)

    def backward(q_ref, k_ref, v_ref, do_ref, cos_ref, ss_ref, dm_ref, xi_ref, zt_ref, cd_ref, dk_ref, dv_ref, u_sc):
        @pl.when(pl.program_id(1) == 0)
        def _():
            u_sc[...] = jnp.zeros_like(u_sc)
        m0, m1 = _head_masks((128, 128))
        dm, xi, zt, cd = dm_ref[...], xi_ref[...], zt_ref[...], cd_ref[...]
        bm = (cd > 0).astype(F32)
        for c in reversed(range(nchunk)):
            rs = pl.ds(c * RET_CHUNK, RET_CHUNK)
            cs, sn = cos_ref[rs, :], ss_ref[rs, :]
            kb = k_ref[rs, :]
            q = q_ref[rs, :].astype(F32)
            vb = v_ref[rs, :].astype(BF16)
            dob = do_ref[rs, :]
            dof = dob.astype(F32)
            qs = jnp.concatenate([q * m0, q * m1], axis=0).astype(BF16)
            dos = jnp.concatenate([dof * m0, dof * m1], axis=0).astype(BF16)
            s = (_dot_nt(qs, kb) * dm).astype(BF16)
            a = (_dot_nt(dos, vb) * dm).astype(BF16)
            ub = u_sc[...].astype(BF16)
            dk = _dot_tn(a, qs) + _dot_nt(vb, ub) * zt
            dv = _dot_tn(s, dos) + _dot(kb, ub) * zt
            u_sc[...] = cd * u_sc[...] + bm * _dot_tn((q * xi).astype(BF16), dob)
            dk_ref[rs, :] = (_rope_t(dk, cs, sn, 32, 64) * K_SCALE).astype(BF16)
            dv_ref[rs, :] = dv.astype(BF16)

    qkv, rope, tspec = _ret_specs(tb, True, nt, roped=True)
    blk = pl.BlockSpec((tb, 128), lambda p, t: (nt - 1 - t, p))
    out_shape = [jax.ShapeDtypeStruct((T, RET_WIDTH), BF16)] * 2
    if swap is None:
        return pl.pallas_call(
            body, name=name, grid=(4, nt), in_specs=qkv + [blk] + rope + tspec, out_specs=[blk, blk],
            out_shape=out_shape, scratch_shapes=[pltpu.VMEM((128, 128), F32)],
            compiler_params=_cp(("parallel", "arbitrary")))(qr, kr, proj, do, cos, ss, *tabs)
    return pl.pallas_call(
        body, name=name, grid=(4, nt), in_specs=qkv + [blk] + rope + tspec + [ANY], out_specs=[blk, blk, ANY],
        out_shape=out_shape + [jax.ShapeDtypeStruct((4,) + swap.shape[2:], swap.dtype)],
        scratch_shapes=[pltpu.VMEM((128, 128), F32)] + list(SWAP_SCRATCH),
        compiler_params=_cp(("arbitrary", "arbitrary")))(qr, kr, proj, do, cos, ss, *tabs, swap)


def _mix_bwd(dx1, w_out, o_ret, proj, y_mla, gnw, *, name):
    T = dx1.shape[0]
    tm = min(T, 512)

    def body(dx_ref, wo_ref, o_ref, g_ref, ym_ref, gnw_ref, do_ref, dg_ref, dom_ref, dl_ref, dw_ref, dm_ref):
        @pl.when(pl.program_id(0) == 0)
        def _():
            dw_ref[...] = jnp.zeros_like(dw_ref)
        dm_ref[...] = _dot_nt(dx_ref[...].astype(BF16), wo_ref[...])
        m0, m1 = _head_masks((tm, 128))
        lane = lax.broadcasted_iota(jnp.int32, (tm, 128), 1)
        delta = jnp.zeros((tm, 128), F32)

        def gsum(z):
            return jnp.sum(z * m0, axis=1, keepdims=True) * m0 + jnp.sum(z * m1, axis=1, keepdims=True) * m1

        for p in range(4):
            cs = slice(128 * p, 128 * p + 128)
            dy = dm_ref[:, cs]
            o = o_ref[:, cs]
            g = g_ref[:, cs]
            w = gnw_ref[:, cs]
            d = o - gsum(o) * (1.0 / 64)
            rstd = lax.rsqrt(gsum(d * d) * (1.0 / 64) + EPS)
            oh = d * rstd
            sg = _sigmoid(g)
            dn = dy * (g * sg)
            dg_ref[:, cs] = (dy * (oh * w) * (sg * (1.0 + g * (1.0 - sg)))).astype(BF16)
            dw_ref[:, cs] += jnp.sum(dn * oh, axis=0, keepdims=True)
            doh = dn * w
            do = rstd * (doh - gsum(doh) * (1.0 / 64) - oh * (gsum(doh * oh) * (1.0 / 64)))
            do_ref[:, cs] = do.astype(BF16)
            dom = dm_ref[:, 512 + 128 * p:512 + 128 * p + 128]
            dom_ref[:, cs] = dom.astype(BF16)
            pr = dom * ym_ref[:, cs].astype(F32)
            delta = jnp.where(lane == 2 * p, jnp.sum(pr * m0, axis=1, keepdims=True), delta)
            delta = jnp.where(lane == 2 * p + 1, jnp.sum(pr * m1, axis=1, keepdims=True), delta)
        dl_ref[...] = delta.T[0:MLA_HEADS]

    half = pl.BlockSpec((tm, 512), lambda i: (i, 0))
    return pl.pallas_call(
        body, name=name, grid=(T // tm,),
        in_specs=[pl.BlockSpec((tm, D_MODEL), lambda i: (i, 0)), pl.BlockSpec(w_out.shape, lambda i: (0, 0)), half,
                  pl.BlockSpec((tm, 512), lambda i: (i, 3)), half, pl.BlockSpec((1, 512), lambda i: (0, 0))],
        out_specs=[half, half, half, pl.BlockSpec((MLA_HEADS, tm), lambda i: (0, i)),
                   pl.BlockSpec((1, 512), lambda i: (0, 0))],
        out_shape=[jax.ShapeDtypeStruct((T, 512), BF16)] * 3 + [jax.ShapeDtypeStruct((MLA_HEADS, T), F32),
                                                                jax.ShapeDtypeStruct((1, 512), F32)],
        scratch_shapes=[pltpu.VMEM((tm, 1024), F32)],
        compiler_params=_cp(("arbitrary",)))(dx1, w_out, o_ret, proj, y_mla, gnw)


def _mla_prep_fwd(proj, qnw, kvnw, wuq, wk, wv, cos, ss, *, name):
    T = proj.shape[0]
    tm = min(T, 512)

    def body(lat_ref, qnw_ref, kvnw_ref, wuq_ref, wk_ref, wv_ref, cos_ref, ss_ref,
             q_ref, k_ref, v_ref, cqn_ref, ckvn_ref):
        cq = lat_ref[:, 0:256]
        ckv = lat_ref[:, 256:384]
        g3 = lat_ref[:, 384:512]
        cqn = (cq * lax.rsqrt(jnp.mean(cq * cq, axis=-1, keepdims=True) + EPS) * qnw_ref[...]).astype(BF16)
        ckvn = (ckv * lax.rsqrt(jnp.mean(ckv * ckv, axis=-1, keepdims=True) + EPS) * kvnw_ref[...]).astype(BF16)
        cqn_ref[...] = cqn
        ckvn_ref[...] = ckvn
        cs, sn = cos_ref[...], ss_ref[...]
        q = _dot_nt(cqn, wuq_ref[...])
        k = _dot_nt(ckvn, wk_ref[...])
        kpe = _rope(g3, cs, sn, 16, 32)
        for h in range(MLA_HEADS):
            hs = slice(128 * h, 128 * h + 128)
            q_ref[:, hs] = (_rope(q[:, hs], cs, sn, 16, 32) * SCALE).astype(BF16)
            k_ref[:, hs] = (k[:, hs] + kpe).astype(BF16)
        v = _dot_nt(ckvn, wv_ref[...])
        lane = lax.broadcasted_iota(jnp.int32, (tm, 128), 1)
        for p in range(4):
            vp = v[:, 128 * p:128 * p + 128]
            v_ref[:, 256 * p:256 * p + 128] = jnp.where(lane < 64, vp, 1.0).astype(BF16)
            v_ref[:, 256 * p + 128:256 * p + 256] = jnp.where(lane < 64, 1.0, vp).astype(BF16)

    def full(shape):
        return pl.BlockSpec(shape, lambda i: (0, 0))

    def row(w):
        return pl.BlockSpec((tm, w), lambda i: (i, 0))

    return pl.pallas_call(
        body, name=name, grid=(T // tm,),
        in_specs=[pl.BlockSpec((tm, 512), lambda i: (i, 4)), full((1, 256)), full((1, 128)), full((1024, 256)),
                  full((1024, 128)), full((512, 128)), row(128), row(128)],
        out_specs=[row(1024), row(1024), row(1024), row(256), row(128)],
        out_shape=[jax.ShapeDtypeStruct((T, 1024), BF16), jax.ShapeDtypeStruct((T, 1024), BF16),
                   jax.ShapeDtypeStruct((T, 1024), BF16), jax.ShapeDtypeStruct((T, 256), BF16),
                   jax.ShapeDtypeStruct((T, 128), BF16)],
        compiler_params=_cp(("parallel",)))(proj, qnw, kvnw, wuq, wk, wv, cos, ss)


def _mla_prep_bwd(dq, dk, dv, proj, qnw, kvnw, wuq_t, wk_t, wv_t, cos, ss, ret_grads, cqn, ckvn, h, *, name):
    T = proj.shape[0]
    tm = min(T, 256)

    def body(dq_ref, dk_ref, dv_ref, lat_ref, qnw_ref, kvnw_ref, wuq_ref, wk_ref, wv_ref, cos_ref, ss_ref,
             rq_ref, rk_ref, rv_ref, rg_ref, cqn_ref, ckvn_ref, h_ref,
             dproj_ref, gwin_ref, gwuq_ref, gwk_ref, gwv_ref, dqnw_ref, dkvnw_ref, dqp_ref):
        for j, r in enumerate((rq_ref, rk_ref, rv_ref, rg_ref)):
            dproj_ref[:, 512 * j:512 * j + 512] = r[...]
        dlat_ref = dproj_ref.at[:, 2048:2560]

        @pl.when(pl.program_id(0) == 0)
        def _():
            for r in (gwin_ref, gwuq_ref, gwk_ref, gwv_ref, dqnw_ref, dkvnw_ref):
                r[...] = jnp.zeros_like(r)
        cs, sn = cos_ref[...], ss_ref[...]
        dkpe = jnp.zeros((tm, 128), F32)
        for h in range(MLA_HEADS):
            hs = slice(128 * h, 128 * h + 128)
            dqp_ref[:, hs] = _rope_t(dq_ref[:, hs] * SCALE, cs, sn, 16, 32).astype(BF16)
            dkpe = dkpe + dk_ref[:, hs]
        lane = lax.broadcasted_iota(jnp.int32, (tm, 128), 1)
        rope_lane = (lane >= MLA_NOPE) & (lane < MLA_NOPE + MLA_ROPE)
        dg3 = jnp.where(rope_lane, _rope_t(jnp.where(rope_lane, dkpe, 0.0), cs, sn, 16, 32), 0.0)

        def norm_bwd(x, w, dn):
            r = lax.rsqrt(jnp.mean(x * x, axis=-1, keepdims=True) + EPS)
            xh = x * r
            g = dn * w
            return r * (g - xh * jnp.mean(g * xh, axis=-1, keepdims=True)), jnp.sum(dn * xh, axis=0, keepdims=True)

        dqp = dqp_ref[...]
        dkb = dk_ref[...].astype(BF16)
        dvb = dv_ref[...]
        dcqn = _dot(dqp, wuq_ref[...])
        dcq, dqnw = norm_bwd(lat_ref[:, 0:256], qnw_ref[...], dcqn)
        dckvn = _dot(dkb, wk_ref[...]) + _dot(dvb, wv_ref[...])
        dckv, dkvnw = norm_bwd(lat_ref[:, 256:384], kvnw_ref[...], dckvn)
        gwuq_ref[...] += _dot_tn(dqp, cqn_ref[...])
        gwk_ref[...] += _dot_tn(dkb, ckvn_ref[...])
        gwv_ref[...] += _dot_tn(dvb, ckvn_ref[...])
        dqnw_ref[...] += dqnw
        dkvnw_ref[...] += dkvnw
        dlat_ref[:, 0:256] = dcq.astype(BF16)
        dlat_ref[:, 256:384] = dckv.astype(BF16)
        dlat_ref[:, 384:512] = dg3.astype(BF16)
        gwin_ref[...] += _dot_tn(dproj_ref[...], h_ref[...])

    def full(shape):
        return pl.BlockSpec(shape, lambda i: (0, 0))

    def row(w):
        return pl.BlockSpec((tm, w), lambda i: (i, 0))

    return pl.pallas_call(
        body, name=name, grid=(T // tm,),
        in_specs=[row(1024), row(1024), row(512), pl.BlockSpec((tm, 512), lambda i: (i, 4)), full((1, 256)),
                  full((1, 128)), full((1024, 256)), full((1024, 128)), full((512, 128)), row(128), row(128)]
                 + [row(512)] * 4 + [row(256), row(128), row(D_MODEL)],
        out_specs=[row(IN_PAD), full((IN_PAD, D_MODEL)), full((1024, 256)), full((1024, 128)), full((512, 128)),
                   full((1, 256)), full((1, 128))],
        out_shape=[jax.ShapeDtypeStruct((T, IN_PAD), BF16), jax.ShapeDtypeStruct((IN_PAD, D_MODEL), F32),
                   jax.ShapeDtypeStruct((1024, 256), F32), jax.ShapeDtypeStruct((1024, 128), F32),
                   jax.ShapeDtypeStruct((512, 128), F32), jax.ShapeDtypeStruct((1, 256), F32),
                   jax.ShapeDtypeStruct((1, 128), F32)],
        scratch_shapes=[pltpu.VMEM((tm, 1024), BF16)],
        compiler_params=_cp(("arbitrary",)))(dq, dk, dv, proj, qnw, kvnw, wuq_t, wk_t, wv_t, cos, ss, *ret_grads,
                                             cqn, ckvn, h)


def _flash_fwd(q, k, v1, *, name, gather=None):
    T = q.shape[0]
    tq = min(T, 512)
    tk = tq
    nq = T // tq

    def body(q_ref, k_ref, v_ref, *rest):
        if gather is None:
            y_ref, lse_ref = rest
        else:
            x_ref, y_ref, lse_ref, g_ref, *sems = rest
            start, forward, finish = _gather_phases(x_ref, g_ref, *sems)
            pl.when((pl.program_id(0) == 0) & (pl.program_id(1) == 0))(start)
            pl.when((pl.program_id(0) == 1) & (pl.program_id(1) == 0))(forward)
        attend(q_ref, k_ref, v_ref, y_ref, lse_ref)
        if gather is not None:
            pl.when((pl.program_id(0) == 3) & (pl.program_id(1) == nq - 1))(finish)

    def attend(q_ref, k_ref, v_ref, y_ref, lse_ref):
        qi = pl.program_id(1)
        row = lax.broadcasted_iota(jnp.int32, (tq, tk), 0)
        col = lax.broadcasted_iota(jnp.int32, (tq, tk), 1)

        def step(kb, carry, masked):
            ks = pl.ds(pl.multiple_of(kb * tk, tk), tk)
            new = []
            for h in range(2):
                hs = slice(128 * h, 128 * h + 128)
                m, acc = carry[h]
                s = _dot_nt(q_ref[:, hs], k_ref[ks, hs])
                if masked:
                    s = jnp.where(col <= row, s, NEG)
                mn = jnp.maximum(m, jnp.max(s, axis=1, keepdims=True))
                p = jnp.exp((s - mn).astype(BF16))
                acc = jnp.exp(m - mn) * acc + _dot(p, v_ref[ks, hs])
                new.append((mn, acc))
            return tuple(new)

        def unrolled(j, c):
            for u in range(FLASH_UNROLL):
                c = step(FLASH_UNROLL * j + u, c, False)
            return c

        init = (jnp.full((tq, 1), NEG, F32), jnp.zeros((tq, 128), F32))
        carry = lax.fori_loop(0, qi // FLASH_UNROLL, unrolled, (init, init))
        carry = lax.fori_loop(FLASH_UNROLL * (qi // FLASH_UNROLL), qi, lambda kb, c: step(kb, c, False), carry)
        (ma, acca), (mb, accb) = step(qi, carry, True)
        lane = lax.broadcasted_iota(jnp.int32, (tq, 128), 1)
        la, lb = pltpu.roll(acca, 64, 1), pltpu.roll(accb, 64, 1)
        y_ref[...] = jnp.where(lane < 64, acca / la, accb / lb).astype(BF16)
        lse_ref[0, 0] = jnp.broadcast_to(ma + jnp.log(acca[:, 64:65]), (tq, 128)).T[0:1]
        lse_ref[1, 0] = jnp.broadcast_to(mb + jnp.log(accb[:, 0:1]), (tq, 128)).T[0:1]

    in_specs = [pl.BlockSpec((tq, 256), lambda p, i: (i, p)), pl.BlockSpec((T, 256), lambda p, i: (0, p)),
                pl.BlockSpec((T, 256), lambda p, i: (0, p))]
    out_specs = [pl.BlockSpec((tq, 128), lambda p, i: (i, p)), pl.BlockSpec((2, 1, 1, tq), lambda p, i: (p, i, 0, 0))]
    out_shape = [jax.ShapeDtypeStruct((T, MLA_WIDTH), BF16), jax.ShapeDtypeStruct((MLA_HEADS, nq, 1, tq), F32)]
    if gather is None:
        return pl.pallas_call(body, name=name, grid=(4, nq), in_specs=in_specs, out_specs=out_specs,
                              out_shape=out_shape, compiler_params=_cp(("parallel", "arbitrary")))(q, k, v1)
    return pl.pallas_call(
        body, name=name, grid=(4, nq), in_specs=in_specs + [ANY], out_specs=out_specs + [ANY],
        out_shape=out_shape + [jax.ShapeDtypeStruct((N_DEV,) + gather.shape, gather.dtype)],
        scratch_shapes=list(GATHER_SCRATCH),
        compiler_params=_cp(("arbitrary", "arbitrary")))(q, k, v1, gather)


def _flash_bwd(q, k, v, do, lse, delta, *, name, exchange=None):
    T = q.shape[0]
    tq = min(T, 512)
    tk = tq
    nq = T // tq

    def body(q_ref, k_ref, v_ref, do_ref, lse_ref, dl_ref, *rest):
        if exchange is None:
            backward(q_ref, k_ref, v_ref, do_ref, lse_ref, dl_ref, *rest)
        else:
            p_ref, dqt_ref, dk_ref, dv_ref, got_ref, *sems = rest
            start, finish = _exchange_phases(p_ref, got_ref, *sems)
            pl.when((pl.program_id(0) == 0) & (pl.program_id(1) == 0))(start)
            backward(q_ref, k_ref, v_ref, do_ref, lse_ref, dl_ref, dqt_ref, dk_ref, dv_ref)
            pl.when((pl.program_id(0) == 3) & (pl.program_id(1) == nq - 1))(finish)

    def backward(q_ref, k_ref, v_ref, do_ref, lse_ref, dl_ref, dqt_ref, dk_ref, dv_ref):
        kb = pl.program_id(1)

        @pl.when(kb == 0)
        def _():
            dqt_ref[...] = jnp.zeros_like(dqt_ref)
        krow = lax.broadcasted_iota(jnp.int32, (tk, tq), 0)
        qcol = lax.broadcasted_iota(jnp.int32, (tk, tq), 1)
        masks = _head_masks((tk, 128))
        vms = [(v_ref[:, 128 * h:128 * h + 128].astype(F32) * masks[h]).astype(BF16) for h in range(2)]

        def step(qi, carry, masked):
            qs = pl.ds(pl.multiple_of(qi * tq, tq), tq)
            dob = do_ref[qs, :]
            dof = dob.astype(F32)
            dks, dv_acc = list(carry[:2]), carry[2]
            for h in range(2):
                hs = slice(128 * h, 128 * h + 128)
                kh = k_ref[:, hs]
                qh = q_ref[qs, hs]
                st = _dot_nt(kh, qh)
                pt = jnp.exp((st - lse_ref[h, qi]).astype(BF16))
                if masked:
                    pt = jnp.where(krow <= qcol, pt, jnp.zeros_like(pt))
                dv_acc = dv_acc + _dot(pt, (dof * masks[h]).astype(BF16))
                dpt = _dot_nt(vms[h], dob)
                dst = pt * (dpt - dl_ref[h, qi]).astype(BF16)
                dks[h] = dks[h] + _dot(dst, qh)
                dqt_ref[qi, hs, :] += _dot_tn(kh, dst)
            return dks[0], dks[1], dv_acc

        zero = jnp.zeros((tk, 128), F32)
        carry = step(kb, (zero, zero, zero), True)

        def unrolled(j, c):
            for u in range(FLASH_BWD_UNROLL):
                c = step(kb + 1 + FLASH_BWD_UNROLL * j + u, c, False)
            return c

        trips = (nq - 1 - kb) // FLASH_BWD_UNROLL
        carry = lax.fori_loop(0, trips, unrolled, carry)
        dk0, dk1, dv_acc = lax.fori_loop(kb + 1 + FLASH_BWD_UNROLL * trips, nq, lambda qi, c: step(qi, c, False), carry)
        dk_ref[:, 0:128] = dk0
        dk_ref[:, 128:256] = dk1
        dv_ref[...] = dv_acc.astype(BF16)

    stat = pl.BlockSpec((2, nq, 1, tq), lambda p, j: (p, 0, 0, 0))
    in_specs = [pl.BlockSpec((T, 256), lambda p, j: (0, p)), pl.BlockSpec((tk, 256), lambda p, j: (j, p)),
                pl.BlockSpec((tk, 256), lambda p, j: (j, p)), pl.BlockSpec((T, 128), lambda p, j: (0, p)), stat, stat]
    out_specs = [pl.BlockSpec((None, nq, 256, tq), lambda p, j: (p, 0, 0, 0)),
                 pl.BlockSpec((tk, 256), lambda p, j: (j, p)), pl.BlockSpec((tk, 128), lambda p, j: (j, p))]
    out_shape = [jax.ShapeDtypeStruct((4, nq, 256, tq), F32), jax.ShapeDtypeStruct((T, 1024), F32),
                 jax.ShapeDtypeStruct((T, MLA_WIDTH), BF16)]
    if exchange is None:
        return pl.pallas_call(body, name=name, grid=(4, nq), in_specs=in_specs, out_specs=out_specs,
                              out_shape=out_shape,
                              compiler_params=_cp(("parallel", "arbitrary")))(q, k, v, do, lse, delta)
    return pl.pallas_call(
        body, name=name, grid=(4, nq), in_specs=in_specs + [ANY], out_specs=out_specs + [ANY],
        out_shape=out_shape + [jax.ShapeDtypeStruct(exchange.shape, exchange.dtype)],
        scratch_shapes=list(EXCHANGE_SCRATCH),
        compiler_params=_cp(("arbitrary", "arbitrary")))(q, k, v, do, lse, delta, exchange)


def _shift_down(x, n, prev8):
    r = pltpu.roll(x, n, 0)
    row = lax.broadcasted_iota(jnp.int32, prev8.shape, 0)
    first = jnp.where(row < n, pltpu.roll(prev8, n, 0), r[:8])
    if x.shape[0] == 8:
        return first
    return jnp.concatenate([first, r[8:]], axis=0)


def _shift_up(x, n, next8):
    tm = x.shape[0]
    r = pltpu.roll(x, tm - n, 0)
    row = lax.broadcasted_iota(jnp.int32, next8.shape, 0)
    last = jnp.where(row >= 8 - n, pltpu.roll(next8, 8 - n, 0), r[tm - 8:])
    return jnp.concatenate([r[:tm - 8], last], axis=0)


def _conv_pre(u, prev8, cw_ref, cb_ref):
    p1 = _shift_down(u, 1, prev8)
    p2 = _shift_down(u, 2, prev8)
    up = cb_ref[...] + cw_ref[0:1, :] * p2 + cw_ref[1:2, :] * p1 + cw_ref[2:3, :] * u
    return up, p1, p2


def _up_proj_conv(x, y_ret, y_mla, w_out, nw, w_up_t, cw, cb, *, name):
    T, K = x.shape
    tm = min(T, 256)

    def body(x_ref, yr_ref, ym_ref, wo_ref, nw_ref, w_ref, cw_ref, cb_ref, x1_ref, h_ref, u_ref, a_ref, carry_sc):
        @pl.when(pl.program_id(0) == 0)
        def _():
            carry_sc[...] = jnp.zeros_like(carry_sc)
        xv = x_ref[...] + _dot(jnp.concatenate([yr_ref[...], ym_ref[...]], axis=1), wo_ref[...])
        x1_ref[...] = xv
        h = (xv * lax.rsqrt(jnp.mean(xv * xv, axis=-1, keepdims=True) + EPS) * nw_ref[...]).astype(BF16)
        h_ref[...] = h
        for blk in range(2):
            ups = []
            for half in range(2):
                cs = slice((2 * blk + half) * FF_HALF, (2 * blk + half + 1) * FF_HALF)
                u = _dot_nt(h, w_ref[cs, :])
                u_ref[:, cs] = u
                prev = carry_sc[:, cs]
                ups.append(cb_ref[:, cs] + cw_ref[0:1, cs] * _shift_down(u, 2, prev)
                           + cw_ref[1:2, cs] * _shift_down(u, 1, prev) + cw_ref[2:3, cs] * u)
                carry_sc[:, cs] = u[tm - 8:]
            gate, val = ups
            a_ref[:, blk * FF_HALF:(blk + 1) * FF_HALF] = (gate * _sigmoid(gate) * val).astype(BF16)

    def full(shape):
        return pl.BlockSpec(shape, lambda i: (0, 0))

    return pl.pallas_call(
        body, name=name, grid=(T // tm,),
        in_specs=[pl.BlockSpec((tm, K), lambda i: (i, 0)), pl.BlockSpec((tm, RET_WIDTH), lambda i: (i, 0)),
                  pl.BlockSpec((tm, MLA_WIDTH), lambda i: (i, 0)), full(w_out.shape), full(nw.shape),
                  full(w_up_t.shape), full(cw.shape), full(cb.shape)],
        out_specs=[pl.BlockSpec((tm, K), lambda i: (i, 0)), pl.BlockSpec((tm, K), lambda i: (i, 0)),
                   pl.BlockSpec((tm, 2 * D_FF), lambda i: (i, 0)), pl.BlockSpec((tm, D_FF), lambda i: (i, 0))],
        out_shape=[jax.ShapeDtypeStruct((T, K), F32), jax.ShapeDtypeStruct((T, K), BF16),
                   jax.ShapeDtypeStruct((T, 2 * D_FF), F32), jax.ShapeDtypeStruct((T, D_FF), BF16)],
        scratch_shapes=[pltpu.VMEM((8, 2 * D_FF), F32)],
        compiler_params=_cp(("arbitrary",)))(x, y_ret, y_mla, w_out, nw, w_up_t, cw, cb)


def _conv_bwd(u, da, cw, cb, *, name):
    T = u.shape[0]
    tm = min(T, 512)
    W = 2 * FF_HALF
    nt = T // tm

    def body(u_ref, prev_ref, next_ref, da_ref, dan_ref, cw_ref, cb_ref, du_ref, dw0_ref, dw1_ref, dw2_ref, db_ref):
        i = pl.program_id(1)

        @pl.when(i == 0)
        def _():
            for r in (dw0_ref, dw1_ref, dw2_ref, db_ref):
                r[...] = jnp.zeros_like(r)

        def dpre(u, prev8, da):
            up, p1, p2 = _conv_pre(u, prev8, cw_ref, cb_ref)
            gate, val = up[:, :FF_HALF], up[:, FF_HALF:]
            sg = _sigmoid(gate)
            dgate = da * val * (sg * (1.0 + gate * (1.0 - sg)))
            dval = da * (gate * sg)
            return jnp.concatenate([dgate, dval], axis=1), p1, p2

        u = u_ref[...]
        prev = jnp.where(i > 0, prev_ref[...], 0.0)
        dup, p1, p2 = dpre(u, prev, da_ref[...])
        dupn, _, _ = dpre(next_ref[...], u[tm - 8:], dan_ref[...])
        dupn = jnp.where(i < nt - 1, dupn, 0.0)
        du = cw_ref[2:3, :] * dup + cw_ref[1:2, :] * _shift_up(dup, 1, dupn) + cw_ref[0:1, :] * _shift_up(dup, 2, dupn)
        du_ref[...] = du.astype(BF16)
        dw0_ref[...] += jnp.sum(dup * p2, axis=0, keepdims=True)
        dw1_ref[...] += jnp.sum(dup * p1, axis=0, keepdims=True)
        dw2_ref[...] += jnp.sum(dup * u, axis=0, keepdims=True)
        db_ref[...] += jnp.sum(dup, axis=0, keepdims=True)

    nxt = lambda j, i: (jnp.minimum((i + 1) * (tm // 8), T // 8 - 1), j)
    vec = pl.BlockSpec((1, W), lambda j, i: (0, j))
    return pl.pallas_call(
        body, name=name, grid=(2, nt),
        in_specs=[pl.BlockSpec((tm, W), lambda j, i: (i, j)),
                  pl.BlockSpec((8, W), lambda j, i: (jnp.maximum(i * (tm // 8) - 1, 0), j)),
                  pl.BlockSpec((8, W), nxt),
                  pl.BlockSpec((tm, FF_HALF), lambda j, i: (i, j)), pl.BlockSpec((8, FF_HALF), nxt),
                  pl.BlockSpec((3, W), lambda j, i: (0, j)), vec],
        out_specs=[pl.BlockSpec((tm, W), lambda j, i: (i, j)), vec, vec, vec, vec],
        out_shape=[jax.ShapeDtypeStruct((T, 2 * D_FF), BF16)] + [jax.ShapeDtypeStruct((1, 2 * D_FF), F32)] * 4,
        compiler_params=_cp(("parallel", "arbitrary")))(u, u, u, da, da, cw, cb)


def _sum_chips(slots, *, name):
    ns, R, C = slots.shape
    tr = _row_tile(R)

    def body(g_ref, o_ref):
        g = g_ref[0].astype(F32)
        for s in range(1, ns):
            g = g + g_ref[s].astype(F32)
        o_ref[...] = g

    return pl.pallas_call(
        body, name=name, grid=(R // tr,), in_specs=[pl.BlockSpec((ns, tr, C), lambda i: (0, i, 0))],
        out_specs=pl.BlockSpec((tr, C), lambda i: (i, 0)), out_shape=jax.ShapeDtypeStruct((R, C), F32),
        compiler_params=_cp(("parallel",)))(slots)


def _place():
    return lax.axis_index("x"), lax.axis_index("y"), lax.axis_index("c")


GATHER_SCRATCH = (pltpu.SemaphoreType.DMA((7,)), pltpu.SemaphoreType.DMA((7,)), pltpu.SemaphoreType.DMA)
EXCHANGE_SCRATCH = (pltpu.SemaphoreType.DMA((3,)), pltpu.SemaphoreType.DMA((3,)), pltpu.SemaphoreType.DMA)


def _gather_phases(x_ref, out_ref, send_sems, recv_sems, local_sem):
    x_, y_, c_ = _place()
    me, sibling = (x_, y_, c_), (x_, y_, 1 - c_)
    chips = [(1 - x_, y_), (x_, 1 - y_), (1 - x_, 1 - y_)]

    def slot(px, py, pc):
        return out_ref.at[4 * px + 2 * py + pc]

    def copy(k, block, to, src=None):
        return pltpu.make_async_remote_copy(
            src_ref=slot(*block) if src is None else src, dst_ref=slot(*block),
            send_sem=send_sems.at[k], recv_sem=recv_sems.at[k], device_id=to, device_id_type=MESH)

    def mine():
        return pltpu.make_async_copy(x_ref, slot(*me), local_sem)

    def first():
        return [copy(0, me, sibling, src=x_ref)] + [copy(1 + j, me, (*chip, c_), src=x_ref)
                                                     for j, chip in enumerate(chips)]

    def passed():
        return [copy(4 + j, (*chip, c_), sibling) for j, chip in enumerate(chips)]

    def start():
        mine().start()
        for cp in first():
            cp.start()

    def forward():
        fwd = passed()
        for j, chip in enumerate(chips):
            copy(1 + j, (*chip, c_), me).wait_recv()
            fwd[j].start()

    def finish():
        copy(0, sibling, me).wait_recv()
        for j, chip in enumerate(chips):
            copy(4 + j, (*chip, 1 - c_), me).wait_recv()
        for cp in first() + passed():
            cp.wait_send()
        mine().wait()

    return start, forward, finish


def _exchange_phases(p_ref, out_ref, send_sems, recv_sems, local_sem):
    x_, y_, c_ = _place()
    me_k = 2 * x_ + y_
    chips = [(1 - x_, y_), (x_, 1 - y_), (1 - x_, 1 - y_)]

    def local():
        return pltpu.make_async_copy(p_ref.at[me_k], out_ref.at[me_k], local_sem)

    def copy(j, src_k, dst_k, chip):
        return pltpu.make_async_remote_copy(
            src_ref=p_ref.at[src_k], dst_ref=out_ref.at[dst_k], send_sem=send_sems.at[j],
            recv_sem=recv_sems.at[j], device_id=(*chip, c_), device_id_type=MESH)

    def sends():
        return [copy(j, 2 * px + py, me_k, (px, py)) for j, (px, py) in enumerate(chips)]

    def start():
        local().start()
        for cp in sends():
            cp.start()

    def finish():
        for j, (px, py) in enumerate(chips):
            copy(j, me_k, 2 * px + py, (px, py)).wait_recv()
        for cp in sends():
            cp.wait_send()
        local().wait()

    return start, finish


def _all_gather(x, *, name, in_vmem):
    def body(x_ref, out_ref, send_sems, recv_sems, local_sem):
        for phase in _gather_phases(x_ref, out_ref, send_sems, recv_sems, local_sem):
            phase()

    spec = pl.BlockSpec(memory_space=pltpu.VMEM) if in_vmem else ANY
    return pl.pallas_call(
        body, name=name, out_shape=jax.ShapeDtypeStruct((N_DEV,) + x.shape, x.dtype),
        in_specs=[spec], out_specs=spec, scratch_shapes=list(GATHER_SCRATCH),
        compiler_params=pltpu.CompilerParams(vmem_limit_bytes=VMEM_LIMIT))(x)


def _small_rows():
    table, row = [], 0
    for n, size in SMALL_VECTORS:
        table.append((n, size, row))
        row += -(-size // PACK_COLS)
    return table


def _ff_chunk_source(c):
    block, off = divmod(c * 128, FF_HALF)
    return (0, 2, 1, 3)[block] * FF_HALF + off


def _pack_small(parts, *, name):
    table = _small_rows()

    def body(*refs):
        out = refs[-1]
        out[...] = jnp.zeros_like(out)
        for ref, (n, size, row) in zip(refs, table):
            if size != 2 * D_FF:
                out[row:row + 1, 0:size] = ref[...]
                continue
            for c in range(size // 128):
                src = _ff_chunk_source(c)
                r, lane = divmod(c * 128, PACK_COLS)
                out[row + r:row + r + 1, lane:lane + 128] = ref[:, src:src + 128]

    return pl.pallas_call(body, name=name, out_shape=jax.ShapeDtypeStruct((SMALL_ROWS, PACK_COLS), F32))(
        *[parts[n] for n, _, _ in table])


def _sum_small(g, *, name):
    table = _small_rows()
    shapes = [(n, size) for n, size, _ in table if not n.startswith("conv_w")]
    shapes.insert(7, ("conv_w", 2 * D_FF))

    def body(g_ref, *outs):
        def total(row, width):
            acc = g_ref[0, row:row + 1, 0:width]
            for d in range(1, N_DEV):
                acc = acc + g_ref[d, row:row + 1, 0:width]
            return acc

        out_of = {n: o for (n, _), o in zip(shapes, outs)}
        for n, size, row in table:
            o, j = (out_of["conv_w"], int(n[-1])) if n.startswith("conv_w") else (out_of[n], 0)
            for i in range(-(-size // PACK_COLS)):
                width = min(PACK_COLS, size - PACK_COLS * i)
                o[j:j + 1, PACK_COLS * i:PACK_COLS * i + width] = total(row + i, width)

    out_shape = [jax.ShapeDtypeStruct((3 if n == "conv_w" else 1, size), F32) for n, size in shapes]
    res = pl.pallas_call(body, name=name, out_shape=out_shape)(g)
    return {n: r for (n, _), r in zip(shapes, res)}


def _adamw_multi(ws, ms, vs, gs, *, name):
    k = len(ws)

    def body(*refs):
        w_refs, m_refs, v_refs, g_refs = (refs[i * k:(i + 1) * k] for i in range(4))
        outs = refs[4 * k:]
        for i in range(k):
            g = g_refs[i][...]
            mn = ADAM_B1 * m_refs[i][...] + (1.0 - ADAM_B1) * g
            vn = ADAM_B2 * v_refs[i][...] + (1.0 - ADAM_B2) * (g * g)
            m_hat = mn / (1.0 - ADAM_B1 ** ADAM_STEP)
            v_hat = vn / (1.0 - ADAM_B2 ** ADAM_STEP)
            outs[i][...] = g
            outs[k + i][...] = -ADAM_LR * (m_hat / (jnp.sqrt(v_hat) + ADAM_EPS) + ADAM_WD * w_refs[i][...])
            outs[2 * k + i][...] = mn
            outs[3 * k + i][...] = vn

    out_shape = [jax.ShapeDtypeStruct(w.shape, F32) for _ in range(4) for w in ws]
    res = pl.pallas_call(body, name=name, out_shape=out_shape, compiler_params=_cp())(*ws, *ms, *vs, *gs)
    return [res[i * k:(i + 1) * k] for i in range(4)]


SWAP_SCRATCH = (pltpu.SemaphoreType.DMA((4,)), pltpu.SemaphoreType.DMA((4,)))


def _swap_phases(g_ref, out_ref, send_sems, recv_sems):
    x_, y_, c_ = _place()

    def copies():
        return [pltpu.make_async_remote_copy(src_ref=g_ref.at[k, 1 - c_], dst_ref=out_ref.at[k],
                                             send_sem=send_sems.at[k], recv_sem=recv_sems.at[k],
                                             device_id=(x_, y_, 1 - c_), device_id_type=MESH) for k in range(4)]

    def start():
        for cp in copies():
            cp.start()

    def finish():
        for cp in copies():
            cp.wait()

    return start, finish


def _swap_sibling(g, *, name):
    def body(g_ref, out_ref, send_sems, recv_sems):
        for phase in _swap_phases(g_ref, out_ref, send_sems, recv_sems):
            phase()

    return pl.pallas_call(
        body, name=name, out_shape=jax.ShapeDtypeStruct((4,) + g.shape[2:], g.dtype), in_specs=[ANY], out_specs=ANY,
        scratch_shapes=list(SWAP_SCRATCH))(g)


def _row_tile(R):
    for cand in (256, 400, 200):
        if R % cand == 0:
            return cand
    return R


def _add_own(g, b, *, name, out_dtype):
    n, _, R, C = g.shape
    tr = _row_tile(R)

    def body(c_ref, g_ref, b_ref, o_ref):
        del c_ref
        o_ref[...] = (g_ref[...] + b_ref[...]).astype(out_dtype)

    blk = pl.BlockSpec((None, tr, C), lambda s, i, c: (s, i, 0))
    grid_spec = pltpu.PrefetchScalarGridSpec(
        num_scalar_prefetch=1, grid=(n, R // tr),
        in_specs=[pl.BlockSpec((None, None, tr, C), lambda s, i, c: (s, c[0], i, 0)), blk], out_specs=blk)
    core = jnp.reshape(lax.axis_index("c"), (1,)).astype(jnp.int32)
    return pl.pallas_call(body, name=name, grid_spec=grid_spec, out_shape=jax.ShapeDtypeStruct(b.shape, out_dtype),
                          compiler_params=_cp(("parallel", "parallel")))(core, g, b)


def _pack_local(parts, group, tail=None):
    table, rows = group
    segs = []
    for n, r, rp, tr in table:
        w = parts[n].T if tr else parts[n]
        segs.append(jnp.pad(w.reshape(r, PACK_COLS), ((0, rp - r), (0, 0))))
    spare = rows - sum(rp for _, _, rp, _ in table)
    segs.append(jnp.zeros((spare, PACK_COLS), segs[0].dtype) if tail is None else tail)
    return jnp.concatenate(segs, axis=0)


CONV_W_BITS = 2 * 3 * 704
SPARE_EARLY = 16


def _conv_w_as_rows(conv_w_shard):
    bits = lax.bitcast_convert_type(conv_w_shard.reshape(-1), BF16).reshape(-1)
    return jnp.pad(bits, (0, SPARE_EARLY * PACK_COLS - CONV_W_BITS)).reshape(SPARE_EARLY, PACK_COLS)


def _conv_w_from_rows(gathered):
    bits = gathered[:, EARLY[1] - SPARE_EARLY:].reshape(N_DEV, -1)[:, :CONV_W_BITS].reshape(N_DEV, 3 * 704, 2)
    w = lax.bitcast_convert_type(bits, F32).reshape(N_DEV, 3, 704)
    return w.transpose(1, 0, 2).reshape(3, 2 * D_FF)


def _unpack_local(packed, like, group):
    out, off = {}, 0
    for n, r, rp, tr in group[0]:
        rows, cols = like[n].shape
        seg = packed[off:off + r]
        out[n] = (seg.reshape(cols, rows).T if tr else seg)[None]
        off += rp
    return out


def _segments(g, group):
    out, off = {}, 0
    for n, r, rp, _ in group[0]:
        out[n] = g[:, off:off + r]
        off += rp
    return out


def _pack_grads(parts, group):
    table, rows = group
    segs = [jnp.pad(parts[n], ((0, 0), (0, rp - parts[n].shape[1]), (0, 0))) for n, _, rp, _ in table]
    segs.append(jnp.zeros((N_DEV, rows - sum(rp for _, _, rp, _ in table), PACK_COLS), F32))
    return jnp.concatenate(segs, axis=1)


def _owner_rows_early(g):
    g_in = jnp.concatenate([g["w_in_t"][:2432], g["w_in_t"][2496:2528]], axis=0).reshape(N_DEV, 308, PACK_COLS)
    g_uq = g["w_uq_t"].reshape(N_DEV, 128, MLA_Q_RANK)[:, :96].reshape(N_DEV, 24, PACK_COLS)
    g_ukv = jnp.concatenate([g["w_k_t"].reshape(N_DEV, 128, MLA_KV_RANK)[:, :64],
                             g["w_v_t"].reshape(N_DEV, 64, MLA_KV_RANK)], axis=1).reshape(N_DEV, 16, PACK_COLS)
    return dict(w_in=g_in, w_uq=g_uq, w_ukv=g_ukv)


def _owner_rows_late(g):
    g_up = g["w_up_t"].reshape(2, 2, 2, 704, PACK_COLS).swapaxes(0, 1).reshape(N_DEV, 704, PACK_COLS)
    return dict(w_out=g["w_out"].reshape(N_DEV, 128, PACK_COLS), w_up=g_up,
                w_down=g["w_down"].reshape(N_DEV, 352, PACK_COLS))


def _reduce_to_pairs(gp, *, name):
    gp = gp.reshape(4, 2, gp.shape[1], PACK_COLS)
    return _add_own(gp, _swap_sibling(gp, name=name + "_swap"), out_dtype=BF16, name=name + "_sum")


def _interleave_ff(w):
    g, v = w[..., :D_FF], w[..., D_FF:]
    return jnp.concatenate([g[..., :FF_HALF], v[..., :FF_HALF], g[..., FF_HALF:], v[..., FF_HALF:]], axis=-1)


def _rope_tables(pos):
    p = pos.astype(F32)[:, None]
    inv_r = ROPE_BASE ** (-jnp.arange(0, RET_HEAD_DIM, 2, dtype=F32) / RET_HEAD_DIM)
    ang = p * jnp.tile(inv_r, 4)
    sign_r = jnp.tile(jnp.concatenate([-jnp.ones((32,), F32), jnp.ones((32,), F32)]), 2)
    cos_r, ss_r = jnp.cos(ang), jnp.sin(ang) * sign_r
    inv_m = ROPE_BASE ** (-jnp.arange(0, MLA_ROPE, 2, dtype=F32) / MLA_ROPE)
    ang = p * jnp.concatenate([jnp.zeros((64,), F32), inv_m, inv_m, jnp.zeros((32,), F32)])
    sign_m = jnp.concatenate([jnp.zeros((64,), F32), -jnp.ones((16,), F32), jnp.ones((16,), F32), jnp.zeros((32,), F32)])
    cos_m, ss_m = jnp.cos(ang), jnp.sin(ang) * sign_m
    return cos_r, ss_r, cos_m, ss_m


def _prep_early(gathered):
    seg = _segments(gathered, EARLY)
    w_in_t = seg["w_in"].reshape(IN_WIDTH, D_MODEL)
    z = lambda n: jnp.zeros((n, D_MODEL), BF16)
    w_in_t = jnp.concatenate([w_in_t[:2432], z(64), w_in_t[2432:2464], z(32)], axis=0)
    w_uq_t = jnp.pad(seg["w_uq"].reshape(MLA_HEADS, 96, MLA_Q_RANK), ((0, 0), (0, 32), (0, 0))).reshape(1024, MLA_Q_RANK)
    ukv = seg["w_ukv"].reshape(MLA_HEADS, 128, MLA_KV_RANK)
    w_k_t = jnp.pad(ukv[:, :64], ((0, 0), (0, 64), (0, 0))).reshape(1024, MLA_KV_RANK)
    w_v_t = ukv[:, 64:].reshape(512, MLA_KV_RANK)
    return dict(w_in_t=w_in_t, w_uq_t=w_uq_t, w_k_t=w_k_t, w_v_t=w_v_t)


def _prep_late(gathered):
    seg = _segments(gathered, LATE)
    w_up_t = seg["w_up"].reshape(2, 2, 2, 704, D_MODEL).swapaxes(0, 1).reshape(2 * D_FF, D_MODEL)
    return dict(w_out=seg["w_out"].reshape(1024, D_MODEL), w_up_t=w_up_t, w_down=seg["w_down"].reshape(D_FF, D_MODEL))


def _local_step(x, pos, tgt, early, sm, late):
    dist = not isinstance(late, dict)
    cos_r, ss_r, cos_m, ss_m = _rope_tables(pos)
    tabs = _ret_tables()

    if dist:
        h, gathered = _rmsnorm_fwd(x, sm["attn_norm_w"], gather=early, name="attn_norm")
        W = _prep_early(gathered)
        sm = {**sm, "conv_w": _interleave_ff(_conv_w_from_rows(gathered))}
    else:
        h = _rmsnorm_fwd(x, sm["attn_norm_w"], name="attn_norm")
        W = early
    proj = _mm(h, W["w_in_t"], bt=True, name="in_proj")
    y_ret, o_ret, qr, kr = _ret_fwd(proj, cos_r, ss_r, tabs, sm["ret_gn_w"], name="ret_fwd")
    q, k, v1, cqn, ckvn = _mla_prep_fwd(proj, sm["mla_q_norm_w"], sm["mla_kv_norm_w"], W["w_uq_t"], W["w_k_t"],
                                       W["w_v_t"], cos_m, ss_m, name="mla_prep")
    T = x.shape[0]
    tq = min(T, 512)
    if dist:
        y_mla, lse, gathered = _flash_fwd(q, k, v1, gather=late, name="mla_attn")
        W = {**W, **_prep_late(gathered)}
    else:
        y_mla, lse = _flash_fwd(q, k, v1, name="mla_attn")
        W = {**W, **late}
    mixed = (y_ret, y_mla)
    x1, h2, u, a = _up_proj_conv(x, y_ret, y_mla, W["w_out"], sm["ffn_norm_w"], W["w_up_t"], sm["conv_w"],
                                 sm["conv_b"], name="out_proj_ffn_up_conv")
    loss, dx2, dx2b, d_final = _down_proj_loss(a, W["w_down"], x1, tgt, sm["final_norm_w"], name="down_proj_loss")

    g = {}
    g["w_down"] = _mm_tn(a, dx2b, name="dw_down")
    da = _mm(dx2b, W["w_down"], bt=True, name="d_act")
    du, dcw0, dcw1, dcw2, dcb = _conv_bwd(u, da, sm["conv_w"], sm["conv_b"], name="conv_bwd")
    g["w_up_t"] = _mm_tn(du, h2, name="dw_up")
    dx1, d_ffn = _mm_norm_bwd(du, W["w_up_t"], x1, sm["ffn_norm_w"], dx2, name="d_h2_ffn_norm_bwd")

    g["w_out"] = _mm_tn(mixed, dx1, name="dw_out")
    do_ret, dg, do_mla, delta, d_gn = _mix_bwd(dx1, W["w_out"], o_ret, proj, y_mla, sm["ret_gn_w"], name="d_mixed_mix_bwd")
    drq = _ret_bwd_dq(qr, kr, proj, do_ret, cos_r, ss_r, tabs, name="ret_bwd_dq")
    delta_r = delta.reshape(MLA_HEADS, T // tq, 1, tq)
    if dist:
        gl = _pack_grads(_owner_rows_late(g), LATE).reshape(4, 2, LATE[1], PACK_COLS)
        drk, drv, theirs = _ret_bwd_dkv(qr, kr, proj, do_ret, cos_r, ss_r, tabs, swap=gl, name="ret_bwd_dkv")
        pair = _add_own(gl, theirs, out_dtype=BF16, name="grad_late_sum")
        dqt, dk, dv, slots_late = _flash_bwd(q, k, v1, do_mla, lse, delta_r, exchange=pair, name="mla_attn_bwd")
    else:
        drk, drv = _ret_bwd_dkv(qr, kr, proj, do_ret, cos_r, ss_r, tabs, name="ret_bwd_dkv")
        dqt, dk, dv = _flash_bwd(q, k, v1, do_mla, lse, delta_r, name="mla_attn_bwd")
        slots_late = None
    dq = dqt.transpose(1, 3, 0, 2).reshape(T, MLA_HEADS * 128)
    dproj, g["w_in_t"], g["w_uq_t"], g["w_k_t"], g["w_v_t"], d_qn, d_kvn = _mla_prep_bwd(
        dq, dk, dv, proj, sm["mla_q_norm_w"], sm["mla_kv_norm_w"], W["w_uq_t"], W["w_k_t"], W["w_v_t"], cos_m, ss_m,
        (drq, drk, drv, dg), cqn, ckvn, h, name="mla_prep_bwd")
    if dist:
        pair = _reduce_to_pairs(_pack_grads(_owner_rows_early(g), EARLY), name="grad_early")
        grad_x, d_attn, slots_early = _mm_norm_bwd(dproj, W["w_in_t"], x, sm["attn_norm_w"], dx1, exchange=pair,
                                                   name="d_h_attn_norm_bwd")
    else:
        grad_x, d_attn = _mm_norm_bwd(dproj, W["w_in_t"], x, sm["attn_norm_w"], dx1, name="d_h_attn_norm_bwd")
        slots_early = None

    small = dict(attn_norm_w=d_attn, ret_gn_w=d_gn, mla_q_norm_w=d_qn, mla_kv_norm_w=d_kvn, ffn_norm_w=d_ffn,
                 conv_b=dcb, final_norm_w=d_final, conv_w0=dcw0, conv_w1=dcw1, conv_w2=dcw2, loss=loss)
    return loss, grad_x, g, small, slots_early, slots_late


def kernel(x, positions, attn_norm_w, w_in, ret_gn_w, mla_q_norm_w, w_uq, mla_kv_norm_w, w_ukv, w_out, ffn_norm_w, w_up, conv_w, conv_b, w_down, final_norm_w, loss_target, m_attn_norm_w, m_w_in, m_ret_gn_w, m_mla_q_norm_w, m_w_uq, m_mla_kv_norm_w, m_w_ukv, m_w_out, m_ffn_norm_w, m_w_up, m_conv_w, m_conv_b, m_w_down, m_final_norm_w, v_attn_norm_w, v_w_in, v_ret_gn_w, v_mla_q_norm_w, v_w_uq, v_mla_kv_norm_w, v_w_ukv, v_w_out, v_ffn_norm_w, v_w_up, v_conv_w, v_conv_b, v_w_down, v_final_norm_w):
    a = dict(locals())
    x_, y_, c_ = _place()
    dev = 4 * x_ + 2 * y_ + c_

    shard = {n: a[n][0] for n in BIG_NAMES}
    shard16 = {n: w.astype(BF16) for n, w in shard.items()}
    sm = dict(attn_norm_w=attn_norm_w, ret_gn_w=ret_gn_w, mla_q_norm_w=mla_q_norm_w, mla_kv_norm_w=mla_kv_norm_w,
              ffn_norm_w=ffn_norm_w, final_norm_w=final_norm_w.reshape(1, D_MODEL), conv_b=_interleave_ff(conv_b))

    loss, grad_x, _, gs, slots_early, slots_late = _local_step(
        x[0], positions[0], loss_target[0], _pack_local(shard16, EARLY, tail=_conv_w_as_rows(conv_w[0])), sm,
        _pack_local(shard16, LATE))

    big = [{}, {}, {}, {}]
    for group, slots, tag, calls in ((EARLY, slots_early, "early", (("w_in", "w_uq", "w_ukv"),)),
                                     (LATE, slots_late, "late", (("w_out", "w_down"), ("w_up",)))):
        grads = _unpack_local(_sum_chips(slots, name="grad_sum_" + tag), shard, group)
        for names_c in calls:
            res = _adamw_multi([shard[n] for n in names_c], [a["m_" + n][0] for n in names_c],
                               [a["v_" + n][0] for n in names_c], [grads[n][0] for n in names_c],
                               name="adamw_" + "_".join(names_c))
            for kind in range(4):
                for n, r in zip(names_c, res[kind]):
                    big[kind][n] = r[None]

    packed = _pack_small(gs, name="pack_small_grads")
    tot = _sum_small(_all_gather(packed, name="gather_small_grads", in_vmem=True), name="sum_small_grads")
    loss_out = tot["loss"][0, 0]
    g_cw = lax.dynamic_slice_in_dim(tot["conv_w"], dev * 704, 704, axis=1)

    def rows_of(prefix):
        return [a[prefix + n].reshape(1, size) for n, size in SMALL]

    sml = _adamw_multi(rows_of("") + [conv_w[0]], rows_of("m_") + [m_conv_w[0]], rows_of("v_") + [v_conv_w[0]],
                       [tot[n] for n, _ in SMALL] + [g_cw], name="adamw_small")
    cwo = [kind[-1] for kind in sml]

    def small_of(kind, n):
        return sml[kind][[nm for nm, _ in SMALL].index(n)].reshape(a[n].shape)

    names = ['attn_norm_w', 'w_in', 'ret_gn_w', 'mla_q_norm_w', 'w_uq', 'mla_kv_norm_w', 'w_ukv', 'w_out',
             'ffn_norm_w', 'w_up', 'conv_w', 'conv_b', 'w_down', 'final_norm_w']
    outs = [loss_out, grad_x[None]]
    for kind in range(4):
        for n in names:
            if n == "conv_w":
                outs.append(cwo[kind][None])
            elif n in big[kind]:
                outs.append(big[kind][n])
            else:
                outs.append(small_of(kind, n))
    return tuple(outs)
```

```python
import jax
import jax.numpy as jnp
from jax import lax
from jax.experimental import pallas as pl
from jax.experimental.pallas import tpu as pltpu

F32 = jnp.float32
BF16 = jnp.bfloat16
MESH = pl.DeviceIdType.MESH
ANY = pl.BlockSpec(memory_space=pl.ANY)

D_MODEL = 1024
RET_HEADS = 8
RET_HEAD_DIM = 64
RET_WIDTH = 512
RET_CHUNK = 128
MLA_HEADS = 8
MLA_NOPE = 64
MLA_ROPE = 32
MLA_V = 64
MLA_Q_RANK = 256
MLA_KV_RANK = 128
MLA_WIDTH = 512
IN_WIDTH = 2464
IN_PAD = 2560
D_FF = 2816
FF_HALF = 1408
ROPE_BASE = 10000.0
EPS = 1e-6
SCALE = float((MLA_NOPE + MLA_ROPE) ** -0.5)
K_SCALE = 0.125
N_DEV = 8

ADAM_LR = 0.001
ADAM_B1 = 0.9
ADAM_B2 = 0.999
ADAM_EPS = 1e-08
ADAM_WD = 0.01
ADAM_STEP = 10

VMEM_LIMIT = 56 * 1024 * 1024
MM_BUDGET = 40 * 1024 * 1024
NEG = -1e30
FLASH_UNROLL = 4
FLASH_BWD_UNROLL = 3

PACK_COLS = 1024
EARLY = ((("w_in", 308, 320, True), ("w_uq", 24, 32, True), ("w_ukv", 16, 16, True)), 384)
LATE = ((("w_out", 128, 128, False), ("w_up", 704, 704, True), ("w_down", 352, 352, False)), 1200)
BIG_NAMES = ("w_in", "w_uq", "w_ukv", "w_out", "w_up", "w_down")
SMALL = (("attn_norm_w", 1024), ("ret_gn_w", 512), ("mla_q_norm_w", 256), ("mla_kv_norm_w", 128),
         ("ffn_norm_w", 1024), ("conv_b", 5632), ("final_norm_w", 1024))
SMALL_VECTORS = SMALL + (("conv_w0", 5632), ("conv_w1", 5632), ("conv_w2", 5632), ("loss", 128))
SMALL_ROWS = 32


def _cp(sem=None, vmem=VMEM_LIMIT):
    return pltpu.CompilerParams(dimension_semantics=sem, vmem_limit_bytes=vmem)


def _dot(a, b):
    return jnp.dot(a, b, preferred_element_type=F32)


def _dot_nt(a, b):
    return lax.dot_general(a, b, (((1,), (1,)), ((), ())), preferred_element_type=F32)


def _dot_tn(a, b):
    return lax.dot_general(a, b, (((0,), (0,)), ((), ())), preferred_element_type=F32)


def _sigmoid(x):
    return 0.5 * jnp.tanh(0.5 * x) + 0.5


def _partner(x, half, period):
    n = x.shape[-1]
    lane = lax.broadcasted_iota(jnp.int32, x.shape, 1)
    return jnp.where((lane % period) < half, pltpu.roll(x, n - half, 1), pltpu.roll(x, half, 1))


def _rope(x, cos, ss, half, period):
    return x * cos + _partner(x, half, period) * ss


def _rope_t(dy, cos, ss, half, period):
    return dy * cos - _partner(dy, half, period) * ss


def _head_masks(shape):
    lane = lax.broadcasted_iota(jnp.int32, shape, 1)
    m0 = (lane < 64).astype(F32)
    return m0, 1.0 - m0


def _mm_nt(a, b, *, name):
    M, K = a.shape
    N = b.shape[0]
    per_row = 2 * (K * a.dtype.itemsize + N * 4)
    tm = 128
    for cand in (512, 256):
        if M % cand == 0 and cand * per_row + 4 * K * N <= MM_BUDGET:
            tm = cand
            break
    tm = min(tm, M)

    def body(a_ref, b_ref, o_ref):
        o_ref[...] = _dot_nt(a_ref[...], b_ref[...])

    return pl.pallas_call(
        body, name=name, grid=(M // tm,),
        in_specs=[pl.BlockSpec((tm, K), lambda i: (i, 0)), pl.BlockSpec(b.shape, lambda i: (0, 0))],
        out_specs=pl.BlockSpec((tm, N), lambda i: (i, 0)), out_shape=jax.ShapeDtypeStruct((M, N), F32),
        compiler_params=_cp(("parallel",)))(a, b)


def _mm_tn(a, b, *, name):
    parts = a if isinstance(a, tuple) else (a,)
    T = parts[0].shape[0]
    M = sum(p.shape[1] for p in parts)
    N = b.shape[1]
    tk = min(T, 512)

    def tile(n):
        for cand in (1408, 1280):
            if n > 1408 and n % cand == 0:
                return cand
        return n

    tm, tn = tile(M), tile(N)
    nk = T // tk
    n_a = len(parts)
    assert n_a == 1 or tm == M

    def body(*refs):
        o_ref = refs[n_a + 1]

        @pl.when(pl.program_id(2) == 0)
        def _():
            o_ref[...] = jnp.zeros_like(o_ref)
        av = refs[0][...] if n_a == 1 else jnp.concatenate([r[...] for r in refs[:n_a]], axis=1)
        o_ref[...] += _dot_tn(av.astype(BF16), refs[n_a][...].astype(BF16))

    if n_a == 1:
        a_specs = [pl.BlockSpec((tk, tm), lambda i, j, k: (k, i))]
    else:
        a_specs = [pl.BlockSpec((tk, p.shape[1]), lambda i, j, k: (k, 0)) for p in parts]
    return pl.pallas_call(
        body, name=name, grid=(M // tm, N // tn, nk),
        in_specs=a_specs + [pl.BlockSpec((tk, tn), lambda i, j, k: (k, j))],
        out_specs=pl.BlockSpec((tm, tn), lambda i, j, k: (i, j)),
        out_shape=jax.ShapeDtypeStruct((M, N), F32),
        compiler_params=_cp(("parallel", "parallel", "arbitrary")))(*parts, b)


def _rmsnorm_fwd(x, w, *, name, gather=None):
    T, D = x.shape
    tm = min(T, 1024)
    n = T // tm

    def body(x_ref, w_ref, *rest):
        if gather is not None:
            s_ref, o_ref, g_ref, *sems = rest
            start, forward, finish = _gather_phases(s_ref, g_ref, *sems)
            pl.when(pl.program_id(0) == 0)(start)
            pl.when(pl.program_id(0) == n // 2)(forward)
        else:
            o_ref, = rest
        xv = x_ref[...]
        r = lax.rsqrt(jnp.mean(xv * xv, axis=-1, keepdims=True) + EPS)
        o_ref[...] = (xv * r * w_ref[...]).astype(BF16)
        if gather is not None:
            pl.when(pl.program_id(0) == n - 1)(finish)

    in_specs = [pl.BlockSpec((tm, D), lambda i: (i, 0)), pl.BlockSpec((1, D), lambda i: (0, 0))]
    out_spec = pl.BlockSpec((tm, D), lambda i: (i, 0))
    out_shape = jax.ShapeDtypeStruct((T, D), BF16)
    if gather is None:
        return pl.pallas_call(body, name=name, grid=(n,), in_specs=in_specs, out_specs=out_spec, out_shape=out_shape,
                              compiler_params=_cp(("parallel",)))(x, w)
    return pl.pallas_call(
        body, name=name, grid=(n,), in_specs=in_specs + [ANY], out_specs=[out_spec, ANY],
        out_shape=[out_shape, jax.ShapeDtypeStruct((N_DEV,) + gather.shape, gather.dtype)],
        scratch_shapes=list(GATHER_SCRATCH), compiler_params=_cp(("arbitrary",)))(x, w, gather)


def _mm_norm_bwd(a, b, x, w, dres, *, name, exchange=None):
    T, K = a.shape
    D = b.shape[1]
    tm = min(T, 256 if K > 4096 else 512)
    n = T // tm

    def body(a_ref, b_ref, x_ref, w_ref, dr_ref, *rest):
        if exchange is None:
            dx_ref, dw_ref = rest
        else:
            p_ref, dx_ref, dw_ref, got_ref, *sems = rest
            start, finish = _exchange_phases(p_ref, got_ref, *sems)
            pl.when(pl.program_id(0) == 0)(start)

        @pl.when(pl.program_id(0) == 0)
        def _():
            dw_ref[...] = jnp.zeros_like(dw_ref)
        dh = _dot(a_ref[...], b_ref[...])
        xv = x_ref[...]
        r = lax.rsqrt(jnp.mean(xv * xv, axis=-1, keepdims=True) + EPS)
        xh = xv * r
        g = dh * w_ref[...]
        dx_ref[...] = dr_ref[...] + r * (g - xh * jnp.mean(g * xh, axis=-1, keepdims=True))
        dw_ref[...] += jnp.sum(dh * xh, axis=0, keepdims=True)
        if exchange is not None:
            pl.when(pl.program_id(0) == n - 1)(finish)

    row = pl.BlockSpec((tm, D), lambda i: (i, 0))
    vec = pl.BlockSpec((1, D), lambda i: (0, 0))
    in_specs = [pl.BlockSpec((tm, K), lambda i: (i, 0)), pl.BlockSpec((K, D), lambda i: (0, 0)), row, vec, row]
    out_shape = [jax.ShapeDtypeStruct((T, D), F32), jax.ShapeDtypeStruct((1, D), F32)]
    if exchange is None:
        return pl.pallas_call(body, name=name, grid=(n,), in_specs=in_specs, out_specs=[row, vec], out_shape=out_shape,
                              compiler_params=_cp(("arbitrary",)))(a, b, x, w, dres)
    return pl.pallas_call(
        body, name=name, grid=(n,), in_specs=in_specs + [ANY], out_specs=[row, vec, ANY],
        out_shape=out_shape + [jax.ShapeDtypeStruct(exchange.shape, exchange.dtype)],
        scratch_shapes=list(EXCHANGE_SCRATCH), compiler_params=_cp(("arbitrary",)))(a, b, x, w, dres, exchange)


def _down_proj_loss(a, w_down, x1, tgt, w, *, name):
    T, D = x1.shape
    K = a.shape[1]
    tm = min(T, 512)

    def body(a_ref, b_ref, x_ref, t_ref, w_ref, loss_ref, dx_ref, dxb_ref, dw_ref):
        @pl.when(pl.program_id(0) == 0)
        def _():
            dw_ref[...] = jnp.zeros_like(dw_ref)
            loss_ref[...] = jnp.zeros_like(loss_ref)
        xv = x_ref[...] + _dot(a_ref[...], b_ref[...])
        wv = w_ref[...]
        r = lax.rsqrt(jnp.mean(xv * xv, axis=-1, keepdims=True) + EPS)
        xh = xv * r
        e = xh * wv - t_ref[...]
        part = 0.5 * jnp.sum(jnp.mean(e * e, axis=-1, keepdims=True), axis=0, keepdims=True)
        loss_ref[...] += jnp.broadcast_to(part, loss_ref.shape)
        dy = e * (1.0 / D)
        g = dy * wv
        dx = r * (g - xh * jnp.mean(g * xh, axis=-1, keepdims=True))
        dx_ref[...] = dx
        dxb_ref[...] = dx.astype(BF16)
        dw_ref[...] += jnp.sum(dy * xh, axis=0, keepdims=True)

    row = pl.BlockSpec((tm, D), lambda i: (i, 0))
    vec = pl.BlockSpec((1, D), lambda i: (0, 0))
    return pl.pallas_call(
        body, name=name, grid=(T // tm,),
        in_specs=[pl.BlockSpec((tm, K), lambda i: (i, 0)), pl.BlockSpec((K, D), lambda i: (0, 0)), row, row, vec],
        out_specs=[pl.BlockSpec((1, 128), lambda i: (0, 0)), row, row, vec],
        out_shape=[jax.ShapeDtypeStruct((1, 128), F32), jax.ShapeDtypeStruct((T, D), F32),
                   jax.ShapeDtypeStruct((T, D), BF16), jax.ShapeDtypeStruct((1, D), F32)],
        compiler_params=_cp(("arbitrary",)))(a, w_down, x1, tgt, w)


def _ret_tables():
    C = RET_CHUNK
    h = jnp.arange(RET_HEADS, dtype=F32)
    log_gamma = jnp.log1p(-jnp.power(2.0, -5.0 - h))
    idx = jnp.arange(C, dtype=F32)
    diff = idx[:, None] - idx[None, :]
    dm = jnp.where(diff >= 0, jnp.exp(log_gamma[:, None, None] * jnp.maximum(diff, 0.0)), 0.0)
    dm = dm.reshape(4, 2 * C, C)
    lane_head = jnp.repeat(jnp.arange(RET_HEADS).reshape(4, 2), 64, axis=1)
    lg = log_gamma[lane_head]
    xi = jnp.exp(lg[:, None, :] * (idx[None, :, None] + 1.0))
    zeta = jnp.exp(lg[:, None, :] * (C - 1.0 - idx[None, :, None]))
    blk = (jnp.arange(128)[:, None] // 64) == (jnp.arange(128)[None, :] // 64)
    cd = jnp.where(blk[None], jnp.exp(lg * C)[:, :, None], 0.0)
    return dm.astype(F32), xi.astype(F32), zeta.astype(F32), cd.astype(F32)


def _ret_specs(tb, rev, nt, roped=False):
    def tmap(t):
        return (nt - 1 - t) if rev else t
    offsets = (0, 0, 8) if roped else (0, 4, 8)
    qkv = [pl.BlockSpec((tb, 128), lambda p, t, o=o: (tmap(t), o + p)) for o in offsets]
    rope = [pl.BlockSpec((tb, 128), lambda p, t: (tmap(t), 0))] * 2
    tabs = [pl.BlockSpec((None, 256, 128), lambda p, t: (p, 0, 0))] + \
           [pl.BlockSpec((None, 128, 128), lambda p, t: (p, 0, 0))] * 3
    return qkv, rope, tabs


def _ret_fwd(proj, cos, ss, tabs, gnw, *, name):
    T = proj.shape[0]
    tb = min(T, 1024)
    nt = T // tb
    nchunk = tb // RET_CHUNK

    def body(q_ref, k_ref, v_ref, g_ref, cos_ref, ss_ref, dm_ref, xi_ref, zt_ref, cd_ref, gnw_ref,
             y_ref, o_ref, qr_ref, kr_ref, r_sc):
        @pl.when(pl.program_id(1) == 0)
        def _():
            r_sc[...] = jnp.zeros_like(r_sc)
        m0, m1 = _head_masks((128, 128))
        dm, xi, zt, cd = dm_ref[...], xi_ref[...], zt_ref[...], cd_ref[...]
        bm = (cd > 0).astype(F32)
        gnw = gnw_ref[...]
        for c in range(nchunk):
            rs = pl.ds(c * RET_CHUNK, RET_CHUNK)
            cs, sn = cos_ref[rs, :], ss_ref[rs, :]
            q = _rope(q_ref[rs, :], cs, sn, 32, 64)
            k = _rope(k_ref[rs, :], cs, sn, 32, 64) * K_SCALE
            v = v_ref[rs, :]
            kb, vb = k.astype(BF16), v.astype(BF16)
            qr_ref[rs, :] = q.astype(BF16)
            kr_ref[rs, :] = kb
            qs = jnp.concatenate([q * m0, q * m1], axis=0).astype(BF16)
            s = (_dot_nt(qs, kb) * dm).astype(BF16)
            vs = jnp.concatenate([v * m0, v * m1], axis=0).astype(BF16)
            o = _dot(jnp.concatenate([s[:128], s[128:]], axis=1), vs)
            r = r_sc[...]
            o = o + _dot(q.astype(BF16), r.astype(BF16)) * xi
            r_sc[...] = cd * r + bm * _dot_tn((k * zt).astype(BF16), vb)
            mu = (jnp.sum(o * m0, axis=1, keepdims=True) * m0 + jnp.sum(o * m1, axis=1, keepdims=True) * m1) * (1.0 / 64)
            d = o - mu
            dd = d * d
            var = (jnp.sum(dd * m0, axis=1, keepdims=True) * m0 + jnp.sum(dd * m1, axis=1, keepdims=True) * m1) * (1.0 / 64)
            oh = d * lax.rsqrt(var + EPS)
            g = g_ref[rs, :]
            y_ref[rs, :] = (g * _sigmoid(g) * (oh * gnw)).astype(BF16)
            o_ref[rs, :] = o

    qkv, rope, tspec = _ret_specs(tb, False, nt)
    gspec = pl.BlockSpec((tb, 128), lambda p, t: (t, 12 + p))
    out = pl.BlockSpec((tb, 128), lambda p, t: (t, p))
    return pl.pallas_call(
        body, name=name, grid=(4, nt),
        in_specs=qkv + [gspec] + rope + tspec + [pl.BlockSpec((1, 128), lambda p, t: (0, p))],
        out_specs=[out, out, out, out],
        out_shape=[jax.ShapeDtypeStruct((T, RET_WIDTH), BF16), jax.ShapeDtypeStruct((T, RET_WIDTH), F32),
                   jax.ShapeDtypeStruct((T, RET_WIDTH), BF16), jax.ShapeDtypeStruct((T, RET_WIDTH), BF16)],
        scratch_shapes=[pltpu.VMEM((128, 128), F32)],
        compiler_params=_cp(("parallel", "arbitrary")))(proj, proj, proj, proj, cos, ss, *tabs, gnw)


def _ret_bwd_dq(kr, proj, do, cos, ss, tabs, *, name):
    T = proj.shape[0]
    tb = min(T, 1024)
    nt = T // tb
    nchunk = tb // RET_CHUNK

    def body(k_ref, v_ref, do_ref, cos_ref, ss_ref, dm_ref, xi_ref, zt_ref, cd_ref, dq_ref, r_sc):
        @pl.when(pl.program_id(1) == 0)
        def _():
            r_sc[...] = jnp.zeros_like(r_sc)
        m0, m1 = _head_masks((128, 128))
        dm, xi, zt, cd = dm_ref[...], xi_ref[...], zt_ref[...], cd_ref[...]
        bm = (cd > 0).astype(F32)
        for c in range(nchunk):
            rs = pl.ds(c * RET_CHUNK, RET_CHUNK)
            cs, sn = cos_ref[rs, :], ss_ref[rs, :]
            k = k_ref[rs, :].astype(F32)
            vb = v_ref[rs, :].astype(BF16)
            dob = do_ref[rs, :]
            dof = dob.astype(F32)
            dos = jnp.concatenate([dof * m0, dof * m1], axis=0).astype(BF16)
            a = (_dot_nt(dos, vb) * dm).astype(BF16)
            ks = jnp.concatenate([k * m0, k * m1], axis=0).astype(BF16)
            r = r_sc[...]
            dq = _dot(jnp.concatenate([a[:128], a[128:]], axis=1), ks) + _dot_nt(dob, r.astype(BF16)) * xi
            r_sc[...] = cd * r + bm * _dot_tn((k * zt).astype(BF16), vb)
            dq_ref[rs, :] = _rope_t(dq, cs, sn, 32, 64).astype(BF16)

    qkv, rope, tspec = _ret_specs(tb, False, nt, roped=True)
    blk = pl.BlockSpec((tb, 128), lambda p, t: (t, p))
    return pl.pallas_call(
        body, name=name, grid=(4, nt), in_specs=qkv[1:] + [blk] + rope + tspec, out_specs=blk,
        out_shape=jax.ShapeDtypeStruct((T, RET_WIDTH), BF16),
        scratch_shapes=[pltpu.VMEM((128, 128), F32)],
        compiler_params=_cp(("parallel", "arbitrary")))(kr, proj, do, cos, ss, *tabs)


def _ret_bwd_dkv(qr, kr, proj, do, cos, ss, tabs, *, name, swap=None):
    T = proj.shape[0]
    tb = min(T, 1024)
    nt = T // tb
    nchunk = tb // RET_CHUNK

    def body(q_ref, k_ref, v_ref, do_ref, cos_ref, ss_ref, dm_ref, xi_ref, zt_ref, cd_ref, *rest):
        if swap is None:
            backward(q_ref, k_ref, v_ref, do_ref, cos_ref, ss_ref, dm_ref, xi_ref, zt_ref, cd_ref, *rest)
        else:
            g_ref, dk_ref, dv_ref, got_ref, u_sc, *sems = rest
            start, finish = _swap_phases(g_ref, got_ref, *sems)
            pl.when((pl.program_id(0) == 0) & (pl.program_id(1) == 0))(start)
            backward(q_ref, k_ref, v_ref, do_ref, cos_ref, ss_ref, dm_ref, xi_ref, zt_ref, cd_ref, dk_ref, dv_ref, u_sc)
            pl.when((pl.program_id(0) == 3) & (pl.program_id(1) == nt - 1))(finish)

    def backward(q_ref, k_ref, v_ref, do_ref, cos_ref, ss_ref, dm_ref, xi_ref, zt_ref, cd_ref, dk_ref, dv_ref, u_sc):
        @pl.when(pl.program_id(1) == 0)
        def _():
            u_sc[...] = jnp.zeros_like(u_sc)
        m0, m1 = _head_masks((128, 128))
        dm, xi, zt, cd = dm_ref[...], xi_ref[...], zt_ref[...], cd_ref[...]
        bm = (cd > 0).astype(F32)
        for c in reversed(range(nchunk)):
            rs = pl.ds(c * RET_CHUNK, RET_CHUNK)
            cs, sn = cos_ref[rs, :], ss_ref[rs, :]
            kb = k_ref[rs, :]
            q = q_ref[rs, :].astype(F32)
            vb = v_ref[rs, :].astype(BF16)
            dob = do_ref[rs, :]
            dof = dob.astype(F32)
            qs = jnp.concatenate([q * m0, q * m1], axis=0).astype(BF16)
            dos = jnp.concatenate([dof * m0, dof * m1], axis=0).astype(BF16)
            s = (_dot_nt(qs, kb) * dm).astype(BF16)
            a = (_dot_nt(dos, vb) * dm).astype(BF16)
            ub = u_sc[...].astype(BF16)
            dk = _dot_tn(a, qs) + _dot_nt(vb, ub) * zt
            dv = _dot_tn(s, dos) + _dot(kb, ub) * zt
            u_sc[...] = cd * u_sc[...] + bm * _dot_tn((q * xi).astype(BF16), dob)
            dk_ref[rs, :] = (_rope_t(dk, cs, sn, 32, 64) * K_SCALE).astype(BF16)
            dv_ref[rs, :] = dv.astype(BF16)

    qkv, rope, tspec = _ret_specs(tb, True, nt, roped=True)
    blk = pl.BlockSpec((tb, 128), lambda p, t: (nt - 1 - t, p))
    out_shape = [jax.ShapeDtypeStruct((T, RET_WIDTH), BF16)] * 2
    if swap is None:
        return pl.pallas_call(
            body, name=name, grid=(4, nt), in_specs=qkv + [blk] + rope + tspec, out_specs=[blk, blk],
            out_shape=out_shape, scratch_shapes=[pltpu.VMEM((128, 128), F32)],
            compiler_params=_cp(("parallel", "arbitrary")))(qr, kr, proj, do, cos, ss, *tabs)
    return pl.pallas_call(
        body, name=name, grid=(4, nt), in_specs=qkv + [blk] + rope + tspec + [ANY], out_specs=[blk, blk, ANY],
        out_shape=out_shape + [jax.ShapeDtypeStruct((4,) + swap.shape[2:], swap.dtype)],
        scratch_shapes=[pltpu.VMEM((128, 128), F32)] + list(SWAP_SCRATCH),
        compiler_params=_cp(("arbitrary", "arbitrary")))(qr, kr, proj, do, cos, ss, *tabs, swap)


def _mix_bwd(dx1, w_out, o_ret, proj, y_mla, gnw, *, name):
    T = dx1.shape[0]
    tm = min(T, 512)

    def body(dx_ref, wo_ref, o_ref, g_ref, ym_ref, gnw_ref, do_ref, dg_ref, dom_ref, dl_ref, dw_ref, dm_ref):
        @pl.when(pl.program_id(0) == 0)
        def _():
            dw_ref[...] = jnp.zeros_like(dw_ref)
        dm_ref[...] = _dot_nt(dx_ref[...].astype(BF16), wo_ref[...])
        m0, m1 = _head_masks((tm, 128))
        lane = lax.broadcasted_iota(jnp.int32, (tm, 128), 1)
        delta = jnp.zeros((tm, 128), F32)

        def gsum(z):
            return jnp.sum(z * m0, axis=1, keepdims=True) * m0 + jnp.sum(z * m1, axis=1, keepdims=True) * m1

        for p in range(4):
            cs = slice(128 * p, 128 * p + 128)
            dy = dm_ref[:, cs]
            o = o_ref[:, cs]
            g = g_ref[:, cs]
            w = gnw_ref[:, cs]
            d = o - gsum(o) * (1.0 / 64)
            rstd = lax.rsqrt(gsum(d * d) * (1.0 / 64) + EPS)
            oh = d * rstd
            sg = _sigmoid(g)
            dn = dy * (g * sg)
            dg_ref[:, cs] = (dy * (oh * w) * (sg * (1.0 + g * (1.0 - sg)))).astype(BF16)
            dw_ref[:, cs] += jnp.sum(dn * oh, axis=0, keepdims=True)
            doh = dn * w
            do = rstd * (doh - gsum(doh) * (1.0 / 64) - oh * (gsum(doh * oh) * (1.0 / 64)))
            do_ref[:, cs] = do.astype(BF16)
            dom = dm_ref[:, 512 + 128 * p:512 + 128 * p + 128]
            dom_ref[:, cs] = dom.astype(BF16)
            pr = dom * ym_ref[:, cs].astype(F32)
            delta = jnp.where(lane == 2 * p, jnp.sum(pr * m0, axis=1, keepdims=True), delta)
            delta = jnp.where(lane == 2 * p + 1, jnp.sum(pr * m1, axis=1, keepdims=True), delta)
        dl_ref[...] = delta.T[0:MLA_HEADS]

    half = pl.BlockSpec((tm, 512), lambda i: (i, 0))
    return pl.pallas_call(
        body, name=name, grid=(T // tm,),
        in_specs=[pl.BlockSpec((tm, D_MODEL), lambda i: (i, 0)), pl.BlockSpec(w_out.shape, lambda i: (0, 0)), half,
                  pl.BlockSpec((tm, 512), lambda i: (i, 3)), half, pl.BlockSpec((1, 512), lambda i: (0, 0))],
        out_specs=[half, half, half, pl.BlockSpec((MLA_HEADS, tm), lambda i: (0, i)),
                   pl.BlockSpec((1, 512), lambda i: (0, 0))],
        out_shape=[jax.ShapeDtypeStruct((T, 512), BF16)] * 3 + [jax.ShapeDtypeStruct((MLA_HEADS, T), F32),
                                                                jax.ShapeDtypeStruct((1, 512), F32)],
        scratch_shapes=[pltpu.VMEM((tm, 1024), F32)],
        compiler_params=_cp(("arbitrary",)))(dx1, w_out, o_ret, proj, y_mla, gnw)


def _mla_prep_fwd(proj, qnw, kvnw, wuq, wk, wv, cos, ss, *, name):
    T = proj.shape[0]
    tm = min(T, 512)

    def body(lat_ref, qnw_ref, kvnw_ref, wuq_ref, wk_ref, wv_ref, cos_ref, ss_ref,
             q_ref, k_ref, v_ref, cqn_ref, ckvn_ref):
        cq = lat_ref[:, 0:256]
        ckv = lat_ref[:, 256:384]
        g3 = lat_ref[:, 384:512]
        cqn = (cq * lax.rsqrt(jnp.mean(cq * cq, axis=-1, keepdims=True) + EPS) * qnw_ref[...]).astype(BF16)
        ckvn = (ckv * lax.rsqrt(jnp.mean(ckv * ckv, axis=-1, keepdims=True) + EPS) * kvnw_ref[...]).astype(BF16)
        cqn_ref[...] = cqn
        ckvn_ref[...] = ckvn
        cs, sn = cos_ref[...], ss_ref[...]
        q = _dot_nt(cqn, wuq_ref[...])
        k = _dot_nt(ckvn, wk_ref[...])
        kpe = _rope(g3, cs, sn, 16, 32)
        for h in range(MLA_HEADS):
            hs = slice(128 * h, 128 * h + 128)
            q_ref[:, hs] = (_rope(q[:, hs], cs, sn, 16, 32) * SCALE).astype(BF16)
            k_ref[:, hs] = (k[:, hs] + kpe).astype(BF16)
        v = _dot_nt(ckvn, wv_ref[...])
        lane = lax.broadcasted_iota(jnp.int32, (tm, 128), 1)
        for p in range(4):
            vp = v[:, 128 * p:128 * p + 128]
            v_ref[:, 256 * p:256 * p + 128] = jnp.where(lane < 64, vp, 1.0).astype(BF16)
            v_ref[:, 256 * p + 128:256 * p + 256] = jnp.where(lane < 64, 1.0, vp).astype(BF16)

    def full(shape):
        return pl.BlockSpec(shape, lambda i: (0, 0))

    def row(w):
        return pl.BlockSpec((tm, w), lambda i: (i, 0))

    return pl.pallas_call(
        body, name=name, grid=(T // tm,),
        in_specs=[pl.BlockSpec((tm, 512), lambda i: (i, 4)), full((1, 256)), full((1, 128)), full((1024, 256)),
                  full((1024, 128)), full((512, 128)), row(128), row(128)],
        out_specs=[row(1024), row(1024), row(1024), row(256), row(128)],
        out_shape=[jax.ShapeDtypeStruct((T, 1024), BF16), jax.ShapeDtypeStruct((T, 1024), BF16),
                   jax.ShapeDtypeStruct((T, 1024), BF16), jax.ShapeDtypeStruct((T, 256), BF16),
                   jax.ShapeDtypeStruct((T, 128), BF16)],
        compiler_params=_cp(("parallel",)))(proj, qnw, kvnw, wuq, wk, wv, cos, ss)


def _mla_prep_bwd(dq, dk, dv, proj, qnw, kvnw, wuq_t, wk_t, wv_t, cos, ss, ret_grads, cqn, ckvn, h, *, name):
    T = proj.shape[0]
    tm = min(T, 256)

    def body(dq_ref, dk_ref, dv_ref, lat_ref, qnw_ref, kvnw_ref, wuq_ref, wk_ref, wv_ref, cos_ref, ss_ref,
             rq_ref, rk_ref, rv_ref, rg_ref, cqn_ref, ckvn_ref, h_ref,
             dproj_ref, gwin_ref, gwuq_ref, gwk_ref, gwv_ref, dqnw_ref, dkvnw_ref, dqp_ref):
        for j, r in enumerate((rq_ref, rk_ref, rv_ref, rg_ref)):
            dproj_ref[:, 512 * j:512 * j + 512] = r[...]
        dlat_ref = dproj_ref.at[:, 2048:2560]

        @pl.when(pl.program_id(0) == 0)
        def _():
            for r in (gwin_ref, gwuq_ref, gwk_ref, gwv_ref, dqnw_ref, dkvnw_ref):
                r[...] = jnp.zeros_like(r)
        cs, sn = cos_ref[...], ss_ref[...]
        dkpe = jnp.zeros((tm, 128), F32)
        for h in range(MLA_HEADS):
            hs = slice(128 * h, 128 * h + 128)
            dqp_ref[:, hs] = _rope_t(dq_ref[:, hs] * SCALE, cs, sn, 16, 32).astype(BF16)
            dkpe = dkpe + dk_ref[:, hs]
        lane = lax.broadcasted_iota(jnp.int32, (tm, 128), 1)
        rope_lane = (lane >= MLA_NOPE) & (lane < MLA_NOPE + MLA_ROPE)
        dg3 = jnp.where(rope_lane, _rope_t(jnp.where(rope_lane, dkpe, 0.0), cs, sn, 16, 32), 0.0)

        def norm_bwd(x, w, dn):
            r = lax.rsqrt(jnp.mean(x * x, axis=-1, keepdims=True) + EPS)
            xh = x * r
            g = dn * w
            return r * (g - xh * jnp.mean(g * xh, axis=-1, keepdims=True)), jnp.sum(dn * xh, axis=0, keepdims=True)

        dqp = dqp_ref[...]
        dkb = dk_ref[...].astype(BF16)
        dvb = dv_ref[...]
        dcqn = _dot(dqp, wuq_ref[...])
        dcq, dqnw = norm_bwd(lat_ref[:, 0:256], qnw_ref[...], dcqn)
        dckvn = _dot(dkb, wk_ref[...]) + _dot(dvb, wv_ref[...])
        dckv, dkvnw = norm_bwd(lat_ref[:, 256:384], kvnw_ref[...], dckvn)
        gwuq_ref[...] += _dot_tn(dqp, cqn_ref[...])
        gwk_ref[...] += _dot_tn(dkb, ckvn_ref[...])
        gwv_ref[...] += _dot_tn(dvb, ckvn_ref[...])
        dqnw_ref[...] += dqnw
        dkvnw_ref[...] += dkvnw
        dlat_ref[:, 0:256] = dcq.astype(BF16)
        dlat_ref[:, 256:384] = dckv.astype(BF16)
        dlat_ref[:, 384:512] = dg3.astype(BF16)
        gwin_ref[...] += _dot_tn(dproj_ref[...], h_ref[...])

    def full(shape):
        return pl.BlockSpec(shape, lambda i: (0, 0))

    def row(w):
        return pl.BlockSpec((tm, w), lambda i: (i, 0))

    return pl.pallas_call(
        body, name=name, grid=(T // tm,),
        in_specs=[row(1024), row(1024), row(512), pl.BlockSpec((tm, 512), lambda i: (i, 4)), full((1, 256)),
                  full((1, 128)), full((1024, 256)), full((1024, 128)), full((512, 128)), row(128), row(128)]
                 + [row(512)] * 4 + [row(256), row(128), row(D_MODEL)],
        out_specs=[row(IN_PAD), full((IN_PAD, D_MODEL)), full((1024, 256)), full((1024, 128)), full((512, 128)),
                   full((1, 256)), full((1, 128))],
        out_shape=[jax.ShapeDtypeStruct((T, IN_PAD), BF16), jax.ShapeDtypeStruct((IN_PAD, D_MODEL), F32),
                   jax.ShapeDtypeStruct((1024, 256), F32), jax.ShapeDtypeStruct((1024, 128), F32),
                   jax.ShapeDtypeStruct((512, 128), F32), jax.ShapeDtypeStruct((1, 256), F32),
                   jax.ShapeDtypeStruct((1, 128), F32)],
        scratch_shapes=[pltpu.VMEM((tm, 1024), BF16)],
        compiler_params=_cp(("arbitrary",)))(dq, dk, dv, proj, qnw, kvnw, wuq_t, wk_t, wv_t, cos, ss, *ret_grads,
                                             cqn, ckvn, h)


def _flash_fwd(q, k, v1, *, name, gather=None):
    T = q.shape[0]
    tq = min(T, 512)
    tk = tq
    nq = T // tq

    def body(q_ref, k_ref, v_ref, *rest):
        if gather is None:
            y_ref, lse_ref = rest
        else:
            x_ref, y_ref, lse_ref, g_ref, *sems = rest
            start, forward, finish = _gather_phases(x_ref, g_ref, *sems)
            pl.when((pl.program_id(0) == 0) & (pl.program_id(1) == 0))(start)
            pl.when((pl.program_id(0) == 1) & (pl.program_id(1) == 0))(forward)
        attend(q_ref, k_ref, v_ref, y_ref, lse_ref)
        if gather is not None:
            pl.when((pl.program_id(0) == 3) & (pl.program_id(1) == nq - 1))(finish)

    def attend(q_ref, k_ref, v_ref, y_ref, lse_ref):
        qi = pl.program_id(1)
        row = lax.broadcasted_iota(jnp.int32, (tq, tk), 0)
        col = lax.broadcasted_iota(jnp.int32, (tq, tk), 1)

        def step(kb, carry, masked):
            ks = pl.ds(pl.multiple_of(kb * tk, tk), tk)
            new = []
            for h in range(2):
                hs = slice(128 * h, 128 * h + 128)
                m, acc = carry[h]
                s = _dot_nt(q_ref[:, hs], k_ref[ks, hs])
                if masked:
                    s = jnp.where(col <= row, s, NEG)
                mn = jnp.maximum(m, jnp.max(s, axis=1, keepdims=True))
                p = jnp.exp((s - mn).astype(BF16))
                acc = jnp.exp(m - mn) * acc + _dot(p, v_ref[ks, hs])
                new.append((mn, acc))
            return tuple(new)

        def unrolled(j, c):
            for u in range(FLASH_UNROLL):
                c = step(FLASH_UNROLL * j + u, c, False)
            return c

        init = (jnp.full((tq, 1), NEG, F32), jnp.zeros((tq, 128), F32))
        carry = lax.fori_loop(0, qi // FLASH_UNROLL, unrolled, (init, init))
        carry = lax.fori_loop(FLASH_UNROLL * (qi // FLASH_UNROLL), qi, lambda kb, c: step(kb, c, False), carry)
        (ma, acca), (mb, accb) = step(qi, carry, True)
        lane = lax.broadcasted_iota(jnp.int32, (tq, 128), 1)
        la, lb = pltpu.roll(acca, 64, 1), pltpu.roll(accb, 64, 1)
        y_ref[...] = jnp.where(lane < 64, acca / la, accb / lb).astype(BF16)
        lse_ref[0, 0] = jnp.broadcast_to(ma + jnp.log(acca[:, 64:65]), (tq, 128)).T[0:1]
        lse_ref[1, 0] = jnp.broadcast_to(mb + jnp.log(accb[:, 0:1]), (tq, 128)).T[0:1]

    in_specs = [pl.BlockSpec((tq, 256), lambda p, i: (i, p)), pl.BlockSpec((T, 256), lambda p, i: (0, p)),
                pl.BlockSpec((T, 256), lambda p, i: (0, p))]
    out_specs = [pl.BlockSpec((tq, 128), lambda p, i: (i, p)), pl.BlockSpec((2, 1, 1, tq), lambda p, i: (p, i, 0, 0))]
    out_shape = [jax.ShapeDtypeStruct((T, MLA_WIDTH), BF16), jax.ShapeDtypeStruct((MLA_HEADS, nq, 1, tq), F32)]
    if gather is None:
        return pl.pallas_call(body, name=name, grid=(4, nq), in_specs=in_specs, out_specs=out_specs,
                              out_shape=out_shape, compiler_params=_cp(("parallel", "arbitrary")))(q, k, v1)
    return pl.pallas_call(
        body, name=name, grid=(4, nq), in_specs=in_specs + [ANY], out_specs=out_specs + [ANY],
        out_shape=out_shape + [jax.ShapeDtypeStruct((N_DEV,) + gather.shape, gather.dtype)],
        scratch_shapes=list(GATHER_SCRATCH),
        compiler_params=_cp(("arbitrary", "arbitrary")))(q, k, v1, gather)


def _flash_bwd(q, k, v, do, lse, delta, *, name, exchange=None):
    T = q.shape[0]
    tq = min(T, 512)
    tk = tq
    nq = T // tq

    def body(q_ref, k_ref, v_ref, do_ref, lse_ref, dl_ref, *rest):
        if exchange is None:
            backward(q_ref, k_ref, v_ref, do_ref, lse_ref, dl_ref, *rest)
        else:
            p_ref, dqt_ref, dk_ref, dv_ref, got_ref, *sems = rest
            start, finish = _exchange_phases(p_ref, got_ref, *sems)
            pl.when((pl.program_id(0) == 0) & (pl.program_id(1) == 0))(start)
            backward(q_ref, k_ref, v_ref, do_ref, lse_ref, dl_ref, dqt_ref, dk_ref, dv_ref)
            pl.when((pl.program_id(0) == 3) & (pl.program_id(1) == nq - 1))(finish)

    def backward(q_ref, k_ref, v_ref, do_ref, lse_ref, dl_ref, dqt_ref, dk_ref, dv_ref):
        kb = pl.program_id(1)

        @pl.when(kb == 0)
        def _():
            dqt_ref[...] = jnp.zeros_like(dqt_ref)
        krow = lax.broadcasted_iota(jnp.int32, (tk, tq), 0)
        qcol = lax.broadcasted_iota(jnp.int32, (tk, tq), 1)
        masks = _head_masks((tk, 128))
        vms = [(v_ref[:, 128 * h:128 * h + 128].astype(F32) * masks[h]).astype(BF16) for h in range(2)]

        def step(qi, carry, masked):
            qs = pl.ds(pl.multiple_of(qi * tq, tq), tq)
            dob = do_ref[qs, :]
            dof = dob.astype(F32)
            dks, dv_acc = list(carry[:2]), carry[2]
            for h in range(2):
                hs = slice(128 * h, 128 * h + 128)
                kh = k_ref[:, hs]
                qh = q_ref[qs, hs]
                st = _dot_nt(kh, qh)
                pt = jnp.exp((st - lse_ref[h, qi]).astype(BF16))
                if masked:
                    pt = jnp.where(krow <= qcol, pt, jnp.zeros_like(pt))
                dv_acc = dv_acc + _dot(pt, (dof * masks[h]).astype(BF16))
                dpt = _dot_nt(vms[h], dob)
                dst = pt * (dpt - dl_ref[h, qi]).astype(BF16)
                dks[h] = dks[h] + _dot(dst, qh)
                dqt_ref[qi, hs, :] += _dot_tn(kh, dst)
            return dks[0], dks[1], dv_acc

        zero = jnp.zeros((tk, 128), F32)
        carry = step(kb, (zero, zero, zero), True)

        def unrolled(j, c):
            for u in range(FLASH_BWD_UNROLL):
                c = step(kb + 1 + FLASH_BWD_UNROLL * j + u, c, False)
            return c

        trips = (nq - 1 - kb) // FLASH_BWD_UNROLL
        carry = lax.fori_loop(0, trips, unrolled, carry)
        dk0, dk1, dv_acc = lax.fori_loop(kb + 1 + FLASH_BWD_UNROLL * trips, nq, lambda qi, c: step(qi, c, False), carry)
        dk_ref[:, 0:128] = dk0
        dk_ref[:, 128:256] = dk1
        dv_ref[...] = dv_acc.astype(BF16)

    stat = pl.BlockSpec((2, nq, 1, tq), lambda p, j: (p, 0, 0, 0))
    in_specs = [pl.BlockSpec((T, 256), lambda p, j: (0, p)), pl.BlockSpec((tk, 256), lambda p, j: (j, p)),
                pl.BlockSpec((tk, 256), lambda p, j: (j, p)), pl.BlockSpec((T, 128), lambda p, j: (0, p)), stat, stat]
    out_specs = [pl.BlockSpec((None, nq, 256, tq), lambda p, j: (p, 0, 0, 0)),
                 pl.BlockSpec((tk, 256), lambda p, j: (j, p)), pl.BlockSpec((tk, 128), lambda p, j: (j, p))]
    out_shape = [jax.ShapeDtypeStruct((4, nq, 256, tq), F32), jax.ShapeDtypeStruct((T, 1024), F32),
                 jax.ShapeDtypeStruct((T, MLA_WIDTH), BF16)]
    if exchange is None:
        return pl.pallas_call(body, name=name, grid=(4, nq), in_specs=in_specs, out_specs=out_specs,
                              out_shape=out_shape,
                              compiler_params=_cp(("parallel", "arbitrary")))(q, k, v, do, lse, delta)
    return pl.pallas_call(
        body, name=name, grid=(4, nq), in_specs=in_specs + [ANY], out_specs=out_specs + [ANY],
        out_shape=out_shape + [jax.ShapeDtypeStruct(exchange.shape, exchange.dtype)],
        scratch_shapes=list(EXCHANGE_SCRATCH),
        compiler_params=_cp(("arbitrary", "arbitrary")))(q, k, v, do, lse, delta, exchange)


def _shift_down(x, n, prev8):
    r = pltpu.roll(x, n, 0)
    row = lax.broadcasted_iota(jnp.int32, prev8.shape, 0)
    first = jnp.where(row < n, pltpu.roll(prev8, n, 0), r[:8])
    if x.shape[0] == 8:
        return first
    return jnp.concatenate([first, r[8:]], axis=0)


def _shift_up(x, n, next8):
    tm = x.shape[0]
    r = pltpu.roll(x, tm - n, 0)
    row = lax.broadcasted_iota(jnp.int32, next8.shape, 0)
    last = jnp.where(row >= 8 - n, pltpu.roll(next8, 8 - n, 0), r[tm - 8:])
    return jnp.concatenate([r[:tm - 8], last], axis=0)


def _conv_pre(u, prev8, cw_ref, cb_ref):
    p1 = _shift_down(u, 1, prev8)
    p2 = _shift_down(u, 2, prev8)
    up = cb_ref[...] + cw_ref[0:1, :] * p2 + cw_ref[1:2, :] * p1 + cw_ref[2:3, :] * u
    return up, p1, p2


def _up_proj_conv(x, y_ret, y_mla, w_out, nw, w_up_t, cw, cb, *, name):
    T, K = x.shape
    tm = min(T, 256)

    def body(x_ref, yr_ref, ym_ref, wo_ref, nw_ref, w_ref, cw_ref, cb_ref, x1_ref, h_ref, u_ref, a_ref, carry_sc):
        @pl.when(pl.program_id(0) == 0)
        def _():
            carry_sc[...] = jnp.zeros_like(carry_sc)
        xv = x_ref[...] + _dot(jnp.concatenate([yr_ref[...], ym_ref[...]], axis=1), wo_ref[...])
        x1_ref[...] = xv
        h = (xv * lax.rsqrt(jnp.mean(xv * xv, axis=-1, keepdims=True) + EPS) * nw_ref[...]).astype(BF16)
        h_ref[...] = h
        for blk in range(2):
            ups = []
            for half in range(2):
                cs = slice((2 * blk + half) * FF_HALF, (2 * blk + half + 1) * FF_HALF)
                u = _dot_nt(h, w_ref[cs, :])
                u_ref[:, cs] = u
                prev = carry_sc[:, cs]
                ups.append(cb_ref[:, cs] + cw_ref[0:1, cs] * _shift_down(u, 2, prev)
                           + cw_ref[1:2, cs] * _shift_down(u, 1, prev) + cw_ref[2:3, cs] * u)
                carry_sc[:, cs] = u[tm - 8:]
            gate, val = ups
            a_ref[:, blk * FF_HALF:(blk + 1) * FF_HALF] = (gate * _sigmoid(gate) * val).astype(BF16)

    def full(shape):
        return pl.BlockSpec(shape, lambda i: (0, 0))

    return pl.pallas_call(
        body, name=name, grid=(T // tm,),
        in_specs=[pl.BlockSpec((tm, K), lambda i: (i, 0)), pl.BlockSpec((tm, RET_WIDTH), lambda i: (i, 0)),
                  pl.BlockSpec((tm, MLA_WIDTH), lambda i: (i, 0)), full(w_out.shape), full(nw.shape),
                  full(w_up_t.shape), full(cw.shape), full(cb.shape)],
        out_specs=[pl.BlockSpec((tm, K), lambda i: (i, 0)), pl.BlockSpec((tm, K), lambda i: (i, 0)),
                   pl.BlockSpec((tm, 2 * D_FF), lambda i: (i, 0)), pl.BlockSpec((tm, D_FF), lambda i: (i, 0))],
        out_shape=[jax.ShapeDtypeStruct((T, K), F32), jax.ShapeDtypeStruct((T, K), BF16),
                   jax.ShapeDtypeStruct((T, 2 * D_FF), F32), jax.ShapeDtypeStruct((T, D_FF), BF16)],
        scratch_shapes=[pltpu.VMEM((8, 2 * D_FF), F32)],
        compiler_params=_cp(("arbitrary",)))(x, y_ret, y_mla, w_out, nw, w_up_t, cw, cb)


def _conv_bwd(u, da, cw, cb, *, name):
    T = u.shape[0]
    tm = min(T, 512)
    W = 2 * FF_HALF
    nt = T // tm

    def body(u_ref, prev_ref, next_ref, da_ref, dan_ref, cw_ref, cb_ref, du_ref, dw0_ref, dw1_ref, dw2_ref, db_ref):
        i = pl.program_id(1)

        @pl.when(i == 0)
        def _():
            for r in (dw0_ref, dw1_ref, dw2_ref, db_ref):
                r[...] = jnp.zeros_like(r)

        def dpre(u, prev8, da):
            up, p1, p2 = _conv_pre(u, prev8, cw_ref, cb_ref)
            gate, val = up[:, :FF_HALF], up[:, FF_HALF:]
            sg = _sigmoid(gate)
            dgate = da * val * (sg * (1.0 + gate * (1.0 - sg)))
            dval = da * (gate * sg)
            return jnp.concatenate([dgate, dval], axis=1), p1, p2

        u = u_ref[...]
        prev = jnp.where(i > 0, prev_ref[...], 0.0)
        dup, p1, p2 = dpre(u, prev, da_ref[...])
        dupn, _, _ = dpre(next_ref[...], u[tm - 8:], dan_ref[...])
        dupn = jnp.where(i < nt - 1, dupn, 0.0)
        du = cw_ref[2:3, :] * dup + cw_ref[1:2, :] * _shift_up(dup, 1, dupn) + cw_ref[0:1, :] * _shift_up(dup, 2, dupn)
        du_ref[...] = du.astype(BF16)
        dw0_ref[...] += jnp.sum(dup * p2, axis=0, keepdims=True)
        dw1_ref[...] += jnp.sum(dup * p1, axis=0, keepdims=True)
        dw2_ref[...] += jnp.sum(dup * u, axis=0, keepdims=True)
        db_ref[...] += jnp.sum(dup, axis=0, keepdims=True)

    nxt = lambda j, i: (jnp.minimum((i + 1) * (tm // 8), T // 8 - 1), j)
    vec = pl.BlockSpec((1, W), lambda j, i: (0, j))
    return pl.pallas_call(
        body, name=name, grid=(2, nt),
        in_specs=[pl.BlockSpec((tm, W), lambda j, i: (i, j)),
                  pl.BlockSpec((8, W), lambda j, i: (jnp.maximum(i * (tm // 8) - 1, 0), j)),
                  pl.BlockSpec((8, W), nxt),
                  pl.BlockSpec((tm, FF_HALF), lambda j, i: (i, j)), pl.BlockSpec((8, FF_HALF), nxt),
                  pl.BlockSpec((3, W), lambda j, i: (0, j)), vec],
        out_specs=[pl.BlockSpec((tm, W), lambda j, i: (i, j)), vec, vec, vec, vec],
        out_shape=[jax.ShapeDtypeStruct((T, 2 * D_FF), BF16)] + [jax.ShapeDtypeStruct((1, 2 * D_FF), F32)] * 4,
        compiler_params=_cp(("parallel", "arbitrary")))(u, u, u, da, da, cw, cb)


def _sum_chips(slots, *, name):
    ns, R, C = slots.shape
    tr = _row_tile(R)

    def body(g_ref, o_ref):
        g = g_ref[0].astype(F32)
        for s in range(1, ns):
            g = g + g_ref[s].astype(F32)
        o_ref[...] = g

    return pl.pallas_call(
        body, name=name, grid=(R // tr,), in_specs=[pl.BlockSpec((ns, tr, C), lambda i: (0, i, 0))],
        out_specs=pl.BlockSpec((tr, C), lambda i: (i, 0)), out_shape=jax.ShapeDtypeStruct((R, C), F32),
        compiler_params=_cp(("parallel",)))(slots)


def _place():
    return lax.axis_index("x"), lax.axis_index("y"), lax.axis_index("c")


GATHER_SCRATCH = (pltpu.SemaphoreType.DMA((7,)), pltpu.SemaphoreType.DMA((7,)), pltpu.SemaphoreType.DMA)
EXCHANGE_SCRATCH = (pltpu.SemaphoreType.DMA((3,)), pltpu.SemaphoreType.DMA((3,)), pltpu.SemaphoreType.DMA)


def _gather_phases(x_ref, out_ref, send_sems, recv_sems, local_sem):
    x_, y_, c_ = _place()
    me, sibling = (x_, y_, c_), (x_, y_, 1 - c_)
    chips = [(1 - x_, y_), (x_, 1 - y_), (1 - x_, 1 - y_)]

    def slot(px, py, pc):
        return out_ref.at[4 * px + 2 * py + pc]

    def copy(k, block, to, src=None):
        return pltpu.make_async_remote_copy(
            src_ref=slot(*block) if src is None else src, dst_ref=slot(*block),
            send_sem=send_sems.at[k], recv_sem=recv_sems.at[k], device_id=to, device_id_type=MESH)

    def mine():
        return pltpu.make_async_copy(x_ref, slot(*me), local_sem)

    def first():
        return [copy(0, me, sibling, src=x_ref)] + [copy(1 + j, me, (*chip, c_), src=x_ref)
                                                     for j, chip in enumerate(chips)]

    def passed():
        return [copy(4 + j, (*chip, c_), sibling) for j, chip in enumerate(chips)]

    def start():
        mine().start()
        for cp in first():
            cp.start()

    def forward():
        fwd = passed()
        for j, chip in enumerate(chips):
            copy(1 + j, (*chip, c_), me).wait_recv()
            fwd[j].start()

    def finish():
        copy(0, sibling, me).wait_recv()
        for j, chip in enumerate(chips):
            copy(4 + j, (*chip, 1 - c_), me).wait_recv()
        for cp in first() + passed():
            cp.wait_send()
        mine().wait()

    return start, forward, finish


def _exchange_phases(p_ref, out_ref, send_sems, recv_sems, local_sem):
    x_, y_, c_ = _place()
    me_k = 2 * x_ + y_
    chips = [(1 - x_, y_), (x_, 1 - y_), (1 - x_, 1 - y_)]

    def local():
        return pltpu.make_async_copy(p_ref.at[me_k], out_ref.at[me_k], local_sem)

    def copy(j, src_k, dst_k, chip):
        return pltpu.make_async_remote_copy(
            src_ref=p_ref.at[src_k], dst_ref=out_ref.at[dst_k], send_sem=send_sems.at[j],
            recv_sem=recv_sems.at[j], device_id=(*chip, c_), device_id_type=MESH)

    def sends():
        return [copy(j, 2 * px + py, me_k, (px, py)) for j, (px, py) in enumerate(chips)]

    def start():
        local().start()
        for cp in sends():
            cp.start()

    def finish():
        for j, (px, py) in enumerate(chips):
            copy(j, me_k, 2 * px + py, (px, py)).wait_recv()
        for cp in sends():
            cp.wait_send()
        local().wait()

    return start, finish


def _all_gather(x, *, name):
    def body(x_ref, out_ref, send_sems, recv_sems, local_sem):
        for phase in _gather_phases(x_ref, out_ref, send_sems, recv_sems, local_sem):
            phase()

    spec = pl.BlockSpec(memory_space=pltpu.VMEM)
    return pl.pallas_call(
        body, name=name, out_shape=jax.ShapeDtypeStruct((N_DEV,) + x.shape, x.dtype),
        in_specs=[spec], out_specs=spec, scratch_shapes=list(GATHER_SCRATCH),
        compiler_params=pltpu.CompilerParams(vmem_limit_bytes=VMEM_LIMIT))(x)


def _small_rows():
    table, row = [], 0
    for n, size in SMALL_VECTORS:
        table.append((n, size, row))
        row += -(-size // PACK_COLS)
    return table


def _ff_chunk_source(c):
    block, off = divmod(c * 128, FF_HALF)
    return (0, 2, 1, 3)[block] * FF_HALF + off


def _pack_small(parts, *, name):
    table = _small_rows()

    def body(*refs):
        out = refs[-1]
        out[...] = jnp.zeros_like(out)
        for ref, (n, size, row) in zip(refs, table):
            if size != 2 * D_FF:
                out[row:row + 1, 0:size] = ref[...]
                continue
            for c in range(size // 128):
                src = _ff_chunk_source(c)
                r, lane = divmod(c * 128, PACK_COLS)
                out[row + r:row + r + 1, lane:lane + 128] = ref[:, src:src + 128]

    return pl.pallas_call(body, name=name, out_shape=jax.ShapeDtypeStruct((SMALL_ROWS, PACK_COLS), F32))(
        *[parts[n] for n, _, _ in table])


def _sum_small(g, *, name):
    table = _small_rows()
    shapes = [(n, size) for n, size, _ in table if not n.startswith("conv_w")]
    shapes.insert(7, ("conv_w", 2 * D_FF))

    def body(g_ref, *outs):
        def total(row, width):
            acc = g_ref[0, row:row + 1, 0:width]
            for d in range(1, N_DEV):
                acc = acc + g_ref[d, row:row + 1, 0:width]
            return acc

        out_of = {n: o for (n, _), o in zip(shapes, outs)}
        for n, size, row in table:
            o, j = (out_of["conv_w"], int(n[-1])) if n.startswith("conv_w") else (out_of[n], 0)
            for i in range(-(-size // PACK_COLS)):
                width = min(PACK_COLS, size - PACK_COLS * i)
                o[j:j + 1, PACK_COLS * i:PACK_COLS * i + width] = total(row + i, width)

    out_shape = [jax.ShapeDtypeStruct((3 if n == "conv_w" else 1, size), F32) for n, size in shapes]
    res = pl.pallas_call(body, name=name, out_shape=out_shape)(g)
    return {n: r for (n, _), r in zip(shapes, res)}


def _adamw_multi(ws, ms, vs, gs, *, name):
    k = len(ws)

    def body(*refs):
        w_refs, m_refs, v_refs, g_refs = (refs[i * k:(i + 1) * k] for i in range(4))
        outs = refs[4 * k:]
        for i in range(k):
            g = g_refs[i][...]
            mn = ADAM_B1 * m_refs[i][...] + (1.0 - ADAM_B1) * g
            vn = ADAM_B2 * v_refs[i][...] + (1.0 - ADAM_B2) * (g * g)
            m_hat = mn / (1.0 - ADAM_B1 ** ADAM_STEP)
            v_hat = vn / (1.0 - ADAM_B2 ** ADAM_STEP)
            outs[i][...] = g
            outs[k + i][...] = -ADAM_LR * (m_hat / (jnp.sqrt(v_hat) + ADAM_EPS) + ADAM_WD * w_refs[i][...])
            outs[2 * k + i][...] = mn
            outs[3 * k + i][...] = vn

    out_shape = [jax.ShapeDtypeStruct(w.shape, F32) for _ in range(4) for w in ws]
    res = pl.pallas_call(body, name=name, out_shape=out_shape, compiler_params=_cp())(*ws, *ms, *vs, *gs)
    return [res[i * k:(i + 1) * k] for i in range(4)]


SWAP_SCRATCH = (pltpu.SemaphoreType.DMA((4,)), pltpu.SemaphoreType.DMA((4,)))


def _swap_phases(g_ref, out_ref, send_sems, recv_sems):
    x_, y_, c_ = _place()

    def copies():
        return [pltpu.make_async_remote_copy(src_ref=g_ref.at[k, 1 - c_], dst_ref=out_ref.at[k],
                                             send_sem=send_sems.at[k], recv_sem=recv_sems.at[k],
                                             device_id=(x_, y_, 1 - c_), device_id_type=MESH) for k in range(4)]

    def start():
        for cp in copies():
            cp.start()

    def finish():
        for cp in copies():
            cp.wait()

    return start, finish


def _swap_sibling(g, *, name):
    def body(g_ref, out_ref, send_sems, recv_sems):
        for phase in _swap_phases(g_ref, out_ref, send_sems, recv_sems):
            phase()

    return pl.pallas_call(
        body, name=name, out_shape=jax.ShapeDtypeStruct((4,) + g.shape[2:], g.dtype), in_specs=[ANY], out_specs=ANY,
        scratch_shapes=list(SWAP_SCRATCH))(g)


def _row_tile(R):
    for cand in (256, 400, 200):
        if R % cand == 0:
            return cand
    return R


def _add_own(g, b, *, name, out_dtype):
    n, _, R, C = g.shape
    tr = _row_tile(R)

    def body(c_ref, g_ref, b_ref, o_ref):
        del c_ref
        o_ref[...] = (g_ref[...] + b_ref[...]).astype(out_dtype)

    blk = pl.BlockSpec((None, tr, C), lambda s, i, c: (s, i, 0))
    grid_spec = pltpu.PrefetchScalarGridSpec(
        num_scalar_prefetch=1, grid=(n, R // tr),
        in_specs=[pl.BlockSpec((None, None, tr, C), lambda s, i, c: (s, c[0], i, 0)), blk], out_specs=blk)
    core = jnp.reshape(lax.axis_index("c"), (1,)).astype(jnp.int32)
    return pl.pallas_call(body, name=name, grid_spec=grid_spec, out_shape=jax.ShapeDtypeStruct(b.shape, out_dtype),
                          compiler_params=_cp(("parallel", "parallel")))(core, g, b)


def _pack_local(parts, group, tail=None):
    table, rows = group
    segs = []
    for n, r, rp, tr in table:
        w = parts[n].T if tr else parts[n]
        segs.append(jnp.pad(w.reshape(r, PACK_COLS), ((0, rp - r), (0, 0))))
    spare = rows - sum(rp for _, _, rp, _ in table)
    segs.append(jnp.zeros((spare, PACK_COLS), segs[0].dtype) if tail is None else tail)
    return jnp.concatenate(segs, axis=0)


CONV_W_BITS = 2 * 3 * 704
SPARE_EARLY = 16


def _conv_w_as_rows(conv_w_shard):
    bits = lax.bitcast_convert_type(conv_w_shard.reshape(-1), BF16).reshape(-1)
    return jnp.pad(bits, (0, SPARE_EARLY * PACK_COLS - CONV_W_BITS)).reshape(SPARE_EARLY, PACK_COLS)


def _conv_w_from_rows(gathered):
    bits = gathered[:, EARLY[1] - SPARE_EARLY:].reshape(N_DEV, -1)[:, :CONV_W_BITS].reshape(N_DEV, 3 * 704, 2)
    w = lax.bitcast_convert_type(bits, F32).reshape(N_DEV, 3, 704)
    return w.transpose(1, 0, 2).reshape(3, 2 * D_FF)


def _unpack_local(packed, like, group):
    out, off = {}, 0
    for n, r, rp, tr in group[0]:
        rows, cols = like[n].shape
        seg = packed[off:off + r]
        out[n] = (seg.reshape(cols, rows).T if tr else seg)[None]
        off += rp
    return out


def _segments(g, group):
    out, off = {}, 0
    for n, r, rp, _ in group[0]:
        out[n] = g[:, off:off + r]
        off += rp
    return out


def _pack_grads(parts, group):
    table, rows = group
    segs = [jnp.pad(parts[n], ((0, 0), (0, rp - parts[n].shape[1]), (0, 0))) for n, _, rp, _ in table]
    segs.append(jnp.zeros((N_DEV, rows - sum(rp for _, _, rp, _ in table), PACK_COLS), F32))
    return jnp.concatenate(segs, axis=1)


def _owner_rows_early(g):
    g_in = jnp.concatenate([g["w_in_t"][:2432], g["w_in_t"][2496:2528]], axis=0).reshape(N_DEV, 308, PACK_COLS)
    g_uq = g["w_uq_t"].reshape(N_DEV, 128, MLA_Q_RANK)[:, :96].reshape(N_DEV, 24, PACK_COLS)
    g_ukv = jnp.concatenate([g["w_k_t"].reshape(N_DEV, 128, MLA_KV_RANK)[:, :64],
                             g["w_v_t"].reshape(N_DEV, 64, MLA_KV_RANK)], axis=1).reshape(N_DEV, 16, PACK_COLS)
    return dict(w_in=g_in, w_uq=g_uq, w_ukv=g_ukv)


def _owner_rows_late(g):
    g_up = g["w_up_t"].reshape(2, 2, 2, 704, PACK_COLS).swapaxes(0, 1).reshape(N_DEV, 704, PACK_COLS)
    return dict(w_out=g["w_out"].reshape(N_DEV, 128, PACK_COLS), w_up=g_up,
                w_down=g["w_down"].reshape(N_DEV, 352, PACK_COLS))


def _reduce_to_pairs(gp, *, name):
    gp = gp.reshape(4, 2, gp.shape[1], PACK_COLS)
    return _add_own(gp, _swap_sibling(gp, name=name + "_swap"), out_dtype=BF16, name=name + "_sum")


def _interleave_ff(w):
    g, v = w[..., :D_FF], w[..., D_FF:]
    return jnp.concatenate([g[..., :FF_HALF], v[..., :FF_HALF], g[..., FF_HALF:], v[..., FF_HALF:]], axis=-1)


def _rope_tables(pos):
    p = pos.astype(F32)[:, None]
    inv_r = ROPE_BASE ** (-jnp.arange(0, RET_HEAD_DIM, 2, dtype=F32) / RET_HEAD_DIM)
    ang = p * jnp.tile(inv_r, 4)
    sign_r = jnp.tile(jnp.concatenate([-jnp.ones((32,), F32), jnp.ones((32,), F32)]), 2)
    cos_r, ss_r = jnp.cos(ang), jnp.sin(ang) * sign_r
    inv_m = ROPE_BASE ** (-jnp.arange(0, MLA_ROPE, 2, dtype=F32) / MLA_ROPE)
    ang = p * jnp.concatenate([jnp.zeros((64,), F32), inv_m, inv_m, jnp.zeros((32,), F32)])
    sign_m = jnp.concatenate([jnp.zeros((64,), F32), -jnp.ones((16,), F32), jnp.ones((16,), F32), jnp.zeros((32,), F32)])
    cos_m, ss_m = jnp.cos(ang), jnp.sin(ang) * sign_m
    return cos_r, ss_r, cos_m, ss_m


def _prep_early(gathered):
    seg = _segments(gathered, EARLY)
    w_in_t = seg["w_in"].reshape(IN_WIDTH, D_MODEL)
    z = lambda n: jnp.zeros((n, D_MODEL), BF16)
    w_in_t = jnp.concatenate([w_in_t[:2432], z(64), w_in_t[2432:2464], z(32)], axis=0)
    w_uq_t = jnp.pad(seg["w_uq"].reshape(MLA_HEADS, 96, MLA_Q_RANK), ((0, 0), (0, 32), (0, 0))).reshape(1024, MLA_Q_RANK)
    ukv = seg["w_ukv"].reshape(MLA_HEADS, 128, MLA_KV_RANK)
    w_k_t = jnp.pad(ukv[:, :64], ((0, 0), (0, 64), (0, 0))).reshape(1024, MLA_KV_RANK)
    w_v_t = ukv[:, 64:].reshape(512, MLA_KV_RANK)
    return dict(w_in_t=w_in_t, w_uq_t=w_uq_t, w_k_t=w_k_t, w_v_t=w_v_t)


def _prep_late(gathered):
    seg = _segments(gathered, LATE)
    w_up_t = seg["w_up"].reshape(2, 2, 2, 704, D_MODEL).swapaxes(0, 1).reshape(2 * D_FF, D_MODEL)
    return dict(w_out=seg["w_out"].reshape(1024, D_MODEL), w_up_t=w_up_t, w_down=seg["w_down"].reshape(D_FF, D_MODEL))


def _local_step(x, pos, tgt, early, sm, late):
    dist = not isinstance(late, dict)
    cos_r, ss_r, cos_m, ss_m = _rope_tables(pos)
    tabs = _ret_tables()

    if dist:
        h, gathered = _rmsnorm_fwd(x, sm["attn_norm_w"], gather=early, name="attn_norm")
        W = _prep_early(gathered)
        sm = {**sm, "conv_w": _interleave_ff(_conv_w_from_rows(gathered))}
    else:
        h = _rmsnorm_fwd(x, sm["attn_norm_w"], name="attn_norm")
        W = early
    proj = _mm_nt(h, W["w_in_t"], name="in_proj")
    y_ret, o_ret, qr, kr = _ret_fwd(proj, cos_r, ss_r, tabs, sm["ret_gn_w"], name="ret_fwd")
    q, k, v1, cqn, ckvn = _mla_prep_fwd(proj, sm["mla_q_norm_w"], sm["mla_kv_norm_w"], W["w_uq_t"], W["w_k_t"],
                                       W["w_v_t"], cos_m, ss_m, name="mla_prep")
    T = x.shape[0]
    tq = min(T, 512)
    if dist:
        y_mla, lse, gathered = _flash_fwd(q, k, v1, gather=late, name="mla_attn")
        W = {**W, **_prep_late(gathered)}
    else:
        y_mla, lse = _flash_fwd(q, k, v1, name="mla_attn")
        W = {**W, **late}
    mixed = (y_ret, y_mla)
    x1, h2, u, a = _up_proj_conv(x, y_ret, y_mla, W["w_out"], sm["ffn_norm_w"], W["w_up_t"], sm["conv_w"],
                                 sm["conv_b"], name="out_proj_ffn_up_conv")
    loss, dx2, dx2b, d_final = _down_proj_loss(a, W["w_down"], x1, tgt, sm["final_norm_w"], name="down_proj_loss")

    g = {}
    g["w_down"] = _mm_tn(a, dx2b, name="dw_down")
    da = _mm_nt(dx2b, W["w_down"], name="d_act")
    du, dcw0, dcw1, dcw2, dcb = _conv_bwd(u, da, sm["conv_w"], sm["conv_b"], name="conv_bwd")
    g["w_up_t"] = _mm_tn(du, h2, name="dw_up")
    dx1, d_ffn = _mm_norm_bwd(du, W["w_up_t"], x1, sm["ffn_norm_w"], dx2, name="d_h2_ffn_norm_bwd")

    g["w_out"] = _mm_tn(mixed, dx1, name="dw_out")
    do_ret, dg, do_mla, delta, d_gn = _mix_bwd(dx1, W["w_out"], o_ret, proj, y_mla, sm["ret_gn_w"], name="d_mixed_mix_bwd")
    drq = _ret_bwd_dq(kr, proj, do_ret, cos_r, ss_r, tabs, name="ret_bwd_dq")
    delta_r = delta.reshape(MLA_HEADS, T // tq, 1, tq)
    if dist:
        gl = _pack_grads(_owner_rows_late(g), LATE).reshape(4, 2, LATE[1], PACK_COLS)
        drk, drv, theirs = _ret_bwd_dkv(qr, kr, proj, do_ret, cos_r, ss_r, tabs, swap=gl, name="ret_bwd_dkv")
        pair = _add_own(gl, theirs, out_dtype=BF16, name="grad_late_sum")
        dqt, dk, dv, slots_late = _flash_bwd(q, k, v1, do_mla, lse, delta_r, exchange=pair, name="mla_attn_bwd")
    else:
        drk, drv = _ret_bwd_dkv(qr, kr, proj, do_ret, cos_r, ss_r, tabs, name="ret_bwd_dkv")
        dqt, dk, dv = _flash_bwd(q, k, v1, do_mla, lse, delta_r, name="mla_attn_bwd")
        slots_late = None
    dq = dqt.transpose(1, 3, 0, 2).reshape(T, MLA_HEADS * 128)
    dproj, g["w_in_t"], g["w_uq_t"], g["w_k_t"], g["w_v_t"], d_qn, d_kvn = _mla_prep_bwd(
        dq, dk, dv, proj, sm["mla_q_norm_w"], sm["mla_kv_norm_w"], W["w_uq_t"], W["w_k_t"], W["w_v_t"], cos_m, ss_m,
        (drq, drk, drv, dg), cqn, ckvn, h, name="mla_prep_bwd")
    if dist:
        pair = _reduce_to_pairs(_pack_grads(_owner_rows_early(g), EARLY), name="grad_early")
        grad_x, d_attn, slots_early = _mm_norm_bwd(dproj, W["w_in_t"], x, sm["attn_norm_w"], dx1, exchange=pair,
                                                   name="d_h_attn_norm_bwd")
    else:
        grad_x, d_attn = _mm_norm_bwd(dproj, W["w_in_t"], x, sm["attn_norm_w"], dx1, name="d_h_attn_norm_bwd")
        slots_early = None

    small = dict(attn_norm_w=d_attn, ret_gn_w=d_gn, mla_q_norm_w=d_qn, mla_kv_norm_w=d_kvn, ffn_norm_w=d_ffn,
                 conv_b=dcb, final_norm_w=d_final, conv_w0=dcw0, conv_w1=dcw1, conv_w2=dcw2, loss=loss)
    return loss, grad_x, g, small, slots_early, slots_late


def kernel(x, positions, attn_norm_w, w_in, ret_gn_w, mla_q_norm_w, w_uq, mla_kv_norm_w, w_ukv, w_out, ffn_norm_w, w_up, conv_w, conv_b, w_down, final_norm_w, loss_target, m_attn_norm_w, m_w_in, m_ret_gn_w, m_mla_q_norm_w, m_w_uq, m_mla_kv_norm_w, m_w_ukv, m_w_out, m_ffn_norm_w, m_w_up, m_conv_w, m_conv_b, m_w_down, m_final_norm_w, v_attn_norm_w, v_w_in, v_ret_gn_w, v_mla_q_norm_w, v_w_uq, v_mla_kv_norm_w, v_w_ukv, v_w_out, v_ffn_norm_w, v_w_up, v_conv_w, v_conv_b, v_w_down, v_final_norm_w):
    a = dict(locals())
    x_, y_, c_ = _place()
    dev = 4 * x_ + 2 * y_ + c_

    shard = {n: a[n][0] for n in BIG_NAMES}
    shard16 = {n: w.astype(BF16) for n, w in shard.items()}
    sm = dict(attn_norm_w=attn_norm_w, ret_gn_w=ret_gn_w, mla_q_norm_w=mla_q_norm_w, mla_kv_norm_w=mla_kv_norm_w,
              ffn_norm_w=ffn_norm_w, final_norm_w=final_norm_w.reshape(1, D_MODEL), conv_b=_interleave_ff(conv_b))

    loss, grad_x, _, gs, slots_early, slots_late = _local_step(
        x[0], positions[0], loss_target[0], _pack_local(shard16, EARLY, tail=_conv_w_as_rows(conv_w[0])), sm,
        _pack_local(shard16, LATE))

    big = [{}, {}, {}, {}]
    for group, slots, tag, calls in ((EARLY, slots_early, "early", (("w_in", "w_uq", "w_ukv"),)),
                                     (LATE, slots_late, "late", (("w_out", "w_down"), ("w_up",)))):
        grads = _unpack_local(_sum_chips(slots, name="grad_sum_" + tag), shard, group)
        for names_c in calls:
            res = _adamw_multi([shard[n] for n in names_c], [a["m_" + n][0] for n in names_c],
                               [a["v_" + n][0] for n in names_c], [grads[n][0] for n in names_c],
                               name="adamw_" + "_".join(names_c))
            for kind in range(4):
                for n, r in zip(names_c, res[kind]):
                    big[kind][n] = r[None]

    packed = _pack_small(gs, name="pack_small_grads")
    tot = _sum_small(_all_gather(packed, name="gather_small_grads"), name="sum_small_grads")
    loss_out = tot["loss"][0, 0]
    g_cw = lax.dynamic_slice_in_dim(tot["conv_w"], dev * 704, 704, axis=1)

    def rows_of(prefix):
        return [a[prefix + n].reshape(1, size) for n, size in SMALL]

    sml = _adamw_multi(rows_of("") + [conv_w[0]], rows_of("m_") + [m_conv_w[0]], rows_of("v_") + [v_conv_w[0]],
                       [tot[n] for n, _ in SMALL] + [g_cw], name="adamw_small")
    cwo = [kind[-1] for kind in sml]

    def small_of(kind, n):
        return sml[kind][[nm for nm, _ in SMALL].index(n)].reshape(a[n].shape)

    names = ['attn_norm_w', 'w_in', 'ret_gn_w', 'mla_q_norm_w', 'w_uq', 'mla_kv_norm_w', 'w_ukv', 'w_out',
             'ffn_norm_w', 'w_up', 'conv_w', 'conv_b', 'w_down', 'final_norm_w']
    outs = [loss_out, grad_x[None]]
    for kind in range(4):
        for n in names:
            if n == "conv_w":
                outs.append(cwo[kind][None])
            elif n in big[kind]:
                outs.append(big[kind][n])
            else:
                outs.append(small_of(kind, n))
    return tuple(outs)
```

```python
import jax
import jax.numpy as jnp
from jax import lax
from jax.experimental import pallas as pl
from jax.experimental.pallas import tpu as pltpu

F32 = jnp.float32
BF16 = jnp.bfloat16
MESH = pl.DeviceIdType.MESH
ANY = pl.BlockSpec(memory_space=pl.ANY)

D_MODEL = 1024
RET_HEADS = 8
RET_HEAD_DIM = 64
RET_WIDTH = 512
RET_CHUNK = 128
MLA_HEADS = 8
MLA_NOPE = 64
MLA_ROPE = 32
MLA_V = 64
MLA_Q_RANK = 256
MLA_KV_RANK = 128
MLA_WIDTH = 512
IN_WIDTH = 2464
IN_PAD = 2560
D_FF = 2816
FF_HALF = 1408
ROPE_BASE = 10000.0
EPS = 1e-6
SCALE = float((MLA_NOPE + MLA_ROPE) ** -0.5)
K_SCALE = 0.125
N_DEV = 8

ADAM_LR = 0.001
ADAM_B1 = 0.9
ADAM_B2 = 0.999
ADAM_EPS = 1e-08
ADAM_WD = 0.01
ADAM_STEP = 10

VMEM_LIMIT = 56 * 1024 * 1024
MM_BUDGET = 40 * 1024 * 1024
NEG = -1e30
FLASH_UNROLL = 4
FLASH_BWD_UNROLL = 3

PACK_COLS = 1024
EARLY = ((("w_in", 308, 320, True), ("w_uq", 24, 32, True), ("w_ukv", 16, 16, True)), 384)
LATE = ((("w_out", 128, 128, False), ("w_up", 704, 704, True), ("w_down", 352, 352, False)), 1200)
BIG_NAMES = ("w_in", "w_uq", "w_ukv", "w_out", "w_up", "w_down")
SMALL = (("attn_norm_w", 1024), ("ret_gn_w", 512), ("mla_q_norm_w", 256), ("mla_kv_norm_w", 128),
         ("ffn_norm_w", 1024), ("conv_b", 5632), ("final_norm_w", 1024))
SMALL_VECTORS = SMALL + (("conv_w0", 5632), ("conv_w1", 5632), ("conv_w2", 5632), ("loss", 128))
SMALL_ROWS = 32


def _cp(sem=None, vmem=VMEM_LIMIT):
    return pltpu.CompilerParams(dimension_semantics=sem, vmem_limit_bytes=vmem)


def _dot(a, b):
    return jnp.dot(a, b, preferred_element_type=F32)


def _dot_nt(a, b):
    return lax.dot_general(a, b, (((1,), (1,)), ((), ())), preferred_element_type=F32)


def _dot_tn(a, b):
    return lax.dot_general(a, b, (((0,), (0,)), ((), ())), preferred_element_type=F32)


def _sigmoid(x):
    return 0.5 * jnp.tanh(0.5 * x) + 0.5


def _partner(x, half, period):
    n = x.shape[-1]
    lane = lax.broadcasted_iota(jnp.int32, x.shape, 1)
    return jnp.where((lane % period) < half, pltpu.roll(x, n - half, 1), pltpu.roll(x, half, 1))


def _rope(x, cos, ss, half, period):
    return x * cos + _partner(x, half, period) * ss


def _rope_t(dy, cos, ss, half, period):
    return dy * cos - _partner(dy, half, period) * ss


def _head_masks(shape):
    lane = lax.broadcasted_iota(jnp.int32, shape, 1)
    m0 = (lane < 64).astype(F32)
    return m0, 1.0 - m0


def _mm_nt(a, b, *, name):
    M, K = a.shape
    N = b.shape[0]
    per_row = 2 * (K * a.dtype.itemsize + N * 4)
    tm = 128
    for cand in (512, 256):
        if M % cand == 0 and cand * per_row + 4 * K * N <= MM_BUDGET:
            tm = cand
            break
    tm = min(tm, M)

    def body(a_ref, b_ref, o_ref):
        o_ref[...] = _dot_nt(a_ref[...], b_ref[...])

    return pl.pallas_call(
        body, name=name, grid=(M // tm,),
        in_specs=[pl.BlockSpec((tm, K), lambda i: (i, 0)), pl.BlockSpec(b.shape, lambda i: (0, 0))],
        out_specs=pl.BlockSpec((tm, N), lambda i: (i, 0)), out_shape=jax.ShapeDtypeStruct((M, N), F32),
        compiler_params=_cp(("parallel",)))(a, b)


def _mm_tn(a, b, *, name):
    T, M = a.shape
    N = b.shape[1]
    tk = min(T, 512)

    def tile(n):
        for cand in (1408, 1280):
            if n > 1408 and n % cand == 0:
                return cand
        return n

    tm, tn = tile(M), tile(N)
    nk = T // tk

    def body(a_ref, b_ref, o_ref):
        @pl.when(pl.program_id(2) == 0)
        def _():
            o_ref[...] = jnp.zeros_like(o_ref)
        o_ref[...] += _dot_tn(a_ref[...], b_ref[...])

    return pl.pallas_call(
        body, name=name, grid=(M // tm, N // tn, nk),
        in_specs=[pl.BlockSpec((tk, tm), lambda i, j, k: (k, i)), pl.BlockSpec((tk, tn), lambda i, j, k: (k, j))],
        out_specs=pl.BlockSpec((tm, tn), lambda i, j, k: (i, j)),
        out_shape=jax.ShapeDtypeStruct((M, N), F32),
        compiler_params=_cp(("parallel", "parallel", "arbitrary")))(a, b)


def _rmsnorm_fwd(x, w, *, name, gather=None):
    T, D = x.shape
    tm = min(T, 1024)
    n = T // tm

    def body(x_ref, w_ref, *rest):
        if gather is not None:
            s_ref, o_ref, g_ref, *sems = rest
            start, forward, finish = _gather_phases(s_ref, g_ref, *sems)
            pl.when(pl.program_id(0) == 0)(start)
            pl.when(pl.program_id(0) == n // 2)(forward)
        else:
            o_ref, = rest
        xv = x_ref[...]
        r = lax.rsqrt(jnp.mean(xv * xv, axis=-1, keepdims=True) + EPS)
        o_ref[...] = (xv * r * w_ref[...]).astype(BF16)
        if gather is not None:
            pl.when(pl.program_id(0) == n - 1)(finish)

    in_specs = [pl.BlockSpec((tm, D), lambda i: (i, 0)), pl.BlockSpec((1, D), lambda i: (0, 0))]
    out_spec = pl.BlockSpec((tm, D), lambda i: (i, 0))
    out_shape = jax.ShapeDtypeStruct((T, D), BF16)
    if gather is None:
        return pl.pallas_call(body, name=name, grid=(n,), in_specs=in_specs, out_specs=out_spec, out_shape=out_shape,
                              compiler_params=_cp(("parallel",)))(x, w)
    return pl.pallas_call(
        body, name=name, grid=(n,), in_specs=in_specs + [ANY], out_specs=[out_spec, ANY],
        out_shape=[out_shape, jax.ShapeDtypeStruct((N_DEV,) + gather.shape, gather.dtype)],
        scratch_shapes=list(GATHER_SCRATCH), compiler_params=_cp(("arbitrary",)))(x, w, gather)


def _mm_norm_bwd(a, b, x, w, dres, *, name, exchange=None, left=None):
    T, K = a.shape
    D = b.shape[1]
    tm = min(T, 256 if K > 4096 else 512)
    n = T // tm
    n_left = 0 if left is None else len(left)

    def body(a_ref, b_ref, x_ref, w_ref, dr_ref, *rest):
        if exchange is not None:
            p_ref, dx_ref, dw_ref, got_ref, *sems = rest
            start, finish = _exchange_phases(p_ref, got_ref, *sems)
            pl.when(pl.program_id(0) == 0)(start)
        elif left is not None:
            left_refs, (dx_ref, dw_ref, gw_ref) = rest[:n_left], rest[n_left:]
        else:
            dx_ref, dw_ref = rest

        @pl.when(pl.program_id(0) == 0)
        def _():
            dw_ref[...] = jnp.zeros_like(dw_ref)
            if left is not None:
                gw_ref[...] = jnp.zeros_like(gw_ref)
        dh = _dot(a_ref[...], b_ref[...])
        xv = x_ref[...]
        r = lax.rsqrt(jnp.mean(xv * xv, axis=-1, keepdims=True) + EPS)
        xh = xv * r
        g = dh * w_ref[...]
        dx = dr_ref[...] + r * (g - xh * jnp.mean(g * xh, axis=-1, keepdims=True))
        dx_ref[...] = dx
        dw_ref[...] += jnp.sum(dh * xh, axis=0, keepdims=True)
        if left is not None:
            gw_ref[...] += _dot_tn(jnp.concatenate([r_[...] for r_ in left_refs], axis=1), dx.astype(BF16))
        if exchange is not None:
            pl.when(pl.program_id(0) == n - 1)(finish)

    row = pl.BlockSpec((tm, D), lambda i: (i, 0))
    vec = pl.BlockSpec((1, D), lambda i: (0, 0))
    in_specs = [pl.BlockSpec((tm, K), lambda i: (i, 0)), pl.BlockSpec((K, D), lambda i: (0, 0)), row, vec, row]
    out_shape = [jax.ShapeDtypeStruct((T, D), F32), jax.ShapeDtypeStruct((1, D), F32)]
    if left is not None:
        m = sum(p.shape[1] for p in left)
        return pl.pallas_call(
            body, name=name, grid=(n,),
            in_specs=in_specs + [pl.BlockSpec((tm, p.shape[1]), lambda i: (i, 0)) for p in left],
            out_specs=[row, vec, pl.BlockSpec((m, D), lambda i: (0, 0))],
            out_shape=out_shape + [jax.ShapeDtypeStruct((m, D), F32)],
            compiler_params=_cp(("arbitrary",)))(a, b, x, w, dres, *left)
    if exchange is None:
        return pl.pallas_call(body, name=name, grid=(n,), in_specs=in_specs, out_specs=[row, vec], out_shape=out_shape,
                              compiler_params=_cp(("arbitrary",)))(a, b, x, w, dres)
    return pl.pallas_call(
        body, name=name, grid=(n,), in_specs=in_specs + [ANY], out_specs=[row, vec, ANY],
        out_shape=out_shape + [jax.ShapeDtypeStruct(exchange.shape, exchange.dtype)],
        scratch_shapes=list(EXCHANGE_SCRATCH), compiler_params=_cp(("arbitrary",)))(a, b, x, w, dres, exchange)


def _down_proj_loss(a, w_down, x1, tgt, w, *, name):
    T, D = x1.shape
    K = a.shape[1]
    tm = min(T, 512)

    def body(a_ref, b_ref, x_ref, t_ref, w_ref, loss_ref, dx_ref, dxb_ref, dw_ref):
        @pl.when(pl.program_id(0) == 0)
        def _():
            dw_ref[...] = jnp.zeros_like(dw_ref)
            loss_ref[...] = jnp.zeros_like(loss_ref)
        xv = x_ref[...] + _dot(a_ref[...], b_ref[...])
        wv = w_ref[...]
        r = lax.rsqrt(jnp.mean(xv * xv, axis=-1, keepdims=True) + EPS)
        xh = xv * r
        e = xh * wv - t_ref[...]
        part = 0.5 * jnp.sum(jnp.mean(e * e, axis=-1, keepdims=True), axis=0, keepdims=True)
        loss_ref[...] += jnp.broadcast_to(part, loss_ref.shape)
        dy = e * (1.0 / D)
        g = dy * wv
        dx = r * (g - xh * jnp.mean(g * xh, axis=-1, keepdims=True))
        dx_ref[...] = dx
        dxb_ref[...] = dx.astype(BF16)
        dw_ref[...] += jnp.sum(dy * xh, axis=0, keepdims=True)

    row = pl.BlockSpec((tm, D), lambda i: (i, 0))
    vec = pl.BlockSpec((1, D), lambda i: (0, 0))
    return pl.pallas_call(
        body, name=name, grid=(T // tm,),
        in_specs=[pl.BlockSpec((tm, K), lambda i: (i, 0)), pl.BlockSpec((K, D), lambda i: (0, 0)), row, row, vec],
        out_specs=[pl.BlockSpec((1, 128), lambda i: (0, 0)), row, row, vec],
        out_shape=[jax.ShapeDtypeStruct((1, 128), F32), jax.ShapeDtypeStruct((T, D), F32),
                   jax.ShapeDtypeStruct((T, D), BF16), jax.ShapeDtypeStruct((1, D), F32)],
        compiler_params=_cp(("arbitrary",)))(a, w_down, x1, tgt, w)


def _ret_tables():
    C = RET_CHUNK
    h = jnp.arange(RET_HEADS, dtype=F32)
    log_gamma = jnp.log1p(-jnp.power(2.0, -5.0 - h))
    idx = jnp.arange(C, dtype=F32)
    diff = idx[:, None] - idx[None, :]
    dm = jnp.where(diff >= 0, jnp.exp(log_gamma[:, None, None] * jnp.maximum(diff, 0.0)), 0.0)
    dm = dm.reshape(4, 2 * C, C)
    lane_head = jnp.repeat(jnp.arange(RET_HEADS).reshape(4, 2), 64, axis=1)
    lg = log_gamma[lane_head]
    xi = jnp.exp(lg[:, None, :] * (idx[None, :, None] + 1.0))
    zeta = jnp.exp(lg[:, None, :] * (C - 1.0 - idx[None, :, None]))
    blk = (jnp.arange(128)[:, None] // 64) == (jnp.arange(128)[None, :] // 64)
    cd = jnp.where(blk[None], jnp.exp(lg * C)[:, :, None], 0.0)
    return dm.astype(F32), xi.astype(F32), zeta.astype(F32), cd.astype(F32)


def _ret_specs(tb, rev, nt, roped=False):
    def tmap(t):
        return (nt - 1 - t) if rev else t
    offsets = (0, 0, 8) if roped else (0, 4, 8)
    qkv = [pl.BlockSpec((tb, 128), lambda p, t, o=o: (tmap(t), o + p)) for o in offsets]
    rope = [pl.BlockSpec((tb, 128), lambda p, t: (tmap(t), 0))] * 2
    tabs = [pl.BlockSpec((None, 256, 128), lambda p, t: (p, 0, 0))] + \
           [pl.BlockSpec((None, 128, 128), lambda p, t: (p, 0, 0))] * 3
    return qkv, rope, tabs


def _ret_fwd(proj, cos, ss, tabs, gnw, *, name):
    T = proj.shape[0]
    tb = min(T, 1024)
    nt = T // tb
    nchunk = tb // RET_CHUNK

    def body(q_ref, k_ref, v_ref, g_ref, cos_ref, ss_ref, dm_ref, xi_ref, zt_ref, cd_ref, gnw_ref,
             y_ref, o_ref, qr_ref, kr_ref, r_sc):
        @pl.when(pl.program_id(1) == 0)
        def _():
            r_sc[...] = jnp.zeros_like(r_sc)
        m0, m1 = _head_masks((128, 128))
        dm, xi, zt, cd = dm_ref[...], xi_ref[...], zt_ref[...], cd_ref[...]
        bm = (cd > 0).astype(F32)
        gnw = gnw_ref[...]
        for c in range(nchunk):
            rs = pl.ds(c * RET_CHUNK, RET_CHUNK)
            cs, sn = cos_ref[rs, :], ss_ref[rs, :]
            q = _rope(q_ref[rs, :], cs, sn, 32, 64)
            k = _rope(k_ref[rs, :], cs, sn, 32, 64) * K_SCALE
            v = v_ref[rs, :]
            kb, vb = k.astype(BF16), v.astype(BF16)
            qr_ref[rs, :] = q.astype(BF16)
            kr_ref[rs, :] = kb
            qs = jnp.concatenate([q * m0, q * m1], axis=0).astype(BF16)
            s = (_dot_nt(qs, kb) * dm).astype(BF16)
            vs = jnp.concatenate([v * m0, v * m1], axis=0).astype(BF16)
            o = _dot(jnp.concatenate([s[:128], s[128:]], axis=1), vs)
            r = r_sc[...]
            o = o + _dot(q.astype(BF16), r.astype(BF16)) * xi
            r_sc[...] = cd * r + bm * _dot_tn((k * zt).astype(BF16), vb)
            mu = (jnp.sum(o * m0, axis=1, keepdims=True) * m0 + jnp.sum(o * m1, axis=1, keepdims=True) * m1) * (1.0 / 64)
            d = o - mu
            dd = d * d
            var = (jnp.sum(dd * m0, axis=1, keepdims=True) * m0 + jnp.sum(dd * m1, axis=1, keepdims=True) * m1) * (1.0 / 64)
            oh = d * lax.rsqrt(var + EPS)
            g = g_ref[rs, :]
            y_ref[rs, :] = (g * _sigmoid(g) * (oh * gnw)).astype(BF16)
            o_ref[rs, :] = o

    qkv, rope, tspec = _ret_specs(tb, False, nt)
    gspec = pl.BlockSpec((tb, 128), lambda p, t: (t, 12 + p))
    out = pl.BlockSpec((tb, 128), lambda p, t: (t, p))
    return pl.pallas_call(
        body, name=name, grid=(4, nt),
        in_specs=qkv + [gspec] + rope + tspec + [pl.BlockSpec((1, 128), lambda p, t: (0, p))],
        out_specs=[out, out, out, out],
        out_shape=[jax.ShapeDtypeStruct((T, RET_WIDTH), BF16), jax.ShapeDtypeStruct((T, RET_WIDTH), F32),
                   jax.ShapeDtypeStruct((T, RET_WIDTH), BF16), jax.ShapeDtypeStruct((T, RET_WIDTH), BF16)],
        scratch_shapes=[pltpu.VMEM((128, 128), F32)],
        compiler_params=_cp(("parallel", "arbitrary")))(proj, proj, proj, proj, cos, ss, *tabs, gnw)


def _ret_bwd_dq(kr, proj, do, cos, ss, tabs, *, name):
    T = proj.shape[0]
    tb = min(T, 1024)
    nt = T // tb
    nchunk = tb // RET_CHUNK

    def body(k_ref, v_ref, do_ref, cos_ref, ss_ref, dm_ref, xi_ref, zt_ref, cd_ref, dq_ref, r_sc):
        @pl.when(pl.program_id(1) == 0)
        def _():
            r_sc[...] = jnp.zeros_like(r_sc)
        m0, m1 = _head_masks((128, 128))
        dm, xi, zt, cd = dm_ref[...], xi_ref[...], zt_ref[...], cd_ref[...]
        bm = (cd > 0).astype(F32)
        for c in range(nchunk):
            rs = pl.ds(c * RET_CHUNK, RET_CHUNK)
            cs, sn = cos_ref[rs, :], ss_ref[rs, :]
            k = k_ref[rs, :].astype(F32)
            vb = v_ref[rs, :].astype(BF16)
            dob = do_ref[rs, :]
            dof = dob.astype(F32)
            dos = jnp.concatenate([dof * m0, dof * m1], axis=0).astype(BF16)
            a = (_dot_nt(dos, vb) * dm).astype(BF16)
            ks = jnp.concatenate([k * m0, k * m1], axis=0).astype(BF16)
            r = r_sc[...]
            dq = _dot(jnp.concatenate([a[:128], a[128:]], axis=1), ks) + _dot_nt(dob, r.astype(BF16)) * xi
            r_sc[...] = cd * r + bm * _dot_tn((k * zt).astype(BF16), vb)
            dq_ref[rs, :] = _rope_t(dq, cs, sn, 32, 64).astype(BF16)

    qkv, rope, tspec = _ret_specs(tb, False, nt, roped=True)
    blk = pl.BlockSpec((tb, 128), lambda p, t: (t, p))
    return pl.pallas_call(
        body, name=name, grid=(4, nt), in_specs=qkv[1:] + [blk] + rope + tspec, out_specs=blk,
        out_shape=jax.ShapeDtypeStruct((T, RET_WIDTH), BF16),
        scratch_shapes=[pltpu.VMEM((128, 128), F32)],
        compiler_params=_cp(("parallel", "arbitrary")))(kr, proj, do, cos, ss, *tabs)


def _ret_bwd_dkv(qr, kr, proj, do, cos, ss, tabs, *, name, swap=None):
    T = proj.shape[0]
    tb = min(T, 1024)
    nt = T // tb
    nchunk = tb // RET_CHUNK

    def body(q_ref, k_ref, v_ref, do_ref, cos_ref, ss_ref, dm_ref, xi_ref, zt_ref, cd_ref, *rest):
        if swap is None:
            backward(q_ref, k_ref, v_ref, do_ref, cos_ref, ss_ref, dm_ref, xi_ref, zt_ref, cd_ref, *rest)
        else:
            g_ref, dk_ref, dv_ref, got_ref, u_sc, *sems = rest
            start, finish = _swap_phases(g_ref, got_ref, *sems)
            pl.when((pl.program_id(0) == 0) & (pl.program_id(1) == 0))(start)
            backward(q_ref, k_ref, v_ref, do_ref, cos_ref, ss_ref, dm_ref, xi_ref, zt_ref, cd_ref, dk_ref, dv_ref, u_sc)
            pl.when((pl.program_id(0) == 3) & (pl.program_id(1) == nt - 1))(finish)

    def backward(q_ref, k_ref, v_ref, do_ref, cos_ref, ss_ref, dm_ref, xi_ref, zt_ref, cd_ref, dk_ref, dv_ref, u_sc):
        @pl.when(pl.program_id(1) == 0)
        def _():
            u_sc[...] = jnp.zeros_like(u_sc)
        m0, m1 = _head_masks((128, 128))
        dm, xi, zt, cd = dm_ref[...], xi_ref[...], zt_ref[...], cd_ref[...]
        bm = (cd > 0).astype(F32)
        for c in reversed(range(nchunk)):
            rs = pl.ds(c * RET_CHUNK, RET_CHUNK)
            cs, sn = cos_ref[rs, :], ss_ref[rs, :]
            kb = k_ref[rs, :]
            q = q_ref[rs, :].astype(F32)
            vb = v_ref[rs, :].astype(BF16)
            dob = do_ref[rs, :]
            dof = dob.astype(F32)
            qs = jnp.concatenate([q * m0, q * m1], axis=0).astype(BF16)
            dos = jnp.concatenate([dof * m0, dof * m1], axis=0).astype(BF16)
            s = (_dot_nt(qs, kb) * dm).astype(BF16)
            a = (_dot_nt(dos, vb) * dm).astype(BF16)
            ub = u_sc[...].astype(BF16)
            dk = _dot_tn(a, qs) + _dot_nt(vb, ub) * zt
            dv = _dot_tn(s, dos) + _dot(kb, ub) * zt
            u_sc[...] = cd * u_sc[...] + bm * _dot_tn((q * xi).astype(BF16), dob)
            dk_ref[rs, :] = (_rope_t(dk, cs, sn, 32, 64) * K_SCALE).astype(BF16)
            dv_ref[rs, :] = dv.astype(BF16)

    qkv, rope, tspec = _ret_specs(tb, True, nt, roped=True)
    blk = pl.BlockSpec((tb, 128), lambda p, t: (nt - 1 - t, p))
    out_shape = [jax.ShapeDtypeStruct((T, RET_WIDTH), BF16)] * 2
    if swap is None:
        return pl.pallas_call(
            body, name=name, grid=(4, nt), in_specs=qkv + [blk] + rope + tspec, out_specs=[blk, blk],
            out_shape=out_shape, scratch_shapes=[pltpu.VMEM((128, 128), F32)],
            compiler_params=_cp(("parallel", "arbitrary")))(qr, kr, proj, do, cos, ss, *tabs)
    return pl.pallas_call(
        body, name=name, grid=(4, nt), in_specs=qkv + [blk] + rope + tspec + [ANY], out_specs=[blk, blk, ANY],
        out_shape=out_shape + [jax.ShapeDtypeStruct((4,) + swap.shape[2:], swap.dtype)],
        scratch_shapes=[pltpu.VMEM((128, 128), F32)] + list(SWAP_SCRATCH),
        compiler_params=_cp(("arbitrary", "arbitrary")))(qr, kr, proj, do, cos, ss, *tabs, swap)


def _mix_bwd(dx1, w_out, o_ret, proj, y_mla, gnw, *, name):
    T = dx1.shape[0]
    tm = min(T, 512)

    def body(dx_ref, wo_ref, o_ref, g_ref, ym_ref, gnw_ref, do_ref, dg_ref, dom_ref, dl_ref, dw_ref, dm_ref):
        @pl.when(pl.program_id(0) == 0)
        def _():
            dw_ref[...] = jnp.zeros_like(dw_ref)
        dm_ref[...] = _dot_nt(dx_ref[...].astype(BF16), wo_ref[...])
        m0, m1 = _head_masks((tm, 128))
        lane = lax.broadcasted_iota(jnp.int32, (tm, 128), 1)
        delta = jnp.zeros((tm, 128), F32)

        def gsum(z):
            return jnp.sum(z * m0, axis=1, keepdims=True) * m0 + jnp.sum(z * m1, axis=1, keepdims=True) * m1

        for p in range(4):
            cs = slice(128 * p, 128 * p + 128)
            dy = dm_ref[:, cs]
            o = o_ref[:, cs]
            g = g_ref[:, cs]
            w = gnw_ref[:, cs]
            d = o - gsum(o) * (1.0 / 64)
            rstd = lax.rsqrt(gsum(d * d) * (1.0 / 64) + EPS)
            oh = d * rstd
            sg = _sigmoid(g)
            dn = dy * (g * sg)
            dg_ref[:, cs] = (dy * (oh * w) * (sg * (1.0 + g * (1.0 - sg)))).astype(BF16)
            dw_ref[:, cs] += jnp.sum(dn * oh, axis=0, keepdims=True)
            doh = dn * w
            do = rstd * (doh - gsum(doh) * (1.0 / 64) - oh * (gsum(doh * oh) * (1.0 / 64)))
            do_ref[:, cs] = do.astype(BF16)
            dom = dm_ref[:, 512 + 128 * p:512 + 128 * p + 128]
            dom_ref[:, cs] = dom.astype(BF16)
            pr = dom * ym_ref[:, cs].astype(F32)
            delta = jnp.where(lane == 2 * p, jnp.sum(pr * m0, axis=1, keepdims=True), delta)
            delta = jnp.where(lane == 2 * p + 1, jnp.sum(pr * m1, axis=1, keepdims=True), delta)
        dl_ref[...] = delta.T[0:MLA_HEADS]

    half = pl.BlockSpec((tm, 512), lambda i: (i, 0))
    return pl.pallas_call(
        body, name=name, grid=(T // tm,),
        in_specs=[pl.BlockSpec((tm, D_MODEL), lambda i: (i, 0)), pl.BlockSpec(w_out.shape, lambda i: (0, 0)), half,
                  pl.BlockSpec((tm, 512), lambda i: (i, 3)), half, pl.BlockSpec((1, 512), lambda i: (0, 0))],
        out_specs=[half, half, half, pl.BlockSpec((MLA_HEADS, tm), lambda i: (0, i)),
                   pl.BlockSpec((1, 512), lambda i: (0, 0))],
        out_shape=[jax.ShapeDtypeStruct((T, 512), BF16)] * 3 + [jax.ShapeDtypeStruct((MLA_HEADS, T), F32),
                                                                jax.ShapeDtypeStruct((1, 512), F32)],
        scratch_shapes=[pltpu.VMEM((tm, 1024), F32)],
        compiler_params=_cp(("arbitrary",)))(dx1, w_out, o_ret, proj, y_mla, gnw)


def _mla_prep_fwd(proj, qnw, kvnw, wuq, wk, wv, cos, ss, *, name):
    T = proj.shape[0]
    tm = min(T, 512)

    def body(lat_ref, qnw_ref, kvnw_ref, wuq_ref, wk_ref, wv_ref, cos_ref, ss_ref,
             q_ref, k_ref, v_ref, cqn_ref, ckvn_ref):
        cq = lat_ref[:, 0:256]
        ckv = lat_ref[:, 256:384]
        g3 = lat_ref[:, 384:512]
        cqn = (cq * lax.rsqrt(jnp.mean(cq * cq, axis=-1, keepdims=True) + EPS) * qnw_ref[...]).astype(BF16)
        ckvn = (ckv * lax.rsqrt(jnp.mean(ckv * ckv, axis=-1, keepdims=True) + EPS) * kvnw_ref[...]).astype(BF16)
        cqn_ref[...] = cqn
        ckvn_ref[...] = ckvn
        cs, sn = cos_ref[...], ss_ref[...]
        q = _dot_nt(cqn, wuq_ref[...])
        k = _dot_nt(ckvn, wk_ref[...])
        kpe = _rope(g3, cs, sn, 16, 32)
        for h in range(MLA_HEADS):
            hs = slice(128 * h, 128 * h + 128)
            q_ref[:, hs] = (_rope(q[:, hs], cs, sn, 16, 32) * SCALE).astype(BF16)
            k_ref[:, hs] = (k[:, hs] + kpe).astype(BF16)
        v = _dot_nt(ckvn, wv_ref[...])
        lane = lax.broadcasted_iota(jnp.int32, (tm, 128), 1)
        for p in range(4):
            vp = v[:, 128 * p:128 * p + 128]
            v_ref[:, 256 * p:256 * p + 128] = jnp.where(lane < 64, vp, 1.0).astype(BF16)
            v_ref[:, 256 * p + 128:256 * p + 256] = jnp.where(lane < 64, 1.0, vp).astype(BF16)

    def full(shape):
        return pl.BlockSpec(shape, lambda i: (0, 0))

    def row(w):
        return pl.BlockSpec((tm, w), lambda i: (i, 0))

    return pl.pallas_call(
        body, name=name, grid=(T // tm,),
        in_specs=[pl.BlockSpec((tm, 512), lambda i: (i, 4)), full((1, 256)), full((1, 128)), full((1024, 256)),
                  full((1024, 128)), full((512, 128)), row(128), row(128)],
        out_specs=[row(1024), row(1024), row(1024), row(256), row(128)],
        out_shape=[jax.ShapeDtypeStruct((T, 1024), BF16), jax.ShapeDtypeStruct((T, 1024), BF16),
                   jax.ShapeDtypeStruct((T, 1024), BF16), jax.ShapeDtypeStruct((T, 256), BF16),
                   jax.ShapeDtypeStruct((T, 128), BF16)],
        compiler_params=_cp(("parallel",)))(proj, qnw, kvnw, wuq, wk, wv, cos, ss)


def _mla_prep_bwd(dq, dk, dv, proj, qnw, kvnw, wuq_t, wk_t, wv_t, cos, ss, ret_grads, cqn, ckvn, h, *, name):
    T = proj.shape[0]
    tm = min(T, 256)

    def body(dq_ref, dk_ref, dv_ref, lat_ref, qnw_ref, kvnw_ref, wuq_ref, wk_ref, wv_ref, cos_ref, ss_ref,
             rq_ref, rk_ref, rv_ref, rg_ref, cqn_ref, ckvn_ref, h_ref,
             dproj_ref, gwin_ref, gwuq_ref, gwk_ref, gwv_ref, dqnw_ref, dkvnw_ref, dqp_ref):
        for j, r in enumerate((rq_ref, rk_ref, rv_ref, rg_ref)):
            dproj_ref[:, 512 * j:512 * j + 512] = r[...]
        dlat_ref = dproj_ref.at[:, 2048:2560]

        @pl.when(pl.program_id(0) == 0)
        def _():
            for r in (gwin_ref, gwuq_ref, gwk_ref, gwv_ref, dqnw_ref, dkvnw_ref):
                r[...] = jnp.zeros_like(r)
        cs, sn = cos_ref[...], ss_ref[...]
        dkpe = jnp.zeros((tm, 128), F32)
        for h in range(MLA_HEADS):
            hs = slice(128 * h, 128 * h + 128)
            dqp_ref[:, hs] = _rope_t(dq_ref[:, hs] * SCALE, cs, sn, 16, 32).astype(BF16)
            dkpe = dkpe + dk_ref[:, hs]
        lane = lax.broadcasted_iota(jnp.int32, (tm, 128), 1)
        rope_lane = (lane >= MLA_NOPE) & (lane < MLA_NOPE + MLA_ROPE)
        dg3 = jnp.where(rope_lane, _rope_t(jnp.where(rope_lane, dkpe, 0.0), cs, sn, 16, 32), 0.0)

        def norm_bwd(x, w, dn):
            r = lax.rsqrt(jnp.mean(x * x, axis=-1, keepdims=True) + EPS)
            xh = x * r
            g = dn * w
            return r * (g - xh * jnp.mean(g * xh, axis=-1, keepdims=True)), jnp.sum(dn * xh, axis=0, keepdims=True)

        dqp = dqp_ref[...]
        dkb = dk_ref[...].astype(BF16)
        dvb = dv_ref[...]
        dcqn = _dot(dqp, wuq_ref[...])
        dcq, dqnw = norm_bwd(lat_ref[:, 0:256], qnw_ref[...], dcqn)
        dckvn = _dot(dkb, wk_ref[...]) + _dot(dvb, wv_ref[...])
        dckv, dkvnw = norm_bwd(lat_ref[:, 256:384], kvnw_ref[...], dckvn)
        gwuq_ref[...] += _dot_tn(dqp, cqn_ref[...])
        gwk_ref[...] += _dot_tn(dkb, ckvn_ref[...])
        gwv_ref[...] += _dot_tn(dvb, ckvn_ref[...])
        dqnw_ref[...] += dqnw
        dkvnw_ref[...] += dkvnw
        dlat_ref[:, 0:256] = dcq.astype(BF16)
        dlat_ref[:, 256:384] = dckv.astype(BF16)
        dlat_ref[:, 384:512] = dg3.astype(BF16)
        gwin_ref[...] += _dot_tn(dproj_ref[...], h_ref[...])

    def full(shape):
        return pl.BlockSpec(shape, lambda i: (0, 0))

    def row(w):
        return pl.BlockSpec((tm, w), lambda i: (i, 0))

    return pl.pallas_call(
        body, name=name, grid=(T // tm,),
        in_specs=[row(1024), row(1024), row(512), pl.BlockSpec((tm, 512), lambda i: (i, 4)), full((1, 256)),
                  full((1, 128)), full((1024, 256)), full((1024, 128)), full((512, 128)), row(128), row(128)]
                 + [row(512)] * 4 + [row(256), row(128), row(D_MODEL)],
        out_specs=[row(IN_PAD), full((IN_PAD, D_MODEL)), full((1024, 256)), full((1024, 128)), full((512, 128)),
                   full((1, 256)), full((1, 128))],
        out_shape=[jax.ShapeDtypeStruct((T, IN_PAD), BF16), jax.ShapeDtypeStruct((IN_PAD, D_MODEL), F32),
                   jax.ShapeDtypeStruct((1024, 256), F32), jax.ShapeDtypeStruct((1024, 128), F32),
                   jax.ShapeDtypeStruct((512, 128), F32), jax.ShapeDtypeStruct((1, 256), F32),
                   jax.ShapeDtypeStruct((1, 128), F32)],
        scratch_shapes=[pltpu.VMEM((tm, 1024), BF16)],
        compiler_params=_cp(("arbitrary",)))(dq, dk, dv, proj, qnw, kvnw, wuq_t, wk_t, wv_t, cos, ss, *ret_grads,
                                             cqn, ckvn, h)


def _flash_fwd(q, k, v1, *, name, gather=None):
    T = q.shape[0]
    tq = min(T, 512)
    tk = tq
    nq = T // tq

    def body(q_ref, k_ref, v_ref, *rest):
        if gather is None:
            y_ref, lse_ref = rest
        else:
            x_ref, y_ref, lse_ref, g_ref, *sems = rest
            start, forward, finish = _gather_phases(x_ref, g_ref, *sems)
            pl.when((pl.program_id(0) == 0) & (pl.program_id(1) == 0))(start)
            pl.when((pl.program_id(0) == 1) & (pl.program_id(1) == 0))(forward)
        attend(q_ref, k_ref, v_ref, y_ref, lse_ref)
        if gather is not None:
            pl.when((pl.program_id(0) == 3) & (pl.program_id(1) == nq - 1))(finish)

    def attend(q_ref, k_ref, v_ref, y_ref, lse_ref):
        qi = pl.program_id(1)
        row = lax.broadcasted_iota(jnp.int32, (tq, tk), 0)
        col = lax.broadcasted_iota(jnp.int32, (tq, tk), 1)

        def step(kb, carry, masked):
            ks = pl.ds(pl.multiple_of(kb * tk, tk), tk)
            new = []
            for h in range(2):
                hs = slice(128 * h, 128 * h + 128)
                m, acc = carry[h]
                s = _dot_nt(q_ref[:, hs], k_ref[ks, hs])
                if masked:
                    s = jnp.where(col <= row, s, NEG)
                mn = jnp.maximum(m, jnp.max(s, axis=1, keepdims=True))
                p = jnp.exp((s - mn).astype(BF16))
                acc = jnp.exp(m - mn) * acc + _dot(p, v_ref[ks, hs])
                new.append((mn, acc))
            return tuple(new)

        def unrolled(j, c):
            for u in range(FLASH_UNROLL):
                c = step(FLASH_UNROLL * j + u, c, False)
            return c

        init = (jnp.full((tq, 1), NEG, F32), jnp.zeros((tq, 128), F32))
        carry = lax.fori_loop(0, qi // FLASH_UNROLL, unrolled, (init, init))
        carry = lax.fori_loop(FLASH_UNROLL * (qi // FLASH_UNROLL), qi, lambda kb, c: step(kb, c, False), carry)
        (ma, acca), (mb, accb) = step(qi, carry, True)
        lane = lax.broadcasted_iota(jnp.int32, (tq, 128), 1)
        la, lb = pltpu.roll(acca, 64, 1), pltpu.roll(accb, 64, 1)
        y_ref[...] = jnp.where(lane < 64, acca / la, accb / lb).astype(BF16)
        lse_ref[0, 0] = jnp.broadcast_to(ma + jnp.log(acca[:, 64:65]), (tq, 128)).T[0:1]
        lse_ref[1, 0] = jnp.broadcast_to(mb + jnp.log(accb[:, 0:1]), (tq, 128)).T[0:1]

    in_specs = [pl.BlockSpec((tq, 256), lambda p, i: (i, p)), pl.BlockSpec((T, 256), lambda p, i: (0, p)),
                pl.BlockSpec((T, 256), lambda p, i: (0, p))]
    out_specs = [pl.BlockSpec((tq, 128), lambda p, i: (i, p)), pl.BlockSpec((2, 1, 1, tq), lambda p, i: (p, i, 0, 0))]
    out_shape = [jax.ShapeDtypeStruct((T, MLA_WIDTH), BF16), jax.ShapeDtypeStruct((MLA_HEADS, nq, 1, tq), F32)]
    if gather is None:
        return pl.pallas_call(body, name=name, grid=(4, nq), in_specs=in_specs, out_specs=out_specs,
                              out_shape=out_shape, compiler_params=_cp(("parallel", "arbitrary")))(q, k, v1)
    return pl.pallas_call(
        body, name=name, grid=(4, nq), in_specs=in_specs + [ANY], out_specs=out_specs + [ANY],
        out_shape=out_shape + [jax.ShapeDtypeStruct((N_DEV,) + gather.shape, gather.dtype)],
        scratch_shapes=list(GATHER_SCRATCH),
        compiler_params=_cp(("arbitrary", "arbitrary")))(q, k, v1, gather)


def _flash_bwd(q, k, v, do, lse, delta, *, name, exchange=None):
    T = q.shape[0]
    tq = min(T, 512)
    tk = tq
    nq = T // tq

    def body(q_ref, k_ref, v_ref, do_ref, lse_ref, dl_ref, *rest):
        if exchange is None:
            backward(q_ref, k_ref, v_ref, do_ref, lse_ref, dl_ref, *rest)
        else:
            p_ref, dqt_ref, dk_ref, dv_ref, got_ref, *sems = rest
            start, finish = _exchange_phases(p_ref, got_ref, *sems)
            pl.when((pl.program_id(0) == 0) & (pl.program_id(1) == 0))(start)
            backward(q_ref, k_ref, v_ref, do_ref, lse_ref, dl_ref, dqt_ref, dk_ref, dv_ref)
            pl.when((pl.program_id(0) == 3) & (pl.program_id(1) == nq - 1))(finish)

    def backward(q_ref, k_ref, v_ref, do_ref, lse_ref, dl_ref, dqt_ref, dk_ref, dv_ref):
        kb = pl.program_id(1)

        @pl.when(kb == 0)
        def _():
            dqt_ref[...] = jnp.zeros_like(dqt_ref)
        krow = lax.broadcasted_iota(jnp.int32, (tk, tq), 0)
        qcol = lax.broadcasted_iota(jnp.int32, (tk, tq), 1)
        masks = _head_masks((tk, 128))
        vms = [(v_ref[:, 128 * h:128 * h + 128].astype(F32) * masks[h]).astype(BF16) for h in range(2)]

        def step(qi, carry, masked):
            qs = pl.ds(pl.multiple_of(qi * tq, tq), tq)
            dob = do_ref[qs, :]
            dof = dob.astype(F32)
            dks, dv_acc = list(carry[:2]), carry[2]
            for h in range(2):
                hs = slice(128 * h, 128 * h + 128)
                kh = k_ref[:, hs]
                qh = q_ref[qs, hs]
                st = _dot_nt(kh, qh)
                pt = jnp.exp((st - lse_ref[h, qi]).astype(BF16))
                if masked:
                    pt = jnp.where(krow <= qcol, pt, jnp.zeros_like(pt))
                dv_acc = dv_acc + _dot(pt, (dof * masks[h]).astype(BF16))
                dpt = _dot_nt(vms[h], dob)
                dst = pt * (dpt - dl_ref[h, qi]).astype(BF16)
                dks[h] = dks[h] + _dot(dst, qh)
                dqt_ref[qi, hs, :] += _dot_tn(kh, dst)
            return dks[0], dks[1], dv_acc

        zero = jnp.zeros((tk, 128), F32)
        carry = step(kb, (zero, zero, zero), True)

        def unrolled(j, c):
            for u in range(FLASH_BWD_UNROLL):
                c = step(kb + 1 + FLASH_BWD_UNROLL * j + u, c, False)
            return c

        trips = (nq - 1 - kb) // FLASH_BWD_UNROLL
        carry = lax.fori_loop(0, trips, unrolled, carry)
        dk0, dk1, dv_acc = lax.fori_loop(kb + 1 + FLASH_BWD_UNROLL * trips, nq, lambda qi, c: step(qi, c, False), carry)
        dk_ref[:, 0:128] = dk0
        dk_ref[:, 128:256] = dk1
        dv_ref[...] = dv_acc.astype(BF16)

    stat = pl.BlockSpec((2, nq, 1, tq), lambda p, j: (p, 0, 0, 0))
    in_specs = [pl.BlockSpec((T, 256), lambda p, j: (0, p)), pl.BlockSpec((tk, 256), lambda p, j: (j, p)),
                pl.BlockSpec((tk, 256), lambda p, j: (j, p)), pl.BlockSpec((T, 128), lambda p, j: (0, p)), stat, stat]
    out_specs = [pl.BlockSpec((None, nq, 256, tq), lambda p, j: (p, 0, 0, 0)),
                 pl.BlockSpec((tk, 256), lambda p, j: (j, p)), pl.BlockSpec((tk, 128), lambda p, j: (j, p))]
    out_shape = [jax.ShapeDtypeStruct((4, nq, 256, tq), F32), jax.ShapeDtypeStruct((T, 1024), F32),
                 jax.ShapeDtypeStruct((T, MLA_WIDTH), BF16)]
    if exchange is None:
        return pl.pallas_call(body, name=name, grid=(4, nq), in_specs=in_specs, out_specs=out_specs,
                              out_shape=out_shape,
                              compiler_params=_cp(("parallel", "arbitrary")))(q, k, v, do, lse, delta)
    return pl.pallas_call(
        body, name=name, grid=(4, nq), in_specs=in_specs + [ANY], out_specs=out_specs + [ANY],
        out_shape=out_shape + [jax.ShapeDtypeStruct(exchange.shape, exchange.dtype)],
        scratch_shapes=list(EXCHANGE_SCRATCH),
        compiler_params=_cp(("arbitrary", "arbitrary")))(q, k, v, do, lse, delta, exchange)


def _shift_down(x, n, prev8):
    r = pltpu.roll(x, n, 0)
    row = lax.broadcasted_iota(jnp.int32, prev8.shape, 0)
    first = jnp.where(row < n, pltpu.roll(prev8, n, 0), r[:8])
    if x.shape[0] == 8:
        return first
    return jnp.concatenate([first, r[8:]], axis=0)


def _shift_up(x, n, next8):
    tm = x.shape[0]
    r = pltpu.roll(x, tm - n, 0)
    row = lax.broadcasted_iota(jnp.int32, next8.shape, 0)
    last = jnp.where(row >= 8 - n, pltpu.roll(next8, 8 - n, 0), r[tm - 8:])
    return jnp.concatenate([r[:tm - 8], last], axis=0)


def _conv_pre(u, prev8, cw_ref, cb_ref):
    p1 = _shift_down(u, 1, prev8)
    p2 = _shift_down(u, 2, prev8)
    up = cb_ref[...] + cw_ref[0:1, :] * p2 + cw_ref[1:2, :] * p1 + cw_ref[2:3, :] * u
    return up, p1, p2


def _up_proj_conv(x, y_ret, y_mla, w_out, nw, w_up_t, cw, cb, *, name):
    T, K = x.shape
    tm = min(T, 256)

    def body(x_ref, yr_ref, ym_ref, wo_ref, nw_ref, w_ref, cw_ref, cb_ref, x1_ref, h_ref, u_ref, a_ref, carry_sc):
        @pl.when(pl.program_id(0) == 0)
        def _():
            carry_sc[...] = jnp.zeros_like(carry_sc)
        xv = x_ref[...] + _dot(jnp.concatenate([yr_ref[...], ym_ref[...]], axis=1), wo_ref[...])
        x1_ref[...] = xv
        h = (xv * lax.rsqrt(jnp.mean(xv * xv, axis=-1, keepdims=True) + EPS) * nw_ref[...]).astype(BF16)
        h_ref[...] = h
        for blk in range(2):
            ups = []
            for half in range(2):
                cs = slice((2 * blk + half) * FF_HALF, (2 * blk + half + 1) * FF_HALF)
                u = _dot_nt(h, w_ref[cs, :])
                u_ref[:, cs] = u
                prev = carry_sc[:, cs]
                ups.append(cb_ref[:, cs] + cw_ref[0:1, cs] * _shift_down(u, 2, prev)
                           + cw_ref[1:2, cs] * _shift_down(u, 1, prev) + cw_ref[2:3, cs] * u)
                carry_sc[:, cs] = u[tm - 8:]
            gate, val = ups
            a_ref[:, blk * FF_HALF:(blk + 1) * FF_HALF] = (gate * _sigmoid(gate) * val).astype(BF16)

    def full(shape):
        return pl.BlockSpec(shape, lambda i: (0, 0))

    return pl.pallas_call(
        body, name=name, grid=(T // tm,),
        in_specs=[pl.BlockSpec((tm, K), lambda i: (i, 0)), pl.BlockSpec((tm, RET_WIDTH), lambda i: (i, 0)),
                  pl.BlockSpec((tm, MLA_WIDTH), lambda i: (i, 0)), full(w_out.shape), full(nw.shape),
                  full(w_up_t.shape), full(cw.shape), full(cb.shape)],
        out_specs=[pl.BlockSpec((tm, K), lambda i: (i, 0)), pl.BlockSpec((tm, K), lambda i: (i, 0)),
                   pl.BlockSpec((tm, 2 * D_FF), lambda i: (i, 0)), pl.BlockSpec((tm, D_FF), lambda i: (i, 0))],
        out_shape=[jax.ShapeDtypeStruct((T, K), F32), jax.ShapeDtypeStruct((T, K), BF16),
                   jax.ShapeDtypeStruct((T, 2 * D_FF), F32), jax.ShapeDtypeStruct((T, D_FF), BF16)],
        scratch_shapes=[pltpu.VMEM((8, 2 * D_FF), F32)],
        compiler_params=_cp(("arbitrary",)))(x, y_ret, y_mla, w_out, nw, w_up_t, cw, cb)


def _conv_bwd(u, da, cw, cb, *, name):
    T = u.shape[0]
    tm = min(T, 512)
    W = 2 * FF_HALF
    nt = T // tm

    def body(u_ref, prev_ref, next_ref, da_ref, dan_ref, cw_ref, cb_ref, du_ref, dw0_ref, dw1_ref, dw2_ref, db_ref):
        i = pl.program_id(1)

        @pl.when(i == 0)
        def _():
            for r in (dw0_ref, dw1_ref, dw2_ref, db_ref):
                r[...] = jnp.zeros_like(r)

        def dpre(u, prev8, da):
            up, p1, p2 = _conv_pre(u, prev8, cw_ref, cb_ref)
            gate, val = up[:, :FF_HALF], up[:, FF_HALF:]
            sg = _sigmoid(gate)
            dgate = da * val * (sg * (1.0 + gate * (1.0 - sg)))
            dval = da * (gate * sg)
            return jnp.concatenate([dgate, dval], axis=1), p1, p2

        u = u_ref[...]
        prev = jnp.where(i > 0, prev_ref[...], 0.0)
        dup, p1, p2 = dpre(u, prev, da_ref[...])
        dupn, _, _ = dpre(next_ref[...], u[tm - 8:], dan_ref[...])
        dupn = jnp.where(i < nt - 1, dupn, 0.0)
        du = cw_ref[2:3, :] * dup + cw_ref[1:2, :] * _shift_up(dup, 1, dupn) + cw_ref[0:1, :] * _shift_up(dup, 2, dupn)
        du_ref[...] = du.astype(BF16)
        dw0_ref[...] += jnp.sum(dup * p2, axis=0, keepdims=True)
        dw1_ref[...] += jnp.sum(dup * p1, axis=0, keepdims=True)
        dw2_ref[...] += jnp.sum(dup * u, axis=0, keepdims=True)
        db_ref[...] += jnp.sum(dup, axis=0, keepdims=True)

    nxt = lambda j, i: (jnp.minimum((i + 1) * (tm // 8), T // 8 - 1), j)
    vec = pl.BlockSpec((1, W), lambda j, i: (0, j))
    return pl.pallas_call(
        body, name=name, grid=(2, nt),
        in_specs=[pl.BlockSpec((tm, W), lambda j, i: (i, j)),
                  pl.BlockSpec((8, W), lambda j, i: (jnp.maximum(i * (tm // 8) - 1, 0), j)),
                  pl.BlockSpec((8, W), nxt),
                  pl.BlockSpec((tm, FF_HALF), lambda j, i: (i, j)), pl.BlockSpec((8, FF_HALF), nxt),
                  pl.BlockSpec((3, W), lambda j, i: (0, j)), vec],
        out_specs=[pl.BlockSpec((tm, W), lambda j, i: (i, j)), vec, vec, vec, vec],
        out_shape=[jax.ShapeDtypeStruct((T, 2 * D_FF), BF16)] + [jax.ShapeDtypeStruct((1, 2 * D_FF), F32)] * 4,
        compiler_params=_cp(("parallel", "arbitrary")))(u, u, u, da, da, cw, cb)


def _sum_chips(slots, *, name):
    ns, R, C = slots.shape
    tr = _row_tile(R)

    def body(g_ref, o_ref):
        g = g_ref[0].astype(F32)
        for s in range(1, ns):
            g = g + g_ref[s].astype(F32)
        o_ref[...] = g

    return pl.pallas_call(
        body, name=name, grid=(R // tr,), in_specs=[pl.BlockSpec((ns, tr, C), lambda i: (0, i, 0))],
        out_specs=pl.BlockSpec((tr, C), lambda i: (i, 0)), out_shape=jax.ShapeDtypeStruct((R, C), F32),
        compiler_params=_cp(("parallel",)))(slots)


def _place():
    return lax.axis_index("x"), lax.axis_index("y"), lax.axis_index("c")


GATHER_SCRATCH = (pltpu.SemaphoreType.DMA((7,)), pltpu.SemaphoreType.DMA((7,)), pltpu.SemaphoreType.DMA)
EXCHANGE_SCRATCH = (pltpu.SemaphoreType.DMA((3,)), pltpu.SemaphoreType.DMA((3,)), pltpu.SemaphoreType.DMA)


def _gather_phases(x_ref, out_ref, send_sems, recv_sems, local_sem):
    x_, y_, c_ = _place()
    me, sibling = (x_, y_, c_), (x_, y_, 1 - c_)
    chips = [(1 - x_, y_), (x_, 1 - y_), (1 - x_, 1 - y_)]

    def slot(px, py, pc):
        return out_ref.at[4 * px + 2 * py + pc]

    def copy(k, block, to, src=None):
        return pltpu.make_async_remote_copy(
            src_ref=slot(*block) if src is None else src, dst_ref=slot(*block),
            send_sem=send_sems.at[k], recv_sem=recv_sems.at[k], device_id=to, device_id_type=MESH)

    def mine():
        return pltpu.make_async_copy(x_ref, slot(*me), local_sem)

    def first():
        return [copy(0, me, sibling, src=x_ref)] + [copy(1 + j, me, (*chip, c_), src=x_ref)
                                                     for j, chip in enumerate(chips)]

    def passed():
        return [copy(4 + j, (*chip, c_), sibling) for j, chip in enumerate(chips)]

    def start():
        mine().start()
        for cp in first():
            cp.start()

    def forward():
        fwd = passed()
        for j, chip in enumerate(chips):
            copy(1 + j, (*chip, c_), me).wait_recv()
            fwd[j].start()

    def finish():
        copy(0, sibling, me).wait_recv()
        for j, chip in enumerate(chips):
            copy(4 + j, (*chip, 1 - c_), me).wait_recv()
        for cp in first() + passed():
            cp.wait_send()
        mine().wait()

    return start, forward, finish


def _exchange_phases(p_ref, out_ref, send_sems, recv_sems, local_sem):
    x_, y_, c_ = _place()
    me_k = 2 * x_ + y_
    chips = [(1 - x_, y_), (x_, 1 - y_), (1 - x_, 1 - y_)]

    def local():
        return pltpu.make_async_copy(p_ref.at[me_k], out_ref.at[me_k], local_sem)

    def copy(j, src_k, dst_k, chip):
        return pltpu.make_async_remote_copy(
            src_ref=p_ref.at[src_k], dst_ref=out_ref.at[dst_k], send_sem=send_sems.at[j],
            recv_sem=recv_sems.at[j], device_id=(*chip, c_), device_id_type=MESH)

    def sends():
        return [copy(j, 2 * px + py, me_k, (px, py)) for j, (px, py) in enumerate(chips)]

    def start():
        local().start()
        for cp in sends():
            cp.start()

    def finish():
        for j, (px, py) in enumerate(chips):
            copy(j, me_k, 2 * px + py, (px, py)).wait_recv()
        for cp in sends():
            cp.wait_send()
        local().wait()

    return start, finish


def _all_gather(x, *, name):
    def body(x_ref, out_ref, send_sems, recv_sems, local_sem):
        for phase in _gather_phases(x_ref, out_ref, send_sems, recv_sems, local_sem):
            phase()

    spec = pl.BlockSpec(memory_space=pltpu.VMEM)
    return pl.pallas_call(
        body, name=name, out_shape=jax.ShapeDtypeStruct((N_DEV,) + x.shape, x.dtype),
        in_specs=[spec], out_specs=spec, scratch_shapes=list(GATHER_SCRATCH),
        compiler_params=pltpu.CompilerParams(vmem_limit_bytes=VMEM_LIMIT))(x)


def _small_rows():
    table, row = [], 0
    for n, size in SMALL_VECTORS:
        table.append((n, size, row))
        row += -(-size // PACK_COLS)
    return table


def _ff_chunk_source(c):
    block, off = divmod(c * 128, FF_HALF)
    return (0, 2, 1, 3)[block] * FF_HALF + off


def _pack_small(parts, *, name):
    table = _small_rows()

    def body(*refs):
        out = refs[-1]
        out[...] = jnp.zeros_like(out)
        for ref, (n, size, row) in zip(refs, table):
            if size != 2 * D_FF:
                out[row:row + 1, 0:size] = ref[...]
                continue
            for c in range(size // 128):
                src = _ff_chunk_source(c)
                r, lane = divmod(c * 128, PACK_COLS)
                out[row + r:row + r + 1, lane:lane + 128] = ref[:, src:src + 128]

    return pl.pallas_call(body, name=name, out_shape=jax.ShapeDtypeStruct((SMALL_ROWS, PACK_COLS), F32))(
        *[parts[n] for n, _, _ in table])


def _sum_small(g, *, name):
    table = _small_rows()
    shapes = [(n, size) for n, size, _ in table if not n.startswith("conv_w")]
    shapes.insert(7, ("conv_w", 2 * D_FF))

    def body(g_ref, *outs):
        def total(row, width):
            acc = g_ref[0, row:row + 1, 0:width]
            for d in range(1, N_DEV):
                acc = acc + g_ref[d, row:row + 1, 0:width]
            return acc

        out_of = {n: o for (n, _), o in zip(shapes, outs)}
        for n, size, row in table:
            o, j = (out_of["conv_w"], int(n[-1])) if n.startswith("conv_w") else (out_of[n], 0)
            for i in range(-(-size // PACK_COLS)):
                width = min(PACK_COLS, size - PACK_COLS * i)
                o[j:j + 1, PACK_COLS * i:PACK_COLS * i + width] = total(row + i, width)

    out_shape = [jax.ShapeDtypeStruct((3 if n == "conv_w" else 1, size), F32) for n, size in shapes]
    res = pl.pallas_call(body, name=name, out_shape=out_shape)(g)
    return {n: r for (n, _), r in zip(shapes, res)}


def _adamw_multi(ws, ms, vs, gs, *, name):
    k = len(ws)

    def body(*refs):
        w_refs, m_refs, v_refs, g_refs = (refs[i * k:(i + 1) * k] for i in range(4))
        outs = refs[4 * k:]
        for i in range(k):
            g = g_refs[i][...]
            mn = ADAM_B1 * m_refs[i][...] + (1.0 - ADAM_B1) * g
            vn = ADAM_B2 * v_refs[i][...] + (1.0 - ADAM_B2) * (g * g)
            m_hat = mn / (1.0 - ADAM_B1 ** ADAM_STEP)
            v_hat = vn / (1.0 - ADAM_B2 ** ADAM_STEP)
            outs[i][...] = g
            outs[k + i][...] = -ADAM_LR * (m_hat / (jnp.sqrt(v_hat) + ADAM_EPS) + ADAM_WD * w_refs[i][...])
            outs[2 * k + i][...] = mn
            outs[3 * k + i][...] = vn

    out_shape = [jax.ShapeDtypeStruct(w.shape, F32) for _ in range(4) for w in ws]
    res = pl.pallas_call(body, name=name, out_shape=out_shape, compiler_params=_cp())(*ws, *ms, *vs, *gs)
    return [res[i * k:(i + 1) * k] for i in range(4)]


SWAP_SCRATCH = (pltpu.SemaphoreType.DMA((4,)), pltpu.SemaphoreType.DMA((4,)))


def _swap_phases(g_ref, out_ref, send_sems, recv_sems):
    x_, y_, c_ = _place()

    def copies():
        return [pltpu.make_async_remote_copy(src_ref=g_ref.at[k, 1 - c_], dst_ref=out_ref.at[k],
                                             send_sem=send_sems.at[k], recv_sem=recv_sems.at[k],
                                             device_id=(x_, y_, 1 - c_), device_id_type=MESH) for k in range(4)]

    def start():
        for cp in copies():
            cp.start()

    def finish():
        for cp in copies():
            cp.wait()

    return start, finish


def _swap_sibling(g, *, name):
    def body(g_ref, out_ref, send_sems, recv_sems):
        for phase in _swap_phases(g_ref, out_ref, send_sems, recv_sems):
            phase()

    return pl.pallas_call(
        body, name=name, out_shape=jax.ShapeDtypeStruct((4,) + g.shape[2:], g.dtype), in_specs=[ANY], out_specs=ANY,
        scratch_shapes=list(SWAP_SCRATCH))(g)


def _row_tile(R):
    for cand in (256, 400, 200):
        if R % cand == 0:
            return cand
    return R


def _add_own(g, b, *, name, out_dtype):
    n, _, R, C = g.shape
    tr = _row_tile(R)

    def body(c_ref, g_ref, b_ref, o_ref):
        del c_ref
        o_ref[...] = (g_ref[...] + b_ref[...]).astype(out_dtype)

    blk = pl.BlockSpec((None, tr, C), lambda s, i, c: (s, i, 0))
    grid_spec = pltpu.PrefetchScalarGridSpec(
        num_scalar_prefetch=1, grid=(n, R // tr),
        in_specs=[pl.BlockSpec((None, None, tr, C), lambda s, i, c: (s, c[0], i, 0)), blk], out_specs=blk)
    core = jnp.reshape(lax.axis_index("c"), (1,)).astype(jnp.int32)
    return pl.pallas_call(body, name=name, grid_spec=grid_spec, out_shape=jax.ShapeDtypeStruct(b.shape, out_dtype),
                          compiler_params=_cp(("parallel", "parallel")))(core, g, b)


def _pack_local(parts, group, tail=None):
    table, rows = group
    segs = []
    for n, r, rp, tr in table:
        w = parts[n].T if tr else parts[n]
        segs.append(jnp.pad(w.reshape(r, PACK_COLS), ((0, rp - r), (0, 0))))
    spare = rows - sum(rp for _, _, rp, _ in table)
    segs.append(jnp.zeros((spare, PACK_COLS), segs[0].dtype) if tail is None else tail)
    return jnp.concatenate(segs, axis=0)


CONV_W_BITS = 2 * 3 * 704
SPARE_EARLY = 16


def _conv_w_as_rows(conv_w_shard):
    bits = lax.bitcast_convert_type(conv_w_shard.reshape(-1), BF16).reshape(-1)
    return jnp.pad(bits, (0, SPARE_EARLY * PACK_COLS - CONV_W_BITS)).reshape(SPARE_EARLY, PACK_COLS)


def _conv_w_from_rows(gathered):
    bits = gathered[:, EARLY[1] - SPARE_EARLY:].reshape(N_DEV, -1)[:, :CONV_W_BITS].reshape(N_DEV, 3 * 704, 2)
    w = lax.bitcast_convert_type(bits, F32).reshape(N_DEV, 3, 704)
    return w.transpose(1, 0, 2).reshape(3, 2 * D_FF)


def _unpack_local(packed, like, group):
    out, off = {}, 0
    for n, r, rp, tr in group[0]:
        rows, cols = like[n].shape
        seg = packed[off:off + r]
        out[n] = (seg.reshape(cols, rows).T if tr else seg)[None]
        off += rp
    return out


def _segments(g, group):
    out, off = {}, 0
    for n, r, rp, _ in group[0]:
        out[n] = g[:, off:off + r]
        off += rp
    return out


def _pack_grads(parts, group):
    table, rows = group
    segs = [jnp.pad(parts[n], ((0, 0), (0, rp - parts[n].shape[1]), (0, 0))) for n, _, rp, _ in table]
    segs.append(jnp.zeros((N_DEV, rows - sum(rp for _, _, rp, _ in table), PACK_COLS), F32))
    return jnp.concatenate(segs, axis=1)


def _owner_rows_early(g):
    g_in = jnp.concatenate([g["w_in_t"][:2432], g["w_in_t"][2496:2528]], axis=0).reshape(N_DEV, 308, PACK_COLS)
    g_uq = g["w_uq_t"].reshape(N_DEV, 128, MLA_Q_RANK)[:, :96].reshape(N_DEV, 24, PACK_COLS)
    g_ukv = jnp.concatenate([g["w_k_t"].reshape(N_DEV, 128, MLA_KV_RANK)[:, :64],
                             g["w_v_t"].reshape(N_DEV, 64, MLA_KV_RANK)], axis=1).reshape(N_DEV, 16, PACK_COLS)
    return dict(w_in=g_in, w_uq=g_uq, w_ukv=g_ukv)


def _owner_rows_late(g):
    g_up = g["w_up_t"].reshape(2, 2, 2, 704, PACK_COLS).swapaxes(0, 1).reshape(N_DEV, 704, PACK_COLS)
    return dict(w_out=g["w_out"].reshape(N_DEV, 128, PACK_COLS), w_up=g_up,
                w_down=g["w_down"].reshape(N_DEV, 352, PACK_COLS))


def _reduce_to_pairs(gp, *, name):
    gp = gp.reshape(4, 2, gp.shape[1], PACK_COLS)
    return _add_own(gp, _swap_sibling(gp, name=name + "_swap"), out_dtype=BF16, name=name + "_sum")


def _interleave_ff(w):
    g, v = w[..., :D_FF], w[..., D_FF:]
    return jnp.concatenate([g[..., :FF_HALF], v[..., :FF_HALF], g[..., FF_HALF:], v[..., FF_HALF:]], axis=-1)


def _rope_tables(pos):
    p = pos.astype(F32)[:, None]
    inv_r = ROPE_BASE ** (-jnp.arange(0, RET_HEAD_DIM, 2, dtype=F32) / RET_HEAD_DIM)
    ang = p * jnp.tile(inv_r, 4)
    sign_r = jnp.tile(jnp.concatenate([-jnp.ones((32,), F32), jnp.ones((32,), F32)]), 2)
    cos_r, ss_r = jnp.cos(ang), jnp.sin(ang) * sign_r
    inv_m = ROPE_BASE ** (-jnp.arange(0, MLA_ROPE, 2, dtype=F32) / MLA_ROPE)
    ang = p * jnp.concatenate([jnp.zeros((64,), F32), inv_m, inv_m, jnp.zeros((32,), F32)])
    sign_m = jnp.concatenate([jnp.zeros((64,), F32), -jnp.ones((16,), F32), jnp.ones((16,), F32), jnp.zeros((32,), F32)])
    cos_m, ss_m = jnp.cos(ang), jnp.sin(ang) * sign_m
    return cos_r, ss_r, cos_m, ss_m


def _prep_early(gathered):
    seg = _segments(gathered, EARLY)
    w_in_t = seg["w_in"].reshape(IN_WIDTH, D_MODEL)
    z = lambda n: jnp.zeros((n, D_MODEL), BF16)
    w_in_t = jnp.concatenate([w_in_t[:2432], z(64), w_in_t[2432:2464], z(32)], axis=0)
    w_uq_t = jnp.pad(seg["w_uq"].reshape(MLA_HEADS, 96, MLA_Q_RANK), ((0, 0), (0, 32), (0, 0))).reshape(1024, MLA_Q_RANK)
    ukv = seg["w_ukv"].reshape(MLA_HEADS, 128, MLA_KV_RANK)
    w_k_t = jnp.pad(ukv[:, :64], ((0, 0), (0, 64), (0, 0))).reshape(1024, MLA_KV_RANK)
    w_v_t = ukv[:, 64:].reshape(512, MLA_KV_RANK)
    return dict(w_in_t=w_in_t, w_uq_t=w_uq_t, w_k_t=w_k_t, w_v_t=w_v_t)


def _prep_late(gathered):
    seg = _segments(gathered, LATE)
    w_up_t = seg["w_up"].reshape(2, 2, 2, 704, D_MODEL).swapaxes(0, 1).reshape(2 * D_FF, D_MODEL)
    return dict(w_out=seg["w_out"].reshape(1024, D_MODEL), w_up_t=w_up_t, w_down=seg["w_down"].reshape(D_FF, D_MODEL))


def _local_step(x, pos, tgt, early, sm, late):
    dist = not isinstance(late, dict)
    cos_r, ss_r, cos_m, ss_m = _rope_tables(pos)
    tabs = _ret_tables()

    if dist:
        h, gathered = _rmsnorm_fwd(x, sm["attn_norm_w"], gather=early, name="attn_norm")
        W = _prep_early(gathered)
        sm = {**sm, "conv_w": _interleave_ff(_conv_w_from_rows(gathered))}
    else:
        h = _rmsnorm_fwd(x, sm["attn_norm_w"], name="attn_norm")
        W = early
    proj = _mm_nt(h, W["w_in_t"], name="in_proj")
    y_ret, o_ret, qr, kr = _ret_fwd(proj, cos_r, ss_r, tabs, sm["ret_gn_w"], name="ret_fwd")
    q, k, v1, cqn, ckvn = _mla_prep_fwd(proj, sm["mla_q_norm_w"], sm["mla_kv_norm_w"], W["w_uq_t"], W["w_k_t"],
                                       W["w_v_t"], cos_m, ss_m, name="mla_prep")
    T = x.shape[0]
    tq = min(T, 512)
    if dist:
        y_mla, lse, gathered = _flash_fwd(q, k, v1, gather=late, name="mla_attn")
        W = {**W, **_prep_late(gathered)}
    else:
        y_mla, lse = _flash_fwd(q, k, v1, name="mla_attn")
        W = {**W, **late}
    mixed = (y_ret, y_mla)
    x1, h2, u, a = _up_proj_conv(x, y_ret, y_mla, W["w_out"], sm["ffn_norm_w"], W["w_up_t"], sm["conv_w"],
                                 sm["conv_b"], name="out_proj_ffn_up_conv")
    loss, dx2, dx2b, d_final = _down_proj_loss(a, W["w_down"], x1, tgt, sm["final_norm_w"], name="down_proj_loss")

    g = {}
    g["w_down"] = _mm_tn(a, dx2b, name="dw_down")
    da = _mm_nt(dx2b, W["w_down"], name="d_act")
    du, dcw0, dcw1, dcw2, dcb = _conv_bwd(u, da, sm["conv_w"], sm["conv_b"], name="conv_bwd")
    g["w_up_t"] = _mm_tn(du, h2, name="dw_up")
    dx1, d_ffn, g["w_out"] = _mm_norm_bwd(du, W["w_up_t"], x1, sm["ffn_norm_w"], dx2, left=mixed,
                                          name="d_h2_ffn_norm_bwd_dw_out")

    do_ret, dg, do_mla, delta, d_gn = _mix_bwd(dx1, W["w_out"], o_ret, proj, y_mla, sm["ret_gn_w"], name="d_mixed_mix_bwd")
    drq = _ret_bwd_dq(kr, proj, do_ret, cos_r, ss_r, tabs, name="ret_bwd_dq")
    delta_r = delta.reshape(MLA_HEADS, T // tq, 1, tq)
    if dist:
        gl = _pack_grads(_owner_rows_late(g), LATE).reshape(4, 2, LATE[1], PACK_COLS)
        drk, drv, theirs = _ret_bwd_dkv(qr, kr, proj, do_ret, cos_r, ss_r, tabs, swap=gl, name="ret_bwd_dkv")
        pair = _add_own(gl, theirs, out_dtype=BF16, name="grad_late_sum")
        dqt, dk, dv, slots_late = _flash_bwd(q, k, v1, do_mla, lse, delta_r, exchange=pair, name="mla_attn_bwd")
    else:
        drk, drv = _ret_bwd_dkv(qr, kr, proj, do_ret, cos_r, ss_r, tabs, name="ret_bwd_dkv")
        dqt, dk, dv = _flash_bwd(q, k, v1, do_mla, lse, delta_r, name="mla_attn_bwd")
        slots_late = None
    dq = dqt.transpose(1, 3, 0, 2).reshape(T, MLA_HEADS * 128)
    dproj, g["w_in_t"], g["w_uq_t"], g["w_k_t"], g["w_v_t"], d_qn, d_kvn = _mla_prep_bwd(
        dq, dk, dv, proj, sm["mla_q_norm_w"], sm["mla_kv_norm_w"], W["w_uq_t"], W["w_k_t"], W["w_v_t"], cos_m, ss_m,
        (drq, drk, drv, dg), cqn, ckvn, h, name="mla_prep_bwd")
    if dist:
        pair = _reduce_to_pairs(_pack_grads(_owner_rows_early(g), EARLY), name="grad_early")
        grad_x, d_attn, slots_early = _mm_norm_bwd(dproj, W["w_in_t"], x, sm["attn_norm_w"], dx1, exchange=pair,
                                                   name="d_h_attn_norm_bwd")
    else:
        grad_x, d_attn = _mm_norm_bwd(dproj, W["w_in_t"], x, sm["attn_norm_w"], dx1, name="d_h_attn_norm_bwd")
        slots_early = None

    small = dict(attn_norm_w=d_attn, ret_gn_w=d_gn, mla_q_norm_w=d_qn, mla_kv_norm_w=d_kvn, ffn_norm_w=d_ffn,
                 conv_b=dcb, final_norm_w=d_final, conv_w0=dcw0, conv_w1=dcw1, conv_w2=dcw2, loss=loss)
    return loss, grad_x, g, small, slots_early, slots_late


def kernel(x, positions, attn_norm_w, w_in, ret_gn_w, mla_q_norm_w, w_uq, mla_kv_norm_w, w_ukv, w_out, ffn_norm_w, w_up, conv_w, conv_b, w_down, final_norm_w, loss_target, m_attn_norm_w, m_w_in, m_ret_gn_w, m_mla_q_norm_w, m_w_uq, m_mla_kv_norm_w, m_w_ukv, m_w_out, m_ffn_norm_w, m_w_up, m_conv_w, m_conv_b, m_w_down, m_final_norm_w, v_attn_norm_w, v_w_in, v_ret_gn_w, v_mla_q_norm_w, v_w_uq, v_mla_kv_norm_w, v_w_ukv, v_w_out, v_ffn_norm_w, v_w_up, v_conv_w, v_conv_b, v_w_down, v_final_norm_w):
    a = dict(locals())
    x_, y_, c_ = _place()
    dev = 4 * x_ + 2 * y_ + c_

    shard = {n: a[n][0] for n in BIG_NAMES}
    shard16 = {n: w.astype(BF16) for n, w in shard.items()}
    sm = dict(attn_norm_w=attn_norm_w, ret_gn_w=ret_gn_w, mla_q_norm_w=mla_q_norm_w, mla_kv_norm_w=mla_kv_norm_w,
              ffn_norm_w=ffn_norm_w, final_norm_w=final_norm_w.reshape(1, D_MODEL), conv_b=_interleave_ff(conv_b))

    loss, grad_x, _, gs, slots_early, slots_late = _local_step(
        x[0], positions[0], loss_target[0], _pack_local(shard16, EARLY, tail=_conv_w_as_rows(conv_w[0])), sm,
        _pack_local(shard16, LATE))

    big = [{}, {}, {}, {}]
    for group, slots, tag, calls in ((EARLY, slots_early, "early", (("w_in", "w_uq", "w_ukv"),)),
                                     (LATE, slots_late, "late", (("w_out", "w_down"), ("w_up",)))):
        grads = _unpack_local(_sum_chips(slots, name="grad_sum_" + tag), shard, group)
        for names_c in calls:
            res = _adamw_multi([shard[n] for n in names_c], [a["m_" + n][0] for n in names_c],
                               [a["v_" + n][0] for n in names_c], [grads[n][0] for n in names_c],
                               name="adamw_" + "_".join(names_c))
            for kind in range(4):
                for n, r in zip(names_c, res[kind]):
                    big[kind][n] = r[None]

    packed = _pack_small(gs, name="pack_small_grads")
    tot = _sum_small(_all_gather(packed, name="gather_small_grads"), name="sum_small_grads")
    loss_out = tot["loss"][0, 0]
    g_cw = lax.dynamic_slice_in_dim(tot["conv_w"], dev * 704, 704, axis=1)

    def rows_of(prefix):
        return [a[prefix + n].reshape(1, size) for n, size in SMALL]

    sml = _adamw_multi(rows_of("") + [conv_w[0]], rows_of("m_") + [m_conv_w[0]], rows_of("v_") + [v_conv_w[0]],
                       [tot[n] for n, _ in SMALL] + [g_cw], name="adamw_small")
    cwo = [kind[-1] for kind in sml]

    def small_of(kind, n):
        return sml[kind][[nm for nm, _ in SMALL].index(n)].reshape(a[n].shape)

    names = ['attn_norm_w', 'w_in', 'ret_gn_w', 'mla_q_norm_w', 'w_uq', 'mla_kv_norm_w', 'w_ukv', 'w_out',
             'ffn_norm_w', 'w_up', 'conv_w', 'conv_b', 'w_down', 'final_norm_w']
    outs = [loss_out, grad_x[None]]
    for kind in range(4):
        for n in names:
            if n == "conv_w":
                outs.append(cwo[kind][None])
            elif n in big[kind]:
                outs.append(big[kind][n])
            else:
                outs.append(small_of(kind, n))
    return tuple(outs)
```

```python
import jax
import jax.numpy as jnp
from jax import lax
from jax.experimental import pallas as pl
from jax.experimental.pallas import tpu as pltpu

F32 = jnp.float32
BF16 = jnp.bfloat16
MESH = pl.DeviceIdType.MESH
ANY = pl.BlockSpec(memory_space=pl.ANY)

D_MODEL = 1024
RET_HEADS = 8
RET_HEAD_DIM = 64
RET_WIDTH = 512
RET_CHUNK = 128
RET_TILE = 2048
MLA_HEADS = 8
MLA_NOPE = 64
MLA_ROPE = 32
MLA_V = 64
MLA_Q_RANK = 256
MLA_KV_RANK = 128
MLA_WIDTH = 512
IN_WIDTH = 2464
IN_PAD = 2560
D_FF = 2816
FF_HALF = 1408
ROPE_BASE = 10000.0
EPS = 1e-6
SCALE = float((MLA_NOPE + MLA_ROPE) ** -0.5)
K_SCALE = 0.125
N_DEV = 8

ADAM_LR = 0.001
ADAM_B1 = 0.9
ADAM_B2 = 0.999
ADAM_EPS = 1e-08
ADAM_WD = 0.01
ADAM_STEP = 10

VMEM_LIMIT = 56 * 1024 * 1024
MM_BUDGET = 40 * 1024 * 1024
NEG = -1e30
FLASH_UNROLL = 4
FLASH_BWD_UNROLL = 3

PACK_COLS = 1024
EARLY = ((("w_in", 308, 320, True), ("w_uq", 24, 32, True), ("w_ukv", 16, 16, True)), 384)
LATE = ((("w_out", 128, 128, False), ("w_up", 704, 704, True), ("w_down", 352, 352, False)), 1200)
BIG_NAMES = ("w_in", "w_uq", "w_ukv", "w_out", "w_up", "w_down")
SMALL = (("attn_norm_w", 1024), ("ret_gn_w", 512), ("mla_q_norm_w", 256), ("mla_kv_norm_w", 128),
         ("ffn_norm_w", 1024), ("conv_b", 5632), ("final_norm_w", 1024))
SMALL_VECTORS = SMALL + (("conv_w0", 5632), ("conv_w1", 5632), ("conv_w2", 5632), ("loss", 128))
SMALL_ROWS = 32


def _cp(sem=None, vmem=VMEM_LIMIT):
    return pltpu.CompilerParams(dimension_semantics=sem, vmem_limit_bytes=vmem)


def _dot(a, b):
    return jnp.dot(a, b, preferred_element_type=F32)


def _dot_nt(a, b):
    return lax.dot_general(a, b, (((1,), (1,)), ((), ())), preferred_element_type=F32)


def _dot_tn(a, b):
    return lax.dot_general(a, b, (((0,), (0,)), ((), ())), preferred_element_type=F32)


def _sigmoid(x):
    return 0.5 * jnp.tanh(0.5 * x) + 0.5


def _partner(x, half, period):
    n = x.shape[-1]
    lane = lax.broadcasted_iota(jnp.int32, x.shape, 1)
    return jnp.where((lane % period) < half, pltpu.roll(x, n - half, 1), pltpu.roll(x, half, 1))


def _rope(x, cos, ss, half, period):
    return x * cos + _partner(x, half, period) * ss


def _rope_t(dy, cos, ss, half, period):
    return dy * cos - _partner(dy, half, period) * ss


def _head_masks(shape):
    lane = lax.broadcasted_iota(jnp.int32, shape, 1)
    m0 = (lane < 64).astype(F32)
    return m0, 1.0 - m0


def _mm_nt(a, b, *, name):
    M, K = a.shape
    N = b.shape[0]
    per_row = 2 * (K * a.dtype.itemsize + N * 4)
    tm = 128
    for cand in (512, 256):
        if M % cand == 0 and cand * per_row + 4 * K * N <= MM_BUDGET:
            tm = cand
            break
    tm = min(tm, M)

    def body(a_ref, b_ref, o_ref):
        o_ref[...] = _dot_nt(a_ref[...], b_ref[...])

    return pl.pallas_call(
        body, name=name, grid=(M // tm,),
        in_specs=[pl.BlockSpec((tm, K), lambda i: (i, 0)), pl.BlockSpec(b.shape, lambda i: (0, 0))],
        out_specs=pl.BlockSpec((tm, N), lambda i: (i, 0)), out_shape=jax.ShapeDtypeStruct((M, N), F32),
        compiler_params=_cp(("parallel",)))(a, b)


def _mm_tn(a, b, *, name):
    T, M = a.shape
    N = b.shape[1]
    tk = min(T, 512)

    def tile(n):
        for cand in (1408, 1280):
            if n > 1408 and n % cand == 0:
                return cand
        return n

    tm, tn = tile(M), tile(N)
    nk = T // tk

    def body(a_ref, b_ref, o_ref):
        @pl.when(pl.program_id(2) == 0)
        def _():
            o_ref[...] = jnp.zeros_like(o_ref)
        o_ref[...] += _dot_tn(a_ref[...], b_ref[...])

    return pl.pallas_call(
        body, name=name, grid=(M // tm, N // tn, nk),
        in_specs=[pl.BlockSpec((tk, tm), lambda i, j, k: (k, i)), pl.BlockSpec((tk, tn), lambda i, j, k: (k, j))],
        out_specs=pl.BlockSpec((tm, tn), lambda i, j, k: (i, j)),
        out_shape=jax.ShapeDtypeStruct((M, N), F32),
        compiler_params=_cp(("parallel", "parallel", "arbitrary")))(a, b)


def _rmsnorm_fwd(x, w, *, name, gather=None):
    T, D = x.shape
    tm = min(T, 1024)
    n = T // tm

    def body(x_ref, w_ref, *rest):
        if gather is not None:
            s_ref, o_ref, g_ref, *sems = rest
            start, forward, finish = _gather_phases(s_ref, g_ref, *sems)
            pl.when(pl.program_id(0) == 0)(start)
            pl.when(pl.program_id(0) == n // 2)(forward)
        else:
            o_ref, = rest
        xv = x_ref[...]
        r = lax.rsqrt(jnp.mean(xv * xv, axis=-1, keepdims=True) + EPS)
        o_ref[...] = (xv * r * w_ref[...]).astype(BF16)
        if gather is not None:
            pl.when(pl.program_id(0) == n - 1)(finish)

    in_specs = [pl.BlockSpec((tm, D), lambda i: (i, 0)), pl.BlockSpec((1, D), lambda i: (0, 0))]
    out_spec = pl.BlockSpec((tm, D), lambda i: (i, 0))
    out_shape = jax.ShapeDtypeStruct((T, D), BF16)
    if gather is None:
        return pl.pallas_call(body, name=name, grid=(n,), in_specs=in_specs, out_specs=out_spec, out_shape=out_shape,
                              compiler_params=_cp(("parallel",)))(x, w)
    return pl.pallas_call(
        body, name=name, grid=(n,), in_specs=in_specs + [ANY], out_specs=[out_spec, ANY],
        out_shape=[out_shape, jax.ShapeDtypeStruct((N_DEV,) + gather.shape, gather.dtype)],
        scratch_shapes=list(GATHER_SCRATCH), compiler_params=_cp(("arbitrary",)))(x, w, gather)


def _mm_norm_bwd(a, b, x, w, dres, *, name, exchange=None, left=None):
    T, K = a.shape
    D = b.shape[1]
    tm = min(T, 256 if K > 4096 else 512)
    n = T // tm
    n_left = 0 if left is None else len(left)

    def body(a_ref, b_ref, x_ref, w_ref, dr_ref, *rest):
        if exchange is not None:
            p_ref, dx_ref, dw_ref, got_ref, *sems = rest
            start, finish = _exchange_phases(p_ref, got_ref, *sems)
            pl.when(pl.program_id(0) == 0)(start)
        elif left is not None:
            left_refs, (dx_ref, dw_ref, gw_ref) = rest[:n_left], rest[n_left:]
        else:
            dx_ref, dw_ref = rest

        @pl.when(pl.program_id(0) == 0)
        def _():
            dw_ref[...] = jnp.zeros_like(dw_ref)
            if left is not None:
                gw_ref[...] = jnp.zeros_like(gw_ref)
        dh = _dot(a_ref[...], b_ref[...])
        xv = x_ref[...]
        r = lax.rsqrt(jnp.mean(xv * xv, axis=-1, keepdims=True) + EPS)
        xh = xv * r
        g = dh * w_ref[...]
        dx = dr_ref[...] + r * (g - xh * jnp.mean(g * xh, axis=-1, keepdims=True))
        dx_ref[...] = dx
        dw_ref[...] += jnp.sum(dh * xh, axis=0, keepdims=True)
        if left is not None:
            gw_ref[...] += _dot_tn(jnp.concatenate([r_[...] for r_ in left_refs], axis=1), dx.astype(BF16))
        if exchange is not None:
            pl.when(pl.program_id(0) == n - 1)(finish)

    row = pl.BlockSpec((tm, D), lambda i: (i, 0))
    vec = pl.BlockSpec((1, D), lambda i: (0, 0))
    in_specs = [pl.BlockSpec((tm, K), lambda i: (i, 0)), pl.BlockSpec((K, D), lambda i: (0, 0)), row, vec, row]
    out_shape = [jax.ShapeDtypeStruct((T, D), F32), jax.ShapeDtypeStruct((1, D), F32)]
    if left is not None:
        m = sum(p.shape[1] for p in left)
        return pl.pallas_call(
            body, name=name, grid=(n,),
            in_specs=in_specs + [pl.BlockSpec((tm, p.shape[1]), lambda i: (i, 0)) for p in left],
            out_specs=[row, vec, pl.BlockSpec((m, D), lambda i: (0, 0))],
            out_shape=out_shape + [jax.ShapeDtypeStruct((m, D), F32)],
            compiler_params=_cp(("arbitrary",)))(a, b, x, w, dres, *left)
    if exchange is None:
        return pl.pallas_call(body, name=name, grid=(n,), in_specs=in_specs, out_specs=[row, vec], out_shape=out_shape,
                              compiler_params=_cp(("arbitrary",)))(a, b, x, w, dres)
    return pl.pallas_call(
        body, name=name, grid=(n,), in_specs=in_specs + [ANY], out_specs=[row, vec, ANY],
        out_shape=out_shape + [jax.ShapeDtypeStruct(exchange.shape, exchange.dtype)],
        scratch_shapes=list(EXCHANGE_SCRATCH), compiler_params=_cp(("arbitrary",)))(a, b, x, w, dres, exchange)


def _down_proj_loss(a, w_down, x1, tgt, w, *, name):
    T, D = x1.shape
    K = a.shape[1]
    tm = min(T, 512)

    def body(a_ref, b_ref, x_ref, t_ref, w_ref, loss_ref, dx_ref, dxb_ref, dw_ref):
        @pl.when(pl.program_id(0) == 0)
        def _():
            dw_ref[...] = jnp.zeros_like(dw_ref)
            loss_ref[...] = jnp.zeros_like(loss_ref)
        xv = x_ref[...] + _dot(a_ref[...], b_ref[...])
        wv = w_ref[...]
        r = lax.rsqrt(jnp.mean(xv * xv, axis=-1, keepdims=True) + EPS)
        xh = xv * r
        e = xh * wv - t_ref[...]
        part = 0.5 * jnp.sum(jnp.mean(e * e, axis=-1, keepdims=True), axis=0, keepdims=True)
        loss_ref[...] += jnp.broadcast_to(part, loss_ref.shape)
        dy = e * (1.0 / D)
        g = dy * wv
        dx = r * (g - xh * jnp.mean(g * xh, axis=-1, keepdims=True))
        dx_ref[...] = dx
        dxb_ref[...] = dx.astype(BF16)
        dw_ref[...] += jnp.sum(dy * xh, axis=0, keepdims=True)

    row = pl.BlockSpec((tm, D), lambda i: (i, 0))
    vec = pl.BlockSpec((1, D), lambda i: (0, 0))
    return pl.pallas_call(
        body, name=name, grid=(T // tm,),
        in_specs=[pl.BlockSpec((tm, K), lambda i: (i, 0)), pl.BlockSpec((K, D), lambda i: (0, 0)), row, row, vec],
        out_specs=[pl.BlockSpec((1, 128), lambda i: (0, 0)), row, row, vec],
        out_shape=[jax.ShapeDtypeStruct((1, 128), F32), jax.ShapeDtypeStruct((T, D), F32),
                   jax.ShapeDtypeStruct((T, D), BF16), jax.ShapeDtypeStruct((1, D), F32)],
        compiler_params=_cp(("arbitrary",)))(a, w_down, x1, tgt, w)


def _ret_tables():
    C = RET_CHUNK
    h = jnp.arange(RET_HEADS, dtype=F32)
    log_gamma = jnp.log1p(-jnp.power(2.0, -5.0 - h))
    idx = jnp.arange(C, dtype=F32)
    diff = idx[:, None] - idx[None, :]
    dm = jnp.where(diff >= 0, jnp.exp(log_gamma[:, None, None] * jnp.maximum(diff, 0.0)), 0.0)
    dm = dm.reshape(4, 2 * C, C)
    lane_head = jnp.repeat(jnp.arange(RET_HEADS).reshape(4, 2), 64, axis=1)
    lg = log_gamma[lane_head]
    xi = jnp.exp(lg[:, None, :] * (idx[None, :, None] + 1.0))
    zeta = jnp.exp(lg[:, None, :] * (C - 1.0 - idx[None, :, None]))
    blk = (jnp.arange(128)[:, None] // 64) == (jnp.arange(128)[None, :] // 64)
    cd = jnp.where(blk[None], jnp.exp(lg * C)[:, :, None], 0.0)
    return dm.astype(F32), xi.astype(F32), zeta.astype(F32), cd.astype(F32)


def _ret_specs(tb, rev, nt, roped=False):
    def tmap(t):
        return (nt - 1 - t) if rev else t
    offsets = (0, 0, 8) if roped else (0, 4, 8)
    qkv = [pl.BlockSpec((tb, 128), lambda p, t, o=o: (tmap(t), o + p)) for o in offsets]
    rope = [pl.BlockSpec((tb, 128), lambda p, t: (tmap(t), 0))] * 2
    tabs = [pl.BlockSpec((None, 256, 128), lambda p, t: (p, 0, 0))] + \
           [pl.BlockSpec((None, 128, 128), lambda p, t: (p, 0, 0))] * 3
    return qkv, rope, tabs


def _ret_fwd(proj, cos, ss, tabs, gnw, *, name):
    T = proj.shape[0]
    tb = min(T, RET_TILE)
    nt = T // tb
    nchunk = tb // RET_CHUNK

    def body(q_ref, k_ref, v_ref, g_ref, cos_ref, ss_ref, dm_ref, xi_ref, zt_ref, cd_ref, gnw_ref,
             y_ref, o_ref, qr_ref, kr_ref, r_sc):
        @pl.when(pl.program_id(1) == 0)
        def _():
            r_sc[...] = jnp.zeros_like(r_sc)
        m0, m1 = _head_masks((128, 128))
        dm, xi, zt, cd = dm_ref[...], xi_ref[...], zt_ref[...], cd_ref[...]
        bm = (cd > 0).astype(F32)
        gnw = gnw_ref[...]
        for c in range(nchunk):
            rs = pl.ds(c * RET_CHUNK, RET_CHUNK)
            cs, sn = cos_ref[rs, :], ss_ref[rs, :]
            q = _rope(q_ref[rs, :], cs, sn, 32, 64)
            k = _rope(k_ref[rs, :], cs, sn, 32, 64) * K_SCALE
            v = v_ref[rs, :]
            kb, vb = k.astype(BF16), v.astype(BF16)
            qr_ref[rs, :] = q.astype(BF16)
            kr_ref[rs, :] = kb
            qs = jnp.concatenate([q * m0, q * m1], axis=0).astype(BF16)
            s = (_dot_nt(qs, kb) * dm).astype(BF16)
            vs = jnp.concatenate([v * m0, v * m1], axis=0).astype(BF16)
            o = _dot(jnp.concatenate([s[:128], s[128:]], axis=1), vs)
            r = r_sc[...]
            o = o + _dot(q.astype(BF16), r.astype(BF16)) * xi
            r_sc[...] = cd * r + bm * _dot_tn((k * zt).astype(BF16), vb)
            mu = (jnp.sum(o * m0, axis=1, keepdims=True) * m0 + jnp.sum(o * m1, axis=1, keepdims=True) * m1) * (1.0 / 64)
            d = o - mu
            dd = d * d
            var = (jnp.sum(dd * m0, axis=1, keepdims=True) * m0 + jnp.sum(dd * m1, axis=1, keepdims=True) * m1) * (1.0 / 64)
            oh = d * lax.rsqrt(var + EPS)
            g = g_ref[rs, :]
            y_ref[rs, :] = (g * _sigmoid(g) * (oh * gnw)).astype(BF16)
            o_ref[rs, :] = o

    qkv, rope, tspec = _ret_specs(tb, False, nt)
    gspec = pl.BlockSpec((tb, 128), lambda p, t: (t, 12 + p))
    out = pl.BlockSpec((tb, 128), lambda p, t: (t, p))
    return pl.pallas_call(
        body, name=name, grid=(4, nt),
        in_specs=qkv + [gspec] + rope + tspec + [pl.BlockSpec((1, 128), lambda p, t: (0, p))],
        out_specs=[out, out, out, out],
        out_shape=[jax.ShapeDtypeStruct((T, RET_WIDTH), BF16), jax.ShapeDtypeStruct((T, RET_WIDTH), F32),
                   jax.ShapeDtypeStruct((T, RET_WIDTH), BF16), jax.ShapeDtypeStruct((T, RET_WIDTH), BF16)],
        scratch_shapes=[pltpu.VMEM((128, 128), F32)],
        compiler_params=_cp(("parallel", "arbitrary")))(proj, proj, proj, proj, cos, ss, *tabs, gnw)


def _ret_bwd_dq(kr, proj, do, cos, ss, tabs, *, name):
    T = proj.shape[0]
    tb = min(T, RET_TILE)
    nt = T // tb
    nchunk = tb // RET_CHUNK

    def body(k_ref, v_ref, do_ref, cos_ref, ss_ref, dm_ref, xi_ref, zt_ref, cd_ref, dq_ref, r_sc):
        @pl.when(pl.program_id(1) == 0)
        def _():
            r_sc[...] = jnp.zeros_like(r_sc)
        m0, m1 = _head_masks((128, 128))
        dm, xi, zt, cd = dm_ref[...], xi_ref[...], zt_ref[...], cd_ref[...]
        bm = (cd > 0).astype(F32)
        for c in range(nchunk):
            rs = pl.ds(c * RET_CHUNK, RET_CHUNK)
            cs, sn = cos_ref[rs, :], ss_ref[rs, :]
            k = k_ref[rs, :].astype(F32)
            vb = v_ref[rs, :].astype(BF16)
            dob = do_ref[rs, :]
            dof = dob.astype(F32)
            dos = jnp.concatenate([dof * m0, dof * m1], axis=0).astype(BF16)
            a = (_dot_nt(dos, vb) * dm).astype(BF16)
            ks = jnp.concatenate([k * m0, k * m1], axis=0).astype(BF16)
            r = r_sc[...]
            dq = _dot(jnp.concatenate([a[:128], a[128:]], axis=1), ks) + _dot_nt(dob, r.astype(BF16)) * xi
            r_sc[...] = cd * r + bm * _dot_tn((k * zt).astype(BF16), vb)
            dq_ref[rs, :] = _rope_t(dq, cs, sn, 32, 64).astype(BF16)

    qkv, rope, tspec = _ret_specs(tb, False, nt, roped=True)
    blk = pl.BlockSpec((tb, 128), lambda p, t: (t, p))
    return pl.pallas_call(
        body, name=name, grid=(4, nt), in_specs=qkv[1:] + [blk] + rope + tspec, out_specs=blk,
        out_shape=jax.ShapeDtypeStruct((T, RET_WIDTH), BF16),
        scratch_shapes=[pltpu.VMEM((128, 128), F32)],
        compiler_params=_cp(("parallel", "arbitrary")))(kr, proj, do, cos, ss, *tabs)


def _ret_bwd_dkv(qr, kr, proj, do, cos, ss, tabs, *, name, swap=None):
    T = proj.shape[0]
    tb = min(T, RET_TILE)
    nt = T // tb
    nchunk = tb // RET_CHUNK

    def body(q_ref, k_ref, v_ref, do_ref, cos_ref, ss_ref, dm_ref, xi_ref, zt_ref, cd_ref, *rest):
        if swap is None:
            backward(q_ref, k_ref, v_ref, do_ref, cos_ref, ss_ref, dm_ref, xi_ref, zt_ref, cd_ref, *rest)
        else:
            g_ref, dk_ref, dv_ref, got_ref, u_sc, *sems = rest
            start, finish = _swap_phases(g_ref, got_ref, *sems)
            pl.when((pl.program_id(0) == 0) & (pl.program_id(1) == 0))(start)
            backward(q_ref, k_ref, v_ref, do_ref, cos_ref, ss_ref, dm_ref, xi_ref, zt_ref, cd_ref, dk_ref, dv_ref, u_sc)
            pl.when((pl.program_id(0) == 3) & (pl.program_id(1) == nt - 1))(finish)

    def backward(q_ref, k_ref, v_ref, do_ref, cos_ref, ss_ref, dm_ref, xi_ref, zt_ref, cd_ref, dk_ref, dv_ref, u_sc):
        @pl.when(pl.program_id(1) == 0)
        def _():
            u_sc[...] = jnp.zeros_like(u_sc)
        m0, m1 = _head_masks((128, 128))
        dm, xi, zt, cd = dm_ref[...], xi_ref[...], zt_ref[...], cd_ref[...]
        bm = (cd > 0).astype(F32)
        for c in reversed(range(nchunk)):
            rs = pl.ds(c * RET_CHUNK, RET_CHUNK)
            cs, sn = cos_ref[rs, :], ss_ref[rs, :]
            kb = k_ref[rs, :]
            q = q_ref[rs, :].astype(F32)
            vb = v_ref[rs, :].astype(BF16)
            dob = do_ref[rs, :]
            dof = dob.astype(F32)
            qs = jnp.concatenate([q * m0, q * m1], axis=0).astype(BF16)
            dos = jnp.concatenate([dof * m0, dof * m1], axis=0).astype(BF16)
            s = (_dot_nt(qs, kb) * dm).astype(BF16)
            a = (_dot_nt(dos, vb) * dm).astype(BF16)
            ub = u_sc[...].astype(BF16)
            dk = _dot_tn(a, qs) + _dot_nt(vb, ub) * zt
            dv = _dot_tn(s, dos) + _dot(kb, ub) * zt
            u_sc[...] = cd * u_sc[...] + bm * _dot_tn((q * xi).astype(BF16), dob)
            dk_ref[rs, :] = (_rope_t(dk, cs, sn, 32, 64) * K_SCALE).astype(BF16)
            dv_ref[rs, :] = dv.astype(BF16)

    qkv, rope, tspec = _ret_specs(tb, True, nt, roped=True)
    blk = pl.BlockSpec((tb, 128), lambda p, t: (nt - 1 - t, p))
    out_shape = [jax.ShapeDtypeStruct((T, RET_WIDTH), BF16)] * 2
    if swap is None:
        return pl.pallas_call(
            body, name=name, grid=(4, nt), in_specs=qkv + [blk] + rope + tspec, out_specs=[blk, blk],
            out_shape=out_shape, scratch_shapes=[pltpu.VMEM((128, 128), F32)],
            compiler_params=_cp(("parallel", "arbitrary")))(qr, kr, proj, do, cos, ss, *tabs)
    return pl.pallas_call(
        body, name=name, grid=(4, nt), in_specs=qkv + [blk] + rope + tspec + [ANY], out_specs=[blk, blk, ANY],
        out_shape=out_shape + [jax.ShapeDtypeStruct((4,) + swap.shape[2:], swap.dtype)],
        scratch_shapes=[pltpu.VMEM((128, 128), F32)] + list(SWAP_SCRATCH),
        compiler_params=_cp(("arbitrary", "arbitrary")))(qr, kr, proj, do, cos, ss, *tabs, swap)


def _mix_bwd(dx1, w_out, o_ret, proj, y_mla, gnw, *, name):
    T = dx1.shape[0]
    tm = min(T, 512)

    def body(dx_ref, wo_ref, o_ref, g_ref, ym_ref, gnw_ref, do_ref, dg_ref, dom_ref, dl_ref, dw_ref, dm_ref):
        @pl.when(pl.program_id(0) == 0)
        def _():
            dw_ref[...] = jnp.zeros_like(dw_ref)
        dm_ref[...] = _dot_nt(dx_ref[...].astype(BF16), wo_ref[...])
        m0, m1 = _head_masks((tm, 128))
        lane = lax.broadcasted_iota(jnp.int32, (tm, 128), 1)
        delta = jnp.zeros((tm, 128), F32)

        def gsum(z):
            return jnp.sum(z * m0, axis=1, keepdims=True) * m0 + jnp.sum(z * m1, axis=1, keepdims=True) * m1

        for p in range(4):
            cs = slice(128 * p, 128 * p + 128)
            dy = dm_ref[:, cs]
            o = o_ref[:, cs]
            g = g_ref[:, cs]
            w = gnw_ref[:, cs]
            d = o - gsum(o) * (1.0 / 64)
            rstd = lax.rsqrt(gsum(d * d) * (1.0 / 64) + EPS)
            oh = d * rstd
            sg = _sigmoid(g)
            dn = dy * (g * sg)
            dg_ref[:, cs] = (dy * (oh * w) * (sg * (1.0 + g * (1.0 - sg)))).astype(BF16)
            dw_ref[:, cs] += jnp.sum(dn * oh, axis=0, keepdims=True)
            doh = dn * w
            do = rstd * (doh - gsum(doh) * (1.0 / 64) - oh * (gsum(doh * oh) * (1.0 / 64)))
            do_ref[:, cs] = do.astype(BF16)
            dom = dm_ref[:, 512 + 128 * p:512 + 128 * p + 128]
            dom_ref[:, cs] = dom.astype(BF16)
            pr = dom * ym_ref[:, cs].astype(F32)
            delta = jnp.where(lane == 2 * p, jnp.sum(pr * m0, axis=1, keepdims=True), delta)
            delta = jnp.where(lane == 2 * p + 1, jnp.sum(pr * m1, axis=1, keepdims=True), delta)
        dl_ref[...] = delta.T[0:MLA_HEADS]

    half = pl.BlockSpec((tm, 512), lambda i: (i, 0))
    return pl.pallas_call(
        body, name=name, grid=(T // tm,),
        in_specs=[pl.BlockSpec((tm, D_MODEL), lambda i: (i, 0)), pl.BlockSpec(w_out.shape, lambda i: (0, 0)), half,
                  pl.BlockSpec((tm, 512), lambda i: (i, 3)), half, pl.BlockSpec((1, 512), lambda i: (0, 0))],
        out_specs=[half, half, half, pl.BlockSpec((MLA_HEADS, tm), lambda i: (0, i)),
                   pl.BlockSpec((1, 512), lambda i: (0, 0))],
        out_shape=[jax.ShapeDtypeStruct((T, 512), BF16)] * 3 + [jax.ShapeDtypeStruct((MLA_HEADS, T), F32),
                                                                jax.ShapeDtypeStruct((1, 512), F32)],
        scratch_shapes=[pltpu.VMEM((tm, 1024), F32)],
        compiler_params=_cp(("arbitrary",)))(dx1, w_out, o_ret, proj, y_mla, gnw)


def _mla_prep_fwd(proj, qnw, kvnw, wuq, wk, wv, cos, ss, *, name):
    T = proj.shape[0]
    tm = min(T, 512)

    def body(lat_ref, qnw_ref, kvnw_ref, wuq_ref, wk_ref, wv_ref, cos_ref, ss_ref,
             q_ref, k_ref, v_ref, cqn_ref, ckvn_ref):
        cq = lat_ref[:, 0:256]
        ckv = lat_ref[:, 256:384]
        g3 = lat_ref[:, 384:512]
        cqn = (cq * lax.rsqrt(jnp.mean(cq * cq, axis=-1, keepdims=True) + EPS) * qnw_ref[...]).astype(BF16)
        ckvn = (ckv * lax.rsqrt(jnp.mean(ckv * ckv, axis=-1, keepdims=True) + EPS) * kvnw_ref[...]).astype(BF16)
        cqn_ref[...] = cqn
        ckvn_ref[...] = ckvn
        cs, sn = cos_ref[...], ss_ref[...]
        q = _dot_nt(cqn, wuq_ref[...])
        k = _dot_nt(ckvn, wk_ref[...])
        kpe = _rope(g3, cs, sn, 16, 32)
        for h in range(MLA_HEADS):
            hs = slice(128 * h, 128 * h + 128)
            q_ref[:, hs] = (_rope(q[:, hs], cs, sn, 16, 32) * SCALE).astype(BF16)
            k_ref[:, hs] = (k[:, hs] + kpe).astype(BF16)
        v = _dot_nt(ckvn, wv_ref[...])
        lane = lax.broadcasted_iota(jnp.int32, (tm, 128), 1)
        for p in range(4):
            vp = v[:, 128 * p:128 * p + 128]
            v_ref[:, 256 * p:256 * p + 128] = jnp.where(lane < 64, vp, 1.0).astype(BF16)
            v_ref[:, 256 * p + 128:256 * p + 256] = jnp.where(lane < 64, 1.0, vp).astype(BF16)

    def full(shape):
        return pl.BlockSpec(shape, lambda i: (0, 0))

    def row(w):
        return pl.BlockSpec((tm, w), lambda i: (i, 0))

    return pl.pallas_call(
        body, name=name, grid=(T // tm,),
        in_specs=[pl.BlockSpec((tm, 512), lambda i: (i, 4)), full((1, 256)), full((1, 128)), full((1024, 256)),
                  full((1024, 128)), full((512, 128)), row(128), row(128)],
        out_specs=[row(1024), row(1024), row(1024), row(256), row(128)],
        out_shape=[jax.ShapeDtypeStruct((T, 1024), BF16), jax.ShapeDtypeStruct((T, 1024), BF16),
                   jax.ShapeDtypeStruct((T, 1024), BF16), jax.ShapeDtypeStruct((T, 256), BF16),
                   jax.ShapeDtypeStruct((T, 128), BF16)],
        compiler_params=_cp(("parallel",)))(proj, qnw, kvnw, wuq, wk, wv, cos, ss)


def _mla_prep_bwd(dq, dk, dv, proj, qnw, kvnw, wuq_t, wk_t, wv_t, cos, ss, ret_grads, cqn, ckvn, h, *, name):
    T = proj.shape[0]
    tm = min(T, 256)

    def body(dq_ref, dk_ref, dv_ref, lat_ref, qnw_ref, kvnw_ref, wuq_ref, wk_ref, wv_ref, cos_ref, ss_ref,
             rq_ref, rk_ref, rv_ref, rg_ref, cqn_ref, ckvn_ref, h_ref,
             dproj_ref, gwin_ref, gwuq_ref, gwk_ref, gwv_ref, dqnw_ref, dkvnw_ref, dqp_ref):
        for j, r in enumerate((rq_ref, rk_ref, rv_ref, rg_ref)):
            dproj_ref[:, 512 * j:512 * j + 512] = r[...]
        dlat_ref = dproj_ref.at[:, 2048:2560]

        @pl.when(pl.program_id(0) == 0)
        def _():
            for r in (gwin_ref, gwuq_ref, gwk_ref, gwv_ref, dqnw_ref, dkvnw_ref):
                r[...] = jnp.zeros_like(r)
        cs, sn = cos_ref[...], ss_ref[...]
        dkpe = jnp.zeros((tm, 128), F32)
        for h in range(MLA_HEADS):
            hs = slice(128 * h, 128 * h + 128)
            dqp_ref[:, hs] = _rope_t(dq_ref[:, hs] * SCALE, cs, sn, 16, 32).astype(BF16)
            dkpe = dkpe + dk_ref[:, hs]
        lane = lax.broadcasted_iota(jnp.int32, (tm, 128), 1)
        rope_lane = (lane >= MLA_NOPE) & (lane < MLA_NOPE + MLA_ROPE)
        dg3 = jnp.where(rope_lane, _rope_t(jnp.where(rope_lane, dkpe, 0.0), cs, sn, 16, 32), 0.0)

        def norm_bwd(x, w, dn):
            r = lax.rsqrt(jnp.mean(x * x, axis=-1, keepdims=True) + EPS)
            xh = x * r
            g = dn * w
            return r * (g - xh * jnp.mean(g * xh, axis=-1, keepdims=True)), jnp.sum(dn * xh, axis=0, keepdims=True)

        dqp = dqp_ref[...]
        dkb = dk_ref[...].astype(BF16)
        dvb = dv_ref[...]
        dcqn = _dot(dqp, wuq_ref[...])
        dcq, dqnw = norm_bwd(lat_ref[:, 0:256], qnw_ref[...], dcqn)
        dckvn = _dot(dkb, wk_ref[...]) + _dot(dvb, wv_ref[...])
        dckv, dkvnw = norm_bwd(lat_ref[:, 256:384], kvnw_ref[...], dckvn)
        gwuq_ref[...] += _dot_tn(dqp, cqn_ref[...])
        gwk_ref[...] += _dot_tn(dkb, ckvn_ref[...])
        gwv_ref[...] += _dot_tn(dvb, ckvn_ref[...])
        dqnw_ref[...] += dqnw
        dkvnw_ref[...] += dkvnw
        dlat_ref[:, 0:256] = dcq.astype(BF16)
        dlat_ref[:, 256:384] = dckv.astype(BF16)
        dlat_ref[:, 384:512] = dg3.astype(BF16)
        gwin_ref[...] += _dot_tn(dproj_ref[...], h_ref[...])

    def full(shape):
        return pl.BlockSpec(shape, lambda i: (0, 0))

    def row(w):
        return pl.BlockSpec((tm, w), lambda i: (i, 0))

    return pl.pallas_call(
        body, name=name, grid=(T // tm,),
        in_specs=[row(1024), row(1024), row(512), pl.BlockSpec((tm, 512), lambda i: (i, 4)), full((1, 256)),
                  full((1, 128)), full((1024, 256)), full((1024, 128)), full((512, 128)), row(128), row(128)]
                 + [row(512)] * 4 + [row(256), row(128), row(D_MODEL)],
        out_specs=[row(IN_PAD), full((IN_PAD, D_MODEL)), full((1024, 256)), full((1024, 128)), full((512, 128)),
                   full((1, 256)), full((1, 128))],
        out_shape=[jax.ShapeDtypeStruct((T, IN_PAD), BF16), jax.ShapeDtypeStruct((IN_PAD, D_MODEL), F32),
                   jax.ShapeDtypeStruct((1024, 256), F32), jax.ShapeDtypeStruct((1024, 128), F32),
                   jax.ShapeDtypeStruct((512, 128), F32), jax.ShapeDtypeStruct((1, 256), F32),
                   jax.ShapeDtypeStruct((1, 128), F32)],
        scratch_shapes=[pltpu.VMEM((tm, 1024), BF16)],
        compiler_params=_cp(("arbitrary",)))(dq, dk, dv, proj, qnw, kvnw, wuq_t, wk_t, wv_t, cos, ss, *ret_grads,
                                             cqn, ckvn, h)


def _flash_fwd(q, k, v1, *, name, gather=None):
    T = q.shape[0]
    tq = min(T, 512)
    tk = tq
    nq = T // tq

    def body(q_ref, k_ref, v_ref, *rest):
        if gather is None:
            y_ref, lse_ref = rest
        else:
            x_ref, y_ref, lse_ref, g_ref, *sems = rest
            start, forward, finish = _gather_phases(x_ref, g_ref, *sems)
            pl.when((pl.program_id(0) == 0) & (pl.program_id(1) == 0))(start)
            pl.when((pl.program_id(0) == 1) & (pl.program_id(1) == 0))(forward)
        attend(q_ref, k_ref, v_ref, y_ref, lse_ref)
        if gather is not None:
            pl.when((pl.program_id(0) == 3) & (pl.program_id(1) == nq - 1))(finish)

    def attend(q_ref, k_ref, v_ref, y_ref, lse_ref):
        qi = pl.program_id(1)
        row = lax.broadcasted_iota(jnp.int32, (tq, tk), 0)
        col = lax.broadcasted_iota(jnp.int32, (tq, tk), 1)

        def step(kb, carry, masked):
            ks = pl.ds(pl.multiple_of(kb * tk, tk), tk)
            new = []
            for h in range(2):
                hs = slice(128 * h, 128 * h + 128)
                m, acc = carry[h]
                s = _dot_nt(q_ref[:, hs], k_ref[ks, hs])
                if masked:
                    s = jnp.where(col <= row, s, NEG)
                mn = jnp.maximum(m, jnp.max(s, axis=1, keepdims=True))
                p = jnp.exp((s - mn).astype(BF16))
                acc = jnp.exp(m - mn) * acc + _dot(p, v_ref[ks, hs])
                new.append((mn, acc))
            return tuple(new)

        def unrolled(j, c):
            for u in range(FLASH_UNROLL):
                c = step(FLASH_UNROLL * j + u, c, False)
            return c

        init = (jnp.full((tq, 1), NEG, F32), jnp.zeros((tq, 128), F32))
        carry = lax.fori_loop(0, qi // FLASH_UNROLL, unrolled, (init, init))
        carry = lax.fori_loop(FLASH_UNROLL * (qi // FLASH_UNROLL), qi, lambda kb, c: step(kb, c, False), carry)
        (ma, acca), (mb, accb) = step(qi, carry, True)
        lane = lax.broadcasted_iota(jnp.int32, (tq, 128), 1)
        la, lb = pltpu.roll(acca, 64, 1), pltpu.roll(accb, 64, 1)
        y_ref[...] = jnp.where(lane < 64, acca / la, accb / lb).astype(BF16)
        lse_ref[0, 0] = jnp.broadcast_to(ma + jnp.log(acca[:, 64:65]), (tq, 128)).T[0:1]
        lse_ref[1, 0] = jnp.broadcast_to(mb + jnp.log(accb[:, 0:1]), (tq, 128)).T[0:1]

    in_specs = [pl.BlockSpec((tq, 256), lambda p, i: (i, p)), pl.BlockSpec((T, 256), lambda p, i: (0, p)),
                pl.BlockSpec((T, 256), lambda p, i: (0, p))]
    out_specs = [pl.BlockSpec((tq, 128), lambda p, i: (i, p)), pl.BlockSpec((2, 1, 1, tq), lambda p, i: (p, i, 0, 0))]
    out_shape = [jax.ShapeDtypeStruct((T, MLA_WIDTH), BF16), jax.ShapeDtypeStruct((MLA_HEADS, nq, 1, tq), F32)]
    if gather is None:
        return pl.pallas_call(body, name=name, grid=(4, nq), in_specs=in_specs, out_specs=out_specs,
                              out_shape=out_shape, compiler_params=_cp(("parallel", "arbitrary")))(q, k, v1)
    return pl.pallas_call(
        body, name=name, grid=(4, nq), in_specs=in_specs + [ANY], out_specs=out_specs + [ANY],
        out_shape=out_shape + [jax.ShapeDtypeStruct((N_DEV,) + gather.shape, gather.dtype)],
        scratch_shapes=list(GATHER_SCRATCH),
        compiler_params=_cp(("arbitrary", "arbitrary")))(q, k, v1, gather)


def _flash_bwd(q, k, v, do, lse, delta, *, name, exchange=None):
    T = q.shape[0]
    tq = min(T, 512)
    tk = tq
    nq = T // tq

    def body(q_ref, k_ref, v_ref, do_ref, lse_ref, dl_ref, *rest):
        if exchange is None:
            backward(q_ref, k_ref, v_ref, do_ref, lse_ref, dl_ref, *rest)
        else:
            p_ref, dqt_ref, dk_ref, dv_ref, got_ref, *sems = rest
            start, finish = _exchange_phases(p_ref, got_ref, *sems)
            pl.when((pl.program_id(0) == 0) & (pl.program_id(1) == 0))(start)
            backward(q_ref, k_ref, v_ref, do_ref, lse_ref, dl_ref, dqt_ref, dk_ref, dv_ref)
            pl.when((pl.program_id(0) == 3) & (pl.program_id(1) == nq - 1))(finish)

    def backward(q_ref, k_ref, v_ref, do_ref, lse_ref, dl_ref, dqt_ref, dk_ref, dv_ref):
        kb = pl.program_id(1)

        @pl.when(kb == 0)
        def _():
            dqt_ref[...] = jnp.zeros_like(dqt_ref)
        krow = lax.broadcasted_iota(jnp.int32, (tk, tq), 0)
        qcol = lax.broadcasted_iota(jnp.int32, (tk, tq), 1)
        masks = _head_masks((tk, 128))
        vms = [(v_ref[:, 128 * h:128 * h + 128].astype(F32) * masks[h]).astype(BF16) for h in range(2)]

        def step(qi, carry, masked):
            qs = pl.ds(pl.multiple_of(qi * tq, tq), tq)
            dob = do_ref[qs, :]
            dof = dob.astype(F32)
            dks, dv_acc = list(carry[:2]), carry[2]
            for h in range(2):
                hs = slice(128 * h, 128 * h + 128)
                kh = k_ref[:, hs]
                qh = q_ref[qs, hs]
                st = _dot_nt(kh, qh)
                pt = jnp.exp((st - lse_ref[h, qi]).astype(BF16))
                if masked:
                    pt = jnp.where(krow <= qcol, pt, jnp.zeros_like(pt))
                dv_acc = dv_acc + _dot(pt, (dof * masks[h]).astype(BF16))
                dpt = _dot_nt(vms[h], dob)
                dst = pt * (dpt - dl_ref[h, qi]).astype(BF16)
                dks[h] = dks[h] + _dot(dst, qh)
                dqt_ref[qi, hs, :] += _dot_tn(kh, dst)
            return dks[0], dks[1], dv_acc

        zero = jnp.zeros((tk, 128), F32)
        carry = step(kb, (zero, zero, zero), True)

        def unrolled(j, c):
            for u in range(FLASH_BWD_UNROLL):
                c = step(kb + 1 + FLASH_BWD_UNROLL * j + u, c, False)
            return c

        trips = (nq - 1 - kb) // FLASH_BWD_UNROLL
        carry = lax.fori_loop(0, trips, unrolled, carry)
        dk0, dk1, dv_acc = lax.fori_loop(kb + 1 + FLASH_BWD_UNROLL * trips, nq, lambda qi, c: step(qi, c, False), carry)
        dk_ref[:, 0:128] = dk0
        dk_ref[:, 128:256] = dk1
        dv_ref[...] = dv_acc.astype(BF16)

    stat = pl.BlockSpec((2, nq, 1, tq), lambda p, j: (p, 0, 0, 0))
    in_specs = [pl.BlockSpec((T, 256), lambda p, j: (0, p)), pl.BlockSpec((tk, 256), lambda p, j: (j, p)),
                pl.BlockSpec((tk, 256), lambda p, j: (j, p)), pl.BlockSpec((T, 128), lambda p, j: (0, p)), stat, stat]
    out_specs = [pl.BlockSpec((None, nq, 256, tq), lambda p, j: (p, 0, 0, 0)),
                 pl.BlockSpec((tk, 256), lambda p, j: (j, p)), pl.BlockSpec((tk, 128), lambda p, j: (j, p))]
    out_shape = [jax.ShapeDtypeStruct((4, nq, 256, tq), F32), jax.ShapeDtypeStruct((T, 1024), F32),
                 jax.ShapeDtypeStruct((T, MLA_WIDTH), BF16)]
    if exchange is None:
        return pl.pallas_call(body, name=name, grid=(4, nq), in_specs=in_specs, out_specs=out_specs,
                              out_shape=out_shape,
                              compiler_params=_cp(("parallel", "arbitrary")))(q, k, v, do, lse, delta)
    return pl.pallas_call(
        body, name=name, grid=(4, nq), in_specs=in_specs + [ANY], out_specs=out_specs + [ANY],
        out_shape=out_shape + [jax.ShapeDtypeStruct(exchange.shape, exchange.dtype)],
        scratch_shapes=list(EXCHANGE_SCRATCH),
        compiler_params=_cp(("arbitrary", "arbitrary")))(q, k, v, do, lse, delta, exchange)


def _shift_down(x, n, prev8):
    r = pltpu.roll(x, n, 0)
    row = lax.broadcasted_iota(jnp.int32, prev8.shape, 0)
    first = jnp.where(row < n, pltpu.roll(prev8, n, 0), r[:8])
    if x.shape[0] == 8:
        return first
    return jnp.concatenate([first, r[8:]], axis=0)


def _shift_up(x, n, next8):
    tm = x.shape[0]
    r = pltpu.roll(x, tm - n, 0)
    row = lax.broadcasted_iota(jnp.int32, next8.shape, 0)
    last = jnp.where(row >= 8 - n, pltpu.roll(next8, 8 - n, 0), r[tm - 8:])
    return jnp.concatenate([r[:tm - 8], last], axis=0)


def _conv_pre(u, prev8, cw_ref, cb_ref):
    p1 = _shift_down(u, 1, prev8)
    p2 = _shift_down(u, 2, prev8)
    up = cb_ref[...] + cw_ref[0:1, :] * p2 + cw_ref[1:2, :] * p1 + cw_ref[2:3, :] * u
    return up, p1, p2


def _up_proj_conv(x, y_ret, y_mla, w_out, nw, w_up_t, cw, cb, *, name):
    T, K = x.shape
    tm = min(T, 256)

    def body(x_ref, yr_ref, ym_ref, wo_ref, nw_ref, w_ref, cw_ref, cb_ref, x1_ref, h_ref, u_ref, a_ref, carry_sc):
        @pl.when(pl.program_id(0) == 0)
        def _():
            carry_sc[...] = jnp.zeros_like(carry_sc)
        xv = x_ref[...] + _dot(jnp.concatenate([yr_ref[...], ym_ref[...]], axis=1), wo_ref[...])
        x1_ref[...] = xv
        h = (xv * lax.rsqrt(jnp.mean(xv * xv, axis=-1, keepdims=True) + EPS) * nw_ref[...]).astype(BF16)
        h_ref[...] = h
        for blk in range(2):
            ups = []
            for half in range(2):
                cs = slice((2 * blk + half) * FF_HALF, (2 * blk + half + 1) * FF_HALF)
                u = _dot_nt(h, w_ref[cs, :])
                u_ref[:, cs] = u
                prev = carry_sc[:, cs]
                ups.append(cb_ref[:, cs] + cw_ref[0:1, cs] * _shift_down(u, 2, prev)
                           + cw_ref[1:2, cs] * _shift_down(u, 1, prev) + cw_ref[2:3, cs] * u)
                carry_sc[:, cs] = u[tm - 8:]
            gate, val = ups
            a_ref[:, blk * FF_HALF:(blk + 1) * FF_HALF] = (gate * _sigmoid(gate) * val).astype(BF16)

    def full(shape):
        return pl.BlockSpec(shape, lambda i: (0, 0))

    return pl.pallas_call(
        body, name=name, grid=(T // tm,),
        in_specs=[pl.BlockSpec((tm, K), lambda i: (i, 0)), pl.BlockSpec((tm, RET_WIDTH), lambda i: (i, 0)),
                  pl.BlockSpec((tm, MLA_WIDTH), lambda i: (i, 0)), full(w_out.shape), full(nw.shape),
                  full(w_up_t.shape), full(cw.shape), full(cb.shape)],
        out_specs=[pl.BlockSpec((tm, K), lambda i: (i, 0)), pl.BlockSpec((tm, K), lambda i: (i, 0)),
                   pl.BlockSpec((tm, 2 * D_FF), lambda i: (i, 0)), pl.BlockSpec((tm, D_FF), lambda i: (i, 0))],
        out_shape=[jax.ShapeDtypeStruct((T, K), F32), jax.ShapeDtypeStruct((T, K), BF16),
                   jax.ShapeDtypeStruct((T, 2 * D_FF), F32), jax.ShapeDtypeStruct((T, D_FF), BF16)],
        scratch_shapes=[pltpu.VMEM((8, 2 * D_FF), F32)],
        compiler_params=_cp(("arbitrary",)))(x, y_ret, y_mla, w_out, nw, w_up_t, cw, cb)


def _conv_bwd(u, da, cw, cb, *, name):
    T = u.shape[0]
    tm = min(T, 512)
    W = 2 * FF_HALF
    nt = T // tm

    def body(u_ref, prev_ref, next_ref, da_ref, dan_ref, cw_ref, cb_ref, du_ref, dw0_ref, dw1_ref, dw2_ref, db_ref):
        i = pl.program_id(1)

        @pl.when(i == 0)
        def _():
            for r in (dw0_ref, dw1_ref, dw2_ref, db_ref):
                r[...] = jnp.zeros_like(r)

        def dpre(u, prev8, da):
            up, p1, p2 = _conv_pre(u, prev8, cw_ref, cb_ref)
            gate, val = up[:, :FF_HALF], up[:, FF_HALF:]
            sg = _sigmoid(gate)
            dgate = da * val * (sg * (1.0 + gate * (1.0 - sg)))
            dval = da * (gate * sg)
            return jnp.concatenate([dgate, dval], axis=1), p1, p2

        u = u_ref[...]
        prev = jnp.where(i > 0, prev_ref[...], 0.0)
        dup, p1, p2 = dpre(u, prev, da_ref[...])
        dupn, _, _ = dpre(next_ref[...], u[tm - 8:], dan_ref[...])
        dupn = jnp.where(i < nt - 1, dupn, 0.0)
        du = cw_ref[2:3, :] * dup + cw_ref[1:2, :] * _shift_up(dup, 1, dupn) + cw_ref[0:1, :] * _shift_up(dup, 2, dupn)
        du_ref[...] = du.astype(BF16)
        dw0_ref[...] += jnp.sum(dup * p2, axis=0, keepdims=True)
        dw1_ref[...] += jnp.sum(dup * p1, axis=0, keepdims=True)
        dw2_ref[...] += jnp.sum(dup * u, axis=0, keepdims=True)
        db_ref[...] += jnp.sum(dup, axis=0, keepdims=True)

    nxt = lambda j, i: (jnp.minimum((i + 1) * (tm // 8), T // 8 - 1), j)
    vec = pl.BlockSpec((1, W), lambda j, i: (0, j))
    return pl.pallas_call(
        body, name=name, grid=(2, nt),
        in_specs=[pl.BlockSpec((tm, W), lambda j, i: (i, j)),
                  pl.BlockSpec((8, W), lambda j, i: (jnp.maximum(i * (tm // 8) - 1, 0), j)),
                  pl.BlockSpec((8, W), nxt),
                  pl.BlockSpec((tm, FF_HALF), lambda j, i: (i, j)), pl.BlockSpec((8, FF_HALF), nxt),
                  pl.BlockSpec((3, W), lambda j, i: (0, j)), vec],
        out_specs=[pl.BlockSpec((tm, W), lambda j, i: (i, j)), vec, vec, vec, vec],
        out_shape=[jax.ShapeDtypeStruct((T, 2 * D_FF), BF16)] + [jax.ShapeDtypeStruct((1, 2 * D_FF), F32)] * 4,
        compiler_params=_cp(("parallel", "arbitrary")))(u, u, u, da, da, cw, cb)


def _sum_chips(slots, *, name):
    ns, R, C = slots.shape
    tr = _row_tile(R)

    def body(g_ref, o_ref):
        g = g_ref[0].astype(F32)
        for s in range(1, ns):
            g = g + g_ref[s].astype(F32)
        o_ref[...] = g

    return pl.pallas_call(
        body, name=name, grid=(R // tr,), in_specs=[pl.BlockSpec((ns, tr, C), lambda i: (0, i, 0))],
        out_specs=pl.BlockSpec((tr, C), lambda i: (i, 0)), out_shape=jax.ShapeDtypeStruct((R, C), F32),
        compiler_params=_cp(("parallel",)))(slots)


def _place():
    return lax.axis_index("x"), lax.axis_index("y"), lax.axis_index("c")


GATHER_SCRATCH = (pltpu.SemaphoreType.DMA((7,)), pltpu.SemaphoreType.DMA((7,)), pltpu.SemaphoreType.DMA)
EXCHANGE_SCRATCH = (pltpu.SemaphoreType.DMA((3,)), pltpu.SemaphoreType.DMA((3,)), pltpu.SemaphoreType.DMA)


def _gather_phases(x_ref, out_ref, send_sems, recv_sems, local_sem):
    x_, y_, c_ = _place()
    me, sibling = (x_, y_, c_), (x_, y_, 1 - c_)
    chips = [(1 - x_, y_), (x_, 1 - y_), (1 - x_, 1 - y_)]

    def slot(px, py, pc):
        return out_ref.at[4 * px + 2 * py + pc]

    def copy(k, block, to, src=None):
        return pltpu.make_async_remote_copy(
            src_ref=slot(*block) if src is None else src, dst_ref=slot(*block),
            send_sem=send_sems.at[k], recv_sem=recv_sems.at[k], device_id=to, device_id_type=MESH)

    def mine():
        return pltpu.make_async_copy(x_ref, slot(*me), local_sem)

    def first():
        return [copy(0, me, sibling, src=x_ref)] + [copy(1 + j, me, (*chip, c_), src=x_ref)
                                                     for j, chip in enumerate(chips)]

    def passed():
        return [copy(4 + j, (*chip, c_), sibling) for j, chip in enumerate(chips)]

    def start():
        mine().start()
        for cp in first():
            cp.start()

    def forward():
        fwd = passed()
        for j, chip in enumerate(chips):
            copy(1 + j, (*chip, c_), me).wait_recv()
            fwd[j].start()

    def finish():
        copy(0, sibling, me).wait_recv()
        for j, chip in enumerate(chips):
            copy(4 + j, (*chip, 1 - c_), me).wait_recv()
        for cp in first() + passed():
            cp.wait_send()
        mine().wait()

    return start, forward, finish


def _exchange_phases(p_ref, out_ref, send_sems, recv_sems, local_sem):
    x_, y_, c_ = _place()
    me_k = 2 * x_ + y_
    chips = [(1 - x_, y_), (x_, 1 - y_), (1 - x_, 1 - y_)]

    def local():
        return pltpu.make_async_copy(p_ref.at[me_k], out_ref.at[me_k], local_sem)

    def copy(j, src_k, dst_k, chip):
        return pltpu.make_async_remote_copy(
            src_ref=p_ref.at[src_k], dst_ref=out_ref.at[dst_k], send_sem=send_sems.at[j],
            recv_sem=recv_sems.at[j], device_id=(*chip, c_), device_id_type=MESH)

    def sends():
        return [copy(j, 2 * px + py, me_k, (px, py)) for j, (px, py) in enumerate(chips)]

    def start():
        local().start()
        for cp in sends():
            cp.start()

    def finish():
        for j, (px, py) in enumerate(chips):
            copy(j, me_k, 2 * px + py, (px, py)).wait_recv()
        for cp in sends():
            cp.wait_send()
        local().wait()

    return start, finish


def _all_gather(x, *, name):
    def body(x_ref, out_ref, send_sems, recv_sems, local_sem):
        for phase in _gather_phases(x_ref, out_ref, send_sems, recv_sems, local_sem):
            phase()

    spec = pl.BlockSpec(memory_space=pltpu.VMEM)
    return pl.pallas_call(
        body, name=name, out_shape=jax.ShapeDtypeStruct((N_DEV,) + x.shape, x.dtype),
        in_specs=[spec], out_specs=spec, scratch_shapes=list(GATHER_SCRATCH),
        compiler_params=pltpu.CompilerParams(vmem_limit_bytes=VMEM_LIMIT))(x)


def _small_rows():
    table, row = [], 0
    for n, size in SMALL_VECTORS:
        table.append((n, size, row))
        row += -(-size // PACK_COLS)
    return table


def _ff_chunk_source(c):
    block, off = divmod(c * 128, FF_HALF)
    return (0, 2, 1, 3)[block] * FF_HALF + off


def _pack_small(parts, *, name):
    table = _small_rows()

    def body(*refs):
        out = refs[-1]
        out[...] = jnp.zeros_like(out)
        for ref, (n, size, row) in zip(refs, table):
            if size != 2 * D_FF:
                out[row:row + 1, 0:size] = ref[...]
                continue
            for c in range(size // 128):
                src = _ff_chunk_source(c)
                r, lane = divmod(c * 128, PACK_COLS)
                out[row + r:row + r + 1, lane:lane + 128] = ref[:, src:src + 128]

    return pl.pallas_call(body, name=name, out_shape=jax.ShapeDtypeStruct((SMALL_ROWS, PACK_COLS), F32))(
        *[parts[n] for n, _, _ in table])


def _sum_small(g, *, name):
    table = _small_rows()
    shapes = [(n, size) for n, size, _ in table if not n.startswith("conv_w")]
    shapes.insert(7, ("conv_w", 2 * D_FF))

    def body(g_ref, *outs):
        def total(row, width):
            acc = g_ref[0, row:row + 1, 0:width]
            for d in range(1, N_DEV):
                acc = acc + g_ref[d, row:row + 1, 0:width]
            return acc

        out_of = {n: o for (n, _), o in zip(shapes, outs)}
        for n, size, row in table:
            o, j = (out_of["conv_w"], int(n[-1])) if n.startswith("conv_w") else (out_of[n], 0)
            for i in range(-(-size // PACK_COLS)):
                width = min(PACK_COLS, size - PACK_COLS * i)
                o[j:j + 1, PACK_COLS * i:PACK_COLS * i + width] = total(row + i, width)

    out_shape = [jax.ShapeDtypeStruct((3 if n == "conv_w" else 1, size), F32) for n, size in shapes]
    res = pl.pallas_call(body, name=name, out_shape=out_shape)(g)
    return {n: r for (n, _), r in zip(shapes, res)}


def _adamw_multi(ws, ms, vs, gs, *, name):
    k = len(ws)

    def body(*refs):
        w_refs, m_refs, v_refs, g_refs = (refs[i * k:(i + 1) * k] for i in range(4))
        outs = refs[4 * k:]
        for i in range(k):
            g = g_refs[i][...]
            mn = ADAM_B1 * m_refs[i][...] + (1.0 - ADAM_B1) * g
            vn = ADAM_B2 * v_refs[i][...] + (1.0 - ADAM_B2) * (g * g)
            m_hat = mn / (1.0 - ADAM_B1 ** ADAM_STEP)
            v_hat = vn / (1.0 - ADAM_B2 ** ADAM_STEP)
            outs[i][...] = g
            outs[k + i][...] = -ADAM_LR * (m_hat / (jnp.sqrt(v_hat) + ADAM_EPS) + ADAM_WD * w_refs[i][...])
            outs[2 * k + i][...] = mn
            outs[3 * k + i][...] = vn

    out_shape = [jax.ShapeDtypeStruct(w.shape, F32) for _ in range(4) for w in ws]
    res = pl.pallas_call(body, name=name, out_shape=out_shape, compiler_params=_cp())(*ws, *ms, *vs, *gs)
    return [res[i * k:(i + 1) * k] for i in range(4)]


SWAP_SCRATCH = (pltpu.SemaphoreType.DMA((4,)), pltpu.SemaphoreType.DMA((4,)))


def _swap_phases(g_ref, out_ref, send_sems, recv_sems):
    x_, y_, c_ = _place()

    def copies():
        return [pltpu.make_async_remote_copy(src_ref=g_ref.at[k, 1 - c_], dst_ref=out_ref.at[k],
                                             send_sem=send_sems.at[k], recv_sem=recv_sems.at[k],
                                             device_id=(x_, y_, 1 - c_), device_id_type=MESH) for k in range(4)]

    def start():
        for cp in copies():
            cp.start()

    def finish():
        for cp in copies():
            cp.wait()

    return start, finish


def _swap_sibling(g, *, name):
    def body(g_ref, out_ref, send_sems, recv_sems):
        for phase in _swap_phases(g_ref, out_ref, send_sems, recv_sems):
            phase()

    return pl.pallas_call(
        body, name=name, out_shape=jax.ShapeDtypeStruct((4,) + g.shape[2:], g.dtype), in_specs=[ANY], out_specs=ANY,
        scratch_shapes=list(SWAP_SCRATCH))(g)


def _row_tile(R):
    for cand in (256, 400, 200):
        if R % cand == 0:
            return cand
    return R


def _add_own(g, b, *, name, out_dtype):
    n, _, R, C = g.shape
    tr = _row_tile(R)

    def body(c_ref, g_ref, b_ref, o_ref):
        del c_ref
        o_ref[...] = (g_ref[...] + b_ref[...]).astype(out_dtype)

    blk = pl.BlockSpec((None, tr, C), lambda s, i, c: (s, i, 0))
    grid_spec = pltpu.PrefetchScalarGridSpec(
        num_scalar_prefetch=1, grid=(n, R // tr),
        in_specs=[pl.BlockSpec((None, None, tr, C), lambda s, i, c: (s, c[0], i, 0)), blk], out_specs=blk)
    core = jnp.reshape(lax.axis_index("c"), (1,)).astype(jnp.int32)
    return pl.pallas_call(body, name=name, grid_spec=grid_spec, out_shape=jax.ShapeDtypeStruct(b.shape, out_dtype),
                          compiler_params=_cp(("parallel", "parallel")))(core, g, b)


def _pack_local(parts, group, tail=None):
    table, rows = group
    segs = []
    for n, r, rp, tr in table:
        w = parts[n].T if tr else parts[n]
        segs.append(jnp.pad(w.reshape(r, PACK_COLS), ((0, rp - r), (0, 0))))
    spare = rows - sum(rp for _, _, rp, _ in table)
    segs.append(jnp.zeros((spare, PACK_COLS), segs[0].dtype) if tail is None else tail)
    return jnp.concatenate(segs, axis=0)


CONV_W_BITS = 2 * 3 * 704
SPARE_EARLY = 16


def _conv_w_as_rows(conv_w_shard):
    bits = lax.bitcast_convert_type(conv_w_shard.reshape(-1), BF16).reshape(-1)
    return jnp.pad(bits, (0, SPARE_EARLY * PACK_COLS - CONV_W_BITS)).reshape(SPARE_EARLY, PACK_COLS)


def _conv_w_from_rows(gathered):
    bits = gathered[:, EARLY[1] - SPARE_EARLY:].reshape(N_DEV, -1)[:, :CONV_W_BITS].reshape(N_DEV, 3 * 704, 2)
    w = lax.bitcast_convert_type(bits, F32).reshape(N_DEV, 3, 704)
    return w.transpose(1, 0, 2).reshape(3, 2 * D_FF)


def _unpack_local(packed, like, group):
    out, off = {}, 0
    for n, r, rp, tr in group[0]:
        rows, cols = like[n].shape
        seg = packed[off:off + r]
        out[n] = (seg.reshape(cols, rows).T if tr else seg)[None]
        off += rp
    return out


def _segments(g, group):
    out, off = {}, 0
    for n, r, rp, _ in group[0]:
        out[n] = g[:, off:off + r]
        off += rp
    return out


def _pack_grads(parts, group):
    table, rows = group
    segs = [jnp.pad(parts[n], ((0, 0), (0, rp - parts[n].shape[1]), (0, 0))) for n, _, rp, _ in table]
    segs.append(jnp.zeros((N_DEV, rows - sum(rp for _, _, rp, _ in table), PACK_COLS), F32))
    return jnp.concatenate(segs, axis=1)


def _owner_rows_early(g):
    g_in = jnp.concatenate([g["w_in_t"][:2432], g["w_in_t"][2496:2528]], axis=0).reshape(N_DEV, 308, PACK_COLS)
    g_uq = g["w_uq_t"].reshape(N_DEV, 128, MLA_Q_RANK)[:, :96].reshape(N_DEV, 24, PACK_COLS)
    g_ukv = jnp.concatenate([g["w_k_t"].reshape(N_DEV, 128, MLA_KV_RANK)[:, :64],
                             g["w_v_t"].reshape(N_DEV, 64, MLA_KV_RANK)], axis=1).reshape(N_DEV, 16, PACK_COLS)
    return dict(w_in=g_in, w_uq=g_uq, w_ukv=g_ukv)


def _owner_rows_late(g):
    g_up = g["w_up_t"].reshape(2, 2, 2, 704, PACK_COLS).swapaxes(0, 1).reshape(N_DEV, 704, PACK_COLS)
    return dict(w_out=g["w_out"].reshape(N_DEV, 128, PACK_COLS), w_up=g_up,
                w_down=g["w_down"].reshape(N_DEV, 352, PACK_COLS))


def _reduce_to_pairs(gp, *, name):
    gp = gp.reshape(4, 2, gp.shape[1], PACK_COLS)
    return _add_own(gp, _swap_sibling(gp, name=name + "_swap"), out_dtype=BF16, name=name + "_sum")


def _interleave_ff(w):
    g, v = w[..., :D_FF], w[..., D_FF:]
    return jnp.concatenate([g[..., :FF_HALF], v[..., :FF_HALF], g[..., FF_HALF:], v[..., FF_HALF:]], axis=-1)


def _rope_tables(pos):
    p = pos.astype(F32)[:, None]
    inv_r = ROPE_BASE ** (-jnp.arange(0, RET_HEAD_DIM, 2, dtype=F32) / RET_HEAD_DIM)
    ang = p * jnp.tile(inv_r, 4)
    sign_r = jnp.tile(jnp.concatenate([-jnp.ones((32,), F32), jnp.ones((32,), F32)]), 2)
    cos_r, ss_r = jnp.cos(ang), jnp.sin(ang) * sign_r
    inv_m = ROPE_BASE ** (-jnp.arange(0, MLA_ROPE, 2, dtype=F32) / MLA_ROPE)
    ang = p * jnp.concatenate([jnp.zeros((64,), F32), inv_m, inv_m, jnp.zeros((32,), F32)])
    sign_m = jnp.concatenate([jnp.zeros((64,), F32), -jnp.ones((16,), F32), jnp.ones((16,), F32), jnp.zeros((32,), F32)])
    cos_m, ss_m = jnp.cos(ang), jnp.sin(ang) * sign_m
    return cos_r, ss_r, cos_m, ss_m


def _prep_early(gathered):
    seg = _segments(gathered, EARLY)
    w_in_t = seg["w_in"].reshape(IN_WIDTH, D_MODEL)
    z = lambda n: jnp.zeros((n, D_MODEL), BF16)
    w_in_t = jnp.concatenate([w_in_t[:2432], z(64), w_in_t[2432:2464], z(32)], axis=0)
    w_uq_t = jnp.pad(seg["w_uq"].reshape(MLA_HEADS, 96, MLA_Q_RANK), ((0, 0), (0, 32), (0, 0))).reshape(1024, MLA_Q_RANK)
    ukv = seg["w_ukv"].reshape(MLA_HEADS, 128, MLA_KV_RANK)
    w_k_t = jnp.pad(ukv[:, :64], ((0, 0), (0, 64), (0, 0))).reshape(1024, MLA_KV_RANK)
    w_v_t = ukv[:, 64:].reshape(512, MLA_KV_RANK)
    return dict(w_in_t=w_in_t, w_uq_t=w_uq_t, w_k_t=w_k_t, w_v_t=w_v_t)


def _prep_late(gathered):
    seg = _segments(gathered, LATE)
    w_up_t = seg["w_up"].reshape(2, 2, 2, 704, D_MODEL).swapaxes(0, 1).reshape(2 * D_FF, D_MODEL)
    return dict(w_out=seg["w_out"].reshape(1024, D_MODEL), w_up_t=w_up_t, w_down=seg["w_down"].reshape(D_FF, D_MODEL))


def _local_step(x, pos, tgt, early, sm, late):
    dist = not isinstance(late, dict)
    cos_r, ss_r, cos_m, ss_m = _rope_tables(pos)
    tabs = _ret_tables()

    if dist:
        h, gathered = _rmsnorm_fwd(x, sm["attn_norm_w"], gather=early, name="attn_norm")
        W = _prep_early(gathered)
        sm = {**sm, "conv_w": _interleave_ff(_conv_w_from_rows(gathered))}
    else:
        h = _rmsnorm_fwd(x, sm["attn_norm_w"], name="attn_norm")
        W = early
    proj = _mm_nt(h, W["w_in_t"], name="in_proj")
    y_ret, o_ret, qr, kr = _ret_fwd(proj, cos_r, ss_r, tabs, sm["ret_gn_w"], name="ret_fwd")
    q, k, v1, cqn, ckvn = _mla_prep_fwd(proj, sm["mla_q_norm_w"], sm["mla_kv_norm_w"], W["w_uq_t"], W["w_k_t"],
                                       W["w_v_t"], cos_m, ss_m, name="mla_prep")
    T = x.shape[0]
    tq = min(T, 512)
    if dist:
        y_mla, lse, gathered = _flash_fwd(q, k, v1, gather=late, name="mla_attn")
        W = {**W, **_prep_late(gathered)}
    else:
        y_mla, lse = _flash_fwd(q, k, v1, name="mla_attn")
        W = {**W, **late}
    mixed = (y_ret, y_mla)
    x1, h2, u, a = _up_proj_conv(x, y_ret, y_mla, W["w_out"], sm["ffn_norm_w"], W["w_up_t"], sm["conv_w"],
                                 sm["conv_b"], name="out_proj_ffn_up_conv")
    loss, dx2, dx2b, d_final = _down_proj_loss(a, W["w_down"], x1, tgt, sm["final_norm_w"], name="down_proj_loss")

    g = {}
    g["w_down"] = _mm_tn(a, dx2b, name="dw_down")
    da = _mm_nt(dx2b, W["w_down"], name="d_act")
    du, dcw0, dcw1, dcw2, dcb = _conv_bwd(u, da, sm["conv_w"], sm["conv_b"], name="conv_bwd")
    g["w_up_t"] = _mm_tn(du, h2, name="dw_up")
    dx1, d_ffn, g["w_out"] = _mm_norm_bwd(du, W["w_up_t"], x1, sm["ffn_norm_w"], dx2, left=mixed,
                                          name="d_h2_ffn_norm_bwd_dw_out")

    do_ret, dg, do_mla, delta, d_gn = _mix_bwd(dx1, W["w_out"], o_ret, proj, y_mla, sm["ret_gn_w"], name="d_mixed_mix_bwd")
    drq = _ret_bwd_dq(kr, proj, do_ret, cos_r, ss_r, tabs, name="ret_bwd_dq")
    delta_r = delta.reshape(MLA_HEADS, T // tq, 1, tq)
    if dist:
        gl = _pack_grads(_owner_rows_late(g), LATE).reshape(4, 2, LATE[1], PACK_COLS)
        drk, drv, theirs = _ret_bwd_dkv(qr, kr, proj, do_ret, cos_r, ss_r, tabs, swap=gl, name="ret_bwd_dkv")
        pair = _add_own(gl, theirs, out_dtype=BF16, name="grad_late_sum")
        dqt, dk, dv, slots_late = _flash_bwd(q, k, v1, do_mla, lse, delta_r, exchange=pair, name="mla_attn_bwd")
    else:
        drk, drv = _ret_bwd_dkv(qr, kr, proj, do_ret, cos_r, ss_r, tabs, name="ret_bwd_dkv")
        dqt, dk, dv = _flash_bwd(q, k, v1, do_mla, lse, delta_r, name="mla_attn_bwd")
        slots_late = None
    dq = dqt.transpose(1, 3, 0, 2).reshape(T, MLA_HEADS * 128)
    dproj, g["w_in_t"], g["w_uq_t"], g["w_k_t"], g["w_v_t"], d_qn, d_kvn = _mla_prep_bwd(
        dq, dk, dv, proj, sm["mla_q_norm_w"], sm["mla_kv_norm_w"], W["w_uq_t"], W["w_k_t"], W["w_v_t"], cos_m, ss_m,
        (drq, drk, drv, dg), cqn, ckvn, h, name="mla_prep_bwd")
    if dist:
        pair = _reduce_to_pairs(_pack_grads(_owner_rows_early(g), EARLY), name="grad_early")
        grad_x, d_attn, slots_early = _mm_norm_bwd(dproj, W["w_in_t"], x, sm["attn_norm_w"], dx1, exchange=pair,
                                                   name="d_h_attn_norm_bwd")
    else:
        grad_x, d_attn = _mm_norm_bwd(dproj, W["w_in_t"], x, sm["attn_norm_w"], dx1, name="d_h_attn_norm_bwd")
        slots_early = None

    small = dict(attn_norm_w=d_attn, ret_gn_w=d_gn, mla_q_norm_w=d_qn, mla_kv_norm_w=d_kvn, ffn_norm_w=d_ffn,
                 conv_b=dcb, final_norm_w=d_final, conv_w0=dcw0, conv_w1=dcw1, conv_w2=dcw2, loss=loss)
    return loss, grad_x, g, small, slots_early, slots_late


def kernel(x, positions, attn_norm_w, w_in, ret_gn_w, mla_q_norm_w, w_uq, mla_kv_norm_w, w_ukv, w_out, ffn_norm_w, w_up, conv_w, conv_b, w_down, final_norm_w, loss_target, m_attn_norm_w, m_w_in, m_ret_gn_w, m_mla_q_norm_w, m_w_uq, m_mla_kv_norm_w, m_w_ukv, m_w_out, m_ffn_norm_w, m_w_up, m_conv_w, m_conv_b, m_w_down, m_final_norm_w, v_attn_norm_w, v_w_in, v_ret_gn_w, v_mla_q_norm_w, v_w_uq, v_mla_kv_norm_w, v_w_ukv, v_w_out, v_ffn_norm_w, v_w_up, v_conv_w, v_conv_b, v_w_down, v_final_norm_w):
    a = dict(locals())
    x_, y_, c_ = _place()
    dev = 4 * x_ + 2 * y_ + c_

    shard = {n: a[n][0] for n in BIG_NAMES}
    shard16 = {n: w.astype(BF16) for n, w in shard.items()}
    sm = dict(attn_norm_w=attn_norm_w, ret_gn_w=ret_gn_w, mla_q_norm_w=mla_q_norm_w, mla_kv_norm_w=mla_kv_norm_w,
              ffn_norm_w=ffn_norm_w, final_norm_w=final_norm_w.reshape(1, D_MODEL), conv_b=_interleave_ff(conv_b))

    loss, grad_x, _, gs, slots_early, slots_late = _local_step(
        x[0], positions[0], loss_target[0], _pack_local(shard16, EARLY, tail=_conv_w_as_rows(conv_w[0])), sm,
        _pack_local(shard16, LATE))

    big = [{}, {}, {}, {}]
    for group, slots, tag, calls in ((EARLY, slots_early, "early", (("w_in", "w_uq", "w_ukv"),)),
                                     (LATE, slots_late, "late", (("w_out", "w_down"), ("w_up",)))):
        grads = _unpack_local(_sum_chips(slots, name="grad_sum_" + tag), shard, group)
        for names_c in calls:
            res = _adamw_multi([shard[n] for n in names_c], [a["m_" + n][0] for n in names_c],
                               [a["v_" + n][0] for n in names_c], [grads[n][0] for n in names_c],
                               name="adamw_" + "_".join(names_c))
            for kind in range(4):
                for n, r in zip(names_c, res[kind]):
                    big[kind][n] = r[None]

    packed = _pack_small(gs, name="pack_small_grads")
    tot = _sum_small(_all_gather(packed, name="gather_small_grads"), name="sum_small_grads")
    loss_out = tot["loss"][0, 0]
    g_cw = lax.dynamic_slice_in_dim(tot["conv_w"], dev * 704, 704, axis=1)

    def rows_of(prefix):
        return [a[prefix + n].reshape(1, size) for n, size in SMALL]

    sml = _adamw_multi(rows_of("") + [conv_w[0]], rows_of("m_") + [m_conv_w[0]], rows_of("v_") + [v_conv_w[0]],
                       [tot[n] for n, _ in SMALL] + [g_cw], name="adamw_small")
    cwo = [kind[-1] for kind in sml]

    def small_of(kind, n):
        return sml[kind][[nm for nm, _ in SMALL].index(n)].reshape(a[n].shape)

    names = ['attn_norm_w', 'w_in', 'ret_gn_w', 'mla_q_norm_w', 'w_uq', 'mla_kv_norm_w', 'w_ukv', 'w_out',
             'ffn_norm_w', 'w_up', 'conv_w', 'conv_b', 'w_down', 'final_norm_w']
    outs = [loss_out, grad_x[None]]
    for kind in range(4):
        for n in names:
            if n == "conv_w":
                outs.append(cwo[kind][None])
            elif n in big[kind]:
                outs.append(big[kind][n])
            else:
                outs.append(small_of(kind, n))
    return tuple(outs)
```

```python
import jax
import jax.numpy as jnp
from jax import lax
from jax.experimental import pallas as pl
from jax.experimental.pallas import tpu as pltpu

F32 = jnp.float32
BF16 = jnp.bfloat16
MESH = pl.DeviceIdType.MESH
ANY = pl.BlockSpec(memory_space=pl.ANY)

D_MODEL = 1024
RET_HEADS = 8
RET_HEAD_DIM = 64
RET_WIDTH = 512
RET_CHUNK = 128
RET_TILE = 4096
MLA_HEADS = 8
MLA_NOPE = 64
MLA_ROPE = 32
MLA_V = 64
MLA_Q_RANK = 256
MLA_KV_RANK = 128
MLA_WIDTH = 512
IN_WIDTH = 2464
IN_PAD = 2560
D_FF = 2816
FF_HALF = 1408
ROPE_BASE = 10000.0
EPS = 1e-6
SCALE = float((MLA_NOPE + MLA_ROPE) ** -0.5)
K_SCALE = 0.125
N_DEV = 8

ADAM_LR = 0.001
ADAM_B1 = 0.9
ADAM_B2 = 0.999
ADAM_EPS = 1e-08
ADAM_WD = 0.01
ADAM_STEP = 10

VMEM_LIMIT = 56 * 1024 * 1024
MM_BUDGET = 40 * 1024 * 1024
NEG = -1e30
FLASH_UNROLL = 4
FLASH_BWD_UNROLL = 3

PACK_COLS = 1024
EARLY = ((("w_in", 308, 320, True), ("w_uq", 24, 32, True), ("w_ukv", 16, 16, True)), 384)
LATE = ((("w_out", 128, 128, False), ("w_up", 704, 704, True), ("w_down", 352, 352, False)), 1200)
BIG_NAMES = ("w_in", "w_uq", "w_ukv", "w_out", "w_up", "w_down")
SMALL = (("attn_norm_w", 1024), ("ret_gn_w", 512), ("mla_q_norm_w", 256), ("mla_kv_norm_w", 128),
         ("ffn_norm_w", 1024), ("conv_b", 5632), ("final_norm_w", 1024))
SMALL_VECTORS = SMALL + (("conv_w0", 5632), ("conv_w1", 5632), ("conv_w2", 5632), ("loss", 128))
SMALL_ROWS = 32


def _cp(sem=None, vmem=VMEM_LIMIT):
    return pltpu.CompilerParams(dimension_semantics=sem, vmem_limit_bytes=vmem)


def _dot(a, b):
    return jnp.dot(a, b, preferred_element_type=F32)


def _dot_nt(a, b):
    return lax.dot_general(a, b, (((1,), (1,)), ((), ())), preferred_element_type=F32)


def _dot_tn(a, b):
    return lax.dot_general(a, b, (((0,), (0,)), ((), ())), preferred_element_type=F32)


def _sigmoid(x):
    return 0.5 * jnp.tanh(0.5 * x) + 0.5


def _partner(x, half, period):
    n = x.shape[-1]
    lane = lax.broadcasted_iota(jnp.int32, x.shape, 1)
    return jnp.where((lane % period) < half, pltpu.roll(x, n - half, 1), pltpu.roll(x, half, 1))


def _rope(x, cos, ss, half, period):
    return x * cos + _partner(x, half, period) * ss


def _rope_t(dy, cos, ss, half, period):
    return dy * cos - _partner(dy, half, period) * ss


def _head_masks(shape):
    lane = lax.broadcasted_iota(jnp.int32, shape, 1)
    m0 = (lane < 64).astype(F32)
    return m0, 1.0 - m0


def _mm_nt(a, b, *, name):
    M, K = a.shape
    N = b.shape[0]
    per_row = 2 * (K * a.dtype.itemsize + N * 4)
    tm = 128
    for cand in (512, 256):
        if M % cand == 0 and cand * per_row + 4 * K * N <= MM_BUDGET:
            tm = cand
            break
    tm = min(tm, M)

    def body(a_ref, b_ref, o_ref):
        o_ref[...] = _dot_nt(a_ref[...], b_ref[...])

    return pl.pallas_call(
        body, name=name, grid=(M // tm,),
        in_specs=[pl.BlockSpec((tm, K), lambda i: (i, 0)), pl.BlockSpec(b.shape, lambda i: (0, 0))],
        out_specs=pl.BlockSpec((tm, N), lambda i: (i, 0)), out_shape=jax.ShapeDtypeStruct((M, N), F32),
        compiler_params=_cp(("parallel",)))(a, b)


def _mm_tn(a, b, *, name):
    T, M = a.shape
    N = b.shape[1]
    tk = min(T, 512)

    def tile(n):
        for cand in (1408, 1280):
            if n > 1408 and n % cand == 0:
                return cand
        return n

    tm, tn = tile(M), tile(N)
    nk = T // tk

    def body(a_ref, b_ref, o_ref):
        @pl.when(pl.program_id(2) == 0)
        def _():
            o_ref[...] = jnp.zeros_like(o_ref)
        o_ref[...] += _dot_tn(a_ref[...], b_ref[...])

    return pl.pallas_call(
        body, name=name, grid=(M // tm, N // tn, nk),
        in_specs=[pl.BlockSpec((tk, tm), lambda i, j, k: (k, i)), pl.BlockSpec((tk, tn), lambda i, j, k: (k, j))],
        out_specs=pl.BlockSpec((tm, tn), lambda i, j, k: (i, j)),
        out_shape=jax.ShapeDtypeStruct((M, N), F32),
        compiler_params=_cp(("parallel", "parallel", "arbitrary")))(a, b)


def _rmsnorm_fwd(x, w, *, name, gather=None):
    T, D = x.shape
    tm = min(T, 1024)
    n = T // tm

    def body(x_ref, w_ref, *rest):
        if gather is not None:
            s_ref, o_ref, g_ref, *sems = rest
            start, forward, finish = _gather_phases(s_ref, g_ref, *sems)
            pl.when(pl.program_id(0) == 0)(start)
            pl.when(pl.program_id(0) == n // 2)(forward)
        else:
            o_ref, = rest
        xv = x_ref[...]
        r = lax.rsqrt(jnp.mean(xv * xv, axis=-1, keepdims=True) + EPS)
        o_ref[...] = (xv * r * w_ref[...]).astype(BF16)
        if gather is not None:
            pl.when(pl.program_id(0) == n - 1)(finish)

    in_specs = [pl.BlockSpec((tm, D), lambda i: (i, 0)), pl.BlockSpec((1, D), lambda i: (0, 0))]
    out_spec = pl.BlockSpec((tm, D), lambda i: (i, 0))
    out_shape = jax.ShapeDtypeStruct((T, D), BF16)
    if gather is None:
        return pl.pallas_call(body, name=name, grid=(n,), in_specs=in_specs, out_specs=out_spec, out_shape=out_shape,
                              compiler_params=_cp(("parallel",)))(x, w)
    return pl.pallas_call(
        body, name=name, grid=(n,), in_specs=in_specs + [ANY], out_specs=[out_spec, ANY],
        out_shape=[out_shape, jax.ShapeDtypeStruct((N_DEV,) + gather.shape, gather.dtype)],
        scratch_shapes=list(GATHER_SCRATCH), compiler_params=_cp(("arbitrary",)))(x, w, gather)


def _mm_norm_bwd(a, b, x, w, dres, *, name, exchange=None, left=None):
    T, K = a.shape
    D = b.shape[1]
    tm = min(T, 256 if K > 4096 else 512)
    n = T // tm
    n_left = 0 if left is None else len(left)

    def body(a_ref, b_ref, x_ref, w_ref, dr_ref, *rest):
        if exchange is not None:
            p_ref, dx_ref, dw_ref, got_ref, *sems = rest
            start, finish = _exchange_phases(p_ref, got_ref, *sems)
            pl.when(pl.program_id(0) == 0)(start)
        elif left is not None:
            left_refs, (dx_ref, dw_ref, gw_ref) = rest[:n_left], rest[n_left:]
        else:
            dx_ref, dw_ref = rest

        @pl.when(pl.program_id(0) == 0)
        def _():
            dw_ref[...] = jnp.zeros_like(dw_ref)
            if left is not None:
                gw_ref[...] = jnp.zeros_like(gw_ref)
        dh = _dot(a_ref[...], b_ref[...])
        xv = x_ref[...]
        r = lax.rsqrt(jnp.mean(xv * xv, axis=-1, keepdims=True) + EPS)
        xh = xv * r
        g = dh * w_ref[...]
        dx = dr_ref[...] + r * (g - xh * jnp.mean(g * xh, axis=-1, keepdims=True))
        dx_ref[...] = dx
        dw_ref[...] += jnp.sum(dh * xh, axis=0, keepdims=True)
        if left is not None:
            gw_ref[...] += _dot_tn(jnp.concatenate([r_[...] for r_ in left_refs], axis=1), dx.astype(BF16))
        if exchange is not None:
            pl.when(pl.program_id(0) == n - 1)(finish)

    row = pl.BlockSpec((tm, D), lambda i: (i, 0))
    vec = pl.BlockSpec((1, D), lambda i: (0, 0))
    in_specs = [pl.BlockSpec((tm, K), lambda i: (i, 0)), pl.BlockSpec((K, D), lambda i: (0, 0)), row, vec, row]
    out_shape = [jax.ShapeDtypeStruct((T, D), F32), jax.ShapeDtypeStruct((1, D), F32)]
    if left is not None:
        m = sum(p.shape[1] for p in left)
        return pl.pallas_call(
            body, name=name, grid=(n,),
            in_specs=in_specs + [pl.BlockSpec((tm, p.shape[1]), lambda i: (i, 0)) for p in left],
            out_specs=[row, vec, pl.BlockSpec((m, D), lambda i: (0, 0))],
            out_shape=out_shape + [jax.ShapeDtypeStruct((m, D), F32)],
            compiler_params=_cp(("arbitrary",)))(a, b, x, w, dres, *left)
    if exchange is None:
        return pl.pallas_call(body, name=name, grid=(n,), in_specs=in_specs, out_specs=[row, vec], out_shape=out_shape,
                              compiler_params=_cp(("arbitrary",)))(a, b, x, w, dres)
    return pl.pallas_call(
        body, name=name, grid=(n,), in_specs=in_specs + [ANY], out_specs=[row, vec, ANY],
        out_shape=out_shape + [jax.ShapeDtypeStruct(exchange.shape, exchange.dtype)],
        scratch_shapes=list(EXCHANGE_SCRATCH), compiler_params=_cp(("arbitrary",)))(a, b, x, w, dres, exchange)


def _down_proj_loss(a, w_down, x1, tgt, w, *, name):
    T, D = x1.shape
    K = a.shape[1]
    tm = min(T, 512)

    def body(a_ref, b_ref, x_ref, t_ref, w_ref, loss_ref, dx_ref, dxb_ref, dw_ref):
        @pl.when(pl.program_id(0) == 0)
        def _():
            dw_ref[...] = jnp.zeros_like(dw_ref)
            loss_ref[...] = jnp.zeros_like(loss_ref)
        xv = x_ref[...] + _dot(a_ref[...], b_ref[...])
        wv = w_ref[...]
        r = lax.rsqrt(jnp.mean(xv * xv, axis=-1, keepdims=True) + EPS)
        xh = xv * r
        e = xh * wv - t_ref[...]
        part = 0.5 * jnp.sum(jnp.mean(e * e, axis=-1, keepdims=True), axis=0, keepdims=True)
        loss_ref[...] += jnp.broadcast_to(part, loss_ref.shape)
        dy = e * (1.0 / D)
        g = dy * wv
        dx = r * (g - xh * jnp.mean(g * xh, axis=-1, keepdims=True))
        dx_ref[...] = dx
        dxb_ref[...] = dx.astype(BF16)
        dw_ref[...] += jnp.sum(dy * xh, axis=0, keepdims=True)

    row = pl.BlockSpec((tm, D), lambda i: (i, 0))
    vec = pl.BlockSpec((1, D), lambda i: (0, 0))
    return pl.pallas_call(
        body, name=name, grid=(T // tm,),
        in_specs=[pl.BlockSpec((tm, K), lambda i: (i, 0)), pl.BlockSpec((K, D), lambda i: (0, 0)), row, row, vec],
        out_specs=[pl.BlockSpec((1, 128), lambda i: (0, 0)), row, row, vec],
        out_shape=[jax.ShapeDtypeStruct((1, 128), F32), jax.ShapeDtypeStruct((T, D), F32),
                   jax.ShapeDtypeStruct((T, D), BF16), jax.ShapeDtypeStruct((1, D), F32)],
        compiler_params=_cp(("arbitrary",)))(a, w_down, x1, tgt, w)


def _ret_tables():
    C = RET_CHUNK
    h = jnp.arange(RET_HEADS, dtype=F32)
    log_gamma = jnp.log1p(-jnp.power(2.0, -5.0 - h))
    idx = jnp.arange(C, dtype=F32)
    diff = idx[:, None] - idx[None, :]
    dm = jnp.where(diff >= 0, jnp.exp(log_gamma[:, None, None] * jnp.maximum(diff, 0.0)), 0.0)
    dm = dm.reshape(4, 2 * C, C)
    lane_head = jnp.repeat(jnp.arange(RET_HEADS).reshape(4, 2), 64, axis=1)
    lg = log_gamma[lane_head]
    xi = jnp.exp(lg[:, None, :] * (idx[None, :, None] + 1.0))
    zeta = jnp.exp(lg[:, None, :] * (C - 1.0 - idx[None, :, None]))
    blk = (jnp.arange(128)[:, None] // 64) == (jnp.arange(128)[None, :] // 64)
    cd = jnp.where(blk[None], jnp.exp(lg * C)[:, :, None], 0.0)
    return dm.astype(F32), xi.astype(F32), zeta.astype(F32), cd.astype(F32)


def _ret_specs(tb, rev, nt, roped=False):
    def tmap(t):
        return (nt - 1 - t) if rev else t
    offsets = (0, 0, 8) if roped else (0, 4, 8)
    qkv = [pl.BlockSpec((tb, 128), lambda p, t, o=o: (tmap(t), o + p)) for o in offsets]
    rope = [pl.BlockSpec((tb, 128), lambda p, t: (tmap(t), 0))] * 2
    tabs = [pl.BlockSpec((None, 256, 128), lambda p, t: (p, 0, 0))] + \
           [pl.BlockSpec((None, 128, 128), lambda p, t: (p, 0, 0))] * 3
    return qkv, rope, tabs


def _ret_fwd(proj, cos, ss, tabs, gnw, *, name):
    T = proj.shape[0]
    tb = min(T, RET_TILE)
    nt = T // tb
    nchunk = tb // RET_CHUNK

    def body(q_ref, k_ref, v_ref, g_ref, cos_ref, ss_ref, dm_ref, xi_ref, zt_ref, cd_ref, gnw_ref,
             y_ref, o_ref, qr_ref, kr_ref, r_sc):
        @pl.when(pl.program_id(1) == 0)
        def _():
            r_sc[...] = jnp.zeros_like(r_sc)
        m0, m1 = _head_masks((128, 128))
        dm, xi, zt, cd = dm_ref[...], xi_ref[...], zt_ref[...], cd_ref[...]
        bm = (cd > 0).astype(F32)
        gnw = gnw_ref[...]
        for c in range(nchunk):
            rs = pl.ds(c * RET_CHUNK, RET_CHUNK)
            cs, sn = cos_ref[rs, :], ss_ref[rs, :]
            q = _rope(q_ref[rs, :], cs, sn, 32, 64)
            k = _rope(k_ref[rs, :], cs, sn, 32, 64) * K_SCALE
            v = v_ref[rs, :]
            kb, vb = k.astype(BF16), v.astype(BF16)
            qr_ref[rs, :] = q.astype(BF16)
            kr_ref[rs, :] = kb
            qs = jnp.concatenate([q * m0, q * m1], axis=0).astype(BF16)
            s = (_dot_nt(qs, kb) * dm).astype(BF16)
            vs = jnp.concatenate([v * m0, v * m1], axis=0).astype(BF16)
            o = _dot(jnp.concatenate([s[:128], s[128:]], axis=1), vs)
            r = r_sc[...]
            o = o + _dot(q.astype(BF16), r.astype(BF16)) * xi
            r_sc[...] = cd * r + bm * _dot_tn((k * zt).astype(BF16), vb)
            mu = (jnp.sum(o * m0, axis=1, keepdims=True) * m0 + jnp.sum(o * m1, axis=1, keepdims=True) * m1) * (1.0 / 64)
            d = o - mu
            dd = d * d
            var = (jnp.sum(dd * m0, axis=1, keepdims=True) * m0 + jnp.sum(dd * m1, axis=1, keepdims=True) * m1) * (1.0 / 64)
            oh = d * lax.rsqrt(var + EPS)
            g = g_ref[rs, :]
            y_ref[rs, :] = (g * _sigmoid(g) * (oh * gnw)).astype(BF16)
            o_ref[rs, :] = o

    qkv, rope, tspec = _ret_specs(tb, False, nt)
    gspec = pl.BlockSpec((tb, 128), lambda p, t: (t, 12 + p))
    out = pl.BlockSpec((tb, 128), lambda p, t: (t, p))
    return pl.pallas_call(
        body, name=name, grid=(4, nt),
        in_specs=qkv + [gspec] + rope + tspec + [pl.BlockSpec((1, 128), lambda p, t: (0, p))],
        out_specs=[out, out, out, out],
        out_shape=[jax.ShapeDtypeStruct((T, RET_WIDTH), BF16), jax.ShapeDtypeStruct((T, RET_WIDTH), F32),
                   jax.ShapeDtypeStruct((T, RET_WIDTH), BF16), jax.ShapeDtypeStruct((T, RET_WIDTH), BF16)],
        scratch_shapes=[pltpu.VMEM((128, 128), F32)],
        compiler_params=_cp(("parallel", "arbitrary")))(proj, proj, proj, proj, cos, ss, *tabs, gnw)


def _ret_bwd_dq(kr, proj, do, cos, ss, tabs, *, name):
    T = proj.shape[0]
    tb = min(T, RET_TILE)
    nt = T // tb
    nchunk = tb // RET_CHUNK

    def body(k_ref, v_ref, do_ref, cos_ref, ss_ref, dm_ref, xi_ref, zt_ref, cd_ref, dq_ref, r_sc):
        @pl.when(pl.program_id(1) == 0)
        def _():
            r_sc[...] = jnp.zeros_like(r_sc)
        m0, m1 = _head_masks((128, 128))
        dm, xi, zt, cd = dm_ref[...], xi_ref[...], zt_ref[...], cd_ref[...]
        bm = (cd > 0).astype(F32)
        for c in range(nchunk):
            rs = pl.ds(c * RET_CHUNK, RET_CHUNK)
            cs, sn = cos_ref[rs, :], ss_ref[rs, :]
            k = k_ref[rs, :].astype(F32)
            vb = v_ref[rs, :].astype(BF16)
            dob = do_ref[rs, :]
            dof = dob.astype(F32)
            dos = jnp.concatenate([dof * m0, dof * m1], axis=0).astype(BF16)
            a = (_dot_nt(dos, vb) * dm).astype(BF16)
            ks = jnp.concatenate([k * m0, k * m1], axis=0).astype(BF16)
            r = r_sc[...]
            dq = _dot(jnp.concatenate([a[:128], a[128:]], axis=1), ks) + _dot_nt(dob, r.astype(BF16)) * xi
            r_sc[...] = cd * r + bm * _dot_tn((k * zt).astype(BF16), vb)
            dq_ref[rs, :] = _rope_t(dq, cs, sn, 32, 64).astype(BF16)

    qkv, rope, tspec = _ret_specs(tb, False, nt, roped=True)
    blk = pl.BlockSpec((tb, 128), lambda p, t: (t, p))
    return pl.pallas_call(
        body, name=name, grid=(4, nt), in_specs=qkv[1:] + [blk] + rope + tspec, out_specs=blk,
        out_shape=jax.ShapeDtypeStruct((T, RET_WIDTH), BF16),
        scratch_shapes=[pltpu.VMEM((128, 128), F32)],
        compiler_params=_cp(("parallel", "arbitrary")))(kr, proj, do, cos, ss, *tabs)


def _ret_bwd_dkv(qr, kr, proj, do, cos, ss, tabs, *, name, swap=None):
    T = proj.shape[0]
    tb = min(T, RET_TILE)
    nt = T // tb
    nchunk = tb // RET_CHUNK

    def body(q_ref, k_ref, v_ref, do_ref, cos_ref, ss_ref, dm_ref, xi_ref, zt_ref, cd_ref, *rest):
        if swap is None:
            backward(q_ref, k_ref, v_ref, do_ref, cos_ref, ss_ref, dm_ref, xi_ref, zt_ref, cd_ref, *rest)
        else:
            g_ref, dk_ref, dv_ref, got_ref, u_sc, *sems = rest
            start, finish = _swap_phases(g_ref, got_ref, *sems)
            pl.when((pl.program_id(0) == 0) & (pl.program_id(1) == 0))(start)
            backward(q_ref, k_ref, v_ref, do_ref, cos_ref, ss_ref, dm_ref, xi_ref, zt_ref, cd_ref, dk_ref, dv_ref, u_sc)
            pl.when((pl.program_id(0) == 3) & (pl.program_id(1) == nt - 1))(finish)

    def backward(q_ref, k_ref, v_ref, do_ref, cos_ref, ss_ref, dm_ref, xi_ref, zt_ref, cd_ref, dk_ref, dv_ref, u_sc):
        @pl.when(pl.program_id(1) == 0)
        def _():
            u_sc[...] = jnp.zeros_like(u_sc)
        m0, m1 = _head_masks((128, 128))
        dm, xi, zt, cd = dm_ref[...], xi_ref[...], zt_ref[...], cd_ref[...]
        bm = (cd > 0).astype(F32)
        for c in reversed(range(nchunk)):
            rs = pl.ds(c * RET_CHUNK, RET_CHUNK)
            cs, sn = cos_ref[rs, :], ss_ref[rs, :]
            kb = k_ref[rs, :]
            q = q_ref[rs, :].astype(F32)
            vb = v_ref[rs, :].astype(BF16)
            dob = do_ref[rs, :]
            dof = dob.astype(F32)
            qs = jnp.concatenate([q * m0, q * m1], axis=0).astype(BF16)
            dos = jnp.concatenate([dof * m0, dof * m1], axis=0).astype(BF16)
            s = (_dot_nt(qs, kb) * dm).astype(BF16)
            a = (_dot_nt(dos, vb) * dm).astype(BF16)
            ub = u_sc[...].astype(BF16)
            dk = _dot_tn(a, qs) + _dot_nt(vb, ub) * zt
            dv = _dot_tn(s, dos) + _dot(kb, ub) * zt
            u_sc[...] = cd * u_sc[...] + bm * _dot_tn((q * xi).astype(BF16), dob)
            dk_ref[rs, :] = (_rope_t(dk, cs, sn, 32, 64) * K_SCALE).astype(BF16)
            dv_ref[rs, :] = dv.astype(BF16)

    qkv, rope, tspec = _ret_specs(tb, True, nt, roped=True)
    blk = pl.BlockSpec((tb, 128), lambda p, t: (nt - 1 - t, p))
    out_shape = [jax.ShapeDtypeStruct((T, RET_WIDTH), BF16)] * 2
    if swap is None:
        return pl.pallas_call(
            body, name=name, grid=(4, nt), in_specs=qkv + [blk] + rope + tspec, out_specs=[blk, blk],
            out_shape=out_shape, scratch_shapes=[pltpu.VMEM((128, 128), F32)],
            compiler_params=_cp(("parallel", "arbitrary")))(qr, kr, proj, do, cos, ss, *tabs)
    return pl.pallas_call(
        body, name=name, grid=(4, nt), in_specs=qkv + [blk] + rope + tspec + [ANY], out_specs=[blk, blk, ANY],
        out_shape=out_shape + [jax.ShapeDtypeStruct((4,) + swap.shape[2:], swap.dtype)],
        scratch_shapes=[pltpu.VMEM((128, 128), F32)] + list(SWAP_SCRATCH),
        compiler_params=_cp(("arbitrary", "arbitrary")))(qr, kr, proj, do, cos, ss, *tabs, swap)


def _mix_bwd(dx1, w_out, o_ret, proj, y_mla, gnw, *, name):
    T = dx1.shape[0]
    tm = min(T, 512)

    def body(dx_ref, wo_ref, o_ref, g_ref, ym_ref, gnw_ref, do_ref, dg_ref, dom_ref, dl_ref, dw_ref, dm_ref):
        @pl.when(pl.program_id(0) == 0)
        def _():
            dw_ref[...] = jnp.zeros_like(dw_ref)
        dm_ref[...] = _dot_nt(dx_ref[...].astype(BF16), wo_ref[...])
        m0, m1 = _head_masks((tm, 128))
        lane = lax.broadcasted_iota(jnp.int32, (tm, 128), 1)
        delta = jnp.zeros((tm, 128), F32)

        def gsum(z):
            return jnp.sum(z * m0, axis=1, keepdims=True) * m0 + jnp.sum(z * m1, axis=1, keepdims=True) * m1

        for p in range(4):
            cs = slice(128 * p, 128 * p + 128)
            dy = dm_ref[:, cs]
            o = o_ref[:, cs]
            g = g_ref[:, cs]
            w = gnw_ref[:, cs]
            d = o - gsum(o) * (1.0 / 64)
            rstd = lax.rsqrt(gsum(d * d) * (1.0 / 64) + EPS)
            oh = d * rstd
            sg = _sigmoid(g)
            dn = dy * (g * sg)
            dg_ref[:, cs] = (dy * (oh * w) * (sg * (1.0 + g * (1.0 - sg)))).astype(BF16)
            dw_ref[:, cs] += jnp.sum(dn * oh, axis=0, keepdims=True)
            doh = dn * w
            do = rstd * (doh - gsum(doh) * (1.0 / 64) - oh * (gsum(doh * oh) * (1.0 / 64)))
            do_ref[:, cs] = do.astype(BF16)
            dom = dm_ref[:, 512 + 128 * p:512 + 128 * p + 128]
            dom_ref[:, cs] = dom.astype(BF16)
            pr = dom * ym_ref[:, cs].astype(F32)
            delta = jnp.where(lane == 2 * p, jnp.sum(pr * m0, axis=1, keepdims=True), delta)
            delta = jnp.where(lane == 2 * p + 1, jnp.sum(pr * m1, axis=1, keepdims=True), delta)
        dl_ref[...] = delta.T[0:MLA_HEADS]

    half = pl.BlockSpec((tm, 512), lambda i: (i, 0))
    return pl.pallas_call(
        body, name=name, grid=(T // tm,),
        in_specs=[pl.BlockSpec((tm, D_MODEL), lambda i: (i, 0)), pl.BlockSpec(w_out.shape, lambda i: (0, 0)), half,
                  pl.BlockSpec((tm, 512), lambda i: (i, 3)), half, pl.BlockSpec((1, 512), lambda i: (0, 0))],
        out_specs=[half, half, half, pl.BlockSpec((MLA_HEADS, tm), lambda i: (0, i)),
                   pl.BlockSpec((1, 512), lambda i: (0, 0))],
        out_shape=[jax.ShapeDtypeStruct((T, 512), BF16)] * 3 + [jax.ShapeDtypeStruct((MLA_HEADS, T), F32),
                                                                jax.ShapeDtypeStruct((1, 512), F32)],
        scratch_shapes=[pltpu.VMEM((tm, 1024), F32)],
        compiler_params=_cp(("arbitrary",)))(dx1, w_out, o_ret, proj, y_mla, gnw)


def _mla_prep_fwd(proj, qnw, kvnw, wuq, wk, wv, cos, ss, *, name):
    T = proj.shape[0]
    tm = min(T, 512)

    def body(lat_ref, qnw_ref, kvnw_ref, wuq_ref, wk_ref, wv_ref, cos_ref, ss_ref,
             q_ref, k_ref, v_ref, cqn_ref, ckvn_ref):
        cq = lat_ref[:, 0:256]
        ckv = lat_ref[:, 256:384]
        g3 = lat_ref[:, 384:512]
        cqn = (cq * lax.rsqrt(jnp.mean(cq * cq, axis=-1, keepdims=True) + EPS) * qnw_ref[...]).astype(BF16)
        ckvn = (ckv * lax.rsqrt(jnp.mean(ckv * ckv, axis=-1, keepdims=True) + EPS) * kvnw_ref[...]).astype(BF16)
        cqn_ref[...] = cqn
        ckvn_ref[...] = ckvn
        cs, sn = cos_ref[...], ss_ref[...]
        q = _dot_nt(cqn, wuq_ref[...])
        k = _dot_nt(ckvn, wk_ref[...])
        kpe = _rope(g3, cs, sn, 16, 32)
        for h in range(MLA_HEADS):
            hs = slice(128 * h, 128 * h + 128)
            q_ref[:, hs] = (_rope(q[:, hs], cs, sn, 16, 32) * SCALE).astype(BF16)
            k_ref[:, hs] = (k[:, hs] + kpe).astype(BF16)
        v = _dot_nt(ckvn, wv_ref[...])
        lane = lax.broadcasted_iota(jnp.int32, (tm, 128), 1)
        for p in range(4):
            vp = v[:, 128 * p:128 * p + 128]
            v_ref[:, 256 * p:256 * p + 128] = jnp.where(lane < 64, vp, 1.0).astype(BF16)
            v_ref[:, 256 * p + 128:256 * p + 256] = jnp.where(lane < 64, 1.0, vp).astype(BF16)

    def full(shape):
        return pl.BlockSpec(shape, lambda i: (0, 0))

    def row(w):
        return pl.BlockSpec((tm, w), lambda i: (i, 0))

    return pl.pallas_call(
        body, name=name, grid=(T // tm,),
        in_specs=[pl.BlockSpec((tm, 512), lambda i: (i, 4)), full((1, 256)), full((1, 128)), full((1024, 256)),
                  full((1024, 128)), full((512, 128)), row(128), row(128)],
        out_specs=[row(1024), row(1024), row(1024), row(256), row(128)],
        out_shape=[jax.ShapeDtypeStruct((T, 1024), BF16), jax.ShapeDtypeStruct((T, 1024), BF16),
                   jax.ShapeDtypeStruct((T, 1024), BF16), jax.ShapeDtypeStruct((T, 256), BF16),
                   jax.ShapeDtypeStruct((T, 128), BF16)],
        compiler_params=_cp(("parallel",)))(proj, qnw, kvnw, wuq, wk, wv, cos, ss)


def _mla_prep_bwd(dq, dk, dv, proj, qnw, kvnw, wuq_t, wk_t, wv_t, cos, ss, ret_grads, cqn, ckvn, h, *, name):
    T = proj.shape[0]
    tm = min(T, 256)

    def body(dq_ref, dk_ref, dv_ref, lat_ref, qnw_ref, kvnw_ref, wuq_ref, wk_ref, wv_ref, cos_ref, ss_ref,
             rq_ref, rk_ref, rv_ref, rg_ref, cqn_ref, ckvn_ref, h_ref,
             dproj_ref, gwin_ref, gwuq_ref, gwk_ref, gwv_ref, dqnw_ref, dkvnw_ref, dqp_ref):
        for j, r in enumerate((rq_ref, rk_ref, rv_ref, rg_ref)):
            dproj_ref[:, 512 * j:512 * j + 512] = r[...]
        dlat_ref = dproj_ref.at[:, 2048:2560]

        @pl.when(pl.program_id(0) == 0)
        def _():
            for r in (gwin_ref, gwuq_ref, gwk_ref, gwv_ref, dqnw_ref, dkvnw_ref):
                r[...] = jnp.zeros_like(r)
        cs, sn = cos_ref[...], ss_ref[...]
        dkpe = jnp.zeros((tm, 128), F32)
        for h in range(MLA_HEADS):
            hs = slice(128 * h, 128 * h + 128)
            dqp_ref[:, hs] = _rope_t(dq_ref[:, hs] * SCALE, cs, sn, 16, 32).astype(BF16)
            dkpe = dkpe + dk_ref[:, hs]
        lane = lax.broadcasted_iota(jnp.int32, (tm, 128), 1)
        rope_lane = (lane >= MLA_NOPE) & (lane < MLA_NOPE + MLA_ROPE)
        dg3 = jnp.where(rope_lane, _rope_t(jnp.where(rope_lane, dkpe, 0.0), cs, sn, 16, 32), 0.0)

        def norm_bwd(x, w, dn):
            r = lax.rsqrt(jnp.mean(x * x, axis=-1, keepdims=True) + EPS)
            xh = x * r
            g = dn * w
            return r * (g - xh * jnp.mean(g * xh, axis=-1, keepdims=True)), jnp.sum(dn * xh, axis=0, keepdims=True)

        dqp = dqp_ref[...]
        dkb = dk_ref[...].astype(BF16)
        dvb = dv_ref[...]
        dcqn = _dot(dqp, wuq_ref[...])
        dcq, dqnw = norm_bwd(lat_ref[:, 0:256], qnw_ref[...], dcqn)
        dckvn = _dot(dkb, wk_ref[...]) + _dot(dvb, wv_ref[...])
        dckv, dkvnw = norm_bwd(lat_ref[:, 256:384], kvnw_ref[...], dckvn)
        gwuq_ref[...] += _dot_tn(dqp, cqn_ref[...])
        gwk_ref[...] += _dot_tn(dkb, ckvn_ref[...])
        gwv_ref[...] += _dot_tn(dvb, ckvn_ref[...])
        dqnw_ref[...] += dqnw
        dkvnw_ref[...] += dkvnw
        dlat_ref[:, 0:256] = dcq.astype(BF16)
        dlat_ref[:, 256:384] = dckv.astype(BF16)
        dlat_ref[:, 384:512] = dg3.astype(BF16)
        gwin_ref[...] += _dot_tn(dproj_ref[...], h_ref[...])

    def full(shape):
        return pl.BlockSpec(shape, lambda i: (0, 0))

    def row(w):
        return pl.BlockSpec((tm, w), lambda i: (i, 0))

    return pl.pallas_call(
        body, name=name, grid=(T // tm,),
        in_specs=[row(1024), row(1024), row(512), pl.BlockSpec((tm, 512), lambda i: (i, 4)), full((1, 256)),
                  full((1, 128)), full((1024, 256)), full((1024, 128)), full((512, 128)), row(128), row(128)]
                 + [row(512)] * 4 + [row(256), row(128), row(D_MODEL)],
        out_specs=[row(IN_PAD), full((IN_PAD, D_MODEL)), full((1024, 256)), full((1024, 128)), full((512, 128)),
                   full((1, 256)), full((1, 128))],
        out_shape=[jax.ShapeDtypeStruct((T, IN_PAD), BF16), jax.ShapeDtypeStruct((IN_PAD, D_MODEL), F32),
                   jax.ShapeDtypeStruct((1024, 256), F32), jax.ShapeDtypeStruct((1024, 128), F32),
                   jax.ShapeDtypeStruct((512, 128), F32), jax.ShapeDtypeStruct((1, 256), F32),
                   jax.ShapeDtypeStruct((1, 128), F32)],
        scratch_shapes=[pltpu.VMEM((tm, 1024), BF16)],
        compiler_params=_cp(("arbitrary",)))(dq, dk, dv, proj, qnw, kvnw, wuq_t, wk_t, wv_t, cos, ss, *ret_grads,
                                             cqn, ckvn, h)


def _flash_fwd(q, k, v1, *, name, gather=None):
    T = q.shape[0]
    tq = min(T, 512)
    tk = tq
    nq = T // tq

    def body(q_ref, k_ref, v_ref, *rest):
        if gather is None:
            y_ref, lse_ref = rest
        else:
            x_ref, y_ref, lse_ref, g_ref, *sems = rest
            start, forward, finish = _gather_phases(x_ref, g_ref, *sems)
            pl.when((pl.program_id(0) == 0) & (pl.program_id(1) == 0))(start)
            pl.when((pl.program_id(0) == 1) & (pl.program_id(1) == 0))(forward)
        attend(q_ref, k_ref, v_ref, y_ref, lse_ref)
        if gather is not None:
            pl.when((pl.program_id(0) == 3) & (pl.program_id(1) == nq - 1))(finish)

    def attend(q_ref, k_ref, v_ref, y_ref, lse_ref):
        qi = pl.program_id(1)
        row = lax.broadcasted_iota(jnp.int32, (tq, tk), 0)
        col = lax.broadcasted_iota(jnp.int32, (tq, tk), 1)

        def step(kb, carry, masked):
            ks = pl.ds(pl.multiple_of(kb * tk, tk), tk)
            new = []
            for h in range(2):
                hs = slice(128 * h, 128 * h + 128)
                m, acc = carry[h]
                s = _dot_nt(q_ref[:, hs], k_ref[ks, hs])
                if masked:
                    s = jnp.where(col <= row, s, NEG)
                mn = jnp.maximum(m, jnp.max(s, axis=1, keepdims=True))
                p = jnp.exp((s - mn).astype(BF16))
                acc = jnp.exp(m - mn) * acc + _dot(p, v_ref[ks, hs])
                new.append((mn, acc))
            return tuple(new)

        def unrolled(j, c):
            for u in range(FLASH_UNROLL):
                c = step(FLASH_UNROLL * j + u, c, False)
            return c

        init = (jnp.full((tq, 1), NEG, F32), jnp.zeros((tq, 128), F32))
        carry = lax.fori_loop(0, qi // FLASH_UNROLL, unrolled, (init, init))
        carry = lax.fori_loop(FLASH_UNROLL * (qi // FLASH_UNROLL), qi, lambda kb, c: step(kb, c, False), carry)
        (ma, acca), (mb, accb) = step(qi, carry, True)
        lane = lax.broadcasted_iota(jnp.int32, (tq, 128), 1)
        la, lb = pltpu.roll(acca, 64, 1), pltpu.roll(accb, 64, 1)
        y_ref[...] = jnp.where(lane < 64, acca / la, accb / lb).astype(BF16)
        lse_ref[0, 0] = jnp.broadcast_to(ma + jnp.log(acca[:, 64:65]), (tq, 128)).T[0:1]
        lse_ref[1, 0] = jnp.broadcast_to(mb + jnp.log(accb[:, 0:1]), (tq, 128)).T[0:1]

    in_specs = [pl.BlockSpec((tq, 256), lambda p, i: (i, p)), pl.BlockSpec((T, 256), lambda p, i: (0, p)),
                pl.BlockSpec((T, 256), lambda p, i: (0, p))]
    out_specs = [pl.BlockSpec((tq, 128), lambda p, i: (i, p)), pl.BlockSpec((2, 1, 1, tq), lambda p, i: (p, i, 0, 0))]
    out_shape = [jax.ShapeDtypeStruct((T, MLA_WIDTH), BF16), jax.ShapeDtypeStruct((MLA_HEADS, nq, 1, tq), F32)]
    if gather is None:
        return pl.pallas_call(body, name=name, grid=(4, nq), in_specs=in_specs, out_specs=out_specs,
                              out_shape=out_shape, compiler_params=_cp(("parallel", "arbitrary")))(q, k, v1)
    return pl.pallas_call(
        body, name=name, grid=(4, nq), in_specs=in_specs + [ANY], out_specs=out_specs + [ANY],
        out_shape=out_shape + [jax.ShapeDtypeStruct((N_DEV,) + gather.shape, gather.dtype)],
        scratch_shapes=list(GATHER_SCRATCH),
        compiler_params=_cp(("arbitrary", "arbitrary")))(q, k, v1, gather)


def _flash_bwd(q, k, v, do, lse, delta, *, name, exchange=None):
    T = q.shape[0]
    tq = min(T, 512)
    tk = tq
    nq = T // tq

    def body(q_ref, k_ref, v_ref, do_ref, lse_ref, dl_ref, *rest):
        if exchange is None:
            backward(q_ref, k_ref, v_ref, do_ref, lse_ref, dl_ref, *rest)
        else:
            p_ref, dqt_ref, dk_ref, dv_ref, got_ref, *sems = rest
            start, finish = _exchange_phases(p_ref, got_ref, *sems)
            pl.when((pl.program_id(0) == 0) & (pl.program_id(1) == 0))(start)
            backward(q_ref, k_ref, v_ref, do_ref, lse_ref, dl_ref, dqt_ref, dk_ref, dv_ref)
            pl.when((pl.program_id(0) == 3) & (pl.program_id(1) == nq - 1))(finish)

    def backward(q_ref, k_ref, v_ref, do_ref, lse_ref, dl_ref, dqt_ref, dk_ref, dv_ref):
        kb = pl.program_id(1)

        @pl.when(kb == 0)
        def _():
            dqt_ref[...] = jnp.zeros_like(dqt_ref)
        krow = lax.broadcasted_iota(jnp.int32, (tk, tq), 0)
        qcol = lax.broadcasted_iota(jnp.int32, (tk, tq), 1)
        masks = _head_masks((tk, 128))
        vms = [(v_ref[:, 128 * h:128 * h + 128].astype(F32) * masks[h]).astype(BF16) for h in range(2)]

        def step(qi, carry, masked):
            qs = pl.ds(pl.multiple_of(qi * tq, tq), tq)
            dob = do_ref[qs, :]
            dof = dob.astype(F32)
            dks, dv_acc = list(carry[:2]), carry[2]
            for h in range(2):
                hs = slice(128 * h, 128 * h + 128)
                kh = k_ref[:, hs]
                qh = q_ref[qs, hs]
                st = _dot_nt(kh, qh)
                pt = jnp.exp((st - lse_ref[h, qi]).astype(BF16))
                if masked:
                    pt = jnp.where(krow <= qcol, pt, jnp.zeros_like(pt))
                dv_acc = dv_acc + _dot(pt, (dof * masks[h]).astype(BF16))
                dpt = _dot_nt(vms[h], dob)
                dst = pt * (dpt - dl_ref[h, qi]).astype(BF16)
                dks[h] = dks[h] + _dot(dst, qh)
                dqt_ref[qi, hs, :] += _dot_tn(kh, dst)
            return dks[0], dks[1], dv_acc

        zero = jnp.zeros((tk, 128), F32)
        carry = step(kb, (zero, zero, zero), True)

        def unrolled(j, c):
            for u in range(FLASH_BWD_UNROLL):
                c = step(kb + 1 + FLASH_BWD_UNROLL * j + u, c, False)
            return c

        trips = (nq - 1 - kb) // FLASH_BWD_UNROLL
        carry = lax.fori_loop(0, trips, unrolled, carry)
        dk0, dk1, dv_acc = lax.fori_loop(kb + 1 + FLASH_BWD_UNROLL * trips, nq, lambda qi, c: step(qi, c, False), carry)
        dk_ref[:, 0:128] = dk0
        dk_ref[:, 128:256] = dk1
        dv_ref[...] = dv_acc.astype(BF16)

    stat = pl.BlockSpec((2, nq, 1, tq), lambda p, j: (p, 0, 0, 0))
    in_specs = [pl.BlockSpec((T, 256), lambda p, j: (0, p)), pl.BlockSpec((tk, 256), lambda p, j: (j, p)),
                pl.BlockSpec((tk, 256), lambda p, j: (j, p)), pl.BlockSpec((T, 128), lambda p, j: (0, p)), stat, stat]
    out_specs = [pl.BlockSpec((None, nq, 256, tq), lambda p, j: (p, 0, 0, 0)),
                 pl.BlockSpec((tk, 256), lambda p, j: (j, p)), pl.BlockSpec((tk, 128), lambda p, j: (j, p))]
    out_shape = [jax.ShapeDtypeStruct((4, nq, 256, tq), F32), jax.ShapeDtypeStruct((T, 1024), F32),
                 jax.ShapeDtypeStruct((T, MLA_WIDTH), BF16)]
    if exchange is None:
        return pl.pallas_call(body, name=name, grid=(4, nq), in_specs=in_specs, out_specs=out_specs,
                              out_shape=out_shape,
                              compiler_params=_cp(("parallel", "arbitrary")))(q, k, v, do, lse, delta)
    return pl.pallas_call(
        body, name=name, grid=(4, nq), in_specs=in_specs + [ANY], out_specs=out_specs + [ANY],
        out_shape=out_shape + [jax.ShapeDtypeStruct(exchange.shape, exchange.dtype)],
        scratch_shapes=list(EXCHANGE_SCRATCH),
        compiler_params=_cp(("arbitrary", "arbitrary")))(q, k, v, do, lse, delta, exchange)


def _shift_down(x, n, prev8):
    r = pltpu.roll(x, n, 0)
    row = lax.broadcasted_iota(jnp.int32, prev8.shape, 0)
    first = jnp.where(row < n, pltpu.roll(prev8, n, 0), r[:8])
    if x.shape[0] == 8:
        return first
    return jnp.concatenate([first, r[8:]], axis=0)


def _shift_up(x, n, next8):
    tm = x.shape[0]
    r = pltpu.roll(x, tm - n, 0)
    row = lax.broadcasted_iota(jnp.int32, next8.shape, 0)
    last = jnp.where(row >= 8 - n, pltpu.roll(next8, 8 - n, 0), r[tm - 8:])
    return jnp.concatenate([r[:tm - 8], last], axis=0)


def _conv_pre(u, prev8, cw_ref, cb_ref):
    p1 = _shift_down(u, 1, prev8)
    p2 = _shift_down(u, 2, prev8)
    up = cb_ref[...] + cw_ref[0:1, :] * p2 + cw_ref[1:2, :] * p1 + cw_ref[2:3, :] * u
    return up, p1, p2


def _up_proj_conv(x, y_ret, y_mla, w_out, nw, w_up_t, cw, cb, *, name):
    T, K = x.shape
    tm = min(T, 256)

    def body(x_ref, yr_ref, ym_ref, wo_ref, nw_ref, w_ref, cw_ref, cb_ref, x1_ref, h_ref, u_ref, a_ref, carry_sc):
        @pl.when(pl.program_id(0) == 0)
        def _():
            carry_sc[...] = jnp.zeros_like(carry_sc)
        xv = x_ref[...] + _dot(jnp.concatenate([yr_ref[...], ym_ref[...]], axis=1), wo_ref[...])
        x1_ref[...] = xv
        h = (xv * lax.rsqrt(jnp.mean(xv * xv, axis=-1, keepdims=True) + EPS) * nw_ref[...]).astype(BF16)
        h_ref[...] = h
        for blk in range(2):
            ups = []
            for half in range(2):
                cs = slice((2 * blk + half) * FF_HALF, (2 * blk + half + 1) * FF_HALF)
                u = _dot_nt(h, w_ref[cs, :])
                u_ref[:, cs] = u
                prev = carry_sc[:, cs]
                ups.append(cb_ref[:, cs] + cw_ref[0:1, cs] * _shift_down(u, 2, prev)
                           + cw_ref[1:2, cs] * _shift_down(u, 1, prev) + cw_ref[2:3, cs] * u)
                carry_sc[:, cs] = u[tm - 8:]
            gate, val = ups
            a_ref[:, blk * FF_HALF:(blk + 1) * FF_HALF] = (gate * _sigmoid(gate) * val).astype(BF16)

    def full(shape):
        return pl.BlockSpec(shape, lambda i: (0, 0))

    return pl.pallas_call(
        body, name=name, grid=(T // tm,),
        in_specs=[pl.BlockSpec((tm, K), lambda i: (i, 0)), pl.BlockSpec((tm, RET_WIDTH), lambda i: (i, 0)),
                  pl.BlockSpec((tm, MLA_WIDTH), lambda i: (i, 0)), full(w_out.shape), full(nw.shape),
                  full(w_up_t.shape), full(cw.shape), full(cb.shape)],
        out_specs=[pl.BlockSpec((tm, K), lambda i: (i, 0)), pl.BlockSpec((tm, K), lambda i: (i, 0)),
                   pl.BlockSpec((tm, 2 * D_FF), lambda i: (i, 0)), pl.BlockSpec((tm, D_FF), lambda i: (i, 0))],
        out_shape=[jax.ShapeDtypeStruct((T, K), F32), jax.ShapeDtypeStruct((T, K), BF16),
                   jax.ShapeDtypeStruct((T, 2 * D_FF), F32), jax.ShapeDtypeStruct((T, D_FF), BF16)],
        scratch_shapes=[pltpu.VMEM((8, 2 * D_FF), F32)],
        compiler_params=_cp(("arbitrary",)))(x, y_ret, y_mla, w_out, nw, w_up_t, cw, cb)


def _conv_bwd(u, da, cw, cb, *, name):
    T = u.shape[0]
    tm = min(T, 512)
    W = 2 * FF_HALF
    nt = T // tm

    def body(u_ref, prev_ref, next_ref, da_ref, dan_ref, cw_ref, cb_ref, du_ref, dw0_ref, dw1_ref, dw2_ref, db_ref):
        i = pl.program_id(1)

        @pl.when(i == 0)
        def _():
            for r in (dw0_ref, dw1_ref, dw2_ref, db_ref):
                r[...] = jnp.zeros_like(r)

        def dpre(u, prev8, da):
            up, p1, p2 = _conv_pre(u, prev8, cw_ref, cb_ref)
            gate, val = up[:, :FF_HALF], up[:, FF_HALF:]
            sg = _sigmoid(gate)
            dgate = da * val * (sg * (1.0 + gate * (1.0 - sg)))
            dval = da * (gate * sg)
            return jnp.concatenate([dgate, dval], axis=1), p1, p2

        u = u_ref[...]
        prev = jnp.where(i > 0, prev_ref[...], 0.0)
        dup, p1, p2 = dpre(u, prev, da_ref[...])
        dupn, _, _ = dpre(next_ref[...], u[tm - 8:], dan_ref[...])
        dupn = jnp.where(i < nt - 1, dupn, 0.0)
        du = cw_ref[2:3, :] * dup + cw_ref[1:2, :] * _shift_up(dup, 1, dupn) + cw_ref[0:1, :] * _shift_up(dup, 2, dupn)
        du_ref[...] = du.astype(BF16)
        dw0_ref[...] += jnp.sum(dup * p2, axis=0, keepdims=True)
        dw1_ref[...] += jnp.sum(dup * p1, axis=0, keepdims=True)
        dw2_ref[...] += jnp.sum(dup * u, axis=0, keepdims=True)
        db_ref[...] += jnp.sum(dup, axis=0, keepdims=True)

    nxt = lambda j, i: (jnp.minimum((i + 1) * (tm // 8), T // 8 - 1), j)
    vec = pl.BlockSpec((1, W), lambda j, i: (0, j))
    return pl.pallas_call(
        body, name=name, grid=(2, nt),
        in_specs=[pl.BlockSpec((tm, W), lambda j, i: (i, j)),
                  pl.BlockSpec((8, W), lambda j, i: (jnp.maximum(i * (tm // 8) - 1, 0), j)),
                  pl.BlockSpec((8, W), nxt),
                  pl.BlockSpec((tm, FF_HALF), lambda j, i: (i, j)), pl.BlockSpec((8, FF_HALF), nxt),
                  pl.BlockSpec((3, W), lambda j, i: (0, j)), vec],
        out_specs=[pl.BlockSpec((tm, W), lambda j, i: (i, j)), vec, vec, vec, vec],
        out_shape=[jax.ShapeDtypeStruct((T, 2 * D_FF), BF16)] + [jax.ShapeDtypeStruct((1, 2 * D_FF), F32)] * 4,
        compiler_params=_cp(("parallel", "arbitrary")))(u, u, u, da, da, cw, cb)


def _sum_chips(slots, *, name):
    ns, R, C = slots.shape
    tr = _row_tile(R)

    def body(g_ref, o_ref):
        g = g_ref[0].astype(F32)
        for s in range(1, ns):
            g = g + g_ref[s].astype(F32)
        o_ref[...] = g

    return pl.pallas_call(
        body, name=name, grid=(R // tr,), in_specs=[pl.BlockSpec((ns, tr, C), lambda i: (0, i, 0))],
        out_specs=pl.BlockSpec((tr, C), lambda i: (i, 0)), out_shape=jax.ShapeDtypeStruct((R, C), F32),
        compiler_params=_cp(("parallel",)))(slots)


def _place():
    return lax.axis_index("x"), lax.axis_index("y"), lax.axis_index("c")


GATHER_SCRATCH = (pltpu.SemaphoreType.DMA((7,)), pltpu.SemaphoreType.DMA((7,)), pltpu.SemaphoreType.DMA)
EXCHANGE_SCRATCH = (pltpu.SemaphoreType.DMA((3,)), pltpu.SemaphoreType.DMA((3,)), pltpu.SemaphoreType.DMA)


def _gather_phases(x_ref, out_ref, send_sems, recv_sems, local_sem):
    x_, y_, c_ = _place()
    me, sibling = (x_, y_, c_), (x_, y_, 1 - c_)
    chips = [(1 - x_, y_), (x_, 1 - y_), (1 - x_, 1 - y_)]

    def slot(px, py, pc):
        return out_ref.at[4 * px + 2 * py + pc]

    def copy(k, block, to, src=None):
        return pltpu.make_async_remote_copy(
            src_ref=slot(*block) if src is None else src, dst_ref=slot(*block),
            send_sem=send_sems.at[k], recv_sem=recv_sems.at[k], device_id=to, device_id_type=MESH)

    def mine():
        return pltpu.make_async_copy(x_ref, slot(*me), local_sem)

    def first():
        return [copy(0, me, sibling, src=x_ref)] + [copy(1 + j, me, (*chip, c_), src=x_ref)
                                                     for j, chip in enumerate(chips)]

    def passed():
        return [copy(4 + j, (*chip, c_), sibling) for j, chip in enumerate(chips)]

    def start():
        mine().start()
        for cp in first():
            cp.start()

    def forward():
        fwd = passed()
        for j, chip in enumerate(chips):
            copy(1 + j, (*chip, c_), me).wait_recv()
            fwd[j].start()

    def finish():
        copy(0, sibling, me).wait_recv()
        for j, chip in enumerate(chips):
            copy(4 + j, (*chip, 1 - c_), me).wait_recv()
        for cp in first() + passed():
            cp.wait_send()
        mine().wait()

    return start, forward, finish


def _exchange_phases(p_ref, out_ref, send_sems, recv_sems, local_sem):
    x_, y_, c_ = _place()
    me_k = 2 * x_ + y_
    chips = [(1 - x_, y_), (x_, 1 - y_), (1 - x_, 1 - y_)]

    def local():
        return pltpu.make_async_copy(p_ref.at[me_k], out_ref.at[me_k], local_sem)

    def copy(j, src_k, dst_k, chip):
        return pltpu.make_async_remote_copy(
            src_ref=p_ref.at[src_k], dst_ref=out_ref.at[dst_k], send_sem=send_sems.at[j],
            recv_sem=recv_sems.at[j], device_id=(*chip, c_), device_id_type=MESH)

    def sends():
        return [copy(j, 2 * px + py, me_k, (px, py)) for j, (px, py) in enumerate(chips)]

    def start():
        local().start()
        for cp in sends():
            cp.start()

    def finish():
        for j, (px, py) in enumerate(chips):
            copy(j, me_k, 2 * px + py, (px, py)).wait_recv()
        for cp in sends():
            cp.wait_send()
        local().wait()

    return start, finish


def _all_gather(x, *, name):
    def body(x_ref, out_ref, send_sems, recv_sems, local_sem):
        for phase in _gather_phases(x_ref, out_ref, send_sems, recv_sems, local_sem):
            phase()

    spec = pl.BlockSpec(memory_space=pltpu.VMEM)
    return pl.pallas_call(
        body, name=name, out_shape=jax.ShapeDtypeStruct((N_DEV,) + x.shape, x.dtype),
        in_specs=[spec], out_specs=spec, scratch_shapes=list(GATHER_SCRATCH),
        compiler_params=pltpu.CompilerParams(vmem_limit_bytes=VMEM_LIMIT))(x)


def _small_rows():
    table, row = [], 0
    for n, size in SMALL_VECTORS:
        table.append((n, size, row))
        row += -(-size // PACK_COLS)
    return table


def _ff_chunk_source(c):
    block, off = divmod(c * 128, FF_HALF)
    return (0, 2, 1, 3)[block] * FF_HALF + off


def _pack_small(parts, *, name):
    table = _small_rows()

    def body(*refs):
        out = refs[-1]
        out[...] = jnp.zeros_like(out)
        for ref, (n, size, row) in zip(refs, table):
            if size != 2 * D_FF:
                out[row:row + 1, 0:size] = ref[...]
                continue
            for c in range(size // 128):
                src = _ff_chunk_source(c)
                r, lane = divmod(c * 128, PACK_COLS)
                out[row + r:row + r + 1, lane:lane + 128] = ref[:, src:src + 128]

    return pl.pallas_call(body, name=name, out_shape=jax.ShapeDtypeStruct((SMALL_ROWS, PACK_COLS), F32))(
        *[parts[n] for n, _, _ in table])


def _sum_small(g, *, name):
    table = _small_rows()
    shapes = [(n, size) for n, size, _ in table if not n.startswith("conv_w")]
    shapes.insert(7, ("conv_w", 2 * D_FF))

    def body(g_ref, *outs):
        def total(row, width):
            acc = g_ref[0, row:row + 1, 0:width]
            for d in range(1, N_DEV):
                acc = acc + g_ref[d, row:row + 1, 0:width]
            return acc

        out_of = {n: o for (n, _), o in zip(shapes, outs)}
        for n, size, row in table:
            o, j = (out_of["conv_w"], int(n[-1])) if n.startswith("conv_w") else (out_of[n], 0)
            for i in range(-(-size // PACK_COLS)):
                width = min(PACK_COLS, size - PACK_COLS * i)
                o[j:j + 1, PACK_COLS * i:PACK_COLS * i + width] = total(row + i, width)

    out_shape = [jax.ShapeDtypeStruct((3 if n == "conv_w" else 1, size), F32) for n, size in shapes]
    res = pl.pallas_call(body, name=name, out_shape=out_shape)(g)
    return {n: r for (n, _), r in zip(shapes, res)}


def _adamw_multi(ws, ms, vs, gs, *, name):
    k = len(ws)

    def body(*refs):
        w_refs, m_refs, v_refs, g_refs = (refs[i * k:(i + 1) * k] for i in range(4))
        outs = refs[4 * k:]
        for i in range(k):
            g = g_refs[i][...]
            mn = ADAM_B1 * m_refs[i][...] + (1.0 - ADAM_B1) * g
            vn = ADAM_B2 * v_refs[i][...] + (1.0 - ADAM_B2) * (g * g)
            m_hat = mn / (1.0 - ADAM_B1 ** ADAM_STEP)
            v_hat = vn / (1.0 - ADAM_B2 ** ADAM_STEP)
            outs[i][...] = g
            outs[k + i][...] = -ADAM_LR * (m_hat / (jnp.sqrt(v_hat) + ADAM_EPS) + ADAM_WD * w_refs[i][...])
            outs[2 * k + i][...] = mn
            outs[3 * k + i][...] = vn

    out_shape = [jax.ShapeDtypeStruct(w.shape, F32) for _ in range(4) for w in ws]
    res = pl.pallas_call(body, name=name, out_shape=out_shape, compiler_params=_cp())(*ws, *ms, *vs, *gs)
    return [res[i * k:(i + 1) * k] for i in range(4)]


SWAP_SCRATCH = (pltpu.SemaphoreType.DMA((4,)), pltpu.SemaphoreType.DMA((4,)))


def _swap_phases(g_ref, out_ref, send_sems, recv_sems):
    x_, y_, c_ = _place()

    def copies():
        return [pltpu.make_async_remote_copy(src_ref=g_ref.at[k, 1 - c_], dst_ref=out_ref.at[k],
                                             send_sem=send_sems.at[k], recv_sem=recv_sems.at[k],
                                             device_id=(x_, y_, 1 - c_), device_id_type=MESH) for k in range(4)]

    def start():
        for cp in copies():
            cp.start()

    def finish():
        for cp in copies():
            cp.wait()

    return start, finish


def _swap_sibling(g, *, name):
    def body(g_ref, out_ref, send_sems, recv_sems):
        for phase in _swap_phases(g_ref, out_ref, send_sems, recv_sems):
            phase()

    return pl.pallas_call(
        body, name=name, out_shape=jax.ShapeDtypeStruct((4,) + g.shape[2:], g.dtype), in_specs=[ANY], out_specs=ANY,
        scratch_shapes=list(SWAP_SCRATCH))(g)


def _row_tile(R):
    for cand in (256, 400, 200):
        if R % cand == 0:
            return cand
    return R


def _add_own(g, b, *, name, out_dtype):
    n, _, R, C = g.shape
    tr = _row_tile(R)

    def body(c_ref, g_ref, b_ref, o_ref):
        del c_ref
        o_ref[...] = (g_ref[...] + b_ref[...]).astype(out_dtype)

    blk = pl.BlockSpec((None, tr, C), lambda s, i, c: (s, i, 0))
    grid_spec = pltpu.PrefetchScalarGridSpec(
        num_scalar_prefetch=1, grid=(n, R // tr),
        in_specs=[pl.BlockSpec((None, None, tr, C), lambda s, i, c: (s, c[0], i, 0)), blk], out_specs=blk)
    core = jnp.reshape(lax.axis_index("c"), (1,)).astype(jnp.int32)
    return pl.pallas_call(body, name=name, grid_spec=grid_spec, out_shape=jax.ShapeDtypeStruct(b.shape, out_dtype),
                          compiler_params=_cp(("parallel", "parallel")))(core, g, b)


def _pack_local(parts, group, tail=None):
    table, rows = group
    segs = []
    for n, r, rp, tr in table:
        w = parts[n].T if tr else parts[n]
        segs.append(jnp.pad(w.reshape(r, PACK_COLS), ((0, rp - r), (0, 0))))
    spare = rows - sum(rp for _, _, rp, _ in table)
    segs.append(jnp.zeros((spare, PACK_COLS), segs[0].dtype) if tail is None else tail)
    return jnp.concatenate(segs, axis=0)


CONV_W_BITS = 2 * 3 * 704
SPARE_EARLY = 16


def _conv_w_as_rows(conv_w_shard):
    bits = lax.bitcast_convert_type(conv_w_shard.reshape(-1), BF16).reshape(-1)
    return jnp.pad(bits, (0, SPARE_EARLY * PACK_COLS - CONV_W_BITS)).reshape(SPARE_EARLY, PACK_COLS)


def _conv_w_from_rows(gathered):
    bits = gathered[:, EARLY[1] - SPARE_EARLY:].reshape(N_DEV, -1)[:, :CONV_W_BITS].reshape(N_DEV, 3 * 704, 2)
    w = lax.bitcast_convert_type(bits, F32).reshape(N_DEV, 3, 704)
    return w.transpose(1, 0, 2).reshape(3, 2 * D_FF)


def _unpack_local(packed, like, group):
    out, off = {}, 0
    for n, r, rp, tr in group[0]:
        rows, cols = like[n].shape
        seg = packed[off:off + r]
        out[n] = (seg.reshape(cols, rows).T if tr else seg)[None]
        off += rp
    return out


def _segments(g, group):
    out, off = {}, 0
    for n, r, rp, _ in group[0]:
        out[n] = g[:, off:off + r]
        off += rp
    return out


def _pack_grads(parts, group):
    table, rows = group
    segs = [jnp.pad(parts[n], ((0, 0), (0, rp - parts[n].shape[1]), (0, 0))) for n, _, rp, _ in table]
    segs.append(jnp.zeros((N_DEV, rows - sum(rp for _, _, rp, _ in table), PACK_COLS), F32))
    return jnp.concatenate(segs, axis=1)


def _owner_rows_early(g):
    g_in = jnp.concatenate([g["w_in_t"][:2432], g["w_in_t"][2496:2528]], axis=0).reshape(N_DEV, 308, PACK_COLS)
    g_uq = g["w_uq_t"].reshape(N_DEV, 128, MLA_Q_RANK)[:, :96].reshape(N_DEV, 24, PACK_COLS)
    g_ukv = jnp.concatenate([g["w_k_t"].reshape(N_DEV, 128, MLA_KV_RANK)[:, :64],
                             g["w_v_t"].reshape(N_DEV, 64, MLA_KV_RANK)], axis=1).reshape(N_DEV, 16, PACK_COLS)
    return dict(w_in=g_in, w_uq=g_uq, w_ukv=g_ukv)


def _owner_rows_late(g):
    g_up = g["w_up_t"].reshape(2, 2, 2, 704, PACK_COLS).swapaxes(0, 1).reshape(N_DEV, 704, PACK_COLS)
    return dict(w_out=g["w_out"].reshape(N_DEV, 128, PACK_COLS), w_up=g_up,
                w_down=g["w_down"].reshape(N_DEV, 352, PACK_COLS))


def _reduce_to_pairs(gp, *, name):
    gp = gp.reshape(4, 2, gp.shape[1], PACK_COLS)
    return _add_own(gp, _swap_sibling(gp, name=name + "_swap"), out_dtype=BF16, name=name + "_sum")


def _interleave_ff(w):
    g, v = w[..., :D_FF], w[..., D_FF:]
    return jnp.concatenate([g[..., :FF_HALF], v[..., :FF_HALF], g[..., FF_HALF:], v[..., FF_HALF:]], axis=-1)


def _rope_tables(pos):
    p = pos.astype(F32)[:, None]
    inv_r = ROPE_BASE ** (-jnp.arange(0, RET_HEAD_DIM, 2, dtype=F32) / RET_HEAD_DIM)
    ang = p * jnp.tile(inv_r, 4)
    sign_r = jnp.tile(jnp.concatenate([-jnp.ones((32,), F32), jnp.ones((32,), F32)]), 2)
    cos_r, ss_r = jnp.cos(ang), jnp.sin(ang) * sign_r
    inv_m = ROPE_BASE ** (-jnp.arange(0, MLA_ROPE, 2, dtype=F32) / MLA_ROPE)
    ang = p * jnp.concatenate([jnp.zeros((64,), F32), inv_m, inv_m, jnp.zeros((32,), F32)])
    sign_m = jnp.concatenate([jnp.zeros((64,), F32), -jnp.ones((16,), F32), jnp.ones((16,), F32), jnp.zeros((32,), F32)])
    cos_m, ss_m = jnp.cos(ang), jnp.sin(ang) * sign_m
    return cos_r, ss_r, cos_m, ss_m


def _prep_early(gathered):
    seg = _segments(gathered, EARLY)
    w_in_t = seg["w_in"].reshape(IN_WIDTH, D_MODEL)
    z = lambda n: jnp.zeros((n, D_MODEL), BF16)
    w_in_t = jnp.concatenate([w_in_t[:2432], z(64), w_in_t[2432:2464], z(32)], axis=0)
    w_uq_t = jnp.pad(seg["w_uq"].reshape(MLA_HEADS, 96, MLA_Q_RANK), ((0, 0), (0, 32), (0, 0))).reshape(1024, MLA_Q_RANK)
    ukv = seg["w_ukv"].reshape(MLA_HEADS, 128, MLA_KV_RANK)
    w_k_t = jnp.pad(ukv[:, :64], ((0, 0), (0, 64), (0, 0))).reshape(1024, MLA_KV_RANK)
    w_v_t = ukv[:, 64:].reshape(512, MLA_KV_RANK)
    return dict(w_in_t=w_in_t, w_uq_t=w_uq_t, w_k_t=w_k_t, w_v_t=w_v_t)


def _prep_late(gathered):
    seg = _segments(gathered, LATE)
    w_up_t = seg["w_up"].reshape(2, 2, 2, 704, D_MODEL).swapaxes(0, 1).reshape(2 * D_FF, D_MODEL)
    return dict(w_out=seg["w_out"].reshape(1024, D_MODEL), w_up_t=w_up_t, w_down=seg["w_down"].reshape(D_FF, D_MODEL))


def _local_step(x, pos, tgt, early, sm, late):
    dist = not isinstance(late, dict)
    cos_r, ss_r, cos_m, ss_m = _rope_tables(pos)
    tabs = _ret_tables()

    if dist:
        h, gathered = _rmsnorm_fwd(x, sm["attn_norm_w"], gather=early, name="attn_norm")
        W = _prep_early(gathered)
        sm = {**sm, "conv_w": _interleave_ff(_conv_w_from_rows(gathered))}
    else:
        h = _rmsnorm_fwd(x, sm["attn_norm_w"], name="attn_norm")
        W = early
    proj = _mm_nt(h, W["w_in_t"], name="in_proj")
    y_ret, o_ret, qr, kr = _ret_fwd(proj, cos_r, ss_r, tabs, sm["ret_gn_w"], name="ret_fwd")
    q, k, v1, cqn, ckvn = _mla_prep_fwd(proj, sm["mla_q_norm_w"], sm["mla_kv_norm_w"], W["w_uq_t"], W["w_k_t"],
                                       W["w_v_t"], cos_m, ss_m, name="mla_prep")
    T = x.shape[0]
    tq = min(T, 512)
    if dist:
        y_mla, lse, gathered = _flash_fwd(q, k, v1, gather=late, name="mla_attn")
        W = {**W, **_prep_late(gathered)}
    else:
        y_mla, lse = _flash_fwd(q, k, v1, name="mla_attn")
        W = {**W, **late}
    mixed = (y_ret, y_mla)
    x1, h2, u, a = _up_proj_conv(x, y_ret, y_mla, W["w_out"], sm["ffn_norm_w"], W["w_up_t"], sm["conv_w"],
                                 sm["conv_b"], name="out_proj_ffn_up_conv")
    loss, dx2, dx2b, d_final = _down_proj_loss(a, W["w_down"], x1, tgt, sm["final_norm_w"], name="down_proj_loss")

    g = {}
    g["w_down"] = _mm_tn(a, dx2b, name="dw_down")
    da = _mm_nt(dx2b, W["w_down"], name="d_act")
    du, dcw0, dcw1, dcw2, dcb = _conv_bwd(u, da, sm["conv_w"], sm["conv_b"], name="conv_bwd")
    g["w_up_t"] = _mm_tn(du, h2, name="dw_up")
    dx1, d_ffn, g["w_out"] = _mm_norm_bwd(du, W["w_up_t"], x1, sm["ffn_norm_w"], dx2, left=mixed,
                                          name="d_h2_ffn_norm_bwd_dw_out")

    do_ret, dg, do_mla, delta, d_gn = _mix_bwd(dx1, W["w_out"], o_ret, proj, y_mla, sm["ret_gn_w"], name="d_mixed_mix_bwd")
    drq = _ret_bwd_dq(kr, proj, do_ret, cos_r, ss_r, tabs, name="ret_bwd_dq")
    delta_r = delta.reshape(MLA_HEADS, T // tq, 1, tq)
    if dist:
        gl = _pack_grads(_owner_rows_late(g), LATE).reshape(4, 2, LATE[1], PACK_COLS)
        drk, drv, theirs = _ret_bwd_dkv(qr, kr, proj, do_ret, cos_r, ss_r, tabs, swap=gl, name="ret_bwd_dkv")
        pair = _add_own(gl, theirs, out_dtype=BF16, name="grad_late_sum")
        dqt, dk, dv, slots_late = _flash_bwd(q, k, v1, do_mla, lse, delta_r, exchange=pair, name="mla_attn_bwd")
    else:
        drk, drv = _ret_bwd_dkv(qr, kr, proj, do_ret, cos_r, ss_r, tabs, name="ret_bwd_dkv")
        dqt, dk, dv = _flash_bwd(q, k, v1, do_mla, lse, delta_r, name="mla_attn_bwd")
        slots_late = None
    dq = dqt.transpose(1, 3, 0, 2).reshape(T, MLA_HEADS * 128)
    dproj, g["w_in_t"], g["w_uq_t"], g["w_k_t"], g["w_v_t"], d_qn, d_kvn = _mla_prep_bwd(
        dq, dk, dv, proj, sm["mla_q_norm_w"], sm["mla_kv_norm_w"], W["w_uq_t"], W["w_k_t"], W["w_v_t"], cos_m, ss_m,
        (drq, drk, drv, dg), cqn, ckvn, h, name="mla_prep_bwd")
    if dist:
        pair = _reduce_to_pairs(_pack_grads(_owner_rows_early(g), EARLY), name="grad_early")
        grad_x, d_attn, slots_early = _mm_norm_bwd(dproj, W["w_in_t"], x, sm["attn_norm_w"], dx1, exchange=pair,
                                                   name="d_h_attn_norm_bwd")
    else:
        grad_x, d_attn = _mm_norm_bwd(dproj, W["w_in_t"], x, sm["attn_norm_w"], dx1, name="d_h_attn_norm_bwd")
        slots_early = None

    small = dict(attn_norm_w=d_attn, ret_gn_w=d_gn, mla_q_norm_w=d_qn, mla_kv_norm_w=d_kvn, ffn_norm_w=d_ffn,
                 conv_b=dcb, final_norm_w=d_final, conv_w0=dcw0, conv_w1=dcw1, conv_w2=dcw2, loss=loss)
    return loss, grad_x, g, small, slots_early, slots_late


def kernel(x, positions, attn_norm_w, w_in, ret_gn_w, mla_q_norm_w, w_uq, mla_kv_norm_w, w_ukv, w_out, ffn_norm_w, w_up, conv_w, conv_b, w_down, final_norm_w, loss_target, m_attn_norm_w, m_w_in, m_ret_gn_w, m_mla_q_norm_w, m_w_uq, m_mla_kv_norm_w, m_w_ukv, m_w_out, m_ffn_norm_w, m_w_up, m_conv_w, m_conv_b, m_w_down, m_final_norm_w, v_attn_norm_w, v_w_in, v_ret_gn_w, v_mla_q_norm_w, v_w_uq, v_mla_kv_norm_w, v_w_ukv, v_w_out, v_ffn_norm_w, v_w_up, v_conv_w, v_conv_b, v_w_down, v_final_norm_w):
    a = dict(locals())
    x_, y_, c_ = _place()
    dev = 4 * x_ + 2 * y_ + c_

    shard = {n: a[n][0] for n in BIG_NAMES}
    shard16 = {n: w.astype(BF16) for n, w in shard.items()}
    sm = dict(attn_norm_w=attn_norm_w, ret_gn_w=ret_gn_w, mla_q_norm_w=mla_q_norm_w, mla_kv_norm_w=mla_kv_norm_w,
              ffn_norm_w=ffn_norm_w, final_norm_w=final_norm_w.reshape(1, D_MODEL), conv_b=_interleave_ff(conv_b))

    loss, grad_x, _, gs, slots_early, slots_late = _local_step(
        x[0], positions[0], loss_target[0], _pack_local(shard16, EARLY, tail=_conv_w_as_rows(conv_w[0])), sm,
        _pack_local(shard16, LATE))

    big = [{}, {}, {}, {}]
    for group, slots, tag, calls in ((EARLY, slots_early, "early", (("w_in", "w_uq", "w_ukv"),)),
                                     (LATE, slots_late, "late", (("w_out", "w_down"), ("w_up",)))):
        grads = _unpack_local(_sum_chips(slots, name="grad_sum_" + tag), shard, group)
        for names_c in calls:
            res = _adamw_multi([shard[n] for n in names_c], [a["m_" + n][0] for n in names_c],
                               [a["v_" + n][0] for n in names_c], [grads[n][0] for n in names_c],
                               name="adamw_" + "_".join(names_c))
            for kind in range(4):
                for n, r in zip(names_c, res[kind]):
                    big[kind][n] = r[None]

    packed = _pack_small(gs, name="pack_small_grads")
    tot = _sum_small(_all_gather(packed, name="gather_small_grads"), name="sum_small_grads")
    loss_out = tot["loss"][0, 0]
    g_cw = lax.dynamic_slice_in_dim(tot["conv_w"], dev * 704, 704, axis=1)

    def rows_of(prefix):
        return [a[prefix + n].reshape(1, size) for n, size in SMALL]

    sml = _adamw_multi(rows_of("") + [conv_w[0]], rows_of("m_") + [m_conv_w[0]], rows_of("v_") + [v_conv_w[0]],
                       [tot[n] for n, _ in SMALL] + [g_cw], name="adamw_small")
    cwo = [kind[-1] for kind in sml]

    def small_of(kind, n):
        return sml[kind][[nm for nm, _ in SMALL].index(n)].reshape(a[n].shape)

    names = ['attn_norm_w', 'w_in', 'ret_gn_w', 'mla_q_norm_w', 'w_uq', 'mla_kv_norm_w', 'w_ukv', 'w_out',
             'ffn_norm_w', 'w_up', 'conv_w', 'conv_b', 'w_down', 'final_norm_w']
    outs = [loss_out, grad_x[None]]
    for kind in range(4):
        for n in names:
            if n == "conv_w":
                outs.append(cwo[kind][None])
            elif n in big[kind]:
                outs.append(big[kind][n])
            else:
                outs.append(small_of(kind, n))
    return tuple(outs)
```

```python
import jax
import jax.numpy as jnp
from jax import lax
from jax.experimental import pallas as pl
from jax.experimental.pallas import tpu as pltpu

F32 = jnp.float32
BF16 = jnp.bfloat16
MESH = pl.DeviceIdType.MESH
ANY = pl.BlockSpec(memory_space=pl.ANY)

D_MODEL = 1024
RET_HEADS = 8
RET_HEAD_DIM = 64
RET_WIDTH = 512
RET_CHUNK = 128
RET_TILE = 4096
MLA_HEADS = 8
MLA_NOPE = 64
MLA_ROPE = 32
MLA_V = 64
MLA_Q_RANK = 256
MLA_KV_RANK = 128
MLA_WIDTH = 512
IN_WIDTH = 2464
IN_PAD = 2560
D_FF = 2816
FF_HALF = 1408
ROPE_BASE = 10000.0
EPS = 1e-6
SCALE = float((MLA_NOPE + MLA_ROPE) ** -0.5)
K_SCALE = 0.125
N_DEV = 8

ADAM_LR = 0.001
ADAM_B1 = 0.9
ADAM_B2 = 0.999
ADAM_EPS = 1e-08
ADAM_WD = 0.01
ADAM_STEP = 10

VMEM_LIMIT = 56 * 1024 * 1024
MM_BUDGET = 40 * 1024 * 1024
NEG = -1e30
FLASH_UNROLL = 4
FLASH_BWD_UNROLL = 3

PACK_COLS = 1024
EARLY = ((("w_in", 308, 320, True), ("w_uq", 24, 32, True), ("w_ukv", 16, 16, True)), 384)
LATE = ((("w_out", 128, 128, False), ("w_up", 704, 704, True), ("w_down", 352, 352, False)), 1200)
BIG_NAMES = ("w_in", "w_uq", "w_ukv", "w_out", "w_up", "w_down")
SMALL = (("attn_norm_w", 1024), ("ret_gn_w", 512), ("mla_q_norm_w", 256), ("mla_kv_norm_w", 128),
         ("ffn_norm_w", 1024), ("conv_b", 5632), ("final_norm_w", 1024))
SMALL_VECTORS = SMALL + (("conv_w0", 5632), ("conv_w1", 5632), ("conv_w2", 5632), ("loss", 128))
SMALL_ROWS = 32


def _cp(sem=None, vmem=VMEM_LIMIT):
    return pltpu.CompilerParams(dimension_semantics=sem, vmem_limit_bytes=vmem)


def _dot(a, b):
    return jnp.dot(a, b, preferred_element_type=F32)


def _dot_nt(a, b):
    return lax.dot_general(a, b, (((1,), (1,)), ((), ())), preferred_element_type=F32)


def _dot_tn(a, b):
    return lax.dot_general(a, b, (((0,), (0,)), ((), ())), preferred_element_type=F32)


def _sigmoid(x):
    return 0.5 * jnp.tanh(0.5 * x) + 0.5


def _partner(x, half, period):
    n = x.shape[-1]
    lane = lax.broadcasted_iota(jnp.int32, x.shape, 1)
    return jnp.where((lane % period) < half, pltpu.roll(x, n - half, 1), pltpu.roll(x, half, 1))


def _rope(x, cos, ss, half, period):
    return x * cos + _partner(x, half, period) * ss


def _rope_t(dy, cos, ss, half, period):
    return dy * cos - _partner(dy, half, period) * ss


def _head_masks(shape):
    lane = lax.broadcasted_iota(jnp.int32, shape, 1)
    m0 = (lane < 64).astype(F32)
    return m0, 1.0 - m0


def _mm_nt(a, b, *, name):
    M, K = a.shape
    N = b.shape[0]
    per_row = 2 * (K * a.dtype.itemsize + N * 4)
    tm = 128
    for cand in (512, 256):
        if M % cand == 0 and cand * per_row + 4 * K * N <= MM_BUDGET:
            tm = cand
            break
    tm = min(tm, M)

    def body(a_ref, b_ref, o_ref):
        o_ref[...] = _dot_nt(a_ref[...], b_ref[...])

    return pl.pallas_call(
        body, name=name, grid=(M // tm,),
        in_specs=[pl.BlockSpec((tm, K), lambda i: (i, 0)), pl.BlockSpec(b.shape, lambda i: (0, 0))],
        out_specs=pl.BlockSpec((tm, N), lambda i: (i, 0)), out_shape=jax.ShapeDtypeStruct((M, N), F32),
        compiler_params=_cp(("parallel",)))(a, b)


def _mm_tn(a, b, *, name):
    T, M = a.shape
    N = b.shape[1]
    tk = min(T, 512)

    def tile(n):
        for cand in (1408, 1280):
            if n > 1408 and n % cand == 0:
                return cand
        return n

    tm, tn = tile(M), tile(N)
    nk = T // tk

    def body(a_ref, b_ref, o_ref):
        @pl.when(pl.program_id(2) == 0)
        def _():
            o_ref[...] = jnp.zeros_like(o_ref)
        o_ref[...] += _dot_tn(a_ref[...], b_ref[...])

    return pl.pallas_call(
        body, name=name, grid=(M // tm, N // tn, nk),
        in_specs=[pl.BlockSpec((tk, tm), lambda i, j, k: (k, i)), pl.BlockSpec((tk, tn), lambda i, j, k: (k, j))],
        out_specs=pl.BlockSpec((tm, tn), lambda i, j, k: (i, j)),
        out_shape=jax.ShapeDtypeStruct((M, N), F32),
        compiler_params=_cp(("parallel", "parallel", "arbitrary")))(a, b)


def _rmsnorm_fwd(x, w, *, name, gather=None):
    T, D = x.shape
    tm = min(T, 1024)
    n = T // tm

    def body(x_ref, w_ref, *rest):
        if gather is not None:
            s_ref, o_ref, g_ref, *sems = rest
            start, forward, finish = _gather_phases(s_ref, g_ref, *sems)
            pl.when(pl.program_id(0) == 0)(start)
            pl.when(pl.program_id(0) == n // 2)(forward)
        else:
            o_ref, = rest
        xv = x_ref[...]
        r = lax.rsqrt(jnp.mean(xv * xv, axis=-1, keepdims=True) + EPS)
        o_ref[...] = (xv * r * w_ref[...]).astype(BF16)
        if gather is not None:
            pl.when(pl.program_id(0) == n - 1)(finish)

    in_specs = [pl.BlockSpec((tm, D), lambda i: (i, 0)), pl.BlockSpec((1, D), lambda i: (0, 0))]
    out_spec = pl.BlockSpec((tm, D), lambda i: (i, 0))
    out_shape = jax.ShapeDtypeStruct((T, D), BF16)
    if gather is None:
        return pl.pallas_call(body, name=name, grid=(n,), in_specs=in_specs, out_specs=out_spec, out_shape=out_shape,
                              compiler_params=_cp(("parallel",)))(x, w)
    return pl.pallas_call(
        body, name=name, grid=(n,), in_specs=in_specs + [ANY], out_specs=[out_spec, ANY],
        out_shape=[out_shape, jax.ShapeDtypeStruct((N_DEV,) + gather.shape, gather.dtype)],
        scratch_shapes=list(GATHER_SCRATCH), compiler_params=_cp(("arbitrary",)))(x, w, gather)


def _mm_norm_bwd(a, b, x, w, dres, *, name, exchange=None, left=None):
    T, K = a.shape
    D = b.shape[1]
    tm = min(T, 256 if K > 4096 else 512)
    n = T // tm
    n_left = 0 if left is None else len(left)

    def body(a_ref, b_ref, x_ref, w_ref, dr_ref, *rest):
        if exchange is not None:
            p_ref, dx_ref, dw_ref, got_ref, *sems = rest
            start, finish = _exchange_phases(p_ref, got_ref, *sems)
            pl.when(pl.program_id(0) == 0)(start)
        elif left is not None:
            left_refs, (dx_ref, dw_ref, gw_ref) = rest[:n_left], rest[n_left:]
        else:
            dx_ref, dw_ref = rest

        @pl.when(pl.program_id(0) == 0)
        def _():
            dw_ref[...] = jnp.zeros_like(dw_ref)
            if left is not None:
                gw_ref[...] = jnp.zeros_like(gw_ref)
        dh = _dot(a_ref[...], b_ref[...])
        xv = x_ref[...]
        r = lax.rsqrt(jnp.mean(xv * xv, axis=-1, keepdims=True) + EPS)
        xh = xv * r
        g = dh * w_ref[...]
        dx = dr_ref[...] + r * (g - xh * jnp.mean(g * xh, axis=-1, keepdims=True))
        dx_ref[...] = dx
        dw_ref[...] += jnp.sum(dh * xh, axis=0, keepdims=True)
        if left is not None:
            gw_ref[...] += _dot_tn(jnp.concatenate([r_[...] for r_ in left_refs], axis=1), dx.astype(BF16))
        if exchange is not None:
            pl.when(pl.program_id(0) == n - 1)(finish)

    row = pl.BlockSpec((tm, D), lambda i: (i, 0))
    vec = pl.BlockSpec((1, D), lambda i: (0, 0))
    in_specs = [pl.BlockSpec((tm, K), lambda i: (i, 0)), pl.BlockSpec((K, D), lambda i: (0, 0)), row, vec, row]
    out_shape = [jax.ShapeDtypeStruct((T, D), F32), jax.ShapeDtypeStruct((1, D), F32)]
    if left is not None:
        m = sum(p.shape[1] for p in left)
        return pl.pallas_call(
            body, name=name, grid=(n,),
            in_specs=in_specs + [pl.BlockSpec((tm, p.shape[1]), lambda i: (i, 0)) for p in left],
            out_specs=[row, vec, pl.BlockSpec((m, D), lambda i: (0, 0))],
            out_shape=out_shape + [jax.ShapeDtypeStruct((m, D), F32)],
            compiler_params=_cp(("arbitrary",)))(a, b, x, w, dres, *left)
    if exchange is None:
        return pl.pallas_call(body, name=name, grid=(n,), in_specs=in_specs, out_specs=[row, vec], out_shape=out_shape,
                              compiler_params=_cp(("arbitrary",)))(a, b, x, w, dres)
    return pl.pallas_call(
        body, name=name, grid=(n,), in_specs=in_specs + [ANY], out_specs=[row, vec, ANY],
        out_shape=out_shape + [jax.ShapeDtypeStruct(exchange.shape, exchange.dtype)],
        scratch_shapes=list(EXCHANGE_SCRATCH), compiler_params=_cp(("arbitrary",)))(a, b, x, w, dres, exchange)


def _down_proj_loss(a, w_down, x1, tgt, w, *, name):
    T, D = x1.shape
    K = a.shape[1]
    tm = min(T, 512)

    def body(a_ref, b_ref, x_ref, t_ref, w_ref, loss_ref, dx_ref, dxb_ref, dw_ref):
        @pl.when(pl.program_id(0) == 0)
        def _():
            dw_ref[...] = jnp.zeros_like(dw_ref)
            loss_ref[...] = jnp.zeros_like(loss_ref)
        xv = x_ref[...] + _dot(a_ref[...], b_ref[...])
        wv = w_ref[...]
        r = lax.rsqrt(jnp.mean(xv * xv, axis=-1, keepdims=True) + EPS)
        xh = xv * r
        e = xh * wv - t_ref[...]
        part = 0.5 * jnp.sum(jnp.mean(e * e, axis=-1, keepdims=True), axis=0, keepdims=True)
        loss_ref[...] += jnp.broadcast_to(part, loss_ref.shape)
        dy = e * (1.0 / D)
        g = dy * wv
        dx = r * (g - xh * jnp.mean(g * xh, axis=-1, keepdims=True))
        dx_ref[...] = dx
        dxb_ref[...] = dx.astype(BF16)
        dw_ref[...] += jnp.sum(dy * xh, axis=0, keepdims=True)

    row = pl.BlockSpec((tm, D), lambda i: (i, 0))
    vec = pl.BlockSpec((1, D), lambda i: (0, 0))
    return pl.pallas_call(
        body, name=name, grid=(T // tm,),
        in_specs=[pl.BlockSpec((tm, K), lambda i: (i, 0)), pl.BlockSpec((K, D), lambda i: (0, 0)), row, row, vec],
        out_specs=[pl.BlockSpec((1, 128), lambda i: (0, 0)), row, row, vec],
        out_shape=[jax.ShapeDtypeStruct((1, 128), F32), jax.ShapeDtypeStruct((T, D), F32),
                   jax.ShapeDtypeStruct((T, D), BF16), jax.ShapeDtypeStruct((1, D), F32)],
        compiler_params=_cp(("arbitrary",)))(a, w_down, x1, tgt, w)


def _ret_tables():
    C = RET_CHUNK
    h = jnp.arange(RET_HEADS, dtype=F32)
    log_gamma = jnp.log1p(-jnp.power(2.0, -5.0 - h))
    idx = jnp.arange(C, dtype=F32)
    diff = idx[:, None] - idx[None, :]
    dm = jnp.where(diff >= 0, jnp.exp(log_gamma[:, None, None] * jnp.maximum(diff, 0.0)), 0.0)
    dm = dm.reshape(4, 2 * C, C)
    lane_head = jnp.repeat(jnp.arange(RET_HEADS).reshape(4, 2), 64, axis=1)
    lg = log_gamma[lane_head]
    xi = jnp.exp(lg[:, None, :] * (idx[None, :, None] + 1.0))
    zeta = jnp.exp(lg[:, None, :] * (C - 1.0 - idx[None, :, None]))
    blk = (jnp.arange(128)[:, None] // 64) == (jnp.arange(128)[None, :] // 64)
    cd = jnp.where(blk[None], jnp.exp(lg * C)[:, :, None], 0.0)
    return dm.astype(F32), xi.astype(F32), zeta.astype(F32), cd.astype(F32)


def _ret_specs(tb, rev, nt, roped=False):
    def tmap(t):
        return (nt - 1 - t) if rev else t
    offsets = (0, 0, 8) if roped else (0, 4, 8)
    qkv = [pl.BlockSpec((tb, 128), lambda p, t, o=o: (tmap(t), o + p)) for o in offsets]
    rope = [pl.BlockSpec((tb, 128), lambda p, t: (tmap(t), 0))] * 2
    tabs = [pl.BlockSpec((None, 256, 128), lambda p, t: (p, 0, 0))] + \
           [pl.BlockSpec((None, 128, 128), lambda p, t: (p, 0, 0))] * 3
    return qkv, rope, tabs


def _ret_fwd(proj, cos, ss, tabs, gnw, *, name):
    T = proj.shape[0]
    tb = min(T, RET_TILE)
    nt = T // tb
    nchunk = tb // RET_CHUNK

    def body(q_ref, k_ref, v_ref, g_ref, cos_ref, ss_ref, dm_ref, xi_ref, zt_ref, cd_ref, gnw_ref,
             y_ref, o_ref, qr_ref, kr_ref, r_sc):
        @pl.when(pl.program_id(1) == 0)
        def _():
            r_sc[...] = jnp.zeros_like(r_sc)
        m0, m1 = _head_masks((128, 128))
        dm, xi, zt, cd = dm_ref[...], xi_ref[...], zt_ref[...], cd_ref[...]
        bm = (cd > 0).astype(F32)
        gnw = gnw_ref[...]
        for c in range(nchunk):
            rs = pl.ds(c * RET_CHUNK, RET_CHUNK)
            cs, sn = cos_ref[rs, :], ss_ref[rs, :]
            q = _rope(q_ref[rs, :], cs, sn, 32, 64)
            k = _rope(k_ref[rs, :], cs, sn, 32, 64) * K_SCALE
            v = v_ref[rs, :]
            kb, vb = k.astype(BF16), v.astype(BF16)
            qr_ref[rs, :] = q.astype(BF16)
            kr_ref[rs, :] = kb
            qs = jnp.concatenate([q * m0, q * m1], axis=0).astype(BF16)
            s = (_dot_nt(qs, kb) * dm).astype(BF16)
            vs = jnp.concatenate([v * m0, v * m1], axis=0).astype(BF16)
            o = _dot(jnp.concatenate([s[:128], s[128:]], axis=1), vs)
            r = r_sc[...]
            o = o + _dot(q.astype(BF16), r.astype(BF16)) * xi
            r_sc[...] = cd * r + bm * _dot_tn((k * zt).astype(BF16), vb)
            mu = (jnp.sum(o * m0, axis=1, keepdims=True) * m0 + jnp.sum(o * m1, axis=1, keepdims=True) * m1) * (1.0 / 64)
            d = o - mu
            dd = d * d
            var = (jnp.sum(dd * m0, axis=1, keepdims=True) * m0 + jnp.sum(dd * m1, axis=1, keepdims=True) * m1) * (1.0 / 64)
            oh = d * lax.rsqrt(var + EPS)
            g = g_ref[rs, :]
            y_ref[rs, :] = (g * _sigmoid(g) * (oh * gnw)).astype(BF16)
            o_ref[rs, :] = o

    qkv, rope, tspec = _ret_specs(tb, False, nt)
    gspec = pl.BlockSpec((tb, 128), lambda p, t: (t, 12 + p))
    out = pl.BlockSpec((tb, 128), lambda p, t: (t, p))
    return pl.pallas_call(
        body, name=name, grid=(4, nt),
        in_specs=qkv + [gspec] + rope + tspec + [pl.BlockSpec((1, 128), lambda p, t: (0, p))],
        out_specs=[out, out, out, out],
        out_shape=[jax.ShapeDtypeStruct((T, RET_WIDTH), BF16), jax.ShapeDtypeStruct((T, RET_WIDTH), F32),
                   jax.ShapeDtypeStruct((T, RET_WIDTH), BF16), jax.ShapeDtypeStruct((T, RET_WIDTH), BF16)],
        scratch_shapes=[pltpu.VMEM((128, 128), F32)],
        compiler_params=_cp(("parallel", "arbitrary")))(proj, proj, proj, proj, cos, ss, *tabs, gnw)


def _ret_bwd_dq(kr, proj, do, cos, ss, tabs, *, name):
    T = proj.shape[0]
    tb = min(T, RET_TILE)
    nt = T // tb
    nchunk = tb // RET_CHUNK

    def body(k_ref, v_ref, do_ref, cos_ref, ss_ref, dm_ref, xi_ref, zt_ref, cd_ref, dq_ref, r_sc):
        @pl.when(pl.program_id(1) == 0)
        def _():
            r_sc[...] = jnp.zeros_like(r_sc)
        m0, m1 = _head_masks((128, 128))
        dm, xi, zt, cd = dm_ref[...], xi_ref[...], zt_ref[...], cd_ref[...]
        bm = (cd > 0).astype(F32)
        for c in range(nchunk):
            rs = pl.ds(c * RET_CHUNK, RET_CHUNK)
            cs, sn = cos_ref[rs, :], ss_ref[rs, :]
            k = k_ref[rs, :].astype(F32)
            vb = v_ref[rs, :].astype(BF16)
            dob = do_ref[rs, :]
            dof = dob.astype(F32)
            dos = jnp.concatenate([dof * m0, dof * m1], axis=0).astype(BF16)
            a = (_dot_nt(dos, vb) * dm).astype(BF16)
            ks = jnp.concatenate([k * m0, k * m1], axis=0).astype(BF16)
            r = r_sc[...]
            dq = _dot(jnp.concatenate([a[:128], a[128:]], axis=1), ks) + _dot_nt(dob, r.astype(BF16)) * xi
            r_sc[...] = cd * r + bm * _dot_tn((k * zt).astype(BF16), vb)
            dq_ref[rs, :] = _rope_t(dq, cs, sn, 32, 64).astype(BF16)

    qkv, rope, tspec = _ret_specs(tb, False, nt, roped=True)
    blk = pl.BlockSpec((tb, 128), lambda p, t: (t, p))
    return pl.pallas_call(
        body, name=name, grid=(4, nt), in_specs=qkv[1:] + [blk] + rope + tspec, out_specs=blk,
        out_shape=jax.ShapeDtypeStruct((T, RET_WIDTH), BF16),
        scratch_shapes=[pltpu.VMEM((128, 128), F32)],
        compiler_params=_cp(("parallel", "arbitrary")))(kr, proj, do, cos, ss, *tabs)


def _ret_bwd_dkv(qr, kr, proj, do, cos, ss, tabs, *, name, swap=None):
    T = proj.shape[0]
    tb = min(T, RET_TILE)
    nt = T // tb
    nchunk = tb // RET_CHUNK

    def body(q_ref, k_ref, v_ref, do_ref, cos_ref, ss_ref, dm_ref, xi_ref, zt_ref, cd_ref, *rest):
        if swap is None:
            backward(q_ref, k_ref, v_ref, do_ref, cos_ref, ss_ref, dm_ref, xi_ref, zt_ref, cd_ref, *rest)
        else:
            g_ref, dk_ref, dv_ref, got_ref, u_sc, *sems = rest
            start, finish = _swap_phases(g_ref, got_ref, *sems)
            pl.when((pl.program_id(0) == 0) & (pl.program_id(1) == 0))(start)
            backward(q_ref, k_ref, v_ref, do_ref, cos_ref, ss_ref, dm_ref, xi_ref, zt_ref, cd_ref, dk_ref, dv_ref, u_sc)
            pl.when((pl.program_id(0) == 3) & (pl.program_id(1) == nt - 1))(finish)

    def backward(q_ref, k_ref, v_ref, do_ref, cos_ref, ss_ref, dm_ref, xi_ref, zt_ref, cd_ref, dk_ref, dv_ref, u_sc):
        @pl.when(pl.program_id(1) == 0)
        def _():
            u_sc[...] = jnp.zeros_like(u_sc)
        m0, m1 = _head_masks((128, 128))
        dm, xi, zt, cd = dm_ref[...], xi_ref[...], zt_ref[...], cd_ref[...]
        bm = (cd > 0).astype(F32)
        for c in reversed(range(nchunk)):
            rs = pl.ds(c * RET_CHUNK, RET_CHUNK)
            cs, sn = cos_ref[rs, :], ss_ref[rs, :]
            kb = k_ref[rs, :]
            q = q_ref[rs, :].astype(F32)
            vb = v_ref[rs, :].astype(BF16)
            dob = do_ref[rs, :]
            dof = dob.astype(F32)
            qs = jnp.concatenate([q * m0, q * m1], axis=0).astype(BF16)
            dos = jnp.concatenate([dof * m0, dof * m1], axis=0).astype(BF16)
            s = (_dot_nt(qs, kb) * dm).astype(BF16)
            a = (_dot_nt(dos, vb) * dm).astype(BF16)
            ub = u_sc[...].astype(BF16)
            dk = _dot_tn(a, qs) + _dot_nt(vb, ub) * zt
            dv = _dot_tn(s, dos) + _dot(kb, ub) * zt
            u_sc[...] = cd * u_sc[...] + bm * _dot_tn((q * xi).astype(BF16), dob)
            dk_ref[rs, :] = (_rope_t(dk, cs, sn, 32, 64) * K_SCALE).astype(BF16)
            dv_ref[rs, :] = dv.astype(BF16)

    qkv, rope, tspec = _ret_specs(tb, True, nt, roped=True)
    blk = pl.BlockSpec((tb, 128), lambda p, t: (nt - 1 - t, p))
    out_shape = [jax.ShapeDtypeStruct((T, RET_WIDTH), BF16)] * 2
    if swap is None:
        return pl.pallas_call(
            body, name=name, grid=(4, nt), in_specs=qkv + [blk] + rope + tspec, out_specs=[blk, blk],
            out_shape=out_shape, scratch_shapes=[pltpu.VMEM((128, 128), F32)],
            compiler_params=_cp(("parallel", "arbitrary")))(qr, kr, proj, do, cos, ss, *tabs)
    return pl.pallas_call(
        body, name=name, grid=(4, nt), in_specs=qkv + [blk] + rope + tspec + [ANY], out_specs=[blk, blk, ANY],
        out_shape=out_shape + [jax.ShapeDtypeStruct((4,) + swap.shape[2:], swap.dtype)],
        scratch_shapes=[pltpu.VMEM((128, 128), F32)] + list(SWAP_SCRATCH),
        compiler_params=_cp(("arbitrary", "arbitrary")))(qr, kr, proj, do, cos, ss, *tabs, swap)


def _mix_bwd(dx1, w_out, o_ret, proj, y_mla, gnw, *, name):
    T = dx1.shape[0]
    tm = min(T, 512)

    def body(dx_ref, wo_ref, o_ref, g_ref, ym_ref, gnw_ref, do_ref, dg_ref, dom_ref, dl_ref, dw_ref, dm_ref):
        @pl.when(pl.program_id(0) == 0)
        def _():
            dw_ref[...] = jnp.zeros_like(dw_ref)
        dm_ref[...] = _dot_nt(dx_ref[...].astype(BF16), wo_ref[...])
        m0, m1 = _head_masks((tm, 128))
        lane = lax.broadcasted_iota(jnp.int32, (tm, 128), 1)
        delta = jnp.zeros((tm, 128), F32)

        def gsum(z):
            return jnp.sum(z * m0, axis=1, keepdims=True) * m0 + jnp.sum(z * m1, axis=1, keepdims=True) * m1

        for p in range(4):
            cs = slice(128 * p, 128 * p + 128)
            dy = dm_ref[:, cs]
            o = o_ref[:, cs]
            g = g_ref[:, cs]
            w = gnw_ref[:, cs]
            d = o - gsum(o) * (1.0 / 64)
            rstd = lax.rsqrt(gsum(d * d) * (1.0 / 64) + EPS)
            oh = d * rstd
            sg = _sigmoid(g)
            dn = dy * (g * sg)
            dg_ref[:, cs] = (dy * (oh * w) * (sg * (1.0 + g * (1.0 - sg)))).astype(BF16)
            dw_ref[:, cs] += jnp.sum(dn * oh, axis=0, keepdims=True)
            doh = dn * w
            do = rstd * (doh - gsum(doh) * (1.0 / 64) - oh * (gsum(doh * oh) * (1.0 / 64)))
            do_ref[:, cs] = do.astype(BF16)
            dom = dm_ref[:, 512 + 128 * p:512 + 128 * p + 128]
            dom_ref[:, cs] = dom.astype(BF16)
            pr = dom * ym_ref[:, cs].astype(F32)
            delta = jnp.where(lane == 2 * p, jnp.sum(pr * m0, axis=1, keepdims=True), delta)
            delta = jnp.where(lane == 2 * p + 1, jnp.sum(pr * m1, axis=1, keepdims=True), delta)
        dl_ref[...] = delta.T[0:MLA_HEADS]

    half = pl.BlockSpec((tm, 512), lambda i: (i, 0))
    return pl.pallas_call(
        body, name=name, grid=(T // tm,),
        in_specs=[pl.BlockSpec((tm, D_MODEL), lambda i: (i, 0)), pl.BlockSpec(w_out.shape, lambda i: (0, 0)), half,
                  pl.BlockSpec((tm, 512), lambda i: (i, 3)), half, pl.BlockSpec((1, 512), lambda i: (0, 0))],
        out_specs=[half, half, half, pl.BlockSpec((MLA_HEADS, tm), lambda i: (0, i)),
                   pl.BlockSpec((1, 512), lambda i: (0, 0))],
        out_shape=[jax.ShapeDtypeStruct((T, 512), BF16)] * 3 + [jax.ShapeDtypeStruct((MLA_HEADS, T), F32),
                                                                jax.ShapeDtypeStruct((1, 512), F32)],
        scratch_shapes=[pltpu.VMEM((tm, 1024), F32)],
        compiler_params=_cp(("arbitrary",)))(dx1, w_out, o_ret, proj, y_mla, gnw)


def _mla_prep_fwd(proj, qnw, kvnw, wuq, wk, wv, cos, ss, *, name):
    T = proj.shape[0]
    tm = min(T, 512)

    def body(lat_ref, qnw_ref, kvnw_ref, wuq_ref, wk_ref, wv_ref, cos_ref, ss_ref,
             q_ref, k_ref, v_ref, cqn_ref, ckvn_ref):
        cq = lat_ref[:, 0:256]
        ckv = lat_ref[:, 256:384]
        g3 = lat_ref[:, 384:512]
        cqn = (cq * lax.rsqrt(jnp.mean(cq * cq, axis=-1, keepdims=True) + EPS) * qnw_ref[...]).astype(BF16)
        ckvn = (ckv * lax.rsqrt(jnp.mean(ckv * ckv, axis=-1, keepdims=True) + EPS) * kvnw_ref[...]).astype(BF16)
        cqn_ref[...] = cqn
        ckvn_ref[...] = ckvn
        cs, sn = cos_ref[...], ss_ref[...]
        q = _dot_nt(cqn, wuq_ref[...])
        k = _dot_nt(ckvn, wk_ref[...])
        kpe = _rope(g3, cs, sn, 16, 32)
        for h in range(MLA_HEADS):
            hs = slice(128 * h, 128 * h + 128)
            q_ref[:, hs] = (_rope(q[:, hs], cs, sn, 16, 32) * SCALE).astype(BF16)
            k_ref[:, hs] = (k[:, hs] + kpe).astype(BF16)
        v = _dot_nt(ckvn, wv_ref[...])
        lane = lax.broadcasted_iota(jnp.int32, (tm, 128), 1)
        for p in range(4):
            vp = v[:, 128 * p:128 * p + 128]
            v_ref[:, 256 * p:256 * p + 128] = jnp.where(lane < 64, vp, 1.0).astype(BF16)
            v_ref[:, 256 * p + 128:256 * p + 256] = jnp.where(lane < 64, 1.0, vp).astype(BF16)

    def full(shape):
        return pl.BlockSpec(shape, lambda i: (0, 0))

    def row(w):
        return pl.BlockSpec((tm, w), lambda i: (i, 0))

    return pl.pallas_call(
        body, name=name, grid=(T // tm,),
        in_specs=[pl.BlockSpec((tm, 512), lambda i: (i, 4)), full((1, 256)), full((1, 128)), full((1024, 256)),
                  full((1024, 128)), full((512, 128)), row(128), row(128)],
        out_specs=[row(1024), row(1024), row(1024), row(256), row(128)],
        out_shape=[jax.ShapeDtypeStruct((T, 1024), BF16), jax.ShapeDtypeStruct((T, 1024), BF16),
                   jax.ShapeDtypeStruct((T, 1024), BF16), jax.ShapeDtypeStruct((T, 256), BF16),
                   jax.ShapeDtypeStruct((T, 128), BF16)],
        compiler_params=_cp(("parallel",)))(proj, qnw, kvnw, wuq, wk, wv, cos, ss)


def _mla_prep_bwd(dq, dk, dv, proj, qnw, kvnw, wuq_t, wk_t, wv_t, cos, ss, ret_grads, cqn, ckvn, h, *, name):
    T = proj.shape[0]
    tm = min(T, 512)

    def body(dq_ref, dk_ref, dv_ref, lat_ref, qnw_ref, kvnw_ref, wuq_ref, wk_ref, wv_ref, cos_ref, ss_ref,
             rq_ref, rk_ref, rv_ref, rg_ref, cqn_ref, ckvn_ref, h_ref,
             dproj_ref, gwin_ref, gwuq_ref, gwk_ref, gwv_ref, dqnw_ref, dkvnw_ref, dqp_ref):
        for j, r in enumerate((rq_ref, rk_ref, rv_ref, rg_ref)):
            dproj_ref[:, 512 * j:512 * j + 512] = r[...]
        dlat_ref = dproj_ref.at[:, 2048:2560]

        @pl.when(pl.program_id(0) == 0)
        def _():
            for r in (gwin_ref, gwuq_ref, gwk_ref, gwv_ref, dqnw_ref, dkvnw_ref):
                r[...] = jnp.zeros_like(r)
        cs, sn = cos_ref[...], ss_ref[...]
        dkpe = jnp.zeros((tm, 128), F32)
        for h in range(MLA_HEADS):
            hs = slice(128 * h, 128 * h + 128)
            dqp_ref[:, hs] = _rope_t(dq_ref[:, hs] * SCALE, cs, sn, 16, 32).astype(BF16)
            dkpe = dkpe + dk_ref[:, hs]
        lane = lax.broadcasted_iota(jnp.int32, (tm, 128), 1)
        rope_lane = (lane >= MLA_NOPE) & (lane < MLA_NOPE + MLA_ROPE)
        dg3 = jnp.where(rope_lane, _rope_t(jnp.where(rope_lane, dkpe, 0.0), cs, sn, 16, 32), 0.0)

        def norm_bwd(x, w, dn):
            r = lax.rsqrt(jnp.mean(x * x, axis=-1, keepdims=True) + EPS)
            xh = x * r
            g = dn * w
            return r * (g - xh * jnp.mean(g * xh, axis=-1, keepdims=True)), jnp.sum(dn * xh, axis=0, keepdims=True)

        dqp = dqp_ref[...]
        dkb = dk_ref[...].astype(BF16)
        dvb = dv_ref[...]
        dcqn = _dot(dqp, wuq_ref[...])
        dcq, dqnw = norm_bwd(lat_ref[:, 0:256], qnw_ref[...], dcqn)
        dckvn = _dot(dkb, wk_ref[...]) + _dot(dvb, wv_ref[...])
        dckv, dkvnw = norm_bwd(lat_ref[:, 256:384], kvnw_ref[...], dckvn)
        gwuq_ref[...] += _dot_tn(dqp, cqn_ref[...])
        gwk_ref[...] += _dot_tn(dkb, ckvn_ref[...])
        gwv_ref[...] += _dot_tn(dvb, ckvn_ref[...])
        dqnw_ref[...] += dqnw
        dkvnw_ref[...] += dkvnw
        dlat_ref[:, 0:256] = dcq.astype(BF16)
        dlat_ref[:, 256:384] = dckv.astype(BF16)
        dlat_ref[:, 384:512] = dg3.astype(BF16)
        gwin_ref[...] += _dot_tn(dproj_ref[...], h_ref[...])

    def full(shape):
        return pl.BlockSpec(shape, lambda i: (0, 0))

    def row(w):
        return pl.BlockSpec((tm, w), lambda i: (i, 0))

    return pl.pallas_call(
        body, name=name, grid=(T // tm,),
        in_specs=[row(1024), row(1024), row(512), pl.BlockSpec((tm, 512), lambda i: (i, 4)), full((1, 256)),
                  full((1, 128)), full((1024, 256)), full((1024, 128)), full((512, 128)), row(128), row(128)]
                 + [row(512)] * 4 + [row(256), row(128), row(D_MODEL)],
        out_specs=[row(IN_PAD), full((IN_PAD, D_MODEL)), full((1024, 256)), full((1024, 128)), full((512, 128)),
                   full((1, 256)), full((1, 128))],
        out_shape=[jax.ShapeDtypeStruct((T, IN_PAD), BF16), jax.ShapeDtypeStruct((IN_PAD, D_MODEL), F32),
                   jax.ShapeDtypeStruct((1024, 256), F32), jax.ShapeDtypeStruct((1024, 128), F32),
                   jax.ShapeDtypeStruct((512, 128), F32), jax.ShapeDtypeStruct((1, 256), F32),
                   jax.ShapeDtypeStruct((1, 128), F32)],
        scratch_shapes=[pltpu.VMEM((tm, 1024), BF16)],
        compiler_params=_cp(("arbitrary",)))(dq, dk, dv, proj, qnw, kvnw, wuq_t, wk_t, wv_t, cos, ss, *ret_grads,
                                             cqn, ckvn, h)


def _flash_fwd(q, k, v1, *, name, gather=None):
    T = q.shape[0]
    tq = min(T, 512)
    tk = tq
    nq = T // tq

    def body(q_ref, k_ref, v_ref, *rest):
        if gather is None:
            y_ref, lse_ref = rest
        else:
            x_ref, y_ref, lse_ref, g_ref, *sems = rest
            start, forward, finish = _gather_phases(x_ref, g_ref, *sems)
            pl.when((pl.program_id(0) == 0) & (pl.program_id(1) == 0))(start)
            pl.when((pl.program_id(0) == 1) & (pl.program_id(1) == 0))(forward)
        attend(q_ref, k_ref, v_ref, y_ref, lse_ref)
        if gather is not None:
            pl.when((pl.program_id(0) == 3) & (pl.program_id(1) == nq - 1))(finish)

    def attend(q_ref, k_ref, v_ref, y_ref, lse_ref):
        qi = pl.program_id(1)
        row = lax.broadcasted_iota(jnp.int32, (tq, tk), 0)
        col = lax.broadcasted_iota(jnp.int32, (tq, tk), 1)

        def step(kb, carry, masked):
            ks = pl.ds(pl.multiple_of(kb * tk, tk), tk)
            new = []
            for h in range(2):
                hs = slice(128 * h, 128 * h + 128)
                m, acc = carry[h]
                s = _dot_nt(q_ref[:, hs], k_ref[ks, hs])
                if masked:
                    s = jnp.where(col <= row, s, NEG)
                mn = jnp.maximum(m, jnp.max(s, axis=1, keepdims=True))
                p = jnp.exp((s - mn).astype(BF16))
                acc = jnp.exp(m - mn) * acc + _dot(p, v_ref[ks, hs])
                new.append((mn, acc))
            return tuple(new)

        def unrolled(j, c):
            for u in range(FLASH_UNROLL):
                c = step(FLASH_UNROLL * j + u, c, False)
            return c

        init = (jnp.full((tq, 1), NEG, F32), jnp.zeros((tq, 128), F32))
        carry = lax.fori_loop(0, qi // FLASH_UNROLL, unrolled, (init, init))
        carry = lax.fori_loop(FLASH_UNROLL * (qi // FLASH_UNROLL), qi, lambda kb, c: step(kb, c, False), carry)
        (ma, acca), (mb, accb) = step(qi, carry, True)
        lane = lax.broadcasted_iota(jnp.int32, (tq, 128), 1)
        la, lb = pltpu.roll(acca, 64, 1), pltpu.roll(accb, 64, 1)
        y_ref[...] = jnp.where(lane < 64, acca / la, accb / lb).astype(BF16)
        lse_ref[0, 0] = jnp.broadcast_to(ma + jnp.log(acca[:, 64:65]), (tq, 128)).T[0:1]
        lse_ref[1, 0] = jnp.broadcast_to(mb + jnp.log(accb[:, 0:1]), (tq, 128)).T[0:1]

    in_specs = [pl.BlockSpec((tq, 256), lambda p, i: (i, p)), pl.BlockSpec((T, 256), lambda p, i: (0, p)),
                pl.BlockSpec((T, 256), lambda p, i: (0, p))]
    out_specs = [pl.BlockSpec((tq, 128), lambda p, i: (i, p)), pl.BlockSpec((2, 1, 1, tq), lambda p, i: (p, i, 0, 0))]
    out_shape = [jax.ShapeDtypeStruct((T, MLA_WIDTH), BF16), jax.ShapeDtypeStruct((MLA_HEADS, nq, 1, tq), F32)]
    if gather is None:
        return pl.pallas_call(body, name=name, grid=(4, nq), in_specs=in_specs, out_specs=out_specs,
                              out_shape=out_shape, compiler_params=_cp(("parallel", "arbitrary")))(q, k, v1)
    return pl.pallas_call(
        body, name=name, grid=(4, nq), in_specs=in_specs + [ANY], out_specs=out_specs + [ANY],
        out_shape=out_shape + [jax.ShapeDtypeStruct((N_DEV,) + gather.shape, gather.dtype)],
        scratch_shapes=list(GATHER_SCRATCH),
        compiler_params=_cp(("arbitrary", "arbitrary")))(q, k, v1, gather)


def _flash_bwd(q, k, v, do, lse, delta, *, name, exchange=None):
    T = q.shape[0]
    tq = min(T, 512)
    tk = tq
    nq = T // tq

    def body(q_ref, k_ref, v_ref, do_ref, lse_ref, dl_ref, *rest):
        if exchange is None:
            backward(q_ref, k_ref, v_ref, do_ref, lse_ref, dl_ref, *rest)
        else:
            p_ref, dqt_ref, dk_ref, dv_ref, got_ref, *sems = rest
            start, finish = _exchange_phases(p_ref, got_ref, *sems)
            pl.when((pl.program_id(0) == 0) & (pl.program_id(1) == 0))(start)
            backward(q_ref, k_ref, v_ref, do_ref, lse_ref, dl_ref, dqt_ref, dk_ref, dv_ref)
            pl.when((pl.program_id(0) == 3) & (pl.program_id(1) == nq - 1))(finish)

    def backward(q_ref, k_ref, v_ref, do_ref, lse_ref, dl_ref, dqt_ref, dk_ref, dv_ref):
        kb = pl.program_id(1)

        @pl.when(kb == 0)
        def _():
            dqt_ref[...] = jnp.zeros_like(dqt_ref)
        krow = lax.broadcasted_iota(jnp.int32, (tk, tq), 0)
        qcol = lax.broadcasted_iota(jnp.int32, (tk, tq), 1)
        masks = _head_masks((tk, 128))
        vms = [(v_ref[:, 128 * h:128 * h + 128].astype(F32) * masks[h]).astype(BF16) for h in range(2)]

        def step(qi, carry, masked):
            qs = pl.ds(pl.multiple_of(qi * tq, tq), tq)
            dob = do_ref[qs, :]
            dof = dob.astype(F32)
            dks, dv_acc = list(carry[:2]), carry[2]
            for h in range(2):
                hs = slice(128 * h, 128 * h + 128)
                kh = k_ref[:, hs]
                qh = q_ref[qs, hs]
                st = _dot_nt(kh, qh)
                pt = jnp.exp((st - lse_ref[h, qi]).astype(BF16))
                if masked:
                    pt = jnp.where(krow <= qcol, pt, jnp.zeros_like(pt))
                dv_acc = dv_acc + _dot(pt, (dof * masks[h]).astype(BF16))
                dpt = _dot_nt(vms[h], dob)
                dst = pt * (dpt - dl_ref[h, qi]).astype(BF16)
                dks[h] = dks[h] + _dot(dst, qh)
                dqt_ref[qi, hs, :] += _dot_tn(kh, dst)
            return dks[0], dks[1], dv_acc

        zero = jnp.zeros((tk, 128), F32)
        carry = step(kb, (zero, zero, zero), True)

        def unrolled(j, c):
            for u in range(FLASH_BWD_UNROLL):
                c = step(kb + 1 + FLASH_BWD_UNROLL * j + u, c, False)
            return c

        trips = (nq - 1 - kb) // FLASH_BWD_UNROLL
        carry = lax.fori_loop(0, trips, unrolled, carry)
        dk0, dk1, dv_acc = lax.fori_loop(kb + 1 + FLASH_BWD_UNROLL * trips, nq, lambda qi, c: step(qi, c, False), carry)
        dk_ref[:, 0:128] = dk0
        dk_ref[:, 128:256] = dk1
        dv_ref[...] = dv_acc.astype(BF16)

    stat = pl.BlockSpec((2, nq, 1, tq), lambda p, j: (p, 0, 0, 0))
    in_specs = [pl.BlockSpec((T, 256), lambda p, j: (0, p)), pl.BlockSpec((tk, 256), lambda p, j: (j, p)),
                pl.BlockSpec((tk, 256), lambda p, j: (j, p)), pl.BlockSpec((T, 128), lambda p, j: (0, p)), stat, stat]
    out_specs = [pl.BlockSpec((None, nq, 256, tq), lambda p, j: (p, 0, 0, 0)),
                 pl.BlockSpec((tk, 256), lambda p, j: (j, p)), pl.BlockSpec((tk, 128), lambda p, j: (j, p))]
    out_shape = [jax.ShapeDtypeStruct((4, nq, 256, tq), F32), jax.ShapeDtypeStruct((T, 1024), F32),
                 jax.ShapeDtypeStruct((T, MLA_WIDTH), BF16)]
    if exchange is None:
        return pl.pallas_call(body, name=name, grid=(4, nq), in_specs=in_specs, out_specs=out_specs,
                              out_shape=out_shape,
                              compiler_params=_cp(("parallel", "arbitrary")))(q, k, v, do, lse, delta)
    return pl.pallas_call(
        body, name=name, grid=(4, nq), in_specs=in_specs + [ANY], out_specs=out_specs + [ANY],
        out_shape=out_shape + [jax.ShapeDtypeStruct(exchange.shape, exchange.dtype)],
        scratch_shapes=list(EXCHANGE_SCRATCH),
        compiler_params=_cp(("arbitrary", "arbitrary")))(q, k, v, do, lse, delta, exchange)


def _shift_down(x, n, prev8):
    r = pltpu.roll(x, n, 0)
    row = lax.broadcasted_iota(jnp.int32, prev8.shape, 0)
    first = jnp.where(row < n, pltpu.roll(prev8, n, 0), r[:8])
    if x.shape[0] == 8:
        return first
    return jnp.concatenate([first, r[8:]], axis=0)


def _shift_up(x, n, next8):
    tm = x.shape[0]
    r = pltpu.roll(x, tm - n, 0)
    row = lax.broadcasted_iota(jnp.int32, next8.shape, 0)
    last = jnp.where(row >= 8 - n, pltpu.roll(next8, 8 - n, 0), r[tm - 8:])
    return jnp.concatenate([r[:tm - 8], last], axis=0)


def _conv_pre(u, prev8, cw_ref, cb_ref):
    p1 = _shift_down(u, 1, prev8)
    p2 = _shift_down(u, 2, prev8)
    up = cb_ref[...] + cw_ref[0:1, :] * p2 + cw_ref[1:2, :] * p1 + cw_ref[2:3, :] * u
    return up, p1, p2


def _up_proj_conv(x, y_ret, y_mla, w_out, nw, w_up_t, cw, cb, *, name):
    T, K = x.shape
    tm = min(T, 256)

    def body(x_ref, yr_ref, ym_ref, wo_ref, nw_ref, w_ref, cw_ref, cb_ref, x1_ref, h_ref, u_ref, a_ref, carry_sc):
        @pl.when(pl.program_id(0) == 0)
        def _():
            carry_sc[...] = jnp.zeros_like(carry_sc)
        xv = x_ref[...] + _dot(jnp.concatenate([yr_ref[...], ym_ref[...]], axis=1), wo_ref[...])
        x1_ref[...] = xv
        h = (xv * lax.rsqrt(jnp.mean(xv * xv, axis=-1, keepdims=True) + EPS) * nw_ref[...]).astype(BF16)
        h_ref[...] = h
        for blk in range(2):
            ups = []
            for half in range(2):
                cs = slice((2 * blk + half) * FF_HALF, (2 * blk + half + 1) * FF_HALF)
                u = _dot_nt(h, w_ref[cs, :])
                u_ref[:, cs] = u
                prev = carry_sc[:, cs]
                ups.append(cb_ref[:, cs] + cw_ref[0:1, cs] * _shift_down(u, 2, prev)
                           + cw_ref[1:2, cs] * _shift_down(u, 1, prev) + cw_ref[2:3, cs] * u)
                carry_sc[:, cs] = u[tm - 8:]
            gate, val = ups
            a_ref[:, blk * FF_HALF:(blk + 1) * FF_HALF] = (gate * _sigmoid(gate) * val).astype(BF16)

    def full(shape):
        return pl.BlockSpec(shape, lambda i: (0, 0))

    return pl.pallas_call(
        body, name=name, grid=(T // tm,),
        in_specs=[pl.BlockSpec((tm, K), lambda i: (i, 0)), pl.BlockSpec((tm, RET_WIDTH), lambda i: (i, 0)),
                  pl.BlockSpec((tm, MLA_WIDTH), lambda i: (i, 0)), full(w_out.shape), full(nw.shape),
                  full(w_up_t.shape), full(cw.shape), full(cb.shape)],
        out_specs=[pl.BlockSpec((tm, K), lambda i: (i, 0)), pl.BlockSpec((tm, K), lambda i: (i, 0)),
                   pl.BlockSpec((tm, 2 * D_FF), lambda i: (i, 0)), pl.BlockSpec((tm, D_FF), lambda i: (i, 0))],
        out_shape=[jax.ShapeDtypeStruct((T, K), F32), jax.ShapeDtypeStruct((T, K), BF16),
                   jax.ShapeDtypeStruct((T, 2 * D_FF), F32), jax.ShapeDtypeStruct((T, D_FF), BF16)],
        scratch_shapes=[pltpu.VMEM((8, 2 * D_FF), F32)],
        compiler_params=_cp(("arbitrary",)))(x, y_ret, y_mla, w_out, nw, w_up_t, cw, cb)


def _conv_bwd(u, da, cw, cb, *, name):
    T = u.shape[0]
    tm = min(T, 512)
    W = 2 * FF_HALF
    nt = T // tm

    def body(u_ref, prev_ref, next_ref, da_ref, dan_ref, cw_ref, cb_ref, du_ref, dw0_ref, dw1_ref, dw2_ref, db_ref):
        i = pl.program_id(1)

        @pl.when(i == 0)
        def _():
            for r in (dw0_ref, dw1_ref, dw2_ref, db_ref):
                r[...] = jnp.zeros_like(r)

        def dpre(u, prev8, da):
            up, p1, p2 = _conv_pre(u, prev8, cw_ref, cb_ref)
            gate, val = up[:, :FF_HALF], up[:, FF_HALF:]
            sg = _sigmoid(gate)
            dgate = da * val * (sg * (1.0 + gate * (1.0 - sg)))
            dval = da * (gate * sg)
            return jnp.concatenate([dgate, dval], axis=1), p1, p2

        u = u_ref[...]
        prev = jnp.where(i > 0, prev_ref[...], 0.0)
        dup, p1, p2 = dpre(u, prev, da_ref[...])
        dupn, _, _ = dpre(next_ref[...], u[tm - 8:], dan_ref[...])
        dupn = jnp.where(i < nt - 1, dupn, 0.0)
        du = cw_ref[2:3, :] * dup + cw_ref[1:2, :] * _shift_up(dup, 1, dupn) + cw_ref[0:1, :] * _shift_up(dup, 2, dupn)
        du_ref[...] = du.astype(BF16)
        dw0_ref[...] += jnp.sum(dup * p2, axis=0, keepdims=True)
        dw1_ref[...] += jnp.sum(dup * p1, axis=0, keepdims=True)
        dw2_ref[...] += jnp.sum(dup * u, axis=0, keepdims=True)
        db_ref[...] += jnp.sum(dup, axis=0, keepdims=True)

    nxt = lambda j, i: (jnp.minimum((i + 1) * (tm // 8), T // 8 - 1), j)
    vec = pl.BlockSpec((1, W), lambda j, i: (0, j))
    return pl.pallas_call(
        body, name=name, grid=(2, nt),
        in_specs=[pl.BlockSpec((tm, W), lambda j, i: (i, j)),
                  pl.BlockSpec((8, W), lambda j, i: (jnp.maximum(i * (tm // 8) - 1, 0), j)),
                  pl.BlockSpec((8, W), nxt),
                  pl.BlockSpec((tm, FF_HALF), lambda j, i: (i, j)), pl.BlockSpec((8, FF_HALF), nxt),
                  pl.BlockSpec((3, W), lambda j, i: (0, j)), vec],
        out_specs=[pl.BlockSpec((tm, W), lambda j, i: (i, j)), vec, vec, vec, vec],
        out_shape=[jax.ShapeDtypeStruct((T, 2 * D_FF), BF16)] + [jax.ShapeDtypeStruct((1, 2 * D_FF), F32)] * 4,
        compiler_params=_cp(("parallel", "arbitrary")))(u, u, u, da, da, cw, cb)


def _sum_chips(slots, *, name):
    ns, R, C = slots.shape
    tr = _row_tile(R)

    def body(g_ref, o_ref):
        g = g_ref[0].astype(F32)
        for s in range(1, ns):
            g = g + g_ref[s].astype(F32)
        o_ref[...] = g

    return pl.pallas_call(
        body, name=name, grid=(R // tr,), in_specs=[pl.BlockSpec((ns, tr, C), lambda i: (0, i, 0))],
        out_specs=pl.BlockSpec((tr, C), lambda i: (i, 0)), out_shape=jax.ShapeDtypeStruct((R, C), F32),
        compiler_params=_cp(("parallel",)))(slots)


def _place():
    return lax.axis_index("x"), lax.axis_index("y"), lax.axis_index("c")


GATHER_SCRATCH = (pltpu.SemaphoreType.DMA((7,)), pltpu.SemaphoreType.DMA((7,)), pltpu.SemaphoreType.DMA)
EXCHANGE_SCRATCH = (pltpu.SemaphoreType.DMA((3,)), pltpu.SemaphoreType.DMA((3,)), pltpu.SemaphoreType.DMA)


def _gather_phases(x_ref, out_ref, send_sems, recv_sems, local_sem):
    x_, y_, c_ = _place()
    me, sibling = (x_, y_, c_), (x_, y_, 1 - c_)
    chips = [(1 - x_, y_), (x_, 1 - y_), (1 - x_, 1 - y_)]

    def slot(px, py, pc):
        return out_ref.at[4 * px + 2 * py + pc]

    def copy(k, block, to, src=None):
        return pltpu.make_async_remote_copy(
            src_ref=slot(*block) if src is None else src, dst_ref=slot(*block),
            send_sem=send_sems.at[k], recv_sem=recv_sems.at[k], device_id=to, device_id_type=MESH)

    def mine():
        return pltpu.make_async_copy(x_ref, slot(*me), local_sem)

    def first():
        return [copy(0, me, sibling, src=x_ref)] + [copy(1 + j, me, (*chip, c_), src=x_ref)
                                                     for j, chip in enumerate(chips)]

    def passed():
        return [copy(4 + j, (*chip, c_), sibling) for j, chip in enumerate(chips)]

    def start():
        mine().start()
        for cp in first():
            cp.start()

    def forward():
        fwd = passed()
        for j, chip in enumerate(chips):
            copy(1 + j, (*chip, c_), me).wait_recv()
            fwd[j].start()

    def finish():
        copy(0, sibling, me).wait_recv()
        for j, chip in enumerate(chips):
            copy(4 + j, (*chip, 1 - c_), me).wait_recv()
        for cp in first() + passed():
            cp.wait_send()
        mine().wait()

    return start, forward, finish


def _exchange_phases(p_ref, out_ref, send_sems, recv_sems, local_sem):
    x_, y_, c_ = _place()
    me_k = 2 * x_ + y_
    chips = [(1 - x_, y_), (x_, 1 - y_), (1 - x_, 1 - y_)]

    def local():
        return pltpu.make_async_copy(p_ref.at[me_k], out_ref.at[me_k], local_sem)

    def copy(j, src_k, dst_k, chip):
        return pltpu.make_async_remote_copy(
            src_ref=p_ref.at[src_k], dst_ref=out_ref.at[dst_k], send_sem=send_sems.at[j],
            recv_sem=recv_sems.at[j], device_id=(*chip, c_), device_id_type=MESH)

    def sends():
        return [copy(j, 2 * px + py, me_k, (px, py)) for j, (px, py) in enumerate(chips)]

    def start():
        local().start()
        for cp in sends():
            cp.start()

    def finish():
        for j, (px, py) in enumerate(chips):
            copy(j, me_k, 2 * px + py, (px, py)).wait_recv()
        for cp in sends():
            cp.wait_send()
        local().wait()

    return start, finish


def _all_gather(x, *, name):
    def body(x_ref, out_ref, send_sems, recv_sems, local_sem):
        for phase in _gather_phases(x_ref, out_ref, send_sems, recv_sems, local_sem):
            phase()

    spec = pl.BlockSpec(memory_space=pltpu.VMEM)
    return pl.pallas_call(
        body, name=name, out_shape=jax.ShapeDtypeStruct((N_DEV,) + x.shape, x.dtype),
        in_specs=[spec], out_specs=spec, scratch_shapes=list(GATHER_SCRATCH),
        compiler_params=pltpu.CompilerParams(vmem_limit_bytes=VMEM_LIMIT))(x)


def _small_rows():
    table, row = [], 0
    for n, size in SMALL_VECTORS:
        table.append((n, size, row))
        row += -(-size // PACK_COLS)
    return table


def _ff_chunk_source(c):
    block, off = divmod(c * 128, FF_HALF)
    return (0, 2, 1, 3)[block] * FF_HALF + off


def _pack_small(parts, *, name):
    table = _small_rows()

    def body(*refs):
        out = refs[-1]
        out[...] = jnp.zeros_like(out)
        for ref, (n, size, row) in zip(refs, table):
            if size != 2 * D_FF:
                out[row:row + 1, 0:size] = ref[...]
                continue
            for c in range(size // 128):
                src = _ff_chunk_source(c)
                r, lane = divmod(c * 128, PACK_COLS)
                out[row + r:row + r + 1, lane:lane + 128] = ref[:, src:src + 128]

    return pl.pallas_call(body, name=name, out_shape=jax.ShapeDtypeStruct((SMALL_ROWS, PACK_COLS), F32))(
        *[parts[n] for n, _, _ in table])


def _sum_small(g, *, name):
    table = _small_rows()
    shapes = [(n, size) for n, size, _ in table if not n.startswith("conv_w")]
    shapes.insert(7, ("conv_w", 2 * D_FF))

    def body(g_ref, *outs):
        def total(row, width):
            acc = g_ref[0, row:row + 1, 0:width]
            for d in range(1, N_DEV):
                acc = acc + g_ref[d, row:row + 1, 0:width]
            return acc

        out_of = {n: o for (n, _), o in zip(shapes, outs)}
        for n, size, row in table:
            o, j = (out_of["conv_w"], int(n[-1])) if n.startswith("conv_w") else (out_of[n], 0)
            for i in range(-(-size // PACK_COLS)):
                width = min(PACK_COLS, size - PACK_COLS * i)
                o[j:j + 1, PACK_COLS * i:PACK_COLS * i + width] = total(row + i, width)

    out_shape = [jax.ShapeDtypeStruct((3 if n == "conv_w" else 1, size), F32) for n, size in shapes]
    res = pl.pallas_call(body, name=name, out_shape=out_shape)(g)
    return {n: r for (n, _), r in zip(shapes, res)}


def _adamw_multi(ws, ms, vs, gs, *, name):
    k = len(ws)

    def body(*refs):
        w_refs, m_refs, v_refs, g_refs = (refs[i * k:(i + 1) * k] for i in range(4))
        outs = refs[4 * k:]
        for i in range(k):
            g = g_refs[i][...]
            mn = ADAM_B1 * m_refs[i][...] + (1.0 - ADAM_B1) * g
            vn = ADAM_B2 * v_refs[i][...] + (1.0 - ADAM_B2) * (g * g)
            m_hat = mn / (1.0 - ADAM_B1 ** ADAM_STEP)
            v_hat = vn / (1.0 - ADAM_B2 ** ADAM_STEP)
            outs[i][...] = g
            outs[k + i][...] = -ADAM_LR * (m_hat / (jnp.sqrt(v_hat) + ADAM_EPS) + ADAM_WD * w_refs[i][...])
            outs[2 * k + i][...] = mn
            outs[3 * k + i][...] = vn

    out_shape = [jax.ShapeDtypeStruct(w.shape, F32) for _ in range(4) for w in ws]
    res = pl.pallas_call(body, name=name, out_shape=out_shape, compiler_params=_cp())(*ws, *ms, *vs, *gs)
    return [res[i * k:(i + 1) * k] for i in range(4)]


SWAP_SCRATCH = (pltpu.SemaphoreType.DMA((4,)), pltpu.SemaphoreType.DMA((4,)))


def _swap_phases(g_ref, out_ref, send_sems, recv_sems):
    x_, y_, c_ = _place()

    def copies():
        return [pltpu.make_async_remote_copy(src_ref=g_ref.at[k, 1 - c_], dst_ref=out_ref.at[k],
                                             send_sem=send_sems.at[k], recv_sem=recv_sems.at[k],
                                             device_id=(x_, y_, 1 - c_), device_id_type=MESH) for k in range(4)]

    def start():
        for cp in copies():
            cp.start()

    def finish():
        for cp in copies():
            cp.wait()

    return start, finish


def _swap_sibling(g, *, name):
    def body(g_ref, out_ref, send_sems, recv_sems):
        for phase in _swap_phases(g_ref, out_ref, send_sems, recv_sems):
            phase()

    return pl.pallas_call(
        body, name=name, out_shape=jax.ShapeDtypeStruct((4,) + g.shape[2:], g.dtype), in_specs=[ANY], out_specs=ANY,
        scratch_shapes=list(SWAP_SCRATCH))(g)


def _row_tile(R):
    for cand in (256, 400, 200):
        if R % cand == 0:
            return cand
    return R


def _add_own(g, b, *, name, out_dtype):
    n, _, R, C = g.shape
    tr = _row_tile(R)

    def body(c_ref, g_ref, b_ref, o_ref):
        del c_ref
        o_ref[...] = (g_ref[...] + b_ref[...]).astype(out_dtype)

    blk = pl.BlockSpec((None, tr, C), lambda s, i, c: (s, i, 0))
    grid_spec = pltpu.PrefetchScalarGridSpec(
        num_scalar_prefetch=1, grid=(n, R // tr),
        in_specs=[pl.BlockSpec((None, None, tr, C), lambda s, i, c: (s, c[0], i, 0)), blk], out_specs=blk)
    core = jnp.reshape(lax.axis_index("c"), (1,)).astype(jnp.int32)
    return pl.pallas_call(body, name=name, grid_spec=grid_spec, out_shape=jax.ShapeDtypeStruct(b.shape, out_dtype),
                          compiler_params=_cp(("parallel", "parallel")))(core, g, b)


def _pack_local(parts, group, tail=None):
    table, rows = group
    segs = []
    for n, r, rp, tr in table:
        w = parts[n].T if tr else parts[n]
        segs.append(jnp.pad(w.reshape(r, PACK_COLS), ((0, rp - r), (0, 0))))
    spare = rows - sum(rp for _, _, rp, _ in table)
    segs.append(jnp.zeros((spare, PACK_COLS), segs[0].dtype) if tail is None else tail)
    return jnp.concatenate(segs, axis=0)


CONV_W_BITS = 2 * 3 * 704
SPARE_EARLY = 16


def _conv_w_as_rows(conv_w_shard):
    bits = lax.bitcast_convert_type(conv_w_shard.reshape(-1), BF16).reshape(-1)
    return jnp.pad(bits, (0, SPARE_EARLY * PACK_COLS - CONV_W_BITS)).reshape(SPARE_EARLY, PACK_COLS)


def _conv_w_from_rows(gathered):
    bits = gathered[:, EARLY[1] - SPARE_EARLY:].reshape(N_DEV, -1)[:, :CONV_W_BITS].reshape(N_DEV, 3 * 704, 2)
    w = lax.bitcast_convert_type(bits, F32).reshape(N_DEV, 3, 704)
    return w.transpose(1, 0, 2).reshape(3, 2 * D_FF)


def _unpack_local(packed, like, group):
    out, off = {}, 0
    for n, r, rp, tr in group[0]:
        rows, cols = like[n].shape
        seg = packed[off:off + r]
        out[n] = (seg.reshape(cols, rows).T if tr else seg)[None]
        off += rp
    return out


def _segments(g, group):
    out, off = {}, 0
    for n, r, rp, _ in group[0]:
        out[n] = g[:, off:off + r]
        off += rp
    return out


def _pack_grads(parts, group):
    table, rows = group
    segs = [jnp.pad(parts[n], ((0, 0), (0, rp - parts[n].shape[1]), (0, 0))) for n, _, rp, _ in table]
    segs.append(jnp.zeros((N_DEV, rows - sum(rp for _, _, rp, _ in table), PACK_COLS), F32))
    return jnp.concatenate(segs, axis=1)


def _owner_rows_early(g):
    g_in = jnp.concatenate([g["w_in_t"][:2432], g["w_in_t"][2496:2528]], axis=0).reshape(N_DEV, 308, PACK_COLS)
    g_uq = g["w_uq_t"].reshape(N_DEV, 128, MLA_Q_RANK)[:, :96].reshape(N_DEV, 24, PACK_COLS)
    g_ukv = jnp.concatenate([g["w_k_t"].reshape(N_DEV, 128, MLA_KV_RANK)[:, :64],
                             g["w_v_t"].reshape(N_DEV, 64, MLA_KV_RANK)], axis=1).reshape(N_DEV, 16, PACK_COLS)
    return dict(w_in=g_in, w_uq=g_uq, w_ukv=g_ukv)


def _owner_rows_late(g):
    g_up = g["w_up_t"].reshape(2, 2, 2, 704, PACK_COLS).swapaxes(0, 1).reshape(N_DEV, 704, PACK_COLS)
    return dict(w_out=g["w_out"].reshape(N_DEV, 128, PACK_COLS), w_up=g_up,
                w_down=g["w_down"].reshape(N_DEV, 352, PACK_COLS))


def _reduce_to_pairs(gp, *, name):
    gp = gp.reshape(4, 2, gp.shape[1], PACK_COLS)
    return _add_own(gp, _swap_sibling(gp, name=name + "_swap"), out_dtype=BF16, name=name + "_sum")


def _interleave_ff(w):
    g, v = w[..., :D_FF], w[..., D_FF:]
    return jnp.concatenate([g[..., :FF_HALF], v[..., :FF_HALF], g[..., FF_HALF:], v[..., FF_HALF:]], axis=-1)


def _rope_tables(pos):
    p = pos.astype(F32)[:, None]
    inv_r = ROPE_BASE ** (-jnp.arange(0, RET_HEAD_DIM, 2, dtype=F32) / RET_HEAD_DIM)
    ang = p * jnp.tile(inv_r, 4)
    sign_r = jnp.tile(jnp.concatenate([-jnp.ones((32,), F32), jnp.ones((32,), F32)]), 2)
    cos_r, ss_r = jnp.cos(ang), jnp.sin(ang) * sign_r
    inv_m = ROPE_BASE ** (-jnp.arange(0, MLA_ROPE, 2, dtype=F32) / MLA_ROPE)
    ang = p * jnp.concatenate([jnp.zeros((64,), F32), inv_m, inv_m, jnp.zeros((32,), F32)])
    sign_m = jnp.concatenate([jnp.zeros((64,), F32), -jnp.ones((16,), F32), jnp.ones((16,), F32), jnp.zeros((32,), F32)])
    cos_m, ss_m = jnp.cos(ang), jnp.sin(ang) * sign_m
    return cos_r, ss_r, cos_m, ss_m


def _prep_early(gathered):
    seg = _segments(gathered, EARLY)
    w_in_t = seg["w_in"].reshape(IN_WIDTH, D_MODEL)
    z = lambda n: jnp.zeros((n, D_MODEL), BF16)
    w_in_t = jnp.concatenate([w_in_t[:2432], z(64), w_in_t[2432:2464], z(32)], axis=0)
    w_uq_t = jnp.pad(seg["w_uq"].reshape(MLA_HEADS, 96, MLA_Q_RANK), ((0, 0), (0, 32), (0, 0))).reshape(1024, MLA_Q_RANK)
    ukv = seg["w_ukv"].reshape(MLA_HEADS, 128, MLA_KV_RANK)
    w_k_t = jnp.pad(ukv[:, :64], ((0, 0), (0, 64), (0, 0))).reshape(1024, MLA_KV_RANK)
    w_v_t = ukv[:, 64:].reshape(512, MLA_KV_RANK)
    return dict(w_in_t=w_in_t, w_uq_t=w_uq_t, w_k_t=w_k_t, w_v_t=w_v_t)


def _prep_late(gathered):
    seg = _segments(gathered, LATE)
    w_up_t = seg["w_up"].reshape(2, 2, 2, 704, D_MODEL).swapaxes(0, 1).reshape(2 * D_FF, D_MODEL)
    return dict(w_out=seg["w_out"].reshape(1024, D_MODEL), w_up_t=w_up_t, w_down=seg["w_down"].reshape(D_FF, D_MODEL))


def _local_step(x, pos, tgt, early, sm, late):
    dist = not isinstance(late, dict)
    cos_r, ss_r, cos_m, ss_m = _rope_tables(pos)
    tabs = _ret_tables()

    if dist:
        h, gathered = _rmsnorm_fwd(x, sm["attn_norm_w"], gather=early, name="attn_norm")
        W = _prep_early(gathered)
        sm = {**sm, "conv_w": _interleave_ff(_conv_w_from_rows(gathered))}
    else:
        h = _rmsnorm_fwd(x, sm["attn_norm_w"], name="attn_norm")
        W = early
    proj = _mm_nt(h, W["w_in_t"], name="in_proj")
    y_ret, o_ret, qr, kr = _ret_fwd(proj, cos_r, ss_r, tabs, sm["ret_gn_w"], name="ret_fwd")
    q, k, v1, cqn, ckvn = _mla_prep_fwd(proj, sm["mla_q_norm_w"], sm["mla_kv_norm_w"], W["w_uq_t"], W["w_k_t"],
                                       W["w_v_t"], cos_m, ss_m, name="mla_prep")
    T = x.shape[0]
    tq = min(T, 512)
    if dist:
        y_mla, lse, gathered = _flash_fwd(q, k, v1, gather=late, name="mla_attn")
        W = {**W, **_prep_late(gathered)}
    else:
        y_mla, lse = _flash_fwd(q, k, v1, name="mla_attn")
        W = {**W, **late}
    mixed = (y_ret, y_mla)
    x1, h2, u, a = _up_proj_conv(x, y_ret, y_mla, W["w_out"], sm["ffn_norm_w"], W["w_up_t"], sm["conv_w"],
                                 sm["conv_b"], name="out_proj_ffn_up_conv")
    loss, dx2, dx2b, d_final = _down_proj_loss(a, W["w_down"], x1, tgt, sm["final_norm_w"], name="down_proj_loss")

    g = {}
    g["w_down"] = _mm_tn(a, dx2b, name="dw_down")
    da = _mm_nt(dx2b, W["w_down"], name="d_act")
    du, dcw0, dcw1, dcw2, dcb = _conv_bwd(u, da, sm["conv_w"], sm["conv_b"], name="conv_bwd")
    g["w_up_t"] = _mm_tn(du, h2, name="dw_up")
    dx1, d_ffn, g["w_out"] = _mm_norm_bwd(du, W["w_up_t"], x1, sm["ffn_norm_w"], dx2, left=mixed,
                                          name="d_h2_ffn_norm_bwd_dw_out")

    do_ret, dg, do_mla, delta, d_gn = _mix_bwd(dx1, W["w_out"], o_ret, proj, y_mla, sm["ret_gn_w"], name="d_mixed_mix_bwd")
    drq = _ret_bwd_dq(kr, proj, do_ret, cos_r, ss_r, tabs, name="ret_bwd_dq")
    delta_r = delta.reshape(MLA_HEADS, T // tq, 1, tq)
    if dist:
        gl = _pack_grads(_owner_rows_late(g), LATE).reshape(4, 2, LATE[1], PACK_COLS)
        drk, drv, theirs = _ret_bwd_dkv(qr, kr, proj, do_ret, cos_r, ss_r, tabs, swap=gl, name="ret_bwd_dkv")
        pair = _add_own(gl, theirs, out_dtype=BF16, name="grad_late_sum")
        dqt, dk, dv, slots_late = _flash_bwd(q, k, v1, do_mla, lse, delta_r, exchange=pair, name="mla_attn_bwd")
    else:
        drk, drv = _ret_bwd_dkv(qr, kr, proj, do_ret, cos_r, ss_r, tabs, name="ret_bwd_dkv")
        dqt, dk, dv = _flash_bwd(q, k, v1, do_mla, lse, delta_r, name="mla_attn_bwd")
        slots_late = None
    dq = dqt.transpose(1, 3, 0, 2).reshape(T, MLA_HEADS * 128)
    dproj, g["w_in_t"], g["w_uq_t"], g["w_k_t"], g["w_v_t"], d_qn, d_kvn = _mla_prep_bwd(
        dq, dk, dv, proj, sm["mla_q_norm_w"], sm["mla_kv_norm_w"], W["w_uq_t"], W["w_k_t"], W["w_v_t"], cos_m, ss_m,
        (drq, drk, drv, dg), cqn, ckvn, h, name="mla_prep_bwd")
    if dist:
        pair = _reduce_to_pairs(_pack_grads(_owner_rows_early(g), EARLY), name="grad_early")
        grad_x, d_attn, slots_early = _mm_norm_bwd(dproj, W["w_in_t"], x, sm["attn_norm_w"], dx1, exchange=pair,
                                                   name="d_h_attn_norm_bwd")
    else:
        grad_x, d_attn = _mm_norm_bwd(dproj, W["w_in_t"], x, sm["attn_norm_w"], dx1, name="d_h_attn_norm_bwd")
        slots_early = None

    small = dict(attn_norm_w=d_attn, ret_gn_w=d_gn, mla_q_norm_w=d_qn, mla_kv_norm_w=d_kvn, ffn_norm_w=d_ffn,
                 conv_b=dcb, final_norm_w=d_final, conv_w0=dcw0, conv_w1=dcw1, conv_w2=dcw2, loss=loss)
    return loss, grad_x, g, small, slots_early, slots_late


def kernel(x, positions, attn_norm_w, w_in, ret_gn_w, mla_q_norm_w, w_uq, mla_kv_norm_w, w_ukv, w_out, ffn_norm_w, w_up, conv_w, conv_b, w_down, final_norm_w, loss_target, m_attn_norm_w, m_w_in, m_ret_gn_w, m_mla_q_norm_w, m_w_uq, m_mla_kv_norm_w, m_w_ukv, m_w_out, m_ffn_norm_w, m_w_up, m_conv_w, m_conv_b, m_w_down, m_final_norm_w, v_attn_norm_w, v_w_in, v_ret_gn_w, v_mla_q_norm_w, v_w_uq, v_mla_kv_norm_w, v_w_ukv, v_w_out, v_ffn_norm_w, v_w_up, v_conv_w, v_conv_b, v_w_down, v_final_norm_w):
    a = dict(locals())
    x_, y_, c_ = _place()
    dev = 4 * x_ + 2 * y_ + c_

    shard = {n: a[n][0] for n in BIG_NAMES}
    shard16 = {n: w.astype(BF16) for n, w in shard.items()}
    sm = dict(attn_norm_w=attn_norm_w, ret_gn_w=ret_gn_w, mla_q_norm_w=mla_q_norm_w, mla_kv_norm_w=mla_kv_norm_w,
              ffn_norm_w=ffn_norm_w, final_norm_w=final_norm_w.reshape(1, D_MODEL), conv_b=_interleave_ff(conv_b))

    loss, grad_x, _, gs, slots_early, slots_late = _local_step(
        x[0], positions[0], loss_target[0], _pack_local(shard16, EARLY, tail=_conv_w_as_rows(conv_w[0])), sm,
        _pack_local(shard16, LATE))

    big = [{}, {}, {}, {}]
    for group, slots, tag, calls in ((EARLY, slots_early, "early", (("w_in", "w_uq", "w_ukv"),)),
                                     (LATE, slots_late, "late", (("w_out", "w_down"), ("w_up",)))):
        grads = _unpack_local(_sum_chips(slots, name="grad_sum_" + tag), shard, group)
        for names_c in calls:
            res = _adamw_multi([shard[n] for n in names_c], [a["m_" + n][0] for n in names_c],
                               [a["v_" + n][0] for n in names_c], [grads[n][0] for n in names_c],
                               name="adamw_" + "_".join(names_c))
            for kind in range(4):
                for n, r in zip(names_c, res[kind]):
                    big[kind][n] = r[None]

    packed = _pack_small(gs, name="pack_small_grads")
    tot = _sum_small(_all_gather(packed, name="gather_small_grads"), name="sum_small_grads")
    loss_out = tot["loss"][0, 0]
    g_cw = lax.dynamic_slice_in_dim(tot["conv_w"], dev * 704, 704, axis=1)

    def rows_of(prefix):
        return [a[prefix + n].reshape(1, size) for n, size in SMALL]

    sml = _adamw_multi(rows_of("") + [conv_w[0]], rows_of("m_") + [m_conv_w[0]], rows_of("v_") + [v_conv_w[0]],
                       [tot[n] for n, _ in SMALL] + [g_cw], name="adamw_small")
    cwo = [kind[-1] for kind in sml]

    def small_of(kind, n):
        return sml[kind][[nm for nm, _ in SMALL].index(n)].reshape(a[n].shape)

    names = ['attn_norm_w', 'w_in', 'ret_gn_w', 'mla_q_norm_w', 'w_uq', 'mla_kv_norm_w', 'w_ukv', 'w_out',
             'ffn_norm_w', 'w_up', 'conv_w', 'conv_b', 'w_down', 'final_norm_w']
    outs = [loss_out, grad_x[None]]
    for kind in range(4):
        for n in names:
            if n == "conv_w":
                outs.append(cwo[kind][None])
            elif n in big[kind]:
                outs.append(big[kind][n])
            else:
                outs.append(small_of(kind, n))
    return tuple(outs)
```
